```python
import jax, jax.numpy as jnp
from jax import lax
import numpy as np

D_MODEL = 1024
BATCH = 8
SEQ = 4096
DEPTH = 2

N_A = DEPTH // 2
N_B = DEPTH - N_A
N_META = 16
HEAD_DIM = 64
N_HEADS_A = D_MODEL // HEAD_DIM
LORA_DECAY = 64
LORA_AAA = 64
LORA_GATE = 128
GN_EPS = 64e-5
N_HEADS_Q = D_MODEL // HEAD_DIM
N_HEADS_KV = 4
GROUP = N_HEADS_Q // N_HEADS_KV
WINDOW = 128
BLOCK = 128
PAD_FRONT = BLOCK - N_META
ROPE_THETA = 10000.0
D_FF = 4 * D_MODEL
ALPHA = (2.0 * DEPTH) ** 0.25
BETA = (8.0 * DEPTH) ** -0.25
LN_EPS = 1e-5

kernel_name = "yoco_rwkv7_swa_sink_hybrid"


def layer_norm(x, g, b):
    xf = x.astype(jnp.float32)
    mu = jnp.mean(xf, axis=-1, keepdims=True)
    xc = xf - mu
    var = jnp.mean(xc * xc, axis=-1, keepdims=True)
    y = xc * lax.rsqrt(var + LN_EPS) * g.astype(jnp.float32) + b.astype(jnp.float32)
    return y.astype(x.dtype)


def rope_tables(length):
    inv_freq = 1.0 / (ROPE_THETA ** (jnp.arange(0, HEAD_DIM, 2, dtype=jnp.float32) / HEAD_DIM))
    ang = jnp.arange(length, dtype=jnp.float32)[:, None] * inv_freq[None, :]
    return jnp.cos(ang), jnp.sin(ang)


def apply_rope(t, cos, sin):
    tf = t.astype(jnp.float32)
    t1, t2 = jnp.split(tf, 2, axis=-1)
    c = cos[None, :, None, :]
    s = sin[None, :, None, :]
    out = jnp.concatenate([t1 * c - t2 * s, t2 * c + t1 * s], axis=-1)
    return out.astype(t.dtype)


def sq_relu_mlp(x, w_up, w_down):
    h = jax.nn.relu(x @ w_up)
    return (h * h) @ w_down


def rwkv7_time_mix(x, mu, w_r, w_k, w_v, w_o, w0, w1, w2, a0, a1, a2, g1, g2,
                   k_k, k_a, r_k, gn_w, gn_b):
    bsz, length, d = x.shape
    H, N = N_HEADS_A, HEAD_DIM
    x_prev = jnp.pad(x, ((0, 0), (1, 0), (0, 0)))[:, :-1]
    xx = x_prev - x
    xr = x + xx * mu[0]
    xw = x + xx * mu[1]
    xk = x + xx * mu[2]
    xv = x + xx * mu[3]
    xa = x + xx * mu[4]
    xg = x + xx * mu[5]
    r = xr @ w_r
    k = xk @ w_k
    v = xv @ w_v
    w = -jax.nn.softplus(-(w0 + jnp.tanh(xw @ w1) @ w2)) - 0.5
    a = jax.nn.sigmoid(a0 + (xa @ a1) @ a2)
    g = jax.nn.sigmoid(xg @ g1) @ g2
    f32 = jnp.float32
    kk = (k * k_k).reshape(bsz, length, H, N).astype(f32)
    kk = kk / jnp.maximum(jnp.linalg.norm(kk, axis=-1, keepdims=True), 1e-12)
    k = k * (1.0 + (a - 1.0) * k_a)
    rh = r.reshape(bsz, length, H, N).astype(f32)
    kh = k.reshape(bsz, length, H, N).astype(f32)
    vh = v.reshape(bsz, length, H, N).astype(f32)
    ah = a.reshape(bsz, length, H, N).astype(f32)
    decay = jnp.exp(-jnp.exp(w.reshape(bsz, length, H, N).astype(f32)))
    seq_first = lambda t: jnp.transpose(t, (1, 0, 2, 3))
    inputs = (seq_first(rh), seq_first(decay), seq_first(kh), seq_first(vh),
              seq_first(-kk), seq_first(kk * ah))

    def step(state, inp):
        r_t, w_t, k_t, v_t, a_t, b_t = inp
        sa = jnp.einsum('bhij,bhj->bhi', state, a_t)
        state = (state * w_t[:, :, None, :] + sa[..., None] * b_t[:, :, None, :]
                 + v_t[..., None] * k_t[:, :, None, :])
        y = jnp.einsum('bhij,bhj->bhi', state, r_t)
        return state, y

    state0 = jnp.zeros((bsz, H, N, N), f32)
    _, ys = lax.scan(step, state0, inputs)
    y = jnp.transpose(ys, (1, 0, 2, 3))
    ym = jnp.mean(y, axis=-1, keepdims=True)
    yc = y - ym
    yv = jnp.mean(yc * yc, axis=-1, keepdims=True)
    y = (yc * lax.rsqrt(yv + GN_EPS) * gn_w.reshape(H, N).astype(f32)
         + gn_b.reshape(H, N).astype(f32))
    bonus = jnp.sum(rh * kh * r_k.astype(f32), axis=-1, keepdims=True) * vh
    o = (y + bonus).reshape(bsz, length, d).astype(x.dtype) * g
    return o @ w_o


def to_blocks(t):
    pad = [(0, 0), (PAD_FRONT, 0)] + [(0, 0)] * (t.ndim - 2)
    t = jnp.pad(t, pad)
    return t.reshape(t.shape[0], -1, BLOCK, *t.shape[2:])


def with_prev_block(tb):
    pad = [(0, 0), (1, 0)] + [(0, 0)] * (tb.ndim - 2)
    prev = jnp.pad(tb, pad)[:, :-1]
    return jnp.concatenate([prev, tb], axis=2)


def band_mask(nb):
    qi = jnp.arange(BLOCK)[:, None]
    kj = jnp.arange(2 * BLOCK)[None, :]
    rel = BLOCK + qi - kj
    in_window = (rel >= 0) & (rel < WINDOW)
    key_pos = (jnp.arange(nb)[:, None] - 1) * BLOCK + jnp.arange(2 * BLOCK)[None, :]
    valid = key_pos >= PAD_FRONT
    return in_window[None] & valid[:, None, :]


def shared_kv(x, w_k, w_v, cos, sin):
    bsz, length, _ = x.shape
    k = (x @ w_k).reshape(bsz, length, N_HEADS_KV, HEAD_DIM)
    v = (x @ w_v).reshape(bsz, length, N_HEADS_KV, HEAD_DIM)
    k = apply_rope(k, cos, sin)
    return with_prev_block(to_blocks(k)), with_prev_block(to_blocks(v))


def swa_sink_attention(x, w_q, sinks, w_o, kb, vb, cos, sin, mask):
    bsz, length, d = x.shape
    q = (x @ w_q).reshape(bsz, length, N_HEADS_Q, HEAD_DIM)
    q = apply_rope(q, cos, sin)
    qb = to_blocks(q)
    nb = qb.shape[1]
    qb = qb.reshape(bsz, nb, BLOCK, N_HEADS_KV, GROUP, HEAD_DIM)
    f32 = jnp.float32
    s = jnp.einsum('bnqhgd,bnshd->bnhgqs', qb.astype(f32), kb.astype(f32)) * (HEAD_DIM ** -0.5)
    s = jnp.where(mask[None, :, None, None], s, -jnp.inf)
    sink = sinks.reshape(N_HEADS_KV, GROUP).astype(f32)[None, None, :, :, None]
    m = jnp.maximum(jnp.max(s, axis=-1), sink)
    p = jnp.exp(s - m[..., None])
    denom = jnp.sum(p, axis=-1) + jnp.exp(sink - m)
    p = p / denom[..., None]
    o = jnp.einsum('bnhgqs,bnshd->bnqhgd', p, vb.astype(f32))
    o = o.reshape(bsz, nb * BLOCK, d)[:, PAD_FRONT:].astype(x.dtype)
    return o @ w_o


def _fwd_setup_inputs(seed: int = 0) -> dict:
    key = jax.random.key(seed)
    ks = iter(jax.random.split(key, 40))
    D = D_MODEL
    inv = D ** -0.5

    def nrm(shape, scale):
        return jax.random.normal(next(ks), shape, jnp.float32) * scale

    return {
        "x": nrm((BATCH, SEQ, D), 1.0),
        "meta_tokens": nrm((N_META, D), 1.0),
        "a_mu": jax.random.uniform(next(ks), (N_A, 6, D), jnp.float32),
        "a_w_r": nrm((N_A, D, D), inv),
        "a_w_k": nrm((N_A, D, D), inv),
        "a_w_v": nrm((N_A, D, D), inv * BETA),
        "a_w_o": nrm((N_A, D, D), inv * BETA),
        "a_w0": jax.random.uniform(next(ks), (N_A, D), jnp.float32, -6.0, 1.0),
        "a_w1": nrm((N_A, D, LORA_DECAY), inv),
        "a_w2": nrm((N_A, LORA_DECAY, D), 0.1 * LORA_DECAY ** -0.5),
        "a_a0": nrm((N_A, D), 0.5),
        "a_a1": nrm((N_A, D, LORA_AAA), inv),
        "a_a2": nrm((N_A, LORA_AAA, D), LORA_AAA ** -0.5),
        "a_g1": nrm((N_A, D, LORA_GATE), inv),
        "a_g2": nrm((N_A, LORA_GATE, D), LORA_GATE ** -0.5),
        "a_k_k": 0.85 + nrm((N_A, D), 0.05),
        "a_k_a": 1.0 + nrm((N_A, D), 0.05),
        "a_r_k": nrm((N_A, N_HEADS_A, HEAD_DIM), 0.1),
        "a_gn_w": 1.0 + nrm((N_A, D), 0.05),
        "a_gn_b": nrm((N_A, D), 0.02),
        "kv_w_k": nrm((D, N_HEADS_KV * HEAD_DIM), inv),
        "kv_w_v": nrm((D, N_HEADS_KV * HEAD_DIM), inv * BETA),
        "b_w_q": nrm((N_B, D, N_HEADS_Q * HEAD_DIM), inv),
        "b_sinks": nrm((N_B, N_HEADS_Q), 1.0),
        "b_w_o": nrm((N_B, N_HEADS_Q * HEAD_DIM, D), inv * BETA),
        "mlp_w_up": nrm((DEPTH, D, D_FF), inv),
        "mlp_w_down": nrm((DEPTH, D_FF, D), D_FF ** -0.5 * BETA),
        "ln_g": 1.0 + nrm((DEPTH, 2, D), 0.05),
        "ln_b": nrm((DEPTH, 2, D), 0.02),
    }


def _fwd_reference(x, meta_tokens, a_mu, a_w_r, a_w_k, a_w_v, a_w_o, a_w0, a_w1, a_w2,
              a_a0, a_a1, a_a2, a_g1, a_g2, a_k_k, a_k_a, a_r_k, a_gn_w, a_gn_b,
              kv_w_k, kv_w_v, b_w_q, b_sinks, b_w_o, mlp_w_up, mlp_w_down, ln_g, ln_b):
    bsz = x.shape[0]
    meta = jnp.broadcast_to(meta_tokens[None].astype(x.dtype), (bsz, N_META, x.shape[2]))
    h = jnp.concatenate([meta, x], axis=1)
    length = h.shape[1]
    cos, sin = rope_tables(length)
    nb = (length + PAD_FRONT) // BLOCK
    mask = band_mask(nb)
    kb = vb = None
    for i in range(DEPTH):
        if i < N_A:
            j = i
            mix = rwkv7_time_mix(h, a_mu[j], a_w_r[j], a_w_k[j], a_w_v[j], a_w_o[j],
                                 a_w0[j], a_w1[j], a_w2[j], a_a0[j], a_a1[j], a_a2[j],
                                 a_g1[j], a_g2[j], a_k_k[j], a_k_a[j], a_r_k[j],
                                 a_gn_w[j], a_gn_b[j])
        else:
            if i == N_A:
                kb, vb = shared_kv(h, kv_w_k, kv_w_v, cos, sin)
            j = i - N_A
            mix = swa_sink_attention(h, b_w_q[j], b_sinks[j], b_w_o[j], kb, vb, cos, sin, mask)
        h = layer_norm(ALPHA * h + mix, ln_g[i, 0], ln_b[i, 0])
        h = layer_norm(ALPHA * h + sq_relu_mlp(h, mlp_w_up[i], mlp_w_down[i]), ln_g[i, 1], ln_b[i, 1])
    return h[:, N_META:]


import jax as _jax
import jax.numpy as _jnp

TWIN_FORMAT = 'train_step'
FWD_PARAMS = ['x', 'meta_tokens', 'a_mu', 'a_w_r', 'a_w_k', 'a_w_v', 'a_w_o', 'a_w0', 'a_w1', 'a_w2', 'a_a0', 'a_a1', 'a_a2', 'a_g1', 'a_g2', 'a_k_k', 'a_k_a', 'a_r_k', 'a_gn_w', 'a_gn_b', 'kv_w_k', 'kv_w_v', 'b_w_q', 'b_sinks', 'b_w_o', 'mlp_w_up', 'mlp_w_down', 'ln_g', 'ln_b']
TWIN_WEIGHTS = ['meta_tokens', 'a_mu', 'a_w_r', 'a_w_k', 'a_w_v', 'a_w_o', 'a_w0', 'a_w1', 'a_w2', 'a_a0', 'a_a1', 'a_a2', 'a_g1', 'a_g2', 'a_k_k', 'a_k_a', 'a_r_k', 'a_gn_w', 'a_gn_b', 'kv_w_k', 'kv_w_v', 'b_w_q', 'b_sinks', 'b_w_o', 'mlp_w_up', 'mlp_w_down', 'ln_g', 'ln_b']
TWIN_DIFF_INPUT = 'x'
TWIN_INPUTS = ['x', 'meta_tokens', 'a_mu', 'a_w_r', 'a_w_k', 'a_w_v', 'a_w_o', 'a_w0', 'a_w1', 'a_w2', 'a_a0', 'a_a1', 'a_a2', 'a_g1', 'a_g2', 'a_k_k', 'a_k_a', 'a_r_k', 'a_gn_w', 'a_gn_b', 'kv_w_k', 'kv_w_v', 'b_w_q', 'b_sinks', 'b_w_o', 'mlp_w_up', 'mlp_w_down', 'ln_g', 'ln_b', 'loss_target', 'm_meta_tokens', 'm_a_mu', 'm_a_w_r', 'm_a_w_k', 'm_a_w_v', 'm_a_w_o', 'm_a_w0', 'm_a_w1', 'm_a_w2', 'm_a_a0', 'm_a_a1', 'm_a_a2', 'm_a_g1', 'm_a_g2', 'm_a_k_k', 'm_a_k_a', 'm_a_r_k', 'm_a_gn_w', 'm_a_gn_b', 'm_kv_w_k', 'm_kv_w_v', 'm_b_w_q', 'm_b_sinks', 'm_b_w_o', 'm_mlp_w_up', 'm_mlp_w_down', 'm_ln_g', 'm_ln_b', 'v_meta_tokens', 'v_a_mu', 'v_a_w_r', 'v_a_w_k', 'v_a_w_v', 'v_a_w_o', 'v_a_w0', 'v_a_w1', 'v_a_w2', 'v_a_a0', 'v_a_a1', 'v_a_a2', 'v_a_g1', 'v_a_g2', 'v_a_k_k', 'v_a_k_a', 'v_a_r_k', 'v_a_gn_w', 'v_a_gn_b', 'v_kv_w_k', 'v_kv_w_v', 'v_b_w_q', 'v_b_sinks', 'v_b_w_o', 'v_mlp_w_up', 'v_mlp_w_down', 'v_ln_g', 'v_ln_b']
TWIN_OUTPUTS = ['loss', 'grad_x', 'grad_meta_tokens', 'grad_a_mu', 'grad_a_w_r', 'grad_a_w_k', 'grad_a_w_v', 'grad_a_w_o', 'grad_a_w0', 'grad_a_w1', 'grad_a_w2', 'grad_a_a0', 'grad_a_a1', 'grad_a_a2', 'grad_a_g1', 'grad_a_g2', 'grad_a_k_k', 'grad_a_k_a', 'grad_a_r_k', 'grad_a_gn_w', 'grad_a_gn_b', 'grad_kv_w_k', 'grad_kv_w_v', 'grad_b_w_q', 'grad_b_sinks', 'grad_b_w_o', 'grad_mlp_w_up', 'grad_mlp_w_down', 'grad_ln_g', 'grad_ln_b', 'delta_meta_tokens', 'delta_a_mu', 'delta_a_w_r', 'delta_a_w_k', 'delta_a_w_v', 'delta_a_w_o', 'delta_a_w0', 'delta_a_w1', 'delta_a_w2', 'delta_a_a0', 'delta_a_a1', 'delta_a_a2', 'delta_a_g1', 'delta_a_g2', 'delta_a_k_k', 'delta_a_k_a', 'delta_a_r_k', 'delta_a_gn_w', 'delta_a_gn_b', 'delta_kv_w_k', 'delta_kv_w_v', 'delta_b_w_q', 'delta_b_sinks', 'delta_b_w_o', 'delta_mlp_w_up', 'delta_mlp_w_down', 'delta_ln_g', 'delta_ln_b', 'new_m_meta_tokens', 'new_m_a_mu', 'new_m_a_w_r', 'new_m_a_w_k', 'new_m_a_w_v', 'new_m_a_w_o', 'new_m_a_w0', 'new_m_a_w1', 'new_m_a_w2', 'new_m_a_a0', 'new_m_a_a1', 'new_m_a_a2', 'new_m_a_g1', 'new_m_a_g2', 'new_m_a_k_k', 'new_m_a_k_a', 'new_m_a_r_k', 'new_m_a_gn_w', 'new_m_a_gn_b', 'new_m_kv_w_k', 'new_m_kv_w_v', 'new_m_b_w_q', 'new_m_b_sinks', 'new_m_b_w_o', 'new_m_mlp_w_up', 'new_m_mlp_w_down', 'new_m_ln_g', 'new_m_ln_b', 'new_v_meta_tokens', 'new_v_a_mu', 'new_v_a_w_r', 'new_v_a_w_k', 'new_v_a_w_v', 'new_v_a_w_o', 'new_v_a_w0', 'new_v_a_w1', 'new_v_a_w2', 'new_v_a_a0', 'new_v_a_a1', 'new_v_a_a2', 'new_v_a_g1', 'new_v_a_g2', 'new_v_a_k_k', 'new_v_a_k_a', 'new_v_a_r_k', 'new_v_a_gn_w', 'new_v_a_gn_b', 'new_v_kv_w_k', 'new_v_kv_w_v', 'new_v_b_w_q', 'new_v_b_sinks', 'new_v_b_w_o', 'new_v_mlp_w_up', 'new_v_mlp_w_down', 'new_v_ln_g', 'new_v_ln_b']
TWIN_LEAF_KINDS = {'loss': 'loss', 'grad_x': 'grad_x', 'grad_meta_tokens': 'grad_w', 'grad_a_mu': 'grad_w', 'grad_a_w_r': 'grad_w', 'grad_a_w_k': 'grad_w', 'grad_a_w_v': 'grad_w', 'grad_a_w_o': 'grad_w', 'grad_a_w0': 'grad_w', 'grad_a_w1': 'grad_w', 'grad_a_w2': 'grad_w', 'grad_a_a0': 'grad_w', 'grad_a_a1': 'grad_w', 'grad_a_a2': 'grad_w', 'grad_a_g1': 'grad_w', 'grad_a_g2': 'grad_w', 'grad_a_k_k': 'grad_w', 'grad_a_k_a': 'grad_w', 'grad_a_r_k': 'grad_w', 'grad_a_gn_w': 'grad_w', 'grad_a_gn_b': 'grad_w', 'grad_kv_w_k': 'grad_w', 'grad_kv_w_v': 'grad_w', 'grad_b_w_q': 'grad_w', 'grad_b_sinks': 'grad_w', 'grad_b_w_o': 'grad_w', 'grad_mlp_w_up': 'grad_w', 'grad_mlp_w_down': 'grad_w', 'grad_ln_g': 'grad_w', 'grad_ln_b': 'grad_w', 'delta_meta_tokens': 'delta_w', 'delta_a_mu': 'delta_w', 'delta_a_w_r': 'delta_w', 'delta_a_w_k': 'delta_w', 'delta_a_w_v': 'delta_w', 'delta_a_w_o': 'delta_w', 'delta_a_w0': 'delta_w', 'delta_a_w1': 'delta_w', 'delta_a_w2': 'delta_w', 'delta_a_a0': 'delta_w', 'delta_a_a1': 'delta_w', 'delta_a_a2': 'delta_w', 'delta_a_g1': 'delta_w', 'delta_a_g2': 'delta_w', 'delta_a_k_k': 'delta_w', 'delta_a_k_a': 'delta_w', 'delta_a_r_k': 'delta_w', 'delta_a_gn_w': 'delta_w', 'delta_a_gn_b': 'delta_w', 'delta_kv_w_k': 'delta_w', 'delta_kv_w_v': 'delta_w', 'delta_b_w_q': 'delta_w', 'delta_b_sinks': 'delta_w', 'delta_b_w_o': 'delta_w', 'delta_mlp_w_up': 'delta_w', 'delta_mlp_w_down': 'delta_w', 'delta_ln_g': 'delta_w', 'delta_ln_b': 'delta_w', 'new_m_meta_tokens': 'new_m', 'new_m_a_mu': 'new_m', 'new_m_a_w_r': 'new_m', 'new_m_a_w_k': 'new_m', 'new_m_a_w_v': 'new_m', 'new_m_a_w_o': 'new_m', 'new_m_a_w0': 'new_m', 'new_m_a_w1': 'new_m', 'new_m_a_w2': 'new_m', 'new_m_a_a0': 'new_m', 'new_m_a_a1': 'new_m', 'new_m_a_a2': 'new_m', 'new_m_a_g1': 'new_m', 'new_m_a_g2': 'new_m', 'new_m_a_k_k': 'new_m', 'new_m_a_k_a': 'new_m', 'new_m_a_r_k': 'new_m', 'new_m_a_gn_w': 'new_m', 'new_m_a_gn_b': 'new_m', 'new_m_kv_w_k': 'new_m', 'new_m_kv_w_v': 'new_m', 'new_m_b_w_q': 'new_m', 'new_m_b_sinks': 'new_m', 'new_m_b_w_o': 'new_m', 'new_m_mlp_w_up': 'new_m', 'new_m_mlp_w_down': 'new_m', 'new_m_ln_g': 'new_m', 'new_m_ln_b': 'new_m', 'new_v_meta_tokens': 'new_v', 'new_v_a_mu': 'new_v', 'new_v_a_w_r': 'new_v', 'new_v_a_w_k': 'new_v', 'new_v_a_w_v': 'new_v', 'new_v_a_w_o': 'new_v', 'new_v_a_w0': 'new_v', 'new_v_a_w1': 'new_v', 'new_v_a_w2': 'new_v', 'new_v_a_a0': 'new_v', 'new_v_a_a1': 'new_v', 'new_v_a_a2': 'new_v', 'new_v_a_g1': 'new_v', 'new_v_a_g2': 'new_v', 'new_v_a_k_k': 'new_v', 'new_v_a_k_a': 'new_v', 'new_v_a_r_k': 'new_v', 'new_v_a_gn_w': 'new_v', 'new_v_a_gn_b': 'new_v', 'new_v_kv_w_k': 'new_v', 'new_v_kv_w_v': 'new_v', 'new_v_b_w_q': 'new_v', 'new_v_b_sinks': 'new_v', 'new_v_b_w_o': 'new_v', 'new_v_mlp_w_up': 'new_v', 'new_v_mlp_w_down': 'new_v', 'new_v_ln_g': 'new_v', 'new_v_ln_b': 'new_v'}


def _forward(args):
    return _fwd_reference(*[args[k] for k in FWD_PARAMS])


def _output_shape():
    out = _jax.eval_shape(lambda: _forward(_fwd_setup_inputs(0)))
    return out.shape, out.dtype

N_MICROBATCH = 1
ADAM_LR = 0.001
ADAM_B1 = 0.9
ADAM_B2 = 0.999
ADAM_EPS = 1e-08
ADAM_WD = 0.01
ADAM_STEP = 10
PER_EXAMPLE_BATCH_AXIS = {'x': 0, 'loss_target': 0}
SHARED_INPUTS = []
_WEIGHT_DTYPES = {'meta_tokens': _jnp.float32, 'a_mu': _jnp.float32, 'a_w_r': _jnp.float32, 'a_w_k': _jnp.float32, 'a_w_v': _jnp.float32, 'a_w_o': _jnp.float32, 'a_w0': _jnp.float32, 'a_w1': _jnp.float32, 'a_w2': _jnp.float32, 'a_a0': _jnp.float32, 'a_a1': _jnp.float32, 'a_a2': _jnp.float32, 'a_g1': _jnp.float32, 'a_g2': _jnp.float32, 'a_k_k': _jnp.float32, 'a_k_a': _jnp.float32, 'a_r_k': _jnp.float32, 'a_gn_w': _jnp.float32, 'a_gn_b': _jnp.float32, 'kv_w_k': _jnp.float32, 'kv_w_v': _jnp.float32, 'b_w_q': _jnp.float32, 'b_sinks': _jnp.float32, 'b_w_o': _jnp.float32, 'mlp_w_up': _jnp.float32, 'mlp_w_down': _jnp.float32, 'ln_g': _jnp.float32, 'ln_b': _jnp.float32}
MOMENT_SCALE = {'meta_tokens': 3.622952e-03, 'a_mu': 4.358467e-02, 'a_w_r': 3.560398e-02, 'a_w_k': 3.851384e-02, 'a_w_v': 6.971377e-02, 'a_w_o': 6.934961e-02, 'a_w0': 1.420912e-02, 'a_w1': 7.193572e-04, 'a_w2': 1.717111e-03, 'a_a0': 1.347723e-02, 'a_a1': 4.684289e-02, 'a_a2': 1.152598e-02, 'a_g1': 3.369429e-02, 'a_g2': 3.537994e-02, 'a_k_k': 4.418267e-02, 'a_k_a': 4.131697e-02, 'a_r_k': 3.985556e-02, 'a_gn_w': 3.509382e-02, 'a_gn_b': 1.266111e-01, 'kv_w_k': 1.022230e-02, 'kv_w_v': 7.129869e-02, 'b_w_q': 5.110048e-03, 'b_sinks': 1.419054e-03, 'b_w_o': 3.490945e-02, 'mlp_w_up': 4.280999e-02, 'mlp_w_down': 2.399827e-01, 'ln_g': 1.634491e+01, 'ln_b': 3.683417e+00}


def _to_microbatches(a, axis):
    t = _jnp.moveaxis(a, axis, 0)
    t = t.reshape((N_MICROBATCH, t.shape[0] // N_MICROBATCH) + t.shape[1:])
    return _jnp.moveaxis(t, 1, axis + 1)


def setup_inputs(seed: int = 0) -> dict:
    inp = _fwd_setup_inputs(seed)
    key = _jax.random.fold_in(_jax.random.key(seed), 7919)
    shape, _ = _output_shape()
    out = dict(inp)
    out["loss_target"] = _jax.random.normal(_jax.random.fold_in(key, 0), shape, _jnp.float32)
    for i, name in enumerate(TWIN_WEIGHTS):
        w = inp[name].astype(_jnp.float32)
        if MOMENT_SCALE is None:
            s = _jnp.sqrt(_jnp.mean(_jnp.square(w)) + 1e-30)
        else:
            s = MOMENT_SCALE[name]
        km, kv = _jax.random.split(_jax.random.fold_in(key, i + 1))
        out[name] = w
        out["m_" + name] = s * _jax.random.normal(km, w.shape, _jnp.float32)
        out["v_" + name] = (s * s) * _jax.random.uniform(kv, w.shape, _jnp.float32, 0.5, 1.5)
    if N_MICROBATCH > 1:
        for name, axis in PER_EXAMPLE_BATCH_AXIS.items():
            out[name] = _to_microbatches(out[name], axis)
    return {'x': out['x'], 'meta_tokens': out['meta_tokens'], 'a_mu': out['a_mu'], 'a_w_r': out['a_w_r'], 'a_w_k': out['a_w_k'], 'a_w_v': out['a_w_v'], 'a_w_o': out['a_w_o'], 'a_w0': out['a_w0'], 'a_w1': out['a_w1'], 'a_w2': out['a_w2'], 'a_a0': out['a_a0'], 'a_a1': out['a_a1'], 'a_a2': out['a_a2'], 'a_g1': out['a_g1'], 'a_g2': out['a_g2'], 'a_k_k': out['a_k_k'], 'a_k_a': out['a_k_a'], 'a_r_k': out['a_r_k'], 'a_gn_w': out['a_gn_w'], 'a_gn_b': out['a_gn_b'], 'kv_w_k': out['kv_w_k'], 'kv_w_v': out['kv_w_v'], 'b_w_q': out['b_w_q'], 'b_sinks': out['b_sinks'], 'b_w_o': out['b_w_o'], 'mlp_w_up': out['mlp_w_up'], 'mlp_w_down': out['mlp_w_down'], 'ln_g': out['ln_g'], 'ln_b': out['ln_b'], 'loss_target': out['loss_target'], 'm_meta_tokens': out['m_meta_tokens'], 'm_a_mu': out['m_a_mu'], 'm_a_w_r': out['m_a_w_r'], 'm_a_w_k': out['m_a_w_k'], 'm_a_w_v': out['m_a_w_v'], 'm_a_w_o': out['m_a_w_o'], 'm_a_w0': out['m_a_w0'], 'm_a_w1': out['m_a_w1'], 'm_a_w2': out['m_a_w2'], 'm_a_a0': out['m_a_a0'], 'm_a_a1': out['m_a_a1'], 'm_a_a2': out['m_a_a2'], 'm_a_g1': out['m_a_g1'], 'm_a_g2': out['m_a_g2'], 'm_a_k_k': out['m_a_k_k'], 'm_a_k_a': out['m_a_k_a'], 'm_a_r_k': out['m_a_r_k'], 'm_a_gn_w': out['m_a_gn_w'], 'm_a_gn_b': out['m_a_gn_b'], 'm_kv_w_k': out['m_kv_w_k'], 'm_kv_w_v': out['m_kv_w_v'], 'm_b_w_q': out['m_b_w_q'], 'm_b_sinks': out['m_b_sinks'], 'm_b_w_o': out['m_b_w_o'], 'm_mlp_w_up': out['m_mlp_w_up'], 'm_mlp_w_down': out['m_mlp_w_down'], 'm_ln_g': out['m_ln_g'], 'm_ln_b': out['m_ln_b'], 'v_meta_tokens': out['v_meta_tokens'], 'v_a_mu': out['v_a_mu'], 'v_a_w_r': out['v_a_w_r'], 'v_a_w_k': out['v_a_w_k'], 'v_a_w_v': out['v_a_w_v'], 'v_a_w_o': out['v_a_w_o'], 'v_a_w0': out['v_a_w0'], 'v_a_w1': out['v_a_w1'], 'v_a_w2': out['v_a_w2'], 'v_a_a0': out['v_a_a0'], 'v_a_a1': out['v_a_a1'], 'v_a_a2': out['v_a_a2'], 'v_a_g1': out['v_a_g1'], 'v_a_g2': out['v_a_g2'], 'v_a_k_k': out['v_a_k_k'], 'v_a_k_a': out['v_a_k_a'], 'v_a_r_k': out['v_a_r_k'], 'v_a_gn_w': out['v_a_gn_w'], 'v_a_gn_b': out['v_a_gn_b'], 'v_kv_w_k': out['v_kv_w_k'], 'v_kv_w_v': out['v_kv_w_v'], 'v_b_w_q': out['v_b_w_q'], 'v_b_sinks': out['v_b_sinks'], 'v_b_w_o': out['v_b_w_o'], 'v_mlp_w_up': out['v_mlp_w_up'], 'v_mlp_w_down': out['v_mlp_w_down'], 'v_ln_g': out['v_ln_g'], 'v_ln_b': out['v_ln_b']}


def _loss(weights, diff, rest, loss_target):
    with _jax.named_scope("forward"):
        args = {**rest, TWIN_DIFF_INPUT: diff, **{k: w.astype(_WEIGHT_DTYPES[k]) for k, w in weights.items()}}
        y = _forward(args)
    with _jax.named_scope("loss_head"):
        err = _jnp.square(y.astype(_jnp.float32) - loss_target)
        return 0.5 * _jnp.sum(_jnp.mean(err, axis=-1)) if err.ndim else 0.5 * err


def _adamw(w, g, m, v):
    m = ADAM_B1 * m + (1.0 - ADAM_B1) * g
    v = ADAM_B2 * v + (1.0 - ADAM_B2) * _jnp.square(g)
    m_hat = m / (1.0 - ADAM_B1 ** ADAM_STEP)
    v_hat = v / (1.0 - ADAM_B2 ** ADAM_STEP)
    delta = -ADAM_LR * (m_hat / (_jnp.sqrt(v_hat) + ADAM_EPS) + ADAM_WD * w)
    return delta, m, v


def reference(x, meta_tokens, a_mu, a_w_r, a_w_k, a_w_v, a_w_o, a_w0, a_w1, a_w2, a_a0, a_a1, a_a2, a_g1, a_g2, a_k_k, a_k_a, a_r_k, a_gn_w, a_gn_b, kv_w_k, kv_w_v, b_w_q, b_sinks, b_w_o, mlp_w_up, mlp_w_down, ln_g, ln_b, loss_target, m_meta_tokens, m_a_mu, m_a_w_r, m_a_w_k, m_a_w_v, m_a_w_o, m_a_w0, m_a_w1, m_a_w2, m_a_a0, m_a_a1, m_a_a2, m_a_g1, m_a_g2, m_a_k_k, m_a_k_a, m_a_r_k, m_a_gn_w, m_a_gn_b, m_kv_w_k, m_kv_w_v, m_b_w_q, m_b_sinks, m_b_w_o, m_mlp_w_up, m_mlp_w_down, m_ln_g, m_ln_b, v_meta_tokens, v_a_mu, v_a_w_r, v_a_w_k, v_a_w_v, v_a_w_o, v_a_w0, v_a_w1, v_a_w2, v_a_a0, v_a_a1, v_a_a2, v_a_g1, v_a_g2, v_a_k_k, v_a_k_a, v_a_r_k, v_a_gn_w, v_a_gn_b, v_kv_w_k, v_kv_w_v, v_b_w_q, v_b_sinks, v_b_w_o, v_mlp_w_up, v_mlp_w_down, v_ln_g, v_ln_b):
    given = dict(x=x, meta_tokens=meta_tokens, a_mu=a_mu, a_w_r=a_w_r, a_w_k=a_w_k, a_w_v=a_w_v, a_w_o=a_w_o, a_w0=a_w0, a_w1=a_w1, a_w2=a_w2, a_a0=a_a0, a_a1=a_a1, a_a2=a_a2, a_g1=a_g1, a_g2=a_g2, a_k_k=a_k_k, a_k_a=a_k_a, a_r_k=a_r_k, a_gn_w=a_gn_w, a_gn_b=a_gn_b, kv_w_k=kv_w_k, kv_w_v=kv_w_v, b_w_q=b_w_q, b_sinks=b_sinks, b_w_o=b_w_o, mlp_w_up=mlp_w_up, mlp_w_down=mlp_w_down, ln_g=ln_g, ln_b=ln_b, loss_target=loss_target, m_meta_tokens=m_meta_tokens, m_a_mu=m_a_mu, m_a_w_r=m_a_w_r, m_a_w_k=m_a_w_k, m_a_w_v=m_a_w_v, m_a_w_o=m_a_w_o, m_a_w0=m_a_w0, m_a_w1=m_a_w1, m_a_w2=m_a_w2, m_a_a0=m_a_a0, m_a_a1=m_a_a1, m_a_a2=m_a_a2, m_a_g1=m_a_g1, m_a_g2=m_a_g2, m_a_k_k=m_a_k_k, m_a_k_a=m_a_k_a, m_a_r_k=m_a_r_k, m_a_gn_w=m_a_gn_w, m_a_gn_b=m_a_gn_b, m_kv_w_k=m_kv_w_k, m_kv_w_v=m_kv_w_v, m_b_w_q=m_b_w_q, m_b_sinks=m_b_sinks, m_b_w_o=m_b_w_o, m_mlp_w_up=m_mlp_w_up, m_mlp_w_down=m_mlp_w_down, m_ln_g=m_ln_g, m_ln_b=m_ln_b, v_meta_tokens=v_meta_tokens, v_a_mu=v_a_mu, v_a_w_r=v_a_w_r, v_a_w_k=v_a_w_k, v_a_w_v=v_a_w_v, v_a_w_o=v_a_w_o, v_a_w0=v_a_w0, v_a_w1=v_a_w1, v_a_w2=v_a_w2, v_a_a0=v_a_a0, v_a_a1=v_a_a1, v_a_a2=v_a_a2, v_a_g1=v_a_g1, v_a_g2=v_a_g2, v_a_k_k=v_a_k_k, v_a_k_a=v_a_k_a, v_a_r_k=v_a_r_k, v_a_gn_w=v_a_gn_w, v_a_gn_b=v_a_gn_b, v_kv_w_k=v_kv_w_k, v_kv_w_v=v_kv_w_v, v_b_w_q=v_b_w_q, v_b_sinks=v_b_sinks, v_b_w_o=v_b_w_o, v_mlp_w_up=v_mlp_w_up, v_mlp_w_down=v_mlp_w_down, v_ln_g=v_ln_g, v_ln_b=v_ln_b)
    weights = {n: given[n] for n in TWIN_WEIGHTS}
    shared = {n: given[n] for n in SHARED_INPUTS}
    per_example = {n: given[n] for n in ['x']}
    grad_fn = _jax.value_and_grad(_loss, argnums=(0, 1))

    def one_microbatch(ex, loss_target):
        ex = dict(ex)
        diff = ex.pop(TWIN_DIFF_INPUT)
        return grad_fn(weights, diff, {**shared, **ex}, loss_target)

    if N_MICROBATCH == 1:
        loss, (grad_w, grad_x) = one_microbatch(per_example, given["loss_target"])
    else:
        def body(carry, xs):
            loss_sum, grad_sum = carry
            l_k, (gw_k, gx_k) = one_microbatch(xs[0], xs[1])
            with _jax.named_scope("update"):
                return (loss_sum + l_k, _jax.tree.map(_jnp.add, grad_sum, gw_k)), gx_k

        init = (_jnp.zeros((), _jnp.float32), _jax.tree.map(_jnp.zeros_like, weights))
        (loss, grad_w), grad_x = _jax.lax.scan(body, init, (per_example, given["loss_target"]))
    with _jax.named_scope("update"):
        delta_w, new_m, new_v = {}, {}, {}
        for n in TWIN_WEIGHTS:
            delta_w[n], new_m[n], new_v[n] = _adamw(weights[n], grad_w[n], given["m_" + n], given["v_" + n])
    return (loss, grad_x, *[grad_w[n] for n in TWIN_WEIGHTS], *[delta_w[n] for n in TWIN_WEIGHTS],
            *[new_m[n] for n in TWIN_WEIGHTS], *[new_v[n] for n in TWIN_WEIGHTS])
```

```python
import functools

import numpy as np
import jax
import jax.numpy as jnp
from jax import lax
from jax.experimental import pallas as pl
from jax.experimental.pallas import tpu as pltpu

F32 = jnp.float32
BF16 = jnp.bfloat16

D_MODEL = 1024
N_HEADS = 16
HEAD_DIM = 64
N_HEADS_KV = 4
GROUP = 4
KV_DIM = N_HEADS_KV * HEAD_DIM
N_META = 16
BLOCK = 128
PAD_FRONT = BLOCK - N_META
TOK0 = PAD_FRONT + N_META
N_FF_CHUNK = 4
N_SHARD = 4
N_DEV = 8
GN_EPS = 64e-5
LN_EPS = 1e-5
ROPE_THETA = 10000.0
ALPHA = 4.0 ** 0.25
ADAM_LR, ADAM_B1, ADAM_B2, ADAM_EPS, ADAM_WD, ADAM_STEP = 0.001, 0.9, 0.999, 1e-08, 0.01, 10
SCAN_T = 64
PAIR = 128
KVW = GROUP * HEAD_DIM
VMEM_LIMIT = 56 * 1024 * 1024
HI = lax.Precision.HIGHEST
MESH = pl.DeviceIdType.MESH


def _dot(a, b, ca, cb):
    return lax.dot_general(a.astype(BF16), b.astype(BF16), (((ca,), (cb,)), ((), ())),
                           preferred_element_type=F32)


@jax.custom_vjp
def mm(a, b):
    return _dot(a, b, 1, 0)


def _mm_fwd(a, b):
    return mm(a, b), b


def _mm_bwd(b, g):
    return _dot(g, b, 1, 1), jnp.zeros_like(b)


mm.defvjp(_mm_fwd, _mm_bwd)


def tmm(x, w, taps, xs):
    y = mm(x, w)
    if taps is not None:
        y = y + taps[len(xs)]
    xs.append(x)
    return y


def vjp_taps(core, tap_shapes, args, cot):
    taps = [jnp.zeros(s, F32) for s in tap_shapes]
    _, vjp, xs = jax.vjp(core, taps, *args, has_aux=True)
    out = vjp(cot)
    return out[1:], [_dot(x, g, 0, 0) for x, g in zip(xs, out[0])]


def _split3(x):
    x1 = x.astype(BF16)
    r1 = x - x1.astype(F32)
    x2 = r1.astype(BF16)
    x3 = (r1 - x2.astype(F32)).astype(BF16)
    return x1, x2, x3


def _exact_dot(x, m01, cb=0):
    acc = None
    for piece in _split3(x):
        t = lax.dot_general(piece, m01, (((1,), (cb,)), ((), ())), preferred_element_type=F32)
        acc = t if acc is None else acc + t
    return acc


def _head_matrices():
    e = np.zeros((D_MODEL, N_HEADS), np.float32)
    e[np.arange(D_MODEL), np.arange(D_MODEL) // HEAD_DIM] = 1.0
    return jnp.asarray(e, BF16), jnp.asarray(e.T, BF16)


@jax.custom_vjp
def hsum(x, e, et):
    return _exact_dot(x, e)


@jax.custom_vjp
def hbc(s, e, et):
    return _exact_dot(s, et)


hsum.defvjp(lambda x, e, et: (_exact_dot(x, e), (e, et)),
            lambda res, g: (hbc(g, *res), jnp.zeros_like(res[0]), jnp.zeros_like(res[1])))
hbc.defvjp(lambda s, e, et: (_exact_dot(s, et), (e, et)),
           lambda res, g: (hsum(g, *res), jnp.zeros_like(res[0]), jnp.zeros_like(res[1])))


def _sigmoid(u):
    return 0.5 * (jnp.tanh(0.5 * u) + 1.0)


def _softplus(u):
    return jnp.maximum(u, 0.0) + jnp.log(1.0 + jnp.exp(-jnp.abs(u)))


def _layer_norm(z, g, b):
    mu = jnp.mean(z, axis=-1, keepdims=True)
    zc = z - mu
    var = jnp.mean(zc * zc, axis=-1, keepdims=True)
    return zc * lax.rsqrt(var + LN_EPS) * g + b


def _zero_map(nd):
    return lambda c, i: (0,) * nd


def _params():
    return pltpu.CompilerParams(dimension_semantics=("arbitrary", "arbitrary"), vmem_limit_bytes=VMEM_LIMIT)


def rowwise(name, fn, rows, consts, out_rows, out_accs, tm, nc=1):
    lp = rows[0].shape[-2]
    nt = lp // tm
    assert nt * tm == lp, (name, lp, tm)
    in_specs, args = [], []
    for a in rows:
        if a.ndim == 2:
            in_specs.append(pl.BlockSpec((tm, a.shape[1]), lambda c, i: (i, 0)))
        else:
            in_specs.append(pl.BlockSpec((a.shape[0], tm, a.shape[2]), lambda c, i: (0, i, 0)))
        args.append(a)
    for cst in consts:
        if isinstance(cst, tuple):
            arr, bs, im = cst
            in_specs.append(pl.BlockSpec(bs, im))
        else:
            arr = cst
            in_specs.append(pl.BlockSpec(arr.shape, _zero_map(arr.ndim), pipeline_mode=pl.Buffered(1)))
        args.append(arr)
    out_shape, out_specs, acc_per_chunk = [], [], []
    for spec in out_rows:
        if len(spec) == 3 and spec[2]:
            out_shape.append(jax.ShapeDtypeStruct((nc, lp, spec[0]), spec[1]))
            out_specs.append(pl.BlockSpec((None, tm, spec[0]), lambda c, i: (c, i, 0)))
        else:
            out_shape.append(jax.ShapeDtypeStruct((lp, spec[0]), spec[1]))
            out_specs.append(pl.BlockSpec((tm, spec[0]), lambda c, i: (i, 0)))
    for spec in out_accs:
        out_shape.append(jax.ShapeDtypeStruct(spec[0], spec[1]))
        if len(spec) == 4:
            out_specs.append(pl.BlockSpec(spec[2], spec[3]))
            acc_per_chunk.append(True)
        else:
            out_specs.append(pl.BlockSpec(spec[0], _zero_map(len(spec[0])), pipeline_mode=pl.Buffered(1)))
            acc_per_chunk.append(False)
    n_in, n_or = len(args), len(out_rows)

    def body(*refs):
        c = pl.program_id(0)
        i = pl.program_id(1)
        vals = [r[...] for r in refs[:n_in]]
        outs_r, outs_a = fn(c, i, *vals)
        for ref, val in zip(refs[n_in:n_in + n_or], outs_r):
            ref[...] = val.astype(ref.dtype)
        for ref, val, per_chunk in zip(refs[n_in + n_or:], outs_a, acc_per_chunk):
            first = (i == 0) if per_chunk else jnp.logical_and(i == 0, c == 0)

            @pl.when(first)
            def _():
                ref[...] = val.astype(ref.dtype)

            @pl.when(jnp.logical_not(first))
            def _():
                ref[...] += val.astype(ref.dtype)

    outs = pl.pallas_call(body, name=name, grid=(nc, nt), in_specs=in_specs, out_specs=out_specs,
                          out_shape=out_shape, compiler_params=_params())(*args)
    return outs[:n_or], outs[n_or:]


def _row_ids(i, tm):
    return i * tm + lax.broadcasted_iota(jnp.int32, (tm, 1), 0)


PRE_TAPS = (D_MODEL, D_MODEL, D_MODEL, 64, D_MODEL, 64, D_MODEL, 128, D_MODEL)


def rwkv_pre(e, et, ws, taps, h, hp, mu_r, mu_w, mu_k, mu_v, mu_a, mu_g, w0, a0, k_k, k_a):
    w_r, w_k, w_v, w1, w2, a1, a2, g1, g2 = ws
    xs = []
    xx = hp - h
    r = tmm(h + xx * mu_r, w_r, taps, xs)
    k = tmm(h + xx * mu_k, w_k, taps, xs)
    v = tmm(h + xx * mu_v, w_v, taps, xs)
    wraw = -_softplus(-(w0 + tmm(jnp.tanh(tmm(h + xx * mu_w, w1, taps, xs)), w2, taps, xs))) - 0.5
    lw = -jnp.exp(wraw)
    a = _sigmoid(a0 + tmm(tmm(h + xx * mu_a, a1, taps, xs), a2, taps, xs))
    g = tmm(_sigmoid(tmm(h + xx * mu_g, g1, taps, xs)), g2, taps, xs)
    kk = k * k_k
    ss = hsum(kk * kk, e, et)
    pos = ss > 0.0
    nrm = jnp.where(pos, jnp.sqrt(jnp.where(pos, ss, 1.0)), 0.0)
    kk = kk * hbc(1.0 / jnp.maximum(nrm, 1e-12), e, et)
    k2 = k * (1.0 + (a - 1.0) * k_a)
    return (r, lw, k2, v, -kk, kk * a, g), xs


def rwkv_post(e, et, w_o, taps, y, r, k2, v, g, h0, gn_w, gn_b, rk, lg, lb):
    xs = []
    inv_n = 1.0 / HEAD_DIM
    yc = y - hbc(hsum(y, e, et) * inv_n, e, et)
    yv = hsum(yc * yc, e, et) * inv_n
    yn = yc * hbc(lax.rsqrt(yv + GN_EPS), e, et) * gn_w + gn_b
    bonus = hbc(hsum(r * k2 * rk, e, et), e, et) * v
    mix = tmm((yn + bonus) * g, w_o, taps, xs)
    return _layer_norm(ALPHA * h0 + mix, lg, lb), xs


def mlp_chunk(wup, wdown, taps, h):
    xs = []
    u = jnp.maximum(tmm(h, wup, taps, xs), 0.0)
    return tmm(u * u, wdown, taps, xs), xs


def _rot_half(t):
    n = t.shape[-1]
    lane = lax.broadcasted_iota(jnp.int32, t.shape, t.ndim - 1)
    lo = (lane % HEAD_DIM) < (HEAD_DIM // 2)
    return jnp.where(lo, -pltpu.roll(t, n - HEAD_DIM // 2, t.ndim - 1), pltpu.roll(t, HEAD_DIM // 2, t.ndim - 1))


@jax.custom_vjp
def rot_half(t):
    return _rot_half(t)


rot_half.defvjp(lambda t: (_rot_half(t), None), lambda _, g: (-_rot_half(g),))


def _tile_lanes(t, width):
    return jnp.concatenate([t] * (width // t.shape[-1]), axis=-1)


def qkv_proj(cos, sin, wq, wk, wv, taps, h):
    xs = []
    q = tmm(h, wq, taps, xs)
    k = tmm(h, wk, taps, xs)
    v = tmm(h, wv, taps, xs)
    cq, sq = _tile_lanes(cos, D_MODEL), _tile_lanes(sin, D_MODEL)
    ck, sk = _tile_lanes(cos, KV_DIM), _tile_lanes(sin, KV_DIM)
    return (q * cq + rot_half(q) * sq, k * ck + rot_half(k) * sk, v), xs


def attn_out(w_o, taps, o, h, lg, lb):
    xs = []
    return _layer_norm(ALPHA * h + tmm(o, w_o, taps, xs), lg, lb), xs


def _scan_consts():
    t = SCAN_T
    tri = np.tril(np.ones((t, t), np.float32))
    rows = np.arange(2 * t)
    same = (rows[:, None] // t) == (rows[None, :] // t)
    strict = same & ((rows[None, :] % t) < (rows[:, None] % t))
    incl = same & ((rows[None, :] % t) <= (rows[:, None] % t))
    lane = np.arange(PAIR)
    masks = np.zeros((8, PAIR), np.float32)
    masks[0] = (lane // HEAD_DIM) == 0
    masks[1] = (lane // HEAD_DIM) == 1
    return (jnp.asarray(tri), jnp.asarray(strict.astype(np.float32)), jnp.asarray(incl.astype(np.float32)),
            jnp.asarray(masks), jnp.asarray(np.eye(2 * t, dtype=np.float32)))


def _dotf(a, b, ca, cb):
    return lax.dot_general(a, b, (((ca,), (cb,)), ((), ())), precision=HI, preferred_element_type=F32)


@jax.custom_vjp
def _unstack2(x):
    t = x.shape[0] // 2
    return x[:t] + x[t:]


_unstack2.defvjp(lambda x: (_unstack2(x), None), lambda _, g: (jnp.concatenate([g, g], axis=0),))


@jax.custom_vjp
def _last_row(x):
    return x[x.shape[0] - 1:, :]


def _last_row_bwd(_, g):
    rows = lax.broadcasted_iota(jnp.int32, (SCAN_T, g.shape[1]), 0)
    return (jnp.where(rows == SCAN_T - 1, jnp.broadcast_to(g, (SCAN_T, g.shape[1])), 0.0),)


_last_row.defvjp(lambda x: (_last_row(x), None), _last_row_bwd)


def scan_chunk(tri, strict, incl, m0, m1, eye, r, lw, k, v, a, b, s0):
    def stack(x):
        return jnp.concatenate([x * m0, x * m1], axis=0)

    cl = _dotf(tri, lw, 1, 0)
    gam = jnp.exp(cl)
    ginv = jnp.exp(-cl)
    a_s = stack(a * jnp.exp(cl - lw))
    r_s = stack(r * gam)
    b_s = stack(b * ginv)
    k_s = stack(k * ginv)
    v_s = stack(v)
    n_ab = jnp.where(strict > 0, _dotf(a_s, b_s, 1, 1), 0.0)
    n_ak = jnp.where(strict > 0, _dotf(a_s, k_s, 1, 1), 0.0)
    r_ab = jnp.where(incl > 0, _dotf(r_s, b_s, 1, 1), 0.0)
    r_ak = jnp.where(incl > 0, _dotf(r_s, k_s, 1, 1), 0.0)
    rhs = _dotf(a_s, s0, 1, 1) + _dotf(n_ak, v_s, 1, 0)
    minv = eye + n_ab
    p = n_ab
    for _ in range(5):
        p = _dotf(p, p, 1, 0)
        minv = minv + _dotf(minv, p, 1, 0)
    u_s = _dotf(minv, rhs, 1, 0)
    y = _unstack2(_dotf(r_s, s0, 1, 1) + _dotf(r_ab, u_s, 1, 0) + _dotf(r_ak, v_s, 1, 0))
    g_end = _last_row(gam)
    s1 = s0 * g_end + _dotf(u_s, b_s * g_end, 0, 0) + _dotf(v_s, k_s * g_end, 0, 0)
    return y, s1


def _scan_specs(consts, order):
    row = pl.BlockSpec((SCAN_T, PAIR), lambda p, c: (order(c), p))
    state = pl.BlockSpec((None, None, PAIR, PAIR), lambda p, c: (order(c), p, 0, 0))
    return row, state, [pl.BlockSpec(x.shape, _zero_map(x.ndim)) for x in consts]


def scan_fwd(r, lw, k, v, a, b):
    lp = r.shape[0]
    nch = lp // SCAN_T
    npair = D_MODEL // PAIR
    consts = _scan_consts()
    row, state, cspecs = _scan_specs(consts, lambda c: c)

    def body(tri, strict, incl, masks, eye, r_ref, lw_ref, k_ref, v_ref, a_ref, b_ref, y_ref, s_ref, carry):
        @pl.when(pl.program_id(1) == 0)
        def _():
            carry[...] = jnp.zeros_like(carry)

        s0 = carry[...]
        s_ref[...] = s0
        y, s1 = scan_chunk(tri[...], strict[...], incl[...], masks[0:1, :], masks[1:2, :], eye[...], r_ref[...],
                           lw_ref[...], k_ref[...], v_ref[...], a_ref[...], b_ref[...], s0)
        y_ref[...] = y
        carry[...] = s1

    return pl.pallas_call(
        body, name="rwkv_scan_fwd", grid=(npair, nch), in_specs=cspecs + [row] * 6, out_specs=[row, state],
        out_shape=[jax.ShapeDtypeStruct((lp, D_MODEL), F32), jax.ShapeDtypeStruct((nch, npair, PAIR, PAIR), F32)],
        scratch_shapes=[pltpu.VMEM((PAIR, PAIR), F32)], compiler_params=_params(),
    )(*consts, r, lw, k, v, a, b)


def scan_bwd(r, lw, k, v, a, b, s_saved, dy):
    lp = r.shape[0]
    nch = lp // SCAN_T
    npair = D_MODEL // PAIR
    consts = _scan_consts()
    row, state, cspecs = _scan_specs(consts, lambda c: nch - 1 - c)

    def body(tri, strict, incl, masks, eye, r_ref, lw_ref, k_ref, v_ref, a_ref, b_ref, s_ref, dy_ref,
             dr_ref, dlw_ref, dk_ref, dv_ref, da_ref, db_ref, carry):
        @pl.when(pl.program_id(1) == 0)
        def _():
            carry[...] = jnp.zeros_like(carry)

        fn = functools.partial(scan_chunk, tri[...], strict[...], incl[...], masks[0:1, :], masks[1:2, :], eye[...])
        _, vjp = jax.vjp(fn, r_ref[...], lw_ref[...], k_ref[...], v_ref[...], a_ref[...], b_ref[...], s_ref[...])
        dr, dlw, dk, dv, da, db, ds0 = vjp((dy_ref[...], carry[...]))
        dr_ref[...] = dr
        dlw_ref[...] = dlw
        dk_ref[...] = dk
        dv_ref[...] = dv
        da_ref[...] = da
        db_ref[...] = db
        carry[...] = ds0

    return pl.pallas_call(
        body, name="rwkv_scan_bwd", grid=(npair, nch), in_specs=cspecs + [row] * 6 + [state, row],
        out_specs=[row] * 6, out_shape=[jax.ShapeDtypeStruct((lp, D_MODEL), F32)] * 6,
        scratch_shapes=[pltpu.VMEM((PAIR, PAIR), F32)], compiler_params=_params(),
    )(*consts, r, lw, k, v, a, b, s_saved, dy)


def _spread_matrices():
    rep = np.zeros((N_HEADS_KV, KV_DIM, KVW), np.float32)
    for h in range(N_HEADS_KV):
        for g in range(GROUP):
            rep[h, h * HEAD_DIM + np.arange(HEAD_DIM), g * HEAD_DIM + np.arange(HEAD_DIM)] = 1.0
    return jnp.asarray(rep, BF16)


def _attn_common(n, q, kp, kc, vp, vc, rep, sink):
    lane = lax.broadcasted_iota(jnp.int32, (1, KVW), 1)
    gmask = [(lane // HEAD_DIM == g).astype(F32) for g in range(GROUP)]
    q_s = jnp.concatenate([q * gmask[g] for g in range(GROUP)], axis=0)
    keys = _dot(jnp.concatenate([kp, kc], axis=0), rep, 1, 0)
    vals = _dot(jnp.concatenate([vp, vc], axis=0), rep, 1, 0)
    s = _dot(q_s, keys, 1, 1) * (HEAD_DIM ** -0.5)
    qi = lax.broadcasted_iota(jnp.int32, (GROUP * BLOCK, 2 * BLOCK), 0) % BLOCK
    kj = lax.broadcasted_iota(jnp.int32, (GROUP * BLOCK, 2 * BLOCK), 1)
    rel = BLOCK + qi - kj
    valid = (rel >= 0) & (rel < BLOCK) & ((n - 1) * BLOCK + kj >= PAD_FRONT)
    s = jnp.where(valid, s, -1e30)
    sink_col = jnp.concatenate([jnp.broadcast_to(sink[g:g + 1, 0:1], (BLOCK, 1)) for g in range(GROUP)], axis=0)
    m = jnp.maximum(jnp.max(s, axis=-1, keepdims=True), sink_col)
    ex = jnp.exp(s - m)
    ex_sink = jnp.exp(sink_col - m)
    inv = 1.0 / (jnp.sum(ex, axis=-1, keepdims=True) + ex_sink)
    return gmask, q_s, keys, vals, ex * inv, ex_sink * inv


def _unstack_groups(x_s, gmask):
    out = None
    for g in range(GROUP):
        t = x_s[g * BLOCK:(g + 1) * BLOCK] * gmask[g]
        out = t if out is None else out + t
    return out


def _attn_specs():
    qspec = pl.BlockSpec((BLOCK, KVW), lambda n, h: (n, h))
    cur = pl.BlockSpec((BLOCK, KV_DIM), lambda n, h: (n, 0))
    prev = pl.BlockSpec((BLOCK, KV_DIM), lambda n, h: (jnp.maximum(n - 1, 0), 0))
    rep = pl.BlockSpec((None, KV_DIM, KVW), lambda n, h: (h, 0, 0))
    sink = pl.BlockSpec((None, 8, PAIR), lambda n, h: (h, 0, 0))
    return qspec, cur, prev, rep, sink


def attn_fwd(q, k, v, sinks_b):
    lp = q.shape[0]
    qspec, cur, prev, rep, sink = _attn_specs()

    def body(q_ref, kp_ref, kc_ref, vp_ref, vc_ref, rep_ref, sink_ref, o_ref):
        gmask, _, _, vals, p, _ = _attn_common(pl.program_id(0), q_ref[...], kp_ref[...], kc_ref[...], vp_ref[...],
                                               vc_ref[...], rep_ref[...], sink_ref[...])
        o_ref[...] = _unstack_groups(_dot(p, vals, 1, 0), gmask)

    return pl.pallas_call(
        body, name="swa_fwd", grid=(lp // BLOCK, N_HEADS_KV), in_specs=[qspec, prev, cur, prev, cur, rep, sink],
        out_specs=qspec, out_shape=jax.ShapeDtypeStruct((lp, D_MODEL), F32), compiler_params=_params(),
    )(q, k, k, v, v, _spread_matrices(), sinks_b)


def attn_bwd(q, k, v, sinks_b, do):
    lp = q.shape[0]
    qspec, cur, prev, rep, sink = _attn_specs()

    def body(q_ref, kp_ref, kc_ref, vp_ref, vc_ref, rep_ref, sink_ref, do_ref, dq_ref, dkc_ref, dkp_ref, dvc_ref,
             dvp_ref, dsink_ref):
        n = pl.program_id(0)
        h = pl.program_id(1)
        gmask, q_s, keys, vals, p, p_sink = _attn_common(n, q_ref[...], kp_ref[...], kc_ref[...], vp_ref[...],
                                                         vc_ref[...], rep_ref[...], sink_ref[...])
        do = do_ref[...]
        do_s = jnp.concatenate([do * gmask[g] for g in range(GROUP)], axis=0)
        dp = _dot(do_s, vals, 1, 1)
        delta = jnp.sum(p * dp, axis=-1, keepdims=True)
        ds = p * (dp - delta) * (HEAD_DIM ** -0.5)
        dq_ref[...] = _unstack_groups(_dot(ds, keys, 1, 0), gmask)
        dkeys = _exact_dot(_dot(ds, q_s, 0, 0), rep_ref[...], cb=1)
        dvals = _exact_dot(_dot(p, do_s, 0, 0), rep_ref[...], cb=1)
        dsk = -(p_sink * delta)
        rows = [jnp.broadcast_to(jnp.sum(dsk[g * BLOCK:(g + 1) * BLOCK], axis=0, keepdims=True), (1, PAIR))
                for g in range(GROUP)]
        dsink = jnp.concatenate(rows + [jnp.zeros((8 - GROUP, PAIR), F32)], axis=0)

        @pl.when(h == 0)
        def _():
            dkp_ref[...] = dkeys[:BLOCK]
            dkc_ref[...] = dkeys[BLOCK:]
            dvp_ref[...] = dvals[:BLOCK]
            dvc_ref[...] = dvals[BLOCK:]

        @pl.when(h > 0)
        def _():
            dkp_ref[...] += dkeys[:BLOCK]
            dkc_ref[...] += dkeys[BLOCK:]
            dvp_ref[...] += dvals[:BLOCK]
            dvc_ref[...] += dvals[BLOCK:]

        @pl.when(n == 0)
        def _():
            dsink_ref[h] = dsink

        @pl.when(n > 0)
        def _():
            dsink_ref[h] += dsink

    kv = jax.ShapeDtypeStruct((lp, KV_DIM), F32)
    sink_all = pl.BlockSpec((N_HEADS_KV, 8, PAIR), lambda n, h: (0, 0, 0))
    return pl.pallas_call(
        body, name="swa_bwd", grid=(lp // BLOCK, N_HEADS_KV), in_specs=[qspec, prev, cur, prev, cur, rep, sink, qspec],
        out_specs=[qspec, cur, cur, cur, cur, sink_all],
        out_shape=[jax.ShapeDtypeStruct((lp, D_MODEL), F32), kv, kv, kv, kv,
                   jax.ShapeDtypeStruct((N_HEADS_KV, 8, PAIR), F32)],
        compiler_params=_params(),
    )(q, k, k, v, v, _spread_matrices(), sinks_b, do)


def _pick_tm(lp, want):
    for tm in (384, 192, 128, 64):
        if tm <= want and lp % tm == 0:
            return tm
    raise ValueError(lp)


def _acc(shape):
    return (tuple(shape), F32)


def _ff_all(w, layer):
    return (w, (N_FF_CHUNK, None, D_MODEL, D_MODEL), lambda c, i: (0, layer, 0, 0))


def _ff_one(w, layer):
    return (w, (None, None, D_MODEL, D_MODEL), lambda c, i: (c, layer, 0, 0))


def _mlp_layer_fwd(name, h, wup, wdown, layer, lg, lb, tm):
    def fn(c, i, h, wup, wdown, lg, lb):
        out = None
        for s in range(N_FF_CHUNK):
            t = mlp_chunk(wup[s], wdown[s], None, h)[0]
            out = t if out is None else out + t
        z = ALPHA * h + out
        return (_layer_norm(z, lg, lb), z), ()

    (h_out, z), _ = rowwise(name, fn, [h], [_ff_all(wup, layer), _ff_all(wdown, layer), lg, lb],
                            [(D_MODEL, F32), (D_MODEL, F32)], [], tm)
    return h_out, z


def _mlp_layer_bwd(name, h_in, z, dh_parts, wup, wdown, layer, lg, lb, tm):
    n_parts = len(dh_parts)

    def fn_ln(c, i, z, *rest):
        dh = rest[0]
        for extra in rest[1:n_parts]:
            dh = dh + extra
        _, vjp = jax.vjp(_layer_norm, z, rest[n_parts], rest[n_parts + 1])
        dz, dlg, dlb = vjp(dh)
        return (dz,), (dlg, dlb)

    (dz,), (dlg, dlb) = rowwise(name + "_ln", fn_ln, [z] + list(dh_parts), [lg, lb], [(D_MODEL, F32)],
                                [_acc((1, D_MODEL)), _acc((1, D_MODEL))], tm)

    def fn_mlp(c, i, h, dz, wup, wdown):
        tile = h.shape[0]
        (dx,), dws = vjp_taps(functools.partial(mlp_chunk, wup, wdown), [(tile, D_MODEL)] * 2, [h], dz)
        return (dx,), dws

    aspec = ((N_FF_CHUNK, D_MODEL, D_MODEL), F32, (None, D_MODEL, D_MODEL), lambda c, i: (c, 0, 0))
    (dx,), (dwup, dwdown) = rowwise(name + "_mm", fn_mlp, [h_in, dz], [_ff_one(wup, layer), _ff_one(wdown, layer)],
                                    [(D_MODEL, F32, True)], [aspec, aspec], tm, nc=N_FF_CHUNK)
    return dz, dx, dwup, dwdown, dlg, dlb


def _sum_parts(dz, dx):
    out = ALPHA * dz
    for s in range(N_FF_CHUNK):
        out = out + dx[s]
    return out


def local_step(x, loss_target, p):
    seq = x.shape[0]
    lp = TOK0 + seq
    tm = _pick_tm(lp, 384)
    tms = _pick_tm(lp, 128)
    e, et = _head_matrices()
    h0 = jnp.concatenate([jnp.zeros((PAD_FRONT, D_MODEL), F32), p["meta_tokens"], x], axis=0)
    hp = jnp.concatenate([jnp.zeros((1, D_MODEL), F32), h0[:-1]], axis=0)
    tgt = jnp.concatenate([jnp.zeros((TOK0, D_MODEL), F32), loss_target], axis=0)
    pos = jnp.maximum(jnp.arange(lp, dtype=F32) - PAD_FRONT, 0.0)
    inv_freq = 1.0 / (ROPE_THETA ** (jnp.arange(0, HEAD_DIM, 2, dtype=F32) / HEAD_DIM))
    ang = pos[:, None] * inv_freq[None, :]
    cos = jnp.tile(jnp.cos(ang), (1, PAIR // (HEAD_DIM // 2)))
    sin = jnp.tile(jnp.sin(ang), (1, PAIR // (HEAD_DIM // 2)))

    pre_vec = [p["a_mu"][j:j + 1] for j in range(6)] + [p["a_w0"], p["a_a0"], p["a_k_k"], p["a_k_a"]]
    pre_w = [p["a_w_r"], p["a_w_k"], p["a_w_v"], p["a_w1"], p["a_w2"], p["a_a1"], p["a_a2"], p["a_g1"], p["a_g2"]]
    n_vec = len(pre_vec)

    def fn_pre(c, i, h, hp, e, et, *ws):
        return rwkv_pre(e, et, ws[n_vec:], None, h, hp, *ws[:n_vec])[0], ()

    (r, lw, k2, v, an, bn, g), _ = rowwise("rwkv_pre", fn_pre, [h0, hp], [e, et] + pre_vec + pre_w,
                                           [(D_MODEL, F32)] * 7, [], tms)
    y, s_saved = scan_fwd(r, lw, k2, v, an, bn)

    post_c = [p["a_w_o"], p["a_gn_w"], p["a_gn_b"], p["a_r_k"], p["ln_g00"], p["ln_b00"]]

    def fn_post(c, i, y, r, k2, v, g, h0, e, et, w_o, *vecs):
        return (rwkv_post(e, et, w_o, None, y, r, k2, v, g, h0, *vecs)[0],), ()

    (h1,), _ = rowwise("rwkv_post", fn_post, [y, r, k2, v, g, h0], [e, et] + post_c, [(D_MODEL, F32)], [], tm)
    h2, z2 = _mlp_layer_fwd("mlp0_fwd", h1, p["mlp_up"], p["mlp_down"], 0, p["ln_g01"], p["ln_b01"], tm)

    qkv_w = [p["b_w_q"], p["kv_w_k"], p["kv_w_v"]]

    def fn_qkv(c, i, h, cos, sin, wq, wk, wv):
        return qkv_proj(cos, sin, wq, wk, wv, None, h)[0], ()

    (q, k, vv), _ = rowwise("qkv_proj", fn_qkv, [h2, cos, sin], qkv_w,
                            [(D_MODEL, F32), (KV_DIM, F32), (KV_DIM, F32)], [], tm)
    sinks_b = jnp.broadcast_to(p["b_sinks"].reshape(N_HEADS_KV, GROUP, 1), (N_HEADS_KV, GROUP, PAIR))
    sinks_b = jnp.concatenate([sinks_b, jnp.zeros((N_HEADS_KV, 8 - GROUP, PAIR), F32)], axis=1)
    o = attn_fwd(q, k, vv, sinks_b)

    ao_c = [p["b_w_o"], p["ln_g10"], p["ln_b10"]]

    def fn_ao(c, i, o, h, w_o, lg, lb):
        return (attn_out(w_o, None, o, h, lg, lb)[0],), ()

    (h3,), _ = rowwise("attn_out", fn_ao, [o, h2], ao_c, [(D_MODEL, F32)], [], tm)
    h4, z4 = _mlp_layer_fwd("mlp1_fwd", h3, p["mlp_up"], p["mlp_down"], 1, p["ln_g11"], p["ln_b11"], tm)

    def fn_loss(c, i, h4, tgt):
        real = (_row_ids(i, tm) >= TOK0).astype(F32)
        err = (h4 - tgt) * real
        part = 0.5 * jnp.sum(jnp.sum(err * err, axis=-1, keepdims=True), axis=0, keepdims=True) / D_MODEL
        return (err * (1.0 / D_MODEL),), (jnp.broadcast_to(part, (8, PAIR)),)

    (dh4,), (loss_acc,) = rowwise("loss", fn_loss, [h4, tgt], [], [(D_MODEL, F32)], [_acc((8, PAIR))], tm)
    loss = loss_acc[0, 0]

    grads = {}
    dz4, dx4, grads["mlp_up1"], grads["mlp_down1"], grads["ln_g11"], grads["ln_b11"] = _mlp_layer_bwd(
        "mlp1_bwd", h3, z4, [dh4], p["mlp_up"], p["mlp_down"], 1, p["ln_g11"], p["ln_b11"], tm)

    def fn_ao_b(c, i, dz, dx, o, h, w_o, lg, lb):
        (do, dh, dlg, dlb), (dw_o,) = vjp_taps(functools.partial(attn_out, w_o), [(tms, D_MODEL)], [o, h, lg, lb],
                                               _sum_parts(dz, dx))
        return (do, dh), (dw_o, dlg, dlb)

    (do, dh2_a), (grads["b_w_o"], grads["ln_g10"], grads["ln_b10"]) = rowwise(
        "attn_out_bwd", fn_ao_b, [dz4, dx4, o, h2], ao_c, [(D_MODEL, F32)] * 2,
        [_acc((D_MODEL, D_MODEL)), _acc((1, D_MODEL)), _acc((1, D_MODEL))], tms)

    dq, dkc, dkp, dvc, dvp, dsinks = attn_bwd(q, k, vv, sinks_b, do)
    grads["b_sinks"] = dsinks[:, :GROUP, 0].reshape(1, N_HEADS)
    zblk = jnp.zeros((BLOCK, KV_DIM), F32)
    dkp_s = jnp.concatenate([dkp[BLOCK:], zblk], axis=0)
    dvp_s = jnp.concatenate([dvp[BLOCK:], zblk], axis=0)

    def fn_qkv_b(c, i, h, cos, sin, dq, dkc, dkp, dvc, dvp, wq, wk, wv):
        return vjp_taps(functools.partial(qkv_proj, cos, sin, wq, wk, wv),
                        [(tms, D_MODEL), (tms, KV_DIM), (tms, KV_DIM)], [h], (dq, dkc + dkp, dvc + dvp))

    (dh2_q,), (grads["b_w_q"], grads["kv_w_k"], grads["kv_w_v"]) = rowwise(
        "qkv_proj_bwd", fn_qkv_b, [h2, cos, sin, dq, dkc, dkp_s, dvc, dvp_s], qkv_w, [(D_MODEL, F32)],
        [_acc((D_MODEL, D_MODEL)), _acc((D_MODEL, KV_DIM)), _acc((D_MODEL, KV_DIM))], tms)

    dz2, dx2, grads["mlp_up0"], grads["mlp_down0"], grads["ln_g01"], grads["ln_b01"] = _mlp_layer_bwd(
        "mlp0_bwd", h1, z2, [dh2_a, dh2_q], p["mlp_up"], p["mlp_down"], 0, p["ln_g01"], p["ln_b01"], tm)

    def fn_post_b(c, i, dz, dx, y, r, k2, v, g, h0, e, et, w_o, *vecs):
        out, dws = vjp_taps(functools.partial(rwkv_post, e, et, w_o), [(tms, D_MODEL)],
                            [y, r, k2, v, g, h0] + list(vecs), _sum_parts(dz, dx))
        return out[:6], tuple(dws) + tuple(out[6:])

    (dy, dr_c, dk_c, dv_c, dg, dh0_c), post_g = rowwise(
        "rwkv_post_bwd", fn_post_b, [dz2, dx2, y, r, k2, v, g, h0], [e, et] + post_c, [(D_MODEL, F32)] * 6,
        [_acc((D_MODEL, D_MODEL))] + [_acc((1, D_MODEL))] * 5, tms)
    for name, val in zip(["a_w_o", "a_gn_w", "a_gn_b", "a_r_k", "ln_g00", "ln_b00"], post_g):
        grads[name] = val

    dr_s, dlw, dk_s, dv_s, dan, dbn = scan_bwd(r, lw, k2, v, an, bn, s_saved, dy)

    def fn_pre_b(c, i, h, hp, dr_c, dr_s, dlw, dk_c, dk_s, dv_c, dv_s, dan, dbn, dg, e, et, *ws):
        real = (_row_ids(i, tms) >= PAD_FRONT).astype(F32)
        cot = tuple(t * real for t in (dr_c + dr_s, dlw, dk_c + dk_s, dv_c + dv_s, dan, dbn, dg))
        out, dws = vjp_taps(functools.partial(rwkv_pre, e, et, ws[n_vec:]), [(tms, n) for n in PRE_TAPS],
                            [h, hp] + list(ws[:n_vec]), cot)
        return out[:2], tuple(out[2:]) + tuple(dws)

    (dh0_p, dhp), pre_g = rowwise(
        "rwkv_pre_bwd", fn_pre_b, [h0, hp, dr_c, dr_s, dlw, dk_c, dk_s, dv_c, dv_s, dan, dbn, dg],
        [e, et] + pre_vec + pre_w, [(D_MODEL, F32)] * 2,
        [_acc((1, D_MODEL))] * n_vec + [_acc(w.shape) for w in pre_w], tms)
    grads["a_mu"] = jnp.concatenate(pre_g[:6], axis=0)
    for name, val in zip(["a_w0", "a_a0", "a_k_k", "a_k_a", "a_w_r", "a_w_k", "a_w_v", "a_w1", "a_w2", "a_a1",
                          "a_a2", "a_g1", "a_g2"], pre_g[6:]):
        grads[name] = val

    dhp_s = jnp.concatenate([dhp[1:], jnp.zeros((1, D_MODEL), F32)], axis=0)

    def fn_add(c, i, a, b, d):
        return (a + b + d,), ()

    (dh0,), _ = rowwise("grad_h0", fn_add, [dh0_c, dh0_p, dhp_s], [], [(D_MODEL, F32)], [], tm)
    grads["meta_tokens"] = dh0[PAD_FRONT:TOK0]
    return loss, dh0[TOK0:], grads


ANY = pl.BlockSpec(memory_space=pl.ANY)
XY_FLIPS = ((0, 1), (1, 0), (1, 1))
ALL_FLIPS = tuple((e >> 2 & 1, e >> 1 & 1, e & 1) for e in range(1, N_DEV))


def _flip(v, bit):
    return 1 - v if bit else v


def all_gather_shards(shards):
    n = len(shards)
    npeer = len(XY_FLIPS)

    def body(*refs):
        src, dst = refs[:n], refs[n:2 * n]
        send_sems, recv_sems, local_sems = refs[2 * n:]
        x, y, c = lax.axis_index("x"), lax.axis_index("y"), lax.axis_index("c")

        def copy(k, j, slot):
            fx, fy = XY_FLIPS[j]
            return pltpu.make_async_remote_copy(
                src_ref=src[k], dst_ref=dst[k].at[slot], send_sem=send_sems.at[k * npeer + j],
                recv_sem=recv_sems.at[k * npeer + j], device_id=(_flip(x, fx), _flip(y, fy), c), device_id_type=MESH)

        mine = [pltpu.make_async_copy(src[k], dst[k].at[2 * x + y], local_sems.at[k]) for k in range(n)]
        sends = [copy(k, j, 2 * x + y) for k in range(n) for j in range(npeer)]
        for cp in mine + sends:
            cp.start()
        for k in range(n):
            for j, (fx, fy) in enumerate(XY_FLIPS):
                copy(k, j, 2 * _flip(x, fx) + _flip(y, fy)).wait_recv()
        for cp in sends:
            cp.wait_send()
        for cp in mine:
            cp.wait()

    return pl.pallas_call(
        body, name="gather_weights", in_specs=[ANY] * n, out_specs=[ANY] * n,
        out_shape=[jax.ShapeDtypeStruct((N_SHARD,) + s.shape, s.dtype) for s in shards],
        scratch_shapes=[pltpu.SemaphoreType.DMA((n * npeer,)), pltpu.SemaphoreType.DMA((n * npeer,)),
                        pltpu.SemaphoreType.DMA((n,))],
    )(*shards)


def exchange_grads(sources, dests):
    n = len(sources)
    npeer = len(ALL_FLIPS)

    def body(*refs):
        src, dst = refs[:n], refs[n:n + len(dests)]
        send_sems, recv_sems, local_sems = refs[n + len(dests):]
        x, y, c = lax.axis_index("x"), lax.axis_index("y"), lax.axis_index("c")
        me = 4 * x + 2 * y + c

        def copy(k, j, sender):
            fx, fy, fc = ALL_FLIPS[j]
            px, py, pc = _flip(x, fx), _flip(y, fy), _flip(c, fc)
            _, d, l = sources[k]
            return pltpu.make_async_remote_copy(
                src_ref=src[k].at[2 * px + py], dst_ref=dst[d].at[sender, l], send_sem=send_sems.at[k * npeer + j],
                recv_sem=recv_sems.at[k * npeer + j], device_id=(px, py, pc), device_id_type=MESH)

        mine = [pltpu.make_async_copy(src[k].at[2 * x + y], dst[sources[k][1]].at[me, sources[k][2]],
                                      local_sems.at[k]) for k in range(n)]
        sends = [copy(k, j, me) for k in range(n) for j in range(npeer)]
        for cp in mine + sends:
            cp.start()
        for k in range(n):
            for j, (fx, fy, fc) in enumerate(ALL_FLIPS):
                copy(k, j, 4 * _flip(x, fx) + 2 * _flip(y, fy) + _flip(c, fc)).wait_recv()
        for cp in sends:
            cp.wait_send()
        for cp in mine:
            cp.wait()

    return pl.pallas_call(
        body, name="exchange_grads", in_specs=[ANY] * n, out_specs=[ANY] * len(dests),
        out_shape=[jax.ShapeDtypeStruct((N_DEV, nsub) + tuple(piece), F32) for nsub, piece in dests],
        scratch_shapes=[pltpu.SemaphoreType.DMA((n * npeer,)), pltpu.SemaphoreType.DMA((n * npeer,)),
                        pltpu.SemaphoreType.DMA((n,))],
    )(*[s[0] for s in sources])


ADAM_ROWS = 256


def adamw_reduce(name, recv, w, m, v):
    nsub, rows, cols = w.shape
    tr = ADAM_ROWS if rows % ADAM_ROWS == 0 else rows

    def body(recv_ref, w_ref, m_ref, v_ref, g_ref, d_ref, nm_ref, nv_ref):
        g = recv_ref[0]
        for s in range(1, N_DEV):
            g = g + recv_ref[s]
        m2 = ADAM_B1 * m_ref[...] + (1.0 - ADAM_B1) * g
        v2 = ADAM_B2 * v_ref[...] + (1.0 - ADAM_B2) * (g * g)
        m_hat = m2 / (1.0 - ADAM_B1 ** ADAM_STEP)
        v_hat = v2 / (1.0 - ADAM_B2 ** ADAM_STEP)
        g_ref[...] = g
        d_ref[...] = -ADAM_LR * (m_hat / (jnp.sqrt(v_hat) + ADAM_EPS) + ADAM_WD * w_ref[...])
        nm_ref[...] = m2
        nv_ref[...] = v2

    blk = pl.BlockSpec((None, tr, cols), lambda l, i: (l, i, 0))
    out = jax.ShapeDtypeStruct((nsub, rows, cols), F32)
    return pl.pallas_call(
        body, name=name, grid=(nsub, rows // tr),
        in_specs=[pl.BlockSpec((N_DEV, None, tr, cols), lambda l, i: (0, l, i, 0)), blk, blk, blk],
        out_specs=[blk] * 4, out_shape=[out] * 4, compiler_params=_params(),
    )(recv, w, m, v)


WEIGHT_NAMES = ("meta_tokens", "a_mu", "a_w_r", "a_w_k", "a_w_v", "a_w_o", "a_w0", "a_w1", "a_w2", "a_a0", "a_a1",
                "a_a2", "a_g1", "a_g2", "a_k_k", "a_k_a", "a_r_k", "a_gn_w", "a_gn_b", "kv_w_k", "kv_w_v", "b_w_q",
                "b_sinks", "b_w_o", "mlp_w_up", "mlp_w_down", "ln_g", "ln_b")
BIG_NAMES = ("a_w_r", "a_w_k", "a_w_v", "a_w_o", "b_w_q", "b_w_o")
PACK_MATS = (("kv_w_k", 256), ("kv_w_v", 256), ("a_w1", 64), ("a_a1", 64), ("a_g1", 128), ("a_w2", 64),
             ("a_a2", 64), ("a_g2", 128))
COLUMN_CUT = ("a_w2", "a_a2", "a_g2")
PACK_VECS = (("a_mu", 6), ("a_w0", 1), ("a_a0", 1), ("a_k_k", 1), ("a_k_a", 1), ("a_gn_w", 1), ("a_gn_b", 1),
             ("ln_g", 4), ("ln_b", 4), ("meta_tokens", 16))
PACK_REPL = (("a_r_k", 4), ("b_sinks", 1))
SHARD_W = D_MODEL // N_SHARD
N_MAT_ROWS = sum(r for _, r in PACK_MATS)
N_VEC_ROWS = sum(r for _, r in PACK_VECS)
N_PACK_ROWS = -(-(N_MAT_ROWS + N_VEC_ROWS + sum(r for _, r in PACK_REPL)) // 8) * 8
N_GATHER_VEC_ROWS = -(-N_VEC_ROWS // 8) * 8


def _pack_rows(arr):
    if arr.size == N_HEADS:
        return jnp.pad(arr.reshape(1, N_HEADS), ((0, 0), (0, SHARD_W - N_HEADS)))
    return arr.reshape(-1, SHARD_W)


def pack_small(get):
    parts = [_pack_rows(get(name)) for name, _ in PACK_MATS + PACK_VECS + PACK_REPL]
    used = sum(p.shape[0] for p in parts)
    return jnp.concatenate(parts + [jnp.zeros((N_PACK_ROWS - used, SHARD_W), F32)], axis=0)


def unpack_small(pack, shapes):
    out, off = {}, 0
    for name, rows in PACK_MATS + PACK_VECS + PACK_REPL:
        piece = pack[off:off + rows]
        off += rows
        out[name] = piece[:, :N_HEADS].reshape(shapes[name]) if name == "b_sinks" else piece.reshape(shapes[name])
    return out


def whole_weights(gathered_big, mats, vecs, a_r_k, b_sinks):
    p = {name: g.reshape(D_MODEL, D_MODEL) for name, g in zip(BIG_NAMES, gathered_big)}
    off = 0
    for name, rows in PACK_MATS:
        piece = mats[:, off:off + rows]
        off += rows
        if name in COLUMN_CUT:
            p[name] = piece.transpose(1, 0, 2).reshape(rows, D_MODEL)
        else:
            p[name] = piece.reshape(D_MODEL, rows)
    v = vecs.transpose(1, 0, 2).reshape(-1, D_MODEL)
    off = 0
    for name, rows in PACK_VECS:
        p[name] = v[off:off + rows]
        off += rows
    for i in range(2):
        for j in range(2):
            p[f"ln_g{i}{j}"] = p["ln_g"][2 * i + j:2 * i + j + 1]
            p[f"ln_b{i}{j}"] = p["ln_b"][2 * i + j:2 * i + j + 1]
    p["a_r_k"] = a_r_k.reshape(1, D_MODEL)
    p["b_sinks"] = b_sinks
    return p


def small_grad_pack(g):
    parts = []
    for name, rows in PACK_MATS:
        if name in COLUMN_CUT:
            parts.append(g[name].reshape(rows, N_SHARD, SHARD_W).transpose(1, 0, 2))
        else:
            parts.append(g[name].reshape(N_SHARD, rows, SHARD_W))
    vec_rows = [g["a_mu"]] + [g[n] for n in ("a_w0", "a_a0", "a_k_k", "a_k_a", "a_gn_w", "a_gn_b")]
    vec_rows += [g[f"ln_g{i}{j}"] for i in range(2) for j in range(2)]
    vec_rows += [g[f"ln_b{i}{j}"] for i in range(2) for j in range(2)] + [g["meta_tokens"]]
    parts.append(jnp.concatenate(vec_rows, axis=0).reshape(N_VEC_ROWS, N_SHARD, SHARD_W).transpose(1, 0, 2))
    parts.append(jnp.broadcast_to(g["a_r_k"].reshape(1, -1, SHARD_W), (N_SHARD, D_MODEL // SHARD_W, SHARD_W)))
    sinks = jnp.pad(g["b_sinks"].reshape(1, 1, N_HEADS), ((0, 0), (0, 0), (0, SHARD_W - N_HEADS)))
    parts.append(jnp.broadcast_to(sinks, (N_SHARD, 1, SHARD_W)))
    used = sum(p.shape[1] for p in parts)
    parts.append(jnp.zeros((N_SHARD, N_PACK_ROWS - used, SHARD_W), F32))
    return jnp.concatenate(parts, axis=1)


def train_step(vals):
    w = {n: vals[n] for n in WEIGHT_NAMES}
    w_pack = pack_small(lambda n: w[n])
    shards = [w[n][0].astype(BF16) for n in BIG_NAMES]
    shards += [w["mlp_w_up"].astype(BF16), w["mlp_w_down"].astype(BF16), w_pack[:N_MAT_ROWS].astype(BF16),
               w_pack[N_MAT_ROWS:N_MAT_ROWS + N_GATHER_VEC_ROWS]]
    gathered = all_gather_shards(shards)
    nb = len(BIG_NAMES)
    p = whole_weights(gathered[:nb], gathered[nb + 2], gathered[nb + 3][:, :N_VEC_ROWS], w["a_r_k"], w["b_sinks"])
    p["mlp_up"], p["mlp_down"] = gathered[nb], gathered[nb + 1]

    loss, gx, g = local_step(vals["x"][0], vals["loss_target"][0], p)
    loss = lax.psum(loss, ("x", "y", "c"))

    sources = [(g[n].reshape(N_SHARD, SHARD_W, D_MODEL), k, 0) for k, n in enumerate(BIG_NAMES)]
    dests = [(1, (SHARD_W, D_MODEL))] * nb
    for name in ("mlp_up", "mlp_down"):
        sources += [(g[name + "0"], len(dests), 0), (g[name + "1"], len(dests), 1)]
        dests.append((2, (D_MODEL, D_MODEL)))
    sources.append((small_grad_pack(g), len(dests), 0))
    dests.append((1, (N_PACK_ROWS, SHARD_W)))
    recv = exchange_grads(sources, dests)

    res = {}
    for k, n in enumerate(BIG_NAMES + ("mlp_w_up", "mlp_w_down")):
        res[n] = adamw_reduce("adamw_" + n, recv[k], w[n], vals["m_" + n], vals["v_" + n])
    packs = adamw_reduce("adamw_small", recv[-1], w_pack[None], pack_small(lambda n: vals["m_" + n])[None],
                         pack_small(lambda n: vals["v_" + n])[None])
    shapes = {n: w[n].shape for n in WEIGHT_NAMES}
    small = [unpack_small(pk[0], shapes) for pk in packs]
    outs = [loss, gx[None]]
    for t in range(4):
        outs += [res[n][t] if n in res else small[t][n] for n in WEIGHT_NAMES]
    return tuple(outs)


def kernel(x, meta_tokens, a_mu, a_w_r, a_w_k, a_w_v, a_w_o, a_w0, a_w1, a_w2, a_a0, a_a1, a_a2, a_g1, a_g2, a_k_k,
           a_k_a, a_r_k, a_gn_w, a_gn_b, kv_w_k, kv_w_v, b_w_q, b_sinks, b_w_o, mlp_w_up, mlp_w_down, ln_g, ln_b,
           loss_target, m_meta_tokens, m_a_mu, m_a_w_r, m_a_w_k, m_a_w_v, m_a_w_o, m_a_w0, m_a_w1, m_a_w2, m_a_a0,
           m_a_a1, m_a_a2, m_a_g1, m_a_g2, m_a_k_k, m_a_k_a, m_a_r_k, m_a_gn_w, m_a_gn_b, m_kv_w_k, m_kv_w_v,
           m_b_w_q, m_b_sinks, m_b_w_o, m_mlp_w_up, m_mlp_w_down, m_ln_g, m_ln_b, v_meta_tokens, v_a_mu, v_a_w_r,
           v_a_w_k, v_a_w_v, v_a_w_o, v_a_w0, v_a_w1, v_a_w2, v_a_a0, v_a_a1, v_a_a2, v_a_g1, v_a_g2, v_a_k_k,
           v_a_k_a, v_a_r_k, v_a_gn_w, v_a_gn_b, v_kv_w_k, v_kv_w_v, v_b_w_q, v_b_sinks, v_b_w_o, v_mlp_w_up,
           v_mlp_w_down, v_ln_g, v_ln_b):
    return train_step(dict(locals()))
```

```python
import functools

import numpy as np
import jax
import jax.numpy as jnp
from jax import lax
from jax.experimental import pallas as pl
from jax.experimental.pallas import tpu as pltpu

F32 = jnp.float32
BF16 = jnp.bfloat16

D_MODEL = 1024
N_HEADS = 16
HEAD_DIM = 64
N_HEADS_KV = 4
GROUP = 4
KV_DIM = N_HEADS_KV * HEAD_DIM
N_META = 16
BLOCK = 128
PAD_FRONT = BLOCK - N_META
TOK0 = PAD_FRONT + N_META
N_FF_CHUNK = 4
N_SHARD = 4
N_DEV = 8
GN_EPS = 64e-5
LN_EPS = 1e-5
ROPE_THETA = 10000.0
ALPHA = 4.0 ** 0.25
ADAM_LR, ADAM_B1, ADAM_B2, ADAM_EPS, ADAM_WD, ADAM_STEP = 0.001, 0.9, 0.999, 1e-08, 0.01, 10
SCAN_T = 64
PAIR = 128
KVW = GROUP * HEAD_DIM
VMEM_LIMIT = 56 * 1024 * 1024
HI = lax.Precision.HIGHEST
MESH = pl.DeviceIdType.MESH


def _dot(a, b, ca, cb):
    return lax.dot_general(a.astype(BF16), b.astype(BF16), (((ca,), (cb,)), ((), ())),
                           preferred_element_type=F32)


@jax.custom_vjp
def mm(a, b):
    return _dot(a, b, 1, 0)


def _mm_fwd(a, b):
    return mm(a, b), b


def _mm_bwd(b, g):
    return _dot(g, b, 1, 1), jnp.zeros_like(b)


mm.defvjp(_mm_fwd, _mm_bwd)


def tmm(x, w, taps, xs):
    y = mm(x, w)
    if taps is not None:
        y = y + taps[len(xs)]
    xs.append(x)
    return y


def vjp_taps(core, tap_shapes, args, cot):
    taps = [jnp.zeros(s, F32) for s in tap_shapes]
    _, vjp, xs = jax.vjp(core, taps, *args, has_aux=True)
    out = vjp(cot)
    return out[1:], [_dot(x, g, 0, 0) for x, g in zip(xs, out[0])]


def _split3(x):
    x1 = x.astype(BF16)
    r1 = x - x1.astype(F32)
    x2 = r1.astype(BF16)
    x3 = (r1 - x2.astype(F32)).astype(BF16)
    return x1, x2, x3


def _exact_dot(x, m01, cb=0):
    acc = None
    for piece in _split3(x):
        t = lax.dot_general(piece, m01, (((1,), (cb,)), ((), ())), preferred_element_type=F32)
        acc = t if acc is None else acc + t
    return acc


def _head_matrices():
    e = np.zeros((D_MODEL, N_HEADS), np.float32)
    e[np.arange(D_MODEL), np.arange(D_MODEL) // HEAD_DIM] = 1.0
    return jnp.asarray(e, BF16), jnp.asarray(e.T, BF16)


@jax.custom_vjp
def hsum(x, e, et):
    return _exact_dot(x, e)


@jax.custom_vjp
def hbc(s, e, et):
    return _exact_dot(s, et)


hsum.defvjp(lambda x, e, et: (_exact_dot(x, e), (e, et)),
            lambda res, g: (hbc(g, *res), jnp.zeros_like(res[0]), jnp.zeros_like(res[1])))
hbc.defvjp(lambda s, e, et: (_exact_dot(s, et), (e, et)),
           lambda res, g: (hsum(g, *res), jnp.zeros_like(res[0]), jnp.zeros_like(res[1])))


def _sigmoid(u):
    return 0.5 * (jnp.tanh(0.5 * u) + 1.0)


def _softplus(u):
    return jnp.maximum(u, 0.0) + jnp.log(1.0 + jnp.exp(-jnp.abs(u)))


def _layer_norm(z, g, b):
    mu = jnp.mean(z, axis=-1, keepdims=True)
    zc = z - mu
    var = jnp.mean(zc * zc, axis=-1, keepdims=True)
    return zc * lax.rsqrt(var + LN_EPS) * g + b


def _zero_map(nd):
    return lambda c, i: (0,) * nd


def _params():
    return pltpu.CompilerParams(dimension_semantics=("arbitrary", "arbitrary"), vmem_limit_bytes=VMEM_LIMIT)


def rowwise(name, fn, rows, consts, out_rows, out_accs, tm, nc=1):
    lp = rows[0].shape[-2]
    nt = lp // tm
    assert nt * tm == lp, (name, lp, tm)
    in_specs, args = [], []
    for a in rows:
        if a.ndim == 2:
            in_specs.append(pl.BlockSpec((tm, a.shape[1]), lambda c, i: (i, 0)))
        else:
            in_specs.append(pl.BlockSpec((a.shape[0], tm, a.shape[2]), lambda c, i: (0, i, 0)))
        args.append(a)
    for cst in consts:
        if isinstance(cst, tuple):
            arr, bs, im = cst
            in_specs.append(pl.BlockSpec(bs, im))
        else:
            arr = cst
            in_specs.append(pl.BlockSpec(arr.shape, _zero_map(arr.ndim), pipeline_mode=pl.Buffered(1)))
        args.append(arr)
    out_shape, out_specs, acc_per_chunk = [], [], []
    for spec in out_rows:
        if len(spec) == 3 and spec[2]:
            out_shape.append(jax.ShapeDtypeStruct((nc, lp, spec[0]), spec[1]))
            out_specs.append(pl.BlockSpec((None, tm, spec[0]), lambda c, i: (c, i, 0)))
        else:
            out_shape.append(jax.ShapeDtypeStruct((lp, spec[0]), spec[1]))
            out_specs.append(pl.BlockSpec((tm, spec[0]), lambda c, i: (i, 0)))
    for spec in out_accs:
        out_shape.append(jax.ShapeDtypeStruct(spec[0], spec[1]))
        if len(spec) == 4:
            out_specs.append(pl.BlockSpec(spec[2], spec[3]))
            acc_per_chunk.append(True)
        else:
            out_specs.append(pl.BlockSpec(spec[0], _zero_map(len(spec[0])), pipeline_mode=pl.Buffered(1)))
            acc_per_chunk.append(False)
    n_in, n_or = len(args), len(out_rows)

    def body(*refs):
        c = pl.program_id(0)
        i = pl.program_id(1)
        vals = [r[...] for r in refs[:n_in]]
        outs_r, outs_a = fn(c, i, *vals)
        for ref, val in zip(refs[n_in:n_in + n_or], outs_r):
            ref[...] = val.astype(ref.dtype)
        for ref, val, per_chunk in zip(refs[n_in + n_or:], outs_a, acc_per_chunk):
            first = (i == 0) if per_chunk else jnp.logical_and(i == 0, c == 0)

            @pl.when(first)
            def _():
                ref[...] = val.astype(ref.dtype)

            @pl.when(jnp.logical_not(first))
            def _():
                ref[...] += val.astype(ref.dtype)

    outs = pl.pallas_call(body, name=name, grid=(nc, nt), in_specs=in_specs, out_specs=out_specs,
                          out_shape=out_shape, compiler_params=_params())(*args)
    return outs[:n_or], outs[n_or:]


def _row_ids(i, tm):
    return i * tm + lax.broadcasted_iota(jnp.int32, (tm, 1), 0)


PRE_TAPS = (D_MODEL, D_MODEL, D_MODEL, 64, D_MODEL, 64, D_MODEL, 128, D_MODEL)


def rwkv_pre(e, et, ws, taps, h, hp, mu_r, mu_w, mu_k, mu_v, mu_a, mu_g, w0, a0, k_k, k_a):
    w_r, w_k, w_v, w1, w2, a1, a2, g1, g2 = ws
    xs = []
    xx = hp - h
    r = tmm(h + xx * mu_r, w_r, taps, xs)
    k = tmm(h + xx * mu_k, w_k, taps, xs)
    v = tmm(h + xx * mu_v, w_v, taps, xs)
    wraw = -_softplus(-(w0 + tmm(jnp.tanh(tmm(h + xx * mu_w, w1, taps, xs)), w2, taps, xs))) - 0.5
    lw = -jnp.exp(wraw)
    a = _sigmoid(a0 + tmm(tmm(h + xx * mu_a, a1, taps, xs), a2, taps, xs))
    g = tmm(_sigmoid(tmm(h + xx * mu_g, g1, taps, xs)), g2, taps, xs)
    kk = k * k_k
    ss = hsum(kk * kk, e, et)
    pos = ss > 0.0
    nrm = jnp.where(pos, jnp.sqrt(jnp.where(pos, ss, 1.0)), 0.0)
    kk = kk * hbc(1.0 / jnp.maximum(nrm, 1e-12), e, et)
    k2 = k * (1.0 + (a - 1.0) * k_a)
    return (r, lw, k2, v, -kk, kk * a, g), xs


def rwkv_post(e, et, w_o, taps, y, r, k2, v, g, h0, gn_w, gn_b, rk, lg, lb):
    xs = []
    inv_n = 1.0 / HEAD_DIM
    yc = y - hbc(hsum(y, e, et) * inv_n, e, et)
    yv = hsum(yc * yc, e, et) * inv_n
    yn = yc * hbc(lax.rsqrt(yv + GN_EPS), e, et) * gn_w + gn_b
    bonus = hbc(hsum(r * k2 * rk, e, et), e, et) * v
    mix = tmm((yn + bonus) * g, w_o, taps, xs)
    return _layer_norm(ALPHA * h0 + mix, lg, lb), xs


def mlp_chunk(wup, wdown, taps, h):
    xs = []
    u = jnp.maximum(tmm(h, wup, taps, xs), 0.0)
    return tmm(u * u, wdown, taps, xs), xs


def _rot_half(t):
    n = t.shape[-1]
    lane = lax.broadcasted_iota(jnp.int32, t.shape, t.ndim - 1)
    lo = (lane % HEAD_DIM) < (HEAD_DIM // 2)
    return jnp.where(lo, -pltpu.roll(t, n - HEAD_DIM // 2, t.ndim - 1), pltpu.roll(t, HEAD_DIM // 2, t.ndim - 1))


@jax.custom_vjp
def rot_half(t):
    return _rot_half(t)


rot_half.defvjp(lambda t: (_rot_half(t), None), lambda _, g: (-_rot_half(g),))


def _tile_lanes(t, width):
    return jnp.concatenate([t] * (width // t.shape[-1]), axis=-1)


def qkv_proj(cos, sin, wq, wk, wv, taps, h):
    xs = []
    q = tmm(h, wq, taps, xs)
    k = tmm(h, wk, taps, xs)
    v = tmm(h, wv, taps, xs)
    cq, sq = _tile_lanes(cos, D_MODEL), _tile_lanes(sin, D_MODEL)
    ck, sk = _tile_lanes(cos, KV_DIM), _tile_lanes(sin, KV_DIM)
    return (q * cq + rot_half(q) * sq, k * ck + rot_half(k) * sk, v), xs


def attn_out(w_o, taps, o, h, lg, lb):
    xs = []
    return _layer_norm(ALPHA * h + tmm(o, w_o, taps, xs), lg, lb), xs


def _scan_consts():
    t = SCAN_T
    tri = np.tril(np.ones((t, t), np.float32))
    rows = np.arange(2 * t)
    same = (rows[:, None] // t) == (rows[None, :] // t)
    strict = same & ((rows[None, :] % t) < (rows[:, None] % t))
    incl = same & ((rows[None, :] % t) <= (rows[:, None] % t))
    lane = np.arange(PAIR)
    masks = np.zeros((8, PAIR), np.float32)
    masks[0] = (lane // HEAD_DIM) == 0
    masks[1] = (lane // HEAD_DIM) == 1
    return (jnp.asarray(tri, BF16), jnp.asarray(strict.astype(np.float32)), jnp.asarray(incl.astype(np.float32)),
            jnp.asarray(masks), jnp.asarray(np.eye(2 * t, dtype=np.float32)))


def _dot_x3(a, b, ca, cb):
    a1 = a.astype(BF16)
    a2 = (a - a1.astype(F32)).astype(BF16)
    b1 = b.astype(BF16)
    b2 = (b - b1.astype(F32)).astype(BF16)

    def d(u, v):
        return lax.dot_general(u, v, (((ca,), (cb,)), ((), ())), preferred_element_type=F32)

    return d(a1, b1) + (d(a1, b2) + d(a2, b1))


@functools.partial(jax.custom_vjp, nondiff_argnums=(2, 3))
def _dotf(a, b, ca, cb):
    return _dot_x3(a, b, ca, cb)


def _dotf_bwd(ca, cb, res, g):
    a, b = res
    if ca == 1:
        da = _dot_x3(g, b, 1, 1 - cb)
    else:
        da = _dot_x3(b, g, 1 - cb, 1)
    if cb == 0:
        db = _dot_x3(a, g, 1 - ca, 0)
    else:
        db = _dot_x3(g, a, 0, 1 - ca)
    return da, db


_dotf.defvjp(lambda a, b, ca, cb: (_dot_x3(a, b, ca, cb), (a, b)), _dotf_bwd)


def _tri_dot(tri, x, ct):
    acc = None
    for piece in _split3(x):
        t = lax.dot_general(tri, piece, (((ct,), (0,)), ((), ())), preferred_element_type=F32)
        acc = t if acc is None else acc + t
    return acc


@jax.custom_vjp
def _cumsum_rows(tri, x):
    return _tri_dot(tri, x, 1)


_cumsum_rows.defvjp(lambda tri, x: (_tri_dot(tri, x, 1), tri),
                    lambda tri, g: (jnp.zeros_like(tri), _tri_dot(tri, g, 0)))


@jax.custom_vjp
def _unstack2(x):
    t = x.shape[0] // 2
    return x[:t] + x[t:]


_unstack2.defvjp(lambda x: (_unstack2(x), None), lambda _, g: (jnp.concatenate([g, g], axis=0),))


@jax.custom_vjp
def _last_row(x):
    return x[x.shape[0] - 1:, :]


def _last_row_bwd(_, g):
    rows = lax.broadcasted_iota(jnp.int32, (SCAN_T, g.shape[1]), 0)
    return (jnp.where(rows == SCAN_T - 1, jnp.broadcast_to(g, (SCAN_T, g.shape[1])), 0.0),)


_last_row.defvjp(lambda x: (_last_row(x), None), _last_row_bwd)


def scan_chunk(tri, strict, incl, m0, m1, eye, r, lw, k, v, a, b, s0):
    def stack(x):
        return jnp.concatenate([x * m0, x * m1], axis=0)

    cl = _cumsum_rows(tri, lw)
    gam = jnp.exp(cl)
    ginv = jnp.exp(-cl)
    a_s = stack(a * jnp.exp(cl - lw))
    r_s = stack(r * gam)
    b_s = stack(b * ginv)
    k_s = stack(k * ginv)
    v_s = stack(v)
    n_ab = jnp.where(strict > 0, _dotf(a_s, b_s, 1, 1), 0.0)
    n_ak = jnp.where(strict > 0, _dotf(a_s, k_s, 1, 1), 0.0)
    r_ab = jnp.where(incl > 0, _dotf(r_s, b_s, 1, 1), 0.0)
    r_ak = jnp.where(incl > 0, _dotf(r_s, k_s, 1, 1), 0.0)
    rhs = _dotf(a_s, s0, 1, 1) + _dotf(n_ak, v_s, 1, 0)
    minv = eye + n_ab
    p = n_ab
    for _ in range(5):
        p = _dotf(p, p, 1, 0)
        minv = minv + _dotf(minv, p, 1, 0)
    u_s = _dotf(minv, rhs, 1, 0)
    y = _unstack2(_dotf(r_s, s0, 1, 1) + _dotf(r_ab, u_s, 1, 0) + _dotf(r_ak, v_s, 1, 0))
    g_end = _last_row(gam)
    s1 = s0 * g_end + _dotf(u_s, b_s * g_end, 0, 0) + _dotf(v_s, k_s * g_end, 0, 0)
    return y, s1


SCAN_PAIRS = 2


def _scan_specs(consts, order):
    row = pl.BlockSpec((SCAN_T, PAIR * SCAN_PAIRS), lambda p, c: (order(c), p))
    state = pl.BlockSpec((None, SCAN_PAIRS, PAIR, PAIR), lambda p, c: (order(c), p, 0, 0))
    return row, state, [pl.BlockSpec(x.shape, _zero_map(x.ndim)) for x in consts]


def _pair_lanes(q):
    return slice(q * PAIR, (q + 1) * PAIR)


def scan_fwd(r, lw, k, v, a, b):
    lp = r.shape[0]
    nch = lp // SCAN_T
    npair = D_MODEL // PAIR
    consts = _scan_consts()
    row, state, cspecs = _scan_specs(consts, lambda c: c)

    def body(tri, strict, incl, masks, eye, r_ref, lw_ref, k_ref, v_ref, a_ref, b_ref, y_ref, s_ref, carry):
        @pl.when(pl.program_id(1) == 0)
        def _():
            carry[...] = jnp.zeros_like(carry)

        for q in range(SCAN_PAIRS):
            ln = _pair_lanes(q)
            s0 = carry[q]
            s_ref[q] = s0
            y, s1 = scan_chunk(tri[...], strict[...], incl[...], masks[0:1, :], masks[1:2, :], eye[...], r_ref[:, ln],
                               lw_ref[:, ln], k_ref[:, ln], v_ref[:, ln], a_ref[:, ln], b_ref[:, ln], s0)
            y_ref[:, ln] = y
            carry[q] = s1

    return pl.pallas_call(
        body, name="rwkv_scan_fwd", grid=(npair // SCAN_PAIRS, nch), in_specs=cspecs + [row] * 6,
        out_specs=[row, state],
        out_shape=[jax.ShapeDtypeStruct((lp, D_MODEL), F32), jax.ShapeDtypeStruct((nch, npair, PAIR, PAIR), F32)],
        scratch_shapes=[pltpu.VMEM((SCAN_PAIRS, PAIR, PAIR), F32)], compiler_params=_params(),
    )(*consts, r, lw, k, v, a, b)


def scan_bwd(r, lw, k, v, a, b, s_saved, dy):
    lp = r.shape[0]
    nch = lp // SCAN_T
    npair = D_MODEL // PAIR
    consts = _scan_consts()
    row, state, cspecs = _scan_specs(consts, lambda c: nch - 1 - c)

    def body(tri, strict, incl, masks, eye, r_ref, lw_ref, k_ref, v_ref, a_ref, b_ref, s_ref, dy_ref,
             dr_ref, dlw_ref, dk_ref, dv_ref, da_ref, db_ref, carry):
        @pl.when(pl.program_id(1) == 0)
        def _():
            carry[...] = jnp.zeros_like(carry)

        fn = functools.partial(scan_chunk, tri[...], strict[...], incl[...], masks[0:1, :], masks[1:2, :], eye[...])
        for q in range(SCAN_PAIRS):
            ln = _pair_lanes(q)
            _, vjp = jax.vjp(fn, r_ref[:, ln], lw_ref[:, ln], k_ref[:, ln], v_ref[:, ln], a_ref[:, ln], b_ref[:, ln],
                             s_ref[q])
            dr, dlw, dk, dv, da, db, ds0 = vjp((dy_ref[:, ln], carry[q]))
            dr_ref[:, ln] = dr
            dlw_ref[:, ln] = dlw
            dk_ref[:, ln] = dk
            dv_ref[:, ln] = dv
            da_ref[:, ln] = da
            db_ref[:, ln] = db
            carry[q] = ds0

    return pl.pallas_call(
        body, name="rwkv_scan_bwd", grid=(npair // SCAN_PAIRS, nch), in_specs=cspecs + [row] * 6 + [state, row],
        out_specs=[row] * 6, out_shape=[jax.ShapeDtypeStruct((lp, D_MODEL), F32)] * 6,
        scratch_shapes=[pltpu.VMEM((SCAN_PAIRS, PAIR, PAIR), F32)], compiler_params=_params(),
    )(*consts, r, lw, k, v, a, b, s_saved, dy)


def _spread_matrices():
    rep = np.zeros((N_HEADS_KV, KV_DIM, KVW), np.float32)
    for h in range(N_HEADS_KV):
        for g in range(GROUP):
            rep[h, h * HEAD_DIM + np.arange(HEAD_DIM), g * HEAD_DIM + np.arange(HEAD_DIM)] = 1.0
    return jnp.asarray(rep, BF16)


def _attn_common(n, q, kp, kc, vp, vc, rep, sink):
    lane = lax.broadcasted_iota(jnp.int32, (1, KVW), 1)
    gmask = [(lane // HEAD_DIM == g).astype(F32) for g in range(GROUP)]
    q_s = jnp.concatenate([q * gmask[g] for g in range(GROUP)], axis=0)
    keys = _dot(jnp.concatenate([kp, kc], axis=0), rep, 1, 0)
    vals = _dot(jnp.concatenate([vp, vc], axis=0), rep, 1, 0)
    s = _dot(q_s, keys, 1, 1) * (HEAD_DIM ** -0.5)
    qi = lax.broadcasted_iota(jnp.int32, (GROUP * BLOCK, 2 * BLOCK), 0) % BLOCK
    kj = lax.broadcasted_iota(jnp.int32, (GROUP * BLOCK, 2 * BLOCK), 1)
    rel = BLOCK + qi - kj
    valid = (rel >= 0) & (rel < BLOCK) & ((n - 1) * BLOCK + kj >= PAD_FRONT)
    s = jnp.where(valid, s, -1e30)
    sink_col = jnp.concatenate([jnp.broadcast_to(sink[g:g + 1, 0:1], (BLOCK, 1)) for g in range(GROUP)], axis=0)
    m = jnp.maximum(jnp.max(s, axis=-1, keepdims=True), sink_col)
    ex = jnp.exp(s - m)
    ex_sink = jnp.exp(sink_col - m)
    inv = 1.0 / (jnp.sum(ex, axis=-1, keepdims=True) + ex_sink)
    return gmask, q_s, keys, vals, ex * inv, ex_sink * inv


def _unstack_groups(x_s, gmask):
    out = None
    for g in range(GROUP):
        t = x_s[g * BLOCK:(g + 1) * BLOCK] * gmask[g]
        out = t if out is None else out + t
    return out


def _attn_specs():
    qspec = pl.BlockSpec((BLOCK, KVW), lambda n, h: (n, h))
    cur = pl.BlockSpec((BLOCK, KV_DIM), lambda n, h: (n, 0))
    prev = pl.BlockSpec((BLOCK, KV_DIM), lambda n, h: (jnp.maximum(n - 1, 0), 0))
    rep = pl.BlockSpec((None, KV_DIM, KVW), lambda n, h: (h, 0, 0))
    sink = pl.BlockSpec((None, 8, PAIR), lambda n, h: (h, 0, 0))
    return qspec, cur, prev, rep, sink


def attn_fwd(q, k, v, sinks_b):
    lp = q.shape[0]
    qspec, cur, prev, rep, sink = _attn_specs()

    def body(q_ref, kp_ref, kc_ref, vp_ref, vc_ref, rep_ref, sink_ref, o_ref):
        gmask, _, _, vals, p, _ = _attn_common(pl.program_id(0), q_ref[...], kp_ref[...], kc_ref[...], vp_ref[...],
                                               vc_ref[...], rep_ref[...], sink_ref[...])
        o_ref[...] = _unstack_groups(_dot(p, vals, 1, 0), gmask)

    return pl.pallas_call(
        body, name="swa_fwd", grid=(lp // BLOCK, N_HEADS_KV), in_specs=[qspec, prev, cur, prev, cur, rep, sink],
        out_specs=qspec, out_shape=jax.ShapeDtypeStruct((lp, D_MODEL), F32), compiler_params=_params(),
    )(q, k, k, v, v, _spread_matrices(), sinks_b)


def attn_bwd(q, k, v, sinks_b, do):
    lp = q.shape[0]
    qspec, cur, prev, rep, sink = _attn_specs()

    def body(q_ref, kp_ref, kc_ref, vp_ref, vc_ref, rep_ref, sink_ref, do_ref, dq_ref, dkc_ref, dkp_ref, dvc_ref,
             dvp_ref, dsink_ref):
        n = pl.program_id(0)
        h = pl.program_id(1)
        gmask, q_s, keys, vals, p, p_sink = _attn_common(n, q_ref[...], kp_ref[...], kc_ref[...], vp_ref[...],
                                                         vc_ref[...], rep_ref[...], sink_ref[...])
        do = do_ref[...]
        do_s = jnp.concatenate([do * gmask[g] for g in range(GROUP)], axis=0)
        dp = _dot(do_s, vals, 1, 1)
        delta = jnp.sum(p * dp, axis=-1, keepdims=True)
        ds = p * (dp - delta) * (HEAD_DIM ** -0.5)
        dq_ref[...] = _unstack_groups(_dot(ds, keys, 1, 0), gmask)
        dkeys = _exact_dot(_dot(ds, q_s, 0, 0), rep_ref[...], cb=1)
        dvals = _exact_dot(_dot(p, do_s, 0, 0), rep_ref[...], cb=1)
        dsk = -(p_sink * delta)
        rows = [jnp.broadcast_to(jnp.sum(dsk[g * BLOCK:(g + 1) * BLOCK], axis=0, keepdims=True), (1, PAIR))
                for g in range(GROUP)]
        dsink = jnp.concatenate(rows + [jnp.zeros((8 - GROUP, PAIR), F32)], axis=0)

        @pl.when(h == 0)
        def _():
            dkp_ref[...] = dkeys[:BLOCK]
            dkc_ref[...] = dkeys[BLOCK:]
            dvp_ref[...] = dvals[:BLOCK]
            dvc_ref[...] = dvals[BLOCK:]

        @pl.when(h > 0)
        def _():
            dkp_ref[...] += dkeys[:BLOCK]
            dkc_ref[...] += dkeys[BLOCK:]
            dvp_ref[...] += dvals[:BLOCK]
            dvc_ref[...] += dvals[BLOCK:]

        @pl.when(n == 0)
        def _():
            dsink_ref[h] = dsink

        @pl.when(n > 0)
        def _():
            dsink_ref[h] += dsink

    kv = jax.ShapeDtypeStruct((lp, KV_DIM), F32)
    sink_all = pl.BlockSpec((N_HEADS_KV, 8, PAIR), lambda n, h: (0, 0, 0))
    return pl.pallas_call(
        body, name="swa_bwd", grid=(lp // BLOCK, N_HEADS_KV), in_specs=[qspec, prev, cur, prev, cur, rep, sink, qspec],
        out_specs=[qspec, cur, cur, cur, cur, sink_all],
        out_shape=[jax.ShapeDtypeStruct((lp, D_MODEL), F32), kv, kv, kv, kv,
                   jax.ShapeDtypeStruct((N_HEADS_KV, 8, PAIR), F32)],
        compiler_params=_params(),
    )(q, k, k, v, v, _spread_matrices(), sinks_b, do)


def _pick_tm(lp, want):
    for tm in (384, 192, 128, 64):
        if tm <= want and lp % tm == 0:
            return tm
    raise ValueError(lp)


def _acc(shape):
    return (tuple(shape), F32)


def _ff_all(w, layer):
    return (w, (N_FF_CHUNK, None, D_MODEL, D_MODEL), lambda c, i: (0, layer, 0, 0))


def _ff_one(w, layer):
    return (w, (None, None, D_MODEL, D_MODEL), lambda c, i: (c, layer, 0, 0))


def _mlp_layer_fwd(name, h, wup, wdown, layer, lg, lb, tm):
    def fn(c, i, h, wup, wdown, lg, lb):
        out = None
        for s in range(N_FF_CHUNK):
            t = mlp_chunk(wup[s], wdown[s], None, h)[0]
            out = t if out is None else out + t
        z = ALPHA * h + out
        return (_layer_norm(z, lg, lb), z), ()

    (h_out, z), _ = rowwise(name, fn, [h], [_ff_all(wup, layer), _ff_all(wdown, layer), lg, lb],
                            [(D_MODEL, F32), (D_MODEL, F32)], [], tm)
    return h_out, z


def _mlp_layer_bwd(name, h_in, z, dh_parts, wup, wdown, layer, lg, lb, tm):
    n_parts = len(dh_parts)

    def fn_ln(c, i, z, *rest):
        dh = rest[0]
        for extra in rest[1:n_parts]:
            dh = dh + extra
        _, vjp = jax.vjp(_layer_norm, z, rest[n_parts], rest[n_parts + 1])
        dz, dlg, dlb = vjp(dh)
        return (dz,), (dlg, dlb)

    (dz,), (dlg, dlb) = rowwise(name + "_ln", fn_ln, [z] + list(dh_parts), [lg, lb], [(D_MODEL, F32)],
                                [_acc((1, D_MODEL)), _acc((1, D_MODEL))], tm)

    def fn_mlp(c, i, h, dz, wup, wdown):
        tile = h.shape[0]
        (dx,), dws = vjp_taps(functools.partial(mlp_chunk, wup, wdown), [(tile, D_MODEL)] * 2, [h], dz)
        return (dx,), dws

    aspec = ((N_FF_CHUNK, D_MODEL, D_MODEL), F32, (None, D_MODEL, D_MODEL), lambda c, i: (c, 0, 0))
    (dx,), (dwup, dwdown) = rowwise(name + "_mm", fn_mlp, [h_in, dz], [_ff_one(wup, layer), _ff_one(wdown, layer)],
                                    [(D_MODEL, F32, True)], [aspec, aspec], tm, nc=N_FF_CHUNK)
    return dz, dx, dwup, dwdown, dlg, dlb


def _sum_parts(dz, dx):
    out = ALPHA * dz
    for s in range(N_FF_CHUNK):
        out = out + dx[s]
    return out


def local_step(x, loss_target, p):
    seq = x.shape[0]
    lp = TOK0 + seq
    tm = _pick_tm(lp, 384)
    tms = _pick_tm(lp, 128)
    e, et = _head_matrices()
    h0 = jnp.concatenate([jnp.zeros((PAD_FRONT, D_MODEL), F32), p["meta_tokens"], x], axis=0)
    hp = jnp.concatenate([jnp.zeros((1, D_MODEL), F32), h0[:-1]], axis=0)
    tgt = jnp.concatenate([jnp.zeros((TOK0, D_MODEL), F32), loss_target], axis=0)
    pos = jnp.maximum(jnp.arange(lp, dtype=F32) - PAD_FRONT, 0.0)
    inv_freq = 1.0 / (ROPE_THETA ** (jnp.arange(0, HEAD_DIM, 2, dtype=F32) / HEAD_DIM))
    ang = pos[:, None] * inv_freq[None, :]
    cos = jnp.tile(jnp.cos(ang), (1, PAIR // (HEAD_DIM // 2)))
    sin = jnp.tile(jnp.sin(ang), (1, PAIR // (HEAD_DIM // 2)))

    pre_vec = [p["a_mu"][j:j + 1] for j in range(6)] + [p["a_w0"], p["a_a0"], p["a_k_k"], p["a_k_a"]]
    pre_w = [p["a_w_r"], p["a_w_k"], p["a_w_v"], p["a_w1"], p["a_w2"], p["a_a1"], p["a_a2"], p["a_g1"], p["a_g2"]]
    n_vec = len(pre_vec)

    def fn_pre(c, i, h, hp, e, et, *ws):
        return rwkv_pre(e, et, ws[n_vec:], None, h, hp, *ws[:n_vec])[0], ()

    (r, lw, k2, v, an, bn, g), _ = rowwise("rwkv_pre", fn_pre, [h0, hp], [e, et] + pre_vec + pre_w,
                                           [(D_MODEL, F32)] * 7, [], tms)
    y, s_saved = scan_fwd(r, lw, k2, v, an, bn)

    post_c = [p["a_w_o"], p["a_gn_w"], p["a_gn_b"], p["a_r_k"], p["ln_g00"], p["ln_b00"]]

    def fn_post(c, i, y, r, k2, v, g, h0, e, et, w_o, *vecs):
        return (rwkv_post(e, et, w_o, None, y, r, k2, v, g, h0, *vecs)[0],), ()

    (h1,), _ = rowwise("rwkv_post", fn_post, [y, r, k2, v, g, h0], [e, et] + post_c, [(D_MODEL, F32)], [], tm)
    h2, z2 = _mlp_layer_fwd("mlp0_fwd", h1, p["mlp_up"], p["mlp_down"], 0, p["ln_g01"], p["ln_b01"], tm)

    qkv_w = [p["b_w_q"], p["kv_w_k"], p["kv_w_v"]]

    def fn_qkv(c, i, h, cos, sin, wq, wk, wv):
        return qkv_proj(cos, sin, wq, wk, wv, None, h)[0], ()

    (q, k, vv), _ = rowwise("qkv_proj", fn_qkv, [h2, cos, sin], qkv_w,
                            [(D_MODEL, F32), (KV_DIM, F32), (KV_DIM, F32)], [], tm)
    sinks_b = jnp.broadcast_to(p["b_sinks"].reshape(N_HEADS_KV, GROUP, 1), (N_HEADS_KV, GROUP, PAIR))
    sinks_b = jnp.concatenate([sinks_b, jnp.zeros((N_HEADS_KV, 8 - GROUP, PAIR), F32)], axis=1)
    o = attn_fwd(q, k, vv, sinks_b)

    ao_c = [p["b_w_o"], p["ln_g10"], p["ln_b10"]]

    def fn_ao(c, i, o, h, w_o, lg, lb):
        return (attn_out(w_o, None, o, h, lg, lb)[0],), ()

    (h3,), _ = rowwise("attn_out", fn_ao, [o, h2], ao_c, [(D_MODEL, F32)], [], tm)
    h4, z4 = _mlp_layer_fwd("mlp1_fwd", h3, p["mlp_up"], p["mlp_down"], 1, p["ln_g11"], p["ln_b11"], tm)

    def fn_loss(c, i, h4, tgt):
        real = (_row_ids(i, tm) >= TOK0).astype(F32)
        err = (h4 - tgt) * real
        part = 0.5 * jnp.sum(jnp.sum(err * err, axis=-1, keepdims=True), axis=0, keepdims=True) / D_MODEL
        return (err * (1.0 / D_MODEL),), (jnp.broadcast_to(part, (8, PAIR)),)

    (dh4,), (loss_acc,) = rowwise("loss", fn_loss, [h4, tgt], [], [(D_MODEL, F32)], [_acc((8, PAIR))], tm)
    loss = loss_acc[0, 0]

    grads = {}
    dz4, dx4, grads["mlp_up1"], grads["mlp_down1"], grads["ln_g11"], grads["ln_b11"] = _mlp_layer_bwd(
        "mlp1_bwd", h3, z4, [dh4], p["mlp_up"], p["mlp_down"], 1, p["ln_g11"], p["ln_b11"], tm)

    def fn_ao_b(c, i, dz, dx, o, h, w_o, lg, lb):
        (do, dh, dlg, dlb), (dw_o,) = vjp_taps(functools.partial(attn_out, w_o), [(tms, D_MODEL)], [o, h, lg, lb],
                                               _sum_parts(dz, dx))
        return (do, dh), (dw_o, dlg, dlb)

    (do, dh2_a), (grads["b_w_o"], grads["ln_g10"], grads["ln_b10"]) = rowwise(
        "attn_out_bwd", fn_ao_b, [dz4, dx4, o, h2], ao_c, [(D_MODEL, F32)] * 2,
        [_acc((D_MODEL, D_MODEL)), _acc((1, D_MODEL)), _acc((1, D_MODEL))], tms)

    dq, dkc, dkp, dvc, dvp, dsinks = attn_bwd(q, k, vv, sinks_b, do)
    grads["b_sinks"] = dsinks[:, :GROUP, 0].reshape(1, N_HEADS)
    zblk = jnp.zeros((BLOCK, KV_DIM), F32)
    dkp_s = jnp.concatenate([dkp[BLOCK:], zblk], axis=0)
    dvp_s = jnp.concatenate([dvp[BLOCK:], zblk], axis=0)

    def fn_qkv_b(c, i, h, cos, sin, dq, dkc, dkp, dvc, dvp, wq, wk, wv):
        return vjp_taps(functools.partial(qkv_proj, cos, sin, wq, wk, wv),
                        [(tms, D_MODEL), (tms, KV_DIM), (tms, KV_DIM)], [h], (dq, dkc + dkp, dvc + dvp))

    (dh2_q,), (grads["b_w_q"], grads["kv_w_k"], grads["kv_w_v"]) = rowwise(
        "qkv_proj_bwd", fn_qkv_b, [h2, cos, sin, dq, dkc, dkp_s, dvc, dvp_s], qkv_w, [(D_MODEL, F32)],
        [_acc((D_MODEL, D_MODEL)), _acc((D_MODEL, KV_DIM)), _acc((D_MODEL, KV_DIM))], tms)

    dz2, dx2, grads["mlp_up0"], grads["mlp_down0"], grads["ln_g01"], grads["ln_b01"] = _mlp_layer_bwd(
        "mlp0_bwd", h1, z2, [dh2_a, dh2_q], p["mlp_up"], p["mlp_down"], 0, p["ln_g01"], p["ln_b01"], tm)

    def fn_post_b(c, i, dz, dx, y, r, k2, v, g, h0, e, et, w_o, *vecs):
        out, dws = vjp_taps(functools.partial(rwkv_post, e, et, w_o), [(tms, D_MODEL)],
                            [y, r, k2, v, g, h0] + list(vecs), _sum_parts(dz, dx))
        return out[:6], tuple(dws) + tuple(out[6:])

    (dy, dr_c, dk_c, dv_c, dg, dh0_c), post_g = rowwise(
        "rwkv_post_bwd", fn_post_b, [dz2, dx2, y, r, k2, v, g, h0], [e, et] + post_c, [(D_MODEL, F32)] * 6,
        [_acc((D_MODEL, D_MODEL))] + [_acc((1, D_MODEL))] * 5, tms)
    for name, val in zip(["a_w_o", "a_gn_w", "a_gn_b", "a_r_k", "ln_g00", "ln_b00"], post_g):
        grads[name] = val

    dr_s, dlw, dk_s, dv_s, dan, dbn = scan_bwd(r, lw, k2, v, an, bn, s_saved, dy)

    def fn_pre_b(c, i, h, hp, dr_c, dr_s, dlw, dk_c, dk_s, dv_c, dv_s, dan, dbn, dg, e, et, *ws):
        real = (_row_ids(i, tms) >= PAD_FRONT).astype(F32)
        cot = tuple(t * real for t in (dr_c + dr_s, dlw, dk_c + dk_s, dv_c + dv_s, dan, dbn, dg))
        out, dws = vjp_taps(functools.partial(rwkv_pre, e, et, ws[n_vec:]), [(tms, n) for n in PRE_TAPS],
                            [h, hp] + list(ws[:n_vec]), cot)
        return out[:2], tuple(out[2:]) + tuple(dws)

    (dh0_p, dhp), pre_g = rowwise(
        "rwkv_pre_bwd", fn_pre_b, [h0, hp, dr_c, dr_s, dlw, dk_c, dk_s, dv_c, dv_s, dan, dbn, dg],
        [e, et] + pre_vec + pre_w, [(D_MODEL, F32)] * 2,
        [_acc((1, D_MODEL))] * n_vec + [_acc(w.shape) for w in pre_w], tms)
    grads["a_mu"] = jnp.concatenate(pre_g[:6], axis=0)
    for name, val in zip(["a_w0", "a_a0", "a_k_k", "a_k_a", "a_w_r", "a_w_k", "a_w_v", "a_w1", "a_w2", "a_a1",
                          "a_a2", "a_g1", "a_g2"], pre_g[6:]):
        grads[name] = val

    dhp_s = jnp.concatenate([dhp[1:], jnp.zeros((1, D_MODEL), F32)], axis=0)

    def fn_add(c, i, a, b, d):
        return (a + b + d,), ()

    (dh0,), _ = rowwise("grad_h0", fn_add, [dh0_c, dh0_p, dhp_s], [], [(D_MODEL, F32)], [], tm)
    grads["meta_tokens"] = dh0[PAD_FRONT:TOK0]
    return loss, dh0[TOK0:], grads


ANY = pl.BlockSpec(memory_space=pl.ANY)
XY_FLIPS = ((0, 1), (1, 0), (1, 1))
ALL_FLIPS = tuple((e >> 2 & 1, e >> 1 & 1, e & 1) for e in range(1, N_DEV))


def _flip(v, bit):
    return 1 - v if bit else v


def all_gather_shards(shards):
    n = len(shards)
    npeer = len(XY_FLIPS)

    def body(*refs):
        src, dst = refs[:n], refs[n:2 * n]
        send_sems, recv_sems, local_sems = refs[2 * n:]
        x, y, c = lax.axis_index("x"), lax.axis_index("y"), lax.axis_index("c")

        def copy(k, j, slot):
            fx, fy = XY_FLIPS[j]
            return pltpu.make_async_remote_copy(
                src_ref=src[k], dst_ref=dst[k].at[slot], send_sem=send_sems.at[k * npeer + j],
                recv_sem=recv_sems.at[k * npeer + j], device_id=(_flip(x, fx), _flip(y, fy), c), device_id_type=MESH)

        mine = [pltpu.make_async_copy(src[k], dst[k].at[2 * x + y], local_sems.at[k]) for k in range(n)]
        sends = [copy(k, j, 2 * x + y) for k in range(n) for j in range(npeer)]
        for cp in mine + sends:
            cp.start()
        for k in range(n):
            for j, (fx, fy) in enumerate(XY_FLIPS):
                copy(k, j, 2 * _flip(x, fx) + _flip(y, fy)).wait_recv()
        for cp in sends:
            cp.wait_send()
        for cp in mine:
            cp.wait()

    return pl.pallas_call(
        body, name="gather_weights", in_specs=[ANY] * n, out_specs=[ANY] * n,
        out_shape=[jax.ShapeDtypeStruct((N_SHARD,) + s.shape, s.dtype) for s in shards],
        scratch_shapes=[pltpu.SemaphoreType.DMA((n * npeer,)), pltpu.SemaphoreType.DMA((n * npeer,)),
                        pltpu.SemaphoreType.DMA((n,))],
    )(*shards)


def _sem_scratch(n):
    return [pltpu.SemaphoreType.DMA((n,)), pltpu.SemaphoreType.DMA((n,)), pltpu.SemaphoreType.DMA((n,))]


def pair_exchange(sources):
    n = len(sources)

    def body(*refs):
        src, own, got = refs[:n], refs[n:2 * n], refs[2 * n:3 * n]
        send_sems, recv_sems, local_sems = refs[3 * n:]
        x, y, c = lax.axis_index("x"), lax.axis_index("y"), lax.axis_index("c")

        def rows(k, which):
            half = sources[k].shape[1] // 2
            return src[k].at[:, pl.ds(pl.multiple_of(which * half, 8), half), :]

        def copy(k):
            return pltpu.make_async_remote_copy(
                src_ref=rows(k, 1 - c), dst_ref=got[k], send_sem=send_sems.at[k], recv_sem=recv_sems.at[k],
                device_id=(x, y, 1 - c), device_id_type=MESH)

        mine = [pltpu.make_async_copy(rows(k, c), own[k], local_sems.at[k]) for k in range(n)]
        sends = [copy(k) for k in range(n)]
        for cp in mine + sends:
            cp.start()
        for cp in sends:
            cp.wait_recv()
        for cp in sends:
            cp.wait_send()
        for cp in mine:
            cp.wait()

    halves = [jax.ShapeDtypeStruct((s.shape[0], s.shape[1] // 2, s.shape[2]), s.dtype) for s in sources]
    out = pl.pallas_call(body, name="grads_pair_exchange", in_specs=[ANY] * n, out_specs=[ANY] * (2 * n),
                         out_shape=halves + halves, scratch_shapes=_sem_scratch(n))(*sources)
    return out[:n], out[n:]


def chip_exchange(parts):
    n = len(parts)
    npeer = len(XY_FLIPS)

    def body(*refs):
        src, dst = refs[:n], refs[n:2 * n]
        send_sems, recv_sems, local_sems = refs[2 * n:]
        x, y, c = lax.axis_index("x"), lax.axis_index("y"), lax.axis_index("c")
        me = 2 * x + y

        def copy(k, j, sending):
            fx, fy = XY_FLIPS[j]
            px, py = _flip(x, fx), _flip(y, fy)
            peer = 2 * px + py
            return pltpu.make_async_remote_copy(
                src_ref=src[k].at[peer], dst_ref=dst[k].at[me if sending else peer],
                send_sem=send_sems.at[k * npeer + j], recv_sem=recv_sems.at[k * npeer + j],
                device_id=(px, py, c), device_id_type=MESH)

        mine = [pltpu.make_async_copy(src[k].at[me], dst[k].at[me], local_sems.at[k]) for k in range(n)]
        sends = [copy(k, j, True) for k in range(n) for j in range(npeer)]
        for cp in mine + sends:
            cp.start()
        for k in range(n):
            for j in range(npeer):
                copy(k, j, False).wait_recv()
        for cp in sends:
            cp.wait_send()
        for cp in mine:
            cp.wait()

    return pl.pallas_call(
        body, name="grads_chip_exchange", in_specs=[ANY] * n, out_specs=[ANY] * n,
        out_shape=[jax.ShapeDtypeStruct(p.shape, p.dtype) for p in parts],
        scratch_shapes=[pltpu.SemaphoreType.DMA((n * npeer,)), pltpu.SemaphoreType.DMA((n * npeer,)),
                        pltpu.SemaphoreType.DMA((n,))],
    )(*parts)


def sibling_share(halves, where, dests):
    n = len(halves)

    def body(*refs):
        src, dst = refs[:n], refs[n:n + len(dests)]
        send_sems, recv_sems, local_sems = refs[n + len(dests):]
        x, y, c = lax.axis_index("x"), lax.axis_index("y"), lax.axis_index("c")

        def copy(k, half):
            d, l = where[k]
            return pltpu.make_async_remote_copy(
                src_ref=src[k], dst_ref=dst[d].at[l, half], send_sem=send_sems.at[k], recv_sem=recv_sems.at[k],
                device_id=(x, y, 1 - c), device_id_type=MESH)

        mine = [pltpu.make_async_copy(src[k], dst[where[k][0]].at[where[k][1], c], local_sems.at[k])
                for k in range(n)]
        sends = [copy(k, c) for k in range(n)]
        for cp in mine + sends:
            cp.start()
        for k in range(n):
            copy(k, 1 - c).wait_recv()
        for cp in sends:
            cp.wait_send()
        for cp in mine:
            cp.wait()

    return pl.pallas_call(
        body, name="grads_sibling_share", in_specs=[ANY] * n, out_specs=[ANY] * len(dests),
        out_shape=[jax.ShapeDtypeStruct((nsub, 2, rows // 2, cols), F32) for nsub, rows, cols in dests],
        scratch_shapes=_sem_scratch(n),
    )(*halves)


ADD_TILE_ELEMS = 512 * 1024


def add_slabs(name, slabs, rows, cols):
    tile = max(t for t in range(8, rows + 1, 8) if rows % t == 0 and t * cols <= ADD_TILE_ELEMS)

    def body(*refs):
        acc = refs[0][...]
        for r in refs[1:-1]:
            acc = acc + r[...]
        refs[-1][...] = acc

    def spec(idx):
        return pl.BlockSpec((None, tile, cols), lambda i: (idx, i, 0))

    return pl.pallas_call(
        body, name=name, grid=(rows // tile,), in_specs=[spec(idx) for _, idx in slabs],
        out_specs=pl.BlockSpec((tile, cols), lambda i: (i, 0)), out_shape=jax.ShapeDtypeStruct((rows, cols), F32),
        compiler_params=pltpu.CompilerParams(dimension_semantics=("arbitrary",), vmem_limit_bytes=VMEM_LIMIT),
    )(*[a for a, _ in slabs])


def reduce_grads(sources, where, dests):
    own, got = pair_exchange(sources)
    parts = []
    for k, (o, g) in enumerate(zip(own, got)):
        n4, half, cols = o.shape
        parts.append(add_slabs(f"grads_pair_add{k}", [(o.reshape(1, n4 * half, cols), 0),
                                                      (g.reshape(1, n4 * half, cols), 0)],
                               n4 * half, cols).reshape(n4, half, cols))
    from_chips = chip_exchange(parts)
    halves = [add_slabs(f"grads_chip_add{k}", [(f, s) for s in range(N_SHARD)], f.shape[1], f.shape[2])
              for k, f in enumerate(from_chips)]
    shared = sibling_share(halves, where, dests)
    return [s.reshape(d) for s, d in zip(shared, dests)]


ADAM_ROWS = 256


def adamw_update(name, g, w, m, v):
    nsub, rows, cols = w.shape
    tr = ADAM_ROWS if rows % ADAM_ROWS == 0 else rows

    def body(g_ref, w_ref, m_ref, v_ref, d_ref, nm_ref, nv_ref):
        g = g_ref[...]
        m2 = ADAM_B1 * m_ref[...] + (1.0 - ADAM_B1) * g
        v2 = ADAM_B2 * v_ref[...] + (1.0 - ADAM_B2) * (g * g)
        m_hat = m2 / (1.0 - ADAM_B1 ** ADAM_STEP)
        v_hat = v2 / (1.0 - ADAM_B2 ** ADAM_STEP)
        d_ref[...] = -ADAM_LR * (m_hat / (jnp.sqrt(v_hat) + ADAM_EPS) + ADAM_WD * w_ref[...])
        nm_ref[...] = m2
        nv_ref[...] = v2

    blk = pl.BlockSpec((None, tr, cols), lambda l, i: (l, i, 0))
    out = jax.ShapeDtypeStruct((nsub, rows, cols), F32)
    return pl.pallas_call(
        body, name=name, grid=(nsub, rows // tr), in_specs=[blk] * 4, out_specs=[blk] * 3, out_shape=[out] * 3,
        compiler_params=_params(),
    )(g, w, m, v)


WEIGHT_NAMES = ("meta_tokens", "a_mu", "a_w_r", "a_w_k", "a_w_v", "a_w_o", "a_w0", "a_w1", "a_w2", "a_a0", "a_a1",
                "a_a2", "a_g1", "a_g2", "a_k_k", "a_k_a", "a_r_k", "a_gn_w", "a_gn_b", "kv_w_k", "kv_w_v", "b_w_q",
                "b_sinks", "b_w_o", "mlp_w_up", "mlp_w_down", "ln_g", "ln_b")
BIG_NAMES = ("a_w_r", "a_w_k", "a_w_v", "a_w_o", "b_w_q", "b_w_o")
PACK_MATS = (("kv_w_k", 256), ("kv_w_v", 256), ("a_w1", 64), ("a_a1", 64), ("a_g1", 128), ("a_w2", 64),
             ("a_a2", 64), ("a_g2", 128))
COLUMN_CUT = ("a_w2", "a_a2", "a_g2")
PACK_VECS = (("a_mu", 6), ("a_w0", 1), ("a_a0", 1), ("a_k_k", 1), ("a_k_a", 1), ("a_gn_w", 1), ("a_gn_b", 1),
             ("ln_g", 4), ("ln_b", 4), ("meta_tokens", 16))
PACK_REPL = (("a_r_k", 4), ("b_sinks", 1))
SHARD_W = D_MODEL // N_SHARD
N_MAT_ROWS = sum(r for _, r in PACK_MATS)
N_VEC_ROWS = sum(r for _, r in PACK_VECS)
N_PACK_ROWS = -(-(N_MAT_ROWS + N_VEC_ROWS + sum(r for _, r in PACK_REPL)) // 8) * 8
N_GATHER_VEC_ROWS = -(-N_VEC_ROWS // 8) * 8


def _pack_rows(arr):
    if arr.size == N_HEADS:
        return jnp.pad(arr.reshape(1, N_HEADS), ((0, 0), (0, SHARD_W - N_HEADS)))
    return arr.reshape(-1, SHARD_W)


def pack_small(get):
    parts = [_pack_rows(get(name)) for name, _ in PACK_MATS + PACK_VECS + PACK_REPL]
    used = sum(p.shape[0] for p in parts)
    return jnp.concatenate(parts + [jnp.zeros((N_PACK_ROWS - used, SHARD_W), F32)], axis=0)


def unpack_small(pack, shapes):
    out, off = {}, 0
    for name, rows in PACK_MATS + PACK_VECS + PACK_REPL:
        piece = pack[off:off + rows]
        off += rows
        out[name] = piece[:, :N_HEADS].reshape(shapes[name]) if name == "b_sinks" else piece.reshape(shapes[name])
    return out


def whole_weights(gathered_big, mats, vecs, a_r_k, b_sinks):
    p = {name: g.reshape(D_MODEL, D_MODEL) for name, g in zip(BIG_NAMES, gathered_big)}
    off = 0
    for name, rows in PACK_MATS:
        piece = mats[:, off:off + rows]
        off += rows
        if name in COLUMN_CUT:
            p[name] = piece.transpose(1, 0, 2).reshape(rows, D_MODEL)
        else:
            p[name] = piece.reshape(D_MODEL, rows)
    v = vecs.transpose(1, 0, 2).reshape(-1, D_MODEL)
    off = 0
    for name, rows in PACK_VECS:
        p[name] = v[off:off + rows]
        off += rows
    for i in range(2):
        for j in range(2):
            p[f"ln_g{i}{j}"] = p["ln_g"][2 * i + j:2 * i + j + 1]
            p[f"ln_b{i}{j}"] = p["ln_b"][2 * i + j:2 * i + j + 1]
    p["a_r_k"] = a_r_k.reshape(1, D_MODEL)
    p["b_sinks"] = b_sinks
    return p


def small_grad_pack(g):
    parts = []
    for name, rows in PACK_MATS:
        if name in COLUMN_CUT:
            parts.append(g[name].reshape(rows, N_SHARD, SHARD_W).transpose(1, 0, 2))
        else:
            parts.append(g[name].reshape(N_SHARD, rows, SHARD_W))
    vec_rows = [g["a_mu"]] + [g[n] for n in ("a_w0", "a_a0", "a_k_k", "a_k_a", "a_gn_w", "a_gn_b")]
    vec_rows += [g[f"ln_g{i}{j}"] for i in range(2) for j in range(2)]
    vec_rows += [g[f"ln_b{i}{j}"] for i in range(2) for j in range(2)] + [g["meta_tokens"]]
    parts.append(jnp.concatenate(vec_rows, axis=0).reshape(N_VEC_ROWS, N_SHARD, SHARD_W).transpose(1, 0, 2))
    parts.append(jnp.broadcast_to(g["a_r_k"].reshape(1, -1, SHARD_W), (N_SHARD, D_MODEL // SHARD_W, SHARD_W)))
    sinks = jnp.pad(g["b_sinks"].reshape(1, 1, N_HEADS), ((0, 0), (0, 0), (0, SHARD_W - N_HEADS)))
    parts.append(jnp.broadcast_to(sinks, (N_SHARD, 1, SHARD_W)))
    used = sum(p.shape[1] for p in parts)
    parts.append(jnp.zeros((N_SHARD, N_PACK_ROWS - used, SHARD_W), F32))
    return jnp.concatenate(parts, axis=1)


def train_step(vals):
    w = {n: vals[n] for n in WEIGHT_NAMES}
    w_pack = pack_small(lambda n: w[n])
    shards = [w[n][0].astype(BF16) for n in BIG_NAMES]
    shards += [w["mlp_w_up"].astype(BF16), w["mlp_w_down"].astype(BF16), w_pack[:N_MAT_ROWS].astype(BF16),
               w_pack[N_MAT_ROWS:N_MAT_ROWS + N_GATHER_VEC_ROWS]]
    gathered = all_gather_shards(shards)
    nb = len(BIG_NAMES)
    p = whole_weights(gathered[:nb], gathered[nb + 2], gathered[nb + 3][:, :N_VEC_ROWS], w["a_r_k"], w["b_sinks"])
    p["mlp_up"], p["mlp_down"] = gathered[nb], gathered[nb + 1]

    loss, gx, g = local_step(vals["x"][0], vals["loss_target"][0], p)
    loss = lax.psum(loss, ("x", "y", "c"))

    sources = [g[n].reshape(N_SHARD, SHARD_W, D_MODEL) for n in BIG_NAMES]
    where = [(k, 0) for k in range(nb)]
    dests = [(1, SHARD_W, D_MODEL)] * nb
    for name in ("mlp_up", "mlp_down"):
        sources += [g[name + "0"], g[name + "1"]]
        where += [(len(dests), 0), (len(dests), 1)]
        dests.append((2, D_MODEL, D_MODEL))
    sources.append(small_grad_pack(g))
    where.append((len(dests), 0))
    dests.append((1, N_PACK_ROWS, SHARD_W))
    summed = reduce_grads(sources, where, dests)

    res = {}
    for k, n in enumerate(BIG_NAMES + ("mlp_w_up", "mlp_w_down")):
        res[n] = [summed[k]] + list(adamw_update("adamw_" + n, summed[k], w[n], vals["m_" + n], vals["v_" + n]))
    packs = [summed[-1]] + list(adamw_update("adamw_small", summed[-1], w_pack[None],
                                             pack_small(lambda n: vals["m_" + n])[None],
                                             pack_small(lambda n: vals["v_" + n])[None]))
    shapes = {n: w[n].shape for n in WEIGHT_NAMES}
    small = [unpack_small(pk[0], shapes) for pk in packs]
    outs = [loss, gx[None]]
    for t in range(4):
        outs += [res[n][t] if n in res else small[t][n] for n in WEIGHT_NAMES]
    return tuple(outs)


def kernel(x, meta_tokens, a_mu, a_w_r, a_w_k, a_w_v, a_w_o, a_w0, a_w1, a_w2, a_a0, a_a1, a_a2, a_g1, a_g2, a_k_k,
           a_k_a, a_r_k, a_gn_w, a_gn_b, kv_w_k, kv_w_v, b_w_q, b_sinks, b_w_o, mlp_w_up, mlp_w_down, ln_g, ln_b,
           loss_target, m_meta_tokens, m_a_mu, m_a_w_r, m_a_w_k, m_a_w_v, m_a_w_o, m_a_w0, m_a_w1, m_a_w2, m_a_a0,
           m_a_a1, m_a_a2, m_a_g1, m_a_g2, m_a_k_k, m_a_k_a, m_a_r_k, m_a_gn_w, m_a_gn_b, m_kv_w_k, m_kv_w_v,
           m_b_w_q, m_b_sinks, m_b_w_o, m_mlp_w_up, m_mlp_w_down, m_ln_g, m_ln_b, v_meta_tokens, v_a_mu, v_a_w_r,
           v_a_w_k, v_a_w_v, v_a_w_o, v_a_w0, v_a_w1, v_a_w2, v_a_a0, v_a_a1, v_a_a2, v_a_g1, v_a_g2, v_a_k_k,
           v_a_k_a, v_a_r_k, v_a_gn_w, v_a_gn_b, v_kv_w_k, v_kv_w_v, v_b_w_q, v_b_sinks, v_b_w_o, v_mlp_w_up,
           v_mlp_w_down, v_ln_g, v_ln_b):
    return train_step(dict(locals()))
```

```python
import functools

import numpy as np
import jax
import jax.numpy as jnp
from jax import lax
from jax.experimental import pallas as pl
from jax.experimental.pallas import tpu as pltpu

F32 = jnp.float32
BF16 = jnp.bfloat16

D_MODEL = 1024
N_HEADS = 16
HEAD_DIM = 64
N_HEADS_KV = 4
GROUP = 4
KV_DIM = N_HEADS_KV * HEAD_DIM
N_META = 16
BLOCK = 128
PAD_FRONT = BLOCK - N_META
TOK0 = PAD_FRONT + N_META
N_FF_CHUNK = 4
N_SHARD = 4
N_DEV = 8
GN_EPS = 64e-5
LN_EPS = 1e-5
ROPE_THETA = 10000.0
ALPHA = 4.0 ** 0.25
ADAM_LR, ADAM_B1, ADAM_B2, ADAM_EPS, ADAM_WD, ADAM_STEP = 0.001, 0.9, 0.999, 1e-08, 0.01, 10
SCAN_T = 64
PAIR = 128
KVW = GROUP * HEAD_DIM
VMEM_LIMIT = 56 * 1024 * 1024
HI = lax.Precision.HIGHEST
MESH = pl.DeviceIdType.MESH


def _dot(a, b, ca, cb):
    return lax.dot_general(a.astype(BF16), b.astype(BF16), (((ca,), (cb,)), ((), ())),
                           preferred_element_type=F32)


@jax.custom_vjp
def mm(a, b):
    return _dot(a, b, 1, 0)


def _mm_fwd(a, b):
    return mm(a, b), b


def _mm_bwd(b, g):
    return _dot(g, b, 1, 1), jnp.zeros_like(b)


mm.defvjp(_mm_fwd, _mm_bwd)


def tmm(x, w, taps, xs):
    y = mm(x, w)
    if taps is not None:
        y = y + taps[len(xs)]
    xs.append(x)
    return y


def vjp_taps(core, tap_shapes, args, cot):
    taps = [jnp.zeros(s, F32) for s in tap_shapes]
    _, vjp, xs = jax.vjp(core, taps, *args, has_aux=True)
    out = vjp(cot)
    return out[1:], [_dot(x, g, 0, 0) for x, g in zip(xs, out[0])]


def _split3(x):
    x1 = x.astype(BF16)
    r1 = x - x1.astype(F32)
    x2 = r1.astype(BF16)
    x3 = (r1 - x2.astype(F32)).astype(BF16)
    return x1, x2, x3


def _exact_dot(x, m01, cb=0):
    acc = None
    for piece in _split3(x):
        t = lax.dot_general(piece, m01, (((1,), (cb,)), ((), ())), preferred_element_type=F32)
        acc = t if acc is None else acc + t
    return acc


def _head_matrices():
    e = np.zeros((D_MODEL, N_HEADS), np.float32)
    e[np.arange(D_MODEL), np.arange(D_MODEL) // HEAD_DIM] = 1.0
    return jnp.asarray(e, BF16), jnp.asarray(e.T, BF16)


@jax.custom_vjp
def hsum(x, e, et):
    return _exact_dot(x, e)


@jax.custom_vjp
def hbc(s, e, et):
    return _exact_dot(s, et)


hsum.defvjp(lambda x, e, et: (_exact_dot(x, e), (e, et)),
            lambda res, g: (hbc(g, *res), jnp.zeros_like(res[0]), jnp.zeros_like(res[1])))
hbc.defvjp(lambda s, e, et: (_exact_dot(s, et), (e, et)),
           lambda res, g: (hsum(g, *res), jnp.zeros_like(res[0]), jnp.zeros_like(res[1])))


def _sigmoid(u):
    return 0.5 * (jnp.tanh(0.5 * u) + 1.0)


def _softplus(u):
    return jnp.maximum(u, 0.0) + jnp.log(1.0 + jnp.exp(-jnp.abs(u)))


def _layer_norm(z, g, b):
    mu = jnp.mean(z, axis=-1, keepdims=True)
    zc = z - mu
    var = jnp.mean(zc * zc, axis=-1, keepdims=True)
    return zc * lax.rsqrt(var + LN_EPS) * g + b


def _zero_map(nd):
    return lambda c, i: (0,) * nd


def _params():
    return pltpu.CompilerParams(dimension_semantics=("arbitrary", "arbitrary"), vmem_limit_bytes=VMEM_LIMIT)


def rowwise(name, fn, rows, consts, out_rows, out_accs, tm, nc=1):
    lp = rows[0].shape[-2]
    nt = lp // tm
    assert nt * tm == lp, (name, lp, tm)
    in_specs, args = [], []
    for a in rows:
        if a.ndim == 2:
            in_specs.append(pl.BlockSpec((tm, a.shape[1]), lambda c, i: (i, 0)))
        else:
            in_specs.append(pl.BlockSpec((a.shape[0], tm, a.shape[2]), lambda c, i: (0, i, 0)))
        args.append(a)
    for cst in consts:
        if isinstance(cst, tuple):
            arr, bs, im = cst
            in_specs.append(pl.BlockSpec(bs, im))
        else:
            arr = cst
            in_specs.append(pl.BlockSpec(arr.shape, _zero_map(arr.ndim), pipeline_mode=pl.Buffered(1)))
        args.append(arr)
    out_shape, out_specs, acc_per_chunk = [], [], []
    for spec in out_rows:
        if len(spec) == 3 and spec[2]:
            out_shape.append(jax.ShapeDtypeStruct((nc, lp, spec[0]), spec[1]))
            out_specs.append(pl.BlockSpec((None, tm, spec[0]), lambda c, i: (c, i, 0)))
        else:
            out_shape.append(jax.ShapeDtypeStruct((lp, spec[0]), spec[1]))
            out_specs.append(pl.BlockSpec((tm, spec[0]), lambda c, i: (i, 0)))
    for spec in out_accs:
        out_shape.append(jax.ShapeDtypeStruct(spec[0], spec[1]))
        if len(spec) == 4:
            out_specs.append(pl.BlockSpec(spec[2], spec[3]))
            acc_per_chunk.append(True)
        else:
            out_specs.append(pl.BlockSpec(spec[0], _zero_map(len(spec[0])), pipeline_mode=pl.Buffered(1)))
            acc_per_chunk.append(False)
    n_in, n_or = len(args), len(out_rows)

    def body(*refs):
        c = pl.program_id(0)
        i = pl.program_id(1)
        vals = [r[...] for r in refs[:n_in]]
        outs_r, outs_a = fn(c, i, *vals)
        for ref, val in zip(refs[n_in:n_in + n_or], outs_r):
            ref[...] = val.astype(ref.dtype)
        for ref, val, per_chunk in zip(refs[n_in + n_or:], outs_a, acc_per_chunk):
            first = (i == 0) if per_chunk else jnp.logical_and(i == 0, c == 0)

            @pl.when(first)
            def _():
                ref[...] = val.astype(ref.dtype)

            @pl.when(jnp.logical_not(first))
            def _():
                ref[...] += val.astype(ref.dtype)

    outs = pl.pallas_call(body, name=name, grid=(nc, nt), in_specs=in_specs, out_specs=out_specs,
                          out_shape=out_shape, compiler_params=_params())(*args)
    return outs[:n_or], outs[n_or:]


def _row_ids(i, tm):
    return i * tm + lax.broadcasted_iota(jnp.int32, (tm, 1), 0)


PRE_TAPS = (D_MODEL, D_MODEL, D_MODEL, 64, D_MODEL, 64, D_MODEL, 128, D_MODEL)


def rwkv_pre(e, et, ws, taps, h, hp, mu_r, mu_w, mu_k, mu_v, mu_a, mu_g, w0, a0, k_k, k_a):
    w_r, w_k, w_v, w1, w2, a1, a2, g1, g2 = ws
    xs = []
    xx = hp - h
    r = tmm(h + xx * mu_r, w_r, taps, xs)
    k = tmm(h + xx * mu_k, w_k, taps, xs)
    v = tmm(h + xx * mu_v, w_v, taps, xs)
    wraw = -_softplus(-(w0 + tmm(jnp.tanh(tmm(h + xx * mu_w, w1, taps, xs)), w2, taps, xs))) - 0.5
    lw = -jnp.exp(wraw)
    a = _sigmoid(a0 + tmm(tmm(h + xx * mu_a, a1, taps, xs), a2, taps, xs))
    g = tmm(_sigmoid(tmm(h + xx * mu_g, g1, taps, xs)), g2, taps, xs)
    kk = k * k_k
    ss = hsum(kk * kk, e, et)
    pos = ss > 0.0
    nrm = jnp.where(pos, jnp.sqrt(jnp.where(pos, ss, 1.0)), 0.0)
    kk = kk * hbc(1.0 / jnp.maximum(nrm, 1e-12), e, et)
    k2 = k * (1.0 + (a - 1.0) * k_a)
    return (r, lw, k2, v, -kk, kk * a, g), xs


def rwkv_post(e, et, w_o, taps, y, r, k2, v, g, h0, gn_w, gn_b, rk, lg, lb):
    xs = []
    inv_n = 1.0 / HEAD_DIM
    yc = y - hbc(hsum(y, e, et) * inv_n, e, et)
    yv = hsum(yc * yc, e, et) * inv_n
    yn = yc * hbc(lax.rsqrt(yv + GN_EPS), e, et) * gn_w + gn_b
    bonus = hbc(hsum(r * k2 * rk, e, et), e, et) * v
    mix = tmm((yn + bonus) * g, w_o, taps, xs)
    return _layer_norm(ALPHA * h0 + mix, lg, lb), xs


def mlp_chunk(wup, wdown, taps, h):
    xs = []
    u = jnp.maximum(tmm(h, wup, taps, xs), 0.0)
    return tmm(u * u, wdown, taps, xs), xs


def _rot_half(t):
    n = t.shape[-1]
    lane = lax.broadcasted_iota(jnp.int32, t.shape, t.ndim - 1)
    lo = (lane % HEAD_DIM) < (HEAD_DIM // 2)
    return jnp.where(lo, -pltpu.roll(t, n - HEAD_DIM // 2, t.ndim - 1), pltpu.roll(t, HEAD_DIM // 2, t.ndim - 1))


@jax.custom_vjp
def rot_half(t):
    return _rot_half(t)


rot_half.defvjp(lambda t: (_rot_half(t), None), lambda _, g: (-_rot_half(g),))


def _tile_lanes(t, width):
    return jnp.concatenate([t] * (width // t.shape[-1]), axis=-1)


def qkv_proj(cos, sin, wq, wk, wv, taps, h):
    xs = []
    q = tmm(h, wq, taps, xs)
    k = tmm(h, wk, taps, xs)
    v = tmm(h, wv, taps, xs)
    cq, sq = _tile_lanes(cos, D_MODEL), _tile_lanes(sin, D_MODEL)
    ck, sk = _tile_lanes(cos, KV_DIM), _tile_lanes(sin, KV_DIM)
    return (q * cq + rot_half(q) * sq, k * ck + rot_half(k) * sk, v), xs


def attn_out(w_o, taps, o, h, lg, lb):
    xs = []
    return _layer_norm(ALPHA * h + tmm(o, w_o, taps, xs), lg, lb), xs


def _scan_consts():
    t = SCAN_T
    tri = np.tril(np.ones((t, t), np.float32))
    rows = np.arange(2 * t)
    same = (rows[:, None] // t) == (rows[None, :] // t)
    strict = same & ((rows[None, :] % t) < (rows[:, None] % t))
    incl = same & ((rows[None, :] % t) <= (rows[:, None] % t))
    lane = np.arange(PAIR)
    masks = np.zeros((8, PAIR), np.float32)
    masks[0] = (lane // HEAD_DIM) == 0
    masks[1] = (lane // HEAD_DIM) == 1
    return (jnp.asarray(tri, BF16), jnp.asarray(strict.astype(np.float32)), jnp.asarray(incl.astype(np.float32)),
            jnp.asarray(masks), jnp.asarray(np.eye(2 * t, dtype=np.float32)))


def _dot_x3(a, b, ca, cb):
    a1 = a.astype(BF16)
    a2 = (a - a1.astype(F32)).astype(BF16)
    b1 = b.astype(BF16)
    b2 = (b - b1.astype(F32)).astype(BF16)

    def d(u, v):
        return lax.dot_general(u, v, (((ca,), (cb,)), ((), ())), preferred_element_type=F32)

    return d(a1, b1) + (d(a1, b2) + d(a2, b1))


@functools.partial(jax.custom_vjp, nondiff_argnums=(2, 3))
def _dotf(a, b, ca, cb):
    return _dot_x3(a, b, ca, cb)


def _dotf_bwd(ca, cb, res, g):
    a, b = res
    if ca == 1:
        da = _dot_x3(g, b, 1, 1 - cb)
    else:
        da = _dot_x3(b, g, 1 - cb, 1)
    if cb == 0:
        db = _dot_x3(a, g, 1 - ca, 0)
    else:
        db = _dot_x3(g, a, 0, 1 - ca)
    return da, db


_dotf.defvjp(lambda a, b, ca, cb: (_dot_x3(a, b, ca, cb), (a, b)), _dotf_bwd)


def _tri_dot(tri, x, ct):
    acc = None
    for piece in _split3(x):
        t = lax.dot_general(tri, piece, (((ct,), (0,)), ((), ())), preferred_element_type=F32)
        acc = t if acc is None else acc + t
    return acc


@jax.custom_vjp
def _cumsum_rows(tri, x):
    return _tri_dot(tri, x, 1)


_cumsum_rows.defvjp(lambda tri, x: (_tri_dot(tri, x, 1), tri),
                    lambda tri, g: (jnp.zeros_like(tri), _tri_dot(tri, g, 0)))


@jax.custom_vjp
def _unstack2(x):
    t = x.shape[0] // 2
    return x[:t] + x[t:]


_unstack2.defvjp(lambda x: (_unstack2(x), None), lambda _, g: (jnp.concatenate([g, g], axis=0),))


@jax.custom_vjp
def _last_row(x):
    return x[x.shape[0] - 1:, :]


def _last_row_bwd(_, g):
    rows = lax.broadcasted_iota(jnp.int32, (SCAN_T, g.shape[1]), 0)
    return (jnp.where(rows == SCAN_T - 1, jnp.broadcast_to(g, (SCAN_T, g.shape[1])), 0.0),)


_last_row.defvjp(lambda x: (_last_row(x), None), _last_row_bwd)


def scan_chunk(tri, strict, incl, m0, m1, eye, r, lw, k, v, a, b, s0):
    def stack(x):
        return jnp.concatenate([x * m0, x * m1], axis=0)

    cl = _cumsum_rows(tri, lw)
    gam = jnp.exp(cl)
    ginv = jnp.exp(-cl)
    a_s = stack(a * jnp.exp(cl - lw))
    r_s = stack(r * gam)
    b_s = stack(b * ginv)
    k_s = stack(k * ginv)
    v_s = stack(v)
    n_ab = jnp.where(strict > 0, _dotf(a_s, b_s, 1, 1), 0.0)
    n_ak = jnp.where(strict > 0, _dotf(a_s, k_s, 1, 1), 0.0)
    r_ab = jnp.where(incl > 0, _dotf(r_s, b_s, 1, 1), 0.0)
    r_ak = jnp.where(incl > 0, _dotf(r_s, k_s, 1, 1), 0.0)
    rhs = _dotf(a_s, s0, 1, 1) + _dotf(n_ak, v_s, 1, 0)
    minv = eye + n_ab
    p = n_ab
    for _ in range(5):
        p = _dotf(p, p, 1, 0)
        minv = minv + _dotf(minv, p, 1, 0)
    u_s = _dotf(minv, rhs, 1, 0)
    y = _unstack2(_dotf(r_s, s0, 1, 1) + _dotf(r_ab, u_s, 1, 0) + _dotf(r_ak, v_s, 1, 0))
    g_end = _last_row(gam)
    s1 = s0 * g_end + _dotf(u_s, b_s * g_end, 0, 0) + _dotf(v_s, k_s * g_end, 0, 0)
    return y, s1


SCAN_PAIRS = 2


def _scan_specs(consts, order):
    row = pl.BlockSpec((SCAN_T, PAIR * SCAN_PAIRS), lambda p, c: (order(c), p))
    state = pl.BlockSpec((None, SCAN_PAIRS, PAIR, PAIR), lambda p, c: (order(c), p, 0, 0))
    return row, state, [pl.BlockSpec(x.shape, _zero_map(x.ndim)) for x in consts]


def _pair_lanes(q):
    return slice(q * PAIR, (q + 1) * PAIR)


def scan_fwd(r, lw, k, v, a, b):
    lp = r.shape[0]
    nch = lp // SCAN_T
    npair = D_MODEL // PAIR
    consts = _scan_consts()
    row, state, cspecs = _scan_specs(consts, lambda c: c)

    def body(tri, strict, incl, masks, eye, r_ref, lw_ref, k_ref, v_ref, a_ref, b_ref, y_ref, s_ref, carry):
        @pl.when(pl.program_id(1) == 0)
        def _():
            carry[...] = jnp.zeros_like(carry)

        for q in range(SCAN_PAIRS):
            ln = _pair_lanes(q)
            s0 = carry[q]
            s_ref[q] = s0
            y, s1 = scan_chunk(tri[...], strict[...], incl[...], masks[0:1, :], masks[1:2, :], eye[...], r_ref[:, ln],
                               lw_ref[:, ln], k_ref[:, ln], v_ref[:, ln], a_ref[:, ln], b_ref[:, ln], s0)
            y_ref[:, ln] = y
            carry[q] = s1

    return pl.pallas_call(
        body, name="rwkv_scan_fwd", grid=(npair // SCAN_PAIRS, nch), in_specs=cspecs + [row] * 6,
        out_specs=[row, state],
        out_shape=[jax.ShapeDtypeStruct((lp, D_MODEL), F32), jax.ShapeDtypeStruct((nch, npair, PAIR, PAIR), F32)],
        scratch_shapes=[pltpu.VMEM((SCAN_PAIRS, PAIR, PAIR), F32)], compiler_params=_params(),
    )(*consts, r, lw, k, v, a, b)


def scan_bwd(r, lw, k, v, a, b, s_saved, dy):
    lp = r.shape[0]
    nch = lp // SCAN_T
    npair = D_MODEL // PAIR
    consts = _scan_consts()
    row, state, cspecs = _scan_specs(consts, lambda c: nch - 1 - c)

    def body(tri, strict, incl, masks, eye, r_ref, lw_ref, k_ref, v_ref, a_ref, b_ref, s_ref, dy_ref,
             dr_ref, dlw_ref, dk_ref, dv_ref, da_ref, db_ref, carry):
        @pl.when(pl.program_id(1) == 0)
        def _():
            carry[...] = jnp.zeros_like(carry)

        fn = functools.partial(scan_chunk, tri[...], strict[...], incl[...], masks[0:1, :], masks[1:2, :], eye[...])
        for q in range(SCAN_PAIRS):
            ln = _pair_lanes(q)
            _, vjp = jax.vjp(fn, r_ref[:, ln], lw_ref[:, ln], k_ref[:, ln], v_ref[:, ln], a_ref[:, ln], b_ref[:, ln],
                             s_ref[q])
            dr, dlw, dk, dv, da, db, ds0 = vjp((dy_ref[:, ln], carry[q]))
            dr_ref[:, ln] = dr
            dlw_ref[:, ln] = dlw
            dk_ref[:, ln] = dk
            dv_ref[:, ln] = dv
            da_ref[:, ln] = da
            db_ref[:, ln] = db
            carry[q] = ds0

    return pl.pallas_call(
        body, name="rwkv_scan_bwd", grid=(npair // SCAN_PAIRS, nch), in_specs=cspecs + [row] * 6 + [state, row],
        out_specs=[row] * 6, out_shape=[jax.ShapeDtypeStruct((lp, D_MODEL), F32)] * 6,
        scratch_shapes=[pltpu.VMEM((SCAN_PAIRS, PAIR, PAIR), F32)], compiler_params=_params(),
    )(*consts, r, lw, k, v, a, b, s_saved, dy)


def _spread_matrices():
    rep = np.zeros((N_HEADS_KV, KV_DIM, KVW), np.float32)
    for h in range(N_HEADS_KV):
        for g in range(GROUP):
            rep[h, h * HEAD_DIM + np.arange(HEAD_DIM), g * HEAD_DIM + np.arange(HEAD_DIM)] = 1.0
    return jnp.asarray(rep, BF16)


def _attn_common(n, q, kp, kc, vp, vc, rep, sink):
    lane = lax.broadcasted_iota(jnp.int32, (1, KVW), 1)
    gmask = [(lane // HEAD_DIM == g).astype(F32) for g in range(GROUP)]
    q_s = jnp.concatenate([q * gmask[g] for g in range(GROUP)], axis=0)
    keys = _dot(jnp.concatenate([kp, kc], axis=0), rep, 1, 0)
    vals = _dot(jnp.concatenate([vp, vc], axis=0), rep, 1, 0)
    s = _dot(q_s, keys, 1, 1) * (HEAD_DIM ** -0.5)
    qi = lax.broadcasted_iota(jnp.int32, (GROUP * BLOCK, 2 * BLOCK), 0) % BLOCK
    kj = lax.broadcasted_iota(jnp.int32, (GROUP * BLOCK, 2 * BLOCK), 1)
    rel = BLOCK + qi - kj
    valid = (rel >= 0) & (rel < BLOCK) & ((n - 1) * BLOCK + kj >= PAD_FRONT)
    s = jnp.where(valid, s, -1e30)
    sink_col = jnp.concatenate([jnp.broadcast_to(sink[g:g + 1, 0:1], (BLOCK, 1)) for g in range(GROUP)], axis=0)
    m = jnp.maximum(jnp.max(s, axis=-1, keepdims=True), sink_col)
    ex = jnp.exp(s - m)
    ex_sink = jnp.exp(sink_col - m)
    inv = 1.0 / (jnp.sum(ex, axis=-1, keepdims=True) + ex_sink)
    return gmask, q_s, keys, vals, ex * inv, ex_sink * inv


def _unstack_groups(x_s, gmask):
    out = None
    for g in range(GROUP):
        t = x_s[g * BLOCK:(g + 1) * BLOCK] * gmask[g]
        out = t if out is None else out + t
    return out


def _attn_specs():
    qspec = pl.BlockSpec((BLOCK, KVW), lambda n, h: (n, h))
    cur = pl.BlockSpec((BLOCK, KV_DIM), lambda n, h: (n, 0))
    prev = pl.BlockSpec((BLOCK, KV_DIM), lambda n, h: (jnp.maximum(n - 1, 0), 0))
    rep = pl.BlockSpec((None, KV_DIM, KVW), lambda n, h: (h, 0, 0))
    sink = pl.BlockSpec((None, 8, PAIR), lambda n, h: (h, 0, 0))
    return qspec, cur, prev, rep, sink


def attn_fwd(q, k, v, sinks_b):
    lp = q.shape[0]
    qspec, cur, prev, rep, sink = _attn_specs()

    def body(q_ref, kp_ref, kc_ref, vp_ref, vc_ref, rep_ref, sink_ref, o_ref):
        gmask, _, _, vals, p, _ = _attn_common(pl.program_id(0), q_ref[...], kp_ref[...], kc_ref[...], vp_ref[...],
                                               vc_ref[...], rep_ref[...], sink_ref[...])
        o_ref[...] = _unstack_groups(_dot(p, vals, 1, 0), gmask)

    return pl.pallas_call(
        body, name="swa_fwd", grid=(lp // BLOCK, N_HEADS_KV), in_specs=[qspec, prev, cur, prev, cur, rep, sink],
        out_specs=qspec, out_shape=jax.ShapeDtypeStruct((lp, D_MODEL), F32), compiler_params=_params(),
    )(q, k, k, v, v, _spread_matrices(), sinks_b)


def attn_bwd(q, k, v, sinks_b, do):
    lp = q.shape[0]
    qspec, cur, prev, rep, sink = _attn_specs()

    def body(q_ref, kp_ref, kc_ref, vp_ref, vc_ref, rep_ref, sink_ref, do_ref, dq_ref, dkc_ref, dkp_ref, dvc_ref,
             dvp_ref, dsink_ref):
        n = pl.program_id(0)
        h = pl.program_id(1)
        gmask, q_s, keys, vals, p, p_sink = _attn_common(n, q_ref[...], kp_ref[...], kc_ref[...], vp_ref[...],
                                                         vc_ref[...], rep_ref[...], sink_ref[...])
        do = do_ref[...]
        do_s = jnp.concatenate([do * gmask[g] for g in range(GROUP)], axis=0)
        dp = _dot(do_s, vals, 1, 1)
        delta = jnp.sum(p * dp, axis=-1, keepdims=True)
        ds = p * (dp - delta) * (HEAD_DIM ** -0.5)
        dq_ref[...] = _unstack_groups(_dot(ds, keys, 1, 0), gmask)
        dkeys = _exact_dot(_dot(ds, q_s, 0, 0), rep_ref[...], cb=1)
        dvals = _exact_dot(_dot(p, do_s, 0, 0), rep_ref[...], cb=1)
        dsk = -(p_sink * delta)
        rows = [jnp.broadcast_to(jnp.sum(dsk[g * BLOCK:(g + 1) * BLOCK], axis=0, keepdims=True), (1, PAIR))
                for g in range(GROUP)]
        dsink = jnp.concatenate(rows + [jnp.zeros((8 - GROUP, PAIR), F32)], axis=0)

        @pl.when(h == 0)
        def _():
            dkp_ref[...] = dkeys[:BLOCK]
            dkc_ref[...] = dkeys[BLOCK:]
            dvp_ref[...] = dvals[:BLOCK]
            dvc_ref[...] = dvals[BLOCK:]

        @pl.when(h > 0)
        def _():
            dkp_ref[...] += dkeys[:BLOCK]
            dkc_ref[...] += dkeys[BLOCK:]
            dvp_ref[...] += dvals[:BLOCK]
            dvc_ref[...] += dvals[BLOCK:]

        @pl.when(n == 0)
        def _():
            dsink_ref[h] = dsink

        @pl.when(n > 0)
        def _():
            dsink_ref[h] += dsink

    kv = jax.ShapeDtypeStruct((lp, KV_DIM), F32)
    sink_all = pl.BlockSpec((N_HEADS_KV, 8, PAIR), lambda n, h: (0, 0, 0))
    return pl.pallas_call(
        body, name="swa_bwd", grid=(lp // BLOCK, N_HEADS_KV), in_specs=[qspec, prev, cur, prev, cur, rep, sink, qspec],
        out_specs=[qspec, cur, cur, cur, cur, sink_all],
        out_shape=[jax.ShapeDtypeStruct((lp, D_MODEL), F32), kv, kv, kv, kv,
                   jax.ShapeDtypeStruct((N_HEADS_KV, 8, PAIR), F32)],
        compiler_params=_params(),
    )(q, k, k, v, v, _spread_matrices(), sinks_b, do)


def _pick_tm(lp, want):
    for tm in (384, 192, 128, 64):
        if tm <= want and lp % tm == 0:
            return tm
    raise ValueError(lp)


def _acc(shape):
    return (tuple(shape), F32)


def _ff_all(w, layer):
    return (w, (N_FF_CHUNK, None, D_MODEL, D_MODEL), lambda c, i: (0, layer, 0, 0))


def _ff_one(w, layer):
    return (w, (None, None, D_MODEL, D_MODEL), lambda c, i: (c, layer, 0, 0))


def _mlp_layer_fwd(name, h, wup, wdown, layer, lg, lb, tm):
    def fn(c, i, h, wup, wdown, lg, lb):
        out = None
        for s in range(N_FF_CHUNK):
            t = mlp_chunk(wup[s], wdown[s], None, h)[0]
            out = t if out is None else out + t
        z = ALPHA * h + out
        return (_layer_norm(z, lg, lb), z), ()

    (h_out, z), _ = rowwise(name, fn, [h], [_ff_all(wup, layer), _ff_all(wdown, layer), lg, lb],
                            [(D_MODEL, F32), (D_MODEL, F32)], [], tm)
    return h_out, z


def _mlp_layer_bwd(name, h_in, z, dh_parts, wup, wdown, layer, lg, lb, tm):
    n_parts = len(dh_parts)

    def fn_ln(c, i, z, *rest):
        dh = rest[0]
        for extra in rest[1:n_parts]:
            dh = dh + extra
        _, vjp = jax.vjp(_layer_norm, z, rest[n_parts], rest[n_parts + 1])
        dz, dlg, dlb = vjp(dh)
        return (dz,), (dlg, dlb)

    (dz,), (dlg, dlb) = rowwise(name + "_ln", fn_ln, [z] + list(dh_parts), [lg, lb], [(D_MODEL, F32)],
                                [_acc((1, D_MODEL)), _acc((1, D_MODEL))], tm)

    def fn_mlp(c, i, h, dz, wup, wdown):
        tile = h.shape[0]
        (dx,), dws = vjp_taps(functools.partial(mlp_chunk, wup, wdown), [(tile, D_MODEL)] * 2, [h], dz)
        return (dx,), dws

    aspec = ((N_FF_CHUNK, D_MODEL, D_MODEL), F32, (None, D_MODEL, D_MODEL), lambda c, i: (c, 0, 0))
    (dx,), (dwup, dwdown) = rowwise(name + "_mm", fn_mlp, [h_in, dz], [_ff_one(wup, layer), _ff_one(wdown, layer)],
                                    [(D_MODEL, F32, True)], [aspec, aspec], tm, nc=N_FF_CHUNK)
    return dz, dx, dwup, dwdown, dlg, dlb


def _sum_parts(dz, dx):
    out = ALPHA * dz
    for s in range(N_FF_CHUNK):
        out = out + dx[s]
    return out


def local_step(x, loss_target, p):
    seq = x.shape[0]
    lp = TOK0 + seq
    tm = _pick_tm(lp, 384)
    tms = _pick_tm(lp, 128)
    e, et = _head_matrices()
    h0 = jnp.concatenate([jnp.zeros((PAD_FRONT, D_MODEL), F32), p["meta_tokens"], x], axis=0)
    hp = jnp.concatenate([jnp.zeros((1, D_MODEL), F32), h0[:-1]], axis=0)
    tgt = jnp.concatenate([jnp.zeros((TOK0, D_MODEL), F32), loss_target], axis=0)
    pos = jnp.maximum(jnp.arange(lp, dtype=F32) - PAD_FRONT, 0.0)
    inv_freq = 1.0 / (ROPE_THETA ** (jnp.arange(0, HEAD_DIM, 2, dtype=F32) / HEAD_DIM))
    ang = pos[:, None] * inv_freq[None, :]
    cos = jnp.tile(jnp.cos(ang), (1, PAIR // (HEAD_DIM // 2)))
    sin = jnp.tile(jnp.sin(ang), (1, PAIR // (HEAD_DIM // 2)))

    pre_vec = [p["a_mu"][j:j + 1] for j in range(6)] + [p["a_w0"], p["a_a0"], p["a_k_k"], p["a_k_a"]]
    pre_w = [p["a_w_r"], p["a_w_k"], p["a_w_v"], p["a_w1"], p["a_w2"], p["a_a1"], p["a_a2"], p["a_g1"], p["a_g2"]]
    n_vec = len(pre_vec)

    def fn_pre(c, i, h, hp, e, et, *ws):
        return rwkv_pre(e, et, ws[n_vec:], None, h, hp, *ws[:n_vec])[0], ()

    (r, lw, k2, v, an, bn, g), _ = rowwise("rwkv_pre", fn_pre, [h0, hp], [e, et] + pre_vec + pre_w,
                                           [(D_MODEL, F32)] * 7, [], tms)
    y, s_saved = scan_fwd(r, lw, k2, v, an, bn)

    post_c = [p["a_w_o"], p["a_gn_w"], p["a_gn_b"], p["a_r_k"], p["ln_g00"], p["ln_b00"]]

    def fn_post(c, i, y, r, k2, v, g, h0, e, et, w_o, *vecs):
        return (rwkv_post(e, et, w_o, None, y, r, k2, v, g, h0, *vecs)[0],), ()

    (h1,), _ = rowwise("rwkv_post", fn_post, [y, r, k2, v, g, h0], [e, et] + post_c, [(D_MODEL, F32)], [], tm)
    h2, z2 = _mlp_layer_fwd("mlp0_fwd", h1, p["mlp_up"], p["mlp_down"], 0, p["ln_g01"], p["ln_b01"], tm)

    qkv_w = [p["b_w_q"], p["kv_w_k"], p["kv_w_v"]]

    def fn_qkv(c, i, h, cos, sin, wq, wk, wv):
        return qkv_proj(cos, sin, wq, wk, wv, None, h)[0], ()

    (q, k, vv), _ = rowwise("qkv_proj", fn_qkv, [h2, cos, sin], qkv_w,
                            [(D_MODEL, F32), (KV_DIM, F32), (KV_DIM, F32)], [], tm)
    sinks_b = jnp.broadcast_to(p["b_sinks"].reshape(N_HEADS_KV, GROUP, 1), (N_HEADS_KV, GROUP, PAIR))
    sinks_b = jnp.concatenate([sinks_b, jnp.zeros((N_HEADS_KV, 8 - GROUP, PAIR), F32)], axis=1)
    o = attn_fwd(q, k, vv, sinks_b)

    ao_c = [p["b_w_o"], p["ln_g10"], p["ln_b10"]]

    def fn_ao(c, i, o, h, w_o, lg, lb):
        return (attn_out(w_o, None, o, h, lg, lb)[0],), ()

    (h3,), _ = rowwise("attn_out", fn_ao, [o, h2], ao_c, [(D_MODEL, F32)], [], tm)
    h4, z4 = _mlp_layer_fwd("mlp1_fwd", h3, p["mlp_up"], p["mlp_down"], 1, p["ln_g11"], p["ln_b11"], tm)

    def fn_loss(c, i, h4, tgt):
        real = (_row_ids(i, tm) >= TOK0).astype(F32)
        err = (h4 - tgt) * real
        part = 0.5 * jnp.sum(jnp.sum(err * err, axis=-1, keepdims=True), axis=0, keepdims=True) / D_MODEL
        return (err * (1.0 / D_MODEL),), (jnp.broadcast_to(part, (8, PAIR)),)

    (dh4,), (loss_acc,) = rowwise("loss", fn_loss, [h4, tgt], [], [(D_MODEL, F32)], [_acc((8, PAIR))], tm)
    loss = loss_acc[0, 0]

    grads = {}
    dz4, dx4, grads["mlp_up1"], grads["mlp_down1"], grads["ln_g11"], grads["ln_b11"] = _mlp_layer_bwd(
        "mlp1_bwd", h3, z4, [dh4], p["mlp_up"], p["mlp_down"], 1, p["ln_g11"], p["ln_b11"], tm)

    def fn_ao_b(c, i, dz, dx, o, h, w_o, lg, lb):
        (do, dh, dlg, dlb), (dw_o,) = vjp_taps(functools.partial(attn_out, w_o), [(tms, D_MODEL)], [o, h, lg, lb],
                                               _sum_parts(dz, dx))
        return (do, dh), (dw_o, dlg, dlb)

    (do, dh2_a), (grads["b_w_o"], grads["ln_g10"], grads["ln_b10"]) = rowwise(
        "attn_out_bwd", fn_ao_b, [dz4, dx4, o, h2], ao_c, [(D_MODEL, F32)] * 2,
        [_acc((D_MODEL, D_MODEL)), _acc((1, D_MODEL)), _acc((1, D_MODEL))], tms)

    dq, dkc, dkp, dvc, dvp, dsinks = attn_bwd(q, k, vv, sinks_b, do)
    grads["b_sinks"] = dsinks[:, :GROUP, 0].reshape(1, N_HEADS)
    zblk = jnp.zeros((BLOCK, KV_DIM), F32)
    dkp_s = jnp.concatenate([dkp[BLOCK:], zblk], axis=0)
    dvp_s = jnp.concatenate([dvp[BLOCK:], zblk], axis=0)

    def fn_qkv_b(c, i, h, cos, sin, dq, dkc, dkp, dvc, dvp, wq, wk, wv):
        return vjp_taps(functools.partial(qkv_proj, cos, sin, wq, wk, wv),
                        [(tms, D_MODEL), (tms, KV_DIM), (tms, KV_DIM)], [h], (dq, dkc + dkp, dvc + dvp))

    (dh2_q,), (grads["b_w_q"], grads["kv_w_k"], grads["kv_w_v"]) = rowwise(
        "qkv_proj_bwd", fn_qkv_b, [h2, cos, sin, dq, dkc, dkp_s, dvc, dvp_s], qkv_w, [(D_MODEL, F32)],
        [_acc((D_MODEL, D_MODEL)), _acc((D_MODEL, KV_DIM)), _acc((D_MODEL, KV_DIM))], tms)

    dz2, dx2, grads["mlp_up0"], grads["mlp_down0"], grads["ln_g01"], grads["ln_b01"] = _mlp_layer_bwd(
        "mlp0_bwd", h1, z2, [dh2_a, dh2_q], p["mlp_up"], p["mlp_down"], 0, p["ln_g01"], p["ln_b01"], tm)

    def fn_post_b(c, i, dz, dx, y, r, k2, v, g, h0, e, et, w_o, *vecs):
        out, dws = vjp_taps(functools.partial(rwkv_post, e, et, w_o), [(tms, D_MODEL)],
                            [y, r, k2, v, g, h0] + list(vecs), _sum_parts(dz, dx))
        return out[:6], tuple(dws) + tuple(out[6:])

    (dy, dr_c, dk_c, dv_c, dg, dh0_c), post_g = rowwise(
        "rwkv_post_bwd", fn_post_b, [dz2, dx2, y, r, k2, v, g, h0], [e, et] + post_c, [(D_MODEL, F32)] * 6,
        [_acc((D_MODEL, D_MODEL))] + [_acc((1, D_MODEL))] * 5, tms)
    for name, val in zip(["a_w_o", "a_gn_w", "a_gn_b", "a_r_k", "ln_g00", "ln_b00"], post_g):
        grads[name] = val

    dr_s, dlw, dk_s, dv_s, dan, dbn = scan_bwd(r, lw, k2, v, an, bn, s_saved, dy)

    def fn_pre_b(c, i, h, hp, dr_c, dr_s, dlw, dk_c, dk_s, dv_c, dv_s, dan, dbn, dg, e, et, *ws):
        real = (_row_ids(i, tms) >= PAD_FRONT).astype(F32)
        cot = tuple(t * real for t in (dr_c + dr_s, dlw, dk_c + dk_s, dv_c + dv_s, dan, dbn, dg))
        out, dws = vjp_taps(functools.partial(rwkv_pre, e, et, ws[n_vec:]), [(tms, n) for n in PRE_TAPS],
                            [h, hp] + list(ws[:n_vec]), cot)
        return out[:2], tuple(out[2:]) + tuple(dws)

    (dh0_p, dhp), pre_g = rowwise(
        "rwkv_pre_bwd", fn_pre_b, [h0, hp, dr_c, dr_s, dlw, dk_c, dk_s, dv_c, dv_s, dan, dbn, dg],
        [e, et] + pre_vec + pre_w, [(D_MODEL, F32)] * 2,
        [_acc((1, D_MODEL))] * n_vec + [_acc(w.shape) for w in pre_w], tms)
    grads["a_mu"] = jnp.concatenate(pre_g[:6], axis=0)
    for name, val in zip(["a_w0", "a_a0", "a_k_k", "a_k_a", "a_w_r", "a_w_k", "a_w_v", "a_w1", "a_w2", "a_a1",
                          "a_a2", "a_g1", "a_g2"], pre_g[6:]):
        grads[name] = val

    dhp_s = jnp.concatenate([dhp[1:], jnp.zeros((1, D_MODEL), F32)], axis=0)

    def fn_add(c, i, a, b, d):
        return (a + b + d,), ()

    (dh0,), _ = rowwise("grad_h0", fn_add, [dh0_c, dh0_p, dhp_s], [], [(D_MODEL, F32)], [], tm)
    grads["meta_tokens"] = dh0[PAD_FRONT:TOK0]
    return loss, dh0[TOK0:], grads


ANY = pl.BlockSpec(memory_space=pl.ANY)
XY_FLIPS = ((0, 1), (1, 0), (1, 1))
ALL_FLIPS = tuple((e >> 2 & 1, e >> 1 & 1, e & 1) for e in range(1, N_DEV))


def _flip(v, bit):
    return 1 - v if bit else v


def all_gather_shards(shards):
    n = len(shards)
    npeer = len(XY_FLIPS)

    def body(*refs):
        src, dst = refs[:n], refs[n:2 * n]
        send_sems, recv_sems, local_sems = refs[2 * n:]
        x, y, c = lax.axis_index("x"), lax.axis_index("y"), lax.axis_index("c")

        def copy(k, j, slot):
            fx, fy = XY_FLIPS[j]
            return pltpu.make_async_remote_copy(
                src_ref=src[k], dst_ref=dst[k].at[slot], send_sem=send_sems.at[k * npeer + j],
                recv_sem=recv_sems.at[k * npeer + j], device_id=(_flip(x, fx), _flip(y, fy), c), device_id_type=MESH)

        mine = [pltpu.make_async_copy(src[k], dst[k].at[2 * x + y], local_sems.at[k]) for k in range(n)]
        sends = [copy(k, j, 2 * x + y) for k in range(n) for j in range(npeer)]
        for cp in mine + sends:
            cp.start()
        for k in range(n):
            for j, (fx, fy) in enumerate(XY_FLIPS):
                copy(k, j, 2 * _flip(x, fx) + _flip(y, fy)).wait_recv()
        for cp in sends:
            cp.wait_send()
        for cp in mine:
            cp.wait()

    return pl.pallas_call(
        body, name="gather_weights", in_specs=[ANY] * n, out_specs=[ANY] * n,
        out_shape=[jax.ShapeDtypeStruct((N_SHARD,) + s.shape, s.dtype) for s in shards],
        scratch_shapes=[pltpu.SemaphoreType.DMA((n * npeer,)), pltpu.SemaphoreType.DMA((n * npeer,)),
                        pltpu.SemaphoreType.DMA((n,))],
    )(*shards)


def _sem_scratch(n):
    return [pltpu.SemaphoreType.DMA((n,)), pltpu.SemaphoreType.DMA((n,))]


def placement():
    x, y, c = lax.axis_index("x"), lax.axis_index("y"), lax.axis_index("c")
    me = 2 * x + y
    others = [j + (j >= me).astype(jnp.int32) for j in range(N_SHARD - 1)]
    return jnp.stack([c, me] + others).astype(jnp.int32)


def pair_exchange(sources):
    n = len(sources)

    def body(*refs):
        src, got = refs[:n], refs[n:2 * n]
        send_sems, recv_sems = refs[2 * n:]
        x, y, c = lax.axis_index("x"), lax.axis_index("y"), lax.axis_index("c")

        def copy(k):
            half = sources[k].shape[1] // 2
            theirs = src[k].at[:, pl.ds(pl.multiple_of((1 - c) * half, 8), half), :]
            return pltpu.make_async_remote_copy(
                src_ref=theirs, dst_ref=got[k], send_sem=send_sems.at[k], recv_sem=recv_sems.at[k],
                device_id=(x, y, 1 - c), device_id_type=MESH)

        sends = [copy(k) for k in range(n)]
        for cp in sends:
            cp.start()
        for cp in sends:
            cp.wait_recv()
        for cp in sends:
            cp.wait_send()

    halves = [jax.ShapeDtypeStruct((s.shape[0], s.shape[1] // 2, s.shape[2]), s.dtype) for s in sources]
    return pl.pallas_call(body, name="grads_pair_exchange", in_specs=[ANY] * n, out_specs=[ANY] * n,
                          out_shape=halves, scratch_shapes=_sem_scratch(n))(*sources)


def chip_exchange(parts):
    n = len(parts)
    npeer = len(XY_FLIPS)

    def body(*refs):
        src, dst = refs[:n], refs[n:2 * n]
        send_sems, recv_sems = refs[2 * n:]
        x, y, c = lax.axis_index("x"), lax.axis_index("y"), lax.axis_index("c")
        me = 2 * x + y

        def copy(k, j, sending):
            fx, fy = XY_FLIPS[j]
            px, py = _flip(x, fx), _flip(y, fy)
            peer = 2 * px + py
            return pltpu.make_async_remote_copy(
                src_ref=src[k].at[peer], dst_ref=dst[k].at[me if sending else peer],
                send_sem=send_sems.at[k * npeer + j], recv_sem=recv_sems.at[k * npeer + j],
                device_id=(px, py, c), device_id_type=MESH)

        sends = [copy(k, j, True) for k in range(n) for j in range(npeer)]
        for cp in sends:
            cp.start()
        for k in range(n):
            for j in range(npeer):
                copy(k, j, False).wait_recv()
        for cp in sends:
            cp.wait_send()

    return pl.pallas_call(
        body, name="grads_chip_exchange", in_specs=[ANY] * n, out_specs=[ANY] * n,
        out_shape=[jax.ShapeDtypeStruct(p.shape, p.dtype) for p in parts], scratch_shapes=_sem_scratch(n * npeer),
    )(*parts)


def sibling_share(halves):
    n = len(halves)

    def body(*refs):
        src, got = refs[:n], refs[n:2 * n]
        send_sems, recv_sems = refs[2 * n:]
        x, y, c = lax.axis_index("x"), lax.axis_index("y"), lax.axis_index("c")
        sends = [pltpu.make_async_remote_copy(
            src_ref=src[k], dst_ref=got[k], send_sem=send_sems.at[k], recv_sem=recv_sems.at[k],
            device_id=(x, y, 1 - c), device_id_type=MESH) for k in range(n)]
        for cp in sends:
            cp.start()
        for cp in sends:
            cp.wait_recv()
        for cp in sends:
            cp.wait_send()

    return pl.pallas_call(
        body, name="grads_sibling_share", in_specs=[ANY] * n, out_specs=[ANY] * n,
        out_shape=[jax.ShapeDtypeStruct(h.shape, h.dtype) for h in halves], scratch_shapes=_sem_scratch(n),
    )(*halves)


ADD_TILE_ELEMS = 512 * 1024


def _row_tile(rows, cols):
    return max(t for t in range(8, rows + 1, 8) if rows % t == 0 and t * cols <= ADD_TILE_ELEMS)


def _prefetch_call(body, name, place, grid, in_specs, out_specs, out_shape, args):
    return pl.pallas_call(
        body, name=name, out_shape=out_shape,
        grid_spec=pltpu.PrefetchScalarGridSpec(num_scalar_prefetch=1, grid=grid, in_specs=in_specs,
                                               out_specs=out_specs),
        compiler_params=pltpu.CompilerParams(dimension_semantics=("arbitrary",) * len(grid),
                                             vmem_limit_bytes=VMEM_LIMIT),
    )(place, *args)


def pair_add(name, place, src, got, dtype):
    n4, half, cols = got.shape
    tile = _row_tile(half, cols)
    nt = half // tile

    def body(pr, a_ref, b_ref, o_ref):
        o_ref[...] = (a_ref[...] + b_ref[...]).astype(o_ref.dtype)

    mine = pl.BlockSpec((None, tile, cols), lambda s, i, pr: (s, pr[0] * nt + i, 0))
    blk = pl.BlockSpec((None, tile, cols), lambda s, i, pr: (s, i, 0))
    return _prefetch_call(body, name, place, (n4, nt), [mine, blk], blk,
                          jax.ShapeDtypeStruct(got.shape, dtype), (src, got))


def chip_add(name, place, part, from_chips):
    _, half, cols = part.shape
    tile = _row_tile(half, cols)

    def body(pr, own_ref, r0_ref, r1_ref, r2_ref, o_ref):
        me = pr[1]
        own, r0, r1, r2 = (r[...].astype(F32) for r in (own_ref, r0_ref, r1_ref, r2_ref))
        t0 = jnp.where(me == 0, own, r0)
        t1 = jnp.where(me == 0, r0, jnp.where(me == 1, own, r1))
        t2 = jnp.where(me <= 1, r1, jnp.where(me == 2, own, r2))
        t3 = jnp.where(me == 3, own, r2)
        o_ref[...] = ((t0 + t1) + t2) + t3

    def slab(j):
        return pl.BlockSpec((None, tile, cols), lambda i, pr: (pr[j], i, 0))

    return _prefetch_call(body, name, place, (half // tile,), [slab(1), slab(2), slab(3), slab(4)],
                          pl.BlockSpec((tile, cols), lambda i, pr: (i, 0)),
                          jax.ShapeDtypeStruct((half, cols), F32), (part, from_chips, from_chips, from_chips))


def reduce_grads(sources, narrow):
    place = placement()
    got = pair_exchange(sources)
    parts = [pair_add(f"grads_pair_add{k}", place, s, g, BF16 if nar else F32)
             for k, (s, g, nar) in enumerate(zip(sources, got, narrow))]
    from_chips = chip_exchange(parts)
    halves = [chip_add(f"grads_chip_add{k}", place, p, f) for k, (p, f) in enumerate(zip(parts, from_chips))]
    return place, halves, sibling_share(halves)


ADAM_ROWS = 256


def adamw_update(name, place, halves, w, m, v):
    nsub, rows, cols = w.shape
    half = rows // 2
    tr = ADAM_ROWS if half % ADAM_ROWS == 0 else half
    nth = half // tr

    def body(pr, *refs):
        g_refs, (w_ref, m_ref, v_ref, g_ref, d_ref, nm_ref, nv_ref) = refs[:2 * nsub], refs[2 * nsub:]
        l = pl.program_id(0)
        mine = (pl.program_id(1) // nth) == pr[0]
        g = None
        for s in range(nsub):
            gs = jnp.where(mine, g_refs[2 * s][...], g_refs[2 * s + 1][...])
            g = gs if g is None else jnp.where(l == s, gs, g)
        m2 = ADAM_B1 * m_ref[...] + (1.0 - ADAM_B1) * g
        v2 = ADAM_B2 * v_ref[...] + (1.0 - ADAM_B2) * (g * g)
        m_hat = m2 / (1.0 - ADAM_B1 ** ADAM_STEP)
        v_hat = v2 / (1.0 - ADAM_B2 ** ADAM_STEP)
        g_ref[...] = g
        d_ref[...] = -ADAM_LR * (m_hat / (jnp.sqrt(v_hat) + ADAM_EPS) + ADAM_WD * w_ref[...])
        nm_ref[...] = m2
        nv_ref[...] = v2

    gblk = pl.BlockSpec((tr, cols), lambda l, i, pr: (i % nth, 0))
    blk = pl.BlockSpec((None, tr, cols), lambda l, i, pr: (l, i, 0))
    out = jax.ShapeDtypeStruct((nsub, rows, cols), F32)
    return _prefetch_call(body, name, place, (nsub, rows // tr), [gblk] * (2 * nsub) + [blk] * 3, [blk] * 4,
                          [out] * 4, [h for pair in halves for h in pair] + [w, m, v])


WEIGHT_NAMES = ("meta_tokens", "a_mu", "a_w_r", "a_w_k", "a_w_v", "a_w_o", "a_w0", "a_w1", "a_w2", "a_a0", "a_a1",
                "a_a2", "a_g1", "a_g2", "a_k_k", "a_k_a", "a_r_k", "a_gn_w", "a_gn_b", "kv_w_k", "kv_w_v", "b_w_q",
                "b_sinks", "b_w_o", "mlp_w_up", "mlp_w_down", "ln_g", "ln_b")
BIG_NAMES = ("a_w_r", "a_w_k", "a_w_v", "a_w_o", "b_w_q", "b_w_o")
PACK_MATS = (("kv_w_k", 256), ("kv_w_v", 256), ("a_w1", 64), ("a_a1", 64), ("a_g1", 128), ("a_w2", 64),
             ("a_a2", 64), ("a_g2", 128))
COLUMN_CUT = ("a_w2", "a_a2", "a_g2")
PACK_VECS = (("a_mu", 6), ("a_w0", 1), ("a_a0", 1), ("a_k_k", 1), ("a_k_a", 1), ("a_gn_w", 1), ("a_gn_b", 1),
             ("ln_g", 4), ("ln_b", 4), ("meta_tokens", 16))
PACK_REPL = (("a_r_k", 4), ("b_sinks", 1))
SHARD_W = D_MODEL // N_SHARD
N_MAT_ROWS = sum(r for _, r in PACK_MATS)
N_VEC_ROWS = sum(r for _, r in PACK_VECS)
N_PACK_ROWS = -(-(N_MAT_ROWS + N_VEC_ROWS + sum(r for _, r in PACK_REPL)) // 8) * 8
N_GATHER_VEC_ROWS = -(-N_VEC_ROWS // 8) * 8


def _pack_rows(arr):
    if arr.size == N_HEADS:
        return jnp.pad(arr.reshape(1, N_HEADS), ((0, 0), (0, SHARD_W - N_HEADS)))
    return arr.reshape(-1, SHARD_W)


def pack_small(get):
    parts = [_pack_rows(get(name)) for name, _ in PACK_MATS + PACK_VECS + PACK_REPL]
    used = sum(p.shape[0] for p in parts)
    return jnp.concatenate(parts + [jnp.zeros((N_PACK_ROWS - used, SHARD_W), F32)], axis=0)


def unpack_small(pack, shapes):
    out, off = {}, 0
    for name, rows in PACK_MATS + PACK_VECS + PACK_REPL:
        piece = pack[off:off + rows]
        off += rows
        out[name] = piece[:, :N_HEADS].reshape(shapes[name]) if name == "b_sinks" else piece.reshape(shapes[name])
    return out


def whole_weights(gathered_big, mats, vecs, a_r_k, b_sinks):
    p = {name: g.reshape(D_MODEL, D_MODEL) for name, g in zip(BIG_NAMES, gathered_big)}
    off = 0
    for name, rows in PACK_MATS:
        piece = mats[:, off:off + rows]
        off += rows
        if name in COLUMN_CUT:
            p[name] = piece.transpose(1, 0, 2).reshape(rows, D_MODEL)
        else:
            p[name] = piece.reshape(D_MODEL, rows)
    v = vecs.transpose(1, 0, 2).reshape(-1, D_MODEL)
    off = 0
    for name, rows in PACK_VECS:
        p[name] = v[off:off + rows]
        off += rows
    for i in range(2):
        for j in range(2):
            p[f"ln_g{i}{j}"] = p["ln_g"][2 * i + j:2 * i + j + 1]
            p[f"ln_b{i}{j}"] = p["ln_b"][2 * i + j:2 * i + j + 1]
    p["a_r_k"] = a_r_k.reshape(1, D_MODEL)
    p["b_sinks"] = b_sinks
    return p


def small_grad_pack(g):
    parts = []
    for name, rows in PACK_MATS:
        if name in COLUMN_CUT:
            parts.append(g[name].reshape(rows, N_SHARD, SHARD_W).transpose(1, 0, 2))
        else:
            parts.append(g[name].reshape(N_SHARD, rows, SHARD_W))
    vec_rows = [g["a_mu"]] + [g[n] for n in ("a_w0", "a_a0", "a_k_k", "a_k_a", "a_gn_w", "a_gn_b")]
    vec_rows += [g[f"ln_g{i}{j}"] for i in range(2) for j in range(2)]
    vec_rows += [g[f"ln_b{i}{j}"] for i in range(2) for j in range(2)] + [g["meta_tokens"]]
    parts.append(jnp.concatenate(vec_rows, axis=0).reshape(N_VEC_ROWS, N_SHARD, SHARD_W).transpose(1, 0, 2))
    parts.append(jnp.broadcast_to(g["a_r_k"].reshape(1, -1, SHARD_W), (N_SHARD, D_MODEL // SHARD_W, SHARD_W)))
    sinks = jnp.pad(g["b_sinks"].reshape(1, 1, N_HEADS), ((0, 0), (0, 0), (0, SHARD_W - N_HEADS)))
    parts.append(jnp.broadcast_to(sinks, (N_SHARD, 1, SHARD_W)))
    used = sum(p.shape[1] for p in parts)
    parts.append(jnp.zeros((N_SHARD, N_PACK_ROWS - used, SHARD_W), F32))
    return jnp.concatenate(parts, axis=1)


def train_step(vals):
    w = {n: vals[n] for n in WEIGHT_NAMES}
    w_pack = pack_small(lambda n: w[n])
    shards = [w[n][0].astype(BF16) for n in BIG_NAMES]
    shards += [w["mlp_w_up"].astype(BF16), w["mlp_w_down"].astype(BF16), w_pack[:N_MAT_ROWS].astype(BF16),
               w_pack[N_MAT_ROWS:N_MAT_ROWS + N_GATHER_VEC_ROWS]]
    gathered = all_gather_shards(shards)
    nb = len(BIG_NAMES)
    p = whole_weights(gathered[:nb], gathered[nb + 2], gathered[nb + 3][:, :N_VEC_ROWS], w["a_r_k"], w["b_sinks"])
    p["mlp_up"], p["mlp_down"] = gathered[nb], gathered[nb + 1]

    loss, gx, g = local_step(vals["x"][0], vals["loss_target"][0], p)
    loss = lax.psum(loss, ("x", "y", "c"))

    sources = [g[n].reshape(N_SHARD, SHARD_W, D_MODEL) for n in BIG_NAMES]
    sources += [g["mlp_up0"], g["mlp_up1"], g["mlp_down0"], g["mlp_down1"], small_grad_pack(g)]
    place, mine, theirs = reduce_grads(sources, [True] * (len(sources) - 1) + [False])
    halves = list(zip(mine, theirs))

    res = {}
    for k, n in enumerate(BIG_NAMES):
        res[n] = adamw_update("adamw_" + n, place, halves[k:k + 1], w[n], vals["m_" + n], vals["v_" + n])
    for k, n in ((nb, "mlp_w_up"), (nb + 2, "mlp_w_down")):
        res[n] = adamw_update("adamw_" + n, place, halves[k:k + 2], w[n], vals["m_" + n], vals["v_" + n])
    packs = adamw_update("adamw_small", place, halves[-1:], w_pack[None], pack_small(lambda n: vals["m_" + n])[None],
                         pack_small(lambda n: vals["v_" + n])[None])
    shapes = {n: w[n].shape for n in WEIGHT_NAMES}
    small = [unpack_small(pk[0], shapes) for pk in packs]
    outs = [loss, gx[None]]
    for t in range(4):
        outs += [res[n][t] if n in res else small[t][n] for n in WEIGHT_NAMES]
    return tuple(outs)


def kernel(x, meta_tokens, a_mu, a_w_r, a_w_k, a_w_v, a_w_o, a_w0, a_w1, a_w2, a_a0, a_a1, a_a2, a_g1, a_g2, a_k_k,
           a_k_a, a_r_k, a_gn_w, a_gn_b, kv_w_k, kv_w_v, b_w_q, b_sinks, b_w_o, mlp_w_up, mlp_w_down, ln_g, ln_b,
           loss_target, m_meta_tokens, m_a_mu, m_a_w_r, m_a_w_k, m_a_w_v, m_a_w_o, m_a_w0, m_a_w1, m_a_w2, m_a_a0,
           m_a_a1, m_a_a2, m_a_g1, m_a_g2, m_a_k_k, m_a_k_a, m_a_r_k, m_a_gn_w, m_a_gn_b, m_kv_w_k, m_kv_w_v,
           m_b_w_q, m_b_sinks, m_b_w_o, m_mlp_w_up, m_mlp_w_down, m_ln_g, m_ln_b, v_meta_tokens, v_a_mu, v_a_w_r,
           v_a_w_k, v_a_w_v, v_a_w_o, v_a_w0, v_a_w1, v_a_w2, v_a_a0, v_a_a1, v_a_a2, v_a_g1, v_a_g2, v_a_k_k,
           v_a_k_a, v_a_r_k, v_a_gn_w, v_a_gn_b, v_kv_w_k, v_kv_w_v, v_b_w_q, v_b_sinks, v_b_w_o, v_mlp_w_up,
           v_mlp_w_down, v_ln_g, v_ln_b):
    return train_step(dict(locals()))
```

```python
import functools

import numpy as np
import jax
import jax.numpy as jnp
from jax import lax
from jax.experimental import pallas as pl
from jax.experimental.pallas import tpu as pltpu

F32 = jnp.float32
BF16 = jnp.bfloat16

D_MODEL = 1024
N_HEADS = 16
HEAD_DIM = 64
N_HEADS_KV = 4
GROUP = 4
KV_DIM = N_HEADS_KV * HEAD_DIM
N_META = 16
BLOCK = 128
PAD_FRONT = BLOCK - N_META
TOK0 = PAD_FRONT + N_META
N_FF_CHUNK = 4
N_SHARD = 4
N_DEV = 8
GN_EPS = 64e-5
LN_EPS = 1e-5
ROPE_THETA = 10000.0
ALPHA = 4.0 ** 0.25
ADAM_LR, ADAM_B1, ADAM_B2, ADAM_EPS, ADAM_WD, ADAM_STEP = 0.001, 0.9, 0.999, 1e-08, 0.01, 10
SCAN_T = 64
PAIR = 128
KVW = GROUP * HEAD_DIM
VMEM_LIMIT = 56 * 1024 * 1024
HI = lax.Precision.HIGHEST
MESH = pl.DeviceIdType.MESH


def _dot(a, b, ca, cb):
    return lax.dot_general(a.astype(BF16), b.astype(BF16), (((ca,), (cb,)), ((), ())),
                           preferred_element_type=F32)


@jax.custom_vjp
def mm(a, b):
    return _dot(a, b, 1, 0)


def _mm_fwd(a, b):
    return mm(a, b), b


def _mm_bwd(b, g):
    return _dot(g, b, 1, 1), jnp.zeros_like(b)


mm.defvjp(_mm_fwd, _mm_bwd)


def tmm(x, w, taps, xs):
    y = mm(x, w)
    if taps is not None:
        y = y + taps[len(xs)]
    xs.append(x)
    return y


def vjp_taps(core, tap_shapes, args, cot):
    taps = [jnp.zeros(s, F32) for s in tap_shapes]
    _, vjp, xs = jax.vjp(core, taps, *args, has_aux=True)
    out = vjp(cot)
    return out[1:], [_dot(x, g, 0, 0) for x, g in zip(xs, out[0])]


def _split3(x):
    x1 = x.astype(BF16)
    r1 = x - x1.astype(F32)
    x2 = r1.astype(BF16)
    x3 = (r1 - x2.astype(F32)).astype(BF16)
    return x1, x2, x3


def _exact_dot(x, m01, cb=0):
    acc = None
    for piece in _split3(x):
        t = lax.dot_general(piece, m01, (((1,), (cb,)), ((), ())), preferred_element_type=F32)
        acc = t if acc is None else acc + t
    return acc


def _head_matrices():
    e = np.zeros((D_MODEL, N_HEADS), np.float32)
    e[np.arange(D_MODEL), np.arange(D_MODEL) // HEAD_DIM] = 1.0
    return jnp.asarray(e, BF16), jnp.asarray(e.T, BF16)


@jax.custom_vjp
def hsum(x, e, et):
    return _exact_dot(x, e)


@jax.custom_vjp
def hbc(s, e, et):
    return _exact_dot(s, et)


hsum.defvjp(lambda x, e, et: (_exact_dot(x, e), (e, et)),
            lambda res, g: (hbc(g, *res), jnp.zeros_like(res[0]), jnp.zeros_like(res[1])))
hbc.defvjp(lambda s, e, et: (_exact_dot(s, et), (e, et)),
           lambda res, g: (hsum(g, *res), jnp.zeros_like(res[0]), jnp.zeros_like(res[1])))


def _sigmoid(u):
    return 0.5 * (jnp.tanh(0.5 * u) + 1.0)


def _softplus(u):
    return jnp.maximum(u, 0.0) + jnp.log(1.0 + jnp.exp(-jnp.abs(u)))


def _layer_norm(z, g, b):
    mu = jnp.mean(z, axis=-1, keepdims=True)
    zc = z - mu
    var = jnp.mean(zc * zc, axis=-1, keepdims=True)
    return zc * lax.rsqrt(var + LN_EPS) * g + b


def _zero_map(nd):
    return lambda c, i: (0,) * nd


def _params():
    return pltpu.CompilerParams(dimension_semantics=("arbitrary", "arbitrary"), vmem_limit_bytes=VMEM_LIMIT)


def rowwise(name, fn, rows, consts, out_rows, out_accs, tm, nc=1):
    lp = rows[0].shape[-2]
    nt = lp // tm
    assert nt * tm == lp, (name, lp, tm)
    in_specs, args = [], []
    for a in rows:
        if a.ndim == 2:
            in_specs.append(pl.BlockSpec((tm, a.shape[1]), lambda c, i: (i, 0)))
        else:
            in_specs.append(pl.BlockSpec((a.shape[0], tm, a.shape[2]), lambda c, i: (0, i, 0)))
        args.append(a)
    for cst in consts:
        if isinstance(cst, tuple):
            arr, bs, im = cst
            in_specs.append(pl.BlockSpec(bs, im))
        else:
            arr = cst
            in_specs.append(pl.BlockSpec(arr.shape, _zero_map(arr.ndim), pipeline_mode=pl.Buffered(1)))
        args.append(arr)
    out_shape, out_specs, acc_per_chunk = [], [], []
    for spec in out_rows:
        if len(spec) == 3 and spec[2]:
            out_shape.append(jax.ShapeDtypeStruct((nc, lp, spec[0]), spec[1]))
            out_specs.append(pl.BlockSpec((None, tm, spec[0]), lambda c, i: (c, i, 0)))
        else:
            out_shape.append(jax.ShapeDtypeStruct((lp, spec[0]), spec[1]))
            out_specs.append(pl.BlockSpec((tm, spec[0]), lambda c, i: (i, 0)))
    for spec in out_accs:
        out_shape.append(jax.ShapeDtypeStruct(spec[0], spec[1]))
        if len(spec) == 4:
            out_specs.append(pl.BlockSpec(spec[2], spec[3]))
            acc_per_chunk.append(True)
        else:
            out_specs.append(pl.BlockSpec(spec[0], _zero_map(len(spec[0])), pipeline_mode=pl.Buffered(1)))
            acc_per_chunk.append(False)
    n_in, n_or = len(args), len(out_rows)

    def body(*refs):
        c = pl.program_id(0)
        i = pl.program_id(1)
        vals = [r[...] for r in refs[:n_in]]
        outs_r, outs_a = fn(c, i, *vals)
        for ref, val in zip(refs[n_in:n_in + n_or], outs_r):
            ref[...] = val.astype(ref.dtype)
        for ref, val, per_chunk in zip(refs[n_in + n_or:], outs_a, acc_per_chunk):
            first = (i == 0) if per_chunk else jnp.logical_and(i == 0, c == 0)

            @pl.when(first)
            def _():
                ref[...] = val.astype(ref.dtype)

            @pl.when(jnp.logical_not(first))
            def _():
                ref[...] += val.astype(ref.dtype)

    outs = pl.pallas_call(body, name=name, grid=(nc, nt), in_specs=in_specs, out_specs=out_specs,
                          out_shape=out_shape, compiler_params=_params())(*args)
    return outs[:n_or], outs[n_or:]


def _row_ids(i, tm):
    return i * tm + lax.broadcasted_iota(jnp.int32, (tm, 1), 0)


PRE_TAPS = (D_MODEL, D_MODEL, D_MODEL, 64, D_MODEL, 64, D_MODEL, 128, D_MODEL)


def rwkv_pre(e, et, ws, taps, h, hp, mu_r, mu_w, mu_k, mu_v, mu_a, mu_g, w0, a0, k_k, k_a):
    w_r, w_k, w_v, w1, w2, a1, a2, g1, g2 = ws
    xs = []
    xx = hp - h
    r = tmm(h + xx * mu_r, w_r, taps, xs)
    k = tmm(h + xx * mu_k, w_k, taps, xs)
    v = tmm(h + xx * mu_v, w_v, taps, xs)
    wraw = -_softplus(-(w0 + tmm(jnp.tanh(tmm(h + xx * mu_w, w1, taps, xs)), w2, taps, xs))) - 0.5
    lw = -jnp.exp(wraw)
    a = _sigmoid(a0 + tmm(tmm(h + xx * mu_a, a1, taps, xs), a2, taps, xs))
    g = tmm(_sigmoid(tmm(h + xx * mu_g, g1, taps, xs)), g2, taps, xs)
    kk = k * k_k
    ss = hsum(kk * kk, e, et)
    pos = ss > 0.0
    nrm = jnp.where(pos, jnp.sqrt(jnp.where(pos, ss, 1.0)), 0.0)
    kk = kk * hbc(1.0 / jnp.maximum(nrm, 1e-12), e, et)
    k2 = k * (1.0 + (a - 1.0) * k_a)
    return (r, lw, k2, v, -kk, kk * a, g), xs


def rwkv_post(e, et, w_o, taps, y, r, k2, v, g, h0, gn_w, gn_b, rk, lg, lb):
    xs = []
    inv_n = 1.0 / HEAD_DIM
    yc = y - hbc(hsum(y, e, et) * inv_n, e, et)
    yv = hsum(yc * yc, e, et) * inv_n
    yn = yc * hbc(lax.rsqrt(yv + GN_EPS), e, et) * gn_w + gn_b
    bonus = hbc(hsum(r * k2 * rk, e, et), e, et) * v
    mix = tmm((yn + bonus) * g, w_o, taps, xs)
    return _layer_norm(ALPHA * h0 + mix, lg, lb), xs


def mlp_chunk(wup, wdown, taps, h):
    xs = []
    u = jnp.maximum(tmm(h, wup, taps, xs), 0.0)
    return tmm(u * u, wdown, taps, xs), xs


def _rot_half(t):
    n = t.shape[-1]
    lane = lax.broadcasted_iota(jnp.int32, t.shape, t.ndim - 1)
    lo = (lane % HEAD_DIM) < (HEAD_DIM // 2)
    return jnp.where(lo, -pltpu.roll(t, n - HEAD_DIM // 2, t.ndim - 1), pltpu.roll(t, HEAD_DIM // 2, t.ndim - 1))


@jax.custom_vjp
def rot_half(t):
    return _rot_half(t)


rot_half.defvjp(lambda t: (_rot_half(t), None), lambda _, g: (-_rot_half(g),))


def _tile_lanes(t, width):
    return jnp.concatenate([t] * (width // t.shape[-1]), axis=-1)


def qkv_proj(cos, sin, wq, wk, wv, taps, h):
    xs = []
    q = tmm(h, wq, taps, xs)
    k = tmm(h, wk, taps, xs)
    v = tmm(h, wv, taps, xs)
    cq, sq = _tile_lanes(cos, D_MODEL), _tile_lanes(sin, D_MODEL)
    ck, sk = _tile_lanes(cos, KV_DIM), _tile_lanes(sin, KV_DIM)
    return (q * cq + rot_half(q) * sq, k * ck + rot_half(k) * sk, v), xs


def attn_out(w_o, taps, o, h, lg, lb):
    xs = []
    return _layer_norm(ALPHA * h + tmm(o, w_o, taps, xs), lg, lb), xs


def _scan_consts():
    t = SCAN_T
    tri = np.tril(np.ones((t, t), np.float32))
    rows = np.arange(2 * t)
    same = (rows[:, None] // t) == (rows[None, :] // t)
    strict = same & ((rows[None, :] % t) < (rows[:, None] % t))
    incl = same & ((rows[None, :] % t) <= (rows[:, None] % t))
    lane = np.arange(PAIR)
    masks = np.zeros((8, PAIR), np.float32)
    masks[0] = (lane // HEAD_DIM) == 0
    masks[1] = (lane // HEAD_DIM) == 1
    return (jnp.asarray(tri, BF16), jnp.asarray(strict.astype(np.float32)), jnp.asarray(incl.astype(np.float32)),
            jnp.asarray(masks), jnp.asarray(np.eye(2 * t, dtype=np.float32)))


def _dot_x3(a, b, ca, cb):
    a1 = a.astype(BF16)
    a2 = (a - a1.astype(F32)).astype(BF16)
    b1 = b.astype(BF16)
    b2 = (b - b1.astype(F32)).astype(BF16)

    def d(u, v):
        return lax.dot_general(u, v, (((ca,), (cb,)), ((), ())), preferred_element_type=F32)

    return d(a1, b1) + (d(a1, b2) + d(a2, b1))


@functools.partial(jax.custom_vjp, nondiff_argnums=(2, 3))
def _dotf(a, b, ca, cb):
    return _dot_x3(a, b, ca, cb)


def _dotf_bwd(ca, cb, res, g):
    a, b = res
    if ca == 1:
        da = _dot_x3(g, b, 1, 1 - cb)
    else:
        da = _dot_x3(b, g, 1 - cb, 1)
    if cb == 0:
        db = _dot_x3(a, g, 1 - ca, 0)
    else:
        db = _dot_x3(g, a, 0, 1 - ca)
    return da, db


_dotf.defvjp(lambda a, b, ca, cb: (_dot_x3(a, b, ca, cb), (a, b)), _dotf_bwd)


def _tri_dot(tri, x, ct):
    acc = None
    for piece in _split3(x):
        t = lax.dot_general(tri, piece, (((ct,), (0,)), ((), ())), preferred_element_type=F32)
        acc = t if acc is None else acc + t
    return acc


@jax.custom_vjp
def _cumsum_rows(tri, x):
    return _tri_dot(tri, x, 1)


_cumsum_rows.defvjp(lambda tri, x: (_tri_dot(tri, x, 1), tri),
                    lambda tri, g: (jnp.zeros_like(tri), _tri_dot(tri, g, 0)))


@jax.custom_vjp
def _unstack2(x):
    t = x.shape[0] // 2
    return x[:t] + x[t:]


_unstack2.defvjp(lambda x: (_unstack2(x), None), lambda _, g: (jnp.concatenate([g, g], axis=0),))


@jax.custom_vjp
def _last_row(x):
    return x[x.shape[0] - 1:, :]


def _last_row_bwd(_, g):
    rows = lax.broadcasted_iota(jnp.int32, (SCAN_T, g.shape[1]), 0)
    return (jnp.where(rows == SCAN_T - 1, jnp.broadcast_to(g, (SCAN_T, g.shape[1])), 0.0),)


_last_row.defvjp(lambda x: (_last_row(x), None), _last_row_bwd)


@jax.custom_vjp
def _solve_saved(n, rhs, minv, u):
    return u


def _solve_saved_bwd(res, du):
    minv, u = res
    drhs = _dotf(minv, du, 0, 0)
    return _dotf(drhs, u, 1, 1), drhs, jnp.zeros_like(minv), jnp.zeros_like(u)


_solve_saved.defvjp(lambda n, rhs, minv, u: (u, (minv, u)), _solve_saved_bwd)


def scan_chunk(tri, strict, incl, m0, m1, eye, r, lw, k, v, a, b, s0, saved=None):
    lower = strict > 0
    lower_incl = incl > 0

    def stack(x):
        return jnp.concatenate([x * m0, x * m1], axis=0)

    def dots(xs, ys, ca, cb, mask=None):
        out = [_dotf(x, y, ca, cb) for x, y in zip(xs, ys)]
        return out if mask is None else [jnp.where(mask, o, 0.0) for o in out]

    cl = [_cumsum_rows(tri, x) for x in lw]
    gam = [jnp.exp(c) for c in cl]
    ginv = [jnp.exp(-c) for c in cl]
    a_s = [stack(x * jnp.exp(c - w)) for x, c, w in zip(a, cl, lw)]
    r_s = [stack(x * g) for x, g in zip(r, gam)]
    b_s = [stack(x * g) for x, g in zip(b, ginv)]
    k_s = [stack(x * g) for x, g in zip(k, ginv)]
    v_s = [stack(x) for x in v]
    n_ab = dots(a_s, b_s, 1, 1, lower)
    n_ak = dots(a_s, k_s, 1, 1, lower)
    r_ab = dots(r_s, b_s, 1, 1, lower_incl)
    r_ak = dots(r_s, k_s, 1, 1, lower_incl)
    rhs = [x + y for x, y in zip(dots(a_s, s0, 1, 1), dots(n_ak, v_s, 1, 0))]
    if saved is None:
        minv = [eye + n for n in n_ab]
        p = n_ab
        for _ in range(5):
            p = dots(p, p, 1, 0)
            minv = [m + mp for m, mp in zip(minv, dots(minv, p, 1, 0))]
        u_s = dots(minv, rhs, 1, 0)
    else:
        minv = saved[0]
        u_s = [_solve_saved(n, x, m, u) for n, x, m, u in zip(n_ab, rhs, *saved)]
    y = [_unstack2(x0 + x1 + x2)
         for x0, x1, x2 in zip(dots(r_s, s0, 1, 1), dots(r_ab, u_s, 1, 0), dots(r_ak, v_s, 1, 0))]
    g_end = [_last_row(g) for g in gam]
    s1 = [s * g + x + z for s, g, x, z in zip(s0, g_end, dots(u_s, [x * g for x, g in zip(b_s, g_end)], 0, 0),
                                              dots(v_s, [x * g for x, g in zip(k_s, g_end)], 0, 0))]
    return y, s1, (minv, u_s)


SCAN_PAIRS = 4


def _scan_specs(consts, order):
    row = pl.BlockSpec((SCAN_T, PAIR * SCAN_PAIRS), lambda p, c: (order(c), p))
    state = pl.BlockSpec((None, SCAN_PAIRS, PAIR, PAIR), lambda p, c: (order(c), p, 0, 0))
    return row, state, [pl.BlockSpec(x.shape, _zero_map(x.ndim)) for x in consts]


def _pair_lanes(q):
    return slice(q * PAIR, (q + 1) * PAIR)


def scan_fwd(r, lw, k, v, a, b):
    lp = r.shape[0]
    nch = lp // SCAN_T
    npair = D_MODEL // PAIR
    consts = _scan_consts()
    row, state, cspecs = _scan_specs(consts, lambda c: c)

    def body(tri, strict, incl, masks, eye, r_ref, lw_ref, k_ref, v_ref, a_ref, b_ref, y_ref, s_ref, minv_ref,
             u_ref, carry):
        @pl.when(pl.program_id(1) == 0)
        def _():
            carry[...] = jnp.zeros_like(carry)

        pairs = range(SCAN_PAIRS)
        s0 = [carry[q] for q in pairs]
        rows = [[ref[:, _pair_lanes(q)] for q in pairs] for ref in (r_ref, lw_ref, k_ref, v_ref, a_ref, b_ref)]
        y, s1, (minv, u) = scan_chunk(tri[...], strict[...], incl[...], masks[0:1, :], masks[1:2, :], eye[...],
                                      *rows, s0)
        for q in pairs:
            s_ref[q] = s0[q]
            minv_ref[q] = minv[q]
            u_ref[q] = u[q]
            y_ref[:, _pair_lanes(q)] = y[q]
            carry[q] = s1[q]

    mats = jax.ShapeDtypeStruct((nch, npair, PAIR, PAIR), F32)
    return pl.pallas_call(
        body, name="rwkv_scan_fwd", grid=(npair // SCAN_PAIRS, nch), in_specs=cspecs + [row] * 6,
        out_specs=[row, state, state, state], out_shape=[jax.ShapeDtypeStruct((lp, D_MODEL), F32), mats, mats, mats],
        scratch_shapes=[pltpu.VMEM((SCAN_PAIRS, PAIR, PAIR), F32)], compiler_params=_params(),
    )(*consts, r, lw, k, v, a, b)


def scan_bwd(r, lw, k, v, a, b, saved, dy):
    lp = r.shape[0]
    nch = lp // SCAN_T
    npair = D_MODEL // PAIR
    consts = _scan_consts()
    row, state, cspecs = _scan_specs(consts, lambda c: nch - 1 - c)

    def body(tri, strict, incl, masks, eye, r_ref, lw_ref, k_ref, v_ref, a_ref, b_ref, s_ref, minv_ref, u_ref,
             dy_ref, dr_ref, dlw_ref, dk_ref, dv_ref, da_ref, db_ref, carry):
        @pl.when(pl.program_id(1) == 0)
        def _():
            carry[...] = jnp.zeros_like(carry)

        pairs = range(SCAN_PAIRS)
        kept = ([minv_ref[q] for q in pairs], [u_ref[q] for q in pairs])

        def fn(*args):
            y, s1, _ = scan_chunk(tri[...], strict[...], incl[...], masks[0:1, :], masks[1:2, :], eye[...], *args,
                                  saved=kept)
            return y, s1

        rows = [[ref[:, _pair_lanes(q)] for q in pairs] for ref in (r_ref, lw_ref, k_ref, v_ref, a_ref, b_ref)]
        _, vjp = jax.vjp(fn, *rows, [s_ref[q] for q in pairs])
        grads = vjp(([dy_ref[:, _pair_lanes(q)] for q in pairs], [carry[q] for q in pairs]))
        for q in pairs:
            for ref, g in zip((dr_ref, dlw_ref, dk_ref, dv_ref, da_ref, db_ref), grads[:6]):
                ref[:, _pair_lanes(q)] = g[q]
            carry[q] = grads[6][q]

    return pl.pallas_call(
        body, name="rwkv_scan_bwd", grid=(npair // SCAN_PAIRS, nch), in_specs=cspecs + [row] * 6 + [state] * 3 + [row],
        out_specs=[row] * 6, out_shape=[jax.ShapeDtypeStruct((lp, D_MODEL), F32)] * 6,
        scratch_shapes=[pltpu.VMEM((SCAN_PAIRS, PAIR, PAIR), F32)], compiler_params=_params(),
    )(*consts, r, lw, k, v, a, b, *saved, dy)


def _spread_matrices():
    rep = np.zeros((N_HEADS_KV, KV_DIM, KVW), np.float32)
    for h in range(N_HEADS_KV):
        for g in range(GROUP):
            rep[h, h * HEAD_DIM + np.arange(HEAD_DIM), g * HEAD_DIM + np.arange(HEAD_DIM)] = 1.0
    return jnp.asarray(rep, BF16)


def _attn_common(n, q, kp, kc, vp, vc, rep, sink):
    lane = lax.broadcasted_iota(jnp.int32, (1, KVW), 1)
    gmask = [(lane // HEAD_DIM == g).astype(F32) for g in range(GROUP)]
    q_s = jnp.concatenate([q * gmask[g] for g in range(GROUP)], axis=0)
    keys = _dot(jnp.concatenate([kp, kc], axis=0), rep, 1, 0)
    vals = _dot(jnp.concatenate([vp, vc], axis=0), rep, 1, 0)
    s = _dot(q_s, keys, 1, 1) * (HEAD_DIM ** -0.5)
    qi = lax.broadcasted_iota(jnp.int32, (GROUP * BLOCK, 2 * BLOCK), 0) % BLOCK
    kj = lax.broadcasted_iota(jnp.int32, (GROUP * BLOCK, 2 * BLOCK), 1)
    rel = BLOCK + qi - kj
    valid = (rel >= 0) & (rel < BLOCK) & ((n - 1) * BLOCK + kj >= PAD_FRONT)
    s = jnp.where(valid, s, -1e30)
    sink_col = jnp.concatenate([jnp.broadcast_to(sink[g:g + 1, 0:1], (BLOCK, 1)) for g in range(GROUP)], axis=0)
    m = jnp.maximum(jnp.max(s, axis=-1, keepdims=True), sink_col)
    ex = jnp.exp(s - m)
    ex_sink = jnp.exp(sink_col - m)
    inv = 1.0 / (jnp.sum(ex, axis=-1, keepdims=True) + ex_sink)
    return gmask, q_s, keys, vals, ex * inv, ex_sink * inv


def _unstack_groups(x_s, gmask):
    out = None
    for g in range(GROUP):
        t = x_s[g * BLOCK:(g + 1) * BLOCK] * gmask[g]
        out = t if out is None else out + t
    return out


def _attn_specs():
    qspec = pl.BlockSpec((BLOCK, KVW), lambda n, h: (n, h))
    cur = pl.BlockSpec((BLOCK, KV_DIM), lambda n, h: (n, 0))
    prev = pl.BlockSpec((BLOCK, KV_DIM), lambda n, h: (jnp.maximum(n - 1, 0), 0))
    rep = pl.BlockSpec((None, KV_DIM, KVW), lambda n, h: (h, 0, 0))
    sink = pl.BlockSpec((None, 8, PAIR), lambda n, h: (h, 0, 0))
    return qspec, cur, prev, rep, sink


def attn_fwd(q, k, v, sinks_b):
    lp = q.shape[0]
    qspec, cur, prev, rep, sink = _attn_specs()

    def body(q_ref, kp_ref, kc_ref, vp_ref, vc_ref, rep_ref, sink_ref, o_ref):
        gmask, _, _, vals, p, _ = _attn_common(pl.program_id(0), q_ref[...], kp_ref[...], kc_ref[...], vp_ref[...],
                                               vc_ref[...], rep_ref[...], sink_ref[...])
        o_ref[...] = _unstack_groups(_dot(p, vals, 1, 0), gmask)

    return pl.pallas_call(
        body, name="swa_fwd", grid=(lp // BLOCK, N_HEADS_KV), in_specs=[qspec, prev, cur, prev, cur, rep, sink],
        out_specs=qspec, out_shape=jax.ShapeDtypeStruct((lp, D_MODEL), F32), compiler_params=_params(),
    )(q, k, k, v, v, _spread_matrices(), sinks_b)


def attn_bwd(q, k, v, sinks_b, do):
    lp = q.shape[0]
    qspec, cur, prev, rep, sink = _attn_specs()

    def body(q_ref, kp_ref, kc_ref, vp_ref, vc_ref, rep_ref, sink_ref, do_ref, dq_ref, dkc_ref, dkp_ref, dvc_ref,
             dvp_ref, dsink_ref):
        n = pl.program_id(0)
        h = pl.program_id(1)
        gmask, q_s, keys, vals, p, p_sink = _attn_common(n, q_ref[...], kp_ref[...], kc_ref[...], vp_ref[...],
                                                         vc_ref[...], rep_ref[...], sink_ref[...])
        do = do_ref[...]
        do_s = jnp.concatenate([do * gmask[g] for g in range(GROUP)], axis=0)
        dp = _dot(do_s, vals, 1, 1)
        delta = jnp.sum(p * dp, axis=-1, keepdims=True)
        ds = p * (dp - delta) * (HEAD_DIM ** -0.5)
        dq_ref[...] = _unstack_groups(_dot(ds, keys, 1, 0), gmask)
        dkeys = _exact_dot(_dot(ds, q_s, 0, 0), rep_ref[...], cb=1)
        dvals = _exact_dot(_dot(p, do_s, 0, 0), rep_ref[...], cb=1)
        dsk = -(p_sink * delta)
        rows = [jnp.broadcast_to(jnp.sum(dsk[g * BLOCK:(g + 1) * BLOCK], axis=0, keepdims=True), (1, PAIR))
                for g in range(GROUP)]
        dsink = jnp.concatenate(rows + [jnp.zeros((8 - GROUP, PAIR), F32)], axis=0)

        @pl.when(h == 0)
        def _():
            dkp_ref[...] = dkeys[:BLOCK]
            dkc_ref[...] = dkeys[BLOCK:]
            dvp_ref[...] = dvals[:BLOCK]
            dvc_ref[...] = dvals[BLOCK:]

        @pl.when(h > 0)
        def _():
            dkp_ref[...] += dkeys[:BLOCK]
            dkc_ref[...] += dkeys[BLOCK:]
            dvp_ref[...] += dvals[:BLOCK]
            dvc_ref[...] += dvals[BLOCK:]

        @pl.when(n == 0)
        def _():
            dsink_ref[h] = dsink

        @pl.when(n > 0)
        def _():
            dsink_ref[h] += dsink

    kv = jax.ShapeDtypeStruct((lp, KV_DIM), F32)
    sink_all = pl.BlockSpec((N_HEADS_KV, 8, PAIR), lambda n, h: (0, 0, 0))
    return pl.pallas_call(
        body, name="swa_bwd", grid=(lp // BLOCK, N_HEADS_KV), in_specs=[qspec, prev, cur, prev, cur, rep, sink, qspec],
        out_specs=[qspec, cur, cur, cur, cur, sink_all],
        out_shape=[jax.ShapeDtypeStruct((lp, D_MODEL), F32), kv, kv, kv, kv,
                   jax.ShapeDtypeStruct((N_HEADS_KV, 8, PAIR), F32)],
        compiler_params=_params(),
    )(q, k, k, v, v, _spread_matrices(), sinks_b, do)


def _pick_tm(lp, want):
    for tm in (384, 192, 128, 64):
        if tm <= want and lp % tm == 0:
            return tm
    raise ValueError(lp)


def _acc(shape):
    return (tuple(shape), F32)


def _ff_all(w, layer):
    return (w, (N_FF_CHUNK, None, D_MODEL, D_MODEL), lambda c, i: (0, layer, 0, 0))


def _ff_one(w, layer):
    return (w, (None, None, D_MODEL, D_MODEL), lambda c, i: (c, layer, 0, 0))


def _mlp_layer_fwd(name, h, wup, wdown, layer, lg, lb, tm):
    def fn(c, i, h, wup, wdown, lg, lb):
        out = None
        for s in range(N_FF_CHUNK):
            t = mlp_chunk(wup[s], wdown[s], None, h)[0]
            out = t if out is None else out + t
        z = ALPHA * h + out
        return (_layer_norm(z, lg, lb), z), ()

    (h_out, z), _ = rowwise(name, fn, [h], [_ff_all(wup, layer), _ff_all(wdown, layer), lg, lb],
                            [(D_MODEL, F32), (D_MODEL, F32)], [], tm)
    return h_out, z


def _mlp_layer_bwd(name, h_in, z, dh_parts, wup, wdown, layer, lg, lb, tm):
    n_parts = len(dh_parts)

    def fn_ln(c, i, z, *rest):
        dh = rest[0]
        for extra in rest[1:n_parts]:
            dh = dh + extra
        _, vjp = jax.vjp(_layer_norm, z, rest[n_parts], rest[n_parts + 1])
        dz, dlg, dlb = vjp(dh)
        return (dz,), (dlg, dlb)

    (dz,), (dlg, dlb) = rowwise(name + "_ln", fn_ln, [z] + list(dh_parts), [lg, lb], [(D_MODEL, F32)],
                                [_acc((1, D_MODEL)), _acc((1, D_MODEL))], tm)

    def fn_mlp(c, i, h, dz, wup, wdown):
        tile = h.shape[0]
        (dx,), dws = vjp_taps(functools.partial(mlp_chunk, wup, wdown), [(tile, D_MODEL)] * 2, [h], dz)
        return (dx,), dws

    aspec = ((N_FF_CHUNK, D_MODEL, D_MODEL), F32, (None, D_MODEL, D_MODEL), lambda c, i: (c, 0, 0))
    (dx,), (dwup, dwdown) = rowwise(name + "_mm", fn_mlp, [h_in, dz], [_ff_one(wup, layer), _ff_one(wdown, layer)],
                                    [(D_MODEL, F32, True)], [aspec, aspec], tm, nc=N_FF_CHUNK)
    return dz, dx, dwup, dwdown, dlg, dlb


def _sum_parts(dz, dx):
    out = ALPHA * dz
    for s in range(N_FF_CHUNK):
        out = out + dx[s]
    return out


def local_step(x, loss_target, p):
    seq = x.shape[0]
    lp = TOK0 + seq
    tm = _pick_tm(lp, 384)
    tms = _pick_tm(lp, 128)
    e, et = _head_matrices()
    h0 = jnp.concatenate([jnp.zeros((PAD_FRONT, D_MODEL), F32), p["meta_tokens"], x], axis=0)
    hp = jnp.concatenate([jnp.zeros((1, D_MODEL), F32), h0[:-1]], axis=0)
    tgt = jnp.concatenate([jnp.zeros((TOK0, D_MODEL), F32), loss_target], axis=0)
    pos = jnp.maximum(jnp.arange(lp, dtype=F32) - PAD_FRONT, 0.0)
    inv_freq = 1.0 / (ROPE_THETA ** (jnp.arange(0, HEAD_DIM, 2, dtype=F32) / HEAD_DIM))
    ang = pos[:, None] * inv_freq[None, :]
    cos = jnp.tile(jnp.cos(ang), (1, PAIR // (HEAD_DIM // 2)))
    sin = jnp.tile(jnp.sin(ang), (1, PAIR // (HEAD_DIM // 2)))

    pre_vec = [p["a_mu"][j:j + 1] for j in range(6)] + [p["a_w0"], p["a_a0"], p["a_k_k"], p["a_k_a"]]
    pre_w = [p["a_w_r"], p["a_w_k"], p["a_w_v"], p["a_w1"], p["a_w2"], p["a_a1"], p["a_a2"], p["a_g1"], p["a_g2"]]
    n_vec = len(pre_vec)

    def fn_pre(c, i, h, hp, e, et, *ws):
        return rwkv_pre(e, et, ws[n_vec:], None, h, hp, *ws[:n_vec])[0], ()

    (r, lw, k2, v, an, bn, g), _ = rowwise("rwkv_pre", fn_pre, [h0, hp], [e, et] + pre_vec + pre_w,
                                           [(D_MODEL, F32)] * 7, [], tms)
    y, *scan_saved = scan_fwd(r, lw, k2, v, an, bn)

    post_c = [p["a_w_o"], p["a_gn_w"], p["a_gn_b"], p["a_r_k"], p["ln_g00"], p["ln_b00"]]

    def fn_post(c, i, y, r, k2, v, g, h0, e, et, w_o, *vecs):
        return (rwkv_post(e, et, w_o, None, y, r, k2, v, g, h0, *vecs)[0],), ()

    (h1,), _ = rowwise("rwkv_post", fn_post, [y, r, k2, v, g, h0], [e, et] + post_c, [(D_MODEL, F32)], [], tm)
    h2, z2 = _mlp_layer_fwd("mlp0_fwd", h1, p["mlp_up"], p["mlp_down"], 0, p["ln_g01"], p["ln_b01"], tm)

    qkv_w = [p["b_w_q"], p["kv_w_k"], p["kv_w_v"]]

    def fn_qkv(c, i, h, cos, sin, wq, wk, wv):
        return qkv_proj(cos, sin, wq, wk, wv, None, h)[0], ()

    (q, k, vv), _ = rowwise("qkv_proj", fn_qkv, [h2, cos, sin], qkv_w,
                            [(D_MODEL, F32), (KV_DIM, F32), (KV_DIM, F32)], [], tm)
    sinks_b = jnp.broadcast_to(p["b_sinks"].reshape(N_HEADS_KV, GROUP, 1), (N_HEADS_KV, GROUP, PAIR))
    sinks_b = jnp.concatenate([sinks_b, jnp.zeros((N_HEADS_KV, 8 - GROUP, PAIR), F32)], axis=1)
    o = attn_fwd(q, k, vv, sinks_b)

    ao_c = [p["b_w_o"], p["ln_g10"], p["ln_b10"]]

    def fn_ao(c, i, o, h, w_o, lg, lb):
        return (attn_out(w_o, None, o, h, lg, lb)[0],), ()

    (h3,), _ = rowwise("attn_out", fn_ao, [o, h2], ao_c, [(D_MODEL, F32)], [], tm)
    h4, z4 = _mlp_layer_fwd("mlp1_fwd", h3, p["mlp_up"], p["mlp_down"], 1, p["ln_g11"], p["ln_b11"], tm)

    def fn_loss(c, i, h4, tgt):
        real = (_row_ids(i, tm) >= TOK0).astype(F32)
        err = (h4 - tgt) * real
        part = 0.5 * jnp.sum(jnp.sum(err * err, axis=-1, keepdims=True), axis=0, keepdims=True) / D_MODEL
        return (err * (1.0 / D_MODEL),), (jnp.broadcast_to(part, (8, PAIR)),)

    (dh4,), (loss_acc,) = rowwise("loss", fn_loss, [h4, tgt], [], [(D_MODEL, F32)], [_acc((8, PAIR))], tm)
    loss = loss_acc[0, 0]

    grads = {}
    dz4, dx4, grads["mlp_up1"], grads["mlp_down1"], grads["ln_g11"], grads["ln_b11"] = _mlp_layer_bwd(
        "mlp1_bwd", h3, z4, [dh4], p["mlp_up"], p["mlp_down"], 1, p["ln_g11"], p["ln_b11"], tm)

    def fn_ao_b(c, i, dz, dx, o, h, w_o, lg, lb):
        (do, dh, dlg, dlb), (dw_o,) = vjp_taps(functools.partial(attn_out, w_o), [(tms, D_MODEL)], [o, h, lg, lb],
                                               _sum_parts(dz, dx))
        return (do, dh), (dw_o, dlg, dlb)

    (do, dh2_a), (grads["b_w_o"], grads["ln_g10"], grads["ln_b10"]) = rowwise(
        "attn_out_bwd", fn_ao_b, [dz4, dx4, o, h2], ao_c, [(D_MODEL, F32)] * 2,
        [_acc((D_MODEL, D_MODEL)), _acc((1, D_MODEL)), _acc((1, D_MODEL))], tms)

    dq, dkc, dkp, dvc, dvp, dsinks = attn_bwd(q, k, vv, sinks_b, do)
    grads["b_sinks"] = dsinks[:, :GROUP, 0].reshape(1, N_HEADS)
    zblk = jnp.zeros((BLOCK, KV_DIM), F32)
    dkp_s = jnp.concatenate([dkp[BLOCK:], zblk], axis=0)
    dvp_s = jnp.concatenate([dvp[BLOCK:], zblk], axis=0)

    def fn_qkv_b(c, i, h, cos, sin, dq, dkc, dkp, dvc, dvp, wq, wk, wv):
        return vjp_taps(functools.partial(qkv_proj, cos, sin, wq, wk, wv),
                        [(tms, D_MODEL), (tms, KV_DIM), (tms, KV_DIM)], [h], (dq, dkc + dkp, dvc + dvp))

    (dh2_q,), (grads["b_w_q"], grads["kv_w_k"], grads["kv_w_v"]) = rowwise(
        "qkv_proj_bwd", fn_qkv_b, [h2, cos, sin, dq, dkc, dkp_s, dvc, dvp_s], qkv_w, [(D_MODEL, F32)],
        [_acc((D_MODEL, D_MODEL)), _acc((D_MODEL, KV_DIM)), _acc((D_MODEL, KV_DIM))], tms)

    dz2, dx2, grads["mlp_up0"], grads["mlp_down0"], grads["ln_g01"], grads["ln_b01"] = _mlp_layer_bwd(
        "mlp0_bwd", h1, z2, [dh2_a, dh2_q], p["mlp_up"], p["mlp_down"], 0, p["ln_g01"], p["ln_b01"], tm)

    def fn_post_b(c, i, dz, dx, y, r, k2, v, g, h0, e, et, w_o, *vecs):
        out, dws = vjp_taps(functools.partial(rwkv_post, e, et, w_o), [(tms, D_MODEL)],
                            [y, r, k2, v, g, h0] + list(vecs), _sum_parts(dz, dx))
        return out[:6], tuple(dws) + tuple(out[6:])

    (dy, dr_c, dk_c, dv_c, dg, dh0_c), post_g = rowwise(
        "rwkv_post_bwd", fn_post_b, [dz2, dx2, y, r, k2, v, g, h0], [e, et] + post_c, [(D_MODEL, F32)] * 6,
        [_acc((D_MODEL, D_MODEL))] + [_acc((1, D_MODEL))] * 5, tms)
    for name, val in zip(["a_w_o", "a_gn_w", "a_gn_b", "a_r_k", "ln_g00", "ln_b00"], post_g):
        grads[name] = val

    dr_s, dlw, dk_s, dv_s, dan, dbn = scan_bwd(r, lw, k2, v, an, bn, scan_saved, dy)

    def fn_pre_b(c, i, h, hp, dr_c, dr_s, dlw, dk_c, dk_s, dv_c, dv_s, dan, dbn, dg, e, et, *ws):
        real = (_row_ids(i, tms) >= PAD_FRONT).astype(F32)
        cot = tuple(t * real for t in (dr_c + dr_s, dlw, dk_c + dk_s, dv_c + dv_s, dan, dbn, dg))
        out, dws = vjp_taps(functools.partial(rwkv_pre, e, et, ws[n_vec:]), [(tms, n) for n in PRE_TAPS],
                            [h, hp] + list(ws[:n_vec]), cot)
        return out[:2], tuple(out[2:]) + tuple(dws)

    (dh0_p, dhp), pre_g = rowwise(
        "rwkv_pre_bwd", fn_pre_b, [h0, hp, dr_c, dr_s, dlw, dk_c, dk_s, dv_c, dv_s, dan, dbn, dg],
        [e, et] + pre_vec + pre_w, [(D_MODEL, F32)] * 2,
        [_acc((1, D_MODEL))] * n_vec + [_acc(w.shape) for w in pre_w], tms)
    grads["a_mu"] = jnp.concatenate(pre_g[:6], axis=0)
    for name, val in zip(["a_w0", "a_a0", "a_k_k", "a_k_a", "a_w_r", "a_w_k", "a_w_v", "a_w1", "a_w2", "a_a1",
                          "a_a2", "a_g1", "a_g2"], pre_g[6:]):
        grads[name] = val

    dhp_s = jnp.concatenate([dhp[1:], jnp.zeros((1, D_MODEL), F32)], axis=0)

    def fn_add(c, i, a, b, d):
        return (a + b + d,), ()

    (dh0,), _ = rowwise("grad_h0", fn_add, [dh0_c, dh0_p, dhp_s], [], [(D_MODEL, F32)], [], tm)
    grads["meta_tokens"] = dh0[PAD_FRONT:TOK0]
    return loss, dh0[TOK0:], grads


ANY = pl.BlockSpec(memory_space=pl.ANY)
XY_FLIPS = ((0, 1), (1, 0), (1, 1))
ALL_FLIPS = tuple((e >> 2 & 1, e >> 1 & 1, e & 1) for e in range(1, N_DEV))


def _flip(v, bit):
    return 1 - v if bit else v


def all_gather_shards(shards):
    n = len(shards)
    npeer = len(XY_FLIPS)

    def body(*refs):
        src, dst = refs[:n], refs[n:2 * n]
        send_sems, recv_sems, local_sems = refs[2 * n:]
        x, y, c = lax.axis_index("x"), lax.axis_index("y"), lax.axis_index("c")

        def copy(k, j, slot):
            fx, fy = XY_FLIPS[j]
            return pltpu.make_async_remote_copy(
                src_ref=src[k], dst_ref=dst[k].at[slot], send_sem=send_sems.at[k * npeer + j],
                recv_sem=recv_sems.at[k * npeer + j], device_id=(_flip(x, fx), _flip(y, fy), c), device_id_type=MESH)

        mine = [pltpu.make_async_copy(src[k], dst[k].at[2 * x + y], local_sems.at[k]) for k in range(n)]
        sends = [copy(k, j, 2 * x + y) for k in range(n) for j in range(npeer)]
        for cp in mine + sends:
            cp.start()
        for k in range(n):
            for j, (fx, fy) in enumerate(XY_FLIPS):
                copy(k, j, 2 * _flip(x, fx) + _flip(y, fy)).wait_recv()
        for cp in sends:
            cp.wait_send()
        for cp in mine:
            cp.wait()

    return pl.pallas_call(
        body, name="gather_weights", in_specs=[ANY] * n, out_specs=[ANY] * n,
        out_shape=[jax.ShapeDtypeStruct((N_SHARD,) + s.shape, s.dtype) for s in shards],
        scratch_shapes=[pltpu.SemaphoreType.DMA((n * npeer,)), pltpu.SemaphoreType.DMA((n * npeer,)),
                        pltpu.SemaphoreType.DMA((n,))],
    )(*shards)


def _sem_scratch(n):
    return [pltpu.SemaphoreType.DMA((n,)), pltpu.SemaphoreType.DMA((n,))]


def placement():
    x, y, c = lax.axis_index("x"), lax.axis_index("y"), lax.axis_index("c")
    me = 2 * x + y
    others = [j + (j >= me).astype(jnp.int32) for j in range(N_SHARD - 1)]
    return jnp.stack([c, me] + others).astype(jnp.int32)


def pair_exchange(sources):
    n = len(sources)

    def body(*refs):
        src, got = refs[:n], refs[n:2 * n]
        send_sems, recv_sems = refs[2 * n:]
        x, y, c = lax.axis_index("x"), lax.axis_index("y"), lax.axis_index("c")

        def copy(k):
            half = sources[k].shape[1] // 2
            theirs = src[k].at[:, pl.ds(pl.multiple_of((1 - c) * half, 8), half), :]
            return pltpu.make_async_remote_copy(
                src_ref=theirs, dst_ref=got[k], send_sem=send_sems.at[k], recv_sem=recv_sems.at[k],
                device_id=(x, y, 1 - c), device_id_type=MESH)

        sends = [copy(k) for k in range(n)]
        for cp in sends:
            cp.start()
        for cp in sends:
            cp.wait_recv()
        for cp in sends:
            cp.wait_send()

    halves = [jax.ShapeDtypeStruct((s.shape[0], s.shape[1] // 2, s.shape[2]), s.dtype) for s in sources]
    return pl.pallas_call(body, name="grads_pair_exchange", in_specs=[ANY] * n, out_specs=[ANY] * n,
                          out_shape=halves, scratch_shapes=_sem_scratch(n))(*sources)


def chip_exchange(parts):
    n = len(parts)
    npeer = len(XY_FLIPS)

    def body(*refs):
        src, dst = refs[:n], refs[n:2 * n]
        send_sems, recv_sems = refs[2 * n:]
        x, y, c = lax.axis_index("x"), lax.axis_index("y"), lax.axis_index("c")
        me = 2 * x + y

        def copy(k, j, sending):
            fx, fy = XY_FLIPS[j]
            px, py = _flip(x, fx), _flip(y, fy)
            peer = 2 * px + py
            return pltpu.make_async_remote_copy(
                src_ref=src[k].at[peer], dst_ref=dst[k].at[me if sending else peer],
                send_sem=send_sems.at[k * npeer + j], recv_sem=recv_sems.at[k * npeer + j],
                device_id=(px, py, c), device_id_type=MESH)

        sends = [copy(k, j, True) for k in range(n) for j in range(npeer)]
        for cp in sends:
            cp.start()
        for k in range(n):
            for j in range(npeer):
                copy(k, j, False).wait_recv()
        for cp in sends:
            cp.wait_send()

    return pl.pallas_call(
        body, name="grads_chip_exchange", in_specs=[ANY] * n, out_specs=[ANY] * n,
        out_shape=[jax.ShapeDtypeStruct(p.shape, p.dtype) for p in parts], scratch_shapes=_sem_scratch(n * npeer),
    )(*parts)


def sibling_share(halves):
    n = len(halves)

    def body(*refs):
        src, got = refs[:n], refs[n:2 * n]
        send_sems, recv_sems = refs[2 * n:]
        x, y, c = lax.axis_index("x"), lax.axis_index("y"), lax.axis_index("c")
        sends = [pltpu.make_async_remote_copy(
            src_ref=src[k], dst_ref=got[k], send_sem=send_sems.at[k], recv_sem=recv_sems.at[k],
            device_id=(x, y, 1 - c), device_id_type=MESH) for k in range(n)]
        for cp in sends:
            cp.start()
        for cp in sends:
            cp.wait_recv()
        for cp in sends:
            cp.wait_send()

    return pl.pallas_call(
        body, name="grads_sibling_share", in_specs=[ANY] * n, out_specs=[ANY] * n,
        out_shape=[jax.ShapeDtypeStruct(h.shape, h.dtype) for h in halves], scratch_shapes=_sem_scratch(n),
    )(*halves)


ADD_TILE_ELEMS = 512 * 1024


def _row_tile(rows, cols):
    return max(t for t in range(8, rows + 1, 8) if rows % t == 0 and t * cols <= ADD_TILE_ELEMS)


def _prefetch_call(body, name, place, grid, in_specs, out_specs, out_shape, args):
    return pl.pallas_call(
        body, name=name, out_shape=out_shape,
        grid_spec=pltpu.PrefetchScalarGridSpec(num_scalar_prefetch=1, grid=grid, in_specs=in_specs,
                                               out_specs=out_specs),
        compiler_params=pltpu.CompilerParams(dimension_semantics=("arbitrary",) * len(grid),
                                             vmem_limit_bytes=VMEM_LIMIT),
    )(place, *args)


def pair_add(name, place, src, got, dtype):
    n4, half, cols = got.shape
    tile = _row_tile(half, cols)
    nt = half // tile

    def body(pr, a_ref, b_ref, o_ref):
        o_ref[...] = (a_ref[...] + b_ref[...]).astype(o_ref.dtype)

    mine = pl.BlockSpec((None, tile, cols), lambda s, i, pr: (s, pr[0] * nt + i, 0))
    blk = pl.BlockSpec((None, tile, cols), lambda s, i, pr: (s, i, 0))
    return _prefetch_call(body, name, place, (n4, nt), [mine, blk], blk,
                          jax.ShapeDtypeStruct(got.shape, dtype), (src, got))


def chip_add(name, place, part, from_chips):
    _, half, cols = part.shape
    tile = _row_tile(half, cols)

    def body(pr, own_ref, r0_ref, r1_ref, r2_ref, o_ref):
        me = pr[1]
        own, r0, r1, r2 = (r[...].astype(F32) for r in (own_ref, r0_ref, r1_ref, r2_ref))
        t0 = jnp.where(me == 0, own, r0)
        t1 = jnp.where(me == 0, r0, jnp.where(me == 1, own, r1))
        t2 = jnp.where(me <= 1, r1, jnp.where(me == 2, own, r2))
        t3 = jnp.where(me == 3, own, r2)
        o_ref[...] = ((t0 + t1) + t2) + t3

    def slab(j):
        return pl.BlockSpec((None, tile, cols), lambda i, pr: (pr[j], i, 0))

    return _prefetch_call(body, name, place, (half // tile,), [slab(1), slab(2), slab(3), slab(4)],
                          pl.BlockSpec((tile, cols), lambda i, pr: (i, 0)),
                          jax.ShapeDtypeStruct((half, cols), F32), (part, from_chips, from_chips, from_chips))


def reduce_grads(sources, narrow):
    place = placement()
    got = pair_exchange(sources)
    parts = [pair_add(f"grads_pair_add{k}", place, s, g, BF16 if nar else F32)
             for k, (s, g, nar) in enumerate(zip(sources, got, narrow))]
    from_chips = chip_exchange(parts)
    halves = [chip_add(f"grads_chip_add{k}", place, p, f) for k, (p, f) in enumerate(zip(parts, from_chips))]
    return place, halves, sibling_share(halves)


ADAM_ROWS = 256


def adamw_update(name, place, halves, w, m, v):
    nsub, rows, cols = w.shape
    half = rows // 2
    tr = ADAM_ROWS if half % ADAM_ROWS == 0 else half
    nth = half // tr

    def body(pr, *refs):
        g_refs, (w_ref, m_ref, v_ref, g_ref, d_ref, nm_ref, nv_ref) = refs[:2 * nsub], refs[2 * nsub:]
        l = pl.program_id(0)
        mine = (pl.program_id(1) // nth) == pr[0]
        g = None
        for s in range(nsub):
            gs = jnp.where(mine, g_refs[2 * s][...], g_refs[2 * s + 1][...])
            g = gs if g is None else jnp.where(l == s, gs, g)
        m2 = ADAM_B1 * m_ref[...] + (1.0 - ADAM_B1) * g
        v2 = ADAM_B2 * v_ref[...] + (1.0 - ADAM_B2) * (g * g)
        m_hat = m2 / (1.0 - ADAM_B1 ** ADAM_STEP)
        v_hat = v2 / (1.0 - ADAM_B2 ** ADAM_STEP)
        g_ref[...] = g
        d_ref[...] = -ADAM_LR * (m_hat / (jnp.sqrt(v_hat) + ADAM_EPS) + ADAM_WD * w_ref[...])
        nm_ref[...] = m2
        nv_ref[...] = v2

    gblk = pl.BlockSpec((tr, cols), lambda l, i, pr: (i % nth, 0))
    blk = pl.BlockSpec((None, tr, cols), lambda l, i, pr: (l, i, 0))
    out = jax.ShapeDtypeStruct((nsub, rows, cols), F32)
    return _prefetch_call(body, name, place, (nsub, rows // tr), [gblk] * (2 * nsub) + [blk] * 3, [blk] * 4,
                          [out] * 4, [h for pair in halves for h in pair] + [w, m, v])


WEIGHT_NAMES = ("meta_tokens", "a_mu", "a_w_r", "a_w_k", "a_w_v", "a_w_o", "a_w0", "a_w1", "a_w2", "a_a0", "a_a1",
                "a_a2", "a_g1", "a_g2", "a_k_k", "a_k_a", "a_r_k", "a_gn_w", "a_gn_b", "kv_w_k", "kv_w_v", "b_w_q",
                "b_sinks", "b_w_o", "mlp_w_up", "mlp_w_down", "ln_g", "ln_b")
BIG_NAMES = ("a_w_r", "a_w_k", "a_w_v", "a_w_o", "b_w_q", "b_w_o")
PACK_MATS = (("kv_w_k", 256), ("kv_w_v", 256), ("a_w1", 64), ("a_a1", 64), ("a_g1", 128), ("a_w2", 64),
             ("a_a2", 64), ("a_g2", 128))
COLUMN_CUT = ("a_w2", "a_a2", "a_g2")
PACK_VECS = (("a_mu", 6), ("a_w0", 1), ("a_a0", 1), ("a_k_k", 1), ("a_k_a", 1), ("a_gn_w", 1), ("a_gn_b", 1),
             ("ln_g", 4), ("ln_b", 4), ("meta_tokens", 16))
PACK_REPL = (("a_r_k", 4), ("b_sinks", 1))
SHARD_W = D_MODEL // N_SHARD
N_MAT_ROWS = sum(r for _, r in PACK_MATS)
N_VEC_ROWS = sum(r for _, r in PACK_VECS)
N_PACK_ROWS = -(-(N_MAT_ROWS + N_VEC_ROWS + sum(r for _, r in PACK_REPL)) // 8) * 8
N_GATHER_VEC_ROWS = -(-N_VEC_ROWS // 8) * 8


def _pack_rows(arr):
    if arr.size == N_HEADS:
        return jnp.pad(arr.reshape(1, N_HEADS), ((0, 0), (0, SHARD_W - N_HEADS)))
    return arr.reshape(-1, SHARD_W)


def pack_small(get):
    parts = [_pack_rows(get(name)) for name, _ in PACK_MATS + PACK_VECS + PACK_REPL]
    used = sum(p.shape[0] for p in parts)
    return jnp.concatenate(parts + [jnp.zeros((N_PACK_ROWS - used, SHARD_W), F32)], axis=0)


def unpack_small(pack, shapes):
    out, off = {}, 0
    for name, rows in PACK_MATS + PACK_VECS + PACK_REPL:
        piece = pack[off:off + rows]
        off += rows
        out[name] = piece[:, :N_HEADS].reshape(shapes[name]) if name == "b_sinks" else piece.reshape(shapes[name])
    return out


def whole_weights(gathered_big, mats, vecs, a_r_k, b_sinks):
    p = {name: g.reshape(D_MODEL, D_MODEL) for name, g in zip(BIG_NAMES, gathered_big)}
    off = 0
    for name, rows in PACK_MATS:
        piece = mats[:, off:off + rows]
        off += rows
        if name in COLUMN_CUT:
            p[name] = piece.transpose(1, 0, 2).reshape(rows, D_MODEL)
        else:
            p[name] = piece.reshape(D_MODEL, rows)
    v = vecs.transpose(1, 0, 2).reshape(-1, D_MODEL)
    off = 0
    for name, rows in PACK_VECS:
        p[name] = v[off:off + rows]
        off += rows
    for i in range(2):
        for j in range(2):
            p[f"ln_g{i}{j}"] = p["ln_g"][2 * i + j:2 * i + j + 1]
            p[f"ln_b{i}{j}"] = p["ln_b"][2 * i + j:2 * i + j + 1]
    p["a_r_k"] = a_r_k.reshape(1, D_MODEL)
    p["b_sinks"] = b_sinks
    return p


def small_grad_pack(g):
    parts = []
    for name, rows in PACK_MATS:
        if name in COLUMN_CUT:
            parts.append(g[name].reshape(rows, N_SHARD, SHARD_W).transpose(1, 0, 2))
        else:
            parts.append(g[name].reshape(N_SHARD, rows, SHARD_W))
    vec_rows = [g["a_mu"]] + [g[n] for n in ("a_w0", "a_a0", "a_k_k", "a_k_a", "a_gn_w", "a_gn_b")]
    vec_rows += [g[f"ln_g{i}{j}"] for i in range(2) for j in range(2)]
    vec_rows += [g[f"ln_b{i}{j}"] for i in range(2) for j in range(2)] + [g["meta_tokens"]]
    parts.append(jnp.concatenate(vec_rows, axis=0).reshape(N_VEC_ROWS, N_SHARD, SHARD_W).transpose(1, 0, 2))
    parts.append(jnp.broadcast_to(g["a_r_k"].reshape(1, -1, SHARD_W), (N_SHARD, D_MODEL // SHARD_W, SHARD_W)))
    sinks = jnp.pad(g["b_sinks"].reshape(1, 1, N_HEADS), ((0, 0), (0, 0), (0, SHARD_W - N_HEADS)))
    parts.append(jnp.broadcast_to(sinks, (N_SHARD, 1, SHARD_W)))
    used = sum(p.shape[1] for p in parts)
    parts.append(jnp.zeros((N_SHARD, N_PACK_ROWS - used, SHARD_W), F32))
    return jnp.concatenate(parts, axis=1)


def train_step(vals):
    w = {n: vals[n] for n in WEIGHT_NAMES}
    w_pack = pack_small(lambda n: w[n])
    shards = [w[n][0].astype(BF16) for n in BIG_NAMES]
    shards += [w["mlp_w_up"].astype(BF16), w["mlp_w_down"].astype(BF16), w_pack[:N_MAT_ROWS].astype(BF16),
               w_pack[N_MAT_ROWS:N_MAT_ROWS + N_GATHER_VEC_ROWS]]
    gathered = all_gather_shards(shards)
    nb = len(BIG_NAMES)
    p = whole_weights(gathered[:nb], gathered[nb + 2], gathered[nb + 3][:, :N_VEC_ROWS], w["a_r_k"], w["b_sinks"])
    p["mlp_up"], p["mlp_down"] = gathered[nb], gathered[nb + 1]

    loss, gx, g = local_step(vals["x"][0], vals["loss_target"][0], p)
    loss = lax.psum(loss, ("x", "y", "c"))

    sources = [g[n].reshape(N_SHARD, SHARD_W, D_MODEL) for n in BIG_NAMES]
    sources += [g["mlp_up0"], g["mlp_up1"], g["mlp_down0"], g["mlp_down1"], small_grad_pack(g)]
    place, mine, theirs = reduce_grads(sources, [True] * (len(sources) - 1) + [False])
    halves = list(zip(mine, theirs))

    res = {}
    for k, n in enumerate(BIG_NAMES):
        res[n] = adamw_update("adamw_" + n, place, halves[k:k + 1], w[n], vals["m_" + n], vals["v_" + n])
    for k, n in ((nb, "mlp_w_up"), (nb + 2, "mlp_w_down")):
        res[n] = adamw_update("adamw_" + n, place, halves[k:k + 2], w[n], vals["m_" + n], vals["v_" + n])
    packs = adamw_update("adamw_small", place, halves[-1:], w_pack[None], pack_small(lambda n: vals["m_" + n])[None],
                         pack_small(lambda n: vals["v_" + n])[None])
    shapes = {n: w[n].shape for n in WEIGHT_NAMES}
    small = [unpack_small(pk[0], shapes) for pk in packs]
    outs = [loss, gx[None]]
    for t in range(4):
        outs += [res[n][t] if n in res else small[t][n] for n in WEIGHT_NAMES]
    return tuple(outs)


def kernel(x, meta_tokens, a_mu, a_w_r, a_w_k, a_w_v, a_w_o, a_w0, a_w1, a_w2, a_a0, a_a1, a_a2, a_g1, a_g2, a_k_k,
           a_k_a, a_r_k, a_gn_w, a_gn_b, kv_w_k, kv_w_v, b_w_q, b_sinks, b_w_o, mlp_w_up, mlp_w_down, ln_g, ln_b,
           loss_target, m_meta_tokens, m_a_mu, m_a_w_r, m_a_w_k, m_a_w_v, m_a_w_o, m_a_w0, m_a_w1, m_a_w2, m_a_a0,
           m_a_a1, m_a_a2, m_a_g1, m_a_g2, m_a_k_k, m_a_k_a, m_a_r_k, m_a_gn_w, m_a_gn_b, m_kv_w_k, m_kv_w_v,
           m_b_w_q, m_b_sinks, m_b_w_o, m_mlp_w_up, m_mlp_w_down, m_ln_g, m_ln_b, v_meta_tokens, v_a_mu, v_a_w_r,
           v_a_w_k, v_a_w_v, v_a_w_o, v_a_w0, v_a_w1, v_a_w2, v_a_a0, v_a_a1, v_a_a2, v_a_g1, v_a_g2, v_a_k_k,
           v_a_k_a, v_a_r_k, v_a_gn_w, v_a_gn_b, v_kv_w_k, v_kv_w_v, v_b_w_q, v_b_sinks, v_b_w_o, v_mlp_w_up,
           v_mlp_w_down, v_ln_g, v_ln_b):
    return train_step(dict(locals()))
```

```python
import functools

import numpy as np
import jax
import jax.numpy as jnp
from jax import lax
from jax.experimental import pallas as pl
from jax.experimental.pallas import tpu as pltpu

F32 = jnp.float32
BF16 = jnp.bfloat16

D_MODEL = 1024
N_HEADS = 16
HEAD_DIM = 64
N_HEADS_KV = 4
GROUP = 4
KV_DIM = N_HEADS_KV * HEAD_DIM
N_META = 16
BLOCK = 128
PAD_FRONT = BLOCK - N_META
TOK0 = PAD_FRONT + N_META
N_FF_CHUNK = 4
N_SHARD = 4
N_DEV = 8
GN_EPS = 64e-5
LN_EPS = 1e-5
ROPE_THETA = 10000.0
ALPHA = 4.0 ** 0.25
ADAM_LR, ADAM_B1, ADAM_B2, ADAM_EPS, ADAM_WD, ADAM_STEP = 0.001, 0.9, 0.999, 1e-08, 0.01, 10
SCAN_T = 64
PAIR = 128
KVW = GROUP * HEAD_DIM
VMEM_LIMIT = 56 * 1024 * 1024
HI = lax.Precision.HIGHEST
MESH = pl.DeviceIdType.MESH


def _dot(a, b, ca, cb):
    return lax.dot_general(a.astype(BF16), b.astype(BF16), (((ca,), (cb,)), ((), ())),
                           preferred_element_type=F32)


@jax.custom_vjp
def mm(a, b):
    return _dot(a, b, 1, 0)


def _mm_fwd(a, b):
    return mm(a, b), b


def _mm_bwd(b, g):
    return _dot(g, b, 1, 1), jnp.zeros_like(b)


mm.defvjp(_mm_fwd, _mm_bwd)


def tmm(x, w, taps, xs):
    y = mm(x, w)
    if taps is not None:
        y = y + taps[len(xs)]
    xs.append(x)
    return y


def vjp_taps(core, tap_shapes, args, cot):
    taps = [jnp.zeros(s, F32) for s in tap_shapes]
    _, vjp, xs = jax.vjp(core, taps, *args, has_aux=True)
    out = vjp(cot)
    return out[1:], [_dot(x, g, 0, 0) for x, g in zip(xs, out[0])]


def _split3(x):
    x1 = x.astype(BF16)
    r1 = x - x1.astype(F32)
    x2 = r1.astype(BF16)
    x3 = (r1 - x2.astype(F32)).astype(BF16)
    return x1, x2, x3


def _exact_dot(x, m01, cb=0):
    acc = None
    for piece in _split3(x):
        t = lax.dot_general(piece, m01, (((1,), (cb,)), ((), ())), preferred_element_type=F32)
        acc = t if acc is None else acc + t
    return acc


def _head_matrices():
    e = np.zeros((D_MODEL, N_HEADS), np.float32)
    e[np.arange(D_MODEL), np.arange(D_MODEL) // HEAD_DIM] = 1.0
    return jnp.asarray(e, BF16), jnp.asarray(e.T, BF16)


@jax.custom_vjp
def hsum(x, e, et):
    return _exact_dot(x, e)


@jax.custom_vjp
def hbc(s, e, et):
    return _exact_dot(s, et)


hsum.defvjp(lambda x, e, et: (_exact_dot(x, e), (e, et)),
            lambda res, g: (hbc(g, *res), jnp.zeros_like(res[0]), jnp.zeros_like(res[1])))
hbc.defvjp(lambda s, e, et: (_exact_dot(s, et), (e, et)),
           lambda res, g: (hsum(g, *res), jnp.zeros_like(res[0]), jnp.zeros_like(res[1])))


def _sigmoid(u):
    return 0.5 * (jnp.tanh(0.5 * u) + 1.0)


def _softplus(u):
    return jnp.maximum(u, 0.0) + jnp.log(1.0 + jnp.exp(-jnp.abs(u)))


def _layer_norm(z, g, b):
    mu = jnp.mean(z, axis=-1, keepdims=True)
    zc = z - mu
    var = jnp.mean(zc * zc, axis=-1, keepdims=True)
    return zc * lax.rsqrt(var + LN_EPS) * g + b


def _zero_map(nd):
    return lambda c, i: (0,) * nd


def _params():
    return pltpu.CompilerParams(dimension_semantics=("arbitrary", "arbitrary"), vmem_limit_bytes=VMEM_LIMIT)


def rowwise(name, fn, rows, consts, out_rows, out_accs, tm, nc=1):
    lp = rows[0].shape[-2]
    nt = lp // tm
    assert nt * tm == lp, (name, lp, tm)
    in_specs, args = [], []
    for a in rows:
        if a.ndim == 2:
            in_specs.append(pl.BlockSpec((tm, a.shape[1]), lambda c, i: (i, 0)))
        else:
            in_specs.append(pl.BlockSpec((a.shape[0], tm, a.shape[2]), lambda c, i: (0, i, 0)))
        args.append(a)
    for cst in consts:
        if isinstance(cst, tuple):
            arr, bs, im = cst
            in_specs.append(pl.BlockSpec(bs, im))
        else:
            arr = cst
            in_specs.append(pl.BlockSpec(arr.shape, _zero_map(arr.ndim), pipeline_mode=pl.Buffered(1)))
        args.append(arr)
    out_shape, out_specs, acc_per_chunk = [], [], []
    for spec in out_rows:
        if len(spec) == 3 and spec[2]:
            out_shape.append(jax.ShapeDtypeStruct((nc, lp, spec[0]), spec[1]))
            out_specs.append(pl.BlockSpec((None, tm, spec[0]), lambda c, i: (c, i, 0)))
        else:
            out_shape.append(jax.ShapeDtypeStruct((lp, spec[0]), spec[1]))
            out_specs.append(pl.BlockSpec((tm, spec[0]), lambda c, i: (i, 0)))
    for spec in out_accs:
        out_shape.append(jax.ShapeDtypeStruct(spec[0], spec[1]))
        if len(spec) == 4:
            out_specs.append(pl.BlockSpec(spec[2], spec[3]))
            acc_per_chunk.append(True)
        else:
            out_specs.append(pl.BlockSpec(spec[0], _zero_map(len(spec[0])), pipeline_mode=pl.Buffered(1)))
            acc_per_chunk.append(False)
    n_in, n_or = len(args), len(out_rows)

    def body(*refs):
        c = pl.program_id(0)
        i = pl.program_id(1)
        vals = [r[...] for r in refs[:n_in]]
        outs_r, outs_a = fn(c, i, *vals)
        for ref, val in zip(refs[n_in:n_in + n_or], outs_r):
            ref[...] = val.astype(ref.dtype)
        for ref, val, per_chunk in zip(refs[n_in + n_or:], outs_a, acc_per_chunk):
            first = (i == 0) if per_chunk else jnp.logical_and(i == 0, c == 0)

            @pl.when(first)
            def _():
                ref[...] = val.astype(ref.dtype)

            @pl.when(jnp.logical_not(first))
            def _():
                ref[...] += val.astype(ref.dtype)

    outs = pl.pallas_call(body, name=name, grid=(nc, nt), in_specs=in_specs, out_specs=out_specs,
                          out_shape=out_shape, compiler_params=_params())(*args)
    return outs[:n_or], outs[n_or:]


def _row_ids(i, tm):
    return i * tm + lax.broadcasted_iota(jnp.int32, (tm, 1), 0)


PRE_TAPS = (D_MODEL, D_MODEL, D_MODEL, 64, D_MODEL, 64, D_MODEL, 128, D_MODEL)


def rwkv_pre(e, et, ws, taps, h, hp, mu_r, mu_w, mu_k, mu_v, mu_a, mu_g, w0, a0, k_k, k_a):
    w_r, w_k, w_v, w1, w2, a1, a2, g1, g2 = ws
    xs = []
    xx = hp - h
    r = tmm(h + xx * mu_r, w_r, taps, xs)
    k = tmm(h + xx * mu_k, w_k, taps, xs)
    v = tmm(h + xx * mu_v, w_v, taps, xs)
    wraw = -_softplus(-(w0 + tmm(jnp.tanh(tmm(h + xx * mu_w, w1, taps, xs)), w2, taps, xs))) - 0.5
    lw = -jnp.exp(wraw)
    a = _sigmoid(a0 + tmm(tmm(h + xx * mu_a, a1, taps, xs), a2, taps, xs))
    g = tmm(_sigmoid(tmm(h + xx * mu_g, g1, taps, xs)), g2, taps, xs)
    kk = k * k_k
    ss = hsum(kk * kk, e, et)
    pos = ss > 0.0
    nrm = jnp.where(pos, jnp.sqrt(jnp.where(pos, ss, 1.0)), 0.0)
    kk = kk * hbc(1.0 / jnp.maximum(nrm, 1e-12), e, et)
    k2 = k * (1.0 + (a - 1.0) * k_a)
    return (r, lw, k2, v, -kk, kk * a, g), xs


def rwkv_post(e, et, w_o, taps, y, r, k2, v, g, h0, gn_w, gn_b, rk, lg, lb):
    xs = []
    inv_n = 1.0 / HEAD_DIM
    yc = y - hbc(hsum(y, e, et) * inv_n, e, et)
    yv = hsum(yc * yc, e, et) * inv_n
    yn = yc * hbc(lax.rsqrt(yv + GN_EPS), e, et) * gn_w + gn_b
    bonus = hbc(hsum(r * k2 * rk, e, et), e, et) * v
    mix = tmm((yn + bonus) * g, w_o, taps, xs)
    return _layer_norm(ALPHA * h0 + mix, lg, lb), xs


def mlp_chunk(wup, wdown, taps, h):
    xs = []
    u = jnp.maximum(tmm(h, wup, taps, xs), 0.0)
    return tmm(u * u, wdown, taps, xs), xs


def _rot_half(t):
    n = t.shape[-1]
    lane = lax.broadcasted_iota(jnp.int32, t.shape, t.ndim - 1)
    lo = (lane % HEAD_DIM) < (HEAD_DIM // 2)
    return jnp.where(lo, -pltpu.roll(t, n - HEAD_DIM // 2, t.ndim - 1), pltpu.roll(t, HEAD_DIM // 2, t.ndim - 1))


@jax.custom_vjp
def rot_half(t):
    return _rot_half(t)


rot_half.defvjp(lambda t: (_rot_half(t), None), lambda _, g: (-_rot_half(g),))


def _tile_lanes(t, width):
    return jnp.concatenate([t] * (width // t.shape[-1]), axis=-1)


def qkv_proj(cos, sin, wq, wk, wv, taps, h):
    xs = []
    q = tmm(h, wq, taps, xs)
    k = tmm(h, wk, taps, xs)
    v = tmm(h, wv, taps, xs)
    cq, sq = _tile_lanes(cos, D_MODEL), _tile_lanes(sin, D_MODEL)
    ck, sk = _tile_lanes(cos, KV_DIM), _tile_lanes(sin, KV_DIM)
    return (q * cq + rot_half(q) * sq, k * ck + rot_half(k) * sk, v), xs


def attn_out(w_o, taps, o, h, lg, lb):
    xs = []
    return _layer_norm(ALPHA * h + tmm(o, w_o, taps, xs), lg, lb), xs


def _scan_consts():
    t = SCAN_T
    tri = np.tril(np.ones((t, t), np.float32))
    rows = np.arange(2 * t)
    same = (rows[:, None] // t) == (rows[None, :] // t)
    strict = same & ((rows[None, :] % t) < (rows[:, None] % t))
    incl = same & ((rows[None, :] % t) <= (rows[:, None] % t))
    lane = np.arange(PAIR)
    masks = np.zeros((8, PAIR), np.float32)
    masks[0] = (lane // HEAD_DIM) == 0
    masks[1] = (lane // HEAD_DIM) == 1
    return (jnp.asarray(tri, BF16), jnp.asarray(strict.astype(np.float32)), jnp.asarray(incl.astype(np.float32)),
            jnp.asarray(masks), jnp.asarray(np.eye(2 * t, dtype=np.float32)))


def _dot_x3(a, b, ca, cb):
    a1 = a.astype(BF16)
    a2 = (a - a1.astype(F32)).astype(BF16)
    b1 = b.astype(BF16)
    b2 = (b - b1.astype(F32)).astype(BF16)

    def d(u, v):
        return lax.dot_general(u, v, (((ca,), (cb,)), ((), ())), preferred_element_type=F32)

    return d(a1, b1) + (d(a1, b2) + d(a2, b1))


@functools.partial(jax.custom_vjp, nondiff_argnums=(2, 3))
def _dotf(a, b, ca, cb):
    return _dot_x3(a, b, ca, cb)


def _dotf_bwd(ca, cb, res, g):
    a, b = res
    if ca == 1:
        da = _dot_x3(g, b, 1, 1 - cb)
    else:
        da = _dot_x3(b, g, 1 - cb, 1)
    if cb == 0:
        db = _dot_x3(a, g, 1 - ca, 0)
    else:
        db = _dot_x3(g, a, 0, 1 - ca)
    return da, db


_dotf.defvjp(lambda a, b, ca, cb: (_dot_x3(a, b, ca, cb), (a, b)), _dotf_bwd)


def _tri_dot(tri, x, ct):
    acc = None
    for piece in _split3(x):
        t = lax.dot_general(tri, piece, (((ct,), (0,)), ((), ())), preferred_element_type=F32)
        acc = t if acc is None else acc + t
    return acc


@jax.custom_vjp
def _cumsum_rows(tri, x):
    return _tri_dot(tri, x, 1)


_cumsum_rows.defvjp(lambda tri, x: (_tri_dot(tri, x, 1), tri),
                    lambda tri, g: (jnp.zeros_like(tri), _tri_dot(tri, g, 0)))


@jax.custom_vjp
def _unstack2(x):
    t = x.shape[0] // 2
    return x[:t] + x[t:]


_unstack2.defvjp(lambda x: (_unstack2(x), None), lambda _, g: (jnp.concatenate([g, g], axis=0),))


@jax.custom_vjp
def _last_row(x):
    return x[x.shape[0] - 1:, :]


def _last_row_bwd(_, g):
    rows = lax.broadcasted_iota(jnp.int32, (SCAN_T, g.shape[1]), 0)
    return (jnp.where(rows == SCAN_T - 1, jnp.broadcast_to(g, (SCAN_T, g.shape[1])), 0.0),)


_last_row.defvjp(lambda x: (_last_row(x), None), _last_row_bwd)


@jax.custom_vjp
def _solve_saved(n, rhs, minv, u):
    return u


def _solve_saved_bwd(res, du):
    minv, u = res
    drhs = _dotf(minv, du, 0, 0)
    return _dotf(drhs, u, 1, 1), drhs, jnp.zeros_like(minv), jnp.zeros_like(u)


_solve_saved.defvjp(lambda n, rhs, minv, u: (u, (minv, u)), _solve_saved_bwd)


def scan_chunk(tri, strict, incl, m0, m1, eye, r, lw, k, v, a, b, s0, saved=None):
    lower = strict > 0
    lower_incl = incl > 0

    def stack(x):
        return jnp.concatenate([x * m0, x * m1], axis=0)

    def dots(xs, ys, ca, cb, mask=None):
        out = [_dotf(x, y, ca, cb) for x, y in zip(xs, ys)]
        return out if mask is None else [jnp.where(mask, o, 0.0) for o in out]

    cl = [_cumsum_rows(tri, x) for x in lw]
    gam = [jnp.exp(c) for c in cl]
    ginv = [jnp.exp(-c) for c in cl]
    a_s = [stack(x * jnp.exp(c - w)) for x, c, w in zip(a, cl, lw)]
    r_s = [stack(x * g) for x, g in zip(r, gam)]
    b_s = [stack(x * g) for x, g in zip(b, ginv)]
    k_s = [stack(x * g) for x, g in zip(k, ginv)]
    v_s = [stack(x) for x in v]
    n_ab = dots(a_s, b_s, 1, 1, lower)
    n_ak = dots(a_s, k_s, 1, 1, lower)
    r_ab = dots(r_s, b_s, 1, 1, lower_incl)
    r_ak = dots(r_s, k_s, 1, 1, lower_incl)
    rhs = [x + y for x, y in zip(dots(a_s, s0, 1, 1), dots(n_ak, v_s, 1, 0))]
    if saved is None:
        minv = [eye + n for n in n_ab]
        p = n_ab
        for _ in range(5):
            p = dots(p, p, 1, 0)
            minv = [m + mp for m, mp in zip(minv, dots(minv, p, 1, 0))]
        u_s = dots(minv, rhs, 1, 0)
    else:
        minv = saved[0]
        u_s = [_solve_saved(n, x, m, u) for n, x, m, u in zip(n_ab, rhs, *saved)]
    y = [_unstack2(x0 + x1 + x2)
         for x0, x1, x2 in zip(dots(r_s, s0, 1, 1), dots(r_ab, u_s, 1, 0), dots(r_ak, v_s, 1, 0))]
    g_end = [_last_row(g) for g in gam]
    s1 = [s * g + x + z for s, g, x, z in zip(s0, g_end, dots(u_s, [x * g for x, g in zip(b_s, g_end)], 0, 0),
                                              dots(v_s, [x * g for x, g in zip(k_s, g_end)], 0, 0))]
    return y, s1, (minv, u_s)


SCAN_PAIRS = 8


def _scan_specs(consts, order):
    row = pl.BlockSpec((SCAN_T, PAIR * SCAN_PAIRS), lambda p, c: (order(c), p))
    state = pl.BlockSpec((None, SCAN_PAIRS, PAIR, PAIR), lambda p, c: (order(c), p, 0, 0))
    return row, state, [pl.BlockSpec(x.shape, _zero_map(x.ndim)) for x in consts]


def _pair_lanes(q):
    return slice(q * PAIR, (q + 1) * PAIR)


def scan_fwd(r, lw, k, v, a, b):
    lp = r.shape[0]
    nch = lp // SCAN_T
    npair = D_MODEL // PAIR
    consts = _scan_consts()
    row, state, cspecs = _scan_specs(consts, lambda c: c)

    def body(tri, strict, incl, masks, eye, r_ref, lw_ref, k_ref, v_ref, a_ref, b_ref, y_ref, s_ref, minv_ref,
             u_ref, carry):
        @pl.when(pl.program_id(1) == 0)
        def _():
            carry[...] = jnp.zeros_like(carry)

        pairs = range(SCAN_PAIRS)
        s0 = [carry[q] for q in pairs]
        rows = [[ref[:, _pair_lanes(q)] for q in pairs] for ref in (r_ref, lw_ref, k_ref, v_ref, a_ref, b_ref)]
        y, s1, (minv, u) = scan_chunk(tri[...], strict[...], incl[...], masks[0:1, :], masks[1:2, :], eye[...],
                                      *rows, s0)
        for q in pairs:
            s_ref[q] = s0[q]
            minv_ref[q] = minv[q]
            u_ref[q] = u[q]
            y_ref[:, _pair_lanes(q)] = y[q]
            carry[q] = s1[q]

    mats = jax.ShapeDtypeStruct((nch, npair, PAIR, PAIR), F32)
    return pl.pallas_call(
        body, name="rwkv_scan_fwd", grid=(npair // SCAN_PAIRS, nch), in_specs=cspecs + [row] * 6,
        out_specs=[row, state, state, state], out_shape=[jax.ShapeDtypeStruct((lp, D_MODEL), F32), mats, mats, mats],
        scratch_shapes=[pltpu.VMEM((SCAN_PAIRS, PAIR, PAIR), F32)], compiler_params=_params(),
    )(*consts, r, lw, k, v, a, b)


def scan_bwd(r, lw, k, v, a, b, saved, dy):
    lp = r.shape[0]
    nch = lp // SCAN_T
    npair = D_MODEL // PAIR
    consts = _scan_consts()
    row, state, cspecs = _scan_specs(consts, lambda c: nch - 1 - c)

    def body(tri, strict, incl, masks, eye, r_ref, lw_ref, k_ref, v_ref, a_ref, b_ref, s_ref, minv_ref, u_ref,
             dy_ref, dr_ref, dlw_ref, dk_ref, dv_ref, da_ref, db_ref, carry):
        @pl.when(pl.program_id(1) == 0)
        def _():
            carry[...] = jnp.zeros_like(carry)

        pairs = range(SCAN_PAIRS)
        kept = ([minv_ref[q] for q in pairs], [u_ref[q] for q in pairs])

        def fn(*args):
            y, s1, _ = scan_chunk(tri[...], strict[...], incl[...], masks[0:1, :], masks[1:2, :], eye[...], *args,
                                  saved=kept)
            return y, s1

        rows = [[ref[:, _pair_lanes(q)] for q in pairs] for ref in (r_ref, lw_ref, k_ref, v_ref, a_ref, b_ref)]
        _, vjp = jax.vjp(fn, *rows, [s_ref[q] for q in pairs])
        grads = vjp(([dy_ref[:, _pair_lanes(q)] for q in pairs], [carry[q] for q in pairs]))
        for q in pairs:
            for ref, g in zip((dr_ref, dlw_ref, dk_ref, dv_ref, da_ref, db_ref), grads[:6]):
                ref[:, _pair_lanes(q)] = g[q]
            carry[q] = grads[6][q]

    return pl.pallas_call(
        body, name="rwkv_scan_bwd", grid=(npair // SCAN_PAIRS, nch), in_specs=cspecs + [row] * 6 + [state] * 3 + [row],
        out_specs=[row] * 6, out_shape=[jax.ShapeDtypeStruct((lp, D_MODEL), F32)] * 6,
        scratch_shapes=[pltpu.VMEM((SCAN_PAIRS, PAIR, PAIR), F32)], compiler_params=_params(),
    )(*consts, r, lw, k, v, a, b, *saved, dy)


def _spread_matrices():
    rep = np.zeros((N_HEADS_KV, KV_DIM, KVW), np.float32)
    for h in range(N_HEADS_KV):
        for g in range(GROUP):
            rep[h, h * HEAD_DIM + np.arange(HEAD_DIM), g * HEAD_DIM + np.arange(HEAD_DIM)] = 1.0
    return jnp.asarray(rep, BF16)


KV_HEADS = range(N_HEADS_KV)


def _attn_common(n, q_ref, kp, kc, vp, vc, rep_ref, sink_ref):
    lane = lax.broadcasted_iota(jnp.int32, (1, KVW), 1)
    gmask = [(lane // HEAD_DIM == g).astype(F32) for g in range(GROUP)]
    kk = jnp.concatenate([kp, kc], axis=0)
    vv = jnp.concatenate([vp, vc], axis=0)
    qs = [q_ref[:, h * KVW:(h + 1) * KVW] for h in KV_HEADS]
    q_s = [jnp.concatenate([q * gmask[g] for g in range(GROUP)], axis=0) for q in qs]
    keys = [_dot(kk, rep_ref[h], 1, 0) for h in KV_HEADS]
    vals = [_dot(vv, rep_ref[h], 1, 0) for h in KV_HEADS]
    qi = lax.broadcasted_iota(jnp.int32, (GROUP * BLOCK, 2 * BLOCK), 0) % BLOCK
    kj = lax.broadcasted_iota(jnp.int32, (GROUP * BLOCK, 2 * BLOCK), 1)
    rel = BLOCK + qi - kj
    valid = (rel >= 0) & (rel < BLOCK) & ((n - 1) * BLOCK + kj >= PAD_FRONT)
    s = [jnp.where(valid, _dot(x, y, 1, 1) * (HEAD_DIM ** -0.5), -1e30) for x, y in zip(q_s, keys)]
    sink_col = [jnp.concatenate([jnp.broadcast_to(sink_ref[h, g:g + 1, 0:1], (BLOCK, 1)) for g in range(GROUP)],
                                axis=0) for h in KV_HEADS]
    m = [jnp.maximum(jnp.max(x, axis=-1, keepdims=True), c) for x, c in zip(s, sink_col)]
    ex = [jnp.exp(x - y) for x, y in zip(s, m)]
    ex_sink = [jnp.exp(c - y) for c, y in zip(sink_col, m)]
    inv = [1.0 / (jnp.sum(x, axis=-1, keepdims=True) + c) for x, c in zip(ex, ex_sink)]
    return (gmask, q_s, keys, vals, [x * y for x, y in zip(ex, inv)], [x * y for x, y in zip(ex_sink, inv)])


def _unstack_groups(x_s, gmask):
    out = None
    for g in range(GROUP):
        t = x_s[g * BLOCK:(g + 1) * BLOCK] * gmask[g]
        out = t if out is None else out + t
    return out


def _attn_specs():
    qspec = pl.BlockSpec((BLOCK, D_MODEL), lambda n: (n, 0))
    cur = pl.BlockSpec((BLOCK, KV_DIM), lambda n: (n, 0))
    prev = pl.BlockSpec((BLOCK, KV_DIM), lambda n: (jnp.maximum(n - 1, 0), 0))
    rep = pl.BlockSpec((N_HEADS_KV, KV_DIM, KVW), lambda n: (0, 0, 0))
    sink = pl.BlockSpec((N_HEADS_KV, 8, PAIR), lambda n: (0, 0, 0))
    return qspec, cur, prev, rep, sink


def _attn_params():
    return pltpu.CompilerParams(dimension_semantics=("arbitrary",), vmem_limit_bytes=VMEM_LIMIT)


def attn_fwd(q, k, v, sinks_b):
    lp = q.shape[0]
    qspec, cur, prev, rep, sink = _attn_specs()

    def body(q_ref, kp_ref, kc_ref, vp_ref, vc_ref, rep_ref, sink_ref, o_ref):
        gmask, _, _, vals, p, _ = _attn_common(pl.program_id(0), q_ref, kp_ref[...], kc_ref[...], vp_ref[...],
                                               vc_ref[...], rep_ref, sink_ref)
        o = [_dot(x, y, 1, 0) for x, y in zip(p, vals)]
        for h in KV_HEADS:
            o_ref[:, h * KVW:(h + 1) * KVW] = _unstack_groups(o[h], gmask)

    return pl.pallas_call(
        body, name="swa_fwd", grid=(lp // BLOCK,), in_specs=[qspec, prev, cur, prev, cur, rep, sink],
        out_specs=qspec, out_shape=jax.ShapeDtypeStruct((lp, D_MODEL), F32), compiler_params=_attn_params(),
    )(q, k, k, v, v, _spread_matrices(), sinks_b)


def attn_bwd(q, k, v, sinks_b, do):
    lp = q.shape[0]
    qspec, cur, prev, rep, sink = _attn_specs()

    def body(q_ref, kp_ref, kc_ref, vp_ref, vc_ref, rep_ref, sink_ref, do_ref, dq_ref, dkc_ref, dkp_ref, dvc_ref,
             dvp_ref, dsink_ref):
        n = pl.program_id(0)
        gmask, q_s, keys, vals, p, p_sink = _attn_common(n, q_ref, kp_ref[...], kc_ref[...], vp_ref[...], vc_ref[...],
                                                         rep_ref, sink_ref)
        do_s = [jnp.concatenate([do_ref[:, h * KVW:(h + 1) * KVW] * gmask[g] for g in range(GROUP)], axis=0)
                for h in KV_HEADS]
        dp = [_dot(x, y, 1, 1) for x, y in zip(do_s, vals)]
        delta = [jnp.sum(x * y, axis=-1, keepdims=True) for x, y in zip(p, dp)]
        ds = [x * (y - z) * (HEAD_DIM ** -0.5) for x, y, z in zip(p, dp, delta)]
        dq = [_dot(x, y, 1, 0) for x, y in zip(ds, keys)]
        dkeys_s = [_dot(x, y, 0, 0) for x, y in zip(ds, q_s)]
        dvals_s = [_dot(x, y, 0, 0) for x, y in zip(p, do_s)]
        dkeys = [_exact_dot(x, rep_ref[h], cb=1) for h, x in enumerate(dkeys_s)]
        dvals = [_exact_dot(x, rep_ref[h], cb=1) for h, x in enumerate(dvals_s)]
        dk_all = (dkeys[0] + dkeys[1]) + (dkeys[2] + dkeys[3])
        dv_all = (dvals[0] + dvals[1]) + (dvals[2] + dvals[3])
        dkp_ref[...] = dk_all[:BLOCK]
        dkc_ref[...] = dk_all[BLOCK:]
        dvp_ref[...] = dv_all[:BLOCK]
        dvc_ref[...] = dv_all[BLOCK:]
        dsinks = []
        for h in KV_HEADS:
            dq_ref[:, h * KVW:(h + 1) * KVW] = _unstack_groups(dq[h], gmask)
            dsk = -(p_sink[h] * delta[h])
            rows = [jnp.broadcast_to(jnp.sum(dsk[g * BLOCK:(g + 1) * BLOCK], axis=0, keepdims=True), (1, PAIR))
                    for g in range(GROUP)]
            dsinks.append(jnp.concatenate(rows + [jnp.zeros((8 - GROUP, PAIR), F32)], axis=0))

        @pl.when(n == 0)
        def _():
            for h in KV_HEADS:
                dsink_ref[h] = dsinks[h]

        @pl.when(n > 0)
        def _():
            for h in KV_HEADS:
                dsink_ref[h] += dsinks[h]

    kv = jax.ShapeDtypeStruct((lp, KV_DIM), F32)
    return pl.pallas_call(
        body, name="swa_bwd", grid=(lp // BLOCK,), in_specs=[qspec, prev, cur, prev, cur, rep, sink, qspec],
        out_specs=[qspec, cur, cur, cur, cur, sink],
        out_shape=[jax.ShapeDtypeStruct((lp, D_MODEL), F32), kv, kv, kv, kv,
                   jax.ShapeDtypeStruct((N_HEADS_KV, 8, PAIR), F32)],
        compiler_params=_attn_params(),
    )(q, k, k, v, v, _spread_matrices(), sinks_b, do)


def _pick_tm(lp, want):
    for tm in (384, 192, 128, 64):
        if tm <= want and lp % tm == 0:
            return tm
    raise ValueError(lp)


def _acc(shape):
    return (tuple(shape), F32)


def _ff_all(w, layer):
    return (w, (N_FF_CHUNK, None, D_MODEL, D_MODEL), lambda c, i: (0, layer, 0, 0))


def _ff_one(w, layer):
    return (w, (None, None, D_MODEL, D_MODEL), lambda c, i: (c, layer, 0, 0))


def _mlp_layer_fwd(name, h, wup, wdown, layer, lg, lb, tm):
    def fn(c, i, h, wup, wdown, lg, lb):
        out = None
        for s in range(N_FF_CHUNK):
            t = mlp_chunk(wup[s], wdown[s], None, h)[0]
            out = t if out is None else out + t
        z = ALPHA * h + out
        return (_layer_norm(z, lg, lb), z), ()

    (h_out, z), _ = rowwise(name, fn, [h], [_ff_all(wup, layer), _ff_all(wdown, layer), lg, lb],
                            [(D_MODEL, F32), (D_MODEL, F32)], [], tm)
    return h_out, z


def _mlp_layer_bwd(name, h_in, z, dh_parts, wup, wdown, layer, lg, lb, tm):
    n_parts = len(dh_parts)

    def fn_ln(c, i, z, *rest):
        dh = rest[0]
        for extra in rest[1:n_parts]:
            dh = dh + extra
        _, vjp = jax.vjp(_layer_norm, z, rest[n_parts], rest[n_parts + 1])
        dz, dlg, dlb = vjp(dh)
        return (dz,), (dlg, dlb)

    (dz,), (dlg, dlb) = rowwise(name + "_ln", fn_ln, [z] + list(dh_parts), [lg, lb], [(D_MODEL, F32)],
                                [_acc((1, D_MODEL)), _acc((1, D_MODEL))], tm)

    def fn_mlp(c, i, h, dz, wup, wdown):
        tile = h.shape[0]
        (dx,), dws = vjp_taps(functools.partial(mlp_chunk, wup, wdown), [(tile, D_MODEL)] * 2, [h], dz)
        return (dx,), dws

    aspec = ((N_FF_CHUNK, D_MODEL, D_MODEL), F32, (None, D_MODEL, D_MODEL), lambda c, i: (c, 0, 0))
    (dx,), (dwup, dwdown) = rowwise(name + "_mm", fn_mlp, [h_in, dz], [_ff_one(wup, layer), _ff_one(wdown, layer)],
                                    [(D_MODEL, F32, True)], [aspec, aspec], tm, nc=N_FF_CHUNK)
    return dz, dx, dwup, dwdown, dlg, dlb


def _sum_parts(dz, dx):
    out = ALPHA * dz
    for s in range(N_FF_CHUNK):
        out = out + dx[s]
    return out


def local_step(x, loss_target, p):
    seq = x.shape[0]
    lp = TOK0 + seq
    tm = _pick_tm(lp, 384)
    tms = _pick_tm(lp, 128)
    e, et = _head_matrices()
    h0 = jnp.concatenate([jnp.zeros((PAD_FRONT, D_MODEL), F32), p["meta_tokens"], x], axis=0)
    hp = jnp.concatenate([jnp.zeros((1, D_MODEL), F32), h0[:-1]], axis=0)
    tgt = jnp.concatenate([jnp.zeros((TOK0, D_MODEL), F32), loss_target], axis=0)
    pos = jnp.maximum(jnp.arange(lp, dtype=F32) - PAD_FRONT, 0.0)
    inv_freq = 1.0 / (ROPE_THETA ** (jnp.arange(0, HEAD_DIM, 2, dtype=F32) / HEAD_DIM))
    ang = pos[:, None] * inv_freq[None, :]
    cos = jnp.tile(jnp.cos(ang), (1, PAIR // (HEAD_DIM // 2)))
    sin = jnp.tile(jnp.sin(ang), (1, PAIR // (HEAD_DIM // 2)))

    pre_vec = [p["a_mu"][j:j + 1] for j in range(6)] + [p["a_w0"], p["a_a0"], p["a_k_k"], p["a_k_a"]]
    pre_w = [p["a_w_r"], p["a_w_k"], p["a_w_v"], p["a_w1"], p["a_w2"], p["a_a1"], p["a_a2"], p["a_g1"], p["a_g2"]]
    n_vec = len(pre_vec)

    def fn_pre(c, i, h, hp, e, et, *ws):
        return rwkv_pre(e, et, ws[n_vec:], None, h, hp, *ws[:n_vec])[0], ()

    (r, lw, k2, v, an, bn, g), _ = rowwise("rwkv_pre", fn_pre, [h0, hp], [e, et] + pre_vec + pre_w,
                                           [(D_MODEL, F32)] * 7, [], tms)
    y, *scan_saved = scan_fwd(r, lw, k2, v, an, bn)

    post_c = [p["a_w_o"], p["a_gn_w"], p["a_gn_b"], p["a_r_k"], p["ln_g00"], p["ln_b00"]]

    def fn_post(c, i, y, r, k2, v, g, h0, e, et, w_o, *vecs):
        return (rwkv_post(e, et, w_o, None, y, r, k2, v, g, h0, *vecs)[0],), ()

    (h1,), _ = rowwise("rwkv_post", fn_post, [y, r, k2, v, g, h0], [e, et] + post_c, [(D_MODEL, F32)], [], tm)
    h2, z2 = _mlp_layer_fwd("mlp0_fwd", h1, p["mlp_up"], p["mlp_down"], 0, p["ln_g01"], p["ln_b01"], tm)

    qkv_w = [p["b_w_q"], p["kv_w_k"], p["kv_w_v"]]

    def fn_qkv(c, i, h, cos, sin, wq, wk, wv):
        return qkv_proj(cos, sin, wq, wk, wv, None, h)[0], ()

    (q, k, vv), _ = rowwise("qkv_proj", fn_qkv, [h2, cos, sin], qkv_w,
                            [(D_MODEL, F32), (KV_DIM, F32), (KV_DIM, F32)], [], tm)
    sinks_b = jnp.broadcast_to(p["b_sinks"].reshape(N_HEADS_KV, GROUP, 1), (N_HEADS_KV, GROUP, PAIR))
    sinks_b = jnp.concatenate([sinks_b, jnp.zeros((N_HEADS_KV, 8 - GROUP, PAIR), F32)], axis=1)
    o = attn_fwd(q, k, vv, sinks_b)

    ao_c = [p["b_w_o"], p["ln_g10"], p["ln_b10"]]

    def fn_ao(c, i, o, h, w_o, lg, lb):
        return (attn_out(w_o, None, o, h, lg, lb)[0],), ()

    (h3,), _ = rowwise("attn_out", fn_ao, [o, h2], ao_c, [(D_MODEL, F32)], [], tm)
    h4, z4 = _mlp_layer_fwd("mlp1_fwd", h3, p["mlp_up"], p["mlp_down"], 1, p["ln_g11"], p["ln_b11"], tm)

    def fn_loss(c, i, h4, tgt):
        real = (_row_ids(i, tm) >= TOK0).astype(F32)
        err = (h4 - tgt) * real
        part = 0.5 * jnp.sum(jnp.sum(err * err, axis=-1, keepdims=True), axis=0, keepdims=True) / D_MODEL
        return (err * (1.0 / D_MODEL),), (jnp.broadcast_to(part, (8, PAIR)),)

    (dh4,), (loss_acc,) = rowwise("loss", fn_loss, [h4, tgt], [], [(D_MODEL, F32)], [_acc((8, PAIR))], tm)
    loss = loss_acc[0, 0]

    grads = {}
    dz4, dx4, grads["mlp_up1"], grads["mlp_down1"], grads["ln_g11"], grads["ln_b11"] = _mlp_layer_bwd(
        "mlp1_bwd", h3, z4, [dh4], p["mlp_up"], p["mlp_down"], 1, p["ln_g11"], p["ln_b11"], tm)

    def fn_ao_b(c, i, dz, dx, o, h, w_o, lg, lb):
        (do, dh, dlg, dlb), (dw_o,) = vjp_taps(functools.partial(attn_out, w_o), [(tms, D_MODEL)], [o, h, lg, lb],
                                               _sum_parts(dz, dx))
        return (do, dh), (dw_o, dlg, dlb)

    (do, dh2_a), (grads["b_w_o"], grads["ln_g10"], grads["ln_b10"]) = rowwise(
        "attn_out_bwd", fn_ao_b, [dz4, dx4, o, h2], ao_c, [(D_MODEL, F32)] * 2,
        [_acc((D_MODEL, D_MODEL)), _acc((1, D_MODEL)), _acc((1, D_MODEL))], tms)

    dq, dkc, dkp, dvc, dvp, dsinks = attn_bwd(q, k, vv, sinks_b, do)
    grads["b_sinks"] = dsinks[:, :GROUP, 0].reshape(1, N_HEADS)
    zblk = jnp.zeros((BLOCK, KV_DIM), F32)
    dkp_s = jnp.concatenate([dkp[BLOCK:], zblk], axis=0)
    dvp_s = jnp.concatenate([dvp[BLOCK:], zblk], axis=0)

    def fn_qkv_b(c, i, h, cos, sin, dq, dkc, dkp, dvc, dvp, wq, wk, wv):
        return vjp_taps(functools.partial(qkv_proj, cos, sin, wq, wk, wv),
                        [(tms, D_MODEL), (tms, KV_DIM), (tms, KV_DIM)], [h], (dq, dkc + dkp, dvc + dvp))

    (dh2_q,), (grads["b_w_q"], grads["kv_w_k"], grads["kv_w_v"]) = rowwise(
        "qkv_proj_bwd", fn_qkv_b, [h2, cos, sin, dq, dkc, dkp_s, dvc, dvp_s], qkv_w, [(D_MODEL, F32)],
        [_acc((D_MODEL, D_MODEL)), _acc((D_MODEL, KV_DIM)), _acc((D_MODEL, KV_DIM))], tms)

    dz2, dx2, grads["mlp_up0"], grads["mlp_down0"], grads["ln_g01"], grads["ln_b01"] = _mlp_layer_bwd(
        "mlp0_bwd", h1, z2, [dh2_a, dh2_q], p["mlp_up"], p["mlp_down"], 0, p["ln_g01"], p["ln_b01"], tm)

    def fn_post_b(c, i, dz, dx, y, r, k2, v, g, h0, e, et, w_o, *vecs):
        out, dws = vjp_taps(functools.partial(rwkv_post, e, et, w_o), [(tms, D_MODEL)],
                            [y, r, k2, v, g, h0] + list(vecs), _sum_parts(dz, dx))
        return out[:6], tuple(dws) + tuple(out[6:])

    (dy, dr_c, dk_c, dv_c, dg, dh0_c), post_g = rowwise(
        "rwkv_post_bwd", fn_post_b, [dz2, dx2, y, r, k2, v, g, h0], [e, et] + post_c, [(D_MODEL, F32)] * 6,
        [_acc((D_MODEL, D_MODEL))] + [_acc((1, D_MODEL))] * 5, tms)
    for name, val in zip(["a_w_o", "a_gn_w", "a_gn_b", "a_r_k", "ln_g00", "ln_b00"], post_g):
        grads[name] = val

    dr_s, dlw, dk_s, dv_s, dan, dbn = scan_bwd(r, lw, k2, v, an, bn, scan_saved, dy)

    def fn_pre_b(c, i, h, hp, dr_c, dr_s, dlw, dk_c, dk_s, dv_c, dv_s, dan, dbn, dg, e, et, *ws):
        real = (_row_ids(i, tms) >= PAD_FRONT).astype(F32)
        cot = tuple(t * real for t in (dr_c + dr_s, dlw, dk_c + dk_s, dv_c + dv_s, dan, dbn, dg))
        out, dws = vjp_taps(functools.partial(rwkv_pre, e, et, ws[n_vec:]), [(tms, n) for n in PRE_TAPS],
                            [h, hp] + list(ws[:n_vec]), cot)
        return out[:2], tuple(out[2:]) + tuple(dws)

    (dh0_p, dhp), pre_g = rowwise(
        "rwkv_pre_bwd", fn_pre_b, [h0, hp, dr_c, dr_s, dlw, dk_c, dk_s, dv_c, dv_s, dan, dbn, dg],
        [e, et] + pre_vec + pre_w, [(D_MODEL, F32)] * 2,
        [_acc((1, D_MODEL))] * n_vec + [_acc(w.shape) for w in pre_w], tms)
    grads["a_mu"] = jnp.concatenate(pre_g[:6], axis=0)
    for name, val in zip(["a_w0", "a_a0", "a_k_k", "a_k_a", "a_w_r", "a_w_k", "a_w_v", "a_w1", "a_w2", "a_a1",
                          "a_a2", "a_g1", "a_g2"], pre_g[6:]):
        grads[name] = val

    dhp_s = jnp.concatenate([dhp[1:], jnp.zeros((1, D_MODEL), F32)], axis=0)

    def fn_add(c, i, a, b, d):
        return (a + b + d,), ()

    (dh0,), _ = rowwise("grad_h0", fn_add, [dh0_c, dh0_p, dhp_s], [], [(D_MODEL, F32)], [], tm)
    grads["meta_tokens"] = dh0[PAD_FRONT:TOK0]
    return loss, dh0[TOK0:], grads


ANY = pl.BlockSpec(memory_space=pl.ANY)
XY_FLIPS = ((0, 1), (1, 0), (1, 1))
ALL_FLIPS = tuple((e >> 2 & 1, e >> 1 & 1, e & 1) for e in range(1, N_DEV))


def _flip(v, bit):
    return 1 - v if bit else v


def all_gather_shards(shards):
    n = len(shards)
    npeer = len(XY_FLIPS)

    def body(*refs):
        src, dst = refs[:n], refs[n:2 * n]
        send_sems, recv_sems, local_sems = refs[2 * n:]
        x, y, c = lax.axis_index("x"), lax.axis_index("y"), lax.axis_index("c")

        def copy(k, j, slot):
            fx, fy = XY_FLIPS[j]
            return pltpu.make_async_remote_copy(
                src_ref=src[k], dst_ref=dst[k].at[slot], send_sem=send_sems.at[k * npeer + j],
                recv_sem=recv_sems.at[k * npeer + j], device_id=(_flip(x, fx), _flip(y, fy), c), device_id_type=MESH)

        mine = [pltpu.make_async_copy(src[k], dst[k].at[2 * x + y], local_sems.at[k]) for k in range(n)]
        sends = [copy(k, j, 2 * x + y) for k in range(n) for j in range(npeer)]
        for cp in mine + sends:
            cp.start()
        for k in range(n):
            for j, (fx, fy) in enumerate(XY_FLIPS):
                copy(k, j, 2 * _flip(x, fx) + _flip(y, fy)).wait_recv()
        for cp in sends:
            cp.wait_send()
        for cp in mine:
            cp.wait()

    return pl.pallas_call(
        body, name="gather_weights", in_specs=[ANY] * n, out_specs=[ANY] * n,
        out_shape=[jax.ShapeDtypeStruct((N_SHARD,) + s.shape, s.dtype) for s in shards],
        scratch_shapes=[pltpu.SemaphoreType.DMA((n * npeer,)), pltpu.SemaphoreType.DMA((n * npeer,)),
                        pltpu.SemaphoreType.DMA((n,))],
    )(*shards)


def _sem_scratch(n):
    return [pltpu.SemaphoreType.DMA((n,)), pltpu.SemaphoreType.DMA((n,))]


def placement():
    x, y, c = lax.axis_index("x"), lax.axis_index("y"), lax.axis_index("c")
    me = 2 * x + y
    others = [j + (j >= me).astype(jnp.int32) for j in range(N_SHARD - 1)]
    return jnp.stack([c, me] + others).astype(jnp.int32)


def pair_exchange(sources):
    n = len(sources)

    def body(*refs):
        src, got = refs[:n], refs[n:2 * n]
        send_sems, recv_sems = refs[2 * n:]
        x, y, c = lax.axis_index("x"), lax.axis_index("y"), lax.axis_index("c")

        def copy(k):
            half = sources[k].shape[1] // 2
            theirs = src[k].at[:, pl.ds(pl.multiple_of((1 - c) * half, 8), half), :]
            return pltpu.make_async_remote_copy(
                src_ref=theirs, dst_ref=got[k], send_sem=send_sems.at[k], recv_sem=recv_sems.at[k],
                device_id=(x, y, 1 - c), device_id_type=MESH)

        sends = [copy(k) for k in range(n)]
        for cp in sends:
            cp.start()
        for cp in sends:
            cp.wait_recv()
        for cp in sends:
            cp.wait_send()

    halves = [jax.ShapeDtypeStruct((s.shape[0], s.shape[1] // 2, s.shape[2]), s.dtype) for s in sources]
    return pl.pallas_call(body, name="grads_pair_exchange", in_specs=[ANY] * n, out_specs=[ANY] * n,
                          out_shape=halves, scratch_shapes=_sem_scratch(n))(*sources)


def chip_exchange(parts):
    n = len(parts)
    npeer = len(XY_FLIPS)

    def body(*refs):
        src, dst = refs[:n], refs[n:2 * n]
        send_sems, recv_sems = refs[2 * n:]
        x, y, c = lax.axis_index("x"), lax.axis_index("y"), lax.axis_index("c")
        me = 2 * x + y

        def copy(k, j, sending):
            fx, fy = XY_FLIPS[j]
            px, py = _flip(x, fx), _flip(y, fy)
            peer = 2 * px + py
            return pltpu.make_async_remote_copy(
                src_ref=src[k].at[peer], dst_ref=dst[k].at[me if sending else peer],
                send_sem=send_sems.at[k * npeer + j], recv_sem=recv_sems.at[k * npeer + j],
                device_id=(px, py, c), device_id_type=MESH)

        sends = [copy(k, j, True) for k in range(n) for j in range(npeer)]
        for cp in sends:
            cp.start()
        for k in range(n):
            for j in range(npeer):
                copy(k, j, False).wait_recv()
        for cp in sends:
            cp.wait_send()

    return pl.pallas_call(
        body, name="grads_chip_exchange", in_specs=[ANY] * n, out_specs=[ANY] * n,
        out_shape=[jax.ShapeDtypeStruct(p.shape, p.dtype) for p in parts], scratch_shapes=_sem_scratch(n * npeer),
    )(*parts)


def sibling_share(halves):
    n = len(halves)

    def body(*refs):
        src, got = refs[:n], refs[n:2 * n]
        send_sems, recv_sems = refs[2 * n:]
        x, y, c = lax.axis_index("x"), lax.axis_index("y"), lax.axis_index("c")
        sends = [pltpu.make_async_remote_copy(
            src_ref=src[k], dst_ref=got[k], send_sem=send_sems.at[k], recv_sem=recv_sems.at[k],
            device_id=(x, y, 1 - c), device_id_type=MESH) for k in range(n)]
        for cp in sends:
            cp.start()
        for cp in sends:
            cp.wait_recv()
        for cp in sends:
            cp.wait_send()

    return pl.pallas_call(
        body, name="grads_sibling_share", in_specs=[ANY] * n, out_specs=[ANY] * n,
        out_shape=[jax.ShapeDtypeStruct(h.shape, h.dtype) for h in halves], scratch_shapes=_sem_scratch(n),
    )(*halves)


ADD_TILE_ELEMS = 512 * 1024


def _row_tile(rows, cols):
    return max(t for t in range(8, rows + 1, 8) if rows % t == 0 and t * cols <= ADD_TILE_ELEMS)


def _prefetch_call(body, name, place, grid, in_specs, out_specs, out_shape, args):
    return pl.pallas_call(
        body, name=name, out_shape=out_shape,
        grid_spec=pltpu.PrefetchScalarGridSpec(num_scalar_prefetch=1, grid=grid, in_specs=in_specs,
                                               out_specs=out_specs),
        compiler_params=pltpu.CompilerParams(dimension_semantics=("arbitrary",) * len(grid),
                                             vmem_limit_bytes=VMEM_LIMIT),
    )(place, *args)


def pair_add(name, place, src, got, dtype):
    n4, half, cols = got.shape
    tile = _row_tile(half, cols)
    nt = half // tile

    def body(pr, a_ref, b_ref, o_ref):
        o_ref[...] = (a_ref[...] + b_ref[...]).astype(o_ref.dtype)

    mine = pl.BlockSpec((None, tile, cols), lambda s, i, pr: (s, pr[0] * nt + i, 0))
    blk = pl.BlockSpec((None, tile, cols), lambda s, i, pr: (s, i, 0))
    return _prefetch_call(body, name, place, (n4, nt), [mine, blk], blk,
                          jax.ShapeDtypeStruct(got.shape, dtype), (src, got))


def chip_add(name, place, part, from_chips):
    _, half, cols = part.shape
    tile = _row_tile(half, cols)

    def body(pr, own_ref, r0_ref, r1_ref, r2_ref, o_ref):
        me = pr[1]
        own, r0, r1, r2 = (r[...].astype(F32) for r in (own_ref, r0_ref, r1_ref, r2_ref))
        t0 = jnp.where(me == 0, own, r0)
        t1 = jnp.where(me == 0, r0, jnp.where(me == 1, own, r1))
        t2 = jnp.where(me <= 1, r1, jnp.where(me == 2, own, r2))
        t3 = jnp.where(me == 3, own, r2)
        o_ref[...] = ((t0 + t1) + t2) + t3

    def slab(j):
        return pl.BlockSpec((None, tile, cols), lambda i, pr: (pr[j], i, 0))

    return _prefetch_call(body, name, place, (half // tile,), [slab(1), slab(2), slab(3), slab(4)],
                          pl.BlockSpec((tile, cols), lambda i, pr: (i, 0)),
                          jax.ShapeDtypeStruct((half, cols), F32), (part, from_chips, from_chips, from_chips))


def reduce_grads(sources, narrow):
    place = placement()
    got = pair_exchange(sources)
    parts = [pair_add(f"grads_pair_add{k}", place, s, g, BF16 if nar else F32)
             for k, (s, g, nar) in enumerate(zip(sources, got, narrow))]
    from_chips = chip_exchange(parts)
    halves = [chip_add(f"grads_chip_add{k}", place, p, f) for k, (p, f) in enumerate(zip(parts, from_chips))]
    return place, halves, sibling_share(halves)


ADAM_ROWS = 256


def adamw_update(name, place, halves, w, m, v):
    nsub, rows, cols = w.shape
    half = rows // 2
    tr = ADAM_ROWS if half % ADAM_ROWS == 0 else half
    nth = half // tr

    def body(pr, *refs):
        g_refs, (w_ref, m_ref, v_ref, g_ref, d_ref, nm_ref, nv_ref) = refs[:2 * nsub], refs[2 * nsub:]
        l = pl.program_id(0)
        mine = (pl.program_id(1) // nth) == pr[0]
        g = None
        for s in range(nsub):
            gs = jnp.where(mine, g_refs[2 * s][...], g_refs[2 * s + 1][...])
            g = gs if g is None else jnp.where(l == s, gs, g)
        m2 = ADAM_B1 * m_ref[...] + (1.0 - ADAM_B1) * g
        v2 = ADAM_B2 * v_ref[...] + (1.0 - ADAM_B2) * (g * g)
        m_hat = m2 / (1.0 - ADAM_B1 ** ADAM_STEP)
        v_hat = v2 / (1.0 - ADAM_B2 ** ADAM_STEP)
        g_ref[...] = g
        d_ref[...] = -ADAM_LR * (m_hat / (jnp.sqrt(v_hat) + ADAM_EPS) + ADAM_WD * w_ref[...])
        nm_ref[...] = m2
        nv_ref[...] = v2

    gblk = pl.BlockSpec((tr, cols), lambda l, i, pr: (i % nth, 0))
    blk = pl.BlockSpec((None, tr, cols), lambda l, i, pr: (l, i, 0))
    out = jax.ShapeDtypeStruct((nsub, rows, cols), F32)
    return _prefetch_call(body, name, place, (nsub, rows // tr), [gblk] * (2 * nsub) + [blk] * 3, [blk] * 4,
                          [out] * 4, [h for pair in halves for h in pair] + [w, m, v])


WEIGHT_NAMES = ("meta_tokens", "a_mu", "a_w_r", "a_w_k", "a_w_v", "a_w_o", "a_w0", "a_w1", "a_w2", "a_a0", "a_a1",
                "a_a2", "a_g1", "a_g2", "a_k_k", "a_k_a", "a_r_k", "a_gn_w", "a_gn_b", "kv_w_k", "kv_w_v", "b_w_q",
                "b_sinks", "b_w_o", "mlp_w_up", "mlp_w_down", "ln_g", "ln_b")
BIG_NAMES = ("a_w_r", "a_w_k", "a_w_v", "a_w_o", "b_w_q", "b_w_o")
PACK_MATS = (("kv_w_k", 256), ("kv_w_v", 256), ("a_w1", 64), ("a_a1", 64), ("a_g1", 128), ("a_w2", 64),
             ("a_a2", 64), ("a_g2", 128))
COLUMN_CUT = ("a_w2", "a_a2", "a_g2")
PACK_VECS = (("a_mu", 6), ("a_w0", 1), ("a_a0", 1), ("a_k_k", 1), ("a_k_a", 1), ("a_gn_w", 1), ("a_gn_b", 1),
             ("ln_g", 4), ("ln_b", 4), ("meta_tokens", 16))
PACK_REPL = (("a_r_k", 4), ("b_sinks", 1))
SHARD_W = D_MODEL // N_SHARD
N_MAT_ROWS = sum(r for _, r in PACK_MATS)
N_VEC_ROWS = sum(r for _, r in PACK_VECS)
N_PACK_ROWS = -(-(N_MAT_ROWS + N_VEC_ROWS + sum(r for _, r in PACK_REPL)) // 8) * 8
N_GATHER_VEC_ROWS = -(-N_VEC_ROWS // 8) * 8


def _pack_rows(arr):
    if arr.size == N_HEADS:
        return jnp.pad(arr.reshape(1, N_HEADS), ((0, 0), (0, SHARD_W - N_HEADS)))
    return arr.reshape(-1, SHARD_W)


def pack_small(get):
    parts = [_pack_rows(get(name)) for name, _ in PACK_MATS + PACK_VECS + PACK_REPL]
    used = sum(p.shape[0] for p in parts)
    return jnp.concatenate(parts + [jnp.zeros((N_PACK_ROWS - used, SHARD_W), F32)], axis=0)


def unpack_small(pack, shapes):
    out, off = {}, 0
    for name, rows in PACK_MATS + PACK_VECS + PACK_REPL:
        piece = pack[off:off + rows]
        off += rows
        out[name] = piece[:, :N_HEADS].reshape(shapes[name]) if name == "b_sinks" else piece.reshape(shapes[name])
    return out


def whole_weights(gathered_big, mats, vecs, a_r_k, b_sinks):
    p = {name: g.reshape(D_MODEL, D_MODEL) for name, g in zip(BIG_NAMES, gathered_big)}
    off = 0
    for name, rows in PACK_MATS:
        piece = mats[:, off:off + rows]
        off += rows
        if name in COLUMN_CUT:
            p[name] = piece.transpose(1, 0, 2).reshape(rows, D_MODEL)
        else:
            p[name] = piece.reshape(D_MODEL, rows)
    v = vecs.transpose(1, 0, 2).reshape(-1, D_MODEL)
    off = 0
    for name, rows in PACK_VECS:
        p[name] = v[off:off + rows]
        off += rows
    for i in range(2):
        for j in range(2):
            p[f"ln_g{i}{j}"] = p["ln_g"][2 * i + j:2 * i + j + 1]
            p[f"ln_b{i}{j}"] = p["ln_b"][2 * i + j:2 * i + j + 1]
    p["a_r_k"] = a_r_k.reshape(1, D_MODEL)
    p["b_sinks"] = b_sinks
    return p


def small_grad_pack(g):
    parts = []
    for name, rows in PACK_MATS:
        if name in COLUMN_CUT:
            parts.append(g[name].reshape(rows, N_SHARD, SHARD_W).transpose(1, 0, 2))
        else:
            parts.append(g[name].reshape(N_SHARD, rows, SHARD_W))
    vec_rows = [g["a_mu"]] + [g[n] for n in ("a_w0", "a_a0", "a_k_k", "a_k_a", "a_gn_w", "a_gn_b")]
    vec_rows += [g[f"ln_g{i}{j}"] for i in range(2) for j in range(2)]
    vec_rows += [g[f"ln_b{i}{j}"] for i in range(2) for j in range(2)] + [g["meta_tokens"]]
    parts.append(jnp.concatenate(vec_rows, axis=0).reshape(N_VEC_ROWS, N_SHARD, SHARD_W).transpose(1, 0, 2))
    parts.append(jnp.broadcast_to(g["a_r_k"].reshape(1, -1, SHARD_W), (N_SHARD, D_MODEL // SHARD_W, SHARD_W)))
    sinks = jnp.pad(g["b_sinks"].reshape(1, 1, N_HEADS), ((0, 0), (0, 0), (0, SHARD_W - N_HEADS)))
    parts.append(jnp.broadcast_to(sinks, (N_SHARD, 1, SHARD_W)))
    used = sum(p.shape[1] for p in parts)
    parts.append(jnp.zeros((N_SHARD, N_PACK_ROWS - used, SHARD_W), F32))
    return jnp.concatenate(parts, axis=1)


def train_step(vals):
    w = {n: vals[n] for n in WEIGHT_NAMES}
    w_pack = pack_small(lambda n: w[n])
    shards = [w[n][0].astype(BF16) for n in BIG_NAMES]
    shards += [w["mlp_w_up"].astype(BF16), w["mlp_w_down"].astype(BF16), w_pack[:N_MAT_ROWS].astype(BF16),
               w_pack[N_MAT_ROWS:N_MAT_ROWS + N_GATHER_VEC_ROWS]]
    gathered = all_gather_shards(shards)
    nb = len(BIG_NAMES)
    p = whole_weights(gathered[:nb], gathered[nb + 2], gathered[nb + 3][:, :N_VEC_ROWS], w["a_r_k"], w["b_sinks"])
    p["mlp_up"], p["mlp_down"] = gathered[nb], gathered[nb + 1]

    loss, gx, g = local_step(vals["x"][0], vals["loss_target"][0], p)
    loss = lax.psum(loss, ("x", "y", "c"))

    sources = [g[n].reshape(N_SHARD, SHARD_W, D_MODEL) for n in BIG_NAMES]
    sources += [g["mlp_up0"], g["mlp_up1"], g["mlp_down0"], g["mlp_down1"], small_grad_pack(g)]
    place, mine, theirs = reduce_grads(sources, [True] * (len(sources) - 1) + [False])
    halves = list(zip(mine, theirs))

    res = {}
    for k, n in enumerate(BIG_NAMES):
        res[n] = adamw_update("adamw_" + n, place, halves[k:k + 1], w[n], vals["m_" + n], vals["v_" + n])
    for k, n in ((nb, "mlp_w_up"), (nb + 2, "mlp_w_down")):
        res[n] = adamw_update("adamw_" + n, place, halves[k:k + 2], w[n], vals["m_" + n], vals["v_" + n])
    packs = adamw_update("adamw_small", place, halves[-1:], w_pack[None], pack_small(lambda n: vals["m_" + n])[None],
                         pack_small(lambda n: vals["v_" + n])[None])
    shapes = {n: w[n].shape for n in WEIGHT_NAMES}
    small = [unpack_small(pk[0], shapes) for pk in packs]
    outs = [loss, gx[None]]
    for t in range(4):
        outs += [res[n][t] if n in res else small[t][n] for n in WEIGHT_NAMES]
    return tuple(outs)


def kernel(x, meta_tokens, a_mu, a_w_r, a_w_k, a_w_v, a_w_o, a_w0, a_w1, a_w2, a_a0, a_a1, a_a2, a_g1, a_g2, a_k_k,
           a_k_a, a_r_k, a_gn_w, a_gn_b, kv_w_k, kv_w_v, b_w_q, b_sinks, b_w_o, mlp_w_up, mlp_w_down, ln_g, ln_b,
           loss_target, m_meta_tokens, m_a_mu, m_a_w_r, m_a_w_k, m_a_w_v, m_a_w_o, m_a_w0, m_a_w1, m_a_w2, m_a_a0,
           m_a_a1, m_a_a2, m_a_g1, m_a_g2, m_a_k_k, m_a_k_a, m_a_r_k, m_a_gn_w, m_a_gn_b, m_kv_w_k, m_kv_w_v,
           m_b_w_q, m_b_sinks, m_b_w_o, m_mlp_w_up, m_mlp_w_down, m_ln_g, m_ln_b, v_meta_tokens, v_a_mu, v_a_w_r,
           v_a_w_k, v_a_w_v, v_a_w_o, v_a_w0, v_a_w1, v_a_w2, v_a_a0, v_a_a1, v_a_a2, v_a_g1, v_a_g2, v_a_k_k,
           v_a_k_a, v_a_r_k, v_a_gn_w, v_a_gn_b, v_kv_w_k, v_kv_w_v, v_b_w_q, v_b_sinks, v_b_w_o, v_mlp_w_up,
           v_mlp_w_down, v_ln_g, v_ln_b):
    return train_step(dict(locals()))
```

```python
import functools

import numpy as np
import jax
import jax.numpy as jnp
from jax import lax
from jax.experimental import pallas as pl
from jax.experimental.pallas import tpu as pltpu

F32 = jnp.float32
BF16 = jnp.bfloat16

D_MODEL = 1024
N_HEADS = 16
HEAD_DIM = 64
N_HEADS_KV = 4
GROUP = 4
KV_DIM = N_HEADS_KV * HEAD_DIM
N_META = 16
BLOCK = 128
PAD_FRONT = BLOCK - N_META
TOK0 = PAD_FRONT + N_META
N_FF_CHUNK = 4
N_SHARD = 4
N_DEV = 8
GN_EPS = 64e-5
LN_EPS = 1e-5
ROPE_THETA = 10000.0
ALPHA = 4.0 ** 0.25
ADAM_LR, ADAM_B1, ADAM_B2, ADAM_EPS, ADAM_WD, ADAM_STEP = 0.001, 0.9, 0.999, 1e-08, 0.01, 10
SCAN_T = 64
PAIR = 128
KVW = GROUP * HEAD_DIM
VMEM_LIMIT = 56 * 1024 * 1024
HI = lax.Precision.HIGHEST
MESH = pl.DeviceIdType.MESH


def _dot(a, b, ca, cb):
    return lax.dot_general(a.astype(BF16), b.astype(BF16), (((ca,), (cb,)), ((), ())),
                           preferred_element_type=F32)


@jax.custom_vjp
def mm(a, b):
    return _dot(a, b, 1, 0)


def _mm_fwd(a, b):
    return mm(a, b), b


def _mm_bwd(b, g):
    return _dot(g, b, 1, 1), jnp.zeros_like(b)


mm.defvjp(_mm_fwd, _mm_bwd)


def tmm(x, w, taps, xs):
    y = mm(x, w)
    if taps is not None:
        y = y + taps[len(xs)]
    xs.append(x)
    return y


def vjp_taps(core, tap_shapes, args, cot):
    taps = [jnp.zeros(s, F32) for s in tap_shapes]
    _, vjp, xs = jax.vjp(core, taps, *args, has_aux=True)
    out = vjp(cot)
    return out[1:], [_dot(x, g, 0, 0) for x, g in zip(xs, out[0])]


def _split3(x):
    x1 = x.astype(BF16)
    r1 = x - x1.astype(F32)
    x2 = r1.astype(BF16)
    x3 = (r1 - x2.astype(F32)).astype(BF16)
    return x1, x2, x3


def _exact_dot(x, m01, cb=0):
    acc = None
    for piece in _split3(x):
        t = lax.dot_general(piece, m01, (((1,), (cb,)), ((), ())), preferred_element_type=F32)
        acc = t if acc is None else acc + t
    return acc


def _head_matrices():
    e = np.zeros((D_MODEL, N_HEADS), np.float32)
    e[np.arange(D_MODEL), np.arange(D_MODEL) // HEAD_DIM] = 1.0
    return jnp.asarray(e, BF16), jnp.asarray(e.T, BF16)


@jax.custom_vjp
def hsum(x, e, et):
    return _exact_dot(x, e)


@jax.custom_vjp
def hbc(s, e, et):
    return _exact_dot(s, et)


hsum.defvjp(lambda x, e, et: (_exact_dot(x, e), (e, et)),
            lambda res, g: (hbc(g, *res), jnp.zeros_like(res[0]), jnp.zeros_like(res[1])))
hbc.defvjp(lambda s, e, et: (_exact_dot(s, et), (e, et)),
           lambda res, g: (hsum(g, *res), jnp.zeros_like(res[0]), jnp.zeros_like(res[1])))


def _sigmoid(u):
    return 0.5 * (jnp.tanh(0.5 * u) + 1.0)


def _softplus(u):
    return jnp.maximum(u, 0.0) + jnp.log(1.0 + jnp.exp(-jnp.abs(u)))


def _layer_norm(z, g, b):
    mu = jnp.mean(z, axis=-1, keepdims=True)
    zc = z - mu
    var = jnp.mean(zc * zc, axis=-1, keepdims=True)
    return zc * lax.rsqrt(var + LN_EPS) * g + b


def _zero_map(nd):
    return lambda c, i: (0,) * nd


def _params():
    return pltpu.CompilerParams(dimension_semantics=("arbitrary", "arbitrary"), vmem_limit_bytes=VMEM_LIMIT)


def rowwise(name, fn, rows, consts, out_rows, out_accs, tm, nc=1):
    lp = rows[0].shape[-2]
    nt = lp // tm
    assert nt * tm == lp, (name, lp, tm)
    in_specs, args = [], []
    for a in rows:
        if a.ndim == 2:
            in_specs.append(pl.BlockSpec((tm, a.shape[1]), lambda c, i: (i, 0)))
        else:
            in_specs.append(pl.BlockSpec((a.shape[0], tm, a.shape[2]), lambda c, i: (0, i, 0)))
        args.append(a)
    for cst in consts:
        if isinstance(cst, tuple):
            arr, bs, im = cst
            in_specs.append(pl.BlockSpec(bs, im))
        else:
            arr = cst
            in_specs.append(pl.BlockSpec(arr.shape, _zero_map(arr.ndim), pipeline_mode=pl.Buffered(1)))
        args.append(arr)
    out_shape, out_specs, acc_per_chunk = [], [], []
    for spec in out_rows:
        if len(spec) == 3 and spec[2]:
            out_shape.append(jax.ShapeDtypeStruct((nc, lp, spec[0]), spec[1]))
            out_specs.append(pl.BlockSpec((None, tm, spec[0]), lambda c, i: (c, i, 0)))
        else:
            out_shape.append(jax.ShapeDtypeStruct((lp, spec[0]), spec[1]))
            out_specs.append(pl.BlockSpec((tm, spec[0]), lambda c, i: (i, 0)))
    for spec in out_accs:
        out_shape.append(jax.ShapeDtypeStruct(spec[0], spec[1]))
        if len(spec) == 4:
            out_specs.append(pl.BlockSpec(spec[2], spec[3]))
            acc_per_chunk.append(True)
        else:
            out_specs.append(pl.BlockSpec(spec[0], _zero_map(len(spec[0])), pipeline_mode=pl.Buffered(1)))
            acc_per_chunk.append(False)
    n_in, n_or = len(args), len(out_rows)

    def body(*refs):
        c = pl.program_id(0)
        i = pl.program_id(1)
        vals = [r[...] for r in refs[:n_in]]
        outs_r, outs_a = fn(c, i, *vals)
        for ref, val in zip(refs[n_in:n_in + n_or], outs_r):
            ref[...] = val.astype(ref.dtype)
        for ref, val, per_chunk in zip(refs[n_in + n_or:], outs_a, acc_per_chunk):
            first = (i == 0) if per_chunk else jnp.logical_and(i == 0, c == 0)

            @pl.when(first)
            def _():
                ref[...] = val.astype(ref.dtype)

            @pl.when(jnp.logical_not(first))
            def _():
                ref[...] += val.astype(ref.dtype)

    outs = pl.pallas_call(body, name=name, grid=(nc, nt), in_specs=in_specs, out_specs=out_specs,
                          out_shape=out_shape, compiler_params=_params())(*args)
    return outs[:n_or], outs[n_or:]


def _row_ids(i, tm):
    return i * tm + lax.broadcasted_iota(jnp.int32, (tm, 1), 0)


PRE_TAPS = (D_MODEL, D_MODEL, D_MODEL, 64, D_MODEL, 64, D_MODEL, 128, D_MODEL)


def rwkv_pre(e, et, ws, taps, h, hp, mu_r, mu_w, mu_k, mu_v, mu_a, mu_g, w0, a0, k_k, k_a):
    w_r, w_k, w_v, w1, w2, a1, a2, g1, g2 = ws
    xs = []
    xx = hp - h
    r = tmm(h + xx * mu_r, w_r, taps, xs)
    k = tmm(h + xx * mu_k, w_k, taps, xs)
    v = tmm(h + xx * mu_v, w_v, taps, xs)
    wraw = -_softplus(-(w0 + tmm(jnp.tanh(tmm(h + xx * mu_w, w1, taps, xs)), w2, taps, xs))) - 0.5
    lw = -jnp.exp(wraw)
    a = _sigmoid(a0 + tmm(tmm(h + xx * mu_a, a1, taps, xs), a2, taps, xs))
    g = tmm(_sigmoid(tmm(h + xx * mu_g, g1, taps, xs)), g2, taps, xs)
    kk = k * k_k
    ss = hsum(kk * kk, e, et)
    pos = ss > 0.0
    nrm = jnp.where(pos, jnp.sqrt(jnp.where(pos, ss, 1.0)), 0.0)
    kk = kk * hbc(1.0 / jnp.maximum(nrm, 1e-12), e, et)
    k2 = k * (1.0 + (a - 1.0) * k_a)
    return (r, lw, k2, v, -kk, kk * a, g), xs


def rwkv_post(e, et, w_o, taps, y, r, k2, v, g, h0, gn_w, gn_b, rk, lg, lb):
    xs = []
    inv_n = 1.0 / HEAD_DIM
    yc = y - hbc(hsum(y, e, et) * inv_n, e, et)
    yv = hsum(yc * yc, e, et) * inv_n
    yn = yc * hbc(lax.rsqrt(yv + GN_EPS), e, et) * gn_w + gn_b
    bonus = hbc(hsum(r * k2 * rk, e, et), e, et) * v
    mix = tmm((yn + bonus) * g, w_o, taps, xs)
    return _layer_norm(ALPHA * h0 + mix, lg, lb), xs


def mlp_chunk(wup, wdown, taps, h):
    xs = []
    u = jnp.maximum(tmm(h, wup, taps, xs), 0.0)
    return tmm(u * u, wdown, taps, xs), xs


def _rot_half(t):
    n = t.shape[-1]
    lane = lax.broadcasted_iota(jnp.int32, t.shape, t.ndim - 1)
    lo = (lane % HEAD_DIM) < (HEAD_DIM // 2)
    return jnp.where(lo, -pltpu.roll(t, n - HEAD_DIM // 2, t.ndim - 1), pltpu.roll(t, HEAD_DIM // 2, t.ndim - 1))


@jax.custom_vjp
def rot_half(t):
    return _rot_half(t)


rot_half.defvjp(lambda t: (_rot_half(t), None), lambda _, g: (-_rot_half(g),))


def _tile_lanes(t, width):
    return jnp.concatenate([t] * (width // t.shape[-1]), axis=-1)


def qkv_proj(cos, sin, wq, wk, wv, taps, h):
    xs = []
    q = tmm(h, wq, taps, xs)
    k = tmm(h, wk, taps, xs)
    v = tmm(h, wv, taps, xs)
    cq, sq = _tile_lanes(cos, D_MODEL), _tile_lanes(sin, D_MODEL)
    ck, sk = _tile_lanes(cos, KV_DIM), _tile_lanes(sin, KV_DIM)
    return (q * cq + rot_half(q) * sq, k * ck + rot_half(k) * sk, v), xs


def attn_out(w_o, taps, o, h, lg, lb):
    xs = []
    return _layer_norm(ALPHA * h + tmm(o, w_o, taps, xs), lg, lb), xs


def _scan_consts():
    t = SCAN_T
    tri = np.tril(np.ones((t, t), np.float32))
    rows = np.arange(2 * t)
    same = (rows[:, None] // t) == (rows[None, :] // t)
    strict = same & ((rows[None, :] % t) < (rows[:, None] % t))
    incl = same & ((rows[None, :] % t) <= (rows[:, None] % t))
    lane = np.arange(PAIR)
    masks = np.zeros((8, PAIR), np.float32)
    masks[0] = (lane // HEAD_DIM) == 0
    masks[1] = (lane // HEAD_DIM) == 1
    return (jnp.asarray(tri, BF16), jnp.asarray(strict.astype(np.float32)), jnp.asarray(incl.astype(np.float32)),
            jnp.asarray(masks), jnp.asarray(np.eye(2 * t, dtype=np.float32)))


def _dot_x3(a, b, ca, cb):
    a1 = a.astype(BF16)
    a2 = (a - a1.astype(F32)).astype(BF16)
    b1 = b.astype(BF16)
    b2 = (b - b1.astype(F32)).astype(BF16)

    def d(u, v):
        return lax.dot_general(u, v, (((ca,), (cb,)), ((), ())), preferred_element_type=F32)

    return d(a1, b1) + (d(a1, b2) + d(a2, b1))


@functools.partial(jax.custom_vjp, nondiff_argnums=(2, 3))
def _dotf(a, b, ca, cb):
    return _dot_x3(a, b, ca, cb)


def _dotf_bwd(ca, cb, res, g):
    a, b = res
    if ca == 1:
        da = _dot_x3(g, b, 1, 1 - cb)
    else:
        da = _dot_x3(b, g, 1 - cb, 1)
    if cb == 0:
        db = _dot_x3(a, g, 1 - ca, 0)
    else:
        db = _dot_x3(g, a, 0, 1 - ca)
    return da, db


_dotf.defvjp(lambda a, b, ca, cb: (_dot_x3(a, b, ca, cb), (a, b)), _dotf_bwd)


def _tri_dot(tri, x, ct):
    acc = None
    for piece in _split3(x):
        t = lax.dot_general(tri, piece, (((ct,), (0,)), ((), ())), preferred_element_type=F32)
        acc = t if acc is None else acc + t
    return acc


@jax.custom_vjp
def _cumsum_rows(tri, x):
    return _tri_dot(tri, x, 1)


_cumsum_rows.defvjp(lambda tri, x: (_tri_dot(tri, x, 1), tri),
                    lambda tri, g: (jnp.zeros_like(tri), _tri_dot(tri, g, 0)))


@jax.custom_vjp
def _unstack2(x):
    t = x.shape[0] // 2
    return x[:t] + x[t:]


_unstack2.defvjp(lambda x: (_unstack2(x), None), lambda _, g: (jnp.concatenate([g, g], axis=0),))


@jax.custom_vjp
def _last_row(x):
    return x[x.shape[0] - 1:, :]


def _last_row_bwd(_, g):
    rows = lax.broadcasted_iota(jnp.int32, (SCAN_T, g.shape[1]), 0)
    return (jnp.where(rows == SCAN_T - 1, jnp.broadcast_to(g, (SCAN_T, g.shape[1])), 0.0),)


_last_row.defvjp(lambda x: (_last_row(x), None), _last_row_bwd)


@jax.custom_vjp
def _solve_saved(n, rhs, minv, u):
    return u


def _solve_saved_bwd(res, du):
    minv, u = res
    drhs = _dotf(minv, du, 0, 0)
    return _dotf(drhs, u, 1, 1), drhs, jnp.zeros_like(minv), jnp.zeros_like(u)


_solve_saved.defvjp(lambda n, rhs, minv, u: (u, (minv, u)), _solve_saved_bwd)


def scan_chunk(tri, strict, incl, m0, m1, eye, r, lw, k, v, a, b, s0, saved=None):
    lower = strict > 0
    lower_incl = incl > 0

    def stack(x):
        return jnp.concatenate([x * m0, x * m1], axis=0)

    def dots(xs, ys, ca, cb, mask=None):
        out = [_dotf(x, y, ca, cb) for x, y in zip(xs, ys)]
        return out if mask is None else [jnp.where(mask, o, 0.0) for o in out]

    cl = [_cumsum_rows(tri, x) for x in lw]
    gam = [jnp.exp(c) for c in cl]
    ginv = [jnp.exp(-c) for c in cl]
    a_s = [stack(x * jnp.exp(c - w)) for x, c, w in zip(a, cl, lw)]
    r_s = [stack(x * g) for x, g in zip(r, gam)]
    b_s = [stack(x * g) for x, g in zip(b, ginv)]
    k_s = [stack(x * g) for x, g in zip(k, ginv)]
    v_s = [stack(x) for x in v]
    n_ab = dots(a_s, b_s, 1, 1, lower)
    n_ak = dots(a_s, k_s, 1, 1, lower)
    r_ab = dots(r_s, b_s, 1, 1, lower_incl)
    r_ak = dots(r_s, k_s, 1, 1, lower_incl)
    rhs = [x + y for x, y in zip(dots(a_s, s0, 1, 1), dots(n_ak, v_s, 1, 0))]
    if saved is None:
        minv = [eye + n for n in n_ab]
        p = n_ab
        for _ in range(5):
            p = dots(p, p, 1, 0)
            minv = [m + mp for m, mp in zip(minv, dots(minv, p, 1, 0))]
        u_s = dots(minv, rhs, 1, 0)
    else:
        minv = saved[0]
        u_s = [_solve_saved(n, x, m, u) for n, x, m, u in zip(n_ab, rhs, *saved)]
    y = [_unstack2(x0 + x1 + x2)
         for x0, x1, x2 in zip(dots(r_s, s0, 1, 1), dots(r_ab, u_s, 1, 0), dots(r_ak, v_s, 1, 0))]
    g_end = [_last_row(g) for g in gam]
    s1 = [s * g + x + z for s, g, x, z in zip(s0, g_end, dots(u_s, [x * g for x, g in zip(b_s, g_end)], 0, 0),
                                              dots(v_s, [x * g for x, g in zip(k_s, g_end)], 0, 0))]
    return y, s1, (minv, u_s)


SCAN_PAIRS = 8


def _scan_specs(consts, order):
    row = pl.BlockSpec((SCAN_T, PAIR * SCAN_PAIRS), lambda p, c: (order(c), p))
    state = pl.BlockSpec((None, SCAN_PAIRS, PAIR, PAIR), lambda p, c: (order(c), p, 0, 0))
    return row, state, [pl.BlockSpec(x.shape, _zero_map(x.ndim)) for x in consts]


def _pair_lanes(q):
    return slice(q * PAIR, (q + 1) * PAIR)


def scan_fwd(r, lw, k, v, a, b, shards=()):
    lp = r.shape[0]
    nch = lp // SCAN_T
    npair = D_MODEL // PAIR
    ng = len(shards)
    consts = _scan_consts()
    row, state, cspecs = _scan_specs(consts, lambda c: c)

    def body(tri, strict, incl, masks, eye, r_ref, lw_ref, k_ref, v_ref, a_ref, b_ref, *rest):
        src, (y_ref, s_ref, minv_ref, u_ref), dst = rest[:ng], rest[ng:ng + 4], rest[ng + 4:2 * ng + 4]
        carry = rest[2 * ng + 4]
        first = jnp.logical_and(pl.program_id(0) == 0, pl.program_id(1) == 0)
        last = jnp.logical_and(pl.program_id(0) == npair // SCAN_PAIRS - 1, pl.program_id(1) == nch - 1)
        if ng:
            sends, arrivals = gather_copies(src, dst, *rest[2 * ng + 5:])

            @pl.when(first)
            def _():
                for cp in sends:
                    cp.start()

        @pl.when(pl.program_id(1) == 0)
        def _():
            carry[...] = jnp.zeros_like(carry)

        pairs = range(SCAN_PAIRS)
        s0 = [carry[q] for q in pairs]
        rows = [[ref[:, _pair_lanes(q)] for q in pairs] for ref in (r_ref, lw_ref, k_ref, v_ref, a_ref, b_ref)]
        y, s1, (minv, u) = scan_chunk(tri[...], strict[...], incl[...], masks[0:1, :], masks[1:2, :], eye[...],
                                      *rows, s0)
        for q in pairs:
            s_ref[q] = s0[q]
            minv_ref[q] = minv[q]
            u_ref[q] = u[q]
            y_ref[:, _pair_lanes(q)] = y[q]
            carry[q] = s1[q]

        if ng:
            @pl.when(last)
            def _():
                for cp in arrivals:
                    cp.wait_recv()
                for cp in sends:
                    cp.wait_send()

    mats = jax.ShapeDtypeStruct((nch, npair, PAIR, PAIR), F32)
    out = pl.pallas_call(
        body, name="rwkv_scan_fwd", grid=(npair // SCAN_PAIRS, nch), in_specs=cspecs + [row] * 6 + [ANY] * ng,
        out_specs=[row, state, state, state] + [ANY] * ng,
        out_shape=[jax.ShapeDtypeStruct((lp, D_MODEL), F32), mats, mats, mats] + gathered_shapes(shards),
        scratch_shapes=[pltpu.VMEM((SCAN_PAIRS, PAIR, PAIR), F32)] + (_sem_scratch(ng * len(XY_FLIPS)) if ng else []),
        compiler_params=_params(),
    )(*consts, r, lw, k, v, a, b, *shards)
    return out[:4], fill_own(out[4:], shards)


def scan_bwd(r, lw, k, v, a, b, saved, dy):
    lp = r.shape[0]
    nch = lp // SCAN_T
    npair = D_MODEL // PAIR
    consts = _scan_consts()
    row, state, cspecs = _scan_specs(consts, lambda c: nch - 1 - c)

    def body(tri, strict, incl, masks, eye, r_ref, lw_ref, k_ref, v_ref, a_ref, b_ref, s_ref, minv_ref, u_ref,
             dy_ref, dr_ref, dlw_ref, dk_ref, dv_ref, da_ref, db_ref, carry):
        @pl.when(pl.program_id(1) == 0)
        def _():
            carry[...] = jnp.zeros_like(carry)

        pairs = range(SCAN_PAIRS)
        kept = ([minv_ref[q] for q in pairs], [u_ref[q] for q in pairs])

        def fn(*args):
            y, s1, _ = scan_chunk(tri[...], strict[...], incl[...], masks[0:1, :], masks[1:2, :], eye[...], *args,
                                  saved=kept)
            return y, s1

        rows = [[ref[:, _pair_lanes(q)] for q in pairs] for ref in (r_ref, lw_ref, k_ref, v_ref, a_ref, b_ref)]
        _, vjp = jax.vjp(fn, *rows, [s_ref[q] for q in pairs])
        grads = vjp(([dy_ref[:, _pair_lanes(q)] for q in pairs], [carry[q] for q in pairs]))
        for q in pairs:
            for ref, g in zip((dr_ref, dlw_ref, dk_ref, dv_ref, da_ref, db_ref), grads[:6]):
                ref[:, _pair_lanes(q)] = g[q]
            carry[q] = grads[6][q]

    return pl.pallas_call(
        body, name="rwkv_scan_bwd", grid=(npair // SCAN_PAIRS, nch), in_specs=cspecs + [row] * 6 + [state] * 3 + [row],
        out_specs=[row] * 6, out_shape=[jax.ShapeDtypeStruct((lp, D_MODEL), F32)] * 6,
        scratch_shapes=[pltpu.VMEM((SCAN_PAIRS, PAIR, PAIR), F32)], compiler_params=_params(),
    )(*consts, r, lw, k, v, a, b, *saved, dy)


def _spread_matrices():
    rep = np.zeros((N_HEADS_KV, KV_DIM, KVW), np.float32)
    for h in range(N_HEADS_KV):
        for g in range(GROUP):
            rep[h, h * HEAD_DIM + np.arange(HEAD_DIM), g * HEAD_DIM + np.arange(HEAD_DIM)] = 1.0
    return jnp.asarray(rep, BF16)


KV_HEADS = range(N_HEADS_KV)


def _attn_common(n, q_ref, kp, kc, vp, vc, rep_ref, sink_ref):
    lane = lax.broadcasted_iota(jnp.int32, (1, KVW), 1)
    gmask = [(lane // HEAD_DIM == g).astype(F32) for g in range(GROUP)]
    kk = jnp.concatenate([kp, kc], axis=0)
    vv = jnp.concatenate([vp, vc], axis=0)
    qs = [q_ref[:, h * KVW:(h + 1) * KVW] for h in KV_HEADS]
    q_s = [jnp.concatenate([q * gmask[g] for g in range(GROUP)], axis=0) for q in qs]
    keys = [_dot(kk, rep_ref[h], 1, 0) for h in KV_HEADS]
    vals = [_dot(vv, rep_ref[h], 1, 0) for h in KV_HEADS]
    qi = lax.broadcasted_iota(jnp.int32, (GROUP * BLOCK, 2 * BLOCK), 0) % BLOCK
    kj = lax.broadcasted_iota(jnp.int32, (GROUP * BLOCK, 2 * BLOCK), 1)
    rel = BLOCK + qi - kj
    valid = (rel >= 0) & (rel < BLOCK) & ((n - 1) * BLOCK + kj >= PAD_FRONT)
    s = [jnp.where(valid, _dot(x, y, 1, 1) * (HEAD_DIM ** -0.5), -1e30) for x, y in zip(q_s, keys)]
    sink_col = [jnp.concatenate([jnp.broadcast_to(sink_ref[h, g:g + 1, 0:1], (BLOCK, 1)) for g in range(GROUP)],
                                axis=0) for h in KV_HEADS]
    m = [jnp.maximum(jnp.max(x, axis=-1, keepdims=True), c) for x, c in zip(s, sink_col)]
    ex = [jnp.exp(x - y) for x, y in zip(s, m)]
    ex_sink = [jnp.exp(c - y) for c, y in zip(sink_col, m)]
    inv = [1.0 / (jnp.sum(x, axis=-1, keepdims=True) + c) for x, c in zip(ex, ex_sink)]
    return (gmask, q_s, keys, vals, [x * y for x, y in zip(ex, inv)], [x * y for x, y in zip(ex_sink, inv)])


def _unstack_groups(x_s, gmask):
    out = None
    for g in range(GROUP):
        t = x_s[g * BLOCK:(g + 1) * BLOCK] * gmask[g]
        out = t if out is None else out + t
    return out


def _attn_specs():
    qspec = pl.BlockSpec((BLOCK, D_MODEL), lambda n: (n, 0))
    cur = pl.BlockSpec((BLOCK, KV_DIM), lambda n: (n, 0))
    prev = pl.BlockSpec((BLOCK, KV_DIM), lambda n: (jnp.maximum(n - 1, 0), 0))
    rep = pl.BlockSpec((N_HEADS_KV, KV_DIM, KVW), lambda n: (0, 0, 0))
    sink = pl.BlockSpec((N_HEADS_KV, 8, PAIR), lambda n: (0, 0, 0))
    return qspec, cur, prev, rep, sink


def _attn_params():
    return pltpu.CompilerParams(dimension_semantics=("arbitrary",), vmem_limit_bytes=VMEM_LIMIT)


def attn_fwd(q, k, v, sinks_b):
    lp = q.shape[0]
    qspec, cur, prev, rep, sink = _attn_specs()

    def body(q_ref, kp_ref, kc_ref, vp_ref, vc_ref, rep_ref, sink_ref, o_ref):
        gmask, _, _, vals, p, _ = _attn_common(pl.program_id(0), q_ref, kp_ref[...], kc_ref[...], vp_ref[...],
                                               vc_ref[...], rep_ref, sink_ref)
        o = [_dot(x, y, 1, 0) for x, y in zip(p, vals)]
        for h in KV_HEADS:
            o_ref[:, h * KVW:(h + 1) * KVW] = _unstack_groups(o[h], gmask)

    return pl.pallas_call(
        body, name="swa_fwd", grid=(lp // BLOCK,), in_specs=[qspec, prev, cur, prev, cur, rep, sink],
        out_specs=qspec, out_shape=jax.ShapeDtypeStruct((lp, D_MODEL), F32), compiler_params=_attn_params(),
    )(q, k, k, v, v, _spread_matrices(), sinks_b)


def attn_bwd(q, k, v, sinks_b, do):
    lp = q.shape[0]
    qspec, cur, prev, rep, sink = _attn_specs()

    def body(q_ref, kp_ref, kc_ref, vp_ref, vc_ref, rep_ref, sink_ref, do_ref, dq_ref, dkc_ref, dkp_ref, dvc_ref,
             dvp_ref, dsink_ref):
        n = pl.program_id(0)
        gmask, q_s, keys, vals, p, p_sink = _attn_common(n, q_ref, kp_ref[...], kc_ref[...], vp_ref[...], vc_ref[...],
                                                         rep_ref, sink_ref)
        do_s = [jnp.concatenate([do_ref[:, h * KVW:(h + 1) * KVW] * gmask[g] for g in range(GROUP)], axis=0)
                for h in KV_HEADS]
        dp = [_dot(x, y, 1, 1) for x, y in zip(do_s, vals)]
        delta = [jnp.sum(x * y, axis=-1, keepdims=True) for x, y in zip(p, dp)]
        ds = [x * (y - z) * (HEAD_DIM ** -0.5) for x, y, z in zip(p, dp, delta)]
        dq = [_dot(x, y, 1, 0) for x, y in zip(ds, keys)]
        dkeys_s = [_dot(x, y, 0, 0) for x, y in zip(ds, q_s)]
        dvals_s = [_dot(x, y, 0, 0) for x, y in zip(p, do_s)]
        dkeys = [_exact_dot(x, rep_ref[h], cb=1) for h, x in enumerate(dkeys_s)]
        dvals = [_exact_dot(x, rep_ref[h], cb=1) for h, x in enumerate(dvals_s)]
        dk_all = (dkeys[0] + dkeys[1]) + (dkeys[2] + dkeys[3])
        dv_all = (dvals[0] + dvals[1]) + (dvals[2] + dvals[3])
        dkp_ref[...] = dk_all[:BLOCK]
        dkc_ref[...] = dk_all[BLOCK:]
        dvp_ref[...] = dv_all[:BLOCK]
        dvc_ref[...] = dv_all[BLOCK:]
        dsinks = []
        for h in KV_HEADS:
            dq_ref[:, h * KVW:(h + 1) * KVW] = _unstack_groups(dq[h], gmask)
            dsk = -(p_sink[h] * delta[h])
            rows = [jnp.broadcast_to(jnp.sum(dsk[g * BLOCK:(g + 1) * BLOCK], axis=0, keepdims=True), (1, PAIR))
                    for g in range(GROUP)]
            dsinks.append(jnp.concatenate(rows + [jnp.zeros((8 - GROUP, PAIR), F32)], axis=0))

        @pl.when(n == 0)
        def _():
            for h in KV_HEADS:
                dsink_ref[h] = dsinks[h]

        @pl.when(n > 0)
        def _():
            for h in KV_HEADS:
                dsink_ref[h] += dsinks[h]

    kv = jax.ShapeDtypeStruct((lp, KV_DIM), F32)
    return pl.pallas_call(
        body, name="swa_bwd", grid=(lp // BLOCK,), in_specs=[qspec, prev, cur, prev, cur, rep, sink, qspec],
        out_specs=[qspec, cur, cur, cur, cur, sink],
        out_shape=[jax.ShapeDtypeStruct((lp, D_MODEL), F32), kv, kv, kv, kv,
                   jax.ShapeDtypeStruct((N_HEADS_KV, 8, PAIR), F32)],
        compiler_params=_attn_params(),
    )(q, k, k, v, v, _spread_matrices(), sinks_b, do)


def _pick_tm(lp, want):
    for tm in (384, 192, 128, 64):
        if tm <= want and lp % tm == 0:
            return tm
    raise ValueError(lp)


def _acc(shape):
    return (tuple(shape), F32)


def _ff_all(w, layer):
    return (w, (N_FF_CHUNK, None, D_MODEL, D_MODEL), lambda c, i: (0, layer, 0, 0))


def _ff_one(w, layer):
    return (w, (None, None, D_MODEL, D_MODEL), lambda c, i: (c, layer, 0, 0))


def _mlp_layer_fwd(name, h, wup, wdown, layer, lg, lb, tm):
    def fn(c, i, h, wup, wdown, lg, lb):
        out = None
        for s in range(N_FF_CHUNK):
            t = mlp_chunk(wup[s], wdown[s], None, h)[0]
            out = t if out is None else out + t
        z = ALPHA * h + out
        return (_layer_norm(z, lg, lb), z), ()

    (h_out, z), _ = rowwise(name, fn, [h], [_ff_all(wup, layer), _ff_all(wdown, layer), lg, lb],
                            [(D_MODEL, F32), (D_MODEL, F32)], [], tm)
    return h_out, z


def _mlp_layer_bwd(name, h_in, z, dh_parts, wup, wdown, layer, lg, lb, tm):
    n_parts = len(dh_parts)

    def fn_ln(c, i, z, *rest):
        dh = rest[0]
        for extra in rest[1:n_parts]:
            dh = dh + extra
        _, vjp = jax.vjp(_layer_norm, z, rest[n_parts], rest[n_parts + 1])
        dz, dlg, dlb = vjp(dh)
        return (dz,), (dlg, dlb)

    (dz,), (dlg, dlb) = rowwise(name + "_ln", fn_ln, [z] + list(dh_parts), [lg, lb], [(D_MODEL, F32)],
                                [_acc((1, D_MODEL)), _acc((1, D_MODEL))], tm)

    def fn_mlp(c, i, h, dz, wup, wdown):
        tile = h.shape[0]
        (dx,), dws = vjp_taps(functools.partial(mlp_chunk, wup, wdown), [(tile, D_MODEL)] * 2, [h], dz)
        return (dx,), dws

    aspec = ((N_FF_CHUNK, D_MODEL, D_MODEL), F32, (None, D_MODEL, D_MODEL), lambda c, i: (c, 0, 0))
    (dx,), (dwup, dwdown) = rowwise(name + "_mm", fn_mlp, [h_in, dz], [_ff_one(wup, layer), _ff_one(wdown, layer)],
                                    [(D_MODEL, F32, True)], [aspec, aspec], tm, nc=N_FF_CHUNK)
    return dz, dx, dwup, dwdown, dlg, dlb


def _sum_parts(dz, dx):
    out = ALPHA * dz
    for s in range(N_FF_CHUNK):
        out = out + dx[s]
    return out


def local_step(x, loss_target, p, late=None):
    seq = x.shape[0]
    lp = TOK0 + seq
    tm = _pick_tm(lp, 384)
    tms = _pick_tm(lp, 128)
    e, et = _head_matrices()
    h0 = jnp.concatenate([jnp.zeros((PAD_FRONT, D_MODEL), F32), p["meta_tokens"], x], axis=0)
    hp = jnp.concatenate([jnp.zeros((1, D_MODEL), F32), h0[:-1]], axis=0)
    tgt = jnp.concatenate([jnp.zeros((TOK0, D_MODEL), F32), loss_target], axis=0)
    pos = jnp.maximum(jnp.arange(lp, dtype=F32) - PAD_FRONT, 0.0)
    inv_freq = 1.0 / (ROPE_THETA ** (jnp.arange(0, HEAD_DIM, 2, dtype=F32) / HEAD_DIM))
    ang = pos[:, None] * inv_freq[None, :]
    cos = jnp.tile(jnp.cos(ang), (1, PAIR // (HEAD_DIM // 2)))
    sin = jnp.tile(jnp.sin(ang), (1, PAIR // (HEAD_DIM // 2)))

    pre_vec = [p["a_mu"][j:j + 1] for j in range(6)] + [p["a_w0"], p["a_a0"], p["a_k_k"], p["a_k_a"]]
    pre_w = [p["a_w_r"], p["a_w_k"], p["a_w_v"], p["a_w1"], p["a_w2"], p["a_a1"], p["a_a2"], p["a_g1"], p["a_g2"]]
    n_vec = len(pre_vec)

    def fn_pre(c, i, h, hp, e, et, *ws):
        return rwkv_pre(e, et, ws[n_vec:], None, h, hp, *ws[:n_vec])[0], ()

    (r, lw, k2, v, an, bn, g), _ = rowwise("rwkv_pre", fn_pre, [h0, hp], [e, et] + pre_vec + pre_w,
                                           [(D_MODEL, F32)] * 7, [], tms)
    (y, *scan_saved), late_gathered = scan_fwd(r, lw, k2, v, an, bn, late[0] if late else ())
    if late:
        p = {**p, **late[1](late_gathered)}

    post_c =[p["a_w_o"], p["a_gn_w"], p["a_gn_b"], p["a_r_k"], p["ln_g00"], p["ln_b00"]]

    def fn_post(c, i, y, r, k2, v, g, h0, e, et, w_o, *vecs):
        return (rwkv_post(e, et, w_o, None, y, r, k2, v, g, h0, *vecs)[0],), ()

    (h1,), _ = rowwise("rwkv_post", fn_post, [y, r, k2, v, g, h0], [e, et] + post_c, [(D_MODEL, F32)], [], tm)
    h2, z2 = _mlp_layer_fwd("mlp0_fwd", h1, p["mlp_up"], p["mlp_down"], 0, p["ln_g01"], p["ln_b01"], tm)

    qkv_w = [p["b_w_q"], p["kv_w_k"], p["kv_w_v"]]

    def fn_qkv(c, i, h, cos, sin, wq, wk, wv):
        return qkv_proj(cos, sin, wq, wk, wv, None, h)[0], ()

    (q, k, vv), _ = rowwise("qkv_proj", fn_qkv, [h2, cos, sin], qkv_w,
                            [(D_MODEL, F32), (KV_DIM, F32), (KV_DIM, F32)], [], tm)
    sinks_b = jnp.broadcast_to(p["b_sinks"].reshape(N_HEADS_KV, GROUP, 1), (N_HEADS_KV, GROUP, PAIR))
    sinks_b = jnp.concatenate([sinks_b, jnp.zeros((N_HEADS_KV, 8 - GROUP, PAIR), F32)], axis=1)
    o = attn_fwd(q, k, vv, sinks_b)

    ao_c = [p["b_w_o"], p["ln_g10"], p["ln_b10"]]

    def fn_ao(c, i, o, h, w_o, lg, lb):
        return (attn_out(w_o, None, o, h, lg, lb)[0],), ()

    (h3,), _ = rowwise("attn_out", fn_ao, [o, h2], ao_c, [(D_MODEL, F32)], [], tm)
    h4, z4 = _mlp_layer_fwd("mlp1_fwd", h3, p["mlp_up"], p["mlp_down"], 1, p["ln_g11"], p["ln_b11"], tm)

    def fn_loss(c, i, h4, tgt):
        real = (_row_ids(i, tm) >= TOK0).astype(F32)
        err = (h4 - tgt) * real
        part = 0.5 * jnp.sum(jnp.sum(err * err, axis=-1, keepdims=True), axis=0, keepdims=True) / D_MODEL
        return (err * (1.0 / D_MODEL),), (jnp.broadcast_to(part, (8, PAIR)),)

    (dh4,), (loss_acc,) = rowwise("loss", fn_loss, [h4, tgt], [], [(D_MODEL, F32)], [_acc((8, PAIR))], tm)
    loss = loss_acc[0, 0]

    grads = {}
    dz4, dx4, grads["mlp_up1"], grads["mlp_down1"], grads["ln_g11"], grads["ln_b11"] = _mlp_layer_bwd(
        "mlp1_bwd", h3, z4, [dh4], p["mlp_up"], p["mlp_down"], 1, p["ln_g11"], p["ln_b11"], tm)

    def fn_ao_b(c, i, dz, dx, o, h, w_o, lg, lb):
        (do, dh, dlg, dlb), (dw_o,) = vjp_taps(functools.partial(attn_out, w_o), [(tms, D_MODEL)], [o, h, lg, lb],
                                               _sum_parts(dz, dx))
        return (do, dh), (dw_o, dlg, dlb)

    (do, dh2_a), (grads["b_w_o"], grads["ln_g10"], grads["ln_b10"]) = rowwise(
        "attn_out_bwd", fn_ao_b, [dz4, dx4, o, h2], ao_c, [(D_MODEL, F32)] * 2,
        [_acc((D_MODEL, D_MODEL)), _acc((1, D_MODEL)), _acc((1, D_MODEL))], tms)

    dq, dkc, dkp, dvc, dvp, dsinks = attn_bwd(q, k, vv, sinks_b, do)
    grads["b_sinks"] = dsinks[:, :GROUP, 0].reshape(1, N_HEADS)
    zblk = jnp.zeros((BLOCK, KV_DIM), F32)
    dkp_s = jnp.concatenate([dkp[BLOCK:], zblk], axis=0)
    dvp_s = jnp.concatenate([dvp[BLOCK:], zblk], axis=0)

    def fn_qkv_b(c, i, h, cos, sin, dq, dkc, dkp, dvc, dvp, wq, wk, wv):
        return vjp_taps(functools.partial(qkv_proj, cos, sin, wq, wk, wv),
                        [(tms, D_MODEL), (tms, KV_DIM), (tms, KV_DIM)], [h], (dq, dkc + dkp, dvc + dvp))

    (dh2_q,), (grads["b_w_q"], grads["kv_w_k"], grads["kv_w_v"]) = rowwise(
        "qkv_proj_bwd", fn_qkv_b, [h2, cos, sin, dq, dkc, dkp_s, dvc, dvp_s], qkv_w, [(D_MODEL, F32)],
        [_acc((D_MODEL, D_MODEL)), _acc((D_MODEL, KV_DIM)), _acc((D_MODEL, KV_DIM))], tms)

    dz2, dx2, grads["mlp_up0"], grads["mlp_down0"], grads["ln_g01"], grads["ln_b01"] = _mlp_layer_bwd(
        "mlp0_bwd", h1, z2, [dh2_a, dh2_q], p["mlp_up"], p["mlp_down"], 0, p["ln_g01"], p["ln_b01"], tm)

    def fn_post_b(c, i, dz, dx, y, r, k2, v, g, h0, e, et, w_o, *vecs):
        out, dws = vjp_taps(functools.partial(rwkv_post, e, et, w_o), [(tms, D_MODEL)],
                            [y, r, k2, v, g, h0] + list(vecs), _sum_parts(dz, dx))
        return out[:6], tuple(dws) + tuple(out[6:])

    (dy, dr_c, dk_c, dv_c, dg, dh0_c), post_g = rowwise(
        "rwkv_post_bwd", fn_post_b, [dz2, dx2, y, r, k2, v, g, h0], [e, et] + post_c, [(D_MODEL, F32)] * 6,
        [_acc((D_MODEL, D_MODEL))] + [_acc((1, D_MODEL))] * 5, tms)
    for name, val in zip(["a_w_o", "a_gn_w", "a_gn_b", "a_r_k", "ln_g00", "ln_b00"], post_g):
        grads[name] = val

    dr_s, dlw, dk_s, dv_s, dan, dbn = scan_bwd(r, lw, k2, v, an, bn, scan_saved, dy)

    def fn_pre_b(c, i, h, hp, dr_c, dr_s, dlw, dk_c, dk_s, dv_c, dv_s, dan, dbn, dg, e, et, *ws):
        real = (_row_ids(i, tms) >= PAD_FRONT).astype(F32)
        cot = tuple(t * real for t in (dr_c + dr_s, dlw, dk_c + dk_s, dv_c + dv_s, dan, dbn, dg))
        out, dws = vjp_taps(functools.partial(rwkv_pre, e, et, ws[n_vec:]), [(tms, n) for n in PRE_TAPS],
                            [h, hp] + list(ws[:n_vec]), cot)
        return out[:2], tuple(out[2:]) + tuple(dws)

    (dh0_p, dhp), pre_g = rowwise(
        "rwkv_pre_bwd", fn_pre_b, [h0, hp, dr_c, dr_s, dlw, dk_c, dk_s, dv_c, dv_s, dan, dbn, dg],
        [e, et] + pre_vec + pre_w, [(D_MODEL, F32)] * 2,
        [_acc((1, D_MODEL))] * n_vec + [_acc(w.shape) for w in pre_w], tms)
    grads["a_mu"] = jnp.concatenate(pre_g[:6], axis=0)
    for name, val in zip(["a_w0", "a_a0", "a_k_k", "a_k_a", "a_w_r", "a_w_k", "a_w_v", "a_w1", "a_w2", "a_a1",
                          "a_a2", "a_g1", "a_g2"], pre_g[6:]):
        grads[name] = val

    dhp_s = jnp.concatenate([dhp[1:], jnp.zeros((1, D_MODEL), F32)], axis=0)

    def fn_add(c, i, a, b, d):
        return (a + b + d,), ()

    (dh0,), _ = rowwise("grad_h0", fn_add, [dh0_c, dh0_p, dhp_s], [], [(D_MODEL, F32)], [], tm)
    grads["meta_tokens"] = dh0[PAD_FRONT:TOK0]
    return loss, dh0[TOK0:], grads


ANY = pl.BlockSpec(memory_space=pl.ANY)
XY_FLIPS = ((0, 1), (1, 0), (1, 1))
ALL_FLIPS = tuple((e >> 2 & 1, e >> 1 & 1, e & 1) for e in range(1, N_DEV))


def _flip(v, bit):
    return 1 - v if bit else v


def _sem_scratch(n):
    return [pltpu.SemaphoreType.DMA((n,)), pltpu.SemaphoreType.DMA((n,))]


def gather_copies(src, dst, send_sems, recv_sems):
    npeer = len(XY_FLIPS)
    x, y, c = lax.axis_index("x"), lax.axis_index("y"), lax.axis_index("c")

    def copy(k, j, slot):
        fx, fy = XY_FLIPS[j]
        return pltpu.make_async_remote_copy(
            src_ref=src[k], dst_ref=dst[k].at[slot], send_sem=send_sems.at[k * npeer + j],
            recv_sem=recv_sems.at[k * npeer + j], device_id=(_flip(x, fx), _flip(y, fy), c), device_id_type=MESH)

    sends = [copy(k, j, 2 * x + y) for k in range(len(src)) for j in range(npeer)]
    arrivals = [copy(k, j, 2 * _flip(x, fx) + _flip(y, fy)) for k in range(len(src))
                for j, (fx, fy) in enumerate(XY_FLIPS)]
    return sends, arrivals


def gathered_shapes(shards):
    return [jax.ShapeDtypeStruct((N_SHARD,) + s.shape, s.dtype) for s in shards]


def fill_own(gathered, shards):
    slot = 2 * lax.axis_index("x") + lax.axis_index("y")
    return [lax.dynamic_update_index_in_dim(g, s, slot, 0) for g, s in zip(gathered, shards)]


def all_gather_shards(shards):
    n = len(shards)

    def body(*refs):
        sends, arrivals = gather_copies(refs[:n], refs[n:2 * n], *refs[2 * n:])
        for cp in sends:
            cp.start()
        for cp in arrivals:
            cp.wait_recv()
        for cp in sends:
            cp.wait_send()

    out = pl.pallas_call(body, name="gather_weights", in_specs=[ANY] * n, out_specs=[ANY] * n,
                         out_shape=gathered_shapes(shards), scratch_shapes=_sem_scratch(n * len(XY_FLIPS)))(*shards)
    return fill_own(out, shards)


def placement():
    x, y, c = lax.axis_index("x"), lax.axis_index("y"), lax.axis_index("c")
    me = 2 * x + y
    others = [j + (j >= me).astype(jnp.int32) for j in range(N_SHARD - 1)]
    return jnp.stack([c, me] + others).astype(jnp.int32)


def pair_exchange(sources):
    n = len(sources)

    def body(*refs):
        src, got = refs[:n], refs[n:2 * n]
        send_sems, recv_sems = refs[2 * n:]
        x, y, c = lax.axis_index("x"), lax.axis_index("y"), lax.axis_index("c")

        def copy(k):
            half = sources[k].shape[1] // 2
            theirs = src[k].at[:, pl.ds(pl.multiple_of((1 - c) * half, 8), half), :]
            return pltpu.make_async_remote_copy(
                src_ref=theirs, dst_ref=got[k], send_sem=send_sems.at[k], recv_sem=recv_sems.at[k],
                device_id=(x, y, 1 - c), device_id_type=MESH)

        sends = [copy(k) for k in range(n)]
        for cp in sends:
            cp.start()
        for cp in sends:
            cp.wait_recv()
        for cp in sends:
            cp.wait_send()

    halves = [jax.ShapeDtypeStruct((s.shape[0], s.shape[1] // 2, s.shape[2]), s.dtype) for s in sources]
    return pl.pallas_call(body, name="grads_pair_exchange", in_specs=[ANY] * n, out_specs=[ANY] * n,
                          out_shape=halves, scratch_shapes=_sem_scratch(n))(*sources)


def chip_exchange(parts):
    n = len(parts)
    npeer = len(XY_FLIPS)

    def body(*refs):
        src, dst = refs[:n], refs[n:2 * n]
        send_sems, recv_sems = refs[2 * n:]
        x, y, c = lax.axis_index("x"), lax.axis_index("y"), lax.axis_index("c")
        me = 2 * x + y

        def copy(k, j, sending):
            fx, fy = XY_FLIPS[j]
            px, py = _flip(x, fx), _flip(y, fy)
            peer = 2 * px + py
            return pltpu.make_async_remote_copy(
                src_ref=src[k].at[peer], dst_ref=dst[k].at[me if sending else peer],
                send_sem=send_sems.at[k * npeer + j], recv_sem=recv_sems.at[k * npeer + j],
                device_id=(px, py, c), device_id_type=MESH)

        sends = [copy(k, j, True) for k in range(n) for j in range(npeer)]
        for cp in sends:
            cp.start()
        for k in range(n):
            for j in range(npeer):
                copy(k, j, False).wait_recv()
        for cp in sends:
            cp.wait_send()

    return pl.pallas_call(
        body, name="grads_chip_exchange", in_specs=[ANY] * n, out_specs=[ANY] * n,
        out_shape=[jax.ShapeDtypeStruct(p.shape, p.dtype) for p in parts], scratch_shapes=_sem_scratch(n * npeer),
    )(*parts)


def sibling_share(halves):
    n = len(halves)

    def body(*refs):
        src, got = refs[:n], refs[n:2 * n]
        send_sems, recv_sems = refs[2 * n:]
        x, y, c = lax.axis_index("x"), lax.axis_index("y"), lax.axis_index("c")
        sends = [pltpu.make_async_remote_copy(
            src_ref=src[k], dst_ref=got[k], send_sem=send_sems.at[k], recv_sem=recv_sems.at[k],
            device_id=(x, y, 1 - c), device_id_type=MESH) for k in range(n)]
        for cp in sends:
            cp.start()
        for cp in sends:
            cp.wait_recv()
        for cp in sends:
            cp.wait_send()

    return pl.pallas_call(
        body, name="grads_sibling_share", in_specs=[ANY] * n, out_specs=[ANY] * n,
        out_shape=[jax.ShapeDtypeStruct(h.shape, h.dtype) for h in halves], scratch_shapes=_sem_scratch(n),
    )(*halves)


ADD_TILE_ELEMS = 512 * 1024


def _row_tile(rows, cols):
    return max(t for t in range(8, rows + 1, 8) if rows % t == 0 and t * cols <= ADD_TILE_ELEMS)


def _prefetch_call(body, name, place, grid, in_specs, out_specs, out_shape, args):
    return pl.pallas_call(
        body, name=name, out_shape=out_shape,
        grid_spec=pltpu.PrefetchScalarGridSpec(num_scalar_prefetch=1, grid=grid, in_specs=in_specs,
                                               out_specs=out_specs),
        compiler_params=pltpu.CompilerParams(dimension_semantics=("arbitrary",) * len(grid),
                                             vmem_limit_bytes=VMEM_LIMIT),
    )(place, *args)


def pair_add(name, place, src, got, dtype):
    n4, half, cols = got.shape
    tile = _row_tile(half, cols)
    nt = half // tile

    def body(pr, a_ref, b_ref, o_ref):
        o_ref[...] = (a_ref[...] + b_ref[...]).astype(o_ref.dtype)

    mine = pl.BlockSpec((None, tile, cols), lambda s, i, pr: (s, pr[0] * nt + i, 0))
    blk = pl.BlockSpec((None, tile, cols), lambda s, i, pr: (s, i, 0))
    return _prefetch_call(body, name, place, (n4, nt), [mine, blk], blk,
                          jax.ShapeDtypeStruct(got.shape, dtype), (src, got))


def chip_add(name, place, part, from_chips):
    _, half, cols = part.shape
    tile = _row_tile(half, cols)

    def body(pr, own_ref, r0_ref, r1_ref, r2_ref, o_ref):
        me = pr[1]
        own, r0, r1, r2 = (r[...].astype(F32) for r in (own_ref, r0_ref, r1_ref, r2_ref))
        t0 = jnp.where(me == 0, own, r0)
        t1 = jnp.where(me == 0, r0, jnp.where(me == 1, own, r1))
        t2 = jnp.where(me <= 1, r1, jnp.where(me == 2, own, r2))
        t3 = jnp.where(me == 3, own, r2)
        o_ref[...] = ((t0 + t1) + t2) + t3

    def slab(j):
        return pl.BlockSpec((None, tile, cols), lambda i, pr: (pr[j], i, 0))

    return _prefetch_call(body, name, place, (half // tile,), [slab(1), slab(2), slab(3), slab(4)],
                          pl.BlockSpec((tile, cols), lambda i, pr: (i, 0)),
                          jax.ShapeDtypeStruct((half, cols), F32), (part, from_chips, from_chips, from_chips))


def reduce_grads(sources, narrow):
    place = placement()
    got = pair_exchange(sources)
    parts = [pair_add(f"grads_pair_add{k}", place, s, g, BF16 if nar else F32)
             for k, (s, g, nar) in enumerate(zip(sources, got, narrow))]
    from_chips = chip_exchange(parts)
    halves = [chip_add(f"grads_chip_add{k}", place, p, f) for k, (p, f) in enumerate(zip(parts, from_chips))]
    return place, halves, sibling_share(halves)


ADAM_ROWS = 256


def adamw_update(name, place, halves, w, m, v):
    nsub, rows, cols = w.shape
    half = rows // 2
    tr = ADAM_ROWS if half % ADAM_ROWS == 0 else half
    nth = half // tr

    def body(pr, *refs):
        g_refs, (w_ref, m_ref, v_ref, g_ref, d_ref, nm_ref, nv_ref) = refs[:2 * nsub], refs[2 * nsub:]
        l = pl.program_id(0)
        mine = (pl.program_id(1) // nth) == pr[0]
        g = None
        for s in range(nsub):
            gs = jnp.where(mine, g_refs[2 * s][...], g_refs[2 * s + 1][...])
            g = gs if g is None else jnp.where(l == s, gs, g)
        m2 = ADAM_B1 * m_ref[...] + (1.0 - ADAM_B1) * g
        v2 = ADAM_B2 * v_ref[...] + (1.0 - ADAM_B2) * (g * g)
        m_hat = m2 / (1.0 - ADAM_B1 ** ADAM_STEP)
        v_hat = v2 / (1.0 - ADAM_B2 ** ADAM_STEP)
        g_ref[...] = g
        d_ref[...] = -ADAM_LR * (m_hat / (jnp.sqrt(v_hat) + ADAM_EPS) + ADAM_WD * w_ref[...])
        nm_ref[...] = m2
        nv_ref[...] = v2

    gblk = pl.BlockSpec((tr, cols), lambda l, i, pr: (i % nth, 0))
    blk = pl.BlockSpec((None, tr, cols), lambda l, i, pr: (l, i, 0))
    out = jax.ShapeDtypeStruct((nsub, rows, cols), F32)
    return _prefetch_call(body, name, place, (nsub, rows // tr), [gblk] * (2 * nsub) + [blk] * 3, [blk] * 4,
                          [out] * 4, [h for pair in halves for h in pair] + [w, m, v])


WEIGHT_NAMES = ("meta_tokens", "a_mu", "a_w_r", "a_w_k", "a_w_v", "a_w_o", "a_w0", "a_w1", "a_w2", "a_a0", "a_a1",
                "a_a2", "a_g1", "a_g2", "a_k_k", "a_k_a", "a_r_k", "a_gn_w", "a_gn_b", "kv_w_k", "kv_w_v", "b_w_q",
                "b_sinks", "b_w_o", "mlp_w_up", "mlp_w_down", "ln_g", "ln_b")
BIG_NAMES = ("a_w_r", "a_w_k", "a_w_v", "a_w_o", "b_w_q", "b_w_o")
EARLY_NAMES, LATE_NAMES = BIG_NAMES[:3], BIG_NAMES[3:]
PACK_MATS = (("kv_w_k", 256), ("kv_w_v", 256), ("a_w1", 64), ("a_a1", 64), ("a_g1", 128), ("a_w2", 64),
             ("a_a2", 64), ("a_g2", 128))
COLUMN_CUT = ("a_w2", "a_a2", "a_g2")
PACK_VECS = (("a_mu", 6), ("a_w0", 1), ("a_a0", 1), ("a_k_k", 1), ("a_k_a", 1), ("a_gn_w", 1), ("a_gn_b", 1),
             ("ln_g", 4), ("ln_b", 4), ("meta_tokens", 16))
PACK_REPL = (("a_r_k", 4), ("b_sinks", 1))
SHARD_W = D_MODEL // N_SHARD
N_MAT_ROWS = sum(r for _, r in PACK_MATS)
N_VEC_ROWS = sum(r for _, r in PACK_VECS)
N_PACK_ROWS = -(-(N_MAT_ROWS + N_VEC_ROWS + sum(r for _, r in PACK_REPL)) // 8) * 8
N_GATHER_VEC_ROWS = -(-N_VEC_ROWS // 8) * 8


def _pack_rows(arr):
    if arr.size == N_HEADS:
        return jnp.pad(arr.reshape(1, N_HEADS), ((0, 0), (0, SHARD_W - N_HEADS)))
    return arr.reshape(-1, SHARD_W)


def pack_small(get):
    parts = [_pack_rows(get(name)) for name, _ in PACK_MATS + PACK_VECS + PACK_REPL]
    used = sum(p.shape[0] for p in parts)
    return jnp.concatenate(parts + [jnp.zeros((N_PACK_ROWS - used, SHARD_W), F32)], axis=0)


def unpack_small(pack, shapes):
    out, off = {}, 0
    for name, rows in PACK_MATS + PACK_VECS + PACK_REPL:
        piece = pack[off:off + rows]
        off += rows
        out[name] = piece[:, :N_HEADS].reshape(shapes[name]) if name == "b_sinks" else piece.reshape(shapes[name])
    return out


def whole_weights(big_names, gathered_big, mats, vecs, a_r_k, b_sinks):
    p = {name: g.reshape(D_MODEL, D_MODEL) for name, g in zip(big_names, gathered_big)}
    off = 0
    for name, rows in PACK_MATS:
        piece = mats[:, off:off + rows]
        off += rows
        if name in COLUMN_CUT:
            p[name] = piece.transpose(1, 0, 2).reshape(rows, D_MODEL)
        else:
            p[name] = piece.reshape(D_MODEL, rows)
    v = vecs.transpose(1, 0, 2).reshape(-1, D_MODEL)
    off = 0
    for name, rows in PACK_VECS:
        p[name] = v[off:off + rows]
        off += rows
    for i in range(2):
        for j in range(2):
            p[f"ln_g{i}{j}"] = p["ln_g"][2 * i + j:2 * i + j + 1]
            p[f"ln_b{i}{j}"] = p["ln_b"][2 * i + j:2 * i + j + 1]
    p["a_r_k"] = a_r_k.reshape(1, D_MODEL)
    p["b_sinks"] = b_sinks
    return p


def small_grad_pack(g):
    parts = []
    for name, rows in PACK_MATS:
        if name in COLUMN_CUT:
            parts.append(g[name].reshape(rows, N_SHARD, SHARD_W).transpose(1, 0, 2))
        else:
            parts.append(g[name].reshape(N_SHARD, rows, SHARD_W))
    vec_rows = [g["a_mu"]] + [g[n] for n in ("a_w0", "a_a0", "a_k_k", "a_k_a", "a_gn_w", "a_gn_b")]
    vec_rows += [g[f"ln_g{i}{j}"] for i in range(2) for j in range(2)]
    vec_rows += [g[f"ln_b{i}{j}"] for i in range(2) for j in range(2)] + [g["meta_tokens"]]
    parts.append(jnp.concatenate(vec_rows, axis=0).reshape(N_VEC_ROWS, N_SHARD, SHARD_W).transpose(1, 0, 2))
    parts.append(jnp.broadcast_to(g["a_r_k"].reshape(1, -1, SHARD_W), (N_SHARD, D_MODEL // SHARD_W, SHARD_W)))
    sinks = jnp.pad(g["b_sinks"].reshape(1, 1, N_HEADS), ((0, 0), (0, 0), (0, SHARD_W - N_HEADS)))
    parts.append(jnp.broadcast_to(sinks, (N_SHARD, 1, SHARD_W)))
    used = sum(p.shape[1] for p in parts)
    parts.append(jnp.zeros((N_SHARD, N_PACK_ROWS - used, SHARD_W), F32))
    return jnp.concatenate(parts, axis=1)


def train_step(vals):
    w = {n: vals[n] for n in WEIGHT_NAMES}
    w_pack = pack_small(lambda n: w[n])
    early = [w[n][0].astype(BF16) for n in EARLY_NAMES]
    early += [w_pack[:N_MAT_ROWS].astype(BF16), w_pack[N_MAT_ROWS:N_MAT_ROWS + N_GATHER_VEC_ROWS]]
    gathered = all_gather_shards(early)
    ne = len(EARLY_NAMES)
    p = whole_weights(EARLY_NAMES, gathered[:ne], gathered[ne], gathered[ne + 1][:, :N_VEC_ROWS], w["a_r_k"],
                      w["b_sinks"])
    late = [w[n][0].astype(BF16) for n in LATE_NAMES] + [w["mlp_w_up"].astype(BF16), w["mlp_w_down"].astype(BF16)]
    nb = len(BIG_NAMES)

    def late_weights(got):
        out = {n: x.reshape(D_MODEL, D_MODEL) for n, x in zip(LATE_NAMES, got)}
        out["mlp_up"], out["mlp_down"] = got[len(LATE_NAMES):]
        return out

    loss, gx, g = local_step(vals["x"][0], vals["loss_target"][0], p, (late, late_weights))
    loss = lax.psum(loss, ("x", "y", "c"))

    sources = [g[n].reshape(N_SHARD, SHARD_W, D_MODEL) for n in BIG_NAMES]
    sources += [g["mlp_up0"], g["mlp_up1"], g["mlp_down0"], g["mlp_down1"], small_grad_pack(g)]
    place, mine, theirs = reduce_grads(sources, [True] * (len(sources) - 1) + [False])
    halves = list(zip(mine, theirs))

    res = {}
    for k, n in enumerate(BIG_NAMES):
        res[n] = adamw_update("adamw_" + n, place, halves[k:k + 1], w[n], vals["m_" + n], vals["v_" + n])
    for k, n in ((nb, "mlp_w_up"), (nb + 2, "mlp_w_down")):
        res[n] = adamw_update("adamw_" + n, place, halves[k:k + 2], w[n], vals["m_" + n], vals["v_" + n])
    packs = adamw_update("adamw_small", place, halves[-1:], w_pack[None], pack_small(lambda n: vals["m_" + n])[None],
                         pack_small(lambda n: vals["v_" + n])[None])
    shapes = {n: w[n].shape for n in WEIGHT_NAMES}
    small = [unpack_small(pk[0], shapes) for pk in packs]
    outs = [loss, gx[None]]
    for t in range(4):
        outs += [res[n][t] if n in res else small[t][n] for n in WEIGHT_NAMES]
    return tuple(outs)


def kernel(x, meta_tokens, a_mu, a_w_r, a_w_k, a_w_v, a_w_o, a_w0, a_w1, a_w2, a_a0, a_a1, a_a2, a_g1, a_g2, a_k_k,
           a_k_a, a_r_k, a_gn_w, a_gn_b, kv_w_k, kv_w_v, b_w_q, b_sinks, b_w_o, mlp_w_up, mlp_w_down, ln_g, ln_b,
           loss_target, m_meta_tokens, m_a_mu, m_a_w_r, m_a_w_k, m_a_w_v, m_a_w_o, m_a_w0, m_a_w1, m_a_w2, m_a_a0,
           m_a_a1, m_a_a2, m_a_g1, m_a_g2, m_a_k_k, m_a_k_a, m_a_r_k, m_a_gn_w, m_a_gn_b, m_kv_w_k, m_kv_w_v,
           m_b_w_q, m_b_sinks, m_b_w_o, m_mlp_w_up, m_mlp_w_down, m_ln_g, m_ln_b, v_meta_tokens, v_a_mu, v_a_w_r,
           v_a_w_k, v_a_w_v, v_a_w_o, v_a_w0, v_a_w1, v_a_w2, v_a_a0, v_a_a1, v_a_a2, v_a_g1, v_a_g2, v_a_k_k,
           v_a_k_a, v_a_r_k, v_a_gn_w, v_a_gn_b, v_kv_w_k, v_kv_w_v, v_b_w_q, v_b_sinks, v_b_w_o, v_mlp_w_up,
           v_mlp_w_down, v_ln_g, v_ln_b):
    return train_step(dict(locals()))
```

```python
import functools

import numpy as np
import jax
import jax.numpy as jnp
from jax import lax
from jax.experimental import pallas as pl
from jax.experimental.pallas import tpu as pltpu

F32 = jnp.float32
BF16 = jnp.bfloat16

D_MODEL = 1024
N_HEADS = 16
HEAD_DIM = 64
N_HEADS_KV = 4
GROUP = 4
KV_DIM = N_HEADS_KV * HEAD_DIM
N_META = 16
BLOCK = 128
PAD_FRONT = BLOCK - N_META
TOK0 = PAD_FRONT + N_META
N_FF_CHUNK = 4
N_SHARD = 4
N_DEV = 8
GN_EPS = 64e-5
LN_EPS = 1e-5
ROPE_THETA = 10000.0
ALPHA = 4.0 ** 0.25
ADAM_LR, ADAM_B1, ADAM_B2, ADAM_EPS, ADAM_WD, ADAM_STEP = 0.001, 0.9, 0.999, 1e-08, 0.01, 10
SCAN_T = 64
PAIR = 128
KVW = GROUP * HEAD_DIM
VMEM_LIMIT = 56 * 1024 * 1024
HI = lax.Precision.HIGHEST
MESH = pl.DeviceIdType.MESH


def _dot(a, b, ca, cb):
    return lax.dot_general(a.astype(BF16), b.astype(BF16), (((ca,), (cb,)), ((), ())),
                           preferred_element_type=F32)


@jax.custom_vjp
def mm(a, b):
    return _dot(a, b, 1, 0)


def _mm_fwd(a, b):
    return mm(a, b), b


def _mm_bwd(b, g):
    return _dot(g, b, 1, 1), jnp.zeros_like(b)


mm.defvjp(_mm_fwd, _mm_bwd)


def tmm(x, w, taps, xs):
    y = mm(x, w)
    if taps is not None:
        y = y + taps[len(xs)]
    xs.append(x)
    return y


def vjp_taps(core, tap_shapes, args, cot):
    taps = [jnp.zeros(s, F32) for s in tap_shapes]
    _, vjp, xs = jax.vjp(core, taps, *args, has_aux=True)
    out = vjp(cot)
    return out[1:], [_dot(x, g, 0, 0) for x, g in zip(xs, out[0])]


def _split3(x):
    x1 = x.astype(BF16)
    r1 = x - x1.astype(F32)
    x2 = r1.astype(BF16)
    x3 = (r1 - x2.astype(F32)).astype(BF16)
    return x1, x2, x3


def _exact_dot(x, m01, cb=0):
    acc = None
    for piece in _split3(x):
        t = lax.dot_general(piece, m01, (((1,), (cb,)), ((), ())), preferred_element_type=F32)
        acc = t if acc is None else acc + t
    return acc


def _head_matrices():
    e = np.zeros((D_MODEL, N_HEADS), np.float32)
    e[np.arange(D_MODEL), np.arange(D_MODEL) // HEAD_DIM] = 1.0
    return jnp.asarray(e, BF16), jnp.asarray(e.T, BF16)


@jax.custom_vjp
def hsum(x, e, et):
    return _exact_dot(x, e)


@jax.custom_vjp
def hbc(s, e, et):
    return _exact_dot(s, et)


hsum.defvjp(lambda x, e, et: (_exact_dot(x, e), (e, et)),
            lambda res, g: (hbc(g, *res), jnp.zeros_like(res[0]), jnp.zeros_like(res[1])))
hbc.defvjp(lambda s, e, et: (_exact_dot(s, et), (e, et)),
           lambda res, g: (hsum(g, *res), jnp.zeros_like(res[0]), jnp.zeros_like(res[1])))


def _sigmoid(u):
    return 0.5 * (jnp.tanh(0.5 * u) + 1.0)


def _softplus(u):
    return jnp.maximum(u, 0.0) + jnp.log(1.0 + jnp.exp(-jnp.abs(u)))


def _layer_norm(z, g, b):
    mu = jnp.mean(z, axis=-1, keepdims=True)
    zc = z - mu
    var = jnp.mean(zc * zc, axis=-1, keepdims=True)
    return zc * lax.rsqrt(var + LN_EPS) * g + b


def _zero_map(nd):
    return lambda c, i: (0,) * nd


def _params():
    return pltpu.CompilerParams(dimension_semantics=("arbitrary", "arbitrary"), vmem_limit_bytes=VMEM_LIMIT)


def rowwise(name, fn, rows, consts, out_rows, out_accs, tm, nc=1):
    lp = rows[0].shape[-2]
    nt = lp // tm
    assert nt * tm == lp, (name, lp, tm)
    in_specs, args = [], []
    for a in rows:
        if a.ndim == 2:
            in_specs.append(pl.BlockSpec((tm, a.shape[1]), lambda c, i: (i, 0)))
        else:
            in_specs.append(pl.BlockSpec((a.shape[0], tm, a.shape[2]), lambda c, i: (0, i, 0)))
        args.append(a)
    for cst in consts:
        if isinstance(cst, tuple):
            arr, bs, im = cst
            in_specs.append(pl.BlockSpec(bs, im))
        else:
            arr = cst
            in_specs.append(pl.BlockSpec(arr.shape, _zero_map(arr.ndim), pipeline_mode=pl.Buffered(1)))
        args.append(arr)
    out_shape, out_specs, acc_per_chunk = [], [], []
    for spec in out_rows:
        if len(spec) == 3 and spec[2]:
            out_shape.append(jax.ShapeDtypeStruct((nc, lp, spec[0]), spec[1]))
            out_specs.append(pl.BlockSpec((None, tm, spec[0]), lambda c, i: (c, i, 0)))
        else:
            out_shape.append(jax.ShapeDtypeStruct((lp, spec[0]), spec[1]))
            out_specs.append(pl.BlockSpec((tm, spec[0]), lambda c, i: (i, 0)))
    for spec in out_accs:
        out_shape.append(jax.ShapeDtypeStruct(spec[0], spec[1]))
        if len(spec) == 4:
            out_specs.append(pl.BlockSpec(spec[2], spec[3]))
            acc_per_chunk.append(True)
        else:
            out_specs.append(pl.BlockSpec(spec[0], _zero_map(len(spec[0])), pipeline_mode=pl.Buffered(1)))
            acc_per_chunk.append(False)
    n_in, n_or = len(args), len(out_rows)

    def body(*refs):
        c = pl.program_id(0)
        i = pl.program_id(1)
        vals = [r[...] for r in refs[:n_in]]
        outs_r, outs_a = fn(c, i, *vals)
        for ref, val in zip(refs[n_in:n_in + n_or], outs_r):
            ref[...] = val.astype(ref.dtype)
        for ref, val, per_chunk in zip(refs[n_in + n_or:], outs_a, acc_per_chunk):
            first = (i == 0) if per_chunk else jnp.logical_and(i == 0, c == 0)

            @pl.when(first)
            def _():
                ref[...] = val.astype(ref.dtype)

            @pl.when(jnp.logical_not(first))
            def _():
                ref[...] += val.astype(ref.dtype)

    outs = pl.pallas_call(body, name=name, grid=(nc, nt), in_specs=in_specs, out_specs=out_specs,
                          out_shape=out_shape, compiler_params=_params())(*args)
    return outs[:n_or], outs[n_or:]


def _row_ids(i, tm):
    return i * tm + lax.broadcasted_iota(jnp.int32, (tm, 1), 0)


PRE_TAPS = (D_MODEL, D_MODEL, D_MODEL, 64, D_MODEL, 64, D_MODEL, 128, D_MODEL)


def rwkv_pre(e, et, ws, taps, h, hp, mu_r, mu_w, mu_k, mu_v, mu_a, mu_g, w0, a0, k_k, k_a):
    w_r, w_k, w_v, w1, w2, a1, a2, g1, g2 = ws
    xs = []
    xx = hp - h
    r = tmm(h + xx * mu_r, w_r, taps, xs)
    k = tmm(h + xx * mu_k, w_k, taps, xs)
    v = tmm(h + xx * mu_v, w_v, taps, xs)
    wraw = -_softplus(-(w0 + tmm(jnp.tanh(tmm(h + xx * mu_w, w1, taps, xs)), w2, taps, xs))) - 0.5
    lw = -jnp.exp(wraw)
    a = _sigmoid(a0 + tmm(tmm(h + xx * mu_a, a1, taps, xs), a2, taps, xs))
    g = tmm(_sigmoid(tmm(h + xx * mu_g, g1, taps, xs)), g2, taps, xs)
    kk = k * k_k
    ss = hsum(kk * kk, e, et)
    pos = ss > 0.0
    nrm = jnp.where(pos, jnp.sqrt(jnp.where(pos, ss, 1.0)), 0.0)
    kk = kk * hbc(1.0 / jnp.maximum(nrm, 1e-12), e, et)
    k2 = k * (1.0 + (a - 1.0) * k_a)
    return (r, lw, k2, v, -kk, kk * a, g), xs


def rwkv_post(e, et, w_o, taps, y, r, k2, v, g, h0, gn_w, gn_b, rk, lg, lb):
    xs = []
    inv_n = 1.0 / HEAD_DIM
    yc = y - hbc(hsum(y, e, et) * inv_n, e, et)
    yv = hsum(yc * yc, e, et) * inv_n
    yn = yc * hbc(lax.rsqrt(yv + GN_EPS), e, et) * gn_w + gn_b
    bonus = hbc(hsum(r * k2 * rk, e, et), e, et) * v
    mix = tmm((yn + bonus) * g, w_o, taps, xs)
    return _layer_norm(ALPHA * h0 + mix, lg, lb), xs


def mlp_chunk(wup, wdown, taps, h):
    xs = []
    u = jnp.maximum(tmm(h, wup, taps, xs), 0.0)
    return tmm(u * u, wdown, taps, xs), xs


def _rot_half(t):
    n = t.shape[-1]
    lane = lax.broadcasted_iota(jnp.int32, t.shape, t.ndim - 1)
    lo = (lane % HEAD_DIM) < (HEAD_DIM // 2)
    return jnp.where(lo, -pltpu.roll(t, n - HEAD_DIM // 2, t.ndim - 1), pltpu.roll(t, HEAD_DIM // 2, t.ndim - 1))


@jax.custom_vjp
def rot_half(t):
    return _rot_half(t)


rot_half.defvjp(lambda t: (_rot_half(t), None), lambda _, g: (-_rot_half(g),))


def _tile_lanes(t, width):
    return jnp.concatenate([t] * (width // t.shape[-1]), axis=-1)


def qkv_proj(cos, sin, wq, wk, wv, taps, h):
    xs = []
    q = tmm(h, wq, taps, xs)
    k = tmm(h, wk, taps, xs)
    v = tmm(h, wv, taps, xs)
    cq, sq = _tile_lanes(cos, D_MODEL), _tile_lanes(sin, D_MODEL)
    ck, sk = _tile_lanes(cos, KV_DIM), _tile_lanes(sin, KV_DIM)
    return (q * cq + rot_half(q) * sq, k * ck + rot_half(k) * sk, v), xs


def attn_out(w_o, taps, o, h, lg, lb):
    xs = []
    return _layer_norm(ALPHA * h + tmm(o, w_o, taps, xs), lg, lb), xs


def _scan_consts():
    t = SCAN_T
    tri = np.tril(np.ones((t, t), np.float32))
    rows = np.arange(2 * t)
    same = (rows[:, None] // t) == (rows[None, :] // t)
    strict = same & ((rows[None, :] % t) < (rows[:, None] % t))
    incl = same & ((rows[None, :] % t) <= (rows[:, None] % t))
    lane = np.arange(PAIR)
    masks = np.zeros((8, PAIR), np.float32)
    masks[0] = (lane // HEAD_DIM) == 0
    masks[1] = (lane // HEAD_DIM) == 1
    return (jnp.asarray(tri, BF16), jnp.asarray(strict.astype(np.float32)), jnp.asarray(incl.astype(np.float32)),
            jnp.asarray(masks), jnp.asarray(np.eye(2 * t, dtype=np.float32)))


def _dot_x3(a, b, ca, cb):
    a1 = a.astype(BF16)
    a2 = (a - a1.astype(F32)).astype(BF16)
    b1 = b.astype(BF16)
    b2 = (b - b1.astype(F32)).astype(BF16)

    def d(u, v):
        return lax.dot_general(u, v, (((ca,), (cb,)), ((), ())), preferred_element_type=F32)

    return d(a1, b1) + (d(a1, b2) + d(a2, b1))


@functools.partial(jax.custom_vjp, nondiff_argnums=(2, 3))
def _dotf(a, b, ca, cb):
    return _dot_x3(a, b, ca, cb)


def _dotf_bwd(ca, cb, res, g):
    a, b = res
    if ca == 1:
        da = _dot_x3(g, b, 1, 1 - cb)
    else:
        da = _dot_x3(b, g, 1 - cb, 1)
    if cb == 0:
        db = _dot_x3(a, g, 1 - ca, 0)
    else:
        db = _dot_x3(g, a, 0, 1 - ca)
    return da, db


_dotf.defvjp(lambda a, b, ca, cb: (_dot_x3(a, b, ca, cb), (a, b)), _dotf_bwd)


def _tri_dot(tri, x, ct):
    acc = None
    for piece in _split3(x):
        t = lax.dot_general(tri, piece, (((ct,), (0,)), ((), ())), preferred_element_type=F32)
        acc = t if acc is None else acc + t
    return acc


@jax.custom_vjp
def _cumsum_rows(tri, x):
    return _tri_dot(tri, x, 1)


_cumsum_rows.defvjp(lambda tri, x: (_tri_dot(tri, x, 1), tri),
                    lambda tri, g: (jnp.zeros_like(tri), _tri_dot(tri, g, 0)))


@jax.custom_vjp
def _unstack2(x):
    t = x.shape[0] // 2
    return x[:t] + x[t:]


_unstack2.defvjp(lambda x: (_unstack2(x), None), lambda _, g: (jnp.concatenate([g, g], axis=0),))


@jax.custom_vjp
def _last_row(x):
    return x[x.shape[0] - 1:, :]


def _last_row_bwd(_, g):
    rows = lax.broadcasted_iota(jnp.int32, (SCAN_T, g.shape[1]), 0)
    return (jnp.where(rows == SCAN_T - 1, jnp.broadcast_to(g, (SCAN_T, g.shape[1])), 0.0),)


_last_row.defvjp(lambda x: (_last_row(x), None), _last_row_bwd)


@jax.custom_vjp
def _solve_saved(n, rhs, minv, u):
    return u


def _solve_saved_bwd(res, du):
    minv, u = res
    drhs = _dotf(minv, du, 0, 0)
    return _dotf(drhs, u, 1, 1), drhs, jnp.zeros_like(minv), jnp.zeros_like(u)


_solve_saved.defvjp(lambda n, rhs, minv, u: (u, (minv, u)), _solve_saved_bwd)


def scan_chunk(tri, strict, incl, m0, m1, eye, r, lw, k, v, a, b, s0, saved=None):
    lower = strict > 0
    lower_incl = incl > 0

    def stack(x):
        return jnp.concatenate([x * m0, x * m1], axis=0)

    def dots(xs, ys, ca, cb, mask=None):
        out = [_dotf(x, y, ca, cb) for x, y in zip(xs, ys)]
        return out if mask is None else [jnp.where(mask, o, 0.0) for o in out]

    cl = [_cumsum_rows(tri, x) for x in lw]
    gam = [jnp.exp(c) for c in cl]
    ginv = [jnp.exp(-c) for c in cl]
    a_s = [stack(x * jnp.exp(c - w)) for x, c, w in zip(a, cl, lw)]
    r_s = [stack(x * g) for x, g in zip(r, gam)]
    b_s = [stack(x * g) for x, g in zip(b, ginv)]
    k_s = [stack(x * g) for x, g in zip(k, ginv)]
    v_s = [stack(x) for x in v]
    n_ab = dots(a_s, b_s, 1, 1, lower)
    n_ak = dots(a_s, k_s, 1, 1, lower)
    r_ab = dots(r_s, b_s, 1, 1, lower_incl)
    r_ak = dots(r_s, k_s, 1, 1, lower_incl)
    rhs = [x + y for x, y in zip(dots(a_s, s0, 1, 1), dots(n_ak, v_s, 1, 0))]
    if saved is None:
        minv = [eye + n for n in n_ab]
        p = n_ab
        for _ in range(5):
            p = dots(p, p, 1, 0)
            minv = [m + mp for m, mp in zip(minv, dots(minv, p, 1, 0))]
        u_s = dots(minv, rhs, 1, 0)
    else:
        minv = saved[0]
        u_s = [_solve_saved(n, x, m, u) for n, x, m, u in zip(n_ab, rhs, *saved)]
    y = [_unstack2(x0 + x1 + x2)
         for x0, x1, x2 in zip(dots(r_s, s0, 1, 1), dots(r_ab, u_s, 1, 0), dots(r_ak, v_s, 1, 0))]
    g_end = [_last_row(g) for g in gam]
    s1 = [s * g + x + z for s, g, x, z in zip(s0, g_end, dots(u_s, [x * g for x, g in zip(b_s, g_end)], 0, 0),
                                              dots(v_s, [x * g for x, g in zip(k_s, g_end)], 0, 0))]
    return y, s1, (minv, u_s)


SCAN_PAIRS = 8


def _scan_specs(consts, order):
    row = pl.BlockSpec((SCAN_T, PAIR * SCAN_PAIRS), lambda p, c: (order(c), p))
    state = pl.BlockSpec((None, SCAN_PAIRS, PAIR, PAIR), lambda p, c: (order(c), p, 0, 0))
    return row, state, [pl.BlockSpec(x.shape, _zero_map(x.ndim)) for x in consts]


def _pair_lanes(q):
    return slice(q * PAIR, (q + 1) * PAIR)


def scan_fwd(r, lw, k, v, a, b, shards=()):
    lp = r.shape[0]
    nch = lp // SCAN_T
    npair = D_MODEL // PAIR
    ng = len(shards)
    consts = _scan_consts()
    row, state, cspecs = _scan_specs(consts, lambda c: c)

    def body(tri, strict, incl, masks, eye, r_ref, lw_ref, k_ref, v_ref, a_ref, b_ref, *rest):
        src, (y_ref, s_ref, minv_ref, u_ref), dst = rest[:ng], rest[ng:ng + 4], rest[ng + 4:2 * ng + 4]
        carry = rest[2 * ng + 4]
        first = jnp.logical_and(pl.program_id(0) == 0, pl.program_id(1) == 0)
        last = jnp.logical_and(pl.program_id(0) == npair // SCAN_PAIRS - 1, pl.program_id(1) == nch - 1)
        if ng:
            sends, arrivals = gather_copies(src, dst, *rest[2 * ng + 5:])

            @pl.when(first)
            def _():
                for cp in sends:
                    cp.start()

        @pl.when(pl.program_id(1) == 0)
        def _():
            carry[...] = jnp.zeros_like(carry)

        pairs = range(SCAN_PAIRS)
        s0 = [carry[q] for q in pairs]
        rows = [[ref[:, _pair_lanes(q)] for q in pairs] for ref in (r_ref, lw_ref, k_ref, v_ref, a_ref, b_ref)]
        y, s1, (minv, u) = scan_chunk(tri[...], strict[...], incl[...], masks[0:1, :], masks[1:2, :], eye[...],
                                      *rows, s0)
        for q in pairs:
            s_ref[q] = s0[q]
            minv_ref[q] = minv[q]
            u_ref[q] = u[q]
            y_ref[:, _pair_lanes(q)] = y[q]
            carry[q] = s1[q]

        if ng:
            @pl.when(last)
            def _():
                for cp in arrivals:
                    cp.wait_recv()
                for cp in sends:
                    cp.wait_send()

    mats = jax.ShapeDtypeStruct((nch, npair, PAIR, PAIR), F32)
    out = pl.pallas_call(
        body, name="rwkv_scan_fwd", grid=(npair // SCAN_PAIRS, nch), in_specs=cspecs + [row] * 6 + [ANY] * ng,
        out_specs=[row, state, state, state] + [ANY] * ng,
        out_shape=[jax.ShapeDtypeStruct((lp, D_MODEL), F32), mats, mats, mats] + gathered_shapes(shards),
        scratch_shapes=[pltpu.VMEM((SCAN_PAIRS, PAIR, PAIR), F32)] + (_sem_scratch(ng * len(XY_FLIPS)) if ng else []),
        compiler_params=_params(),
    )(*consts, r, lw, k, v, a, b, *shards)
    return out[:4], fill_own(out[4:], shards)


def scan_bwd(r, lw, k, v, a, b, saved, dy, parts=()):
    lp = r.shape[0]
    nch = lp // SCAN_T
    npair = D_MODEL // PAIR
    consts = _scan_consts()
    row, state, cspecs = _scan_specs(consts, lambda c: nch - 1 - c)

    ng = len(parts)

    def body(tri, strict, incl, masks, eye, r_ref, lw_ref, k_ref, v_ref, a_ref, b_ref, s_ref, minv_ref, u_ref,
             dy_ref, *rest):
        src, (dr_ref, dlw_ref, dk_ref, dv_ref, da_ref, db_ref), dst = rest[:ng], rest[ng:ng + 6], rest[ng + 6:2 * ng + 6]
        carry = rest[2 * ng + 6]
        first = jnp.logical_and(pl.program_id(0) == 0, pl.program_id(1) == 0)
        last = jnp.logical_and(pl.program_id(0) == npair // SCAN_PAIRS - 1, pl.program_id(1) == nch - 1)
        if ng:
            sends, arrivals = chip_exchange_copies(src, dst, *rest[2 * ng + 7:])

            @pl.when(first)
            def _():
                for cp in sends:
                    cp.start()

        @pl.when(pl.program_id(1) == 0)
        def _():
            carry[...] = jnp.zeros_like(carry)

        pairs = range(SCAN_PAIRS)
        kept = ([minv_ref[q] for q in pairs], [u_ref[q] for q in pairs])

        def fn(*args):
            y, s1, _ = scan_chunk(tri[...], strict[...], incl[...], masks[0:1, :], masks[1:2, :], eye[...], *args,
                                  saved=kept)
            return y, s1

        rows = [[ref[:, _pair_lanes(q)] for q in pairs] for ref in (r_ref, lw_ref, k_ref, v_ref, a_ref, b_ref)]
        _, vjp = jax.vjp(fn, *rows, [s_ref[q] for q in pairs])
        grads = vjp(([dy_ref[:, _pair_lanes(q)] for q in pairs], [carry[q] for q in pairs]))
        for q in pairs:
            for ref, g in zip((dr_ref, dlw_ref, dk_ref, dv_ref, da_ref, db_ref), grads[:6]):
                ref[:, _pair_lanes(q)] = g[q]
            carry[q] = grads[6][q]

        if ng:
            @pl.when(last)
            def _():
                for cp in arrivals:
                    cp.wait_recv()
                for cp in sends:
                    cp.wait_send()

    out = pl.pallas_call(
        body, name="rwkv_scan_bwd", grid=(npair // SCAN_PAIRS, nch),
        in_specs=cspecs + [row] * 6 + [state] * 3 + [row] + [ANY] * ng, out_specs=[row] * 6 + [ANY] * ng,
        out_shape=[jax.ShapeDtypeStruct((lp, D_MODEL), F32)] * 6 + [jax.ShapeDtypeStruct(p.shape, p.dtype) for p in parts],
        scratch_shapes=[pltpu.VMEM((SCAN_PAIRS, PAIR, PAIR), F32)] + (_sem_scratch(ng * len(XY_FLIPS)) if ng else []),
        compiler_params=_params(),
    )(*consts, r, lw, k, v, a, b, *saved, dy, *parts)
    return out[:6], out[6:]


def _spread_matrices():
    rep = np.zeros((N_HEADS_KV, KV_DIM, KVW), np.float32)
    for h in range(N_HEADS_KV):
        for g in range(GROUP):
            rep[h, h * HEAD_DIM + np.arange(HEAD_DIM), g * HEAD_DIM + np.arange(HEAD_DIM)] = 1.0
    return jnp.asarray(rep, BF16)


KV_HEADS = range(N_HEADS_KV)


def _attn_common(n, q_ref, kp, kc, vp, vc, rep_ref, sink_ref):
    lane = lax.broadcasted_iota(jnp.int32, (1, KVW), 1)
    gmask = [(lane // HEAD_DIM == g).astype(F32) for g in range(GROUP)]
    kk = jnp.concatenate([kp, kc], axis=0)
    vv = jnp.concatenate([vp, vc], axis=0)
    qs = [q_ref[:, h * KVW:(h + 1) * KVW] for h in KV_HEADS]
    q_s = [jnp.concatenate([q * gmask[g] for g in range(GROUP)], axis=0) for q in qs]
    keys = [_dot(kk, rep_ref[h], 1, 0) for h in KV_HEADS]
    vals = [_dot(vv, rep_ref[h], 1, 0) for h in KV_HEADS]
    qi = lax.broadcasted_iota(jnp.int32, (GROUP * BLOCK, 2 * BLOCK), 0) % BLOCK
    kj = lax.broadcasted_iota(jnp.int32, (GROUP * BLOCK, 2 * BLOCK), 1)
    rel = BLOCK + qi - kj
    valid = (rel >= 0) & (rel < BLOCK) & ((n - 1) * BLOCK + kj >= PAD_FRONT)
    s = [jnp.where(valid, _dot(x, y, 1, 1) * (HEAD_DIM ** -0.5), -1e30) for x, y in zip(q_s, keys)]
    sink_col = [jnp.concatenate([jnp.broadcast_to(sink_ref[h, g:g + 1, 0:1], (BLOCK, 1)) for g in range(GROUP)],
                                axis=0) for h in KV_HEADS]
    m = [jnp.maximum(jnp.max(x, axis=-1, keepdims=True), c) for x, c in zip(s, sink_col)]
    ex = [jnp.exp(x - y) for x, y in zip(s, m)]
    ex_sink = [jnp.exp(c - y) for c, y in zip(sink_col, m)]
    inv = [1.0 / (jnp.sum(x, axis=-1, keepdims=True) + c) for x, c in zip(ex, ex_sink)]
    return (gmask, q_s, keys, vals, [x * y for x, y in zip(ex, inv)], [x * y for x, y in zip(ex_sink, inv)])


def _unstack_groups(x_s, gmask):
    out = None
    for g in range(GROUP):
        t = x_s[g * BLOCK:(g + 1) * BLOCK] * gmask[g]
        out = t if out is None else out + t
    return out


def _attn_specs():
    qspec = pl.BlockSpec((BLOCK, D_MODEL), lambda n: (n, 0))
    cur = pl.BlockSpec((BLOCK, KV_DIM), lambda n: (n, 0))
    prev = pl.BlockSpec((BLOCK, KV_DIM), lambda n: (jnp.maximum(n - 1, 0), 0))
    rep = pl.BlockSpec((N_HEADS_KV, KV_DIM, KVW), lambda n: (0, 0, 0))
    sink = pl.BlockSpec((N_HEADS_KV, 8, PAIR), lambda n: (0, 0, 0))
    return qspec, cur, prev, rep, sink


def _attn_params():
    return pltpu.CompilerParams(dimension_semantics=("arbitrary",), vmem_limit_bytes=VMEM_LIMIT)


def attn_fwd(q, k, v, sinks_b):
    lp = q.shape[0]
    qspec, cur, prev, rep, sink = _attn_specs()

    def body(q_ref, kp_ref, kc_ref, vp_ref, vc_ref, rep_ref, sink_ref, o_ref):
        gmask, _, _, vals, p, _ = _attn_common(pl.program_id(0), q_ref, kp_ref[...], kc_ref[...], vp_ref[...],
                                               vc_ref[...], rep_ref, sink_ref)
        o = [_dot(x, y, 1, 0) for x, y in zip(p, vals)]
        for h in KV_HEADS:
            o_ref[:, h * KVW:(h + 1) * KVW] = _unstack_groups(o[h], gmask)

    return pl.pallas_call(
        body, name="swa_fwd", grid=(lp // BLOCK,), in_specs=[qspec, prev, cur, prev, cur, rep, sink],
        out_specs=qspec, out_shape=jax.ShapeDtypeStruct((lp, D_MODEL), F32), compiler_params=_attn_params(),
    )(q, k, k, v, v, _spread_matrices(), sinks_b)


def attn_bwd(q, k, v, sinks_b, do):
    lp = q.shape[0]
    qspec, cur, prev, rep, sink = _attn_specs()

    def body(q_ref, kp_ref, kc_ref, vp_ref, vc_ref, rep_ref, sink_ref, do_ref, dq_ref, dkc_ref, dkp_ref, dvc_ref,
             dvp_ref, dsink_ref):
        n = pl.program_id(0)
        gmask, q_s, keys, vals, p, p_sink = _attn_common(n, q_ref, kp_ref[...], kc_ref[...], vp_ref[...], vc_ref[...],
                                                         rep_ref, sink_ref)
        do_s = [jnp.concatenate([do_ref[:, h * KVW:(h + 1) * KVW] * gmask[g] for g in range(GROUP)], axis=0)
                for h in KV_HEADS]
        dp = [_dot(x, y, 1, 1) for x, y in zip(do_s, vals)]
        delta = [jnp.sum(x * y, axis=-1, keepdims=True) for x, y in zip(p, dp)]
        ds = [x * (y - z) * (HEAD_DIM ** -0.5) for x, y, z in zip(p, dp, delta)]
        dq = [_dot(x, y, 1, 0) for x, y in zip(ds, keys)]
        dkeys_s = [_dot(x, y, 0, 0) for x, y in zip(ds, q_s)]
        dvals_s = [_dot(x, y, 0, 0) for x, y in zip(p, do_s)]
        dkeys = [_exact_dot(x, rep_ref[h], cb=1) for h, x in enumerate(dkeys_s)]
        dvals = [_exact_dot(x, rep_ref[h], cb=1) for h, x in enumerate(dvals_s)]
        dk_all = (dkeys[0] + dkeys[1]) + (dkeys[2] + dkeys[3])
        dv_all = (dvals[0] + dvals[1]) + (dvals[2] + dvals[3])
        dkp_ref[...] = dk_all[:BLOCK]
        dkc_ref[...] = dk_all[BLOCK:]
        dvp_ref[...] = dv_all[:BLOCK]
        dvc_ref[...] = dv_all[BLOCK:]
        dsinks = []
        for h in KV_HEADS:
            dq_ref[:, h * KVW:(h + 1) * KVW] = _unstack_groups(dq[h], gmask)
            dsk = -(p_sink[h] * delta[h])
            rows = [jnp.broadcast_to(jnp.sum(dsk[g * BLOCK:(g + 1) * BLOCK], axis=0, keepdims=True), (1, PAIR))
                    for g in range(GROUP)]
            dsinks.append(jnp.concatenate(rows + [jnp.zeros((8 - GROUP, PAIR), F32)], axis=0))

        @pl.when(n == 0)
        def _():
            for h in KV_HEADS:
                dsink_ref[h] = dsinks[h]

        @pl.when(n > 0)
        def _():
            for h in KV_HEADS:
                dsink_ref[h] += dsinks[h]

    kv = jax.ShapeDtypeStruct((lp, KV_DIM), F32)
    return pl.pallas_call(
        body, name="swa_bwd", grid=(lp // BLOCK,), in_specs=[qspec, prev, cur, prev, cur, rep, sink, qspec],
        out_specs=[qspec, cur, cur, cur, cur, sink],
        out_shape=[jax.ShapeDtypeStruct((lp, D_MODEL), F32), kv, kv, kv, kv,
                   jax.ShapeDtypeStruct((N_HEADS_KV, 8, PAIR), F32)],
        compiler_params=_attn_params(),
    )(q, k, k, v, v, _spread_matrices(), sinks_b, do)


def _pick_tm(lp, want):
    for tm in (384, 192, 128, 64):
        if tm <= want and lp % tm == 0:
            return tm
    raise ValueError(lp)


def _acc(shape):
    return (tuple(shape), F32)


def _ff_all(w, layer):
    return (w, (N_FF_CHUNK, None, D_MODEL, D_MODEL), lambda c, i: (0, layer, 0, 0))


def _ff_one(w, layer):
    return (w, (None, None, D_MODEL, D_MODEL), lambda c, i: (c, layer, 0, 0))


def _mlp_layer_fwd(name, h, wup, wdown, layer, lg, lb, tm):
    def fn(c, i, h, wup, wdown, lg, lb):
        out = None
        for s in range(N_FF_CHUNK):
            t = mlp_chunk(wup[s], wdown[s], None, h)[0]
            out = t if out is None else out + t
        z = ALPHA * h + out
        return (_layer_norm(z, lg, lb), z), ()

    (h_out, z), _ = rowwise(name, fn, [h], [_ff_all(wup, layer), _ff_all(wdown, layer), lg, lb],
                            [(D_MODEL, F32), (D_MODEL, F32)], [], tm)
    return h_out, z


def _mlp_layer_bwd(name, h_in, z, dh_parts, wup, wdown, layer, lg, lb, tm):
    n_parts = len(dh_parts)

    def fn_ln(c, i, z, *rest):
        dh = rest[0]
        for extra in rest[1:n_parts]:
            dh = dh + extra
        _, vjp = jax.vjp(_layer_norm, z, rest[n_parts], rest[n_parts + 1])
        dz, dlg, dlb = vjp(dh)
        return (dz,), (dlg, dlb)

    (dz,), (dlg, dlb) = rowwise(name + "_ln", fn_ln, [z] + list(dh_parts), [lg, lb], [(D_MODEL, F32)],
                                [_acc((1, D_MODEL)), _acc((1, D_MODEL))], tm)

    def fn_mlp(c, i, h, dz, wup, wdown):
        tile = h.shape[0]
        (dx,), dws = vjp_taps(functools.partial(mlp_chunk, wup, wdown), [(tile, D_MODEL)] * 2, [h], dz)
        return (dx,), dws

    aspec = ((N_FF_CHUNK, D_MODEL, D_MODEL), F32, (None, D_MODEL, D_MODEL), lambda c, i: (c, 0, 0))
    (dx,), (dwup, dwdown) = rowwise(name + "_mm", fn_mlp, [h_in, dz], [_ff_one(wup, layer), _ff_one(wdown, layer)],
                                    [(D_MODEL, F32, True)], [aspec, aspec], tm, nc=N_FF_CHUNK)
    return dz, dx, dwup, dwdown, dlg, dlb


def _sum_parts(dz, dx):
    out = ALPHA * dz
    for s in range(N_FF_CHUNK):
        out = out + dx[s]
    return out


def local_step(x, loss_target, p, late=None, early_hook=None):
    seq = x.shape[0]
    lp = TOK0 + seq
    tm = _pick_tm(lp, 384)
    tms = _pick_tm(lp, 128)
    e, et = _head_matrices()
    h0 = jnp.concatenate([jnp.zeros((PAD_FRONT, D_MODEL), F32), p["meta_tokens"], x], axis=0)
    hp = jnp.concatenate([jnp.zeros((1, D_MODEL), F32), h0[:-1]], axis=0)
    tgt = jnp.concatenate([jnp.zeros((TOK0, D_MODEL), F32), loss_target], axis=0)
    pos = jnp.maximum(jnp.arange(lp, dtype=F32) - PAD_FRONT, 0.0)
    inv_freq = 1.0 / (ROPE_THETA ** (jnp.arange(0, HEAD_DIM, 2, dtype=F32) / HEAD_DIM))
    ang = pos[:, None] * inv_freq[None, :]
    cos = jnp.tile(jnp.cos(ang), (1, PAIR // (HEAD_DIM // 2)))
    sin = jnp.tile(jnp.sin(ang), (1, PAIR // (HEAD_DIM // 2)))

    pre_vec = [p["a_mu"][j:j + 1] for j in range(6)] + [p["a_w0"], p["a_a0"], p["a_k_k"], p["a_k_a"]]
    pre_w = [p["a_w_r"], p["a_w_k"], p["a_w_v"], p["a_w1"], p["a_w2"], p["a_a1"], p["a_a2"], p["a_g1"], p["a_g2"]]
    n_vec = len(pre_vec)

    def fn_pre(c, i, h, hp, e, et, *ws):
        return rwkv_pre(e, et, ws[n_vec:], None, h, hp, *ws[:n_vec])[0], ()

    (r, lw, k2, v, an, bn, g), _ = rowwise("rwkv_pre", fn_pre, [h0, hp], [e, et] + pre_vec + pre_w,
                                           [(D_MODEL, F32)] * 7, [], tms)
    (y, *scan_saved), late_gathered = scan_fwd(r, lw, k2, v, an, bn, late[0] if late else ())
    if late:
        p = {**p, **late[1](late_gathered)}

    post_c =[p["a_w_o"], p["a_gn_w"], p["a_gn_b"], p["a_r_k"], p["ln_g00"], p["ln_b00"]]

    def fn_post(c, i, y, r, k2, v, g, h0, e, et, w_o, *vecs):
        return (rwkv_post(e, et, w_o, None, y, r, k2, v, g, h0, *vecs)[0],), ()

    (h1,), _ = rowwise("rwkv_post", fn_post, [y, r, k2, v, g, h0], [e, et] + post_c, [(D_MODEL, F32)], [], tm)
    h2, z2 = _mlp_layer_fwd("mlp0_fwd", h1, p["mlp_up"], p["mlp_down"], 0, p["ln_g01"], p["ln_b01"], tm)

    qkv_w = [p["b_w_q"], p["kv_w_k"], p["kv_w_v"]]

    def fn_qkv(c, i, h, cos, sin, wq, wk, wv):
        return qkv_proj(cos, sin, wq, wk, wv, None, h)[0], ()

    (q, k, vv), _ = rowwise("qkv_proj", fn_qkv, [h2, cos, sin], qkv_w,
                            [(D_MODEL, F32), (KV_DIM, F32), (KV_DIM, F32)], [], tm)
    sinks_b = jnp.broadcast_to(p["b_sinks"].reshape(N_HEADS_KV, GROUP, 1), (N_HEADS_KV, GROUP, PAIR))
    sinks_b = jnp.concatenate([sinks_b, jnp.zeros((N_HEADS_KV, 8 - GROUP, PAIR), F32)], axis=1)
    o = attn_fwd(q, k, vv, sinks_b)

    ao_c = [p["b_w_o"], p["ln_g10"], p["ln_b10"]]

    def fn_ao(c, i, o, h, w_o, lg, lb):
        return (attn_out(w_o, None, o, h, lg, lb)[0],), ()

    (h3,), _ = rowwise("attn_out", fn_ao, [o, h2], ao_c, [(D_MODEL, F32)], [], tm)
    h4, z4 = _mlp_layer_fwd("mlp1_fwd", h3, p["mlp_up"], p["mlp_down"], 1, p["ln_g11"], p["ln_b11"], tm)

    def fn_loss(c, i, h4, tgt):
        real = (_row_ids(i, tm) >= TOK0).astype(F32)
        err = (h4 - tgt) * real
        part = 0.5 * jnp.sum(jnp.sum(err * err, axis=-1, keepdims=True), axis=0, keepdims=True) / D_MODEL
        return (err * (1.0 / D_MODEL),), (jnp.broadcast_to(part, (8, PAIR)),)

    (dh4,), (loss_acc,) = rowwise("loss", fn_loss, [h4, tgt], [], [(D_MODEL, F32)], [_acc((8, PAIR))], tm)
    loss = loss_acc[0, 0]

    grads = {}
    dz4, dx4, grads["mlp_up1"], grads["mlp_down1"], grads["ln_g11"], grads["ln_b11"] = _mlp_layer_bwd(
        "mlp1_bwd", h3, z4, [dh4], p["mlp_up"], p["mlp_down"], 1, p["ln_g11"], p["ln_b11"], tm)

    def fn_ao_b(c, i, dz, dx, o, h, w_o, lg, lb):
        (do, dh, dlg, dlb), (dw_o,) = vjp_taps(functools.partial(attn_out, w_o), [(tms, D_MODEL)], [o, h, lg, lb],
                                               _sum_parts(dz, dx))
        return (do, dh), (dw_o, dlg, dlb)

    (do, dh2_a), (grads["b_w_o"], grads["ln_g10"], grads["ln_b10"]) = rowwise(
        "attn_out_bwd", fn_ao_b, [dz4, dx4, o, h2], ao_c, [(D_MODEL, F32)] * 2,
        [_acc((D_MODEL, D_MODEL)), _acc((1, D_MODEL)), _acc((1, D_MODEL))], tms)

    dq, dkc, dkp, dvc, dvp, dsinks = attn_bwd(q, k, vv, sinks_b, do)
    grads["b_sinks"] = dsinks[:, :GROUP, 0].reshape(1, N_HEADS)
    zblk = jnp.zeros((BLOCK, KV_DIM), F32)
    dkp_s = jnp.concatenate([dkp[BLOCK:], zblk], axis=0)
    dvp_s = jnp.concatenate([dvp[BLOCK:], zblk], axis=0)

    def fn_qkv_b(c, i, h, cos, sin, dq, dkc, dkp, dvc, dvp, wq, wk, wv):
        return vjp_taps(functools.partial(qkv_proj, cos, sin, wq, wk, wv),
                        [(tms, D_MODEL), (tms, KV_DIM), (tms, KV_DIM)], [h], (dq, dkc + dkp, dvc + dvp))

    (dh2_q,), (grads["b_w_q"], grads["kv_w_k"], grads["kv_w_v"]) = rowwise(
        "qkv_proj_bwd", fn_qkv_b, [h2, cos, sin, dq, dkc, dkp_s, dvc, dvp_s], qkv_w, [(D_MODEL, F32)],
        [_acc((D_MODEL, D_MODEL)), _acc((D_MODEL, KV_DIM)), _acc((D_MODEL, KV_DIM))], tms)

    dz2, dx2, grads["mlp_up0"], grads["mlp_down0"], grads["ln_g01"], grads["ln_b01"] = _mlp_layer_bwd(
        "mlp0_bwd", h1, z2, [dh2_a, dh2_q], p["mlp_up"], p["mlp_down"], 0, p["ln_g01"], p["ln_b01"], tm)

    def fn_post_b(c, i, dz, dx, y, r, k2, v, g, h0, e, et, w_o, *vecs):
        out, dws = vjp_taps(functools.partial(rwkv_post, e, et, w_o), [(tms, D_MODEL)],
                            [y, r, k2, v, g, h0] + list(vecs), _sum_parts(dz, dx))
        return out[:6], tuple(dws) + tuple(out[6:])

    (dy, dr_c, dk_c, dv_c, dg, dh0_c), post_g = rowwise(
        "rwkv_post_bwd", fn_post_b, [dz2, dx2, y, r, k2, v, g, h0], [e, et] + post_c, [(D_MODEL, F32)] * 6,
        [_acc((D_MODEL, D_MODEL))] + [_acc((1, D_MODEL))] * 5, tms)
    for name, val in zip(["a_w_o", "a_gn_w", "a_gn_b", "a_r_k", "ln_g00", "ln_b00"], post_g):
        grads[name] = val

    (dr_s, dlw, dk_s, dv_s, dan, dbn), early_from_chips = scan_bwd(r, lw, k2, v, an, bn, scan_saved, dy,
                                                                   early_hook(grads) if early_hook else ())

    def fn_pre_b(c, i, h, hp, dr_c, dr_s, dlw, dk_c, dk_s, dv_c, dv_s, dan, dbn, dg, e, et, *ws):
        real = (_row_ids(i, tms) >= PAD_FRONT).astype(F32)
        cot = tuple(t * real for t in (dr_c + dr_s, dlw, dk_c + dk_s, dv_c + dv_s, dan, dbn, dg))
        out, dws = vjp_taps(functools.partial(rwkv_pre, e, et, ws[n_vec:]), [(tms, n) for n in PRE_TAPS],
                            [h, hp] + list(ws[:n_vec]), cot)
        return out[:2], tuple(out[2:]) + tuple(dws)

    (dh0_p, dhp), pre_g = rowwise(
        "rwkv_pre_bwd", fn_pre_b, [h0, hp, dr_c, dr_s, dlw, dk_c, dk_s, dv_c, dv_s, dan, dbn, dg],
        [e, et] + pre_vec + pre_w, [(D_MODEL, F32)] * 2,
        [_acc((1, D_MODEL))] * n_vec + [_acc(w.shape) for w in pre_w], tms)
    grads["a_mu"] = jnp.concatenate(pre_g[:6], axis=0)
    for name, val in zip(["a_w0", "a_a0", "a_k_k", "a_k_a", "a_w_r", "a_w_k", "a_w_v", "a_w1", "a_w2", "a_a1",
                          "a_a2", "a_g1", "a_g2"], pre_g[6:]):
        grads[name] = val

    dhp_s = jnp.concatenate([dhp[1:], jnp.zeros((1, D_MODEL), F32)], axis=0)

    def fn_add(c, i, a, b, d):
        return (a + b + d,), ()

    (dh0,), _ = rowwise("grad_h0", fn_add, [dh0_c, dh0_p, dhp_s], [], [(D_MODEL, F32)], [], tm)
    grads["meta_tokens"] = dh0[PAD_FRONT:TOK0]
    return loss, dh0[TOK0:], grads, early_from_chips


ANY = pl.BlockSpec(memory_space=pl.ANY)
XY_FLIPS = ((0, 1), (1, 0), (1, 1))
ALL_FLIPS = tuple((e >> 2 & 1, e >> 1 & 1, e & 1) for e in range(1, N_DEV))


def _flip(v, bit):
    return 1 - v if bit else v


def _sem_scratch(n):
    return [pltpu.SemaphoreType.DMA((n,)), pltpu.SemaphoreType.DMA((n,))]


def gather_copies(src, dst, send_sems, recv_sems):
    npeer = len(XY_FLIPS)
    x, y, c = lax.axis_index("x"), lax.axis_index("y"), lax.axis_index("c")

    def copy(k, j, slot):
        fx, fy = XY_FLIPS[j]
        return pltpu.make_async_remote_copy(
            src_ref=src[k], dst_ref=dst[k].at[slot], send_sem=send_sems.at[k * npeer + j],
            recv_sem=recv_sems.at[k * npeer + j], device_id=(_flip(x, fx), _flip(y, fy), c), device_id_type=MESH)

    sends = [copy(k, j, 2 * x + y) for k in range(len(src)) for j in range(npeer)]
    arrivals = [copy(k, j, 2 * _flip(x, fx) + _flip(y, fy)) for k in range(len(src))
                for j, (fx, fy) in enumerate(XY_FLIPS)]
    return sends, arrivals


def gathered_shapes(shards):
    return [jax.ShapeDtypeStruct((N_SHARD,) + s.shape, s.dtype) for s in shards]


def fill_own(gathered, shards):
    slot = 2 * lax.axis_index("x") + lax.axis_index("y")
    return [lax.dynamic_update_index_in_dim(g, s, slot, 0) for g, s in zip(gathered, shards)]


def all_gather_shards(shards):
    n = len(shards)

    def body(*refs):
        sends, arrivals = gather_copies(refs[:n], refs[n:2 * n], *refs[2 * n:])
        for cp in sends:
            cp.start()
        for cp in arrivals:
            cp.wait_recv()
        for cp in sends:
            cp.wait_send()

    out = pl.pallas_call(body, name="gather_weights", in_specs=[ANY] * n, out_specs=[ANY] * n,
                         out_shape=gathered_shapes(shards), scratch_shapes=_sem_scratch(n * len(XY_FLIPS)))(*shards)
    return fill_own(out, shards)


def placement():
    x, y, c = lax.axis_index("x"), lax.axis_index("y"), lax.axis_index("c")
    me = 2 * x + y
    others = [j + (j >= me).astype(jnp.int32) for j in range(N_SHARD - 1)]
    return jnp.stack([c, me] + others).astype(jnp.int32)


def pair_exchange(name, sources):
    n = len(sources)

    def body(*refs):
        src, got = refs[:n], refs[n:2 * n]
        send_sems, recv_sems = refs[2 * n:]
        x, y, c = lax.axis_index("x"), lax.axis_index("y"), lax.axis_index("c")

        def copy(k):
            half = sources[k].shape[1] // 2
            theirs = src[k].at[:, pl.ds(pl.multiple_of((1 - c) * half, 8), half), :]
            return pltpu.make_async_remote_copy(
                src_ref=theirs, dst_ref=got[k], send_sem=send_sems.at[k], recv_sem=recv_sems.at[k],
                device_id=(x, y, 1 - c), device_id_type=MESH)

        sends = [copy(k) for k in range(n)]
        for cp in sends:
            cp.start()
        for cp in sends:
            cp.wait_recv()
        for cp in sends:
            cp.wait_send()

    halves = [jax.ShapeDtypeStruct((s.shape[0], s.shape[1] // 2, s.shape[2]), s.dtype) for s in sources]
    return pl.pallas_call(body, name=name, in_specs=[ANY] * n, out_specs=[ANY] * n,
                          out_shape=halves, scratch_shapes=_sem_scratch(n))(*sources)


def chip_exchange(parts):
    n = len(parts)

    def body(*refs):
        sends, arrivals = chip_exchange_copies(refs[:n], refs[n:2 * n], *refs[2 * n:])
        for cp in sends:
            cp.start()
        for cp in arrivals:
            cp.wait_recv()
        for cp in sends:
            cp.wait_send()

    return pl.pallas_call(
        body, name="grads_chip_exchange", in_specs=[ANY] * n, out_specs=[ANY] * n,
        out_shape=[jax.ShapeDtypeStruct(p.shape, p.dtype) for p in parts],
        scratch_shapes=_sem_scratch(n * len(XY_FLIPS)),
    )(*parts)


def chip_exchange_copies(src, dst, send_sems, recv_sems):
    npeer = len(XY_FLIPS)
    x, y, c = lax.axis_index("x"), lax.axis_index("y"), lax.axis_index("c")
    me = 2 * x + y

    def copy(k, j, sending):
        fx, fy = XY_FLIPS[j]
        px, py = _flip(x, fx), _flip(y, fy)
        peer = 2 * px + py
        return pltpu.make_async_remote_copy(
            src_ref=src[k].at[peer], dst_ref=dst[k].at[me if sending else peer],
            send_sem=send_sems.at[k * npeer + j], recv_sem=recv_sems.at[k * npeer + j],
            device_id=(px, py, c), device_id_type=MESH)

    pairs = [(k, j) for k in range(len(src)) for j in range(npeer)]
    return [copy(k, j, True) for k, j in pairs], [copy(k, j, False) for k, j in pairs]


def sibling_share(halves):
    n = len(halves)

    def body(*refs):
        src, got = refs[:n], refs[n:2 * n]
        send_sems, recv_sems = refs[2 * n:]
        x, y, c = lax.axis_index("x"), lax.axis_index("y"), lax.axis_index("c")
        sends = [pltpu.make_async_remote_copy(
            src_ref=src[k], dst_ref=got[k], send_sem=send_sems.at[k], recv_sem=recv_sems.at[k],
            device_id=(x, y, 1 - c), device_id_type=MESH) for k in range(n)]
        for cp in sends:
            cp.start()
        for cp in sends:
            cp.wait_recv()
        for cp in sends:
            cp.wait_send()

    return pl.pallas_call(
        body, name="grads_sibling_share", in_specs=[ANY] * n, out_specs=[ANY] * n,
        out_shape=[jax.ShapeDtypeStruct(h.shape, h.dtype) for h in halves], scratch_shapes=_sem_scratch(n),
    )(*halves)


ADD_TILE_ELEMS = 512 * 1024


def _row_tile(rows, cols):
    return max(t for t in range(8, rows + 1, 8) if rows % t == 0 and t * cols <= ADD_TILE_ELEMS)


def _prefetch_call(body, name, place, grid, in_specs, out_specs, out_shape, args):
    return pl.pallas_call(
        body, name=name, out_shape=out_shape,
        grid_spec=pltpu.PrefetchScalarGridSpec(num_scalar_prefetch=1, grid=grid, in_specs=in_specs,
                                               out_specs=out_specs),
        compiler_params=pltpu.CompilerParams(dimension_semantics=("arbitrary",) * len(grid),
                                             vmem_limit_bytes=VMEM_LIMIT),
    )(place, *args)


def pair_add(name, place, src, got, dtype):
    n4, half, cols = got.shape
    tile = _row_tile(half, cols)
    nt = half // tile

    def body(pr, a_ref, b_ref, o_ref):
        o_ref[...] = (a_ref[...] + b_ref[...]).astype(o_ref.dtype)

    mine = pl.BlockSpec((None, tile, cols), lambda s, i, pr: (s, pr[0] * nt + i, 0))
    blk = pl.BlockSpec((None, tile, cols), lambda s, i, pr: (s, i, 0))
    return _prefetch_call(body, name, place, (n4, nt), [mine, blk], blk,
                          jax.ShapeDtypeStruct(got.shape, dtype), (src, got))


def chip_add(name, place, part, from_chips):
    _, half, cols = part.shape
    tile = _row_tile(half, cols)

    def body(pr, own_ref, r0_ref, r1_ref, r2_ref, o_ref):
        me = pr[1]
        own, r0, r1, r2 = (r[...].astype(F32) for r in (own_ref, r0_ref, r1_ref, r2_ref))
        t0 = jnp.where(me == 0, own, r0)
        t1 = jnp.where(me == 0, r0, jnp.where(me == 1, own, r1))
        t2 = jnp.where(me <= 1, r1, jnp.where(me == 2, own, r2))
        t3 = jnp.where(me == 3, own, r2)
        o_ref[...] = ((t0 + t1) + t2) + t3

    def slab(j):
        return pl.BlockSpec((None, tile, cols), lambda i, pr: (pr[j], i, 0))

    return _prefetch_call(body, name, place, (half // tile,), [slab(1), slab(2), slab(3), slab(4)],
                          pl.BlockSpec((tile, cols), lambda i, pr: (i, 0)),
                          jax.ShapeDtypeStruct((half, cols), F32), (part, from_chips, from_chips, from_chips))


def pair_sums(tag, place, sources, narrow):
    got = pair_exchange("grads_pair_exchange_" + tag, sources)
    return [pair_add(f"grads_pair_add_{tag}{k}", place, s, g, BF16 if nar else F32)
            for k, (s, g, nar) in enumerate(zip(sources, got, narrow))]


def finish_sums(place, parts, from_chips):
    halves = [chip_add(f"grads_chip_add{k}", place, p, f) for k, (p, f) in enumerate(zip(parts, from_chips))]
    return list(zip(halves, sibling_share(halves)))


ADAM_ROWS = 256


def adamw_update(name, place, halves, w, m, v):
    nsub, rows, cols = w.shape
    half = rows // 2
    tr = ADAM_ROWS if half % ADAM_ROWS == 0 else half
    nth = half // tr

    def body(pr, *refs):
        g_refs, (w_ref, m_ref, v_ref, g_ref, d_ref, nm_ref, nv_ref) = refs[:2 * nsub], refs[2 * nsub:]
        l = pl.program_id(0)
        mine = (pl.program_id(1) // nth) == pr[0]
        g = None
        for s in range(nsub):
            gs = jnp.where(mine, g_refs[2 * s][...], g_refs[2 * s + 1][...])
            g = gs if g is None else jnp.where(l == s, gs, g)
        m2 = ADAM_B1 * m_ref[...] + (1.0 - ADAM_B1) * g
        v2 = ADAM_B2 * v_ref[...] + (1.0 - ADAM_B2) * (g * g)
        m_hat = m2 / (1.0 - ADAM_B1 ** ADAM_STEP)
        v_hat = v2 / (1.0 - ADAM_B2 ** ADAM_STEP)
        g_ref[...] = g
        d_ref[...] = -ADAM_LR * (m_hat / (jnp.sqrt(v_hat) + ADAM_EPS) + ADAM_WD * w_ref[...])
        nm_ref[...] = m2
        nv_ref[...] = v2

    gblk = pl.BlockSpec((tr, cols), lambda l, i, pr: (i % nth, 0))
    blk = pl.BlockSpec((None, tr, cols), lambda l, i, pr: (l, i, 0))
    out = jax.ShapeDtypeStruct((nsub, rows, cols), F32)
    return _prefetch_call(body, name, place, (nsub, rows // tr), [gblk] * (2 * nsub) + [blk] * 3, [blk] * 4,
                          [out] * 4, [h for pair in halves for h in pair] + [w, m, v])


WEIGHT_NAMES = ("meta_tokens", "a_mu", "a_w_r", "a_w_k", "a_w_v", "a_w_o", "a_w0", "a_w1", "a_w2", "a_a0", "a_a1",
                "a_a2", "a_g1", "a_g2", "a_k_k", "a_k_a", "a_r_k", "a_gn_w", "a_gn_b", "kv_w_k", "kv_w_v", "b_w_q",
                "b_sinks", "b_w_o", "mlp_w_up", "mlp_w_down", "ln_g", "ln_b")
BIG_NAMES = ("a_w_r", "a_w_k", "a_w_v", "a_w_o", "b_w_q", "b_w_o")
EARLY_NAMES, LATE_NAMES = BIG_NAMES[:3], BIG_NAMES[3:]
PACK_MATS = (("kv_w_k", 256), ("kv_w_v", 256), ("a_w1", 64), ("a_a1", 64), ("a_g1", 128), ("a_w2", 64),
             ("a_a2", 64), ("a_g2", 128))
COLUMN_CUT = ("a_w2", "a_a2", "a_g2")
PACK_VECS = (("a_mu", 6), ("a_w0", 1), ("a_a0", 1), ("a_k_k", 1), ("a_k_a", 1), ("a_gn_w", 1), ("a_gn_b", 1),
             ("ln_g", 4), ("ln_b", 4), ("meta_tokens", 16))
PACK_REPL = (("a_r_k", 4), ("b_sinks", 1))
SHARD_W = D_MODEL // N_SHARD
N_MAT_ROWS = sum(r for _, r in PACK_MATS)
N_VEC_ROWS = sum(r for _, r in PACK_VECS)
N_PACK_ROWS = -(-(N_MAT_ROWS + N_VEC_ROWS + sum(r for _, r in PACK_REPL)) // 8) * 8
N_GATHER_VEC_ROWS = -(-N_VEC_ROWS // 8) * 8


def _pack_rows(arr):
    if arr.size == N_HEADS:
        return jnp.pad(arr.reshape(1, N_HEADS), ((0, 0), (0, SHARD_W - N_HEADS)))
    return arr.reshape(-1, SHARD_W)


def pack_small(get):
    parts = [_pack_rows(get(name)) for name, _ in PACK_MATS + PACK_VECS + PACK_REPL]
    used = sum(p.shape[0] for p in parts)
    return jnp.concatenate(parts + [jnp.zeros((N_PACK_ROWS - used, SHARD_W), F32)], axis=0)


def unpack_small(pack, shapes):
    out, off = {}, 0
    for name, rows in PACK_MATS + PACK_VECS + PACK_REPL:
        piece = pack[off:off + rows]
        off += rows
        out[name] = piece[:, :N_HEADS].reshape(shapes[name]) if name == "b_sinks" else piece.reshape(shapes[name])
    return out


def whole_weights(big_names, gathered_big, mats, vecs, a_r_k, b_sinks):
    p = {name: g.reshape(D_MODEL, D_MODEL) for name, g in zip(big_names, gathered_big)}
    off = 0
    for name, rows in PACK_MATS:
        piece = mats[:, off:off + rows]
        off += rows
        if name in COLUMN_CUT:
            p[name] = piece.transpose(1, 0, 2).reshape(rows, D_MODEL)
        else:
            p[name] = piece.reshape(D_MODEL, rows)
    v = vecs.transpose(1, 0, 2).reshape(-1, D_MODEL)
    off = 0
    for name, rows in PACK_VECS:
        p[name] = v[off:off + rows]
        off += rows
    for i in range(2):
        for j in range(2):
            p[f"ln_g{i}{j}"] = p["ln_g"][2 * i + j:2 * i + j + 1]
            p[f"ln_b{i}{j}"] = p["ln_b"][2 * i + j:2 * i + j + 1]
    p["a_r_k"] = a_r_k.reshape(1, D_MODEL)
    p["b_sinks"] = b_sinks
    return p


def small_grad_pack(g):
    parts = []
    for name, rows in PACK_MATS:
        if name in COLUMN_CUT:
            parts.append(g[name].reshape(rows, N_SHARD, SHARD_W).transpose(1, 0, 2))
        else:
            parts.append(g[name].reshape(N_SHARD, rows, SHARD_W))
    vec_rows = [g["a_mu"]] + [g[n] for n in ("a_w0", "a_a0", "a_k_k", "a_k_a", "a_gn_w", "a_gn_b")]
    vec_rows += [g[f"ln_g{i}{j}"] for i in range(2) for j in range(2)]
    vec_rows += [g[f"ln_b{i}{j}"] for i in range(2) for j in range(2)] + [g["meta_tokens"]]
    parts.append(jnp.concatenate(vec_rows, axis=0).reshape(N_VEC_ROWS, N_SHARD, SHARD_W).transpose(1, 0, 2))
    parts.append(jnp.broadcast_to(g["a_r_k"].reshape(1, -1, SHARD_W), (N_SHARD, D_MODEL // SHARD_W, SHARD_W)))
    sinks = jnp.pad(g["b_sinks"].reshape(1, 1, N_HEADS), ((0, 0), (0, 0), (0, SHARD_W - N_HEADS)))
    parts.append(jnp.broadcast_to(sinks, (N_SHARD, 1, SHARD_W)))
    used = sum(p.shape[1] for p in parts)
    parts.append(jnp.zeros((N_SHARD, N_PACK_ROWS - used, SHARD_W), F32))
    return jnp.concatenate(parts, axis=1)


def train_step(vals):
    w = {n: vals[n] for n in WEIGHT_NAMES}
    w_pack = pack_small(lambda n: w[n])
    early = [w[n][0].astype(BF16) for n in EARLY_NAMES]
    early += [w_pack[:N_MAT_ROWS].astype(BF16), w_pack[N_MAT_ROWS:N_MAT_ROWS + N_GATHER_VEC_ROWS]]
    gathered = all_gather_shards(early)
    ne = len(EARLY_NAMES)
    p = whole_weights(EARLY_NAMES, gathered[:ne], gathered[ne], gathered[ne + 1][:, :N_VEC_ROWS], w["a_r_k"],
                      w["b_sinks"])
    late = [w[n][0].astype(BF16) for n in LATE_NAMES] + [w["mlp_w_up"].astype(BF16), w["mlp_w_down"].astype(BF16)]
    nb = len(BIG_NAMES)

    def late_weights(got):
        out = {n: x.reshape(D_MODEL, D_MODEL) for n, x in zip(LATE_NAMES, got)}
        out["mlp_up"], out["mlp_down"] = got[len(LATE_NAMES):]
        return out

    place = placement()
    ready = {}

    def early_hook(g):
        srcs = [g[n].reshape(N_SHARD, SHARD_W, D_MODEL) for n in LATE_NAMES]
        srcs += [g["mlp_up0"], g["mlp_up1"], g["mlp_down0"], g["mlp_down1"]]
        ready["parts"] = pair_sums("early", place, srcs, [True] * len(srcs))
        return ready["parts"]

    loss, gx, g, early_from_chips = local_step(vals["x"][0], vals["loss_target"][0], p, (late, late_weights),
                                               early_hook)
    loss = lax.psum(loss, ("x", "y", "c"))
    srcs = [g[n].reshape(N_SHARD, SHARD_W, D_MODEL) for n in EARLY_NAMES] + [small_grad_pack(g)]
    rest = pair_sums("late", place, srcs, [True] * len(EARLY_NAMES) + [False])
    rest_from_chips = chip_exchange(rest)
    ne = len(EARLY_NAMES)
    halves = finish_sums(place, rest[:ne] + ready["parts"] + rest[ne:],
                         list(rest_from_chips[:ne]) + list(early_from_chips) + list(rest_from_chips[ne:]))

    res = {}
    for k, n in enumerate(BIG_NAMES):
        res[n] = adamw_update("adamw_" + n, place, halves[k:k + 1], w[n], vals["m_" + n], vals["v_" + n])
    for k, n in ((nb, "mlp_w_up"), (nb + 2, "mlp_w_down")):
        res[n] = adamw_update("adamw_" + n, place, halves[k:k + 2], w[n], vals["m_" + n], vals["v_" + n])
    packs = adamw_update("adamw_small", place, halves[-1:], w_pack[None], pack_small(lambda n: vals["m_" + n])[None],
                         pack_small(lambda n: vals["v_" + n])[None])
    shapes = {n: w[n].shape for n in WEIGHT_NAMES}
    small = [unpack_small(pk[0], shapes) for pk in packs]
    outs = [loss, gx[None]]
    for t in range(4):
        outs += [res[n][t] if n in res else small[t][n] for n in WEIGHT_NAMES]
    return tuple(outs)


def kernel(x, meta_tokens, a_mu, a_w_r, a_w_k, a_w_v, a_w_o, a_w0, a_w1, a_w2, a_a0, a_a1, a_a2, a_g1, a_g2, a_k_k,
           a_k_a, a_r_k, a_gn_w, a_gn_b, kv_w_k, kv_w_v, b_w_q, b_sinks, b_w_o, mlp_w_up, mlp_w_down, ln_g, ln_b,
           loss_target, m_meta_tokens, m_a_mu, m_a_w_r, m_a_w_k, m_a_w_v, m_a_w_o, m_a_w0, m_a_w1, m_a_w2, m_a_a0,
           m_a_a1, m_a_a2, m_a_g1, m_a_g2, m_a_k_k, m_a_k_a, m_a_r_k, m_a_gn_w, m_a_gn_b, m_kv_w_k, m_kv_w_v,
           m_b_w_q, m_b_sinks, m_b_w_o, m_mlp_w_up, m_mlp_w_down, m_ln_g, m_ln_b, v_meta_tokens, v_a_mu, v_a_w_r,
           v_a_w_k, v_a_w_v, v_a_w_o, v_a_w0, v_a_w1, v_a_w2, v_a_a0, v_a_a1, v_a_a2, v_a_g1, v_a_g2, v_a_k_k,
           v_a_k_a, v_a_r_k, v_a_gn_w, v_a_gn_b, v_kv_w_k, v_kv_w_v, v_b_w_q, v_b_sinks, v_b_w_o, v_mlp_w_up,
           v_mlp_w_down, v_ln_g, v_ln_b):
    return train_step(dict(locals()))
```

```python
import functools

import numpy as np
import jax
import jax.numpy as jnp
from jax import lax
from jax.experimental import pallas as pl
from jax.experimental.pallas import tpu as pltpu

F32 = jnp.float32
BF16 = jnp.bfloat16

D_MODEL = 1024
N_HEADS = 16
HEAD_DIM = 64
N_HEADS_KV = 4
GROUP = 4
KV_DIM = N_HEADS_KV * HEAD_DIM
N_META = 16
BLOCK = 128
PAD_FRONT = BLOCK - N_META
TOK0 = PAD_FRONT + N_META
N_FF_CHUNK = 4
N_SHARD = 4
N_DEV = 8
GN_EPS = 64e-5
LN_EPS = 1e-5
ROPE_THETA = 10000.0
ALPHA = 4.0 ** 0.25
ADAM_LR, ADAM_B1, ADAM_B2, ADAM_EPS, ADAM_WD, ADAM_STEP = 0.001, 0.9, 0.999, 1e-08, 0.01, 10
SCAN_T = 64
PAIR = 128
KVW = GROUP * HEAD_DIM
VMEM_LIMIT = 56 * 1024 * 1024
HI = lax.Precision.HIGHEST
MESH = pl.DeviceIdType.MESH


def _dot(a, b, ca, cb):
    return lax.dot_general(a.astype(BF16), b.astype(BF16), (((ca,), (cb,)), ((), ())),
                           preferred_element_type=F32)


@jax.custom_vjp
def mm(a, b):
    return _dot(a, b, 1, 0)


def _mm_fwd(a, b):
    return mm(a, b), b


def _mm_bwd(b, g):
    return _dot(g, b, 1, 1), jnp.zeros_like(b)


mm.defvjp(_mm_fwd, _mm_bwd)


def tmm(x, w, taps, xs):
    y = mm(x, w)
    if taps is not None:
        y = y + taps[len(xs)]
    xs.append(x)
    return y


def vjp_taps(core, tap_shapes, args, cot):
    taps = [jnp.zeros(s, F32) for s in tap_shapes]
    _, vjp, xs = jax.vjp(core, taps, *args, has_aux=True)
    out = vjp(cot)
    return out[1:], [_dot(x, g, 0, 0) for x, g in zip(xs, out[0])]


def _split3(x):
    x1 = x.astype(BF16)
    r1 = x - x1.astype(F32)
    x2 = r1.astype(BF16)
    x3 = (r1 - x2.astype(F32)).astype(BF16)
    return x1, x2, x3


def _exact_dot(x, m01, cb=0):
    acc = None
    for piece in _split3(x):
        t = lax.dot_general(piece, m01, (((1,), (cb,)), ((), ())), preferred_element_type=F32)
        acc = t if acc is None else acc + t
    return acc


def _head_matrices():
    e = np.zeros((D_MODEL, N_HEADS), np.float32)
    e[np.arange(D_MODEL), np.arange(D_MODEL) // HEAD_DIM] = 1.0
    return jnp.asarray(e, BF16), jnp.asarray(e.T, BF16)


@jax.custom_vjp
def hsum(x, e, et):
    return _exact_dot(x, e)


@jax.custom_vjp
def hbc(s, e, et):
    return _exact_dot(s, et)


hsum.defvjp(lambda x, e, et: (_exact_dot(x, e), (e, et)),
            lambda res, g: (hbc(g, *res), jnp.zeros_like(res[0]), jnp.zeros_like(res[1])))
hbc.defvjp(lambda s, e, et: (_exact_dot(s, et), (e, et)),
           lambda res, g: (hsum(g, *res), jnp.zeros_like(res[0]), jnp.zeros_like(res[1])))


def _sigmoid(u):
    return 0.5 * (jnp.tanh(0.5 * u) + 1.0)


def _softplus(u):
    return jnp.maximum(u, 0.0) + jnp.log(1.0 + jnp.exp(-jnp.abs(u)))


def _layer_norm(z, g, b):
    mu = jnp.mean(z, axis=-1, keepdims=True)
    zc = z - mu
    var = jnp.mean(zc * zc, axis=-1, keepdims=True)
    return zc * lax.rsqrt(var + LN_EPS) * g + b


def _zero_map(nd):
    return lambda c, i: (0,) * nd


def _params():
    return pltpu.CompilerParams(dimension_semantics=("arbitrary", "arbitrary"), vmem_limit_bytes=VMEM_LIMIT)


def rowwise(name, fn, rows, consts, out_rows, out_accs, tm, nc=1):
    lp = rows[0].shape[-2]
    nt = lp // tm
    assert nt * tm == lp, (name, lp, tm)
    in_specs, args = [], []
    for a in rows:
        if a.ndim == 2:
            in_specs.append(pl.BlockSpec((tm, a.shape[1]), lambda c, i: (i, 0)))
        else:
            in_specs.append(pl.BlockSpec((a.shape[0], tm, a.shape[2]), lambda c, i: (0, i, 0)))
        args.append(a)
    for cst in consts:
        if isinstance(cst, tuple):
            arr, bs, im = cst
            in_specs.append(pl.BlockSpec(bs, im))
        else:
            arr = cst
            in_specs.append(pl.BlockSpec(arr.shape, _zero_map(arr.ndim), pipeline_mode=pl.Buffered(1)))
        args.append(arr)
    out_shape, out_specs, acc_per_chunk = [], [], []
    for spec in out_rows:
        if len(spec) == 3 and spec[2]:
            out_shape.append(jax.ShapeDtypeStruct((nc, lp, spec[0]), spec[1]))
            out_specs.append(pl.BlockSpec((None, tm, spec[0]), lambda c, i: (c, i, 0)))
        else:
            out_shape.append(jax.ShapeDtypeStruct((lp, spec[0]), spec[1]))
            out_specs.append(pl.BlockSpec((tm, spec[0]), lambda c, i: (i, 0)))
    for spec in out_accs:
        out_shape.append(jax.ShapeDtypeStruct(spec[0], spec[1]))
        if len(spec) == 4:
            out_specs.append(pl.BlockSpec(spec[2], spec[3]))
            acc_per_chunk.append(True)
        else:
            out_specs.append(pl.BlockSpec(spec[0], _zero_map(len(spec[0])), pipeline_mode=pl.Buffered(1)))
            acc_per_chunk.append(False)
    n_in, n_or = len(args), len(out_rows)

    def body(*refs):
        c = pl.program_id(0)
        i = pl.program_id(1)
        vals = [r[...] for r in refs[:n_in]]
        outs_r, outs_a = fn(c, i, *vals)
        for ref, val in zip(refs[n_in:n_in + n_or], outs_r):
            ref[...] = val.astype(ref.dtype)
        for ref, val, per_chunk in zip(refs[n_in + n_or:], outs_a, acc_per_chunk):
            first = (i == 0) if per_chunk else jnp.logical_and(i == 0, c == 0)

            @pl.when(first)
            def _():
                ref[...] = val.astype(ref.dtype)

            @pl.when(jnp.logical_not(first))
            def _():
                ref[...] += val.astype(ref.dtype)

    outs = pl.pallas_call(body, name=name, grid=(nc, nt), in_specs=in_specs, out_specs=out_specs,
                          out_shape=out_shape, compiler_params=_params())(*args)
    return outs[:n_or], outs[n_or:]


def _row_ids(i, tm):
    return i * tm + lax.broadcasted_iota(jnp.int32, (tm, 1), 0)


PRE_TAPS = (D_MODEL, D_MODEL, D_MODEL, 64, D_MODEL, 64, D_MODEL, 128, D_MODEL)


def rwkv_pre(e, et, ws, taps, h, hp, mu_r, mu_w, mu_k, mu_v, mu_a, mu_g, w0, a0, k_k, k_a):
    w_r, w_k, w_v, w1, w2, a1, a2, g1, g2 = ws
    xs = []
    xx = hp - h
    r = tmm(h + xx * mu_r, w_r, taps, xs)
    k = tmm(h + xx * mu_k, w_k, taps, xs)
    v = tmm(h + xx * mu_v, w_v, taps, xs)
    wraw = -_softplus(-(w0 + tmm(jnp.tanh(tmm(h + xx * mu_w, w1, taps, xs)), w2, taps, xs))) - 0.5
    lw = -jnp.exp(wraw)
    a = _sigmoid(a0 + tmm(tmm(h + xx * mu_a, a1, taps, xs), a2, taps, xs))
    g = tmm(_sigmoid(tmm(h + xx * mu_g, g1, taps, xs)), g2, taps, xs)
    kk = k * k_k
    ss = hsum(kk * kk, e, et)
    pos = ss > 0.0
    nrm = jnp.where(pos, jnp.sqrt(jnp.where(pos, ss, 1.0)), 0.0)
    kk = kk * hbc(1.0 / jnp.maximum(nrm, 1e-12), e, et)
    k2 = k * (1.0 + (a - 1.0) * k_a)
    return (r, lw, k2, v, -kk, kk * a, g), xs


def rwkv_post(e, et, w_o, taps, y, r, k2, v, g, h0, gn_w, gn_b, rk, lg, lb):
    xs = []
    inv_n = 1.0 / HEAD_DIM
    yc = y - hbc(hsum(y, e, et) * inv_n, e, et)
    yv = hsum(yc * yc, e, et) * inv_n
    yn = yc * hbc(lax.rsqrt(yv + GN_EPS), e, et) * gn_w + gn_b
    bonus = hbc(hsum(r * k2 * rk, e, et), e, et) * v
    mix = tmm((yn + bonus) * g, w_o, taps, xs)
    return _layer_norm(ALPHA * h0 + mix, lg, lb), xs


def mlp_chunk(wup, wdown, taps, h):
    xs = []
    u = jnp.maximum(tmm(h, wup, taps, xs), 0.0)
    return tmm(u * u, wdown, taps, xs), xs


def _rot_half(t):
    n = t.shape[-1]
    lane = lax.broadcasted_iota(jnp.int32, t.shape, t.ndim - 1)
    lo = (lane % HEAD_DIM) < (HEAD_DIM // 2)
    return jnp.where(lo, -pltpu.roll(t, n - HEAD_DIM // 2, t.ndim - 1), pltpu.roll(t, HEAD_DIM // 2, t.ndim - 1))


@jax.custom_vjp
def rot_half(t):
    return _rot_half(t)


rot_half.defvjp(lambda t: (_rot_half(t), None), lambda _, g: (-_rot_half(g),))


def _tile_lanes(t, width):
    return jnp.concatenate([t] * (width // t.shape[-1]), axis=-1)


def qkv_proj(cos, sin, wq, wk, wv, taps, h):
    xs = []
    q = tmm(h, wq, taps, xs)
    k = tmm(h, wk, taps, xs)
    v = tmm(h, wv, taps, xs)
    cq, sq = _tile_lanes(cos, D_MODEL), _tile_lanes(sin, D_MODEL)
    ck, sk = _tile_lanes(cos, KV_DIM), _tile_lanes(sin, KV_DIM)
    return (q * cq + rot_half(q) * sq, k * ck + rot_half(k) * sk, v), xs


def attn_out(w_o, taps, o, h, lg, lb):
    xs = []
    return _layer_norm(ALPHA * h + tmm(o, w_o, taps, xs), lg, lb), xs


def _scan_consts():
    t = SCAN_T
    tri = np.tril(np.ones((t, t), np.float32))
    rows = np.arange(2 * t)
    same = (rows[:, None] // t) == (rows[None, :] // t)
    strict = same & ((rows[None, :] % t) < (rows[:, None] % t))
    incl = same & ((rows[None, :] % t) <= (rows[:, None] % t))
    lane = np.arange(PAIR)
    masks = np.zeros((8, PAIR), np.float32)
    masks[0] = (lane // HEAD_DIM) == 0
    masks[1] = (lane // HEAD_DIM) == 1
    return (jnp.asarray(tri, BF16), jnp.asarray(strict.astype(np.float32)), jnp.asarray(incl.astype(np.float32)),
            jnp.asarray(masks), jnp.asarray(np.eye(2 * t, dtype=np.float32)))


def _scan_dot(a, b, ca, cb):
    return _dot(a, b, ca, cb)


@functools.partial(jax.custom_vjp, nondiff_argnums=(2, 3))
def _dotf(a, b, ca, cb):
    return _scan_dot(a, b, ca, cb)


def _dotf_bwd(ca, cb, res, g):
    a, b = res
    if ca == 1:
        da = _scan_dot(g, b, 1, 1 - cb)
    else:
        da = _scan_dot(b, g, 1 - cb, 1)
    if cb == 0:
        db = _scan_dot(a, g, 1 - ca, 0)
    else:
        db = _scan_dot(g, a, 0, 1 - ca)
    return da, db


_dotf.defvjp(lambda a, b, ca, cb: (_scan_dot(a, b, ca, cb), (a, b)), _dotf_bwd)


def _tri_dot(tri, x, ct):
    acc = None
    for piece in _split3(x):
        t = lax.dot_general(tri, piece, (((ct,), (0,)), ((), ())), preferred_element_type=F32)
        acc = t if acc is None else acc + t
    return acc


@jax.custom_vjp
def _cumsum_rows(tri, x):
    return _tri_dot(tri, x, 1)


_cumsum_rows.defvjp(lambda tri, x: (_tri_dot(tri, x, 1), tri),
                    lambda tri, g: (jnp.zeros_like(tri), _tri_dot(tri, g, 0)))


@jax.custom_vjp
def _unstack2(x):
    t = x.shape[0] // 2
    return x[:t] + x[t:]


_unstack2.defvjp(lambda x: (_unstack2(x), None), lambda _, g: (jnp.concatenate([g, g], axis=0),))


@jax.custom_vjp
def _last_row(x):
    return x[x.shape[0] - 1:, :]


def _last_row_bwd(_, g):
    rows = lax.broadcasted_iota(jnp.int32, (SCAN_T, g.shape[1]), 0)
    return (jnp.where(rows == SCAN_T - 1, jnp.broadcast_to(g, (SCAN_T, g.shape[1])), 0.0),)


_last_row.defvjp(lambda x: (_last_row(x), None), _last_row_bwd)


@jax.custom_vjp
def _solve_saved(n, rhs, minv, u):
    return u


def _solve_saved_bwd(res, du):
    minv, u = res
    drhs = _dotf(minv, du, 0, 0)
    return _dotf(drhs, u, 1, 1), drhs, jnp.zeros_like(minv), jnp.zeros_like(u)


_solve_saved.defvjp(lambda n, rhs, minv, u: (u, (minv, u)), _solve_saved_bwd)


def scan_chunk(tri, strict, incl, m0, m1, eye, r, lw, k, v, a, b, s0, saved=None):
    lower = strict > 0
    lower_incl = incl > 0

    def stack(x):
        return jnp.concatenate([x * m0, x * m1], axis=0)

    def dots(xs, ys, ca, cb, mask=None):
        out = [_dotf(x, y, ca, cb) for x, y in zip(xs, ys)]
        return out if mask is None else [jnp.where(mask, o, 0.0) for o in out]

    cl = [_cumsum_rows(tri, x) for x in lw]
    gam = [jnp.exp(c) for c in cl]
    ginv = [jnp.exp(-c) for c in cl]
    a_s = [stack(x * jnp.exp(c - w)) for x, c, w in zip(a, cl, lw)]
    r_s = [stack(x * g) for x, g in zip(r, gam)]
    b_s = [stack(x * g) for x, g in zip(b, ginv)]
    k_s = [stack(x * g) for x, g in zip(k, ginv)]
    v_s = [stack(x) for x in v]
    n_ab = dots(a_s, b_s, 1, 1, lower)
    n_ak = dots(a_s, k_s, 1, 1, lower)
    r_ab = dots(r_s, b_s, 1, 1, lower_incl)
    r_ak = dots(r_s, k_s, 1, 1, lower_incl)
    rhs = [x + y for x, y in zip(dots(a_s, s0, 1, 1), dots(n_ak, v_s, 1, 0))]
    if saved is None:
        minv = [eye + n for n in n_ab]
        p = n_ab
        for _ in range(5):
            p = dots(p, p, 1, 0)
            minv = [m + mp for m, mp in zip(minv, dots(minv, p, 1, 0))]
        u_s = dots(minv, rhs, 1, 0)
    else:
        minv = saved[0]
        u_s = [_solve_saved(n, x, m, u) for n, x, m, u in zip(n_ab, rhs, *saved)]
    y = [_unstack2(x0 + x1 + x2)
         for x0, x1, x2 in zip(dots(r_s, s0, 1, 1), dots(r_ab, u_s, 1, 0), dots(r_ak, v_s, 1, 0))]
    g_end = [_last_row(g) for g in gam]
    s1 = [s * g + x + z for s, g, x, z in zip(s0, g_end, dots(u_s, [x * g for x, g in zip(b_s, g_end)], 0, 0),
                                              dots(v_s, [x * g for x, g in zip(k_s, g_end)], 0, 0))]
    return y, s1, (minv, u_s)


SCAN_PAIRS = 8


def _scan_specs(consts, order):
    row = pl.BlockSpec((SCAN_T, PAIR * SCAN_PAIRS), lambda p, c: (order(c), p))
    state = pl.BlockSpec((None, SCAN_PAIRS, PAIR, PAIR), lambda p, c: (order(c), p, 0, 0))
    return row, state, [pl.BlockSpec(x.shape, _zero_map(x.ndim)) for x in consts]


def _pair_lanes(q):
    return slice(q * PAIR, (q + 1) * PAIR)


def scan_fwd(r, lw, k, v, a, b, shards=()):
    lp = r.shape[0]
    nch = lp // SCAN_T
    npair = D_MODEL // PAIR
    ng = len(shards)
    consts = _scan_consts()
    row, state, cspecs = _scan_specs(consts, lambda c: c)

    def body(tri, strict, incl, masks, eye, r_ref, lw_ref, k_ref, v_ref, a_ref, b_ref, *rest):
        src, (y_ref, s_ref, minv_ref, u_ref), dst = rest[:ng], rest[ng:ng + 4], rest[ng + 4:2 * ng + 4]
        carry = rest[2 * ng + 4]
        first = jnp.logical_and(pl.program_id(0) == 0, pl.program_id(1) == 0)
        last = jnp.logical_and(pl.program_id(0) == npair // SCAN_PAIRS - 1, pl.program_id(1) == nch - 1)
        if ng:
            sends, arrivals = gather_copies(src, dst, *rest[2 * ng + 5:])

            @pl.when(first)
            def _():
                for cp in sends:
                    cp.start()

        @pl.when(pl.program_id(1) == 0)
        def _():
            carry[...] = jnp.zeros_like(carry)

        pairs = range(SCAN_PAIRS)
        s0 = [carry[q] for q in pairs]
        rows = [[ref[:, _pair_lanes(q)] for q in pairs] for ref in (r_ref, lw_ref, k_ref, v_ref, a_ref, b_ref)]
        y, s1, (minv, u) = scan_chunk(tri[...], strict[...], incl[...], masks[0:1, :], masks[1:2, :], eye[...],
                                      *rows, s0)
        for q in pairs:
            s_ref[q] = s0[q]
            minv_ref[q] = minv[q]
            u_ref[q] = u[q]
            y_ref[:, _pair_lanes(q)] = y[q]
            carry[q] = s1[q]

        if ng:
            @pl.when(last)
            def _():
                for cp in arrivals:
                    cp.wait_recv()
                for cp in sends:
                    cp.wait_send()

    mats = jax.ShapeDtypeStruct((nch, npair, PAIR, PAIR), F32)
    out = pl.pallas_call(
        body, name="rwkv_scan_fwd", grid=(npair // SCAN_PAIRS, nch), in_specs=cspecs + [row] * 6 + [ANY] * ng,
        out_specs=[row, state, state, state] + [ANY] * ng,
        out_shape=[jax.ShapeDtypeStruct((lp, D_MODEL), F32), mats, mats, mats] + gathered_shapes(shards),
        scratch_shapes=[pltpu.VMEM((SCAN_PAIRS, PAIR, PAIR), F32)] + (_sem_scratch(ng * len(XY_FLIPS)) if ng else []),
        compiler_params=_params(),
    )(*consts, r, lw, k, v, a, b, *shards)
    return out[:4], fill_own(out[4:], shards)


def scan_bwd(r, lw, k, v, a, b, saved, dy, parts=()):
    lp = r.shape[0]
    nch = lp // SCAN_T
    npair = D_MODEL // PAIR
    consts = _scan_consts()
    row, state, cspecs = _scan_specs(consts, lambda c: nch - 1 - c)

    ng = len(parts)

    def body(tri, strict, incl, masks, eye, r_ref, lw_ref, k_ref, v_ref, a_ref, b_ref, s_ref, minv_ref, u_ref,
             dy_ref, *rest):
        src, (dr_ref, dlw_ref, dk_ref, dv_ref, da_ref, db_ref), dst = rest[:ng], rest[ng:ng + 6], rest[ng + 6:2 * ng + 6]
        carry = rest[2 * ng + 6]
        first = jnp.logical_and(pl.program_id(0) == 0, pl.program_id(1) == 0)
        last = jnp.logical_and(pl.program_id(0) == npair // SCAN_PAIRS - 1, pl.program_id(1) == nch - 1)
        if ng:
            sends, arrivals = chip_exchange_copies(src, dst, *rest[2 * ng + 7:])

            @pl.when(first)
            def _():
                for cp in sends:
                    cp.start()

        @pl.when(pl.program_id(1) == 0)
        def _():
            carry[...] = jnp.zeros_like(carry)

        pairs = range(SCAN_PAIRS)
        kept = ([minv_ref[q] for q in pairs], [u_ref[q] for q in pairs])

        def fn(*args):
            y, s1, _ = scan_chunk(tri[...], strict[...], incl[...], masks[0:1, :], masks[1:2, :], eye[...], *args,
                                  saved=kept)
            return y, s1

        rows = [[ref[:, _pair_lanes(q)] for q in pairs] for ref in (r_ref, lw_ref, k_ref, v_ref, a_ref, b_ref)]
        _, vjp = jax.vjp(fn, *rows, [s_ref[q] for q in pairs])
        grads = vjp(([dy_ref[:, _pair_lanes(q)] for q in pairs], [carry[q] for q in pairs]))
        for q in pairs:
            for ref, g in zip((dr_ref, dlw_ref, dk_ref, dv_ref, da_ref, db_ref), grads[:6]):
                ref[:, _pair_lanes(q)] = g[q]
            carry[q] = grads[6][q]

        if ng:
            @pl.when(last)
            def _():
                for cp in arrivals:
                    cp.wait_recv()
                for cp in sends:
                    cp.wait_send()

    out = pl.pallas_call(
        body, name="rwkv_scan_bwd", grid=(npair // SCAN_PAIRS, nch),
        in_specs=cspecs + [row] * 6 + [state] * 3 + [row] + [ANY] * ng, out_specs=[row] * 6 + [ANY] * ng,
        out_shape=[jax.ShapeDtypeStruct((lp, D_MODEL), F32)] * 6 + [jax.ShapeDtypeStruct(p.shape, p.dtype) for p in parts],
        scratch_shapes=[pltpu.VMEM((SCAN_PAIRS, PAIR, PAIR), F32)] + (_sem_scratch(ng * len(XY_FLIPS)) if ng else []),
        compiler_params=_params(),
    )(*consts, r, lw, k, v, a, b, *saved, dy, *parts)
    return out[:6], out[6:]


def _spread_matrices():
    rep = np.zeros((N_HEADS_KV, KV_DIM, KVW), np.float32)
    for h in range(N_HEADS_KV):
        for g in range(GROUP):
            rep[h, h * HEAD_DIM + np.arange(HEAD_DIM), g * HEAD_DIM + np.arange(HEAD_DIM)] = 1.0
    return jnp.asarray(rep, BF16)


KV_HEADS = range(N_HEADS_KV)


def _attn_common(n, q_ref, kp, kc, vp, vc, rep_ref, sink_ref):
    lane = lax.broadcasted_iota(jnp.int32, (1, KVW), 1)
    gmask = [(lane // HEAD_DIM == g).astype(F32) for g in range(GROUP)]
    kk = jnp.concatenate([kp, kc], axis=0)
    vv = jnp.concatenate([vp, vc], axis=0)
    qs = [q_ref[:, h * KVW:(h + 1) * KVW] for h in KV_HEADS]
    q_s = [jnp.concatenate([q * gmask[g] for g in range(GROUP)], axis=0) for q in qs]
    keys = [_dot(kk, rep_ref[h], 1, 0) for h in KV_HEADS]
    vals = [_dot(vv, rep_ref[h], 1, 0) for h in KV_HEADS]
    qi = lax.broadcasted_iota(jnp.int32, (GROUP * BLOCK, 2 * BLOCK), 0) % BLOCK
    kj = lax.broadcasted_iota(jnp.int32, (GROUP * BLOCK, 2 * BLOCK), 1)
    rel = BLOCK + qi - kj
    valid = (rel >= 0) & (rel < BLOCK) & ((n - 1) * BLOCK + kj >= PAD_FRONT)
    s = [jnp.where(valid, _dot(x, y, 1, 1) * (HEAD_DIM ** -0.5), -1e30) for x, y in zip(q_s, keys)]
    sink_col = [jnp.concatenate([jnp.broadcast_to(sink_ref[h, g:g + 1, 0:1], (BLOCK, 1)) for g in range(GROUP)],
                                axis=0) for h in KV_HEADS]
    m = [jnp.maximum(jnp.max(x, axis=-1, keepdims=True), c) for x, c in zip(s, sink_col)]
    ex = [jnp.exp(x - y) for x, y in zip(s, m)]
    ex_sink = [jnp.exp(c - y) for c, y in zip(sink_col, m)]
    inv = [1.0 / (jnp.sum(x, axis=-1, keepdims=True) + c) for x, c in zip(ex, ex_sink)]
    return (gmask, q_s, keys, vals, [x * y for x, y in zip(ex, inv)], [x * y for x, y in zip(ex_sink, inv)])


def _unstack_groups(x_s, gmask):
    out = None
    for g in range(GROUP):
        t = x_s[g * BLOCK:(g + 1) * BLOCK] * gmask[g]
        out = t if out is None else out + t
    return out


def _attn_specs():
    qspec = pl.BlockSpec((BLOCK, D_MODEL), lambda n: (n, 0))
    cur = pl.BlockSpec((BLOCK, KV_DIM), lambda n: (n, 0))
    prev = pl.BlockSpec((BLOCK, KV_DIM), lambda n: (jnp.maximum(n - 1, 0), 0))
    rep = pl.BlockSpec((N_HEADS_KV, KV_DIM, KVW), lambda n: (0, 0, 0))
    sink = pl.BlockSpec((N_HEADS_KV, 8, PAIR), lambda n: (0, 0, 0))
    return qspec, cur, prev, rep, sink


def _attn_params():
    return pltpu.CompilerParams(dimension_semantics=("arbitrary",), vmem_limit_bytes=VMEM_LIMIT)


def attn_fwd(q, k, v, sinks_b):
    lp = q.shape[0]
    qspec, cur, prev, rep, sink = _attn_specs()

    def body(q_ref, kp_ref, kc_ref, vp_ref, vc_ref, rep_ref, sink_ref, o_ref):
        gmask, _, _, vals, p, _ = _attn_common(pl.program_id(0), q_ref, kp_ref[...], kc_ref[...], vp_ref[...],
                                               vc_ref[...], rep_ref, sink_ref)
        o = [_dot(x, y, 1, 0) for x, y in zip(p, vals)]
        for h in KV_HEADS:
            o_ref[:, h * KVW:(h + 1) * KVW] = _unstack_groups(o[h], gmask)

    return pl.pallas_call(
        body, name="swa_fwd", grid=(lp // BLOCK,), in_specs=[qspec, prev, cur, prev, cur, rep, sink],
        out_specs=qspec, out_shape=jax.ShapeDtypeStruct((lp, D_MODEL), F32), compiler_params=_attn_params(),
    )(q, k, k, v, v, _spread_matrices(), sinks_b)


def attn_bwd(q, k, v, sinks_b, do):
    lp = q.shape[0]
    qspec, cur, prev, rep, sink = _attn_specs()

    def body(q_ref, kp_ref, kc_ref, vp_ref, vc_ref, rep_ref, sink_ref, do_ref, dq_ref, dkc_ref, dkp_ref, dvc_ref,
             dvp_ref, dsink_ref):
        n = pl.program_id(0)
        gmask, q_s, keys, vals, p, p_sink = _attn_common(n, q_ref, kp_ref[...], kc_ref[...], vp_ref[...], vc_ref[...],
                                                         rep_ref, sink_ref)
        do_s = [jnp.concatenate([do_ref[:, h * KVW:(h + 1) * KVW] * gmask[g] for g in range(GROUP)], axis=0)
                for h in KV_HEADS]
        dp = [_dot(x, y, 1, 1) for x, y in zip(do_s, vals)]
        delta = [jnp.sum(x * y, axis=-1, keepdims=True) for x, y in zip(p, dp)]
        ds = [x * (y - z) * (HEAD_DIM ** -0.5) for x, y, z in zip(p, dp, delta)]
        dq = [_dot(x, y, 1, 0) for x, y in zip(ds, keys)]
        dkeys_s = [_dot(x, y, 0, 0) for x, y in zip(ds, q_s)]
        dvals_s = [_dot(x, y, 0, 0) for x, y in zip(p, do_s)]
        dkeys = [_exact_dot(x, rep_ref[h], cb=1) for h, x in enumerate(dkeys_s)]
        dvals = [_exact_dot(x, rep_ref[h], cb=1) for h, x in enumerate(dvals_s)]
        dk_all = (dkeys[0] + dkeys[1]) + (dkeys[2] + dkeys[3])
        dv_all = (dvals[0] + dvals[1]) + (dvals[2] + dvals[3])
        dkp_ref[...] = dk_all[:BLOCK]
        dkc_ref[...] = dk_all[BLOCK:]
        dvp_ref[...] = dv_all[:BLOCK]
        dvc_ref[...] = dv_all[BLOCK:]
        dsinks = []
        for h in KV_HEADS:
            dq_ref[:, h * KVW:(h + 1) * KVW] = _unstack_groups(dq[h], gmask)
            dsk = -(p_sink[h] * delta[h])
            rows = [jnp.broadcast_to(jnp.sum(dsk[g * BLOCK:(g + 1) * BLOCK], axis=0, keepdims=True), (1, PAIR))
                    for g in range(GROUP)]
            dsinks.append(jnp.concatenate(rows + [jnp.zeros((8 - GROUP, PAIR), F32)], axis=0))

        @pl.when(n == 0)
        def _():
            for h in KV_HEADS:
                dsink_ref[h] = dsinks[h]

        @pl.when(n > 0)
        def _():
            for h in KV_HEADS:
                dsink_ref[h] += dsinks[h]

    kv = jax.ShapeDtypeStruct((lp, KV_DIM), F32)
    return pl.pallas_call(
        body, name="swa_bwd", grid=(lp // BLOCK,), in_specs=[qspec, prev, cur, prev, cur, rep, sink, qspec],
        out_specs=[qspec, cur, cur, cur, cur, sink],
        out_shape=[jax.ShapeDtypeStruct((lp, D_MODEL), F32), kv, kv, kv, kv,
                   jax.ShapeDtypeStruct((N_HEADS_KV, 8, PAIR), F32)],
        compiler_params=_attn_params(),
    )(q, k, k, v, v, _spread_matrices(), sinks_b, do)


def _pick_tm(lp, want):
    for tm in (384, 192, 128, 64):
        if tm <= want and lp % tm == 0:
            return tm
    raise ValueError(lp)


def _acc(shape):
    return (tuple(shape), F32)


def _ff_all(w, layer):
    return (w, (N_FF_CHUNK, None, D_MODEL, D_MODEL), lambda c, i: (0, layer, 0, 0))


def _ff_one(w, layer):
    return (w, (None, None, D_MODEL, D_MODEL), lambda c, i: (c, layer, 0, 0))


def _mlp_layer_fwd(name, h, wup, wdown, layer, lg, lb, tm):
    def fn(c, i, h, wup, wdown, lg, lb):
        out = None
        for s in range(N_FF_CHUNK):
            t = mlp_chunk(wup[s], wdown[s], None, h)[0]
            out = t if out is None else out + t
        z = ALPHA * h + out
        return (_layer_norm(z, lg, lb), z), ()

    (h_out, z), _ = rowwise(name, fn, [h], [_ff_all(wup, layer), _ff_all(wdown, layer), lg, lb],
                            [(D_MODEL, F32), (D_MODEL, F32)], [], tm)
    return h_out, z


def _mlp_layer_bwd(name, h_in, z, dh_parts, wup, wdown, layer, lg, lb, tm):
    n_parts = len(dh_parts)

    def fn_ln(c, i, z, *rest):
        dh = rest[0]
        for extra in rest[1:n_parts]:
            dh = dh + extra
        _, vjp = jax.vjp(_layer_norm, z, rest[n_parts], rest[n_parts + 1])
        dz, dlg, dlb = vjp(dh)
        return (dz,), (dlg, dlb)

    (dz,), (dlg, dlb) = rowwise(name + "_ln", fn_ln, [z] + list(dh_parts), [lg, lb], [(D_MODEL, F32)],
                                [_acc((1, D_MODEL)), _acc((1, D_MODEL))], tm)

    def fn_mlp(c, i, h, dz, wup, wdown):
        tile = h.shape[0]
        (dx,), dws = vjp_taps(functools.partial(mlp_chunk, wup, wdown), [(tile, D_MODEL)] * 2, [h], dz)
        return (dx,), dws

    aspec = ((N_FF_CHUNK, D_MODEL, D_MODEL), F32, (None, D_MODEL, D_MODEL), lambda c, i: (c, 0, 0))
    (dx,), (dwup, dwdown) = rowwise(name + "_mm", fn_mlp, [h_in, dz], [_ff_one(wup, layer), _ff_one(wdown, layer)],
                                    [(D_MODEL, F32, True)], [aspec, aspec], tm, nc=N_FF_CHUNK)
    return dz, dx, dwup, dwdown, dlg, dlb


def _sum_parts(dz, dx):
    out = ALPHA * dz
    for s in range(N_FF_CHUNK):
        out = out + dx[s]
    return out


def local_step(x, loss_target, p, late=None, early_hook=None):
    seq = x.shape[0]
    lp = TOK0 + seq
    tm = _pick_tm(lp, 384)
    tms = _pick_tm(lp, 128)
    e, et = _head_matrices()
    h0 = jnp.concatenate([jnp.zeros((PAD_FRONT, D_MODEL), F32), p["meta_tokens"], x], axis=0)
    hp = jnp.concatenate([jnp.zeros((1, D_MODEL), F32), h0[:-1]], axis=0)
    tgt = jnp.concatenate([jnp.zeros((TOK0, D_MODEL), F32), loss_target], axis=0)
    pos = jnp.maximum(jnp.arange(lp, dtype=F32) - PAD_FRONT, 0.0)
    inv_freq = 1.0 / (ROPE_THETA ** (jnp.arange(0, HEAD_DIM, 2, dtype=F32) / HEAD_DIM))
    ang = pos[:, None] * inv_freq[None, :]
    cos = jnp.tile(jnp.cos(ang), (1, PAIR // (HEAD_DIM // 2)))
    sin = jnp.tile(jnp.sin(ang), (1, PAIR // (HEAD_DIM // 2)))

    pre_vec = [p["a_mu"][j:j + 1] for j in range(6)] + [p["a_w0"], p["a_a0"], p["a_k_k"], p["a_k_a"]]
    pre_w = [p["a_w_r"], p["a_w_k"], p["a_w_v"], p["a_w1"], p["a_w2"], p["a_a1"], p["a_a2"], p["a_g1"], p["a_g2"]]
    n_vec = len(pre_vec)

    def fn_pre(c, i, h, hp, e, et, *ws):
        return rwkv_pre(e, et, ws[n_vec:], None, h, hp, *ws[:n_vec])[0], ()

    (r, lw, k2, v, an, bn, g), _ = rowwise("rwkv_pre", fn_pre, [h0, hp], [e, et] + pre_vec + pre_w,
                                           [(D_MODEL, F32)] * 7, [], tms)
    (y, *scan_saved), late_gathered = scan_fwd(r, lw, k2, v, an, bn, late[0] if late else ())
    if late:
        p = {**p, **late[1](late_gathered)}

    post_c =[p["a_w_o"], p["a_gn_w"], p["a_gn_b"], p["a_r_k"], p["ln_g00"], p["ln_b00"]]

    def fn_post(c, i, y, r, k2, v, g, h0, e, et, w_o, *vecs):
        return (rwkv_post(e, et, w_o, None, y, r, k2, v, g, h0, *vecs)[0],), ()

    (h1,), _ = rowwise("rwkv_post", fn_post, [y, r, k2, v, g, h0], [e, et] + post_c, [(D_MODEL, F32)], [], tm)
    h2, z2 = _mlp_layer_fwd("mlp0_fwd", h1, p["mlp_up"], p["mlp_down"], 0, p["ln_g01"], p["ln_b01"], tm)

    qkv_w = [p["b_w_q"], p["kv_w_k"], p["kv_w_v"]]

    def fn_qkv(c, i, h, cos, sin, wq, wk, wv):
        return qkv_proj(cos, sin, wq, wk, wv, None, h)[0], ()

    (q, k, vv), _ = rowwise("qkv_proj", fn_qkv, [h2, cos, sin], qkv_w,
                            [(D_MODEL, F32), (KV_DIM, F32), (KV_DIM, F32)], [], tm)
    sinks_b = jnp.broadcast_to(p["b_sinks"].reshape(N_HEADS_KV, GROUP, 1), (N_HEADS_KV, GROUP, PAIR))
    sinks_b = jnp.concatenate([sinks_b, jnp.zeros((N_HEADS_KV, 8 - GROUP, PAIR), F32)], axis=1)
    o = attn_fwd(q, k, vv, sinks_b)

    ao_c = [p["b_w_o"], p["ln_g10"], p["ln_b10"]]

    def fn_ao(c, i, o, h, w_o, lg, lb):
        return (attn_out(w_o, None, o, h, lg, lb)[0],), ()

    (h3,), _ = rowwise("attn_out", fn_ao, [o, h2], ao_c, [(D_MODEL, F32)], [], tm)
    h4, z4 = _mlp_layer_fwd("mlp1_fwd", h3, p["mlp_up"], p["mlp_down"], 1, p["ln_g11"], p["ln_b11"], tm)

    def fn_loss(c, i, h4, tgt):
        real = (_row_ids(i, tm) >= TOK0).astype(F32)
        err = (h4 - tgt) * real
        part = 0.5 * jnp.sum(jnp.sum(err * err, axis=-1, keepdims=True), axis=0, keepdims=True) / D_MODEL
        return (err * (1.0 / D_MODEL),), (jnp.broadcast_to(part, (8, PAIR)),)

    (dh4,), (loss_acc,) = rowwise("loss", fn_loss, [h4, tgt], [], [(D_MODEL, F32)], [_acc((8, PAIR))], tm)
    loss = loss_acc[0, 0]

    grads = {}
    dz4, dx4, grads["mlp_up1"], grads["mlp_down1"], grads["ln_g11"], grads["ln_b11"] = _mlp_layer_bwd(
        "mlp1_bwd", h3, z4, [dh4], p["mlp_up"], p["mlp_down"], 1, p["ln_g11"], p["ln_b11"], tm)

    def fn_ao_b(c, i, dz, dx, o, h, w_o, lg, lb):
        (do, dh, dlg, dlb), (dw_o,) = vjp_taps(functools.partial(attn_out, w_o), [(tms, D_MODEL)], [o, h, lg, lb],
                                               _sum_parts(dz, dx))
        return (do, dh), (dw_o, dlg, dlb)

    (do, dh2_a), (grads["b_w_o"], grads["ln_g10"], grads["ln_b10"]) = rowwise(
        "attn_out_bwd", fn_ao_b, [dz4, dx4, o, h2], ao_c, [(D_MODEL, F32)] * 2,
        [_acc((D_MODEL, D_MODEL)), _acc((1, D_MODEL)), _acc((1, D_MODEL))], tms)

    dq, dkc, dkp, dvc, dvp, dsinks = attn_bwd(q, k, vv, sinks_b, do)
    grads["b_sinks"] = dsinks[:, :GROUP, 0].reshape(1, N_HEADS)
    zblk = jnp.zeros((BLOCK, KV_DIM), F32)
    dkp_s = jnp.concatenate([dkp[BLOCK:], zblk], axis=0)
    dvp_s = jnp.concatenate([dvp[BLOCK:], zblk], axis=0)

    def fn_qkv_b(c, i, h, cos, sin, dq, dkc, dkp, dvc, dvp, wq, wk, wv):
        return vjp_taps(functools.partial(qkv_proj, cos, sin, wq, wk, wv),
                        [(tms, D_MODEL), (tms, KV_DIM), (tms, KV_DIM)], [h], (dq, dkc + dkp, dvc + dvp))

    (dh2_q,), (grads["b_w_q"], grads["kv_w_k"], grads["kv_w_v"]) = rowwise(
        "qkv_proj_bwd", fn_qkv_b, [h2, cos, sin, dq, dkc, dkp_s, dvc, dvp_s], qkv_w, [(D_MODEL, F32)],
        [_acc((D_MODEL, D_MODEL)), _acc((D_MODEL, KV_DIM)), _acc((D_MODEL, KV_DIM))], tms)

    dz2, dx2, grads["mlp_up0"], grads["mlp_down0"], grads["ln_g01"], grads["ln_b01"] = _mlp_layer_bwd(
        "mlp0_bwd", h1, z2, [dh2_a, dh2_q], p["mlp_up"], p["mlp_down"], 0, p["ln_g01"], p["ln_b01"], tm)

    def fn_post_b(c, i, dz, dx, y, r, k2, v, g, h0, e, et, w_o, *vecs):
        out, dws = vjp_taps(functools.partial(rwkv_post, e, et, w_o), [(tms, D_MODEL)],
                            [y, r, k2, v, g, h0] + list(vecs), _sum_parts(dz, dx))
        return out[:6], tuple(dws) + tuple(out[6:])

    (dy, dr_c, dk_c, dv_c, dg, dh0_c), post_g = rowwise(
        "rwkv_post_bwd", fn_post_b, [dz2, dx2, y, r, k2, v, g, h0], [e, et] + post_c, [(D_MODEL, F32)] * 6,
        [_acc((D_MODEL, D_MODEL))] + [_acc((1, D_MODEL))] * 5, tms)
    for name, val in zip(["a_w_o", "a_gn_w", "a_gn_b", "a_r_k", "ln_g00", "ln_b00"], post_g):
        grads[name] = val

    (dr_s, dlw, dk_s, dv_s, dan, dbn), early_from_chips = scan_bwd(r, lw, k2, v, an, bn, scan_saved, dy,
                                                                   early_hook(grads) if early_hook else ())

    def fn_pre_b(c, i, h, hp, dr_c, dr_s, dlw, dk_c, dk_s, dv_c, dv_s, dan, dbn, dg, e, et, *ws):
        real = (_row_ids(i, tms) >= PAD_FRONT).astype(F32)
        cot = tuple(t * real for t in (dr_c + dr_s, dlw, dk_c + dk_s, dv_c + dv_s, dan, dbn, dg))
        out, dws = vjp_taps(functools.partial(rwkv_pre, e, et, ws[n_vec:]), [(tms, n) for n in PRE_TAPS],
                            [h, hp] + list(ws[:n_vec]), cot)
        return out[:2], tuple(out[2:]) + tuple(dws)

    (dh0_p, dhp), pre_g = rowwise(
        "rwkv_pre_bwd", fn_pre_b, [h0, hp, dr_c, dr_s, dlw, dk_c, dk_s, dv_c, dv_s, dan, dbn, dg],
        [e, et] + pre_vec + pre_w, [(D_MODEL, F32)] * 2,
        [_acc((1, D_MODEL))] * n_vec + [_acc(w.shape) for w in pre_w], tms)
    grads["a_mu"] = jnp.concatenate(pre_g[:6], axis=0)
    for name, val in zip(["a_w0", "a_a0", "a_k_k", "a_k_a", "a_w_r", "a_w_k", "a_w_v", "a_w1", "a_w2", "a_a1",
                          "a_a2", "a_g1", "a_g2"], pre_g[6:]):
        grads[name] = val

    dhp_s = jnp.concatenate([dhp[1:], jnp.zeros((1, D_MODEL), F32)], axis=0)

    def fn_add(c, i, a, b, d):
        return (a + b + d,), ()

    (dh0,), _ = rowwise("grad_h0", fn_add, [dh0_c, dh0_p, dhp_s], [], [(D_MODEL, F32)], [], tm)
    grads["meta_tokens"] = dh0[PAD_FRONT:TOK0]
    return loss, dh0[TOK0:], grads, early_from_chips


ANY = pl.BlockSpec(memory_space=pl.ANY)
XY_FLIPS = ((0, 1), (1, 0), (1, 1))
ALL_FLIPS = tuple((e >> 2 & 1, e >> 1 & 1, e & 1) for e in range(1, N_DEV))


def _flip(v, bit):
    return 1 - v if bit else v


def _sem_scratch(n):
    return [pltpu.SemaphoreType.DMA((n,)), pltpu.SemaphoreType.DMA((n,))]


def gather_copies(src, dst, send_sems, recv_sems):
    npeer = len(XY_FLIPS)
    x, y, c = lax.axis_index("x"), lax.axis_index("y"), lax.axis_index("c")

    def copy(k, j, slot):
        fx, fy = XY_FLIPS[j]
        return pltpu.make_async_remote_copy(
            src_ref=src[k], dst_ref=dst[k].at[slot], send_sem=send_sems.at[k * npeer + j],
            recv_sem=recv_sems.at[k * npeer + j], device_id=(_flip(x, fx), _flip(y, fy), c), device_id_type=MESH)

    sends = [copy(k, j, 2 * x + y) for k in range(len(src)) for j in range(npeer)]
    arrivals = [copy(k, j, 2 * _flip(x, fx) + _flip(y, fy)) for k in range(len(src))
                for j, (fx, fy) in enumerate(XY_FLIPS)]
    return sends, arrivals


def gathered_shapes(shards):
    return [jax.ShapeDtypeStruct((N_SHARD,) + s.shape, s.dtype) for s in shards]


def fill_own(gathered, shards):
    if not shards:
        return []
    slot = 2 * lax.axis_index("x") + lax.axis_index("y")
    return [lax.dynamic_update_index_in_dim(g, s, slot, 0) for g, s in zip(gathered, shards)]


def all_gather_shards(shards):
    n = len(shards)

    def body(*refs):
        sends, arrivals = gather_copies(refs[:n], refs[n:2 * n], *refs[2 * n:])
        for cp in sends:
            cp.start()
        for cp in arrivals:
            cp.wait_recv()
        for cp in sends:
            cp.wait_send()

    out = pl.pallas_call(body, name="gather_weights", in_specs=[ANY] * n, out_specs=[ANY] * n,
                         out_shape=gathered_shapes(shards), scratch_shapes=_sem_scratch(n * len(XY_FLIPS)))(*shards)
    return fill_own(out, shards)


def placement():
    x, y, c = lax.axis_index("x"), lax.axis_index("y"), lax.axis_index("c")
    me = 2 * x + y
    others = [j + (j >= me).astype(jnp.int32) for j in range(N_SHARD - 1)]
    return jnp.stack([c, me] + others).astype(jnp.int32)


def pair_exchange(name, sources):
    n = len(sources)

    def body(*refs):
        src, got = refs[:n], refs[n:2 * n]
        send_sems, recv_sems = refs[2 * n:]
        x, y, c = lax.axis_index("x"), lax.axis_index("y"), lax.axis_index("c")

        def copy(k):
            half = sources[k].shape[1] // 2
            theirs = src[k].at[:, pl.ds(pl.multiple_of((1 - c) * half, 8), half), :]
            return pltpu.make_async_remote_copy(
                src_ref=theirs, dst_ref=got[k], send_sem=send_sems.at[k], recv_sem=recv_sems.at[k],
                device_id=(x, y, 1 - c), device_id_type=MESH)

        sends = [copy(k) for k in range(n)]
        for cp in sends:
            cp.start()
        for cp in sends:
            cp.wait_recv()
        for cp in sends:
            cp.wait_send()

    halves = [jax.ShapeDtypeStruct((s.shape[0], s.shape[1] // 2, s.shape[2]), s.dtype) for s in sources]
    return pl.pallas_call(body, name=name, in_specs=[ANY] * n, out_specs=[ANY] * n,
                          out_shape=halves, scratch_shapes=_sem_scratch(n))(*sources)


def chip_exchange(parts):
    n = len(parts)

    def body(*refs):
        sends, arrivals = chip_exchange_copies(refs[:n], refs[n:2 * n], *refs[2 * n:])
        for cp in sends:
            cp.start()
        for cp in arrivals:
            cp.wait_recv()
        for cp in sends:
            cp.wait_send()

    return pl.pallas_call(
        body, name="grads_chip_exchange", in_specs=[ANY] * n, out_specs=[ANY] * n,
        out_shape=[jax.ShapeDtypeStruct(p.shape, p.dtype) for p in parts],
        scratch_shapes=_sem_scratch(n * len(XY_FLIPS)),
    )(*parts)


def chip_exchange_copies(src, dst, send_sems, recv_sems):
    npeer = len(XY_FLIPS)
    x, y, c = lax.axis_index("x"), lax.axis_index("y"), lax.axis_index("c")
    me = 2 * x + y

    def copy(k, j, sending):
        fx, fy = XY_FLIPS[j]
        px, py = _flip(x, fx), _flip(y, fy)
        peer = 2 * px + py
        return pltpu.make_async_remote_copy(
            src_ref=src[k].at[peer], dst_ref=dst[k].at[me if sending else peer],
            send_sem=send_sems.at[k * npeer + j], recv_sem=recv_sems.at[k * npeer + j],
            device_id=(px, py, c), device_id_type=MESH)

    pairs = [(k, j) for k in range(len(src)) for j in range(npeer)]
    return [copy(k, j, True) for k, j in pairs], [copy(k, j, False) for k, j in pairs]


def sibling_share(halves):
    n = len(halves)

    def body(*refs):
        src, got = refs[:n], refs[n:2 * n]
        send_sems, recv_sems = refs[2 * n:]
        x, y, c = lax.axis_index("x"), lax.axis_index("y"), lax.axis_index("c")
        sends = [pltpu.make_async_remote_copy(
            src_ref=src[k], dst_ref=got[k], send_sem=send_sems.at[k], recv_sem=recv_sems.at[k],
            device_id=(x, y, 1 - c), device_id_type=MESH) for k in range(n)]
        for cp in sends:
            cp.start()
        for cp in sends:
            cp.wait_recv()
        for cp in sends:
            cp.wait_send()

    return pl.pallas_call(
        body, name="grads_sibling_share", in_specs=[ANY] * n, out_specs=[ANY] * n,
        out_shape=[jax.ShapeDtypeStruct(h.shape, h.dtype) for h in halves], scratch_shapes=_sem_scratch(n),
    )(*halves)


ADD_TILE_ELEMS = 512 * 1024


def _row_tile(rows, cols):
    return max(t for t in range(8, rows + 1, 8) if rows % t == 0 and t * cols <= ADD_TILE_ELEMS)


def _prefetch_call(body, name, place, grid, in_specs, out_specs, out_shape, args):
    return pl.pallas_call(
        body, name=name, out_shape=out_shape,
        grid_spec=pltpu.PrefetchScalarGridSpec(num_scalar_prefetch=1, grid=grid, in_specs=in_specs,
                                               out_specs=out_specs),
        compiler_params=pltpu.CompilerParams(dimension_semantics=("arbitrary",) * len(grid),
                                             vmem_limit_bytes=VMEM_LIMIT),
    )(place, *args)


def pair_add(name, place, src, got, dtype):
    n4, half, cols = got.shape
    tile = _row_tile(half, cols)
    nt = half // tile

    def body(pr, a_ref, b_ref, o_ref):
        o_ref[...] = (a_ref[...] + b_ref[...]).astype(o_ref.dtype)

    mine = pl.BlockSpec((None, tile, cols), lambda s, i, pr: (s, pr[0] * nt + i, 0))
    blk = pl.BlockSpec((None, tile, cols), lambda s, i, pr: (s, i, 0))
    return _prefetch_call(body, name, place, (n4, nt), [mine, blk], blk,
                          jax.ShapeDtypeStruct(got.shape, dtype), (src, got))


def chip_add(name, place, part, from_chips):
    _, half, cols = part.shape
    tile = _row_tile(half, cols)

    def body(pr, own_ref, r0_ref, r1_ref, r2_ref, o_ref):
        me = pr[1]
        own, r0, r1, r2 = (r[...].astype(F32) for r in (own_ref, r0_ref, r1_ref, r2_ref))
        t0 = jnp.where(me == 0, own, r0)
        t1 = jnp.where(me == 0, r0, jnp.where(me == 1, own, r1))
        t2 = jnp.where(me <= 1, r1, jnp.where(me == 2, own, r2))
        t3 = jnp.where(me == 3, own, r2)
        o_ref[...] = ((t0 + t1) + t2) + t3

    def slab(j):
        return pl.BlockSpec((None, tile, cols), lambda i, pr: (pr[j], i, 0))

    return _prefetch_call(body, name, place, (half // tile,), [slab(1), slab(2), slab(3), slab(4)],
                          pl.BlockSpec((tile, cols), lambda i, pr: (i, 0)),
                          jax.ShapeDtypeStruct((half, cols), F32), (part, from_chips, from_chips, from_chips))


def pair_sums(tag, place, sources, narrow):
    got = pair_exchange("grads_pair_exchange_" + tag, sources)
    return [pair_add(f"grads_pair_add_{tag}{k}", place, s, g, BF16 if nar else F32)
            for k, (s, g, nar) in enumerate(zip(sources, got, narrow))]


def finish_sums(place, parts, from_chips):
    halves = [chip_add(f"grads_chip_add{k}", place, p, f) for k, (p, f) in enumerate(zip(parts, from_chips))]
    return list(zip(halves, sibling_share(halves)))


ADAM_ROWS = 256


def adamw_update(name, place, halves, w, m, v):
    nsub, rows, cols = w.shape
    half = rows // 2
    tr = ADAM_ROWS if half % ADAM_ROWS == 0 else half
    nth = half // tr

    def body(pr, *refs):
        g_refs, (w_ref, m_ref, v_ref, g_ref, d_ref, nm_ref, nv_ref) = refs[:2 * nsub], refs[2 * nsub:]
        l = pl.program_id(0)
        mine = (pl.program_id(1) // nth) == pr[0]
        g = None
        for s in range(nsub):
            gs = jnp.where(mine, g_refs[2 * s][...], g_refs[2 * s + 1][...])
            g = gs if g is None else jnp.where(l == s, gs, g)
        m2 = ADAM_B1 * m_ref[...] + (1.0 - ADAM_B1) * g
        v2 = ADAM_B2 * v_ref[...] + (1.0 - ADAM_B2) * (g * g)
        m_hat = m2 / (1.0 - ADAM_B1 ** ADAM_STEP)
        v_hat = v2 / (1.0 - ADAM_B2 ** ADAM_STEP)
        g_ref[...] = g
        d_ref[...] = -ADAM_LR * (m_hat / (jnp.sqrt(v_hat) + ADAM_EPS) + ADAM_WD * w_ref[...])
        nm_ref[...] = m2
        nv_ref[...] = v2

    gblk = pl.BlockSpec((tr, cols), lambda l, i, pr: (i % nth, 0))
    blk = pl.BlockSpec((None, tr, cols), lambda l, i, pr: (l, i, 0))
    out = jax.ShapeDtypeStruct((nsub, rows, cols), F32)
    return _prefetch_call(body, name, place, (nsub, rows // tr), [gblk] * (2 * nsub) + [blk] * 3, [blk] * 4,
                          [out] * 4, [h for pair in halves for h in pair] + [w, m, v])


WEIGHT_NAMES = ("meta_tokens", "a_mu", "a_w_r", "a_w_k", "a_w_v", "a_w_o", "a_w0", "a_w1", "a_w2", "a_a0", "a_a1",
                "a_a2", "a_g1", "a_g2", "a_k_k", "a_k_a", "a_r_k", "a_gn_w", "a_gn_b", "kv_w_k", "kv_w_v", "b_w_q",
                "b_sinks", "b_w_o", "mlp_w_up", "mlp_w_down", "ln_g", "ln_b")
BIG_NAMES = ("a_w_r", "a_w_k", "a_w_v", "a_w_o", "b_w_q", "b_w_o")
EARLY_NAMES, LATE_NAMES = BIG_NAMES[:3], BIG_NAMES[3:]
PACK_MATS = (("kv_w_k", 256), ("kv_w_v", 256), ("a_w1", 64), ("a_a1", 64), ("a_g1", 128), ("a_w2", 64),
             ("a_a2", 64), ("a_g2", 128))
COLUMN_CUT = ("a_w2", "a_a2", "a_g2")
PACK_VECS = (("a_mu", 6), ("a_w0", 1), ("a_a0", 1), ("a_k_k", 1), ("a_k_a", 1), ("a_gn_w", 1), ("a_gn_b", 1),
             ("ln_g", 4), ("ln_b", 4), ("meta_tokens", 16))
PACK_REPL = (("a_r_k", 4), ("b_sinks", 1))
SHARD_W = D_MODEL // N_SHARD
N_MAT_ROWS = sum(r for _, r in PACK_MATS)
N_VEC_ROWS = sum(r for _, r in PACK_VECS)
N_PACK_ROWS = -(-(N_MAT_ROWS + N_VEC_ROWS + sum(r for _, r in PACK_REPL)) // 8) * 8
N_GATHER_VEC_ROWS = -(-N_VEC_ROWS // 8) * 8


def _pack_rows(arr):
    if arr.size == N_HEADS:
        return jnp.pad(arr.reshape(1, N_HEADS), ((0, 0), (0, SHARD_W - N_HEADS)))
    return arr.reshape(-1, SHARD_W)


def pack_small(get):
    parts = [_pack_rows(get(name)) for name, _ in PACK_MATS + PACK_VECS + PACK_REPL]
    used = sum(p.shape[0] for p in parts)
    return jnp.concatenate(parts + [jnp.zeros((N_PACK_ROWS - used, SHARD_W), F32)], axis=0)


def unpack_small(pack, shapes):
    out, off = {}, 0
    for name, rows in PACK_MATS + PACK_VECS + PACK_REPL:
        piece = pack[off:off + rows]
        off += rows
        out[name] = piece[:, :N_HEADS].reshape(shapes[name]) if name == "b_sinks" else piece.reshape(shapes[name])
    return out


def whole_weights(big_names, gathered_big, mats, vecs, a_r_k, b_sinks):
    p = {name: g.reshape(D_MODEL, D_MODEL) for name, g in zip(big_names, gathered_big)}
    off = 0
    for name, rows in PACK_MATS:
        piece = mats[:, off:off + rows]
        off += rows
        if name in COLUMN_CUT:
            p[name] = piece.transpose(1, 0, 2).reshape(rows, D_MODEL)
        else:
            p[name] = piece.reshape(D_MODEL, rows)
    v = vecs.transpose(1, 0, 2).reshape(-1, D_MODEL)
    off = 0
    for name, rows in PACK_VECS:
        p[name] = v[off:off + rows]
        off += rows
    for i in range(2):
        for j in range(2):
            p[f"ln_g{i}{j}"] = p["ln_g"][2 * i + j:2 * i + j + 1]
            p[f"ln_b{i}{j}"] = p["ln_b"][2 * i + j:2 * i + j + 1]
    p["a_r_k"] = a_r_k.reshape(1, D_MODEL)
    p["b_sinks"] = b_sinks
    return p


def small_grad_pack(g):
    parts = []
    for name, rows in PACK_MATS:
        if name in COLUMN_CUT:
            parts.append(g[name].reshape(rows, N_SHARD, SHARD_W).transpose(1, 0, 2))
        else:
            parts.append(g[name].reshape(N_SHARD, rows, SHARD_W))
    vec_rows = [g["a_mu"]] + [g[n] for n in ("a_w0", "a_a0", "a_k_k", "a_k_a", "a_gn_w", "a_gn_b")]
    vec_rows += [g[f"ln_g{i}{j}"] for i in range(2) for j in range(2)]
    vec_rows += [g[f"ln_b{i}{j}"] for i in range(2) for j in range(2)] + [g["meta_tokens"]]
    parts.append(jnp.concatenate(vec_rows, axis=0).reshape(N_VEC_ROWS, N_SHARD, SHARD_W).transpose(1, 0, 2))
    parts.append(jnp.broadcast_to(g["a_r_k"].reshape(1, -1, SHARD_W), (N_SHARD, D_MODEL // SHARD_W, SHARD_W)))
    sinks = jnp.pad(g["b_sinks"].reshape(1, 1, N_HEADS), ((0, 0), (0, 0), (0, SHARD_W - N_HEADS)))
    parts.append(jnp.broadcast_to(sinks, (N_SHARD, 1, SHARD_W)))
    used = sum(p.shape[1] for p in parts)
    parts.append(jnp.zeros((N_SHARD, N_PACK_ROWS - used, SHARD_W), F32))
    return jnp.concatenate(parts, axis=1)


def train_step(vals):
    w = {n: vals[n] for n in WEIGHT_NAMES}
    w_pack = pack_small(lambda n: w[n])
    early = [w[n][0].astype(BF16) for n in EARLY_NAMES]
    early += [w_pack[:N_MAT_ROWS].astype(BF16), w_pack[N_MAT_ROWS:N_MAT_ROWS + N_GATHER_VEC_ROWS]]
    gathered = all_gather_shards(early)
    ne = len(EARLY_NAMES)
    p = whole_weights(EARLY_NAMES, gathered[:ne], gathered[ne], gathered[ne + 1][:, :N_VEC_ROWS], w["a_r_k"],
                      w["b_sinks"])
    late = [w[n][0].astype(BF16) for n in LATE_NAMES] + [w["mlp_w_up"].astype(BF16), w["mlp_w_down"].astype(BF16)]
    nb = len(BIG_NAMES)

    def late_weights(got):
        out = {n: x.reshape(D_MODEL, D_MODEL) for n, x in zip(LATE_NAMES, got)}
        out["mlp_up"], out["mlp_down"] = got[len(LATE_NAMES):]
        return out

    place = placement()
    ready = {}

    def early_hook(g):
        srcs = [g[n].reshape(N_SHARD, SHARD_W, D_MODEL) for n in LATE_NAMES]
        srcs += [g["mlp_up0"], g["mlp_up1"], g["mlp_down0"], g["mlp_down1"]]
        ready["parts"] = pair_sums("early", place, srcs, [True] * len(srcs))
        return ready["parts"]

    loss, gx, g, early_from_chips = local_step(vals["x"][0], vals["loss_target"][0], p, (late, late_weights),
                                               early_hook)
    loss = lax.psum(loss, ("x", "y", "c"))
    srcs = [g[n].reshape(N_SHARD, SHARD_W, D_MODEL) for n in EARLY_NAMES] + [small_grad_pack(g)]
    rest = pair_sums("late", place, srcs, [True] * len(EARLY_NAMES) + [False])
    rest_from_chips = chip_exchange(rest)
    ne = len(EARLY_NAMES)
    halves = finish_sums(place, rest[:ne] + ready["parts"] + rest[ne:],
                         list(rest_from_chips[:ne]) + list(early_from_chips) + list(rest_from_chips[ne:]))

    res = {}
    for k, n in enumerate(BIG_NAMES):
        res[n] = adamw_update("adamw_" + n, place, halves[k:k + 1], w[n], vals["m_" + n], vals["v_" + n])
    for k, n in ((nb, "mlp_w_up"), (nb + 2, "mlp_w_down")):
        res[n] = adamw_update("adamw_" + n, place, halves[k:k + 2], w[n], vals["m_" + n], vals["v_" + n])
    packs = adamw_update("adamw_small", place, halves[-1:], w_pack[None], pack_small(lambda n: vals["m_" + n])[None],
                         pack_small(lambda n: vals["v_" + n])[None])
    shapes = {n: w[n].shape for n in WEIGHT_NAMES}
    small = [unpack_small(pk[0], shapes) for pk in packs]
    outs = [loss, gx[None]]
    for t in range(4):
        outs += [res[n][t] if n in res else small[t][n] for n in WEIGHT_NAMES]
    return tuple(outs)


def kernel(x, meta_tokens, a_mu, a_w_r, a_w_k, a_w_v, a_w_o, a_w0, a_w1, a_w2, a_a0, a_a1, a_a2, a_g1, a_g2, a_k_k,
           a_k_a, a_r_k, a_gn_w, a_gn_b, kv_w_k, kv_w_v, b_w_q, b_sinks, b_w_o, mlp_w_up, mlp_w_down, ln_g, ln_b,
           loss_target, m_meta_tokens, m_a_mu, m_a_w_r, m_a_w_k, m_a_w_v, m_a_w_o, m_a_w0, m_a_w1, m_a_w2, m_a_a0,
           m_a_a1, m_a_a2, m_a_g1, m_a_g2, m_a_k_k, m_a_k_a, m_a_r_k, m_a_gn_w, m_a_gn_b, m_kv_w_k, m_kv_w_v,
           m_b_w_q, m_b_sinks, m_b_w_o, m_mlp_w_up, m_mlp_w_down, m_ln_g, m_ln_b, v_meta_tokens, v_a_mu, v_a_w_r,
           v_a_w_k, v_a_w_v, v_a_w_o, v_a_w0, v_a_w1, v_a_w2, v_a_a0, v_a_a1, v_a_a2, v_a_g1, v_a_g2, v_a_k_k,
           v_a_k_a, v_a_r_k, v_a_gn_w, v_a_gn_b, v_kv_w_k, v_kv_w_v, v_b_w_q, v_b_sinks, v_b_w_o, v_mlp_w_up,
           v_mlp_w_down, v_ln_g, v_ln_b):
    return train_step(dict(locals()))
```

```python
import functools

import numpy as np
import jax
import jax.numpy as jnp
from jax import lax
from jax.experimental import pallas as pl
from jax.experimental.pallas import tpu as pltpu

F32 = jnp.float32
BF16 = jnp.bfloat16

D_MODEL = 1024
N_HEADS = 16
HEAD_DIM = 64
N_HEADS_KV = 4
GROUP = 4
KV_DIM = N_HEADS_KV * HEAD_DIM
N_META = 16
BLOCK = 128
PAD_FRONT = BLOCK - N_META
TOK0 = PAD_FRONT + N_META
N_FF_CHUNK = 4
N_SHARD = 4
N_DEV = 8
GN_EPS = 64e-5
LN_EPS = 1e-5
ROPE_THETA = 10000.0
ALPHA = 4.0 ** 0.25
ADAM_LR, ADAM_B1, ADAM_B2, ADAM_EPS, ADAM_WD, ADAM_STEP = 0.001, 0.9, 0.999, 1e-08, 0.01, 10
SCAN_T = 64
PAIR = 128
KVW = GROUP * HEAD_DIM
VMEM_LIMIT = 56 * 1024 * 1024
HI = lax.Precision.HIGHEST
MESH = pl.DeviceIdType.MESH


def _dot(a, b, ca, cb):
    return lax.dot_general(a.astype(BF16), b.astype(BF16), (((ca,), (cb,)), ((), ())),
                           preferred_element_type=F32)


@jax.custom_vjp
def mm(a, b):
    return _dot(a, b, 1, 0)


def _mm_fwd(a, b):
    return mm(a, b), b


def _mm_bwd(b, g):
    return _dot(g, b, 1, 1), jnp.zeros_like(b)


mm.defvjp(_mm_fwd, _mm_bwd)


def tmm(x, w, taps, xs):
    y = mm(x, w)
    if taps is not None:
        y = y + taps[len(xs)]
    xs.append(x)
    return y


def vjp_taps(core, tap_shapes, args, cot):
    taps = [jnp.zeros(s, F32) for s in tap_shapes]
    _, vjp, xs = jax.vjp(core, taps, *args, has_aux=True)
    out = vjp(cot)
    return out[1:], [_dot(x, g, 0, 0) for x, g in zip(xs, out[0])]


def _split3(x):
    x1 = x.astype(BF16)
    r1 = x - x1.astype(F32)
    x2 = r1.astype(BF16)
    x3 = (r1 - x2.astype(F32)).astype(BF16)
    return x1, x2, x3


def _exact_dot(x, m01, cb=0):
    acc = None
    for piece in _split3(x):
        t = lax.dot_general(piece, m01, (((1,), (cb,)), ((), ())), preferred_element_type=F32)
        acc = t if acc is None else acc + t
    return acc


def _head_matrices():
    e = np.zeros((D_MODEL, N_HEADS), np.float32)
    e[np.arange(D_MODEL), np.arange(D_MODEL) // HEAD_DIM] = 1.0
    return jnp.asarray(e, BF16), jnp.asarray(e.T, BF16)


@jax.custom_vjp
def hsum(x, e, et):
    return _exact_dot(x, e)


@jax.custom_vjp
def hbc(s, e, et):
    return _exact_dot(s, et)


hsum.defvjp(lambda x, e, et: (_exact_dot(x, e), (e, et)),
            lambda res, g: (hbc(g, *res), jnp.zeros_like(res[0]), jnp.zeros_like(res[1])))
hbc.defvjp(lambda s, e, et: (_exact_dot(s, et), (e, et)),
           lambda res, g: (hsum(g, *res), jnp.zeros_like(res[0]), jnp.zeros_like(res[1])))


def _sigmoid(u):
    return 0.5 * (jnp.tanh(0.5 * u) + 1.0)


def _softplus(u):
    return jnp.maximum(u, 0.0) + jnp.log(1.0 + jnp.exp(-jnp.abs(u)))


def _layer_norm(z, g, b):
    mu = jnp.mean(z, axis=-1, keepdims=True)
    zc = z - mu
    var = jnp.mean(zc * zc, axis=-1, keepdims=True)
    return zc * lax.rsqrt(var + LN_EPS) * g + b


def _zero_map(nd):
    return lambda c, i: (0,) * nd


def _params():
    return pltpu.CompilerParams(dimension_semantics=("arbitrary", "arbitrary"), vmem_limit_bytes=VMEM_LIMIT)


def rowwise(name, fn, rows, consts, out_rows, out_accs, tm, nc=1):
    lp = rows[0].shape[-2]
    nt = lp // tm
    assert nt * tm == lp, (name, lp, tm)
    in_specs, args = [], []
    for a in rows:
        if a.ndim == 2:
            in_specs.append(pl.BlockSpec((tm, a.shape[1]), lambda c, i: (i, 0)))
        else:
            in_specs.append(pl.BlockSpec((a.shape[0], tm, a.shape[2]), lambda c, i: (0, i, 0)))
        args.append(a)
    for cst in consts:
        if isinstance(cst, tuple):
            arr, bs, im = cst
            in_specs.append(pl.BlockSpec(bs, im))
        else:
            arr = cst
            in_specs.append(pl.BlockSpec(arr.shape, _zero_map(arr.ndim), pipeline_mode=pl.Buffered(1)))
        args.append(arr)
    out_shape, out_specs, acc_per_chunk = [], [], []
    for spec in out_rows:
        if len(spec) == 3 and spec[2]:
            out_shape.append(jax.ShapeDtypeStruct((nc, lp, spec[0]), spec[1]))
            out_specs.append(pl.BlockSpec((None, tm, spec[0]), lambda c, i: (c, i, 0)))
        else:
            out_shape.append(jax.ShapeDtypeStruct((lp, spec[0]), spec[1]))
            out_specs.append(pl.BlockSpec((tm, spec[0]), lambda c, i: (i, 0)))
    for spec in out_accs:
        out_shape.append(jax.ShapeDtypeStruct(spec[0], spec[1]))
        if len(spec) == 4:
            out_specs.append(pl.BlockSpec(spec[2], spec[3]))
            acc_per_chunk.append(True)
        else:
            out_specs.append(pl.BlockSpec(spec[0], _zero_map(len(spec[0])), pipeline_mode=pl.Buffered(1)))
            acc_per_chunk.append(False)
    n_in, n_or = len(args), len(out_rows)

    def body(*refs):
        c = pl.program_id(0)
        i = pl.program_id(1)
        vals = [r[...] for r in refs[:n_in]]
        outs_r, outs_a = fn(c, i, *vals)
        for ref, val in zip(refs[n_in:n_in + n_or], outs_r):
            ref[...] = val.astype(ref.dtype)
        for ref, val, per_chunk in zip(refs[n_in + n_or:], outs_a, acc_per_chunk):
            first = (i == 0) if per_chunk else jnp.logical_and(i == 0, c == 0)

            @pl.when(first)
            def _():
                ref[...] = val.astype(ref.dtype)

            @pl.when(jnp.logical_not(first))
            def _():
                ref[...] += val.astype(ref.dtype)

    outs = pl.pallas_call(body, name=name, grid=(nc, nt), in_specs=in_specs, out_specs=out_specs,
                          out_shape=out_shape, compiler_params=_params())(*args)
    return outs[:n_or], outs[n_or:]


def _row_ids(i, tm):
    return i * tm + lax.broadcasted_iota(jnp.int32, (tm, 1), 0)


PRE_TAPS = (D_MODEL, D_MODEL, D_MODEL, 64, D_MODEL, 64, D_MODEL, 128, D_MODEL)


def rwkv_pre(e, et, ws, taps, h, hp, mu_r, mu_w, mu_k, mu_v, mu_a, mu_g, w0, a0, k_k, k_a):
    w_r, w_k, w_v, w1, w2, a1, a2, g1, g2 = ws
    xs = []
    xx = hp - h
    r = tmm(h + xx * mu_r, w_r, taps, xs)
    k = tmm(h + xx * mu_k, w_k, taps, xs)
    v = tmm(h + xx * mu_v, w_v, taps, xs)
    wraw = -_softplus(-(w0 + tmm(jnp.tanh(tmm(h + xx * mu_w, w1, taps, xs)), w2, taps, xs))) - 0.5
    lw = -jnp.exp(wraw)
    a = _sigmoid(a0 + tmm(tmm(h + xx * mu_a, a1, taps, xs), a2, taps, xs))
    g = tmm(_sigmoid(tmm(h + xx * mu_g, g1, taps, xs)), g2, taps, xs)
    kk = k * k_k
    ss = hsum(kk * kk, e, et)
    pos = ss > 0.0
    nrm = jnp.where(pos, jnp.sqrt(jnp.where(pos, ss, 1.0)), 0.0)
    kk = kk * hbc(1.0 / jnp.maximum(nrm, 1e-12), e, et)
    k2 = k * (1.0 + (a - 1.0) * k_a)
    return (r, lw, k2, v, -kk, kk * a, g), xs


def rwkv_post(e, et, w_o, taps, y, r, k2, v, g, h0, gn_w, gn_b, rk, lg, lb):
    xs = []
    inv_n = 1.0 / HEAD_DIM
    yc = y - hbc(hsum(y, e, et) * inv_n, e, et)
    yv = hsum(yc * yc, e, et) * inv_n
    yn = yc * hbc(lax.rsqrt(yv + GN_EPS), e, et) * gn_w + gn_b
    bonus = hbc(hsum(r * k2 * rk, e, et), e, et) * v
    mix = tmm((yn + bonus) * g, w_o, taps, xs)
    return _layer_norm(ALPHA * h0 + mix, lg, lb), xs


def mlp_chunk(wup, wdown, taps, h):
    xs = []
    u = jnp.maximum(tmm(h, wup, taps, xs), 0.0)
    return tmm(u * u, wdown, taps, xs), xs


def _rot_half(t):
    n = t.shape[-1]
    lane = lax.broadcasted_iota(jnp.int32, t.shape, t.ndim - 1)
    lo = (lane % HEAD_DIM) < (HEAD_DIM // 2)
    return jnp.where(lo, -pltpu.roll(t, n - HEAD_DIM // 2, t.ndim - 1), pltpu.roll(t, HEAD_DIM // 2, t.ndim - 1))


@jax.custom_vjp
def rot_half(t):
    return _rot_half(t)


rot_half.defvjp(lambda t: (_rot_half(t), None), lambda _, g: (-_rot_half(g),))


def _tile_lanes(t, width):
    return jnp.concatenate([t] * (width // t.shape[-1]), axis=-1)


def qkv_proj(cos, sin, wq, wk, wv, taps, h):
    xs = []
    q = tmm(h, wq, taps, xs)
    k = tmm(h, wk, taps, xs)
    v = tmm(h, wv, taps, xs)
    cq, sq = _tile_lanes(cos, D_MODEL), _tile_lanes(sin, D_MODEL)
    ck, sk = _tile_lanes(cos, KV_DIM), _tile_lanes(sin, KV_DIM)
    return (q * cq + rot_half(q) * sq, k * ck + rot_half(k) * sk, v), xs


def attn_out(w_o, taps, o, h, lg, lb):
    xs = []
    return _layer_norm(ALPHA * h + tmm(o, w_o, taps, xs), lg, lb), xs


def _scan_consts():
    t = SCAN_T
    tri = np.tril(np.ones((t, t), np.float32))
    rows = np.arange(2 * t)
    same = (rows[:, None] // t) == (rows[None, :] // t)
    strict = same & ((rows[None, :] % t) < (rows[:, None] % t))
    incl = same & ((rows[None, :] % t) <= (rows[:, None] % t))
    lane = np.arange(PAIR)
    masks = np.zeros((8, PAIR), np.float32)
    masks[0] = (lane // HEAD_DIM) == 0
    masks[1] = (lane // HEAD_DIM) == 1
    return (jnp.asarray(tri, BF16), jnp.asarray(strict.astype(np.float32)), jnp.asarray(incl.astype(np.float32)),
            jnp.asarray(masks), jnp.asarray(np.eye(2 * t, dtype=np.float32)))


def _scan_dot(a, b, ca, cb):
    return _dot(a, b, ca, cb)


@functools.partial(jax.custom_vjp, nondiff_argnums=(2, 3))
def _dotf(a, b, ca, cb):
    return _scan_dot(a, b, ca, cb)


def _dotf_bwd(ca, cb, res, g):
    a, b = res
    if ca == 1:
        da = _scan_dot(g, b, 1, 1 - cb)
    else:
        da = _scan_dot(b, g, 1 - cb, 1)
    if cb == 0:
        db = _scan_dot(a, g, 1 - ca, 0)
    else:
        db = _scan_dot(g, a, 0, 1 - ca)
    return da, db


_dotf.defvjp(lambda a, b, ca, cb: (_scan_dot(a, b, ca, cb), (a, b)), _dotf_bwd)


def _tri_dot(tri, x, ct):
    acc = None
    for piece in _split3(x):
        t = lax.dot_general(tri, piece, (((ct,), (0,)), ((), ())), preferred_element_type=F32)
        acc = t if acc is None else acc + t
    return acc


@jax.custom_vjp
def _cumsum_rows(tri, x):
    return _tri_dot(tri, x, 1)


_cumsum_rows.defvjp(lambda tri, x: (_tri_dot(tri, x, 1), tri),
                    lambda tri, g: (jnp.zeros_like(tri), _tri_dot(tri, g, 0)))


@jax.custom_vjp
def _unstack2(x):
    t = x.shape[0] // 2
    return x[:t] + x[t:]


_unstack2.defvjp(lambda x: (_unstack2(x), None), lambda _, g: (jnp.concatenate([g, g], axis=0),))


@jax.custom_vjp
def _last_row(x):
    return x[x.shape[0] - 1:, :]


def _last_row_bwd(_, g):
    rows = lax.broadcasted_iota(jnp.int32, (SCAN_T, g.shape[1]), 0)
    return (jnp.where(rows == SCAN_T - 1, jnp.broadcast_to(g, (SCAN_T, g.shape[1])), 0.0),)


_last_row.defvjp(lambda x: (_last_row(x), None), _last_row_bwd)


@jax.custom_vjp
def _solve_saved(n, rhs, minv, u):
    return u


def _solve_saved_bwd(res, du):
    minv, u = res
    drhs = _dotf(minv, du, 0, 0)
    return _dotf(drhs, u, 1, 1), drhs, jnp.zeros_like(minv), jnp.zeros_like(u)


_solve_saved.defvjp(lambda n, rhs, minv, u: (u, (minv, u)), _solve_saved_bwd)


def scan_chunk(tri, strict, incl, m0, m1, eye, r, lw, k, v, a, b, s0, saved=None):
    lower = strict > 0
    lower_incl = incl > 0

    def stack(x):
        return jnp.concatenate([x * m0, x * m1], axis=0)

    def dots(xs, ys, ca, cb, mask=None):
        out = [_dotf(x, y, ca, cb) for x, y in zip(xs, ys)]
        return out if mask is None else [jnp.where(mask, o, 0.0) for o in out]

    cl = [_cumsum_rows(tri, x) for x in lw]
    gam = [jnp.exp(c) for c in cl]
    ginv = [jnp.exp(-c) for c in cl]
    a_s = [stack(x * jnp.exp(c - w)) for x, c, w in zip(a, cl, lw)]
    r_s = [stack(x * g) for x, g in zip(r, gam)]
    b_s = [stack(x * g) for x, g in zip(b, ginv)]
    k_s = [stack(x * g) for x, g in zip(k, ginv)]
    v_s = [stack(x) for x in v]
    n_ab = dots(a_s, b_s, 1, 1, lower)
    n_ak = dots(a_s, k_s, 1, 1, lower)
    r_ab = dots(r_s, b_s, 1, 1, lower_incl)
    r_ak = dots(r_s, k_s, 1, 1, lower_incl)
    rhs = [x + y for x, y in zip(dots(a_s, s0, 1, 1), dots(n_ak, v_s, 1, 0))]
    if saved is None:
        minv = [eye + n for n in n_ab]
        p = n_ab
        for _ in range(5):
            p = dots(p, p, 1, 0)
            minv = [m + mp for m, mp in zip(minv, dots(minv, p, 1, 0))]
        u_s = dots(minv, rhs, 1, 0)
    else:
        minv = saved[0]
        u_s = [_solve_saved(n, x, m, u) for n, x, m, u in zip(n_ab, rhs, *saved)]
    y = [_unstack2(x0 + x1 + x2)
         for x0, x1, x2 in zip(dots(r_s, s0, 1, 1), dots(r_ab, u_s, 1, 0), dots(r_ak, v_s, 1, 0))]
    g_end = [_last_row(g) for g in gam]
    s1 = [s * g + x + z for s, g, x, z in zip(s0, g_end, dots(u_s, [x * g for x, g in zip(b_s, g_end)], 0, 0),
                                              dots(v_s, [x * g for x, g in zip(k_s, g_end)], 0, 0))]
    return y, s1, (minv, u_s)


SCAN_PAIRS = 8


def _scan_specs(consts, order):
    row = pl.BlockSpec((SCAN_T, PAIR * SCAN_PAIRS), lambda p, c: (order(c), p))
    state = pl.BlockSpec((None, SCAN_PAIRS, PAIR, PAIR), lambda p, c: (order(c), p, 0, 0))
    return row, state, [pl.BlockSpec(x.shape, _zero_map(x.ndim)) for x in consts]


def _pair_lanes(q):
    return slice(q * PAIR, (q + 1) * PAIR)


def scan_fwd(r, lw, k, v, a, b, shards=()):
    lp = r.shape[0]
    nch = lp // SCAN_T
    npair = D_MODEL // PAIR
    ng = len(shards)
    consts = _scan_consts()
    row, state, cspecs = _scan_specs(consts, lambda c: c)

    def body(tri, strict, incl, masks, eye, r_ref, lw_ref, k_ref, v_ref, a_ref, b_ref, *rest):
        src, (y_ref, s_ref, minv_ref, u_ref), dst = rest[:ng], rest[ng:ng + 4], rest[ng + 4:2 * ng + 4]
        carry = rest[2 * ng + 4]
        first = jnp.logical_and(pl.program_id(0) == 0, pl.program_id(1) == 0)
        last = jnp.logical_and(pl.program_id(0) == npair // SCAN_PAIRS - 1, pl.program_id(1) == nch - 1)
        if ng:
            sends, arrivals, forwards, forwarded = gather_copies(src, dst, *rest[2 * ng + 5:])

            @pl.when(first)
            def _():
                for cp in sends:
                    cp.start()

            @pl.when(jnp.logical_and(pl.program_id(0) == npair // SCAN_PAIRS - 1, pl.program_id(1) == nch * 3 // 4))
            def _():
                for landed, onward in zip(arrivals, forwards):
                    landed.wait_recv()
                    onward.start()

        @pl.when(pl.program_id(1) == 0)
        def _():
            carry[...] = jnp.zeros_like(carry)

        pairs = range(SCAN_PAIRS)
        s0 = [carry[q] for q in pairs]
        rows = [[ref[:, _pair_lanes(q)] for q in pairs] for ref in (r_ref, lw_ref, k_ref, v_ref, a_ref, b_ref)]
        y, s1, (minv, u) = scan_chunk(tri[...], strict[...], incl[...], masks[0:1, :], masks[1:2, :], eye[...],
                                      *rows, s0)
        for q in pairs:
            s_ref[q] = s0[q]
            minv_ref[q] = minv[q]
            u_ref[q] = u[q]
            y_ref[:, _pair_lanes(q)] = y[q]
            carry[q] = s1[q]

        if ng:
            @pl.when(last)
            def _():
                for cp in forwarded:
                    cp.wait_recv()
                for cp in sends + forwards:
                    cp.wait_send()

    mats = jax.ShapeDtypeStruct((nch, npair, PAIR, PAIR), F32)
    out = pl.pallas_call(
        body, name="rwkv_scan_fwd", grid=(npair // SCAN_PAIRS, nch), in_specs=cspecs + [row] * 6 + [ANY] * ng,
        out_specs=[row, state, state, state] + [ANY] * ng,
        out_shape=[jax.ShapeDtypeStruct((lp, D_MODEL), F32), mats, mats, mats] + gathered_shapes(shards),
        scratch_shapes=[pltpu.VMEM((SCAN_PAIRS, PAIR, PAIR), F32)] + (gather_scratch(ng) if ng else []),
        compiler_params=_params(),
    )(*consts, r, lw, k, v, a, b, *shards)
    return out[:4], fill_own(out[4:], shards)


def scan_bwd(r, lw, k, v, a, b, saved, dy, parts=()):
    lp = r.shape[0]
    nch = lp // SCAN_T
    npair = D_MODEL // PAIR
    consts = _scan_consts()
    row, state, cspecs = _scan_specs(consts, lambda c: nch - 1 - c)

    ng = len(parts)

    def body(tri, strict, incl, masks, eye, r_ref, lw_ref, k_ref, v_ref, a_ref, b_ref, s_ref, minv_ref, u_ref,
             dy_ref, *rest):
        src, (dr_ref, dlw_ref, dk_ref, dv_ref, da_ref, db_ref), dst = rest[:ng], rest[ng:ng + 6], rest[ng + 6:2 * ng + 6]
        carry = rest[2 * ng + 6]
        first = jnp.logical_and(pl.program_id(0) == 0, pl.program_id(1) == 0)
        last = jnp.logical_and(pl.program_id(0) == npair // SCAN_PAIRS - 1, pl.program_id(1) == nch - 1)
        if ng:
            sends, arrivals = chip_exchange_copies(src, dst, *rest[2 * ng + 7:])

            @pl.when(first)
            def _():
                for cp in sends:
                    cp.start()

        @pl.when(pl.program_id(1) == 0)
        def _():
            carry[...] = jnp.zeros_like(carry)

        pairs = range(SCAN_PAIRS)
        kept = ([minv_ref[q] for q in pairs], [u_ref[q] for q in pairs])

        def fn(*args):
            y, s1, _ = scan_chunk(tri[...], strict[...], incl[...], masks[0:1, :], masks[1:2, :], eye[...], *args,
                                  saved=kept)
            return y, s1

        rows = [[ref[:, _pair_lanes(q)] for q in pairs] for ref in (r_ref, lw_ref, k_ref, v_ref, a_ref, b_ref)]
        _, vjp = jax.vjp(fn, *rows, [s_ref[q] for q in pairs])
        grads = vjp(([dy_ref[:, _pair_lanes(q)] for q in pairs], [carry[q] for q in pairs]))
        for q in pairs:
            for ref, g in zip((dr_ref, dlw_ref, dk_ref, dv_ref, da_ref, db_ref), grads[:6]):
                ref[:, _pair_lanes(q)] = g[q]
            carry[q] = grads[6][q]

        if ng:
            @pl.when(last)
            def _():
                for cp in arrivals:
                    cp.wait_recv()
                for cp in sends:
                    cp.wait_send()

    out = pl.pallas_call(
        body, name="rwkv_scan_bwd", grid=(npair // SCAN_PAIRS, nch),
        in_specs=cspecs + [row] * 6 + [state] * 3 + [row] + [ANY] * ng, out_specs=[row] * 6 + [ANY] * ng,
        out_shape=[jax.ShapeDtypeStruct((lp, D_MODEL), F32)] * 6 + [jax.ShapeDtypeStruct(p.shape, p.dtype) for p in parts],
        scratch_shapes=[pltpu.VMEM((SCAN_PAIRS, PAIR, PAIR), F32)] + (_sem_scratch(ng * len(XY_FLIPS)) if ng else []),
        compiler_params=_params(),
    )(*consts, r, lw, k, v, a, b, *saved, dy, *parts)
    return out[:6], out[6:]


def _spread_matrices():
    rep = np.zeros((N_HEADS_KV, KV_DIM, KVW), np.float32)
    for h in range(N_HEADS_KV):
        for g in range(GROUP):
            rep[h, h * HEAD_DIM + np.arange(HEAD_DIM), g * HEAD_DIM + np.arange(HEAD_DIM)] = 1.0
    return jnp.asarray(rep, BF16)


KV_HEADS = range(N_HEADS_KV)


def _attn_common(n, q_ref, kp, kc, vp, vc, rep_ref, sink_ref):
    lane = lax.broadcasted_iota(jnp.int32, (1, KVW), 1)
    gmask = [(lane // HEAD_DIM == g).astype(F32) for g in range(GROUP)]
    kk = jnp.concatenate([kp, kc], axis=0)
    vv = jnp.concatenate([vp, vc], axis=0)
    qs = [q_ref[:, h * KVW:(h + 1) * KVW] for h in KV_HEADS]
    q_s = [jnp.concatenate([q * gmask[g] for g in range(GROUP)], axis=0) for q in qs]
    keys = [_dot(kk, rep_ref[h], 1, 0) for h in KV_HEADS]
    vals = [_dot(vv, rep_ref[h], 1, 0) for h in KV_HEADS]
    qi = lax.broadcasted_iota(jnp.int32, (GROUP * BLOCK, 2 * BLOCK), 0) % BLOCK
    kj = lax.broadcasted_iota(jnp.int32, (GROUP * BLOCK, 2 * BLOCK), 1)
    rel = BLOCK + qi - kj
    valid = (rel >= 0) & (rel < BLOCK) & ((n - 1) * BLOCK + kj >= PAD_FRONT)
    s = [jnp.where(valid, _dot(x, y, 1, 1) * (HEAD_DIM ** -0.5), -1e30) for x, y in zip(q_s, keys)]
    sink_col = [jnp.concatenate([jnp.broadcast_to(sink_ref[h, g:g + 1, 0:1], (BLOCK, 1)) for g in range(GROUP)],
                                axis=0) for h in KV_HEADS]
    m = [jnp.maximum(jnp.max(x, axis=-1, keepdims=True), c) for x, c in zip(s, sink_col)]
    ex = [jnp.exp(x - y) for x, y in zip(s, m)]
    ex_sink = [jnp.exp(c - y) for c, y in zip(sink_col, m)]
    inv = [1.0 / (jnp.sum(x, axis=-1, keepdims=True) + c) for x, c in zip(ex, ex_sink)]
    return (gmask, q_s, keys, vals, [x * y for x, y in zip(ex, inv)], [x * y for x, y in zip(ex_sink, inv)])


def _unstack_groups(x_s, gmask):
    out = None
    for g in range(GROUP):
        t = x_s[g * BLOCK:(g + 1) * BLOCK] * gmask[g]
        out = t if out is None else out + t
    return out


def _attn_specs():
    qspec = pl.BlockSpec((BLOCK, D_MODEL), lambda n: (n, 0))
    cur = pl.BlockSpec((BLOCK, KV_DIM), lambda n: (n, 0))
    prev = pl.BlockSpec((BLOCK, KV_DIM), lambda n: (jnp.maximum(n - 1, 0), 0))
    rep = pl.BlockSpec((N_HEADS_KV, KV_DIM, KVW), lambda n: (0, 0, 0))
    sink = pl.BlockSpec((N_HEADS_KV, 8, PAIR), lambda n: (0, 0, 0))
    return qspec, cur, prev, rep, sink


def _attn_params():
    return pltpu.CompilerParams(dimension_semantics=("arbitrary",), vmem_limit_bytes=VMEM_LIMIT)


def attn_fwd(q, k, v, sinks_b):
    lp = q.shape[0]
    qspec, cur, prev, rep, sink = _attn_specs()

    def body(q_ref, kp_ref, kc_ref, vp_ref, vc_ref, rep_ref, sink_ref, o_ref):
        gmask, _, _, vals, p, _ = _attn_common(pl.program_id(0), q_ref, kp_ref[...], kc_ref[...], vp_ref[...],
                                               vc_ref[...], rep_ref, sink_ref)
        o = [_dot(x, y, 1, 0) for x, y in zip(p, vals)]
        for h in KV_HEADS:
            o_ref[:, h * KVW:(h + 1) * KVW] = _unstack_groups(o[h], gmask)

    return pl.pallas_call(
        body, name="swa_fwd", grid=(lp // BLOCK,), in_specs=[qspec, prev, cur, prev, cur, rep, sink],
        out_specs=qspec, out_shape=jax.ShapeDtypeStruct((lp, D_MODEL), F32), compiler_params=_attn_params(),
    )(q, k, k, v, v, _spread_matrices(), sinks_b)


def attn_bwd(q, k, v, sinks_b, do):
    lp = q.shape[0]
    qspec, cur, prev, rep, sink = _attn_specs()

    def body(q_ref, kp_ref, kc_ref, vp_ref, vc_ref, rep_ref, sink_ref, do_ref, dq_ref, dkc_ref, dkp_ref, dvc_ref,
             dvp_ref, dsink_ref):
        n = pl.program_id(0)
        gmask, q_s, keys, vals, p, p_sink = _attn_common(n, q_ref, kp_ref[...], kc_ref[...], vp_ref[...], vc_ref[...],
                                                         rep_ref, sink_ref)
        do_s = [jnp.concatenate([do_ref[:, h * KVW:(h + 1) * KVW] * gmask[g] for g in range(GROUP)], axis=0)
                for h in KV_HEADS]
        dp = [_dot(x, y, 1, 1) for x, y in zip(do_s, vals)]
        delta = [jnp.sum(x * y, axis=-1, keepdims=True) for x, y in zip(p, dp)]
        ds = [x * (y - z) * (HEAD_DIM ** -0.5) for x, y, z in zip(p, dp, delta)]
        dq = [_dot(x, y, 1, 0) for x, y in zip(ds, keys)]
        dkeys_s = [_dot(x, y, 0, 0) for x, y in zip(ds, q_s)]
        dvals_s = [_dot(x, y, 0, 0) for x, y in zip(p, do_s)]
        dkeys = [_exact_dot(x, rep_ref[h], cb=1) for h, x in enumerate(dkeys_s)]
        dvals = [_exact_dot(x, rep_ref[h], cb=1) for h, x in enumerate(dvals_s)]
        dk_all = (dkeys[0] + dkeys[1]) + (dkeys[2] + dkeys[3])
        dv_all = (dvals[0] + dvals[1]) + (dvals[2] + dvals[3])
        dkp_ref[...] = dk_all[:BLOCK]
        dkc_ref[...] = dk_all[BLOCK:]
        dvp_ref[...] = dv_all[:BLOCK]
        dvc_ref[...] = dv_all[BLOCK:]
        dsinks = []
        for h in KV_HEADS:
            dq_ref[:, h * KVW:(h + 1) * KVW] = _unstack_groups(dq[h], gmask)
            dsk = -(p_sink[h] * delta[h])
            rows = [jnp.broadcast_to(jnp.sum(dsk[g * BLOCK:(g + 1) * BLOCK], axis=0, keepdims=True), (1, PAIR))
                    for g in range(GROUP)]
            dsinks.append(jnp.concatenate(rows + [jnp.zeros((8 - GROUP, PAIR), F32)], axis=0))

        @pl.when(n == 0)
        def _():
            for h in KV_HEADS:
                dsink_ref[h] = dsinks[h]

        @pl.when(n > 0)
        def _():
            for h in KV_HEADS:
                dsink_ref[h] += dsinks[h]

    kv = jax.ShapeDtypeStruct((lp, KV_DIM), F32)
    return pl.pallas_call(
        body, name="swa_bwd", grid=(lp // BLOCK,), in_specs=[qspec, prev, cur, prev, cur, rep, sink, qspec],
        out_specs=[qspec, cur, cur, cur, cur, sink],
        out_shape=[jax.ShapeDtypeStruct((lp, D_MODEL), F32), kv, kv, kv, kv,
                   jax.ShapeDtypeStruct((N_HEADS_KV, 8, PAIR), F32)],
        compiler_params=_attn_params(),
    )(q, k, k, v, v, _spread_matrices(), sinks_b, do)


def _pick_tm(lp, want):
    for tm in (384, 192, 128, 64):
        if tm <= want and lp % tm == 0:
            return tm
    raise ValueError(lp)


def _acc(shape):
    return (tuple(shape), F32)


def _ff_all(w, layer):
    return (w, (N_FF_CHUNK, None, D_MODEL, D_MODEL), lambda c, i: (0, layer, 0, 0))


def _ff_one(w, layer):
    return (w, (None, None, D_MODEL, D_MODEL), lambda c, i: (c, layer, 0, 0))


def _mlp_layer_fwd(name, h, wup, wdown, layer, lg, lb, tm):
    def fn(c, i, h, wup, wdown, lg, lb):
        out = None
        for s in range(N_FF_CHUNK):
            t = mlp_chunk(wup[s], wdown[s], None, h)[0]
            out = t if out is None else out + t
        z = ALPHA * h + out
        return (_layer_norm(z, lg, lb), z), ()

    (h_out, z), _ = rowwise(name, fn, [h], [_ff_all(wup, layer), _ff_all(wdown, layer), lg, lb],
                            [(D_MODEL, F32), (D_MODEL, F32)], [], tm)
    return h_out, z


def _mlp_layer_bwd(name, h_in, z, dh_parts, wup, wdown, layer, lg, lb, tm):
    n_parts = len(dh_parts)

    def fn_ln(c, i, z, *rest):
        dh = rest[0]
        for extra in rest[1:n_parts]:
            dh = dh + extra
        _, vjp = jax.vjp(_layer_norm, z, rest[n_parts], rest[n_parts + 1])
        dz, dlg, dlb = vjp(dh)
        return (dz,), (dlg, dlb)

    (dz,), (dlg, dlb) = rowwise(name + "_ln", fn_ln, [z] + list(dh_parts), [lg, lb], [(D_MODEL, F32)],
                                [_acc((1, D_MODEL)), _acc((1, D_MODEL))], tm)

    def fn_mlp(c, i, h, dz, wup, wdown):
        tile = h.shape[0]
        (dx,), dws = vjp_taps(functools.partial(mlp_chunk, wup, wdown), [(tile, D_MODEL)] * 2, [h], dz)
        return (dx,), dws

    aspec = ((N_FF_CHUNK, D_MODEL, D_MODEL), F32, (None, D_MODEL, D_MODEL), lambda c, i: (c, 0, 0))
    (dx,), (dwup, dwdown) = rowwise(name + "_mm", fn_mlp, [h_in, dz], [_ff_one(wup, layer), _ff_one(wdown, layer)],
                                    [(D_MODEL, F32, True)], [aspec, aspec], tm, nc=N_FF_CHUNK)
    return dz, dx, dwup, dwdown, dlg, dlb


def _sum_parts(dz, dx):
    out = ALPHA * dz
    for s in range(N_FF_CHUNK):
        out = out + dx[s]
    return out


def local_step(x, loss_target, p, late=None, early_hook=None):
    seq = x.shape[0]
    lp = TOK0 + seq
    tm = _pick_tm(lp, 384)
    tms = _pick_tm(lp, 128)
    e, et = _head_matrices()
    h0 = jnp.concatenate([jnp.zeros((PAD_FRONT, D_MODEL), F32), p["meta_tokens"], x], axis=0)
    hp = jnp.concatenate([jnp.zeros((1, D_MODEL), F32), h0[:-1]], axis=0)
    tgt = jnp.concatenate([jnp.zeros((TOK0, D_MODEL), F32), loss_target], axis=0)
    pos = jnp.maximum(jnp.arange(lp, dtype=F32) - PAD_FRONT, 0.0)
    inv_freq = 1.0 / (ROPE_THETA ** (jnp.arange(0, HEAD_DIM, 2, dtype=F32) / HEAD_DIM))
    ang = pos[:, None] * inv_freq[None, :]
    cos = jnp.tile(jnp.cos(ang), (1, PAIR // (HEAD_DIM // 2)))
    sin = jnp.tile(jnp.sin(ang), (1, PAIR // (HEAD_DIM // 2)))

    pre_vec = [p["a_mu"][j:j + 1] for j in range(6)] + [p["a_w0"], p["a_a0"], p["a_k_k"], p["a_k_a"]]
    pre_w = [p["a_w_r"], p["a_w_k"], p["a_w_v"], p["a_w1"], p["a_w2"], p["a_a1"], p["a_a2"], p["a_g1"], p["a_g2"]]
    n_vec = len(pre_vec)

    def fn_pre(c, i, h, hp, e, et, *ws):
        return rwkv_pre(e, et, ws[n_vec:], None, h, hp, *ws[:n_vec])[0], ()

    (r, lw, k2, v, an, bn, g), _ = rowwise("rwkv_pre", fn_pre, [h0, hp], [e, et] + pre_vec + pre_w,
                                           [(D_MODEL, F32)] * 7, [], tms)
    (y, *scan_saved), late_gathered = scan_fwd(r, lw, k2, v, an, bn, late[0] if late else ())
    if late:
        p = {**p, **late[1](late_gathered)}

    post_c =[p["a_w_o"], p["a_gn_w"], p["a_gn_b"], p["a_r_k"], p["ln_g00"], p["ln_b00"]]

    def fn_post(c, i, y, r, k2, v, g, h0, e, et, w_o, *vecs):
        return (rwkv_post(e, et, w_o, None, y, r, k2, v, g, h0, *vecs)[0],), ()

    (h1,), _ = rowwise("rwkv_post", fn_post, [y, r, k2, v, g, h0], [e, et] + post_c, [(D_MODEL, F32)], [], tm)
    h2, z2 = _mlp_layer_fwd("mlp0_fwd", h1, p["mlp_up"], p["mlp_down"], 0, p["ln_g01"], p["ln_b01"], tm)

    qkv_w = [p["b_w_q"], p["kv_w_k"], p["kv_w_v"]]

    def fn_qkv(c, i, h, cos, sin, wq, wk, wv):
        return qkv_proj(cos, sin, wq, wk, wv, None, h)[0], ()

    (q, k, vv), _ = rowwise("qkv_proj", fn_qkv, [h2, cos, sin], qkv_w,
                            [(D_MODEL, F32), (KV_DIM, F32), (KV_DIM, F32)], [], tm)
    sinks_b = jnp.broadcast_to(p["b_sinks"].reshape(N_HEADS_KV, GROUP, 1), (N_HEADS_KV, GROUP, PAIR))
    sinks_b = jnp.concatenate([sinks_b, jnp.zeros((N_HEADS_KV, 8 - GROUP, PAIR), F32)], axis=1)
    o = attn_fwd(q, k, vv, sinks_b)

    ao_c = [p["b_w_o"], p["ln_g10"], p["ln_b10"]]

    def fn_ao(c, i, o, h, w_o, lg, lb):
        return (attn_out(w_o, None, o, h, lg, lb)[0],), ()

    (h3,), _ = rowwise("attn_out", fn_ao, [o, h2], ao_c, [(D_MODEL, F32)], [], tm)
    h4, z4 = _mlp_layer_fwd("mlp1_fwd", h3, p["mlp_up"], p["mlp_down"], 1, p["ln_g11"], p["ln_b11"], tm)

    def fn_loss(c, i, h4, tgt):
        real = (_row_ids(i, tm) >= TOK0).astype(F32)
        err = (h4 - tgt) * real
        part = 0.5 * jnp.sum(jnp.sum(err * err, axis=-1, keepdims=True), axis=0, keepdims=True) / D_MODEL
        return (err * (1.0 / D_MODEL),), (jnp.broadcast_to(part, (8, PAIR)),)

    (dh4,), (loss_acc,) = rowwise("loss", fn_loss, [h4, tgt], [], [(D_MODEL, F32)], [_acc((8, PAIR))], tm)
    loss = loss_acc[0, 0]

    grads = {}
    dz4, dx4, grads["mlp_up1"], grads["mlp_down1"], grads["ln_g11"], grads["ln_b11"] = _mlp_layer_bwd(
        "mlp1_bwd", h3, z4, [dh4], p["mlp_up"], p["mlp_down"], 1, p["ln_g11"], p["ln_b11"], tm)

    def fn_ao_b(c, i, dz, dx, o, h, w_o, lg, lb):
        (do, dh, dlg, dlb), (dw_o,) = vjp_taps(functools.partial(attn_out, w_o), [(tms, D_MODEL)], [o, h, lg, lb],
                                               _sum_parts(dz, dx))
        return (do, dh), (dw_o, dlg, dlb)

    (do, dh2_a), (grads["b_w_o"], grads["ln_g10"], grads["ln_b10"]) = rowwise(
        "attn_out_bwd", fn_ao_b, [dz4, dx4, o, h2], ao_c, [(D_MODEL, F32)] * 2,
        [_acc((D_MODEL, D_MODEL)), _acc((1, D_MODEL)), _acc((1, D_MODEL))], tms)

    dq, dkc, dkp, dvc, dvp, dsinks = attn_bwd(q, k, vv, sinks_b, do)
    grads["b_sinks"] = dsinks[:, :GROUP, 0].reshape(1, N_HEADS)
    zblk = jnp.zeros((BLOCK, KV_DIM), F32)
    dkp_s = jnp.concatenate([dkp[BLOCK:], zblk], axis=0)
    dvp_s = jnp.concatenate([dvp[BLOCK:], zblk], axis=0)

    def fn_qkv_b(c, i, h, cos, sin, dq, dkc, dkp, dvc, dvp, wq, wk, wv):
        return vjp_taps(functools.partial(qkv_proj, cos, sin, wq, wk, wv),
                        [(tms, D_MODEL), (tms, KV_DIM), (tms, KV_DIM)], [h], (dq, dkc + dkp, dvc + dvp))

    (dh2_q,), (grads["b_w_q"], grads["kv_w_k"], grads["kv_w_v"]) = rowwise(
        "qkv_proj_bwd", fn_qkv_b, [h2, cos, sin, dq, dkc, dkp_s, dvc, dvp_s], qkv_w, [(D_MODEL, F32)],
        [_acc((D_MODEL, D_MODEL)), _acc((D_MODEL, KV_DIM)), _acc((D_MODEL, KV_DIM))], tms)

    dz2, dx2, grads["mlp_up0"], grads["mlp_down0"], grads["ln_g01"], grads["ln_b01"] = _mlp_layer_bwd(
        "mlp0_bwd", h1, z2, [dh2_a, dh2_q], p["mlp_up"], p["mlp_down"], 0, p["ln_g01"], p["ln_b01"], tm)

    def fn_post_b(c, i, dz, dx, y, r, k2, v, g, h0, e, et, w_o, *vecs):
        out, dws = vjp_taps(functools.partial(rwkv_post, e, et, w_o), [(tms, D_MODEL)],
                            [y, r, k2, v, g, h0] + list(vecs), _sum_parts(dz, dx))
        return out[:6], tuple(dws) + tuple(out[6:])

    (dy, dr_c, dk_c, dv_c, dg, dh0_c), post_g = rowwise(
        "rwkv_post_bwd", fn_post_b, [dz2, dx2, y, r, k2, v, g, h0], [e, et] + post_c, [(D_MODEL, F32)] * 6,
        [_acc((D_MODEL, D_MODEL))] + [_acc((1, D_MODEL))] * 5, tms)
    for name, val in zip(["a_w_o", "a_gn_w", "a_gn_b", "a_r_k", "ln_g00", "ln_b00"], post_g):
        grads[name] = val

    (dr_s, dlw, dk_s, dv_s, dan, dbn), early_from_chips = scan_bwd(r, lw, k2, v, an, bn, scan_saved, dy,
                                                                   early_hook(grads) if early_hook else ())

    def fn_pre_b(c, i, h, hp, dr_c, dr_s, dlw, dk_c, dk_s, dv_c, dv_s, dan, dbn, dg, e, et, *ws):
        real = (_row_ids(i, tms) >= PAD_FRONT).astype(F32)
        cot = tuple(t * real for t in (dr_c + dr_s, dlw, dk_c + dk_s, dv_c + dv_s, dan, dbn, dg))
        out, dws = vjp_taps(functools.partial(rwkv_pre, e, et, ws[n_vec:]), [(tms, n) for n in PRE_TAPS],
                            [h, hp] + list(ws[:n_vec]), cot)
        return out[:2], tuple(out[2:]) + tuple(dws)

    (dh0_p, dhp), pre_g = rowwise(
        "rwkv_pre_bwd", fn_pre_b, [h0, hp, dr_c, dr_s, dlw, dk_c, dk_s, dv_c, dv_s, dan, dbn, dg],
        [e, et] + pre_vec + pre_w, [(D_MODEL, F32)] * 2,
        [_acc((1, D_MODEL))] * n_vec + [_acc(w.shape) for w in pre_w], tms)
    grads["a_mu"] = jnp.concatenate(pre_g[:6], axis=0)
    for name, val in zip(["a_w0", "a_a0", "a_k_k", "a_k_a", "a_w_r", "a_w_k", "a_w_v", "a_w1", "a_w2", "a_a1",
                          "a_a2", "a_g1", "a_g2"], pre_g[6:]):
        grads[name] = val

    dhp_s = jnp.concatenate([dhp[1:], jnp.zeros((1, D_MODEL), F32)], axis=0)

    def fn_add(c, i, a, b, d):
        return (a + b + d,), ()

    (dh0,), _ = rowwise("grad_h0", fn_add, [dh0_c, dh0_p, dhp_s], [], [(D_MODEL, F32)], [], tm)
    grads["meta_tokens"] = dh0[PAD_FRONT:TOK0]
    return loss, dh0[TOK0:], grads, early_from_chips


ANY = pl.BlockSpec(memory_space=pl.ANY)
XY_FLIPS = ((0, 1), (1, 0), (1, 1))
ALL_FLIPS = tuple((e >> 2 & 1, e >> 1 & 1, e & 1) for e in range(1, N_DEV))


def _flip(v, bit):
    return 1 - v if bit else v


def _sem_scratch(n):
    return [pltpu.SemaphoreType.DMA((n,)), pltpu.SemaphoreType.DMA((n,))]


def gather_copies(src, dst, ici_send, ici_recv, d2d_send, d2d_recv):
    npeer = len(XY_FLIPS)
    x, y, c = lax.axis_index("x"), lax.axis_index("y"), lax.axis_index("c")

    def half(ref, k, which):
        h = src[k].shape[0] // 2
        start = which * h
        return ref.at[pl.ds(pl.multiple_of(start, 8) if h % 8 == 0 else start, h)]

    def ici(k, j, slot):
        fx, fy = XY_FLIPS[j]
        return pltpu.make_async_remote_copy(
            src_ref=half(src[k], k, c), dst_ref=half(dst[k].at[slot], k, c), send_sem=ici_send.at[k * npeer + j],
            recv_sem=ici_recv.at[k * npeer + j], device_id=(_flip(x, fx), _flip(y, fy), c), device_id_type=MESH)

    def d2d(k, j, which):
        fx, fy = XY_FLIPS[j]
        landed = half(dst[k].at[2 * _flip(x, fx) + _flip(y, fy)], k, which)
        return pltpu.make_async_remote_copy(
            src_ref=landed, dst_ref=landed, send_sem=d2d_send.at[k * npeer + j], recv_sem=d2d_recv.at[k * npeer + j],
            device_id=(x, y, 1 - c), device_id_type=MESH)

    pairs = [(k, j) for k in range(len(src)) for j in range(npeer)]
    return ([ici(k, j, 2 * x + y) for k, j in pairs],
            [ici(k, j, 2 * _flip(x, XY_FLIPS[j][0]) + _flip(y, XY_FLIPS[j][1])) for k, j in pairs],
            [d2d(k, j, c) for k, j in pairs], [d2d(k, j, 1 - c) for k, j in pairs])


def gather_scratch(n):
    return _sem_scratch(n * len(XY_FLIPS)) * 2


def gathered_shapes(shards):
    return [jax.ShapeDtypeStruct((N_SHARD,) + s.shape, s.dtype) for s in shards]


def fill_own(gathered, shards):
    if not shards:
        return []
    slot = 2 * lax.axis_index("x") + lax.axis_index("y")
    return [lax.dynamic_update_index_in_dim(g, s, slot, 0) for g, s in zip(gathered, shards)]


def all_gather_shards(shards):
    n = len(shards)

    def body(*refs):
        sends, arrivals, forwards, forwarded = gather_copies(refs[:n], refs[n:2 * n], *refs[2 * n:])
        for cp in sends:
            cp.start()
        for landed, onward in zip(arrivals, forwards):
            landed.wait_recv()
            onward.start()
        for cp in forwarded:
            cp.wait_recv()
        for cp in sends + forwards:
            cp.wait_send()

    out = pl.pallas_call(body, name="gather_weights", in_specs=[ANY] * n, out_specs=[ANY] * n,
                         out_shape=gathered_shapes(shards), scratch_shapes=gather_scratch(n))(*shards)
    return fill_own(out, shards)


def placement():
    x, y, c = lax.axis_index("x"), lax.axis_index("y"), lax.axis_index("c")
    me = 2 * x + y
    others = [j + (j >= me).astype(jnp.int32) for j in range(N_SHARD - 1)]
    return jnp.stack([c, me] + others).astype(jnp.int32)


def pair_exchange(name, sources):
    n = len(sources)

    def body(*refs):
        src, got = refs[:n], refs[n:2 * n]
        send_sems, recv_sems = refs[2 * n:]
        x, y, c = lax.axis_index("x"), lax.axis_index("y"), lax.axis_index("c")

        def copy(k):
            half = sources[k].shape[1] // 2
            theirs = src[k].at[:, pl.ds(pl.multiple_of((1 - c) * half, 8), half), :]
            return pltpu.make_async_remote_copy(
                src_ref=theirs, dst_ref=got[k], send_sem=send_sems.at[k], recv_sem=recv_sems.at[k],
                device_id=(x, y, 1 - c), device_id_type=MESH)

        sends = [copy(k) for k in range(n)]
        for cp in sends:
            cp.start()
        for cp in sends:
            cp.wait_recv()
        for cp in sends:
            cp.wait_send()

    halves = [jax.ShapeDtypeStruct((s.shape[0], s.shape[1] // 2, s.shape[2]), s.dtype) for s in sources]
    return pl.pallas_call(body, name=name, in_specs=[ANY] * n, out_specs=[ANY] * n,
                          out_shape=halves, scratch_shapes=_sem_scratch(n))(*sources)


def chip_exchange(parts):
    n = len(parts)

    def body(*refs):
        sends, arrivals = chip_exchange_copies(refs[:n], refs[n:2 * n], *refs[2 * n:])
        for cp in sends:
            cp.start()
        for cp in arrivals:
            cp.wait_recv()
        for cp in sends:
            cp.wait_send()

    return pl.pallas_call(
        body, name="grads_chip_exchange", in_specs=[ANY] * n, out_specs=[ANY] * n,
        out_shape=[jax.ShapeDtypeStruct(p.shape, p.dtype) for p in parts],
        scratch_shapes=_sem_scratch(n * len(XY_FLIPS)),
    )(*parts)


def chip_exchange_copies(src, dst, send_sems, recv_sems):
    npeer = len(XY_FLIPS)
    x, y, c = lax.axis_index("x"), lax.axis_index("y"), lax.axis_index("c")
    me = 2 * x + y

    def copy(k, j, sending):
        fx, fy = XY_FLIPS[j]
        px, py = _flip(x, fx), _flip(y, fy)
        peer = 2 * px + py
        return pltpu.make_async_remote_copy(
            src_ref=src[k].at[peer], dst_ref=dst[k].at[me if sending else peer],
            send_sem=send_sems.at[k * npeer + j], recv_sem=recv_sems.at[k * npeer + j],
            device_id=(px, py, c), device_id_type=MESH)

    pairs = [(k, j) for k in range(len(src)) for j in range(npeer)]
    return [copy(k, j, True) for k, j in pairs], [copy(k, j, False) for k, j in pairs]


def sibling_share(halves):
    n = len(halves)

    def body(*refs):
        src, got = refs[:n], refs[n:2 * n]
        send_sems, recv_sems = refs[2 * n:]
        x, y, c = lax.axis_index("x"), lax.axis_index("y"), lax.axis_index("c")
        sends = [pltpu.make_async_remote_copy(
            src_ref=src[k], dst_ref=got[k], send_sem=send_sems.at[k], recv_sem=recv_sems.at[k],
            device_id=(x, y, 1 - c), device_id_type=MESH) for k in range(n)]
        for cp in sends:
            cp.start()
        for cp in sends:
            cp.wait_recv()
        for cp in sends:
            cp.wait_send()

    return pl.pallas_call(
        body, name="grads_sibling_share", in_specs=[ANY] * n, out_specs=[ANY] * n,
        out_shape=[jax.ShapeDtypeStruct(h.shape, h.dtype) for h in halves], scratch_shapes=_sem_scratch(n),
    )(*halves)


ADD_TILE_ELEMS = 512 * 1024


def _row_tile(rows, cols):
    return max(t for t in range(8, rows + 1, 8) if rows % t == 0 and t * cols <= ADD_TILE_ELEMS)


def _prefetch_call(body, name, place, grid, in_specs, out_specs, out_shape, args):
    return pl.pallas_call(
        body, name=name, out_shape=out_shape,
        grid_spec=pltpu.PrefetchScalarGridSpec(num_scalar_prefetch=1, grid=grid, in_specs=in_specs,
                                               out_specs=out_specs),
        compiler_params=pltpu.CompilerParams(dimension_semantics=("arbitrary",) * len(grid),
                                             vmem_limit_bytes=VMEM_LIMIT),
    )(place, *args)


def pair_add(name, place, src, got, dtype):
    n4, half, cols = got.shape
    tile = _row_tile(half, cols)
    nt = half // tile

    def body(pr, a_ref, b_ref, o_ref):
        o_ref[...] = (a_ref[...] + b_ref[...]).astype(o_ref.dtype)

    mine = pl.BlockSpec((None, tile, cols), lambda s, i, pr: (s, pr[0] * nt + i, 0))
    blk = pl.BlockSpec((None, tile, cols), lambda s, i, pr: (s, i, 0))
    return _prefetch_call(body, name, place, (n4, nt), [mine, blk], blk,
                          jax.ShapeDtypeStruct(got.shape, dtype), (src, got))


def chip_add(name, place, part, from_chips):
    _, half, cols = part.shape
    tile = _row_tile(half, cols)

    def body(pr, own_ref, r0_ref, r1_ref, r2_ref, o_ref):
        me = pr[1]
        own, r0, r1, r2 = (r[...].astype(F32) for r in (own_ref, r0_ref, r1_ref, r2_ref))
        t0 = jnp.where(me == 0, own, r0)
        t1 = jnp.where(me == 0, r0, jnp.where(me == 1, own, r1))
        t2 = jnp.where(me <= 1, r1, jnp.where(me == 2, own, r2))
        t3 = jnp.where(me == 3, own, r2)
        o_ref[...] = ((t0 + t1) + t2) + t3

    def slab(j):
        return pl.BlockSpec((None, tile, cols), lambda i, pr: (pr[j], i, 0))

    return _prefetch_call(body, name, place, (half // tile,), [slab(1), slab(2), slab(3), slab(4)],
                          pl.BlockSpec((tile, cols), lambda i, pr: (i, 0)),
                          jax.ShapeDtypeStruct((half, cols), F32), (part, from_chips, from_chips, from_chips))


def pair_sums(tag, place, sources, narrow):
    got = pair_exchange("grads_pair_exchange_" + tag, sources)
    return [pair_add(f"grads_pair_add_{tag}{k}", place, s, g, BF16 if nar else F32)
            for k, (s, g, nar) in enumerate(zip(sources, got, narrow))]


def finish_sums(place, parts, from_chips):
    halves = [chip_add(f"grads_chip_add{k}", place, p, f) for k, (p, f) in enumerate(zip(parts, from_chips))]
    return list(zip(halves, sibling_share(halves)))


ADAM_ROWS = 256


def adamw_update(name, place, halves, w, m, v):
    nsub, rows, cols = w.shape
    half = rows // 2
    tr = ADAM_ROWS if half % ADAM_ROWS == 0 else half
    nth = half // tr

    def body(pr, *refs):
        g_refs, (w_ref, m_ref, v_ref, g_ref, d_ref, nm_ref, nv_ref) = refs[:2 * nsub], refs[2 * nsub:]
        l = pl.program_id(0)
        mine = (pl.program_id(1) // nth) == pr[0]
        g = None
        for s in range(nsub):
            gs = jnp.where(mine, g_refs[2 * s][...], g_refs[2 * s + 1][...])
            g = gs if g is None else jnp.where(l == s, gs, g)
        m2 = ADAM_B1 * m_ref[...] + (1.0 - ADAM_B1) * g
        v2 = ADAM_B2 * v_ref[...] + (1.0 - ADAM_B2) * (g * g)
        m_hat = m2 / (1.0 - ADAM_B1 ** ADAM_STEP)
        v_hat = v2 / (1.0 - ADAM_B2 ** ADAM_STEP)
        g_ref[...] = g
        d_ref[...] = -ADAM_LR * (m_hat / (jnp.sqrt(v_hat) + ADAM_EPS) + ADAM_WD * w_ref[...])
        nm_ref[...] = m2
        nv_ref[...] = v2

    gblk = pl.BlockSpec((tr, cols), lambda l, i, pr: (i % nth, 0))
    blk = pl.BlockSpec((None, tr, cols), lambda l, i, pr: (l, i, 0))
    out = jax.ShapeDtypeStruct((nsub, rows, cols), F32)
    return _prefetch_call(body, name, place, (nsub, rows // tr), [gblk] * (2 * nsub) + [blk] * 3, [blk] * 4,
                          [out] * 4, [h for pair in halves for h in pair] + [w, m, v])


WEIGHT_NAMES = ("meta_tokens", "a_mu", "a_w_r", "a_w_k", "a_w_v", "a_w_o", "a_w0", "a_w1", "a_w2", "a_a0", "a_a1",
                "a_a2", "a_g1", "a_g2", "a_k_k", "a_k_a", "a_r_k", "a_gn_w", "a_gn_b", "kv_w_k", "kv_w_v", "b_w_q",
                "b_sinks", "b_w_o", "mlp_w_up", "mlp_w_down", "ln_g", "ln_b")
BIG_NAMES = ("a_w_r", "a_w_k", "a_w_v", "a_w_o", "b_w_q", "b_w_o")
EARLY_NAMES, LATE_NAMES = BIG_NAMES[:3], BIG_NAMES[3:]
PACK_MATS = (("kv_w_k", 256), ("kv_w_v", 256), ("a_w1", 64), ("a_a1", 64), ("a_g1", 128), ("a_w2", 64),
             ("a_a2", 64), ("a_g2", 128))
COLUMN_CUT = ("a_w2", "a_a2", "a_g2")
PACK_VECS = (("a_mu", 6), ("a_w0", 1), ("a_a0", 1), ("a_k_k", 1), ("a_k_a", 1), ("a_gn_w", 1), ("a_gn_b", 1),
             ("ln_g", 4), ("ln_b", 4), ("meta_tokens", 16))
PACK_REPL = (("a_r_k", 4), ("b_sinks", 1))
SHARD_W = D_MODEL // N_SHARD
N_MAT_ROWS = sum(r for _, r in PACK_MATS)
N_VEC_ROWS = sum(r for _, r in PACK_VECS)
N_PACK_ROWS = -(-(N_MAT_ROWS + N_VEC_ROWS + sum(r for _, r in PACK_REPL)) // 8) * 8
N_GATHER_VEC_ROWS = -(-N_VEC_ROWS // 16) * 16


def _pack_rows(arr):
    if arr.size == N_HEADS:
        return jnp.pad(arr.reshape(1, N_HEADS), ((0, 0), (0, SHARD_W - N_HEADS)))
    return arr.reshape(-1, SHARD_W)


def pack_small(get):
    parts = [_pack_rows(get(name)) for name, _ in PACK_MATS + PACK_VECS + PACK_REPL]
    used = sum(p.shape[0] for p in parts)
    return jnp.concatenate(parts + [jnp.zeros((N_PACK_ROWS - used, SHARD_W), F32)], axis=0)


def unpack_small(pack, shapes):
    out, off = {}, 0
    for name, rows in PACK_MATS + PACK_VECS + PACK_REPL:
        piece = pack[off:off + rows]
        off += rows
        out[name] = piece[:, :N_HEADS].reshape(shapes[name]) if name == "b_sinks" else piece.reshape(shapes[name])
    return out


def whole_weights(big_names, gathered_big, mats, vecs, a_r_k, b_sinks):
    p = {name: g.reshape(D_MODEL, D_MODEL) for name, g in zip(big_names, gathered_big)}
    off = 0
    for name, rows in PACK_MATS:
        piece = mats[:, off:off + rows]
        off += rows
        if name in COLUMN_CUT:
            p[name] = piece.transpose(1, 0, 2).reshape(rows, D_MODEL)
        else:
            p[name] = piece.reshape(D_MODEL, rows)
    v = vecs.transpose(1, 0, 2).reshape(-1, D_MODEL)
    off = 0
    for name, rows in PACK_VECS:
        p[name] = v[off:off + rows]
        off += rows
    for i in range(2):
        for j in range(2):
            p[f"ln_g{i}{j}"] = p["ln_g"][2 * i + j:2 * i + j + 1]
            p[f"ln_b{i}{j}"] = p["ln_b"][2 * i + j:2 * i + j + 1]
    p["a_r_k"] = a_r_k.reshape(1, D_MODEL)
    p["b_sinks"] = b_sinks
    return p


def small_grad_pack(g):
    parts = []
    for name, rows in PACK_MATS:
        if name in COLUMN_CUT:
            parts.append(g[name].reshape(rows, N_SHARD, SHARD_W).transpose(1, 0, 2))
        else:
            parts.append(g[name].reshape(N_SHARD, rows, SHARD_W))
    vec_rows = [g["a_mu"]] + [g[n] for n in ("a_w0", "a_a0", "a_k_k", "a_k_a", "a_gn_w", "a_gn_b")]
    vec_rows += [g[f"ln_g{i}{j}"] for i in range(2) for j in range(2)]
    vec_rows += [g[f"ln_b{i}{j}"] for i in range(2) for j in range(2)] + [g["meta_tokens"]]
    parts.append(jnp.concatenate(vec_rows, axis=0).reshape(N_VEC_ROWS, N_SHARD, SHARD_W).transpose(1, 0, 2))
    parts.append(jnp.broadcast_to(g["a_r_k"].reshape(1, -1, SHARD_W), (N_SHARD, D_MODEL // SHARD_W, SHARD_W)))
    sinks = jnp.pad(g["b_sinks"].reshape(1, 1, N_HEADS), ((0, 0), (0, 0), (0, SHARD_W - N_HEADS)))
    parts.append(jnp.broadcast_to(sinks, (N_SHARD, 1, SHARD_W)))
    used = sum(p.shape[1] for p in parts)
    parts.append(jnp.zeros((N_SHARD, N_PACK_ROWS - used, SHARD_W), F32))
    return jnp.concatenate(parts, axis=1)


def train_step(vals):
    w = {n: vals[n] for n in WEIGHT_NAMES}
    w_pack = pack_small(lambda n: w[n])
    early = [w[n][0].astype(BF16) for n in EARLY_NAMES]
    early += [w_pack[:N_MAT_ROWS].astype(BF16), w_pack[N_MAT_ROWS:N_MAT_ROWS + N_GATHER_VEC_ROWS]]
    gathered = all_gather_shards(early)
    ne = len(EARLY_NAMES)
    p = whole_weights(EARLY_NAMES, gathered[:ne], gathered[ne], gathered[ne + 1][:, :N_VEC_ROWS], w["a_r_k"],
                      w["b_sinks"])
    late = [w[n][0].astype(BF16) for n in LATE_NAMES] + [w["mlp_w_up"].astype(BF16), w["mlp_w_down"].astype(BF16)]
    nb = len(BIG_NAMES)

    def late_weights(got):
        out = {n: x.reshape(D_MODEL, D_MODEL) for n, x in zip(LATE_NAMES, got)}
        out["mlp_up"], out["mlp_down"] = got[len(LATE_NAMES):]
        return out

    place = placement()
    ready = {}

    def early_hook(g):
        srcs = [g[n].reshape(N_SHARD, SHARD_W, D_MODEL) for n in LATE_NAMES]
        srcs += [g["mlp_up0"], g["mlp_up1"], g["mlp_down0"], g["mlp_down1"]]
        ready["parts"] = pair_sums("early", place, srcs, [True] * len(srcs))
        return ready["parts"]

    loss, gx, g, early_from_chips = local_step(vals["x"][0], vals["loss_target"][0], p, (late, late_weights),
                                               early_hook)
    loss = lax.psum(loss, ("x", "y", "c"))
    srcs = [g[n].reshape(N_SHARD, SHARD_W, D_MODEL) for n in EARLY_NAMES] + [small_grad_pack(g)]
    rest = pair_sums("late", place, srcs, [True] * len(EARLY_NAMES) + [False])
    rest_from_chips = chip_exchange(rest)
    ne = len(EARLY_NAMES)
    halves = finish_sums(place, rest[:ne] + ready["parts"] + rest[ne:],
                         list(rest_from_chips[:ne]) + list(early_from_chips) + list(rest_from_chips[ne:]))

    res = {}
    for k, n in enumerate(BIG_NAMES):
        res[n] = adamw_update("adamw_" + n, place, halves[k:k + 1], w[n], vals["m_" + n], vals["v_" + n])
    for k, n in ((nb, "mlp_w_up"), (nb + 2, "mlp_w_down")):
        res[n] = adamw_update("adamw_" + n, place, halves[k:k + 2], w[n], vals["m_" + n], vals["v_" + n])
    packs = adamw_update("adamw_small", place, halves[-1:], w_pack[None], pack_small(lambda n: vals["m_" + n])[None],
                         pack_small(lambda n: vals["v_" + n])[None])
    shapes = {n: w[n].shape for n in WEIGHT_NAMES}
    small = [unpack_small(pk[0], shapes) for pk in packs]
    outs = [loss, gx[None]]
    for t in range(4):
        outs += [res[n][t] if n in res else small[t][n] for n in WEIGHT_NAMES]
    return tuple(outs)


def kernel(x, meta_tokens, a_mu, a_w_r, a_w_k, a_w_v, a_w_o, a_w0, a_w1, a_w2, a_a0, a_a1, a_a2, a_g1, a_g2, a_k_k,
           a_k_a, a_r_k, a_gn_w, a_gn_b, kv_w_k, kv_w_v, b_w_q, b_sinks, b_w_o, mlp_w_up, mlp_w_down, ln_g, ln_b,
           loss_target, m_meta_tokens, m_a_mu, m_a_w_r, m_a_w_k, m_a_w_v, m_a_w_o, m_a_w0, m_a_w1, m_a_w2, m_a_a0,
           m_a_a1, m_a_a2, m_a_g1, m_a_g2, m_a_k_k, m_a_k_a, m_a_r_k, m_a_gn_w, m_a_gn_b, m_kv_w_k, m_kv_w_v,
           m_b_w_q, m_b_sinks, m_b_w_o, m_mlp_w_up, m_mlp_w_down, m_ln_g, m_ln_b, v_meta_tokens, v_a_mu, v_a_w_r,
           v_a_w_k, v_a_w_v, v_a_w_o, v_a_w0, v_a_w1, v_a_w2, v_a_a0, v_a_a1, v_a_a2, v_a_g1, v_a_g2, v_a_k_k,
           v_a_k_a, v_a_r_k, v_a_gn_w, v_a_gn_b, v_kv_w_k, v_kv_w_v, v_b_w_q, v_b_sinks, v_b_w_o, v_mlp_w_up,
           v_mlp_w_down, v_ln_g, v_ln_b):
    return train_step(dict(locals()))
```

```python
import functools

import numpy as np
import jax
import jax.numpy as jnp
from jax import lax
from jax.experimental import pallas as pl
from jax.experimental.pallas import tpu as pltpu

F32 = jnp.float32
BF16 = jnp.bfloat16

D_MODEL = 1024
N_HEADS = 16
HEAD_DIM = 64
N_HEADS_KV = 4
GROUP = 4
KV_DIM = N_HEADS_KV * HEAD_DIM
N_META = 16
BLOCK = 128
PAD_FRONT = BLOCK - N_META
TOK0 = PAD_FRONT + N_META
N_FF_CHUNK = 4
N_SHARD = 4
N_DEV = 8
GN_EPS = 64e-5
LN_EPS = 1e-5
ROPE_THETA = 10000.0
ALPHA = 4.0 ** 0.25
ADAM_LR, ADAM_B1, ADAM_B2, ADAM_EPS, ADAM_WD, ADAM_STEP = 0.001, 0.9, 0.999, 1e-08, 0.01, 10
SCAN_T = 64
PAIR = 128
KVW = GROUP * HEAD_DIM
VMEM_LIMIT = 60 * 1024 * 1024
HI = lax.Precision.HIGHEST
MESH = pl.DeviceIdType.MESH


def _dot(a, b, ca, cb):
    return lax.dot_general(a.astype(BF16), b.astype(BF16), (((ca,), (cb,)), ((), ())),
                           preferred_element_type=F32)


@jax.custom_vjp
def mm(a, b):
    return _dot(a, b, 1, 0)


def _mm_fwd(a, b):
    return mm(a, b), b


def _mm_bwd(b, g):
    return _dot(g, b, 1, 1), jnp.zeros_like(b)


mm.defvjp(_mm_fwd, _mm_bwd)


def tmm(x, w, taps, xs):
    y = mm(x, w)
    if taps is not None:
        y = y + taps[len(xs)]
    xs.append(x)
    return y


def vjp_taps(core, tap_shapes, args, cot):
    taps = [jnp.zeros(s, F32) for s in tap_shapes]
    _, vjp, xs = jax.vjp(core, taps, *args, has_aux=True)
    out = vjp(cot)
    return out[1:], [_dot(x, g, 0, 0) for x, g in zip(xs, out[0])]


def _split3(x):
    x1 = x.astype(BF16)
    r1 = x - x1.astype(F32)
    x2 = r1.astype(BF16)
    x3 = (r1 - x2.astype(F32)).astype(BF16)
    return x1, x2, x3


def _exact_dot(x, m01, cb=0):
    acc = None
    for piece in _split3(x):
        t = lax.dot_general(piece, m01, (((1,), (cb,)), ((), ())), preferred_element_type=F32)
        acc = t if acc is None else acc + t
    return acc


def _head_matrices():
    e = np.zeros((D_MODEL, N_HEADS), np.float32)
    e[np.arange(D_MODEL), np.arange(D_MODEL) // HEAD_DIM] = 1.0
    return jnp.asarray(e, BF16), jnp.asarray(e.T, BF16)


@jax.custom_vjp
def hsum(x, e, et):
    return _exact_dot(x, e)


@jax.custom_vjp
def hbc(s, e, et):
    return _exact_dot(s, et)


hsum.defvjp(lambda x, e, et: (_exact_dot(x, e), (e, et)),
            lambda res, g: (hbc(g, *res), jnp.zeros_like(res[0]), jnp.zeros_like(res[1])))
hbc.defvjp(lambda s, e, et: (_exact_dot(s, et), (e, et)),
           lambda res, g: (hsum(g, *res), jnp.zeros_like(res[0]), jnp.zeros_like(res[1])))


def _sigmoid(u):
    return 0.5 * (jnp.tanh(0.5 * u) + 1.0)


def _softplus(u):
    return jnp.maximum(u, 0.0) + jnp.log(1.0 + jnp.exp(-jnp.abs(u)))


def _layer_norm(z, g, b):
    mu = jnp.mean(z, axis=-1, keepdims=True)
    zc = z - mu
    var = jnp.mean(zc * zc, axis=-1, keepdims=True)
    return zc * lax.rsqrt(var + LN_EPS) * g + b


def _zero_map(nd):
    return lambda c, i: (0,) * nd


def _params():
    return pltpu.CompilerParams(dimension_semantics=("arbitrary", "arbitrary"), vmem_limit_bytes=VMEM_LIMIT)


def rowwise(name, fn, rows, consts, out_rows, out_accs, tm, nc=1):
    lp = rows[0].shape[-2]
    nt = lp // tm
    assert nt * tm == lp, (name, lp, tm)
    in_specs, args = [], []
    for a in rows:
        if a.ndim == 2:
            in_specs.append(pl.BlockSpec((tm, a.shape[1]), lambda c, i: (i, 0)))
        else:
            in_specs.append(pl.BlockSpec((a.shape[0], tm, a.shape[2]), lambda c, i: (0, i, 0)))
        args.append(a)
    for cst in consts:
        if isinstance(cst, tuple):
            arr, bs, im = cst
            in_specs.append(pl.BlockSpec(bs, im))
        else:
            arr = cst
            in_specs.append(pl.BlockSpec(arr.shape, _zero_map(arr.ndim), pipeline_mode=pl.Buffered(1)))
        args.append(arr)
    out_shape, out_specs, acc_per_chunk = [], [], []
    for spec in out_rows:
        if len(spec) == 3 and spec[2]:
            out_shape.append(jax.ShapeDtypeStruct((nc, lp, spec[0]), spec[1]))
            out_specs.append(pl.BlockSpec((None, tm, spec[0]), lambda c, i: (c, i, 0)))
        else:
            out_shape.append(jax.ShapeDtypeStruct((lp, spec[0]), spec[1]))
            out_specs.append(pl.BlockSpec((tm, spec[0]), lambda c, i: (i, 0)))
    for spec in out_accs:
        out_shape.append(jax.ShapeDtypeStruct(spec[0], spec[1]))
        if len(spec) == 4:
            out_specs.append(pl.BlockSpec(spec[2], spec[3]))
            acc_per_chunk.append(True)
        else:
            out_specs.append(pl.BlockSpec(spec[0], _zero_map(len(spec[0])), pipeline_mode=pl.Buffered(1)))
            acc_per_chunk.append(False)
    n_in, n_or = len(args), len(out_rows)

    def body(*refs):
        c = pl.program_id(0)
        i = pl.program_id(1)
        vals = [r[...] for r in refs[:n_in]]
        outs_r, outs_a = fn(c, i, *vals)
        for ref, val in zip(refs[n_in:n_in + n_or], outs_r):
            ref[...] = val.astype(ref.dtype)
        for ref, val, per_chunk in zip(refs[n_in + n_or:], outs_a, acc_per_chunk):
            first = (i == 0) if per_chunk else jnp.logical_and(i == 0, c == 0)

            @pl.when(first)
            def _():
                ref[...] = val.astype(ref.dtype)

            @pl.when(jnp.logical_not(first))
            def _():
                ref[...] += val.astype(ref.dtype)

    outs = pl.pallas_call(body, name=name, grid=(nc, nt), in_specs=in_specs, out_specs=out_specs,
                          out_shape=out_shape, compiler_params=_params())(*args)
    return outs[:n_or], outs[n_or:]


def _row_ids(i, tm):
    return i * tm + lax.broadcasted_iota(jnp.int32, (tm, 1), 0)


PRE_TAPS = (D_MODEL, D_MODEL, D_MODEL, 64, D_MODEL, 64, D_MODEL, 128, D_MODEL)


def rwkv_pre(e, et, ws, taps, h, hp, mu_r, mu_w, mu_k, mu_v, mu_a, mu_g, w0, a0, k_k, k_a):
    w_r, w_k, w_v, w1, w2, a1, a2, g1, g2 = ws
    xs = []
    xx = hp - h
    r = tmm(h + xx * mu_r, w_r, taps, xs)
    k = tmm(h + xx * mu_k, w_k, taps, xs)
    v = tmm(h + xx * mu_v, w_v, taps, xs)
    wraw = -_softplus(-(w0 + tmm(jnp.tanh(tmm(h + xx * mu_w, w1, taps, xs)), w2, taps, xs))) - 0.5
    lw = -jnp.exp(wraw)
    a = _sigmoid(a0 + tmm(tmm(h + xx * mu_a, a1, taps, xs), a2, taps, xs))
    g = tmm(_sigmoid(tmm(h + xx * mu_g, g1, taps, xs)), g2, taps, xs)
    kk = k * k_k
    ss = hsum(kk * kk, e, et)
    pos = ss > 0.0
    nrm = jnp.where(pos, jnp.sqrt(jnp.where(pos, ss, 1.0)), 0.0)
    kk = kk * hbc(1.0 / jnp.maximum(nrm, 1e-12), e, et)
    k2 = k * (1.0 + (a - 1.0) * k_a)
    return (r, lw, k2, v, -kk, kk * a, g), xs


def rwkv_post(e, et, w_o, taps, y, r, k2, v, g, h0, gn_w, gn_b, rk, lg, lb):
    xs = []
    inv_n = 1.0 / HEAD_DIM
    yc = y - hbc(hsum(y, e, et) * inv_n, e, et)
    yv = hsum(yc * yc, e, et) * inv_n
    yn = yc * hbc(lax.rsqrt(yv + GN_EPS), e, et) * gn_w + gn_b
    bonus = hbc(hsum(r * k2 * rk, e, et), e, et) * v
    mix = tmm((yn + bonus) * g, w_o, taps, xs)
    return _layer_norm(ALPHA * h0 + mix, lg, lb), xs


def mlp_chunk(wup, wdown, taps, h):
    xs = []
    u = jnp.maximum(tmm(h, wup, taps, xs), 0.0)
    return tmm(u * u, wdown, taps, xs), xs


def _rot_half(t):
    n = t.shape[-1]
    lane = lax.broadcasted_iota(jnp.int32, t.shape, t.ndim - 1)
    lo = (lane % HEAD_DIM) < (HEAD_DIM // 2)
    return jnp.where(lo, -pltpu.roll(t, n - HEAD_DIM // 2, t.ndim - 1), pltpu.roll(t, HEAD_DIM // 2, t.ndim - 1))


@jax.custom_vjp
def rot_half(t):
    return _rot_half(t)


rot_half.defvjp(lambda t: (_rot_half(t), None), lambda _, g: (-_rot_half(g),))


def _tile_lanes(t, width):
    return jnp.concatenate([t] * (width // t.shape[-1]), axis=-1)


def qkv_proj(cos, sin, wq, wk, wv, taps, h):
    xs = []
    q = tmm(h, wq, taps, xs)
    k = tmm(h, wk, taps, xs)
    v = tmm(h, wv, taps, xs)
    cq, sq = _tile_lanes(cos, D_MODEL), _tile_lanes(sin, D_MODEL)
    ck, sk = _tile_lanes(cos, KV_DIM), _tile_lanes(sin, KV_DIM)
    return (q * cq + rot_half(q) * sq, k * ck + rot_half(k) * sk, v), xs


def attn_out(w_o, taps, o, h, lg, lb):
    xs = []
    return _layer_norm(ALPHA * h + tmm(o, w_o, taps, xs), lg, lb), xs


def _scan_consts():
    t = SCAN_T
    tri = np.tril(np.ones((t, t), np.float32))
    rows = np.arange(2 * t)
    same = (rows[:, None] // t) == (rows[None, :] // t)
    strict = same & ((rows[None, :] % t) < (rows[:, None] % t))
    incl = same & ((rows[None, :] % t) <= (rows[:, None] % t))
    lane = np.arange(PAIR)
    masks = np.zeros((8, PAIR), np.float32)
    masks[0] = (lane // HEAD_DIM) == 0
    masks[1] = (lane // HEAD_DIM) == 1
    return (jnp.asarray(tri, BF16), jnp.asarray(strict.astype(np.float32)), jnp.asarray(incl.astype(np.float32)),
            jnp.asarray(masks), jnp.asarray(np.eye(2 * t, dtype=np.float32)))


def _scan_dot(a, b, ca, cb):
    return _dot(a, b, ca, cb)


@functools.partial(jax.custom_vjp, nondiff_argnums=(2, 3))
def _dotf(a, b, ca, cb):
    return _scan_dot(a, b, ca, cb)


def _dotf_bwd(ca, cb, res, g):
    a, b = res
    if ca == 1:
        da = _scan_dot(g, b, 1, 1 - cb)
    else:
        da = _scan_dot(b, g, 1 - cb, 1)
    if cb == 0:
        db = _scan_dot(a, g, 1 - ca, 0)
    else:
        db = _scan_dot(g, a, 0, 1 - ca)
    return da, db


_dotf.defvjp(lambda a, b, ca, cb: (_scan_dot(a, b, ca, cb), (a, b)), _dotf_bwd)


def _tri_dot(tri, x, ct):
    acc = None
    for piece in _split3(x):
        t = lax.dot_general(tri, piece, (((ct,), (0,)), ((), ())), preferred_element_type=F32)
        acc = t if acc is None else acc + t
    return acc


@jax.custom_vjp
def _cumsum_rows(tri, x):
    return _tri_dot(tri, x, 1)


_cumsum_rows.defvjp(lambda tri, x: (_tri_dot(tri, x, 1), tri),
                    lambda tri, g: (jnp.zeros_like(tri), _tri_dot(tri, g, 0)))


@jax.custom_vjp
def _unstack2(x):
    t = x.shape[0] // 2
    return x[:t] + x[t:]


_unstack2.defvjp(lambda x: (_unstack2(x), None), lambda _, g: (jnp.concatenate([g, g], axis=0),))


@jax.custom_vjp
def _last_row(x):
    return x[x.shape[0] - 1:, :]


def _last_row_bwd(_, g):
    rows = lax.broadcasted_iota(jnp.int32, (SCAN_T, g.shape[1]), 0)
    return (jnp.where(rows == SCAN_T - 1, jnp.broadcast_to(g, (SCAN_T, g.shape[1])), 0.0),)


_last_row.defvjp(lambda x: (_last_row(x), None), _last_row_bwd)


@jax.custom_vjp
def _solve_saved(n, rhs, minv, u):
    return u


def _solve_saved_bwd(res, du):
    minv, u = res
    drhs = _dotf(minv, du, 0, 0)
    return _dotf(drhs, u, 1, 1), drhs, jnp.zeros_like(minv), jnp.zeros_like(u)


_solve_saved.defvjp(lambda n, rhs, minv, u: (u, (minv, u)), _solve_saved_bwd)


def scan_chunk(tri, strict, incl, m0, m1, eye, r, lw, k, v, a, b, s0, saved=None):
    lower = strict > 0
    lower_incl = incl > 0

    def stack(x):
        return jnp.concatenate([x * m0, x * m1], axis=0)

    def dots(xs, ys, ca, cb, mask=None):
        out = [_dotf(x, y, ca, cb) for x, y in zip(xs, ys)]
        return out if mask is None else [jnp.where(mask, o, 0.0) for o in out]

    cl = [_cumsum_rows(tri, x) for x in lw]
    gam = [jnp.exp(c) for c in cl]
    ginv = [jnp.exp(-c) for c in cl]
    a_s = [stack(x * jnp.exp(c - w)) for x, c, w in zip(a, cl, lw)]
    r_s = [stack(x * g) for x, g in zip(r, gam)]
    b_s = [stack(x * g) for x, g in zip(b, ginv)]
    k_s = [stack(x * g) for x, g in zip(k, ginv)]
    v_s = [stack(x) for x in v]
    n_ab = dots(a_s, b_s, 1, 1, lower)
    n_ak = dots(a_s, k_s, 1, 1, lower)
    r_ab = dots(r_s, b_s, 1, 1, lower_incl)
    r_ak = dots(r_s, k_s, 1, 1, lower_incl)
    rhs = [x + y for x, y in zip(dots(a_s, s0, 1, 1), dots(n_ak, v_s, 1, 0))]
    if saved is None:
        minv = [eye + n for n in n_ab]
        p = n_ab
        for _ in range(5):
            p = dots(p, p, 1, 0)
            minv = [m + mp for m, mp in zip(minv, dots(minv, p, 1, 0))]
        u_s = dots(minv, rhs, 1, 0)
    else:
        minv = saved[0]
        u_s = [_solve_saved(n, x, m, u) for n, x, m, u in zip(n_ab, rhs, *saved)]
    y = [_unstack2(x0 + x1 + x2)
         for x0, x1, x2 in zip(dots(r_s, s0, 1, 1), dots(r_ab, u_s, 1, 0), dots(r_ak, v_s, 1, 0))]
    g_end = [_last_row(g) for g in gam]
    s1 = [s * g + x + z for s, g, x, z in zip(s0, g_end, dots(u_s, [x * g for x, g in zip(b_s, g_end)], 0, 0),
                                              dots(v_s, [x * g for x, g in zip(k_s, g_end)], 0, 0))]
    return y, s1, (minv, u_s)


SCAN_PAIRS = 8


def _scan_specs(consts, order):
    row = pl.BlockSpec((SCAN_T, PAIR * SCAN_PAIRS), lambda p, c: (order(c), p))
    state = pl.BlockSpec((None, SCAN_PAIRS, PAIR, PAIR), lambda p, c: (order(c), p, 0, 0))
    return row, state, [pl.BlockSpec(x.shape, _zero_map(x.ndim)) for x in consts]


def _pair_lanes(q):
    return slice(q * PAIR, (q + 1) * PAIR)


def scan_fwd(r, lw, k, v, a, b, shards=()):
    lp = r.shape[0]
    nch = lp // SCAN_T
    npair = D_MODEL // PAIR
    ng = len(shards)
    consts = _scan_consts()
    row, state, cspecs = _scan_specs(consts, lambda c: c)

    def body(tri, strict, incl, masks, eye, r_ref, lw_ref, k_ref, v_ref, a_ref, b_ref, *rest):
        src, (y_ref, s_ref, minv_ref, u_ref), dst = rest[:ng], rest[ng:ng + 4], rest[ng + 4:2 * ng + 4]
        carry = rest[2 * ng + 4]
        first = jnp.logical_and(pl.program_id(0) == 0, pl.program_id(1) == 0)
        last = jnp.logical_and(pl.program_id(0) == npair // SCAN_PAIRS - 1, pl.program_id(1) == nch - 1)
        if ng:
            sends, arrivals, forwards, forwarded = gather_copies(src, dst, *rest[2 * ng + 5:])

            @pl.when(first)
            def _():
                for cp in sends:
                    cp.start()

            @pl.when(jnp.logical_and(pl.program_id(0) == npair // SCAN_PAIRS - 1, pl.program_id(1) == nch * 3 // 4))
            def _():
                for landed, onward in zip(arrivals, forwards):
                    landed.wait_recv()
                    onward.start()

        @pl.when(pl.program_id(1) == 0)
        def _():
            carry[...] = jnp.zeros_like(carry)

        pairs = range(SCAN_PAIRS)
        s0 = [carry[q] for q in pairs]
        rows = [[ref[:, _pair_lanes(q)] for q in pairs] for ref in (r_ref, lw_ref, k_ref, v_ref, a_ref, b_ref)]
        y, s1, (minv, u) = scan_chunk(tri[...], strict[...], incl[...], masks[0:1, :], masks[1:2, :], eye[...],
                                      *rows, s0)
        for q in pairs:
            s_ref[q] = s0[q]
            minv_ref[q] = minv[q]
            u_ref[q] = u[q]
            y_ref[:, _pair_lanes(q)] = y[q]
            carry[q] = s1[q]

        if ng:
            @pl.when(last)
            def _():
                for cp in forwarded:
                    cp.wait_recv()
                for cp in sends + forwards:
                    cp.wait_send()

    mats = jax.ShapeDtypeStruct((nch, npair, PAIR, PAIR), F32)
    out = pl.pallas_call(
        body, name="rwkv_scan_fwd", grid=(npair // SCAN_PAIRS, nch), in_specs=cspecs + [row] * 6 + [ANY] * ng,
        out_specs=[row, state, state, state] + [ANY] * ng,
        out_shape=[jax.ShapeDtypeStruct((lp, D_MODEL), F32), mats, mats, mats] + gathered_shapes(shards),
        scratch_shapes=[pltpu.VMEM((SCAN_PAIRS, PAIR, PAIR), F32)] + (gather_scratch(ng) if ng else []),
        compiler_params=_params(),
    )(*consts, r, lw, k, v, a, b, *shards)
    return out[:4], fill_own(out[4:], shards)


def scan_bwd(r, lw, k, v, a, b, saved, dy, direct_grads, parts=()):
    lp = r.shape[0]
    nch = lp // SCAN_T
    npair = D_MODEL // PAIR
    consts = _scan_consts()
    row, state, cspecs = _scan_specs(consts, lambda c: nch - 1 - c)

    ng = len(parts)

    def body(tri, strict, incl, masks, eye, r_ref, lw_ref, k_ref, v_ref, a_ref, b_ref, s_ref, minv_ref, u_ref,
             dy_ref, dr_in, dk_in, dv_in, *rest):
        src, (dr_ref, dlw_ref, dk_ref, dv_ref, da_ref, db_ref), dst = rest[:ng], rest[ng:ng + 6], rest[ng + 6:2 * ng + 6]
        carry = rest[2 * ng + 6]
        first = jnp.logical_and(pl.program_id(0) == 0, pl.program_id(1) == 0)
        last = jnp.logical_and(pl.program_id(0) == npair // SCAN_PAIRS - 1, pl.program_id(1) == nch - 1)
        if ng:
            sends, arrivals = chip_exchange_copies(src, dst, *rest[2 * ng + 7:])

            @pl.when(first)
            def _():
                for cp in sends:
                    cp.start()

        @pl.when(pl.program_id(1) == 0)
        def _():
            carry[...] = jnp.zeros_like(carry)

        pairs = range(SCAN_PAIRS)
        kept = ([minv_ref[q] for q in pairs], [u_ref[q] for q in pairs])

        def fn(*args):
            y, s1, _ = scan_chunk(tri[...], strict[...], incl[...], masks[0:1, :], masks[1:2, :], eye[...], *args,
                                  saved=kept)
            return y, s1

        rows = [[ref[:, _pair_lanes(q)] for q in pairs] for ref in (r_ref, lw_ref, k_ref, v_ref, a_ref, b_ref)]
        _, vjp = jax.vjp(fn, *rows, [s_ref[q] for q in pairs])
        grads = vjp(([dy_ref[:, _pair_lanes(q)] for q in pairs], [carry[q] for q in pairs]))
        direct = (dr_in, None, dk_in, dv_in, None, None)
        for q in pairs:
            ln = _pair_lanes(q)
            for ref, g, extra in zip((dr_ref, dlw_ref, dk_ref, dv_ref, da_ref, db_ref), grads[:6], direct):
                ref[:, ln] = g[q] if extra is None else g[q] + extra[:, ln]
            carry[q] = grads[6][q]

        if ng:
            @pl.when(last)
            def _():
                for cp in arrivals:
                    cp.wait_recv()
                for cp in sends:
                    cp.wait_send()

    out = pl.pallas_call(
        body, name="rwkv_scan_bwd", grid=(npair // SCAN_PAIRS, nch),
        in_specs=cspecs + [row] * 6 + [state] * 3 + [row] * 4 + [ANY] * ng, out_specs=[row] * 6 + [ANY] * ng,
        out_shape=[jax.ShapeDtypeStruct((lp, D_MODEL), F32)] * 6 + [jax.ShapeDtypeStruct(p.shape, p.dtype) for p in parts],
        scratch_shapes=[pltpu.VMEM((SCAN_PAIRS, PAIR, PAIR), F32)] + (_sem_scratch(ng * len(XY_FLIPS)) if ng else []),
        compiler_params=_params(),
    )(*consts, r, lw, k, v, a, b, *saved, dy, *direct_grads, *parts)
    return out[:6], out[6:]


def _spread_matrices():
    rep = np.zeros((N_HEADS_KV, KV_DIM, KVW), np.float32)
    for h in range(N_HEADS_KV):
        for g in range(GROUP):
            rep[h, h * HEAD_DIM + np.arange(HEAD_DIM), g * HEAD_DIM + np.arange(HEAD_DIM)] = 1.0
    return jnp.asarray(rep, BF16)


KV_HEADS = range(N_HEADS_KV)


def _attn_common(n, q_ref, kp, kc, vp, vc, rep_ref, sink_ref):
    lane = lax.broadcasted_iota(jnp.int32, (1, KVW), 1)
    gmask = [(lane // HEAD_DIM == g).astype(F32) for g in range(GROUP)]
    kk = jnp.concatenate([kp, kc], axis=0)
    vv = jnp.concatenate([vp, vc], axis=0)
    qs = [q_ref[:, h * KVW:(h + 1) * KVW] for h in KV_HEADS]
    q_s = [jnp.concatenate([q * gmask[g] for g in range(GROUP)], axis=0) for q in qs]
    keys = [_dot(kk, rep_ref[h], 1, 0) for h in KV_HEADS]
    vals = [_dot(vv, rep_ref[h], 1, 0) for h in KV_HEADS]
    qi = lax.broadcasted_iota(jnp.int32, (GROUP * BLOCK, 2 * BLOCK), 0) % BLOCK
    kj = lax.broadcasted_iota(jnp.int32, (GROUP * BLOCK, 2 * BLOCK), 1)
    rel = BLOCK + qi - kj
    valid = (rel >= 0) & (rel < BLOCK) & ((n - 1) * BLOCK + kj >= PAD_FRONT)
    s = [jnp.where(valid, _dot(x, y, 1, 1) * (HEAD_DIM ** -0.5), -1e30) for x, y in zip(q_s, keys)]
    sink_col = [jnp.concatenate([jnp.broadcast_to(sink_ref[h, g:g + 1, 0:1], (BLOCK, 1)) for g in range(GROUP)],
                                axis=0) for h in KV_HEADS]
    m = [jnp.maximum(jnp.max(x, axis=-1, keepdims=True), c) for x, c in zip(s, sink_col)]
    ex = [jnp.exp(x - y) for x, y in zip(s, m)]
    ex_sink = [jnp.exp(c - y) for c, y in zip(sink_col, m)]
    inv = [1.0 / (jnp.sum(x, axis=-1, keepdims=True) + c) for x, c in zip(ex, ex_sink)]
    return (gmask, q_s, keys, vals, [x * y for x, y in zip(ex, inv)], [x * y for x, y in zip(ex_sink, inv)])


def _unstack_groups(x_s, gmask):
    out = None
    for g in range(GROUP):
        t = x_s[g * BLOCK:(g + 1) * BLOCK] * gmask[g]
        out = t if out is None else out + t
    return out


def _attn_specs():
    qspec = pl.BlockSpec((BLOCK, D_MODEL), lambda n: (n, 0))
    cur = pl.BlockSpec((BLOCK, KV_DIM), lambda n: (n, 0))
    prev = pl.BlockSpec((BLOCK, KV_DIM), lambda n: (jnp.maximum(n - 1, 0), 0))
    rep = pl.BlockSpec((N_HEADS_KV, KV_DIM, KVW), lambda n: (0, 0, 0))
    sink = pl.BlockSpec((N_HEADS_KV, 8, PAIR), lambda n: (0, 0, 0))
    return qspec, cur, prev, rep, sink


def _attn_params():
    return pltpu.CompilerParams(dimension_semantics=("arbitrary",), vmem_limit_bytes=VMEM_LIMIT)


def attn_fwd(q, k, v, sinks_b):
    lp = q.shape[0]
    qspec, cur, prev, rep, sink = _attn_specs()

    def body(q_ref, kp_ref, kc_ref, vp_ref, vc_ref, rep_ref, sink_ref, o_ref):
        gmask, _, _, vals, p, _ = _attn_common(pl.program_id(0), q_ref, kp_ref[...], kc_ref[...], vp_ref[...],
                                               vc_ref[...], rep_ref, sink_ref)
        o = [_dot(x, y, 1, 0) for x, y in zip(p, vals)]
        for h in KV_HEADS:
            o_ref[:, h * KVW:(h + 1) * KVW] = _unstack_groups(o[h], gmask)

    return pl.pallas_call(
        body, name="swa_fwd", grid=(lp // BLOCK,), in_specs=[qspec, prev, cur, prev, cur, rep, sink],
        out_specs=qspec, out_shape=jax.ShapeDtypeStruct((lp, D_MODEL), F32), compiler_params=_attn_params(),
    )(q, k, k, v, v, _spread_matrices(), sinks_b)


def attn_bwd(q, k, v, sinks_b, do):
    lp = q.shape[0]
    qspec, cur, prev, rep, sink = _attn_specs()

    def body(q_ref, kp_ref, kc_ref, vp_ref, vc_ref, rep_ref, sink_ref, do_ref, dq_ref, dkc_ref, dkp_ref, dvc_ref,
             dvp_ref, dsink_ref):
        n = pl.program_id(0)
        gmask, q_s, keys, vals, p, p_sink = _attn_common(n, q_ref, kp_ref[...], kc_ref[...], vp_ref[...], vc_ref[...],
                                                         rep_ref, sink_ref)
        do_s = [jnp.concatenate([do_ref[:, h * KVW:(h + 1) * KVW] * gmask[g] for g in range(GROUP)], axis=0)
                for h in KV_HEADS]
        dp = [_dot(x, y, 1, 1) for x, y in zip(do_s, vals)]
        delta = [jnp.sum(x * y, axis=-1, keepdims=True) for x, y in zip(p, dp)]
        ds = [x * (y - z) * (HEAD_DIM ** -0.5) for x, y, z in zip(p, dp, delta)]
        dq = [_dot(x, y, 1, 0) for x, y in zip(ds, keys)]
        dkeys_s = [_dot(x, y, 0, 0) for x, y in zip(ds, q_s)]
        dvals_s = [_dot(x, y, 0, 0) for x, y in zip(p, do_s)]
        dkeys = [_exact_dot(x, rep_ref[h], cb=1) for h, x in enumerate(dkeys_s)]
        dvals = [_exact_dot(x, rep_ref[h], cb=1) for h, x in enumerate(dvals_s)]
        dk_all = (dkeys[0] + dkeys[1]) + (dkeys[2] + dkeys[3])
        dv_all = (dvals[0] + dvals[1]) + (dvals[2] + dvals[3])
        dkp_ref[...] = dk_all[:BLOCK]
        dkc_ref[...] = dk_all[BLOCK:]
        dvp_ref[...] = dv_all[:BLOCK]
        dvc_ref[...] = dv_all[BLOCK:]
        dsinks = []
        for h in KV_HEADS:
            dq_ref[:, h * KVW:(h + 1) * KVW] = _unstack_groups(dq[h], gmask)
            dsk = -(p_sink[h] * delta[h])
            rows = [jnp.broadcast_to(jnp.sum(dsk[g * BLOCK:(g + 1) * BLOCK], axis=0, keepdims=True), (1, PAIR))
                    for g in range(GROUP)]
            dsinks.append(jnp.concatenate(rows + [jnp.zeros((8 - GROUP, PAIR), F32)], axis=0))

        @pl.when(n == 0)
        def _():
            for h in KV_HEADS:
                dsink_ref[h] = dsinks[h]

        @pl.when(n > 0)
        def _():
            for h in KV_HEADS:
                dsink_ref[h] += dsinks[h]

    kv = jax.ShapeDtypeStruct((lp, KV_DIM), F32)
    return pl.pallas_call(
        body, name="swa_bwd", grid=(lp // BLOCK,), in_specs=[qspec, prev, cur, prev, cur, rep, sink, qspec],
        out_specs=[qspec, cur, cur, cur, cur, sink],
        out_shape=[jax.ShapeDtypeStruct((lp, D_MODEL), F32), kv, kv, kv, kv,
                   jax.ShapeDtypeStruct((N_HEADS_KV, 8, PAIR), F32)],
        compiler_params=_attn_params(),
    )(q, k, k, v, v, _spread_matrices(), sinks_b, do)


def _pick_tm(lp, want):
    for tm in (384, 192, 128, 64):
        if tm <= want and lp % tm == 0:
            return tm
    raise ValueError(lp)


def _acc(shape):
    return (tuple(shape), F32)


def _ff_all(w, layer):
    return (w, (N_FF_CHUNK, None, D_MODEL, D_MODEL), lambda c, i: (0, layer, 0, 0))


def _ff_one(w, layer):
    return (w, (None, None, D_MODEL, D_MODEL), lambda c, i: (c, layer, 0, 0))


def _mlp_layer_fwd(name, h, wup, wdown, layer, lg, lb, tm):
    def fn(c, i, h, wup, wdown, lg, lb):
        out = None
        for s in range(N_FF_CHUNK):
            t = mlp_chunk(wup[s], wdown[s], None, h)[0]
            out = t if out is None else out + t
        z = ALPHA * h + out
        return (_layer_norm(z, lg, lb), z), ()

    (h_out, z), _ = rowwise(name, fn, [h], [_ff_all(wup, layer), _ff_all(wdown, layer), lg, lb],
                            [(D_MODEL, F32), (D_MODEL, F32)], [], tm)
    return h_out, z


def _mlp_layer_bwd(name, h_in, z, dh_parts, wup, wdown, layer, lg, lb, tm):
    n_parts = len(dh_parts)

    def fn_ln(c, i, z, *rest):
        dh = rest[0]
        for extra in rest[1:n_parts]:
            dh = dh + extra
        _, vjp = jax.vjp(_layer_norm, z, rest[n_parts], rest[n_parts + 1])
        dz, dlg, dlb = vjp(dh)
        return (dz,), (dlg, dlb)

    (dz,), (dlg, dlb) = rowwise(name + "_ln", fn_ln, [z] + list(dh_parts), [lg, lb], [(D_MODEL, F32)],
                                [_acc((1, D_MODEL)), _acc((1, D_MODEL))], tm)

    def fn_mlp(c, i, h, dz, wup, wdown):
        tile = h.shape[0]
        (dx,), dws = vjp_taps(functools.partial(mlp_chunk, wup, wdown), [(tile, D_MODEL)] * 2, [h], dz)
        return (dx,), dws

    aspec = ((N_FF_CHUNK, D_MODEL, D_MODEL), F32, (None, D_MODEL, D_MODEL), lambda c, i: (c, 0, 0))
    (dx,), (dwup, dwdown) = rowwise(name + "_mm", fn_mlp, [h_in, dz], [_ff_one(wup, layer), _ff_one(wdown, layer)],
                                    [(D_MODEL, F32, True)], [aspec, aspec], tm, nc=N_FF_CHUNK)
    return dz, dx, dwup, dwdown, dlg, dlb


def _sum_parts(dz, dx):
    out = ALPHA * dz
    for s in range(N_FF_CHUNK):
        out = out + dx[s]
    return out


def local_step(x, loss_target, p, late=None, early_hook=None):
    seq = x.shape[0]
    lp = TOK0 + seq
    tm = _pick_tm(lp, 384)
    tms = _pick_tm(lp, 192)
    e, et = _head_matrices()
    h0 = jnp.concatenate([jnp.zeros((PAD_FRONT, D_MODEL), F32), p["meta_tokens"], x], axis=0)
    hp = jnp.concatenate([jnp.zeros((1, D_MODEL), F32), h0[:-1]], axis=0)
    tgt = jnp.concatenate([jnp.zeros((TOK0, D_MODEL), F32), loss_target], axis=0)
    pos = jnp.maximum(jnp.arange(lp, dtype=F32) - PAD_FRONT, 0.0)
    inv_freq = 1.0 / (ROPE_THETA ** (jnp.arange(0, HEAD_DIM, 2, dtype=F32) / HEAD_DIM))
    ang = pos[:, None] * inv_freq[None, :]
    cos = jnp.tile(jnp.cos(ang), (1, PAIR // (HEAD_DIM // 2)))
    sin = jnp.tile(jnp.sin(ang), (1, PAIR // (HEAD_DIM // 2)))

    pre_vec = [p["a_mu"][j:j + 1] for j in range(6)] + [p["a_w0"], p["a_a0"], p["a_k_k"], p["a_k_a"]]
    pre_w = [p["a_w_r"], p["a_w_k"], p["a_w_v"], p["a_w1"], p["a_w2"], p["a_a1"], p["a_a2"], p["a_g1"], p["a_g2"]]
    n_vec = len(pre_vec)

    def fn_pre(c, i, h, hp, e, et, *ws):
        return rwkv_pre(e, et, ws[n_vec:], None, h, hp, *ws[:n_vec])[0], ()

    (r, lw, k2, v, an, bn, g), _ = rowwise("rwkv_pre", fn_pre, [h0, hp], [e, et] + pre_vec + pre_w,
                                           [(D_MODEL, F32)] * 7, [], tms)
    (y, *scan_saved), late_gathered = scan_fwd(r, lw, k2, v, an, bn, late[0] if late else ())
    if late:
        p = {**p, **late[1](late_gathered)}

    post_c =[p["a_w_o"], p["a_gn_w"], p["a_gn_b"], p["a_r_k"], p["ln_g00"], p["ln_b00"]]

    def fn_post(c, i, y, r, k2, v, g, h0, e, et, w_o, *vecs):
        return (rwkv_post(e, et, w_o, None, y, r, k2, v, g, h0, *vecs)[0],), ()

    (h1,), _ = rowwise("rwkv_post", fn_post, [y, r, k2, v, g, h0], [e, et] + post_c, [(D_MODEL, F32)], [], tm)
    h2, z2 = _mlp_layer_fwd("mlp0_fwd", h1, p["mlp_up"], p["mlp_down"], 0, p["ln_g01"], p["ln_b01"], tm)

    qkv_w = [p["b_w_q"], p["kv_w_k"], p["kv_w_v"]]

    def fn_qkv(c, i, h, cos, sin, wq, wk, wv):
        return qkv_proj(cos, sin, wq, wk, wv, None, h)[0], ()

    (q, k, vv), _ = rowwise("qkv_proj", fn_qkv, [h2, cos, sin], qkv_w,
                            [(D_MODEL, F32), (KV_DIM, F32), (KV_DIM, F32)], [], tm)
    sinks_b = jnp.broadcast_to(p["b_sinks"].reshape(N_HEADS_KV, GROUP, 1), (N_HEADS_KV, GROUP, PAIR))
    sinks_b = jnp.concatenate([sinks_b, jnp.zeros((N_HEADS_KV, 8 - GROUP, PAIR), F32)], axis=1)
    o = attn_fwd(q, k, vv, sinks_b)

    ao_c = [p["b_w_o"], p["ln_g10"], p["ln_b10"]]

    def fn_ao(c, i, o, h, w_o, lg, lb):
        return (attn_out(w_o, None, o, h, lg, lb)[0],), ()

    (h3,), _ = rowwise("attn_out", fn_ao, [o, h2], ao_c, [(D_MODEL, F32)], [], tm)
    h4, z4 = _mlp_layer_fwd("mlp1_fwd", h3, p["mlp_up"], p["mlp_down"], 1, p["ln_g11"], p["ln_b11"], tm)

    def fn_loss(c, i, h4, tgt):
        real = (_row_ids(i, tm) >= TOK0).astype(F32)
        err = (h4 - tgt) * real
        part = 0.5 * jnp.sum(jnp.sum(err * err, axis=-1, keepdims=True), axis=0, keepdims=True) / D_MODEL
        return (err * (1.0 / D_MODEL),), (jnp.broadcast_to(part, (8, PAIR)),)

    (dh4,), (loss_acc,) = rowwise("loss", fn_loss, [h4, tgt], [], [(D_MODEL, F32)], [_acc((8, PAIR))], tm)
    loss = loss_acc[0, 0]

    grads = {}
    dz4, dx4, grads["mlp_up1"], grads["mlp_down1"], grads["ln_g11"], grads["ln_b11"] = _mlp_layer_bwd(
        "mlp1_bwd", h3, z4, [dh4], p["mlp_up"], p["mlp_down"], 1, p["ln_g11"], p["ln_b11"], tm)

    def fn_ao_b(c, i, dz, dx, o, h, w_o, lg, lb):
        (do, dh, dlg, dlb), (dw_o,) = vjp_taps(functools.partial(attn_out, w_o), [(tm, D_MODEL)], [o, h, lg, lb],
                                               _sum_parts(dz, dx))
        return (do, dh), (dw_o, dlg, dlb)

    (do, dh2_a), (grads["b_w_o"], grads["ln_g10"], grads["ln_b10"]) = rowwise(
        "attn_out_bwd", fn_ao_b, [dz4, dx4, o, h2], ao_c, [(D_MODEL, F32)] * 2,
        [_acc((D_MODEL, D_MODEL)), _acc((1, D_MODEL)), _acc((1, D_MODEL))], tm)

    dq, dkc, dkp, dvc, dvp, dsinks = attn_bwd(q, k, vv, sinks_b, do)
    grads["b_sinks"] = dsinks[:, :GROUP, 0].reshape(1, N_HEADS)
    zblk = jnp.zeros((BLOCK, KV_DIM), F32)
    dkp_s = jnp.concatenate([dkp[BLOCK:], zblk], axis=0)
    dvp_s = jnp.concatenate([dvp[BLOCK:], zblk], axis=0)

    def fn_qkv_b(c, i, h, cos, sin, dq, dkc, dkp, dvc, dvp, wq, wk, wv):
        return vjp_taps(functools.partial(qkv_proj, cos, sin, wq, wk, wv),
                        [(tm, D_MODEL), (tm, KV_DIM), (tm, KV_DIM)], [h], (dq, dkc + dkp, dvc + dvp))

    (dh2_q,), (grads["b_w_q"], grads["kv_w_k"], grads["kv_w_v"]) = rowwise(
        "qkv_proj_bwd", fn_qkv_b, [h2, cos, sin, dq, dkc, dkp_s, dvc, dvp_s], qkv_w, [(D_MODEL, F32)],
        [_acc((D_MODEL, D_MODEL)), _acc((D_MODEL, KV_DIM)), _acc((D_MODEL, KV_DIM))], tm)

    dz2, dx2, grads["mlp_up0"], grads["mlp_down0"], grads["ln_g01"], grads["ln_b01"] = _mlp_layer_bwd(
        "mlp0_bwd", h1, z2, [dh2_a, dh2_q], p["mlp_up"], p["mlp_down"], 0, p["ln_g01"], p["ln_b01"], tm)

    def fn_post_b(c, i, dz, dx, y, r, k2, v, g, h0, e, et, w_o, *vecs):
        out, dws = vjp_taps(functools.partial(rwkv_post, e, et, w_o), [(tms, D_MODEL)],
                            [y, r, k2, v, g, h0] + list(vecs), _sum_parts(dz, dx))
        return out[:6], tuple(dws) + tuple(out[6:])

    (dy, dr_c, dk_c, dv_c, dg, dh0_c), post_g = rowwise(
        "rwkv_post_bwd", fn_post_b, [dz2, dx2, y, r, k2, v, g, h0], [e, et] + post_c, [(D_MODEL, F32)] * 6,
        [_acc((D_MODEL, D_MODEL))] + [_acc((1, D_MODEL))] * 5, tms)
    for name, val in zip(["a_w_o", "a_gn_w", "a_gn_b", "a_r_k", "ln_g00", "ln_b00"], post_g):
        grads[name] = val

    (dr, dlw, dk2, dv, dan, dbn), early_from_chips = scan_bwd(r, lw, k2, v, an, bn, scan_saved, dy,
                                                              (dr_c, dk_c, dv_c),
                                                              early_hook(grads) if early_hook else ())

    def fn_pre_b(c, i, h, hp, dr, dlw, dk2, dv, dan, dbn, dg, e, et, *ws):
        real = (_row_ids(i, tms) >= PAD_FRONT).astype(F32)
        cot = tuple(t * real for t in (dr, dlw, dk2, dv, dan, dbn, dg))
        out, dws = vjp_taps(functools.partial(rwkv_pre, e, et, ws[n_vec:]), [(tms, n) for n in PRE_TAPS],
                            [h, hp] + list(ws[:n_vec]), cot)
        return out[:2], tuple(out[2:]) + tuple(dws)

    (dh0_p, dhp), pre_g = rowwise(
        "rwkv_pre_bwd", fn_pre_b, [h0, hp, dr, dlw, dk2, dv, dan, dbn, dg],
        [e, et] + pre_vec + pre_w, [(D_MODEL, F32)] * 2,
        [_acc((1, D_MODEL))] * n_vec + [_acc(w.shape) for w in pre_w], tms)
    grads["a_mu"] = jnp.concatenate(pre_g[:6], axis=0)
    for name, val in zip(["a_w0", "a_a0", "a_k_k", "a_k_a", "a_w_r", "a_w_k", "a_w_v", "a_w1", "a_w2", "a_a1",
                          "a_a2", "a_g1", "a_g2"], pre_g[6:]):
        grads[name] = val

    dhp_s = jnp.concatenate([dhp[1:], jnp.zeros((1, D_MODEL), F32)], axis=0)

    def fn_add(c, i, a, b, d):
        return (a + b + d,), ()

    (dh0,), _ = rowwise("grad_h0", fn_add, [dh0_c, dh0_p, dhp_s], [], [(D_MODEL, F32)], [], tm)
    grads["meta_tokens"] = dh0[PAD_FRONT:TOK0]
    return loss, dh0[TOK0:], grads, early_from_chips


ANY = pl.BlockSpec(memory_space=pl.ANY)
XY_FLIPS = ((0, 1), (1, 0), (1, 1))
ALL_FLIPS = tuple((e >> 2 & 1, e >> 1 & 1, e & 1) for e in range(1, N_DEV))


def _flip(v, bit):
    return 1 - v if bit else v


def _sem_scratch(n):
    return [pltpu.SemaphoreType.DMA((n,)), pltpu.SemaphoreType.DMA((n,))]


def gather_copies(src, dst, ici_send, ici_recv, d2d_send, d2d_recv):
    npeer = len(XY_FLIPS)
    x, y, c = lax.axis_index("x"), lax.axis_index("y"), lax.axis_index("c")

    def half(ref, k, which):
        h = src[k].shape[0] // 2
        start = which * h
        return ref.at[pl.ds(pl.multiple_of(start, 8) if h % 8 == 0 else start, h)]

    def ici(k, j, slot):
        fx, fy = XY_FLIPS[j]
        return pltpu.make_async_remote_copy(
            src_ref=half(src[k], k, c), dst_ref=half(dst[k].at[slot], k, c), send_sem=ici_send.at[k * npeer + j],
            recv_sem=ici_recv.at[k * npeer + j], device_id=(_flip(x, fx), _flip(y, fy), c), device_id_type=MESH)

    def d2d(k, j, which):
        fx, fy = XY_FLIPS[j]
        landed = half(dst[k].at[2 * _flip(x, fx) + _flip(y, fy)], k, which)
        return pltpu.make_async_remote_copy(
            src_ref=landed, dst_ref=landed, send_sem=d2d_send.at[k * npeer + j], recv_sem=d2d_recv.at[k * npeer + j],
            device_id=(x, y, 1 - c), device_id_type=MESH)

    pairs = [(k, j) for k in range(len(src)) for j in range(npeer)]
    return ([ici(k, j, 2 * x + y) for k, j in pairs],
            [ici(k, j, 2 * _flip(x, XY_FLIPS[j][0]) + _flip(y, XY_FLIPS[j][1])) for k, j in pairs],
            [d2d(k, j, c) for k, j in pairs], [d2d(k, j, 1 - c) for k, j in pairs])


def gather_scratch(n):
    return _sem_scratch(n * len(XY_FLIPS)) * 2


def gathered_shapes(shards):
    return [jax.ShapeDtypeStruct((N_SHARD,) + s.shape, s.dtype) for s in shards]


def fill_own(gathered, shards):
    if not shards:
        return []
    slot = 2 * lax.axis_index("x") + lax.axis_index("y")
    return [lax.dynamic_update_index_in_dim(g, s, slot, 0) for g, s in zip(gathered, shards)]


def all_gather_shards(shards):
    n = len(shards)

    def body(*refs):
        sends, arrivals, forwards, forwarded = gather_copies(refs[:n], refs[n:2 * n], *refs[2 * n:])
        for cp in sends:
            cp.start()
        for landed, onward in zip(arrivals, forwards):
            landed.wait_recv()
            onward.start()
        for cp in forwarded:
            cp.wait_recv()
        for cp in sends + forwards:
            cp.wait_send()

    out = pl.pallas_call(body, name="gather_weights", in_specs=[ANY] * n, out_specs=[ANY] * n,
                         out_shape=gathered_shapes(shards), scratch_shapes=gather_scratch(n))(*shards)
    return fill_own(out, shards)


def placement():
    x, y, c = lax.axis_index("x"), lax.axis_index("y"), lax.axis_index("c")
    me = 2 * x + y
    others = [j + (j >= me).astype(jnp.int32) for j in range(N_SHARD - 1)]
    return jnp.stack([c, me] + others).astype(jnp.int32)


def pair_exchange(name, sources):
    n = len(sources)

    def body(*refs):
        src, got = refs[:n], refs[n:2 * n]
        send_sems, recv_sems = refs[2 * n:]
        x, y, c = lax.axis_index("x"), lax.axis_index("y"), lax.axis_index("c")

        def copy(k):
            half = sources[k].shape[1] // 2
            theirs = src[k].at[:, pl.ds(pl.multiple_of((1 - c) * half, 8), half), :]
            return pltpu.make_async_remote_copy(
                src_ref=theirs, dst_ref=got[k], send_sem=send_sems.at[k], recv_sem=recv_sems.at[k],
                device_id=(x, y, 1 - c), device_id_type=MESH)

        sends = [copy(k) for k in range(n)]
        for cp in sends:
            cp.start()
        for cp in sends:
            cp.wait_recv()
        for cp in sends:
            cp.wait_send()

    halves = [jax.ShapeDtypeStruct((s.shape[0], s.shape[1] // 2, s.shape[2]), s.dtype) for s in sources]
    return pl.pallas_call(body, name=name, in_specs=[ANY] * n, out_specs=[ANY] * n,
                          out_shape=halves, scratch_shapes=_sem_scratch(n))(*sources)


def chip_exchange(parts):
    n = len(parts)

    def body(*refs):
        sends, arrivals = chip_exchange_copies(refs[:n], refs[n:2 * n], *refs[2 * n:])
        for cp in sends:
            cp.start()
        for cp in arrivals:
            cp.wait_recv()
        for cp in sends:
            cp.wait_send()

    return pl.pallas_call(
        body, name="grads_chip_exchange", in_specs=[ANY] * n, out_specs=[ANY] * n,
        out_shape=[jax.ShapeDtypeStruct(p.shape, p.dtype) for p in parts],
        scratch_shapes=_sem_scratch(n * len(XY_FLIPS)),
    )(*parts)


def chip_exchange_copies(src, dst, send_sems, recv_sems):
    npeer = len(XY_FLIPS)
    x, y, c = lax.axis_index("x"), lax.axis_index("y"), lax.axis_index("c")
    me = 2 * x + y

    def copy(k, j, sending):
        fx, fy = XY_FLIPS[j]
        px, py = _flip(x, fx), _flip(y, fy)
        peer = 2 * px + py
        return pltpu.make_async_remote_copy(
            src_ref=src[k].at[peer], dst_ref=dst[k].at[me if sending else peer],
            send_sem=send_sems.at[k * npeer + j], recv_sem=recv_sems.at[k * npeer + j],
            device_id=(px, py, c), device_id_type=MESH)

    pairs = [(k, j) for k in range(len(src)) for j in range(npeer)]
    return [copy(k, j, True) for k, j in pairs], [copy(k, j, False) for k, j in pairs]


def sibling_share(halves):
    n = len(halves)

    def body(*refs):
        src, got = refs[:n], refs[n:2 * n]
        send_sems, recv_sems = refs[2 * n:]
        x, y, c = lax.axis_index("x"), lax.axis_index("y"), lax.axis_index("c")
        sends = [pltpu.make_async_remote_copy(
            src_ref=src[k], dst_ref=got[k], send_sem=send_sems.at[k], recv_sem=recv_sems.at[k],
            device_id=(x, y, 1 - c), device_id_type=MESH) for k in range(n)]
        for cp in sends:
            cp.start()
        for cp in sends:
            cp.wait_recv()
        for cp in sends:
            cp.wait_send()

    return pl.pallas_call(
        body, name="grads_sibling_share", in_specs=[ANY] * n, out_specs=[ANY] * n,
        out_shape=[jax.ShapeDtypeStruct(h.shape, h.dtype) for h in halves], scratch_shapes=_sem_scratch(n),
    )(*halves)


ADD_TILE_ELEMS = 512 * 1024


def _row_tile(rows, cols):
    return max(t for t in range(8, rows + 1, 8) if rows % t == 0 and t * cols <= ADD_TILE_ELEMS)


def _prefetch_call(body, name, place, grid, in_specs, out_specs, out_shape, args):
    return pl.pallas_call(
        body, name=name, out_shape=out_shape,
        grid_spec=pltpu.PrefetchScalarGridSpec(num_scalar_prefetch=1, grid=grid, in_specs=in_specs,
                                               out_specs=out_specs),
        compiler_params=pltpu.CompilerParams(dimension_semantics=("arbitrary",) * len(grid),
                                             vmem_limit_bytes=VMEM_LIMIT),
    )(place, *args)


def pair_add(name, place, src, got, dtype):
    n4, half, cols = got.shape
    tile = _row_tile(half, cols)
    nt = half // tile

    def body(pr, a_ref, b_ref, o_ref):
        o_ref[...] = (a_ref[...] + b_ref[...]).astype(o_ref.dtype)

    mine = pl.BlockSpec((None, tile, cols), lambda s, i, pr: (s, pr[0] * nt + i, 0))
    blk = pl.BlockSpec((None, tile, cols), lambda s, i, pr: (s, i, 0))
    return _prefetch_call(body, name, place, (n4, nt), [mine, blk], blk,
                          jax.ShapeDtypeStruct(got.shape, dtype), (src, got))


def chip_add(name, place, part, from_chips):
    _, half, cols = part.shape
    tile = _row_tile(half, cols)

    def body(pr, own_ref, r0_ref, r1_ref, r2_ref, o_ref):
        me = pr[1]
        own, r0, r1, r2 = (r[...].astype(F32) for r in (own_ref, r0_ref, r1_ref, r2_ref))
        t0 = jnp.where(me == 0, own, r0)
        t1 = jnp.where(me == 0, r0, jnp.where(me == 1, own, r1))
        t2 = jnp.where(me <= 1, r1, jnp.where(me == 2, own, r2))
        t3 = jnp.where(me == 3, own, r2)
        o_ref[...] = ((t0 + t1) + t2) + t3

    def slab(j):
        return pl.BlockSpec((None, tile, cols), lambda i, pr: (pr[j], i, 0))

    return _prefetch_call(body, name, place, (half // tile,), [slab(1), slab(2), slab(3), slab(4)],
                          pl.BlockSpec((tile, cols), lambda i, pr: (i, 0)),
                          jax.ShapeDtypeStruct((half, cols), F32), (part, from_chips, from_chips, from_chips))


def pair_sums(tag, place, sources, narrow):
    got = pair_exchange("grads_pair_exchange_" + tag, sources)
    return [pair_add(f"grads_pair_add_{tag}{k}", place, s, g, BF16 if nar else F32)
            for k, (s, g, nar) in enumerate(zip(sources, got, narrow))]


def finish_sums(place, parts, from_chips):
    halves = [chip_add(f"grads_chip_add{k}", place, p, f) for k, (p, f) in enumerate(zip(parts, from_chips))]
    return list(zip(halves, sibling_share(halves)))


ADAM_ROWS = 256


def adamw_update(name, place, halves, w, m, v):
    nsub, rows, cols = w.shape
    half = rows // 2
    tr = ADAM_ROWS if half % ADAM_ROWS == 0 else half
    nth = half // tr

    def body(pr, *refs):
        g_refs, (w_ref, m_ref, v_ref, g_ref, d_ref, nm_ref, nv_ref) = refs[:2 * nsub], refs[2 * nsub:]
        l = pl.program_id(0)
        mine = (pl.program_id(1) // nth) == pr[0]
        g = None
        for s in range(nsub):
            gs = jnp.where(mine, g_refs[2 * s][...], g_refs[2 * s + 1][...])
            g = gs if g is None else jnp.where(l == s, gs, g)
        m2 = ADAM_B1 * m_ref[...] + (1.0 - ADAM_B1) * g
        v2 = ADAM_B2 * v_ref[...] + (1.0 - ADAM_B2) * (g * g)
        m_hat = m2 / (1.0 - ADAM_B1 ** ADAM_STEP)
        v_hat = v2 / (1.0 - ADAM_B2 ** ADAM_STEP)
        g_ref[...] = g
        d_ref[...] = -ADAM_LR * (m_hat / (jnp.sqrt(v_hat) + ADAM_EPS) + ADAM_WD * w_ref[...])
        nm_ref[...] = m2
        nv_ref[...] = v2

    gblk = pl.BlockSpec((tr, cols), lambda l, i, pr: (i % nth, 0))
    blk = pl.BlockSpec((None, tr, cols), lambda l, i, pr: (l, i, 0))
    out = jax.ShapeDtypeStruct((nsub, rows, cols), F32)
    return _prefetch_call(body, name, place, (nsub, rows // tr), [gblk] * (2 * nsub) + [blk] * 3, [blk] * 4,
                          [out] * 4, [h for pair in halves for h in pair] + [w, m, v])


WEIGHT_NAMES = ("meta_tokens", "a_mu", "a_w_r", "a_w_k", "a_w_v", "a_w_o", "a_w0", "a_w1", "a_w2", "a_a0", "a_a1",
                "a_a2", "a_g1", "a_g2", "a_k_k", "a_k_a", "a_r_k", "a_gn_w", "a_gn_b", "kv_w_k", "kv_w_v", "b_w_q",
                "b_sinks", "b_w_o", "mlp_w_up", "mlp_w_down", "ln_g", "ln_b")
BIG_NAMES = ("a_w_r", "a_w_k", "a_w_v", "a_w_o", "b_w_q", "b_w_o")
EARLY_NAMES, LATE_NAMES = BIG_NAMES[:3], BIG_NAMES[3:]
PACK_MATS = (("kv_w_k", 256), ("kv_w_v", 256), ("a_w1", 64), ("a_a1", 64), ("a_g1", 128), ("a_w2", 64),
             ("a_a2", 64), ("a_g2", 128))
COLUMN_CUT = ("a_w2", "a_a2", "a_g2")
PACK_VECS = (("a_mu", 6), ("a_w0", 1), ("a_a0", 1), ("a_k_k", 1), ("a_k_a", 1), ("a_gn_w", 1), ("a_gn_b", 1),
             ("ln_g", 4), ("ln_b", 4), ("meta_tokens", 16))
PACK_REPL = (("a_r_k", 4), ("b_sinks", 1))
SHARD_W = D_MODEL // N_SHARD
N_MAT_ROWS = sum(r for _, r in PACK_MATS)
N_VEC_ROWS = sum(r for _, r in PACK_VECS)
N_PACK_ROWS = -(-(N_MAT_ROWS + N_VEC_ROWS + sum(r for _, r in PACK_REPL)) // 8) * 8
N_GATHER_VEC_ROWS = -(-N_VEC_ROWS // 16) * 16


def _pack_rows(arr):
    if arr.size == N_HEADS:
        return jnp.pad(arr.reshape(1, N_HEADS), ((0, 0), (0, SHARD_W - N_HEADS)))
    return arr.reshape(-1, SHARD_W)


def pack_small(get):
    parts = [_pack_rows(get(name)) for name, _ in PACK_MATS + PACK_VECS + PACK_REPL]
    used = sum(p.shape[0] for p in parts)
    return jnp.concatenate(parts + [jnp.zeros((N_PACK_ROWS - used, SHARD_W), F32)], axis=0)


def unpack_small(pack, shapes):
    out, off = {}, 0
    for name, rows in PACK_MATS + PACK_VECS + PACK_REPL:
        piece = pack[off:off + rows]
        off += rows
        out[name] = piece[:, :N_HEADS].reshape(shapes[name]) if name == "b_sinks" else piece.reshape(shapes[name])
    return out


def whole_weights(big_names, gathered_big, mats, vecs, a_r_k, b_sinks):
    p = {name: g.reshape(D_MODEL, D_MODEL) for name, g in zip(big_names, gathered_big)}
    off = 0
    for name, rows in PACK_MATS:
        piece = mats[:, off:off + rows]
        off += rows
        if name in COLUMN_CUT:
            p[name] = piece.transpose(1, 0, 2).reshape(rows, D_MODEL)
        else:
            p[name] = piece.reshape(D_MODEL, rows)
    v = vecs.transpose(1, 0, 2).reshape(-1, D_MODEL)
    off = 0
    for name, rows in PACK_VECS:
        p[name] = v[off:off + rows]
        off += rows
    for i in range(2):
        for j in range(2):
            p[f"ln_g{i}{j}"] = p["ln_g"][2 * i + j:2 * i + j + 1]
            p[f"ln_b{i}{j}"] = p["ln_b"][2 * i + j:2 * i + j + 1]
    p["a_r_k"] = a_r_k.reshape(1, D_MODEL)
    p["b_sinks"] = b_sinks
    return p


def small_grad_pack(g):
    parts = []
    for name, rows in PACK_MATS:
        if name in COLUMN_CUT:
            parts.append(g[name].reshape(rows, N_SHARD, SHARD_W).transpose(1, 0, 2))
        else:
            parts.append(g[name].reshape(N_SHARD, rows, SHARD_W))
    vec_rows = [g["a_mu"]] + [g[n] for n in ("a_w0", "a_a0", "a_k_k", "a_k_a", "a_gn_w", "a_gn_b")]
    vec_rows += [g[f"ln_g{i}{j}"] for i in range(2) for j in range(2)]
    vec_rows += [g[f"ln_b{i}{j}"] for i in range(2) for j in range(2)] + [g["meta_tokens"]]
    parts.append(jnp.concatenate(vec_rows, axis=0).reshape(N_VEC_ROWS, N_SHARD, SHARD_W).transpose(1, 0, 2))
    parts.append(jnp.broadcast_to(g["a_r_k"].reshape(1, -1, SHARD_W), (N_SHARD, D_MODEL // SHARD_W, SHARD_W)))
    sinks = jnp.pad(g["b_sinks"].reshape(1, 1, N_HEADS), ((0, 0), (0, 0), (0, SHARD_W - N_HEADS)))
    parts.append(jnp.broadcast_to(sinks, (N_SHARD, 1, SHARD_W)))
    used = sum(p.shape[1] for p in parts)
    parts.append(jnp.zeros((N_SHARD, N_PACK_ROWS - used, SHARD_W), F32))
    return jnp.concatenate(parts, axis=1)


def train_step(vals):
    w = {n: vals[n] for n in WEIGHT_NAMES}
    w_pack = pack_small(lambda n: w[n])
    early = [w[n][0].astype(BF16) for n in EARLY_NAMES]
    early += [w_pack[:N_MAT_ROWS].astype(BF16), w_pack[N_MAT_ROWS:N_MAT_ROWS + N_GATHER_VEC_ROWS]]
    gathered = all_gather_shards(early)
    ne = len(EARLY_NAMES)
    p = whole_weights(EARLY_NAMES, gathered[:ne], gathered[ne], gathered[ne + 1][:, :N_VEC_ROWS], w["a_r_k"],
                      w["b_sinks"])
    late = [w[n][0].astype(BF16) for n in LATE_NAMES] + [w["mlp_w_up"].astype(BF16), w["mlp_w_down"].astype(BF16)]
    nb = len(BIG_NAMES)

    def late_weights(got):
        out = {n: x.reshape(D_MODEL, D_MODEL) for n, x in zip(LATE_NAMES, got)}
        out["mlp_up"], out["mlp_down"] = got[len(LATE_NAMES):]
        return out

    place = placement()
    ready = {}

    def early_hook(g):
        srcs = [g[n].reshape(N_SHARD, SHARD_W, D_MODEL) for n in LATE_NAMES]
        srcs += [g["mlp_up0"], g["mlp_up1"], g["mlp_down0"], g["mlp_down1"]]
        ready["parts"] = pair_sums("early", place, srcs, [True] * len(srcs))
        return ready["parts"]

    loss, gx, g, early_from_chips = local_step(vals["x"][0], vals["loss_target"][0], p, (late, late_weights),
                                               early_hook)
    loss = lax.psum(loss, ("x", "y", "c"))
    srcs = [g[n].reshape(N_SHARD, SHARD_W, D_MODEL) for n in EARLY_NAMES] + [small_grad_pack(g)]
    rest = pair_sums("late", place, srcs, [True] * len(EARLY_NAMES) + [False])
    rest_from_chips = chip_exchange(rest)
    ne = len(EARLY_NAMES)
    halves = finish_sums(place, rest[:ne] + ready["parts"] + rest[ne:],
                         list(rest_from_chips[:ne]) + list(early_from_chips) + list(rest_from_chips[ne:]))

    res = {}
    for k, n in enumerate(BIG_NAMES):
        res[n] = adamw_update("adamw_" + n, place, halves[k:k + 1], w[n], vals["m_" + n], vals["v_" + n])
    for k, n in ((nb, "mlp_w_up"), (nb + 2, "mlp_w_down")):
        res[n] = adamw_update("adamw_" + n, place, halves[k:k + 2], w[n], vals["m_" + n], vals["v_" + n])
    packs = adamw_update("adamw_small", place, halves[-1:], w_pack[None], pack_small(lambda n: vals["m_" + n])[None],
                         pack_small(lambda n: vals["v_" + n])[None])
    shapes = {n: w[n].shape for n in WEIGHT_NAMES}
    small = [unpack_small(pk[0], shapes) for pk in packs]
    outs = [loss, gx[None]]
    for t in range(4):
        outs += [res[n][t] if n in res else small[t][n] for n in WEIGHT_NAMES]
    return tuple(outs)


def kernel(x, meta_tokens, a_mu, a_w_r, a_w_k, a_w_v, a_w_o, a_w0, a_w1, a_w2, a_a0, a_a1, a_a2, a_g1, a_g2, a_k_k,
           a_k_a, a_r_k, a_gn_w, a_gn_b, kv_w_k, kv_w_v, b_w_q, b_sinks, b_w_o, mlp_w_up, mlp_w_down, ln_g, ln_b,
           loss_target, m_meta_tokens, m_a_mu, m_a_w_r, m_a_w_k, m_a_w_v, m_a_w_o, m_a_w0, m_a_w1, m_a_w2, m_a_a0,
           m_a_a1, m_a_a2, m_a_g1, m_a_g2, m_a_k_k, m_a_k_a, m_a_r_k, m_a_gn_w, m_a_gn_b, m_kv_w_k, m_kv_w_v,
           m_b_w_q, m_b_sinks, m_b_w_o, m_mlp_w_up, m_mlp_w_down, m_ln_g, m_ln_b, v_meta_tokens, v_a_mu, v_a_w_r,
           v_a_w_k, v_a_w_v, v_a_w_o, v_a_w0, v_a_w1, v_a_w2, v_a_a0, v_a_a1, v_a_a2, v_a_g1, v_a_g2, v_a_k_k,
           v_a_k_a, v_a_r_k, v_a_gn_w, v_a_gn_b, v_kv_w_k, v_kv_w_v, v_b_w_q, v_b_sinks, v_b_w_o, v_mlp_w_up,
           v_mlp_w_down, v_ln_g, v_ln_b):
    return train_step(dict(locals()))
```

```python
import functools

import numpy as np
import jax
import jax.numpy as jnp
from jax import lax
from jax.experimental import pallas as pl
from jax.experimental.pallas import tpu as pltpu

F32 = jnp.float32
BF16 = jnp.bfloat16

D_MODEL = 1024
N_HEADS = 16
HEAD_DIM = 64
N_HEADS_KV = 4
GROUP = 4
KV_DIM = N_HEADS_KV * HEAD_DIM
N_META = 16
BLOCK = 128
PAD_FRONT = BLOCK - N_META
TOK0 = PAD_FRONT + N_META
N_FF_CHUNK = 4
N_SHARD = 4
N_DEV = 8
GN_EPS = 64e-5
LN_EPS = 1e-5
ROPE_THETA = 10000.0
ALPHA = 4.0 ** 0.25
ADAM_LR, ADAM_B1, ADAM_B2, ADAM_EPS, ADAM_WD, ADAM_STEP = 0.001, 0.9, 0.999, 1e-08, 0.01, 10
SCAN_T = 64
PAIR = 128
KVW = GROUP * HEAD_DIM
VMEM_LIMIT = 60 * 1024 * 1024
HI = lax.Precision.HIGHEST
MESH = pl.DeviceIdType.MESH


def _dot(a, b, ca, cb):
    return lax.dot_general(a.astype(BF16), b.astype(BF16), (((ca,), (cb,)), ((), ())),
                           preferred_element_type=F32)


@jax.custom_vjp
def mm(a, b):
    return _dot(a, b, 1, 0)


def _mm_fwd(a, b):
    return mm(a, b), b


def _mm_bwd(b, g):
    return _dot(g, b, 1, 1), jnp.zeros_like(b)


mm.defvjp(_mm_fwd, _mm_bwd)


@jax.custom_vjp
def mm_tap(a, b, tap):
    return _dot(a, b, 1, 0)


mm_tap.defvjp(lambda a, b, tap: (_dot(a, b, 1, 0), b), lambda b, g: (_dot(g, b, 1, 1), jnp.zeros_like(b), g))


def tmm(x, w, taps, xs):
    y = mm(x, w) if taps is None else mm_tap(x, w, taps[len(xs)])
    xs.append(x)
    return y


def vjp_taps(core, tap_shapes, args, cot):
    taps = [jnp.zeros(s, F32) for s in tap_shapes]
    _, vjp, xs = jax.vjp(core, taps, *args, has_aux=True)
    out = vjp(cot)
    return out[1:], [_dot(x, g, 0, 0) for x, g in zip(xs, out[0])]


def _split3(x):
    x1 = x.astype(BF16)
    r1 = x - x1.astype(F32)
    x2 = r1.astype(BF16)
    x3 = (r1 - x2.astype(F32)).astype(BF16)
    return x1, x2, x3


def _exact_dot(x, m01, cb=0):
    acc = None
    for piece in _split3(x):
        t = lax.dot_general(piece, m01, (((1,), (cb,)), ((), ())), preferred_element_type=F32)
        acc = t if acc is None else acc + t
    return acc


def _head_matrices():
    e = np.zeros((D_MODEL, N_HEADS), np.float32)
    e[np.arange(D_MODEL), np.arange(D_MODEL) // HEAD_DIM] = 1.0
    return jnp.asarray(e, BF16), jnp.asarray(e.T, BF16)


@jax.custom_vjp
def hsum(x, e, et):
    return _exact_dot(x, e)


@jax.custom_vjp
def hbc(s, e, et):
    return _exact_dot(s, et)


hsum.defvjp(lambda x, e, et: (_exact_dot(x, e), (e, et)),
            lambda res, g: (hbc(g, *res), jnp.zeros_like(res[0]), jnp.zeros_like(res[1])))
hbc.defvjp(lambda s, e, et: (_exact_dot(s, et), (e, et)),
           lambda res, g: (hsum(g, *res), jnp.zeros_like(res[0]), jnp.zeros_like(res[1])))


def _sigmoid(u):
    return 0.5 * (jnp.tanh(0.5 * u) + 1.0)


def _softplus(u):
    return jnp.maximum(u, 0.0) + jnp.log(1.0 + jnp.exp(-jnp.abs(u)))


def _layer_norm(z, g, b):
    mu = jnp.mean(z, axis=-1, keepdims=True)
    zc = z - mu
    var = jnp.mean(zc * zc, axis=-1, keepdims=True)
    return zc * lax.rsqrt(var + LN_EPS) * g + b


def _zero_map(nd):
    return lambda c, i: (0,) * nd


def _params():
    return pltpu.CompilerParams(dimension_semantics=("arbitrary", "arbitrary"), vmem_limit_bytes=VMEM_LIMIT)


def rowwise(name, fn, rows, consts, out_rows, out_accs, tm, nc=1, gather=()):
    lp = rows[0].shape[-2]
    nt = lp // tm
    assert nt * tm == lp, (name, lp, tm)
    ng = len(gather)
    in_specs, args = [], []
    for a in rows:
        if a.ndim == 2:
            in_specs.append(pl.BlockSpec((tm, a.shape[1]), lambda c, i: (i, 0)))
        else:
            in_specs.append(pl.BlockSpec((a.shape[0], tm, a.shape[2]), lambda c, i: (0, i, 0)))
        args.append(a)
    for cst in consts:
        if isinstance(cst, tuple):
            arr, bs, im = cst
            in_specs.append(pl.BlockSpec(bs, im))
        else:
            arr = cst
            in_specs.append(pl.BlockSpec(arr.shape, _zero_map(arr.ndim), pipeline_mode=pl.Buffered(1)))
        args.append(arr)
    out_shape, out_specs, acc_per_chunk = [], [], []
    for spec in out_rows:
        if len(spec) == 3 and spec[2]:
            out_shape.append(jax.ShapeDtypeStruct((nc, lp, spec[0]), spec[1]))
            out_specs.append(pl.BlockSpec((None, tm, spec[0]), lambda c, i: (c, i, 0)))
        else:
            out_shape.append(jax.ShapeDtypeStruct((lp, spec[0]), spec[1]))
            out_specs.append(pl.BlockSpec((tm, spec[0]), lambda c, i: (i, 0)))
    for spec in out_accs:
        out_shape.append(jax.ShapeDtypeStruct(spec[0], spec[1]))
        if len(spec) == 4:
            out_specs.append(pl.BlockSpec(spec[2], spec[3]))
            acc_per_chunk.append(True)
        else:
            out_specs.append(pl.BlockSpec(spec[0], _zero_map(len(spec[0])), pipeline_mode=pl.Buffered(1)))
            acc_per_chunk.append(False)
    n_in, n_or, n_out = len(args), len(out_rows), len(out_shape)

    def body(*refs):
        c = pl.program_id(0)
        i = pl.program_id(1)
        if ng:
            src, dst = refs[n_in:n_in + ng], refs[n_in + ng + n_out:n_in + 2 * ng + n_out]
            sends, arrivals, forwards, forwarded = gather_copies(src, dst, *refs[n_in + 2 * ng + n_out:])

            @pl.when(jnp.logical_and(c == 0, i == 0))
            def _():
                for cp in sends:
                    cp.start()

        vals = [r[...] for r in refs[:n_in]]
        outs_r, outs_a = fn(c, i, *vals)
        out_refs = refs[n_in + ng:n_in + ng + n_out]
        for ref, val in zip(out_refs[:n_or], outs_r):
            ref[...] = val.astype(ref.dtype)
        for ref, val, per_chunk in zip(out_refs[n_or:], outs_a, acc_per_chunk):
            first = (i == 0) if per_chunk else jnp.logical_and(i == 0, c == 0)

            @pl.when(first)
            def _():
                ref[...] = val.astype(ref.dtype)

            @pl.when(jnp.logical_not(first))
            def _():
                ref[...] += val.astype(ref.dtype)

        if ng:
            @pl.when(jnp.logical_and(c == nc - 1, i == max(nt - 3, 0)))
            def _():
                for landed, onward in zip(arrivals, forwards):
                    landed.wait_recv()
                    onward.start()

            @pl.when(jnp.logical_and(c == nc - 1, i == nt - 1))
            def _():
                for cp in forwarded:
                    cp.wait_recv()
                for cp in sends + forwards:
                    cp.wait_send()

    outs = pl.pallas_call(body, name=name, grid=(nc, nt), in_specs=in_specs + [ANY] * ng,
                          out_specs=out_specs + [ANY] * ng, out_shape=out_shape + gathered_shapes(gather),
                          scratch_shapes=gather_scratch(ng) if ng else [], compiler_params=_params())(*args, *gather)
    if ng:
        return outs[:n_or], outs[n_or:n_out], fill_own(outs[n_out:], gather)
    return outs[:n_or], outs[n_or:]


def _row_ids(i, tm):
    return i * tm + lax.broadcasted_iota(jnp.int32, (tm, 1), 0)


PRE_TAPS = (D_MODEL, D_MODEL, D_MODEL, 64, D_MODEL, 64, D_MODEL, 128, D_MODEL)


def rwkv_pre(e, et, ws, taps, h, hp, mu_r, mu_w, mu_k, mu_v, mu_a, mu_g, w0, a0, k_k, k_a):
    w_r, w_k, w_v, w1, w2, a1, a2, g1, g2 = ws
    xs = []
    xx = hp - h
    r = tmm(h + xx * mu_r, w_r, taps, xs)
    k = tmm(h + xx * mu_k, w_k, taps, xs)
    v = tmm(h + xx * mu_v, w_v, taps, xs)
    wraw = -_softplus(-(w0 + tmm(jnp.tanh(tmm(h + xx * mu_w, w1, taps, xs)), w2, taps, xs))) - 0.5
    lw = -jnp.exp(wraw)
    a = _sigmoid(a0 + tmm(tmm(h + xx * mu_a, a1, taps, xs), a2, taps, xs))
    g = tmm(_sigmoid(tmm(h + xx * mu_g, g1, taps, xs)), g2, taps, xs)
    kk = k * k_k
    ss = hsum(kk * kk, e, et)
    pos = ss > 0.0
    nrm = jnp.where(pos, jnp.sqrt(jnp.where(pos, ss, 1.0)), 0.0)
    kk = kk * hbc(1.0 / jnp.maximum(nrm, 1e-12), e, et)
    k2 = k * (1.0 + (a - 1.0) * k_a)
    return (r, lw, k2, v, -kk, kk * a, g), xs


def rwkv_post(e, et, w_o, taps, y, r, k2, v, g, h0, gn_w, gn_b, rk, lg, lb):
    xs = []
    inv_n = 1.0 / HEAD_DIM
    yc = y - hbc(hsum(y, e, et) * inv_n, e, et)
    yv = hsum(yc * yc, e, et) * inv_n
    yn = yc * hbc(lax.rsqrt(yv + GN_EPS), e, et) * gn_w + gn_b
    bonus = hbc(hsum(r * k2 * rk, e, et), e, et) * v
    mix = tmm((yn + bonus) * g, w_o, taps, xs)
    return _layer_norm(ALPHA * h0 + mix, lg, lb), xs


@jax.custom_vjp
def sq_relu(x):
    r = jnp.maximum(x, 0.0)
    return r * r


sq_relu.defvjp(lambda x: (sq_relu(x), x), lambda x, g: (g * (2.0 * jnp.maximum(x, 0.0)),))


def mlp_chunk(wup, wdown, taps, h):
    xs = []
    return tmm(sq_relu(tmm(h, wup, taps, xs)), wdown, taps, xs), xs


def _rot_half(t):
    n = t.shape[-1]
    lane = lax.broadcasted_iota(jnp.int32, t.shape, t.ndim - 1)
    lo = (lane % HEAD_DIM) < (HEAD_DIM // 2)
    return jnp.where(lo, -pltpu.roll(t, n - HEAD_DIM // 2, t.ndim - 1), pltpu.roll(t, HEAD_DIM // 2, t.ndim - 1))


@jax.custom_vjp
def rot_half(t):
    return _rot_half(t)


rot_half.defvjp(lambda t: (_rot_half(t), None), lambda _, g: (-_rot_half(g),))


def _tile_lanes(t, width):
    return jnp.concatenate([t] * (width // t.shape[-1]), axis=-1)


def qkv_proj(cos, sin, wq, wk, wv, taps, h):
    xs = []
    q = tmm(h, wq, taps, xs)
    k = tmm(h, wk, taps, xs)
    v = tmm(h, wv, taps, xs)
    cq, sq = _tile_lanes(cos, D_MODEL), _tile_lanes(sin, D_MODEL)
    ck, sk = _tile_lanes(cos, KV_DIM), _tile_lanes(sin, KV_DIM)
    return (q * cq + rot_half(q) * sq, k * ck + rot_half(k) * sk, v), xs


def attn_out(w_o, taps, o, h, lg, lb):
    xs = []
    return _layer_norm(ALPHA * h + tmm(o, w_o, taps, xs), lg, lb), xs


def _scan_consts():
    t = SCAN_T
    tri = np.tril(np.ones((t, t), np.float32))
    rows = np.arange(2 * t)
    same = (rows[:, None] // t) == (rows[None, :] // t)
    strict = same & ((rows[None, :] % t) < (rows[:, None] % t))
    incl = same & ((rows[None, :] % t) <= (rows[:, None] % t))
    lane = np.arange(PAIR)
    masks = np.zeros((8, PAIR), np.float32)
    masks[0] = (lane // HEAD_DIM) == 0
    masks[1] = (lane // HEAD_DIM) == 1
    return (jnp.asarray(tri, BF16), jnp.asarray(strict.astype(np.float32)), jnp.asarray(incl.astype(np.float32)),
            jnp.asarray(masks), jnp.asarray(np.eye(2 * t, dtype=np.float32)))


def _scan_dot(a, b, ca, cb):
    return _dot(a, b, ca, cb)


@functools.partial(jax.custom_vjp, nondiff_argnums=(2, 3))
def _dotf(a, b, ca, cb):
    return _scan_dot(a, b, ca, cb)


def _dotf_bwd(ca, cb, res, g):
    a, b = res
    if ca == 1:
        da = _scan_dot(g, b, 1, 1 - cb)
    else:
        da = _scan_dot(b, g, 1 - cb, 1)
    if cb == 0:
        db = _scan_dot(a, g, 1 - ca, 0)
    else:
        db = _scan_dot(g, a, 0, 1 - ca)
    return da, db


_dotf.defvjp(lambda a, b, ca, cb: (_scan_dot(a, b, ca, cb), (a, b)), _dotf_bwd)


def _tri_dot(tri, x, ct):
    acc = None
    for piece in _split3(x):
        t = lax.dot_general(tri, piece, (((ct,), (0,)), ((), ())), preferred_element_type=F32)
        acc = t if acc is None else acc + t
    return acc


@jax.custom_vjp
def _cumsum_rows(tri, x):
    return _tri_dot(tri, x, 1)


_cumsum_rows.defvjp(lambda tri, x: (_tri_dot(tri, x, 1), tri),
                    lambda tri, g: (jnp.zeros_like(tri), _tri_dot(tri, g, 0)))


@jax.custom_vjp
def _unstack2(x):
    t = x.shape[0] // 2
    return x[:t] + x[t:]


_unstack2.defvjp(lambda x: (_unstack2(x), None), lambda _, g: (jnp.concatenate([g, g], axis=0),))


@jax.custom_vjp
def _last_row(x):
    return x[x.shape[0] - 1:, :]


def _last_row_bwd(_, g):
    rows = lax.broadcasted_iota(jnp.int32, (SCAN_T, g.shape[1]), 0)
    return (jnp.where(rows == SCAN_T - 1, jnp.broadcast_to(g, (SCAN_T, g.shape[1])), 0.0),)


_last_row.defvjp(lambda x: (_last_row(x), None), _last_row_bwd)


@jax.custom_vjp
def _solve_saved(n, rhs, minv, u):
    return u


def _solve_saved_bwd(res, du):
    minv, u = res
    drhs = _dotf(minv, du, 0, 0)
    return _dotf(drhs, u, 1, 1), drhs, jnp.zeros_like(minv), jnp.zeros_like(u)


_solve_saved.defvjp(lambda n, rhs, minv, u: (u, (minv, u)), _solve_saved_bwd)


def scan_chunk(tri, strict, incl, m0, m1, eye, r, lw, k, v, a, b, s0, saved=None):
    lower = strict > 0
    lower_incl = incl > 0

    def stack(x):
        return jnp.concatenate([x * m0, x * m1], axis=0)

    def dots(xs, ys, ca, cb, mask=None):
        out = [_dotf(x, y, ca, cb) for x, y in zip(xs, ys)]
        return out if mask is None else [jnp.where(mask, o, 0.0) for o in out]

    cl = [_cumsum_rows(tri, x) for x in lw]
    gam = [jnp.exp(c) for c in cl]
    ginv = [jnp.exp(-c) for c in cl]
    a_s = [stack(x * jnp.exp(c - w)) for x, c, w in zip(a, cl, lw)]
    r_s = [stack(x * g) for x, g in zip(r, gam)]
    b_s = [stack(x * g) for x, g in zip(b, ginv)]
    k_s = [stack(x * g) for x, g in zip(k, ginv)]
    v_s = [stack(x) for x in v]
    n_ab = dots(a_s, b_s, 1, 1, lower)
    n_ak = dots(a_s, k_s, 1, 1, lower)
    r_ab = dots(r_s, b_s, 1, 1, lower_incl)
    r_ak = dots(r_s, k_s, 1, 1, lower_incl)
    rhs = [x + y for x, y in zip(dots(a_s, s0, 1, 1), dots(n_ak, v_s, 1, 0))]
    if saved is None:
        minv = [eye + n for n in n_ab]
        p = n_ab
        for _ in range(5):
            p = dots(p, p, 1, 0)
            minv = [m + mp for m, mp in zip(minv, dots(minv, p, 1, 0))]
        u_s = dots(minv, rhs, 1, 0)
    else:
        minv = saved[0]
        u_s = [_solve_saved(n, x, m, u) for n, x, m, u in zip(n_ab, rhs, *saved)]
    y = [_unstack2(x0 + x1 + x2)
         for x0, x1, x2 in zip(dots(r_s, s0, 1, 1), dots(r_ab, u_s, 1, 0), dots(r_ak, v_s, 1, 0))]
    g_end = [_last_row(g) for g in gam]
    s1 = [s * g + x + z for s, g, x, z in zip(s0, g_end, dots(u_s, [x * g for x, g in zip(b_s, g_end)], 0, 0),
                                              dots(v_s, [x * g for x, g in zip(k_s, g_end)], 0, 0))]
    return y, s1, (minv, u_s)


SCAN_PAIRS = 8


def _scan_specs(consts, order):
    row = pl.BlockSpec((SCAN_T, PAIR * SCAN_PAIRS), lambda p, c: (order(c), p))
    state = pl.BlockSpec((None, SCAN_PAIRS, PAIR, PAIR), lambda p, c: (order(c), p, 0, 0))
    return row, state, [pl.BlockSpec(x.shape, _zero_map(x.ndim)) for x in consts]


def _pair_lanes(q):
    return slice(q * PAIR, (q + 1) * PAIR)


def scan_fwd(r, lw, k, v, a, b, shards=()):
    lp = r.shape[0]
    nch = lp // SCAN_T
    npair = D_MODEL // PAIR
    ng = len(shards)
    consts = _scan_consts()
    row, state, cspecs = _scan_specs(consts, lambda c: c)

    def body(tri, strict, incl, masks, eye, r_ref, lw_ref, k_ref, v_ref, a_ref, b_ref, *rest):
        src, (y_ref, s_ref, minv_ref, u_ref), dst = rest[:ng], rest[ng:ng + 4], rest[ng + 4:2 * ng + 4]
        carry = rest[2 * ng + 4]
        first = jnp.logical_and(pl.program_id(0) == 0, pl.program_id(1) == 0)
        last = jnp.logical_and(pl.program_id(0) == npair // SCAN_PAIRS - 1, pl.program_id(1) == nch - 1)
        if ng:
            sends, arrivals, forwards, forwarded = gather_copies(src, dst, *rest[2 * ng + 5:])

            @pl.when(first)
            def _():
                for cp in sends:
                    cp.start()

            @pl.when(jnp.logical_and(pl.program_id(0) == npair // SCAN_PAIRS - 1, pl.program_id(1) == nch * 3 // 4))
            def _():
                for landed, onward in zip(arrivals, forwards):
                    landed.wait_recv()
                    onward.start()

        @pl.when(pl.program_id(1) == 0)
        def _():
            carry[...] = jnp.zeros_like(carry)

        pairs = range(SCAN_PAIRS)
        s0 = [carry[q] for q in pairs]
        rows = [[ref[:, _pair_lanes(q)] for q in pairs] for ref in (r_ref, lw_ref, k_ref, v_ref, a_ref, b_ref)]
        y, s1, (minv, u) = scan_chunk(tri[...], strict[...], incl[...], masks[0:1, :], masks[1:2, :], eye[...],
                                      *rows, s0)
        for q in pairs:
            s_ref[q] = s0[q]
            minv_ref[q] = minv[q]
            u_ref[q] = u[q]
            y_ref[:, _pair_lanes(q)] = y[q]
            carry[q] = s1[q]

        if ng:
            @pl.when(last)
            def _():
                for cp in forwarded:
                    cp.wait_recv()
                for cp in sends + forwards:
                    cp.wait_send()

    mats = jax.ShapeDtypeStruct((nch, npair, PAIR, PAIR), F32)
    out = pl.pallas_call(
        body, name="rwkv_scan_fwd", grid=(npair // SCAN_PAIRS, nch), in_specs=cspecs + [row] * 6 + [ANY] * ng,
        out_specs=[row, state, state, state] + [ANY] * ng,
        out_shape=[jax.ShapeDtypeStruct((lp, D_MODEL), F32), mats, mats, mats] + gathered_shapes(shards),
        scratch_shapes=[pltpu.VMEM((SCAN_PAIRS, PAIR, PAIR), F32)] + (gather_scratch(ng) if ng else []),
        compiler_params=_params(),
    )(*consts, r, lw, k, v, a, b, *shards)
    return out[:4], fill_own(out[4:], shards)


def scan_bwd(r, lw, k, v, a, b, saved, dy, direct_grads, parts=()):
    lp = r.shape[0]
    nch = lp // SCAN_T
    npair = D_MODEL // PAIR
    consts = _scan_consts()
    row, state, cspecs = _scan_specs(consts, lambda c: nch - 1 - c)

    ng = len(parts)

    def body(tri, strict, incl, masks, eye, r_ref, lw_ref, k_ref, v_ref, a_ref, b_ref, s_ref, minv_ref, u_ref,
             dy_ref, dr_in, dk_in, dv_in, *rest):
        src, (dr_ref, dlw_ref, dk_ref, dv_ref, da_ref, db_ref), dst = rest[:ng], rest[ng:ng + 6], rest[ng + 6:2 * ng + 6]
        carry = rest[2 * ng + 6]
        first = jnp.logical_and(pl.program_id(0) == 0, pl.program_id(1) == 0)
        last = jnp.logical_and(pl.program_id(0) == npair // SCAN_PAIRS - 1, pl.program_id(1) == nch - 1)
        if ng:
            sends, arrivals = chip_exchange_copies(src, dst, *rest[2 * ng + 7:])

            @pl.when(first)
            def _():
                for cp in sends:
                    cp.start()

        @pl.when(pl.program_id(1) == 0)
        def _():
            carry[...] = jnp.zeros_like(carry)

        pairs = range(SCAN_PAIRS)
        kept = ([minv_ref[q] for q in pairs], [u_ref[q] for q in pairs])

        def fn(*args):
            y, s1, _ = scan_chunk(tri[...], strict[...], incl[...], masks[0:1, :], masks[1:2, :], eye[...], *args,
                                  saved=kept)
            return y, s1

        rows = [[ref[:, _pair_lanes(q)] for q in pairs] for ref in (r_ref, lw_ref, k_ref, v_ref, a_ref, b_ref)]
        _, vjp = jax.vjp(fn, *rows, [s_ref[q] for q in pairs])
        grads = vjp(([dy_ref[:, _pair_lanes(q)] for q in pairs], [carry[q] for q in pairs]))
        direct = (dr_in, None, dk_in, dv_in, None, None)
        for q in pairs:
            ln = _pair_lanes(q)
            for ref, g, extra in zip((dr_ref, dlw_ref, dk_ref, dv_ref, da_ref, db_ref), grads[:6], direct):
                ref[:, ln] = g[q] if extra is None else g[q] + extra[:, ln]
            carry[q] = grads[6][q]

        if ng:
            @pl.when(last)
            def _():
                for cp in arrivals:
                    cp.wait_recv()
                for cp in sends:
                    cp.wait_send()

    out = pl.pallas_call(
        body, name="rwkv_scan_bwd", grid=(npair // SCAN_PAIRS, nch),
        in_specs=cspecs + [row] * 6 + [state] * 3 + [row] * 4 + [ANY] * ng, out_specs=[row] * 6 + [ANY] * ng,
        out_shape=[jax.ShapeDtypeStruct((lp, D_MODEL), F32)] * 6 + [jax.ShapeDtypeStruct(p.shape, p.dtype) for p in parts],
        scratch_shapes=[pltpu.VMEM((SCAN_PAIRS, PAIR, PAIR), F32)] + (_sem_scratch(ng * len(XY_FLIPS)) if ng else []),
        compiler_params=_params(),
    )(*consts, r, lw, k, v, a, b, *saved, dy, *direct_grads, *parts)
    return out[:6], out[6:]


def _spread_matrices():
    rep = np.zeros((N_HEADS_KV, KV_DIM, KVW), np.float32)
    for h in range(N_HEADS_KV):
        for g in range(GROUP):
            rep[h, h * HEAD_DIM + np.arange(HEAD_DIM), g * HEAD_DIM + np.arange(HEAD_DIM)] = 1.0
    return jnp.asarray(rep, BF16)


KV_HEADS = range(N_HEADS_KV)


def _attn_common(n, q_ref, kp, kc, vp, vc, rep_ref, sink_ref):
    lane = lax.broadcasted_iota(jnp.int32, (1, KVW), 1)
    gmask = [(lane // HEAD_DIM == g).astype(F32) for g in range(GROUP)]
    kk = jnp.concatenate([kp, kc], axis=0)
    vv = jnp.concatenate([vp, vc], axis=0)
    qs = [q_ref[:, h * KVW:(h + 1) * KVW] for h in KV_HEADS]
    q_s = [jnp.concatenate([q * gmask[g] for g in range(GROUP)], axis=0) for q in qs]
    keys = [_dot(kk, rep_ref[h], 1, 0) for h in KV_HEADS]
    vals = [_dot(vv, rep_ref[h], 1, 0) for h in KV_HEADS]
    qi = lax.broadcasted_iota(jnp.int32, (GROUP * BLOCK, 2 * BLOCK), 0) % BLOCK
    kj = lax.broadcasted_iota(jnp.int32, (GROUP * BLOCK, 2 * BLOCK), 1)
    rel = BLOCK + qi - kj
    valid = (rel >= 0) & (rel < BLOCK) & ((n - 1) * BLOCK + kj >= PAD_FRONT)
    s = [jnp.where(valid, _dot(x, y, 1, 1) * (HEAD_DIM ** -0.5), -1e30) for x, y in zip(q_s, keys)]
    sink_col = [jnp.concatenate([jnp.broadcast_to(sink_ref[h, g:g + 1, 0:1], (BLOCK, 1)) for g in range(GROUP)],
                                axis=0) for h in KV_HEADS]
    m = [jnp.maximum(jnp.max(x, axis=-1, keepdims=True), c) for x, c in zip(s, sink_col)]
    ex = [jnp.exp(x - y) for x, y in zip(s, m)]
    ex_sink = [jnp.exp(c - y) for c, y in zip(sink_col, m)]
    inv = [1.0 / (jnp.sum(x, axis=-1, keepdims=True) + c) for x, c in zip(ex, ex_sink)]
    return (gmask, q_s, keys, vals, [x * y for x, y in zip(ex, inv)], [x * y for x, y in zip(ex_sink, inv)])


def _unstack_groups(x_s, gmask):
    out = None
    for g in range(GROUP):
        t = x_s[g * BLOCK:(g + 1) * BLOCK] * gmask[g]
        out = t if out is None else out + t
    return out


def _attn_specs():
    qspec = pl.BlockSpec((BLOCK, D_MODEL), lambda n: (n, 0))
    cur = pl.BlockSpec((BLOCK, KV_DIM), lambda n: (n, 0))
    prev = pl.BlockSpec((BLOCK, KV_DIM), lambda n: (jnp.maximum(n - 1, 0), 0))
    rep = pl.BlockSpec((N_HEADS_KV, KV_DIM, KVW), lambda n: (0, 0, 0))
    sink = pl.BlockSpec((N_HEADS_KV, 8, PAIR), lambda n: (0, 0, 0))
    return qspec, cur, prev, rep, sink


def _attn_params():
    return pltpu.CompilerParams(dimension_semantics=("arbitrary",), vmem_limit_bytes=VMEM_LIMIT)


def attn_fwd(q, k, v, sinks_b):
    lp = q.shape[0]
    qspec, cur, prev, rep, sink = _attn_specs()

    def body(q_ref, kp_ref, kc_ref, vp_ref, vc_ref, rep_ref, sink_ref, o_ref):
        gmask, _, _, vals, p, _ = _attn_common(pl.program_id(0), q_ref, kp_ref[...], kc_ref[...], vp_ref[...],
                                               vc_ref[...], rep_ref, sink_ref)
        o = [_dot(x, y, 1, 0) for x, y in zip(p, vals)]
        for h in KV_HEADS:
            o_ref[:, h * KVW:(h + 1) * KVW] = _unstack_groups(o[h], gmask)

    return pl.pallas_call(
        body, name="swa_fwd", grid=(lp // BLOCK,), in_specs=[qspec, prev, cur, prev, cur, rep, sink],
        out_specs=qspec, out_shape=jax.ShapeDtypeStruct((lp, D_MODEL), F32), compiler_params=_attn_params(),
    )(q, k, k, v, v, _spread_matrices(), sinks_b)


def attn_bwd(q, k, v, sinks_b, do):
    lp = q.shape[0]
    qspec, cur, prev, rep, sink = _attn_specs()

    def body(q_ref, kp_ref, kc_ref, vp_ref, vc_ref, rep_ref, sink_ref, do_ref, dq_ref, dkc_ref, dkp_ref, dvc_ref,
             dvp_ref, dsink_ref):
        n = pl.program_id(0)
        gmask, q_s, keys, vals, p, p_sink = _attn_common(n, q_ref, kp_ref[...], kc_ref[...], vp_ref[...], vc_ref[...],
                                                         rep_ref, sink_ref)
        do_s = [jnp.concatenate([do_ref[:, h * KVW:(h + 1) * KVW] * gmask[g] for g in range(GROUP)], axis=0)
                for h in KV_HEADS]
        dp = [_dot(x, y, 1, 1) for x, y in zip(do_s, vals)]
        delta = [jnp.sum(x * y, axis=-1, keepdims=True) for x, y in zip(p, dp)]
        ds = [x * (y - z) * (HEAD_DIM ** -0.5) for x, y, z in zip(p, dp, delta)]
        dq = [_dot(x, y, 1, 0) for x, y in zip(ds, keys)]
        dkeys_s = [_dot(x, y, 0, 0) for x, y in zip(ds, q_s)]
        dvals_s = [_dot(x, y, 0, 0) for x, y in zip(p, do_s)]
        dkeys = [_exact_dot(x, rep_ref[h], cb=1) for h, x in enumerate(dkeys_s)]
        dvals = [_exact_dot(x, rep_ref[h], cb=1) for h, x in enumerate(dvals_s)]
        dk_all = (dkeys[0] + dkeys[1]) + (dkeys[2] + dkeys[3])
        dv_all = (dvals[0] + dvals[1]) + (dvals[2] + dvals[3])
        dkp_ref[...] = dk_all[:BLOCK]
        dkc_ref[...] = dk_all[BLOCK:]
        dvp_ref[...] = dv_all[:BLOCK]
        dvc_ref[...] = dv_all[BLOCK:]
        dsinks = []
        for h in KV_HEADS:
            dq_ref[:, h * KVW:(h + 1) * KVW] = _unstack_groups(dq[h], gmask)
            dsk = -(p_sink[h] * delta[h])
            rows = [jnp.broadcast_to(jnp.sum(dsk[g * BLOCK:(g + 1) * BLOCK], axis=0, keepdims=True), (1, PAIR))
                    for g in range(GROUP)]
            dsinks.append(jnp.concatenate(rows + [jnp.zeros((8 - GROUP, PAIR), F32)], axis=0))

        @pl.when(n == 0)
        def _():
            for h in KV_HEADS:
                dsink_ref[h] = dsinks[h]

        @pl.when(n > 0)
        def _():
            for h in KV_HEADS:
                dsink_ref[h] += dsinks[h]

    kv = jax.ShapeDtypeStruct((lp, KV_DIM), F32)
    return pl.pallas_call(
        body, name="swa_bwd", grid=(lp // BLOCK,), in_specs=[qspec, prev, cur, prev, cur, rep, sink, qspec],
        out_specs=[qspec, cur, cur, cur, cur, sink],
        out_shape=[jax.ShapeDtypeStruct((lp, D_MODEL), F32), kv, kv, kv, kv,
                   jax.ShapeDtypeStruct((N_HEADS_KV, 8, PAIR), F32)],
        compiler_params=_attn_params(),
    )(q, k, k, v, v, _spread_matrices(), sinks_b, do)


def _pick_tm(lp, want):
    for tm in (384, 192, 128, 64):
        if tm <= want and lp % tm == 0:
            return tm
    raise ValueError(lp)


def _acc(shape):
    return (tuple(shape), F32)


def _ff_one(w):
    return (w, (None, D_MODEL, D_MODEL), lambda c, i: (c, 0, 0))


def _mlp_layer_fwd(name, h, wup, wdown, lg, lb, tm):
    def fn(c, i, h, wup, wdown, lg, lb):
        out = None
        for s in range(N_FF_CHUNK):
            t = mlp_chunk(wup[s], wdown[s], None, h)[0]
            out = t if out is None else out + t
        z = ALPHA * h + out
        return (_layer_norm(z, lg, lb), z), ()

    (h_out, z), _ = rowwise(name, fn, [h], [wup, wdown, lg, lb], [(D_MODEL, F32), (D_MODEL, F32)], [], tm)
    return h_out, z


def _mlp_layer_bwd(name, h_in, z, dh_parts, wup, wdown, lg, lb, tm):
    n_parts = len(dh_parts)

    def fn_ln(c, i, z, *rest):
        dh = rest[0]
        for extra in rest[1:n_parts]:
            dh = dh + extra
        _, vjp = jax.vjp(_layer_norm, z, rest[n_parts], rest[n_parts + 1])
        dz, dlg, dlb = vjp(dh)
        return (dz,), (dlg, dlb)

    (dz,), (dlg, dlb) = rowwise(name + "_ln", fn_ln, [z] + list(dh_parts), [lg, lb], [(D_MODEL, F32)],
                                [_acc((1, D_MODEL)), _acc((1, D_MODEL))], tm)

    def fn_mlp(c, i, h, dz, wup, wdown):
        tile = h.shape[0]
        (dx,), dws = vjp_taps(functools.partial(mlp_chunk, wup, wdown), [(tile, D_MODEL)] * 2, [h], dz)
        return (dx,), dws

    aspec = ((N_FF_CHUNK, D_MODEL, D_MODEL), F32, (None, D_MODEL, D_MODEL), lambda c, i: (c, 0, 0))
    (dx,), (dwup, dwdown) = rowwise(name + "_mm", fn_mlp, [h_in, dz], [_ff_one(wup), _ff_one(wdown)],
                                    [(D_MODEL, F32, True)], [aspec, aspec], tm, nc=N_FF_CHUNK)
    return dz, dx, dwup, dwdown, dlg, dlb


def _sum_parts(dz, dx):
    out = ALPHA * dz
    for s in range(N_FF_CHUNK):
        out = out + dx[s]
    return out


def local_step(x, loss_target, p, late=None, early_hook=None):
    seq = x.shape[0]
    lp = TOK0 + seq
    tm = _pick_tm(lp, 384)
    tms = _pick_tm(lp, 192)
    e, et = _head_matrices()
    h0 = jnp.concatenate([jnp.zeros((PAD_FRONT, D_MODEL), F32), p["meta_tokens"], x], axis=0)
    hp = jnp.concatenate([jnp.zeros((1, D_MODEL), F32), h0[:-1]], axis=0)
    tgt = jnp.concatenate([jnp.zeros((TOK0, D_MODEL), F32), loss_target], axis=0)
    pos = jnp.maximum(jnp.arange(lp, dtype=F32) - PAD_FRONT, 0.0)
    inv_freq = 1.0 / (ROPE_THETA ** (jnp.arange(0, HEAD_DIM, 2, dtype=F32) / HEAD_DIM))
    ang = pos[:, None] * inv_freq[None, :]
    cos = jnp.tile(jnp.cos(ang), (1, PAIR // (HEAD_DIM // 2)))
    sin = jnp.tile(jnp.sin(ang), (1, PAIR // (HEAD_DIM // 2)))

    pre_vec = [p["a_mu"][j:j + 1] for j in range(6)] + [p["a_w0"], p["a_a0"], p["a_k_k"], p["a_k_a"]]
    pre_w = [p["a_w_r"], p["a_w_k"], p["a_w_v"], p["a_w1"], p["a_w2"], p["a_a1"], p["a_a2"], p["a_g1"], p["a_g2"]]
    n_vec = len(pre_vec)

    def fn_pre(c, i, h, hp, e, et, *ws):
        return rwkv_pre(e, et, ws[n_vec:], None, h, hp, *ws[:n_vec])[0], ()

    (r, lw, k2, v, an, bn, g), _, *pre_gathered = rowwise(
        "rwkv_pre", fn_pre, [h0, hp], [e, et] + pre_vec + pre_w, [(D_MODEL, F32)] * 7, [], tms,
        gather=late[0][0] if late else ())
    (y, *scan_saved), scan_gathered = scan_fwd(r, lw, k2, v, an, bn, late[1][0] if late else ())
    if late:
        p = {**p, **late[0][1](pre_gathered[0]), **late[1][1](scan_gathered)}

    post_c = [p["a_w_o"], p["a_gn_w"], p["a_gn_b"], p["a_r_k"], p["ln_g00"], p["ln_b00"]]

    def fn_post(c, i, y, r, k2, v, g, h0, e, et, w_o, *vecs):
        return (rwkv_post(e, et, w_o, None, y, r, k2, v, g, h0, *vecs)[0],), ()

    (h1,), _ = rowwise("rwkv_post", fn_post, [y, r, k2, v, g, h0], [e, et] + post_c, [(D_MODEL, F32)], [], tm)
    h2, z2 = _mlp_layer_fwd("mlp0_fwd", h1, p["mlp_up0"], p["mlp_down0"], p["ln_g01"], p["ln_b01"], tm)

    qkv_w = [p["b_w_q"], p["kv_w_k"], p["kv_w_v"]]

    def fn_qkv(c, i, h, cos, sin, wq, wk, wv):
        return qkv_proj(cos, sin, wq, wk, wv, None, h)[0], ()

    (q, k, vv), _ = rowwise("qkv_proj", fn_qkv, [h2, cos, sin], qkv_w,
                            [(D_MODEL, F32), (KV_DIM, F32), (KV_DIM, F32)], [], tm)
    sinks_b = jnp.broadcast_to(p["b_sinks"].reshape(N_HEADS_KV, GROUP, 1), (N_HEADS_KV, GROUP, PAIR))
    sinks_b = jnp.concatenate([sinks_b, jnp.zeros((N_HEADS_KV, 8 - GROUP, PAIR), F32)], axis=1)
    o = attn_fwd(q, k, vv, sinks_b)

    ao_c = [p["b_w_o"], p["ln_g10"], p["ln_b10"]]

    def fn_ao(c, i, o, h, w_o, lg, lb):
        return (attn_out(w_o, None, o, h, lg, lb)[0],), ()

    (h3,), _ = rowwise("attn_out", fn_ao, [o, h2], ao_c, [(D_MODEL, F32)], [], tm)
    h4, z4 = _mlp_layer_fwd("mlp1_fwd", h3, p["mlp_up1"], p["mlp_down1"], p["ln_g11"], p["ln_b11"], tm)

    def fn_loss(c, i, h4, tgt):
        real = (_row_ids(i, tm) >= TOK0).astype(F32)
        err = (h4 - tgt) * real
        part = 0.5 * jnp.sum(jnp.sum(err * err, axis=-1, keepdims=True), axis=0, keepdims=True) / D_MODEL
        return (err * (1.0 / D_MODEL),), (jnp.broadcast_to(part, (8, PAIR)),)

    (dh4,), (loss_acc,) = rowwise("loss", fn_loss, [h4, tgt], [], [(D_MODEL, F32)], [_acc((8, PAIR))], tm)
    loss = loss_acc[0, 0]

    grads = {}
    dz4, dx4, grads["mlp_up1"], grads["mlp_down1"], grads["ln_g11"], grads["ln_b11"] = _mlp_layer_bwd(
        "mlp1_bwd", h3, z4, [dh4], p["mlp_up1"], p["mlp_down1"], p["ln_g11"], p["ln_b11"], tm)

    def fn_ao_b(c, i, dz, dx, o, h, w_o, lg, lb):
        (do, dh, dlg, dlb), (dw_o,) = vjp_taps(functools.partial(attn_out, w_o), [(tm, D_MODEL)], [o, h, lg, lb],
                                               _sum_parts(dz, dx))
        return (do, dh), (dw_o, dlg, dlb)

    (do, dh2_a), (grads["b_w_o"], grads["ln_g10"], grads["ln_b10"]) = rowwise(
        "attn_out_bwd", fn_ao_b, [dz4, dx4, o, h2], ao_c, [(D_MODEL, F32)] * 2,
        [_acc((D_MODEL, D_MODEL)), _acc((1, D_MODEL)), _acc((1, D_MODEL))], tm)

    dq, dkc, dkp, dvc, dvp, dsinks = attn_bwd(q, k, vv, sinks_b, do)
    grads["b_sinks"] = dsinks[:, :GROUP, 0].reshape(1, N_HEADS)
    zblk = jnp.zeros((BLOCK, KV_DIM), F32)
    dkp_s = jnp.concatenate([dkp[BLOCK:], zblk], axis=0)
    dvp_s = jnp.concatenate([dvp[BLOCK:], zblk], axis=0)

    def fn_qkv_b(c, i, h, cos, sin, dq, dkc, dkp, dvc, dvp, wq, wk, wv):
        return vjp_taps(functools.partial(qkv_proj, cos, sin, wq, wk, wv),
                        [(tm, D_MODEL), (tm, KV_DIM), (tm, KV_DIM)], [h], (dq, dkc + dkp, dvc + dvp))

    (dh2_q,), (grads["b_w_q"], grads["kv_w_k"], grads["kv_w_v"]) = rowwise(
        "qkv_proj_bwd", fn_qkv_b, [h2, cos, sin, dq, dkc, dkp_s, dvc, dvp_s], qkv_w, [(D_MODEL, F32)],
        [_acc((D_MODEL, D_MODEL)), _acc((D_MODEL, KV_DIM)), _acc((D_MODEL, KV_DIM))], tm)

    dz2, dx2, grads["mlp_up0"], grads["mlp_down0"], grads["ln_g01"], grads["ln_b01"] = _mlp_layer_bwd(
        "mlp0_bwd", h1, z2, [dh2_a, dh2_q], p["mlp_up0"], p["mlp_down0"], p["ln_g01"], p["ln_b01"], tm)

    def fn_post_b(c, i, dz, dx, y, r, k2, v, g, h0, e, et, w_o, *vecs):
        out, dws = vjp_taps(functools.partial(rwkv_post, e, et, w_o), [(tms, D_MODEL)],
                            [y, r, k2, v, g, h0] + list(vecs), _sum_parts(dz, dx))
        return out[:6], tuple(dws) + tuple(out[6:])

    (dy, dr_c, dk_c, dv_c, dg, dh0_c), post_g = rowwise(
        "rwkv_post_bwd", fn_post_b, [dz2, dx2, y, r, k2, v, g, h0], [e, et] + post_c, [(D_MODEL, F32)] * 6,
        [_acc((D_MODEL, D_MODEL))] + [_acc((1, D_MODEL))] * 5, tms)
    for name, val in zip(["a_w_o", "a_gn_w", "a_gn_b", "a_r_k", "ln_g00", "ln_b00"], post_g):
        grads[name] = val

    (dr, dlw, dk2, dv, dan, dbn), early_from_chips = scan_bwd(r, lw, k2, v, an, bn, scan_saved, dy,
                                                              (dr_c, dk_c, dv_c),
                                                              early_hook(grads) if early_hook else ())

    def fn_pre_b(c, i, h, hp, dr, dlw, dk2, dv, dan, dbn, dg, e, et, *ws):
        real = (_row_ids(i, tms) >= PAD_FRONT).astype(F32)
        cot = tuple(t * real for t in (dr, dlw, dk2, dv, dan, dbn, dg))
        out, dws = vjp_taps(functools.partial(rwkv_pre, e, et, ws[n_vec:]), [(tms, n) for n in PRE_TAPS],
                            [h, hp] + list(ws[:n_vec]), cot)
        return out[:2], tuple(out[2:]) + tuple(dws)

    (dh0_p, dhp), pre_g = rowwise(
        "rwkv_pre_bwd", fn_pre_b, [h0, hp, dr, dlw, dk2, dv, dan, dbn, dg],
        [e, et] + pre_vec + pre_w, [(D_MODEL, F32)] * 2,
        [_acc((1, D_MODEL))] * n_vec + [_acc(w.shape) for w in pre_w], tms)
    grads["a_mu"] = jnp.concatenate(pre_g[:6], axis=0)
    for name, val in zip(["a_w0", "a_a0", "a_k_k", "a_k_a", "a_w_r", "a_w_k", "a_w_v", "a_w1", "a_w2", "a_a1",
                          "a_a2", "a_g1", "a_g2"], pre_g[6:]):
        grads[name] = val

    dhp_s = jnp.concatenate([dhp[1:], jnp.zeros((1, D_MODEL), F32)], axis=0)

    def fn_add(c, i, a, b, d):
        return (a + b + d,), ()

    (dh0,), _ = rowwise("grad_h0", fn_add, [dh0_c, dh0_p, dhp_s], [], [(D_MODEL, F32)], [], tm)
    grads["meta_tokens"] = dh0[PAD_FRONT:TOK0]
    return loss, dh0[TOK0:], grads, early_from_chips


ANY = pl.BlockSpec(memory_space=pl.ANY)
XY_FLIPS = ((0, 1), (1, 0), (1, 1))
ALL_FLIPS = tuple((e >> 2 & 1, e >> 1 & 1, e & 1) for e in range(1, N_DEV))


def _flip(v, bit):
    return 1 - v if bit else v


def _sem_scratch(n):
    return [pltpu.SemaphoreType.DMA((n,)), pltpu.SemaphoreType.DMA((n,))]


def gather_copies(src, dst, ici_send, ici_recv, d2d_send, d2d_recv):
    npeer = len(XY_FLIPS)
    x, y, c = lax.axis_index("x"), lax.axis_index("y"), lax.axis_index("c")

    def half(ref, k, which):
        h = src[k].shape[0] // 2
        start = which * h
        return ref.at[pl.ds(pl.multiple_of(start, 8) if h % 8 == 0 else start, h)]

    def ici(k, j, slot):
        fx, fy = XY_FLIPS[j]
        return pltpu.make_async_remote_copy(
            src_ref=half(src[k], k, c), dst_ref=half(dst[k].at[slot], k, c), send_sem=ici_send.at[k * npeer + j],
            recv_sem=ici_recv.at[k * npeer + j], device_id=(_flip(x, fx), _flip(y, fy), c), device_id_type=MESH)

    def d2d(k, j, which):
        fx, fy = XY_FLIPS[j]
        landed = half(dst[k].at[2 * _flip(x, fx) + _flip(y, fy)], k, which)
        return pltpu.make_async_remote_copy(
            src_ref=landed, dst_ref=landed, send_sem=d2d_send.at[k * npeer + j], recv_sem=d2d_recv.at[k * npeer + j],
            device_id=(x, y, 1 - c), device_id_type=MESH)

    pairs = [(k, j) for k in range(len(src)) for j in range(npeer)]
    return ([ici(k, j, 2 * x + y) for k, j in pairs],
            [ici(k, j, 2 * _flip(x, XY_FLIPS[j][0]) + _flip(y, XY_FLIPS[j][1])) for k, j in pairs],
            [d2d(k, j, c) for k, j in pairs], [d2d(k, j, 1 - c) for k, j in pairs])


def gather_scratch(n):
    return _sem_scratch(n * len(XY_FLIPS)) * 2


def gathered_shapes(shards):
    return [jax.ShapeDtypeStruct((N_SHARD,) + s.shape, s.dtype) for s in shards]


def fill_own(gathered, shards):
    if not shards:
        return []
    slot = 2 * lax.axis_index("x") + lax.axis_index("y")
    return [lax.dynamic_update_index_in_dim(g, s, slot, 0) for g, s in zip(gathered, shards)]


def all_gather_shards(shards):
    n = len(shards)

    def body(*refs):
        sends, arrivals, forwards, forwarded = gather_copies(refs[:n], refs[n:2 * n], *refs[2 * n:])
        for cp in sends:
            cp.start()
        for landed, onward in zip(arrivals, forwards):
            landed.wait_recv()
            onward.start()
        for cp in forwarded:
            cp.wait_recv()
        for cp in sends + forwards:
            cp.wait_send()

    out = pl.pallas_call(body, name="gather_weights", in_specs=[ANY] * n, out_specs=[ANY] * n,
                         out_shape=gathered_shapes(shards), scratch_shapes=gather_scratch(n))(*shards)
    return fill_own(out, shards)


def placement():
    x, y, c = lax.axis_index("x"), lax.axis_index("y"), lax.axis_index("c")
    me = 2 * x + y
    others = [j + (j >= me).astype(jnp.int32) for j in range(N_SHARD - 1)]
    return jnp.stack([c, me] + others).astype(jnp.int32)


def pair_exchange(name, sources):
    n = len(sources)

    def body(*refs):
        src, got = refs[:n], refs[n:2 * n]
        send_sems, recv_sems = refs[2 * n:]
        x, y, c = lax.axis_index("x"), lax.axis_index("y"), lax.axis_index("c")

        def copy(k):
            half = sources[k].shape[1] // 2
            theirs = src[k].at[:, pl.ds(pl.multiple_of((1 - c) * half, 8), half), :]
            return pltpu.make_async_remote_copy(
                src_ref=theirs, dst_ref=got[k], send_sem=send_sems.at[k], recv_sem=recv_sems.at[k],
                device_id=(x, y, 1 - c), device_id_type=MESH)

        sends = [copy(k) for k in range(n)]
        for cp in sends:
            cp.start()
        for cp in sends:
            cp.wait_recv()
        for cp in sends:
            cp.wait_send()

    halves = [jax.ShapeDtypeStruct((s.shape[0], s.shape[1] // 2, s.shape[2]), s.dtype) for s in sources]
    return pl.pallas_call(body, name=name, in_specs=[ANY] * n, out_specs=[ANY] * n,
                          out_shape=halves, scratch_shapes=_sem_scratch(n))(*sources)


def chip_exchange(parts):
    n = len(parts)

    def body(*refs):
        sends, arrivals = chip_exchange_copies(refs[:n], refs[n:2 * n], *refs[2 * n:])
        for cp in sends:
            cp.start()
        for cp in arrivals:
            cp.wait_recv()
        for cp in sends:
            cp.wait_send()

    return pl.pallas_call(
        body, name="grads_chip_exchange", in_specs=[ANY] * n, out_specs=[ANY] * n,
        out_shape=[jax.ShapeDtypeStruct(p.shape, p.dtype) for p in parts],
        scratch_shapes=_sem_scratch(n * len(XY_FLIPS)),
    )(*parts)


def chip_exchange_copies(src, dst, send_sems, recv_sems):
    npeer = len(XY_FLIPS)
    x, y, c = lax.axis_index("x"), lax.axis_index("y"), lax.axis_index("c")
    me = 2 * x + y

    def copy(k, j, sending):
        fx, fy = XY_FLIPS[j]
        px, py = _flip(x, fx), _flip(y, fy)
        peer = 2 * px + py
        return pltpu.make_async_remote_copy(
            src_ref=src[k].at[peer], dst_ref=dst[k].at[me if sending else peer],
            send_sem=send_sems.at[k * npeer + j], recv_sem=recv_sems.at[k * npeer + j],
            device_id=(px, py, c), device_id_type=MESH)

    pairs = [(k, j) for k in range(len(src)) for j in range(npeer)]
    return [copy(k, j, True) for k, j in pairs], [copy(k, j, False) for k, j in pairs]


def sibling_share(halves):
    n = len(halves)

    def body(*refs):
        src, got = refs[:n], refs[n:2 * n]
        send_sems, recv_sems = refs[2 * n:]
        x, y, c = lax.axis_index("x"), lax.axis_index("y"), lax.axis_index("c")
        sends = [pltpu.make_async_remote_copy(
            src_ref=src[k], dst_ref=got[k], send_sem=send_sems.at[k], recv_sem=recv_sems.at[k],
            device_id=(x, y, 1 - c), device_id_type=MESH) for k in range(n)]
        for cp in sends:
            cp.start()
        for cp in sends:
            cp.wait_recv()
        for cp in sends:
            cp.wait_send()

    return pl.pallas_call(
        body, name="grads_sibling_share", in_specs=[ANY] * n, out_specs=[ANY] * n,
        out_shape=[jax.ShapeDtypeStruct(h.shape, h.dtype) for h in halves], scratch_shapes=_sem_scratch(n),
    )(*halves)


ADD_TILE_ELEMS = 512 * 1024


def _row_tile(rows, cols):
    return max(t for t in range(8, rows + 1, 8) if rows % t == 0 and t * cols <= ADD_TILE_ELEMS)


def _prefetch_call(body, name, place, grid, in_specs, out_specs, out_shape, args):
    return pl.pallas_call(
        body, name=name, out_shape=out_shape,
        grid_spec=pltpu.PrefetchScalarGridSpec(num_scalar_prefetch=1, grid=grid, in_specs=in_specs,
                                               out_specs=out_specs),
        compiler_params=pltpu.CompilerParams(dimension_semantics=("arbitrary",) * len(grid),
                                             vmem_limit_bytes=VMEM_LIMIT),
    )(place, *args)


def pair_add(name, place, src, got, dtype):
    n4, half, cols = got.shape
    tile = _row_tile(half, cols)
    nt = half // tile

    def body(pr, a_ref, b_ref, o_ref):
        o_ref[...] = (a_ref[...] + b_ref[...]).astype(o_ref.dtype)

    mine = pl.BlockSpec((None, tile, cols), lambda s, i, pr: (s, pr[0] * nt + i, 0))
    blk = pl.BlockSpec((None, tile, cols), lambda s, i, pr: (s, i, 0))
    return _prefetch_call(body, name, place, (n4, nt), [mine, blk], blk,
                          jax.ShapeDtypeStruct(got.shape, dtype), (src, got))


def chip_add(name, place, part, from_chips):
    _, half, cols = part.shape
    tile = _row_tile(half, cols)

    def body(pr, own_ref, r0_ref, r1_ref, r2_ref, o_ref):
        me = pr[1]
        own, r0, r1, r2 = (r[...].astype(F32) for r in (own_ref, r0_ref, r1_ref, r2_ref))
        t0 = jnp.where(me == 0, own, r0)
        t1 = jnp.where(me == 0, r0, jnp.where(me == 1, own, r1))
        t2 = jnp.where(me <= 1, r1, jnp.where(me == 2, own, r2))
        t3 = jnp.where(me == 3, own, r2)
        o_ref[...] = ((t0 + t1) + t2) + t3

    def slab(j):
        return pl.BlockSpec((None, tile, cols), lambda i, pr: (pr[j], i, 0))

    return _prefetch_call(body, name, place, (half // tile,), [slab(1), slab(2), slab(3), slab(4)],
                          pl.BlockSpec((tile, cols), lambda i, pr: (i, 0)),
                          jax.ShapeDtypeStruct((half, cols), F32), (part, from_chips, from_chips, from_chips))


def pair_sums(tag, place, sources, narrow):
    got = pair_exchange("grads_pair_exchange_" + tag, sources)
    return [pair_add(f"grads_pair_add_{tag}{k}", place, s, g, BF16 if nar else F32)
            for k, (s, g, nar) in enumerate(zip(sources, got, narrow))]


def finish_sums(place, parts, from_chips):
    halves = [chip_add(f"grads_chip_add{k}", place, p, f) for k, (p, f) in enumerate(zip(parts, from_chips))]
    return list(zip(halves, sibling_share(halves)))


ADAM_ROWS = 256


def adamw_update(name, place, halves, w, m, v):
    nsub, rows, cols = w.shape
    half = rows // 2
    tr = ADAM_ROWS if half % ADAM_ROWS == 0 else half
    nth = half // tr

    def body(pr, *refs):
        g_refs, (w_ref, m_ref, v_ref, g_ref, d_ref, nm_ref, nv_ref) = refs[:2 * nsub], refs[2 * nsub:]
        l = pl.program_id(0)
        mine = (pl.program_id(1) // nth) == pr[0]
        g = None
        for s in range(nsub):
            gs = jnp.where(mine, g_refs[2 * s][...], g_refs[2 * s + 1][...])
            g = gs if g is None else jnp.where(l == s, gs, g)
        m2 = ADAM_B1 * m_ref[...] + (1.0 - ADAM_B1) * g
        v2 = ADAM_B2 * v_ref[...] + (1.0 - ADAM_B2) * (g * g)
        m_hat = m2 / (1.0 - ADAM_B1 ** ADAM_STEP)
        v_hat = v2 / (1.0 - ADAM_B2 ** ADAM_STEP)
        g_ref[...] = g
        d_ref[...] = -ADAM_LR * (m_hat / (jnp.sqrt(v_hat) + ADAM_EPS) + ADAM_WD * w_ref[...])
        nm_ref[...] = m2
        nv_ref[...] = v2

    gblk = pl.BlockSpec((tr, cols), lambda l, i, pr: (i % nth, 0))
    blk = pl.BlockSpec((None, tr, cols), lambda l, i, pr: (l, i, 0))
    out = jax.ShapeDtypeStruct((nsub, rows, cols), F32)
    return _prefetch_call(body, name, place, (nsub, rows // tr), [gblk] * (2 * nsub) + [blk] * 3, [blk] * 4,
                          [out] * 4, [h for pair in halves for h in pair] + [w, m, v])


WEIGHT_NAMES = ("meta_tokens", "a_mu", "a_w_r", "a_w_k", "a_w_v", "a_w_o", "a_w0", "a_w1", "a_w2", "a_a0", "a_a1",
                "a_a2", "a_g1", "a_g2", "a_k_k", "a_k_a", "a_r_k", "a_gn_w", "a_gn_b", "kv_w_k", "kv_w_v", "b_w_q",
                "b_sinks", "b_w_o", "mlp_w_up", "mlp_w_down", "ln_g", "ln_b")
BIG_NAMES = ("a_w_r", "a_w_k", "a_w_v", "a_w_o", "b_w_q", "b_w_o")
EARLY_NAMES, LATE_NAMES = BIG_NAMES[:3], BIG_NAMES[3:]
PACK_MATS = (("kv_w_k", 256), ("kv_w_v", 256), ("a_w1", 64), ("a_a1", 64), ("a_g1", 128), ("a_w2", 64),
             ("a_a2", 64), ("a_g2", 128))
COLUMN_CUT = ("a_w2", "a_a2", "a_g2")
PACK_VECS = (("a_mu", 6), ("a_w0", 1), ("a_a0", 1), ("a_k_k", 1), ("a_k_a", 1), ("a_gn_w", 1), ("a_gn_b", 1),
             ("ln_g", 4), ("ln_b", 4), ("meta_tokens", 16))
PACK_REPL = (("a_r_k", 4), ("b_sinks", 1))
SHARD_W = D_MODEL // N_SHARD
N_MAT_ROWS = sum(r for _, r in PACK_MATS)
N_VEC_ROWS = sum(r for _, r in PACK_VECS)
N_PACK_ROWS = -(-(N_MAT_ROWS + N_VEC_ROWS + sum(r for _, r in PACK_REPL)) // 8) * 8
N_GATHER_VEC_ROWS = -(-N_VEC_ROWS // 16) * 16


def _pack_rows(arr):
    if arr.size == N_HEADS:
        return jnp.pad(arr.reshape(1, N_HEADS), ((0, 0), (0, SHARD_W - N_HEADS)))
    return arr.reshape(-1, SHARD_W)


def pack_small(get):
    parts = [_pack_rows(get(name)) for name, _ in PACK_MATS + PACK_VECS + PACK_REPL]
    used = sum(p.shape[0] for p in parts)
    return jnp.concatenate(parts + [jnp.zeros((N_PACK_ROWS - used, SHARD_W), F32)], axis=0)


def unpack_small(pack, shapes):
    out, off = {}, 0
    for name, rows in PACK_MATS + PACK_VECS + PACK_REPL:
        piece = pack[off:off + rows]
        off += rows
        out[name] = piece[:, :N_HEADS].reshape(shapes[name]) if name == "b_sinks" else piece.reshape(shapes[name])
    return out


def whole_weights(big_names, gathered_big, mats, vecs, a_r_k, b_sinks):
    p = {name: g.reshape(D_MODEL, D_MODEL) for name, g in zip(big_names, gathered_big)}
    off = 0
    for name, rows in PACK_MATS:
        piece = mats[:, off:off + rows]
        off += rows
        if name in COLUMN_CUT:
            p[name] = piece.transpose(1, 0, 2).reshape(rows, D_MODEL)
        else:
            p[name] = piece.reshape(D_MODEL, rows)
    v = vecs.transpose(1, 0, 2).reshape(-1, D_MODEL)
    off = 0
    for name, rows in PACK_VECS:
        p[name] = v[off:off + rows]
        off += rows
    for i in range(2):
        for j in range(2):
            p[f"ln_g{i}{j}"] = p["ln_g"][2 * i + j:2 * i + j + 1]
            p[f"ln_b{i}{j}"] = p["ln_b"][2 * i + j:2 * i + j + 1]
    p["a_r_k"] = a_r_k.reshape(1, D_MODEL)
    p["b_sinks"] = b_sinks
    return p


def small_grad_pack(g):
    parts = []
    for name, rows in PACK_MATS:
        if name in COLUMN_CUT:
            parts.append(g[name].reshape(rows, N_SHARD, SHARD_W).transpose(1, 0, 2))
        else:
            parts.append(g[name].reshape(N_SHARD, rows, SHARD_W))
    vec_rows = [g["a_mu"]] + [g[n] for n in ("a_w0", "a_a0", "a_k_k", "a_k_a", "a_gn_w", "a_gn_b")]
    vec_rows += [g[f"ln_g{i}{j}"] for i in range(2) for j in range(2)]
    vec_rows += [g[f"ln_b{i}{j}"] for i in range(2) for j in range(2)] + [g["meta_tokens"]]
    parts.append(jnp.concatenate(vec_rows, axis=0).reshape(N_VEC_ROWS, N_SHARD, SHARD_W).transpose(1, 0, 2))
    parts.append(jnp.broadcast_to(g["a_r_k"].reshape(1, -1, SHARD_W), (N_SHARD, D_MODEL // SHARD_W, SHARD_W)))
    sinks = jnp.pad(g["b_sinks"].reshape(1, 1, N_HEADS), ((0, 0), (0, 0), (0, SHARD_W - N_HEADS)))
    parts.append(jnp.broadcast_to(sinks, (N_SHARD, 1, SHARD_W)))
    used = sum(p.shape[1] for p in parts)
    parts.append(jnp.zeros((N_SHARD, N_PACK_ROWS - used, SHARD_W), F32))
    return jnp.concatenate(parts, axis=1)


def train_step(vals):
    w = {n: vals[n] for n in WEIGHT_NAMES}
    w_pack = pack_small(lambda n: w[n])
    early = [w[n][0].astype(BF16) for n in EARLY_NAMES]
    early += [w_pack[:N_MAT_ROWS].astype(BF16), w_pack[N_MAT_ROWS:N_MAT_ROWS + N_GATHER_VEC_ROWS]]
    gathered = all_gather_shards(early)
    ne = len(EARLY_NAMES)
    p = whole_weights(EARLY_NAMES, gathered[:ne], gathered[ne], gathered[ne + 1][:, :N_VEC_ROWS], w["a_r_k"],
                      w["b_sinks"])
    nb = len(BIG_NAMES)

    def late_set(big, layer):
        shards = [w[n][0].astype(BF16) for n in big]
        shards += [w["mlp_w_up"][layer].astype(BF16), w["mlp_w_down"][layer].astype(BF16)]

        def weights(got):
            out = {n: x.reshape(D_MODEL, D_MODEL) for n, x in zip(big, got)}
            out[f"mlp_up{layer}"], out[f"mlp_down{layer}"] = got[len(big):]
            return out

        return shards, weights

    late = (late_set(("b_w_q", "b_w_o"), 1), late_set(("a_w_o",), 0))

    place = placement()
    ready = {}

    def early_hook(g):
        srcs = [g[n].reshape(N_SHARD, SHARD_W, D_MODEL) for n in LATE_NAMES]
        srcs += [g["mlp_up0"], g["mlp_up1"], g["mlp_down0"], g["mlp_down1"]]
        ready["parts"] = pair_sums("early", place, srcs, [True] * len(srcs))
        return ready["parts"]

    loss, gx, g, early_from_chips = local_step(vals["x"][0], vals["loss_target"][0], p, late, early_hook)
    loss = lax.psum(loss, ("x", "y", "c"))
    srcs = [g[n].reshape(N_SHARD, SHARD_W, D_MODEL) for n in EARLY_NAMES] + [small_grad_pack(g)]
    rest = pair_sums("late", place, srcs, [True] * len(EARLY_NAMES) + [False])
    rest_from_chips = chip_exchange(rest)
    ne = len(EARLY_NAMES)
    halves = finish_sums(place, rest[:ne] + ready["parts"] + rest[ne:],
                         list(rest_from_chips[:ne]) + list(early_from_chips) + list(rest_from_chips[ne:]))

    res = {}
    for k, n in enumerate(BIG_NAMES):
        res[n] = adamw_update("adamw_" + n, place, halves[k:k + 1], w[n], vals["m_" + n], vals["v_" + n])
    for k, n in ((nb, "mlp_w_up"), (nb + 2, "mlp_w_down")):
        res[n] = adamw_update("adamw_" + n, place, halves[k:k + 2], w[n], vals["m_" + n], vals["v_" + n])
    packs = adamw_update("adamw_small", place, halves[-1:], w_pack[None], pack_small(lambda n: vals["m_" + n])[None],
                         pack_small(lambda n: vals["v_" + n])[None])
    shapes = {n: w[n].shape for n in WEIGHT_NAMES}
    small = [unpack_small(pk[0], shapes) for pk in packs]
    outs = [loss, gx[None]]
    for t in range(4):
        outs += [res[n][t] if n in res else small[t][n] for n in WEIGHT_NAMES]
    return tuple(outs)


def kernel(x, meta_tokens, a_mu, a_w_r, a_w_k, a_w_v, a_w_o, a_w0, a_w1, a_w2, a_a0, a_a1, a_a2, a_g1, a_g2, a_k_k,
           a_k_a, a_r_k, a_gn_w, a_gn_b, kv_w_k, kv_w_v, b_w_q, b_sinks, b_w_o, mlp_w_up, mlp_w_down, ln_g, ln_b,
           loss_target, m_meta_tokens, m_a_mu, m_a_w_r, m_a_w_k, m_a_w_v, m_a_w_o, m_a_w0, m_a_w1, m_a_w2, m_a_a0,
           m_a_a1, m_a_a2, m_a_g1, m_a_g2, m_a_k_k, m_a_k_a, m_a_r_k, m_a_gn_w, m_a_gn_b, m_kv_w_k, m_kv_w_v,
           m_b_w_q, m_b_sinks, m_b_w_o, m_mlp_w_up, m_mlp_w_down, m_ln_g, m_ln_b, v_meta_tokens, v_a_mu, v_a_w_r,
           v_a_w_k, v_a_w_v, v_a_w_o, v_a_w0, v_a_w1, v_a_w2, v_a_a0, v_a_a1, v_a_a2, v_a_g1, v_a_g2, v_a_k_k,
           v_a_k_a, v_a_r_k, v_a_gn_w, v_a_gn_b, v_kv_w_k, v_kv_w_v, v_b_w_q, v_b_sinks, v_b_w_o, v_mlp_w_up,
           v_mlp_w_down, v_ln_g, v_ln_b):
    return train_step(dict(locals()))
```

```python
import functools

import numpy as np
import jax
import jax.numpy as jnp
from jax import lax
from jax.experimental import pallas as pl
from jax.experimental.pallas import tpu as pltpu

F32 = jnp.float32
BF16 = jnp.bfloat16

D_MODEL = 1024
N_HEADS = 16
HEAD_DIM = 64
N_HEADS_KV = 4
GROUP = 4
KV_DIM = N_HEADS_KV * HEAD_DIM
N_META = 16
BLOCK = 128
PAD_FRONT = BLOCK - N_META
TOK0 = PAD_FRONT + N_META
N_FF_CHUNK = 4
N_SHARD = 4
N_DEV = 8
GN_EPS = 64e-5
LN_EPS = 1e-5
ROPE_THETA = 10000.0
ALPHA = 4.0 ** 0.25
ADAM_LR, ADAM_B1, ADAM_B2, ADAM_EPS, ADAM_WD, ADAM_STEP = 0.001, 0.9, 0.999, 1e-08, 0.01, 10
SCAN_T = 64
PAIR = 128
KVW = GROUP * HEAD_DIM
VMEM_LIMIT = 60 * 1024 * 1024
HI = lax.Precision.HIGHEST
MESH = pl.DeviceIdType.MESH


def _dot(a, b, ca, cb):
    return lax.dot_general(a.astype(BF16), b.astype(BF16), (((ca,), (cb,)), ((), ())),
                           preferred_element_type=F32)


@jax.custom_vjp
def mm(a, b):
    return _dot(a, b, 1, 0)


def _mm_fwd(a, b):
    return mm(a, b), b


def _mm_bwd(b, g):
    return _dot(g, b, 1, 1), jnp.zeros_like(b)


mm.defvjp(_mm_fwd, _mm_bwd)


@jax.custom_vjp
def mm_tap(a, b, tap):
    return _dot(a, b, 1, 0)


mm_tap.defvjp(lambda a, b, tap: (_dot(a, b, 1, 0), b), lambda b, g: (_dot(g, b, 1, 1), jnp.zeros_like(b), g))


def tmm(x, w, taps, xs):
    y = mm(x, w) if taps is None else mm_tap(x, w, taps[len(xs)])
    xs.append(x)
    return y


def vjp_taps(core, tap_shapes, args, cot):
    taps = [jnp.zeros(s, F32) for s in tap_shapes]
    _, vjp, xs = jax.vjp(core, taps, *args, has_aux=True)
    out = vjp(cot)
    return out[1:], [_dot(x, g, 0, 0) for x, g in zip(xs, out[0])]


def _split3(x):
    x1 = x.astype(BF16)
    r1 = x - x1.astype(F32)
    x2 = r1.astype(BF16)
    x3 = (r1 - x2.astype(F32)).astype(BF16)
    return x1, x2, x3


def _exact_dot(x, m01, cb=0):
    acc = None
    for piece in _split3(x):
        t = lax.dot_general(piece, m01, (((1,), (cb,)), ((), ())), preferred_element_type=F32)
        acc = t if acc is None else acc + t
    return acc


def _head_matrices():
    e = np.zeros((D_MODEL, N_HEADS), np.float32)
    e[np.arange(D_MODEL), np.arange(D_MODEL) // HEAD_DIM] = 1.0
    return jnp.asarray(e, BF16), jnp.asarray(e.T, BF16)


@jax.custom_vjp
def hsum(x, e, et):
    return _exact_dot(x, e)


@jax.custom_vjp
def hbc(s, e, et):
    return _exact_dot(s, et)


hsum.defvjp(lambda x, e, et: (_exact_dot(x, e), (e, et)),
            lambda res, g: (hbc(g, *res), jnp.zeros_like(res[0]), jnp.zeros_like(res[1])))
hbc.defvjp(lambda s, e, et: (_exact_dot(s, et), (e, et)),
           lambda res, g: (hsum(g, *res), jnp.zeros_like(res[0]), jnp.zeros_like(res[1])))


def _sigmoid(u):
    return 0.5 * (jnp.tanh(0.5 * u) + 1.0)


def _softplus(u):
    return jnp.maximum(u, 0.0) + jnp.log(1.0 + jnp.exp(-jnp.abs(u)))


def _layer_norm(z, g, b):
    mu = jnp.mean(z, axis=-1, keepdims=True)
    zc = z - mu
    var = jnp.mean(zc * zc, axis=-1, keepdims=True)
    return zc * lax.rsqrt(var + LN_EPS) * g + b


def _zero_map(nd):
    return lambda c, i: (0,) * nd


def _params():
    return pltpu.CompilerParams(dimension_semantics=("arbitrary", "arbitrary"), vmem_limit_bytes=VMEM_LIMIT)


def rowwise(name, fn, rows, consts, out_rows, out_accs, tm, nc=1, hosted=None):
    lp = rows[0].shape[-2]
    nt = lp // tm
    assert nt * tm == lp, (name, lp, tm)
    copies_fn, hosted_src, hosted_shapes, hosted_scratch, hosted_post = hosted or (None, (), [], [], None)
    ng = len(hosted_src)
    in_specs, args = [], []
    for a in rows:
        if a.ndim == 2:
            in_specs.append(pl.BlockSpec((tm, a.shape[1]), lambda c, i: (i, 0)))
        else:
            in_specs.append(pl.BlockSpec((a.shape[0], tm, a.shape[2]), lambda c, i: (0, i, 0)))
        args.append(a)
    for cst in consts:
        if isinstance(cst, tuple):
            arr, bs, im = cst
            in_specs.append(pl.BlockSpec(bs, im))
        else:
            arr = cst
            in_specs.append(pl.BlockSpec(arr.shape, _zero_map(arr.ndim), pipeline_mode=pl.Buffered(1)))
        args.append(arr)
    out_shape, out_specs, acc_per_chunk = [], [], []
    for spec in out_rows:
        if len(spec) == 3 and spec[2]:
            out_shape.append(jax.ShapeDtypeStruct((nc, lp, spec[0]), spec[1]))
            out_specs.append(pl.BlockSpec((None, tm, spec[0]), lambda c, i: (c, i, 0)))
        else:
            out_shape.append(jax.ShapeDtypeStruct((lp, spec[0]), spec[1]))
            out_specs.append(pl.BlockSpec((tm, spec[0]), lambda c, i: (i, 0)))
    for spec in out_accs:
        out_shape.append(jax.ShapeDtypeStruct(spec[0], spec[1]))
        if len(spec) == 4:
            out_specs.append(pl.BlockSpec(spec[2], spec[3]))
            acc_per_chunk.append(True)
        else:
            out_specs.append(pl.BlockSpec(spec[0], _zero_map(len(spec[0])), pipeline_mode=pl.Buffered(1)))
            acc_per_chunk.append(False)
    n_in, n_or, n_out = len(args), len(out_rows), len(out_shape)

    def body(*refs):
        c = pl.program_id(0)
        i = pl.program_id(1)
        if ng:
            src, dst = refs[n_in:n_in + ng], refs[n_in + ng + n_out:n_in + 2 * ng + n_out]
            sends, arrivals, forwards, forwarded = copies_fn(src, dst, *refs[n_in + 2 * ng + n_out:])

            @pl.when(jnp.logical_and(c == 0, i == 0))
            def _():
                for cp in sends:
                    cp.start()

        vals = [r[...] for r in refs[:n_in]]
        outs_r, outs_a = fn(c, i, *vals)
        out_refs = refs[n_in + ng:n_in + ng + n_out]
        for ref, val in zip(out_refs[:n_or], outs_r):
            ref[...] = val.astype(ref.dtype)
        for ref, val, per_chunk in zip(out_refs[n_or:], outs_a, acc_per_chunk):
            first = (i == 0) if per_chunk else jnp.logical_and(i == 0, c == 0)

            @pl.when(first)
            def _():
                ref[...] = val.astype(ref.dtype)

            @pl.when(jnp.logical_not(first))
            def _():
                ref[...] += val.astype(ref.dtype)

        if ng:
            @pl.when(jnp.logical_and(c == nc - 1, i == max(nt - 3, 0)))
            def _():
                for k, landed in enumerate(arrivals):
                    landed.wait_recv()
                    if forwards:
                        forwards[k].start()

            @pl.when(jnp.logical_and(c == nc - 1, i == nt - 1))
            def _():
                for cp in forwarded:
                    cp.wait_recv()
                for cp in sends + forwards:
                    cp.wait_send()

    outs = pl.pallas_call(body, name=name, grid=(nc, nt), in_specs=in_specs + [ANY] * ng,
                          out_specs=out_specs + [ANY] * ng, out_shape=out_shape + list(hosted_shapes),
                          scratch_shapes=list(hosted_scratch), compiler_params=_params())(*args, *hosted_src)
    if ng:
        return outs[:n_or], outs[n_or:n_out], hosted_post(outs[n_out:])
    return outs[:n_or], outs[n_or:]


def _row_ids(i, tm):
    return i * tm + lax.broadcasted_iota(jnp.int32, (tm, 1), 0)


PRE_TAPS = (D_MODEL, D_MODEL, D_MODEL, 64, D_MODEL, 64, D_MODEL, 128, D_MODEL)


def rwkv_pre(e, et, ws, taps, h, hp, mu_r, mu_w, mu_k, mu_v, mu_a, mu_g, w0, a0, k_k, k_a):
    w_r, w_k, w_v, w1, w2, a1, a2, g1, g2 = ws
    xs = []
    xx = hp - h
    r = tmm(h + xx * mu_r, w_r, taps, xs)
    k = tmm(h + xx * mu_k, w_k, taps, xs)
    v = tmm(h + xx * mu_v, w_v, taps, xs)
    wraw = -_softplus(-(w0 + tmm(jnp.tanh(tmm(h + xx * mu_w, w1, taps, xs)), w2, taps, xs))) - 0.5
    lw = -jnp.exp(wraw)
    a = _sigmoid(a0 + tmm(tmm(h + xx * mu_a, a1, taps, xs), a2, taps, xs))
    g = tmm(_sigmoid(tmm(h + xx * mu_g, g1, taps, xs)), g2, taps, xs)
    kk = k * k_k
    ss = hsum(kk * kk, e, et)
    pos = ss > 0.0
    nrm = jnp.where(pos, jnp.sqrt(jnp.where(pos, ss, 1.0)), 0.0)
    kk = kk * hbc(1.0 / jnp.maximum(nrm, 1e-12), e, et)
    k2 = k * (1.0 + (a - 1.0) * k_a)
    return (r, lw, k2, v, -kk, kk * a, g), xs


def rwkv_post(e, et, w_o, taps, y, r, k2, v, g, h0, gn_w, gn_b, rk, lg, lb):
    xs = []
    inv_n = 1.0 / HEAD_DIM
    yc = y - hbc(hsum(y, e, et) * inv_n, e, et)
    yv = hsum(yc * yc, e, et) * inv_n
    yn = yc * hbc(lax.rsqrt(yv + GN_EPS), e, et) * gn_w + gn_b
    bonus = hbc(hsum(r * k2 * rk, e, et), e, et) * v
    mix = tmm((yn + bonus) * g, w_o, taps, xs)
    return _layer_norm(ALPHA * h0 + mix, lg, lb), xs


@jax.custom_vjp
def sq_relu(x):
    r = jnp.maximum(x, 0.0)
    return r * r


sq_relu.defvjp(lambda x: (sq_relu(x), x), lambda x, g: (g * (2.0 * jnp.maximum(x, 0.0)),))


def mlp_chunk(wup, wdown, taps, h):
    xs = []
    return tmm(sq_relu(tmm(h, wup, taps, xs)), wdown, taps, xs), xs


def _rot_half(t):
    n = t.shape[-1]
    lane = lax.broadcasted_iota(jnp.int32, t.shape, t.ndim - 1)
    lo = (lane % HEAD_DIM) < (HEAD_DIM // 2)
    return jnp.where(lo, -pltpu.roll(t, n - HEAD_DIM // 2, t.ndim - 1), pltpu.roll(t, HEAD_DIM // 2, t.ndim - 1))


@jax.custom_vjp
def rot_half(t):
    return _rot_half(t)


rot_half.defvjp(lambda t: (_rot_half(t), None), lambda _, g: (-_rot_half(g),))


def _tile_lanes(t, width):
    return jnp.concatenate([t] * (width // t.shape[-1]), axis=-1)


def qkv_proj(cos, sin, wq, wk, wv, taps, h):
    xs = []
    q = tmm(h, wq, taps, xs)
    k = tmm(h, wk, taps, xs)
    v = tmm(h, wv, taps, xs)
    cq, sq = _tile_lanes(cos, D_MODEL), _tile_lanes(sin, D_MODEL)
    ck, sk = _tile_lanes(cos, KV_DIM), _tile_lanes(sin, KV_DIM)
    return (q * cq + rot_half(q) * sq, k * ck + rot_half(k) * sk, v), xs


def attn_out(w_o, taps, o, h, lg, lb):
    xs = []
    return _layer_norm(ALPHA * h + tmm(o, w_o, taps, xs), lg, lb), xs


def _scan_consts():
    t = SCAN_T
    tri = np.tril(np.ones((t, t), np.float32))
    rows = np.arange(2 * t)
    same = (rows[:, None] // t) == (rows[None, :] // t)
    strict = same & ((rows[None, :] % t) < (rows[:, None] % t))
    incl = same & ((rows[None, :] % t) <= (rows[:, None] % t))
    lane = np.arange(PAIR)
    masks = np.zeros((8, PAIR), np.float32)
    masks[0] = (lane // HEAD_DIM) == 0
    masks[1] = (lane // HEAD_DIM) == 1
    return (jnp.asarray(tri, BF16), jnp.asarray(strict.astype(np.float32)), jnp.asarray(incl.astype(np.float32)),
            jnp.asarray(masks), jnp.asarray(np.eye(2 * t, dtype=np.float32)))


def _scan_dot(a, b, ca, cb):
    return _dot(a, b, ca, cb)


@functools.partial(jax.custom_vjp, nondiff_argnums=(2, 3))
def _dotf(a, b, ca, cb):
    return _scan_dot(a, b, ca, cb)


def _dotf_bwd(ca, cb, res, g):
    a, b = res
    if ca == 1:
        da = _scan_dot(g, b, 1, 1 - cb)
    else:
        da = _scan_dot(b, g, 1 - cb, 1)
    if cb == 0:
        db = _scan_dot(a, g, 1 - ca, 0)
    else:
        db = _scan_dot(g, a, 0, 1 - ca)
    return da, db


_dotf.defvjp(lambda a, b, ca, cb: (_scan_dot(a, b, ca, cb), (a, b)), _dotf_bwd)


def _tri_dot(tri, x, ct):
    acc = None
    for piece in _split3(x):
        t = lax.dot_general(tri, piece, (((ct,), (0,)), ((), ())), preferred_element_type=F32)
        acc = t if acc is None else acc + t
    return acc


@jax.custom_vjp
def _cumsum_rows(tri, x):
    return _tri_dot(tri, x, 1)


_cumsum_rows.defvjp(lambda tri, x: (_tri_dot(tri, x, 1), tri),
                    lambda tri, g: (jnp.zeros_like(tri), _tri_dot(tri, g, 0)))


@jax.custom_vjp
def _unstack2(x):
    t = x.shape[0] // 2
    return x[:t] + x[t:]


_unstack2.defvjp(lambda x: (_unstack2(x), None), lambda _, g: (jnp.concatenate([g, g], axis=0),))


@jax.custom_vjp
def _last_row(x):
    return x[x.shape[0] - 1:, :]


def _last_row_bwd(_, g):
    rows = lax.broadcasted_iota(jnp.int32, (SCAN_T, g.shape[1]), 0)
    return (jnp.where(rows == SCAN_T - 1, jnp.broadcast_to(g, (SCAN_T, g.shape[1])), 0.0),)


_last_row.defvjp(lambda x: (_last_row(x), None), _last_row_bwd)


@jax.custom_vjp
def _solve_saved(n, rhs, minv, u):
    return u


def _solve_saved_bwd(res, du):
    minv, u = res
    drhs = _dotf(minv, du, 0, 0)
    return _dotf(drhs, u, 1, 1), drhs, jnp.zeros_like(minv), jnp.zeros_like(u)


_solve_saved.defvjp(lambda n, rhs, minv, u: (u, (minv, u)), _solve_saved_bwd)


def scan_chunk(tri, strict, incl, m0, m1, eye, r, lw, k, v, a, b, s0, saved=None):
    lower = strict > 0
    lower_incl = incl > 0

    def stack(x):
        return jnp.concatenate([x * m0, x * m1], axis=0)

    def dots(xs, ys, ca, cb, mask=None):
        out = [_dotf(x, y, ca, cb) for x, y in zip(xs, ys)]
        return out if mask is None else [jnp.where(mask, o, 0.0) for o in out]

    cl = [_cumsum_rows(tri, x) for x in lw]
    gam = [jnp.exp(c) for c in cl]
    ginv = [jnp.exp(-c) for c in cl]
    a_s = [stack(x * jnp.exp(c - w)) for x, c, w in zip(a, cl, lw)]
    r_s = [stack(x * g) for x, g in zip(r, gam)]
    b_s = [stack(x * g) for x, g in zip(b, ginv)]
    k_s = [stack(x * g) for x, g in zip(k, ginv)]
    v_s = [stack(x) for x in v]
    n_ab = dots(a_s, b_s, 1, 1, lower)
    n_ak = dots(a_s, k_s, 1, 1, lower)
    r_ab = dots(r_s, b_s, 1, 1, lower_incl)
    r_ak = dots(r_s, k_s, 1, 1, lower_incl)
    rhs = [x + y for x, y in zip(dots(a_s, s0, 1, 1), dots(n_ak, v_s, 1, 0))]
    if saved is None:
        minv = [eye + n for n in n_ab]
        p = n_ab
        for _ in range(5):
            p = dots(p, p, 1, 0)
            minv = [m + mp for m, mp in zip(minv, dots(minv, p, 1, 0))]
        u_s = dots(minv, rhs, 1, 0)
    else:
        minv = saved[0]
        u_s = [_solve_saved(n, x, m, u) for n, x, m, u in zip(n_ab, rhs, *saved)]
    y = [_unstack2(x0 + x1 + x2)
         for x0, x1, x2 in zip(dots(r_s, s0, 1, 1), dots(r_ab, u_s, 1, 0), dots(r_ak, v_s, 1, 0))]
    g_end = [_last_row(g) for g in gam]
    s1 = [s * g + x + z for s, g, x, z in zip(s0, g_end, dots(u_s, [x * g for x, g in zip(b_s, g_end)], 0, 0),
                                              dots(v_s, [x * g for x, g in zip(k_s, g_end)], 0, 0))]
    return y, s1, (minv, u_s)


SCAN_PAIRS = 8


def _scan_specs(consts, order):
    row = pl.BlockSpec((SCAN_T, PAIR * SCAN_PAIRS), lambda p, c: (order(c), p))
    state = pl.BlockSpec((None, SCAN_PAIRS, PAIR, PAIR), lambda p, c: (order(c), p, 0, 0))
    return row, state, [pl.BlockSpec(x.shape, _zero_map(x.ndim)) for x in consts]


def _pair_lanes(q):
    return slice(q * PAIR, (q + 1) * PAIR)


def scan_fwd(r, lw, k, v, a, b, shards=()):
    lp = r.shape[0]
    nch = lp // SCAN_T
    npair = D_MODEL // PAIR
    ng = len(shards)
    consts = _scan_consts()
    row, state, cspecs = _scan_specs(consts, lambda c: c)

    def body(tri, strict, incl, masks, eye, r_ref, lw_ref, k_ref, v_ref, a_ref, b_ref, *rest):
        src, (y_ref, s_ref, minv_ref, u_ref), dst = rest[:ng], rest[ng:ng + 4], rest[ng + 4:2 * ng + 4]
        carry = rest[2 * ng + 4]
        first = jnp.logical_and(pl.program_id(0) == 0, pl.program_id(1) == 0)
        last = jnp.logical_and(pl.program_id(0) == npair // SCAN_PAIRS - 1, pl.program_id(1) == nch - 1)
        if ng:
            sends, arrivals, forwards, forwarded = gather_copies(src, dst, *rest[2 * ng + 5:])

            @pl.when(first)
            def _():
                for cp in sends:
                    cp.start()

            @pl.when(jnp.logical_and(pl.program_id(0) == npair // SCAN_PAIRS - 1, pl.program_id(1) == nch * 3 // 4))
            def _():
                for landed, onward in zip(arrivals, forwards):
                    landed.wait_recv()
                    onward.start()

        @pl.when(pl.program_id(1) == 0)
        def _():
            carry[...] = jnp.zeros_like(carry)

        pairs = range(SCAN_PAIRS)
        s0 = [carry[q] for q in pairs]
        rows = [[ref[:, _pair_lanes(q)] for q in pairs] for ref in (r_ref, lw_ref, k_ref, v_ref, a_ref, b_ref)]
        y, s1, (minv, u) = scan_chunk(tri[...], strict[...], incl[...], masks[0:1, :], masks[1:2, :], eye[...],
                                      *rows, s0)
        for q in pairs:
            s_ref[q] = s0[q]
            minv_ref[q] = minv[q]
            u_ref[q] = u[q]
            y_ref[:, _pair_lanes(q)] = y[q]
            carry[q] = s1[q]

        if ng:
            @pl.when(last)
            def _():
                for cp in forwarded:
                    cp.wait_recv()
                for cp in sends + forwards:
                    cp.wait_send()

    mats = jax.ShapeDtypeStruct((nch, npair, PAIR, PAIR), F32)
    out = pl.pallas_call(
        body, name="rwkv_scan_fwd", grid=(npair // SCAN_PAIRS, nch), in_specs=cspecs + [row] * 6 + [ANY] * ng,
        out_specs=[row, state, state, state] + [ANY] * ng,
        out_shape=[jax.ShapeDtypeStruct((lp, D_MODEL), F32), mats, mats, mats] + gathered_shapes(shards),
        scratch_shapes=[pltpu.VMEM((SCAN_PAIRS, PAIR, PAIR), F32)] + (gather_scratch(ng) if ng else []),
        compiler_params=_params(),
    )(*consts, r, lw, k, v, a, b, *shards)
    return out[:4], fill_own(out[4:], shards)


def scan_bwd(r, lw, k, v, a, b, saved, dy, direct_grads, parts=()):
    lp = r.shape[0]
    nch = lp // SCAN_T
    npair = D_MODEL // PAIR
    consts = _scan_consts()
    row, state, cspecs = _scan_specs(consts, lambda c: nch - 1 - c)

    ng = len(parts)

    def body(tri, strict, incl, masks, eye, r_ref, lw_ref, k_ref, v_ref, a_ref, b_ref, s_ref, minv_ref, u_ref,
             dy_ref, dr_in, dk_in, dv_in, *rest):
        src, (dr_ref, dlw_ref, dk_ref, dv_ref, da_ref, db_ref), dst = rest[:ng], rest[ng:ng + 6], rest[ng + 6:2 * ng + 6]
        carry = rest[2 * ng + 6]
        first = jnp.logical_and(pl.program_id(0) == 0, pl.program_id(1) == 0)
        last = jnp.logical_and(pl.program_id(0) == npair // SCAN_PAIRS - 1, pl.program_id(1) == nch - 1)
        if ng:
            sends, arrivals = chip_exchange_copies(src, dst, *rest[2 * ng + 7:])

            @pl.when(first)
            def _():
                for cp in sends:
                    cp.start()

        @pl.when(pl.program_id(1) == 0)
        def _():
            carry[...] = jnp.zeros_like(carry)

        pairs = range(SCAN_PAIRS)
        kept = ([minv_ref[q] for q in pairs], [u_ref[q] for q in pairs])

        def fn(*args):
            y, s1, _ = scan_chunk(tri[...], strict[...], incl[...], masks[0:1, :], masks[1:2, :], eye[...], *args,
                                  saved=kept)
            return y, s1

        rows = [[ref[:, _pair_lanes(q)] for q in pairs] for ref in (r_ref, lw_ref, k_ref, v_ref, a_ref, b_ref)]
        _, vjp = jax.vjp(fn, *rows, [s_ref[q] for q in pairs])
        grads = vjp(([dy_ref[:, _pair_lanes(q)] for q in pairs], [carry[q] for q in pairs]))
        direct = (dr_in, None, dk_in, dv_in, None, None)
        for q in pairs:
            ln = _pair_lanes(q)
            for ref, g, extra in zip((dr_ref, dlw_ref, dk_ref, dv_ref, da_ref, db_ref), grads[:6], direct):
                ref[:, ln] = g[q] if extra is None else g[q] + extra[:, ln]
            carry[q] = grads[6][q]

        if ng:
            @pl.when(last)
            def _():
                for cp in arrivals:
                    cp.wait_recv()
                for cp in sends:
                    cp.wait_send()

    out = pl.pallas_call(
        body, name="rwkv_scan_bwd", grid=(npair // SCAN_PAIRS, nch),
        in_specs=cspecs + [row] * 6 + [state] * 3 + [row] * 4 + [ANY] * ng, out_specs=[row] * 6 + [ANY] * ng,
        out_shape=[jax.ShapeDtypeStruct((lp, D_MODEL), F32)] * 6 + [jax.ShapeDtypeStruct(p.shape, p.dtype) for p in parts],
        scratch_shapes=[pltpu.VMEM((SCAN_PAIRS, PAIR, PAIR), F32)] + (_sem_scratch(ng * len(XY_FLIPS)) if ng else []),
        compiler_params=_params(),
    )(*consts, r, lw, k, v, a, b, *saved, dy, *direct_grads, *parts)
    return out[:6], out[6:]


def _spread_matrices():
    rep = np.zeros((N_HEADS_KV, KV_DIM, KVW), np.float32)
    for h in range(N_HEADS_KV):
        for g in range(GROUP):
            rep[h, h * HEAD_DIM + np.arange(HEAD_DIM), g * HEAD_DIM + np.arange(HEAD_DIM)] = 1.0
    return jnp.asarray(rep, BF16)


KV_HEADS = range(N_HEADS_KV)


def _attn_common(n, q_ref, kp, kc, vp, vc, rep_ref, sink_ref):
    lane = lax.broadcasted_iota(jnp.int32, (1, KVW), 1)
    gmask = [(lane // HEAD_DIM == g).astype(F32) for g in range(GROUP)]
    kk = jnp.concatenate([kp, kc], axis=0)
    vv = jnp.concatenate([vp, vc], axis=0)
    qs = [q_ref[:, h * KVW:(h + 1) * KVW] for h in KV_HEADS]
    q_s = [jnp.concatenate([q * gmask[g] for g in range(GROUP)], axis=0) for q in qs]
    keys = [_dot(kk, rep_ref[h], 1, 0) for h in KV_HEADS]
    vals = [_dot(vv, rep_ref[h], 1, 0) for h in KV_HEADS]
    qi = lax.broadcasted_iota(jnp.int32, (GROUP * BLOCK, 2 * BLOCK), 0) % BLOCK
    kj = lax.broadcasted_iota(jnp.int32, (GROUP * BLOCK, 2 * BLOCK), 1)
    rel = BLOCK + qi - kj
    valid = (rel >= 0) & (rel < BLOCK) & ((n - 1) * BLOCK + kj >= PAD_FRONT)
    s = [jnp.where(valid, _dot(x, y, 1, 1) * (HEAD_DIM ** -0.5), -1e30) for x, y in zip(q_s, keys)]
    sink_col = [jnp.concatenate([jnp.broadcast_to(sink_ref[h, g:g + 1, 0:1], (BLOCK, 1)) for g in range(GROUP)],
                                axis=0) for h in KV_HEADS]
    m = [jnp.maximum(jnp.max(x, axis=-1, keepdims=True), c) for x, c in zip(s, sink_col)]
    ex = [jnp.exp(x - y) for x, y in zip(s, m)]
    ex_sink = [jnp.exp(c - y) for c, y in zip(sink_col, m)]
    inv = [1.0 / (jnp.sum(x, axis=-1, keepdims=True) + c) for x, c in zip(ex, ex_sink)]
    return (gmask, q_s, keys, vals, [x * y for x, y in zip(ex, inv)], [x * y for x, y in zip(ex_sink, inv)])


def _unstack_groups(x_s, gmask):
    out = None
    for g in range(GROUP):
        t = x_s[g * BLOCK:(g + 1) * BLOCK] * gmask[g]
        out = t if out is None else out + t
    return out


def _attn_specs():
    qspec = pl.BlockSpec((BLOCK, D_MODEL), lambda n: (n, 0))
    cur = pl.BlockSpec((BLOCK, KV_DIM), lambda n: (n, 0))
    prev = pl.BlockSpec((BLOCK, KV_DIM), lambda n: (jnp.maximum(n - 1, 0), 0))
    rep = pl.BlockSpec((N_HEADS_KV, KV_DIM, KVW), lambda n: (0, 0, 0))
    sink = pl.BlockSpec((N_HEADS_KV, 8, PAIR), lambda n: (0, 0, 0))
    return qspec, cur, prev, rep, sink


def _attn_params():
    return pltpu.CompilerParams(dimension_semantics=("arbitrary",), vmem_limit_bytes=VMEM_LIMIT)


def attn_fwd(q, k, v, sinks_b):
    lp = q.shape[0]
    qspec, cur, prev, rep, sink = _attn_specs()

    def body(q_ref, kp_ref, kc_ref, vp_ref, vc_ref, rep_ref, sink_ref, o_ref):
        gmask, _, _, vals, p, _ = _attn_common(pl.program_id(0), q_ref, kp_ref[...], kc_ref[...], vp_ref[...],
                                               vc_ref[...], rep_ref, sink_ref)
        o = [_dot(x, y, 1, 0) for x, y in zip(p, vals)]
        for h in KV_HEADS:
            o_ref[:, h * KVW:(h + 1) * KVW] = _unstack_groups(o[h], gmask)

    return pl.pallas_call(
        body, name="swa_fwd", grid=(lp // BLOCK,), in_specs=[qspec, prev, cur, prev, cur, rep, sink],
        out_specs=qspec, out_shape=jax.ShapeDtypeStruct((lp, D_MODEL), F32), compiler_params=_attn_params(),
    )(q, k, k, v, v, _spread_matrices(), sinks_b)


def attn_bwd(q, k, v, sinks_b, do):
    lp = q.shape[0]
    qspec, cur, prev, rep, sink = _attn_specs()

    def body(q_ref, kp_ref, kc_ref, vp_ref, vc_ref, rep_ref, sink_ref, do_ref, dq_ref, dkc_ref, dkp_ref, dvc_ref,
             dvp_ref, dsink_ref):
        n = pl.program_id(0)
        gmask, q_s, keys, vals, p, p_sink = _attn_common(n, q_ref, kp_ref[...], kc_ref[...], vp_ref[...], vc_ref[...],
                                                         rep_ref, sink_ref)
        do_s = [jnp.concatenate([do_ref[:, h * KVW:(h + 1) * KVW] * gmask[g] for g in range(GROUP)], axis=0)
                for h in KV_HEADS]
        dp = [_dot(x, y, 1, 1) for x, y in zip(do_s, vals)]
        delta = [jnp.sum(x * y, axis=-1, keepdims=True) for x, y in zip(p, dp)]
        ds = [x * (y - z) * (HEAD_DIM ** -0.5) for x, y, z in zip(p, dp, delta)]
        dq = [_dot(x, y, 1, 0) for x, y in zip(ds, keys)]
        dkeys_s = [_dot(x, y, 0, 0) for x, y in zip(ds, q_s)]
        dvals_s = [_dot(x, y, 0, 0) for x, y in zip(p, do_s)]
        dkeys = [_exact_dot(x, rep_ref[h], cb=1) for h, x in enumerate(dkeys_s)]
        dvals = [_exact_dot(x, rep_ref[h], cb=1) for h, x in enumerate(dvals_s)]
        dk_all = (dkeys[0] + dkeys[1]) + (dkeys[2] + dkeys[3])
        dv_all = (dvals[0] + dvals[1]) + (dvals[2] + dvals[3])
        dkp_ref[...] = dk_all[:BLOCK]
        dkc_ref[...] = dk_all[BLOCK:]
        dvp_ref[...] = dv_all[:BLOCK]
        dvc_ref[...] = dv_all[BLOCK:]
        dsinks = []
        for h in KV_HEADS:
            dq_ref[:, h * KVW:(h + 1) * KVW] = _unstack_groups(dq[h], gmask)
            dsk = -(p_sink[h] * delta[h])
            rows = [jnp.broadcast_to(jnp.sum(dsk[g * BLOCK:(g + 1) * BLOCK], axis=0, keepdims=True), (1, PAIR))
                    for g in range(GROUP)]
            dsinks.append(jnp.concatenate(rows + [jnp.zeros((8 - GROUP, PAIR), F32)], axis=0))

        @pl.when(n == 0)
        def _():
            for h in KV_HEADS:
                dsink_ref[h] = dsinks[h]

        @pl.when(n > 0)
        def _():
            for h in KV_HEADS:
                dsink_ref[h] += dsinks[h]

    kv = jax.ShapeDtypeStruct((lp, KV_DIM), F32)
    return pl.pallas_call(
        body, name="swa_bwd", grid=(lp // BLOCK,), in_specs=[qspec, prev, cur, prev, cur, rep, sink, qspec],
        out_specs=[qspec, cur, cur, cur, cur, sink],
        out_shape=[jax.ShapeDtypeStruct((lp, D_MODEL), F32), kv, kv, kv, kv,
                   jax.ShapeDtypeStruct((N_HEADS_KV, 8, PAIR), F32)],
        compiler_params=_attn_params(),
    )(q, k, k, v, v, _spread_matrices(), sinks_b, do)


def _pick_tm(lp, want):
    for tm in (384, 192, 128, 64):
        if tm <= want and lp % tm == 0:
            return tm
    raise ValueError(lp)


def _acc(shape):
    return (tuple(shape), F32)


def _ff_one(w):
    return (w, (None, D_MODEL, D_MODEL), lambda c, i: (c, 0, 0))


def _mlp_layer_fwd(name, h, wup, wdown, lg, lb, tm):
    def fn(c, i, h, wup, wdown, lg, lb):
        out = None
        for s in range(N_FF_CHUNK):
            t = mlp_chunk(wup[s], wdown[s], None, h)[0]
            out = t if out is None else out + t
        z = ALPHA * h + out
        return (_layer_norm(z, lg, lb), z), ()

    (h_out, z), _ = rowwise(name, fn, [h], [wup, wdown, lg, lb], [(D_MODEL, F32), (D_MODEL, F32)], [], tm)
    return h_out, z


def _mlp_layer_bwd(name, h_in, z, dh_parts, wup, wdown, lg, lb, tm):
    n_parts = len(dh_parts)

    def fn_ln(c, i, z, *rest):
        dh = rest[0]
        for extra in rest[1:n_parts]:
            dh = dh + extra
        _, vjp = jax.vjp(_layer_norm, z, rest[n_parts], rest[n_parts + 1])
        dz, dlg, dlb = vjp(dh)
        return (dz,), (dlg, dlb)

    (dz,), (dlg, dlb) = rowwise(name + "_ln", fn_ln, [z] + list(dh_parts), [lg, lb], [(D_MODEL, F32)],
                                [_acc((1, D_MODEL)), _acc((1, D_MODEL))], tm)

    def fn_mlp(c, i, h, dz, wup, wdown):
        tile = h.shape[0]
        (dx,), dws = vjp_taps(functools.partial(mlp_chunk, wup, wdown), [(tile, D_MODEL)] * 2, [h], dz)
        return (dx,), dws

    aspec = ((N_FF_CHUNK, D_MODEL, D_MODEL), F32, (None, D_MODEL, D_MODEL), lambda c, i: (c, 0, 0))
    (dx,), (dwup, dwdown) = rowwise(name + "_mm", fn_mlp, [h_in, dz], [_ff_one(wup), _ff_one(wdown)],
                                    [(D_MODEL, F32, True)], [aspec, aspec], tm, nc=N_FF_CHUNK)
    return dz, dx, dwup, dwdown, dlg, dlb


def _sum_parts(dz, dx):
    out = ALPHA * dz
    for s in range(N_FF_CHUNK):
        out = out + dx[s]
    return out


def local_step(x, loss_target, p, late=None, early_hook=None):
    seq = x.shape[0]
    lp = TOK0 + seq
    tm = _pick_tm(lp, 384)
    tms = _pick_tm(lp, 192)
    e, et = _head_matrices()
    h0 = jnp.concatenate([jnp.zeros((PAD_FRONT, D_MODEL), F32), p["meta_tokens"], x], axis=0)
    hp = jnp.concatenate([jnp.zeros((1, D_MODEL), F32), h0[:-1]], axis=0)
    tgt = jnp.concatenate([jnp.zeros((TOK0, D_MODEL), F32), loss_target], axis=0)
    pos = jnp.maximum(jnp.arange(lp, dtype=F32) - PAD_FRONT, 0.0)
    inv_freq = 1.0 / (ROPE_THETA ** (jnp.arange(0, HEAD_DIM, 2, dtype=F32) / HEAD_DIM))
    ang = pos[:, None] * inv_freq[None, :]
    cos = jnp.tile(jnp.cos(ang), (1, PAIR // (HEAD_DIM // 2)))
    sin = jnp.tile(jnp.sin(ang), (1, PAIR // (HEAD_DIM // 2)))

    pre_vec = [p["a_mu"][j:j + 1] for j in range(6)] + [p["a_w0"], p["a_a0"], p["a_k_k"], p["a_k_a"]]
    pre_w = [p["a_w_r"], p["a_w_k"], p["a_w_v"], p["a_w1"], p["a_w2"], p["a_a1"], p["a_a2"], p["a_g1"], p["a_g2"]]
    n_vec = len(pre_vec)

    def fn_pre(c, i, h, hp, e, et, *ws):
        return rwkv_pre(e, et, ws[n_vec:], None, h, hp, *ws[:n_vec])[0], ()

    (r, lw, k2, v, an, bn, g), _, *pre_gathered = rowwise(
        "rwkv_pre", fn_pre, [h0, hp], [e, et] + pre_vec + pre_w, [(D_MODEL, F32)] * 7, [], tms,
        hosted=hosted_gather(late[0][0]) if late else None)
    (y, *scan_saved), scan_gathered = scan_fwd(r, lw, k2, v, an, bn, late[1][0] if late else ())
    if late:
        p = {**p, **late[0][1](pre_gathered[0]), **late[1][1](scan_gathered)}

    post_c = [p["a_w_o"], p["a_gn_w"], p["a_gn_b"], p["a_r_k"], p["ln_g00"], p["ln_b00"]]

    def fn_post(c, i, y, r, k2, v, g, h0, e, et, w_o, *vecs):
        return (rwkv_post(e, et, w_o, None, y, r, k2, v, g, h0, *vecs)[0],), ()

    (h1,), _ = rowwise("rwkv_post", fn_post, [y, r, k2, v, g, h0], [e, et] + post_c, [(D_MODEL, F32)], [], tm)
    h2, z2 = _mlp_layer_fwd("mlp0_fwd", h1, p["mlp_up0"], p["mlp_down0"], p["ln_g01"], p["ln_b01"], tm)

    qkv_w = [p["b_w_q"], p["kv_w_k"], p["kv_w_v"]]

    def fn_qkv(c, i, h, cos, sin, wq, wk, wv):
        return qkv_proj(cos, sin, wq, wk, wv, None, h)[0], ()

    (q, k, vv), _ = rowwise("qkv_proj", fn_qkv, [h2, cos, sin], qkv_w,
                            [(D_MODEL, F32), (KV_DIM, F32), (KV_DIM, F32)], [], tm)
    sinks_b = jnp.broadcast_to(p["b_sinks"].reshape(N_HEADS_KV, GROUP, 1), (N_HEADS_KV, GROUP, PAIR))
    sinks_b = jnp.concatenate([sinks_b, jnp.zeros((N_HEADS_KV, 8 - GROUP, PAIR), F32)], axis=1)
    o = attn_fwd(q, k, vv, sinks_b)

    ao_c = [p["b_w_o"], p["ln_g10"], p["ln_b10"]]

    def fn_ao(c, i, o, h, w_o, lg, lb):
        return (attn_out(w_o, None, o, h, lg, lb)[0],), ()

    (h3,), _ = rowwise("attn_out", fn_ao, [o, h2], ao_c, [(D_MODEL, F32)], [], tm)
    h4, z4 = _mlp_layer_fwd("mlp1_fwd", h3, p["mlp_up1"], p["mlp_down1"], p["ln_g11"], p["ln_b11"], tm)

    def fn_loss(c, i, h4, tgt):
        real = (_row_ids(i, tm) >= TOK0).astype(F32)
        err = (h4 - tgt) * real
        part = 0.5 * jnp.sum(jnp.sum(err * err, axis=-1, keepdims=True), axis=0, keepdims=True) / D_MODEL
        return (err * (1.0 / D_MODEL),), (jnp.broadcast_to(part, (8, PAIR)),)

    (dh4,), (loss_acc,) = rowwise("loss", fn_loss, [h4, tgt], [], [(D_MODEL, F32)], [_acc((8, PAIR))], tm)
    loss = loss_acc[0, 0]

    grads = {}
    dz4, dx4, grads["mlp_up1"], grads["mlp_down1"], grads["ln_g11"], grads["ln_b11"] = _mlp_layer_bwd(
        "mlp1_bwd", h3, z4, [dh4], p["mlp_up1"], p["mlp_down1"], p["ln_g11"], p["ln_b11"], tm)

    def fn_ao_b(c, i, dz, dx, o, h, w_o, lg, lb):
        (do, dh, dlg, dlb), (dw_o,) = vjp_taps(functools.partial(attn_out, w_o), [(tm, D_MODEL)], [o, h, lg, lb],
                                               _sum_parts(dz, dx))
        return (do, dh), (dw_o, dlg, dlb)

    (do, dh2_a), (grads["b_w_o"], grads["ln_g10"], grads["ln_b10"]) = rowwise(
        "attn_out_bwd", fn_ao_b, [dz4, dx4, o, h2], ao_c, [(D_MODEL, F32)] * 2,
        [_acc((D_MODEL, D_MODEL)), _acc((1, D_MODEL)), _acc((1, D_MODEL))], tm)

    dq, dkc, dkp, dvc, dvp, dsinks = attn_bwd(q, k, vv, sinks_b, do)
    grads["b_sinks"] = dsinks[:, :GROUP, 0].reshape(1, N_HEADS)
    zblk = jnp.zeros((BLOCK, KV_DIM), F32)
    dkp_s = jnp.concatenate([dkp[BLOCK:], zblk], axis=0)
    dvp_s = jnp.concatenate([dvp[BLOCK:], zblk], axis=0)

    def fn_qkv_b(c, i, h, cos, sin, dq, dkc, dkp, dvc, dvp, wq, wk, wv):
        return vjp_taps(functools.partial(qkv_proj, cos, sin, wq, wk, wv),
                        [(tm, D_MODEL), (tm, KV_DIM), (tm, KV_DIM)], [h], (dq, dkc + dkp, dvc + dvp))

    (dh2_q,), (grads["b_w_q"], grads["kv_w_k"], grads["kv_w_v"]) = rowwise(
        "qkv_proj_bwd", fn_qkv_b, [h2, cos, sin, dq, dkc, dkp_s, dvc, dvp_s], qkv_w, [(D_MODEL, F32)],
        [_acc((D_MODEL, D_MODEL)), _acc((D_MODEL, KV_DIM)), _acc((D_MODEL, KV_DIM))], tm)

    dz2, dx2, grads["mlp_up0"], grads["mlp_down0"], grads["ln_g01"], grads["ln_b01"] = _mlp_layer_bwd(
        "mlp0_bwd", h1, z2, [dh2_a, dh2_q], p["mlp_up0"], p["mlp_down0"], p["ln_g01"], p["ln_b01"], tm)

    def fn_post_b(c, i, dz, dx, y, r, k2, v, g, h0, e, et, w_o, *vecs):
        out, dws = vjp_taps(functools.partial(rwkv_post, e, et, w_o), [(tms, D_MODEL)],
                            [y, r, k2, v, g, h0] + list(vecs), _sum_parts(dz, dx))
        return out[:6], tuple(dws) + tuple(out[6:])

    early_srcs = early_hook[0](grads) if early_hook else ()
    (dy, dr_c, dk_c, dv_c, dg, dh0_c), post_g, *early_got = rowwise(
        "rwkv_post_bwd", fn_post_b, [dz2, dx2, y, r, k2, v, g, h0], [e, et] + post_c, [(D_MODEL, F32)] * 6,
        [_acc((D_MODEL, D_MODEL))] + [_acc((1, D_MODEL))] * 5, tms,
        hosted=hosted_pair_exchange(early_srcs) if early_hook else None)
    for name, val in zip(["a_w_o", "a_gn_w", "a_gn_b", "a_r_k", "ln_g00", "ln_b00"], post_g):
        grads[name] = val

    (dr, dlw, dk2, dv, dan, dbn), early_from_chips = scan_bwd(
        r, lw, k2, v, an, bn, scan_saved, dy, (dr_c, dk_c, dv_c),
        early_hook[1](early_srcs, early_got[0]) if early_hook else ())

    def fn_pre_b(c, i, h, hp, dr, dlw, dk2, dv, dan, dbn, dg, e, et, *ws):
        real = (_row_ids(i, tms) >= PAD_FRONT).astype(F32)
        cot = tuple(t * real for t in (dr, dlw, dk2, dv, dan, dbn, dg))
        out, dws = vjp_taps(functools.partial(rwkv_pre, e, et, ws[n_vec:]), [(tms, n) for n in PRE_TAPS],
                            [h, hp] + list(ws[:n_vec]), cot)
        return out[:2], tuple(out[2:]) + tuple(dws)

    (dh0_p, dhp), pre_g = rowwise(
        "rwkv_pre_bwd", fn_pre_b, [h0, hp, dr, dlw, dk2, dv, dan, dbn, dg],
        [e, et] + pre_vec + pre_w, [(D_MODEL, F32)] * 2,
        [_acc((1, D_MODEL))] * n_vec + [_acc(w.shape) for w in pre_w], tms)
    grads["a_mu"] = jnp.concatenate(pre_g[:6], axis=0)
    for name, val in zip(["a_w0", "a_a0", "a_k_k", "a_k_a", "a_w_r", "a_w_k", "a_w_v", "a_w1", "a_w2", "a_a1",
                          "a_a2", "a_g1", "a_g2"], pre_g[6:]):
        grads[name] = val

    dhp_s = jnp.concatenate([dhp[1:], jnp.zeros((1, D_MODEL), F32)], axis=0)

    def fn_add(c, i, a, b, d):
        return (a + b + d,), ()

    (dh0,), _ = rowwise("grad_h0", fn_add, [dh0_c, dh0_p, dhp_s], [], [(D_MODEL, F32)], [], tm)
    grads["meta_tokens"] = dh0[PAD_FRONT:TOK0]
    return loss, dh0[TOK0:], grads, early_from_chips


ANY = pl.BlockSpec(memory_space=pl.ANY)
XY_FLIPS = ((0, 1), (1, 0), (1, 1))
ALL_FLIPS = tuple((e >> 2 & 1, e >> 1 & 1, e & 1) for e in range(1, N_DEV))


def _flip(v, bit):
    return 1 - v if bit else v


def _sem_scratch(n):
    return [pltpu.SemaphoreType.DMA((n,)), pltpu.SemaphoreType.DMA((n,))]


def gather_copies(src, dst, ici_send, ici_recv, d2d_send, d2d_recv):
    npeer = len(XY_FLIPS)
    x, y, c = lax.axis_index("x"), lax.axis_index("y"), lax.axis_index("c")

    def half(ref, k, which):
        h = src[k].shape[0] // 2
        start = which * h
        return ref.at[pl.ds(pl.multiple_of(start, 8) if h % 8 == 0 else start, h)]

    def ici(k, j, slot):
        fx, fy = XY_FLIPS[j]
        return pltpu.make_async_remote_copy(
            src_ref=half(src[k], k, c), dst_ref=half(dst[k].at[slot], k, c), send_sem=ici_send.at[k * npeer + j],
            recv_sem=ici_recv.at[k * npeer + j], device_id=(_flip(x, fx), _flip(y, fy), c), device_id_type=MESH)

    def d2d(k, j, which):
        fx, fy = XY_FLIPS[j]
        landed = half(dst[k].at[2 * _flip(x, fx) + _flip(y, fy)], k, which)
        return pltpu.make_async_remote_copy(
            src_ref=landed, dst_ref=landed, send_sem=d2d_send.at[k * npeer + j], recv_sem=d2d_recv.at[k * npeer + j],
            device_id=(x, y, 1 - c), device_id_type=MESH)

    pairs = [(k, j) for k in range(len(src)) for j in range(npeer)]
    return ([ici(k, j, 2 * x + y) for k, j in pairs],
            [ici(k, j, 2 * _flip(x, XY_FLIPS[j][0]) + _flip(y, XY_FLIPS[j][1])) for k, j in pairs],
            [d2d(k, j, c) for k, j in pairs], [d2d(k, j, 1 - c) for k, j in pairs])


def gather_scratch(n):
    return _sem_scratch(n * len(XY_FLIPS)) * 2


def gathered_shapes(shards):
    return [jax.ShapeDtypeStruct((N_SHARD,) + s.shape, s.dtype) for s in shards]


def fill_own(gathered, shards):
    if not shards:
        return []
    slot = 2 * lax.axis_index("x") + lax.axis_index("y")
    return [lax.dynamic_update_index_in_dim(g, s, slot, 0) for g, s in zip(gathered, shards)]


def all_gather_shards(shards):
    n = len(shards)

    def body(*refs):
        sends, arrivals, forwards, forwarded = gather_copies(refs[:n], refs[n:2 * n], *refs[2 * n:])
        for cp in sends:
            cp.start()
        for landed, onward in zip(arrivals, forwards):
            landed.wait_recv()
            onward.start()
        for cp in forwarded:
            cp.wait_recv()
        for cp in sends + forwards:
            cp.wait_send()

    out = pl.pallas_call(body, name="gather_weights", in_specs=[ANY] * n, out_specs=[ANY] * n,
                         out_shape=gathered_shapes(shards), scratch_shapes=gather_scratch(n))(*shards)
    return fill_own(out, shards)


def placement():
    x, y, c = lax.axis_index("x"), lax.axis_index("y"), lax.axis_index("c")
    me = 2 * x + y
    others = [j + (j >= me).astype(jnp.int32) for j in range(N_SHARD - 1)]
    return jnp.stack([c, me] + others).astype(jnp.int32)


def hosted_gather(shards):
    return (gather_copies, list(shards), gathered_shapes(shards), gather_scratch(len(shards)),
            lambda got: fill_own(got, shards))


def pair_exchange_copies(src, got, send_sems, recv_sems):
    x, y, c = lax.axis_index("x"), lax.axis_index("y"), lax.axis_index("c")

    def copy(k):
        half = src[k].shape[1] // 2
        theirs = src[k].at[:, pl.ds(pl.multiple_of((1 - c) * half, 8), half), :]
        return pltpu.make_async_remote_copy(
            src_ref=theirs, dst_ref=got[k], send_sem=send_sems.at[k], recv_sem=recv_sems.at[k],
            device_id=(x, y, 1 - c), device_id_type=MESH)

    sends = [copy(k) for k in range(len(src))]
    return sends, sends, [], []


def _half_shapes(sources):
    return [jax.ShapeDtypeStruct((s.shape[0], s.shape[1] // 2, s.shape[2]), s.dtype) for s in sources]


def hosted_pair_exchange(sources):
    return (pair_exchange_copies, list(sources), _half_shapes(sources), _sem_scratch(len(sources)), list)


def pair_exchange(name, sources):
    n = len(sources)

    def body(*refs):
        sends, arrivals, _, _ = pair_exchange_copies(refs[:n], refs[n:2 * n], *refs[2 * n:])
        for cp in sends:
            cp.start()
        for cp in arrivals:
            cp.wait_recv()
        for cp in sends:
            cp.wait_send()

    halves = _half_shapes(sources)
    return pl.pallas_call(body, name=name, in_specs=[ANY] * n, out_specs=[ANY] * n,
                          out_shape=halves, scratch_shapes=_sem_scratch(n))(*sources)


def chip_exchange(parts):
    n = len(parts)

    def body(*refs):
        sends, arrivals = chip_exchange_copies(refs[:n], refs[n:2 * n], *refs[2 * n:])
        for cp in sends:
            cp.start()
        for cp in arrivals:
            cp.wait_recv()
        for cp in sends:
            cp.wait_send()

    return pl.pallas_call(
        body, name="grads_chip_exchange", in_specs=[ANY] * n, out_specs=[ANY] * n,
        out_shape=[jax.ShapeDtypeStruct(p.shape, p.dtype) for p in parts],
        scratch_shapes=_sem_scratch(n * len(XY_FLIPS)),
    )(*parts)


def chip_exchange_copies(src, dst, send_sems, recv_sems):
    npeer = len(XY_FLIPS)
    x, y, c = lax.axis_index("x"), lax.axis_index("y"), lax.axis_index("c")
    me = 2 * x + y

    def copy(k, j, sending):
        fx, fy = XY_FLIPS[j]
        px, py = _flip(x, fx), _flip(y, fy)
        peer = 2 * px + py
        return pltpu.make_async_remote_copy(
            src_ref=src[k].at[peer], dst_ref=dst[k].at[me if sending else peer],
            send_sem=send_sems.at[k * npeer + j], recv_sem=recv_sems.at[k * npeer + j],
            device_id=(px, py, c), device_id_type=MESH)

    pairs = [(k, j) for k in range(len(src)) for j in range(npeer)]
    return [copy(k, j, True) for k, j in pairs], [copy(k, j, False) for k, j in pairs]


def sibling_share(halves):
    n = len(halves)

    def body(*refs):
        src, got = refs[:n], refs[n:2 * n]
        send_sems, recv_sems = refs[2 * n:]
        x, y, c = lax.axis_index("x"), lax.axis_index("y"), lax.axis_index("c")
        sends = [pltpu.make_async_remote_copy(
            src_ref=src[k], dst_ref=got[k], send_sem=send_sems.at[k], recv_sem=recv_sems.at[k],
            device_id=(x, y, 1 - c), device_id_type=MESH) for k in range(n)]
        for cp in sends:
            cp.start()
        for cp in sends:
            cp.wait_recv()
        for cp in sends:
            cp.wait_send()

    return pl.pallas_call(
        body, name="grads_sibling_share", in_specs=[ANY] * n, out_specs=[ANY] * n,
        out_shape=[jax.ShapeDtypeStruct(h.shape, h.dtype) for h in halves], scratch_shapes=_sem_scratch(n),
    )(*halves)


ADD_TILE_ELEMS = 512 * 1024


def _row_tile(rows, cols):
    return max(t for t in range(8, rows + 1, 8) if rows % t == 0 and t * cols <= ADD_TILE_ELEMS)


def _prefetch_call(body, name, place, grid, in_specs, out_specs, out_shape, args):
    return pl.pallas_call(
        body, name=name, out_shape=out_shape,
        grid_spec=pltpu.PrefetchScalarGridSpec(num_scalar_prefetch=1, grid=grid, in_specs=in_specs,
                                               out_specs=out_specs),
        compiler_params=pltpu.CompilerParams(dimension_semantics=("arbitrary",) * len(grid),
                                             vmem_limit_bytes=VMEM_LIMIT),
    )(place, *args)


def pair_add(name, place, src, got, dtype):
    n4, half, cols = got.shape
    tile = _row_tile(half, cols)
    nt = half // tile

    def body(pr, a_ref, b_ref, o_ref):
        o_ref[...] = (a_ref[...] + b_ref[...]).astype(o_ref.dtype)

    mine = pl.BlockSpec((None, tile, cols), lambda s, i, pr: (s, pr[0] * nt + i, 0))
    blk = pl.BlockSpec((None, tile, cols), lambda s, i, pr: (s, i, 0))
    return _prefetch_call(body, name, place, (n4, nt), [mine, blk], blk,
                          jax.ShapeDtypeStruct(got.shape, dtype), (src, got))


def chip_add(name, place, part, from_chips):
    _, half, cols = part.shape
    tile = _row_tile(half, cols)

    def body(pr, own_ref, r0_ref, r1_ref, r2_ref, o_ref):
        me = pr[1]
        own, r0, r1, r2 = (r[...].astype(F32) for r in (own_ref, r0_ref, r1_ref, r2_ref))
        t0 = jnp.where(me == 0, own, r0)
        t1 = jnp.where(me == 0, r0, jnp.where(me == 1, own, r1))
        t2 = jnp.where(me <= 1, r1, jnp.where(me == 2, own, r2))
        t3 = jnp.where(me == 3, own, r2)
        o_ref[...] = ((t0 + t1) + t2) + t3

    def slab(j):
        return pl.BlockSpec((None, tile, cols), lambda i, pr: (pr[j], i, 0))

    return _prefetch_call(body, name, place, (half // tile,), [slab(1), slab(2), slab(3), slab(4)],
                          pl.BlockSpec((tile, cols), lambda i, pr: (i, 0)),
                          jax.ShapeDtypeStruct((half, cols), F32), (part, from_chips, from_chips, from_chips))


def pair_adds(tag, place, sources, got, narrow):
    return [pair_add(f"grads_pair_add_{tag}{k}", place, s, g, BF16 if nar else F32)
            for k, (s, g, nar) in enumerate(zip(sources, got, narrow))]


def finish_sums(place, parts, from_chips):
    halves = [chip_add(f"grads_chip_add{k}", place, p, f) for k, (p, f) in enumerate(zip(parts, from_chips))]
    return list(zip(halves, sibling_share(halves)))


ADAM_ROWS = 256


def adamw_update(name, place, halves, w, m, v):
    nsub, rows, cols = w.shape
    half = rows // 2
    tr = ADAM_ROWS if half % ADAM_ROWS == 0 else half
    nth = half // tr

    def body(pr, *refs):
        g_refs, (w_ref, m_ref, v_ref, g_ref, d_ref, nm_ref, nv_ref) = refs[:2 * nsub], refs[2 * nsub:]
        l = pl.program_id(0)
        mine = (pl.program_id(1) // nth) == pr[0]
        g = None
        for s in range(nsub):
            gs = jnp.where(mine, g_refs[2 * s][...], g_refs[2 * s + 1][...])
            g = gs if g is None else jnp.where(l == s, gs, g)
        m2 = ADAM_B1 * m_ref[...] + (1.0 - ADAM_B1) * g
        v2 = ADAM_B2 * v_ref[...] + (1.0 - ADAM_B2) * (g * g)
        m_hat = m2 / (1.0 - ADAM_B1 ** ADAM_STEP)
        v_hat = v2 / (1.0 - ADAM_B2 ** ADAM_STEP)
        g_ref[...] = g
        d_ref[...] = -ADAM_LR * (m_hat / (jnp.sqrt(v_hat) + ADAM_EPS) + ADAM_WD * w_ref[...])
        nm_ref[...] = m2
        nv_ref[...] = v2

    gblk = pl.BlockSpec((tr, cols), lambda l, i, pr: (i % nth, 0))
    blk = pl.BlockSpec((None, tr, cols), lambda l, i, pr: (l, i, 0))
    out = jax.ShapeDtypeStruct((nsub, rows, cols), F32)
    return _prefetch_call(body, name, place, (nsub, rows // tr), [gblk] * (2 * nsub) + [blk] * 3, [blk] * 4,
                          [out] * 4, [h for pair in halves for h in pair] + [w, m, v])


WEIGHT_NAMES = ("meta_tokens", "a_mu", "a_w_r", "a_w_k", "a_w_v", "a_w_o", "a_w0", "a_w1", "a_w2", "a_a0", "a_a1",
                "a_a2", "a_g1", "a_g2", "a_k_k", "a_k_a", "a_r_k", "a_gn_w", "a_gn_b", "kv_w_k", "kv_w_v", "b_w_q",
                "b_sinks", "b_w_o", "mlp_w_up", "mlp_w_down", "ln_g", "ln_b")
BIG_NAMES = ("a_w_r", "a_w_k", "a_w_v", "a_w_o", "b_w_q", "b_w_o")
EARLY_NAMES, LATE_NAMES = BIG_NAMES[:3], BIG_NAMES[3:]
PACK_MATS = (("kv_w_k", 256), ("kv_w_v", 256), ("a_w1", 64), ("a_a1", 64), ("a_g1", 128), ("a_w2", 64),
             ("a_a2", 64), ("a_g2", 128))
COLUMN_CUT = ("a_w2", "a_a2", "a_g2")
PACK_VECS = (("a_mu", 6), ("a_w0", 1), ("a_a0", 1), ("a_k_k", 1), ("a_k_a", 1), ("a_gn_w", 1), ("a_gn_b", 1),
             ("ln_g", 4), ("ln_b", 4), ("meta_tokens", 16))
PACK_REPL = (("a_r_k", 4), ("b_sinks", 1))
SHARD_W = D_MODEL // N_SHARD
N_MAT_ROWS = sum(r for _, r in PACK_MATS)
N_VEC_ROWS = sum(r for _, r in PACK_VECS)
N_PACK_ROWS = -(-(N_MAT_ROWS + N_VEC_ROWS + sum(r for _, r in PACK_REPL)) // 8) * 8
N_GATHER_VEC_ROWS = -(-N_VEC_ROWS // 16) * 16


def _pack_rows(arr):
    if arr.size == N_HEADS:
        return jnp.pad(arr.reshape(1, N_HEADS), ((0, 0), (0, SHARD_W - N_HEADS)))
    return arr.reshape(-1, SHARD_W)


def pack_small(get):
    parts = [_pack_rows(get(name)) for name, _ in PACK_MATS + PACK_VECS + PACK_REPL]
    used = sum(p.shape[0] for p in parts)
    return jnp.concatenate(parts + [jnp.zeros((N_PACK_ROWS - used, SHARD_W), F32)], axis=0)


def unpack_small(pack, shapes):
    out, off = {}, 0
    for name, rows in PACK_MATS + PACK_VECS + PACK_REPL:
        piece = pack[off:off + rows]
        off += rows
        out[name] = piece[:, :N_HEADS].reshape(shapes[name]) if name == "b_sinks" else piece.reshape(shapes[name])
    return out


def whole_weights(big_names, gathered_big, mats, vecs, a_r_k, b_sinks):
    p = {name: g.reshape(D_MODEL, D_MODEL) for name, g in zip(big_names, gathered_big)}
    off = 0
    for name, rows in PACK_MATS:
        piece = mats[:, off:off + rows]
        off += rows
        if name in COLUMN_CUT:
            p[name] = piece.transpose(1, 0, 2).reshape(rows, D_MODEL)
        else:
            p[name] = piece.reshape(D_MODEL, rows)
    v = vecs.transpose(1, 0, 2).reshape(-1, D_MODEL)
    off = 0
    for name, rows in PACK_VECS:
        p[name] = v[off:off + rows]
        off += rows
    for i in range(2):
        for j in range(2):
            p[f"ln_g{i}{j}"] = p["ln_g"][2 * i + j:2 * i + j + 1]
            p[f"ln_b{i}{j}"] = p["ln_b"][2 * i + j:2 * i + j + 1]
    p["a_r_k"] = a_r_k.reshape(1, D_MODEL)
    p["b_sinks"] = b_sinks
    return p


def small_grad_pack(g):
    parts = []
    for name, rows in PACK_MATS:
        if name in COLUMN_CUT:
            parts.append(g[name].reshape(rows, N_SHARD, SHARD_W).transpose(1, 0, 2))
        else:
            parts.append(g[name].reshape(N_SHARD, rows, SHARD_W))
    vec_rows = [g["a_mu"]] + [g[n] for n in ("a_w0", "a_a0", "a_k_k", "a_k_a", "a_gn_w", "a_gn_b")]
    vec_rows += [g[f"ln_g{i}{j}"] for i in range(2) for j in range(2)]
    vec_rows += [g[f"ln_b{i}{j}"] for i in range(2) for j in range(2)] + [g["meta_tokens"]]
    parts.append(jnp.concatenate(vec_rows, axis=0).reshape(N_VEC_ROWS, N_SHARD, SHARD_W).transpose(1, 0, 2))
    parts.append(jnp.broadcast_to(g["a_r_k"].reshape(1, -1, SHARD_W), (N_SHARD, D_MODEL // SHARD_W, SHARD_W)))
    sinks = jnp.pad(g["b_sinks"].reshape(1, 1, N_HEADS), ((0, 0), (0, 0), (0, SHARD_W - N_HEADS)))
    parts.append(jnp.broadcast_to(sinks, (N_SHARD, 1, SHARD_W)))
    used = sum(p.shape[1] for p in parts)
    parts.append(jnp.zeros((N_SHARD, N_PACK_ROWS - used, SHARD_W), F32))
    return jnp.concatenate(parts, axis=1)


def train_step(vals):
    w = {n: vals[n] for n in WEIGHT_NAMES}
    w_pack = pack_small(lambda n: w[n])
    early = [w[n][0].astype(BF16) for n in EARLY_NAMES]
    early += [w_pack[:N_MAT_ROWS].astype(BF16), w_pack[N_MAT_ROWS:N_MAT_ROWS + N_GATHER_VEC_ROWS]]
    gathered = all_gather_shards(early)
    ne = len(EARLY_NAMES)
    p = whole_weights(EARLY_NAMES, gathered[:ne], gathered[ne], gathered[ne + 1][:, :N_VEC_ROWS], w["a_r_k"],
                      w["b_sinks"])
    nb = len(BIG_NAMES)

    def late_set(big, layer):
        shards = [w[n][0].astype(BF16) for n in big]
        shards += [w["mlp_w_up"][layer].astype(BF16), w["mlp_w_down"][layer].astype(BF16)]

        def weights(got):
            out = {n: x.reshape(D_MODEL, D_MODEL) for n, x in zip(big, got)}
            out[f"mlp_up{layer}"], out[f"mlp_down{layer}"] = got[len(big):]
            return out

        return shards, weights

    late = (late_set((), 1), late_set(LATE_NAMES, 0))

    place = placement()
    ready = {}
    a_names, b_names = BIG_NAMES[:4], BIG_NAMES[4:]

    def early_sources(g):
        return ([g[n].reshape(N_SHARD, SHARD_W, D_MODEL) for n in b_names]
                + [g["mlp_up0"], g["mlp_up1"], g["mlp_down0"], g["mlp_down1"]])

    def early_parts(srcs, got):
        ready["parts"] = pair_adds("early", place, srcs, got, [True] * len(srcs))
        return ready["parts"]

    loss, gx, g, early_from_chips = local_step(vals["x"][0], vals["loss_target"][0], p, late,
                                               (early_sources, early_parts))
    loss = lax.psum(loss, ("x", "y", "c"))
    srcs = [g[n].reshape(N_SHARD, SHARD_W, D_MODEL) for n in a_names] + [small_grad_pack(g)]
    rest = pair_adds("late", place, srcs, pair_exchange("grads_pair_exchange", srcs), [True] * len(a_names) + [False])
    rest_from_chips = chip_exchange(rest)
    na = len(a_names)
    halves = finish_sums(place, rest[:na] + ready["parts"] + rest[na:],
                         list(rest_from_chips[:na]) + list(early_from_chips) + list(rest_from_chips[na:]))

    res = {}
    for k, n in enumerate(BIG_NAMES):
        res[n] = adamw_update("adamw_" + n, place, halves[k:k + 1], w[n], vals["m_" + n], vals["v_" + n])
    for k, n in ((nb, "mlp_w_up"), (nb + 2, "mlp_w_down")):
        res[n] = adamw_update("adamw_" + n, place, halves[k:k + 2], w[n], vals["m_" + n], vals["v_" + n])
    packs = adamw_update("adamw_small", place, halves[-1:], w_pack[None], pack_small(lambda n: vals["m_" + n])[None],
                         pack_small(lambda n: vals["v_" + n])[None])
    shapes = {n: w[n].shape for n in WEIGHT_NAMES}
    small = [unpack_small(pk[0], shapes) for pk in packs]
    outs = [loss, gx[None]]
    for t in range(4):
        outs += [res[n][t] if n in res else small[t][n] for n in WEIGHT_NAMES]
    return tuple(outs)


def kernel(x, meta_tokens, a_mu, a_w_r, a_w_k, a_w_v, a_w_o, a_w0, a_w1, a_w2, a_a0, a_a1, a_a2, a_g1, a_g2, a_k_k,
           a_k_a, a_r_k, a_gn_w, a_gn_b, kv_w_k, kv_w_v, b_w_q, b_sinks, b_w_o, mlp_w_up, mlp_w_down, ln_g, ln_b,
           loss_target, m_meta_tokens, m_a_mu, m_a_w_r, m_a_w_k, m_a_w_v, m_a_w_o, m_a_w0, m_a_w1, m_a_w2, m_a_a0,
           m_a_a1, m_a_a2, m_a_g1, m_a_g2, m_a_k_k, m_a_k_a, m_a_r_k, m_a_gn_w, m_a_gn_b, m_kv_w_k, m_kv_w_v,
           m_b_w_q, m_b_sinks, m_b_w_o, m_mlp_w_up, m_mlp_w_down, m_ln_g, m_ln_b, v_meta_tokens, v_a_mu, v_a_w_r,
           v_a_w_k, v_a_w_v, v_a_w_o, v_a_w0, v_a_w1, v_a_w2, v_a_a0, v_a_a1, v_a_a2, v_a_g1, v_a_g2, v_a_k_k,
           v_a_k_a, v_a_r_k, v_a_gn_w, v_a_gn_b, v_kv_w_k, v_kv_w_v, v_b_w_q, v_b_sinks, v_b_w_o, v_mlp_w_up,
           v_mlp_w_down, v_ln_g, v_ln_b):
    return train_step(dict(locals()))
```

```python
import functools

import numpy as np
import jax
import jax.numpy as jnp
from jax import lax
from jax.experimental import pallas as pl
from jax.experimental.pallas import tpu as pltpu

F32 = jnp.float32
BF16 = jnp.bfloat16

D_MODEL = 1024
N_HEADS = 16
HEAD_DIM = 64
N_HEADS_KV = 4
GROUP = 4
KV_DIM = N_HEADS_KV * HEAD_DIM
N_META = 16
BLOCK = 128
PAD_FRONT = BLOCK - N_META
TOK0 = PAD_FRONT + N_META
N_FF_CHUNK = 4
N_SHARD = 4
N_DEV = 8
GN_EPS = 64e-5
LN_EPS = 1e-5
ROPE_THETA = 10000.0
ALPHA = 4.0 ** 0.25
ADAM_LR, ADAM_B1, ADAM_B2, ADAM_EPS, ADAM_WD, ADAM_STEP = 0.001, 0.9, 0.999, 1e-08, 0.01, 10
SCAN_T = 64
PAIR = 128
KVW = GROUP * HEAD_DIM
VMEM_LIMIT = 60 * 1024 * 1024
HI = lax.Precision.HIGHEST
MESH = pl.DeviceIdType.MESH


def _dot(a, b, ca, cb):
    return lax.dot_general(a.astype(BF16), b.astype(BF16), (((ca,), (cb,)), ((), ())),
                           preferred_element_type=F32)


@jax.custom_vjp
def mm(a, b):
    return _dot(a, b, 1, 0)


def _mm_fwd(a, b):
    return mm(a, b), b


def _mm_bwd(b, g):
    return _dot(g, b, 1, 1), jnp.zeros_like(b)


mm.defvjp(_mm_fwd, _mm_bwd)


@jax.custom_vjp
def mm_tap(a, b, tap):
    return _dot(a, b, 1, 0)


mm_tap.defvjp(lambda a, b, tap: (_dot(a, b, 1, 0), b), lambda b, g: (_dot(g, b, 1, 1), jnp.zeros_like(b), g))


def tmm(x, w, taps, xs):
    y = mm(x, w) if taps is None else mm_tap(x, w, taps[len(xs)])
    xs.append(x)
    return y


def vjp_taps(core, tap_shapes, args, cot):
    taps = [jnp.zeros(s, F32) for s in tap_shapes]
    _, vjp, xs = jax.vjp(core, taps, *args, has_aux=True)
    out = vjp(cot)
    return out[1:], [_dot(x, g, 0, 0) for x, g in zip(xs, out[0])]


def _split3(x):
    x1 = x.astype(BF16)
    r1 = x - x1.astype(F32)
    x2 = r1.astype(BF16)
    x3 = (r1 - x2.astype(F32)).astype(BF16)
    return x1, x2, x3


def _exact_dot(x, m01, cb=0):
    acc = None
    for piece in _split3(x):
        t = lax.dot_general(piece, m01, (((1,), (cb,)), ((), ())), preferred_element_type=F32)
        acc = t if acc is None else acc + t
    return acc


def _head_matrices():
    e = np.zeros((D_MODEL, N_HEADS), np.float32)
    e[np.arange(D_MODEL), np.arange(D_MODEL) // HEAD_DIM] = 1.0
    return jnp.asarray(e, BF16), jnp.asarray(e.T, BF16)


@jax.custom_vjp
def hsum(x, e, et):
    return _exact_dot(x, e)


@jax.custom_vjp
def hbc(s, e, et):
    return _exact_dot(s, et)


hsum.defvjp(lambda x, e, et: (_exact_dot(x, e), (e, et)),
            lambda res, g: (hbc(g, *res), jnp.zeros_like(res[0]), jnp.zeros_like(res[1])))
hbc.defvjp(lambda s, e, et: (_exact_dot(s, et), (e, et)),
           lambda res, g: (hsum(g, *res), jnp.zeros_like(res[0]), jnp.zeros_like(res[1])))


def _sigmoid(u):
    return 0.5 * (jnp.tanh(0.5 * u) + 1.0)


def _softplus(u):
    return jnp.maximum(u, 0.0) + jnp.log(1.0 + jnp.exp(-jnp.abs(u)))


def _layer_norm(z, g, b):
    mu = jnp.mean(z, axis=-1, keepdims=True)
    zc = z - mu
    var = jnp.mean(zc * zc, axis=-1, keepdims=True)
    return zc * lax.rsqrt(var + LN_EPS) * g + b


def _zero_map(nd):
    return lambda c, i: (0,) * nd


def _params():
    return pltpu.CompilerParams(dimension_semantics=("arbitrary", "arbitrary"), vmem_limit_bytes=VMEM_LIMIT)


def rowwise(name, fn, rows, consts, out_rows, out_accs, tm, nc=1, hosted=None):
    lp = rows[0].shape[-2]
    nt = lp // tm
    assert nt * tm == lp, (name, lp, tm)
    copies_fn, hosted_src, hosted_shapes, hosted_scratch, hosted_post = hosted or (None, (), [], [], None)
    ng = len(hosted_src)
    in_specs, args = [], []
    for a in rows:
        if isinstance(a, tuple):
            a, block_rows, block_index = a
            in_specs.append(pl.BlockSpec((block_rows, a.shape[1]),
                                         functools.partial(lambda f, c, i: (f(i), 0), block_index)))
        elif a.ndim == 2:
            in_specs.append(pl.BlockSpec((tm, a.shape[1]), lambda c, i: (i, 0)))
        else:
            in_specs.append(pl.BlockSpec((a.shape[0], tm, a.shape[2]), lambda c, i: (0, i, 0)))
        args.append(a)
    for cst in consts:
        if isinstance(cst, tuple):
            arr, bs, im = cst
            in_specs.append(pl.BlockSpec(bs, im))
        else:
            arr = cst
            in_specs.append(pl.BlockSpec(arr.shape, _zero_map(arr.ndim), pipeline_mode=pl.Buffered(1)))
        args.append(arr)
    out_shape, out_specs, acc_per_chunk = [], [], []
    for spec in out_rows:
        if len(spec) == 3 and spec[2]:
            out_shape.append(jax.ShapeDtypeStruct((nc, lp, spec[0]), spec[1]))
            out_specs.append(pl.BlockSpec((None, tm, spec[0]), lambda c, i: (c, i, 0)))
        else:
            out_shape.append(jax.ShapeDtypeStruct((lp, spec[0]), spec[1]))
            out_specs.append(pl.BlockSpec((tm, spec[0]), lambda c, i: (i, 0)))
    for spec in out_accs:
        out_shape.append(jax.ShapeDtypeStruct(spec[0], spec[1]))
        if len(spec) == 4:
            out_specs.append(pl.BlockSpec(spec[2], spec[3]))
            acc_per_chunk.append(True)
        else:
            out_specs.append(pl.BlockSpec(spec[0], _zero_map(len(spec[0])), pipeline_mode=pl.Buffered(1)))
            acc_per_chunk.append(False)
    n_in, n_or, n_out = len(args), len(out_rows), len(out_shape)

    def body(*refs):
        c = pl.program_id(0)
        i = pl.program_id(1)
        if ng:
            src, dst = refs[n_in:n_in + ng], refs[n_in + ng + n_out:n_in + 2 * ng + n_out]
            sends, arrivals, forwards, forwarded = copies_fn(src, dst, *refs[n_in + 2 * ng + n_out:])

            @pl.when(jnp.logical_and(c == 0, i == 0))
            def _():
                for cp in sends:
                    cp.start()

        vals = [r[...] for r in refs[:n_in]]
        outs_r, outs_a = fn(c, i, *vals)
        out_refs = refs[n_in + ng:n_in + ng + n_out]
        for ref, val in zip(out_refs[:n_or], outs_r):
            ref[...] = val.astype(ref.dtype)
        for ref, val, per_chunk in zip(out_refs[n_or:], outs_a, acc_per_chunk):
            first = (i == 0) if per_chunk else jnp.logical_and(i == 0, c == 0)

            @pl.when(first)
            def _():
                ref[...] = val.astype(ref.dtype)

            @pl.when(jnp.logical_not(first))
            def _():
                ref[...] += val.astype(ref.dtype)

        if ng:
            @pl.when(jnp.logical_and(c == nc - 1, i == max(nt - 3, 0)))
            def _():
                for k, landed in enumerate(arrivals):
                    landed.wait_recv()
                    if forwards:
                        forwards[k].start()

            @pl.when(jnp.logical_and(c == nc - 1, i == nt - 1))
            def _():
                for cp in forwarded:
                    cp.wait_recv()
                for cp in sends + forwards:
                    cp.wait_send()

    outs = pl.pallas_call(body, name=name, grid=(nc, nt), in_specs=in_specs + [ANY] * ng,
                          out_specs=out_specs + [ANY] * ng, out_shape=out_shape + list(hosted_shapes),
                          scratch_shapes=list(hosted_scratch), compiler_params=_params())(*args, *hosted_src)
    if ng:
        return outs[:n_or], outs[n_or:n_out], hosted_post(outs[n_out:])
    return outs[:n_or], outs[n_or:]


def _row_ids(i, tm):
    return i * tm + lax.broadcasted_iota(jnp.int32, (tm, 1), 0)


SUBLANES = 8


def _halo_before(arr, tm):
    return (arr, SUBLANES, lambda i: jnp.maximum(i * (tm // SUBLANES) - 1, 0))


def _halo_after(arr, tm):
    last = arr.shape[0] // SUBLANES - 1
    return (arr, SUBLANES, lambda i: jnp.minimum((i + 1) * (tm // SUBLANES), last))


def _pick_row(block8, row):
    rows = lax.broadcasted_iota(jnp.int32, block8.shape, 0)
    return jnp.sum(jnp.where(rows == row, block8, 0.0), axis=0, keepdims=True)


def _shift_down(x, before8, i):
    rows = lax.broadcasted_iota(jnp.int32, x.shape, 0)
    top = _pick_row(before8, SUBLANES - 1) * (i > 0).astype(F32)
    return jnp.where(rows == 0, top, pltpu.roll(x, 1, 0))


def _shift_up(x, after8, i, nt):
    rows = lax.broadcasted_iota(jnp.int32, x.shape, 0)
    bottom = _pick_row(after8, 0) * (i < nt - 1).astype(F32)
    return jnp.where(rows == x.shape[0] - 1, bottom, pltpu.roll(x, x.shape[0] - 1, 0))


PRE_TAPS = (D_MODEL, D_MODEL, D_MODEL, 64, D_MODEL, 64, D_MODEL, 128, D_MODEL)


def rwkv_pre(e, et, ws, taps, h, hp, mu_r, mu_w, mu_k, mu_v, mu_a, mu_g, w0, a0, k_k, k_a):
    w_r, w_k, w_v, w1, w2, a1, a2, g1, g2 = ws
    xs = []
    xx = hp - h
    r = tmm(h + xx * mu_r, w_r, taps, xs)
    k = tmm(h + xx * mu_k, w_k, taps, xs)
    v = tmm(h + xx * mu_v, w_v, taps, xs)
    wraw = -_softplus(-(w0 + tmm(jnp.tanh(tmm(h + xx * mu_w, w1, taps, xs)), w2, taps, xs))) - 0.5
    lw = -jnp.exp(wraw)
    a = _sigmoid(a0 + tmm(tmm(h + xx * mu_a, a1, taps, xs), a2, taps, xs))
    g = tmm(_sigmoid(tmm(h + xx * mu_g, g1, taps, xs)), g2, taps, xs)
    kk = k * k_k
    ss = hsum(kk * kk, e, et)
    pos = ss > 0.0
    nrm = jnp.where(pos, jnp.sqrt(jnp.where(pos, ss, 1.0)), 0.0)
    kk = kk * hbc(1.0 / jnp.maximum(nrm, 1e-12), e, et)
    k2 = k * (1.0 + (a - 1.0) * k_a)
    return (r, lw, k2, v, -kk, kk * a, g), xs


def rwkv_post(e, et, w_o, taps, y, r, k2, v, g, h0, gn_w, gn_b, rk, lg, lb):
    xs = []
    inv_n = 1.0 / HEAD_DIM
    yc = y - hbc(hsum(y, e, et) * inv_n, e, et)
    yv = hsum(yc * yc, e, et) * inv_n
    yn = yc * hbc(lax.rsqrt(yv + GN_EPS), e, et) * gn_w + gn_b
    bonus = hbc(hsum(r * k2 * rk, e, et), e, et) * v
    mix = tmm((yn + bonus) * g, w_o, taps, xs)
    return _layer_norm(ALPHA * h0 + mix, lg, lb), xs


@jax.custom_vjp
def sq_relu(x):
    r = jnp.maximum(x, 0.0)
    return r * r


sq_relu.defvjp(lambda x: (sq_relu(x), x), lambda x, g: (g * (2.0 * jnp.maximum(x, 0.0)),))


def mlp_chunk(wup, wdown, taps, h):
    xs = []
    return tmm(sq_relu(tmm(h, wup, taps, xs)), wdown, taps, xs), xs


def _rot_half(t):
    n = t.shape[-1]
    lane = lax.broadcasted_iota(jnp.int32, t.shape, t.ndim - 1)
    lo = (lane % HEAD_DIM) < (HEAD_DIM // 2)
    return jnp.where(lo, -pltpu.roll(t, n - HEAD_DIM // 2, t.ndim - 1), pltpu.roll(t, HEAD_DIM // 2, t.ndim - 1))


@jax.custom_vjp
def rot_half(t):
    return _rot_half(t)


rot_half.defvjp(lambda t: (_rot_half(t), None), lambda _, g: (-_rot_half(g),))


def _tile_lanes(t, width):
    return jnp.concatenate([t] * (width // t.shape[-1]), axis=-1)


def qkv_proj(cos, sin, wq, wk, wv, taps, h):
    xs = []
    q = tmm(h, wq, taps, xs)
    k = tmm(h, wk, taps, xs)
    v = tmm(h, wv, taps, xs)
    cq, sq = _tile_lanes(cos, D_MODEL), _tile_lanes(sin, D_MODEL)
    ck, sk = _tile_lanes(cos, KV_DIM), _tile_lanes(sin, KV_DIM)
    return (q * cq + rot_half(q) * sq, k * ck + rot_half(k) * sk, v), xs


def attn_out(w_o, taps, o, h, lg, lb):
    xs = []
    return _layer_norm(ALPHA * h + tmm(o, w_o, taps, xs), lg, lb), xs


def _scan_consts():
    t = SCAN_T
    tri = np.tril(np.ones((t, t), np.float32))
    rows = np.arange(2 * t)
    same = (rows[:, None] // t) == (rows[None, :] // t)
    strict = same & ((rows[None, :] % t) < (rows[:, None] % t))
    incl = same & ((rows[None, :] % t) <= (rows[:, None] % t))
    lane = np.arange(PAIR)
    masks = np.zeros((8, PAIR), np.float32)
    masks[0] = (lane // HEAD_DIM) == 0
    masks[1] = (lane // HEAD_DIM) == 1
    return (jnp.asarray(tri, BF16), jnp.asarray(strict.astype(np.float32)), jnp.asarray(incl.astype(np.float32)),
            jnp.asarray(masks), jnp.asarray(np.eye(2 * t, dtype=np.float32)))


def _scan_dot(a, b, ca, cb):
    return _dot(a, b, ca, cb)


@functools.partial(jax.custom_vjp, nondiff_argnums=(2, 3))
def _dotf(a, b, ca, cb):
    return _scan_dot(a, b, ca, cb)


def _dotf_bwd(ca, cb, res, g):
    a, b = res
    if ca == 1:
        da = _scan_dot(g, b, 1, 1 - cb)
    else:
        da = _scan_dot(b, g, 1 - cb, 1)
    if cb == 0:
        db = _scan_dot(a, g, 1 - ca, 0)
    else:
        db = _scan_dot(g, a, 0, 1 - ca)
    return da, db


_dotf.defvjp(lambda a, b, ca, cb: (_scan_dot(a, b, ca, cb), (a, b)), _dotf_bwd)


def _tri_dot(tri, x, ct):
    acc = None
    for piece in _split3(x):
        t = lax.dot_general(tri, piece, (((ct,), (0,)), ((), ())), preferred_element_type=F32)
        acc = t if acc is None else acc + t
    return acc


@jax.custom_vjp
def _cumsum_rows(tri, x):
    return _tri_dot(tri, x, 1)


_cumsum_rows.defvjp(lambda tri, x: (_tri_dot(tri, x, 1), tri),
                    lambda tri, g: (jnp.zeros_like(tri), _tri_dot(tri, g, 0)))


@jax.custom_vjp
def _unstack2(x):
    t = x.shape[0] // 2
    return x[:t] + x[t:]


_unstack2.defvjp(lambda x: (_unstack2(x), None), lambda _, g: (jnp.concatenate([g, g], axis=0),))


@jax.custom_vjp
def _last_row(x):
    return x[x.shape[0] - 1:, :]


def _last_row_bwd(_, g):
    rows = lax.broadcasted_iota(jnp.int32, (SCAN_T, g.shape[1]), 0)
    return (jnp.where(rows == SCAN_T - 1, jnp.broadcast_to(g, (SCAN_T, g.shape[1])), 0.0),)


_last_row.defvjp(lambda x: (_last_row(x), None), _last_row_bwd)


@jax.custom_vjp
def _solve_saved(n, rhs, minv, u):
    return u


def _solve_saved_bwd(res, du):
    minv, u = res
    drhs = _dotf(minv, du, 0, 0)
    return _dotf(drhs, u, 1, 1), drhs, jnp.zeros_like(minv), jnp.zeros_like(u)


_solve_saved.defvjp(lambda n, rhs, minv, u: (u, (minv, u)), _solve_saved_bwd)


def scan_chunk(tri, strict, incl, m0, m1, eye, r, lw, k, v, a, b, s0, saved=None):
    lower = strict > 0
    lower_incl = incl > 0

    def stack(x):
        return jnp.concatenate([x * m0, x * m1], axis=0)

    def dots(xs, ys, ca, cb, mask=None):
        out = [_dotf(x, y, ca, cb) for x, y in zip(xs, ys)]
        return out if mask is None else [jnp.where(mask, o, 0.0) for o in out]

    cl = [_cumsum_rows(tri, x) for x in lw]
    gam = [jnp.exp(c) for c in cl]
    ginv = [jnp.exp(-c) for c in cl]
    a_s = [stack(x * jnp.exp(c - w)) for x, c, w in zip(a, cl, lw)]
    r_s = [stack(x * g) for x, g in zip(r, gam)]
    b_s = [stack(x * g) for x, g in zip(b, ginv)]
    k_s = [stack(x * g) for x, g in zip(k, ginv)]
    v_s = [stack(x) for x in v]
    n_ab = dots(a_s, b_s, 1, 1, lower)
    n_ak = dots(a_s, k_s, 1, 1, lower)
    r_ab = dots(r_s, b_s, 1, 1, lower_incl)
    r_ak = dots(r_s, k_s, 1, 1, lower_incl)
    rhs = [x + y for x, y in zip(dots(a_s, s0, 1, 1), dots(n_ak, v_s, 1, 0))]
    if saved is None:
        minv = [eye + n for n in n_ab]
        p = n_ab
        for _ in range(5):
            p = dots(p, p, 1, 0)
            minv = [m + mp for m, mp in zip(minv, dots(minv, p, 1, 0))]
        u_s = dots(minv, rhs, 1, 0)
    else:
        minv = saved[0]
        u_s = [_solve_saved(n, x, m, u) for n, x, m, u in zip(n_ab, rhs, *saved)]
    y = [_unstack2(x0 + x1 + x2)
         for x0, x1, x2 in zip(dots(r_s, s0, 1, 1), dots(r_ab, u_s, 1, 0), dots(r_ak, v_s, 1, 0))]
    g_end = [_last_row(g) for g in gam]
    s1 = [s * g + x + z for s, g, x, z in zip(s0, g_end, dots(u_s, [x * g for x, g in zip(b_s, g_end)], 0, 0),
                                              dots(v_s, [x * g for x, g in zip(k_s, g_end)], 0, 0))]
    return y, s1, (minv, u_s)


SCAN_PAIRS = 8


def _scan_specs(consts, order):
    row = pl.BlockSpec((SCAN_T, PAIR * SCAN_PAIRS), lambda p, c: (order(c), p))
    state = pl.BlockSpec((None, SCAN_PAIRS, PAIR, PAIR), lambda p, c: (order(c), p, 0, 0))
    return row, state, [pl.BlockSpec(x.shape, _zero_map(x.ndim)) for x in consts]


def _pair_lanes(q):
    return slice(q * PAIR, (q + 1) * PAIR)


def scan_fwd(r, lw, k, v, a, b, shards=()):
    lp = r.shape[0]
    nch = lp // SCAN_T
    npair = D_MODEL // PAIR
    ng = len(shards)
    consts = _scan_consts()
    row, state, cspecs = _scan_specs(consts, lambda c: c)

    def body(tri, strict, incl, masks, eye, r_ref, lw_ref, k_ref, v_ref, a_ref, b_ref, *rest):
        src, (y_ref, s_ref, minv_ref, u_ref), dst = rest[:ng], rest[ng:ng + 4], rest[ng + 4:2 * ng + 4]
        carry = rest[2 * ng + 4]
        first = jnp.logical_and(pl.program_id(0) == 0, pl.program_id(1) == 0)
        last = jnp.logical_and(pl.program_id(0) == npair // SCAN_PAIRS - 1, pl.program_id(1) == nch - 1)
        if ng:
            sends, arrivals, forwards, forwarded = gather_copies(src, dst, *rest[2 * ng + 5:])

            @pl.when(first)
            def _():
                for cp in sends:
                    cp.start()

            @pl.when(jnp.logical_and(pl.program_id(0) == npair // SCAN_PAIRS - 1, pl.program_id(1) == nch * 3 // 4))
            def _():
                for landed, onward in zip(arrivals, forwards):
                    landed.wait_recv()
                    onward.start()

        @pl.when(pl.program_id(1) == 0)
        def _():
            carry[...] = jnp.zeros_like(carry)

        pairs = range(SCAN_PAIRS)
        s0 = [carry[q] for q in pairs]
        rows = [[ref[:, _pair_lanes(q)] for q in pairs] for ref in (r_ref, lw_ref, k_ref, v_ref, a_ref, b_ref)]
        y, s1, (minv, u) = scan_chunk(tri[...], strict[...], incl[...], masks[0:1, :], masks[1:2, :], eye[...],
                                      *rows, s0)
        for q in pairs:
            s_ref[q] = s0[q]
            minv_ref[q] = minv[q]
            u_ref[q] = u[q]
            y_ref[:, _pair_lanes(q)] = y[q]
            carry[q] = s1[q]

        if ng:
            @pl.when(last)
            def _():
                for cp in forwarded:
                    cp.wait_recv()
                for cp in sends + forwards:
                    cp.wait_send()

    mats = jax.ShapeDtypeStruct((nch, npair, PAIR, PAIR), F32)
    out = pl.pallas_call(
        body, name="rwkv_scan_fwd", grid=(npair // SCAN_PAIRS, nch), in_specs=cspecs + [row] * 6 + [ANY] * ng,
        out_specs=[row, state, state, state] + [ANY] * ng,
        out_shape=[jax.ShapeDtypeStruct((lp, D_MODEL), F32), mats, mats, mats] + gathered_shapes(shards),
        scratch_shapes=[pltpu.VMEM((SCAN_PAIRS, PAIR, PAIR), F32)] + (gather_scratch(ng) if ng else []),
        compiler_params=_params(),
    )(*consts, r, lw, k, v, a, b, *shards)
    return out[:4], fill_own(out[4:], shards)


def scan_bwd(r, lw, k, v, a, b, saved, dy, direct_grads, parts=()):
    lp = r.shape[0]
    nch = lp // SCAN_T
    npair = D_MODEL // PAIR
    consts = _scan_consts()
    row, state, cspecs = _scan_specs(consts, lambda c: nch - 1 - c)

    ng = len(parts)

    def body(tri, strict, incl, masks, eye, r_ref, lw_ref, k_ref, v_ref, a_ref, b_ref, s_ref, minv_ref, u_ref,
             dy_ref, dr_in, dk_in, dv_in, *rest):
        src, (dr_ref, dlw_ref, dk_ref, dv_ref, da_ref, db_ref), dst = rest[:ng], rest[ng:ng + 6], rest[ng + 6:2 * ng + 6]
        carry = rest[2 * ng + 6]
        first = jnp.logical_and(pl.program_id(0) == 0, pl.program_id(1) == 0)
        last = jnp.logical_and(pl.program_id(0) == npair // SCAN_PAIRS - 1, pl.program_id(1) == nch - 1)
        if ng:
            sends, arrivals = chip_exchange_copies(src, dst, *rest[2 * ng + 7:])

            @pl.when(first)
            def _():
                for cp in sends:
                    cp.start()

        @pl.when(pl.program_id(1) == 0)
        def _():
            carry[...] = jnp.zeros_like(carry)

        pairs = range(SCAN_PAIRS)
        kept = ([minv_ref[q] for q in pairs], [u_ref[q] for q in pairs])

        def fn(*args):
            y, s1, _ = scan_chunk(tri[...], strict[...], incl[...], masks[0:1, :], masks[1:2, :], eye[...], *args,
                                  saved=kept)
            return y, s1

        rows = [[ref[:, _pair_lanes(q)] for q in pairs] for ref in (r_ref, lw_ref, k_ref, v_ref, a_ref, b_ref)]
        _, vjp = jax.vjp(fn, *rows, [s_ref[q] for q in pairs])
        grads = vjp(([dy_ref[:, _pair_lanes(q)] for q in pairs], [carry[q] for q in pairs]))
        direct = (dr_in, None, dk_in, dv_in, None, None)
        for q in pairs:
            ln = _pair_lanes(q)
            for ref, g, extra in zip((dr_ref, dlw_ref, dk_ref, dv_ref, da_ref, db_ref), grads[:6], direct):
                ref[:, ln] = g[q] if extra is None else g[q] + extra[:, ln]
            carry[q] = grads[6][q]

        if ng:
            @pl.when(last)
            def _():
                for cp in arrivals:
                    cp.wait_recv()
                for cp in sends:
                    cp.wait_send()

    out = pl.pallas_call(
        body, name="rwkv_scan_bwd", grid=(npair // SCAN_PAIRS, nch),
        in_specs=cspecs + [row] * 6 + [state] * 3 + [row] * 4 + [ANY] * ng, out_specs=[row] * 6 + [ANY] * ng,
        out_shape=[jax.ShapeDtypeStruct((lp, D_MODEL), F32)] * 6 + [jax.ShapeDtypeStruct(p.shape, p.dtype) for p in parts],
        scratch_shapes=[pltpu.VMEM((SCAN_PAIRS, PAIR, PAIR), F32)] + (_sem_scratch(ng * len(XY_FLIPS)) if ng else []),
        compiler_params=_params(),
    )(*consts, r, lw, k, v, a, b, *saved, dy, *direct_grads, *parts)
    return out[:6], out[6:]


def _spread_matrices():
    rep = np.zeros((N_HEADS_KV, KV_DIM, KVW), np.float32)
    for h in range(N_HEADS_KV):
        for g in range(GROUP):
            rep[h, h * HEAD_DIM + np.arange(HEAD_DIM), g * HEAD_DIM + np.arange(HEAD_DIM)] = 1.0
    return jnp.asarray(rep, BF16)


KV_HEADS = range(N_HEADS_KV)


def _attn_common(n, q_ref, kp, kc, vp, vc, rep_ref, sink_ref):
    lane = lax.broadcasted_iota(jnp.int32, (1, KVW), 1)
    gmask = [(lane // HEAD_DIM == g).astype(F32) for g in range(GROUP)]
    kk = jnp.concatenate([kp, kc], axis=0)
    vv = jnp.concatenate([vp, vc], axis=0)
    qs = [q_ref[:, h * KVW:(h + 1) * KVW] for h in KV_HEADS]
    q_s = [jnp.concatenate([q * gmask[g] for g in range(GROUP)], axis=0) for q in qs]
    keys = [_dot(kk, rep_ref[h], 1, 0) for h in KV_HEADS]
    vals = [_dot(vv, rep_ref[h], 1, 0) for h in KV_HEADS]
    qi = lax.broadcasted_iota(jnp.int32, (GROUP * BLOCK, 2 * BLOCK), 0) % BLOCK
    kj = lax.broadcasted_iota(jnp.int32, (GROUP * BLOCK, 2 * BLOCK), 1)
    rel = BLOCK + qi - kj
    valid = (rel >= 0) & (rel < BLOCK) & ((n - 1) * BLOCK + kj >= PAD_FRONT)
    s = [jnp.where(valid, _dot(x, y, 1, 1) * (HEAD_DIM ** -0.5), -1e30) for x, y in zip(q_s, keys)]
    sink_col = [jnp.concatenate([jnp.broadcast_to(sink_ref[h, g:g + 1, 0:1], (BLOCK, 1)) for g in range(GROUP)],
                                axis=0) for h in KV_HEADS]
    m = [jnp.maximum(jnp.max(x, axis=-1, keepdims=True), c) for x, c in zip(s, sink_col)]
    ex = [jnp.exp(x - y) for x, y in zip(s, m)]
    ex_sink = [jnp.exp(c - y) for c, y in zip(sink_col, m)]
    inv = [1.0 / (jnp.sum(x, axis=-1, keepdims=True) + c) for x, c in zip(ex, ex_sink)]
    return (gmask, q_s, keys, vals, [x * y for x, y in zip(ex, inv)], [x * y for x, y in zip(ex_sink, inv)])


def _unstack_groups(x_s, gmask):
    out = None
    for g in range(GROUP):
        t = x_s[g * BLOCK:(g + 1) * BLOCK] * gmask[g]
        out = t if out is None else out + t
    return out


def _attn_specs():
    qspec = pl.BlockSpec((BLOCK, D_MODEL), lambda n: (n, 0))
    cur = pl.BlockSpec((BLOCK, KV_DIM), lambda n: (n, 0))
    prev = pl.BlockSpec((BLOCK, KV_DIM), lambda n: (jnp.maximum(n - 1, 0), 0))
    rep = pl.BlockSpec((N_HEADS_KV, KV_DIM, KVW), lambda n: (0, 0, 0))
    sink = pl.BlockSpec((N_HEADS_KV, 8, PAIR), lambda n: (0, 0, 0))
    return qspec, cur, prev, rep, sink


def _attn_params():
    return pltpu.CompilerParams(dimension_semantics=("arbitrary",), vmem_limit_bytes=VMEM_LIMIT)


def attn_fwd(q, k, v, sinks_b):
    lp = q.shape[0]
    qspec, cur, prev, rep, sink = _attn_specs()

    def body(q_ref, kp_ref, kc_ref, vp_ref, vc_ref, rep_ref, sink_ref, o_ref):
        gmask, _, _, vals, p, _ = _attn_common(pl.program_id(0), q_ref, kp_ref[...], kc_ref[...], vp_ref[...],
                                               vc_ref[...], rep_ref, sink_ref)
        o = [_dot(x, y, 1, 0) for x, y in zip(p, vals)]
        for h in KV_HEADS:
            o_ref[:, h * KVW:(h + 1) * KVW] = _unstack_groups(o[h], gmask)

    return pl.pallas_call(
        body, name="swa_fwd", grid=(lp // BLOCK,), in_specs=[qspec, prev, cur, prev, cur, rep, sink],
        out_specs=qspec, out_shape=jax.ShapeDtypeStruct((lp, D_MODEL), F32), compiler_params=_attn_params(),
    )(q, k, k, v, v, _spread_matrices(), sinks_b)


def attn_bwd(q, k, v, sinks_b, do):
    lp = q.shape[0]
    qspec, cur, prev, rep, sink = _attn_specs()

    def body(q_ref, kp_ref, kc_ref, vp_ref, vc_ref, rep_ref, sink_ref, do_ref, dq_ref, dkc_ref, dkp_ref, dvc_ref,
             dvp_ref, dsink_ref):
        n = pl.program_id(0)
        gmask, q_s, keys, vals, p, p_sink = _attn_common(n, q_ref, kp_ref[...], kc_ref[...], vp_ref[...], vc_ref[...],
                                                         rep_ref, sink_ref)
        do_s = [jnp.concatenate([do_ref[:, h * KVW:(h + 1) * KVW] * gmask[g] for g in range(GROUP)], axis=0)
                for h in KV_HEADS]
        dp = [_dot(x, y, 1, 1) for x, y in zip(do_s, vals)]
        delta = [jnp.sum(x * y, axis=-1, keepdims=True) for x, y in zip(p, dp)]
        ds = [x * (y - z) * (HEAD_DIM ** -0.5) for x, y, z in zip(p, dp, delta)]
        dq = [_dot(x, y, 1, 0) for x, y in zip(ds, keys)]
        dkeys_s = [_dot(x, y, 0, 0) for x, y in zip(ds, q_s)]
        dvals_s = [_dot(x, y, 0, 0) for x, y in zip(p, do_s)]
        dkeys = [_exact_dot(x, rep_ref[h], cb=1) for h, x in enumerate(dkeys_s)]
        dvals = [_exact_dot(x, rep_ref[h], cb=1) for h, x in enumerate(dvals_s)]
        dk_all = (dkeys[0] + dkeys[1]) + (dkeys[2] + dkeys[3])
        dv_all = (dvals[0] + dvals[1]) + (dvals[2] + dvals[3])
        dkp_ref[...] = dk_all[:BLOCK]
        dkc_ref[...] = dk_all[BLOCK:]
        dvp_ref[...] = dv_all[:BLOCK]
        dvc_ref[...] = dv_all[BLOCK:]
        dsinks = []
        for h in KV_HEADS:
            dq_ref[:, h * KVW:(h + 1) * KVW] = _unstack_groups(dq[h], gmask)
            dsk = -(p_sink[h] * delta[h])
            rows = [jnp.broadcast_to(jnp.sum(dsk[g * BLOCK:(g + 1) * BLOCK], axis=0, keepdims=True), (1, PAIR))
                    for g in range(GROUP)]
            dsinks.append(jnp.concatenate(rows + [jnp.zeros((8 - GROUP, PAIR), F32)], axis=0))

        @pl.when(n == 0)
        def _():
            for h in KV_HEADS:
                dsink_ref[h] = dsinks[h]

        @pl.when(n > 0)
        def _():
            for h in KV_HEADS:
                dsink_ref[h] += dsinks[h]

    kv = jax.ShapeDtypeStruct((lp, KV_DIM), F32)
    return pl.pallas_call(
        body, name="swa_bwd", grid=(lp // BLOCK,), in_specs=[qspec, prev, cur, prev, cur, rep, sink, qspec],
        out_specs=[qspec, cur, cur, cur, cur, sink],
        out_shape=[jax.ShapeDtypeStruct((lp, D_MODEL), F32), kv, kv, kv, kv,
                   jax.ShapeDtypeStruct((N_HEADS_KV, 8, PAIR), F32)],
        compiler_params=_attn_params(),
    )(q, k, k, v, v, _spread_matrices(), sinks_b, do)


def _pick_tm(lp, want):
    for tm in (384, 192, 128, 64):
        if tm <= want and lp % tm == 0:
            return tm
    raise ValueError(lp)


def _acc(shape):
    return (tuple(shape), F32)


def _ff_one(w):
    return (w, (None, D_MODEL, D_MODEL), lambda c, i: (c, 0, 0))


def _mlp_layer_fwd(name, h, wup, wdown, lg, lb, tm):
    def fn(c, i, h, wup, wdown, lg, lb):
        out = None
        for s in range(N_FF_CHUNK):
            t = mlp_chunk(wup[s], wdown[s], None, h)[0]
            out = t if out is None else out + t
        z = ALPHA * h + out
        return (_layer_norm(z, lg, lb), z), ()

    (h_out, z), _ = rowwise(name, fn, [h], [wup, wdown, lg, lb], [(D_MODEL, F32), (D_MODEL, F32)], [], tm)
    return h_out, z


def _mlp_layer_bwd(name, h_in, z, dh_parts, wup, wdown, lg, lb, tm):
    n_parts = len(dh_parts)

    def fn_ln(c, i, z, *rest):
        dh = rest[0]
        for extra in rest[1:n_parts]:
            dh = dh + extra
        _, vjp = jax.vjp(_layer_norm, z, rest[n_parts], rest[n_parts + 1])
        dz, dlg, dlb = vjp(dh)
        return (dz,), (dlg, dlb)

    (dz,), (dlg, dlb) = rowwise(name + "_ln", fn_ln, [z] + list(dh_parts), [lg, lb], [(D_MODEL, F32)],
                                [_acc((1, D_MODEL)), _acc((1, D_MODEL))], tm)

    def fn_mlp(c, i, h, dz, wup, wdown):
        tile = h.shape[0]
        (dx,), dws = vjp_taps(functools.partial(mlp_chunk, wup, wdown), [(tile, D_MODEL)] * 2, [h], dz)
        return (dx,), dws

    aspec = ((N_FF_CHUNK, D_MODEL, D_MODEL), F32, (None, D_MODEL, D_MODEL), lambda c, i: (c, 0, 0))
    (dx,), (dwup, dwdown) = rowwise(name + "_mm", fn_mlp, [h_in, dz], [_ff_one(wup), _ff_one(wdown)],
                                    [(D_MODEL, F32, True)], [aspec, aspec], tm, nc=N_FF_CHUNK)
    return dz, dx, dwup, dwdown, dlg, dlb


def _sum_parts(dz, dx):
    out = ALPHA * dz
    for s in range(N_FF_CHUNK):
        out = out + dx[s]
    return out


def local_step(x, loss_target, p, late=None, early_hook=None):
    seq = x.shape[0]
    lp = TOK0 + seq
    tm = _pick_tm(lp, 384)
    tms = _pick_tm(lp, 192)
    e, et = _head_matrices()
    h0 = jnp.concatenate([jnp.zeros((PAD_FRONT, D_MODEL), F32), p["meta_tokens"], x], axis=0)
    pos = jnp.maximum(jnp.arange(lp, dtype=F32) - PAD_FRONT, 0.0)
    inv_freq = 1.0 / (ROPE_THETA ** (jnp.arange(0, HEAD_DIM, 2, dtype=F32) / HEAD_DIM))
    ang = pos[:, None] * inv_freq[None, :]
    cos = jnp.tile(jnp.cos(ang), (1, PAIR // (HEAD_DIM // 2)))
    sin = jnp.tile(jnp.sin(ang), (1, PAIR // (HEAD_DIM // 2)))

    pre_vec = [p["a_mu"][j:j + 1] for j in range(6)] + [p["a_w0"], p["a_a0"], p["a_k_k"], p["a_k_a"]]
    pre_w = [p["a_w_r"], p["a_w_k"], p["a_w_v"], p["a_w1"], p["a_w2"], p["a_a1"], p["a_a2"], p["a_g1"], p["a_g2"]]
    n_vec = len(pre_vec)

    def fn_pre(c, i, h, before, e, et, *ws):
        return rwkv_pre(e, et, ws[n_vec:], None, h, _shift_down(h, before, i), *ws[:n_vec])[0], ()

    (r, lw, k2, v, an, bn, g), _, *pre_gathered = rowwise(
        "rwkv_pre", fn_pre, [h0, _halo_before(h0, tms)], [e, et] + pre_vec + pre_w, [(D_MODEL, F32)] * 7, [], tms,
        hosted=hosted_gather(late[0][0]) if late else None)
    (y, *scan_saved), scan_gathered = scan_fwd(r, lw, k2, v, an, bn, late[1][0] if late else ())
    if late:
        p = {**p, **late[0][1](pre_gathered[0]), **late[1][1](scan_gathered)}

    post_c = [p["a_w_o"], p["a_gn_w"], p["a_gn_b"], p["a_r_k"], p["ln_g00"], p["ln_b00"]]

    def fn_post(c, i, y, r, k2, v, g, h0, e, et, w_o, *vecs):
        return (rwkv_post(e, et, w_o, None, y, r, k2, v, g, h0, *vecs)[0],), ()

    (h1,), _ = rowwise("rwkv_post", fn_post, [y, r, k2, v, g, h0], [e, et] + post_c, [(D_MODEL, F32)], [], tm)
    h2, z2 = _mlp_layer_fwd("mlp0_fwd", h1, p["mlp_up0"], p["mlp_down0"], p["ln_g01"], p["ln_b01"], tm)

    qkv_w = [p["b_w_q"], p["kv_w_k"], p["kv_w_v"]]

    def fn_qkv(c, i, h, cos, sin, wq, wk, wv):
        return qkv_proj(cos, sin, wq, wk, wv, None, h)[0], ()

    (q, k, vv), _ = rowwise("qkv_proj", fn_qkv, [h2, cos, sin], qkv_w,
                            [(D_MODEL, F32), (KV_DIM, F32), (KV_DIM, F32)], [], tm)
    sinks_b = jnp.broadcast_to(p["b_sinks"].reshape(N_HEADS_KV, GROUP, 1), (N_HEADS_KV, GROUP, PAIR))
    sinks_b = jnp.concatenate([sinks_b, jnp.zeros((N_HEADS_KV, 8 - GROUP, PAIR), F32)], axis=1)
    o = attn_fwd(q, k, vv, sinks_b)

    ao_c = [p["b_w_o"], p["ln_g10"], p["ln_b10"]]

    def fn_ao(c, i, o, h, w_o, lg, lb):
        return (attn_out(w_o, None, o, h, lg, lb)[0],), ()

    (h3,), _ = rowwise("attn_out", fn_ao, [o, h2], ao_c, [(D_MODEL, F32)], [], tm)
    h4, z4 = _mlp_layer_fwd("mlp1_fwd", h3, p["mlp_up1"], p["mlp_down1"], p["ln_g11"], p["ln_b11"], tm)

    def fn_loss(c, i, h4, tgt):
        real = (_row_ids(i, TOK0) >= TOK0).astype(F32)
        err = (h4 - tgt) * real
        part = 0.5 * jnp.sum(jnp.sum(err * err, axis=-1, keepdims=True), axis=0, keepdims=True) / D_MODEL
        return (err * (1.0 / D_MODEL),), (jnp.broadcast_to(part, (8, PAIR)),)

    (dh4,), (loss_acc,) = rowwise("loss", fn_loss, [h4, (loss_target, TOK0, lambda i: jnp.maximum(i - 1, 0))], [],
                                  [(D_MODEL, F32)], [_acc((8, PAIR))], TOK0)
    loss = loss_acc[0, 0]

    grads = {}
    dz4, dx4, grads["mlp_up1"], grads["mlp_down1"], grads["ln_g11"], grads["ln_b11"] = _mlp_layer_bwd(
        "mlp1_bwd", h3, z4, [dh4], p["mlp_up1"], p["mlp_down1"], p["ln_g11"], p["ln_b11"], tm)

    def fn_ao_b(c, i, dz, dx, o, h, w_o, lg, lb):
        (do, dh, dlg, dlb), (dw_o,) = vjp_taps(functools.partial(attn_out, w_o), [(tm, D_MODEL)], [o, h, lg, lb],
                                               _sum_parts(dz, dx))
        return (do, dh), (dw_o, dlg, dlb)

    (do, dh2_a), (grads["b_w_o"], grads["ln_g10"], grads["ln_b10"]) = rowwise(
        "attn_out_bwd", fn_ao_b, [dz4, dx4, o, h2], ao_c, [(D_MODEL, F32)] * 2,
        [_acc((D_MODEL, D_MODEL)), _acc((1, D_MODEL)), _acc((1, D_MODEL))], tm)

    dq, dkc, dkp, dvc, dvp, dsinks = attn_bwd(q, k, vv, sinks_b, do)
    grads["b_sinks"] = dsinks[:, :GROUP, 0].reshape(1, N_HEADS)
    zblk = jnp.zeros((BLOCK, KV_DIM), F32)
    dkp_s = jnp.concatenate([dkp[BLOCK:], zblk], axis=0)
    dvp_s = jnp.concatenate([dvp[BLOCK:], zblk], axis=0)

    def fn_qkv_b(c, i, h, cos, sin, dq, dkc, dkp, dvc, dvp, wq, wk, wv):
        return vjp_taps(functools.partial(qkv_proj, cos, sin, wq, wk, wv),
                        [(tm, D_MODEL), (tm, KV_DIM), (tm, KV_DIM)], [h], (dq, dkc + dkp, dvc + dvp))

    (dh2_q,), (grads["b_w_q"], grads["kv_w_k"], grads["kv_w_v"]) = rowwise(
        "qkv_proj_bwd", fn_qkv_b, [h2, cos, sin, dq, dkc, dkp_s, dvc, dvp_s], qkv_w, [(D_MODEL, F32)],
        [_acc((D_MODEL, D_MODEL)), _acc((D_MODEL, KV_DIM)), _acc((D_MODEL, KV_DIM))], tm)

    dz2, dx2, grads["mlp_up0"], grads["mlp_down0"], grads["ln_g01"], grads["ln_b01"] = _mlp_layer_bwd(
        "mlp0_bwd", h1, z2, [dh2_a, dh2_q], p["mlp_up0"], p["mlp_down0"], p["ln_g01"], p["ln_b01"], tm)

    def fn_post_b(c, i, dz, dx, y, r, k2, v, g, h0, e, et, w_o, *vecs):
        out, dws = vjp_taps(functools.partial(rwkv_post, e, et, w_o), [(tms, D_MODEL)],
                            [y, r, k2, v, g, h0] + list(vecs), _sum_parts(dz, dx))
        return out[:6], tuple(dws) + tuple(out[6:])

    early_srcs = early_hook[0](grads) if early_hook else ()
    (dy, dr_c, dk_c, dv_c, dg, dh0_c), post_g, *early_got = rowwise(
        "rwkv_post_bwd", fn_post_b, [dz2, dx2, y, r, k2, v, g, h0], [e, et] + post_c, [(D_MODEL, F32)] * 6,
        [_acc((D_MODEL, D_MODEL))] + [_acc((1, D_MODEL))] * 5, tms,
        hosted=hosted_pair_exchange(early_srcs) if early_hook else None)
    for name, val in zip(["a_w_o", "a_gn_w", "a_gn_b", "a_r_k", "ln_g00", "ln_b00"], post_g):
        grads[name] = val

    (dr, dlw, dk2, dv, dan, dbn), early_from_chips = scan_bwd(
        r, lw, k2, v, an, bn, scan_saved, dy, (dr_c, dk_c, dv_c),
        early_hook[1](early_srcs, early_got[0]) if early_hook else ())

    def fn_pre_b(c, i, h, before, dr, dlw, dk2, dv, dan, dbn, dg, e, et, *ws):
        hp = _shift_down(h, before, i)
        real = (_row_ids(i, tms) >= PAD_FRONT).astype(F32)
        cot = tuple(t * real for t in (dr, dlw, dk2, dv, dan, dbn, dg))
        out, dws = vjp_taps(functools.partial(rwkv_pre, e, et, ws[n_vec:]), [(tms, n) for n in PRE_TAPS],
                            [h, hp] + list(ws[:n_vec]), cot)
        return out[:2], tuple(out[2:]) + tuple(dws)

    (dh0_p, dhp), pre_g = rowwise(
        "rwkv_pre_bwd", fn_pre_b, [h0, _halo_before(h0, tms), dr, dlw, dk2, dv, dan, dbn, dg],
        [e, et] + pre_vec + pre_w, [(D_MODEL, F32)] * 2,
        [_acc((1, D_MODEL))] * n_vec + [_acc(w.shape) for w in pre_w], tms)
    grads["a_mu"] = jnp.concatenate(pre_g[:6], axis=0)
    for name, val in zip(["a_w0", "a_a0", "a_k_k", "a_k_a", "a_w_r", "a_w_k", "a_w_v", "a_w1", "a_w2", "a_a1",
                          "a_a2", "a_g1", "a_g2"], pre_g[6:]):
        grads[name] = val

    def fn_add(c, i, a, b, d, after):
        return (a + b + _shift_up(d, after, i, lp // tm),), ()

    (dh0,), _ = rowwise("grad_h0", fn_add, [dh0_c, dh0_p, dhp, _halo_after(dhp, tm)], [], [(D_MODEL, F32)], [], tm)
    grads["meta_tokens"] = dh0[PAD_FRONT:TOK0]
    return loss, dh0[TOK0:], grads, early_from_chips


ANY = pl.BlockSpec(memory_space=pl.ANY)
XY_FLIPS = ((0, 1), (1, 0), (1, 1))
ALL_FLIPS = tuple((e >> 2 & 1, e >> 1 & 1, e & 1) for e in range(1, N_DEV))


def _flip(v, bit):
    return 1 - v if bit else v


def _sem_scratch(n):
    return [pltpu.SemaphoreType.DMA((n,)), pltpu.SemaphoreType.DMA((n,))]


def gather_copies(src, dst, ici_send, ici_recv, d2d_send, d2d_recv):
    npeer = len(XY_FLIPS)
    x, y, c = lax.axis_index("x"), lax.axis_index("y"), lax.axis_index("c")

    def half(ref, k, which):
        h = src[k].shape[0] // 2
        start = which * h
        return ref.at[pl.ds(pl.multiple_of(start, 8) if h % 8 == 0 else start, h)]

    def ici(k, j, slot):
        fx, fy = XY_FLIPS[j]
        return pltpu.make_async_remote_copy(
            src_ref=half(src[k], k, c), dst_ref=half(dst[k].at[slot], k, c), send_sem=ici_send.at[k * npeer + j],
            recv_sem=ici_recv.at[k * npeer + j], device_id=(_flip(x, fx), _flip(y, fy), c), device_id_type=MESH)

    def d2d(k, j, which):
        fx, fy = XY_FLIPS[j]
        landed = half(dst[k].at[2 * _flip(x, fx) + _flip(y, fy)], k, which)
        return pltpu.make_async_remote_copy(
            src_ref=landed, dst_ref=landed, send_sem=d2d_send.at[k * npeer + j], recv_sem=d2d_recv.at[k * npeer + j],
            device_id=(x, y, 1 - c), device_id_type=MESH)

    pairs = [(k, j) for k in range(len(src)) for j in range(npeer)]
    return ([ici(k, j, 2 * x + y) for k, j in pairs],
            [ici(k, j, 2 * _flip(x, XY_FLIPS[j][0]) + _flip(y, XY_FLIPS[j][1])) for k, j in pairs],
            [d2d(k, j, c) for k, j in pairs], [d2d(k, j, 1 - c) for k, j in pairs])


def gather_scratch(n):
    return _sem_scratch(n * len(XY_FLIPS)) * 2


def gathered_shapes(shards):
    return [jax.ShapeDtypeStruct((N_SHARD,) + s.shape, s.dtype) for s in shards]


def fill_own(gathered, shards):
    if not shards:
        return []
    slot = 2 * lax.axis_index("x") + lax.axis_index("y")
    return [lax.dynamic_update_index_in_dim(g, s, slot, 0) for g, s in zip(gathered, shards)]


def all_gather_shards(shards):
    n = len(shards)

    def body(*refs):
        sends, arrivals, forwards, forwarded = gather_copies(refs[:n], refs[n:2 * n], *refs[2 * n:])
        for cp in sends:
            cp.start()
        for landed, onward in zip(arrivals, forwards):
            landed.wait_recv()
            onward.start()
        for cp in forwarded:
            cp.wait_recv()
        for cp in sends + forwards:
            cp.wait_send()

    out = pl.pallas_call(body, name="gather_weights", in_specs=[ANY] * n, out_specs=[ANY] * n,
                         out_shape=gathered_shapes(shards), scratch_shapes=gather_scratch(n))(*shards)
    return fill_own(out, shards)


def placement():
    x, y, c = lax.axis_index("x"), lax.axis_index("y"), lax.axis_index("c")
    me = 2 * x + y
    others = [j + (j >= me).astype(jnp.int32) for j in range(N_SHARD - 1)]
    return jnp.stack([c, me] + others).astype(jnp.int32)


def hosted_gather(shards):
    return (gather_copies, list(shards), gathered_shapes(shards), gather_scratch(len(shards)),
            lambda got: fill_own(got, shards))


def pair_exchange_copies(src, got, send_sems, recv_sems):
    x, y, c = lax.axis_index("x"), lax.axis_index("y"), lax.axis_index("c")

    def copy(k):
        half = src[k].shape[1] // 2
        theirs = src[k].at[:, pl.ds(pl.multiple_of((1 - c) * half, 8), half), :]
        return pltpu.make_async_remote_copy(
            src_ref=theirs, dst_ref=got[k], send_sem=send_sems.at[k], recv_sem=recv_sems.at[k],
            device_id=(x, y, 1 - c), device_id_type=MESH)

    sends = [copy(k) for k in range(len(src))]
    return sends, sends, [], []


def _half_shapes(sources):
    return [jax.ShapeDtypeStruct((s.shape[0], s.shape[1] // 2, s.shape[2]), s.dtype) for s in sources]


def hosted_pair_exchange(sources):
    return (pair_exchange_copies, list(sources), _half_shapes(sources), _sem_scratch(len(sources)), list)


def pair_exchange(name, sources):
    n = len(sources)

    def body(*refs):
        sends, arrivals, _, _ = pair_exchange_copies(refs[:n], refs[n:2 * n], *refs[2 * n:])
        for cp in sends:
            cp.start()
        for cp in arrivals:
            cp.wait_recv()
        for cp in sends:
            cp.wait_send()

    halves = _half_shapes(sources)
    return pl.pallas_call(body, name=name, in_specs=[ANY] * n, out_specs=[ANY] * n,
                          out_shape=halves, scratch_shapes=_sem_scratch(n))(*sources)


def chip_exchange(parts):
    n = len(parts)

    def body(*refs):
        sends, arrivals = chip_exchange_copies(refs[:n], refs[n:2 * n], *refs[2 * n:])
        for cp in sends:
            cp.start()
        for cp in arrivals:
            cp.wait_recv()
        for cp in sends:
            cp.wait_send()

    return pl.pallas_call(
        body, name="grads_chip_exchange", in_specs=[ANY] * n, out_specs=[ANY] * n,
        out_shape=[jax.ShapeDtypeStruct(p.shape, p.dtype) for p in parts],
        scratch_shapes=_sem_scratch(n * len(XY_FLIPS)),
    )(*parts)


def chip_exchange_copies(src, dst, send_sems, recv_sems):
    npeer = len(XY_FLIPS)
    x, y, c = lax.axis_index("x"), lax.axis_index("y"), lax.axis_index("c")
    me = 2 * x + y

    def copy(k, j, sending):
        fx, fy = XY_FLIPS[j]
        px, py = _flip(x, fx), _flip(y, fy)
        peer = 2 * px + py
        return pltpu.make_async_remote_copy(
            src_ref=src[k].at[peer], dst_ref=dst[k].at[me if sending else peer],
            send_sem=send_sems.at[k * npeer + j], recv_sem=recv_sems.at[k * npeer + j],
            device_id=(px, py, c), device_id_type=MESH)

    pairs = [(k, j) for k in range(len(src)) for j in range(npeer)]
    return [copy(k, j, True) for k, j in pairs], [copy(k, j, False) for k, j in pairs]


def sibling_share(halves):
    n = len(halves)

    def body(*refs):
        src, got = refs[:n], refs[n:2 * n]
        send_sems, recv_sems = refs[2 * n:]
        x, y, c = lax.axis_index("x"), lax.axis_index("y"), lax.axis_index("c")
        sends = [pltpu.make_async_remote_copy(
            src_ref=src[k], dst_ref=got[k], send_sem=send_sems.at[k], recv_sem=recv_sems.at[k],
            device_id=(x, y, 1 - c), device_id_type=MESH) for k in range(n)]
        for cp in sends:
            cp.start()
        for cp in sends:
            cp.wait_recv()
        for cp in sends:
            cp.wait_send()

    return pl.pallas_call(
        body, name="grads_sibling_share", in_specs=[ANY] * n, out_specs=[ANY] * n,
        out_shape=[jax.ShapeDtypeStruct(h.shape, h.dtype) for h in halves], scratch_shapes=_sem_scratch(n),
    )(*halves)


ADD_TILE_ELEMS = 512 * 1024


def _row_tile(rows, cols):
    return max(t for t in range(8, rows + 1, 8) if rows % t == 0 and t * cols <= ADD_TILE_ELEMS)


def _prefetch_call(body, name, place, grid, in_specs, out_specs, out_shape, args):
    return pl.pallas_call(
        body, name=name, out_shape=out_shape,
        grid_spec=pltpu.PrefetchScalarGridSpec(num_scalar_prefetch=1, grid=grid, in_specs=in_specs,
                                               out_specs=out_specs),
        compiler_params=pltpu.CompilerParams(dimension_semantics=("arbitrary",) * len(grid),
                                             vmem_limit_bytes=VMEM_LIMIT),
    )(place, *args)


def pair_add(name, place, src, got, dtype):
    n4, half, cols = got.shape
    tile = _row_tile(half, cols)
    nt = half // tile

    def body(pr, a_ref, b_ref, o_ref):
        o_ref[...] = (a_ref[...] + b_ref[...]).astype(o_ref.dtype)

    mine = pl.BlockSpec((None, tile, cols), lambda s, i, pr: (s, pr[0] * nt + i, 0))
    blk = pl.BlockSpec((None, tile, cols), lambda s, i, pr: (s, i, 0))
    return _prefetch_call(body, name, place, (n4, nt), [mine, blk], blk,
                          jax.ShapeDtypeStruct(got.shape, dtype), (src, got))


def chip_add(name, place, part, from_chips):
    _, half, cols = part.shape
    tile = _row_tile(half, cols)

    def body(pr, own_ref, r0_ref, r1_ref, r2_ref, o_ref):
        me = pr[1]
        own, r0, r1, r2 = (r[...].astype(F32) for r in (own_ref, r0_ref, r1_ref, r2_ref))
        t0 = jnp.where(me == 0, own, r0)
        t1 = jnp.where(me == 0, r0, jnp.where(me == 1, own, r1))
        t2 = jnp.where(me <= 1, r1, jnp.where(me == 2, own, r2))
        t3 = jnp.where(me == 3, own, r2)
        o_ref[...] = ((t0 + t1) + t2) + t3

    def slab(j):
        return pl.BlockSpec((None, tile, cols), lambda i, pr: (pr[j], i, 0))

    return _prefetch_call(body, name, place, (half // tile,), [slab(1), slab(2), slab(3), slab(4)],
                          pl.BlockSpec((tile, cols), lambda i, pr: (i, 0)),
                          jax.ShapeDtypeStruct((half, cols), F32), (part, from_chips, from_chips, from_chips))


def pair_adds(tag, place, sources, got, narrow):
    return [pair_add(f"grads_pair_add_{tag}{k}", place, s, g, BF16 if nar else F32)
            for k, (s, g, nar) in enumerate(zip(sources, got, narrow))]


def finish_sums(place, parts, from_chips):
    halves = [chip_add(f"grads_chip_add{k}", place, p, f) for k, (p, f) in enumerate(zip(parts, from_chips))]
    return list(zip(halves, sibling_share(halves)))


ADAM_ROWS = 256


def adamw_update(name, place, halves, w, m, v):
    nsub, rows, cols = w.shape
    half = rows // 2
    tr = ADAM_ROWS if half % ADAM_ROWS == 0 else half
    nth = half // tr

    def body(pr, *refs):
        g_refs, (w_ref, m_ref, v_ref, g_ref, d_ref, nm_ref, nv_ref) = refs[:2 * nsub], refs[2 * nsub:]
        l = pl.program_id(0)
        mine = (pl.program_id(1) // nth) == pr[0]
        g = None
        for s in range(nsub):
            gs = jnp.where(mine, g_refs[2 * s][...], g_refs[2 * s + 1][...])
            g = gs if g is None else jnp.where(l == s, gs, g)
        m2 = ADAM_B1 * m_ref[...] + (1.0 - ADAM_B1) * g
        v2 = ADAM_B2 * v_ref[...] + (1.0 - ADAM_B2) * (g * g)
        m_hat = m2 / (1.0 - ADAM_B1 ** ADAM_STEP)
        v_hat = v2 / (1.0 - ADAM_B2 ** ADAM_STEP)
        g_ref[...] = g
        d_ref[...] = -ADAM_LR * (m_hat / (jnp.sqrt(v_hat) + ADAM_EPS) + ADAM_WD * w_ref[...])
        nm_ref[...] = m2
        nv_ref[...] = v2

    gblk = pl.BlockSpec((tr, cols), lambda l, i, pr: (i % nth, 0))
    blk = pl.BlockSpec((None, tr, cols), lambda l, i, pr: (l, i, 0))
    out = jax.ShapeDtypeStruct((nsub, rows, cols), F32)
    return _prefetch_call(body, name, place, (nsub, rows // tr), [gblk] * (2 * nsub) + [blk] * 3, [blk] * 4,
                          [out] * 4, [h for pair in halves for h in pair] + [w, m, v])


WEIGHT_NAMES = ("meta_tokens", "a_mu", "a_w_r", "a_w_k", "a_w_v", "a_w_o", "a_w0", "a_w1", "a_w2", "a_a0", "a_a1",
                "a_a2", "a_g1", "a_g2", "a_k_k", "a_k_a", "a_r_k", "a_gn_w", "a_gn_b", "kv_w_k", "kv_w_v", "b_w_q",
                "b_sinks", "b_w_o", "mlp_w_up", "mlp_w_down", "ln_g", "ln_b")
BIG_NAMES = ("a_w_r", "a_w_k", "a_w_v", "a_w_o", "b_w_q", "b_w_o")
EARLY_NAMES, LATE_NAMES = BIG_NAMES[:3], BIG_NAMES[3:]
PACK_MATS = (("kv_w_k", 256), ("kv_w_v", 256), ("a_w1", 64), ("a_a1", 64), ("a_g1", 128), ("a_w2", 64),
             ("a_a2", 64), ("a_g2", 128))
COLUMN_CUT = ("a_w2", "a_a2", "a_g2")
PACK_VECS = (("a_mu", 6), ("a_w0", 1), ("a_a0", 1), ("a_k_k", 1), ("a_k_a", 1), ("a_gn_w", 1), ("a_gn_b", 1),
             ("ln_g", 4), ("ln_b", 4), ("meta_tokens", 16))
PACK_REPL = (("a_r_k", 4), ("b_sinks", 1))
SHARD_W = D_MODEL // N_SHARD
N_MAT_ROWS = sum(r for _, r in PACK_MATS)
N_VEC_ROWS = sum(r for _, r in PACK_VECS)
N_PACK_ROWS = -(-(N_MAT_ROWS + N_VEC_ROWS + sum(r for _, r in PACK_REPL)) // 8) * 8
N_GATHER_VEC_ROWS = -(-N_VEC_ROWS // 16) * 16


def _pack_rows(arr):
    if arr.size == N_HEADS:
        return jnp.pad(arr.reshape(1, N_HEADS), ((0, 0), (0, SHARD_W - N_HEADS)))
    return arr.reshape(-1, SHARD_W)


def pack_small(get):
    parts = [_pack_rows(get(name)) for name, _ in PACK_MATS + PACK_VECS + PACK_REPL]
    used = sum(p.shape[0] for p in parts)
    return jnp.concatenate(parts + [jnp.zeros((N_PACK_ROWS - used, SHARD_W), F32)], axis=0)


def unpack_small(pack, shapes):
    out, off = {}, 0
    for name, rows in PACK_MATS + PACK_VECS + PACK_REPL:
        piece = pack[off:off + rows]
        off += rows
        out[name] = piece[:, :N_HEADS].reshape(shapes[name]) if name == "b_sinks" else piece.reshape(shapes[name])
    return out


def whole_weights(big_names, gathered_big, mats, vecs, a_r_k, b_sinks):
    p = {name: g.reshape(D_MODEL, D_MODEL) for name, g in zip(big_names, gathered_big)}
    off = 0
    for name, rows in PACK_MATS:
        piece = mats[:, off:off + rows]
        off += rows
        if name in COLUMN_CUT:
            p[name] = piece.transpose(1, 0, 2).reshape(rows, D_MODEL)
        else:
            p[name] = piece.reshape(D_MODEL, rows)
    v = vecs.transpose(1, 0, 2).reshape(-1, D_MODEL)
    off = 0
    for name, rows in PACK_VECS:
        p[name] = v[off:off + rows]
        off += rows
    for i in range(2):
        for j in range(2):
            p[f"ln_g{i}{j}"] = p["ln_g"][2 * i + j:2 * i + j + 1]
            p[f"ln_b{i}{j}"] = p["ln_b"][2 * i + j:2 * i + j + 1]
    p["a_r_k"] = a_r_k.reshape(1, D_MODEL)
    p["b_sinks"] = b_sinks
    return p


def small_grad_pack(g):
    parts = []
    for name, rows in PACK_MATS:
        if name in COLUMN_CUT:
            parts.append(g[name].reshape(rows, N_SHARD, SHARD_W).transpose(1, 0, 2))
        else:
            parts.append(g[name].reshape(N_SHARD, rows, SHARD_W))
    vec_rows = [g["a_mu"]] + [g[n] for n in ("a_w0", "a_a0", "a_k_k", "a_k_a", "a_gn_w", "a_gn_b")]
    vec_rows += [g[f"ln_g{i}{j}"] for i in range(2) for j in range(2)]
    vec_rows += [g[f"ln_b{i}{j}"] for i in range(2) for j in range(2)] + [g["meta_tokens"]]
    parts.append(jnp.concatenate(vec_rows, axis=0).reshape(N_VEC_ROWS, N_SHARD, SHARD_W).transpose(1, 0, 2))
    parts.append(jnp.broadcast_to(g["a_r_k"].reshape(1, -1, SHARD_W), (N_SHARD, D_MODEL // SHARD_W, SHARD_W)))
    sinks = jnp.pad(g["b_sinks"].reshape(1, 1, N_HEADS), ((0, 0), (0, 0), (0, SHARD_W - N_HEADS)))
    parts.append(jnp.broadcast_to(sinks, (N_SHARD, 1, SHARD_W)))
    used = sum(p.shape[1] for p in parts)
    parts.append(jnp.zeros((N_SHARD, N_PACK_ROWS - used, SHARD_W), F32))
    return jnp.concatenate(parts, axis=1)


def train_step(vals):
    w = {n: vals[n] for n in WEIGHT_NAMES}
    w_pack = pack_small(lambda n: w[n])
    early = [w[n][0].astype(BF16) for n in EARLY_NAMES]
    early += [w_pack[:N_MAT_ROWS].astype(BF16), w_pack[N_MAT_ROWS:N_MAT_ROWS + N_GATHER_VEC_ROWS]]
    gathered = all_gather_shards(early)
    ne = len(EARLY_NAMES)
    p = whole_weights(EARLY_NAMES, gathered[:ne], gathered[ne], gathered[ne + 1][:, :N_VEC_ROWS], w["a_r_k"],
                      w["b_sinks"])
    nb = len(BIG_NAMES)

    def late_set(big, layer):
        shards = [w[n][0].astype(BF16) for n in big]
        shards += [w["mlp_w_up"][layer].astype(BF16), w["mlp_w_down"][layer].astype(BF16)]

        def weights(got):
            out = {n: x.reshape(D_MODEL, D_MODEL) for n, x in zip(big, got)}
            out[f"mlp_up{layer}"], out[f"mlp_down{layer}"] = got[len(big):]
            return out

        return shards, weights

    late = (late_set((), 1), late_set(LATE_NAMES, 0))

    place = placement()
    ready = {}
    a_names, b_names = BIG_NAMES[:4], BIG_NAMES[4:]

    def early_sources(g):
        return ([g[n].reshape(N_SHARD, SHARD_W, D_MODEL) for n in b_names]
                + [g["mlp_up0"], g["mlp_up1"], g["mlp_down0"], g["mlp_down1"]])

    def early_parts(srcs, got):
        ready["parts"] = pair_adds("early", place, srcs, got, [True] * len(srcs))
        return ready["parts"]

    loss, gx, g, early_from_chips = local_step(vals["x"][0], vals["loss_target"][0], p, late,
                                               (early_sources, early_parts))
    loss = lax.psum(loss, ("x", "y", "c"))
    srcs = [g[n].reshape(N_SHARD, SHARD_W, D_MODEL) for n in a_names] + [small_grad_pack(g)]
    rest = pair_adds("late", place, srcs, pair_exchange("grads_pair_exchange", srcs), [True] * len(a_names) + [False])
    rest_from_chips = chip_exchange(rest)
    na = len(a_names)
    halves = finish_sums(place, rest[:na] + ready["parts"] + rest[na:],
                         list(rest_from_chips[:na]) + list(early_from_chips) + list(rest_from_chips[na:]))

    res = {}
    for k, n in enumerate(BIG_NAMES):
        res[n] = adamw_update("adamw_" + n, place, halves[k:k + 1], w[n], vals["m_" + n], vals["v_" + n])
    for k, n in ((nb, "mlp_w_up"), (nb + 2, "mlp_w_down")):
        res[n] = adamw_update("adamw_" + n, place, halves[k:k + 2], w[n], vals["m_" + n], vals["v_" + n])
    packs = adamw_update("adamw_small", place, halves[-1:], w_pack[None], pack_small(lambda n: vals["m_" + n])[None],
                         pack_small(lambda n: vals["v_" + n])[None])
    shapes = {n: w[n].shape for n in WEIGHT_NAMES}
    small = [unpack_small(pk[0], shapes) for pk in packs]
    outs = [loss, gx[None]]
    for t in range(4):
        outs += [res[n][t] if n in res else small[t][n] for n in WEIGHT_NAMES]
    return tuple(outs)


def kernel(x, meta_tokens, a_mu, a_w_r, a_w_k, a_w_v, a_w_o, a_w0, a_w1, a_w2, a_a0, a_a1, a_a2, a_g1, a_g2, a_k_k,
           a_k_a, a_r_k, a_gn_w, a_gn_b, kv_w_k, kv_w_v, b_w_q, b_sinks, b_w_o, mlp_w_up, mlp_w_down, ln_g, ln_b,
           loss_target, m_meta_tokens, m_a_mu, m_a_w_r, m_a_w_k, m_a_w_v, m_a_w_o, m_a_w0, m_a_w1, m_a_w2, m_a_a0,
           m_a_a1, m_a_a2, m_a_g1, m_a_g2, m_a_k_k, m_a_k_a, m_a_r_k, m_a_gn_w, m_a_gn_b, m_kv_w_k, m_kv_w_v,
           m_b_w_q, m_b_sinks, m_b_w_o, m_mlp_w_up, m_mlp_w_down, m_ln_g, m_ln_b, v_meta_tokens, v_a_mu, v_a_w_r,
           v_a_w_k, v_a_w_v, v_a_w_o, v_a_w0, v_a_w1, v_a_w2, v_a_a0, v_a_a1, v_a_a2, v_a_g1, v_a_g2, v_a_k_k,
           v_a_k_a, v_a_r_k, v_a_gn_w, v_a_gn_b, v_kv_w_k, v_kv_w_v, v_b_w_q, v_b_sinks, v_b_w_o, v_mlp_w_up,
           v_mlp_w_down, v_ln_g, v_ln_b):
    return train_step(dict(locals()))
```

```python
import functools

import numpy as np
import jax
import jax.numpy as jnp
from jax import lax
from jax.experimental import pallas as pl
from jax.experimental.pallas import tpu as pltpu

F32 = jnp.float32
BF16 = jnp.bfloat16

D_MODEL = 1024
N_HEADS = 16
HEAD_DIM = 64
N_HEADS_KV = 4
GROUP = 4
KV_DIM = N_HEADS_KV * HEAD_DIM
N_META = 16
BLOCK = 128
PAD_FRONT = BLOCK - N_META
TOK0 = PAD_FRONT + N_META
N_FF_CHUNK = 4
N_SHARD = 4
N_DEV = 8
GN_EPS = 64e-5
LN_EPS = 1e-5
ROPE_THETA = 10000.0
ALPHA = 4.0 ** 0.25
ADAM_LR, ADAM_B1, ADAM_B2, ADAM_EPS, ADAM_WD, ADAM_STEP = 0.001, 0.9, 0.999, 1e-08, 0.01, 10
SCAN_T = 64
PAIR = 128
KVW = GROUP * HEAD_DIM
VMEM_LIMIT = 60 * 1024 * 1024
HI = lax.Precision.HIGHEST
MESH = pl.DeviceIdType.MESH


def _dot(a, b, ca, cb):
    return lax.dot_general(a.astype(BF16), b.astype(BF16), (((ca,), (cb,)), ((), ())),
                           preferred_element_type=F32)


@jax.custom_vjp
def mm(a, b):
    return _dot(a, b, 1, 0)


def _mm_fwd(a, b):
    return mm(a, b), b


def _mm_bwd(b, g):
    return _dot(g, b, 1, 1), jnp.zeros_like(b)


mm.defvjp(_mm_fwd, _mm_bwd)


@jax.custom_vjp
def mm_tap(a, b, tap):
    return _dot(a, b, 1, 0)


mm_tap.defvjp(lambda a, b, tap: (_dot(a, b, 1, 0), b), lambda b, g: (_dot(g, b, 1, 1), jnp.zeros_like(b), g))


def tmm(x, w, taps, xs):
    y = mm(x, w) if taps is None else mm_tap(x, w, taps[len(xs)])
    xs.append(x)
    return y


def vjp_taps(core, tap_shapes, args, cot):
    taps = [jnp.zeros(s, F32) for s in tap_shapes]
    _, vjp, xs = jax.vjp(core, taps, *args, has_aux=True)
    out = vjp(cot)
    return out[1:], [_dot(x, g, 0, 0) for x, g in zip(xs, out[0])]


def _split3(x):
    x1 = x.astype(BF16)
    r1 = x - x1.astype(F32)
    x2 = r1.astype(BF16)
    x3 = (r1 - x2.astype(F32)).astype(BF16)
    return x1, x2, x3


def _exact_dot(x, m01, cb=0):
    acc = None
    for piece in _split3(x):
        t = lax.dot_general(piece, m01, (((1,), (cb,)), ((), ())), preferred_element_type=F32)
        acc = t if acc is None else acc + t
    return acc


def _head_matrices():
    e = np.zeros((D_MODEL, N_HEADS), np.float32)
    e[np.arange(D_MODEL), np.arange(D_MODEL) // HEAD_DIM] = 1.0
    return jnp.asarray(e, BF16), jnp.asarray(e.T, BF16)


@jax.custom_vjp
def hsum(x, e, et):
    return _exact_dot(x, e)


@jax.custom_vjp
def hbc(s, e, et):
    return _exact_dot(s, et)


hsum.defvjp(lambda x, e, et: (_exact_dot(x, e), (e, et)),
            lambda res, g: (hbc(g, *res), jnp.zeros_like(res[0]), jnp.zeros_like(res[1])))
hbc.defvjp(lambda s, e, et: (_exact_dot(s, et), (e, et)),
           lambda res, g: (hsum(g, *res), jnp.zeros_like(res[0]), jnp.zeros_like(res[1])))


def _sigmoid(u):
    return 0.5 * (jnp.tanh(0.5 * u) + 1.0)


def _softplus(u):
    return jnp.maximum(u, 0.0) + jnp.log(1.0 + jnp.exp(-jnp.abs(u)))


def _layer_norm(z, g, b):
    mu = jnp.mean(z, axis=-1, keepdims=True)
    zc = z - mu
    var = jnp.mean(zc * zc, axis=-1, keepdims=True)
    return zc * lax.rsqrt(var + LN_EPS) * g + b


def _zero_map(nd):
    return lambda c, i: (0,) * nd


def _params():
    return pltpu.CompilerParams(dimension_semantics=("arbitrary", "arbitrary"), vmem_limit_bytes=VMEM_LIMIT)


def rowwise(name, fn, rows, consts, out_rows, out_accs, tm, nc=1, hosted=None):
    lp = rows[0].shape[-2]
    nt = lp // tm
    assert nt * tm == lp, (name, lp, tm)
    copies_fn, hosted_src, hosted_shapes, hosted_scratch, hosted_post = hosted or (None, (), [], [], None)
    ng = len(hosted_src)
    in_specs, args = [], []
    for a in rows:
        if isinstance(a, tuple):
            a, block_rows, block_index = a
            in_specs.append(pl.BlockSpec((block_rows, a.shape[1]),
                                         functools.partial(lambda f, c, i: (f(i), 0), block_index)))
        elif a.ndim == 2:
            in_specs.append(pl.BlockSpec((tm, a.shape[1]), lambda c, i: (i, 0)))
        else:
            in_specs.append(pl.BlockSpec((a.shape[0], tm, a.shape[2]), lambda c, i: (0, i, 0)))
        args.append(a)
    for cst in consts:
        if isinstance(cst, tuple):
            arr, bs, im = cst
            in_specs.append(pl.BlockSpec(bs, im))
        else:
            arr = cst
            in_specs.append(pl.BlockSpec(arr.shape, _zero_map(arr.ndim), pipeline_mode=pl.Buffered(1)))
        args.append(arr)
    out_shape, out_specs, acc_per_chunk = [], [], []
    for spec in out_rows:
        if len(spec) == 3 and spec[2]:
            out_shape.append(jax.ShapeDtypeStruct((nc, lp, spec[0]), spec[1]))
            out_specs.append(pl.BlockSpec((None, tm, spec[0]), lambda c, i: (c, i, 0)))
        else:
            out_shape.append(jax.ShapeDtypeStruct((lp, spec[0]), spec[1]))
            out_specs.append(pl.BlockSpec((tm, spec[0]), lambda c, i: (i, 0)))
    for spec in out_accs:
        out_shape.append(jax.ShapeDtypeStruct(spec[0], spec[1]))
        if len(spec) == 4:
            out_specs.append(pl.BlockSpec(spec[2], spec[3]))
            acc_per_chunk.append(True)
        else:
            out_specs.append(pl.BlockSpec(spec[0], _zero_map(len(spec[0])), pipeline_mode=pl.Buffered(1)))
            acc_per_chunk.append(False)
    n_in, n_or, n_out = len(args), len(out_rows), len(out_shape)

    def body(*refs):
        c = pl.program_id(0)
        i = pl.program_id(1)
        if ng:
            src, dst = refs[n_in:n_in + ng], refs[n_in + ng + n_out:n_in + 2 * ng + n_out]
            sends, arrivals, forwards, forwarded = copies_fn(src, dst, *refs[n_in + 2 * ng + n_out:])

            @pl.when(jnp.logical_and(c == 0, i == 0))
            def _():
                for cp in sends:
                    cp.start()

        vals = [r[...] for r in refs[:n_in]]
        outs_r, outs_a = fn(c, i, *vals)
        out_refs = refs[n_in + ng:n_in + ng + n_out]
        for ref, val in zip(out_refs[:n_or], outs_r):
            ref[...] = val.astype(ref.dtype)
        for ref, val, per_chunk in zip(out_refs[n_or:], outs_a, acc_per_chunk):
            first = (i == 0) if per_chunk else jnp.logical_and(i == 0, c == 0)

            @pl.when(first)
            def _():
                ref[...] = val.astype(ref.dtype)

            @pl.when(jnp.logical_not(first))
            def _():
                ref[...] += val.astype(ref.dtype)

        if ng:
            @pl.when(jnp.logical_and(c == nc - 1, i == max(nt - 3, 0)))
            def _():
                for k, landed in enumerate(arrivals):
                    landed.wait_recv()
                    if forwards:
                        forwards[k].start()

            @pl.when(jnp.logical_and(c == nc - 1, i == nt - 1))
            def _():
                for cp in forwarded:
                    cp.wait_recv()
                for cp in sends + forwards:
                    cp.wait_send()

    outs = pl.pallas_call(body, name=name, grid=(nc, nt), in_specs=in_specs + [ANY] * ng,
                          out_specs=out_specs + [ANY] * ng, out_shape=out_shape + list(hosted_shapes),
                          scratch_shapes=list(hosted_scratch), compiler_params=_params())(*args, *hosted_src)
    if ng:
        return outs[:n_or], outs[n_or:n_out], hosted_post(outs[n_out:])
    return outs[:n_or], outs[n_or:]


def _row_ids(i, tm):
    return i * tm + lax.broadcasted_iota(jnp.int32, (tm, 1), 0)


SUBLANES = 8


def _halo_before(arr, tm):
    return (arr, SUBLANES, lambda i: jnp.maximum(i * (tm // SUBLANES) - 1, 0))


def _halo_after(arr, tm):
    last = arr.shape[0] // SUBLANES - 1
    return (arr, SUBLANES, lambda i: jnp.minimum((i + 1) * (tm // SUBLANES), last))


def _pick_row(block8, row):
    rows = lax.broadcasted_iota(jnp.int32, block8.shape, 0)
    return jnp.sum(jnp.where(rows == row, block8, 0.0), axis=0, keepdims=True)


def _shift_down(x, before8, i):
    rows = lax.broadcasted_iota(jnp.int32, x.shape, 0)
    top = _pick_row(before8, SUBLANES - 1) * (i > 0).astype(F32)
    return jnp.where(rows == 0, top, pltpu.roll(x, 1, 0))


def _shift_up(x, after8, i, nt):
    rows = lax.broadcasted_iota(jnp.int32, x.shape, 0)
    bottom = _pick_row(after8, 0) * (i < nt - 1).astype(F32)
    return jnp.where(rows == x.shape[0] - 1, bottom, pltpu.roll(x, x.shape[0] - 1, 0))


PRE_TAPS = (D_MODEL, D_MODEL, D_MODEL, 64, D_MODEL, 64, D_MODEL, 128, D_MODEL)


def rwkv_pre(e, et, ws, taps, h, hp, mu_r, mu_w, mu_k, mu_v, mu_a, mu_g, w0, a0, k_k, k_a):
    w_r, w_k, w_v, w1, w2, a1, a2, g1, g2 = ws
    xs = []
    xx = hp - h
    r = tmm(h + xx * mu_r, w_r, taps, xs)
    k = tmm(h + xx * mu_k, w_k, taps, xs)
    v = tmm(h + xx * mu_v, w_v, taps, xs)
    wraw = -_softplus(-(w0 + tmm(jnp.tanh(tmm(h + xx * mu_w, w1, taps, xs)), w2, taps, xs))) - 0.5
    lw = -jnp.exp(wraw)
    a = _sigmoid(a0 + tmm(tmm(h + xx * mu_a, a1, taps, xs), a2, taps, xs))
    g = tmm(_sigmoid(tmm(h + xx * mu_g, g1, taps, xs)), g2, taps, xs)
    kk = k * k_k
    ss = hsum(kk * kk, e, et)
    pos = ss > 0.0
    nrm = jnp.where(pos, jnp.sqrt(jnp.where(pos, ss, 1.0)), 0.0)
    kk = kk * hbc(1.0 / jnp.maximum(nrm, 1e-12), e, et)
    k2 = k * (1.0 + (a - 1.0) * k_a)
    return (r, lw, k2, v, -kk, kk * a, g), xs


def rwkv_post(e, et, w_o, taps, y, r, k2, v, g, h0, gn_w, gn_b, rk, lg, lb):
    xs = []
    inv_n = 1.0 / HEAD_DIM
    yc = y - hbc(hsum(y, e, et) * inv_n, e, et)
    yv = hsum(yc * yc, e, et) * inv_n
    yn = yc * hbc(lax.rsqrt(yv + GN_EPS), e, et) * gn_w + gn_b
    bonus = hbc(hsum(r * k2 * rk, e, et), e, et) * v
    mix = tmm((yn + bonus) * g, w_o, taps, xs)
    return _layer_norm(ALPHA * h0 + mix, lg, lb), xs


@jax.custom_vjp
def sq_relu(x):
    r = jnp.maximum(x, 0.0)
    return r * r


sq_relu.defvjp(lambda x: (sq_relu(x), x), lambda x, g: (g * (2.0 * jnp.maximum(x, 0.0)),))


def mlp_chunk(wup, wdown, taps, h):
    xs = []
    return tmm(sq_relu(tmm(h, wup, taps, xs)), wdown, taps, xs), xs


def _rot_half(t):
    n = t.shape[-1]
    lane = lax.broadcasted_iota(jnp.int32, t.shape, t.ndim - 1)
    lo = (lane % HEAD_DIM) < (HEAD_DIM // 2)
    return jnp.where(lo, -pltpu.roll(t, n - HEAD_DIM // 2, t.ndim - 1), pltpu.roll(t, HEAD_DIM // 2, t.ndim - 1))


@jax.custom_vjp
def rot_half(t):
    return _rot_half(t)


rot_half.defvjp(lambda t: (_rot_half(t), None), lambda _, g: (-_rot_half(g),))


def _tile_lanes(t, width):
    return jnp.concatenate([t] * (width // t.shape[-1]), axis=-1)


def qkv_proj(cos, sin, wq, wk, wv, taps, h):
    xs = []
    q = tmm(h, wq, taps, xs)
    k = tmm(h, wk, taps, xs)
    v = tmm(h, wv, taps, xs)
    cq, sq = _tile_lanes(cos, D_MODEL), _tile_lanes(sin, D_MODEL)
    ck, sk = _tile_lanes(cos, KV_DIM), _tile_lanes(sin, KV_DIM)
    return (q * cq + rot_half(q) * sq, k * ck + rot_half(k) * sk, v), xs


def attn_out(w_o, taps, o, h, lg, lb):
    xs = []
    return _layer_norm(ALPHA * h + tmm(o, w_o, taps, xs), lg, lb), xs


def _scan_consts():
    t = SCAN_T
    tri = np.tril(np.ones((t, t), np.float32))
    rows = np.arange(2 * t)
    same = (rows[:, None] // t) == (rows[None, :] // t)
    strict = same & ((rows[None, :] % t) < (rows[:, None] % t))
    incl = same & ((rows[None, :] % t) <= (rows[:, None] % t))
    lane = np.arange(PAIR)
    masks = np.zeros((8, PAIR), np.float32)
    masks[0] = (lane // HEAD_DIM) == 0
    masks[1] = (lane // HEAD_DIM) == 1
    return (jnp.asarray(tri, BF16), jnp.asarray(strict.astype(np.float32)), jnp.asarray(incl.astype(np.float32)),
            jnp.asarray(masks), jnp.asarray(np.eye(2 * t, dtype=np.float32)))


def _scan_dot(a, b, ca, cb):
    return _dot(a, b, ca, cb)


@functools.partial(jax.custom_vjp, nondiff_argnums=(2, 3))
def _dotf(a, b, ca, cb):
    return _scan_dot(a, b, ca, cb)


def _dotf_bwd(ca, cb, res, g):
    a, b = res
    if ca == 1:
        da = _scan_dot(g, b, 1, 1 - cb)
    else:
        da = _scan_dot(b, g, 1 - cb, 1)
    if cb == 0:
        db = _scan_dot(a, g, 1 - ca, 0)
    else:
        db = _scan_dot(g, a, 0, 1 - ca)
    return da, db


_dotf.defvjp(lambda a, b, ca, cb: (_scan_dot(a, b, ca, cb), (a, b)), _dotf_bwd)


def _tri_dot(tri, x, ct):
    acc = None
    for piece in _split3(x):
        t = lax.dot_general(tri, piece, (((ct,), (0,)), ((), ())), preferred_element_type=F32)
        acc = t if acc is None else acc + t
    return acc


@jax.custom_vjp
def _cumsum_rows(tri, x):
    return _tri_dot(tri, x, 1)


_cumsum_rows.defvjp(lambda tri, x: (_tri_dot(tri, x, 1), tri),
                    lambda tri, g: (jnp.zeros_like(tri), _tri_dot(tri, g, 0)))


@jax.custom_vjp
def _unstack2(x):
    t = x.shape[0] // 2
    return x[:t] + x[t:]


_unstack2.defvjp(lambda x: (_unstack2(x), None), lambda _, g: (jnp.concatenate([g, g], axis=0),))


@jax.custom_vjp
def _last_row(x):
    return x[x.shape[0] - 1:, :]


def _last_row_bwd(_, g):
    rows = lax.broadcasted_iota(jnp.int32, (SCAN_T, g.shape[1]), 0)
    return (jnp.where(rows == SCAN_T - 1, jnp.broadcast_to(g, (SCAN_T, g.shape[1])), 0.0),)


_last_row.defvjp(lambda x: (_last_row(x), None), _last_row_bwd)


@jax.custom_vjp
def _solve_saved(n, rhs, minv, u):
    return u


def _solve_saved_bwd(res, du):
    minv, u = res
    drhs = _dotf(minv, du, 0, 0)
    return _dotf(drhs, u, 1, 1), drhs, jnp.zeros_like(minv), jnp.zeros_like(u)


_solve_saved.defvjp(lambda n, rhs, minv, u: (u, (minv, u)), _solve_saved_bwd)


def scan_chunk(tri, strict, incl, m0, m1, eye, r, lw, k, v, a, b, s0, saved=None):
    lower = strict > 0
    lower_incl = incl > 0

    def stack(x):
        return jnp.concatenate([x * m0, x * m1], axis=0)

    def dots(xs, ys, ca, cb, mask=None):
        out = [_dotf(x, y, ca, cb) for x, y in zip(xs, ys)]
        return out if mask is None else [jnp.where(mask, o, 0.0) for o in out]

    cl = [_cumsum_rows(tri, x) for x in lw]
    gam = [jnp.exp(c) for c in cl]
    ginv = [jnp.exp(-c) for c in cl]
    a_s = [stack(x * jnp.exp(c - w)) for x, c, w in zip(a, cl, lw)]
    r_s = [stack(x * g) for x, g in zip(r, gam)]
    b_s = [stack(x * g) for x, g in zip(b, ginv)]
    k_s = [stack(x * g) for x, g in zip(k, ginv)]
    v_s = [stack(x) for x in v]
    n_ab = dots(a_s, b_s, 1, 1, lower)
    n_ak = dots(a_s, k_s, 1, 1, lower)
    r_ab = dots(r_s, b_s, 1, 1, lower_incl)
    r_ak = dots(r_s, k_s, 1, 1, lower_incl)
    rhs = [x + y for x, y in zip(dots(a_s, s0, 1, 1), dots(n_ak, v_s, 1, 0))]
    if saved is None:
        minv = [eye + n for n in n_ab]
        p = n_ab
        for _ in range(5):
            p = dots(p, p, 1, 0)
            minv = [m + mp for m, mp in zip(minv, dots(minv, p, 1, 0))]
        u_s = dots(minv, rhs, 1, 0)
    else:
        minv = saved[0]
        u_s = [_solve_saved(n, x, m, u) for n, x, m, u in zip(n_ab, rhs, *saved)]
    y = [_unstack2(x0 + x1 + x2)
         for x0, x1, x2 in zip(dots(r_s, s0, 1, 1), dots(r_ab, u_s, 1, 0), dots(r_ak, v_s, 1, 0))]
    g_end = [_last_row(g) for g in gam]
    s1 = [s * g + x + z for s, g, x, z in zip(s0, g_end, dots(u_s, [x * g for x, g in zip(b_s, g_end)], 0, 0),
                                              dots(v_s, [x * g for x, g in zip(k_s, g_end)], 0, 0))]
    return y, s1, (minv, u_s)


SCAN_PAIRS = 8


def _scan_specs(consts, order):
    row = pl.BlockSpec((SCAN_T, PAIR * SCAN_PAIRS), lambda p, c: (order(c), p))
    state = pl.BlockSpec((None, SCAN_PAIRS, PAIR, PAIR), lambda p, c: (order(c), p, 0, 0))
    return row, state, [pl.BlockSpec(x.shape, _zero_map(x.ndim)) for x in consts]


def _pair_lanes(q):
    return slice(q * PAIR, (q + 1) * PAIR)


def scan_fwd(r, lw, k, v, a, b, shards=()):
    lp = r.shape[0]
    nch = lp // SCAN_T
    npair = D_MODEL // PAIR
    ng = len(shards)
    consts = _scan_consts()
    row, state, cspecs = _scan_specs(consts, lambda c: c)

    def body(tri, strict, incl, masks, eye, r_ref, lw_ref, k_ref, v_ref, a_ref, b_ref, *rest):
        src, (y_ref, s_ref, minv_ref, u_ref), dst = rest[:ng], rest[ng:ng + 4], rest[ng + 4:2 * ng + 4]
        carry = rest[2 * ng + 4]
        first = jnp.logical_and(pl.program_id(0) == 0, pl.program_id(1) == 0)
        last = jnp.logical_and(pl.program_id(0) == npair // SCAN_PAIRS - 1, pl.program_id(1) == nch - 1)
        if ng:
            sends, arrivals, forwards, forwarded = gather_copies(src, dst, *rest[2 * ng + 5:])

            @pl.when(first)
            def _():
                for cp in sends:
                    cp.start()

            @pl.when(jnp.logical_and(pl.program_id(0) == npair // SCAN_PAIRS - 1, pl.program_id(1) == nch * 3 // 4))
            def _():
                for landed, onward in zip(arrivals, forwards):
                    landed.wait_recv()
                    onward.start()

        @pl.when(pl.program_id(1) == 0)
        def _():
            carry[...] = jnp.zeros_like(carry)

        pairs = range(SCAN_PAIRS)
        s0 = [carry[q] for q in pairs]
        rows = [[ref[:, _pair_lanes(q)] for q in pairs] for ref in (r_ref, lw_ref, k_ref, v_ref, a_ref, b_ref)]
        y, s1, (minv, u) = scan_chunk(tri[...], strict[...], incl[...], masks[0:1, :], masks[1:2, :], eye[...],
                                      *rows, s0)
        for q in pairs:
            s_ref[q] = s0[q]
            minv_ref[q] = minv[q]
            u_ref[q] = u[q]
            y_ref[:, _pair_lanes(q)] = y[q]
            carry[q] = s1[q]

        if ng:
            @pl.when(last)
            def _():
                for cp in forwarded:
                    cp.wait_recv()
                for cp in sends + forwards:
                    cp.wait_send()

    mats = jax.ShapeDtypeStruct((nch, npair, PAIR, PAIR), F32)
    out = pl.pallas_call(
        body, name="rwkv_scan_fwd", grid=(npair // SCAN_PAIRS, nch), in_specs=cspecs + [row] * 6 + [ANY] * ng,
        out_specs=[row, state, state, state] + [ANY] * ng,
        out_shape=[jax.ShapeDtypeStruct((lp, D_MODEL), F32), mats, mats, mats] + gathered_shapes(shards),
        scratch_shapes=[pltpu.VMEM((SCAN_PAIRS, PAIR, PAIR), F32)] + (gather_scratch(ng) if ng else []),
        compiler_params=_params(),
    )(*consts, r, lw, k, v, a, b, *shards)
    return out[:4], fill_own(out[4:], shards)


def scan_bwd(r, lw, k, v, a, b, saved, dy, direct_grads, parts=()):
    lp = r.shape[0]
    nch = lp // SCAN_T
    npair = D_MODEL // PAIR
    consts = _scan_consts()
    row, state, cspecs = _scan_specs(consts, lambda c: nch - 1 - c)

    ng = len(parts)

    def body(tri, strict, incl, masks, eye, r_ref, lw_ref, k_ref, v_ref, a_ref, b_ref, s_ref, minv_ref, u_ref,
             dy_ref, dr_in, dk_in, dv_in, *rest):
        src, (dr_ref, dlw_ref, dk_ref, dv_ref, da_ref, db_ref), dst = rest[:ng], rest[ng:ng + 6], rest[ng + 6:2 * ng + 6]
        carry = rest[2 * ng + 6]
        first = jnp.logical_and(pl.program_id(0) == 0, pl.program_id(1) == 0)
        last = jnp.logical_and(pl.program_id(0) == npair // SCAN_PAIRS - 1, pl.program_id(1) == nch - 1)
        if ng:
            sends, arrivals = chip_exchange_copies(src, dst, *rest[2 * ng + 7:])

            @pl.when(first)
            def _():
                for cp in sends:
                    cp.start()

        @pl.when(pl.program_id(1) == 0)
        def _():
            carry[...] = jnp.zeros_like(carry)

        pairs = range(SCAN_PAIRS)
        kept = ([minv_ref[q] for q in pairs], [u_ref[q] for q in pairs])

        def fn(*args):
            y, s1, _ = scan_chunk(tri[...], strict[...], incl[...], masks[0:1, :], masks[1:2, :], eye[...], *args,
                                  saved=kept)
            return y, s1

        rows = [[ref[:, _pair_lanes(q)] for q in pairs] for ref in (r_ref, lw_ref, k_ref, v_ref, a_ref, b_ref)]
        _, vjp = jax.vjp(fn, *rows, [s_ref[q] for q in pairs])
        grads = vjp(([dy_ref[:, _pair_lanes(q)] for q in pairs], [carry[q] for q in pairs]))
        direct = (dr_in, None, dk_in, dv_in, None, None)
        for q in pairs:
            ln = _pair_lanes(q)
            for ref, g, extra in zip((dr_ref, dlw_ref, dk_ref, dv_ref, da_ref, db_ref), grads[:6], direct):
                ref[:, ln] = g[q] if extra is None else g[q] + extra[:, ln]
            carry[q] = grads[6][q]

        if ng:
            @pl.when(last)
            def _():
                for cp in arrivals:
                    cp.wait_recv()
                for cp in sends:
                    cp.wait_send()

    out = pl.pallas_call(
        body, name="rwkv_scan_bwd", grid=(npair // SCAN_PAIRS, nch),
        in_specs=cspecs + [row] * 6 + [state] * 3 + [row] * 4 + [ANY] * ng, out_specs=[row] * 6 + [ANY] * ng,
        out_shape=[jax.ShapeDtypeStruct((lp, D_MODEL), F32)] * 6 + [jax.ShapeDtypeStruct(p.shape, p.dtype) for p in parts],
        scratch_shapes=[pltpu.VMEM((SCAN_PAIRS, PAIR, PAIR), F32)] + (_sem_scratch(ng * len(XY_FLIPS)) if ng else []),
        compiler_params=_params(),
    )(*consts, r, lw, k, v, a, b, *saved, dy, *direct_grads, *parts)
    return out[:6], out[6:]


def _spread_matrices():
    rep = np.zeros((N_HEADS_KV, KV_DIM, KVW), np.float32)
    for h in range(N_HEADS_KV):
        for g in range(GROUP):
            rep[h, h * HEAD_DIM + np.arange(HEAD_DIM), g * HEAD_DIM + np.arange(HEAD_DIM)] = 1.0
    return jnp.asarray(rep, BF16)


KV_HEADS = range(N_HEADS_KV)


def _attn_common(n, q_ref, kp, kc, vp, vc, rep_ref, sink_ref):
    lane = lax.broadcasted_iota(jnp.int32, (1, KVW), 1)
    gmask = [(lane // HEAD_DIM == g).astype(F32) for g in range(GROUP)]
    kk = jnp.concatenate([kp, kc], axis=0)
    vv = jnp.concatenate([vp, vc], axis=0)
    qs = [q_ref[:, h * KVW:(h + 1) * KVW] for h in KV_HEADS]
    q_s = [jnp.concatenate([q * gmask[g] for g in range(GROUP)], axis=0) for q in qs]
    keys = [_dot(kk, rep_ref[h], 1, 0) for h in KV_HEADS]
    vals = [_dot(vv, rep_ref[h], 1, 0) for h in KV_HEADS]
    qi = lax.broadcasted_iota(jnp.int32, (GROUP * BLOCK, 2 * BLOCK), 0) % BLOCK
    kj = lax.broadcasted_iota(jnp.int32, (GROUP * BLOCK, 2 * BLOCK), 1)
    rel = BLOCK + qi - kj
    valid = (rel >= 0) & (rel < BLOCK) & ((n - 1) * BLOCK + kj >= PAD_FRONT)
    s = [jnp.where(valid, _dot(x, y, 1, 1) * (HEAD_DIM ** -0.5), -1e30) for x, y in zip(q_s, keys)]
    sink_col = [jnp.concatenate([jnp.broadcast_to(sink_ref[h, g:g + 1, 0:1], (BLOCK, 1)) for g in range(GROUP)],
                                axis=0) for h in KV_HEADS]
    m = [jnp.maximum(jnp.max(x, axis=-1, keepdims=True), c) for x, c in zip(s, sink_col)]
    ex = [jnp.exp(x - y) for x, y in zip(s, m)]
    ex_sink = [jnp.exp(c - y) for c, y in zip(sink_col, m)]
    inv = [1.0 / (jnp.sum(x, axis=-1, keepdims=True) + c) for x, c in zip(ex, ex_sink)]
    return (gmask, q_s, keys, vals, [x * y for x, y in zip(ex, inv)], [x * y for x, y in zip(ex_sink, inv)])


def _unstack_groups(x_s, gmask):
    out = None
    for g in range(GROUP):
        t = x_s[g * BLOCK:(g + 1) * BLOCK] * gmask[g]
        out = t if out is None else out + t
    return out


def _attn_specs():
    qspec = pl.BlockSpec((BLOCK, D_MODEL), lambda n: (n, 0))
    cur = pl.BlockSpec((BLOCK, KV_DIM), lambda n: (n, 0))
    prev = pl.BlockSpec((BLOCK, KV_DIM), lambda n: (jnp.maximum(n - 1, 0), 0))
    rep = pl.BlockSpec((N_HEADS_KV, KV_DIM, KVW), lambda n: (0, 0, 0))
    sink = pl.BlockSpec((N_HEADS_KV, 8, PAIR), lambda n: (0, 0, 0))
    return qspec, cur, prev, rep, sink


def _attn_params():
    return pltpu.CompilerParams(dimension_semantics=("arbitrary",), vmem_limit_bytes=VMEM_LIMIT)


def attn_fwd(q, k, v, sinks_b):
    lp = q.shape[0]
    qspec, cur, prev, rep, sink = _attn_specs()

    def body(q_ref, kp_ref, kc_ref, vp_ref, vc_ref, rep_ref, sink_ref, o_ref):
        gmask, _, _, vals, p, _ = _attn_common(pl.program_id(0), q_ref, kp_ref[...], kc_ref[...], vp_ref[...],
                                               vc_ref[...], rep_ref, sink_ref)
        o = [_dot(x, y, 1, 0) for x, y in zip(p, vals)]
        for h in KV_HEADS:
            o_ref[:, h * KVW:(h + 1) * KVW] = _unstack_groups(o[h], gmask)

    return pl.pallas_call(
        body, name="swa_fwd", grid=(lp // BLOCK,), in_specs=[qspec, prev, cur, prev, cur, rep, sink],
        out_specs=qspec, out_shape=jax.ShapeDtypeStruct((lp, D_MODEL), F32), compiler_params=_attn_params(),
    )(q, k, k, v, v, _spread_matrices(), sinks_b)


def attn_bwd(q, k, v, sinks_b, do):
    lp = q.shape[0]
    qspec, cur, prev, rep, sink = _attn_specs()

    def body(q_ref, kp_ref, kc_ref, vp_ref, vc_ref, rep_ref, sink_ref, do_ref, dq_ref, dkc_ref, dkp_ref, dvc_ref,
             dvp_ref, dsink_ref):
        n = pl.program_id(0)
        gmask, q_s, keys, vals, p, p_sink = _attn_common(n, q_ref, kp_ref[...], kc_ref[...], vp_ref[...], vc_ref[...],
                                                         rep_ref, sink_ref)
        do_s = [jnp.concatenate([do_ref[:, h * KVW:(h + 1) * KVW] * gmask[g] for g in range(GROUP)], axis=0)
                for h in KV_HEADS]
        dp = [_dot(x, y, 1, 1) for x, y in zip(do_s, vals)]
        delta = [jnp.sum(x * y, axis=-1, keepdims=True) for x, y in zip(p, dp)]
        ds = [x * (y - z) * (HEAD_DIM ** -0.5) for x, y, z in zip(p, dp, delta)]
        dq = [_dot(x, y, 1, 0) for x, y in zip(ds, keys)]
        dkeys_s = [_dot(x, y, 0, 0) for x, y in zip(ds, q_s)]
        dvals_s = [_dot(x, y, 0, 0) for x, y in zip(p, do_s)]
        dkeys = [_exact_dot(x, rep_ref[h], cb=1) for h, x in enumerate(dkeys_s)]
        dvals = [_exact_dot(x, rep_ref[h], cb=1) for h, x in enumerate(dvals_s)]
        dk_all = (dkeys[0] + dkeys[1]) + (dkeys[2] + dkeys[3])
        dv_all = (dvals[0] + dvals[1]) + (dvals[2] + dvals[3])
        dkp_ref[...] = dk_all[:BLOCK]
        dkc_ref[...] = dk_all[BLOCK:]
        dvp_ref[...] = dv_all[:BLOCK]
        dvc_ref[...] = dv_all[BLOCK:]
        dsinks = []
        for h in KV_HEADS:
            dq_ref[:, h * KVW:(h + 1) * KVW] = _unstack_groups(dq[h], gmask)
            dsk = -(p_sink[h] * delta[h])
            rows = [jnp.broadcast_to(jnp.sum(dsk[g * BLOCK:(g + 1) * BLOCK], axis=0, keepdims=True), (1, PAIR))
                    for g in range(GROUP)]
            dsinks.append(jnp.concatenate(rows + [jnp.zeros((8 - GROUP, PAIR), F32)], axis=0))

        @pl.when(n == 0)
        def _():
            for h in KV_HEADS:
                dsink_ref[h] = dsinks[h]

        @pl.when(n > 0)
        def _():
            for h in KV_HEADS:
                dsink_ref[h] += dsinks[h]

    kv = jax.ShapeDtypeStruct((lp, KV_DIM), F32)
    return pl.pallas_call(
        body, name="swa_bwd", grid=(lp // BLOCK,), in_specs=[qspec, prev, cur, prev, cur, rep, sink, qspec],
        out_specs=[qspec, cur, cur, cur, cur, sink],
        out_shape=[jax.ShapeDtypeStruct((lp, D_MODEL), F32), kv, kv, kv, kv,
                   jax.ShapeDtypeStruct((N_HEADS_KV, 8, PAIR), F32)],
        compiler_params=_attn_params(),
    )(q, k, k, v, v, _spread_matrices(), sinks_b, do)


def _pick_tm(lp, want):
    for tm in (384, 192, 128, 64):
        if tm <= want and lp % tm == 0:
            return tm
    raise ValueError(lp)


def _acc(shape):
    return (tuple(shape), F32)


def _ff_one(w):
    return (w, (None, D_MODEL, D_MODEL), lambda c, i: (c, 0, 0))


def _mlp_layer_fwd(name, h, wup, wdown, lg, lb, tm):
    def fn(c, i, h, wup, wdown, lg, lb):
        out = None
        for s in range(N_FF_CHUNK):
            t = mlp_chunk(wup[s], wdown[s], None, h)[0]
            out = t if out is None else out + t
        z = ALPHA * h + out
        return (_layer_norm(z, lg, lb), z), ()

    (h_out, z), _ = rowwise(name, fn, [h], [wup, wdown, lg, lb], [(D_MODEL, F32), (D_MODEL, F32)], [], tm)
    return h_out, z


def _mlp_layer_bwd(name, h_in, z, dh_parts, wup, wdown, lg, lb, tm):
    n_parts = len(dh_parts)

    def fn_ln(c, i, z, *rest):
        dh = rest[0]
        for extra in rest[1:n_parts]:
            dh = dh + extra
        _, vjp = jax.vjp(_layer_norm, z, rest[n_parts], rest[n_parts + 1])
        dz, dlg, dlb = vjp(dh)
        return (dz,), (dlg, dlb)

    (dz,), (dlg, dlb) = rowwise(name + "_ln", fn_ln, [z] + list(dh_parts), [lg, lb], [(D_MODEL, F32)],
                                [_acc((1, D_MODEL)), _acc((1, D_MODEL))], tm)

    def fn_mlp(c, i, h, dz, wup, wdown):
        tile = h.shape[0]
        (dx,), dws = vjp_taps(functools.partial(mlp_chunk, wup, wdown), [(tile, D_MODEL)] * 2, [h], dz)
        return (dx,), dws

    aspec = ((N_FF_CHUNK, D_MODEL, D_MODEL), F32, (None, D_MODEL, D_MODEL), lambda c, i: (c, 0, 0))
    (dx,), (dwup, dwdown) = rowwise(name + "_mm", fn_mlp, [h_in, dz], [_ff_one(wup), _ff_one(wdown)],
                                    [(D_MODEL, F32, True)], [aspec, aspec], tm, nc=N_FF_CHUNK)
    return dz, dx, dwup, dwdown, dlg, dlb


def _sum_parts(dz, dx):
    out = ALPHA * dz
    for s in range(N_FF_CHUNK):
        out = out + dx[s]
    return out


def local_step(x, loss_target, p, late=None, early_hook=None):
    seq = x.shape[0]
    lp = TOK0 + seq
    tm = _pick_tm(lp, 384)
    tms = _pick_tm(lp, 192)
    e, et = _head_matrices()
    h0 = jnp.concatenate([jnp.zeros((PAD_FRONT, D_MODEL), F32), p["meta_tokens"], x], axis=0)
    pos = jnp.maximum(jnp.arange(lp, dtype=F32) - PAD_FRONT, 0.0)
    inv_freq = 1.0 / (ROPE_THETA ** (jnp.arange(0, HEAD_DIM, 2, dtype=F32) / HEAD_DIM))
    ang = pos[:, None] * inv_freq[None, :]
    cos = jnp.tile(jnp.cos(ang), (1, PAIR // (HEAD_DIM // 2)))
    sin = jnp.tile(jnp.sin(ang), (1, PAIR // (HEAD_DIM // 2)))

    pre_vec = [p["a_mu"][j:j + 1] for j in range(6)] + [p["a_w0"], p["a_a0"], p["a_k_k"], p["a_k_a"]]
    pre_w = [p["a_w_r"], p["a_w_k"], p["a_w_v"], p["a_w1"], p["a_w2"], p["a_a1"], p["a_a2"], p["a_g1"], p["a_g2"]]
    n_vec = len(pre_vec)

    def fn_pre(c, i, h, before, e, et, *ws):
        return rwkv_pre(e, et, ws[n_vec:], None, h, _shift_down(h, before, i), *ws[:n_vec])[0], ()

    (r, lw, k2, v, an, bn, g), _, *pre_gathered = rowwise(
        "rwkv_pre", fn_pre, [h0, _halo_before(h0, tms)], [e, et] + pre_vec + pre_w, [(D_MODEL, F32)] * 7, [], tms,
        hosted=hosted_gather(late[0][0]) if late else None)
    (y, *scan_saved), scan_gathered = scan_fwd(r, lw, k2, v, an, bn, late[1][0] if late else ())
    if late:
        p = {**p, **late[0][1](pre_gathered[0]), **late[1][1](scan_gathered)}

    post_c = [p["a_w_o"], p["a_gn_w"], p["a_gn_b"], p["a_r_k"], p["ln_g00"], p["ln_b00"]]

    def fn_post(c, i, y, r, k2, v, g, h0, e, et, w_o, *vecs):
        return (rwkv_post(e, et, w_o, None, y, r, k2, v, g, h0, *vecs)[0],), ()

    (h1,), _ = rowwise("rwkv_post", fn_post, [y, r, k2, v, g, h0], [e, et] + post_c, [(D_MODEL, F32)], [], tm)
    h2, z2 = _mlp_layer_fwd("mlp0_fwd", h1, p["mlp_up0"], p["mlp_down0"], p["ln_g01"], p["ln_b01"], tm)

    qkv_w = [p["b_w_q"], p["kv_w_k"], p["kv_w_v"]]

    def fn_qkv(c, i, h, cos, sin, wq, wk, wv):
        return qkv_proj(cos, sin, wq, wk, wv, None, h)[0], ()

    (q, k, vv), _ = rowwise("qkv_proj", fn_qkv, [h2, cos, sin], qkv_w,
                            [(D_MODEL, F32), (KV_DIM, F32), (KV_DIM, F32)], [], tm)
    sinks_b = jnp.broadcast_to(p["b_sinks"].reshape(N_HEADS_KV, GROUP, 1), (N_HEADS_KV, GROUP, PAIR))
    sinks_b = jnp.concatenate([sinks_b, jnp.zeros((N_HEADS_KV, 8 - GROUP, PAIR), F32)], axis=1)
    o = attn_fwd(q, k, vv, sinks_b)

    ao_c = [p["b_w_o"], p["ln_g10"], p["ln_b10"]]

    def fn_ao(c, i, o, h, w_o, lg, lb):
        return (attn_out(w_o, None, o, h, lg, lb)[0],), ()

    (h3,), _ = rowwise("attn_out", fn_ao, [o, h2], ao_c, [(D_MODEL, F32)], [], tm)
    h4, z4 = _mlp_layer_fwd("mlp1_fwd", h3, p["mlp_up1"], p["mlp_down1"], p["ln_g11"], p["ln_b11"], tm)

    def fn_loss(c, i, h4, tgt):
        real = (_row_ids(i, TOK0) >= TOK0).astype(F32)
        err = (h4 - tgt) * real
        part = 0.5 * jnp.sum(jnp.sum(err * err, axis=-1, keepdims=True), axis=0, keepdims=True) / D_MODEL
        return (err * (1.0 / D_MODEL),), (jnp.broadcast_to(part, (8, PAIR)),)

    (dh4,), (loss_acc,) = rowwise("loss", fn_loss, [h4, (loss_target, TOK0, lambda i: jnp.maximum(i - 1, 0))], [],
                                  [(D_MODEL, F32)], [_acc((8, PAIR))], TOK0)
    loss = loss_acc[0, 0]

    grads = {}
    dz4, dx4, grads["mlp_up1"], grads["mlp_down1"], grads["ln_g11"], grads["ln_b11"] = _mlp_layer_bwd(
        "mlp1_bwd", h3, z4, [dh4], p["mlp_up1"], p["mlp_down1"], p["ln_g11"], p["ln_b11"], tm)

    def fn_ao_b(c, i, dz, dx, o, h, w_o, lg, lb):
        (do, dh, dlg, dlb), (dw_o,) = vjp_taps(functools.partial(attn_out, w_o), [(tm, D_MODEL)], [o, h, lg, lb],
                                               _sum_parts(dz, dx))
        return (do, dh), (dw_o, dlg, dlb)

    (do, dh2_a), (grads["b_w_o"], grads["ln_g10"], grads["ln_b10"]) = rowwise(
        "attn_out_bwd", fn_ao_b, [dz4, dx4, o, h2], ao_c, [(D_MODEL, F32)] * 2,
        [_acc((D_MODEL, D_MODEL)), _acc((1, D_MODEL)), _acc((1, D_MODEL))], tm)

    dq, dkc, dkp, dvc, dvp, dsinks = attn_bwd(q, k, vv, sinks_b, do)
    grads["b_sinks"] = dsinks[:, :GROUP, 0].reshape(1, N_HEADS)
    zblk = jnp.zeros((BLOCK, KV_DIM), F32)
    dkp_s = jnp.concatenate([dkp[BLOCK:], zblk], axis=0)
    dvp_s = jnp.concatenate([dvp[BLOCK:], zblk], axis=0)

    def fn_qkv_b(c, i, h, cos, sin, dq, dkc, dkp, dvc, dvp, wq, wk, wv):
        return vjp_taps(functools.partial(qkv_proj, cos, sin, wq, wk, wv),
                        [(tm, D_MODEL), (tm, KV_DIM), (tm, KV_DIM)], [h], (dq, dkc + dkp, dvc + dvp))

    (dh2_q,), (grads["b_w_q"], grads["kv_w_k"], grads["kv_w_v"]) = rowwise(
        "qkv_proj_bwd", fn_qkv_b, [h2, cos, sin, dq, dkc, dkp_s, dvc, dvp_s], qkv_w, [(D_MODEL, F32)],
        [_acc((D_MODEL, D_MODEL)), _acc((D_MODEL, KV_DIM)), _acc((D_MODEL, KV_DIM))], tm)

    dz2, dx2, grads["mlp_up0"], grads["mlp_down0"], grads["ln_g01"], grads["ln_b01"] = _mlp_layer_bwd(
        "mlp0_bwd", h1, z2, [dh2_a, dh2_q], p["mlp_up0"], p["mlp_down0"], p["ln_g01"], p["ln_b01"], tm)

    def fn_post_b(c, i, dz, dx, y, r, k2, v, g, h0, e, et, w_o, *vecs):
        out, dws = vjp_taps(functools.partial(rwkv_post, e, et, w_o), [(tms, D_MODEL)],
                            [y, r, k2, v, g, h0] + list(vecs), _sum_parts(dz, dx))
        return out[:6], tuple(dws) + tuple(out[6:])

    early_srcs = early_hook[0](grads) if early_hook else ()
    (dy, dr_c, dk_c, dv_c, dg, dh0_c), post_g, *early_got = rowwise(
        "rwkv_post_bwd", fn_post_b, [dz2, dx2, y, r, k2, v, g, h0], [e, et] + post_c, [(D_MODEL, F32)] * 6,
        [_acc((D_MODEL, D_MODEL))] + [_acc((1, D_MODEL))] * 5, tms,
        hosted=hosted_pair_exchange(early_srcs) if early_hook else None)
    for name, val in zip(["a_w_o", "a_gn_w", "a_gn_b", "a_r_k", "ln_g00", "ln_b00"], post_g):
        grads[name] = val

    (dr, dlw, dk2, dv, dan, dbn), early_from_chips = scan_bwd(
        r, lw, k2, v, an, bn, scan_saved, dy, (dr_c, dk_c, dv_c),
        early_hook[1](early_srcs, early_got[0]) if early_hook else ())

    def fn_pre_b(c, i, h, before, dr, dlw, dk2, dv, dan, dbn, dg, e, et, *ws):
        hp = _shift_down(h, before, i)
        real = (_row_ids(i, tms) >= PAD_FRONT).astype(F32)
        cot = tuple(t * real for t in (dr, dlw, dk2, dv, dan, dbn, dg))
        out, dws = vjp_taps(functools.partial(rwkv_pre, e, et, ws[n_vec:]), [(tms, n) for n in PRE_TAPS],
                            [h, hp] + list(ws[:n_vec]), cot)
        return out[:2], tuple(out[2:]) + tuple(dws)

    (dh0_p, dhp), pre_g = rowwise(
        "rwkv_pre_bwd", fn_pre_b, [h0, _halo_before(h0, tms), dr, dlw, dk2, dv, dan, dbn, dg],
        [e, et] + pre_vec + pre_w, [(D_MODEL, F32)] * 2,
        [_acc((1, D_MODEL))] * n_vec + [_acc(w.shape) for w in pre_w], tms)
    grads["a_mu"] = jnp.concatenate(pre_g[:6], axis=0)
    for name, val in zip(["a_w0", "a_a0", "a_k_k", "a_k_a", "a_w_r", "a_w_k", "a_w_v", "a_w1", "a_w2", "a_a1",
                          "a_a2", "a_g1", "a_g2"], pre_g[6:]):
        grads[name] = val

    def fn_add(c, i, a, b, d, after):
        return (a + b + _shift_up(d, after, i, lp // tm),), ()

    (dh0,), _ = rowwise("grad_h0", fn_add, [dh0_c, dh0_p, dhp, _halo_after(dhp, tm)], [], [(D_MODEL, F32)], [], tm)
    grads["meta_tokens"] = dh0[PAD_FRONT:TOK0]
    return loss, dh0[TOK0:], grads, early_from_chips


ANY = pl.BlockSpec(memory_space=pl.ANY)
XY_FLIPS = ((0, 1), (1, 0), (1, 1))
ALL_FLIPS = tuple((e >> 2 & 1, e >> 1 & 1, e & 1) for e in range(1, N_DEV))


def _flip(v, bit):
    return 1 - v if bit else v


def _sem_scratch(n):
    return [pltpu.SemaphoreType.DMA((n,)), pltpu.SemaphoreType.DMA((n,))]


def gather_copies(src, dst, ici_send, ici_recv, d2d_send, d2d_recv):
    npeer = len(XY_FLIPS)
    x, y, c = lax.axis_index("x"), lax.axis_index("y"), lax.axis_index("c")

    def half(ref, k, which):
        h = src[k].shape[0] // 2
        start = which * h
        return ref.at[pl.ds(pl.multiple_of(start, 8) if h % 8 == 0 else start, h)]

    def ici(k, j, slot):
        fx, fy = XY_FLIPS[j]
        return pltpu.make_async_remote_copy(
            src_ref=half(src[k], k, c), dst_ref=half(dst[k].at[slot], k, c), send_sem=ici_send.at[k * npeer + j],
            recv_sem=ici_recv.at[k * npeer + j], device_id=(_flip(x, fx), _flip(y, fy), c), device_id_type=MESH)

    def d2d(k, j, which):
        fx, fy = XY_FLIPS[j]
        landed = half(dst[k].at[2 * _flip(x, fx) + _flip(y, fy)], k, which)
        return pltpu.make_async_remote_copy(
            src_ref=landed, dst_ref=landed, send_sem=d2d_send.at[k * npeer + j], recv_sem=d2d_recv.at[k * npeer + j],
            device_id=(x, y, 1 - c), device_id_type=MESH)

    pairs = [(k, j) for k in range(len(src)) for j in range(npeer)]
    return ([ici(k, j, 2 * x + y) for k, j in pairs],
            [ici(k, j, 2 * _flip(x, XY_FLIPS[j][0]) + _flip(y, XY_FLIPS[j][1])) for k, j in pairs],
            [d2d(k, j, c) for k, j in pairs], [d2d(k, j, 1 - c) for k, j in pairs])


def gather_scratch(n):
    return _sem_scratch(n * len(XY_FLIPS)) * 2


def gathered_shapes(shards):
    return [jax.ShapeDtypeStruct((N_SHARD,) + s.shape, s.dtype) for s in shards]


def fill_own(gathered, shards):
    if not shards:
        return []
    slot = 2 * lax.axis_index("x") + lax.axis_index("y")
    return [lax.dynamic_update_index_in_dim(g, s, slot, 0) for g, s in zip(gathered, shards)]


def all_gather_shards(shards):
    n = len(shards)

    def body(*refs):
        sends, arrivals, forwards, forwarded = gather_copies(refs[:n], refs[n:2 * n], *refs[2 * n:])
        for cp in sends:
            cp.start()
        for landed, onward in zip(arrivals, forwards):
            landed.wait_recv()
            onward.start()
        for cp in forwarded:
            cp.wait_recv()
        for cp in sends + forwards:
            cp.wait_send()

    out = pl.pallas_call(body, name="gather_weights", in_specs=[ANY] * n, out_specs=[ANY] * n,
                         out_shape=gathered_shapes(shards), scratch_shapes=gather_scratch(n))(*shards)
    return fill_own(out, shards)


def placement():
    x, y, c = lax.axis_index("x"), lax.axis_index("y"), lax.axis_index("c")
    me = 2 * x + y
    others = [j + (j >= me).astype(jnp.int32) for j in range(N_SHARD - 1)]
    return jnp.stack([c, me] + others).astype(jnp.int32)


def hosted_gather(shards):
    return (gather_copies, list(shards), gathered_shapes(shards), gather_scratch(len(shards)),
            lambda got: fill_own(got, shards))


def pair_exchange_copies(src, got, send_sems, recv_sems):
    x, y, c = lax.axis_index("x"), lax.axis_index("y"), lax.axis_index("c")

    def copy(k):
        half = src[k].shape[1] // 2
        theirs = src[k].at[:, pl.ds(pl.multiple_of((1 - c) * half, 8), half), :]
        return pltpu.make_async_remote_copy(
            src_ref=theirs, dst_ref=got[k], send_sem=send_sems.at[k], recv_sem=recv_sems.at[k],
            device_id=(x, y, 1 - c), device_id_type=MESH)

    sends = [copy(k) for k in range(len(src))]
    return sends, sends, [], []


def _half_shapes(sources):
    return [jax.ShapeDtypeStruct((s.shape[0], s.shape[1] // 2, s.shape[2]), s.dtype) for s in sources]


def hosted_pair_exchange(sources):
    return (pair_exchange_copies, list(sources), _half_shapes(sources), _sem_scratch(len(sources)), list)


def pair_exchange(name, sources):
    n = len(sources)

    def body(*refs):
        sends, arrivals, _, _ = pair_exchange_copies(refs[:n], refs[n:2 * n], *refs[2 * n:])
        for cp in sends:
            cp.start()
        for cp in arrivals:
            cp.wait_recv()
        for cp in sends:
            cp.wait_send()

    halves = _half_shapes(sources)
    return pl.pallas_call(body, name=name, in_specs=[ANY] * n, out_specs=[ANY] * n,
                          out_shape=halves, scratch_shapes=_sem_scratch(n))(*sources)


def chip_exchange(parts):
    n = len(parts)

    def body(*refs):
        sends, arrivals = chip_exchange_copies(refs[:n], refs[n:2 * n], *refs[2 * n:])
        for cp in sends:
            cp.start()
        for cp in arrivals:
            cp.wait_recv()
        for cp in sends:
            cp.wait_send()

    return pl.pallas_call(
        body, name="grads_chip_exchange", in_specs=[ANY] * n, out_specs=[ANY] * n,
        out_shape=[jax.ShapeDtypeStruct(p.shape, p.dtype) for p in parts],
        scratch_shapes=_sem_scratch(n * len(XY_FLIPS)),
    )(*parts)


def chip_exchange_copies(src, dst, send_sems, recv_sems):
    npeer = len(XY_FLIPS)
    x, y, c = lax.axis_index("x"), lax.axis_index("y"), lax.axis_index("c")
    me = 2 * x + y

    def copy(k, j, sending):
        fx, fy = XY_FLIPS[j]
        px, py = _flip(x, fx), _flip(y, fy)
        peer = 2 * px + py
        return pltpu.make_async_remote_copy(
            src_ref=src[k].at[peer], dst_ref=dst[k].at[me if sending else peer],
            send_sem=send_sems.at[k * npeer + j], recv_sem=recv_sems.at[k * npeer + j],
            device_id=(px, py, c), device_id_type=MESH)

    pairs = [(k, j) for k in range(len(src)) for j in range(npeer)]
    return [copy(k, j, True) for k, j in pairs], [copy(k, j, False) for k, j in pairs]


def sibling_share(halves):
    n = len(halves)

    def body(*refs):
        src, got = refs[:n], refs[n:2 * n]
        send_sems, recv_sems = refs[2 * n:]
        x, y, c = lax.axis_index("x"), lax.axis_index("y"), lax.axis_index("c")
        sends = [pltpu.make_async_remote_copy(
            src_ref=src[k], dst_ref=got[k], send_sem=send_sems.at[k], recv_sem=recv_sems.at[k],
            device_id=(x, y, 1 - c), device_id_type=MESH) for k in range(n)]
        for cp in sends:
            cp.start()
        for cp in sends:
            cp.wait_recv()
        for cp in sends:
            cp.wait_send()

    return pl.pallas_call(
        body, name="grads_sibling_share", in_specs=[ANY] * n, out_specs=[ANY] * n,
        out_shape=[jax.ShapeDtypeStruct(h.shape, h.dtype) for h in halves], scratch_shapes=_sem_scratch(n),
    )(*halves)


ADD_TILE_ELEMS = 512 * 1024


def _row_tile(rows, cols):
    return max(t for t in range(8, rows + 1, 8) if rows % t == 0 and t * cols <= ADD_TILE_ELEMS)


def _prefetch_call(body, name, place, grid, in_specs, out_specs, out_shape, args):
    return pl.pallas_call(
        body, name=name, out_shape=out_shape,
        grid_spec=pltpu.PrefetchScalarGridSpec(num_scalar_prefetch=1, grid=grid, in_specs=in_specs,
                                               out_specs=out_specs),
        compiler_params=pltpu.CompilerParams(dimension_semantics=("arbitrary",) * len(grid),
                                             vmem_limit_bytes=VMEM_LIMIT),
    )(place, *args)


def pair_add(name, place, src, got, dtype):
    n4, half, cols = got.shape
    tile = _row_tile(half, cols)
    nt = half // tile

    def body(pr, a_ref, b_ref, o_ref):
        o_ref[...] = (a_ref[...] + b_ref[...]).astype(o_ref.dtype)

    mine = pl.BlockSpec((None, tile, cols), lambda s, i, pr: (s, pr[0] * nt + i, 0))
    blk = pl.BlockSpec((None, tile, cols), lambda s, i, pr: (s, i, 0))
    return _prefetch_call(body, name, place, (n4, nt), [mine, blk], blk,
                          jax.ShapeDtypeStruct(got.shape, dtype), (src, got))


def chip_add(name, place, part, from_chips):
    _, half, cols = part.shape
    tile = _row_tile(half, cols)

    def body(pr, own_ref, r0_ref, r1_ref, r2_ref, o_ref):
        me = pr[1]
        own, r0, r1, r2 = (r[...].astype(F32) for r in (own_ref, r0_ref, r1_ref, r2_ref))
        t0 = jnp.where(me == 0, own, r0)
        t1 = jnp.where(me == 0, r0, jnp.where(me == 1, own, r1))
        t2 = jnp.where(me <= 1, r1, jnp.where(me == 2, own, r2))
        t3 = jnp.where(me == 3, own, r2)
        o_ref[...] = ((t0 + t1) + t2) + t3

    def slab(j):
        return pl.BlockSpec((None, tile, cols), lambda i, pr: (pr[j], i, 0))

    return _prefetch_call(body, name, place, (half // tile,), [slab(1), slab(2), slab(3), slab(4)],
                          pl.BlockSpec((tile, cols), lambda i, pr: (i, 0)),
                          jax.ShapeDtypeStruct((half, cols), F32), (part, from_chips, from_chips, from_chips))


def pair_adds(tag, place, sources, got, narrow):
    return [pair_add(f"grads_pair_add_{tag}{k}", place, s, g, BF16 if nar else F32)
            for k, (s, g, nar) in enumerate(zip(sources, got, narrow))]


def finish_sums(place, parts, from_chips):
    halves = [chip_add(f"grads_chip_add{k}", place, p, f) for k, (p, f) in enumerate(zip(parts, from_chips))]
    return list(zip(halves, sibling_share(halves)))


ADAM_ROWS = 256


def adamw_update(name, place, halves, w, m, v):
    nsub, rows, cols = w.shape
    half = rows // 2
    tr = ADAM_ROWS if half % ADAM_ROWS == 0 else half
    nth = half // tr

    def body(pr, *refs):
        g_refs, (w_ref, m_ref, v_ref, g_ref, d_ref, nm_ref, nv_ref) = refs[:2 * nsub], refs[2 * nsub:]
        l = pl.program_id(0)
        mine = (pl.program_id(1) // nth) == pr[0]
        g = None
        for s in range(nsub):
            gs = jnp.where(mine, g_refs[2 * s][...], g_refs[2 * s + 1][...])
            g = gs if g is None else jnp.where(l == s, gs, g)
        m2 = ADAM_B1 * m_ref[...] + (1.0 - ADAM_B1) * g
        v2 = ADAM_B2 * v_ref[...] + (1.0 - ADAM_B2) * (g * g)
        m_hat = m2 / (1.0 - ADAM_B1 ** ADAM_STEP)
        v_hat = v2 / (1.0 - ADAM_B2 ** ADAM_STEP)
        g_ref[...] = g
        d_ref[...] = -ADAM_LR * (m_hat / (jnp.sqrt(v_hat) + ADAM_EPS) + ADAM_WD * w_ref[...])
        nm_ref[...] = m2
        nv_ref[...] = v2

    gblk = pl.BlockSpec((tr, cols), lambda l, i, pr: (i % nth, 0))
    blk = pl.BlockSpec((None, tr, cols), lambda l, i, pr: (l, i, 0))
    out = jax.ShapeDtypeStruct((nsub, rows, cols), F32)
    return _prefetch_call(body, name, place, (nsub, rows // tr), [gblk] * (2 * nsub) + [blk] * 3, [blk] * 4,
                          [out] * 4, [h for pair in halves for h in pair] + [w, m, v])


WEIGHT_NAMES = ("meta_tokens", "a_mu", "a_w_r", "a_w_k", "a_w_v", "a_w_o", "a_w0", "a_w1", "a_w2", "a_a0", "a_a1",
                "a_a2", "a_g1", "a_g2", "a_k_k", "a_k_a", "a_r_k", "a_gn_w", "a_gn_b", "kv_w_k", "kv_w_v", "b_w_q",
                "b_sinks", "b_w_o", "mlp_w_up", "mlp_w_down", "ln_g", "ln_b")
BIG_NAMES = ("a_w_r", "a_w_k", "a_w_v", "a_w_o", "b_w_q", "b_w_o")
EARLY_NAMES, LATE_NAMES = BIG_NAMES[:3], BIG_NAMES[3:]
PACK_MATS = (("kv_w_k", 256), ("kv_w_v", 256), ("a_w1", 64), ("a_a1", 64), ("a_g1", 128), ("a_w2", 64),
             ("a_a2", 64), ("a_g2", 128))
COLUMN_CUT = ("a_w2", "a_a2", "a_g2")
PACK_VECS = (("a_mu", 6), ("a_w0", 1), ("a_a0", 1), ("a_k_k", 1), ("a_k_a", 1), ("a_gn_w", 1), ("a_gn_b", 1),
             ("ln_g", 4), ("ln_b", 4), ("meta_tokens", 16))
PACK_REPL = (("a_r_k", 4), ("b_sinks", 1))
SHARD_W = D_MODEL // N_SHARD


def _tiles(rows):
    return -(-rows // SUBLANES) * SUBLANES


N_MAT_ROWS = sum(_tiles(r) for _, r in PACK_MATS)
N_VEC_ROWS = sum(_tiles(r) for _, r in PACK_VECS)
N_PACK_ROWS = -(-(N_MAT_ROWS + N_VEC_ROWS + sum(_tiles(r) for _, r in PACK_REPL)) // 16) * 16
N_GATHER_VEC_ROWS = -(-N_VEC_ROWS // 16) * 16


def _pad_rows(arr, axis):
    rows = arr.shape[axis]
    pad = [(0, 0)] * arr.ndim
    pad[axis] = (0, _tiles(rows) - rows)
    return jnp.pad(arr, pad) if _tiles(rows) != rows else arr


def _pack_rows(arr):
    if arr.size == N_HEADS:
        arr = jnp.pad(arr.reshape(1, N_HEADS), ((0, 0), (0, SHARD_W - N_HEADS)))
    return _pad_rows(arr.reshape(-1, SHARD_W), 0)


def pack_small(get):
    parts = [_pack_rows(get(name)) for name, _ in PACK_MATS + PACK_VECS + PACK_REPL]
    used = sum(p.shape[0] for p in parts)
    return jnp.concatenate(parts + [jnp.zeros((N_PACK_ROWS - used, SHARD_W), F32)], axis=0)


def unpack_small(pack, shapes):
    out, off = {}, 0
    for name, rows in PACK_MATS + PACK_VECS + PACK_REPL:
        piece = pack[off:off + rows]
        off += _tiles(rows)
        out[name] = piece[:, :N_HEADS].reshape(shapes[name]) if name == "b_sinks" else piece.reshape(shapes[name])
    return out


def whole_weights(big_names, gathered_big, mats, vecs, a_r_k, b_sinks):
    p = {name: g.reshape(D_MODEL, D_MODEL) for name, g in zip(big_names, gathered_big)}
    off = 0
    for name, rows in PACK_MATS:
        piece = mats[:, off:off + rows]
        off += rows
        if name in COLUMN_CUT:
            p[name] = piece.transpose(1, 0, 2).reshape(rows, D_MODEL)
        else:
            p[name] = piece.reshape(D_MODEL, rows)
    v = vecs.transpose(1, 0, 2).reshape(-1, D_MODEL)
    off = 0
    for name, rows in PACK_VECS:
        p[name] = v[off:off + rows]
        off += _tiles(rows)
    for i in range(2):
        for j in range(2):
            p[f"ln_g{i}{j}"] = p["ln_g"][2 * i + j:2 * i + j + 1]
            p[f"ln_b{i}{j}"] = p["ln_b"][2 * i + j:2 * i + j + 1]
    p["a_r_k"] = a_r_k.reshape(1, D_MODEL)
    p["b_sinks"] = b_sinks
    return p


def small_grad_pack(g):
    parts = []
    for name, rows in PACK_MATS:
        if name in COLUMN_CUT:
            parts.append(g[name].reshape(rows, N_SHARD, SHARD_W).transpose(1, 0, 2))
        else:
            parts.append(g[name].reshape(N_SHARD, rows, SHARD_W))
    vecs = {n: g[n] for n in ("a_mu", "a_w0", "a_a0", "a_k_k", "a_k_a", "a_gn_w", "a_gn_b", "meta_tokens")}
    vecs["ln_g"] = jnp.concatenate([g[f"ln_g{i}{j}"] for i in range(2) for j in range(2)], axis=0)
    vecs["ln_b"] = jnp.concatenate([g[f"ln_b{i}{j}"] for i in range(2) for j in range(2)], axis=0)
    for name, rows in PACK_VECS:
        parts.append(_pad_rows(vecs[name].reshape(rows, N_SHARD, SHARD_W).transpose(1, 0, 2), 1))
    r_k = jnp.broadcast_to(g["a_r_k"].reshape(1, -1, SHARD_W), (N_SHARD, D_MODEL // SHARD_W, SHARD_W))
    sinks = jnp.pad(g["b_sinks"].reshape(1, 1, N_HEADS), ((0, 0), (0, 0), (0, SHARD_W - N_HEADS)))
    parts += [_pad_rows(r_k, 1), _pad_rows(jnp.broadcast_to(sinks, (N_SHARD, 1, SHARD_W)), 1)]
    used = sum(p.shape[1] for p in parts)
    parts.append(jnp.zeros((N_SHARD, N_PACK_ROWS - used, SHARD_W), F32))
    return jnp.concatenate(parts, axis=1)


def train_step(vals):
    w = {n: vals[n] for n in WEIGHT_NAMES}
    w_pack = pack_small(lambda n: w[n])
    early = [w[n][0].astype(BF16) for n in EARLY_NAMES]
    early += [w_pack[:N_MAT_ROWS].astype(BF16), w_pack[N_MAT_ROWS:N_MAT_ROWS + N_GATHER_VEC_ROWS]]
    gathered = all_gather_shards(early)
    ne = len(EARLY_NAMES)
    p = whole_weights(EARLY_NAMES, gathered[:ne], gathered[ne], gathered[ne + 1][:, :N_VEC_ROWS], w["a_r_k"],
                      w["b_sinks"])
    nb = len(BIG_NAMES)

    def late_set(big, layer):
        shards = [w[n][0].astype(BF16) for n in big]
        shards += [w["mlp_w_up"][layer].astype(BF16), w["mlp_w_down"][layer].astype(BF16)]

        def weights(got):
            out = {n: x.reshape(D_MODEL, D_MODEL) for n, x in zip(big, got)}
            out[f"mlp_up{layer}"], out[f"mlp_down{layer}"] = got[len(big):]
            return out

        return shards, weights

    late = (late_set((), 1), late_set(LATE_NAMES, 0))

    place = placement()
    ready = {}
    a_names, b_names = BIG_NAMES[:4], BIG_NAMES[4:]

    def early_sources(g):
        return ([g[n].reshape(N_SHARD, SHARD_W, D_MODEL) for n in b_names]
                + [g["mlp_up0"], g["mlp_up1"], g["mlp_down0"], g["mlp_down1"]])

    def early_parts(srcs, got):
        ready["parts"] = pair_adds("early", place, srcs, got, [True] * len(srcs))
        return ready["parts"]

    loss, gx, g, early_from_chips = local_step(vals["x"][0], vals["loss_target"][0], p, late,
                                               (early_sources, early_parts))
    loss = lax.psum(loss, ("x", "y", "c"))
    srcs = [g[n].reshape(N_SHARD, SHARD_W, D_MODEL) for n in a_names] + [small_grad_pack(g)]
    rest = pair_adds("late", place, srcs, pair_exchange("grads_pair_exchange", srcs), [True] * len(a_names) + [False])
    rest_from_chips = chip_exchange(rest)
    na = len(a_names)
    halves = finish_sums(place, rest[:na] + ready["parts"] + rest[na:],
                         list(rest_from_chips[:na]) + list(early_from_chips) + list(rest_from_chips[na:]))

    res = {}
    for k, n in enumerate(BIG_NAMES):
        res[n] = adamw_update("adamw_" + n, place, halves[k:k + 1], w[n], vals["m_" + n], vals["v_" + n])
    for k, n in ((nb, "mlp_w_up"), (nb + 2, "mlp_w_down")):
        res[n] = adamw_update("adamw_" + n, place, halves[k:k + 2], w[n], vals["m_" + n], vals["v_" + n])
    packs = adamw_update("adamw_small", place, halves[-1:], w_pack[None], pack_small(lambda n: vals["m_" + n])[None],
                         pack_small(lambda n: vals["v_" + n])[None])
    shapes = {n: w[n].shape for n in WEIGHT_NAMES}
    small = [unpack_small(pk[0], shapes) for pk in packs]
    outs = [loss, gx[None]]
    for t in range(4):
        outs += [res[n][t] if n in res else small[t][n] for n in WEIGHT_NAMES]
    return tuple(outs)


def kernel(x, meta_tokens, a_mu, a_w_r, a_w_k, a_w_v, a_w_o, a_w0, a_w1, a_w2, a_a0, a_a1, a_a2, a_g1, a_g2, a_k_k,
           a_k_a, a_r_k, a_gn_w, a_gn_b, kv_w_k, kv_w_v, b_w_q, b_sinks, b_w_o, mlp_w_up, mlp_w_down, ln_g, ln_b,
           loss_target, m_meta_tokens, m_a_mu, m_a_w_r, m_a_w_k, m_a_w_v, m_a_w_o, m_a_w0, m_a_w1, m_a_w2, m_a_a0,
           m_a_a1, m_a_a2, m_a_g1, m_a_g2, m_a_k_k, m_a_k_a, m_a_r_k, m_a_gn_w, m_a_gn_b, m_kv_w_k, m_kv_w_v,
           m_b_w_q, m_b_sinks, m_b_w_o, m_mlp_w_up, m_mlp_w_down, m_ln_g, m_ln_b, v_meta_tokens, v_a_mu, v_a_w_r,
           v_a_w_k, v_a_w_v, v_a_w_o, v_a_w0, v_a_w1, v_a_w2, v_a_a0, v_a_a1, v_a_a2, v_a_g1, v_a_g2, v_a_k_k,
           v_a_k_a, v_a_r_k, v_a_gn_w, v_a_gn_b, v_kv_w_k, v_kv_w_v, v_b_w_q, v_b_sinks, v_b_w_o, v_mlp_w_up,
           v_mlp_w_down, v_ln_g, v_ln_b):
    return train_step(dict(locals()))
```

```python
import functools

import numpy as np
import jax
import jax.numpy as jnp
from jax import lax
from jax.experimental import pallas as pl
from jax.experimental.pallas import tpu as pltpu

F32 = jnp.float32
BF16 = jnp.bfloat16

D_MODEL = 1024
N_HEADS = 16
HEAD_DIM = 64
N_HEADS_KV = 4
GROUP = 4
KV_DIM = N_HEADS_KV * HEAD_DIM
N_META = 16
BLOCK = 128
PAD_FRONT = BLOCK - N_META
TOK0 = PAD_FRONT + N_META
N_FF_CHUNK = 4
N_SHARD = 4
N_DEV = 8
GN_EPS = 64e-5
LN_EPS = 1e-5
ROPE_THETA = 10000.0
ALPHA = 4.0 ** 0.25
ADAM_LR, ADAM_B1, ADAM_B2, ADAM_EPS, ADAM_WD, ADAM_STEP = 0.001, 0.9, 0.999, 1e-08, 0.01, 10
SCAN_T = 64
PAIR = 128
KVW = GROUP * HEAD_DIM
VMEM_LIMIT = 60 * 1024 * 1024
HI = lax.Precision.HIGHEST
MESH = pl.DeviceIdType.MESH


def _dot(a, b, ca, cb):
    return lax.dot_general(a.astype(BF16), b.astype(BF16), (((ca,), (cb,)), ((), ())),
                           preferred_element_type=F32)


@jax.custom_vjp
def mm(a, b):
    return _dot(a, b, 1, 0)


def _mm_fwd(a, b):
    return mm(a, b), b


def _mm_bwd(b, g):
    return _dot(g, b, 1, 1), jnp.zeros_like(b)


mm.defvjp(_mm_fwd, _mm_bwd)


@jax.custom_vjp
def mm_tap(a, b, tap):
    return _dot(a, b, 1, 0)


mm_tap.defvjp(lambda a, b, tap: (_dot(a, b, 1, 0), b), lambda b, g: (_dot(g, b, 1, 1), jnp.zeros_like(b), g))


def tmm(x, w, taps, xs):
    y = mm(x, w) if taps is None else mm_tap(x, w, taps[len(xs)])
    xs.append(x)
    return y


def vjp_taps(core, tap_shapes, args, cot):
    taps = [jnp.zeros(s, F32) for s in tap_shapes]
    _, vjp, xs = jax.vjp(core, taps, *args, has_aux=True)
    out = vjp(cot)
    return out[1:], [_dot(x, g, 0, 0) for x, g in zip(xs, out[0])]


def _split3(x):
    x1 = x.astype(BF16)
    r1 = x - x1.astype(F32)
    x2 = r1.astype(BF16)
    x3 = (r1 - x2.astype(F32)).astype(BF16)
    return x1, x2, x3


def _exact_dot(x, m01, cb=0):
    acc = None
    for piece in _split3(x):
        t = lax.dot_general(piece, m01, (((1,), (cb,)), ((), ())), preferred_element_type=F32)
        acc = t if acc is None else acc + t
    return acc


def _head_matrices():
    e = np.zeros((D_MODEL, N_HEADS), np.float32)
    e[np.arange(D_MODEL), np.arange(D_MODEL) // HEAD_DIM] = 1.0
    return jnp.asarray(e, BF16), jnp.asarray(e.T, BF16)


@jax.custom_vjp
def hsum(x, e, et):
    return _exact_dot(x, e)


@jax.custom_vjp
def hbc(s, e, et):
    return _exact_dot(s, et)


hsum.defvjp(lambda x, e, et: (_exact_dot(x, e), (e, et)),
            lambda res, g: (hbc(g, *res), jnp.zeros_like(res[0]), jnp.zeros_like(res[1])))
hbc.defvjp(lambda s, e, et: (_exact_dot(s, et), (e, et)),
           lambda res, g: (hsum(g, *res), jnp.zeros_like(res[0]), jnp.zeros_like(res[1])))


def _sigmoid(u):
    return 0.5 * (jnp.tanh(0.5 * u) + 1.0)


def _softplus(u):
    return jnp.maximum(u, 0.0) + jnp.log(1.0 + jnp.exp(-jnp.abs(u)))


def _layer_norm(z, g, b):
    mu = jnp.mean(z, axis=-1, keepdims=True)
    zc = z - mu
    var = jnp.mean(zc * zc, axis=-1, keepdims=True)
    return zc * lax.rsqrt(var + LN_EPS) * g + b


def _zero_map(nd):
    return lambda c, i: (0,) * nd


def _params():
    return pltpu.CompilerParams(dimension_semantics=("arbitrary", "arbitrary"), vmem_limit_bytes=VMEM_LIMIT)


def rowwise(name, fn, rows, consts, out_rows, out_accs, tm, nc=1, hosted=None):
    lp = rows[0].shape[-2]
    nt = lp // tm
    assert nt * tm == lp, (name, lp, tm)
    copies_fn, hosted_src, hosted_shapes, hosted_scratch, hosted_post = hosted or (None, (), [], [], None)
    ng = len(hosted_src)
    in_specs, args = [], []
    for a in rows:
        if isinstance(a, tuple):
            a, block_rows, block_index = a
            in_specs.append(pl.BlockSpec((block_rows, a.shape[1]),
                                         functools.partial(lambda f, c, i: (f(i), 0), block_index)))
        elif a.ndim == 2:
            in_specs.append(pl.BlockSpec((tm, a.shape[1]), lambda c, i: (i, 0)))
        else:
            in_specs.append(pl.BlockSpec((a.shape[0], tm, a.shape[2]), lambda c, i: (0, i, 0)))
        args.append(a)
    for cst in consts:
        if isinstance(cst, tuple):
            arr, bs, im = cst
            in_specs.append(pl.BlockSpec(bs, im))
        else:
            arr = cst
            in_specs.append(pl.BlockSpec(arr.shape, _zero_map(arr.ndim), pipeline_mode=pl.Buffered(1)))
        args.append(arr)
    out_shape, out_specs, acc_per_chunk = [], [], []
    for spec in out_rows:
        if len(spec) == 3 and spec[2]:
            out_shape.append(jax.ShapeDtypeStruct((nc, lp, spec[0]), spec[1]))
            out_specs.append(pl.BlockSpec((None, tm, spec[0]), lambda c, i: (c, i, 0)))
        else:
            out_shape.append(jax.ShapeDtypeStruct((lp, spec[0]), spec[1]))
            out_specs.append(pl.BlockSpec((tm, spec[0]), lambda c, i: (i, 0)))
    for spec in out_accs:
        out_shape.append(jax.ShapeDtypeStruct(spec[0], spec[1]))
        if len(spec) == 4:
            out_specs.append(pl.BlockSpec(spec[2], spec[3]))
            acc_per_chunk.append(True)
        else:
            out_specs.append(pl.BlockSpec(spec[0], _zero_map(len(spec[0])), pipeline_mode=pl.Buffered(1)))
            acc_per_chunk.append(False)
    n_in, n_or, n_out = len(args), len(out_rows), len(out_shape)

    def body(*refs):
        c = pl.program_id(0)
        i = pl.program_id(1)
        if ng:
            src, dst = refs[n_in:n_in + ng], refs[n_in + ng + n_out:n_in + 2 * ng + n_out]
            sends, arrivals, forwards, forwarded = copies_fn(src, dst, *refs[n_in + 2 * ng + n_out:])

            @pl.when(jnp.logical_and(c == 0, i == 0))
            def _():
                for cp in sends:
                    cp.start()

        vals = [r[...] for r in refs[:n_in]]
        outs_r, outs_a = fn(c, i, *vals)
        out_refs = refs[n_in + ng:n_in + ng + n_out]
        for ref, val in zip(out_refs[:n_or], outs_r):
            ref[...] = val.astype(ref.dtype)
        for ref, val, per_chunk in zip(out_refs[n_or:], outs_a, acc_per_chunk):
            first = (i == 0) if per_chunk else jnp.logical_and(i == 0, c == 0)

            @pl.when(first)
            def _():
                ref[...] = val.astype(ref.dtype)

            @pl.when(jnp.logical_not(first))
            def _():
                ref[...] += val.astype(ref.dtype)

        if ng:
            @pl.when(jnp.logical_and(c == nc - 1, i == max(nt - 3, 0)))
            def _():
                for k, landed in enumerate(arrivals):
                    landed.wait_recv()
                    if forwards:
                        forwards[k].start()

            @pl.when(jnp.logical_and(c == nc - 1, i == nt - 1))
            def _():
                for cp in forwarded:
                    cp.wait_recv()
                for cp in sends + forwards:
                    cp.wait_send()

    outs = pl.pallas_call(body, name=name, grid=(nc, nt), in_specs=in_specs + [ANY] * ng,
                          out_specs=out_specs + [ANY] * ng, out_shape=out_shape + list(hosted_shapes),
                          scratch_shapes=list(hosted_scratch), compiler_params=_params())(*args, *hosted_src)
    if ng:
        return outs[:n_or], outs[n_or:n_out], hosted_post(outs[n_out:])
    return outs[:n_or], outs[n_or:]


def _row_ids(i, tm):
    return i * tm + lax.broadcasted_iota(jnp.int32, (tm, 1), 0)


SUBLANES = 8


def _halo_before(arr, tm):
    return (arr, SUBLANES, lambda i: jnp.maximum(i * (tm // SUBLANES) - 1, 0))


def _halo_after(arr, tm):
    last = arr.shape[0] // SUBLANES - 1
    return (arr, SUBLANES, lambda i: jnp.minimum((i + 1) * (tm // SUBLANES), last))


def _pick_row(block8, row):
    rows = lax.broadcasted_iota(jnp.int32, block8.shape, 0)
    return jnp.sum(jnp.where(rows == row, block8, 0.0), axis=0, keepdims=True)


def _shift_down(x, before8, i):
    rows = lax.broadcasted_iota(jnp.int32, x.shape, 0)
    top = _pick_row(before8, SUBLANES - 1) * (i > 0).astype(F32)
    return jnp.where(rows == 0, top, pltpu.roll(x, 1, 0))


def _shift_up(x, after8, i, nt):
    rows = lax.broadcasted_iota(jnp.int32, x.shape, 0)
    bottom = _pick_row(after8, 0) * (i < nt - 1).astype(F32)
    return jnp.where(rows == x.shape[0] - 1, bottom, pltpu.roll(x, x.shape[0] - 1, 0))


PRE_TAPS = (D_MODEL, D_MODEL, D_MODEL, 64, D_MODEL, 64, D_MODEL, 128, D_MODEL)


def rwkv_pre(e, et, ws, taps, h, hp, mu_r, mu_w, mu_k, mu_v, mu_a, mu_g, w0, a0, k_k, k_a):
    w_r, w_k, w_v, w1, w2, a1, a2, g1, g2 = ws
    xs = []
    xx = hp - h
    r = tmm(h + xx * mu_r, w_r, taps, xs)
    k = tmm(h + xx * mu_k, w_k, taps, xs)
    v = tmm(h + xx * mu_v, w_v, taps, xs)
    wraw = -_softplus(-(w0 + tmm(jnp.tanh(tmm(h + xx * mu_w, w1, taps, xs)), w2, taps, xs))) - 0.5
    lw = -jnp.exp(wraw)
    a = _sigmoid(a0 + tmm(tmm(h + xx * mu_a, a1, taps, xs), a2, taps, xs))
    g = tmm(_sigmoid(tmm(h + xx * mu_g, g1, taps, xs)), g2, taps, xs)
    kk = k * k_k
    ss = hsum(kk * kk, e, et)
    pos = ss > 0.0
    nrm = jnp.where(pos, jnp.sqrt(jnp.where(pos, ss, 1.0)), 0.0)
    kk = kk * hbc(1.0 / jnp.maximum(nrm, 1e-12), e, et)
    k2 = k * (1.0 + (a - 1.0) * k_a)
    return (r, lw, k2, v, -kk, kk * a, g), xs


def rwkv_post(e, et, w_o, taps, y, r, k2, v, g, h0, gn_w, gn_b, rk, lg, lb):
    xs = []
    inv_n = 1.0 / HEAD_DIM
    yc = y - hbc(hsum(y, e, et) * inv_n, e, et)
    yv = hsum(yc * yc, e, et) * inv_n
    yn = yc * hbc(lax.rsqrt(yv + GN_EPS), e, et) * gn_w + gn_b
    bonus = hbc(hsum(r * k2 * rk, e, et), e, et) * v
    mix = tmm((yn + bonus) * g, w_o, taps, xs)
    return _layer_norm(ALPHA * h0 + mix, lg, lb), xs


@jax.custom_vjp
def sq_relu(x):
    r = jnp.maximum(x, 0.0)
    return r * r


sq_relu.defvjp(lambda x: (sq_relu(x), x), lambda x, g: (g * (2.0 * jnp.maximum(x, 0.0)),))


def mlp_chunk(wup, wdown, taps, h):
    xs = []
    return tmm(sq_relu(tmm(h, wup, taps, xs)), wdown, taps, xs), xs


def _rot_half(t):
    n = t.shape[-1]
    lane = lax.broadcasted_iota(jnp.int32, t.shape, t.ndim - 1)
    lo = (lane % HEAD_DIM) < (HEAD_DIM // 2)
    return jnp.where(lo, -pltpu.roll(t, n - HEAD_DIM // 2, t.ndim - 1), pltpu.roll(t, HEAD_DIM // 2, t.ndim - 1))


@jax.custom_vjp
def rot_half(t):
    return _rot_half(t)


rot_half.defvjp(lambda t: (_rot_half(t), None), lambda _, g: (-_rot_half(g),))


def _tile_lanes(t, width):
    return jnp.concatenate([t] * (width // t.shape[-1]), axis=-1)


def qkv_proj(cos, sin, wq, wk, wv, taps, h):
    xs = []
    q = tmm(h, wq, taps, xs)
    k = tmm(h, wk, taps, xs)
    v = tmm(h, wv, taps, xs)
    cq, sq = _tile_lanes(cos, D_MODEL), _tile_lanes(sin, D_MODEL)
    ck, sk = _tile_lanes(cos, KV_DIM), _tile_lanes(sin, KV_DIM)
    return (q * cq + rot_half(q) * sq, k * ck + rot_half(k) * sk, v), xs


def attn_out(w_o, taps, o, h, lg, lb):
    xs = []
    return _layer_norm(ALPHA * h + tmm(o, w_o, taps, xs), lg, lb), xs


def _scan_consts():
    t = SCAN_T
    tri = np.tril(np.ones((t, t), np.float32))
    rows = np.arange(2 * t)
    same = (rows[:, None] // t) == (rows[None, :] // t)
    strict = same & ((rows[None, :] % t) < (rows[:, None] % t))
    incl = same & ((rows[None, :] % t) <= (rows[:, None] % t))
    lane = np.arange(PAIR)
    masks = np.zeros((8, PAIR), np.float32)
    masks[0] = (lane // HEAD_DIM) == 0
    masks[1] = (lane // HEAD_DIM) == 1
    return (jnp.asarray(tri, BF16), jnp.asarray(strict.astype(np.float32)), jnp.asarray(incl.astype(np.float32)),
            jnp.asarray(masks), jnp.asarray(np.eye(2 * t, dtype=np.float32)))


def _scan_dot(a, b, ca, cb):
    return _dot(a, b, ca, cb)


@functools.partial(jax.custom_vjp, nondiff_argnums=(2, 3))
def _dotf(a, b, ca, cb):
    return _scan_dot(a, b, ca, cb)


def _dotf_bwd(ca, cb, res, g):
    a, b = res
    if ca == 1:
        da = _scan_dot(g, b, 1, 1 - cb)
    else:
        da = _scan_dot(b, g, 1 - cb, 1)
    if cb == 0:
        db = _scan_dot(a, g, 1 - ca, 0)
    else:
        db = _scan_dot(g, a, 0, 1 - ca)
    return da, db


_dotf.defvjp(lambda a, b, ca, cb: (_scan_dot(a, b, ca, cb), (a, b)), _dotf_bwd)


def _tri_dot(tri, x, ct):
    acc = None
    for piece in _split3(x):
        t = lax.dot_general(tri, piece, (((ct,), (0,)), ((), ())), preferred_element_type=F32)
        acc = t if acc is None else acc + t
    return acc


@jax.custom_vjp
def _cumsum_rows(tri, x):
    return _tri_dot(tri, x, 1)


_cumsum_rows.defvjp(lambda tri, x: (_tri_dot(tri, x, 1), tri),
                    lambda tri, g: (jnp.zeros_like(tri), _tri_dot(tri, g, 0)))


@jax.custom_vjp
def _unstack2(x):
    t = x.shape[0] // 2
    return x[:t] + x[t:]


_unstack2.defvjp(lambda x: (_unstack2(x), None), lambda _, g: (jnp.concatenate([g, g], axis=0),))


@jax.custom_vjp
def _last_row(x):
    return x[x.shape[0] - 1:, :]


def _last_row_bwd(_, g):
    rows = lax.broadcasted_iota(jnp.int32, (SCAN_T, g.shape[1]), 0)
    return (jnp.where(rows == SCAN_T - 1, jnp.broadcast_to(g, (SCAN_T, g.shape[1])), 0.0),)


_last_row.defvjp(lambda x: (_last_row(x), None), _last_row_bwd)


@jax.custom_vjp
def _halves(x):
    n = x.shape[0] // 2
    return x[:n], x[n:]


_halves.defvjp(lambda x: (_halves(x), None), lambda _, g: (jnp.concatenate(list(g), axis=0),))


@jax.custom_vjp
def _quads(x):
    n, m = x.shape[0] // 2, x.shape[1] // 2
    return x[:n, :m], x[:n, m:], x[n:, :m], x[n:, m:]


_quads.defvjp(lambda x: (_quads(x), None),
              lambda _, g: (jnp.concatenate([jnp.concatenate([g[0], g[1]], axis=1),
                                             jnp.concatenate([g[2], g[3]], axis=1)], axis=0),))


@jax.custom_vjp
def _solve_saved(n, rhs, minv, u):
    return u


def _solve_saved_bwd(res, du):
    minv, u = res
    drhs = _dotf(minv, du, 0, 0)
    return _dotf(drhs, u, 1, 1), drhs, jnp.zeros_like(minv), jnp.zeros_like(u)


_solve_saved.defvjp(lambda n, rhs, minv, u: (u, (minv, u)), _solve_saved_bwd)


def scan_chunk(tri, strict, incl, m0, m1, eye, r, lw, k, v, a, b, s0, saved=None):
    lower = strict > 0
    lower_incl = incl > 0

    def stack(x):
        return jnp.concatenate([x * m0, x * m1], axis=0)

    def dots(xs, ys, ca, cb, mask=None):
        out = [_dotf(x, y, ca, cb) for x, y in zip(xs, ys)]
        return out if mask is None else [jnp.where(mask, o, 0.0) for o in out]

    cl = [_cumsum_rows(tri, x) for x in lw]
    gam = [jnp.exp(c) for c in cl]
    ginv = [jnp.exp(-c) for c in cl]
    ar_s = [jnp.concatenate([stack(x * jnp.exp(c - w)), stack(y * g)], axis=0)
            for x, c, w, y, g in zip(a, cl, lw, r, gam)]
    bk_s = [jnp.concatenate([stack(x * g), stack(y * g)], axis=0) for x, y, g in zip(b, k, ginv)]
    v_s = [stack(x) for x in v]
    quads = [_quads(x) for x in dots(ar_s, bk_s, 1, 1)]
    n_ab = [jnp.where(lower, q[0], 0.0) for q in quads]
    n_ak = [jnp.where(lower, q[1], 0.0) for q in quads]
    r_ab = [jnp.where(lower_incl, q[2], 0.0) for q in quads]
    r_ak = [jnp.where(lower_incl, q[3], 0.0) for q in quads]
    from_state = [_halves(x) for x in dots(ar_s, s0, 1, 1)]
    rhs = [x[0] + y for x, y in zip(from_state, dots(n_ak, v_s, 1, 0))]
    if saved is None:
        minv = [eye + n for n in n_ab]
        p = n_ab
        for _ in range(5):
            p = dots(p, p, 1, 0)
            minv = [m + mp for m, mp in zip(minv, dots(minv, p, 1, 0))]
        u_s = dots(minv, rhs, 1, 0)
    else:
        minv = saved[0]
        u_s = [_solve_saved(n, x, m, u) for n, x, m, u in zip(n_ab, rhs, *saved)]
    uv_s = [jnp.concatenate([x, y], axis=0) for x, y in zip(u_s, v_s)]
    r_uv = [jnp.concatenate([x, y], axis=1) for x, y in zip(r_ab, r_ak)]
    y = [_unstack2(x[1] + z) for x, z in zip(from_state, dots(r_uv, uv_s, 1, 0))]
    g_end = [_last_row(g) for g in gam]
    s1 = [s * g + x for s, g, x in zip(s0, g_end, dots(uv_s, [x * g for x, g in zip(bk_s, g_end)], 0, 0))]
    return y, s1, (minv, u_s)


SCAN_PAIRS = 8


def _scan_specs(consts, order):
    row = pl.BlockSpec((SCAN_T, PAIR * SCAN_PAIRS), lambda p, c: (order(c), p))
    state = pl.BlockSpec((None, SCAN_PAIRS, PAIR, PAIR), lambda p, c: (order(c), p, 0, 0))
    return row, state, [pl.BlockSpec(x.shape, _zero_map(x.ndim)) for x in consts]


def _pair_lanes(q):
    return slice(q * PAIR, (q + 1) * PAIR)


def scan_fwd(r, lw, k, v, a, b, shards=()):
    lp = r.shape[0]
    nch = lp // SCAN_T
    npair = D_MODEL // PAIR
    ng = len(shards)
    consts = _scan_consts()
    row, state, cspecs = _scan_specs(consts, lambda c: c)

    def body(tri, strict, incl, masks, eye, r_ref, lw_ref, k_ref, v_ref, a_ref, b_ref, *rest):
        src, (y_ref, s_ref, minv_ref, u_ref), dst = rest[:ng], rest[ng:ng + 4], rest[ng + 4:2 * ng + 4]
        carry = rest[2 * ng + 4]
        first = jnp.logical_and(pl.program_id(0) == 0, pl.program_id(1) == 0)
        last = jnp.logical_and(pl.program_id(0) == npair // SCAN_PAIRS - 1, pl.program_id(1) == nch - 1)
        if ng:
            sends, arrivals, forwards, forwarded = gather_copies(src, dst, *rest[2 * ng + 5:])

            @pl.when(first)
            def _():
                for cp in sends:
                    cp.start()

            @pl.when(jnp.logical_and(pl.program_id(0) == npair // SCAN_PAIRS - 1, pl.program_id(1) == nch * 3 // 4))
            def _():
                for landed, onward in zip(arrivals, forwards):
                    landed.wait_recv()
                    onward.start()

        @pl.when(pl.program_id(1) == 0)
        def _():
            carry[...] = jnp.zeros_like(carry)

        pairs = range(SCAN_PAIRS)
        s0 = [carry[q] for q in pairs]
        rows = [[ref[:, _pair_lanes(q)] for q in pairs] for ref in (r_ref, lw_ref, k_ref, v_ref, a_ref, b_ref)]
        y, s1, (minv, u) = scan_chunk(tri[...], strict[...], incl[...], masks[0:1, :], masks[1:2, :], eye[...],
                                      *rows, s0)
        for q in pairs:
            s_ref[q] = s0[q]
            minv_ref[q] = minv[q]
            u_ref[q] = u[q]
            y_ref[:, _pair_lanes(q)] = y[q]
            carry[q] = s1[q]

        if ng:
            @pl.when(last)
            def _():
                for cp in forwarded:
                    cp.wait_recv()
                for cp in sends + forwards:
                    cp.wait_send()

    mats = jax.ShapeDtypeStruct((nch, npair, PAIR, PAIR), F32)
    out = pl.pallas_call(
        body, name="rwkv_scan_fwd", grid=(npair // SCAN_PAIRS, nch), in_specs=cspecs + [row] * 6 + [ANY] * ng,
        out_specs=[row, state, state, state] + [ANY] * ng,
        out_shape=[jax.ShapeDtypeStruct((lp, D_MODEL), F32), mats, mats, mats] + gathered_shapes(shards),
        scratch_shapes=[pltpu.VMEM((SCAN_PAIRS, PAIR, PAIR), F32)] + (gather_scratch(ng) if ng else []),
        compiler_params=_params(),
    )(*consts, r, lw, k, v, a, b, *shards)
    return out[:4], fill_own(out[4:], shards)


def scan_bwd(r, lw, k, v, a, b, saved, dy, direct_grads, parts=()):
    lp = r.shape[0]
    nch = lp // SCAN_T
    npair = D_MODEL // PAIR
    consts = _scan_consts()
    row, state, cspecs = _scan_specs(consts, lambda c: nch - 1 - c)

    ng = len(parts)

    def body(tri, strict, incl, masks, eye, r_ref, lw_ref, k_ref, v_ref, a_ref, b_ref, s_ref, minv_ref, u_ref,
             dy_ref, dr_in, dk_in, dv_in, *rest):
        src, (dr_ref, dlw_ref, dk_ref, dv_ref, da_ref, db_ref), dst = rest[:ng], rest[ng:ng + 6], rest[ng + 6:2 * ng + 6]
        carry = rest[2 * ng + 6]
        first = jnp.logical_and(pl.program_id(0) == 0, pl.program_id(1) == 0)
        last = jnp.logical_and(pl.program_id(0) == npair // SCAN_PAIRS - 1, pl.program_id(1) == nch - 1)
        if ng:
            sends, arrivals = chip_exchange_copies(src, dst, *rest[2 * ng + 7:])

            @pl.when(first)
            def _():
                for cp in sends:
                    cp.start()

        @pl.when(pl.program_id(1) == 0)
        def _():
            carry[...] = jnp.zeros_like(carry)

        pairs = range(SCAN_PAIRS)
        kept = ([minv_ref[q] for q in pairs], [u_ref[q] for q in pairs])

        def fn(*args):
            y, s1, _ = scan_chunk(tri[...], strict[...], incl[...], masks[0:1, :], masks[1:2, :], eye[...], *args,
                                  saved=kept)
            return y, s1

        rows = [[ref[:, _pair_lanes(q)] for q in pairs] for ref in (r_ref, lw_ref, k_ref, v_ref, a_ref, b_ref)]
        _, vjp = jax.vjp(fn, *rows, [s_ref[q] for q in pairs])
        grads = vjp(([dy_ref[:, _pair_lanes(q)] for q in pairs], [carry[q] for q in pairs]))
        direct = (dr_in, None, dk_in, dv_in, None, None)
        for q in pairs:
            ln = _pair_lanes(q)
            for ref, g, extra in zip((dr_ref, dlw_ref, dk_ref, dv_ref, da_ref, db_ref), grads[:6], direct):
                ref[:, ln] = g[q] if extra is None else g[q] + extra[:, ln]
            carry[q] = grads[6][q]

        if ng:
            @pl.when(last)
            def _():
                for cp in arrivals:
                    cp.wait_recv()
                for cp in sends:
                    cp.wait_send()

    out = pl.pallas_call(
        body, name="rwkv_scan_bwd", grid=(npair // SCAN_PAIRS, nch),
        in_specs=cspecs + [row] * 6 + [state] * 3 + [row] * 4 + [ANY] * ng, out_specs=[row] * 6 + [ANY] * ng,
        out_shape=[jax.ShapeDtypeStruct((lp, D_MODEL), F32)] * 6 + [jax.ShapeDtypeStruct(p.shape, p.dtype) for p in parts],
        scratch_shapes=[pltpu.VMEM((SCAN_PAIRS, PAIR, PAIR), F32)] + (_sem_scratch(ng * len(XY_FLIPS)) if ng else []),
        compiler_params=_params(),
    )(*consts, r, lw, k, v, a, b, *saved, dy, *direct_grads, *parts)
    return out[:6], out[6:]


def _spread_matrices():
    rep = np.zeros((N_HEADS_KV, KV_DIM, KVW), np.float32)
    for h in range(N_HEADS_KV):
        for g in range(GROUP):
            rep[h, h * HEAD_DIM + np.arange(HEAD_DIM), g * HEAD_DIM + np.arange(HEAD_DIM)] = 1.0
    return jnp.asarray(rep, BF16)


KV_HEADS = range(N_HEADS_KV)


def _attn_common(n, q_ref, kp, kc, vp, vc, rep_ref, sink_ref):
    lane = lax.broadcasted_iota(jnp.int32, (1, KVW), 1)
    gmask = [(lane // HEAD_DIM == g).astype(F32) for g in range(GROUP)]
    kk = jnp.concatenate([kp, kc], axis=0)
    vv = jnp.concatenate([vp, vc], axis=0)
    qs = [q_ref[:, h * KVW:(h + 1) * KVW] for h in KV_HEADS]
    q_s = [jnp.concatenate([q * gmask[g] for g in range(GROUP)], axis=0) for q in qs]
    keys = [_dot(kk, rep_ref[h], 1, 0) for h in KV_HEADS]
    vals = [_dot(vv, rep_ref[h], 1, 0) for h in KV_HEADS]
    qi = lax.broadcasted_iota(jnp.int32, (GROUP * BLOCK, 2 * BLOCK), 0) % BLOCK
    kj = lax.broadcasted_iota(jnp.int32, (GROUP * BLOCK, 2 * BLOCK), 1)
    rel = BLOCK + qi - kj
    valid = (rel >= 0) & (rel < BLOCK) & ((n - 1) * BLOCK + kj >= PAD_FRONT)
    s = [jnp.where(valid, _dot(x, y, 1, 1) * (HEAD_DIM ** -0.5), -1e30) for x, y in zip(q_s, keys)]
    sink_col = [jnp.concatenate([jnp.broadcast_to(sink_ref[h, g:g + 1, 0:1], (BLOCK, 1)) for g in range(GROUP)],
                                axis=0) for h in KV_HEADS]
    m = [jnp.maximum(jnp.max(x, axis=-1, keepdims=True), c) for x, c in zip(s, sink_col)]
    ex = [jnp.exp(x - y) for x, y in zip(s, m)]
    ex_sink = [jnp.exp(c - y) for c, y in zip(sink_col, m)]
    inv = [1.0 / (jnp.sum(x, axis=-1, keepdims=True) + c) for x, c in zip(ex, ex_sink)]
    return (gmask, q_s, keys, vals, [x * y for x, y in zip(ex, inv)], [x * y for x, y in zip(ex_sink, inv)])


def _unstack_groups(x_s, gmask):
    out = None
    for g in range(GROUP):
        t = x_s[g * BLOCK:(g + 1) * BLOCK] * gmask[g]
        out = t if out is None else out + t
    return out


def _attn_specs():
    qspec = pl.BlockSpec((BLOCK, D_MODEL), lambda n: (n, 0))
    cur = pl.BlockSpec((BLOCK, KV_DIM), lambda n: (n, 0))
    prev = pl.BlockSpec((BLOCK, KV_DIM), lambda n: (jnp.maximum(n - 1, 0), 0))
    rep = pl.BlockSpec((N_HEADS_KV, KV_DIM, KVW), lambda n: (0, 0, 0))
    sink = pl.BlockSpec((N_HEADS_KV, 8, PAIR), lambda n: (0, 0, 0))
    return qspec, cur, prev, rep, sink


def _attn_params():
    return pltpu.CompilerParams(dimension_semantics=("arbitrary",), vmem_limit_bytes=VMEM_LIMIT)


def attn_fwd(q, k, v, sinks_b):
    lp = q.shape[0]
    qspec, cur, prev, rep, sink = _attn_specs()

    def body(q_ref, kp_ref, kc_ref, vp_ref, vc_ref, rep_ref, sink_ref, o_ref):
        gmask, _, _, vals, p, _ = _attn_common(pl.program_id(0), q_ref, kp_ref[...], kc_ref[...], vp_ref[...],
                                               vc_ref[...], rep_ref, sink_ref)
        o = [_dot(x, y, 1, 0) for x, y in zip(p, vals)]
        for h in KV_HEADS:
            o_ref[:, h * KVW:(h + 1) * KVW] = _unstack_groups(o[h], gmask)

    return pl.pallas_call(
        body, name="swa_fwd", grid=(lp // BLOCK,), in_specs=[qspec, prev, cur, prev, cur, rep, sink],
        out_specs=qspec, out_shape=jax.ShapeDtypeStruct((lp, D_MODEL), F32), compiler_params=_attn_params(),
    )(q, k, k, v, v, _spread_matrices(), sinks_b)


def attn_bwd(q, k, v, sinks_b, do):
    lp = q.shape[0]
    qspec, cur, prev, rep, sink = _attn_specs()

    def body(q_ref, kp_ref, kc_ref, vp_ref, vc_ref, rep_ref, sink_ref, do_ref, dq_ref, dkc_ref, dkp_ref, dvc_ref,
             dvp_ref, dsink_ref):
        n = pl.program_id(0)
        gmask, q_s, keys, vals, p, p_sink = _attn_common(n, q_ref, kp_ref[...], kc_ref[...], vp_ref[...], vc_ref[...],
                                                         rep_ref, sink_ref)
        do_s = [jnp.concatenate([do_ref[:, h * KVW:(h + 1) * KVW] * gmask[g] for g in range(GROUP)], axis=0)
                for h in KV_HEADS]
        dp = [_dot(x, y, 1, 1) for x, y in zip(do_s, vals)]
        delta = [jnp.sum(x * y, axis=-1, keepdims=True) for x, y in zip(p, dp)]
        ds = [x * (y - z) * (HEAD_DIM ** -0.5) for x, y, z in zip(p, dp, delta)]
        dq = [_dot(x, y, 1, 0) for x, y in zip(ds, keys)]
        dkeys_s = [_dot(x, y, 0, 0) for x, y in zip(ds, q_s)]
        dvals_s = [_dot(x, y, 0, 0) for x, y in zip(p, do_s)]
        dkeys = [_exact_dot(x, rep_ref[h], cb=1) for h, x in enumerate(dkeys_s)]
        dvals = [_exact_dot(x, rep_ref[h], cb=1) for h, x in enumerate(dvals_s)]
        dk_all = (dkeys[0] + dkeys[1]) + (dkeys[2] + dkeys[3])
        dv_all = (dvals[0] + dvals[1]) + (dvals[2] + dvals[3])
        dkp_ref[...] = dk_all[:BLOCK]
        dkc_ref[...] = dk_all[BLOCK:]
        dvp_ref[...] = dv_all[:BLOCK]
        dvc_ref[...] = dv_all[BLOCK:]
        dsinks = []
        for h in KV_HEADS:
            dq_ref[:, h * KVW:(h + 1) * KVW] = _unstack_groups(dq[h], gmask)
            dsk = -(p_sink[h] * delta[h])
            rows = [jnp.broadcast_to(jnp.sum(dsk[g * BLOCK:(g + 1) * BLOCK], axis=0, keepdims=True), (1, PAIR))
                    for g in range(GROUP)]
            dsinks.append(jnp.concatenate(rows + [jnp.zeros((8 - GROUP, PAIR), F32)], axis=0))

        @pl.when(n == 0)
        def _():
            for h in KV_HEADS:
                dsink_ref[h] = dsinks[h]

        @pl.when(n > 0)
        def _():
            for h in KV_HEADS:
                dsink_ref[h] += dsinks[h]

    kv = jax.ShapeDtypeStruct((lp, KV_DIM), F32)
    return pl.pallas_call(
        body, name="swa_bwd", grid=(lp // BLOCK,), in_specs=[qspec, prev, cur, prev, cur, rep, sink, qspec],
        out_specs=[qspec, cur, cur, cur, cur, sink],
        out_shape=[jax.ShapeDtypeStruct((lp, D_MODEL), F32), kv, kv, kv, kv,
                   jax.ShapeDtypeStruct((N_HEADS_KV, 8, PAIR), F32)],
        compiler_params=_attn_params(),
    )(q, k, k, v, v, _spread_matrices(), sinks_b, do)


def _pick_tm(lp, want):
    for tm in (384, 192, 128, 64):
        if tm <= want and lp % tm == 0:
            return tm
    raise ValueError(lp)


def _acc(shape):
    return (tuple(shape), F32)


def _ff_one(w):
    return (w, (None, D_MODEL, D_MODEL), lambda c, i: (c, 0, 0))


def _mlp_layer_fwd(name, h, wup, wdown, lg, lb, tm):
    def fn(c, i, h, wup, wdown, lg, lb):
        out = None
        for s in range(N_FF_CHUNK):
            t = mlp_chunk(wup[s], wdown[s], None, h)[0]
            out = t if out is None else out + t
        z = ALPHA * h + out
        return (_layer_norm(z, lg, lb), z), ()

    (h_out, z), _ = rowwise(name, fn, [h], [wup, wdown, lg, lb], [(D_MODEL, F32), (D_MODEL, F32)], [], tm)
    return h_out, z


MLP_BWD_TILE = 528


def _mlp_layer_bwd(name, h_in, z, dh_parts, wup, wdown, lg, lb, tm):
    n_parts = len(dh_parts)

    def fn_ln(c, i, z, *rest):
        dh = rest[0]
        for extra in rest[1:n_parts]:
            dh = dh + extra
        _, vjp = jax.vjp(_layer_norm, z, rest[n_parts], rest[n_parts + 1])
        dz, dlg, dlb = vjp(dh)
        return (dz,), (dlg, dlb)

    (dz,), (dlg, dlb) = rowwise(name + "_ln", fn_ln, [z] + list(dh_parts), [lg, lb], [(D_MODEL, F32)],
                                [_acc((1, D_MODEL)), _acc((1, D_MODEL))], tm)

    def fn_mlp(c, i, h, dz, wup, wdown):
        tile = h.shape[0]
        (dx,), dws = vjp_taps(functools.partial(mlp_chunk, wup, wdown), [(tile, D_MODEL)] * 2, [h], dz)
        return (dx,), dws

    aspec = ((N_FF_CHUNK, D_MODEL, D_MODEL), F32, (None, D_MODEL, D_MODEL), lambda c, i: (c, 0, 0))
    lp = h_in.shape[0]
    tile = MLP_BWD_TILE if lp % MLP_BWD_TILE == 0 else tm
    (dx,), (dwup, dwdown) = rowwise(name + "_mm", fn_mlp, [h_in, dz], [_ff_one(wup), _ff_one(wdown)],
                                    [(D_MODEL, F32, True)], [aspec, aspec], tile, nc=N_FF_CHUNK)
    return dz, dx, dwup, dwdown, dlg, dlb


def _sum_parts(dz, dx):
    out = ALPHA * dz
    for s in range(N_FF_CHUNK):
        out = out + dx[s]
    return out


def local_step(x, loss_target, p, late=None, early_hook=None):
    seq = x.shape[0]
    lp = TOK0 + seq
    tm = _pick_tm(lp, 384)
    tms = _pick_tm(lp, 192)
    e, et = _head_matrices()
    h0 = jnp.concatenate([jnp.zeros((PAD_FRONT, D_MODEL), F32), p["meta_tokens"], x], axis=0)
    pos = jnp.maximum(jnp.arange(lp, dtype=F32) - PAD_FRONT, 0.0)
    inv_freq = 1.0 / (ROPE_THETA ** (jnp.arange(0, HEAD_DIM, 2, dtype=F32) / HEAD_DIM))
    ang = pos[:, None] * inv_freq[None, :]
    cos = jnp.tile(jnp.cos(ang), (1, PAIR // (HEAD_DIM // 2)))
    sin = jnp.tile(jnp.sin(ang), (1, PAIR // (HEAD_DIM // 2)))

    pre_vec = [p["a_mu"][j:j + 1] for j in range(6)] + [p["a_w0"], p["a_a0"], p["a_k_k"], p["a_k_a"]]
    pre_w = [p["a_w_r"], p["a_w_k"], p["a_w_v"], p["a_w1"], p["a_w2"], p["a_a1"], p["a_a2"], p["a_g1"], p["a_g2"]]
    n_vec = len(pre_vec)

    def fn_pre(c, i, h, before, e, et, *ws):
        return rwkv_pre(e, et, ws[n_vec:], None, h, _shift_down(h, before, i), *ws[:n_vec])[0], ()

    (r, lw, k2, v, an, bn, g), _, *pre_gathered = rowwise(
        "rwkv_pre", fn_pre, [h0, _halo_before(h0, tms)], [e, et] + pre_vec + pre_w, [(D_MODEL, F32)] * 7, [], tms,
        hosted=hosted_gather(late[0][0]) if late else None)
    (y, *scan_saved), scan_gathered = scan_fwd(r, lw, k2, v, an, bn, late[1][0] if late else ())
    if late:
        p = {**p, **late[0][1](pre_gathered[0]), **late[1][1](scan_gathered)}

    post_c = [p["a_w_o"], p["a_gn_w"], p["a_gn_b"], p["a_r_k"], p["ln_g00"], p["ln_b00"]]

    def fn_post(c, i, y, r, k2, v, g, h0, e, et, w_o, *vecs):
        return (rwkv_post(e, et, w_o, None, y, r, k2, v, g, h0, *vecs)[0],), ()

    (h1,), _ = rowwise("rwkv_post", fn_post, [y, r, k2, v, g, h0], [e, et] + post_c, [(D_MODEL, F32)], [], tm)
    h2, z2 = _mlp_layer_fwd("mlp0_fwd", h1, p["mlp_up0"], p["mlp_down0"], p["ln_g01"], p["ln_b01"], tm)

    qkv_w = [p["b_w_q"], p["kv_w_k"], p["kv_w_v"]]

    def fn_qkv(c, i, h, cos, sin, wq, wk, wv):
        return qkv_proj(cos, sin, wq, wk, wv, None, h)[0], ()

    (q, k, vv), _ = rowwise("qkv_proj", fn_qkv, [h2, cos, sin], qkv_w,
                            [(D_MODEL, F32), (KV_DIM, F32), (KV_DIM, F32)], [], tm)
    sinks_b = jnp.broadcast_to(p["b_sinks"].reshape(N_HEADS_KV, GROUP, 1), (N_HEADS_KV, GROUP, PAIR))
    sinks_b = jnp.concatenate([sinks_b, jnp.zeros((N_HEADS_KV, 8 - GROUP, PAIR), F32)], axis=1)
    o = attn_fwd(q, k, vv, sinks_b)

    ao_c = [p["b_w_o"], p["ln_g10"], p["ln_b10"]]

    def fn_ao(c, i, o, h, w_o, lg, lb):
        return (attn_out(w_o, None, o, h, lg, lb)[0],), ()

    (h3,), _ = rowwise("attn_out", fn_ao, [o, h2], ao_c, [(D_MODEL, F32)], [], tm)
    h4, z4 = _mlp_layer_fwd("mlp1_fwd", h3, p["mlp_up1"], p["mlp_down1"], p["ln_g11"], p["ln_b11"], tm)

    def fn_loss(c, i, h4, tgt):
        real = (_row_ids(i, TOK0) >= TOK0).astype(F32)
        err = (h4 - tgt) * real
        part = 0.5 * jnp.sum(jnp.sum(err * err, axis=-1, keepdims=True), axis=0, keepdims=True) / D_MODEL
        return (err * (1.0 / D_MODEL),), (jnp.broadcast_to(part, (8, PAIR)),)

    (dh4,), (loss_acc,) = rowwise("loss", fn_loss, [h4, (loss_target, TOK0, lambda i: jnp.maximum(i - 1, 0))], [],
                                  [(D_MODEL, F32)], [_acc((8, PAIR))], TOK0)
    loss = loss_acc[0, 0]

    grads = {}
    dz4, dx4, grads["mlp_up1"], grads["mlp_down1"], grads["ln_g11"], grads["ln_b11"] = _mlp_layer_bwd(
        "mlp1_bwd", h3, z4, [dh4], p["mlp_up1"], p["mlp_down1"], p["ln_g11"], p["ln_b11"], tm)

    def fn_ao_b(c, i, dz, dx, o, h, w_o, lg, lb):
        (do, dh, dlg, dlb), (dw_o,) = vjp_taps(functools.partial(attn_out, w_o), [(tm, D_MODEL)], [o, h, lg, lb],
                                               _sum_parts(dz, dx))
        return (do, dh), (dw_o, dlg, dlb)

    (do, dh2_a), (grads["b_w_o"], grads["ln_g10"], grads["ln_b10"]) = rowwise(
        "attn_out_bwd", fn_ao_b, [dz4, dx4, o, h2], ao_c, [(D_MODEL, F32)] * 2,
        [_acc((D_MODEL, D_MODEL)), _acc((1, D_MODEL)), _acc((1, D_MODEL))], tm)

    dq, dkc, dkp, dvc, dvp, dsinks = attn_bwd(q, k, vv, sinks_b, do)
    grads["b_sinks"] = dsinks[:, :GROUP, 0].reshape(1, N_HEADS)
    zblk = jnp.zeros((BLOCK, KV_DIM), F32)
    dkp_s = jnp.concatenate([dkp[BLOCK:], zblk], axis=0)
    dvp_s = jnp.concatenate([dvp[BLOCK:], zblk], axis=0)

    def fn_qkv_b(c, i, h, cos, sin, dq, dkc, dkp, dvc, dvp, wq, wk, wv):
        return vjp_taps(functools.partial(qkv_proj, cos, sin, wq, wk, wv),
                        [(tm, D_MODEL), (tm, KV_DIM), (tm, KV_DIM)], [h], (dq, dkc + dkp, dvc + dvp))

    (dh2_q,), (grads["b_w_q"], grads["kv_w_k"], grads["kv_w_v"]) = rowwise(
        "qkv_proj_bwd", fn_qkv_b, [h2, cos, sin, dq, dkc, dkp_s, dvc, dvp_s], qkv_w, [(D_MODEL, F32)],
        [_acc((D_MODEL, D_MODEL)), _acc((D_MODEL, KV_DIM)), _acc((D_MODEL, KV_DIM))], tm)

    dz2, dx2, grads["mlp_up0"], grads["mlp_down0"], grads["ln_g01"], grads["ln_b01"] = _mlp_layer_bwd(
        "mlp0_bwd", h1, z2, [dh2_a, dh2_q], p["mlp_up0"], p["mlp_down0"], p["ln_g01"], p["ln_b01"], tm)

    def fn_post_b(c, i, dz, dx, y, r, k2, v, g, h0, e, et, w_o, *vecs):
        out, dws = vjp_taps(functools.partial(rwkv_post, e, et, w_o), [(tms, D_MODEL)],
                            [y, r, k2, v, g, h0] + list(vecs), _sum_parts(dz, dx))
        return out[:6], tuple(dws) + tuple(out[6:])

    early_srcs = early_hook[0](grads) if early_hook else ()
    (dy, dr_c, dk_c, dv_c, dg, dh0_c), post_g, *early_got = rowwise(
        "rwkv_post_bwd", fn_post_b, [dz2, dx2, y, r, k2, v, g, h0], [e, et] + post_c, [(D_MODEL, F32)] * 6,
        [_acc((D_MODEL, D_MODEL))] + [_acc((1, D_MODEL))] * 5, tms,
        hosted=hosted_pair_exchange(early_srcs) if early_hook else None)
    for name, val in zip(["a_w_o", "a_gn_w", "a_gn_b", "a_r_k", "ln_g00", "ln_b00"], post_g):
        grads[name] = val

    (dr, dlw, dk2, dv, dan, dbn), early_from_chips = scan_bwd(
        r, lw, k2, v, an, bn, scan_saved, dy, (dr_c, dk_c, dv_c),
        early_hook[1](early_srcs, early_got[0]) if early_hook else ())

    def fn_pre_b(c, i, h, before, dr, dlw, dk2, dv, dan, dbn, dg, e, et, *ws):
        hp = _shift_down(h, before, i)
        real = (_row_ids(i, tms) >= PAD_FRONT).astype(F32)
        cot = tuple(t * real for t in (dr, dlw, dk2, dv, dan, dbn, dg))
        out, dws = vjp_taps(functools.partial(rwkv_pre, e, et, ws[n_vec:]), [(tms, n) for n in PRE_TAPS],
                            [h, hp] + list(ws[:n_vec]), cot)
        return out[:2], tuple(out[2:]) + tuple(dws)

    (dh0_p, dhp), pre_g = rowwise(
        "rwkv_pre_bwd", fn_pre_b, [h0, _halo_before(h0, tms), dr, dlw, dk2, dv, dan, dbn, dg],
        [e, et] + pre_vec + pre_w, [(D_MODEL, F32)] * 2,
        [_acc((1, D_MODEL))] * n_vec + [_acc(w.shape) for w in pre_w], tms)
    grads["a_mu"] = jnp.concatenate(pre_g[:6], axis=0)
    for name, val in zip(["a_w0", "a_a0", "a_k_k", "a_k_a", "a_w_r", "a_w_k", "a_w_v", "a_w1", "a_w2", "a_a1",
                          "a_a2", "a_g1", "a_g2"], pre_g[6:]):
        grads[name] = val

    def fn_add(c, i, a, b, d, after):
        return (a + b + _shift_up(d, after, i, lp // tm),), ()

    (dh0,), _ = rowwise("grad_h0", fn_add, [dh0_c, dh0_p, dhp, _halo_after(dhp, tm)], [], [(D_MODEL, F32)], [], tm)
    grads["meta_tokens"] = dh0[PAD_FRONT:TOK0]
    return loss, dh0[TOK0:], grads, early_from_chips


ANY = pl.BlockSpec(memory_space=pl.ANY)
XY_FLIPS = ((0, 1), (1, 0), (1, 1))
ALL_FLIPS = tuple((e >> 2 & 1, e >> 1 & 1, e & 1) for e in range(1, N_DEV))


def _flip(v, bit):
    return 1 - v if bit else v


def _sem_scratch(n):
    return [pltpu.SemaphoreType.DMA((n,)), pltpu.SemaphoreType.DMA((n,))]


def gather_copies(src, dst, ici_send, ici_recv, d2d_send, d2d_recv):
    npeer = len(XY_FLIPS)
    x, y, c = lax.axis_index("x"), lax.axis_index("y"), lax.axis_index("c")

    def half(ref, k, which):
        h = src[k].shape[0] // 2
        start = which * h
        return ref.at[pl.ds(pl.multiple_of(start, 8) if h % 8 == 0 else start, h)]

    def ici(k, j, slot):
        fx, fy = XY_FLIPS[j]
        return pltpu.make_async_remote_copy(
            src_ref=half(src[k], k, c), dst_ref=half(dst[k].at[slot], k, c), send_sem=ici_send.at[k * npeer + j],
            recv_sem=ici_recv.at[k * npeer + j], device_id=(_flip(x, fx), _flip(y, fy), c), device_id_type=MESH)

    def d2d(k, j, which):
        fx, fy = XY_FLIPS[j]
        landed = half(dst[k].at[2 * _flip(x, fx) + _flip(y, fy)], k, which)
        return pltpu.make_async_remote_copy(
            src_ref=landed, dst_ref=landed, send_sem=d2d_send.at[k * npeer + j], recv_sem=d2d_recv.at[k * npeer + j],
            device_id=(x, y, 1 - c), device_id_type=MESH)

    pairs = [(k, j) for k in range(len(src)) for j in range(npeer)]
    return ([ici(k, j, 2 * x + y) for k, j in pairs],
            [ici(k, j, 2 * _flip(x, XY_FLIPS[j][0]) + _flip(y, XY_FLIPS[j][1])) for k, j in pairs],
            [d2d(k, j, c) for k, j in pairs], [d2d(k, j, 1 - c) for k, j in pairs])


def gather_scratch(n):
    return _sem_scratch(n * len(XY_FLIPS)) * 2


def gathered_shapes(shards):
    return [jax.ShapeDtypeStruct((N_SHARD,) + s.shape, s.dtype) for s in shards]


def fill_own(gathered, shards):
    if not shards:
        return []
    slot = 2 * lax.axis_index("x") + lax.axis_index("y")
    return [lax.dynamic_update_index_in_dim(g, s, slot, 0) for g, s in zip(gathered, shards)]


def all_gather_shards(shards):
    n = len(shards)

    def body(*refs):
        sends, arrivals, forwards, forwarded = gather_copies(refs[:n], refs[n:2 * n], *refs[2 * n:])
        for cp in sends:
            cp.start()
        for landed, onward in zip(arrivals, forwards):
            landed.wait_recv()
            onward.start()
        for cp in forwarded:
            cp.wait_recv()
        for cp in sends + forwards:
            cp.wait_send()

    out = pl.pallas_call(body, name="gather_weights", in_specs=[ANY] * n, out_specs=[ANY] * n,
                         out_shape=gathered_shapes(shards), scratch_shapes=gather_scratch(n))(*shards)
    return fill_own(out, shards)


def placement():
    x, y, c = lax.axis_index("x"), lax.axis_index("y"), lax.axis_index("c")
    me = 2 * x + y
    others = [j + (j >= me).astype(jnp.int32) for j in range(N_SHARD - 1)]
    return jnp.stack([c, me] + others).astype(jnp.int32)


def hosted_gather(shards):
    return (gather_copies, list(shards), gathered_shapes(shards), gather_scratch(len(shards)),
            lambda got: fill_own(got, shards))


def pair_exchange_copies(src, got, send_sems, recv_sems):
    x, y, c = lax.axis_index("x"), lax.axis_index("y"), lax.axis_index("c")

    def copy(k):
        half = src[k].shape[1] // 2
        theirs = src[k].at[:, pl.ds(pl.multiple_of((1 - c) * half, 8), half), :]
        return pltpu.make_async_remote_copy(
            src_ref=theirs, dst_ref=got[k], send_sem=send_sems.at[k], recv_sem=recv_sems.at[k],
            device_id=(x, y, 1 - c), device_id_type=MESH)

    sends = [copy(k) for k in range(len(src))]
    return sends, sends, [], []


def _half_shapes(sources):
    return [jax.ShapeDtypeStruct((s.shape[0], s.shape[1] // 2, s.shape[2]), s.dtype) for s in sources]


def hosted_pair_exchange(sources):
    return (pair_exchange_copies, list(sources), _half_shapes(sources), _sem_scratch(len(sources)), list)


def pair_exchange(name, sources):
    n = len(sources)

    def body(*refs):
        sends, arrivals, _, _ = pair_exchange_copies(refs[:n], refs[n:2 * n], *refs[2 * n:])
        for cp in sends:
            cp.start()
        for cp in arrivals:
            cp.wait_recv()
        for cp in sends:
            cp.wait_send()

    halves = _half_shapes(sources)
    return pl.pallas_call(body, name=name, in_specs=[ANY] * n, out_specs=[ANY] * n,
                          out_shape=halves, scratch_shapes=_sem_scratch(n))(*sources)


def chip_exchange(parts):
    n = len(parts)

    def body(*refs):
        sends, arrivals = chip_exchange_copies(refs[:n], refs[n:2 * n], *refs[2 * n:])
        for cp in sends:
            cp.start()
        for cp in arrivals:
            cp.wait_recv()
        for cp in sends:
            cp.wait_send()

    return pl.pallas_call(
        body, name="grads_chip_exchange", in_specs=[ANY] * n, out_specs=[ANY] * n,
        out_shape=[jax.ShapeDtypeStruct(p.shape, p.dtype) for p in parts],
        scratch_shapes=_sem_scratch(n * len(XY_FLIPS)),
    )(*parts)


def chip_exchange_copies(src, dst, send_sems, recv_sems):
    npeer = len(XY_FLIPS)
    x, y, c = lax.axis_index("x"), lax.axis_index("y"), lax.axis_index("c")
    me = 2 * x + y

    def copy(k, j, sending):
        fx, fy = XY_FLIPS[j]
        px, py = _flip(x, fx), _flip(y, fy)
        peer = 2 * px + py
        return pltpu.make_async_remote_copy(
            src_ref=src[k].at[peer], dst_ref=dst[k].at[me if sending else peer],
            send_sem=send_sems.at[k * npeer + j], recv_sem=recv_sems.at[k * npeer + j],
            device_id=(px, py, c), device_id_type=MESH)

    pairs = [(k, j) for k in range(len(src)) for j in range(npeer)]
    return [copy(k, j, True) for k, j in pairs], [copy(k, j, False) for k, j in pairs]


def sibling_share(halves):
    n = len(halves)

    def body(*refs):
        src, got = refs[:n], refs[n:2 * n]
        send_sems, recv_sems = refs[2 * n:]
        x, y, c = lax.axis_index("x"), lax.axis_index("y"), lax.axis_index("c")
        sends = [pltpu.make_async_remote_copy(
            src_ref=src[k], dst_ref=got[k], send_sem=send_sems.at[k], recv_sem=recv_sems.at[k],
            device_id=(x, y, 1 - c), device_id_type=MESH) for k in range(n)]
        for cp in sends:
            cp.start()
        for cp in sends:
            cp.wait_recv()
        for cp in sends:
            cp.wait_send()

    return pl.pallas_call(
        body, name="grads_sibling_share", in_specs=[ANY] * n, out_specs=[ANY] * n,
        out_shape=[jax.ShapeDtypeStruct(h.shape, h.dtype) for h in halves], scratch_shapes=_sem_scratch(n),
    )(*halves)


ADD_TILE_ELEMS = 512 * 1024


def _row_tile(rows, cols):
    return max(t for t in range(8, rows + 1, 8) if rows % t == 0 and t * cols <= ADD_TILE_ELEMS)


def _prefetch_call(body, name, place, grid, in_specs, out_specs, out_shape, args):
    return pl.pallas_call(
        body, name=name, out_shape=out_shape,
        grid_spec=pltpu.PrefetchScalarGridSpec(num_scalar_prefetch=1, grid=grid, in_specs=in_specs,
                                               out_specs=out_specs),
        compiler_params=pltpu.CompilerParams(dimension_semantics=("arbitrary",) * len(grid),
                                             vmem_limit_bytes=VMEM_LIMIT),
    )(place, *args)


def pair_add(name, place, src, got, dtype):
    n4, half, cols = got.shape
    tile = _row_tile(half, cols)
    nt = half // tile

    def body(pr, a_ref, b_ref, o_ref):
        o_ref[...] = (a_ref[...] + b_ref[...]).astype(o_ref.dtype)

    mine = pl.BlockSpec((None, tile, cols), lambda s, i, pr: (s, pr[0] * nt + i, 0))
    blk = pl.BlockSpec((None, tile, cols), lambda s, i, pr: (s, i, 0))
    return _prefetch_call(body, name, place, (n4, nt), [mine, blk], blk,
                          jax.ShapeDtypeStruct(got.shape, dtype), (src, got))


def chip_add(name, place, part, from_chips):
    _, half, cols = part.shape
    tile = _row_tile(half, cols)

    def body(pr, own_ref, r0_ref, r1_ref, r2_ref, o_ref):
        me = pr[1]
        own, r0, r1, r2 = (r[...].astype(F32) for r in (own_ref, r0_ref, r1_ref, r2_ref))
        t0 = jnp.where(me == 0, own, r0)
        t1 = jnp.where(me == 0, r0, jnp.where(me == 1, own, r1))
        t2 = jnp.where(me <= 1, r1, jnp.where(me == 2, own, r2))
        t3 = jnp.where(me == 3, own, r2)
        o_ref[...] = ((t0 + t1) + t2) + t3

    def slab(j):
        return pl.BlockSpec((None, tile, cols), lambda i, pr: (pr[j], i, 0))

    return _prefetch_call(body, name, place, (half // tile,), [slab(1), slab(2), slab(3), slab(4)],
                          pl.BlockSpec((tile, cols), lambda i, pr: (i, 0)),
                          jax.ShapeDtypeStruct((half, cols), F32), (part, from_chips, from_chips, from_chips))


def pair_adds(tag, place, sources, got, narrow):
    return [pair_add(f"grads_pair_add_{tag}{k}", place, s, g, BF16 if nar else F32)
            for k, (s, g, nar) in enumerate(zip(sources, got, narrow))]


def finish_sums(place, parts, from_chips):
    halves = [chip_add(f"grads_chip_add{k}", place, p, f) for k, (p, f) in enumerate(zip(parts, from_chips))]
    return list(zip(halves, sibling_share(halves)))


ADAM_ROWS = 256


def adamw_update(name, place, halves, w, m, v):
    nsub, rows, cols = w.shape
    half = rows // 2
    tr = ADAM_ROWS if half % ADAM_ROWS == 0 else half
    nth = half // tr

    def body(pr, *refs):
        g_refs, (w_ref, m_ref, v_ref, g_ref, d_ref, nm_ref, nv_ref) = refs[:2 * nsub], refs[2 * nsub:]
        l = pl.program_id(0)
        mine = (pl.program_id(1) // nth) == pr[0]
        g = None
        for s in range(nsub):
            gs = jnp.where(mine, g_refs[2 * s][...], g_refs[2 * s + 1][...])
            g = gs if g is None else jnp.where(l == s, gs, g)
        m2 = ADAM_B1 * m_ref[...] + (1.0 - ADAM_B1) * g
        v2 = ADAM_B2 * v_ref[...] + (1.0 - ADAM_B2) * (g * g)
        m_hat = m2 / (1.0 - ADAM_B1 ** ADAM_STEP)
        v_hat = v2 / (1.0 - ADAM_B2 ** ADAM_STEP)
        g_ref[...] = g
        d_ref[...] = -ADAM_LR * (m_hat / (jnp.sqrt(v_hat) + ADAM_EPS) + ADAM_WD * w_ref[...])
        nm_ref[...] = m2
        nv_ref[...] = v2

    gblk = pl.BlockSpec((tr, cols), lambda l, i, pr: (i % nth, 0))
    blk = pl.BlockSpec((None, tr, cols), lambda l, i, pr: (l, i, 0))
    out = jax.ShapeDtypeStruct((nsub, rows, cols), F32)
    return _prefetch_call(body, name, place, (nsub, rows // tr), [gblk] * (2 * nsub) + [blk] * 3, [blk] * 4,
                          [out] * 4, [h for pair in halves for h in pair] + [w, m, v])


WEIGHT_NAMES = ("meta_tokens", "a_mu", "a_w_r", "a_w_k", "a_w_v", "a_w_o", "a_w0", "a_w1", "a_w2", "a_a0", "a_a1",
                "a_a2", "a_g1", "a_g2", "a_k_k", "a_k_a", "a_r_k", "a_gn_w", "a_gn_b", "kv_w_k", "kv_w_v", "b_w_q",
                "b_sinks", "b_w_o", "mlp_w_up", "mlp_w_down", "ln_g", "ln_b")
BIG_NAMES = ("a_w_r", "a_w_k", "a_w_v", "a_w_o", "b_w_q", "b_w_o")
EARLY_NAMES, LATE_NAMES = BIG_NAMES[:3], BIG_NAMES[3:]
PACK_MATS = (("kv_w_k", 256), ("kv_w_v", 256), ("a_w1", 64), ("a_a1", 64), ("a_g1", 128), ("a_w2", 64),
             ("a_a2", 64), ("a_g2", 128))
COLUMN_CUT = ("a_w2", "a_a2", "a_g2")
PACK_VECS = (("a_mu", 6), ("a_w0", 1), ("a_a0", 1), ("a_k_k", 1), ("a_k_a", 1), ("a_gn_w", 1), ("a_gn_b", 1),
             ("ln_g", 4), ("ln_b", 4), ("meta_tokens", 16))
PACK_REPL = (("a_r_k", 4), ("b_sinks", 1))
SHARD_W = D_MODEL // N_SHARD


def _tiles(rows):
    return -(-rows // SUBLANES) * SUBLANES


N_MAT_ROWS = sum(_tiles(r) for _, r in PACK_MATS)
N_VEC_ROWS = sum(_tiles(r) for _, r in PACK_VECS)
N_PACK_ROWS = -(-(N_MAT_ROWS + N_VEC_ROWS + sum(_tiles(r) for _, r in PACK_REPL)) // 16) * 16
N_GATHER_VEC_ROWS = -(-N_VEC_ROWS // 16) * 16


def _pad_rows(arr, axis):
    rows = arr.shape[axis]
    pad = [(0, 0)] * arr.ndim
    pad[axis] = (0, _tiles(rows) - rows)
    return jnp.pad(arr, pad) if _tiles(rows) != rows else arr


def _pack_rows(arr):
    if arr.size == N_HEADS:
        arr = jnp.pad(arr.reshape(1, N_HEADS), ((0, 0), (0, SHARD_W - N_HEADS)))
    return _pad_rows(arr.reshape(-1, SHARD_W), 0)


def pack_small(get):
    parts = [_pack_rows(get(name)) for name, _ in PACK_MATS + PACK_VECS + PACK_REPL]
    used = sum(p.shape[0] for p in parts)
    return jnp.concatenate(parts + [jnp.zeros((N_PACK_ROWS - used, SHARD_W), F32)], axis=0)


def unpack_small(pack, shapes):
    out, off = {}, 0
    for name, rows in PACK_MATS + PACK_VECS + PACK_REPL:
        piece = pack[off:off + rows]
        off += _tiles(rows)
        out[name] = piece[:, :N_HEADS].reshape(shapes[name]) if name == "b_sinks" else piece.reshape(shapes[name])
    return out


def whole_weights(big_names, gathered_big, mats, vecs, a_r_k, b_sinks):
    p = {name: g.reshape(D_MODEL, D_MODEL) for name, g in zip(big_names, gathered_big)}
    off = 0
    for name, rows in PACK_MATS:
        piece = mats[:, off:off + rows]
        off += rows
        if name in COLUMN_CUT:
            p[name] = piece.transpose(1, 0, 2).reshape(rows, D_MODEL)
        else:
            p[name] = piece.reshape(D_MODEL, rows)
    v = vecs.transpose(1, 0, 2).reshape(-1, D_MODEL)
    off = 0
    for name, rows in PACK_VECS:
        p[name] = v[off:off + rows]
        off += _tiles(rows)
    for i in range(2):
        for j in range(2):
            p[f"ln_g{i}{j}"] = p["ln_g"][2 * i + j:2 * i + j + 1]
            p[f"ln_b{i}{j}"] = p["ln_b"][2 * i + j:2 * i + j + 1]
    p["a_r_k"] = a_r_k.reshape(1, D_MODEL)
    p["b_sinks"] = b_sinks
    return p


def small_grad_pack(g):
    parts = []
    for name, rows in PACK_MATS:
        if name in COLUMN_CUT:
            parts.append(g[name].reshape(rows, N_SHARD, SHARD_W).transpose(1, 0, 2))
        else:
            parts.append(g[name].reshape(N_SHARD, rows, SHARD_W))
    vecs = {n: g[n] for n in ("a_mu", "a_w0", "a_a0", "a_k_k", "a_k_a", "a_gn_w", "a_gn_b", "meta_tokens")}
    vecs["ln_g"] = jnp.concatenate([g[f"ln_g{i}{j}"] for i in range(2) for j in range(2)], axis=0)
    vecs["ln_b"] = jnp.concatenate([g[f"ln_b{i}{j}"] for i in range(2) for j in range(2)], axis=0)
    for name, rows in PACK_VECS:
        parts.append(_pad_rows(vecs[name].reshape(rows, N_SHARD, SHARD_W).transpose(1, 0, 2), 1))
    r_k = jnp.broadcast_to(g["a_r_k"].reshape(1, -1, SHARD_W), (N_SHARD, D_MODEL // SHARD_W, SHARD_W))
    sinks = jnp.pad(g["b_sinks"].reshape(1, 1, N_HEADS), ((0, 0), (0, 0), (0, SHARD_W - N_HEADS)))
    parts += [_pad_rows(r_k, 1), _pad_rows(jnp.broadcast_to(sinks, (N_SHARD, 1, SHARD_W)), 1)]
    used = sum(p.shape[1] for p in parts)
    parts.append(jnp.zeros((N_SHARD, N_PACK_ROWS - used, SHARD_W), F32))
    return jnp.concatenate(parts, axis=1)


def train_step(vals):
    w = {n: vals[n] for n in WEIGHT_NAMES}
    w_pack = pack_small(lambda n: w[n])
    early = [w[n][0].astype(BF16) for n in EARLY_NAMES]
    early += [w_pack[:N_MAT_ROWS].astype(BF16), w_pack[N_MAT_ROWS:N_MAT_ROWS + N_GATHER_VEC_ROWS]]
    gathered = all_gather_shards(early)
    ne = len(EARLY_NAMES)
    p = whole_weights(EARLY_NAMES, gathered[:ne], gathered[ne], gathered[ne + 1][:, :N_VEC_ROWS], w["a_r_k"],
                      w["b_sinks"])
    nb = len(BIG_NAMES)

    def late_set(big, layer):
        shards = [w[n][0].astype(BF16) for n in big]
        shards += [w["mlp_w_up"][layer].astype(BF16), w["mlp_w_down"][layer].astype(BF16)]

        def weights(got):
            out = {n: x.reshape(D_MODEL, D_MODEL) for n, x in zip(big, got)}
            out[f"mlp_up{layer}"], out[f"mlp_down{layer}"] = got[len(big):]
            return out

        return shards, weights

    late = (late_set((), 1), late_set(LATE_NAMES, 0))

    place = placement()
    ready = {}
    a_names, b_names = BIG_NAMES[:4], BIG_NAMES[4:]

    def early_sources(g):
        return ([g[n].reshape(N_SHARD, SHARD_W, D_MODEL) for n in b_names]
                + [g["mlp_up0"], g["mlp_up1"], g["mlp_down0"], g["mlp_down1"]])

    def early_parts(srcs, got):
        ready["parts"] = pair_adds("early", place, srcs, got, [True] * len(srcs))
        return ready["parts"]

    loss, gx, g, early_from_chips = local_step(vals["x"][0], vals["loss_target"][0], p, late,
                                               (early_sources, early_parts))
    loss = lax.psum(loss, ("x", "y", "c"))
    srcs = [g[n].reshape(N_SHARD, SHARD_W, D_MODEL) for n in a_names] + [small_grad_pack(g)]
    rest = pair_adds("late", place, srcs, pair_exchange("grads_pair_exchange", srcs), [True] * len(a_names) + [False])
    rest_from_chips = chip_exchange(rest)
    na = len(a_names)
    halves = finish_sums(place, rest[:na] + ready["parts"] + rest[na:],
                         list(rest_from_chips[:na]) + list(early_from_chips) + list(rest_from_chips[na:]))

    res = {}
    for k, n in enumerate(BIG_NAMES):
        res[n] = adamw_update("adamw_" + n, place, halves[k:k + 1], w[n], vals["m_" + n], vals["v_" + n])
    for k, n in ((nb, "mlp_w_up"), (nb + 2, "mlp_w_down")):
        res[n] = adamw_update("adamw_" + n, place, halves[k:k + 2], w[n], vals["m_" + n], vals["v_" + n])
    packs = adamw_update("adamw_small", place, halves[-1:], w_pack[None], pack_small(lambda n: vals["m_" + n])[None],
                         pack_small(lambda n: vals["v_" + n])[None])
    shapes = {n: w[n].shape for n in WEIGHT_NAMES}
    small = [unpack_small(pk[0], shapes) for pk in packs]
    outs = [loss, gx[None]]
    for t in range(4):
        outs += [res[n][t] if n in res else small[t][n] for n in WEIGHT_NAMES]
    return tuple(outs)


def kernel(x, meta_tokens, a_mu, a_w_r, a_w_k, a_w_v, a_w_o, a_w0, a_w1, a_w2, a_a0, a_a1, a_a2, a_g1, a_g2, a_k_k,
           a_k_a, a_r_k, a_gn_w, a_gn_b, kv_w_k, kv_w_v, b_w_q, b_sinks, b_w_o, mlp_w_up, mlp_w_down, ln_g, ln_b,
           loss_target, m_meta_tokens, m_a_mu, m_a_w_r, m_a_w_k, m_a_w_v, m_a_w_o, m_a_w0, m_a_w1, m_a_w2, m_a_a0,
           m_a_a1, m_a_a2, m_a_g1, m_a_g2, m_a_k_k, m_a_k_a, m_a_r_k, m_a_gn_w, m_a_gn_b, m_kv_w_k, m_kv_w_v,
           m_b_w_q, m_b_sinks, m_b_w_o, m_mlp_w_up, m_mlp_w_down, m_ln_g, m_ln_b, v_meta_tokens, v_a_mu, v_a_w_r,
           v_a_w_k, v_a_w_v, v_a_w_o, v_a_w0, v_a_w1, v_a_w2, v_a_a0, v_a_a1, v_a_a2, v_a_g1, v_a_g2, v_a_k_k,
           v_a_k_a, v_a_r_k, v_a_gn_w, v_a_gn_b, v_kv_w_k, v_kv_w_v, v_b_w_q, v_b_sinks, v_b_w_o, v_mlp_w_up,
           v_mlp_w_down, v_ln_g, v_ln_b):
    return train_step(dict(locals()))
```

```python
import functools

import numpy as np
import jax
import jax.numpy as jnp
from jax import lax
from jax.experimental import pallas as pl
from jax.experimental.pallas import tpu as pltpu

F32 = jnp.float32
BF16 = jnp.bfloat16

D_MODEL = 1024
N_HEADS = 16
HEAD_DIM = 64
N_HEADS_KV = 4
GROUP = 4
KV_DIM = N_HEADS_KV * HEAD_DIM
N_META = 16
BLOCK = 128
PAD_FRONT = BLOCK - N_META
TOK0 = PAD_FRONT + N_META
N_FF_CHUNK = 4
N_SHARD = 4
GN_EPS = 64e-5
LN_EPS = 1e-5
ROPE_THETA = 10000.0
ALPHA = 4.0 ** 0.25
ADAM_LR, ADAM_B1, ADAM_B2, ADAM_EPS, ADAM_WD, ADAM_STEP = 0.001, 0.9, 0.999, 1e-08, 0.01, 10
SCAN_T = 64
PAIR = 128
KVW = GROUP * HEAD_DIM
VMEM_LIMIT = 60 * 1024 * 1024
MESH = pl.DeviceIdType.MESH


def _dot(a, b, ca, cb):
    return lax.dot_general(a.astype(BF16), b.astype(BF16), (((ca,), (cb,)), ((), ())),
                           preferred_element_type=F32)


@jax.custom_vjp
def mm(a, b):
    return _dot(a, b, 1, 0)


def _mm_fwd(a, b):
    return mm(a, b), b


def _mm_bwd(b, g):
    return _dot(g, b, 1, 1), jnp.zeros_like(b)


mm.defvjp(_mm_fwd, _mm_bwd)


@jax.custom_vjp
def mm_tap(a, b, tap):
    return _dot(a, b, 1, 0)


mm_tap.defvjp(lambda a, b, tap: (_dot(a, b, 1, 0), b), lambda b, g: (_dot(g, b, 1, 1), jnp.zeros_like(b), g))


def tmm(x, w, taps, xs):
    y = mm(x, w) if taps is None else mm_tap(x, w, taps[len(xs)])
    xs.append(x)
    return y


def vjp_taps(core, tap_shapes, args, cot):
    taps = [jnp.zeros(s, F32) for s in tap_shapes]
    _, vjp, xs = jax.vjp(core, taps, *args, has_aux=True)
    out = vjp(cot)
    return out[1:], [_dot(x, g, 0, 0) for x, g in zip(xs, out[0])]


def _split3(x):
    x1 = x.astype(BF16)
    r1 = x - x1.astype(F32)
    x2 = r1.astype(BF16)
    x3 = (r1 - x2.astype(F32)).astype(BF16)
    return x1, x2, x3


def _exact_dot(x, m01, cb=0):
    acc = None
    for piece in _split3(x)[:2]:
        t = lax.dot_general(piece, m01, (((1,), (cb,)), ((), ())), preferred_element_type=F32)
        acc = t if acc is None else acc + t
    return acc


def _head_matrices():
    e = np.zeros((D_MODEL, N_HEADS), np.float32)
    e[np.arange(D_MODEL), np.arange(D_MODEL) // HEAD_DIM] = 1.0
    return jnp.asarray(e, BF16), jnp.asarray(e.T, BF16)


@jax.custom_vjp
def hsum(x, e, et):
    return _exact_dot(x, e)


@jax.custom_vjp
def hbc(s, e, et):
    return _exact_dot(s, et)


hsum.defvjp(lambda x, e, et: (_exact_dot(x, e), (e, et)),
            lambda res, g: (hbc(g, *res), jnp.zeros_like(res[0]), jnp.zeros_like(res[1])))
hbc.defvjp(lambda s, e, et: (_exact_dot(s, et), (e, et)),
           lambda res, g: (hsum(g, *res), jnp.zeros_like(res[0]), jnp.zeros_like(res[1])))


def _sigmoid(u):
    return 0.5 * (jnp.tanh(0.5 * u) + 1.0)


def _softplus(u):
    return jnp.maximum(u, 0.0) + jnp.log(1.0 + jnp.exp(-jnp.abs(u)))


def _layer_norm(z, g, b):
    mu = jnp.mean(z, axis=-1, keepdims=True)
    zc = z - mu
    var = jnp.mean(zc * zc, axis=-1, keepdims=True)
    return zc * lax.rsqrt(var + LN_EPS) * g + b


def _zero_map(nd):
    return lambda c, i: (0,) * nd


def _params():
    return pltpu.CompilerParams(dimension_semantics=("arbitrary", "arbitrary"), vmem_limit_bytes=VMEM_LIMIT)


def rowwise(name, fn, rows, consts, out_rows, out_accs, tm, nc=1, hosted=None):
    lp = rows[0].shape[-2]
    nt = lp // tm
    assert nt * tm == lp, (name, lp, tm)
    copies_fn, hosted_src, hosted_shapes, hosted_scratch, hosted_post = hosted or (None, (), [], [], None)
    ng = len(hosted_src)
    in_specs, args = [], []
    for a in rows:
        if isinstance(a, tuple):
            a, block_rows, block_index = a
            in_specs.append(pl.BlockSpec((block_rows, a.shape[1]),
                                         functools.partial(lambda f, c, i: (f(i), 0), block_index)))
        elif a.ndim == 2:
            in_specs.append(pl.BlockSpec((tm, a.shape[1]), lambda c, i: (i, 0)))
        else:
            in_specs.append(pl.BlockSpec((a.shape[0], tm, a.shape[2]), lambda c, i: (0, i, 0)))
        args.append(a)
    for cst in consts:
        if isinstance(cst, tuple):
            arr, bs, im = cst
            in_specs.append(pl.BlockSpec(bs, im))
        else:
            arr = cst
            in_specs.append(pl.BlockSpec(arr.shape, _zero_map(arr.ndim), pipeline_mode=pl.Buffered(1)))
        args.append(arr)
    out_shape, out_specs, acc_per_chunk = [], [], []
    for spec in out_rows:
        if len(spec) == 3 and spec[2]:
            out_shape.append(jax.ShapeDtypeStruct((nc, lp, spec[0]), spec[1]))
            out_specs.append(pl.BlockSpec((None, tm, spec[0]), lambda c, i: (c, i, 0)))
        else:
            out_shape.append(jax.ShapeDtypeStruct((lp, spec[0]), spec[1]))
            out_specs.append(pl.BlockSpec((tm, spec[0]), lambda c, i: (i, 0)))
    for spec in out_accs:
        out_shape.append(jax.ShapeDtypeStruct(spec[0], spec[1]))
        if len(spec) == 4:
            out_specs.append(pl.BlockSpec(spec[2], spec[3]))
            acc_per_chunk.append(True)
        else:
            out_specs.append(pl.BlockSpec(spec[0], _zero_map(len(spec[0])), pipeline_mode=pl.Buffered(1)))
            acc_per_chunk.append(False)
    n_in, n_or, n_out = len(args), len(out_rows), len(out_shape)

    def body(*refs):
        c = pl.program_id(0)
        i = pl.program_id(1)
        if ng:
            src, dst = refs[n_in:n_in + ng], refs[n_in + ng + n_out:n_in + 2 * ng + n_out]
            sends, arrivals, forwards, forwarded = copies_fn(src, dst, *refs[n_in + 2 * ng + n_out:])

            @pl.when(jnp.logical_and(c == 0, i == 0))
            def _():
                for cp in sends:
                    cp.start()

        vals = [r[...] for r in refs[:n_in]]
        outs_r, outs_a = fn(c, i, *vals)
        out_refs = refs[n_in + ng:n_in + ng + n_out]
        for ref, val in zip(out_refs[:n_or], outs_r):
            ref[...] = val.astype(ref.dtype)
        for ref, val, per_chunk in zip(out_refs[n_or:], outs_a, acc_per_chunk):
            first = (i == 0) if per_chunk else jnp.logical_and(i == 0, c == 0)

            @pl.when(first)
            def _():
                ref[...] = val.astype(ref.dtype)

            @pl.when(jnp.logical_not(first))
            def _():
                ref[...] += val.astype(ref.dtype)

        if ng:
            @pl.when(jnp.logical_and(c == nc - 1, i == max(nt - 3, 0)))
            def _():
                for k, landed in enumerate(arrivals):
                    landed.wait_recv()
                    if forwards:
                        forwards[k].start()

            @pl.when(jnp.logical_and(c == nc - 1, i == nt - 1))
            def _():
                for cp in forwarded:
                    cp.wait_recv()
                for cp in sends + forwards:
                    cp.wait_send()

    outs = pl.pallas_call(body, name=name, grid=(nc, nt), in_specs=in_specs + [ANY] * ng,
                          out_specs=out_specs + [ANY] * ng, out_shape=out_shape + list(hosted_shapes),
                          scratch_shapes=list(hosted_scratch), compiler_params=_params())(*args, *hosted_src)
    if ng:
        return outs[:n_or], outs[n_or:n_out], hosted_post(outs[n_out:])
    return outs[:n_or], outs[n_or:]


def _row_ids(i, tm):
    return i * tm + lax.broadcasted_iota(jnp.int32, (tm, 1), 0)


SUBLANES = 8


def _halo_before(arr, tm):
    return (arr, SUBLANES, lambda i: jnp.maximum(i * (tm // SUBLANES) - 1, 0))


def _halo_after(arr, tm):
    last = arr.shape[0] // SUBLANES - 1
    return (arr, SUBLANES, lambda i: jnp.minimum((i + 1) * (tm // SUBLANES), last))


def _pick_row(block8, row):
    rows = lax.broadcasted_iota(jnp.int32, block8.shape, 0)
    return jnp.sum(jnp.where(rows == row, block8, 0.0), axis=0, keepdims=True)


def _shift_down(x, before8, i):
    rows = lax.broadcasted_iota(jnp.int32, x.shape, 0)
    top = _pick_row(before8, SUBLANES - 1) * (i > 0).astype(F32)
    return jnp.where(rows == 0, top, pltpu.roll(x, 1, 0))


def _shift_up(x, after8, i, nt):
    rows = lax.broadcasted_iota(jnp.int32, x.shape, 0)
    bottom = _pick_row(after8, 0) * (i < nt - 1).astype(F32)
    return jnp.where(rows == x.shape[0] - 1, bottom, pltpu.roll(x, x.shape[0] - 1, 0))


PRE_TAPS = (D_MODEL, D_MODEL, D_MODEL, 64, D_MODEL, 64, D_MODEL, 128, D_MODEL)


def rwkv_pre(e, et, ws, taps, h, hp, mu_r, mu_w, mu_k, mu_v, mu_a, mu_g, w0, a0, k_k, k_a):
    w_r, w_k, w_v, w1, w2, a1, a2, g1, g2 = ws
    xs = []
    xx = hp - h
    r = tmm(h + xx * mu_r, w_r, taps, xs)
    k = tmm(h + xx * mu_k, w_k, taps, xs)
    v = tmm(h + xx * mu_v, w_v, taps, xs)
    wraw = -_softplus(-(w0 + tmm(jnp.tanh(tmm(h + xx * mu_w, w1, taps, xs)), w2, taps, xs))) - 0.5
    lw = -jnp.exp(wraw)
    a = _sigmoid(a0 + tmm(tmm(h + xx * mu_a, a1, taps, xs), a2, taps, xs))
    g = tmm(_sigmoid(tmm(h + xx * mu_g, g1, taps, xs)), g2, taps, xs)
    kk = k * k_k
    ss = hsum(kk * kk, e, et)
    pos = ss > 0.0
    nrm = jnp.where(pos, jnp.sqrt(jnp.where(pos, ss, 1.0)), 0.0)
    kk = kk * hbc(1.0 / jnp.maximum(nrm, 1e-12), e, et)
    k2 = k * (1.0 + (a - 1.0) * k_a)
    return (r, lw, k2, v, -kk, kk * a, g), xs


def rwkv_post(e, et, w_o, taps, y, r, k2, v, g, h0, gn_w, gn_b, rk, lg, lb):
    xs = []
    inv_n = 1.0 / HEAD_DIM
    yc = y - hbc(hsum(y, e, et) * inv_n, e, et)
    yv = hsum(yc * yc, e, et) * inv_n
    yn = yc * hbc(lax.rsqrt(yv + GN_EPS), e, et) * gn_w + gn_b
    bonus = hbc(hsum(r * k2 * rk, e, et), e, et) * v
    mix = tmm((yn + bonus) * g, w_o, taps, xs)
    return _layer_norm(ALPHA * h0 + mix, lg, lb), xs


@jax.custom_vjp
def sq_relu(x):
    r = jnp.maximum(x, 0.0)
    return r * r


sq_relu.defvjp(lambda x: (sq_relu(x), x), lambda x, g: (g * (2.0 * jnp.maximum(x, 0.0)),))


def mlp_chunk(wup, wdown, taps, h):
    xs = []
    return tmm(sq_relu(tmm(h, wup, taps, xs)), wdown, taps, xs), xs


def _rot_half(t):
    n = t.shape[-1]
    lane = lax.broadcasted_iota(jnp.int32, t.shape, t.ndim - 1)
    lo = (lane % HEAD_DIM) < (HEAD_DIM // 2)
    return jnp.where(lo, -pltpu.roll(t, n - HEAD_DIM // 2, t.ndim - 1), pltpu.roll(t, HEAD_DIM // 2, t.ndim - 1))


@jax.custom_vjp
def rot_half(t):
    return _rot_half(t)


rot_half.defvjp(lambda t: (_rot_half(t), None), lambda _, g: (-_rot_half(g),))


def _tile_lanes(t, width):
    return jnp.concatenate([t] * (width // t.shape[-1]), axis=-1)


def qkv_proj(cos, sin, wq, wk, wv, taps, h):
    xs = []
    q = tmm(h, wq, taps, xs)
    k = tmm(h, wk, taps, xs)
    v = tmm(h, wv, taps, xs)
    cq, sq = _tile_lanes(cos, D_MODEL), _tile_lanes(sin, D_MODEL)
    ck, sk = _tile_lanes(cos, KV_DIM), _tile_lanes(sin, KV_DIM)
    return (q * cq + rot_half(q) * sq, k * ck + rot_half(k) * sk, v), xs


def attn_out(w_o, taps, o, h, lg, lb):
    xs = []
    return _layer_norm(ALPHA * h + tmm(o, w_o, taps, xs), lg, lb), xs


def _scan_consts():
    t = SCAN_T
    tri = np.tril(np.ones((t, t), np.float32))
    rows = np.arange(2 * t)
    same = (rows[:, None] // t) == (rows[None, :] // t)
    strict = same & ((rows[None, :] % t) < (rows[:, None] % t))
    incl = same & ((rows[None, :] % t) <= (rows[:, None] % t))
    lane = np.arange(PAIR)
    masks = np.zeros((8, PAIR), np.float32)
    masks[0] = (lane // HEAD_DIM) == 0
    masks[1] = (lane // HEAD_DIM) == 1
    return (jnp.asarray(tri, BF16), jnp.asarray(strict.astype(np.float32)), jnp.asarray(incl.astype(np.float32)),
            jnp.asarray(masks), jnp.asarray(np.eye(2 * t, dtype=np.float32)))


def _scan_dot(a, b, ca, cb):
    return _dot(a, b, ca, cb)


@functools.partial(jax.custom_vjp, nondiff_argnums=(2, 3))
def _dotf(a, b, ca, cb):
    return _scan_dot(a, b, ca, cb)


def _dotf_bwd(ca, cb, res, g):
    a, b = res
    if ca == 1:
        da = _scan_dot(g, b, 1, 1 - cb)
    else:
        da = _scan_dot(b, g, 1 - cb, 1)
    if cb == 0:
        db = _scan_dot(a, g, 1 - ca, 0)
    else:
        db = _scan_dot(g, a, 0, 1 - ca)
    return da, db


_dotf.defvjp(lambda a, b, ca, cb: (_scan_dot(a, b, ca, cb), (a, b)), _dotf_bwd)


def _tri_dot(tri, x, ct):
    acc = None
    for piece in _split3(x):
        t = lax.dot_general(tri, piece, (((ct,), (0,)), ((), ())), preferred_element_type=F32)
        acc = t if acc is None else acc + t
    return acc


@jax.custom_vjp
def _cumsum_rows(tri, x):
    return _tri_dot(tri, x, 1)


_cumsum_rows.defvjp(lambda tri, x: (_tri_dot(tri, x, 1), tri),
                    lambda tri, g: (jnp.zeros_like(tri), _tri_dot(tri, g, 0)))


@jax.custom_vjp
def _unstack2(x):
    t = x.shape[0] // 2
    return x[:t] + x[t:]


_unstack2.defvjp(lambda x: (_unstack2(x), None), lambda _, g: (jnp.concatenate([g, g], axis=0),))


@jax.custom_vjp
def _last_row(x):
    return x[x.shape[0] - 1:, :]


def _last_row_bwd(_, g):
    rows = lax.broadcasted_iota(jnp.int32, (SCAN_T, g.shape[1]), 0)
    return (jnp.where(rows == SCAN_T - 1, jnp.broadcast_to(g, (SCAN_T, g.shape[1])), 0.0),)


_last_row.defvjp(lambda x: (_last_row(x), None), _last_row_bwd)


@jax.custom_vjp
def _halves(x):
    n = x.shape[0] // 2
    return x[:n], x[n:]


_halves.defvjp(lambda x: (_halves(x), None), lambda _, g: (jnp.concatenate(list(g), axis=0),))


@jax.custom_vjp
def _quads(x):
    n, m = x.shape[0] // 2, x.shape[1] // 2
    return x[:n, :m], x[:n, m:], x[n:, :m], x[n:, m:]


_quads.defvjp(lambda x: (_quads(x), None),
              lambda _, g: (jnp.concatenate([jnp.concatenate([g[0], g[1]], axis=1),
                                             jnp.concatenate([g[2], g[3]], axis=1)], axis=0),))


@jax.custom_vjp
def _solve_saved(n, rhs, minv, u):
    return u


def _solve_saved_bwd(res, du):
    minv, u = res
    drhs = _dotf(minv, du, 0, 0)
    return _dotf(drhs, u, 1, 1), drhs, jnp.zeros_like(minv), jnp.zeros_like(u)


_solve_saved.defvjp(lambda n, rhs, minv, u: (u, (minv, u)), _solve_saved_bwd)


def scan_chunk(tri, strict, incl, m0, m1, eye, r, lw, k, v, a, b, s0, saved=None):
    lower = strict > 0
    lower_incl = incl > 0

    def stack(x):
        return jnp.concatenate([x * m0, x * m1], axis=0)

    def dots(xs, ys, ca, cb, mask=None):
        out = [_dotf(x, y, ca, cb) for x, y in zip(xs, ys)]
        return out if mask is None else [jnp.where(mask, o, 0.0) for o in out]

    cl = [_cumsum_rows(tri, x) for x in lw]
    gam = [jnp.exp(c) for c in cl]
    ginv = [jnp.exp(-c) for c in cl]
    ar_s = [jnp.concatenate([stack(x * jnp.exp(c - w)), stack(y * g)], axis=0)
            for x, c, w, y, g in zip(a, cl, lw, r, gam)]
    bk_s = [jnp.concatenate([stack(x * g), stack(y * g)], axis=0) for x, y, g in zip(b, k, ginv)]
    v_s = [stack(x) for x in v]
    quads = [_quads(x) for x in dots(ar_s, bk_s, 1, 1)]
    n_ab = [jnp.where(lower, q[0], 0.0) for q in quads]
    n_ak = [jnp.where(lower, q[1], 0.0) for q in quads]
    r_ab = [jnp.where(lower_incl, q[2], 0.0) for q in quads]
    r_ak = [jnp.where(lower_incl, q[3], 0.0) for q in quads]
    from_state = [_halves(x) for x in dots(ar_s, s0, 1, 1)]
    rhs = [x[0] + y for x, y in zip(from_state, dots(n_ak, v_s, 1, 0))]
    if saved is None:
        minv = [eye + n for n in n_ab]
        p = n_ab
        for _ in range(5):
            p = dots(p, p, 1, 0)
            minv = [m + mp for m, mp in zip(minv, dots(minv, p, 1, 0))]
        u_s = dots(minv, rhs, 1, 0)
    else:
        minv = saved[0]
        u_s = [_solve_saved(n, x, m, u) for n, x, m, u in zip(n_ab, rhs, *saved)]
    uv_s = [jnp.concatenate([x, y], axis=0) for x, y in zip(u_s, v_s)]
    r_uv = [jnp.concatenate([x, y], axis=1) for x, y in zip(r_ab, r_ak)]
    y = [_unstack2(x[1] + z) for x, z in zip(from_state, dots(r_uv, uv_s, 1, 0))]
    g_end = [_last_row(g) for g in gam]
    s1 = [s * g + x for s, g, x in zip(s0, g_end, dots(uv_s, [x * g for x, g in zip(bk_s, g_end)], 0, 0))]
    return y, s1, (minv, u_s)


SCAN_PAIRS = 8


def _scan_specs(consts, order):
    row = pl.BlockSpec((SCAN_T, PAIR * SCAN_PAIRS), lambda p, c: (order(c), p))
    state = pl.BlockSpec((None, SCAN_PAIRS, PAIR, PAIR), lambda p, c: (order(c), p, 0, 0))
    return row, state, [pl.BlockSpec(x.shape, _zero_map(x.ndim)) for x in consts]


def _pair_lanes(q):
    return slice(q * PAIR, (q + 1) * PAIR)


def scan_fwd(r, lw, k, v, a, b, shards=()):
    lp = r.shape[0]
    nch = lp // SCAN_T
    npair = D_MODEL // PAIR
    ng = len(shards)
    consts = _scan_consts()
    row, state, cspecs = _scan_specs(consts, lambda c: c)

    def body(tri, strict, incl, masks, eye, r_ref, lw_ref, k_ref, v_ref, a_ref, b_ref, *rest):
        src, (y_ref, s_ref, minv_ref, u_ref), dst = rest[:ng], rest[ng:ng + 4], rest[ng + 4:2 * ng + 4]
        carry = rest[2 * ng + 4]
        first = jnp.logical_and(pl.program_id(0) == 0, pl.program_id(1) == 0)
        last = jnp.logical_and(pl.program_id(0) == npair // SCAN_PAIRS - 1, pl.program_id(1) == nch - 1)
        if ng:
            sends, arrivals, forwards, forwarded = gather_copies(src, dst, *rest[2 * ng + 5:])

            @pl.when(first)
            def _():
                for cp in sends:
                    cp.start()

            @pl.when(jnp.logical_and(pl.program_id(0) == npair // SCAN_PAIRS - 1, pl.program_id(1) == nch * 3 // 4))
            def _():
                for landed, onward in zip(arrivals, forwards):
                    landed.wait_recv()
                    onward.start()

        @pl.when(pl.program_id(1) == 0)
        def _():
            carry[...] = jnp.zeros_like(carry)

        pairs = range(SCAN_PAIRS)
        s0 = [carry[q] for q in pairs]
        rows = [[ref[:, _pair_lanes(q)] for q in pairs] for ref in (r_ref, lw_ref, k_ref, v_ref, a_ref, b_ref)]
        y, s1, (minv, u) = scan_chunk(tri[...], strict[...], incl[...], masks[0:1, :], masks[1:2, :], eye[...],
                                      *rows, s0)
        for q in pairs:
            s_ref[q] = s0[q]
            minv_ref[q] = minv[q]
            u_ref[q] = u[q]
            y_ref[:, _pair_lanes(q)] = y[q]
            carry[q] = s1[q]

        if ng:
            @pl.when(last)
            def _():
                for cp in forwarded:
                    cp.wait_recv()
                for cp in sends + forwards:
                    cp.wait_send()

    mats = jax.ShapeDtypeStruct((nch, npair, PAIR, PAIR), F32)
    out = pl.pallas_call(
        body, name="rwkv_scan_fwd", grid=(npair // SCAN_PAIRS, nch), in_specs=cspecs + [row] * 6 + [ANY] * ng,
        out_specs=[row, state, state, state] + [ANY] * ng,
        out_shape=[jax.ShapeDtypeStruct((lp, D_MODEL), F32), mats, mats, mats] + gathered_shapes(shards),
        scratch_shapes=[pltpu.VMEM((SCAN_PAIRS, PAIR, PAIR), F32)] + (gather_scratch(ng) if ng else []),
        compiler_params=_params(),
    )(*consts, r, lw, k, v, a, b, *shards)
    return out[:4], fill_own(out[4:], shards)


def scan_bwd(r, lw, k, v, a, b, saved, dy, direct_grads, parts=()):
    lp = r.shape[0]
    nch = lp // SCAN_T
    npair = D_MODEL // PAIR
    consts = _scan_consts()
    row, state, cspecs = _scan_specs(consts, lambda c: nch - 1 - c)

    ng = len(parts)

    def body(tri, strict, incl, masks, eye, r_ref, lw_ref, k_ref, v_ref, a_ref, b_ref, s_ref, minv_ref, u_ref,
             dy_ref, dr_in, dk_in, dv_in, *rest):
        src, (dr_ref, dlw_ref, dk_ref, dv_ref, da_ref, db_ref), dst = rest[:ng], rest[ng:ng + 6], rest[ng + 6:2 * ng + 6]
        carry = rest[2 * ng + 6]
        first = jnp.logical_and(pl.program_id(0) == 0, pl.program_id(1) == 0)
        last = jnp.logical_and(pl.program_id(0) == npair // SCAN_PAIRS - 1, pl.program_id(1) == nch - 1)
        if ng:
            sends, arrivals = chip_exchange_copies(src, dst, *rest[2 * ng + 7:])

            @pl.when(first)
            def _():
                for cp in sends:
                    cp.start()

        @pl.when(pl.program_id(1) == 0)
        def _():
            carry[...] = jnp.zeros_like(carry)

        pairs = range(SCAN_PAIRS)
        kept = ([minv_ref[q] for q in pairs], [u_ref[q] for q in pairs])

        def fn(*args):
            y, s1, _ = scan_chunk(tri[...], strict[...], incl[...], masks[0:1, :], masks[1:2, :], eye[...], *args,
                                  saved=kept)
            return y, s1

        rows = [[ref[:, _pair_lanes(q)] for q in pairs] for ref in (r_ref, lw_ref, k_ref, v_ref, a_ref, b_ref)]
        _, vjp = jax.vjp(fn, *rows, [s_ref[q] for q in pairs])
        grads = vjp(([dy_ref[:, _pair_lanes(q)] for q in pairs], [carry[q] for q in pairs]))
        direct = (dr_in, None, dk_in, dv_in, None, None)
        for q in pairs:
            ln = _pair_lanes(q)
            for ref, g, extra in zip((dr_ref, dlw_ref, dk_ref, dv_ref, da_ref, db_ref), grads[:6], direct):
                ref[:, ln] = g[q] if extra is None else g[q] + extra[:, ln]
            carry[q] = grads[6][q]

        if ng:
            @pl.when(last)
            def _():
                for cp in arrivals:
                    cp.wait_recv()
                for cp in sends:
                    cp.wait_send()

    out = pl.pallas_call(
        body, name="rwkv_scan_bwd", grid=(npair // SCAN_PAIRS, nch),
        in_specs=cspecs + [row] * 6 + [state] * 3 + [row] * 4 + [ANY] * ng, out_specs=[row] * 6 + [ANY] * ng,
        out_shape=[jax.ShapeDtypeStruct((lp, D_MODEL), F32)] * 6 + [jax.ShapeDtypeStruct(p.shape, p.dtype) for p in parts],
        scratch_shapes=[pltpu.VMEM((SCAN_PAIRS, PAIR, PAIR), F32)] + (_sem_scratch(ng * len(XY_FLIPS)) if ng else []),
        compiler_params=_params(),
    )(*consts, r, lw, k, v, a, b, *saved, dy, *direct_grads, *parts)
    return out[:6], out[6:]


def _spread_matrices():
    rep = np.zeros((N_HEADS_KV, KV_DIM, KVW), np.float32)
    for h in range(N_HEADS_KV):
        for g in range(GROUP):
            rep[h, h * HEAD_DIM + np.arange(HEAD_DIM), g * HEAD_DIM + np.arange(HEAD_DIM)] = 1.0
    return jnp.asarray(rep, BF16)


KV_HEADS = range(N_HEADS_KV)


def _attn_common(n, q_ref, kp, kc, vp, vc, rep_ref, sink_ref):
    lane = lax.broadcasted_iota(jnp.int32, (1, KVW), 1)
    gmask = [(lane // HEAD_DIM == g).astype(F32) for g in range(GROUP)]
    kk = jnp.concatenate([kp, kc], axis=0)
    vv = jnp.concatenate([vp, vc], axis=0)
    qs = [q_ref[:, h * KVW:(h + 1) * KVW] for h in KV_HEADS]
    q_s = [jnp.concatenate([q * gmask[g] for g in range(GROUP)], axis=0) for q in qs]
    keys = [_dot(kk, rep_ref[h], 1, 0) for h in KV_HEADS]
    vals = [_dot(vv, rep_ref[h], 1, 0) for h in KV_HEADS]
    qi = lax.broadcasted_iota(jnp.int32, (GROUP * BLOCK, 2 * BLOCK), 0) % BLOCK
    kj = lax.broadcasted_iota(jnp.int32, (GROUP * BLOCK, 2 * BLOCK), 1)
    rel = BLOCK + qi - kj
    valid = (rel >= 0) & (rel < BLOCK) & ((n - 1) * BLOCK + kj >= PAD_FRONT)
    s = [jnp.where(valid, _dot(x, y, 1, 1) * (HEAD_DIM ** -0.5), -1e30) for x, y in zip(q_s, keys)]
    sink_col = [jnp.concatenate([jnp.broadcast_to(sink_ref[h, g:g + 1, 0:1], (BLOCK, 1)) for g in range(GROUP)],
                                axis=0) for h in KV_HEADS]
    m = [jnp.maximum(jnp.max(x, axis=-1, keepdims=True), c) for x, c in zip(s, sink_col)]
    ex = [jnp.exp(x - y) for x, y in zip(s, m)]
    ex_sink = [jnp.exp(c - y) for c, y in zip(sink_col, m)]
    inv = [1.0 / (jnp.sum(x, axis=-1, keepdims=True) + c) for x, c in zip(ex, ex_sink)]
    return (gmask, q_s, keys, vals, [x * y for x, y in zip(ex, inv)], [x * y for x, y in zip(ex_sink, inv)])


def _unstack_groups(x_s, gmask):
    out = None
    for g in range(GROUP):
        t = x_s[g * BLOCK:(g + 1) * BLOCK] * gmask[g]
        out = t if out is None else out + t
    return out


def _attn_specs():
    qspec = pl.BlockSpec((BLOCK, D_MODEL), lambda n: (n, 0))
    cur = pl.BlockSpec((BLOCK, KV_DIM), lambda n: (n, 0))
    prev = pl.BlockSpec((BLOCK, KV_DIM), lambda n: (jnp.maximum(n - 1, 0), 0))
    rep = pl.BlockSpec((N_HEADS_KV, KV_DIM, KVW), lambda n: (0, 0, 0))
    sink = pl.BlockSpec((N_HEADS_KV, 8, PAIR), lambda n: (0, 0, 0))
    return qspec, cur, prev, rep, sink


def _attn_params():
    return pltpu.CompilerParams(dimension_semantics=("arbitrary",), vmem_limit_bytes=VMEM_LIMIT)


def attn_fwd(q, k, v, sinks_b):
    lp = q.shape[0]
    qspec, cur, prev, rep, sink = _attn_specs()

    def body(q_ref, kp_ref, kc_ref, vp_ref, vc_ref, rep_ref, sink_ref, o_ref):
        gmask, _, _, vals, p, _ = _attn_common(pl.program_id(0), q_ref, kp_ref[...], kc_ref[...], vp_ref[...],
                                               vc_ref[...], rep_ref, sink_ref)
        o = [_dot(x, y, 1, 0) for x, y in zip(p, vals)]
        for h in KV_HEADS:
            o_ref[:, h * KVW:(h + 1) * KVW] = _unstack_groups(o[h], gmask)

    return pl.pallas_call(
        body, name="swa_fwd", grid=(lp // BLOCK,), in_specs=[qspec, prev, cur, prev, cur, rep, sink],
        out_specs=qspec, out_shape=jax.ShapeDtypeStruct((lp, D_MODEL), F32), compiler_params=_attn_params(),
    )(q, k, k, v, v, _spread_matrices(), sinks_b)


def attn_bwd(q, k, v, sinks_b, do):
    lp = q.shape[0]
    qspec, cur, prev, rep, sink = _attn_specs()

    def body(q_ref, kp_ref, kc_ref, vp_ref, vc_ref, rep_ref, sink_ref, do_ref, dq_ref, dkc_ref, dkp_ref, dvc_ref,
             dvp_ref, dsink_ref):
        n = pl.program_id(0)
        gmask, q_s, keys, vals, p, p_sink = _attn_common(n, q_ref, kp_ref[...], kc_ref[...], vp_ref[...], vc_ref[...],
                                                         rep_ref, sink_ref)
        do_s = [jnp.concatenate([do_ref[:, h * KVW:(h + 1) * KVW] * gmask[g] for g in range(GROUP)], axis=0)
                for h in KV_HEADS]
        dp = [_dot(x, y, 1, 1) for x, y in zip(do_s, vals)]
        delta = [jnp.sum(x * y, axis=-1, keepdims=True) for x, y in zip(p, dp)]
        ds = [x * (y - z) * (HEAD_DIM ** -0.5) for x, y, z in zip(p, dp, delta)]
        dq = [_dot(x, y, 1, 0) for x, y in zip(ds, keys)]
        dkeys_s = [_dot(x, y, 0, 0) for x, y in zip(ds, q_s)]
        dvals_s = [_dot(x, y, 0, 0) for x, y in zip(p, do_s)]
        dkeys = [_exact_dot(x, rep_ref[h], cb=1) for h, x in enumerate(dkeys_s)]
        dvals = [_exact_dot(x, rep_ref[h], cb=1) for h, x in enumerate(dvals_s)]
        dk_all = (dkeys[0] + dkeys[1]) + (dkeys[2] + dkeys[3])
        dv_all = (dvals[0] + dvals[1]) + (dvals[2] + dvals[3])
        dkp_ref[...] = dk_all[:BLOCK]
        dkc_ref[...] = dk_all[BLOCK:]
        dvp_ref[...] = dv_all[:BLOCK]
        dvc_ref[...] = dv_all[BLOCK:]
        dsinks = []
        for h in KV_HEADS:
            dq_ref[:, h * KVW:(h + 1) * KVW] = _unstack_groups(dq[h], gmask)
            dsk = -(p_sink[h] * delta[h])
            rows = [jnp.broadcast_to(jnp.sum(dsk[g * BLOCK:(g + 1) * BLOCK], axis=0, keepdims=True), (1, PAIR))
                    for g in range(GROUP)]
            dsinks.append(jnp.concatenate(rows + [jnp.zeros((8 - GROUP, PAIR), F32)], axis=0))

        @pl.when(n == 0)
        def _():
            for h in KV_HEADS:
                dsink_ref[h] = dsinks[h]

        @pl.when(n > 0)
        def _():
            for h in KV_HEADS:
                dsink_ref[h] += dsinks[h]

    kv = jax.ShapeDtypeStruct((lp, KV_DIM), F32)
    return pl.pallas_call(
        body, name="swa_bwd", grid=(lp // BLOCK,), in_specs=[qspec, prev, cur, prev, cur, rep, sink, qspec],
        out_specs=[qspec, cur, cur, cur, cur, sink],
        out_shape=[jax.ShapeDtypeStruct((lp, D_MODEL), F32), kv, kv, kv, kv,
                   jax.ShapeDtypeStruct((N_HEADS_KV, 8, PAIR), F32)],
        compiler_params=_attn_params(),
    )(q, k, k, v, v, _spread_matrices(), sinks_b, do)


def _pick_tm(lp, want):
    for tm in (384, 192, 128, 64):
        if tm <= want and lp % tm == 0:
            return tm
    raise ValueError(lp)


def _acc(shape):
    return (tuple(shape), F32)


def _ff_one(w):
    return (w, (None, D_MODEL, D_MODEL), lambda c, i: (c, 0, 0))


def _mlp_layer_fwd(name, h, wup, wdown, lg, lb, tm):
    def fn(c, i, h, wup, wdown, lg, lb):
        out = None
        for s in range(N_FF_CHUNK):
            t = mlp_chunk(wup[s], wdown[s], None, h)[0]
            out = t if out is None else out + t
        z = ALPHA * h + out
        return (_layer_norm(z, lg, lb), z), ()

    (h_out, z), _ = rowwise(name, fn, [h], [wup, wdown, lg, lb], [(D_MODEL, F32), (D_MODEL, F32)], [], tm)
    return h_out, z


MLP_BWD_TILE = 528


def _mlp_layer_bwd(name, h_in, z, dh_parts, wup, wdown, lg, lb, tm):
    n_parts = len(dh_parts)

    def fn_ln(c, i, z, *rest):
        dh = rest[0]
        for extra in rest[1:n_parts]:
            dh = dh + extra
        _, vjp = jax.vjp(_layer_norm, z, rest[n_parts], rest[n_parts + 1])
        dz, dlg, dlb = vjp(dh)
        return (dz,), (dlg, dlb)

    (dz,), (dlg, dlb) = rowwise(name + "_ln", fn_ln, [z] + list(dh_parts), [lg, lb], [(D_MODEL, F32)],
                                [_acc((1, D_MODEL)), _acc((1, D_MODEL))], tm)

    def fn_mlp(c, i, h, dz, wup, wdown):
        tile = h.shape[0]
        (dx,), dws = vjp_taps(functools.partial(mlp_chunk, wup, wdown), [(tile, D_MODEL)] * 2, [h], dz)
        return (dx,), dws

    aspec = ((N_FF_CHUNK, D_MODEL, D_MODEL), F32, (None, D_MODEL, D_MODEL), lambda c, i: (c, 0, 0))
    lp = h_in.shape[0]
    tile = MLP_BWD_TILE if lp % MLP_BWD_TILE == 0 else tm
    (dx,), (dwup, dwdown) = rowwise(name + "_mm", fn_mlp, [h_in, dz], [_ff_one(wup), _ff_one(wdown)],
                                    [(D_MODEL, F32, True)], [aspec, aspec], tile, nc=N_FF_CHUNK)
    return dz, dx, dwup, dwdown, dlg, dlb


def _sum_parts(dz, dx):
    out = ALPHA * dz
    for s in range(N_FF_CHUNK):
        out = out + dx[s]
    return out


def local_step(x, loss_target, p, late=None, early_hook=None):
    seq = x.shape[0]
    lp = TOK0 + seq
    tm = _pick_tm(lp, 384)
    tms = _pick_tm(lp, 192)
    e, et = _head_matrices()
    h0 = jnp.concatenate([jnp.zeros((PAD_FRONT, D_MODEL), F32), p["meta_tokens"], x], axis=0)
    pos = jnp.maximum(jnp.arange(lp, dtype=F32) - PAD_FRONT, 0.0)
    inv_freq = 1.0 / (ROPE_THETA ** (jnp.arange(0, HEAD_DIM, 2, dtype=F32) / HEAD_DIM))
    ang = pos[:, None] * inv_freq[None, :]
    cos = jnp.tile(jnp.cos(ang), (1, PAIR // (HEAD_DIM // 2)))
    sin = jnp.tile(jnp.sin(ang), (1, PAIR // (HEAD_DIM // 2)))

    pre_vec = [p["a_mu"][j:j + 1] for j in range(6)] + [p["a_w0"], p["a_a0"], p["a_k_k"], p["a_k_a"]]
    pre_w = [p["a_w_r"], p["a_w_k"], p["a_w_v"], p["a_w1"], p["a_w2"], p["a_a1"], p["a_a2"], p["a_g1"], p["a_g2"]]
    n_vec = len(pre_vec)

    def fn_pre(c, i, h, before, e, et, *ws):
        return rwkv_pre(e, et, ws[n_vec:], None, h, _shift_down(h, before, i), *ws[:n_vec])[0], ()

    (r, lw, k2, v, an, bn, g), _, *pre_gathered = rowwise(
        "rwkv_pre", fn_pre, [h0, _halo_before(h0, tms)], [e, et] + pre_vec + pre_w, [(D_MODEL, F32)] * 7, [], tms,
        hosted=hosted_gather(late[0][0]) if late else None)
    (y, *scan_saved), scan_gathered = scan_fwd(r, lw, k2, v, an, bn, late[1][0] if late else ())
    if late:
        p = {**p, **late[0][1](pre_gathered[0]), **late[1][1](scan_gathered)}

    post_c = [p["a_w_o"], p["a_gn_w"], p["a_gn_b"], p["a_r_k"], p["ln_g00"], p["ln_b00"]]

    def fn_post(c, i, y, r, k2, v, g, h0, e, et, w_o, *vecs):
        return (rwkv_post(e, et, w_o, None, y, r, k2, v, g, h0, *vecs)[0],), ()

    (h1,), _ = rowwise("rwkv_post", fn_post, [y, r, k2, v, g, h0], [e, et] + post_c, [(D_MODEL, F32)], [], tm)
    h2, z2 = _mlp_layer_fwd("mlp0_fwd", h1, p["mlp_up0"], p["mlp_down0"], p["ln_g01"], p["ln_b01"], tm)

    qkv_w = [p["b_w_q"], p["kv_w_k"], p["kv_w_v"]]

    def fn_qkv(c, i, h, cos, sin, wq, wk, wv):
        return qkv_proj(cos, sin, wq, wk, wv, None, h)[0], ()

    (q, k, vv), _ = rowwise("qkv_proj", fn_qkv, [h2, cos, sin], qkv_w,
                            [(D_MODEL, F32), (KV_DIM, F32), (KV_DIM, F32)], [], tm)
    sinks_b = jnp.broadcast_to(p["b_sinks"].reshape(N_HEADS_KV, GROUP, 1), (N_HEADS_KV, GROUP, PAIR))
    sinks_b = jnp.concatenate([sinks_b, jnp.zeros((N_HEADS_KV, 8 - GROUP, PAIR), F32)], axis=1)
    o = attn_fwd(q, k, vv, sinks_b)

    ao_c = [p["b_w_o"], p["ln_g10"], p["ln_b10"]]

    def fn_ao(c, i, o, h, w_o, lg, lb):
        return (attn_out(w_o, None, o, h, lg, lb)[0],), ()

    (h3,), _ = rowwise("attn_out", fn_ao, [o, h2], ao_c, [(D_MODEL, F32)], [], tm)
    h4, z4 = _mlp_layer_fwd("mlp1_fwd", h3, p["mlp_up1"], p["mlp_down1"], p["ln_g11"], p["ln_b11"], tm)

    def fn_loss(c, i, h4, tgt):
        real = (_row_ids(i, TOK0) >= TOK0).astype(F32)
        err = (h4 - tgt) * real
        part = 0.5 * jnp.sum(jnp.sum(err * err, axis=-1, keepdims=True), axis=0, keepdims=True) / D_MODEL
        return (err * (1.0 / D_MODEL),), (jnp.broadcast_to(part, (8, PAIR)),)

    (dh4,), (loss_acc,) = rowwise("loss", fn_loss, [h4, (loss_target, TOK0, lambda i: jnp.maximum(i - 1, 0))], [],
                                  [(D_MODEL, F32)], [_acc((8, PAIR))], TOK0)
    loss = loss_acc[0, 0]

    grads = {}
    dz4, dx4, grads["mlp_up1"], grads["mlp_down1"], grads["ln_g11"], grads["ln_b11"] = _mlp_layer_bwd(
        "mlp1_bwd", h3, z4, [dh4], p["mlp_up1"], p["mlp_down1"], p["ln_g11"], p["ln_b11"], tm)

    def fn_ao_b(c, i, dz, dx, o, h, w_o, lg, lb):
        (do, dh, dlg, dlb), (dw_o,) = vjp_taps(functools.partial(attn_out, w_o), [(tm, D_MODEL)], [o, h, lg, lb],
                                               _sum_parts(dz, dx))
        return (do, dh), (dw_o, dlg, dlb)

    (do, dh2_a), (grads["b_w_o"], grads["ln_g10"], grads["ln_b10"]) = rowwise(
        "attn_out_bwd", fn_ao_b, [dz4, dx4, o, h2], ao_c, [(D_MODEL, F32)] * 2,
        [_acc((D_MODEL, D_MODEL)), _acc((1, D_MODEL)), _acc((1, D_MODEL))], tm)

    dq, dkc, dkp, dvc, dvp, dsinks = attn_bwd(q, k, vv, sinks_b, do)
    grads["b_sinks"] = dsinks[:, :GROUP, 0].reshape(1, N_HEADS)
    zblk = jnp.zeros((BLOCK, KV_DIM), F32)
    dkp_s = jnp.concatenate([dkp[BLOCK:], zblk], axis=0)
    dvp_s = jnp.concatenate([dvp[BLOCK:], zblk], axis=0)

    def fn_qkv_b(c, i, h, cos, sin, dq, dkc, dkp, dvc, dvp, wq, wk, wv):
        return vjp_taps(functools.partial(qkv_proj, cos, sin, wq, wk, wv),
                        [(tm, D_MODEL), (tm, KV_DIM), (tm, KV_DIM)], [h], (dq, dkc + dkp, dvc + dvp))

    (dh2_q,), (grads["b_w_q"], grads["kv_w_k"], grads["kv_w_v"]) = rowwise(
        "qkv_proj_bwd", fn_qkv_b, [h2, cos, sin, dq, dkc, dkp_s, dvc, dvp_s], qkv_w, [(D_MODEL, F32)],
        [_acc((D_MODEL, D_MODEL)), _acc((D_MODEL, KV_DIM)), _acc((D_MODEL, KV_DIM))], tm)

    dz2, dx2, grads["mlp_up0"], grads["mlp_down0"], grads["ln_g01"], grads["ln_b01"] = _mlp_layer_bwd(
        "mlp0_bwd", h1, z2, [dh2_a, dh2_q], p["mlp_up0"], p["mlp_down0"], p["ln_g01"], p["ln_b01"], tm)

    def fn_post_b(c, i, dz, dx, y, r, k2, v, g, h0, e, et, w_o, *vecs):
        out, dws = vjp_taps(functools.partial(rwkv_post, e, et, w_o), [(tms, D_MODEL)],
                            [y, r, k2, v, g, h0] + list(vecs), _sum_parts(dz, dx))
        return out[:6], tuple(dws) + tuple(out[6:])

    early_srcs = early_hook[0](grads) if early_hook else ()
    (dy, dr_c, dk_c, dv_c, dg, dh0_c), post_g, *early_got = rowwise(
        "rwkv_post_bwd", fn_post_b, [dz2, dx2, y, r, k2, v, g, h0], [e, et] + post_c, [(D_MODEL, F32)] * 6,
        [_acc((D_MODEL, D_MODEL))] + [_acc((1, D_MODEL))] * 5, tms,
        hosted=hosted_pair_exchange(early_srcs) if early_hook else None)
    for name, val in zip(["a_w_o", "a_gn_w", "a_gn_b", "a_r_k", "ln_g00", "ln_b00"], post_g):
        grads[name] = val

    (dr, dlw, dk2, dv, dan, dbn), early_from_chips = scan_bwd(
        r, lw, k2, v, an, bn, scan_saved, dy, (dr_c, dk_c, dv_c),
        early_hook[1](early_srcs, early_got[0]) if early_hook else ())

    def fn_pre_b(c, i, h, before, dr, dlw, dk2, dv, dan, dbn, dg, e, et, *ws):
        hp = _shift_down(h, before, i)
        real = (_row_ids(i, tms) >= PAD_FRONT).astype(F32)
        cot = tuple(t * real for t in (dr, dlw, dk2, dv, dan, dbn, dg))
        out, dws = vjp_taps(functools.partial(rwkv_pre, e, et, ws[n_vec:]), [(tms, n) for n in PRE_TAPS],
                            [h, hp] + list(ws[:n_vec]), cot)
        return out[:2], tuple(out[2:]) + tuple(dws)

    (dh0_p, dhp), pre_g = rowwise(
        "rwkv_pre_bwd", fn_pre_b, [h0, _halo_before(h0, tms), dr, dlw, dk2, dv, dan, dbn, dg],
        [e, et] + pre_vec + pre_w, [(D_MODEL, F32)] * 2,
        [_acc((1, D_MODEL))] * n_vec + [_acc(w.shape) for w in pre_w], tms)
    grads["a_mu"] = jnp.concatenate(pre_g[:6], axis=0)
    for name, val in zip(["a_w0", "a_a0", "a_k_k", "a_k_a", "a_w_r", "a_w_k", "a_w_v", "a_w1", "a_w2", "a_a1",
                          "a_a2", "a_g1", "a_g2"], pre_g[6:]):
        grads[name] = val

    def fn_add(c, i, a, b, d, after):
        return (a + b + _shift_up(d, after, i, lp // tm),), ()

    (dh0,), _ = rowwise("grad_h0", fn_add, [dh0_c, dh0_p, dhp, _halo_after(dhp, tm)], [], [(D_MODEL, F32)], [], tm)
    grads["meta_tokens"] = dh0[PAD_FRONT:TOK0]
    return loss, dh0[TOK0:], grads, early_from_chips


ANY = pl.BlockSpec(memory_space=pl.ANY)
XY_FLIPS = ((0, 1), (1, 0), (1, 1))


def _flip(v, bit):
    return 1 - v if bit else v


def _sem_scratch(n):
    return [pltpu.SemaphoreType.DMA((n,)), pltpu.SemaphoreType.DMA((n,))]


def gather_copies(src, dst, ici_send, ici_recv, d2d_send, d2d_recv):
    npeer = len(XY_FLIPS)
    x, y, c = lax.axis_index("x"), lax.axis_index("y"), lax.axis_index("c")

    def half(ref, k, which):
        h = src[k].shape[0] // 2
        start = which * h
        return ref.at[pl.ds(pl.multiple_of(start, 8) if h % 8 == 0 else start, h)]

    def ici(k, j, slot):
        fx, fy = XY_FLIPS[j]
        return pltpu.make_async_remote_copy(
            src_ref=half(src[k], k, c), dst_ref=half(dst[k].at[slot], k, c), send_sem=ici_send.at[k * npeer + j],
            recv_sem=ici_recv.at[k * npeer + j], device_id=(_flip(x, fx), _flip(y, fy), c), device_id_type=MESH)

    def d2d(k, j, which):
        fx, fy = XY_FLIPS[j]
        landed = half(dst[k].at[2 * _flip(x, fx) + _flip(y, fy)], k, which)
        return pltpu.make_async_remote_copy(
            src_ref=landed, dst_ref=landed, send_sem=d2d_send.at[k * npeer + j], recv_sem=d2d_recv.at[k * npeer + j],
            device_id=(x, y, 1 - c), device_id_type=MESH)

    pairs = [(k, j) for k in range(len(src)) for j in range(npeer)]
    return ([ici(k, j, 2 * x + y) for k, j in pairs],
            [ici(k, j, 2 * _flip(x, XY_FLIPS[j][0]) + _flip(y, XY_FLIPS[j][1])) for k, j in pairs],
            [d2d(k, j, c) for k, j in pairs], [d2d(k, j, 1 - c) for k, j in pairs])


def gather_scratch(n):
    return _sem_scratch(n * len(XY_FLIPS)) * 2


def gathered_shapes(shards):
    return [jax.ShapeDtypeStruct((N_SHARD,) + s.shape, s.dtype) for s in shards]


def fill_own(gathered, shards):
    if not shards:
        return []
    slot = 2 * lax.axis_index("x") + lax.axis_index("y")
    return [lax.dynamic_update_index_in_dim(g, s, slot, 0) for g, s in zip(gathered, shards)]


def all_gather_shards(shards):
    n = len(shards)

    def body(*refs):
        sends, arrivals, forwards, forwarded = gather_copies(refs[:n], refs[n:2 * n], *refs[2 * n:])
        for cp in sends:
            cp.start()
        for landed, onward in zip(arrivals, forwards):
            landed.wait_recv()
            onward.start()
        for cp in forwarded:
            cp.wait_recv()
        for cp in sends + forwards:
            cp.wait_send()

    out = pl.pallas_call(body, name="gather_weights", in_specs=[ANY] * n, out_specs=[ANY] * n,
                         out_shape=gathered_shapes(shards), scratch_shapes=gather_scratch(n))(*shards)
    return fill_own(out, shards)


def placement():
    x, y, c = lax.axis_index("x"), lax.axis_index("y"), lax.axis_index("c")
    me = 2 * x + y
    others = [j + (j >= me).astype(jnp.int32) for j in range(N_SHARD - 1)]
    return jnp.stack([c, me] + others).astype(jnp.int32)


def hosted_gather(shards):
    return (gather_copies, list(shards), gathered_shapes(shards), gather_scratch(len(shards)),
            lambda got: fill_own(got, shards))


def pair_exchange_copies(src, got, send_sems, recv_sems):
    x, y, c = lax.axis_index("x"), lax.axis_index("y"), lax.axis_index("c")

    def copy(k):
        half = src[k].shape[1] // 2
        theirs = src[k].at[:, pl.ds(pl.multiple_of((1 - c) * half, 8), half), :]
        return pltpu.make_async_remote_copy(
            src_ref=theirs, dst_ref=got[k], send_sem=send_sems.at[k], recv_sem=recv_sems.at[k],
            device_id=(x, y, 1 - c), device_id_type=MESH)

    sends = [copy(k) for k in range(len(src))]
    return sends, sends, [], []


def _half_shapes(sources):
    return [jax.ShapeDtypeStruct((s.shape[0], s.shape[1] // 2, s.shape[2]), s.dtype) for s in sources]


def hosted_pair_exchange(sources):
    return (pair_exchange_copies, list(sources), _half_shapes(sources), _sem_scratch(len(sources)), list)


def pair_exchange(name, sources):
    n = len(sources)

    def body(*refs):
        sends, arrivals, _, _ = pair_exchange_copies(refs[:n], refs[n:2 * n], *refs[2 * n:])
        for cp in sends:
            cp.start()
        for cp in arrivals:
            cp.wait_recv()
        for cp in sends:
            cp.wait_send()

    halves = _half_shapes(sources)
    return pl.pallas_call(body, name=name, in_specs=[ANY] * n, out_specs=[ANY] * n,
                          out_shape=halves, scratch_shapes=_sem_scratch(n))(*sources)


def chip_exchange(parts):
    n = len(parts)

    def body(*refs):
        sends, arrivals = chip_exchange_copies(refs[:n], refs[n:2 * n], *refs[2 * n:])
        for cp in sends:
            cp.start()
        for cp in arrivals:
            cp.wait_recv()
        for cp in sends:
            cp.wait_send()

    return pl.pallas_call(
        body, name="grads_chip_exchange", in_specs=[ANY] * n, out_specs=[ANY] * n,
        out_shape=[jax.ShapeDtypeStruct(p.shape, p.dtype) for p in parts],
        scratch_shapes=_sem_scratch(n * len(XY_FLIPS)),
    )(*parts)


def chip_exchange_copies(src, dst, send_sems, recv_sems):
    npeer = len(XY_FLIPS)
    x, y, c = lax.axis_index("x"), lax.axis_index("y"), lax.axis_index("c")
    me = 2 * x + y

    def copy(k, j, sending):
        fx, fy = XY_FLIPS[j]
        px, py = _flip(x, fx), _flip(y, fy)
        peer = 2 * px + py
        return pltpu.make_async_remote_copy(
            src_ref=src[k].at[peer], dst_ref=dst[k].at[me if sending else peer],
            send_sem=send_sems.at[k * npeer + j], recv_sem=recv_sems.at[k * npeer + j],
            device_id=(px, py, c), device_id_type=MESH)

    pairs = [(k, j) for k in range(len(src)) for j in range(npeer)]
    return [copy(k, j, True) for k, j in pairs], [copy(k, j, False) for k, j in pairs]


def sibling_share(halves):
    n = len(halves)

    def body(*refs):
        src, got = refs[:n], refs[n:2 * n]
        send_sems, recv_sems = refs[2 * n:]
        x, y, c = lax.axis_index("x"), lax.axis_index("y"), lax.axis_index("c")
        sends = [pltpu.make_async_remote_copy(
            src_ref=src[k], dst_ref=got[k], send_sem=send_sems.at[k], recv_sem=recv_sems.at[k],
            device_id=(x, y, 1 - c), device_id_type=MESH) for k in range(n)]
        for cp in sends:
            cp.start()
        for cp in sends:
            cp.wait_recv()
        for cp in sends:
            cp.wait_send()

    return pl.pallas_call(
        body, name="grads_sibling_share", in_specs=[ANY] * n, out_specs=[ANY] * n,
        out_shape=[jax.ShapeDtypeStruct(h.shape, h.dtype) for h in halves], scratch_shapes=_sem_scratch(n),
    )(*halves)


ADD_TILE_ELEMS = 512 * 1024


def _row_tile(rows, cols):
    return max(t for t in range(8, rows + 1, 8) if rows % t == 0 and t * cols <= ADD_TILE_ELEMS)


def _prefetch_call(body, name, place, grid, in_specs, out_specs, out_shape, args):
    return pl.pallas_call(
        body, name=name, out_shape=out_shape,
        grid_spec=pltpu.PrefetchScalarGridSpec(num_scalar_prefetch=1, grid=grid, in_specs=in_specs,
                                               out_specs=out_specs),
        compiler_params=pltpu.CompilerParams(dimension_semantics=("arbitrary",) * len(grid),
                                             vmem_limit_bytes=VMEM_LIMIT),
    )(place, *args)


def pair_add(name, place, src, got, dtype):
    n4, half, cols = got.shape
    tile = _row_tile(half, cols)
    nt = half // tile

    def body(pr, a_ref, b_ref, o_ref):
        o_ref[...] = (a_ref[...] + b_ref[...]).astype(o_ref.dtype)

    mine = pl.BlockSpec((None, tile, cols), lambda s, i, pr: (s, pr[0] * nt + i, 0))
    blk = pl.BlockSpec((None, tile, cols), lambda s, i, pr: (s, i, 0))
    return _prefetch_call(body, name, place, (n4, nt), [mine, blk], blk,
                          jax.ShapeDtypeStruct(got.shape, dtype), (src, got))


def chip_add(name, place, part, from_chips):
    _, half, cols = part.shape
    tile = _row_tile(half, cols)

    def body(pr, own_ref, r0_ref, r1_ref, r2_ref, o_ref):
        me = pr[1]
        own, r0, r1, r2 = (r[...].astype(F32) for r in (own_ref, r0_ref, r1_ref, r2_ref))
        t0 = jnp.where(me == 0, own, r0)
        t1 = jnp.where(me == 0, r0, jnp.where(me == 1, own, r1))
        t2 = jnp.where(me <= 1, r1, jnp.where(me == 2, own, r2))
        t3 = jnp.where(me == 3, own, r2)
        o_ref[...] = ((t0 + t1) + t2) + t3

    def slab(j):
        return pl.BlockSpec((None, tile, cols), lambda i, pr: (pr[j], i, 0))

    return _prefetch_call(body, name, place, (half // tile,), [slab(1), slab(2), slab(3), slab(4)],
                          pl.BlockSpec((tile, cols), lambda i, pr: (i, 0)),
                          jax.ShapeDtypeStruct((half, cols), F32), (part, from_chips, from_chips, from_chips))


def pair_adds(tag, place, sources, got, narrow):
    return [pair_add(f"grads_pair_add_{tag}{k}", place, s, g, BF16 if nar else F32)
            for k, (s, g, nar) in enumerate(zip(sources, got, narrow))]


def finish_sums(place, parts, from_chips):
    halves = [chip_add(f"grads_chip_add{k}", place, p, f) for k, (p, f) in enumerate(zip(parts, from_chips))]
    return list(zip(halves, sibling_share(halves)))


ADAM_ROWS = 256


def adamw_update(name, place, halves, w, m, v):
    nsub, rows, cols = w.shape
    half = rows // 2
    tr = ADAM_ROWS if half % ADAM_ROWS == 0 else half
    nth = half // tr

    def body(pr, *refs):
        g_refs, (w_ref, m_ref, v_ref, g_ref, d_ref, nm_ref, nv_ref) = refs[:2 * nsub], refs[2 * nsub:]
        l = pl.program_id(0)
        mine = (pl.program_id(1) // nth) == pr[0]
        g = None
        for s in range(nsub):
            gs = jnp.where(mine, g_refs[2 * s][...], g_refs[2 * s + 1][...])
            g = gs if g is None else jnp.where(l == s, gs, g)
        m2 = ADAM_B1 * m_ref[...] + (1.0 - ADAM_B1) * g
        v2 = ADAM_B2 * v_ref[...] + (1.0 - ADAM_B2) * (g * g)
        m_hat = m2 / (1.0 - ADAM_B1 ** ADAM_STEP)
        v_hat = v2 / (1.0 - ADAM_B2 ** ADAM_STEP)
        g_ref[...] = g
        d_ref[...] = -ADAM_LR * (m_hat / (jnp.sqrt(v_hat) + ADAM_EPS) + ADAM_WD * w_ref[...])
        nm_ref[...] = m2
        nv_ref[...] = v2

    gblk = pl.BlockSpec((tr, cols), lambda l, i, pr: (i % nth, 0))
    blk = pl.BlockSpec((None, tr, cols), lambda l, i, pr: (l, i, 0))
    out = jax.ShapeDtypeStruct((nsub, rows, cols), F32)
    return _prefetch_call(body, name, place, (nsub, rows // tr), [gblk] * (2 * nsub) + [blk] * 3, [blk] * 4,
                          [out] * 4, [h for pair in halves for h in pair] + [w, m, v])


WEIGHT_NAMES = ("meta_tokens", "a_mu", "a_w_r", "a_w_k", "a_w_v", "a_w_o", "a_w0", "a_w1", "a_w2", "a_a0", "a_a1",
                "a_a2", "a_g1", "a_g2", "a_k_k", "a_k_a", "a_r_k", "a_gn_w", "a_gn_b", "kv_w_k", "kv_w_v", "b_w_q",
                "b_sinks", "b_w_o", "mlp_w_up", "mlp_w_down", "ln_g", "ln_b")
BIG_NAMES = ("a_w_r", "a_w_k", "a_w_v", "a_w_o", "b_w_q", "b_w_o")
EARLY_NAMES, LATE_NAMES = BIG_NAMES[:3], BIG_NAMES[3:]
PACK_MATS = (("kv_w_k", 256), ("kv_w_v", 256), ("a_w1", 64), ("a_a1", 64), ("a_g1", 128), ("a_w2", 64),
             ("a_a2", 64), ("a_g2", 128))
COLUMN_CUT = ("a_w2", "a_a2", "a_g2")
PACK_VECS = (("a_mu", 6), ("a_w0", 1), ("a_a0", 1), ("a_k_k", 1), ("a_k_a", 1), ("a_gn_w", 1), ("a_gn_b", 1),
             ("ln_g", 4), ("ln_b", 4), ("meta_tokens", 16))
PACK_REPL = (("a_r_k", 4), ("b_sinks", 1))
SHARD_W = D_MODEL // N_SHARD


def _tiles(rows):
    return -(-rows // SUBLANES) * SUBLANES


N_MAT_ROWS = sum(_tiles(r) for _, r in PACK_MATS)
N_VEC_ROWS = sum(_tiles(r) for _, r in PACK_VECS)
N_PACK_ROWS = -(-(N_MAT_ROWS + N_VEC_ROWS + sum(_tiles(r) for _, r in PACK_REPL)) // 16) * 16
N_GATHER_VEC_ROWS = -(-N_VEC_ROWS // 16) * 16


def _pad_rows(arr, axis):
    rows = arr.shape[axis]
    pad = [(0, 0)] * arr.ndim
    pad[axis] = (0, _tiles(rows) - rows)
    return jnp.pad(arr, pad) if _tiles(rows) != rows else arr


def _pack_rows(arr):
    if arr.size == N_HEADS:
        arr = jnp.pad(arr.reshape(1, N_HEADS), ((0, 0), (0, SHARD_W - N_HEADS)))
    return _pad_rows(arr.reshape(-1, SHARD_W), 0)


def pack_small(get):
    parts = [_pack_rows(get(name)) for name, _ in PACK_MATS + PACK_VECS + PACK_REPL]
    used = sum(p.shape[0] for p in parts)
    return jnp.concatenate(parts + [jnp.zeros((N_PACK_ROWS - used, SHARD_W), F32)], axis=0)


def unpack_small(pack, shapes):
    out, off = {}, 0
    for name, rows in PACK_MATS + PACK_VECS + PACK_REPL:
        piece = pack[off:off + rows]
        off += _tiles(rows)
        out[name] = piece[:, :N_HEADS].reshape(shapes[name]) if name == "b_sinks" else piece.reshape(shapes[name])
    return out


def whole_weights(big_names, gathered_big, mats, vecs, a_r_k, b_sinks):
    p = {name: g.reshape(D_MODEL, D_MODEL) for name, g in zip(big_names, gathered_big)}
    off = 0
    for name, rows in PACK_MATS:
        piece = mats[:, off:off + rows]
        off += rows
        if name in COLUMN_CUT:
            p[name] = piece.transpose(1, 0, 2).reshape(rows, D_MODEL)
        else:
            p[name] = piece.reshape(D_MODEL, rows)
    v = vecs.transpose(1, 0, 2).reshape(-1, D_MODEL)
    off = 0
    for name, rows in PACK_VECS:
        p[name] = v[off:off + rows]
        off += _tiles(rows)
    for i in range(2):
        for j in range(2):
            p[f"ln_g{i}{j}"] = p["ln_g"][2 * i + j:2 * i + j + 1]
            p[f"ln_b{i}{j}"] = p["ln_b"][2 * i + j:2 * i + j + 1]
    p["a_r_k"] = a_r_k.reshape(1, D_MODEL)
    p["b_sinks"] = b_sinks
    return p


def small_grad_pack(g):
    parts = []
    for name, rows in PACK_MATS:
        if name in COLUMN_CUT:
            parts.append(g[name].reshape(rows, N_SHARD, SHARD_W).transpose(1, 0, 2))
        else:
            parts.append(g[name].reshape(N_SHARD, rows, SHARD_W))
    vecs = {n: g[n] for n in ("a_mu", "a_w0", "a_a0", "a_k_k", "a_k_a", "a_gn_w", "a_gn_b", "meta_tokens")}
    vecs["ln_g"] = jnp.concatenate([g[f"ln_g{i}{j}"] for i in range(2) for j in range(2)], axis=0)
    vecs["ln_b"] = jnp.concatenate([g[f"ln_b{i}{j}"] for i in range(2) for j in range(2)], axis=0)
    for name, rows in PACK_VECS:
        parts.append(_pad_rows(vecs[name].reshape(rows, N_SHARD, SHARD_W).transpose(1, 0, 2), 1))
    r_k = jnp.broadcast_to(g["a_r_k"].reshape(1, -1, SHARD_W), (N_SHARD, D_MODEL // SHARD_W, SHARD_W))
    sinks = jnp.pad(g["b_sinks"].reshape(1, 1, N_HEADS), ((0, 0), (0, 0), (0, SHARD_W - N_HEADS)))
    parts += [_pad_rows(r_k, 1), _pad_rows(jnp.broadcast_to(sinks, (N_SHARD, 1, SHARD_W)), 1)]
    used = sum(p.shape[1] for p in parts)
    parts.append(jnp.zeros((N_SHARD, N_PACK_ROWS - used, SHARD_W), F32))
    return jnp.concatenate(parts, axis=1)


def train_step(vals):
    w = {n: vals[n] for n in WEIGHT_NAMES}
    w_pack = pack_small(lambda n: w[n])
    early = [w[n][0].astype(BF16) for n in EARLY_NAMES]
    early += [w_pack[:N_MAT_ROWS].astype(BF16), w_pack[N_MAT_ROWS:N_MAT_ROWS + N_GATHER_VEC_ROWS]]
    gathered = all_gather_shards(early)
    ne = len(EARLY_NAMES)
    p = whole_weights(EARLY_NAMES, gathered[:ne], gathered[ne], gathered[ne + 1][:, :N_VEC_ROWS], w["a_r_k"],
                      w["b_sinks"])
    nb = len(BIG_NAMES)

    def late_set(big, layer):
        shards = [w[n][0].astype(BF16) for n in big]
        shards += [w["mlp_w_up"][layer].astype(BF16), w["mlp_w_down"][layer].astype(BF16)]

        def weights(got):
            out = {n: x.reshape(D_MODEL, D_MODEL) for n, x in zip(big, got)}
            out[f"mlp_up{layer}"], out[f"mlp_down{layer}"] = got[len(big):]
            return out

        return shards, weights

    late = (late_set((), 1), late_set(LATE_NAMES, 0))

    place = placement()
    ready = {}
    a_names, b_names = BIG_NAMES[:4], BIG_NAMES[4:]

    def early_sources(g):
        return ([g[n].reshape(N_SHARD, SHARD_W, D_MODEL) for n in b_names]
                + [g["mlp_up0"], g["mlp_up1"], g["mlp_down0"], g["mlp_down1"]])

    def early_parts(srcs, got):
        ready["parts"] = pair_adds("early", place, srcs, got, [True] * len(srcs))
        return ready["parts"]

    loss, gx, g, early_from_chips = local_step(vals["x"][0], vals["loss_target"][0], p, late,
                                               (early_sources, early_parts))
    loss = lax.psum(loss, ("x", "y", "c"))
    srcs = [g[n].reshape(N_SHARD, SHARD_W, D_MODEL) for n in a_names] + [small_grad_pack(g)]
    rest = pair_adds("late", place, srcs, pair_exchange("grads_pair_exchange", srcs), [True] * len(a_names) + [False])
    rest_from_chips = chip_exchange(rest)
    na = len(a_names)
    halves = finish_sums(place, rest[:na] + ready["parts"] + rest[na:],
                         list(rest_from_chips[:na]) + list(early_from_chips) + list(rest_from_chips[na:]))

    res = {}
    for k, n in enumerate(BIG_NAMES):
        res[n] = adamw_update("adamw_" + n, place, halves[k:k + 1], w[n], vals["m_" + n], vals["v_" + n])
    for k, n in ((nb, "mlp_w_up"), (nb + 2, "mlp_w_down")):
        res[n] = adamw_update("adamw_" + n, place, halves[k:k + 2], w[n], vals["m_" + n], vals["v_" + n])
    packs = adamw_update("adamw_small", place, halves[-1:], w_pack[None], pack_small(lambda n: vals["m_" + n])[None],
                         pack_small(lambda n: vals["v_" + n])[None])
    shapes = {n: w[n].shape for n in WEIGHT_NAMES}
    small = [unpack_small(pk[0], shapes) for pk in packs]
    outs = [loss, gx[None]]
    for t in range(4):
        outs += [res[n][t] if n in res else small[t][n] for n in WEIGHT_NAMES]
    return tuple(outs)


def kernel(x, meta_tokens, a_mu, a_w_r, a_w_k, a_w_v, a_w_o, a_w0, a_w1, a_w2, a_a0, a_a1, a_a2, a_g1, a_g2, a_k_k,
           a_k_a, a_r_k, a_gn_w, a_gn_b, kv_w_k, kv_w_v, b_w_q, b_sinks, b_w_o, mlp_w_up, mlp_w_down, ln_g, ln_b,
           loss_target, m_meta_tokens, m_a_mu, m_a_w_r, m_a_w_k, m_a_w_v, m_a_w_o, m_a_w0, m_a_w1, m_a_w2, m_a_a0,
           m_a_a1, m_a_a2, m_a_g1, m_a_g2, m_a_k_k, m_a_k_a, m_a_r_k, m_a_gn_w, m_a_gn_b, m_kv_w_k, m_kv_w_v,
           m_b_w_q, m_b_sinks, m_b_w_o, m_mlp_w_up, m_mlp_w_down, m_ln_g, m_ln_b, v_meta_tokens, v_a_mu, v_a_w_r,
           v_a_w_k, v_a_w_v, v_a_w_o, v_a_w0, v_a_w1, v_a_w2, v_a_a0, v_a_a1, v_a_a2, v_a_g1, v_a_g2, v_a_k_k,
           v_a_k_a, v_a_r_k, v_a_gn_w, v_a_gn_b, v_kv_w_k, v_kv_w_v, v_b_w_q, v_b_sinks, v_b_w_o, v_mlp_w_up,
           v_mlp_w_down, v_ln_g, v_ln_b):
    return train_step(dict(locals()))
```

```python
import functools

import numpy as np
import jax
import jax.numpy as jnp
from jax import lax
from jax.experimental import pallas as pl
from jax.experimental.pallas import tpu as pltpu

F32 = jnp.float32
BF16 = jnp.bfloat16

D_MODEL = 1024
N_HEADS = 16
HEAD_DIM = 64
N_HEADS_KV = 4
GROUP = 4
KV_DIM = N_HEADS_KV * HEAD_DIM
N_META = 16
BLOCK = 128
PAD_FRONT = BLOCK - N_META
TOK0 = PAD_FRONT + N_META
N_FF_CHUNK = 4
N_SHARD = 4
GN_EPS = 64e-5
LN_EPS = 1e-5
ROPE_THETA = 10000.0
ALPHA = 4.0 ** 0.25
ADAM_LR, ADAM_B1, ADAM_B2, ADAM_EPS, ADAM_WD, ADAM_STEP = 0.001, 0.9, 0.999, 1e-08, 0.01, 10
SCAN_T = 64
PAIR = 128
KVW = GROUP * HEAD_DIM
VMEM_LIMIT = 60 * 1024 * 1024
MESH = pl.DeviceIdType.MESH


def _dot(a, b, ca, cb):
    return lax.dot_general(a.astype(BF16), b.astype(BF16), (((ca,), (cb,)), ((), ())),
                           preferred_element_type=F32)


@jax.custom_vjp
def mm(a, b):
    return _dot(a, b, 1, 0)


def _mm_fwd(a, b):
    return mm(a, b), b


def _mm_bwd(b, g):
    return _dot(g, b, 1, 1), jnp.zeros_like(b)


mm.defvjp(_mm_fwd, _mm_bwd)


@jax.custom_vjp
def mm_tap(a, b, tap):
    return _dot(a, b, 1, 0)


mm_tap.defvjp(lambda a, b, tap: (_dot(a, b, 1, 0), b), lambda b, g: (_dot(g, b, 1, 1), jnp.zeros_like(b), g))


def tmm(x, w, taps, xs):
    y = mm(x, w) if taps is None else mm_tap(x, w, taps[len(xs)])
    xs.append(x)
    return y


def vjp_taps(core, tap_shapes, args, cot):
    taps = [jnp.zeros(s, F32) for s in tap_shapes]
    _, vjp, xs = jax.vjp(core, taps, *args, has_aux=True)
    out = vjp(cot)
    return out[1:], [_dot(x, g, 0, 0) for x, g in zip(xs, out[0])]


def _split3(x):
    x1 = x.astype(BF16)
    r1 = x - x1.astype(F32)
    x2 = r1.astype(BF16)
    x3 = (r1 - x2.astype(F32)).astype(BF16)
    return x1, x2, x3


def _exact_dot(x, m01, cb=0):
    acc = None
    for piece in _split3(x)[:2]:
        t = lax.dot_general(piece, m01, (((1,), (cb,)), ((), ())), preferred_element_type=F32)
        acc = t if acc is None else acc + t
    return acc


def _head_matrices():
    e = np.zeros((D_MODEL, N_HEADS), np.float32)
    e[np.arange(D_MODEL), np.arange(D_MODEL) // HEAD_DIM] = 1.0
    return jnp.asarray(e, BF16), jnp.asarray(e.T, BF16)


@jax.custom_vjp
def hsum(x, e, et):
    return _exact_dot(x, e)


@jax.custom_vjp
def hbc(s, e, et):
    return _exact_dot(s, et)


hsum.defvjp(lambda x, e, et: (_exact_dot(x, e), (e, et)),
            lambda res, g: (hbc(g, *res), jnp.zeros_like(res[0]), jnp.zeros_like(res[1])))
hbc.defvjp(lambda s, e, et: (_exact_dot(s, et), (e, et)),
           lambda res, g: (hsum(g, *res), jnp.zeros_like(res[0]), jnp.zeros_like(res[1])))


def _sigmoid(u):
    return 0.5 * (jnp.tanh(0.5 * u) + 1.0)


def _softplus(u):
    return jnp.maximum(u, 0.0) + jnp.log(1.0 + jnp.exp(-jnp.abs(u)))


def _layer_norm(z, g, b):
    mu = jnp.mean(z, axis=-1, keepdims=True)
    zc = z - mu
    var = jnp.mean(zc * zc, axis=-1, keepdims=True)
    return zc * lax.rsqrt(var + LN_EPS) * g + b


def _zero_map(nd):
    return lambda c, i: (0,) * nd


def _params():
    return pltpu.CompilerParams(dimension_semantics=("arbitrary", "arbitrary"), vmem_limit_bytes=VMEM_LIMIT)


def rowwise(name, fn, rows, consts, out_rows, out_accs, tm, nc=1, hosted=None):
    lp = rows[0].shape[-2]
    nt = lp // tm
    assert nt * tm == lp, (name, lp, tm)
    copies_fn, hosted_src, hosted_shapes, hosted_scratch, hosted_post = hosted or (None, (), [], [], None)
    ng = len(hosted_src)
    in_specs, args = [], []
    for a in rows:
        if isinstance(a, tuple):
            a, block_rows, block_index = a
            in_specs.append(pl.BlockSpec((block_rows, a.shape[1]),
                                         functools.partial(lambda f, c, i: (f(i), 0), block_index)))
        elif a.ndim == 2:
            in_specs.append(pl.BlockSpec((tm, a.shape[1]), lambda c, i: (i, 0)))
        else:
            in_specs.append(pl.BlockSpec((a.shape[0], tm, a.shape[2]), lambda c, i: (0, i, 0)))
        args.append(a)
    for cst in consts:
        if isinstance(cst, tuple):
            arr, bs, im = cst
            in_specs.append(pl.BlockSpec(bs, im))
        else:
            arr = cst
            in_specs.append(pl.BlockSpec(arr.shape, _zero_map(arr.ndim), pipeline_mode=pl.Buffered(1)))
        args.append(arr)
    out_shape, out_specs, acc_per_chunk = [], [], []
    for spec in out_rows:
        if len(spec) == 4:
            out_shape.append(jax.ShapeDtypeStruct((spec[3], lp, spec[0]), spec[1]))
            out_specs.append(pl.BlockSpec((spec[3], tm, spec[0]), lambda c, i: (0, i, 0)))
        elif len(spec) == 3 and spec[2]:
            out_shape.append(jax.ShapeDtypeStruct((nc, lp, spec[0]), spec[1]))
            out_specs.append(pl.BlockSpec((None, tm, spec[0]), lambda c, i: (c, i, 0)))
        else:
            out_shape.append(jax.ShapeDtypeStruct((lp, spec[0]), spec[1]))
            out_specs.append(pl.BlockSpec((tm, spec[0]), lambda c, i: (i, 0)))
    for spec in out_accs:
        out_shape.append(jax.ShapeDtypeStruct(spec[0], spec[1]))
        if len(spec) == 4:
            out_specs.append(pl.BlockSpec(spec[2], spec[3]))
            acc_per_chunk.append(True)
        else:
            out_specs.append(pl.BlockSpec(spec[0], _zero_map(len(spec[0])), pipeline_mode=pl.Buffered(1)))
            acc_per_chunk.append(False)
    n_in, n_or, n_out = len(args), len(out_rows), len(out_shape)

    def body(*refs):
        c = pl.program_id(0)
        i = pl.program_id(1)
        if ng:
            src, dst = refs[n_in:n_in + ng], refs[n_in + ng + n_out:n_in + 2 * ng + n_out]
            sends, arrivals, forwards, forwarded = copies_fn(src, dst, *refs[n_in + 2 * ng + n_out:])

            @pl.when(jnp.logical_and(c == 0, i == 0))
            def _():
                for cp in sends:
                    cp.start()

        vals = [r[...] for r in refs[:n_in]]
        outs_r, outs_a = fn(c, i, *vals)
        out_refs = refs[n_in + ng:n_in + ng + n_out]
        for ref, val in zip(out_refs[:n_or], outs_r):
            ref[...] = val.astype(ref.dtype)
        for ref, val, per_chunk in zip(out_refs[n_or:], outs_a, acc_per_chunk):
            first = (i == 0) if per_chunk else jnp.logical_and(i == 0, c == 0)

            @pl.when(first)
            def _():
                ref[...] = val.astype(ref.dtype)

            @pl.when(jnp.logical_not(first))
            def _():
                ref[...] += val.astype(ref.dtype)

        if ng:
            @pl.when(jnp.logical_and(c == nc - 1, i == max(nt - 3, 0)))
            def _():
                for k, landed in enumerate(arrivals):
                    landed.wait_recv()
                    if forwards:
                        forwards[k].start()

            @pl.when(jnp.logical_and(c == nc - 1, i == nt - 1))
            def _():
                for cp in forwarded:
                    cp.wait_recv()
                for cp in sends + forwards:
                    cp.wait_send()

    outs = pl.pallas_call(body, name=name, grid=(nc, nt), in_specs=in_specs + [ANY] * ng,
                          out_specs=out_specs + [ANY] * ng, out_shape=out_shape + list(hosted_shapes),
                          scratch_shapes=list(hosted_scratch), compiler_params=_params())(*args, *hosted_src)
    if ng:
        return outs[:n_or], outs[n_or:n_out], hosted_post(outs[n_out:])
    return outs[:n_or], outs[n_or:]


def _row_ids(i, tm):
    return i * tm + lax.broadcasted_iota(jnp.int32, (tm, 1), 0)


SUBLANES = 8


def _halo_before(arr, tm):
    return (arr, SUBLANES, lambda i: jnp.maximum(i * (tm // SUBLANES) - 1, 0))


def _halo_after(arr, tm):
    last = arr.shape[0] // SUBLANES - 1
    return (arr, SUBLANES, lambda i: jnp.minimum((i + 1) * (tm // SUBLANES), last))


def _pick_row(block8, row):
    rows = lax.broadcasted_iota(jnp.int32, block8.shape, 0)
    return jnp.sum(jnp.where(rows == row, block8, 0.0), axis=0, keepdims=True)


def _shift_down(x, before8, i):
    rows = lax.broadcasted_iota(jnp.int32, x.shape, 0)
    top = _pick_row(before8, SUBLANES - 1) * (i > 0).astype(F32)
    return jnp.where(rows == 0, top, pltpu.roll(x, 1, 0))


def _shift_up(x, after8, i, nt):
    rows = lax.broadcasted_iota(jnp.int32, x.shape, 0)
    bottom = _pick_row(after8, 0) * (i < nt - 1).astype(F32)
    return jnp.where(rows == x.shape[0] - 1, bottom, pltpu.roll(x, x.shape[0] - 1, 0))


PRE_TAPS = (D_MODEL, D_MODEL, D_MODEL, 64, D_MODEL, 64, D_MODEL, 128, D_MODEL)


def rwkv_pre(e, et, ws, taps, h, hp, mu_r, mu_w, mu_k, mu_v, mu_a, mu_g, w0, a0, k_k, k_a):
    w_r, w_k, w_v, w1, w2, a1, a2, g1, g2 = ws
    xs = []
    xx = hp - h
    r = tmm(h + xx * mu_r, w_r, taps, xs)
    k = tmm(h + xx * mu_k, w_k, taps, xs)
    v = tmm(h + xx * mu_v, w_v, taps, xs)
    wraw = -_softplus(-(w0 + tmm(jnp.tanh(tmm(h + xx * mu_w, w1, taps, xs)), w2, taps, xs))) - 0.5
    lw = -jnp.exp(wraw)
    a = _sigmoid(a0 + tmm(tmm(h + xx * mu_a, a1, taps, xs), a2, taps, xs))
    g = tmm(_sigmoid(tmm(h + xx * mu_g, g1, taps, xs)), g2, taps, xs)
    kk = k * k_k
    ss = hsum(kk * kk, e, et)
    pos = ss > 0.0
    nrm = jnp.where(pos, jnp.sqrt(jnp.where(pos, ss, 1.0)), 0.0)
    kk = kk * hbc(1.0 / jnp.maximum(nrm, 1e-12), e, et)
    k2 = k * (1.0 + (a - 1.0) * k_a)
    return (r, lw, k2, v, -kk, kk * a, g), xs


def rwkv_post(e, et, w_o, taps, y, r, k2, v, g, h0, gn_w, gn_b, rk, lg, lb):
    xs = []
    inv_n = 1.0 / HEAD_DIM
    yc = y - hbc(hsum(y, e, et) * inv_n, e, et)
    yv = hsum(yc * yc, e, et) * inv_n
    yn = yc * hbc(lax.rsqrt(yv + GN_EPS), e, et) * gn_w + gn_b
    bonus = hbc(hsum(r * k2 * rk, e, et), e, et) * v
    mix = tmm((yn + bonus) * g, w_o, taps, xs)
    return _layer_norm(ALPHA * h0 + mix, lg, lb), xs


@jax.custom_vjp
def sq_relu(x):
    r = jnp.maximum(x, 0.0)
    return r * r


sq_relu.defvjp(lambda x: (sq_relu(x), x), lambda x, g: (g * (2.0 * jnp.maximum(x, 0.0)),))


def _rot_half(t):
    n = t.shape[-1]
    lane = lax.broadcasted_iota(jnp.int32, t.shape, t.ndim - 1)
    lo = (lane % HEAD_DIM) < (HEAD_DIM // 2)
    return jnp.where(lo, -pltpu.roll(t, n - HEAD_DIM // 2, t.ndim - 1), pltpu.roll(t, HEAD_DIM // 2, t.ndim - 1))


@jax.custom_vjp
def rot_half(t):
    return _rot_half(t)


rot_half.defvjp(lambda t: (_rot_half(t), None), lambda _, g: (-_rot_half(g),))


def _tile_lanes(t, width):
    return jnp.concatenate([t] * (width // t.shape[-1]), axis=-1)


def qkv_proj(cos, sin, wq, wk, wv, taps, h):
    xs = []
    q = tmm(h, wq, taps, xs)
    k = tmm(h, wk, taps, xs)
    v = tmm(h, wv, taps, xs)
    cq, sq = _tile_lanes(cos, D_MODEL), _tile_lanes(sin, D_MODEL)
    ck, sk = _tile_lanes(cos, KV_DIM), _tile_lanes(sin, KV_DIM)
    return (q * cq + rot_half(q) * sq, k * ck + rot_half(k) * sk, v), xs


def attn_out(w_o, taps, o, h, lg, lb):
    xs = []
    return _layer_norm(ALPHA * h + tmm(o, w_o, taps, xs), lg, lb), xs


def _scan_consts():
    t = SCAN_T
    tri = np.tril(np.ones((t, t), np.float32))
    rows = np.arange(2 * t)
    same = (rows[:, None] // t) == (rows[None, :] // t)
    strict = same & ((rows[None, :] % t) < (rows[:, None] % t))
    incl = same & ((rows[None, :] % t) <= (rows[:, None] % t))
    lane = np.arange(PAIR)
    masks = np.zeros((8, PAIR), np.float32)
    masks[0] = (lane // HEAD_DIM) == 0
    masks[1] = (lane // HEAD_DIM) == 1
    return (jnp.asarray(tri, BF16), jnp.asarray(strict.astype(np.float32)), jnp.asarray(incl.astype(np.float32)),
            jnp.asarray(masks), jnp.asarray(np.eye(2 * t, dtype=np.float32)))


def _scan_dot(a, b, ca, cb):
    return _dot(a, b, ca, cb)


@functools.partial(jax.custom_vjp, nondiff_argnums=(2, 3))
def _dotf(a, b, ca, cb):
    return _scan_dot(a, b, ca, cb)


def _dotf_bwd(ca, cb, res, g):
    a, b = res
    if ca == 1:
        da = _scan_dot(g, b, 1, 1 - cb)
    else:
        da = _scan_dot(b, g, 1 - cb, 1)
    if cb == 0:
        db = _scan_dot(a, g, 1 - ca, 0)
    else:
        db = _scan_dot(g, a, 0, 1 - ca)
    return da, db


_dotf.defvjp(lambda a, b, ca, cb: (_scan_dot(a, b, ca, cb), (a, b)), _dotf_bwd)


def _tri_dot(tri, x, ct):
    acc = None
    for piece in _split3(x):
        t = lax.dot_general(tri, piece, (((ct,), (0,)), ((), ())), preferred_element_type=F32)
        acc = t if acc is None else acc + t
    return acc


@jax.custom_vjp
def _cumsum_rows(tri, x):
    return _tri_dot(tri, x, 1)


_cumsum_rows.defvjp(lambda tri, x: (_tri_dot(tri, x, 1), tri),
                    lambda tri, g: (jnp.zeros_like(tri), _tri_dot(tri, g, 0)))


@jax.custom_vjp
def _unstack2(x):
    t = x.shape[0] // 2
    return x[:t] + x[t:]


_unstack2.defvjp(lambda x: (_unstack2(x), None), lambda _, g: (jnp.concatenate([g, g], axis=0),))


@jax.custom_vjp
def _last_row(x):
    return x[x.shape[0] - 1:, :]


def _last_row_bwd(_, g):
    rows = lax.broadcasted_iota(jnp.int32, (SCAN_T, g.shape[1]), 0)
    return (jnp.where(rows == SCAN_T - 1, jnp.broadcast_to(g, (SCAN_T, g.shape[1])), 0.0),)


_last_row.defvjp(lambda x: (_last_row(x), None), _last_row_bwd)


@jax.custom_vjp
def _halves(x):
    n = x.shape[0] // 2
    return x[:n], x[n:]


_halves.defvjp(lambda x: (_halves(x), None), lambda _, g: (jnp.concatenate(list(g), axis=0),))


@jax.custom_vjp
def _quads(x):
    n, m = x.shape[0] // 2, x.shape[1] // 2
    return x[:n, :m], x[:n, m:], x[n:, :m], x[n:, m:]


_quads.defvjp(lambda x: (_quads(x), None),
              lambda _, g: (jnp.concatenate([jnp.concatenate([g[0], g[1]], axis=1),
                                             jnp.concatenate([g[2], g[3]], axis=1)], axis=0),))


@jax.custom_vjp
def _solve_saved(n, rhs, minv, u):
    return u


def _solve_saved_bwd(res, du):
    minv, u = res
    drhs = _dotf(minv, du, 0, 0)
    return _dotf(drhs, u, 1, 1), drhs, jnp.zeros_like(minv), jnp.zeros_like(u)


_solve_saved.defvjp(lambda n, rhs, minv, u: (u, (minv, u)), _solve_saved_bwd)


def scan_chunk(tri, strict, incl, m0, m1, eye, r, lw, k, v, a, b, s0, saved=None):
    lower = strict > 0
    lower_incl = incl > 0

    def stack(x):
        return jnp.concatenate([x * m0, x * m1], axis=0)

    def dots(xs, ys, ca, cb, mask=None):
        out = [_dotf(x, y, ca, cb) for x, y in zip(xs, ys)]
        return out if mask is None else [jnp.where(mask, o, 0.0) for o in out]

    cl = [_cumsum_rows(tri, x) for x in lw]
    gam = [jnp.exp(c) for c in cl]
    ginv = [jnp.exp(-c) for c in cl]
    ar_s = [jnp.concatenate([stack(x * jnp.exp(c - w)), stack(y * g)], axis=0)
            for x, c, w, y, g in zip(a, cl, lw, r, gam)]
    bk_s = [jnp.concatenate([stack(x * g), stack(y * g)], axis=0) for x, y, g in zip(b, k, ginv)]
    v_s = [stack(x) for x in v]
    quads = [_quads(x) for x in dots(ar_s, bk_s, 1, 1)]
    n_ab = [jnp.where(lower, q[0], 0.0) for q in quads]
    n_ak = [jnp.where(lower, q[1], 0.0) for q in quads]
    r_ab = [jnp.where(lower_incl, q[2], 0.0) for q in quads]
    r_ak = [jnp.where(lower_incl, q[3], 0.0) for q in quads]
    from_state = [_halves(x) for x in dots(ar_s, s0, 1, 1)]
    rhs = [x[0] + y for x, y in zip(from_state, dots(n_ak, v_s, 1, 0))]
    if saved is None:
        minv = [eye + n for n in n_ab]
        p = n_ab
        for _ in range(5):
            p = dots(p, p, 1, 0)
            minv = [m + mp for m, mp in zip(minv, dots(minv, p, 1, 0))]
        u_s = dots(minv, rhs, 1, 0)
    else:
        minv = saved[0]
        u_s = [_solve_saved(n, x, m, u) for n, x, m, u in zip(n_ab, rhs, *saved)]
    uv_s = [jnp.concatenate([x, y], axis=0) for x, y in zip(u_s, v_s)]
    r_uv = [jnp.concatenate([x, y], axis=1) for x, y in zip(r_ab, r_ak)]
    y = [_unstack2(x[1] + z) for x, z in zip(from_state, dots(r_uv, uv_s, 1, 0))]
    g_end = [_last_row(g) for g in gam]
    s1 = [s * g + x for s, g, x in zip(s0, g_end, dots(uv_s, [x * g for x, g in zip(bk_s, g_end)], 0, 0))]
    return y, s1, (minv, u_s)


SCAN_PAIRS = 8


def _scan_specs(consts, order):
    row = pl.BlockSpec((SCAN_T, PAIR * SCAN_PAIRS), lambda p, c: (order(c), p))
    state = pl.BlockSpec((None, SCAN_PAIRS, PAIR, PAIR), lambda p, c: (order(c), p, 0, 0))
    return row, state, [pl.BlockSpec(x.shape, _zero_map(x.ndim)) for x in consts]


def _pair_lanes(q):
    return slice(q * PAIR, (q + 1) * PAIR)


def scan_fwd(r, lw, k, v, a, b, shards=()):
    lp = r.shape[0]
    nch = lp // SCAN_T
    npair = D_MODEL // PAIR
    ng = len(shards)
    consts = _scan_consts()
    row, state, cspecs = _scan_specs(consts, lambda c: c)

    def body(tri, strict, incl, masks, eye, r_ref, lw_ref, k_ref, v_ref, a_ref, b_ref, *rest):
        src, (y_ref, s_ref, minv_ref, u_ref), dst = rest[:ng], rest[ng:ng + 4], rest[ng + 4:2 * ng + 4]
        carry = rest[2 * ng + 4]
        first = jnp.logical_and(pl.program_id(0) == 0, pl.program_id(1) == 0)
        last = jnp.logical_and(pl.program_id(0) == npair // SCAN_PAIRS - 1, pl.program_id(1) == nch - 1)
        if ng:
            sends, arrivals, forwards, forwarded = gather_copies(src, dst, *rest[2 * ng + 5:])

            @pl.when(first)
            def _():
                for cp in sends:
                    cp.start()

            @pl.when(jnp.logical_and(pl.program_id(0) == npair // SCAN_PAIRS - 1, pl.program_id(1) == nch * 3 // 4))
            def _():
                for landed, onward in zip(arrivals, forwards):
                    landed.wait_recv()
                    onward.start()

        @pl.when(pl.program_id(1) == 0)
        def _():
            carry[...] = jnp.zeros_like(carry)

        pairs = range(SCAN_PAIRS)
        s0 = [carry[q] for q in pairs]
        rows = [[ref[:, _pair_lanes(q)] for q in pairs] for ref in (r_ref, lw_ref, k_ref, v_ref, a_ref, b_ref)]
        y, s1, (minv, u) = scan_chunk(tri[...], strict[...], incl[...], masks[0:1, :], masks[1:2, :], eye[...],
                                      *rows, s0)
        for q in pairs:
            s_ref[q] = s0[q]
            minv_ref[q] = minv[q]
            u_ref[q] = u[q]
            y_ref[:, _pair_lanes(q)] = y[q]
            carry[q] = s1[q]

        if ng:
            @pl.when(last)
            def _():
                for cp in forwarded:
                    cp.wait_recv()
                for cp in sends + forwards:
                    cp.wait_send()

    mats = jax.ShapeDtypeStruct((nch, npair, PAIR, PAIR), F32)
    out = pl.pallas_call(
        body, name="rwkv_scan_fwd", grid=(npair // SCAN_PAIRS, nch), in_specs=cspecs + [row] * 6 + [ANY] * ng,
        out_specs=[row, state, state, state] + [ANY] * ng,
        out_shape=[jax.ShapeDtypeStruct((lp, D_MODEL), F32), mats, mats, mats] + gathered_shapes(shards),
        scratch_shapes=[pltpu.VMEM((SCAN_PAIRS, PAIR, PAIR), F32)] + (gather_scratch(ng) if ng else []),
        compiler_params=_params(),
    )(*consts, r, lw, k, v, a, b, *shards)
    return out[:4], fill_own(out[4:], shards)


def scan_bwd(r, lw, k, v, a, b, saved, dy, direct_grads, parts=()):
    lp = r.shape[0]
    nch = lp // SCAN_T
    npair = D_MODEL // PAIR
    consts = _scan_consts()
    row, state, cspecs = _scan_specs(consts, lambda c: nch - 1 - c)

    ng = len(parts)

    def body(tri, strict, incl, masks, eye, r_ref, lw_ref, k_ref, v_ref, a_ref, b_ref, s_ref, minv_ref, u_ref,
             dy_ref, dr_in, dk_in, dv_in, *rest):
        src, (dr_ref, dlw_ref, dk_ref, dv_ref, da_ref, db_ref), dst = rest[:ng], rest[ng:ng + 6], rest[ng + 6:2 * ng + 6]
        carry = rest[2 * ng + 6]
        first = jnp.logical_and(pl.program_id(0) == 0, pl.program_id(1) == 0)
        last = jnp.logical_and(pl.program_id(0) == npair // SCAN_PAIRS - 1, pl.program_id(1) == nch - 1)
        if ng:
            sends, arrivals = chip_exchange_copies(src, dst, *rest[2 * ng + 7:])

            @pl.when(first)
            def _():
                for cp in sends:
                    cp.start()

        @pl.when(pl.program_id(1) == 0)
        def _():
            carry[...] = jnp.zeros_like(carry)

        pairs = range(SCAN_PAIRS)
        kept = ([minv_ref[q] for q in pairs], [u_ref[q] for q in pairs])

        def fn(*args):
            y, s1, _ = scan_chunk(tri[...], strict[...], incl[...], masks[0:1, :], masks[1:2, :], eye[...], *args,
                                  saved=kept)
            return y, s1

        rows = [[ref[:, _pair_lanes(q)] for q in pairs] for ref in (r_ref, lw_ref, k_ref, v_ref, a_ref, b_ref)]
        _, vjp = jax.vjp(fn, *rows, [s_ref[q] for q in pairs])
        grads = vjp(([dy_ref[:, _pair_lanes(q)] for q in pairs], [carry[q] for q in pairs]))
        direct = (dr_in, None, dk_in, dv_in, None, None)
        for q in pairs:
            ln = _pair_lanes(q)
            for ref, g, extra in zip((dr_ref, dlw_ref, dk_ref, dv_ref, da_ref, db_ref), grads[:6], direct):
                ref[:, ln] = g[q] if extra is None else g[q] + extra[:, ln]
            carry[q] = grads[6][q]

        if ng:
            @pl.when(last)
            def _():
                for cp in arrivals:
                    cp.wait_recv()
                for cp in sends:
                    cp.wait_send()

    out = pl.pallas_call(
        body, name="rwkv_scan_bwd", grid=(npair // SCAN_PAIRS, nch),
        in_specs=cspecs + [row] * 6 + [state] * 3 + [row] * 4 + [ANY] * ng, out_specs=[row] * 6 + [ANY] * ng,
        out_shape=[jax.ShapeDtypeStruct((lp, D_MODEL), F32)] * 6 + [jax.ShapeDtypeStruct(p.shape, p.dtype) for p in parts],
        scratch_shapes=[pltpu.VMEM((SCAN_PAIRS, PAIR, PAIR), F32)] + (_sem_scratch(ng * len(XY_FLIPS)) if ng else []),
        compiler_params=_params(),
    )(*consts, r, lw, k, v, a, b, *saved, dy, *direct_grads, *parts)
    return out[:6], out[6:]


def _spread_matrices():
    rep = np.zeros((N_HEADS_KV, KV_DIM, KVW), np.float32)
    for h in range(N_HEADS_KV):
        for g in range(GROUP):
            rep[h, h * HEAD_DIM + np.arange(HEAD_DIM), g * HEAD_DIM + np.arange(HEAD_DIM)] = 1.0
    return jnp.asarray(rep, BF16)


KV_HEADS = range(N_HEADS_KV)


def _attn_common(n, q_ref, kp, kc, vp, vc, rep_ref, sink_ref):
    lane = lax.broadcasted_iota(jnp.int32, (1, KVW), 1)
    gmask = [(lane // HEAD_DIM == g).astype(F32) for g in range(GROUP)]
    kk = jnp.concatenate([kp, kc], axis=0)
    vv = jnp.concatenate([vp, vc], axis=0)
    qs = [q_ref[:, h * KVW:(h + 1) * KVW] for h in KV_HEADS]
    q_s = [jnp.concatenate([q * gmask[g] for g in range(GROUP)], axis=0) for q in qs]
    keys = [_dot(kk, rep_ref[h], 1, 0) for h in KV_HEADS]
    vals = [_dot(vv, rep_ref[h], 1, 0) for h in KV_HEADS]
    qi = lax.broadcasted_iota(jnp.int32, (GROUP * BLOCK, 2 * BLOCK), 0) % BLOCK
    kj = lax.broadcasted_iota(jnp.int32, (GROUP * BLOCK, 2 * BLOCK), 1)
    rel = BLOCK + qi - kj
    valid = (rel >= 0) & (rel < BLOCK) & ((n - 1) * BLOCK + kj >= PAD_FRONT)
    s = [jnp.where(valid, _dot(x, y, 1, 1) * (HEAD_DIM ** -0.5), -1e30) for x, y in zip(q_s, keys)]
    sink_col = [jnp.concatenate([jnp.broadcast_to(sink_ref[h, g:g + 1, 0:1], (BLOCK, 1)) for g in range(GROUP)],
                                axis=0) for h in KV_HEADS]
    m = [jnp.maximum(jnp.max(x, axis=-1, keepdims=True), c) for x, c in zip(s, sink_col)]
    ex = [jnp.exp(x - y) for x, y in zip(s, m)]
    ex_sink = [jnp.exp(c - y) for c, y in zip(sink_col, m)]
    inv = [1.0 / (jnp.sum(x, axis=-1, keepdims=True) + c) for x, c in zip(ex, ex_sink)]
    return (gmask, q_s, keys, vals, [x * y for x, y in zip(ex, inv)], [x * y for x, y in zip(ex_sink, inv)])


def _unstack_groups(x_s, gmask):
    out = None
    for g in range(GROUP):
        t = x_s[g * BLOCK:(g + 1) * BLOCK] * gmask[g]
        out = t if out is None else out + t
    return out


def _attn_specs():
    qspec = pl.BlockSpec((BLOCK, D_MODEL), lambda n: (n, 0))
    cur = pl.BlockSpec((BLOCK, KV_DIM), lambda n: (n, 0))
    prev = pl.BlockSpec((BLOCK, KV_DIM), lambda n: (jnp.maximum(n - 1, 0), 0))
    rep = pl.BlockSpec((N_HEADS_KV, KV_DIM, KVW), lambda n: (0, 0, 0))
    sink = pl.BlockSpec((N_HEADS_KV, 8, PAIR), lambda n: (0, 0, 0))
    return qspec, cur, prev, rep, sink


def _attn_params():
    return pltpu.CompilerParams(dimension_semantics=("arbitrary",), vmem_limit_bytes=VMEM_LIMIT)


def attn_fwd(q, k, v, sinks_b):
    lp = q.shape[0]
    qspec, cur, prev, rep, sink = _attn_specs()

    def body(q_ref, kp_ref, kc_ref, vp_ref, vc_ref, rep_ref, sink_ref, o_ref):
        gmask, _, _, vals, p, _ = _attn_common(pl.program_id(0), q_ref, kp_ref[...], kc_ref[...], vp_ref[...],
                                               vc_ref[...], rep_ref, sink_ref)
        o = [_dot(x, y, 1, 0) for x, y in zip(p, vals)]
        for h in KV_HEADS:
            o_ref[:, h * KVW:(h + 1) * KVW] = _unstack_groups(o[h], gmask)

    return pl.pallas_call(
        body, name="swa_fwd", grid=(lp // BLOCK,), in_specs=[qspec, prev, cur, prev, cur, rep, sink],
        out_specs=qspec, out_shape=jax.ShapeDtypeStruct((lp, D_MODEL), F32), compiler_params=_attn_params(),
    )(q, k, k, v, v, _spread_matrices(), sinks_b)


def attn_bwd(q, k, v, sinks_b, do):
    lp = q.shape[0]
    qspec, cur, prev, rep, sink = _attn_specs()

    def body(q_ref, kp_ref, kc_ref, vp_ref, vc_ref, rep_ref, sink_ref, do_ref, dq_ref, dkc_ref, dkp_ref, dvc_ref,
             dvp_ref, dsink_ref):
        n = pl.program_id(0)
        gmask, q_s, keys, vals, p, p_sink = _attn_common(n, q_ref, kp_ref[...], kc_ref[...], vp_ref[...], vc_ref[...],
                                                         rep_ref, sink_ref)
        do_s = [jnp.concatenate([do_ref[:, h * KVW:(h + 1) * KVW] * gmask[g] for g in range(GROUP)], axis=0)
                for h in KV_HEADS]
        dp = [_dot(x, y, 1, 1) for x, y in zip(do_s, vals)]
        delta = [jnp.sum(x * y, axis=-1, keepdims=True) for x, y in zip(p, dp)]
        ds = [x * (y - z) * (HEAD_DIM ** -0.5) for x, y, z in zip(p, dp, delta)]
        dq = [_dot(x, y, 1, 0) for x, y in zip(ds, keys)]
        dkeys_s = [_dot(x, y, 0, 0) for x, y in zip(ds, q_s)]
        dvals_s = [_dot(x, y, 0, 0) for x, y in zip(p, do_s)]
        dkeys = [_exact_dot(x, rep_ref[h], cb=1) for h, x in enumerate(dkeys_s)]
        dvals = [_exact_dot(x, rep_ref[h], cb=1) for h, x in enumerate(dvals_s)]
        dk_all = (dkeys[0] + dkeys[1]) + (dkeys[2] + dkeys[3])
        dv_all = (dvals[0] + dvals[1]) + (dvals[2] + dvals[3])
        dkp_ref[...] = dk_all[:BLOCK]
        dkc_ref[...] = dk_all[BLOCK:]
        dvp_ref[...] = dv_all[:BLOCK]
        dvc_ref[...] = dv_all[BLOCK:]
        dsinks = []
        for h in KV_HEADS:
            dq_ref[:, h * KVW:(h + 1) * KVW] = _unstack_groups(dq[h], gmask)
            dsk = -(p_sink[h] * delta[h])
            rows = [jnp.broadcast_to(jnp.sum(dsk[g * BLOCK:(g + 1) * BLOCK], axis=0, keepdims=True), (1, PAIR))
                    for g in range(GROUP)]
            dsinks.append(jnp.concatenate(rows + [jnp.zeros((8 - GROUP, PAIR), F32)], axis=0))

        @pl.when(n == 0)
        def _():
            for h in KV_HEADS:
                dsink_ref[h] = dsinks[h]

        @pl.when(n > 0)
        def _():
            for h in KV_HEADS:
                dsink_ref[h] += dsinks[h]

    kv = jax.ShapeDtypeStruct((lp, KV_DIM), F32)
    return pl.pallas_call(
        body, name="swa_bwd", grid=(lp // BLOCK,), in_specs=[qspec, prev, cur, prev, cur, rep, sink, qspec],
        out_specs=[qspec, cur, cur, cur, cur, sink],
        out_shape=[jax.ShapeDtypeStruct((lp, D_MODEL), F32), kv, kv, kv, kv,
                   jax.ShapeDtypeStruct((N_HEADS_KV, 8, PAIR), F32)],
        compiler_params=_attn_params(),
    )(q, k, k, v, v, _spread_matrices(), sinks_b, do)


def _pick_tm(lp, want):
    for tm in (384, 192, 128, 64):
        if tm <= want and lp % tm == 0:
            return tm
    raise ValueError(lp)


def _acc(shape):
    return (tuple(shape), F32)


def _ff_one(w):
    return (w, (None, D_MODEL, D_MODEL), lambda c, i: (c, 0, 0))


def _mlp_layer_fwd(name, h, wup, wdown, lg, lb, tm):
    def fn(c, i, h, wup, wdown, lg, lb):
        out, pre = None, []
        for s in range(N_FF_CHUNK):
            u = mm(h, wup[s])
            pre.append(u.astype(BF16))
            t = mm(sq_relu(u), wdown[s])
            out = t if out is None else out + t
        z = ALPHA * h + out
        return (_layer_norm(z, lg, lb), z, jnp.stack(pre)), ()

    (h_out, z, pre), _ = rowwise(name, fn, [h], [wup, wdown, lg, lb],
                                 [(D_MODEL, F32), (D_MODEL, F32), (D_MODEL, BF16, False, N_FF_CHUNK)], [], tm)
    return h_out, z, pre


MLP_BWD_TILE = 528


def _mlp_layer_bwd(name, h_in, z, pre, dh_parts, wup, wdown, lg, lb, tm):
    n_parts = len(dh_parts)

    def fn_ln(c, i, z, *rest):
        dh = rest[0]
        for extra in rest[1:n_parts]:
            dh = dh + extra
        _, vjp = jax.vjp(_layer_norm, z, rest[n_parts], rest[n_parts + 1])
        dz, dlg, dlb = vjp(dh)
        return (dz,), (dlg, dlb)

    (dz,), (dlg, dlb) = rowwise(name + "_ln", fn_ln, [z] + list(dh_parts), [lg, lb], [(D_MODEL, F32)],
                                [_acc((1, D_MODEL)), _acc((1, D_MODEL))], tm)

    def fn_mlp(c, i, h, dz, wup, wdown, u):
        r = jnp.maximum(u.astype(F32), 0.0)
        du = _dot(dz, wdown, 1, 1) * (2.0 * r)
        return (_dot(du, wup, 1, 1),), (_dot(h, du, 0, 0), _dot(r * r, dz, 0, 0))

    aspec = ((N_FF_CHUNK, D_MODEL, D_MODEL), F32, (None, D_MODEL, D_MODEL), lambda c, i: (c, 0, 0))
    lp = h_in.shape[0]
    tile = MLP_BWD_TILE if lp % MLP_BWD_TILE == 0 else tm
    pre_chunk = (pre, (None, tile, D_MODEL), lambda c, i: (c, i, 0))
    (dx,), (dwup, dwdown) = rowwise(name + "_mm", fn_mlp, [h_in, dz], [_ff_one(wup), _ff_one(wdown), pre_chunk],
                                    [(D_MODEL, F32, True)], [aspec, aspec], tile, nc=N_FF_CHUNK)
    return dz, dx, dwup, dwdown, dlg, dlb


def _sum_parts(dz, dx):
    out = ALPHA * dz
    for s in range(N_FF_CHUNK):
        out = out + dx[s]
    return out


def local_step(x, loss_target, p, late=None, early_hook=None):
    seq = x.shape[0]
    lp = TOK0 + seq
    tm = _pick_tm(lp, 384)
    tms = _pick_tm(lp, 192)
    e, et = _head_matrices()
    h0 = jnp.concatenate([jnp.zeros((PAD_FRONT, D_MODEL), F32), p["meta_tokens"], x], axis=0)
    pos = jnp.maximum(jnp.arange(lp, dtype=F32) - PAD_FRONT, 0.0)
    inv_freq = 1.0 / (ROPE_THETA ** (jnp.arange(0, HEAD_DIM, 2, dtype=F32) / HEAD_DIM))
    ang = pos[:, None] * inv_freq[None, :]
    cos = jnp.tile(jnp.cos(ang), (1, PAIR // (HEAD_DIM // 2)))
    sin = jnp.tile(jnp.sin(ang), (1, PAIR // (HEAD_DIM // 2)))

    pre_vec = [p["a_mu"][j:j + 1] for j in range(6)] + [p["a_w0"], p["a_a0"], p["a_k_k"], p["a_k_a"]]
    pre_w = [p["a_w_r"], p["a_w_k"], p["a_w_v"], p["a_w1"], p["a_w2"], p["a_a1"], p["a_a2"], p["a_g1"], p["a_g2"]]
    n_vec = len(pre_vec)

    def fn_pre(c, i, h, before, e, et, *ws):
        return rwkv_pre(e, et, ws[n_vec:], None, h, _shift_down(h, before, i), *ws[:n_vec])[0], ()

    (r, lw, k2, v, an, bn, g), _, *pre_gathered = rowwise(
        "rwkv_pre", fn_pre, [h0, _halo_before(h0, tms)], [e, et] + pre_vec + pre_w, [(D_MODEL, F32)] * 7, [], tms,
        hosted=hosted_gather(late[0][0]) if late else None)
    (y, *scan_saved), scan_gathered = scan_fwd(r, lw, k2, v, an, bn, late[1][0] if late else ())
    if late:
        p = {**p, **late[0][1](pre_gathered[0]), **late[1][1](scan_gathered)}

    post_c = [p["a_w_o"], p["a_gn_w"], p["a_gn_b"], p["a_r_k"], p["ln_g00"], p["ln_b00"]]

    def fn_post(c, i, y, r, k2, v, g, h0, e, et, w_o, *vecs):
        return (rwkv_post(e, et, w_o, None, y, r, k2, v, g, h0, *vecs)[0],), ()

    (h1,), _ = rowwise("rwkv_post", fn_post, [y, r, k2, v, g, h0], [e, et] + post_c, [(D_MODEL, F32)], [], tm)
    h2, z2, pre2 = _mlp_layer_fwd("mlp0_fwd", h1, p["mlp_up0"], p["mlp_down0"], p["ln_g01"], p["ln_b01"], tm)

    qkv_w = [p["b_w_q"], p["kv_w_k"], p["kv_w_v"]]

    def fn_qkv(c, i, h, cos, sin, wq, wk, wv):
        return qkv_proj(cos, sin, wq, wk, wv, None, h)[0], ()

    (q, k, vv), _ = rowwise("qkv_proj", fn_qkv, [h2, cos, sin], qkv_w,
                            [(D_MODEL, F32), (KV_DIM, F32), (KV_DIM, F32)], [], tm)
    sinks_b = jnp.broadcast_to(p["b_sinks"].reshape(N_HEADS_KV, GROUP, 1), (N_HEADS_KV, GROUP, PAIR))
    sinks_b = jnp.concatenate([sinks_b, jnp.zeros((N_HEADS_KV, 8 - GROUP, PAIR), F32)], axis=1)
    o = attn_fwd(q, k, vv, sinks_b)

    ao_c = [p["b_w_o"], p["ln_g10"], p["ln_b10"]]

    def fn_ao(c, i, o, h, w_o, lg, lb):
        return (attn_out(w_o, None, o, h, lg, lb)[0],), ()

    (h3,), _ = rowwise("attn_out", fn_ao, [o, h2], ao_c, [(D_MODEL, F32)], [], tm)
    h4, z4, pre4 = _mlp_layer_fwd("mlp1_fwd", h3, p["mlp_up1"], p["mlp_down1"], p["ln_g11"], p["ln_b11"], tm)

    def fn_loss(c, i, h4, tgt):
        real = (_row_ids(i, TOK0) >= TOK0).astype(F32)
        err = (h4 - tgt) * real
        part = 0.5 * jnp.sum(jnp.sum(err * err, axis=-1, keepdims=True), axis=0, keepdims=True) / D_MODEL
        return (err * (1.0 / D_MODEL),), (jnp.broadcast_to(part, (8, PAIR)),)

    (dh4,), (loss_acc,) = rowwise("loss", fn_loss, [h4, (loss_target, TOK0, lambda i: jnp.maximum(i - 1, 0))], [],
                                  [(D_MODEL, F32)], [_acc((8, PAIR))], TOK0)
    loss = loss_acc[0, 0]

    grads = {}
    dz4, dx4, grads["mlp_up1"], grads["mlp_down1"], grads["ln_g11"], grads["ln_b11"] = _mlp_layer_bwd(
        "mlp1_bwd", h3, z4, pre4, [dh4], p["mlp_up1"], p["mlp_down1"], p["ln_g11"], p["ln_b11"], tm)

    def fn_ao_b(c, i, dz, dx, o, h, w_o, lg, lb):
        (do, dh, dlg, dlb), (dw_o,) = vjp_taps(functools.partial(attn_out, w_o), [(tm, D_MODEL)], [o, h, lg, lb],
                                               _sum_parts(dz, dx))
        return (do, dh), (dw_o, dlg, dlb)

    (do, dh2_a), (grads["b_w_o"], grads["ln_g10"], grads["ln_b10"]) = rowwise(
        "attn_out_bwd", fn_ao_b, [dz4, dx4, o, h2], ao_c, [(D_MODEL, F32)] * 2,
        [_acc((D_MODEL, D_MODEL)), _acc((1, D_MODEL)), _acc((1, D_MODEL))], tm)

    dq, dkc, dkp, dvc, dvp, dsinks = attn_bwd(q, k, vv, sinks_b, do)
    grads["b_sinks"] = dsinks[:, :GROUP, 0].reshape(1, N_HEADS)
    zblk = jnp.zeros((BLOCK, KV_DIM), F32)
    dkp_s = jnp.concatenate([dkp[BLOCK:], zblk], axis=0)
    dvp_s = jnp.concatenate([dvp[BLOCK:], zblk], axis=0)

    def fn_qkv_b(c, i, h, cos, sin, dq, dkc, dkp, dvc, dvp, wq, wk, wv):
        return vjp_taps(functools.partial(qkv_proj, cos, sin, wq, wk, wv),
                        [(tm, D_MODEL), (tm, KV_DIM), (tm, KV_DIM)], [h], (dq, dkc + dkp, dvc + dvp))

    (dh2_q,), (grads["b_w_q"], grads["kv_w_k"], grads["kv_w_v"]) = rowwise(
        "qkv_proj_bwd", fn_qkv_b, [h2, cos, sin, dq, dkc, dkp_s, dvc, dvp_s], qkv_w, [(D_MODEL, F32)],
        [_acc((D_MODEL, D_MODEL)), _acc((D_MODEL, KV_DIM)), _acc((D_MODEL, KV_DIM))], tm)

    dz2, dx2, grads["mlp_up0"], grads["mlp_down0"], grads["ln_g01"], grads["ln_b01"] = _mlp_layer_bwd(
        "mlp0_bwd", h1, z2, pre2, [dh2_a, dh2_q], p["mlp_up0"], p["mlp_down0"], p["ln_g01"], p["ln_b01"], tm)

    def fn_post_b(c, i, dz, dx, y, r, k2, v, g, h0, e, et, w_o, *vecs):
        out, dws = vjp_taps(functools.partial(rwkv_post, e, et, w_o), [(tms, D_MODEL)],
                            [y, r, k2, v, g, h0] + list(vecs), _sum_parts(dz, dx))
        return out[:6], tuple(dws) + tuple(out[6:])

    early_srcs = early_hook[0](grads) if early_hook else ()
    (dy, dr_c, dk_c, dv_c, dg, dh0_c), post_g, *early_got = rowwise(
        "rwkv_post_bwd", fn_post_b, [dz2, dx2, y, r, k2, v, g, h0], [e, et] + post_c, [(D_MODEL, F32)] * 6,
        [_acc((D_MODEL, D_MODEL))] + [_acc((1, D_MODEL))] * 5, tms,
        hosted=hosted_pair_exchange(early_srcs) if early_hook else None)
    for name, val in zip(["a_w_o", "a_gn_w", "a_gn_b", "a_r_k", "ln_g00", "ln_b00"], post_g):
        grads[name] = val

    (dr, dlw, dk2, dv, dan, dbn), early_from_chips = scan_bwd(
        r, lw, k2, v, an, bn, scan_saved, dy, (dr_c, dk_c, dv_c),
        early_hook[1](early_srcs, early_got[0]) if early_hook else ())

    def fn_pre_b(c, i, h, before, dr, dlw, dk2, dv, dan, dbn, dg, e, et, *ws):
        hp = _shift_down(h, before, i)
        real = (_row_ids(i, tms) >= PAD_FRONT).astype(F32)
        cot = tuple(t * real for t in (dr, dlw, dk2, dv, dan, dbn, dg))
        out, dws = vjp_taps(functools.partial(rwkv_pre, e, et, ws[n_vec:]), [(tms, n) for n in PRE_TAPS],
                            [h, hp] + list(ws[:n_vec]), cot)
        return out[:2], tuple(out[2:]) + tuple(dws)

    (dh0_p, dhp), pre_g = rowwise(
        "rwkv_pre_bwd", fn_pre_b, [h0, _halo_before(h0, tms), dr, dlw, dk2, dv, dan, dbn, dg],
        [e, et] + pre_vec + pre_w, [(D_MODEL, F32)] * 2,
        [_acc((1, D_MODEL))] * n_vec + [_acc(w.shape) for w in pre_w], tms)
    grads["a_mu"] = jnp.concatenate(pre_g[:6], axis=0)
    for name, val in zip(["a_w0", "a_a0", "a_k_k", "a_k_a", "a_w_r", "a_w_k", "a_w_v", "a_w1", "a_w2", "a_a1",
                          "a_a2", "a_g1", "a_g2"], pre_g[6:]):
        grads[name] = val

    def fn_add(c, i, a, b, d, after):
        return (a + b + _shift_up(d, after, i, lp // tm),), ()

    (dh0,), _ = rowwise("grad_h0", fn_add, [dh0_c, dh0_p, dhp, _halo_after(dhp, tm)], [], [(D_MODEL, F32)], [], tm)
    grads["meta_tokens"] = dh0[PAD_FRONT:TOK0]
    return loss, dh0[TOK0:], grads, early_from_chips


ANY = pl.BlockSpec(memory_space=pl.ANY)
XY_FLIPS = ((0, 1), (1, 0), (1, 1))


def _flip(v, bit):
    return 1 - v if bit else v


def _sem_scratch(n):
    return [pltpu.SemaphoreType.DMA((n,)), pltpu.SemaphoreType.DMA((n,))]


def gather_copies(src, dst, ici_send, ici_recv, d2d_send, d2d_recv):
    npeer = len(XY_FLIPS)
    x, y, c = lax.axis_index("x"), lax.axis_index("y"), lax.axis_index("c")

    def half(ref, k, which):
        h = src[k].shape[0] // 2
        start = which * h
        return ref.at[pl.ds(pl.multiple_of(start, 8) if h % 8 == 0 else start, h)]

    def ici(k, j, slot):
        fx, fy = XY_FLIPS[j]
        return pltpu.make_async_remote_copy(
            src_ref=half(src[k], k, c), dst_ref=half(dst[k].at[slot], k, c), send_sem=ici_send.at[k * npeer + j],
            recv_sem=ici_recv.at[k * npeer + j], device_id=(_flip(x, fx), _flip(y, fy), c), device_id_type=MESH)

    def d2d(k, j, which):
        fx, fy = XY_FLIPS[j]
        landed = half(dst[k].at[2 * _flip(x, fx) + _flip(y, fy)], k, which)
        return pltpu.make_async_remote_copy(
            src_ref=landed, dst_ref=landed, send_sem=d2d_send.at[k * npeer + j], recv_sem=d2d_recv.at[k * npeer + j],
            device_id=(x, y, 1 - c), device_id_type=MESH)

    pairs = [(k, j) for k in range(len(src)) for j in range(npeer)]
    return ([ici(k, j, 2 * x + y) for k, j in pairs],
            [ici(k, j, 2 * _flip(x, XY_FLIPS[j][0]) + _flip(y, XY_FLIPS[j][1])) for k, j in pairs],
            [d2d(k, j, c) for k, j in pairs], [d2d(k, j, 1 - c) for k, j in pairs])


def gather_scratch(n):
    return _sem_scratch(n * len(XY_FLIPS)) * 2


def gathered_shapes(shards):
    return [jax.ShapeDtypeStruct((N_SHARD,) + s.shape, s.dtype) for s in shards]


def fill_own(gathered, shards):
    if not shards:
        return []
    slot = 2 * lax.axis_index("x") + lax.axis_index("y")
    return [lax.dynamic_update_index_in_dim(g, s, slot, 0) for g, s in zip(gathered, shards)]


def all_gather_shards(shards):
    n = len(shards)

    def body(*refs):
        sends, arrivals, forwards, forwarded = gather_copies(refs[:n], refs[n:2 * n], *refs[2 * n:])
        for cp in sends:
            cp.start()
        for landed, onward in zip(arrivals, forwards):
            landed.wait_recv()
            onward.start()
        for cp in forwarded:
            cp.wait_recv()
        for cp in sends + forwards:
            cp.wait_send()

    out = pl.pallas_call(body, name="gather_weights", in_specs=[ANY] * n, out_specs=[ANY] * n,
                         out_shape=gathered_shapes(shards), scratch_shapes=gather_scratch(n))(*shards)
    return fill_own(out, shards)


def placement():
    x, y, c = lax.axis_index("x"), lax.axis_index("y"), lax.axis_index("c")
    me = 2 * x + y
    others = [j + (j >= me).astype(jnp.int32) for j in range(N_SHARD - 1)]
    return jnp.stack([c, me] + others).astype(jnp.int32)


def hosted_gather(shards):
    return (gather_copies, list(shards), gathered_shapes(shards), gather_scratch(len(shards)),
            lambda got: fill_own(got, shards))


def pair_exchange_copies(src, got, send_sems, recv_sems):
    x, y, c = lax.axis_index("x"), lax.axis_index("y"), lax.axis_index("c")

    def copy(k):
        half = src[k].shape[1] // 2
        theirs = src[k].at[:, pl.ds(pl.multiple_of((1 - c) * half, 8), half), :]
        return pltpu.make_async_remote_copy(
            src_ref=theirs, dst_ref=got[k], send_sem=send_sems.at[k], recv_sem=recv_sems.at[k],
            device_id=(x, y, 1 - c), device_id_type=MESH)

    sends = [copy(k) for k in range(len(src))]
    return sends, sends, [], []


def _half_shapes(sources):
    return [jax.ShapeDtypeStruct((s.shape[0], s.shape[1] // 2, s.shape[2]), s.dtype) for s in sources]


def hosted_pair_exchange(sources):
    return (pair_exchange_copies, list(sources), _half_shapes(sources), _sem_scratch(len(sources)), list)


def pair_exchange(name, sources):
    n = len(sources)

    def body(*refs):
        sends, arrivals, _, _ = pair_exchange_copies(refs[:n], refs[n:2 * n], *refs[2 * n:])
        for cp in sends:
            cp.start()
        for cp in arrivals:
            cp.wait_recv()
        for cp in sends:
            cp.wait_send()

    halves = _half_shapes(sources)
    return pl.pallas_call(body, name=name, in_specs=[ANY] * n, out_specs=[ANY] * n,
                          out_shape=halves, scratch_shapes=_sem_scratch(n))(*sources)


def chip_exchange(parts):
    n = len(parts)

    def body(*refs):
        sends, arrivals = chip_exchange_copies(refs[:n], refs[n:2 * n], *refs[2 * n:])
        for cp in sends:
            cp.start()
        for cp in arrivals:
            cp.wait_recv()
        for cp in sends:
            cp.wait_send()

    return pl.pallas_call(
        body, name="grads_chip_exchange", in_specs=[ANY] * n, out_specs=[ANY] * n,
        out_shape=[jax.ShapeDtypeStruct(p.shape, p.dtype) for p in parts],
        scratch_shapes=_sem_scratch(n * len(XY_FLIPS)),
    )(*parts)


def chip_exchange_copies(src, dst, send_sems, recv_sems):
    npeer = len(XY_FLIPS)
    x, y, c = lax.axis_index("x"), lax.axis_index("y"), lax.axis_index("c")
    me = 2 * x + y

    def copy(k, j, sending):
        fx, fy = XY_FLIPS[j]
        px, py = _flip(x, fx), _flip(y, fy)
        peer = 2 * px + py
        return pltpu.make_async_remote_copy(
            src_ref=src[k].at[peer], dst_ref=dst[k].at[me if sending else peer],
            send_sem=send_sems.at[k * npeer + j], recv_sem=recv_sems.at[k * npeer + j],
            device_id=(px, py, c), device_id_type=MESH)

    pairs = [(k, j) for k in range(len(src)) for j in range(npeer)]
    return [copy(k, j, True) for k, j in pairs], [copy(k, j, False) for k, j in pairs]


def sibling_share(halves):
    n = len(halves)

    def body(*refs):
        src, got = refs[:n], refs[n:2 * n]
        send_sems, recv_sems = refs[2 * n:]
        x, y, c = lax.axis_index("x"), lax.axis_index("y"), lax.axis_index("c")
        sends = [pltpu.make_async_remote_copy(
            src_ref=src[k], dst_ref=got[k], send_sem=send_sems.at[k], recv_sem=recv_sems.at[k],
            device_id=(x, y, 1 - c), device_id_type=MESH) for k in range(n)]
        for cp in sends:
            cp.start()
        for cp in sends:
            cp.wait_recv()
        for cp in sends:
            cp.wait_send()

    return pl.pallas_call(
        body, name="grads_sibling_share", in_specs=[ANY] * n, out_specs=[ANY] * n,
        out_shape=[jax.ShapeDtypeStruct(h.shape, h.dtype) for h in halves], scratch_shapes=_sem_scratch(n),
    )(*halves)


ADD_TILE_ELEMS = 512 * 1024


def _row_tile(rows, cols):
    return max(t for t in range(8, rows + 1, 8) if rows % t == 0 and t * cols <= ADD_TILE_ELEMS)


def _prefetch_call(body, name, place, grid, in_specs, out_specs, out_shape, args):
    return pl.pallas_call(
        body, name=name, out_shape=out_shape,
        grid_spec=pltpu.PrefetchScalarGridSpec(num_scalar_prefetch=1, grid=grid, in_specs=in_specs,
                                               out_specs=out_specs),
        compiler_params=pltpu.CompilerParams(dimension_semantics=("arbitrary",) * len(grid),
                                             vmem_limit_bytes=VMEM_LIMIT),
    )(place, *args)


def pair_add(name, place, src, got, dtype):
    n4, half, cols = got.shape
    tile = _row_tile(half, cols)
    nt = half // tile

    def body(pr, a_ref, b_ref, o_ref):
        o_ref[...] = (a_ref[...] + b_ref[...]).astype(o_ref.dtype)

    mine = pl.BlockSpec((None, tile, cols), lambda s, i, pr: (s, pr[0] * nt + i, 0))
    blk = pl.BlockSpec((None, tile, cols), lambda s, i, pr: (s, i, 0))
    return _prefetch_call(body, name, place, (n4, nt), [mine, blk], blk,
                          jax.ShapeDtypeStruct(got.shape, dtype), (src, got))


def chip_add(name, place, part, from_chips):
    _, half, cols = part.shape
    tile = _row_tile(half, cols)

    def body(pr, own_ref, r0_ref, r1_ref, r2_ref, o_ref):
        me = pr[1]
        own, r0, r1, r2 = (r[...].astype(F32) for r in (own_ref, r0_ref, r1_ref, r2_ref))
        t0 = jnp.where(me == 0, own, r0)
        t1 = jnp.where(me == 0, r0, jnp.where(me == 1, own, r1))
        t2 = jnp.where(me <= 1, r1, jnp.where(me == 2, own, r2))
        t3 = jnp.where(me == 3, own, r2)
        o_ref[...] = ((t0 + t1) + t2) + t3

    def slab(j):
        return pl.BlockSpec((None, tile, cols), lambda i, pr: (pr[j], i, 0))

    return _prefetch_call(body, name, place, (half // tile,), [slab(1), slab(2), slab(3), slab(4)],
                          pl.BlockSpec((tile, cols), lambda i, pr: (i, 0)),
                          jax.ShapeDtypeStruct((half, cols), F32), (part, from_chips, from_chips, from_chips))


def pair_adds(tag, place, sources, got, narrow):
    return [pair_add(f"grads_pair_add_{tag}{k}", place, s, g, BF16 if nar else F32)
            for k, (s, g, nar) in enumerate(zip(sources, got, narrow))]


def finish_sums(place, parts, from_chips):
    halves = [chip_add(f"grads_chip_add{k}", place, p, f) for k, (p, f) in enumerate(zip(parts, from_chips))]
    return list(zip(halves, sibling_share(halves)))


ADAM_ROWS = 256


def adamw_update(name, place, halves, w, m, v):
    nsub, rows, cols = w.shape
    half = rows // 2
    tr = ADAM_ROWS if half % ADAM_ROWS == 0 else half
    nth = half // tr

    def body(pr, *refs):
        g_refs, (w_ref, m_ref, v_ref, g_ref, d_ref, nm_ref, nv_ref) = refs[:2 * nsub], refs[2 * nsub:]
        l = pl.program_id(0)
        mine = (pl.program_id(1) // nth) == pr[0]
        g = None
        for s in range(nsub):
            gs = jnp.where(mine, g_refs[2 * s][...], g_refs[2 * s + 1][...])
            g = gs if g is None else jnp.where(l == s, gs, g)
        m2 = ADAM_B1 * m_ref[...] + (1.0 - ADAM_B1) * g
        v2 = ADAM_B2 * v_ref[...] + (1.0 - ADAM_B2) * (g * g)
        m_hat = m2 / (1.0 - ADAM_B1 ** ADAM_STEP)
        v_hat = v2 / (1.0 - ADAM_B2 ** ADAM_STEP)
        g_ref[...] = g
        d_ref[...] = -ADAM_LR * (m_hat / (jnp.sqrt(v_hat) + ADAM_EPS) + ADAM_WD * w_ref[...])
        nm_ref[...] = m2
        nv_ref[...] = v2

    gblk = pl.BlockSpec((tr, cols), lambda l, i, pr: (i % nth, 0))
    blk = pl.BlockSpec((None, tr, cols), lambda l, i, pr: (l, i, 0))
    out = jax.ShapeDtypeStruct((nsub, rows, cols), F32)
    return _prefetch_call(body, name, place, (nsub, rows // tr), [gblk] * (2 * nsub) + [blk] * 3, [blk] * 4,
                          [out] * 4, [h for pair in halves for h in pair] + [w, m, v])


WEIGHT_NAMES = ("meta_tokens", "a_mu", "a_w_r", "a_w_k", "a_w_v", "a_w_o", "a_w0", "a_w1", "a_w2", "a_a0", "a_a1",
                "a_a2", "a_g1", "a_g2", "a_k_k", "a_k_a", "a_r_k", "a_gn_w", "a_gn_b", "kv_w_k", "kv_w_v", "b_w_q",
                "b_sinks", "b_w_o", "mlp_w_up", "mlp_w_down", "ln_g", "ln_b")
BIG_NAMES = ("a_w_r", "a_w_k", "a_w_v", "a_w_o", "b_w_q", "b_w_o")
EARLY_NAMES, LATE_NAMES = BIG_NAMES[:3], BIG_NAMES[3:]
PACK_MATS = (("kv_w_k", 256), ("kv_w_v", 256), ("a_w1", 64), ("a_a1", 64), ("a_g1", 128), ("a_w2", 64),
             ("a_a2", 64), ("a_g2", 128))
COLUMN_CUT = ("a_w2", "a_a2", "a_g2")
PACK_VECS = (("a_mu", 6), ("a_w0", 1), ("a_a0", 1), ("a_k_k", 1), ("a_k_a", 1), ("a_gn_w", 1), ("a_gn_b", 1),
             ("ln_g", 4), ("ln_b", 4), ("meta_tokens", 16))
PACK_REPL = (("a_r_k", 4), ("b_sinks", 1))
SHARD_W = D_MODEL // N_SHARD


def _tiles(rows):
    return -(-rows // SUBLANES) * SUBLANES


N_MAT_ROWS = sum(_tiles(r) for _, r in PACK_MATS)
N_VEC_ROWS = sum(_tiles(r) for _, r in PACK_VECS)
N_PACK_ROWS = -(-(N_MAT_ROWS + N_VEC_ROWS + sum(_tiles(r) for _, r in PACK_REPL)) // 16) * 16
N_GATHER_VEC_ROWS = -(-N_VEC_ROWS // 16) * 16


def _pad_rows(arr, axis):
    rows = arr.shape[axis]
    pad = [(0, 0)] * arr.ndim
    pad[axis] = (0, _tiles(rows) - rows)
    return jnp.pad(arr, pad) if _tiles(rows) != rows else arr


def _pack_rows(arr):
    if arr.size == N_HEADS:
        arr = jnp.pad(arr.reshape(1, N_HEADS), ((0, 0), (0, SHARD_W - N_HEADS)))
    return _pad_rows(arr.reshape(-1, SHARD_W), 0)


def pack_small(get):
    parts = [_pack_rows(get(name)) for name, _ in PACK_MATS + PACK_VECS + PACK_REPL]
    used = sum(p.shape[0] for p in parts)
    return jnp.concatenate(parts + [jnp.zeros((N_PACK_ROWS - used, SHARD_W), F32)], axis=0)


def unpack_small(pack, shapes):
    out, off = {}, 0
    for name, rows in PACK_MATS + PACK_VECS + PACK_REPL:
        piece = pack[off:off + rows]
        off += _tiles(rows)
        out[name] = piece[:, :N_HEADS].reshape(shapes[name]) if name == "b_sinks" else piece.reshape(shapes[name])
    return out


def whole_weights(big_names, gathered_big, mats, vecs, a_r_k, b_sinks):
    p = {name: g.reshape(D_MODEL, D_MODEL) for name, g in zip(big_names, gathered_big)}
    off = 0
    for name, rows in PACK_MATS:
        piece = mats[:, off:off + rows]
        off += rows
        if name in COLUMN_CUT:
            p[name] = piece.transpose(1, 0, 2).reshape(rows, D_MODEL)
        else:
            p[name] = piece.reshape(D_MODEL, rows)
    v = vecs.transpose(1, 0, 2).reshape(-1, D_MODEL)
    off = 0
    for name, rows in PACK_VECS:
        p[name] = v[off:off + rows]
        off += _tiles(rows)
    for i in range(2):
        for j in range(2):
            p[f"ln_g{i}{j}"] = p["ln_g"][2 * i + j:2 * i + j + 1]
            p[f"ln_b{i}{j}"] = p["ln_b"][2 * i + j:2 * i + j + 1]
    p["a_r_k"] = a_r_k.reshape(1, D_MODEL)
    p["b_sinks"] = b_sinks
    return p


def small_grad_pack(g):
    parts = []
    for name, rows in PACK_MATS:
        if name in COLUMN_CUT:
            parts.append(g[name].reshape(rows, N_SHARD, SHARD_W).transpose(1, 0, 2))
        else:
            parts.append(g[name].reshape(N_SHARD, rows, SHARD_W))
    vecs = {n: g[n] for n in ("a_mu", "a_w0", "a_a0", "a_k_k", "a_k_a", "a_gn_w", "a_gn_b", "meta_tokens")}
    vecs["ln_g"] = jnp.concatenate([g[f"ln_g{i}{j}"] for i in range(2) for j in range(2)], axis=0)
    vecs["ln_b"] = jnp.concatenate([g[f"ln_b{i}{j}"] for i in range(2) for j in range(2)], axis=0)
    for name, rows in PACK_VECS:
        parts.append(_pad_rows(vecs[name].reshape(rows, N_SHARD, SHARD_W).transpose(1, 0, 2), 1))
    r_k = jnp.broadcast_to(g["a_r_k"].reshape(1, -1, SHARD_W), (N_SHARD, D_MODEL // SHARD_W, SHARD_W))
    sinks = jnp.pad(g["b_sinks"].reshape(1, 1, N_HEADS), ((0, 0), (0, 0), (0, SHARD_W - N_HEADS)))
    parts += [_pad_rows(r_k, 1), _pad_rows(jnp.broadcast_to(sinks, (N_SHARD, 1, SHARD_W)), 1)]
    used = sum(p.shape[1] for p in parts)
    parts.append(jnp.zeros((N_SHARD, N_PACK_ROWS - used, SHARD_W), F32))
    return jnp.concatenate(parts, axis=1)


def train_step(vals):
    w = {n: vals[n] for n in WEIGHT_NAMES}
    w_pack = pack_small(lambda n: w[n])
    early = [w[n][0].astype(BF16) for n in EARLY_NAMES]
    early += [w_pack[:N_MAT_ROWS].astype(BF16), w_pack[N_MAT_ROWS:N_MAT_ROWS + N_GATHER_VEC_ROWS]]
    gathered = all_gather_shards(early)
    ne = len(EARLY_NAMES)
    p = whole_weights(EARLY_NAMES, gathered[:ne], gathered[ne], gathered[ne + 1][:, :N_VEC_ROWS], w["a_r_k"],
                      w["b_sinks"])
    nb = len(BIG_NAMES)

    def late_set(big, layer):
        shards = [w[n][0].astype(BF16) for n in big]
        shards += [w["mlp_w_up"][layer].astype(BF16), w["mlp_w_down"][layer].astype(BF16)]

        def weights(got):
            out = {n: x.reshape(D_MODEL, D_MODEL) for n, x in zip(big, got)}
            out[f"mlp_up{layer}"], out[f"mlp_down{layer}"] = got[len(big):]
            return out

        return shards, weights

    late = (late_set((), 1), late_set(LATE_NAMES, 0))

    place = placement()
    ready = {}
    a_names, b_names = BIG_NAMES[:4], BIG_NAMES[4:]

    def early_sources(g):
        return ([g[n].reshape(N_SHARD, SHARD_W, D_MODEL) for n in b_names]
                + [g["mlp_up0"], g["mlp_up1"], g["mlp_down0"], g["mlp_down1"]])

    def early_parts(srcs, got):
        ready["parts"] = pair_adds("early", place, srcs, got, [True] * len(srcs))
        return ready["parts"]

    loss, gx, g, early_from_chips = local_step(vals["x"][0], vals["loss_target"][0], p, late,
                                               (early_sources, early_parts))
    loss = lax.psum(loss, ("x", "y", "c"))
    srcs = [g[n].reshape(N_SHARD, SHARD_W, D_MODEL) for n in a_names] + [small_grad_pack(g)]
    rest = pair_adds("late", place, srcs, pair_exchange("grads_pair_exchange", srcs), [True] * len(a_names) + [False])
    rest_from_chips = chip_exchange(rest)
    na = len(a_names)
    halves = finish_sums(place, rest[:na] + ready["parts"] + rest[na:],
                         list(rest_from_chips[:na]) + list(early_from_chips) + list(rest_from_chips[na:]))

    res = {}
    for k, n in enumerate(BIG_NAMES):
        res[n] = adamw_update("adamw_" + n, place, halves[k:k + 1], w[n], vals["m_" + n], vals["v_" + n])
    for k, n in ((nb, "mlp_w_up"), (nb + 2, "mlp_w_down")):
        res[n] = adamw_update("adamw_" + n, place, halves[k:k + 2], w[n], vals["m_" + n], vals["v_" + n])
    packs = adamw_update("adamw_small", place, halves[-1:], w_pack[None], pack_small(lambda n: vals["m_" + n])[None],
                         pack_small(lambda n: vals["v_" + n])[None])
    shapes = {n: w[n].shape for n in WEIGHT_NAMES}
    small = [unpack_small(pk[0], shapes) for pk in packs]
    outs = [loss, gx[None]]
    for t in range(4):
        outs += [res[n][t] if n in res else small[t][n] for n in WEIGHT_NAMES]
    return tuple(outs)


def kernel(x, meta_tokens, a_mu, a_w_r, a_w_k, a_w_v, a_w_o, a_w0, a_w1, a_w2, a_a0, a_a1, a_a2, a_g1, a_g2, a_k_k,
           a_k_a, a_r_k, a_gn_w, a_gn_b, kv_w_k, kv_w_v, b_w_q, b_sinks, b_w_o, mlp_w_up, mlp_w_down, ln_g, ln_b,
           loss_target, m_meta_tokens, m_a_mu, m_a_w_r, m_a_w_k, m_a_w_v, m_a_w_o, m_a_w0, m_a_w1, m_a_w2, m_a_a0,
           m_a_a1, m_a_a2, m_a_g1, m_a_g2, m_a_k_k, m_a_k_a, m_a_r_k, m_a_gn_w, m_a_gn_b, m_kv_w_k, m_kv_w_v,
           m_b_w_q, m_b_sinks, m_b_w_o, m_mlp_w_up, m_mlp_w_down, m_ln_g, m_ln_b, v_meta_tokens, v_a_mu, v_a_w_r,
           v_a_w_k, v_a_w_v, v_a_w_o, v_a_w0, v_a_w1, v_a_w2, v_a_a0, v_a_a1, v_a_a2, v_a_g1, v_a_g2, v_a_k_k,
           v_a_k_a, v_a_r_k, v_a_gn_w, v_a_gn_b, v_kv_w_k, v_kv_w_v, v_b_w_q, v_b_sinks, v_b_w_o, v_mlp_w_up,
           v_mlp_w_down, v_ln_g, v_ln_b):
    return train_step(dict(locals()))
```

```python
import functools

import numpy as np
import jax
import jax.numpy as jnp
from jax import lax
from jax.experimental import pallas as pl
from jax.experimental.pallas import tpu as pltpu

F32 = jnp.float32
BF16 = jnp.bfloat16

D_MODEL = 1024
N_HEADS = 16
HEAD_DIM = 64
N_HEADS_KV = 4
GROUP = 4
KV_DIM = N_HEADS_KV * HEAD_DIM
N_META = 16
BLOCK = 128
PAD_FRONT = BLOCK - N_META
TOK0 = PAD_FRONT + N_META
N_FF_CHUNK = 4
N_SHARD = 4
GN_EPS = 64e-5
LN_EPS = 1e-5
ROPE_THETA = 10000.0
ALPHA = 4.0 ** 0.25
ADAM_LR, ADAM_B1, ADAM_B2, ADAM_EPS, ADAM_WD, ADAM_STEP = 0.001, 0.9, 0.999, 1e-08, 0.01, 10
SCAN_T = 64
PAIR = 128
KVW = GROUP * HEAD_DIM
VMEM_LIMIT = 60 * 1024 * 1024
MESH = pl.DeviceIdType.MESH


def _dot(a, b, ca, cb):
    return lax.dot_general(a.astype(BF16), b.astype(BF16), (((ca,), (cb,)), ((), ())),
                           preferred_element_type=F32)


@jax.custom_vjp
def mm(a, b):
    return _dot(a, b, 1, 0)


def _mm_fwd(a, b):
    return mm(a, b), b


def _mm_bwd(b, g):
    return _dot(g, b, 1, 1), jnp.zeros_like(b)


mm.defvjp(_mm_fwd, _mm_bwd)


@jax.custom_vjp
def mm_tap(a, b, tap):
    return _dot(a, b, 1, 0)


mm_tap.defvjp(lambda a, b, tap: (_dot(a, b, 1, 0), b), lambda b, g: (_dot(g, b, 1, 1), jnp.zeros_like(b), g))


def tmm(x, w, taps, xs):
    y = mm(x, w) if taps is None else mm_tap(x, w, taps[len(xs)])
    xs.append(x)
    return y


def vjp_taps(core, tap_shapes, args, cot):
    taps = [jnp.zeros(s, F32) for s in tap_shapes]
    _, vjp, xs = jax.vjp(core, taps, *args, has_aux=True)
    out = vjp(cot)
    return out[1:], [_dot(x, g, 0, 0) for x, g in zip(xs, out[0])]


def _split3(x):
    x1 = x.astype(BF16)
    r1 = x - x1.astype(F32)
    x2 = r1.astype(BF16)
    x3 = (r1 - x2.astype(F32)).astype(BF16)
    return x1, x2, x3


def _exact_dot(x, m01, cb=0):
    acc = None
    for piece in _split3(x)[:2]:
        t = lax.dot_general(piece, m01, (((1,), (cb,)), ((), ())), preferred_element_type=F32)
        acc = t if acc is None else acc + t
    return acc


def _head_matrices():
    e = np.zeros((D_MODEL, N_HEADS), np.float32)
    e[np.arange(D_MODEL), np.arange(D_MODEL) // HEAD_DIM] = 1.0
    return jnp.asarray(e, BF16), jnp.asarray(e.T, BF16)


@jax.custom_vjp
def hsum(x, e, et):
    return _exact_dot(x, e)


@jax.custom_vjp
def hbc(s, e, et):
    return _exact_dot(s, et)


hsum.defvjp(lambda x, e, et: (_exact_dot(x, e), (e, et)),
            lambda res, g: (hbc(g, *res), jnp.zeros_like(res[0]), jnp.zeros_like(res[1])))
hbc.defvjp(lambda s, e, et: (_exact_dot(s, et), (e, et)),
           lambda res, g: (hsum(g, *res), jnp.zeros_like(res[0]), jnp.zeros_like(res[1])))


def _sigmoid(u):
    return 0.5 * (jnp.tanh(0.5 * u) + 1.0)


def _softplus(u):
    return jnp.maximum(u, 0.0) + jnp.log(1.0 + jnp.exp(-jnp.abs(u)))


def _layer_norm(z, g, b):
    mu = jnp.mean(z, axis=-1, keepdims=True)
    zc = z - mu
    var = jnp.mean(zc * zc, axis=-1, keepdims=True)
    return zc * lax.rsqrt(var + LN_EPS) * g + b


def _zero_map(nd):
    return lambda c, i: (0,) * nd


def _params():
    return pltpu.CompilerParams(dimension_semantics=("arbitrary", "arbitrary"), vmem_limit_bytes=VMEM_LIMIT)


def rowwise(name, fn, rows, consts, out_rows, out_accs, tm, nc=1, hosted=None):
    lp = rows[0].shape[-2]
    nt = lp // tm
    assert nt * tm == lp, (name, lp, tm)
    copies_fn, hosted_src, hosted_shapes, hosted_scratch, hosted_post = hosted or (None, (), [], [], None)
    ng = len(hosted_src)
    in_specs, args = [], []
    for a in rows:
        if isinstance(a, tuple):
            a, block_rows, block_index = a
            in_specs.append(pl.BlockSpec((block_rows, a.shape[1]),
                                         functools.partial(lambda f, c, i: (f(i), 0), block_index)))
        elif a.ndim == 2:
            in_specs.append(pl.BlockSpec((tm, a.shape[1]), lambda c, i: (i, 0)))
        else:
            in_specs.append(pl.BlockSpec((a.shape[0], tm, a.shape[2]), lambda c, i: (0, i, 0)))
        args.append(a)
    for cst in consts:
        if isinstance(cst, tuple):
            arr, bs, im = cst
            in_specs.append(pl.BlockSpec(bs, im))
        else:
            arr = cst
            in_specs.append(pl.BlockSpec(arr.shape, _zero_map(arr.ndim), pipeline_mode=pl.Buffered(1)))
        args.append(arr)
    out_shape, out_specs, acc_per_chunk = [], [], []
    for spec in out_rows:
        if len(spec) == 4:
            out_shape.append(jax.ShapeDtypeStruct((spec[3], lp, spec[0]), spec[1]))
            out_specs.append(pl.BlockSpec((spec[3], tm, spec[0]), lambda c, i: (0, i, 0)))
        elif len(spec) == 3 and spec[2]:
            out_shape.append(jax.ShapeDtypeStruct((nc, lp, spec[0]), spec[1]))
            out_specs.append(pl.BlockSpec((None, tm, spec[0]), lambda c, i: (c, i, 0)))
        else:
            out_shape.append(jax.ShapeDtypeStruct((lp, spec[0]), spec[1]))
            out_specs.append(pl.BlockSpec((tm, spec[0]), lambda c, i: (i, 0)))
    for spec in out_accs:
        out_shape.append(jax.ShapeDtypeStruct(spec[0], spec[1]))
        if len(spec) == 4:
            out_specs.append(pl.BlockSpec(spec[2], spec[3]))
            acc_per_chunk.append(True)
        else:
            out_specs.append(pl.BlockSpec(spec[0], _zero_map(len(spec[0])), pipeline_mode=pl.Buffered(1)))
            acc_per_chunk.append(False)
    n_in, n_or, n_out = len(args), len(out_rows), len(out_shape)

    def body(*refs):
        c = pl.program_id(0)
        i = pl.program_id(1)
        if ng:
            src, dst = refs[n_in:n_in + ng], refs[n_in + ng + n_out:n_in + 2 * ng + n_out]
            sends, arrivals, forwards, forwarded = copies_fn(src, dst, *refs[n_in + 2 * ng + n_out:])

            @pl.when(jnp.logical_and(c == 0, i == 0))
            def _():
                for cp in sends:
                    cp.start()

        vals = [r[...] for r in refs[:n_in]]
        outs_r, outs_a = fn(c, i, *vals)
        out_refs = refs[n_in + ng:n_in + ng + n_out]
        for ref, val in zip(out_refs[:n_or], outs_r):
            ref[...] = val.astype(ref.dtype)
        for ref, val, per_chunk in zip(out_refs[n_or:], outs_a, acc_per_chunk):
            first = (i == 0) if per_chunk else jnp.logical_and(i == 0, c == 0)

            @pl.when(first)
            def _():
                ref[...] = val.astype(ref.dtype)

            @pl.when(jnp.logical_not(first))
            def _():
                ref[...] += val.astype(ref.dtype)

        if ng:
            @pl.when(jnp.logical_and(c == nc - 1, i == max(nt - 3, 0)))
            def _():
                for k, landed in enumerate(arrivals):
                    landed.wait_recv()
                    if forwards:
                        forwards[k].start()

            @pl.when(jnp.logical_and(c == nc - 1, i == nt - 1))
            def _():
                for cp in forwarded:
                    cp.wait_recv()
                for cp in sends + forwards:
                    cp.wait_send()

    outs = pl.pallas_call(body, name=name, grid=(nc, nt), in_specs=in_specs + [ANY] * ng,
                          out_specs=out_specs + [ANY] * ng, out_shape=out_shape + list(hosted_shapes),
                          scratch_shapes=list(hosted_scratch), compiler_params=_params())(*args, *hosted_src)
    if ng:
        return outs[:n_or], outs[n_or:n_out], hosted_post(outs[n_out:])
    return outs[:n_or], outs[n_or:]


def _row_ids(i, tm):
    return i * tm + lax.broadcasted_iota(jnp.int32, (tm, 1), 0)


SUBLANES = 8


def _halo_before(arr, tm):
    return (arr, SUBLANES, lambda i: jnp.maximum(i * (tm // SUBLANES) - 1, 0))


def _halo_after(arr, tm):
    last = arr.shape[0] // SUBLANES - 1
    return (arr, SUBLANES, lambda i: jnp.minimum((i + 1) * (tm // SUBLANES), last))


def _pick_row(block8, row):
    rows = lax.broadcasted_iota(jnp.int32, block8.shape, 0)
    return jnp.sum(jnp.where(rows == row, block8, 0.0), axis=0, keepdims=True)


def _shift_down(x, before8, i):
    rows = lax.broadcasted_iota(jnp.int32, x.shape, 0)
    top = _pick_row(before8, SUBLANES - 1) * (i > 0).astype(F32)
    return jnp.where(rows == 0, top, pltpu.roll(x, 1, 0))


def _shift_up(x, after8, i, nt):
    rows = lax.broadcasted_iota(jnp.int32, x.shape, 0)
    bottom = _pick_row(after8, 0) * (i < nt - 1).astype(F32)
    return jnp.where(rows == x.shape[0] - 1, bottom, pltpu.roll(x, x.shape[0] - 1, 0))


LORA_DECAY, LORA_AAA, LORA_GATE = 64, 64, 128
PRE_TAPS = (D_MODEL, D_MODEL, D_MODEL, LORA_DECAY, D_MODEL, LORA_AAA, D_MODEL, LORA_GATE, D_MODEL)


def rwkv_pre(e, et, ws, taps, h, hp, mu_r, mu_w, mu_k, mu_v, mu_a, mu_g, w0, a0, k_k, k_a):
    w_r, w_k, w_v, w1, w2, a1, a2, g1, g2 = ws
    xs = []
    xx = hp - h
    r = tmm(h + xx * mu_r, w_r, taps, xs)
    k = tmm(h + xx * mu_k, w_k, taps, xs)
    v = tmm(h + xx * mu_v, w_v, taps, xs)
    wraw = -_softplus(-(w0 + tmm(jnp.tanh(tmm(h + xx * mu_w, w1, taps, xs)), w2, taps, xs))) - 0.5
    lw = -jnp.exp(wraw)
    a = _sigmoid(a0 + tmm(tmm(h + xx * mu_a, a1, taps, xs), a2, taps, xs))
    g = tmm(_sigmoid(tmm(h + xx * mu_g, g1, taps, xs)), g2, taps, xs)
    kk = k * k_k
    ss = hsum(kk * kk, e, et)
    pos = ss > 0.0
    nrm = jnp.where(pos, jnp.sqrt(jnp.where(pos, ss, 1.0)), 0.0)
    kk = kk * hbc(1.0 / jnp.maximum(nrm, 1e-12), e, et)
    k2 = k * (1.0 + (a - 1.0) * k_a)
    return (r, lw, k2, v, -kk, kk * a, g), xs


def rwkv_post(e, et, w_o, taps, y, r, k2, v, g, h0, gn_w, gn_b, rk, lg, lb):
    xs = []
    inv_n = 1.0 / HEAD_DIM
    yc = y - hbc(hsum(y, e, et) * inv_n, e, et)
    yv = hsum(yc * yc, e, et) * inv_n
    yn = yc * hbc(lax.rsqrt(yv + GN_EPS), e, et) * gn_w + gn_b
    bonus = hbc(hsum(r * k2 * rk, e, et), e, et) * v
    mix = tmm((yn + bonus) * g, w_o, taps, xs)
    return _layer_norm(ALPHA * h0 + mix, lg, lb), xs


@jax.custom_vjp
def sq_relu(x):
    r = jnp.maximum(x, 0.0)
    return r * r


sq_relu.defvjp(lambda x: (sq_relu(x), x), lambda x, g: (g * (2.0 * jnp.maximum(x, 0.0)),))


def _rot_half(t):
    n = t.shape[-1]
    lane = lax.broadcasted_iota(jnp.int32, t.shape, t.ndim - 1)
    lo = (lane % HEAD_DIM) < (HEAD_DIM // 2)
    return jnp.where(lo, -pltpu.roll(t, n - HEAD_DIM // 2, t.ndim - 1), pltpu.roll(t, HEAD_DIM // 2, t.ndim - 1))


@jax.custom_vjp
def rot_half(t):
    return _rot_half(t)


rot_half.defvjp(lambda t: (_rot_half(t), None), lambda _, g: (-_rot_half(g),))


def _tile_lanes(t, width):
    return jnp.concatenate([t] * (width // t.shape[-1]), axis=-1)


def qkv_proj(cos, sin, wq, wk, wv, taps, h):
    xs = []
    q = tmm(h, wq, taps, xs)
    k = tmm(h, wk, taps, xs)
    v = tmm(h, wv, taps, xs)
    cq, sq = _tile_lanes(cos, D_MODEL), _tile_lanes(sin, D_MODEL)
    ck, sk = _tile_lanes(cos, KV_DIM), _tile_lanes(sin, KV_DIM)
    return (q * cq + rot_half(q) * sq, k * ck + rot_half(k) * sk, v), xs


def attn_out(w_o, taps, o, h, lg, lb):
    xs = []
    return _layer_norm(ALPHA * h + tmm(o, w_o, taps, xs), lg, lb), xs


def _scan_consts():
    t = SCAN_T
    tri = np.tril(np.ones((t, t), np.float32))
    rows = np.arange(2 * t)
    same = (rows[:, None] // t) == (rows[None, :] // t)
    strict = same & ((rows[None, :] % t) < (rows[:, None] % t))
    incl = same & ((rows[None, :] % t) <= (rows[:, None] % t))
    lane = np.arange(PAIR)
    masks = np.zeros((8, PAIR), np.float32)
    masks[0] = (lane // HEAD_DIM) == 0
    masks[1] = (lane // HEAD_DIM) == 1
    return (jnp.asarray(tri, BF16), jnp.asarray(strict.astype(np.float32)), jnp.asarray(incl.astype(np.float32)),
            jnp.asarray(masks), jnp.asarray(np.eye(2 * t, dtype=np.float32)))


def _scan_dot(a, b, ca, cb):
    return _dot(a, b, ca, cb)


@functools.partial(jax.custom_vjp, nondiff_argnums=(2, 3))
def _dotf(a, b, ca, cb):
    return _scan_dot(a, b, ca, cb)


def _dotf_bwd(ca, cb, res, g):
    a, b = res
    if ca == 1:
        da = _scan_dot(g, b, 1, 1 - cb)
    else:
        da = _scan_dot(b, g, 1 - cb, 1)
    if cb == 0:
        db = _scan_dot(a, g, 1 - ca, 0)
    else:
        db = _scan_dot(g, a, 0, 1 - ca)
    return da, db


_dotf.defvjp(lambda a, b, ca, cb: (_scan_dot(a, b, ca, cb), (a, b)), _dotf_bwd)


def _tri_dot(tri, x, ct):
    acc = None
    for piece in _split3(x):
        t = lax.dot_general(tri, piece, (((ct,), (0,)), ((), ())), preferred_element_type=F32)
        acc = t if acc is None else acc + t
    return acc


@jax.custom_vjp
def _cumsum_rows(tri, x):
    return _tri_dot(tri, x, 1)


_cumsum_rows.defvjp(lambda tri, x: (_tri_dot(tri, x, 1), tri),
                    lambda tri, g: (jnp.zeros_like(tri), _tri_dot(tri, g, 0)))


@jax.custom_vjp
def _unstack2(x):
    t = x.shape[0] // 2
    return x[:t] + x[t:]


_unstack2.defvjp(lambda x: (_unstack2(x), None), lambda _, g: (jnp.concatenate([g, g], axis=0),))


@jax.custom_vjp
def _last_row(x):
    return x[x.shape[0] - 1:, :]


def _last_row_bwd(_, g):
    rows = lax.broadcasted_iota(jnp.int32, (SCAN_T, g.shape[1]), 0)
    return (jnp.where(rows == SCAN_T - 1, jnp.broadcast_to(g, (SCAN_T, g.shape[1])), 0.0),)


_last_row.defvjp(lambda x: (_last_row(x), None), _last_row_bwd)


@jax.custom_vjp
def _halves(x):
    n = x.shape[0] // 2
    return x[:n], x[n:]


_halves.defvjp(lambda x: (_halves(x), None), lambda _, g: (jnp.concatenate(list(g), axis=0),))


@jax.custom_vjp
def _quads(x):
    n, m = x.shape[0] // 2, x.shape[1] // 2
    return x[:n, :m], x[:n, m:], x[n:, :m], x[n:, m:]


_quads.defvjp(lambda x: (_quads(x), None),
              lambda _, g: (jnp.concatenate([jnp.concatenate([g[0], g[1]], axis=1),
                                             jnp.concatenate([g[2], g[3]], axis=1)], axis=0),))


@jax.custom_vjp
def _solve_saved(n, rhs, minv, u):
    return u


def _solve_saved_bwd(res, du):
    minv, u = res
    drhs = _dotf(minv, du, 0, 0)
    return _dotf(drhs, u, 1, 1), drhs, jnp.zeros_like(minv), jnp.zeros_like(u)


_solve_saved.defvjp(lambda n, rhs, minv, u: (u, (minv, u)), _solve_saved_bwd)


def scan_chunk(tri, strict, incl, m0, m1, eye, r, lw, k, v, a, b, s0, saved=None):
    lower = strict > 0
    lower_incl = incl > 0

    def stack(x):
        return jnp.concatenate([x * m0, x * m1], axis=0)

    def dots(xs, ys, ca, cb, mask=None):
        out = [_dotf(x, y, ca, cb) for x, y in zip(xs, ys)]
        return out if mask is None else [jnp.where(mask, o, 0.0) for o in out]

    cl = [_cumsum_rows(tri, x) for x in lw]
    gam = [jnp.exp(c) for c in cl]
    ginv = [jnp.exp(-c) for c in cl]
    ar_s = [jnp.concatenate([stack(x * jnp.exp(c - w)), stack(y * g)], axis=0)
            for x, c, w, y, g in zip(a, cl, lw, r, gam)]
    bk_s = [jnp.concatenate([stack(x * g), stack(y * g)], axis=0) for x, y, g in zip(b, k, ginv)]
    v_s = [stack(x) for x in v]
    quads = [_quads(x) for x in dots(ar_s, bk_s, 1, 1)]
    n_ab = [jnp.where(lower, q[0], 0.0) for q in quads]
    n_ak = [jnp.where(lower, q[1], 0.0) for q in quads]
    r_ab = [jnp.where(lower_incl, q[2], 0.0) for q in quads]
    r_ak = [jnp.where(lower_incl, q[3], 0.0) for q in quads]
    from_state = [_halves(x) for x in dots(ar_s, s0, 1, 1)]
    rhs = [x[0] + y for x, y in zip(from_state, dots(n_ak, v_s, 1, 0))]
    if saved is None:
        minv = [eye + n for n in n_ab]
        p = n_ab
        for _ in range(5):
            p = dots(p, p, 1, 0)
            minv = [m + mp for m, mp in zip(minv, dots(minv, p, 1, 0))]
        u_s = dots(minv, rhs, 1, 0)
    else:
        minv = saved[0]
        u_s = [_solve_saved(n, x, m, u) for n, x, m, u in zip(n_ab, rhs, *saved)]
    uv_s = [jnp.concatenate([x, y], axis=0) for x, y in zip(u_s, v_s)]
    r_uv = [jnp.concatenate([x, y], axis=1) for x, y in zip(r_ab, r_ak)]
    y = [_unstack2(x[1] + z) for x, z in zip(from_state, dots(r_uv, uv_s, 1, 0))]
    g_end = [_last_row(g) for g in gam]
    s1 = [s * g + x for s, g, x in zip(s0, g_end, dots(uv_s, [x * g for x, g in zip(bk_s, g_end)], 0, 0))]
    return y, s1, (minv, u_s)


SCAN_PAIRS = 8


def _scan_specs(consts, order):
    row = pl.BlockSpec((SCAN_T, PAIR * SCAN_PAIRS), lambda p, c: (order(c), p))
    state = pl.BlockSpec((None, SCAN_PAIRS, PAIR, PAIR), lambda p, c: (order(c), p, 0, 0))
    return row, state, [pl.BlockSpec(x.shape, _zero_map(x.ndim)) for x in consts]


def _pair_lanes(q):
    return slice(q * PAIR, (q + 1) * PAIR)


def scan_fwd(r, lw, k, v, a, b, shards=()):
    lp = r.shape[0]
    nch = lp // SCAN_T
    npair = D_MODEL // PAIR
    ng = len(shards)
    consts = _scan_consts()
    row, state, cspecs = _scan_specs(consts, lambda c: c)

    def body(tri, strict, incl, masks, eye, r_ref, lw_ref, k_ref, v_ref, a_ref, b_ref, *rest):
        src, (y_ref, s_ref, minv_ref, u_ref), dst = rest[:ng], rest[ng:ng + 4], rest[ng + 4:2 * ng + 4]
        carry = rest[2 * ng + 4]
        first = jnp.logical_and(pl.program_id(0) == 0, pl.program_id(1) == 0)
        last = jnp.logical_and(pl.program_id(0) == npair // SCAN_PAIRS - 1, pl.program_id(1) == nch - 1)
        if ng:
            sends, arrivals, forwards, forwarded = gather_copies(src, dst, *rest[2 * ng + 5:])

            @pl.when(first)
            def _():
                for cp in sends:
                    cp.start()

            @pl.when(jnp.logical_and(pl.program_id(0) == npair // SCAN_PAIRS - 1, pl.program_id(1) == nch * 3 // 4))
            def _():
                for landed, onward in zip(arrivals, forwards):
                    landed.wait_recv()
                    onward.start()

        @pl.when(pl.program_id(1) == 0)
        def _():
            carry[...] = jnp.zeros_like(carry)

        pairs = range(SCAN_PAIRS)
        s0 = [carry[q] for q in pairs]
        rows = [[ref[:, _pair_lanes(q)] for q in pairs] for ref in (r_ref, lw_ref, k_ref, v_ref, a_ref, b_ref)]
        y, s1, (minv, u) = scan_chunk(tri[...], strict[...], incl[...], masks[0:1, :], masks[1:2, :], eye[...],
                                      *rows, s0)
        for q in pairs:
            s_ref[q] = s0[q]
            minv_ref[q] = minv[q]
            u_ref[q] = u[q]
            y_ref[:, _pair_lanes(q)] = y[q]
            carry[q] = s1[q]

        if ng:
            @pl.when(last)
            def _():
                for cp in forwarded:
                    cp.wait_recv()
                for cp in sends + forwards:
                    cp.wait_send()

    mats = jax.ShapeDtypeStruct((nch, npair, PAIR, PAIR), F32)
    out = pl.pallas_call(
        body, name="rwkv_scan_fwd", grid=(npair // SCAN_PAIRS, nch), in_specs=cspecs + [row] * 6 + [ANY] * ng,
        out_specs=[row, state, state, state] + [ANY] * ng,
        out_shape=[jax.ShapeDtypeStruct((lp, D_MODEL), F32), mats, mats, mats] + gathered_shapes(shards),
        scratch_shapes=[pltpu.VMEM((SCAN_PAIRS, PAIR, PAIR), F32)] + (gather_scratch(ng) if ng else []),
        compiler_params=_params(),
    )(*consts, r, lw, k, v, a, b, *shards)
    return out[:4], fill_own(out[4:], shards)


def scan_bwd(r, lw, k, v, a, b, saved, dy, direct_grads, parts=()):
    lp = r.shape[0]
    nch = lp // SCAN_T
    npair = D_MODEL // PAIR
    consts = _scan_consts()
    row, state, cspecs = _scan_specs(consts, lambda c: nch - 1 - c)

    ng = len(parts)

    def body(tri, strict, incl, masks, eye, r_ref, lw_ref, k_ref, v_ref, a_ref, b_ref, s_ref, minv_ref, u_ref,
             dy_ref, dr_in, dk_in, dv_in, *rest):
        src, (dr_ref, dlw_ref, dk_ref, dv_ref, da_ref, db_ref), dst = rest[:ng], rest[ng:ng + 6], rest[ng + 6:2 * ng + 6]
        carry = rest[2 * ng + 6]
        first = jnp.logical_and(pl.program_id(0) == 0, pl.program_id(1) == 0)
        last = jnp.logical_and(pl.program_id(0) == npair // SCAN_PAIRS - 1, pl.program_id(1) == nch - 1)
        if ng:
            sends, arrivals = chip_exchange_copies(src, dst, *rest[2 * ng + 7:])

            @pl.when(first)
            def _():
                for cp in sends:
                    cp.start()

        @pl.when(pl.program_id(1) == 0)
        def _():
            carry[...] = jnp.zeros_like(carry)

        pairs = range(SCAN_PAIRS)
        kept = ([minv_ref[q] for q in pairs], [u_ref[q] for q in pairs])

        def fn(*args):
            y, s1, _ = scan_chunk(tri[...], strict[...], incl[...], masks[0:1, :], masks[1:2, :], eye[...], *args,
                                  saved=kept)
            return y, s1

        rows = [[ref[:, _pair_lanes(q)] for q in pairs] for ref in (r_ref, lw_ref, k_ref, v_ref, a_ref, b_ref)]
        _, vjp = jax.vjp(fn, *rows, [s_ref[q] for q in pairs])
        grads = vjp(([dy_ref[:, _pair_lanes(q)] for q in pairs], [carry[q] for q in pairs]))
        direct = (dr_in, None, dk_in, dv_in, None, None)
        for q in pairs:
            ln = _pair_lanes(q)
            for ref, g, extra in zip((dr_ref, dlw_ref, dk_ref, dv_ref, da_ref, db_ref), grads[:6], direct):
                ref[:, ln] = g[q] if extra is None else g[q] + extra[:, ln]
            carry[q] = grads[6][q]

        if ng:
            @pl.when(last)
            def _():
                for cp in arrivals:
                    cp.wait_recv()
                for cp in sends:
                    cp.wait_send()

    out = pl.pallas_call(
        body, name="rwkv_scan_bwd", grid=(npair // SCAN_PAIRS, nch),
        in_specs=cspecs + [row] * 6 + [state] * 3 + [row] * 4 + [ANY] * ng, out_specs=[row] * 6 + [ANY] * ng,
        out_shape=[jax.ShapeDtypeStruct((lp, D_MODEL), F32)] * 6 + [jax.ShapeDtypeStruct(p.shape, p.dtype) for p in parts],
        scratch_shapes=[pltpu.VMEM((SCAN_PAIRS, PAIR, PAIR), F32)] + (_sem_scratch(ng * len(XY_FLIPS)) if ng else []),
        compiler_params=_params(),
    )(*consts, r, lw, k, v, a, b, *saved, dy, *direct_grads, *parts)
    return out[:6], out[6:]


def _spread_matrices():
    rep = np.zeros((N_HEADS_KV, KV_DIM, KVW), np.float32)
    for h in range(N_HEADS_KV):
        for g in range(GROUP):
            rep[h, h * HEAD_DIM + np.arange(HEAD_DIM), g * HEAD_DIM + np.arange(HEAD_DIM)] = 1.0
    return jnp.asarray(rep, BF16)


KV_HEADS = range(N_HEADS_KV)


def _attn_common(n, q_ref, kp, kc, vp, vc, rep_ref, sink_ref):
    lane = lax.broadcasted_iota(jnp.int32, (1, KVW), 1)
    gmask = [(lane // HEAD_DIM == g).astype(F32) for g in range(GROUP)]
    kk = jnp.concatenate([kp, kc], axis=0)
    vv = jnp.concatenate([vp, vc], axis=0)
    qs = [q_ref[:, h * KVW:(h + 1) * KVW] for h in KV_HEADS]
    q_s = [jnp.concatenate([q * gmask[g] for g in range(GROUP)], axis=0) for q in qs]
    keys = [_dot(kk, rep_ref[h], 1, 0) for h in KV_HEADS]
    vals = [_dot(vv, rep_ref[h], 1, 0) for h in KV_HEADS]
    qi = lax.broadcasted_iota(jnp.int32, (GROUP * BLOCK, 2 * BLOCK), 0) % BLOCK
    kj = lax.broadcasted_iota(jnp.int32, (GROUP * BLOCK, 2 * BLOCK), 1)
    rel = BLOCK + qi - kj
    valid = (rel >= 0) & (rel < BLOCK) & ((n - 1) * BLOCK + kj >= PAD_FRONT)
    s = [jnp.where(valid, _dot(x, y, 1, 1) * (HEAD_DIM ** -0.5), -1e30) for x, y in zip(q_s, keys)]
    sink_col = [jnp.concatenate([jnp.broadcast_to(sink_ref[h, g:g + 1, 0:1], (BLOCK, 1)) for g in range(GROUP)],
                                axis=0) for h in KV_HEADS]
    m = [jnp.maximum(jnp.max(x, axis=-1, keepdims=True), c) for x, c in zip(s, sink_col)]
    ex = [jnp.exp(x - y) for x, y in zip(s, m)]
    ex_sink = [jnp.exp(c - y) for c, y in zip(sink_col, m)]
    inv = [1.0 / (jnp.sum(x, axis=-1, keepdims=True) + c) for x, c in zip(ex, ex_sink)]
    return (gmask, q_s, keys, vals, [x * y for x, y in zip(ex, inv)], [x * y for x, y in zip(ex_sink, inv)])


def _unstack_groups(x_s, gmask):
    out = None
    for g in range(GROUP):
        t = x_s[g * BLOCK:(g + 1) * BLOCK] * gmask[g]
        out = t if out is None else out + t
    return out


def _attn_specs():
    qspec = pl.BlockSpec((BLOCK, D_MODEL), lambda n: (n, 0))
    cur = pl.BlockSpec((BLOCK, KV_DIM), lambda n: (n, 0))
    prev = pl.BlockSpec((BLOCK, KV_DIM), lambda n: (jnp.maximum(n - 1, 0), 0))
    rep = pl.BlockSpec((N_HEADS_KV, KV_DIM, KVW), lambda n: (0, 0, 0))
    sink = pl.BlockSpec((N_HEADS_KV, 8, PAIR), lambda n: (0, 0, 0))
    return qspec, cur, prev, rep, sink


def _attn_params():
    return pltpu.CompilerParams(dimension_semantics=("arbitrary",), vmem_limit_bytes=VMEM_LIMIT)


def attn_fwd(q, k, v, sinks_b):
    lp = q.shape[0]
    qspec, cur, prev, rep, sink = _attn_specs()

    def body(q_ref, kp_ref, kc_ref, vp_ref, vc_ref, rep_ref, sink_ref, o_ref):
        gmask, _, _, vals, p, _ = _attn_common(pl.program_id(0), q_ref, kp_ref[...], kc_ref[...], vp_ref[...],
                                               vc_ref[...], rep_ref, sink_ref)
        o = [_dot(x, y, 1, 0) for x, y in zip(p, vals)]
        for h in KV_HEADS:
            o_ref[:, h * KVW:(h + 1) * KVW] = _unstack_groups(o[h], gmask)

    return pl.pallas_call(
        body, name="swa_fwd", grid=(lp // BLOCK,), in_specs=[qspec, prev, cur, prev, cur, rep, sink],
        out_specs=qspec, out_shape=jax.ShapeDtypeStruct((lp, D_MODEL), F32), compiler_params=_attn_params(),
    )(q, k, k, v, v, _spread_matrices(), sinks_b)


def attn_bwd(q, k, v, sinks_b, do):
    lp = q.shape[0]
    qspec, cur, prev, rep, sink = _attn_specs()

    def body(q_ref, kp_ref, kc_ref, vp_ref, vc_ref, rep_ref, sink_ref, do_ref, dq_ref, dkc_ref, dkp_ref, dvc_ref,
             dvp_ref, dsink_ref):
        n = pl.program_id(0)
        gmask, q_s, keys, vals, p, p_sink = _attn_common(n, q_ref, kp_ref[...], kc_ref[...], vp_ref[...], vc_ref[...],
                                                         rep_ref, sink_ref)
        do_s = [jnp.concatenate([do_ref[:, h * KVW:(h + 1) * KVW] * gmask[g] for g in range(GROUP)], axis=0)
                for h in KV_HEADS]
        dp = [_dot(x, y, 1, 1) for x, y in zip(do_s, vals)]
        delta = [jnp.sum(x * y, axis=-1, keepdims=True) for x, y in zip(p, dp)]
        ds = [x * (y - z) * (HEAD_DIM ** -0.5) for x, y, z in zip(p, dp, delta)]
        dq = [_dot(x, y, 1, 0) for x, y in zip(ds, keys)]
        dkeys_s = [_dot(x, y, 0, 0) for x, y in zip(ds, q_s)]
        dvals_s = [_dot(x, y, 0, 0) for x, y in zip(p, do_s)]
        dkeys = [_exact_dot(x, rep_ref[h], cb=1) for h, x in enumerate(dkeys_s)]
        dvals = [_exact_dot(x, rep_ref[h], cb=1) for h, x in enumerate(dvals_s)]
        dk_all = (dkeys[0] + dkeys[1]) + (dkeys[2] + dkeys[3])
        dv_all = (dvals[0] + dvals[1]) + (dvals[2] + dvals[3])
        dkp_ref[...] = dk_all[:BLOCK]
        dkc_ref[...] = dk_all[BLOCK:]
        dvp_ref[...] = dv_all[:BLOCK]
        dvc_ref[...] = dv_all[BLOCK:]
        dsinks = []
        for h in KV_HEADS:
            dq_ref[:, h * KVW:(h + 1) * KVW] = _unstack_groups(dq[h], gmask)
            dsk = -(p_sink[h] * delta[h])
            rows = [jnp.broadcast_to(jnp.sum(dsk[g * BLOCK:(g + 1) * BLOCK], axis=0, keepdims=True), (1, PAIR))
                    for g in range(GROUP)]
            dsinks.append(jnp.concatenate(rows + [jnp.zeros((8 - GROUP, PAIR), F32)], axis=0))

        @pl.when(n == 0)
        def _():
            for h in KV_HEADS:
                dsink_ref[h] = dsinks[h]

        @pl.when(n > 0)
        def _():
            for h in KV_HEADS:
                dsink_ref[h] += dsinks[h]

    kv = jax.ShapeDtypeStruct((lp, KV_DIM), F32)
    return pl.pallas_call(
        body, name="swa_bwd", grid=(lp // BLOCK,), in_specs=[qspec, prev, cur, prev, cur, rep, sink, qspec],
        out_specs=[qspec, cur, cur, cur, cur, sink],
        out_shape=[jax.ShapeDtypeStruct((lp, D_MODEL), F32), kv, kv, kv, kv,
                   jax.ShapeDtypeStruct((N_HEADS_KV, 8, PAIR), F32)],
        compiler_params=_attn_params(),
    )(q, k, k, v, v, _spread_matrices(), sinks_b, do)


def _pick_tm(lp, want):
    for tm in (384, 192, 128, 64):
        if tm <= want and lp % tm == 0:
            return tm
    raise ValueError(lp)


def _acc(shape):
    return (tuple(shape), F32)


def _ff_one(w):
    return (w, (None, D_MODEL, D_MODEL), lambda c, i: (c, 0, 0))


def _mlp_layer_fwd(name, h, wup, wdown, lg, lb, tm):
    def fn(c, i, h, wup, wdown, lg, lb):
        out, pre = None, []
        for s in range(N_FF_CHUNK):
            u = mm(h, wup[s])
            pre.append(u.astype(BF16))
            t = mm(sq_relu(u), wdown[s])
            out = t if out is None else out + t
        z = ALPHA * h + out
        return (_layer_norm(z, lg, lb), z, jnp.stack(pre)), ()

    (h_out, z, pre), _ = rowwise(name, fn, [h], [wup, wdown, lg, lb],
                                 [(D_MODEL, F32), (D_MODEL, F32), (D_MODEL, BF16, False, N_FF_CHUNK)], [], tm)
    return h_out, z, pre


MLP_BWD_TILE = 528


def _mlp_layer_bwd(name, h_in, z, pre, dh_parts, wup, wdown, lg, lb, tm):
    n_parts = len(dh_parts)

    def fn_ln(c, i, z, *rest):
        dh = rest[0]
        for extra in rest[1:n_parts]:
            dh = dh + extra
        _, vjp = jax.vjp(_layer_norm, z, rest[n_parts], rest[n_parts + 1])
        dz, dlg, dlb = vjp(dh)
        return (dz,), (dlg, dlb)

    (dz,), (dlg, dlb) = rowwise(name + "_ln", fn_ln, [z] + list(dh_parts), [lg, lb], [(D_MODEL, F32)],
                                [_acc((1, D_MODEL)), _acc((1, D_MODEL))], tm)

    def fn_mlp(c, i, h, dz, wup, wdown, u):
        r = jnp.maximum(u.astype(F32), 0.0)
        du = _dot(dz, wdown, 1, 1) * (2.0 * r)
        return (_dot(du, wup, 1, 1),), (_dot(h, du, 0, 0), _dot(r * r, dz, 0, 0))

    aspec = ((N_FF_CHUNK, D_MODEL, D_MODEL), F32, (None, D_MODEL, D_MODEL), lambda c, i: (c, 0, 0))
    lp = h_in.shape[0]
    tile = MLP_BWD_TILE if lp % MLP_BWD_TILE == 0 else tm
    pre_chunk = (pre, (None, tile, D_MODEL), lambda c, i: (c, i, 0))
    (dx,), (dwup, dwdown) = rowwise(name + "_mm", fn_mlp, [h_in, dz], [_ff_one(wup), _ff_one(wdown), pre_chunk],
                                    [(D_MODEL, F32, True)], [aspec, aspec], tile, nc=N_FF_CHUNK)
    return dz, dx, dwup, dwdown, dlg, dlb


def _sum_parts(dz, dx):
    out = ALPHA * dz
    for s in range(N_FF_CHUNK):
        out = out + dx[s]
    return out


def local_step(x, loss_target, p, late=None, early_hook=None):
    seq = x.shape[0]
    lp = TOK0 + seq
    tm = _pick_tm(lp, 384)
    tms = _pick_tm(lp, 192)
    e, et = _head_matrices()
    h0 = jnp.concatenate([jnp.zeros((PAD_FRONT, D_MODEL), F32), p["meta_tokens"], x], axis=0)
    pos = jnp.maximum(jnp.arange(lp, dtype=F32) - PAD_FRONT, 0.0)
    inv_freq = 1.0 / (ROPE_THETA ** (jnp.arange(0, HEAD_DIM, 2, dtype=F32) / HEAD_DIM))
    ang = pos[:, None] * inv_freq[None, :]
    cos = jnp.tile(jnp.cos(ang), (1, PAIR // (HEAD_DIM // 2)))
    sin = jnp.tile(jnp.sin(ang), (1, PAIR // (HEAD_DIM // 2)))

    pre_vec = [p["a_mu"][j:j + 1] for j in range(6)] + [p["a_w0"], p["a_a0"], p["a_k_k"], p["a_k_a"]]
    pre_w = [p["a_w_r"], p["a_w_k"], p["a_w_v"], p["a_w1"], p["a_w2"], p["a_a1"], p["a_a2"], p["a_g1"], p["a_g2"]]
    n_vec = len(pre_vec)

    def fn_pre(c, i, h, before, e, et, *ws):
        return rwkv_pre(e, et, ws[n_vec:], None, h, _shift_down(h, before, i), *ws[:n_vec])[0], ()

    (r, lw, k2, v, an, bn, g), _, *pre_gathered = rowwise(
        "rwkv_pre", fn_pre, [h0, _halo_before(h0, tms)], [e, et] + pre_vec + pre_w, [(D_MODEL, F32)] * 7, [], tms,
        hosted=hosted_gather(late[0][0]) if late else None)
    (y, *scan_saved), scan_gathered = scan_fwd(r, lw, k2, v, an, bn, late[1][0] if late else ())
    if late:
        p = {**p, **late[0][1](pre_gathered[0]), **late[1][1](scan_gathered)}

    post_c = [p["a_w_o"], p["a_gn_w"], p["a_gn_b"], p["a_r_k"], p["ln_g00"], p["ln_b00"]]

    def fn_post(c, i, y, r, k2, v, g, h0, e, et, w_o, *vecs):
        return (rwkv_post(e, et, w_o, None, y, r, k2, v, g, h0, *vecs)[0],), ()

    (h1,), _ = rowwise("rwkv_post", fn_post, [y, r, k2, v, g, h0], [e, et] + post_c, [(D_MODEL, F32)], [], tm)
    h2, z2, pre2 = _mlp_layer_fwd("mlp0_fwd", h1, p["mlp_up0"], p["mlp_down0"], p["ln_g01"], p["ln_b01"], tm)

    qkv_w = [p["b_w_q"], p["kv_w_k"], p["kv_w_v"]]

    def fn_qkv(c, i, h, cos, sin, wq, wk, wv):
        return qkv_proj(cos, sin, wq, wk, wv, None, h)[0], ()

    (q, k, vv), _ = rowwise("qkv_proj", fn_qkv, [h2, cos, sin], qkv_w,
                            [(D_MODEL, F32), (KV_DIM, F32), (KV_DIM, F32)], [], tm)
    sinks_b = jnp.broadcast_to(p["b_sinks"].reshape(N_HEADS_KV, GROUP, 1), (N_HEADS_KV, GROUP, PAIR))
    sinks_b = jnp.concatenate([sinks_b, jnp.zeros((N_HEADS_KV, 8 - GROUP, PAIR), F32)], axis=1)
    o = attn_fwd(q, k, vv, sinks_b)

    ao_c = [p["b_w_o"], p["ln_g10"], p["ln_b10"]]

    def fn_ao(c, i, o, h, w_o, lg, lb):
        return (attn_out(w_o, None, o, h, lg, lb)[0],), ()

    (h3,), _ = rowwise("attn_out", fn_ao, [o, h2], ao_c, [(D_MODEL, F32)], [], tm)
    h4, z4, pre4 = _mlp_layer_fwd("mlp1_fwd", h3, p["mlp_up1"], p["mlp_down1"], p["ln_g11"], p["ln_b11"], tm)

    def fn_loss(c, i, h4, tgt):
        real = (_row_ids(i, TOK0) >= TOK0).astype(F32)
        err = (h4 - tgt) * real
        part = 0.5 * jnp.sum(jnp.sum(err * err, axis=-1, keepdims=True), axis=0, keepdims=True) / D_MODEL
        return (err * (1.0 / D_MODEL),), (jnp.broadcast_to(part, (8, PAIR)),)

    (dh4,), (loss_acc,) = rowwise("loss", fn_loss, [h4, (loss_target, TOK0, lambda i: jnp.maximum(i - 1, 0))], [],
                                  [(D_MODEL, F32)], [_acc((8, PAIR))], TOK0)
    loss = loss_acc[0, 0]

    grads = {}
    dz4, dx4, grads["mlp_up1"], grads["mlp_down1"], grads["ln_g11"], grads["ln_b11"] = _mlp_layer_bwd(
        "mlp1_bwd", h3, z4, pre4, [dh4], p["mlp_up1"], p["mlp_down1"], p["ln_g11"], p["ln_b11"], tm)

    def fn_ao_b(c, i, dz, dx, o, h, w_o, lg, lb):
        (do, dh, dlg, dlb), (dw_o,) = vjp_taps(functools.partial(attn_out, w_o), [(tm, D_MODEL)], [o, h, lg, lb],
                                               _sum_parts(dz, dx))
        return (do, dh), (dw_o, dlg, dlb)

    (do, dh2_a), (grads["b_w_o"], grads["ln_g10"], grads["ln_b10"]) = rowwise(
        "attn_out_bwd", fn_ao_b, [dz4, dx4, o, h2], ao_c, [(D_MODEL, F32)] * 2,
        [_acc((D_MODEL, D_MODEL)), _acc((1, D_MODEL)), _acc((1, D_MODEL))], tm)

    dq, dkc, dkp, dvc, dvp, dsinks = attn_bwd(q, k, vv, sinks_b, do)
    grads["b_sinks"] = dsinks[:, :GROUP, 0].reshape(1, N_HEADS)
    zblk = jnp.zeros((BLOCK, KV_DIM), F32)
    dkp_s = jnp.concatenate([dkp[BLOCK:], zblk], axis=0)
    dvp_s = jnp.concatenate([dvp[BLOCK:], zblk], axis=0)

    def fn_qkv_b(c, i, h, cos, sin, dq, dkc, dkp, dvc, dvp, wq, wk, wv):
        return vjp_taps(functools.partial(qkv_proj, cos, sin, wq, wk, wv),
                        [(tm, D_MODEL), (tm, KV_DIM), (tm, KV_DIM)], [h], (dq, dkc + dkp, dvc + dvp))

    (dh2_q,), (grads["b_w_q"], grads["kv_w_k"], grads["kv_w_v"]) = rowwise(
        "qkv_proj_bwd", fn_qkv_b, [h2, cos, sin, dq, dkc, dkp_s, dvc, dvp_s], qkv_w, [(D_MODEL, F32)],
        [_acc((D_MODEL, D_MODEL)), _acc((D_MODEL, KV_DIM)), _acc((D_MODEL, KV_DIM))], tm)

    dz2, dx2, grads["mlp_up0"], grads["mlp_down0"], grads["ln_g01"], grads["ln_b01"] = _mlp_layer_bwd(
        "mlp0_bwd", h1, z2, pre2, [dh2_a, dh2_q], p["mlp_up0"], p["mlp_down0"], p["ln_g01"], p["ln_b01"], tm)

    def fn_post_b(c, i, dz, dx, y, r, k2, v, g, h0, e, et, w_o, *vecs):
        out, dws = vjp_taps(functools.partial(rwkv_post, e, et, w_o), [(tms, D_MODEL)],
                            [y, r, k2, v, g, h0] + list(vecs), _sum_parts(dz, dx))
        return out[:6], tuple(dws) + tuple(out[6:])

    early_srcs = early_hook[0](grads) if early_hook else ()
    (dy, dr_c, dk_c, dv_c, dg, dh0_c), post_g, *early_got = rowwise(
        "rwkv_post_bwd", fn_post_b, [dz2, dx2, y, r, k2, v, g, h0], [e, et] + post_c, [(D_MODEL, F32)] * 6,
        [_acc((D_MODEL, D_MODEL))] + [_acc((1, D_MODEL))] * 5, tms,
        hosted=hosted_pair_exchange(early_srcs) if early_hook else None)
    for name, val in zip(["a_w_o", "a_gn_w", "a_gn_b", "a_r_k", "ln_g00", "ln_b00"], post_g):
        grads[name] = val

    (dr, dlw, dk2, dv, dan, dbn), early_from_chips = scan_bwd(
        r, lw, k2, v, an, bn, scan_saved, dy, (dr_c, dk_c, dv_c),
        early_hook[1](early_srcs, early_got[0]) if early_hook else ())

    def fn_pre_b(c, i, h, before, dr, dlw, dk2, dv, dan, dbn, dg, e, et, *ws):
        hp = _shift_down(h, before, i)
        real = (_row_ids(i, tms) >= PAD_FRONT).astype(F32)
        cot = tuple(t * real for t in (dr, dlw, dk2, dv, dan, dbn, dg))
        out, dws = vjp_taps(functools.partial(rwkv_pre, e, et, ws[n_vec:]), [(tms, n) for n in PRE_TAPS],
                            [h, hp] + list(ws[:n_vec]), cot)
        return out[:2], tuple(out[2:]) + tuple(dws)

    (dh0_p, dhp), pre_g = rowwise(
        "rwkv_pre_bwd", fn_pre_b, [h0, _halo_before(h0, tms), dr, dlw, dk2, dv, dan, dbn, dg],
        [e, et] + pre_vec + pre_w, [(D_MODEL, F32)] * 2,
        [_acc((1, D_MODEL))] * n_vec + [_acc(w.shape) for w in pre_w], tms)
    grads["a_mu"] = jnp.concatenate(pre_g[:6], axis=0)
    for name, val in zip(["a_w0", "a_a0", "a_k_k", "a_k_a", "a_w_r", "a_w_k", "a_w_v", "a_w1", "a_w2", "a_a1",
                          "a_a2", "a_g1", "a_g2"], pre_g[6:]):
        grads[name] = val

    def fn_add(c, i, a, b, d, after):
        return (a + b + _shift_up(d, after, i, lp // tm),), ()

    (dh0,), _ = rowwise("grad_h0", fn_add, [dh0_c, dh0_p, dhp, _halo_after(dhp, tm)], [], [(D_MODEL, F32)], [], tm)
    grads["meta_tokens"] = dh0[PAD_FRONT:TOK0]
    return loss, dh0[TOK0:], grads, early_from_chips


ANY = pl.BlockSpec(memory_space=pl.ANY)
XY_FLIPS = ((0, 1), (1, 0), (1, 1))


def _flip(v, bit):
    return 1 - v if bit else v


def _sem_scratch(n):
    return [pltpu.SemaphoreType.DMA((n,)), pltpu.SemaphoreType.DMA((n,))]


def gather_copies(src, dst, ici_send, ici_recv, d2d_send, d2d_recv):
    npeer = len(XY_FLIPS)
    x, y, c = lax.axis_index("x"), lax.axis_index("y"), lax.axis_index("c")

    def half(ref, k, which):
        h = src[k].shape[0] // 2
        start = which * h
        return ref.at[pl.ds(pl.multiple_of(start, 8) if h % 8 == 0 else start, h)]

    def ici(k, j, slot):
        fx, fy = XY_FLIPS[j]
        return pltpu.make_async_remote_copy(
            src_ref=half(src[k], k, c), dst_ref=half(dst[k].at[slot], k, c), send_sem=ici_send.at[k * npeer + j],
            recv_sem=ici_recv.at[k * npeer + j], device_id=(_flip(x, fx), _flip(y, fy), c), device_id_type=MESH)

    def d2d(k, j, which):
        fx, fy = XY_FLIPS[j]
        landed = half(dst[k].at[2 * _flip(x, fx) + _flip(y, fy)], k, which)
        return pltpu.make_async_remote_copy(
            src_ref=landed, dst_ref=landed, send_sem=d2d_send.at[k * npeer + j], recv_sem=d2d_recv.at[k * npeer + j],
            device_id=(x, y, 1 - c), device_id_type=MESH)

    pairs = [(k, j) for k in range(len(src)) for j in range(npeer)]
    return ([ici(k, j, 2 * x + y) for k, j in pairs],
            [ici(k, j, 2 * _flip(x, XY_FLIPS[j][0]) + _flip(y, XY_FLIPS[j][1])) for k, j in pairs],
            [d2d(k, j, c) for k, j in pairs], [d2d(k, j, 1 - c) for k, j in pairs])


def gather_scratch(n):
    return _sem_scratch(n * len(XY_FLIPS)) * 2


def gathered_shapes(shards):
    return [jax.ShapeDtypeStruct((N_SHARD,) + s.shape, s.dtype) for s in shards]


def fill_own(gathered, shards):
    if not shards:
        return []
    slot = 2 * lax.axis_index("x") + lax.axis_index("y")
    return [lax.dynamic_update_index_in_dim(g, s, slot, 0) for g, s in zip(gathered, shards)]


def all_gather_shards(shards):
    n = len(shards)

    def body(*refs):
        sends, arrivals, forwards, forwarded = gather_copies(refs[:n], refs[n:2 * n], *refs[2 * n:])
        for cp in sends:
            cp.start()
        for landed, onward in zip(arrivals, forwards):
            landed.wait_recv()
            onward.start()
        for cp in forwarded:
            cp.wait_recv()
        for cp in sends + forwards:
            cp.wait_send()

    out = pl.pallas_call(body, name="gather_weights", in_specs=[ANY] * n, out_specs=[ANY] * n,
                         out_shape=gathered_shapes(shards), scratch_shapes=gather_scratch(n))(*shards)
    return fill_own(out, shards)


def placement():
    x, y, c = lax.axis_index("x"), lax.axis_index("y"), lax.axis_index("c")
    me = 2 * x + y
    others = [j + (j >= me).astype(jnp.int32) for j in range(N_SHARD - 1)]
    return jnp.stack([c, me] + others).astype(jnp.int32)


def hosted_gather(shards):
    return (gather_copies, list(shards), gathered_shapes(shards), gather_scratch(len(shards)),
            lambda got: fill_own(got, shards))


def pair_exchange_copies(src, got, send_sems, recv_sems):
    x, y, c = lax.axis_index("x"), lax.axis_index("y"), lax.axis_index("c")

    def copy(k):
        half = src[k].shape[1] // 2
        theirs = src[k].at[:, pl.ds(pl.multiple_of((1 - c) * half, 8), half), :]
        return pltpu.make_async_remote_copy(
            src_ref=theirs, dst_ref=got[k], send_sem=send_sems.at[k], recv_sem=recv_sems.at[k],
            device_id=(x, y, 1 - c), device_id_type=MESH)

    sends = [copy(k) for k in range(len(src))]
    return sends, sends, [], []


def _half_shapes(sources):
    return [jax.ShapeDtypeStruct((s.shape[0], s.shape[1] // 2, s.shape[2]), s.dtype) for s in sources]


def hosted_pair_exchange(sources):
    return (pair_exchange_copies, list(sources), _half_shapes(sources), _sem_scratch(len(sources)), list)


def pair_exchange(name, sources):
    n = len(sources)

    def body(*refs):
        sends, arrivals, _, _ = pair_exchange_copies(refs[:n], refs[n:2 * n], *refs[2 * n:])
        for cp in sends:
            cp.start()
        for cp in arrivals:
            cp.wait_recv()
        for cp in sends:
            cp.wait_send()

    halves = _half_shapes(sources)
    return pl.pallas_call(body, name=name, in_specs=[ANY] * n, out_specs=[ANY] * n,
                          out_shape=halves, scratch_shapes=_sem_scratch(n))(*sources)


def chip_exchange(parts):
    n = len(parts)

    def body(*refs):
        sends, arrivals = chip_exchange_copies(refs[:n], refs[n:2 * n], *refs[2 * n:])
        for cp in sends:
            cp.start()
        for cp in arrivals:
            cp.wait_recv()
        for cp in sends:
            cp.wait_send()

    return pl.pallas_call(
        body, name="grads_chip_exchange", in_specs=[ANY] * n, out_specs=[ANY] * n,
        out_shape=[jax.ShapeDtypeStruct(p.shape, p.dtype) for p in parts],
        scratch_shapes=_sem_scratch(n * len(XY_FLIPS)),
    )(*parts)


def chip_exchange_copies(src, dst, send_sems, recv_sems):
    npeer = len(XY_FLIPS)
    x, y, c = lax.axis_index("x"), lax.axis_index("y"), lax.axis_index("c")
    me = 2 * x + y

    def copy(k, j, sending):
        fx, fy = XY_FLIPS[j]
        px, py = _flip(x, fx), _flip(y, fy)
        peer = 2 * px + py
        return pltpu.make_async_remote_copy(
            src_ref=src[k].at[peer], dst_ref=dst[k].at[me if sending else peer],
            send_sem=send_sems.at[k * npeer + j], recv_sem=recv_sems.at[k * npeer + j],
            device_id=(px, py, c), device_id_type=MESH)

    pairs = [(k, j) for k in range(len(src)) for j in range(npeer)]
    return [copy(k, j, True) for k, j in pairs], [copy(k, j, False) for k, j in pairs]


def sibling_share(halves):
    n = len(halves)

    def body(*refs):
        src, got = refs[:n], refs[n:2 * n]
        send_sems, recv_sems = refs[2 * n:]
        x, y, c = lax.axis_index("x"), lax.axis_index("y"), lax.axis_index("c")
        sends = [pltpu.make_async_remote_copy(
            src_ref=src[k], dst_ref=got[k], send_sem=send_sems.at[k], recv_sem=recv_sems.at[k],
            device_id=(x, y, 1 - c), device_id_type=MESH) for k in range(n)]
        for cp in sends:
            cp.start()
        for cp in sends:
            cp.wait_recv()
        for cp in sends:
            cp.wait_send()

    return pl.pallas_call(
        body, name="grads_sibling_share", in_specs=[ANY] * n, out_specs=[ANY] * n,
        out_shape=[jax.ShapeDtypeStruct(h.shape, h.dtype) for h in halves], scratch_shapes=_sem_scratch(n),
    )(*halves)


ADD_TILE_ELEMS = 512 * 1024


def _row_tile(rows, cols):
    return max(t for t in range(8, rows + 1, 8) if rows % t == 0 and t * cols <= ADD_TILE_ELEMS)


def _prefetch_call(body, name, place, grid, in_specs, out_specs, out_shape, args):
    return pl.pallas_call(
        body, name=name, out_shape=out_shape,
        grid_spec=pltpu.PrefetchScalarGridSpec(num_scalar_prefetch=1, grid=grid, in_specs=in_specs,
                                               out_specs=out_specs),
        compiler_params=pltpu.CompilerParams(dimension_semantics=("arbitrary",) * len(grid),
                                             vmem_limit_bytes=VMEM_LIMIT),
    )(place, *args)


def pair_add(name, place, src, got, dtype):
    n4, half, cols = got.shape
    tile = _row_tile(half, cols)
    nt = half // tile

    def body(pr, a_ref, b_ref, o_ref):
        o_ref[...] = (a_ref[...] + b_ref[...]).astype(o_ref.dtype)

    mine = pl.BlockSpec((None, tile, cols), lambda s, i, pr: (s, pr[0] * nt + i, 0))
    blk = pl.BlockSpec((None, tile, cols), lambda s, i, pr: (s, i, 0))
    return _prefetch_call(body, name, place, (n4, nt), [mine, blk], blk,
                          jax.ShapeDtypeStruct(got.shape, dtype), (src, got))


def chip_add(name, place, part, from_chips):
    _, half, cols = part.shape
    tile = _row_tile(half, cols)

    def body(pr, own_ref, r0_ref, r1_ref, r2_ref, o_ref):
        me = pr[1]
        own, r0, r1, r2 = (r[...].astype(F32) for r in (own_ref, r0_ref, r1_ref, r2_ref))
        t0 = jnp.where(me == 0, own, r0)
        t1 = jnp.where(me == 0, r0, jnp.where(me == 1, own, r1))
        t2 = jnp.where(me <= 1, r1, jnp.where(me == 2, own, r2))
        t3 = jnp.where(me == 3, own, r2)
        o_ref[...] = ((t0 + t1) + t2) + t3

    def slab(j):
        return pl.BlockSpec((None, tile, cols), lambda i, pr: (pr[j], i, 0))

    return _prefetch_call(body, name, place, (half // tile,), [slab(1), slab(2), slab(3), slab(4)],
                          pl.BlockSpec((tile, cols), lambda i, pr: (i, 0)),
                          jax.ShapeDtypeStruct((half, cols), F32), (part, from_chips, from_chips, from_chips))


def pair_adds(tag, place, sources, got, narrow):
    return [pair_add(f"grads_pair_add_{tag}{k}", place, s, g, BF16 if nar else F32)
            for k, (s, g, nar) in enumerate(zip(sources, got, narrow))]


def finish_sums(place, parts, from_chips):
    halves = [chip_add(f"grads_chip_add{k}", place, p, f) for k, (p, f) in enumerate(zip(parts, from_chips))]
    return list(zip(halves, sibling_share(halves)))


ADAM_ROWS = 256


def adamw_update(name, place, halves, w, m, v):
    nsub, rows, cols = w.shape
    half = rows // 2
    tr = ADAM_ROWS if half % ADAM_ROWS == 0 else half
    nth = half // tr

    def body(pr, *refs):
        g_refs, (w_ref, m_ref, v_ref, g_ref, d_ref, nm_ref, nv_ref) = refs[:2 * nsub], refs[2 * nsub:]
        l = pl.program_id(0)
        mine = (pl.program_id(1) // nth) == pr[0]
        g = None
        for s in range(nsub):
            gs = jnp.where(mine, g_refs[2 * s][...], g_refs[2 * s + 1][...])
            g = gs if g is None else jnp.where(l == s, gs, g)
        m2 = ADAM_B1 * m_ref[...] + (1.0 - ADAM_B1) * g
        v2 = ADAM_B2 * v_ref[...] + (1.0 - ADAM_B2) * (g * g)
        m_hat = m2 / (1.0 - ADAM_B1 ** ADAM_STEP)
        v_hat = v2 / (1.0 - ADAM_B2 ** ADAM_STEP)
        g_ref[...] = g
        d_ref[...] = -ADAM_LR * (m_hat / (jnp.sqrt(v_hat) + ADAM_EPS) + ADAM_WD * w_ref[...])
        nm_ref[...] = m2
        nv_ref[...] = v2

    own = pl.BlockSpec((tr, cols), lambda l, i, pr: (jnp.where(i // nth == pr[0], i % nth, 0), 0))
    got = pl.BlockSpec((tr, cols), lambda l, i, pr: (jnp.where(i // nth == pr[0], 0, i % nth), 0))
    blk = pl.BlockSpec((None, tr, cols), lambda l, i, pr: (l, i, 0))
    out = jax.ShapeDtypeStruct((nsub, rows, cols), F32)
    return _prefetch_call(body, name, place, (nsub, rows // tr), [own, got] * nsub + [blk] * 3, [blk] * 4,
                          [out] * 4, [h for pair in halves for h in pair] + [w, m, v])


WEIGHT_NAMES = ("meta_tokens", "a_mu", "a_w_r", "a_w_k", "a_w_v", "a_w_o", "a_w0", "a_w1", "a_w2", "a_a0", "a_a1",
                "a_a2", "a_g1", "a_g2", "a_k_k", "a_k_a", "a_r_k", "a_gn_w", "a_gn_b", "kv_w_k", "kv_w_v", "b_w_q",
                "b_sinks", "b_w_o", "mlp_w_up", "mlp_w_down", "ln_g", "ln_b")
BIG_NAMES = ("a_w_r", "a_w_k", "a_w_v", "a_w_o", "b_w_q", "b_w_o")
EARLY_NAMES, LATE_NAMES = BIG_NAMES[:3], BIG_NAMES[3:]
PACK_MATS = (("kv_w_k", 256), ("kv_w_v", 256), ("a_w1", 64), ("a_a1", 64), ("a_g1", 128), ("a_w2", 64),
             ("a_a2", 64), ("a_g2", 128))
COLUMN_CUT = ("a_w2", "a_a2", "a_g2")
PACK_VECS = (("a_mu", 6), ("a_w0", 1), ("a_a0", 1), ("a_k_k", 1), ("a_k_a", 1), ("a_gn_w", 1), ("a_gn_b", 1),
             ("ln_g", 4), ("ln_b", 4), ("meta_tokens", 16))
PACK_REPL = (("a_r_k", 4), ("b_sinks", 1))
SHARD_W = D_MODEL // N_SHARD


def _tiles(rows):
    return -(-rows // SUBLANES) * SUBLANES


N_MAT_ROWS = sum(_tiles(r) for _, r in PACK_MATS)
N_VEC_ROWS = sum(_tiles(r) for _, r in PACK_VECS)
N_PACK_ROWS = -(-(N_MAT_ROWS + N_VEC_ROWS + sum(_tiles(r) for _, r in PACK_REPL)) // 16) * 16
N_GATHER_VEC_ROWS = -(-N_VEC_ROWS // 16) * 16


def _pad_rows(arr, axis):
    rows = arr.shape[axis]
    pad = [(0, 0)] * arr.ndim
    pad[axis] = (0, _tiles(rows) - rows)
    return jnp.pad(arr, pad) if _tiles(rows) != rows else arr


def _pack_rows(arr):
    if arr.size == N_HEADS:
        arr = jnp.pad(arr.reshape(1, N_HEADS), ((0, 0), (0, SHARD_W - N_HEADS)))
    return _pad_rows(arr.reshape(-1, SHARD_W), 0)


def pack_small(get):
    parts = [_pack_rows(get(name)) for name, _ in PACK_MATS + PACK_VECS + PACK_REPL]
    used = sum(p.shape[0] for p in parts)
    return jnp.concatenate(parts + [jnp.zeros((N_PACK_ROWS - used, SHARD_W), F32)], axis=0)


def unpack_small(pack, shapes):
    out, off = {}, 0
    for name, rows in PACK_MATS + PACK_VECS + PACK_REPL:
        piece = pack[off:off + rows]
        off += _tiles(rows)
        out[name] = piece[:, :N_HEADS].reshape(shapes[name]) if name == "b_sinks" else piece.reshape(shapes[name])
    return out


def whole_weights(big_names, gathered_big, mats, vecs, a_r_k, b_sinks):
    p = {name: g.reshape(D_MODEL, D_MODEL) for name, g in zip(big_names, gathered_big)}
    off = 0
    for name, rows in PACK_MATS:
        piece = mats[:, off:off + rows]
        off += rows
        if name in COLUMN_CUT:
            p[name] = piece.transpose(1, 0, 2).reshape(rows, D_MODEL)
        else:
            p[name] = piece.reshape(D_MODEL, rows)
    v = vecs.transpose(1, 0, 2).reshape(-1, D_MODEL)
    off = 0
    for name, rows in PACK_VECS:
        p[name] = v[off:off + rows]
        off += _tiles(rows)
    for i in range(2):
        for j in range(2):
            p[f"ln_g{i}{j}"] = p["ln_g"][2 * i + j:2 * i + j + 1]
            p[f"ln_b{i}{j}"] = p["ln_b"][2 * i + j:2 * i + j + 1]
    p["a_r_k"] = a_r_k.reshape(1, D_MODEL)
    p["b_sinks"] = b_sinks
    return p


def small_grad_pack(g):
    parts = []
    for name, rows in PACK_MATS:
        if name in COLUMN_CUT:
            parts.append(g[name].reshape(rows, N_SHARD, SHARD_W).transpose(1, 0, 2))
        else:
            parts.append(g[name].reshape(N_SHARD, rows, SHARD_W))
    vecs = {n: g[n] for n in ("a_mu", "a_w0", "a_a0", "a_k_k", "a_k_a", "a_gn_w", "a_gn_b", "meta_tokens")}
    vecs["ln_g"] = jnp.concatenate([g[f"ln_g{i}{j}"] for i in range(2) for j in range(2)], axis=0)
    vecs["ln_b"] = jnp.concatenate([g[f"ln_b{i}{j}"] for i in range(2) for j in range(2)], axis=0)
    for name, rows in PACK_VECS:
        parts.append(_pad_rows(vecs[name].reshape(rows, N_SHARD, SHARD_W).transpose(1, 0, 2), 1))
    r_k = jnp.broadcast_to(g["a_r_k"].reshape(1, -1, SHARD_W), (N_SHARD, D_MODEL // SHARD_W, SHARD_W))
    sinks = jnp.pad(g["b_sinks"].reshape(1, 1, N_HEADS), ((0, 0), (0, 0), (0, SHARD_W - N_HEADS)))
    parts += [_pad_rows(r_k, 1), _pad_rows(jnp.broadcast_to(sinks, (N_SHARD, 1, SHARD_W)), 1)]
    used = sum(p.shape[1] for p in parts)
    parts.append(jnp.zeros((N_SHARD, N_PACK_ROWS - used, SHARD_W), F32))
    return jnp.concatenate(parts, axis=1)


def train_step(vals):
    w = {n: vals[n] for n in WEIGHT_NAMES}
    w_pack = pack_small(lambda n: w[n])
    early = [w[n][0].astype(BF16) for n in EARLY_NAMES]
    early += [w_pack[:N_MAT_ROWS].astype(BF16), w_pack[N_MAT_ROWS:N_MAT_ROWS + N_GATHER_VEC_ROWS]]
    gathered = all_gather_shards(early)
    ne = len(EARLY_NAMES)
    p = whole_weights(EARLY_NAMES, gathered[:ne], gathered[ne], gathered[ne + 1][:, :N_VEC_ROWS], w["a_r_k"],
                      w["b_sinks"])
    nb = len(BIG_NAMES)

    def late_set(big, layer):
        shards = [w[n][0].astype(BF16) for n in big]
        shards += [w["mlp_w_up"][layer].astype(BF16), w["mlp_w_down"][layer].astype(BF16)]

        def weights(got):
            out = {n: x.reshape(D_MODEL, D_MODEL) for n, x in zip(big, got)}
            out[f"mlp_up{layer}"], out[f"mlp_down{layer}"] = got[len(big):]
            return out

        return shards, weights

    late = (late_set((), 1), late_set(LATE_NAMES, 0))

    place = placement()
    ready = {}
    a_names, b_names = BIG_NAMES[:4], BIG_NAMES[4:]

    def early_sources(g):
        return ([g[n].reshape(N_SHARD, SHARD_W, D_MODEL) for n in b_names]
                + [g["mlp_up0"], g["mlp_up1"], g["mlp_down0"], g["mlp_down1"]])

    def early_parts(srcs, got):
        ready["parts"] = pair_adds("early", place, srcs, got, [True] * len(srcs))
        return ready["parts"]

    loss, gx, g, early_from_chips = local_step(vals["x"][0], vals["loss_target"][0], p, late,
                                               (early_sources, early_parts))
    loss = lax.psum(loss, ("x", "y", "c"))
    srcs = [g[n].reshape(N_SHARD, SHARD_W, D_MODEL) for n in a_names] + [small_grad_pack(g)]
    rest = pair_adds("late", place, srcs, pair_exchange("grads_pair_exchange", srcs), [True] * len(a_names) + [False])
    rest_from_chips = chip_exchange(rest)
    na = len(a_names)
    halves = finish_sums(place, rest[:na] + ready["parts"] + rest[na:],
                         list(rest_from_chips[:na]) + list(early_from_chips) + list(rest_from_chips[na:]))

    res = {}
    for k, n in enumerate(BIG_NAMES):
        res[n] = adamw_update("adamw_" + n, place, halves[k:k + 1], w[n], vals["m_" + n], vals["v_" + n])
    for k, n in ((nb, "mlp_w_up"), (nb + 2, "mlp_w_down")):
        res[n] = adamw_update("adamw_" + n, place, halves[k:k + 2], w[n], vals["m_" + n], vals["v_" + n])
    packs = adamw_update("adamw_small", place, halves[-1:], w_pack[None], pack_small(lambda n: vals["m_" + n])[None],
                         pack_small(lambda n: vals["v_" + n])[None])
    shapes = {n: w[n].shape for n in WEIGHT_NAMES}
    small = [unpack_small(pk[0], shapes) for pk in packs]
    outs = [loss, gx[None]]
    for t in range(4):
        outs += [res[n][t] if n in res else small[t][n] for n in WEIGHT_NAMES]
    return tuple(outs)


def kernel(x, meta_tokens, a_mu, a_w_r, a_w_k, a_w_v, a_w_o, a_w0, a_w1, a_w2, a_a0, a_a1, a_a2, a_g1, a_g2, a_k_k,
           a_k_a, a_r_k, a_gn_w, a_gn_b, kv_w_k, kv_w_v, b_w_q, b_sinks, b_w_o, mlp_w_up, mlp_w_down, ln_g, ln_b,
           loss_target, m_meta_tokens, m_a_mu, m_a_w_r, m_a_w_k, m_a_w_v, m_a_w_o, m_a_w0, m_a_w1, m_a_w2, m_a_a0,
           m_a_a1, m_a_a2, m_a_g1, m_a_g2, m_a_k_k, m_a_k_a, m_a_r_k, m_a_gn_w, m_a_gn_b, m_kv_w_k, m_kv_w_v,
           m_b_w_q, m_b_sinks, m_b_w_o, m_mlp_w_up, m_mlp_w_down, m_ln_g, m_ln_b, v_meta_tokens, v_a_mu, v_a_w_r,
           v_a_w_k, v_a_w_v, v_a_w_o, v_a_w0, v_a_w1, v_a_w2, v_a_a0, v_a_a1, v_a_a2, v_a_g1, v_a_g2, v_a_k_k,
           v_a_k_a, v_a_r_k, v_a_gn_w, v_a_gn_b, v_kv_w_k, v_kv_w_v, v_b_w_q, v_b_sinks, v_b_w_o, v_mlp_w_up,
           v_mlp_w_down, v_ln_g, v_ln_b):
    return train_step(dict(locals()))
```

```python
import functools

import numpy as np
import jax
import jax.numpy as jnp
from jax import lax
from jax.experimental import pallas as pl
from jax.experimental.pallas import tpu as pltpu

F32 = jnp.float32
BF16 = jnp.bfloat16

D_MODEL = 1024
N_HEADS = 16
HEAD_DIM = 64
N_HEADS_KV = 4
GROUP = 4
KV_DIM = N_HEADS_KV * HEAD_DIM
N_META = 16
BLOCK = 128
PAD_FRONT = BLOCK - N_META
TOK0 = PAD_FRONT + N_META
N_FF_CHUNK = 4
N_SHARD = 4
GN_EPS = 64e-5
LN_EPS = 1e-5
ROPE_THETA = 10000.0
ALPHA = 4.0 ** 0.25
ADAM_LR, ADAM_B1, ADAM_B2, ADAM_EPS, ADAM_WD, ADAM_STEP = 0.001, 0.9, 0.999, 1e-08, 0.01, 10
SCAN_T = 64
PAIR = 128
KVW = GROUP * HEAD_DIM
VMEM_LIMIT = 60 * 1024 * 1024
MESH = pl.DeviceIdType.MESH


def _dot(a, b, ca, cb):
    return lax.dot_general(a.astype(BF16), b.astype(BF16), (((ca,), (cb,)), ((), ())),
                           preferred_element_type=F32)


@jax.custom_vjp
def mm(a, b):
    return _dot(a, b, 1, 0)


def _mm_fwd(a, b):
    return mm(a, b), b


def _mm_bwd(b, g):
    return _dot(g, b, 1, 1), jnp.zeros_like(b)


mm.defvjp(_mm_fwd, _mm_bwd)


@jax.custom_vjp
def mm_tap(a, b, tap):
    return _dot(a, b, 1, 0)


mm_tap.defvjp(lambda a, b, tap: (_dot(a, b, 1, 0), b), lambda b, g: (_dot(g, b, 1, 1), jnp.zeros_like(b), g))


def tmm(x, w, taps, xs):
    y = mm(x, w) if taps is None else mm_tap(x, w, taps[len(xs)])
    xs.append(x)
    return y


def vjp_taps(core, tap_shapes, args, cot):
    taps = [jnp.zeros(s, F32) for s in tap_shapes]
    _, vjp, xs = jax.vjp(core, taps, *args, has_aux=True)
    out = vjp(cot)
    return out[1:], [_dot(x, g, 0, 0) for x, g in zip(xs, out[0])]


def _split3(x):
    x1 = x.astype(BF16)
    r1 = x - x1.astype(F32)
    x2 = r1.astype(BF16)
    x3 = (r1 - x2.astype(F32)).astype(BF16)
    return x1, x2, x3


def _exact_dot(x, m01, cb=0):
    acc = None
    for piece in _split3(x)[:2]:
        t = lax.dot_general(piece, m01, (((1,), (cb,)), ((), ())), preferred_element_type=F32)
        acc = t if acc is None else acc + t
    return acc


def _head_matrices():
    e = np.zeros((D_MODEL, N_HEADS), np.float32)
    e[np.arange(D_MODEL), np.arange(D_MODEL) // HEAD_DIM] = 1.0
    return jnp.asarray(e, BF16), jnp.asarray(e.T, BF16)


@jax.custom_vjp
def hsum(x, e, et):
    return _exact_dot(x, e)


@jax.custom_vjp
def hbc(s, e, et):
    return _exact_dot(s, et)


hsum.defvjp(lambda x, e, et: (_exact_dot(x, e), (e, et)),
            lambda res, g: (hbc(g, *res), jnp.zeros_like(res[0]), jnp.zeros_like(res[1])))
hbc.defvjp(lambda s, e, et: (_exact_dot(s, et), (e, et)),
           lambda res, g: (hsum(g, *res), jnp.zeros_like(res[0]), jnp.zeros_like(res[1])))


def _sigmoid(u):
    return 0.5 * (jnp.tanh(0.5 * u) + 1.0)


def _softplus(u):
    return jnp.maximum(u, 0.0) + jnp.log(1.0 + jnp.exp(-jnp.abs(u)))


def _layer_norm(z, g, b):
    mu = jnp.mean(z, axis=-1, keepdims=True)
    zc = z - mu
    var = jnp.mean(zc * zc, axis=-1, keepdims=True)
    return zc * lax.rsqrt(var + LN_EPS) * g + b


def _zero_map(nd):
    return lambda c, i: (0,) * nd


def _params():
    return pltpu.CompilerParams(dimension_semantics=("arbitrary", "arbitrary"), vmem_limit_bytes=VMEM_LIMIT)


def rowwise(name, fn, rows, consts, out_rows, out_accs, tm, nc=1, hosted=None):
    lp = rows[0].shape[-2]
    nt = lp // tm
    assert nt * tm == lp, (name, lp, tm)
    copies_fn, hosted_src, hosted_shapes, hosted_scratch, hosted_post = hosted or (None, (), [], [], None)
    ng = len(hosted_src)
    in_specs, args = [], []
    for a in rows:
        if isinstance(a, tuple):
            a, block_rows, block_index = a
            in_specs.append(pl.BlockSpec((block_rows, a.shape[1]),
                                         functools.partial(lambda f, c, i: (f(i), 0), block_index)))
        elif a.ndim == 2:
            in_specs.append(pl.BlockSpec((tm, a.shape[1]), lambda c, i: (i, 0)))
        else:
            in_specs.append(pl.BlockSpec((a.shape[0], tm, a.shape[2]), lambda c, i: (0, i, 0)))
        args.append(a)
    for cst in consts:
        if isinstance(cst, tuple):
            arr, bs, im = cst
            in_specs.append(pl.BlockSpec(bs, im))
        else:
            arr = cst
            in_specs.append(pl.BlockSpec(arr.shape, _zero_map(arr.ndim), pipeline_mode=pl.Buffered(1)))
        args.append(arr)
    out_shape, out_specs, acc_per_chunk = [], [], []
    for spec in out_rows:
        if len(spec) == 4:
            out_shape.append(jax.ShapeDtypeStruct((spec[3], lp, spec[0]), spec[1]))
            out_specs.append(pl.BlockSpec((spec[3], tm, spec[0]), lambda c, i: (0, i, 0)))
        elif len(spec) == 3 and spec[2]:
            out_shape.append(jax.ShapeDtypeStruct((nc, lp, spec[0]), spec[1]))
            out_specs.append(pl.BlockSpec((None, tm, spec[0]), lambda c, i: (c, i, 0)))
        else:
            out_shape.append(jax.ShapeDtypeStruct((lp, spec[0]), spec[1]))
            out_specs.append(pl.BlockSpec((tm, spec[0]), lambda c, i: (i, 0)))
    for spec in out_accs:
        out_shape.append(jax.ShapeDtypeStruct(spec[0], spec[1]))
        if len(spec) == 4:
            out_specs.append(pl.BlockSpec(spec[2], spec[3]))
            acc_per_chunk.append(True)
        else:
            out_specs.append(pl.BlockSpec(spec[0], _zero_map(len(spec[0])), pipeline_mode=pl.Buffered(1)))
            acc_per_chunk.append(False)
    n_in, n_or, n_out = len(args), len(out_rows), len(out_shape)

    def body(*refs):
        c = pl.program_id(0)
        i = pl.program_id(1)
        if ng:
            src, dst = refs[n_in:n_in + ng], refs[n_in + ng + n_out:n_in + 2 * ng + n_out]
            sends, arrivals, forwards, forwarded = copies_fn(src, dst, *refs[n_in + 2 * ng + n_out:])

            @pl.when(jnp.logical_and(c == 0, i == 0))
            def _():
                for cp in sends:
                    cp.start()

        vals = [r[...] for r in refs[:n_in]]
        outs_r, outs_a = fn(c, i, *vals)
        out_refs = refs[n_in + ng:n_in + ng + n_out]
        for ref, val in zip(out_refs[:n_or], outs_r):
            ref[...] = val.astype(ref.dtype)
        for ref, val, per_chunk in zip(out_refs[n_or:], outs_a, acc_per_chunk):
            first = (i == 0) if per_chunk else jnp.logical_and(i == 0, c == 0)

            @pl.when(first)
            def _():
                ref[...] = val.astype(ref.dtype)

            @pl.when(jnp.logical_not(first))
            def _():
                ref[...] += val.astype(ref.dtype)

        if ng:
            @pl.when(jnp.logical_and(c == nc - 1, i == max(nt - 3, 0)))
            def _():
                for k, landed in enumerate(arrivals):
                    landed.wait_recv()
                    if forwards:
                        forwards[k].start()

            @pl.when(jnp.logical_and(c == nc - 1, i == nt - 1))
            def _():
                for cp in forwarded:
                    cp.wait_recv()
                for cp in sends + forwards:
                    cp.wait_send()

    outs = pl.pallas_call(body, name=name, grid=(nc, nt), in_specs=in_specs + [ANY] * ng,
                          out_specs=out_specs + [ANY] * ng, out_shape=out_shape + list(hosted_shapes),
                          scratch_shapes=list(hosted_scratch), compiler_params=_params())(*args, *hosted_src)
    if ng:
        return outs[:n_or], outs[n_or:n_out], hosted_post(outs[n_out:])
    return outs[:n_or], outs[n_or:]


def _row_ids(i, tm):
    return i * tm + lax.broadcasted_iota(jnp.int32, (tm, 1), 0)


SUBLANES = 8


def _halo_before(arr, tm):
    return (arr, SUBLANES, lambda i: jnp.maximum(i * (tm // SUBLANES) - 1, 0))


def _halo_after(arr, tm):
    last = arr.shape[0] // SUBLANES - 1
    return (arr, SUBLANES, lambda i: jnp.minimum((i + 1) * (tm // SUBLANES), last))


def _pick_row(block8, row):
    rows = lax.broadcasted_iota(jnp.int32, block8.shape, 0)
    return jnp.sum(jnp.where(rows == row, block8, 0.0), axis=0, keepdims=True)


def _shift_down(x, before8, i):
    rows = lax.broadcasted_iota(jnp.int32, x.shape, 0)
    top = _pick_row(before8, SUBLANES - 1) * (i > 0).astype(F32)
    return jnp.where(rows == 0, top, pltpu.roll(x, 1, 0))


def _shift_up(x, after8, i, nt):
    rows = lax.broadcasted_iota(jnp.int32, x.shape, 0)
    bottom = _pick_row(after8, 0) * (i < nt - 1).astype(F32)
    return jnp.where(rows == x.shape[0] - 1, bottom, pltpu.roll(x, x.shape[0] - 1, 0))


LORA_DECAY, LORA_AAA, LORA_GATE = 64, 64, 128
PRE_TAPS = (D_MODEL, D_MODEL, D_MODEL, LORA_DECAY, D_MODEL, LORA_AAA, D_MODEL, LORA_GATE, D_MODEL)


def rwkv_pre(e, et, ws, taps, h, hp, mu_r, mu_w, mu_k, mu_v, mu_a, mu_g, w0, a0, k_k, k_a):
    w_r, w_k, w_v, w1, w2, a1, a2, g1, g2 = ws
    xs = []
    xx = hp - h
    r = tmm(h + xx * mu_r, w_r, taps, xs)
    k = tmm(h + xx * mu_k, w_k, taps, xs)
    v = tmm(h + xx * mu_v, w_v, taps, xs)
    wraw = -_softplus(-(w0 + tmm(jnp.tanh(tmm(h + xx * mu_w, w1, taps, xs)), w2, taps, xs))) - 0.5
    lw = -jnp.exp(wraw)
    a = _sigmoid(a0 + tmm(tmm(h + xx * mu_a, a1, taps, xs), a2, taps, xs))
    g = tmm(_sigmoid(tmm(h + xx * mu_g, g1, taps, xs)), g2, taps, xs)
    kk = k * k_k
    ss = hsum(kk * kk, e, et)
    pos = ss > 0.0
    nrm = jnp.where(pos, jnp.sqrt(jnp.where(pos, ss, 1.0)), 0.0)
    kk = kk * hbc(1.0 / jnp.maximum(nrm, 1e-12), e, et)
    k2 = k * (1.0 + (a - 1.0) * k_a)
    return (r, lw, k2, v, -kk, kk * a, g), xs


def rwkv_post(e, et, w_o, taps, y, r, k2, v, g, h0, gn_w, gn_b, rk, lg, lb):
    xs = []
    inv_n = 1.0 / HEAD_DIM
    yc = y - hbc(hsum(y, e, et) * inv_n, e, et)
    yv = hsum(yc * yc, e, et) * inv_n
    yn = yc * hbc(lax.rsqrt(yv + GN_EPS), e, et) * gn_w + gn_b
    bonus = hbc(hsum(r * k2 * rk, e, et), e, et) * v
    mix = tmm((yn + bonus) * g, w_o, taps, xs)
    return _layer_norm(ALPHA * h0 + mix, lg, lb), xs


@jax.custom_vjp
def sq_relu(x):
    r = jnp.maximum(x, 0.0)
    return r * r


sq_relu.defvjp(lambda x: (sq_relu(x), x), lambda x, g: (g * (2.0 * jnp.maximum(x, 0.0)),))


def _rot_half(t):
    n = t.shape[-1]
    lane = lax.broadcasted_iota(jnp.int32, t.shape, t.ndim - 1)
    lo = (lane % HEAD_DIM) < (HEAD_DIM // 2)
    return jnp.where(lo, -pltpu.roll(t, n - HEAD_DIM // 2, t.ndim - 1), pltpu.roll(t, HEAD_DIM // 2, t.ndim - 1))


@jax.custom_vjp
def rot_half(t):
    return _rot_half(t)


rot_half.defvjp(lambda t: (_rot_half(t), None), lambda _, g: (-_rot_half(g),))


def _tile_lanes(t, width):
    return jnp.concatenate([t] * (width // t.shape[-1]), axis=-1)


def qkv_proj(cos, sin, wq, wk, wv, taps, h):
    xs = []
    q = tmm(h, wq, taps, xs)
    k = tmm(h, wk, taps, xs)
    v = tmm(h, wv, taps, xs)
    cq, sq = _tile_lanes(cos, D_MODEL), _tile_lanes(sin, D_MODEL)
    ck, sk = _tile_lanes(cos, KV_DIM), _tile_lanes(sin, KV_DIM)
    return (q * cq + rot_half(q) * sq, k * ck + rot_half(k) * sk, v), xs


def attn_out(w_o, taps, o, h, lg, lb):
    xs = []
    return _layer_norm(ALPHA * h + tmm(o, w_o, taps, xs), lg, lb), xs


def _scan_consts():
    t = SCAN_T
    tri = np.tril(np.ones((t, t), np.float32))
    rows = np.arange(2 * t)
    same = (rows[:, None] // t) == (rows[None, :] // t)
    strict = same & ((rows[None, :] % t) < (rows[:, None] % t))
    incl = same & ((rows[None, :] % t) <= (rows[:, None] % t))
    lane = np.arange(PAIR)
    masks = np.zeros((8, PAIR), np.float32)
    masks[0] = (lane // HEAD_DIM) == 0
    masks[1] = (lane // HEAD_DIM) == 1
    return (jnp.asarray(tri, BF16), jnp.asarray(strict.astype(np.float32)), jnp.asarray(incl.astype(np.float32)),
            jnp.asarray(masks), jnp.asarray(np.eye(2 * t, dtype=np.float32)))


def _scan_dot(a, b, ca, cb):
    return _dot(a, b, ca, cb)


@functools.partial(jax.custom_vjp, nondiff_argnums=(2, 3))
def _dotf(a, b, ca, cb):
    return _scan_dot(a, b, ca, cb)


def _dotf_bwd(ca, cb, res, g):
    a, b = res
    if ca == 1:
        da = _scan_dot(g, b, 1, 1 - cb)
    else:
        da = _scan_dot(b, g, 1 - cb, 1)
    if cb == 0:
        db = _scan_dot(a, g, 1 - ca, 0)
    else:
        db = _scan_dot(g, a, 0, 1 - ca)
    return da, db


_dotf.defvjp(lambda a, b, ca, cb: (_scan_dot(a, b, ca, cb), (a, b)), _dotf_bwd)


def _tri_dot(tri, x, ct):
    acc = None
    for piece in _split3(x):
        t = lax.dot_general(tri, piece, (((ct,), (0,)), ((), ())), preferred_element_type=F32)
        acc = t if acc is None else acc + t
    return acc


@jax.custom_vjp
def _cumsum_rows(tri, x):
    return _tri_dot(tri, x, 1)


_cumsum_rows.defvjp(lambda tri, x: (_tri_dot(tri, x, 1), tri),
                    lambda tri, g: (jnp.zeros_like(tri), _tri_dot(tri, g, 0)))


@jax.custom_vjp
def _unstack2(x):
    t = x.shape[0] // 2
    return x[:t] + x[t:]


_unstack2.defvjp(lambda x: (_unstack2(x), None), lambda _, g: (jnp.concatenate([g, g], axis=0),))


@jax.custom_vjp
def _last_row(x):
    return x[x.shape[0] - 1:, :]


def _last_row_bwd(_, g):
    rows = lax.broadcasted_iota(jnp.int32, (SCAN_T, g.shape[1]), 0)
    return (jnp.where(rows == SCAN_T - 1, jnp.broadcast_to(g, (SCAN_T, g.shape[1])), 0.0),)


_last_row.defvjp(lambda x: (_last_row(x), None), _last_row_bwd)


@jax.custom_vjp
def _halves(x):
    n = x.shape[0] // 2
    return x[:n], x[n:]


_halves.defvjp(lambda x: (_halves(x), None), lambda _, g: (jnp.concatenate(list(g), axis=0),))


@jax.custom_vjp
def _quads(x):
    n, m = x.shape[0] // 2, x.shape[1] // 2
    return x[:n, :m], x[:n, m:], x[n:, :m], x[n:, m:]


_quads.defvjp(lambda x: (_quads(x), None),
              lambda _, g: (jnp.concatenate([jnp.concatenate([g[0], g[1]], axis=1),
                                             jnp.concatenate([g[2], g[3]], axis=1)], axis=0),))


@jax.custom_vjp
def _solve_saved(n, rhs, minv, u):
    return u


def _solve_saved_bwd(res, du):
    minv, u = res
    drhs = _dotf(minv, du, 0, 0)
    return _dotf(drhs, u, 1, 1), drhs, jnp.zeros_like(minv), jnp.zeros_like(u)


_solve_saved.defvjp(lambda n, rhs, minv, u: (u, (minv, u)), _solve_saved_bwd)


def scan_chunk(tri, strict, incl, m0, m1, eye, r, lw, k, v, a, b, s0, saved=None):
    lower = strict > 0
    lower_incl = incl > 0

    def stack(x):
        return jnp.concatenate([x * m0, x * m1], axis=0)

    def dots(xs, ys, ca, cb, mask=None):
        out = [_dotf(x, y, ca, cb) for x, y in zip(xs, ys)]
        return out if mask is None else [jnp.where(mask, o, 0.0) for o in out]

    cl = [_cumsum_rows(tri, x) for x in lw]
    gam = [jnp.exp(c) for c in cl]
    ginv = [jnp.exp(-c) for c in cl]
    ar_s = [jnp.concatenate([stack(x * jnp.exp(c - w)), stack(y * g)], axis=0)
            for x, c, w, y, g in zip(a, cl, lw, r, gam)]
    bk_s = [jnp.concatenate([stack(x * g), stack(y * g)], axis=0) for x, y, g in zip(b, k, ginv)]
    v_s = [stack(x) for x in v]
    quads = [_quads(x) for x in dots(ar_s, bk_s, 1, 1)]
    n_ab = [jnp.where(lower, q[0], 0.0) for q in quads]
    n_ak = [jnp.where(lower, q[1], 0.0) for q in quads]
    r_ab = [jnp.where(lower_incl, q[2], 0.0) for q in quads]
    r_ak = [jnp.where(lower_incl, q[3], 0.0) for q in quads]
    from_state = [_halves(x) for x in dots(ar_s, s0, 1, 1)]
    rhs = [x[0] + y for x, y in zip(from_state, dots(n_ak, v_s, 1, 0))]
    if saved is None:
        minv = [eye + n for n in n_ab]
        p = n_ab
        for _ in range(5):
            p = dots(p, p, 1, 0)
            minv = [m + mp for m, mp in zip(minv, dots(minv, p, 1, 0))]
        u_s = dots(minv, rhs, 1, 0)
    else:
        minv = saved[0]
        u_s = [_solve_saved(n, x, m, u) for n, x, m, u in zip(n_ab, rhs, *saved)]
    uv_s = [jnp.concatenate([x, y], axis=0) for x, y in zip(u_s, v_s)]
    r_uv = [jnp.concatenate([x, y], axis=1) for x, y in zip(r_ab, r_ak)]
    y = [_unstack2(x[1] + z) for x, z in zip(from_state, dots(r_uv, uv_s, 1, 0))]
    g_end = [_last_row(g) for g in gam]
    s1 = [s * g + x for s, g, x in zip(s0, g_end, dots(uv_s, [x * g for x, g in zip(bk_s, g_end)], 0, 0))]
    return y, s1, (minv, u_s)


SCAN_PAIRS = 8


def _scan_specs(consts, order):
    row = pl.BlockSpec((SCAN_T, PAIR * SCAN_PAIRS), lambda p, c: (order(c), p))
    state = pl.BlockSpec((None, SCAN_PAIRS, PAIR, PAIR), lambda p, c: (order(c), p, 0, 0))
    return row, state, [pl.BlockSpec(x.shape, _zero_map(x.ndim)) for x in consts]


def _pair_lanes(q):
    return slice(q * PAIR, (q + 1) * PAIR)


def scan_fwd(r, lw, k, v, a, b, shards=()):
    lp = r.shape[0]
    nch = lp // SCAN_T
    npair = D_MODEL // PAIR
    ng = len(shards)
    consts = _scan_consts()
    row, state, cspecs = _scan_specs(consts, lambda c: c)

    def body(tri, strict, incl, masks, eye, r_ref, lw_ref, k_ref, v_ref, a_ref, b_ref, *rest):
        src, (y_ref, s_ref, minv_ref, u_ref), dst = rest[:ng], rest[ng:ng + 4], rest[ng + 4:2 * ng + 4]
        carry = rest[2 * ng + 4]
        first = jnp.logical_and(pl.program_id(0) == 0, pl.program_id(1) == 0)
        last = jnp.logical_and(pl.program_id(0) == npair // SCAN_PAIRS - 1, pl.program_id(1) == nch - 1)
        if ng:
            sends, arrivals, forwards, forwarded = gather_copies(src, dst, *rest[2 * ng + 5:])

            @pl.when(first)
            def _():
                for cp in sends:
                    cp.start()

            @pl.when(jnp.logical_and(pl.program_id(0) == npair // SCAN_PAIRS - 1, pl.program_id(1) == nch * 3 // 4))
            def _():
                for landed, onward in zip(arrivals, forwards):
                    landed.wait_recv()
                    onward.start()

        @pl.when(pl.program_id(1) == 0)
        def _():
            carry[...] = jnp.zeros_like(carry)

        pairs = range(SCAN_PAIRS)
        s0 = [carry[q] for q in pairs]
        rows = [[ref[:, _pair_lanes(q)] for q in pairs] for ref in (r_ref, lw_ref, k_ref, v_ref, a_ref, b_ref)]
        y, s1, (minv, u) = scan_chunk(tri[...], strict[...], incl[...], masks[0:1, :], masks[1:2, :], eye[...],
                                      *rows, s0)
        for q in pairs:
            s_ref[q] = s0[q]
            minv_ref[q] = minv[q]
            u_ref[q] = u[q]
            y_ref[:, _pair_lanes(q)] = y[q]
            carry[q] = s1[q]

        if ng:
            @pl.when(last)
            def _():
                for cp in forwarded:
                    cp.wait_recv()
                for cp in sends + forwards:
                    cp.wait_send()

    mats = jax.ShapeDtypeStruct((nch, npair, PAIR, PAIR), F32)
    out = pl.pallas_call(
        body, name="rwkv_scan_fwd", grid=(npair // SCAN_PAIRS, nch), in_specs=cspecs + [row] * 6 + [ANY] * ng,
        out_specs=[row, state, state, state] + [ANY] * ng,
        out_shape=[jax.ShapeDtypeStruct((lp, D_MODEL), F32), mats, mats, mats] + gathered_shapes(shards),
        scratch_shapes=[pltpu.VMEM((SCAN_PAIRS, PAIR, PAIR), F32)] + (gather_scratch(ng) if ng else []),
        compiler_params=_params(),
    )(*consts, r, lw, k, v, a, b, *shards)
    return out[:4], fill_own(out[4:], shards)


def scan_bwd(r, lw, k, v, a, b, saved, dy, direct_grads, parts=()):
    lp = r.shape[0]
    nch = lp // SCAN_T
    npair = D_MODEL // PAIR
    consts = _scan_consts()
    row, state, cspecs = _scan_specs(consts, lambda c: nch - 1 - c)

    ng = len(parts)

    def body(tri, strict, incl, masks, eye, r_ref, lw_ref, k_ref, v_ref, a_ref, b_ref, s_ref, minv_ref, u_ref,
             dy_ref, dr_in, dk_in, dv_in, *rest):
        src, (dr_ref, dlw_ref, dk_ref, dv_ref, da_ref, db_ref), dst = rest[:ng], rest[ng:ng + 6], rest[ng + 6:2 * ng + 6]
        carry = rest[2 * ng + 6]
        first = jnp.logical_and(pl.program_id(0) == 0, pl.program_id(1) == 0)
        last = jnp.logical_and(pl.program_id(0) == npair // SCAN_PAIRS - 1, pl.program_id(1) == nch - 1)
        if ng:
            sends, arrivals = chip_exchange_copies(src, dst, *rest[2 * ng + 7:])

            @pl.when(first)
            def _():
                for cp in sends:
                    cp.start()

        @pl.when(pl.program_id(1) == 0)
        def _():
            carry[...] = jnp.zeros_like(carry)

        pairs = range(SCAN_PAIRS)
        kept = ([minv_ref[q] for q in pairs], [u_ref[q] for q in pairs])

        def fn(*args):
            y, s1, _ = scan_chunk(tri[...], strict[...], incl[...], masks[0:1, :], masks[1:2, :], eye[...], *args,
                                  saved=kept)
            return y, s1

        rows = [[ref[:, _pair_lanes(q)] for q in pairs] for ref in (r_ref, lw_ref, k_ref, v_ref, a_ref, b_ref)]
        _, vjp = jax.vjp(fn, *rows, [s_ref[q] for q in pairs])
        grads = vjp(([dy_ref[:, _pair_lanes(q)] for q in pairs], [carry[q] for q in pairs]))
        direct = (dr_in, None, dk_in, dv_in, None, None)
        for q in pairs:
            ln = _pair_lanes(q)
            for ref, g, extra in zip((dr_ref, dlw_ref, dk_ref, dv_ref, da_ref, db_ref), grads[:6], direct):
                ref[:, ln] = g[q] if extra is None else g[q] + extra[:, ln]
            carry[q] = grads[6][q]

        if ng:
            @pl.when(last)
            def _():
                for cp in arrivals:
                    cp.wait_recv()
                for cp in sends:
                    cp.wait_send()

    out = pl.pallas_call(
        body, name="rwkv_scan_bwd", grid=(npair // SCAN_PAIRS, nch),
        in_specs=cspecs + [row] * 6 + [state] * 3 + [row] * 4 + [ANY] * ng, out_specs=[row] * 6 + [ANY] * ng,
        out_shape=[jax.ShapeDtypeStruct((lp, D_MODEL), F32)] * 6 + [jax.ShapeDtypeStruct(p.shape, p.dtype) for p in parts],
        scratch_shapes=[pltpu.VMEM((SCAN_PAIRS, PAIR, PAIR), F32)] + (_sem_scratch(ng * len(XY_FLIPS)) if ng else []),
        compiler_params=_params(),
    )(*consts, r, lw, k, v, a, b, *saved, dy, *direct_grads, *parts)
    return out[:6], out[6:]


def _spread_matrices():
    rep = np.zeros((N_HEADS_KV, KV_DIM, KVW), np.float32)
    for h in range(N_HEADS_KV):
        for g in range(GROUP):
            rep[h, h * HEAD_DIM + np.arange(HEAD_DIM), g * HEAD_DIM + np.arange(HEAD_DIM)] = 1.0
    return jnp.asarray(rep, BF16)


KV_HEADS = range(N_HEADS_KV)


def _attn_operands(q_ref, kp, kc, vp, vc, rep_ref):
    lane = lax.broadcasted_iota(jnp.int32, (1, KVW), 1)
    gmask = [(lane // HEAD_DIM == g).astype(F32) for g in range(GROUP)]
    kk = jnp.concatenate([kp, kc], axis=0)
    vv = jnp.concatenate([vp, vc], axis=0)
    qs = [q_ref[:, h * KVW:(h + 1) * KVW] for h in KV_HEADS]
    q_s = [jnp.concatenate([q * gmask[g] for g in range(GROUP)], axis=0) for q in qs]
    keys = [_dot(kk, rep_ref[h], 1, 0) for h in KV_HEADS]
    vals = [_dot(vv, rep_ref[h], 1, 0) for h in KV_HEADS]
    return gmask, q_s, keys, vals


def _attn_probs(n, q_s, keys, sink_ref):
    qi = lax.broadcasted_iota(jnp.int32, (GROUP * BLOCK, 2 * BLOCK), 0) % BLOCK
    kj = lax.broadcasted_iota(jnp.int32, (GROUP * BLOCK, 2 * BLOCK), 1)
    rel = BLOCK + qi - kj
    valid = (rel >= 0) & (rel < BLOCK) & ((n - 1) * BLOCK + kj >= PAD_FRONT)
    s = [jnp.where(valid, _dot(x, y, 1, 1) * (HEAD_DIM ** -0.5), -1e30) for x, y in zip(q_s, keys)]
    sink_col = [jnp.concatenate([jnp.broadcast_to(sink_ref[h, g:g + 1, 0:1], (BLOCK, 1)) for g in range(GROUP)],
                                axis=0) for h in KV_HEADS]
    m = [jnp.maximum(jnp.max(x, axis=-1, keepdims=True), c) for x, c in zip(s, sink_col)]
    ex = [jnp.exp(x - y) for x, y in zip(s, m)]
    ex_sink = [jnp.exp(c - y) for c, y in zip(sink_col, m)]
    inv = [1.0 / (jnp.sum(x, axis=-1, keepdims=True) + c) for x, c in zip(ex, ex_sink)]
    return [x * y for x, y in zip(ex, inv)], [x * y for x, y in zip(ex_sink, inv)]


def _unstack_groups(x_s, gmask):
    out = None
    for g in range(GROUP):
        t = x_s[g * BLOCK:(g + 1) * BLOCK] * gmask[g]
        out = t if out is None else out + t
    return out


def _attn_specs():
    qspec = pl.BlockSpec((BLOCK, D_MODEL), lambda n: (n, 0))
    cur = pl.BlockSpec((BLOCK, KV_DIM), lambda n: (n, 0))
    prev = pl.BlockSpec((BLOCK, KV_DIM), lambda n: (jnp.maximum(n - 1, 0), 0))
    rep = pl.BlockSpec((N_HEADS_KV, KV_DIM, KVW), lambda n: (0, 0, 0))
    sink = pl.BlockSpec((N_HEADS_KV, 8, PAIR), lambda n: (0, 0, 0))
    return qspec, cur, prev, rep, sink


def _attn_params():
    return pltpu.CompilerParams(dimension_semantics=("arbitrary",), vmem_limit_bytes=VMEM_LIMIT)


def _prob_specs():
    rows = GROUP * BLOCK
    return (pl.BlockSpec((None, N_HEADS_KV, rows, 2 * BLOCK), lambda n: (n, 0, 0, 0)),
            pl.BlockSpec((None, N_HEADS_KV, rows, PAIR), lambda n: (n, 0, 0, 0)))


def attn_fwd(q, k, v, sinks_b):
    lp = q.shape[0]
    nb = lp // BLOCK
    qspec, cur, prev, rep, sink = _attn_specs()

    def body(q_ref, kp_ref, kc_ref, vp_ref, vc_ref, rep_ref, sink_ref, o_ref, p_ref, ps_ref):
        gmask, q_s, keys, vals = _attn_operands(q_ref, kp_ref[...], kc_ref[...], vp_ref[...], vc_ref[...], rep_ref)
        p, p_sink = _attn_probs(pl.program_id(0), q_s, keys, sink_ref)
        o = [_dot(x, y, 1, 0) for x, y in zip(p, vals)]
        for h in KV_HEADS:
            o_ref[:, h * KVW:(h + 1) * KVW] = _unstack_groups(o[h], gmask)
            p_ref[h] = p[h].astype(BF16)
            ps_ref[h] = jnp.broadcast_to(p_sink[h], (GROUP * BLOCK, PAIR))

    return pl.pallas_call(
        body, name="swa_fwd", grid=(nb,), in_specs=[qspec, prev, cur, prev, cur, rep, sink],
        out_specs=[qspec, *_prob_specs()],
        out_shape=[jax.ShapeDtypeStruct((lp, D_MODEL), F32),
                   jax.ShapeDtypeStruct((nb, N_HEADS_KV, GROUP * BLOCK, 2 * BLOCK), BF16),
                   jax.ShapeDtypeStruct((nb, N_HEADS_KV, GROUP * BLOCK, PAIR), F32)],
        compiler_params=_attn_params(),
    )(q, k, k, v, v, _spread_matrices(), sinks_b)


def attn_bwd(q, k, v, probs, do):
    lp = q.shape[0]
    qspec, cur, prev, rep, sink = _attn_specs()

    def body(q_ref, kp_ref, kc_ref, vp_ref, vc_ref, rep_ref, p_ref, ps_ref, do_ref, dq_ref, dkc_ref, dkp_ref, dvc_ref,
             dvp_ref, dsink_ref):
        n = pl.program_id(0)
        gmask, q_s, keys, vals = _attn_operands(q_ref, kp_ref[...], kc_ref[...], vp_ref[...], vc_ref[...], rep_ref)
        p = [p_ref[h].astype(F32) for h in KV_HEADS]
        p_sink = [ps_ref[h][:, 0:1] for h in KV_HEADS]
        do_s = [jnp.concatenate([do_ref[:, h * KVW:(h + 1) * KVW] * gmask[g] for g in range(GROUP)], axis=0)
                for h in KV_HEADS]
        dp = [_dot(x, y, 1, 1) for x, y in zip(do_s, vals)]
        delta = [jnp.sum(x * y, axis=-1, keepdims=True) for x, y in zip(p, dp)]
        ds = [x * (y - z) * (HEAD_DIM ** -0.5) for x, y, z in zip(p, dp, delta)]
        dq = [_dot(x, y, 1, 0) for x, y in zip(ds, keys)]
        dkeys_s = [_dot(x, y, 0, 0) for x, y in zip(ds, q_s)]
        dvals_s = [_dot(x, y, 0, 0) for x, y in zip(p, do_s)]
        dkeys = [_exact_dot(x, rep_ref[h], cb=1) for h, x in enumerate(dkeys_s)]
        dvals = [_exact_dot(x, rep_ref[h], cb=1) for h, x in enumerate(dvals_s)]
        dk_all = (dkeys[0] + dkeys[1]) + (dkeys[2] + dkeys[3])
        dv_all = (dvals[0] + dvals[1]) + (dvals[2] + dvals[3])
        dkp_ref[...] = dk_all[:BLOCK]
        dkc_ref[...] = dk_all[BLOCK:]
        dvp_ref[...] = dv_all[:BLOCK]
        dvc_ref[...] = dv_all[BLOCK:]
        dsinks = []
        for h in KV_HEADS:
            dq_ref[:, h * KVW:(h + 1) * KVW] = _unstack_groups(dq[h], gmask)
            dsk = -(p_sink[h] * delta[h])
            rows = [jnp.broadcast_to(jnp.sum(dsk[g * BLOCK:(g + 1) * BLOCK], axis=0, keepdims=True), (1, PAIR))
                    for g in range(GROUP)]
            dsinks.append(jnp.concatenate(rows + [jnp.zeros((8 - GROUP, PAIR), F32)], axis=0))

        @pl.when(n == 0)
        def _():
            for h in KV_HEADS:
                dsink_ref[h] = dsinks[h]

        @pl.when(n > 0)
        def _():
            for h in KV_HEADS:
                dsink_ref[h] += dsinks[h]

    kv = jax.ShapeDtypeStruct((lp, KV_DIM), F32)
    return pl.pallas_call(
        body, name="swa_bwd", grid=(lp // BLOCK,), in_specs=[qspec, prev, cur, prev, cur, rep, *_prob_specs(), qspec],
        out_specs=[qspec, cur, cur, cur, cur, sink],
        out_shape=[jax.ShapeDtypeStruct((lp, D_MODEL), F32), kv, kv, kv, kv,
                   jax.ShapeDtypeStruct((N_HEADS_KV, 8, PAIR), F32)],
        compiler_params=_attn_params(),
    )(q, k, k, v, v, _spread_matrices(), *probs, do)


def _pick_tm(lp, want):
    for tm in (384, 192, 128, 64):
        if tm <= want and lp % tm == 0:
            return tm
    raise ValueError(lp)


def _acc(shape):
    return (tuple(shape), F32)


def _ff_one(w):
    return (w, (None, D_MODEL, D_MODEL), lambda c, i: (c, 0, 0))


def _mlp_layer_fwd(name, h, wup, wdown, lg, lb, tm):
    def fn(c, i, h, wup, wdown, lg, lb):
        out, pre = None, []
        for s in range(N_FF_CHUNK):
            u = mm(h, wup[s])
            pre.append(u.astype(BF16))
            t = mm(sq_relu(u), wdown[s])
            out = t if out is None else out + t
        z = ALPHA * h + out
        return (_layer_norm(z, lg, lb), z, jnp.stack(pre)), ()

    (h_out, z, pre), _ = rowwise(name, fn, [h], [wup, wdown, lg, lb],
                                 [(D_MODEL, F32), (D_MODEL, F32), (D_MODEL, BF16, False, N_FF_CHUNK)], [], tm)
    return h_out, z, pre


MLP_BWD_TILE = 528


def _mlp_layer_bwd(name, h_in, z, pre, dh_parts, wup, wdown, lg, lb, tm):
    n_parts = len(dh_parts)

    def fn_ln(c, i, z, *rest):
        dh = rest[0]
        for extra in rest[1:n_parts]:
            dh = dh + extra
        _, vjp = jax.vjp(_layer_norm, z, rest[n_parts], rest[n_parts + 1])
        dz, dlg, dlb = vjp(dh)
        return (dz,), (dlg, dlb)

    (dz,), (dlg, dlb) = rowwise(name + "_ln", fn_ln, [z] + list(dh_parts), [lg, lb], [(D_MODEL, F32)],
                                [_acc((1, D_MODEL)), _acc((1, D_MODEL))], tm)

    def fn_mlp(c, i, h, dz, wup, wdown, u):
        r = jnp.maximum(u.astype(F32), 0.0)
        du = _dot(dz, wdown, 1, 1) * (2.0 * r)
        return (_dot(du, wup, 1, 1),), (_dot(h, du, 0, 0), _dot(r * r, dz, 0, 0))

    aspec = ((N_FF_CHUNK, D_MODEL, D_MODEL), F32, (None, D_MODEL, D_MODEL), lambda c, i: (c, 0, 0))
    lp = h_in.shape[0]
    tile = MLP_BWD_TILE if lp % MLP_BWD_TILE == 0 else tm
    pre_chunk = (pre, (None, tile, D_MODEL), lambda c, i: (c, i, 0))
    (dx,), (dwup, dwdown) = rowwise(name + "_mm", fn_mlp, [h_in, dz], [_ff_one(wup), _ff_one(wdown), pre_chunk],
                                    [(D_MODEL, F32, True)], [aspec, aspec], tile, nc=N_FF_CHUNK)
    return dz, dx, dwup, dwdown, dlg, dlb


def _sum_parts(dz, dx):
    out = ALPHA * dz
    for s in range(N_FF_CHUNK):
        out = out + dx[s]
    return out


def local_step(x, loss_target, p, late=None, early_hook=None):
    seq = x.shape[0]
    lp = TOK0 + seq
    tm = _pick_tm(lp, 384)
    tms = _pick_tm(lp, 192)
    e, et = _head_matrices()
    h0 = jnp.concatenate([jnp.zeros((PAD_FRONT, D_MODEL), F32), p["meta_tokens"], x], axis=0)
    pos = jnp.maximum(jnp.arange(lp, dtype=F32) - PAD_FRONT, 0.0)
    inv_freq = 1.0 / (ROPE_THETA ** (jnp.arange(0, HEAD_DIM, 2, dtype=F32) / HEAD_DIM))
    ang = pos[:, None] * inv_freq[None, :]
    cos = jnp.tile(jnp.cos(ang), (1, PAIR // (HEAD_DIM // 2)))
    sin = jnp.tile(jnp.sin(ang), (1, PAIR // (HEAD_DIM // 2)))

    pre_vec = [p["a_mu"][j:j + 1] for j in range(6)] + [p["a_w0"], p["a_a0"], p["a_k_k"], p["a_k_a"]]
    pre_w = [p["a_w_r"], p["a_w_k"], p["a_w_v"], p["a_w1"], p["a_w2"], p["a_a1"], p["a_a2"], p["a_g1"], p["a_g2"]]
    n_vec = len(pre_vec)

    def fn_pre(c, i, h, before, e, et, *ws):
        return rwkv_pre(e, et, ws[n_vec:], None, h, _shift_down(h, before, i), *ws[:n_vec])[0], ()

    (r, lw, k2, v, an, bn, g), _, *pre_gathered = rowwise(
        "rwkv_pre", fn_pre, [h0, _halo_before(h0, tms)], [e, et] + pre_vec + pre_w, [(D_MODEL, F32)] * 7, [], tms,
        hosted=hosted_gather(late[0][0]) if late else None)
    (y, *scan_saved), scan_gathered = scan_fwd(r, lw, k2, v, an, bn, late[1][0] if late else ())
    if late:
        p = {**p, **late[0][1](pre_gathered[0]), **late[1][1](scan_gathered)}

    post_c = [p["a_w_o"], p["a_gn_w"], p["a_gn_b"], p["a_r_k"], p["ln_g00"], p["ln_b00"]]

    def fn_post(c, i, y, r, k2, v, g, h0, e, et, w_o, *vecs):
        return (rwkv_post(e, et, w_o, None, y, r, k2, v, g, h0, *vecs)[0],), ()

    (h1,), _ = rowwise("rwkv_post", fn_post, [y, r, k2, v, g, h0], [e, et] + post_c, [(D_MODEL, F32)], [], tm)
    h2, z2, pre2 = _mlp_layer_fwd("mlp0_fwd", h1, p["mlp_up0"], p["mlp_down0"], p["ln_g01"], p["ln_b01"], tm)

    qkv_w = [p["b_w_q"], p["kv_w_k"], p["kv_w_v"]]

    def fn_qkv(c, i, h, cos, sin, wq, wk, wv):
        return qkv_proj(cos, sin, wq, wk, wv, None, h)[0], ()

    (q, k, vv), _ = rowwise("qkv_proj", fn_qkv, [h2, cos, sin], qkv_w,
                            [(D_MODEL, F32), (KV_DIM, F32), (KV_DIM, F32)], [], tm)
    sinks_b = jnp.broadcast_to(p["b_sinks"].reshape(N_HEADS_KV, GROUP, 1), (N_HEADS_KV, GROUP, PAIR))
    sinks_b = jnp.concatenate([sinks_b, jnp.zeros((N_HEADS_KV, 8 - GROUP, PAIR), F32)], axis=1)
    o, *attn_probs = attn_fwd(q, k, vv, sinks_b)

    ao_c = [p["b_w_o"], p["ln_g10"], p["ln_b10"]]

    def fn_ao(c, i, o, h, w_o, lg, lb):
        return (attn_out(w_o, None, o, h, lg, lb)[0],), ()

    (h3,), _ = rowwise("attn_out", fn_ao, [o, h2], ao_c, [(D_MODEL, F32)], [], tm)
    h4, z4, pre4 = _mlp_layer_fwd("mlp1_fwd", h3, p["mlp_up1"], p["mlp_down1"], p["ln_g11"], p["ln_b11"], tm)

    def fn_loss(c, i, h4, tgt):
        real = (_row_ids(i, TOK0) >= TOK0).astype(F32)
        err = (h4 - tgt) * real
        part = 0.5 * jnp.sum(jnp.sum(err * err, axis=-1, keepdims=True), axis=0, keepdims=True) / D_MODEL
        return (err * (1.0 / D_MODEL),), (jnp.broadcast_to(part, (8, PAIR)),)

    (dh4,), (loss_acc,) = rowwise("loss", fn_loss, [h4, (loss_target, TOK0, lambda i: jnp.maximum(i - 1, 0))], [],
                                  [(D_MODEL, F32)], [_acc((8, PAIR))], TOK0)
    loss = loss_acc[0, 0]

    grads = {}
    dz4, dx4, grads["mlp_up1"], grads["mlp_down1"], grads["ln_g11"], grads["ln_b11"] = _mlp_layer_bwd(
        "mlp1_bwd", h3, z4, pre4, [dh4], p["mlp_up1"], p["mlp_down1"], p["ln_g11"], p["ln_b11"], tm)

    def fn_ao_b(c, i, dz, dx, o, h, w_o, lg, lb):
        (do, dh, dlg, dlb), (dw_o,) = vjp_taps(functools.partial(attn_out, w_o), [(tm, D_MODEL)], [o, h, lg, lb],
                                               _sum_parts(dz, dx))
        return (do, dh), (dw_o, dlg, dlb)

    (do, dh2_a), (grads["b_w_o"], grads["ln_g10"], grads["ln_b10"]) = rowwise(
        "attn_out_bwd", fn_ao_b, [dz4, dx4, o, h2], ao_c, [(D_MODEL, F32)] * 2,
        [_acc((D_MODEL, D_MODEL)), _acc((1, D_MODEL)), _acc((1, D_MODEL))], tm)

    dq, dkc, dkp, dvc, dvp, dsinks = attn_bwd(q, k, vv, attn_probs, do)
    grads["b_sinks"] = dsinks[:, :GROUP, 0].reshape(1, N_HEADS)
    zblk = jnp.zeros((BLOCK, KV_DIM), F32)
    dkp_s = jnp.concatenate([dkp[BLOCK:], zblk], axis=0)
    dvp_s = jnp.concatenate([dvp[BLOCK:], zblk], axis=0)

    def fn_qkv_b(c, i, h, cos, sin, dq, dkc, dkp, dvc, dvp, wq, wk, wv):
        return vjp_taps(functools.partial(qkv_proj, cos, sin, wq, wk, wv),
                        [(tm, D_MODEL), (tm, KV_DIM), (tm, KV_DIM)], [h], (dq, dkc + dkp, dvc + dvp))

    (dh2_q,), (grads["b_w_q"], grads["kv_w_k"], grads["kv_w_v"]) = rowwise(
        "qkv_proj_bwd", fn_qkv_b, [h2, cos, sin, dq, dkc, dkp_s, dvc, dvp_s], qkv_w, [(D_MODEL, F32)],
        [_acc((D_MODEL, D_MODEL)), _acc((D_MODEL, KV_DIM)), _acc((D_MODEL, KV_DIM))], tm)

    dz2, dx2, grads["mlp_up0"], grads["mlp_down0"], grads["ln_g01"], grads["ln_b01"] = _mlp_layer_bwd(
        "mlp0_bwd", h1, z2, pre2, [dh2_a, dh2_q], p["mlp_up0"], p["mlp_down0"], p["ln_g01"], p["ln_b01"], tm)

    def fn_post_b(c, i, dz, dx, y, r, k2, v, g, h0, e, et, w_o, *vecs):
        out, dws = vjp_taps(functools.partial(rwkv_post, e, et, w_o), [(tms, D_MODEL)],
                            [y, r, k2, v, g, h0] + list(vecs), _sum_parts(dz, dx))
        return out[:6], tuple(dws) + tuple(out[6:])

    early_srcs = early_hook[0](grads) if early_hook else ()
    (dy, dr_c, dk_c, dv_c, dg, dh0_c), post_g, *early_got = rowwise(
        "rwkv_post_bwd", fn_post_b, [dz2, dx2, y, r, k2, v, g, h0], [e, et] + post_c, [(D_MODEL, F32)] * 6,
        [_acc((D_MODEL, D_MODEL))] + [_acc((1, D_MODEL))] * 5, tms,
        hosted=hosted_pair_exchange(early_srcs) if early_hook else None)
    for name, val in zip(["a_w_o", "a_gn_w", "a_gn_b", "a_r_k", "ln_g00", "ln_b00"], post_g):
        grads[name] = val

    (dr, dlw, dk2, dv, dan, dbn), early_from_chips = scan_bwd(
        r, lw, k2, v, an, bn, scan_saved, dy, (dr_c, dk_c, dv_c),
        early_hook[1](early_srcs, early_got[0]) if early_hook else ())

    def fn_pre_b(c, i, h, before, dr, dlw, dk2, dv, dan, dbn, dg, e, et, *ws):
        hp = _shift_down(h, before, i)
        real = (_row_ids(i, tms) >= PAD_FRONT).astype(F32)
        cot = tuple(t * real for t in (dr, dlw, dk2, dv, dan, dbn, dg))
        out, dws = vjp_taps(functools.partial(rwkv_pre, e, et, ws[n_vec:]), [(tms, n) for n in PRE_TAPS],
                            [h, hp] + list(ws[:n_vec]), cot)
        return out[:2], tuple(out[2:]) + tuple(dws)

    (dh0_p, dhp), pre_g = rowwise(
        "rwkv_pre_bwd", fn_pre_b, [h0, _halo_before(h0, tms), dr, dlw, dk2, dv, dan, dbn, dg],
        [e, et] + pre_vec + pre_w, [(D_MODEL, F32)] * 2,
        [_acc((1, D_MODEL))] * n_vec + [_acc(w.shape) for w in pre_w], tms)
    grads["a_mu"] = jnp.concatenate(pre_g[:6], axis=0)
    for name, val in zip(["a_w0", "a_a0", "a_k_k", "a_k_a", "a_w_r", "a_w_k", "a_w_v", "a_w1", "a_w2", "a_a1",
                          "a_a2", "a_g1", "a_g2"], pre_g[6:]):
        grads[name] = val

    def fn_add(c, i, a, b, d, after):
        return (a + b + _shift_up(d, after, i, lp // tm),), ()

    (dh0,), _ = rowwise("grad_h0", fn_add, [dh0_c, dh0_p, dhp, _halo_after(dhp, tm)], [], [(D_MODEL, F32)], [], tm)
    grads["meta_tokens"] = dh0[PAD_FRONT:TOK0]
    return loss, dh0[TOK0:], grads, early_from_chips


ANY = pl.BlockSpec(memory_space=pl.ANY)
XY_FLIPS = ((0, 1), (1, 0), (1, 1))


def _flip(v, bit):
    return 1 - v if bit else v


def _sem_scratch(n):
    return [pltpu.SemaphoreType.DMA((n,)), pltpu.SemaphoreType.DMA((n,))]


def gather_copies(src, dst, ici_send, ici_recv, d2d_send, d2d_recv):
    npeer = len(XY_FLIPS)
    x, y, c = lax.axis_index("x"), lax.axis_index("y"), lax.axis_index("c")

    def half(ref, k, which):
        h = src[k].shape[0] // 2
        start = which * h
        return ref.at[pl.ds(pl.multiple_of(start, 8) if h % 8 == 0 else start, h)]

    def ici(k, j, slot):
        fx, fy = XY_FLIPS[j]
        return pltpu.make_async_remote_copy(
            src_ref=half(src[k], k, c), dst_ref=half(dst[k].at[slot], k, c), send_sem=ici_send.at[k * npeer + j],
            recv_sem=ici_recv.at[k * npeer + j], device_id=(_flip(x, fx), _flip(y, fy), c), device_id_type=MESH)

    def d2d(k, j, which):
        fx, fy = XY_FLIPS[j]
        landed = half(dst[k].at[2 * _flip(x, fx) + _flip(y, fy)], k, which)
        return pltpu.make_async_remote_copy(
            src_ref=landed, dst_ref=landed, send_sem=d2d_send.at[k * npeer + j], recv_sem=d2d_recv.at[k * npeer + j],
            device_id=(x, y, 1 - c), device_id_type=MESH)

    pairs = [(k, j) for k in range(len(src)) for j in range(npeer)]
    return ([ici(k, j, 2 * x + y) for k, j in pairs],
            [ici(k, j, 2 * _flip(x, XY_FLIPS[j][0]) + _flip(y, XY_FLIPS[j][1])) for k, j in pairs],
            [d2d(k, j, c) for k, j in pairs], [d2d(k, j, 1 - c) for k, j in pairs])


def gather_scratch(n):
    return _sem_scratch(n * len(XY_FLIPS)) * 2


def gathered_shapes(shards):
    return [jax.ShapeDtypeStruct((N_SHARD,) + s.shape, s.dtype) for s in shards]


def fill_own(gathered, shards):
    if not shards:
        return []
    slot = 2 * lax.axis_index("x") + lax.axis_index("y")
    return [lax.dynamic_update_index_in_dim(g, s, slot, 0) for g, s in zip(gathered, shards)]


def all_gather_shards(shards):
    n = len(shards)

    def body(*refs):
        sends, arrivals, forwards, forwarded = gather_copies(refs[:n], refs[n:2 * n], *refs[2 * n:])
        for cp in sends:
            cp.start()
        for landed, onward in zip(arrivals, forwards):
            landed.wait_recv()
            onward.start()
        for cp in forwarded:
            cp.wait_recv()
        for cp in sends + forwards:
            cp.wait_send()

    out = pl.pallas_call(body, name="gather_weights", in_specs=[ANY] * n, out_specs=[ANY] * n,
                         out_shape=gathered_shapes(shards), scratch_shapes=gather_scratch(n))(*shards)
    return fill_own(out, shards)


def placement():
    x, y, c = lax.axis_index("x"), lax.axis_index("y"), lax.axis_index("c")
    me = 2 * x + y
    others = [j + (j >= me).astype(jnp.int32) for j in range(N_SHARD - 1)]
    return jnp.stack([c, me] + others).astype(jnp.int32)


def hosted_gather(shards):
    return (gather_copies, list(shards), gathered_shapes(shards), gather_scratch(len(shards)),
            lambda got: fill_own(got, shards))


def pair_exchange_copies(src, got, send_sems, recv_sems):
    x, y, c = lax.axis_index("x"), lax.axis_index("y"), lax.axis_index("c")

    def copy(k):
        half = src[k].shape[1] // 2
        theirs = src[k].at[:, pl.ds(pl.multiple_of((1 - c) * half, 8), half), :]
        return pltpu.make_async_remote_copy(
            src_ref=theirs, dst_ref=got[k], send_sem=send_sems.at[k], recv_sem=recv_sems.at[k],
            device_id=(x, y, 1 - c), device_id_type=MESH)

    sends = [copy(k) for k in range(len(src))]
    return sends, sends, [], []


def _half_shapes(sources):
    return [jax.ShapeDtypeStruct((s.shape[0], s.shape[1] // 2, s.shape[2]), s.dtype) for s in sources]


def hosted_pair_exchange(sources):
    return (pair_exchange_copies, list(sources), _half_shapes(sources), _sem_scratch(len(sources)), list)


def pair_exchange(name, sources):
    n = len(sources)

    def body(*refs):
        sends, arrivals, _, _ = pair_exchange_copies(refs[:n], refs[n:2 * n], *refs[2 * n:])
        for cp in sends:
            cp.start()
        for cp in arrivals:
            cp.wait_recv()
        for cp in sends:
            cp.wait_send()

    halves = _half_shapes(sources)
    return pl.pallas_call(body, name=name, in_specs=[ANY] * n, out_specs=[ANY] * n,
                          out_shape=halves, scratch_shapes=_sem_scratch(n))(*sources)


def chip_exchange(parts):
    n = len(parts)

    def body(*refs):
        sends, arrivals = chip_exchange_copies(refs[:n], refs[n:2 * n], *refs[2 * n:])
        for cp in sends:
            cp.start()
        for cp in arrivals:
            cp.wait_recv()
        for cp in sends:
            cp.wait_send()

    return pl.pallas_call(
        body, name="grads_chip_exchange", in_specs=[ANY] * n, out_specs=[ANY] * n,
        out_shape=[jax.ShapeDtypeStruct(p.shape, p.dtype) for p in parts],
        scratch_shapes=_sem_scratch(n * len(XY_FLIPS)),
    )(*parts)


def chip_exchange_copies(src, dst, send_sems, recv_sems):
    npeer = len(XY_FLIPS)
    x, y, c = lax.axis_index("x"), lax.axis_index("y"), lax.axis_index("c")
    me = 2 * x + y

    def copy(k, j, sending):
        fx, fy = XY_FLIPS[j]
        px, py = _flip(x, fx), _flip(y, fy)
        peer = 2 * px + py
        return pltpu.make_async_remote_copy(
            src_ref=src[k].at[peer], dst_ref=dst[k].at[me if sending else peer],
            send_sem=send_sems.at[k * npeer + j], recv_sem=recv_sems.at[k * npeer + j],
            device_id=(px, py, c), device_id_type=MESH)

    pairs = [(k, j) for k in range(len(src)) for j in range(npeer)]
    return [copy(k, j, True) for k, j in pairs], [copy(k, j, False) for k, j in pairs]


def sibling_share(halves):
    n = len(halves)

    def body(*refs):
        src, got = refs[:n], refs[n:2 * n]
        send_sems, recv_sems = refs[2 * n:]
        x, y, c = lax.axis_index("x"), lax.axis_index("y"), lax.axis_index("c")
        sends = [pltpu.make_async_remote_copy(
            src_ref=src[k], dst_ref=got[k], send_sem=send_sems.at[k], recv_sem=recv_sems.at[k],
            device_id=(x, y, 1 - c), device_id_type=MESH) for k in range(n)]
        for cp in sends:
            cp.start()
        for cp in sends:
            cp.wait_recv()
        for cp in sends:
            cp.wait_send()

    return pl.pallas_call(
        body, name="grads_sibling_share", in_specs=[ANY] * n, out_specs=[ANY] * n,
        out_shape=[jax.ShapeDtypeStruct(h.shape, h.dtype) for h in halves], scratch_shapes=_sem_scratch(n),
    )(*halves)


ADD_TILE_ELEMS = 512 * 1024


def _row_tile(rows, cols):
    return max(t for t in range(8, rows + 1, 8) if rows % t == 0 and t * cols <= ADD_TILE_ELEMS)


def _prefetch_call(body, name, place, grid, in_specs, out_specs, out_shape, args):
    return pl.pallas_call(
        body, name=name, out_shape=out_shape,
        grid_spec=pltpu.PrefetchScalarGridSpec(num_scalar_prefetch=1, grid=grid, in_specs=in_specs,
                                               out_specs=out_specs),
        compiler_params=pltpu.CompilerParams(dimension_semantics=("arbitrary",) * len(grid),
                                             vmem_limit_bytes=VMEM_LIMIT),
    )(place, *args)


def pair_add(name, place, src, got, dtype):
    n4, half, cols = got.shape
    tile = _row_tile(half, cols)
    nt = half // tile

    def body(pr, a_ref, b_ref, o_ref):
        o_ref[...] = (a_ref[...] + b_ref[...]).astype(o_ref.dtype)

    mine = pl.BlockSpec((None, tile, cols), lambda s, i, pr: (s, pr[0] * nt + i, 0))
    blk = pl.BlockSpec((None, tile, cols), lambda s, i, pr: (s, i, 0))
    return _prefetch_call(body, name, place, (n4, nt), [mine, blk], blk,
                          jax.ShapeDtypeStruct(got.shape, dtype), (src, got))


def chip_add(name, place, part, from_chips):
    _, half, cols = part.shape
    tile = _row_tile(half, cols)

    def body(pr, own_ref, r0_ref, r1_ref, r2_ref, o_ref):
        me = pr[1]
        own, r0, r1, r2 = (r[...].astype(F32) for r in (own_ref, r0_ref, r1_ref, r2_ref))
        t0 = jnp.where(me == 0, own, r0)
        t1 = jnp.where(me == 0, r0, jnp.where(me == 1, own, r1))
        t2 = jnp.where(me <= 1, r1, jnp.where(me == 2, own, r2))
        t3 = jnp.where(me == 3, own, r2)
        o_ref[...] = ((t0 + t1) + t2) + t3

    def slab(j):
        return pl.BlockSpec((None, tile, cols), lambda i, pr: (pr[j], i, 0))

    return _prefetch_call(body, name, place, (half // tile,), [slab(1), slab(2), slab(3), slab(4)],
                          pl.BlockSpec((tile, cols), lambda i, pr: (i, 0)),
                          jax.ShapeDtypeStruct((half, cols), F32), (part, from_chips, from_chips, from_chips))


def pair_adds(tag, place, sources, got, narrow):
    return [pair_add(f"grads_pair_add_{tag}{k}", place, s, g, BF16 if nar else F32)
            for k, (s, g, nar) in enumerate(zip(sources, got, narrow))]


def finish_sums(place, parts, from_chips):
    halves = [chip_add(f"grads_chip_add{k}", place, p, f) for k, (p, f) in enumerate(zip(parts, from_chips))]
    return list(zip(halves, sibling_share(halves)))


ADAM_ROWS = 256


def adamw_update(name, place, halves, w, m, v):
    nsub, rows, cols = w.shape
    half = rows // 2
    tr = ADAM_ROWS if half % ADAM_ROWS == 0 else half
    nth = half // tr

    def body(pr, *refs):
        g_refs, (w_ref, m_ref, v_ref, g_ref, d_ref, nm_ref, nv_ref) = refs[:2 * nsub], refs[2 * nsub:]
        l = pl.program_id(0)
        mine = (pl.program_id(1) // nth) == pr[0]
        g = None
        for s in range(nsub):
            gs = jnp.where(mine, g_refs[2 * s][...], g_refs[2 * s + 1][...])
            g = gs if g is None else jnp.where(l == s, gs, g)
        m2 = ADAM_B1 * m_ref[...] + (1.0 - ADAM_B1) * g
        v2 = ADAM_B2 * v_ref[...] + (1.0 - ADAM_B2) * (g * g)
        m_hat = m2 / (1.0 - ADAM_B1 ** ADAM_STEP)
        v_hat = v2 / (1.0 - ADAM_B2 ** ADAM_STEP)
        g_ref[...] = g
        d_ref[...] = -ADAM_LR * (m_hat / (jnp.sqrt(v_hat) + ADAM_EPS) + ADAM_WD * w_ref[...])
        nm_ref[...] = m2
        nv_ref[...] = v2

    own = pl.BlockSpec((tr, cols), lambda l, i, pr: (jnp.where(i // nth == pr[0], i % nth, 0), 0))
    got = pl.BlockSpec((tr, cols), lambda l, i, pr: (jnp.where(i // nth == pr[0], 0, i % nth), 0))
    blk = pl.BlockSpec((None, tr, cols), lambda l, i, pr: (l, i, 0))
    out = jax.ShapeDtypeStruct((nsub, rows, cols), F32)
    return _prefetch_call(body, name, place, (nsub, rows // tr), [own, got] * nsub + [blk] * 3, [blk] * 4,
                          [out] * 4, [h for pair in halves for h in pair] + [w, m, v])


WEIGHT_NAMES = ("meta_tokens", "a_mu", "a_w_r", "a_w_k", "a_w_v", "a_w_o", "a_w0", "a_w1", "a_w2", "a_a0", "a_a1",
                "a_a2", "a_g1", "a_g2", "a_k_k", "a_k_a", "a_r_k", "a_gn_w", "a_gn_b", "kv_w_k", "kv_w_v", "b_w_q",
                "b_sinks", "b_w_o", "mlp_w_up", "mlp_w_down", "ln_g", "ln_b")
BIG_NAMES = ("a_w_r", "a_w_k", "a_w_v", "a_w_o", "b_w_q", "b_w_o")
EARLY_NAMES, LATE_NAMES = BIG_NAMES[:3], BIG_NAMES[3:]
PACK_MATS = (("kv_w_k", 256), ("kv_w_v", 256), ("a_w1", 64), ("a_a1", 64), ("a_g1", 128), ("a_w2", 64),
             ("a_a2", 64), ("a_g2", 128))
COLUMN_CUT = ("a_w2", "a_a2", "a_g2")
PACK_VECS = (("a_mu", 6), ("a_w0", 1), ("a_a0", 1), ("a_k_k", 1), ("a_k_a", 1), ("a_gn_w", 1), ("a_gn_b", 1),
             ("ln_g", 4), ("ln_b", 4), ("meta_tokens", 16))
PACK_REPL = (("a_r_k", 4), ("b_sinks", 1))
SHARD_W = D_MODEL // N_SHARD


def _tiles(rows):
    return -(-rows // SUBLANES) * SUBLANES


N_MAT_ROWS = sum(_tiles(r) for _, r in PACK_MATS)
N_VEC_ROWS = sum(_tiles(r) for _, r in PACK_VECS)
N_PACK_ROWS = -(-(N_MAT_ROWS + N_VEC_ROWS + sum(_tiles(r) for _, r in PACK_REPL)) // 16) * 16
N_GATHER_VEC_ROWS = -(-N_VEC_ROWS // 16) * 16


def _pad_rows(arr, axis):
    rows = arr.shape[axis]
    pad = [(0, 0)] * arr.ndim
    pad[axis] = (0, _tiles(rows) - rows)
    return jnp.pad(arr, pad) if _tiles(rows) != rows else arr


def _pack_rows(arr):
    if arr.size == N_HEADS:
        arr = jnp.pad(arr.reshape(1, N_HEADS), ((0, 0), (0, SHARD_W - N_HEADS)))
    return _pad_rows(arr.reshape(-1, SHARD_W), 0)


def pack_small(get):
    parts = [_pack_rows(get(name)) for name, _ in PACK_MATS + PACK_VECS + PACK_REPL]
    used = sum(p.shape[0] for p in parts)
    return jnp.concatenate(parts + [jnp.zeros((N_PACK_ROWS - used, SHARD_W), F32)], axis=0)


def unpack_small(pack, shapes):
    out, off = {}, 0
    for name, rows in PACK_MATS + PACK_VECS + PACK_REPL:
        piece = pack[off:off + rows]
        off += _tiles(rows)
        out[name] = piece[:, :N_HEADS].reshape(shapes[name]) if name == "b_sinks" else piece.reshape(shapes[name])
    return out


def whole_weights(big_names, gathered_big, mats, vecs, a_r_k, b_sinks):
    p = {name: g.reshape(D_MODEL, D_MODEL) for name, g in zip(big_names, gathered_big)}
    off = 0
    for name, rows in PACK_MATS:
        piece = mats[:, off:off + rows]
        off += rows
        if name in COLUMN_CUT:
            p[name] = piece.transpose(1, 0, 2).reshape(rows, D_MODEL)
        else:
            p[name] = piece.reshape(D_MODEL, rows)
    v = vecs.transpose(1, 0, 2).reshape(-1, D_MODEL)
    off = 0
    for name, rows in PACK_VECS:
        p[name] = v[off:off + rows]
        off += _tiles(rows)
    for i in range(2):
        for j in range(2):
            p[f"ln_g{i}{j}"] = p["ln_g"][2 * i + j:2 * i + j + 1]
            p[f"ln_b{i}{j}"] = p["ln_b"][2 * i + j:2 * i + j + 1]
    p["a_r_k"] = a_r_k.reshape(1, D_MODEL)
    p["b_sinks"] = b_sinks
    return p


def small_grad_pack(g):
    parts = []
    for name, rows in PACK_MATS:
        if name in COLUMN_CUT:
            parts.append(g[name].reshape(rows, N_SHARD, SHARD_W).transpose(1, 0, 2))
        else:
            parts.append(g[name].reshape(N_SHARD, rows, SHARD_W))
    vecs = {n: g[n] for n in ("a_mu", "a_w0", "a_a0", "a_k_k", "a_k_a", "a_gn_w", "a_gn_b", "meta_tokens")}
    vecs["ln_g"] = jnp.concatenate([g[f"ln_g{i}{j}"] for i in range(2) for j in range(2)], axis=0)
    vecs["ln_b"] = jnp.concatenate([g[f"ln_b{i}{j}"] for i in range(2) for j in range(2)], axis=0)
    for name, rows in PACK_VECS:
        parts.append(_pad_rows(vecs[name].reshape(rows, N_SHARD, SHARD_W).transpose(1, 0, 2), 1))
    r_k = jnp.broadcast_to(g["a_r_k"].reshape(1, -1, SHARD_W), (N_SHARD, D_MODEL // SHARD_W, SHARD_W))
    sinks = jnp.pad(g["b_sinks"].reshape(1, 1, N_HEADS), ((0, 0), (0, 0), (0, SHARD_W - N_HEADS)))
    parts += [_pad_rows(r_k, 1), _pad_rows(jnp.broadcast_to(sinks, (N_SHARD, 1, SHARD_W)), 1)]
    used = sum(p.shape[1] for p in parts)
    parts.append(jnp.zeros((N_SHARD, N_PACK_ROWS - used, SHARD_W), F32))
    return jnp.concatenate(parts, axis=1)


def train_step(vals):
    w = {n: vals[n] for n in WEIGHT_NAMES}
    w_pack = pack_small(lambda n: w[n])
    early = [w[n][0].astype(BF16) for n in EARLY_NAMES]
    early += [w_pack[:N_MAT_ROWS].astype(BF16), w_pack[N_MAT_ROWS:N_MAT_ROWS + N_GATHER_VEC_ROWS]]
    gathered = all_gather_shards(early)
    ne = len(EARLY_NAMES)
    p = whole_weights(EARLY_NAMES, gathered[:ne], gathered[ne], gathered[ne + 1][:, :N_VEC_ROWS], w["a_r_k"],
                      w["b_sinks"])
    nb = len(BIG_NAMES)

    def late_set(big, layer):
        shards = [w[n][0].astype(BF16) for n in big]
        shards += [w["mlp_w_up"][layer].astype(BF16), w["mlp_w_down"][layer].astype(BF16)]

        def weights(got):
            out = {n: x.reshape(D_MODEL, D_MODEL) for n, x in zip(big, got)}
            out[f"mlp_up{layer}"], out[f"mlp_down{layer}"] = got[len(big):]
            return out

        return shards, weights

    late = (late_set((), 1), late_set(LATE_NAMES, 0))

    place = placement()
    ready = {}
    a_names, b_names = BIG_NAMES[:4], BIG_NAMES[4:]

    def early_sources(g):
        return ([g[n].reshape(N_SHARD, SHARD_W, D_MODEL) for n in b_names]
                + [g["mlp_up0"], g["mlp_up1"], g["mlp_down0"], g["mlp_down1"]])

    def early_parts(srcs, got):
        ready["parts"] = pair_adds("early", place, srcs, got, [True] * len(srcs))
        return ready["parts"]

    loss, gx, g, early_from_chips = local_step(vals["x"][0], vals["loss_target"][0], p, late,
                                               (early_sources, early_parts))
    loss = lax.psum(loss, ("x", "y", "c"))
    srcs = [g[n].reshape(N_SHARD, SHARD_W, D_MODEL) for n in a_names] + [small_grad_pack(g)]
    rest = pair_adds("late", place, srcs, pair_exchange("grads_pair_exchange", srcs), [True] * len(a_names) + [False])
    rest_from_chips = chip_exchange(rest)
    na = len(a_names)
    halves = finish_sums(place, rest[:na] + ready["parts"] + rest[na:],
                         list(rest_from_chips[:na]) + list(early_from_chips) + list(rest_from_chips[na:]))

    res = {}
    for k, n in enumerate(BIG_NAMES):
        res[n] = adamw_update("adamw_" + n, place, halves[k:k + 1], w[n], vals["m_" + n], vals["v_" + n])
    for k, n in ((nb, "mlp_w_up"), (nb + 2, "mlp_w_down")):
        res[n] = adamw_update("adamw_" + n, place, halves[k:k + 2], w[n], vals["m_" + n], vals["v_" + n])
    packs = adamw_update("adamw_small", place, halves[-1:], w_pack[None], pack_small(lambda n: vals["m_" + n])[None],
                         pack_small(lambda n: vals["v_" + n])[None])
    shapes = {n: w[n].shape for n in WEIGHT_NAMES}
    small = [unpack_small(pk[0], shapes) for pk in packs]
    outs = [loss, gx[None]]
    for t in range(4):
        outs += [res[n][t] if n in res else small[t][n] for n in WEIGHT_NAMES]
    return tuple(outs)


def kernel(x, meta_tokens, a_mu, a_w_r, a_w_k, a_w_v, a_w_o, a_w0, a_w1, a_w2, a_a0, a_a1, a_a2, a_g1, a_g2, a_k_k,
           a_k_a, a_r_k, a_gn_w, a_gn_b, kv_w_k, kv_w_v, b_w_q, b_sinks, b_w_o, mlp_w_up, mlp_w_down, ln_g, ln_b,
           loss_target, m_meta_tokens, m_a_mu, m_a_w_r, m_a_w_k, m_a_w_v, m_a_w_o, m_a_w0, m_a_w1, m_a_w2, m_a_a0,
           m_a_a1, m_a_a2, m_a_g1, m_a_g2, m_a_k_k, m_a_k_a, m_a_r_k, m_a_gn_w, m_a_gn_b, m_kv_w_k, m_kv_w_v,
           m_b_w_q, m_b_sinks, m_b_w_o, m_mlp_w_up, m_mlp_w_down, m_ln_g, m_ln_b, v_meta_tokens, v_a_mu, v_a_w_r,
           v_a_w_k, v_a_w_v, v_a_w_o, v_a_w0, v_a_w1, v_a_w2, v_a_a0, v_a_a1, v_a_a2, v_a_g1, v_a_g2, v_a_k_k,
           v_a_k_a, v_a_r_k, v_a_gn_w, v_a_gn_b, v_kv_w_k, v_kv_w_v, v_b_w_q, v_b_sinks, v_b_w_o, v_mlp_w_up,
           v_mlp_w_down, v_ln_g, v_ln_b):
    return train_step(dict(locals()))
```

```python
import functools

import numpy as np
import jax
import jax.numpy as jnp
from jax import lax
from jax.experimental import pallas as pl
from jax.experimental.pallas import tpu as pltpu

F32 = jnp.float32
BF16 = jnp.bfloat16

D_MODEL = 1024
N_HEADS = 16
HEAD_DIM = 64
N_HEADS_KV = 4
GROUP = 4
KV_DIM = N_HEADS_KV * HEAD_DIM
N_META = 16
BLOCK = 128
PAD_FRONT = BLOCK - N_META
TOK0 = PAD_FRONT + N_META
N_FF_CHUNK = 4
N_SHARD = 4
GN_EPS = 64e-5
LN_EPS = 1e-5
ROPE_THETA = 10000.0
ALPHA = 4.0 ** 0.25
ADAM_LR, ADAM_B1, ADAM_B2, ADAM_EPS, ADAM_WD, ADAM_STEP = 0.001, 0.9, 0.999, 1e-08, 0.01, 10
SCAN_T = 64
PAIR = 128
KVW = GROUP * HEAD_DIM
VMEM_LIMIT = 60 * 1024 * 1024
MESH = pl.DeviceIdType.MESH


def _dot(a, b, ca, cb):
    return lax.dot_general(a.astype(BF16), b.astype(BF16), (((ca,), (cb,)), ((), ())),
                           preferred_element_type=F32)


@jax.custom_vjp
def mm(a, b):
    return _dot(a, b, 1, 0)


def _mm_fwd(a, b):
    return mm(a, b), b


def _mm_bwd(b, g):
    return _dot(g, b, 1, 1), jnp.zeros_like(b)


mm.defvjp(_mm_fwd, _mm_bwd)


@jax.custom_vjp
def mm_tap(a, b, tap):
    return _dot(a, b, 1, 0)


mm_tap.defvjp(lambda a, b, tap: (_dot(a, b, 1, 0), b), lambda b, g: (_dot(g, b, 1, 1), jnp.zeros_like(b), g))


def tmm(x, w, taps, xs):
    y = mm(x, w) if taps is None else mm_tap(x, w, taps[len(xs)])
    xs.append(x)
    return y


def vjp_taps(core, tap_shapes, args, cot):
    taps = [jnp.zeros(s, F32) for s in tap_shapes]
    _, vjp, xs = jax.vjp(core, taps, *args, has_aux=True)
    out = vjp(cot)
    return out[1:], [_dot(x, g, 0, 0) for x, g in zip(xs, out[0])]


def _split3(x):
    x1 = x.astype(BF16)
    r1 = x - x1.astype(F32)
    x2 = r1.astype(BF16)
    x3 = (r1 - x2.astype(F32)).astype(BF16)
    return x1, x2, x3


def _exact_dot(x, m01, cb=0):
    acc = None
    for piece in _split3(x)[:2]:
        t = lax.dot_general(piece, m01, (((1,), (cb,)), ((), ())), preferred_element_type=F32)
        acc = t if acc is None else acc + t
    return acc


def _head_matrices():
    e = np.zeros((D_MODEL, N_HEADS), np.float32)
    e[np.arange(D_MODEL), np.arange(D_MODEL) // HEAD_DIM] = 1.0
    return jnp.asarray(e, BF16), jnp.asarray(e.T, BF16)


@jax.custom_vjp
def hsum(x, e, et):
    return _exact_dot(x, e)


@jax.custom_vjp
def hbc(s, e, et):
    return _exact_dot(s, et)


hsum.defvjp(lambda x, e, et: (_exact_dot(x, e), (e, et)),
            lambda res, g: (hbc(g, *res), jnp.zeros_like(res[0]), jnp.zeros_like(res[1])))
hbc.defvjp(lambda s, e, et: (_exact_dot(s, et), (e, et)),
           lambda res, g: (hsum(g, *res), jnp.zeros_like(res[0]), jnp.zeros_like(res[1])))


def _sigmoid(u):
    return 0.5 * (jnp.tanh(0.5 * u) + 1.0)


def _softplus(u):
    return jnp.maximum(u, 0.0) + jnp.log(1.0 + jnp.exp(-jnp.abs(u)))


def _layer_norm(z, g, b):
    mu = jnp.mean(z, axis=-1, keepdims=True)
    zc = z - mu
    var = jnp.mean(zc * zc, axis=-1, keepdims=True)
    return zc * lax.rsqrt(var + LN_EPS) * g + b


def _zero_map(nd):
    return lambda c, i: (0,) * nd


def _params():
    return pltpu.CompilerParams(dimension_semantics=("arbitrary", "arbitrary"), vmem_limit_bytes=VMEM_LIMIT)


def rowwise(name, fn, rows, consts, out_rows, out_accs, tm, nc=1, hosted=None):
    lp = rows[0].shape[-2]
    nt = lp // tm
    assert nt * tm == lp, (name, lp, tm)
    copies_fn, hosted_src, hosted_shapes, hosted_scratch, hosted_post = hosted or (None, (), [], [], None)
    ng = len(hosted_src)
    in_specs, args = [], []
    for a in rows:
        if isinstance(a, tuple):
            a, block_rows, block_index = a
            in_specs.append(pl.BlockSpec((block_rows, a.shape[1]),
                                         functools.partial(lambda f, c, i: (f(i), 0), block_index)))
        elif a.ndim == 2:
            in_specs.append(pl.BlockSpec((tm, a.shape[1]), lambda c, i: (i, 0)))
        else:
            in_specs.append(pl.BlockSpec((a.shape[0], tm, a.shape[2]), lambda c, i: (0, i, 0)))
        args.append(a)
    for cst in consts:
        if isinstance(cst, tuple):
            arr, bs, im = cst
            in_specs.append(pl.BlockSpec(bs, im))
        else:
            arr = cst
            in_specs.append(pl.BlockSpec(arr.shape, _zero_map(arr.ndim), pipeline_mode=pl.Buffered(1)))
        args.append(arr)
    out_shape, out_specs, acc_per_chunk = [], [], []
    for spec in out_rows:
        if len(spec) == 4:
            out_shape.append(jax.ShapeDtypeStruct((spec[3], lp, spec[0]), spec[1]))
            out_specs.append(pl.BlockSpec((spec[3], tm, spec[0]), lambda c, i: (0, i, 0)))
        elif len(spec) == 3 and spec[2]:
            out_shape.append(jax.ShapeDtypeStruct((nc, lp, spec[0]), spec[1]))
            out_specs.append(pl.BlockSpec((None, tm, spec[0]), lambda c, i: (c, i, 0)))
        else:
            out_shape.append(jax.ShapeDtypeStruct((lp, spec[0]), spec[1]))
            out_specs.append(pl.BlockSpec((tm, spec[0]), lambda c, i: (i, 0)))
    for spec in out_accs:
        out_shape.append(jax.ShapeDtypeStruct(spec[0], spec[1]))
        if len(spec) == 4:
            out_specs.append(pl.BlockSpec(spec[2], spec[3]))
            acc_per_chunk.append(True)
        else:
            out_specs.append(pl.BlockSpec(spec[0], _zero_map(len(spec[0])), pipeline_mode=pl.Buffered(1)))
            acc_per_chunk.append(False)
    n_in, n_or, n_out = len(args), len(out_rows), len(out_shape)

    def body(*refs):
        c = pl.program_id(0)
        i = pl.program_id(1)
        if ng:
            src, dst = refs[n_in:n_in + ng], refs[n_in + ng + n_out:n_in + 2 * ng + n_out]
            sends, arrivals, forwards, forwarded = copies_fn(src, dst, *refs[n_in + 2 * ng + n_out:])

            @pl.when(jnp.logical_and(c == 0, i == 0))
            def _():
                for cp in sends:
                    cp.start()

        vals = [r[...] for r in refs[:n_in]]
        outs_r, outs_a = fn(c, i, *vals)
        out_refs = refs[n_in + ng:n_in + ng + n_out]
        for ref, val in zip(out_refs[:n_or], outs_r):
            ref[...] = val.astype(ref.dtype)
        for ref, val, per_chunk in zip(out_refs[n_or:], outs_a, acc_per_chunk):
            first = (i == 0) if per_chunk else jnp.logical_and(i == 0, c == 0)

            @pl.when(first)
            def _():
                ref[...] = val.astype(ref.dtype)

            @pl.when(jnp.logical_not(first))
            def _():
                ref[...] += val.astype(ref.dtype)

        if ng:
            @pl.when(jnp.logical_and(c == nc - 1, i == max(nt - 3, 0)))
            def _():
                for k, landed in enumerate(arrivals):
                    landed.wait_recv()
                    if forwards:
                        forwards[k].start()

            @pl.when(jnp.logical_and(c == nc - 1, i == nt - 1))
            def _():
                for cp in forwarded:
                    cp.wait_recv()
                for cp in sends + forwards:
                    cp.wait_send()

    outs = pl.pallas_call(body, name=name, grid=(nc, nt), in_specs=in_specs + [ANY] * ng,
                          out_specs=out_specs + [ANY] * ng, out_shape=out_shape + list(hosted_shapes),
                          scratch_shapes=list(hosted_scratch), compiler_params=_params())(*args, *hosted_src)
    if ng:
        return outs[:n_or], outs[n_or:n_out], hosted_post(outs[n_out:])
    return outs[:n_or], outs[n_or:]


def _row_ids(i, tm):
    return i * tm + lax.broadcasted_iota(jnp.int32, (tm, 1), 0)


SUBLANES = 8


def _halo_before(arr, tm):
    return (arr, SUBLANES, lambda i: jnp.maximum(i * (tm // SUBLANES) - 1, 0))


def _halo_after(arr, tm):
    last = arr.shape[0] // SUBLANES - 1
    return (arr, SUBLANES, lambda i: jnp.minimum((i + 1) * (tm // SUBLANES), last))


def _pick_row(block8, row):
    rows = lax.broadcasted_iota(jnp.int32, block8.shape, 0)
    return jnp.sum(jnp.where(rows == row, block8, 0.0), axis=0, keepdims=True)


def _shift_down(x, before8, i):
    rows = lax.broadcasted_iota(jnp.int32, x.shape, 0)
    top = _pick_row(before8, SUBLANES - 1) * (i > 0).astype(F32)
    return jnp.where(rows == 0, top, pltpu.roll(x, 1, 0))


def _shift_up(x, after8, i, nt):
    rows = lax.broadcasted_iota(jnp.int32, x.shape, 0)
    bottom = _pick_row(after8, 0) * (i < nt - 1).astype(F32)
    return jnp.where(rows == x.shape[0] - 1, bottom, pltpu.roll(x, x.shape[0] - 1, 0))


LORA_DECAY, LORA_AAA, LORA_GATE = 64, 64, 128
PRE_TAPS = (D_MODEL, D_MODEL, D_MODEL, LORA_DECAY, D_MODEL, LORA_AAA, D_MODEL, LORA_GATE, D_MODEL)


def rwkv_pre(e, et, ws, taps, h, hp, mu_r, mu_w, mu_k, mu_v, mu_a, mu_g, w0, a0, k_k, k_a):
    w_r, w_k, w_v, w1, w2, a1, a2, g1, g2 = ws
    xs = []
    xx = hp - h
    r = tmm(h + xx * mu_r, w_r, taps, xs)
    k = tmm(h + xx * mu_k, w_k, taps, xs)
    v = tmm(h + xx * mu_v, w_v, taps, xs)
    wraw = -_softplus(-(w0 + tmm(jnp.tanh(tmm(h + xx * mu_w, w1, taps, xs)), w2, taps, xs))) - 0.5
    lw = -jnp.exp(wraw)
    a = _sigmoid(a0 + tmm(tmm(h + xx * mu_a, a1, taps, xs), a2, taps, xs))
    g = tmm(_sigmoid(tmm(h + xx * mu_g, g1, taps, xs)), g2, taps, xs)
    kk = k * k_k
    ss = hsum(kk * kk, e, et)
    pos = ss > 0.0
    nrm = jnp.where(pos, jnp.sqrt(jnp.where(pos, ss, 1.0)), 0.0)
    kk = kk * hbc(1.0 / jnp.maximum(nrm, 1e-12), e, et)
    k2 = k * (1.0 + (a - 1.0) * k_a)
    return (r, lw, k2, v, -kk, kk * a, g), xs


def rwkv_post(e, et, w_o, taps, y, r, k2, v, g, h0, gn_w, gn_b, rk, lg, lb):
    xs = []
    inv_n = 1.0 / HEAD_DIM
    yc = y - hbc(hsum(y, e, et) * inv_n, e, et)
    yv = hsum(yc * yc, e, et) * inv_n
    yn = yc * hbc(lax.rsqrt(yv + GN_EPS), e, et) * gn_w + gn_b
    bonus = hbc(hsum(r * k2 * rk, e, et), e, et) * v
    mix = tmm((yn + bonus) * g, w_o, taps, xs)
    return _layer_norm(ALPHA * h0 + mix, lg, lb), xs


@jax.custom_vjp
def sq_relu(x):
    r = jnp.maximum(x, 0.0)
    return r * r


sq_relu.defvjp(lambda x: (sq_relu(x), x), lambda x, g: (g * (2.0 * jnp.maximum(x, 0.0)),))


def _rot_half(t):
    n = t.shape[-1]
    lane = lax.broadcasted_iota(jnp.int32, t.shape, t.ndim - 1)
    lo = (lane % HEAD_DIM) < (HEAD_DIM // 2)
    return jnp.where(lo, -pltpu.roll(t, n - HEAD_DIM // 2, t.ndim - 1), pltpu.roll(t, HEAD_DIM // 2, t.ndim - 1))


@jax.custom_vjp
def rot_half(t):
    return _rot_half(t)


rot_half.defvjp(lambda t: (_rot_half(t), None), lambda _, g: (-_rot_half(g),))


def _tile_lanes(t, width):
    return jnp.concatenate([t] * (width // t.shape[-1]), axis=-1)


def qkv_proj(cos, sin, wq, wk, wv, taps, h):
    xs = []
    q = tmm(h, wq, taps, xs)
    k = tmm(h, wk, taps, xs)
    v = tmm(h, wv, taps, xs)
    cq, sq = _tile_lanes(cos, D_MODEL), _tile_lanes(sin, D_MODEL)
    ck, sk = _tile_lanes(cos, KV_DIM), _tile_lanes(sin, KV_DIM)
    return (q * cq + rot_half(q) * sq, k * ck + rot_half(k) * sk, v), xs


def attn_out(w_o, taps, o, h, lg, lb):
    xs = []
    return _layer_norm(ALPHA * h + tmm(o, w_o, taps, xs), lg, lb), xs


def _scan_consts():
    t = SCAN_T
    tri = np.tril(np.ones((t, t), np.float32))
    rows = np.arange(2 * t)
    same = (rows[:, None] // t) == (rows[None, :] // t)
    strict = same & ((rows[None, :] % t) < (rows[:, None] % t))
    incl = same & ((rows[None, :] % t) <= (rows[:, None] % t))
    lane = np.arange(PAIR)
    masks = np.zeros((8, PAIR), np.float32)
    masks[0] = (lane // HEAD_DIM) == 0
    masks[1] = (lane // HEAD_DIM) == 1
    return (jnp.asarray(tri, BF16), jnp.asarray(strict.astype(np.float32)), jnp.asarray(incl.astype(np.float32)),
            jnp.asarray(masks), jnp.asarray(np.eye(2 * t, dtype=np.float32)))


def _scan_dot(a, b, ca, cb):
    return _dot(a, b, ca, cb)


@functools.partial(jax.custom_vjp, nondiff_argnums=(2, 3))
def _dotf(a, b, ca, cb):
    return _scan_dot(a, b, ca, cb)


def _dotf_bwd(ca, cb, res, g):
    a, b = res
    if ca == 1:
        da = _scan_dot(g, b, 1, 1 - cb)
    else:
        da = _scan_dot(b, g, 1 - cb, 1)
    if cb == 0:
        db = _scan_dot(a, g, 1 - ca, 0)
    else:
        db = _scan_dot(g, a, 0, 1 - ca)
    return da, db


_dotf.defvjp(lambda a, b, ca, cb: (_scan_dot(a, b, ca, cb), (a, b)), _dotf_bwd)


def _tri_dot(tri, x, ct):
    acc = None
    for piece in _split3(x):
        t = lax.dot_general(tri, piece, (((ct,), (0,)), ((), ())), preferred_element_type=F32)
        acc = t if acc is None else acc + t
    return acc


@jax.custom_vjp
def _cumsum_rows(tri, x):
    return _tri_dot(tri, x, 1)


_cumsum_rows.defvjp(lambda tri, x: (_tri_dot(tri, x, 1), tri),
                    lambda tri, g: (jnp.zeros_like(tri), _tri_dot(tri, g, 0)))


@jax.custom_vjp
def _unstack2(x):
    t = x.shape[0] // 2
    return x[:t] + x[t:]


_unstack2.defvjp(lambda x: (_unstack2(x), None), lambda _, g: (jnp.concatenate([g, g], axis=0),))


@jax.custom_vjp
def _last_row(x):
    return x[x.shape[0] - 1:, :]


def _last_row_bwd(_, g):
    rows = lax.broadcasted_iota(jnp.int32, (SCAN_T, g.shape[1]), 0)
    return (jnp.where(rows == SCAN_T - 1, jnp.broadcast_to(g, (SCAN_T, g.shape[1])), 0.0),)


_last_row.defvjp(lambda x: (_last_row(x), None), _last_row_bwd)


@jax.custom_vjp
def _halves(x):
    n = x.shape[0] // 2
    return x[:n], x[n:]


_halves.defvjp(lambda x: (_halves(x), None), lambda _, g: (jnp.concatenate(list(g), axis=0),))


@jax.custom_vjp
def _quads(x):
    n, m = x.shape[0] // 2, x.shape[1] // 2
    return x[:n, :m], x[:n, m:], x[n:, :m], x[n:, m:]


_quads.defvjp(lambda x: (_quads(x), None),
              lambda _, g: (jnp.concatenate([jnp.concatenate([g[0], g[1]], axis=1),
                                             jnp.concatenate([g[2], g[3]], axis=1)], axis=0),))


@jax.custom_vjp
def _solve_saved(n, rhs, minv, u):
    return u


def _solve_saved_bwd(res, du):
    minv, u = res
    drhs = _dotf(minv, du, 0, 0)
    return _dotf(drhs, u, 1, 1), drhs, jnp.zeros_like(minv), jnp.zeros_like(u)


_solve_saved.defvjp(lambda n, rhs, minv, u: (u, (minv, u)), _solve_saved_bwd)


def scan_chunk(tri, strict, incl, m0, m1, eye, r, lw, k, v, a, b, s0, saved=None):
    lower = strict > 0
    lower_incl = incl > 0

    def stack(x):
        return jnp.concatenate([x * m0, x * m1], axis=0)

    def dots(xs, ys, ca, cb, mask=None):
        out = [_dotf(x, y, ca, cb) for x, y in zip(xs, ys)]
        return out if mask is None else [jnp.where(mask, o, 0.0) for o in out]

    cl = [_cumsum_rows(tri, x) for x in lw]
    gam = [jnp.exp(c) for c in cl]
    ginv = [jnp.exp(-c) for c in cl]
    ar_s = [jnp.concatenate([stack(x * jnp.exp(c - w)), stack(y * g)], axis=0)
            for x, c, w, y, g in zip(a, cl, lw, r, gam)]
    bk_s = [jnp.concatenate([stack(x * g), stack(y * g)], axis=0) for x, y, g in zip(b, k, ginv)]
    v_s = [stack(x) for x in v]
    quads = [_quads(x) for x in dots(ar_s, bk_s, 1, 1)]
    n_ab = [jnp.where(lower, q[0], 0.0) for q in quads]
    n_ak = [jnp.where(lower, q[1], 0.0) for q in quads]
    r_ab = [jnp.where(lower_incl, q[2], 0.0) for q in quads]
    r_ak = [jnp.where(lower_incl, q[3], 0.0) for q in quads]
    from_state = [_halves(x) for x in dots(ar_s, s0, 1, 1)]
    rhs = [x[0] + y for x, y in zip(from_state, dots(n_ak, v_s, 1, 0))]
    if saved is None:
        minv = [eye + n for n in n_ab]
        p = n_ab
        for _ in range(5):
            p = dots(p, p, 1, 0)
            minv = [m + mp for m, mp in zip(minv, dots(minv, p, 1, 0))]
        u_s = dots(minv, rhs, 1, 0)
    else:
        minv = saved[0]
        u_s = [_solve_saved(n, x, m, u) for n, x, m, u in zip(n_ab, rhs, *saved)]
    uv_s = [jnp.concatenate([x, y], axis=0) for x, y in zip(u_s, v_s)]
    r_uv = [jnp.concatenate([x, y], axis=1) for x, y in zip(r_ab, r_ak)]
    y = [_unstack2(x[1] + z) for x, z in zip(from_state, dots(r_uv, uv_s, 1, 0))]
    g_end = [_last_row(g) for g in gam]
    s1 = [s * g + x for s, g, x in zip(s0, g_end, dots(uv_s, [x * g for x, g in zip(bk_s, g_end)], 0, 0))]
    return y, s1, (minv, u_s)


SCAN_PAIRS = 8


def _scan_specs(consts, order):
    row = pl.BlockSpec((SCAN_T, PAIR * SCAN_PAIRS), lambda p, c: (order(c), p))
    state = pl.BlockSpec((None, SCAN_PAIRS, PAIR, PAIR), lambda p, c: (order(c), p, 0, 0))
    return row, state, [pl.BlockSpec(x.shape, _zero_map(x.ndim)) for x in consts]


def _pair_lanes(q):
    return slice(q * PAIR, (q + 1) * PAIR)


def scan_fwd(r, lw, k, v, a, b, shards=()):
    lp = r.shape[0]
    nch = lp // SCAN_T
    npair = D_MODEL // PAIR
    ng = len(shards)
    consts = _scan_consts()
    row, state, cspecs = _scan_specs(consts, lambda c: c)

    def body(tri, strict, incl, masks, eye, r_ref, lw_ref, k_ref, v_ref, a_ref, b_ref, *rest):
        src, (y_ref, s_ref, minv_ref, u_ref), dst = rest[:ng], rest[ng:ng + 4], rest[ng + 4:2 * ng + 4]
        carry = rest[2 * ng + 4]
        first = jnp.logical_and(pl.program_id(0) == 0, pl.program_id(1) == 0)
        last = jnp.logical_and(pl.program_id(0) == npair // SCAN_PAIRS - 1, pl.program_id(1) == nch - 1)
        if ng:
            sends, arrivals, forwards, forwarded = gather_copies(src, dst, *rest[2 * ng + 5:])

            @pl.when(first)
            def _():
                for cp in sends:
                    cp.start()

            @pl.when(jnp.logical_and(pl.program_id(0) == npair // SCAN_PAIRS - 1, pl.program_id(1) == nch * 3 // 4))
            def _():
                for landed, onward in zip(arrivals, forwards):
                    landed.wait_recv()
                    onward.start()

        @pl.when(pl.program_id(1) == 0)
        def _():
            carry[...] = jnp.zeros_like(carry)

        pairs = range(SCAN_PAIRS)
        s0 = [carry[q] for q in pairs]
        rows = [[ref[:, _pair_lanes(q)] for q in pairs] for ref in (r_ref, lw_ref, k_ref, v_ref, a_ref, b_ref)]
        y, s1, (minv, u) = scan_chunk(tri[...], strict[...], incl[...], masks[0:1, :], masks[1:2, :], eye[...],
                                      *rows, s0)
        for q in pairs:
            s_ref[q] = s0[q]
            minv_ref[q] = minv[q]
            u_ref[q] = u[q]
            y_ref[:, _pair_lanes(q)] = y[q]
            carry[q] = s1[q]

        if ng:
            @pl.when(last)
            def _():
                for cp in forwarded:
                    cp.wait_recv()
                for cp in sends + forwards:
                    cp.wait_send()

    mats = jax.ShapeDtypeStruct((nch, npair, PAIR, PAIR), F32)
    out = pl.pallas_call(
        body, name="rwkv_scan_fwd", grid=(npair // SCAN_PAIRS, nch), in_specs=cspecs + [row] * 6 + [ANY] * ng,
        out_specs=[row, state, state, state] + [ANY] * ng,
        out_shape=[jax.ShapeDtypeStruct((lp, D_MODEL), F32), mats, mats, mats] + gathered_shapes(shards),
        scratch_shapes=[pltpu.VMEM((SCAN_PAIRS, PAIR, PAIR), F32)] + (gather_scratch(ng) if ng else []),
        compiler_params=_params(),
    )(*consts, r, lw, k, v, a, b, *shards)
    return out[:4], fill_own(out[4:], shards)


def scan_bwd(r, lw, k, v, a, b, saved, dy, direct_grads, parts=()):
    lp = r.shape[0]
    nch = lp // SCAN_T
    npair = D_MODEL // PAIR
    consts = _scan_consts()
    row, state, cspecs = _scan_specs(consts, lambda c: nch - 1 - c)

    ng = len(parts)

    def body(tri, strict, incl, masks, eye, r_ref, lw_ref, k_ref, v_ref, a_ref, b_ref, s_ref, minv_ref, u_ref,
             dy_ref, dr_in, dk_in, dv_in, *rest):
        src, (dr_ref, dlw_ref, dk_ref, dv_ref, da_ref, db_ref), dst = rest[:ng], rest[ng:ng + 6], rest[ng + 6:2 * ng + 6]
        carry = rest[2 * ng + 6]
        first = jnp.logical_and(pl.program_id(0) == 0, pl.program_id(1) == 0)
        last = jnp.logical_and(pl.program_id(0) == npair // SCAN_PAIRS - 1, pl.program_id(1) == nch - 1)
        if ng:
            sends, arrivals = chip_exchange_copies(src, dst, *rest[2 * ng + 7:])

            @pl.when(first)
            def _():
                for cp in sends:
                    cp.start()

        @pl.when(pl.program_id(1) == 0)
        def _():
            carry[...] = jnp.zeros_like(carry)

        pairs = range(SCAN_PAIRS)
        kept = ([minv_ref[q] for q in pairs], [u_ref[q] for q in pairs])

        def fn(*args):
            y, s1, _ = scan_chunk(tri[...], strict[...], incl[...], masks[0:1, :], masks[1:2, :], eye[...], *args,
                                  saved=kept)
            return y, s1

        rows = [[ref[:, _pair_lanes(q)] for q in pairs] for ref in (r_ref, lw_ref, k_ref, v_ref, a_ref, b_ref)]
        _, vjp = jax.vjp(fn, *rows, [s_ref[q] for q in pairs])
        grads = vjp(([dy_ref[:, _pair_lanes(q)] for q in pairs], [carry[q] for q in pairs]))
        direct = (dr_in, None, dk_in, dv_in, None, None)
        for q in pairs:
            ln = _pair_lanes(q)
            for ref, g, extra in zip((dr_ref, dlw_ref, dk_ref, dv_ref, da_ref, db_ref), grads[:6], direct):
                ref[:, ln] = g[q] if extra is None else g[q] + extra[:, ln]
            carry[q] = grads[6][q]

        if ng:
            @pl.when(last)
            def _():
                for cp in arrivals:
                    cp.wait_recv()
                for cp in sends:
                    cp.wait_send()

    out = pl.pallas_call(
        body, name="rwkv_scan_bwd", grid=(npair // SCAN_PAIRS, nch),
        in_specs=cspecs + [row] * 6 + [state] * 3 + [row] * 4 + [ANY] * ng, out_specs=[row] * 6 + [ANY] * ng,
        out_shape=[jax.ShapeDtypeStruct((lp, D_MODEL), F32)] * 6 + [jax.ShapeDtypeStruct(p.shape, p.dtype) for p in parts],
        scratch_shapes=[pltpu.VMEM((SCAN_PAIRS, PAIR, PAIR), F32)] + (_sem_scratch(ng * len(XY_FLIPS)) if ng else []),
        compiler_params=_params(),
    )(*consts, r, lw, k, v, a, b, *saved, dy, *direct_grads, *parts)
    return out[:6], out[6:]


def _spread_matrices():
    rep = np.zeros((N_HEADS_KV, KV_DIM, KVW), np.float32)
    for h in range(N_HEADS_KV):
        for g in range(GROUP):
            rep[h, h * HEAD_DIM + np.arange(HEAD_DIM), g * HEAD_DIM + np.arange(HEAD_DIM)] = 1.0
    return jnp.asarray(rep, BF16)


KV_HEADS = range(N_HEADS_KV)


def _attn_operands(q_ref, kp, kc, vp, vc, rep_ref):
    lane = lax.broadcasted_iota(jnp.int32, (1, KVW), 1)
    gmask = [(lane // HEAD_DIM == g).astype(F32) for g in range(GROUP)]
    kk = jnp.concatenate([kp, kc], axis=0)
    vv = jnp.concatenate([vp, vc], axis=0)
    qs = [q_ref[:, h * KVW:(h + 1) * KVW] for h in KV_HEADS]
    q_s = [jnp.concatenate([q * gmask[g] for g in range(GROUP)], axis=0) for q in qs]
    keys = [_dot(kk, rep_ref[h], 1, 0) for h in KV_HEADS]
    vals = [_dot(vv, rep_ref[h], 1, 0) for h in KV_HEADS]
    return gmask, q_s, keys, vals


def _attn_probs(n, q_s, keys, sink_ref):
    qi = lax.broadcasted_iota(jnp.int32, (GROUP * BLOCK, 2 * BLOCK), 0) % BLOCK
    kj = lax.broadcasted_iota(jnp.int32, (GROUP * BLOCK, 2 * BLOCK), 1)
    rel = BLOCK + qi - kj
    valid = (rel >= 0) & (rel < BLOCK) & ((n - 1) * BLOCK + kj >= PAD_FRONT)
    s = [jnp.where(valid, _dot(x, y, 1, 1) * (HEAD_DIM ** -0.5), -1e30) for x, y in zip(q_s, keys)]
    sink_col = [jnp.concatenate([jnp.broadcast_to(sink_ref[h, g:g + 1, 0:1], (BLOCK, 1)) for g in range(GROUP)],
                                axis=0) for h in KV_HEADS]
    m = [jnp.maximum(jnp.max(x, axis=-1, keepdims=True), c) for x, c in zip(s, sink_col)]
    ex = [jnp.exp(x - y) for x, y in zip(s, m)]
    ex_sink = [jnp.exp(c - y) for c, y in zip(sink_col, m)]
    inv = [1.0 / (jnp.sum(x, axis=-1, keepdims=True) + c) for x, c in zip(ex, ex_sink)]
    return [x * y for x, y in zip(ex, inv)], [x * y for x, y in zip(ex_sink, inv)]


def _unstack_groups(x_s, gmask):
    out = None
    for g in range(GROUP):
        t = x_s[g * BLOCK:(g + 1) * BLOCK] * gmask[g]
        out = t if out is None else out + t
    return out


def _attn_specs():
    qspec = pl.BlockSpec((BLOCK, D_MODEL), lambda n: (n, 0))
    cur = pl.BlockSpec((BLOCK, KV_DIM), lambda n: (n, 0))
    prev = pl.BlockSpec((BLOCK, KV_DIM), lambda n: (jnp.maximum(n - 1, 0), 0))
    rep = pl.BlockSpec((N_HEADS_KV, KV_DIM, KVW), lambda n: (0, 0, 0))
    sink = pl.BlockSpec((N_HEADS_KV, 8, PAIR), lambda n: (0, 0, 0))
    return qspec, cur, prev, rep, sink


def _attn_params():
    return pltpu.CompilerParams(dimension_semantics=("arbitrary",), vmem_limit_bytes=VMEM_LIMIT)


def _prob_specs():
    rows = GROUP * BLOCK
    return (pl.BlockSpec((None, N_HEADS_KV, rows, 2 * BLOCK), lambda n: (n, 0, 0, 0)),
            pl.BlockSpec((None, N_HEADS_KV, rows, PAIR), lambda n: (n, 0, 0, 0)))


def attn_fwd(q, k, v, sinks_b):
    lp = q.shape[0]
    nb = lp // BLOCK
    qspec, cur, prev, rep, sink = _attn_specs()

    def body(q_ref, kp_ref, kc_ref, vp_ref, vc_ref, rep_ref, sink_ref, o_ref, p_ref, ps_ref):
        gmask, q_s, keys, vals = _attn_operands(q_ref, kp_ref[...], kc_ref[...], vp_ref[...], vc_ref[...], rep_ref)
        p, p_sink = _attn_probs(pl.program_id(0), q_s, keys, sink_ref)
        o = [_dot(x, y, 1, 0) for x, y in zip(p, vals)]
        for h in KV_HEADS:
            o_ref[:, h * KVW:(h + 1) * KVW] = _unstack_groups(o[h], gmask)
            p_ref[h] = p[h].astype(BF16)
            ps_ref[h] = jnp.broadcast_to(p_sink[h], (GROUP * BLOCK, PAIR))

    return pl.pallas_call(
        body, name="swa_fwd", grid=(nb,), in_specs=[qspec, prev, cur, prev, cur, rep, sink],
        out_specs=[qspec, *_prob_specs()],
        out_shape=[jax.ShapeDtypeStruct((lp, D_MODEL), F32),
                   jax.ShapeDtypeStruct((nb, N_HEADS_KV, GROUP * BLOCK, 2 * BLOCK), BF16),
                   jax.ShapeDtypeStruct((nb, N_HEADS_KV, GROUP * BLOCK, PAIR), F32)],
        compiler_params=_attn_params(),
    )(q, k, k, v, v, _spread_matrices(), sinks_b)


def attn_bwd(q, k, v, probs, do):
    lp = q.shape[0]
    qspec, cur, prev, rep, sink = _attn_specs()

    def body(q_ref, kp_ref, kc_ref, vp_ref, vc_ref, rep_ref, p_ref, ps_ref, do_ref, dq_ref, dkc_ref, dkp_ref, dvc_ref,
             dvp_ref, dsink_ref):
        n = pl.program_id(0)
        gmask, q_s, keys, vals = _attn_operands(q_ref, kp_ref[...], kc_ref[...], vp_ref[...], vc_ref[...], rep_ref)
        p = [p_ref[h].astype(F32) for h in KV_HEADS]
        p_sink = [ps_ref[h][:, 0:1] for h in KV_HEADS]
        do_s = [jnp.concatenate([do_ref[:, h * KVW:(h + 1) * KVW] * gmask[g] for g in range(GROUP)], axis=0)
                for h in KV_HEADS]
        dp = [_dot(x, y, 1, 1) for x, y in zip(do_s, vals)]
        delta = [jnp.sum(x * y, axis=-1, keepdims=True) for x, y in zip(p, dp)]
        ds = [x * (y - z) * (HEAD_DIM ** -0.5) for x, y, z in zip(p, dp, delta)]
        dq = [_dot(x, y, 1, 0) for x, y in zip(ds, keys)]
        dkeys_s = [_dot(x, y, 0, 0) for x, y in zip(ds, q_s)]
        dvals_s = [_dot(x, y, 0, 0) for x, y in zip(p, do_s)]
        dkeys = [_exact_dot(x, rep_ref[h], cb=1) for h, x in enumerate(dkeys_s)]
        dvals = [_exact_dot(x, rep_ref[h], cb=1) for h, x in enumerate(dvals_s)]
        dk_all = (dkeys[0] + dkeys[1]) + (dkeys[2] + dkeys[3])
        dv_all = (dvals[0] + dvals[1]) + (dvals[2] + dvals[3])
        dkp_ref[...] = dk_all[:BLOCK]
        dkc_ref[...] = dk_all[BLOCK:]
        dvp_ref[...] = dv_all[:BLOCK]
        dvc_ref[...] = dv_all[BLOCK:]
        dsinks = []
        for h in KV_HEADS:
            dq_ref[:, h * KVW:(h + 1) * KVW] = _unstack_groups(dq[h], gmask)
            dsk = -(p_sink[h] * delta[h])
            rows = [jnp.broadcast_to(jnp.sum(dsk[g * BLOCK:(g + 1) * BLOCK], axis=0, keepdims=True), (1, PAIR))
                    for g in range(GROUP)]
            dsinks.append(jnp.concatenate(rows + [jnp.zeros((8 - GROUP, PAIR), F32)], axis=0))

        @pl.when(n == 0)
        def _():
            for h in KV_HEADS:
                dsink_ref[h] = dsinks[h]

        @pl.when(n > 0)
        def _():
            for h in KV_HEADS:
                dsink_ref[h] += dsinks[h]

    kv = jax.ShapeDtypeStruct((lp, KV_DIM), F32)
    return pl.pallas_call(
        body, name="swa_bwd", grid=(lp // BLOCK,), in_specs=[qspec, prev, cur, prev, cur, rep, *_prob_specs(), qspec],
        out_specs=[qspec, cur, cur, cur, cur, sink],
        out_shape=[jax.ShapeDtypeStruct((lp, D_MODEL), F32), kv, kv, kv, kv,
                   jax.ShapeDtypeStruct((N_HEADS_KV, 8, PAIR), F32)],
        compiler_params=_attn_params(),
    )(q, k, k, v, v, _spread_matrices(), *probs, do)


def _pick_tm(lp, want):
    for tm in (384, 192, 128, 64):
        if tm <= want and lp % tm == 0:
            return tm
    raise ValueError(lp)


def _acc(shape):
    return (tuple(shape), F32)


def _ff_one(w):
    return (w, (None, D_MODEL, D_MODEL), lambda c, i: (c, 0, 0))


def _mlp_layer_fwd(name, h, wup, wdown, lg, lb, tm):
    def fn(c, i, h, wup, wdown, lg, lb):
        out, pre = None, []
        for s in range(N_FF_CHUNK):
            u = mm(h, wup[s])
            pre.append(u.astype(BF16))
            t = mm(sq_relu(u), wdown[s])
            out = t if out is None else out + t
        z = ALPHA * h + out
        return (_layer_norm(z, lg, lb), z, jnp.stack(pre)), ()

    (h_out, z, pre), _ = rowwise(name, fn, [h], [wup, wdown, lg, lb],
                                 [(D_MODEL, F32), (D_MODEL, F32), (D_MODEL, BF16, False, N_FF_CHUNK)], [], tm)
    return h_out, z, pre


MLP_BWD_TILE = 528


def _mlp_layer_bwd(name, h_in, z, pre, dh_parts, wup, wdown, lg, lb, tm):
    n_parts = len(dh_parts)

    def fn_ln(c, i, z, *rest):
        dh = rest[0]
        for extra in rest[1:n_parts]:
            dh = dh + extra
        _, vjp = jax.vjp(_layer_norm, z, rest[n_parts], rest[n_parts + 1])
        dz, dlg, dlb = vjp(dh)
        return (dz,), (dlg, dlb)

    (dz,), (dlg, dlb) = rowwise(name + "_ln", fn_ln, [z] + list(dh_parts), [lg, lb], [(D_MODEL, F32)],
                                [_acc((1, D_MODEL)), _acc((1, D_MODEL))], tm)

    def fn_mlp(c, i, h, dz, wup, wdown, u):
        r = jnp.maximum(u.astype(F32), 0.0)
        du = _dot(dz, wdown, 1, 1) * (2.0 * r)
        return (_dot(du, wup, 1, 1),), (_dot(h, du, 0, 0), _dot(r * r, dz, 0, 0))

    aspec = ((N_FF_CHUNK, D_MODEL, D_MODEL), F32, (None, D_MODEL, D_MODEL), lambda c, i: (c, 0, 0))
    lp = h_in.shape[0]
    tile = MLP_BWD_TILE if lp % MLP_BWD_TILE == 0 else tm
    pre_chunk = (pre, (None, tile, D_MODEL), lambda c, i: (c, i, 0))
    (dx,), (dwup, dwdown) = rowwise(name + "_mm", fn_mlp, [h_in, dz], [_ff_one(wup), _ff_one(wdown), pre_chunk],
                                    [(D_MODEL, F32, True)], [aspec, aspec], tile, nc=N_FF_CHUNK)
    return dz, dx, dwup, dwdown, dlg, dlb


def _sum_parts(dz, dx):
    out = ALPHA * dz
    for s in range(N_FF_CHUNK):
        out = out + dx[s]
    return out


def local_step(x, loss_target, p, late=None, early_hook=None):
    seq = x.shape[0]
    lp = TOK0 + seq
    tm = _pick_tm(lp, 384)
    tms = _pick_tm(lp, 192)
    e, et = _head_matrices()
    h0 = jnp.concatenate([jnp.zeros((PAD_FRONT, D_MODEL), F32), p["meta_tokens"], x], axis=0)
    pos = jnp.maximum(jnp.arange(lp, dtype=F32) - PAD_FRONT, 0.0)
    inv_freq = 1.0 / (ROPE_THETA ** (jnp.arange(0, HEAD_DIM, 2, dtype=F32) / HEAD_DIM))
    ang = pos[:, None] * inv_freq[None, :]
    cos = jnp.tile(jnp.cos(ang), (1, PAIR // (HEAD_DIM // 2)))
    sin = jnp.tile(jnp.sin(ang), (1, PAIR // (HEAD_DIM // 2)))

    pre_vec = [p["a_mu"][j:j + 1] for j in range(6)] + [p["a_w0"], p["a_a0"], p["a_k_k"], p["a_k_a"]]
    pre_w = [p["a_w_r"], p["a_w_k"], p["a_w_v"], p["a_w1"], p["a_w2"], p["a_a1"], p["a_a2"], p["a_g1"], p["a_g2"]]
    n_vec = len(pre_vec)

    def fn_pre(c, i, h, before, e, et, *ws):
        return rwkv_pre(e, et, ws[n_vec:], None, h, _shift_down(h, before, i), *ws[:n_vec])[0], ()

    (r, lw, k2, v, an, bn, g), _, *pre_gathered = rowwise(
        "rwkv_pre", fn_pre, [h0, _halo_before(h0, tms)], [e, et] + pre_vec + pre_w, [(D_MODEL, F32)] * 7, [], tms,
        hosted=hosted_gather(late[0][0]) if late else None)
    (y, *scan_saved), scan_gathered = scan_fwd(r, lw, k2, v, an, bn, late[1][0] if late else ())
    if late:
        p = {**p, **late[0][1](pre_gathered[0]), **late[1][1](scan_gathered)}

    post_c = [p["a_w_o"], p["a_gn_w"], p["a_gn_b"], p["a_r_k"], p["ln_g00"], p["ln_b00"]]

    def fn_post(c, i, y, r, k2, v, g, h0, e, et, w_o, *vecs):
        return (rwkv_post(e, et, w_o, None, y, r, k2, v, g, h0, *vecs)[0],), ()

    (h1,), _ = rowwise("rwkv_post", fn_post, [y, r, k2, v, g, h0], [e, et] + post_c, [(D_MODEL, F32)], [], tm)
    h2, z2, pre2 = _mlp_layer_fwd("mlp0_fwd", h1, p["mlp_up0"], p["mlp_down0"], p["ln_g01"], p["ln_b01"], tm)

    qkv_w = [p["b_w_q"], p["kv_w_k"], p["kv_w_v"]]

    def fn_qkv(c, i, h, cos, sin, wq, wk, wv):
        return qkv_proj(cos, sin, wq, wk, wv, None, h)[0], ()

    (q, k, vv), _ = rowwise("qkv_proj", fn_qkv, [h2, cos, sin], qkv_w,
                            [(D_MODEL, F32), (KV_DIM, F32), (KV_DIM, F32)], [], tm)
    sinks_b = jnp.broadcast_to(p["b_sinks"].reshape(N_HEADS_KV, GROUP, 1), (N_HEADS_KV, GROUP, PAIR))
    sinks_b = jnp.concatenate([sinks_b, jnp.zeros((N_HEADS_KV, 8 - GROUP, PAIR), F32)], axis=1)
    o, *attn_probs = attn_fwd(q, k, vv, sinks_b)

    ao_c = [p["b_w_o"], p["ln_g10"], p["ln_b10"]]

    def fn_ao(c, i, o, h, w_o, lg, lb):
        return (attn_out(w_o, None, o, h, lg, lb)[0],), ()

    (h3,), _ = rowwise("attn_out", fn_ao, [o, h2], ao_c, [(D_MODEL, F32)], [], tm)
    h4, z4, pre4 = _mlp_layer_fwd("mlp1_fwd", h3, p["mlp_up1"], p["mlp_down1"], p["ln_g11"], p["ln_b11"], tm)

    def fn_loss(c, i, h4, tgt):
        real = (_row_ids(i, TOK0) >= TOK0).astype(F32)
        err = (h4 - tgt) * real
        part = 0.5 * jnp.sum(jnp.sum(err * err, axis=-1, keepdims=True), axis=0, keepdims=True) / D_MODEL
        return (err * (1.0 / D_MODEL),), (jnp.broadcast_to(part, (8, PAIR)),)

    (dh4,), (loss_acc,) = rowwise("loss", fn_loss, [h4, (loss_target, TOK0, lambda i: jnp.maximum(i - 1, 0))], [],
                                  [(D_MODEL, F32)], [_acc((8, PAIR))], TOK0)
    loss = loss_acc[0, 0]

    grads = {}
    dz4, dx4, grads["mlp_up1"], grads["mlp_down1"], grads["ln_g11"], grads["ln_b11"] = _mlp_layer_bwd(
        "mlp1_bwd", h3, z4, pre4, [dh4], p["mlp_up1"], p["mlp_down1"], p["ln_g11"], p["ln_b11"], tm)

    def fn_ao_b(c, i, dz, dx, o, h, w_o, lg, lb):
        (do, dh, dlg, dlb), (dw_o,) = vjp_taps(functools.partial(attn_out, w_o), [(tm, D_MODEL)], [o, h, lg, lb],
                                               _sum_parts(dz, dx))
        return (do, dh), (dw_o, dlg, dlb)

    (do, dh2_a), (grads["b_w_o"], grads["ln_g10"], grads["ln_b10"]) = rowwise(
        "attn_out_bwd", fn_ao_b, [dz4, dx4, o, h2], ao_c, [(D_MODEL, F32)] * 2,
        [_acc((D_MODEL, D_MODEL)), _acc((1, D_MODEL)), _acc((1, D_MODEL))], tm)

    dq, dkc, dkp, dvc, dvp, dsinks = attn_bwd(q, k, vv, attn_probs, do)
    grads["b_sinks"] = dsinks[:, :GROUP, 0].reshape(1, N_HEADS)
    zblk = jnp.zeros((BLOCK, KV_DIM), F32)
    dkp_s = jnp.concatenate([dkp[BLOCK:], zblk], axis=0)
    dvp_s = jnp.concatenate([dvp[BLOCK:], zblk], axis=0)

    def fn_qkv_b(c, i, h, cos, sin, dq, dkc, dkp, dvc, dvp, wq, wk, wv):
        return vjp_taps(functools.partial(qkv_proj, cos, sin, wq, wk, wv),
                        [(tm, D_MODEL), (tm, KV_DIM), (tm, KV_DIM)], [h], (dq, dkc + dkp, dvc + dvp))

    (dh2_q,), (grads["b_w_q"], grads["kv_w_k"], grads["kv_w_v"]) = rowwise(
        "qkv_proj_bwd", fn_qkv_b, [h2, cos, sin, dq, dkc, dkp_s, dvc, dvp_s], qkv_w, [(D_MODEL, F32)],
        [_acc((D_MODEL, D_MODEL)), _acc((D_MODEL, KV_DIM)), _acc((D_MODEL, KV_DIM))], tm)

    dz2, dx2, grads["mlp_up0"], grads["mlp_down0"], grads["ln_g01"], grads["ln_b01"] = _mlp_layer_bwd(
        "mlp0_bwd", h1, z2, pre2, [dh2_a, dh2_q], p["mlp_up0"], p["mlp_down0"], p["ln_g01"], p["ln_b01"], tm)

    def fn_post_b(c, i, dz, dx, y, r, k2, v, g, h0, e, et, w_o, *vecs):
        out, dws = vjp_taps(functools.partial(rwkv_post, e, et, w_o), [(tms, D_MODEL)],
                            [y, r, k2, v, g, h0] + list(vecs), _sum_parts(dz, dx))
        return out[:6], tuple(dws) + tuple(out[6:])

    early_srcs = early_hook[0](grads) if early_hook else ()
    (dy, dr_c, dk_c, dv_c, dg, dh0_c), post_g, *early_got = rowwise(
        "rwkv_post_bwd", fn_post_b, [dz2, dx2, y, r, k2, v, g, h0], [e, et] + post_c, [(D_MODEL, F32)] * 6,
        [_acc((D_MODEL, D_MODEL))] + [_acc((1, D_MODEL))] * 5, tms,
        hosted=hosted_pair_exchange(early_srcs) if early_hook else None)
    for name, val in zip(["a_w_o", "a_gn_w", "a_gn_b", "a_r_k", "ln_g00", "ln_b00"], post_g):
        grads[name] = val

    (dr, dlw, dk2, dv, dan, dbn), early_from_chips = scan_bwd(
        r, lw, k2, v, an, bn, scan_saved, dy, (dr_c, dk_c, dv_c),
        early_hook[1](early_srcs, early_got[0]) if early_hook else ())

    def fn_pre_b(c, i, h, before, dr, dlw, dk2, dv, dan, dbn, dg, e, et, *ws):
        hp = _shift_down(h, before, i)
        real = (_row_ids(i, tms) >= PAD_FRONT).astype(F32)
        cot = tuple(t * real for t in (dr, dlw, dk2, dv, dan, dbn, dg))
        out, dws = vjp_taps(functools.partial(rwkv_pre, e, et, ws[n_vec:]), [(tms, n) for n in PRE_TAPS],
                            [h, hp] + list(ws[:n_vec]), cot)
        return out[:2], tuple(out[2:]) + tuple(dws)

    (dh0_p, dhp), pre_g = rowwise(
        "rwkv_pre_bwd", fn_pre_b, [h0, _halo_before(h0, tms), dr, dlw, dk2, dv, dan, dbn, dg],
        [e, et] + pre_vec + pre_w, [(D_MODEL, F32)] * 2,
        [_acc((1, D_MODEL))] * n_vec + [_acc(w.shape) for w in pre_w], tms)
    grads["a_mu"] = jnp.concatenate(pre_g[:6], axis=0)
    for name, val in zip(["a_w0", "a_a0", "a_k_k", "a_k_a", "a_w_r", "a_w_k", "a_w_v", "a_w1", "a_w2", "a_a1",
                          "a_a2", "a_g1", "a_g2"], pre_g[6:]):
        grads[name] = val

    def fn_add(c, i, a, b, d, after):
        return (a + b + _shift_up(d, after, i, lp // tm),), ()

    (dh0,), _ = rowwise("grad_h0", fn_add, [dh0_c, dh0_p, dhp, _halo_after(dhp, tm)], [], [(D_MODEL, F32)], [], tm)
    grads["meta_tokens"] = dh0[PAD_FRONT:TOK0]
    return loss, dh0[TOK0:], grads, early_from_chips


ANY = pl.BlockSpec(memory_space=pl.ANY)
XY_FLIPS = ((0, 1), (1, 0), (1, 1))


def _flip(v, bit):
    return 1 - v if bit else v


def _sem_scratch(n):
    return [pltpu.SemaphoreType.DMA((n,)), pltpu.SemaphoreType.DMA((n,))]


def gather_copies(src, dst, ici_send, ici_recv, d2d_send, d2d_recv):
    npeer = len(XY_FLIPS)
    x, y, c = lax.axis_index("x"), lax.axis_index("y"), lax.axis_index("c")

    def half(ref, k, which):
        h = src[k].shape[0] // 2
        start = which * h
        return ref.at[pl.ds(pl.multiple_of(start, 8) if h % 8 == 0 else start, h)]

    def ici(k, j, slot):
        fx, fy = XY_FLIPS[j]
        return pltpu.make_async_remote_copy(
            src_ref=half(src[k], k, c), dst_ref=half(dst[k].at[slot], k, c), send_sem=ici_send.at[k * npeer + j],
            recv_sem=ici_recv.at[k * npeer + j], device_id=(_flip(x, fx), _flip(y, fy), c), device_id_type=MESH)

    def d2d(k, j, which):
        fx, fy = XY_FLIPS[j]
        landed = half(dst[k].at[2 * _flip(x, fx) + _flip(y, fy)], k, which)
        return pltpu.make_async_remote_copy(
            src_ref=landed, dst_ref=landed, send_sem=d2d_send.at[k * npeer + j], recv_sem=d2d_recv.at[k * npeer + j],
            device_id=(x, y, 1 - c), device_id_type=MESH)

    pairs = [(k, j) for k in range(len(src)) for j in range(npeer)]
    return ([ici(k, j, 2 * x + y) for k, j in pairs],
            [ici(k, j, 2 * _flip(x, XY_FLIPS[j][0]) + _flip(y, XY_FLIPS[j][1])) for k, j in pairs],
            [d2d(k, j, c) for k, j in pairs], [d2d(k, j, 1 - c) for k, j in pairs])


def gather_scratch(n):
    return _sem_scratch(n * len(XY_FLIPS)) * 2


def gathered_shapes(shards):
    return [jax.ShapeDtypeStruct((N_SHARD,) + s.shape, s.dtype) for s in shards]


def fill_own(gathered, shards):
    if not shards:
        return []
    slot = 2 * lax.axis_index("x") + lax.axis_index("y")
    return [lax.dynamic_update_index_in_dim(g, s, slot, 0) for g, s in zip(gathered, shards)]


def all_gather_shards(shards):
    n = len(shards)

    def body(*refs):
        sends, arrivals, forwards, forwarded = gather_copies(refs[:n], refs[n:2 * n], *refs[2 * n:])
        for cp in sends:
            cp.start()
        for landed, onward in zip(arrivals, forwards):
            landed.wait_recv()
            onward.start()
        for cp in forwarded:
            cp.wait_recv()
        for cp in sends + forwards:
            cp.wait_send()

    out = pl.pallas_call(body, name="gather_weights", in_specs=[ANY] * n, out_specs=[ANY] * n,
                         out_shape=gathered_shapes(shards), scratch_shapes=gather_scratch(n))(*shards)
    return fill_own(out, shards)


def placement():
    x, y, c = lax.axis_index("x"), lax.axis_index("y"), lax.axis_index("c")
    me = 2 * x + y
    others = [j + (j >= me).astype(jnp.int32) for j in range(N_SHARD - 1)]
    return jnp.stack([c, me] + others).astype(jnp.int32)


def hosted_gather(shards):
    return (gather_copies, list(shards), gathered_shapes(shards), gather_scratch(len(shards)),
            lambda got: fill_own(got, shards))


def pair_exchange_copies(src, got, send_sems, recv_sems):
    x, y, c = lax.axis_index("x"), lax.axis_index("y"), lax.axis_index("c")

    def copy(k):
        half = src[k].shape[1] // 2
        theirs = src[k].at[:, pl.ds(pl.multiple_of((1 - c) * half, 8), half), :]
        return pltpu.make_async_remote_copy(
            src_ref=theirs, dst_ref=got[k], send_sem=send_sems.at[k], recv_sem=recv_sems.at[k],
            device_id=(x, y, 1 - c), device_id_type=MESH)

    sends = [copy(k) for k in range(len(src))]
    return sends, sends, [], []


def _half_shapes(sources):
    return [jax.ShapeDtypeStruct((s.shape[0], s.shape[1] // 2, s.shape[2]), s.dtype) for s in sources]


def hosted_pair_exchange(sources):
    return (pair_exchange_copies, list(sources), _half_shapes(sources), _sem_scratch(len(sources)), list)


def pair_exchange(name, sources):
    n = len(sources)

    def body(*refs):
        sends, arrivals, _, _ = pair_exchange_copies(refs[:n], refs[n:2 * n], *refs[2 * n:])
        for cp in sends:
            cp.start()
        for cp in arrivals:
            cp.wait_recv()
        for cp in sends:
            cp.wait_send()

    halves = _half_shapes(sources)
    return pl.pallas_call(body, name=name, in_specs=[ANY] * n, out_specs=[ANY] * n,
                          out_shape=halves, scratch_shapes=_sem_scratch(n))(*sources)


def chip_exchange(parts):
    n = len(parts)

    def body(*refs):
        sends, arrivals = chip_exchange_copies(refs[:n], refs[n:2 * n], *refs[2 * n:])
        for cp in sends:
            cp.start()
        for cp in arrivals:
            cp.wait_recv()
        for cp in sends:
            cp.wait_send()

    return pl.pallas_call(
        body, name="grads_chip_exchange", in_specs=[ANY] * n, out_specs=[ANY] * n,
        out_shape=[jax.ShapeDtypeStruct(p.shape, p.dtype) for p in parts],
        scratch_shapes=_sem_scratch(n * len(XY_FLIPS)),
    )(*parts)


def chip_exchange_copies(src, dst, send_sems, recv_sems):
    npeer = len(XY_FLIPS)
    x, y, c = lax.axis_index("x"), lax.axis_index("y"), lax.axis_index("c")
    me = 2 * x + y

    def copy(k, j, sending):
        fx, fy = XY_FLIPS[j]
        px, py = _flip(x, fx), _flip(y, fy)
        peer = 2 * px + py
        return pltpu.make_async_remote_copy(
            src_ref=src[k].at[peer], dst_ref=dst[k].at[me if sending else peer],
            send_sem=send_sems.at[k * npeer + j], recv_sem=recv_sems.at[k * npeer + j],
            device_id=(px, py, c), device_id_type=MESH)

    pairs = [(k, j) for k in range(len(src)) for j in range(npeer)]
    return [copy(k, j, True) for k, j in pairs], [copy(k, j, False) for k, j in pairs]


def sibling_share(halves):
    n = len(halves)

    def body(*refs):
        src, got = refs[:n], refs[n:2 * n]
        send_sems, recv_sems = refs[2 * n:]
        x, y, c = lax.axis_index("x"), lax.axis_index("y"), lax.axis_index("c")
        sends = [pltpu.make_async_remote_copy(
            src_ref=src[k], dst_ref=got[k], send_sem=send_sems.at[k], recv_sem=recv_sems.at[k],
            device_id=(x, y, 1 - c), device_id_type=MESH) for k in range(n)]
        for cp in sends:
            cp.start()
        for cp in sends:
            cp.wait_recv()
        for cp in sends:
            cp.wait_send()

    return pl.pallas_call(
        body, name="grads_sibling_share", in_specs=[ANY] * n, out_specs=[ANY] * n,
        out_shape=[jax.ShapeDtypeStruct(h.shape, h.dtype) for h in halves], scratch_shapes=_sem_scratch(n),
    )(*halves)


ADD_TILE_ELEMS = 512 * 1024


def _row_tile(rows, cols):
    return max(t for t in range(8, rows + 1, 8) if rows % t == 0 and t * cols <= ADD_TILE_ELEMS)


def _prefetch_call(body, name, place, grid, in_specs, out_specs, out_shape, args):
    return pl.pallas_call(
        body, name=name, out_shape=out_shape,
        grid_spec=pltpu.PrefetchScalarGridSpec(num_scalar_prefetch=1, grid=grid, in_specs=in_specs,
                                               out_specs=out_specs),
        compiler_params=pltpu.CompilerParams(dimension_semantics=("arbitrary",) * len(grid),
                                             vmem_limit_bytes=VMEM_LIMIT),
    )(place, *args)


def pair_add(name, place, src, got, dtype):
    n4, half, cols = got.shape
    tile = _row_tile(half, cols)
    nt = half // tile

    def body(pr, a_ref, b_ref, o_ref):
        o_ref[...] = (a_ref[...] + b_ref[...]).astype(o_ref.dtype)

    mine = pl.BlockSpec((None, tile, cols), lambda s, i, pr: (s, pr[0] * nt + i, 0))
    blk = pl.BlockSpec((None, tile, cols), lambda s, i, pr: (s, i, 0))
    return _prefetch_call(body, name, place, (n4, nt), [mine, blk], blk,
                          jax.ShapeDtypeStruct(got.shape, dtype), (src, got))


def chip_add(name, place, part, from_chips):
    _, half, cols = part.shape
    tile = _row_tile(half, cols)

    def body(pr, own_ref, r0_ref, r1_ref, r2_ref, o_ref):
        me = pr[1]
        own, r0, r1, r2 = (r[...].astype(F32) for r in (own_ref, r0_ref, r1_ref, r2_ref))
        t0 = jnp.where(me == 0, own, r0)
        t1 = jnp.where(me == 0, r0, jnp.where(me == 1, own, r1))
        t2 = jnp.where(me <= 1, r1, jnp.where(me == 2, own, r2))
        t3 = jnp.where(me == 3, own, r2)
        o_ref[...] = ((t0 + t1) + t2) + t3

    def slab(j):
        return pl.BlockSpec((None, tile, cols), lambda i, pr: (pr[j], i, 0))

    return _prefetch_call(body, name, place, (half // tile,), [slab(1), slab(2), slab(3), slab(4)],
                          pl.BlockSpec((tile, cols), lambda i, pr: (i, 0)),
                          jax.ShapeDtypeStruct((half, cols), F32), (part, from_chips, from_chips, from_chips))


def pair_adds(tag, place, sources, got, narrow):
    return [pair_add(f"grads_pair_add_{tag}{k}", place, s, g, BF16 if nar else F32)
            for k, (s, g, nar) in enumerate(zip(sources, got, narrow))]


def finish_sums(place, parts, from_chips):
    halves = [chip_add(f"grads_chip_add{k}", place, p, f) for k, (p, f) in enumerate(zip(parts, from_chips))]
    return list(zip(halves, sibling_share(halves)))


ADAM_ROWS = 256


def adamw_update(name, place, halves, w, m, v):
    nsub, rows, cols = w.shape
    half = rows // 2
    tr = ADAM_ROWS if half % ADAM_ROWS == 0 else half
    nth = half // tr

    def body(pr, *refs):
        g_refs, (w_ref, m_ref, v_ref, g_ref, d_ref, nm_ref, nv_ref) = refs[:2 * nsub], refs[2 * nsub:]
        l = pl.program_id(0)
        mine = (pl.program_id(1) // nth) == pr[0]
        g = None
        for s in range(nsub):
            gs = jnp.where(mine, g_refs[2 * s][...], g_refs[2 * s + 1][...])
            g = gs if g is None else jnp.where(l == s, gs, g)
        m2 = ADAM_B1 * m_ref[...] + (1.0 - ADAM_B1) * g
        v2 = ADAM_B2 * v_ref[...] + (1.0 - ADAM_B2) * (g * g)
        m_hat = m2 / (1.0 - ADAM_B1 ** ADAM_STEP)
        v_hat = v2 / (1.0 - ADAM_B2 ** ADAM_STEP)
        g_ref[...] = g
        d_ref[...] = -ADAM_LR * (m_hat / (jnp.sqrt(v_hat) + ADAM_EPS) + ADAM_WD * w_ref[...])
        nm_ref[...] = m2
        nv_ref[...] = v2

    own = pl.BlockSpec((tr, cols), lambda l, i, pr: (jnp.where(i // nth == pr[0], i % nth, 0), 0))
    got = pl.BlockSpec((tr, cols), lambda l, i, pr: (jnp.where(i // nth == pr[0], 0, i % nth), 0))
    blk = pl.BlockSpec((None, tr, cols), lambda l, i, pr: (l, i, 0))
    out = jax.ShapeDtypeStruct((nsub, rows, cols), F32)
    return _prefetch_call(body, name, place, (nsub, rows // tr), [own, got] * nsub + [blk] * 3, [blk] * 4,
                          [out] * 4, [h for pair in halves for h in pair] + [w, m, v])


WEIGHT_NAMES = ("meta_tokens", "a_mu", "a_w_r", "a_w_k", "a_w_v", "a_w_o", "a_w0", "a_w1", "a_w2", "a_a0", "a_a1",
                "a_a2", "a_g1", "a_g2", "a_k_k", "a_k_a", "a_r_k", "a_gn_w", "a_gn_b", "kv_w_k", "kv_w_v", "b_w_q",
                "b_sinks", "b_w_o", "mlp_w_up", "mlp_w_down", "ln_g", "ln_b")
BIG_NAMES = ("a_w_r", "a_w_k", "a_w_v", "a_w_o", "b_w_q", "b_w_o")
EARLY_NAMES, LATE_NAMES = BIG_NAMES[:3], BIG_NAMES[3:]
PACK_MATS = (("kv_w_k", 256), ("kv_w_v", 256), ("a_w1", 64), ("a_a1", 64), ("a_g1", 128), ("a_w2", 64),
             ("a_a2", 64), ("a_g2", 128))
COLUMN_CUT = ("a_w2", "a_a2", "a_g2")
PACK_VECS = (("a_mu", 6), ("a_w0", 1), ("a_a0", 1), ("a_k_k", 1), ("a_k_a", 1), ("a_gn_w", 1), ("a_gn_b", 1),
             ("ln_g", 4), ("ln_b", 4), ("meta_tokens", 16))
PACK_REPL = (("a_r_k", 4), ("b_sinks", 1))
SHARD_W = D_MODEL // N_SHARD


def _tiles(rows):
    return -(-rows // SUBLANES) * SUBLANES


N_MAT_ROWS = sum(_tiles(r) for _, r in PACK_MATS)
N_VEC_ROWS = sum(_tiles(r) for _, r in PACK_VECS)
N_PACK_ROWS = -(-(N_MAT_ROWS + N_VEC_ROWS + sum(_tiles(r) for _, r in PACK_REPL)) // 16) * 16
N_GATHER_VEC_ROWS = -(-N_VEC_ROWS // 16) * 16


def _pad_rows(arr, axis):
    rows = arr.shape[axis]
    pad = [(0, 0)] * arr.ndim
    pad[axis] = (0, _tiles(rows) - rows)
    return jnp.pad(arr, pad) if _tiles(rows) != rows else arr


def _pack_rows(arr):
    if arr.size == N_HEADS:
        arr = jnp.pad(arr.reshape(1, N_HEADS), ((0, 0), (0, SHARD_W - N_HEADS)))
    return _pad_rows(arr.reshape(-1, SHARD_W), 0)


def pack_small(get):
    parts = [_pack_rows(get(name)) for name, _ in PACK_MATS + PACK_VECS + PACK_REPL]
    used = sum(p.shape[0] for p in parts)
    return jnp.concatenate(parts + [jnp.zeros((N_PACK_ROWS - used, SHARD_W), F32)], axis=0)


def unpack_small(pack, shapes):
    out, off = {}, 0
    for name, rows in PACK_MATS + PACK_VECS + PACK_REPL:
        piece = pack[off:off + rows]
        off += _tiles(rows)
        out[name] = piece[:, :N_HEADS].reshape(shapes[name]) if name == "b_sinks" else piece.reshape(shapes[name])
    return out


def whole_weights(big_names, gathered_big, mats, vecs, a_r_k, b_sinks):
    p = {name: g.reshape(D_MODEL, D_MODEL) for name, g in zip(big_names, gathered_big)}
    off = 0
    for name, rows in PACK_MATS:
        piece = mats[:, off:off + rows]
        off += rows
        if name in COLUMN_CUT:
            p[name] = piece.transpose(1, 0, 2).reshape(rows, D_MODEL)
        else:
            p[name] = piece.reshape(D_MODEL, rows)
    v = vecs.transpose(1, 0, 2).reshape(-1, D_MODEL)
    off = 0
    for name, rows in PACK_VECS:
        p[name] = v[off:off + rows]
        off += _tiles(rows)
    for i in range(2):
        for j in range(2):
            p[f"ln_g{i}{j}"] = p["ln_g"][2 * i + j:2 * i + j + 1]
            p[f"ln_b{i}{j}"] = p["ln_b"][2 * i + j:2 * i + j + 1]
    p["a_r_k"] = a_r_k.reshape(1, D_MODEL)
    p["b_sinks"] = b_sinks
    return p


def small_grad_pack(g):
    parts = []
    for name, rows in PACK_MATS:
        if name in COLUMN_CUT:
            parts.append(g[name].reshape(rows, N_SHARD, SHARD_W).transpose(1, 0, 2))
        else:
            parts.append(g[name].reshape(N_SHARD, rows, SHARD_W))
    vecs = {n: g[n] for n in ("a_mu", "a_w0", "a_a0", "a_k_k", "a_k_a", "a_gn_w", "a_gn_b", "meta_tokens")}
    vecs["ln_g"] = jnp.concatenate([g[f"ln_g{i}{j}"] for i in range(2) for j in range(2)], axis=0)
    vecs["ln_b"] = jnp.concatenate([g[f"ln_b{i}{j}"] for i in range(2) for j in range(2)], axis=0)
    for name, rows in PACK_VECS:
        parts.append(_pad_rows(vecs[name].reshape(rows, N_SHARD, SHARD_W).transpose(1, 0, 2), 1))
    r_k = jnp.broadcast_to(g["a_r_k"].reshape(1, -1, SHARD_W), (N_SHARD, D_MODEL // SHARD_W, SHARD_W))
    sinks = jnp.pad(g["b_sinks"].reshape(1, 1, N_HEADS), ((0, 0), (0, 0), (0, SHARD_W - N_HEADS)))
    parts += [_pad_rows(r_k, 1), _pad_rows(jnp.broadcast_to(sinks, (N_SHARD, 1, SHARD_W)), 1)]
    used = sum(p.shape[1] for p in parts)
    parts.append(jnp.zeros((N_SHARD, N_PACK_ROWS - used, SHARD_W), F32))
    return jnp.concatenate(parts, axis=1)


def train_step(vals):
    w = {n: vals[n] for n in WEIGHT_NAMES}
    w_pack = pack_small(lambda n: w[n])
    early = [w[n][0].astype(BF16) for n in EARLY_NAMES]
    early += [w_pack[:N_MAT_ROWS].astype(BF16), w_pack[N_MAT_ROWS:N_MAT_ROWS + N_GATHER_VEC_ROWS]]
    gathered = all_gather_shards(early)
    ne = len(EARLY_NAMES)
    p = whole_weights(EARLY_NAMES, gathered[:ne], gathered[ne], gathered[ne + 1][:, :N_VEC_ROWS], w["a_r_k"],
                      w["b_sinks"])
    nb = len(BIG_NAMES)

    def late_set(big, mlp):
        shards = [w[n][0].astype(BF16) for n in big]
        shards += [w["mlp_w_" + name][layer].astype(BF16) for name, layer in mlp]

        def weights(got):
            out = {n: x.reshape(D_MODEL, D_MODEL) for n, x in zip(big, got)}
            out.update({f"mlp_{name}{layer}": x for (name, layer), x in zip(mlp, got[len(big):])})
            return out

        return shards, weights

    late = (late_set((), (("up", 1),)), late_set(LATE_NAMES, (("up", 0), ("down", 0), ("down", 1))))

    place = placement()
    ready = {}
    a_names, b_names = BIG_NAMES[:4], BIG_NAMES[4:]

    def early_sources(g):
        return ([g[n].reshape(N_SHARD, SHARD_W, D_MODEL) for n in b_names]
                + [g["mlp_up0"], g["mlp_up1"], g["mlp_down0"], g["mlp_down1"]])

    def early_parts(srcs, got):
        ready["parts"] = pair_adds("early", place, srcs, got, [True] * len(srcs))
        return ready["parts"]

    loss, gx, g, early_from_chips = local_step(vals["x"][0], vals["loss_target"][0], p, late,
                                               (early_sources, early_parts))
    loss = lax.psum(loss, ("x", "y", "c"))
    srcs = [g[n].reshape(N_SHARD, SHARD_W, D_MODEL) for n in a_names] + [small_grad_pack(g)]
    rest = pair_adds("late", place, srcs, pair_exchange("grads_pair_exchange", srcs), [True] * len(a_names) + [False])
    rest_from_chips = chip_exchange(rest)
    na = len(a_names)
    halves = finish_sums(place, rest[:na] + ready["parts"] + rest[na:],
                         list(rest_from_chips[:na]) + list(early_from_chips) + list(rest_from_chips[na:]))

    res = {}
    for k, n in enumerate(BIG_NAMES):
        res[n] = adamw_update("adamw_" + n, place, halves[k:k + 1], w[n], vals["m_" + n], vals["v_" + n])
    for k, n in ((nb, "mlp_w_up"), (nb + 2, "mlp_w_down")):
        res[n] = adamw_update("adamw_" + n, place, halves[k:k + 2], w[n], vals["m_" + n], vals["v_" + n])
    packs = adamw_update("adamw_small", place, halves[-1:], w_pack[None], pack_small(lambda n: vals["m_" + n])[None],
                         pack_small(lambda n: vals["v_" + n])[None])
    shapes = {n: w[n].shape for n in WEIGHT_NAMES}
    small = [unpack_small(pk[0], shapes) for pk in packs]
    outs = [loss, gx[None]]
    for t in range(4):
        outs += [res[n][t] if n in res else small[t][n] for n in WEIGHT_NAMES]
    return tuple(outs)


def kernel(x, meta_tokens, a_mu, a_w_r, a_w_k, a_w_v, a_w_o, a_w0, a_w1, a_w2, a_a0, a_a1, a_a2, a_g1, a_g2, a_k_k,
           a_k_a, a_r_k, a_gn_w, a_gn_b, kv_w_k, kv_w_v, b_w_q, b_sinks, b_w_o, mlp_w_up, mlp_w_down, ln_g, ln_b,
           loss_target, m_meta_tokens, m_a_mu, m_a_w_r, m_a_w_k, m_a_w_v, m_a_w_o, m_a_w0, m_a_w1, m_a_w2, m_a_a0,
           m_a_a1, m_a_a2, m_a_g1, m_a_g2, m_a_k_k, m_a_k_a, m_a_r_k, m_a_gn_w, m_a_gn_b, m_kv_w_k, m_kv_w_v,
           m_b_w_q, m_b_sinks, m_b_w_o, m_mlp_w_up, m_mlp_w_down, m_ln_g, m_ln_b, v_meta_tokens, v_a_mu, v_a_w_r,
           v_a_w_k, v_a_w_v, v_a_w_o, v_a_w0, v_a_w1, v_a_w2, v_a_a0, v_a_a1, v_a_a2, v_a_g1, v_a_g2, v_a_k_k,
           v_a_k_a, v_a_r_k, v_a_gn_w, v_a_gn_b, v_kv_w_k, v_kv_w_v, v_b_w_q, v_b_sinks, v_b_w_o, v_mlp_w_up,
           v_mlp_w_down, v_ln_g, v_ln_b):
    return train_step(dict(locals()))
```

```python
import functools

import numpy as np
import jax
import jax.numpy as jnp
from jax import lax
from jax.experimental import pallas as pl
from jax.experimental.pallas import tpu as pltpu

F32 = jnp.float32
BF16 = jnp.bfloat16

D_MODEL = 1024
N_HEADS = 16
HEAD_DIM = 64
N_HEADS_KV = 4
GROUP = 4
KV_DIM = N_HEADS_KV * HEAD_DIM
N_META = 16
BLOCK = 128
PAD_FRONT = BLOCK - N_META
TOK0 = PAD_FRONT + N_META
N_FF_CHUNK = 4
N_SHARD = 4
GN_EPS = 64e-5
LN_EPS = 1e-5
ROPE_THETA = 10000.0
ALPHA = 4.0 ** 0.25
ADAM_LR, ADAM_B1, ADAM_B2, ADAM_EPS, ADAM_WD, ADAM_STEP = 0.001, 0.9, 0.999, 1e-08, 0.01, 10
SCAN_T = 64
PAIR = 128
KVW = GROUP * HEAD_DIM
VMEM_LIMIT = 60 * 1024 * 1024
MESH = pl.DeviceIdType.MESH


def _dot(a, b, ca, cb):
    return lax.dot_general(a.astype(BF16), b.astype(BF16), (((ca,), (cb,)), ((), ())),
                           preferred_element_type=F32)


@jax.custom_vjp
def mm(a, b):
    return _dot(a, b, 1, 0)


def _mm_fwd(a, b):
    return mm(a, b), b


def _mm_bwd(b, g):
    return _dot(g, b, 1, 1), jnp.zeros_like(b)


mm.defvjp(_mm_fwd, _mm_bwd)


@jax.custom_vjp
def mm_tap(a, b, tap):
    return _dot(a, b, 1, 0)


mm_tap.defvjp(lambda a, b, tap: (_dot(a, b, 1, 0), b), lambda b, g: (_dot(g, b, 1, 1), jnp.zeros_like(b), g))


def tmm(x, w, taps, xs):
    y = mm(x, w) if taps is None else mm_tap(x, w, taps[len(xs)])
    xs.append(x)
    return y


def vjp_taps(core, tap_shapes, args, cot):
    taps = [jnp.zeros(s, F32) for s in tap_shapes]
    _, vjp, xs = jax.vjp(core, taps, *args, has_aux=True)
    out = vjp(cot)
    return out[1:], [_dot(x, g, 0, 0) for x, g in zip(xs, out[0])]


def _split3(x):
    x1 = x.astype(BF16)
    r1 = x - x1.astype(F32)
    x2 = r1.astype(BF16)
    x3 = (r1 - x2.astype(F32)).astype(BF16)
    return x1, x2, x3


def _exact_dot(x, m01, cb=0):
    acc = None
    for piece in _split3(x)[:2]:
        t = lax.dot_general(piece, m01, (((1,), (cb,)), ((), ())), preferred_element_type=F32)
        acc = t if acc is None else acc + t
    return acc


def _head_matrices():
    e = np.zeros((D_MODEL, N_HEADS), np.float32)
    e[np.arange(D_MODEL), np.arange(D_MODEL) // HEAD_DIM] = 1.0
    return jnp.asarray(e, BF16), jnp.asarray(e.T, BF16)


@jax.custom_vjp
def hsum(x, e, et):
    return _exact_dot(x, e)


@jax.custom_vjp
def hbc(s, e, et):
    return _exact_dot(s, et)


hsum.defvjp(lambda x, e, et: (_exact_dot(x, e), (e, et)),
            lambda res, g: (hbc(g, *res), jnp.zeros_like(res[0]), jnp.zeros_like(res[1])))
hbc.defvjp(lambda s, e, et: (_exact_dot(s, et), (e, et)),
           lambda res, g: (hsum(g, *res), jnp.zeros_like(res[0]), jnp.zeros_like(res[1])))


def _sigmoid(u):
    return 0.5 * (jnp.tanh(0.5 * u) + 1.0)


def _softplus(u):
    return jnp.maximum(u, 0.0) + jnp.log(1.0 + jnp.exp(-jnp.abs(u)))


def _layer_norm(z, g, b):
    mu = jnp.mean(z, axis=-1, keepdims=True)
    zc = z - mu
    var = jnp.mean(zc * zc, axis=-1, keepdims=True)
    return zc * lax.rsqrt(var + LN_EPS) * g + b


def _zero_map(nd):
    return lambda c, i: (0,) * nd


def _params():
    return pltpu.CompilerParams(dimension_semantics=("arbitrary", "arbitrary"), vmem_limit_bytes=VMEM_LIMIT)


def rowwise(name, fn, rows, consts, out_rows, out_accs, tm, nc=1, hosted=None):
    lp = rows[0].shape[-2]
    nt = lp // tm
    assert nt * tm == lp, (name, lp, tm)
    copies_fn, hosted_src, hosted_shapes, hosted_scratch, hosted_post = hosted or (None, (), [], [], None)
    ng = len(hosted_src)
    in_specs, args = [], []
    for a in rows:
        if isinstance(a, tuple):
            a, block_rows, block_index = a
            in_specs.append(pl.BlockSpec((block_rows, a.shape[1]),
                                         functools.partial(lambda f, c, i: (f(i), 0), block_index)))
        elif a.ndim == 2:
            in_specs.append(pl.BlockSpec((tm, a.shape[1]), lambda c, i: (i, 0)))
        else:
            in_specs.append(pl.BlockSpec((a.shape[0], tm, a.shape[2]), lambda c, i: (0, i, 0)))
        args.append(a)
    for cst in consts:
        if isinstance(cst, tuple):
            arr, bs, im = cst
            in_specs.append(pl.BlockSpec(bs, im))
        else:
            arr = cst
            in_specs.append(pl.BlockSpec(arr.shape, _zero_map(arr.ndim), pipeline_mode=pl.Buffered(1)))
        args.append(arr)
    out_shape, out_specs, acc_per_chunk = [], [], []
    for spec in out_rows:
        if len(spec) == 4:
            out_shape.append(jax.ShapeDtypeStruct((spec[3], lp, spec[0]), spec[1]))
            out_specs.append(pl.BlockSpec((spec[3], tm, spec[0]), lambda c, i: (0, i, 0)))
        elif len(spec) == 3 and spec[2]:
            out_shape.append(jax.ShapeDtypeStruct((nc, lp, spec[0]), spec[1]))
            out_specs.append(pl.BlockSpec((None, tm, spec[0]), lambda c, i: (c, i, 0)))
        else:
            out_shape.append(jax.ShapeDtypeStruct((lp, spec[0]), spec[1]))
            out_specs.append(pl.BlockSpec((tm, spec[0]), lambda c, i: (i, 0)))
    for spec in out_accs:
        out_shape.append(jax.ShapeDtypeStruct(spec[0], spec[1]))
        if len(spec) == 4:
            out_specs.append(pl.BlockSpec(spec[2], spec[3]))
            acc_per_chunk.append(True)
        else:
            out_specs.append(pl.BlockSpec(spec[0], _zero_map(len(spec[0])), pipeline_mode=pl.Buffered(1)))
            acc_per_chunk.append(False)
    n_in, n_or, n_out = len(args), len(out_rows), len(out_shape)

    def body(*refs):
        c = pl.program_id(0)
        i = pl.program_id(1)
        if ng:
            src, dst = refs[n_in:n_in + ng], refs[n_in + ng + n_out:n_in + 2 * ng + n_out]
            sends, arrivals, forwards, forwarded = copies_fn(src, dst, *refs[n_in + 2 * ng + n_out:])

            @pl.when(jnp.logical_and(c == 0, i == 0))
            def _():
                for cp in sends:
                    cp.start()

        vals = [r[...] for r in refs[:n_in]]
        outs_r, outs_a = fn(c, i, *vals)
        out_refs = refs[n_in + ng:n_in + ng + n_out]
        for ref, val in zip(out_refs[:n_or], outs_r):
            ref[...] = val.astype(ref.dtype)
        for ref, val, per_chunk in zip(out_refs[n_or:], outs_a, acc_per_chunk):
            first = (i == 0) if per_chunk else jnp.logical_and(i == 0, c == 0)

            @pl.when(first)
            def _():
                ref[...] = val.astype(ref.dtype)

            @pl.when(jnp.logical_not(first))
            def _():
                ref[...] += val.astype(ref.dtype)

        if ng:
            @pl.when(jnp.logical_and(c == nc - 1, i == max(nt - 3, 0)))
            def _():
                for k, landed in enumerate(arrivals):
                    landed.wait_recv()
                    if forwards:
                        forwards[k].start()

            @pl.when(jnp.logical_and(c == nc - 1, i == nt - 1))
            def _():
                for cp in forwarded:
                    cp.wait_recv()
                for cp in sends + forwards:
                    cp.wait_send()

    outs = pl.pallas_call(body, name=name, grid=(nc, nt), in_specs=in_specs + [ANY] * ng,
                          out_specs=out_specs + [ANY] * ng, out_shape=out_shape + list(hosted_shapes),
                          scratch_shapes=list(hosted_scratch), compiler_params=_params())(*args, *hosted_src)
    if ng:
        return outs[:n_or], outs[n_or:n_out], hosted_post(outs[n_out:])
    return outs[:n_or], outs[n_or:]


def _row_ids(i, tm):
    return i * tm + lax.broadcasted_iota(jnp.int32, (tm, 1), 0)


SUBLANES = 8


def _halo_before(arr, tm):
    return (arr, SUBLANES, lambda i: jnp.maximum(i * (tm // SUBLANES) - 1, 0))


def _halo_after(arr, tm):
    last = arr.shape[0] // SUBLANES - 1
    return (arr, SUBLANES, lambda i: jnp.minimum((i + 1) * (tm // SUBLANES), last))


def _pick_row(block8, row):
    rows = lax.broadcasted_iota(jnp.int32, block8.shape, 0)
    return jnp.sum(jnp.where(rows == row, block8, 0.0), axis=0, keepdims=True)


def _shift_down(x, before8, i):
    rows = lax.broadcasted_iota(jnp.int32, x.shape, 0)
    top = _pick_row(before8, SUBLANES - 1) * (i > 0).astype(F32)
    return jnp.where(rows == 0, top, pltpu.roll(x, 1, 0))


def _shift_up(x, after8, i, nt):
    rows = lax.broadcasted_iota(jnp.int32, x.shape, 0)
    bottom = _pick_row(after8, 0) * (i < nt - 1).astype(F32)
    return jnp.where(rows == x.shape[0] - 1, bottom, pltpu.roll(x, x.shape[0] - 1, 0))


LORA_DECAY, LORA_AAA, LORA_GATE = 64, 64, 128
PRE_TAPS = (D_MODEL, D_MODEL, D_MODEL, LORA_DECAY, D_MODEL, LORA_AAA, D_MODEL, LORA_GATE, D_MODEL)


def rwkv_pre(e, et, ws, taps, h, hp, mu_r, mu_w, mu_k, mu_v, mu_a, mu_g, w0, a0, k_k, k_a):
    w_r, w_k, w_v, w1, w2, a1, a2, g1, g2 = ws
    xs = []
    xx = hp - h
    r = tmm(h + xx * mu_r, w_r, taps, xs)
    k = tmm(h + xx * mu_k, w_k, taps, xs)
    v = tmm(h + xx * mu_v, w_v, taps, xs)
    wraw = -_softplus(-(w0 + tmm(jnp.tanh(tmm(h + xx * mu_w, w1, taps, xs)), w2, taps, xs))) - 0.5
    lw = -jnp.exp(wraw)
    a = _sigmoid(a0 + tmm(tmm(h + xx * mu_a, a1, taps, xs), a2, taps, xs))
    g = tmm(_sigmoid(tmm(h + xx * mu_g, g1, taps, xs)), g2, taps, xs)
    kk = k * k_k
    ss = hsum(kk * kk, e, et)
    pos = ss > 0.0
    nrm = jnp.where(pos, jnp.sqrt(jnp.where(pos, ss, 1.0)), 0.0)
    kk = kk * hbc(1.0 / jnp.maximum(nrm, 1e-12), e, et)
    k2 = k * (1.0 + (a - 1.0) * k_a)
    return (r, lw, k2, v, -kk, kk * a, g), xs


def rwkv_post(e, et, w_o, taps, y, r, k2, v, g, h0, gn_w, gn_b, rk, lg, lb):
    xs = []
    inv_n = 1.0 / HEAD_DIM
    yc = y - hbc(hsum(y, e, et) * inv_n, e, et)
    yv = hsum(yc * yc, e, et) * inv_n
    yn = yc * hbc(lax.rsqrt(yv + GN_EPS), e, et) * gn_w + gn_b
    bonus = hbc(hsum(r * k2 * rk, e, et), e, et) * v
    mix = tmm((yn + bonus) * g, w_o, taps, xs)
    return _layer_norm(ALPHA * h0 + mix, lg, lb), xs


@jax.custom_vjp
def sq_relu(x):
    r = jnp.maximum(x, 0.0)
    return r * r


sq_relu.defvjp(lambda x: (sq_relu(x), x), lambda x, g: (g * (2.0 * jnp.maximum(x, 0.0)),))


def _rot_half(t):
    n = t.shape[-1]
    lane = lax.broadcasted_iota(jnp.int32, t.shape, t.ndim - 1)
    lo = (lane % HEAD_DIM) < (HEAD_DIM // 2)
    return jnp.where(lo, -pltpu.roll(t, n - HEAD_DIM // 2, t.ndim - 1), pltpu.roll(t, HEAD_DIM // 2, t.ndim - 1))


@jax.custom_vjp
def rot_half(t):
    return _rot_half(t)


rot_half.defvjp(lambda t: (_rot_half(t), None), lambda _, g: (-_rot_half(g),))


def _tile_lanes(t, width):
    return jnp.concatenate([t] * (width // t.shape[-1]), axis=-1)


def qkv_proj(cos, sin, wq, wk, wv, taps, h):
    xs = []
    q = tmm(h, wq, taps, xs)
    k = tmm(h, wk, taps, xs)
    v = tmm(h, wv, taps, xs)
    cq, sq = _tile_lanes(cos, D_MODEL), _tile_lanes(sin, D_MODEL)
    ck, sk = _tile_lanes(cos, KV_DIM), _tile_lanes(sin, KV_DIM)
    return (q * cq + rot_half(q) * sq, k * ck + rot_half(k) * sk, v), xs


def attn_out(w_o, taps, o, h, lg, lb):
    xs = []
    return _layer_norm(ALPHA * h + tmm(o, w_o, taps, xs), lg, lb), xs


def _scan_consts():
    t = SCAN_T
    tri = np.tril(np.ones((t, t), np.float32))
    rows = np.arange(2 * t)
    same = (rows[:, None] // t) == (rows[None, :] // t)
    strict = same & ((rows[None, :] % t) < (rows[:, None] % t))
    incl = same & ((rows[None, :] % t) <= (rows[:, None] % t))
    lane = np.arange(PAIR)
    masks = np.zeros((8, PAIR), np.float32)
    masks[0] = (lane // HEAD_DIM) == 0
    masks[1] = (lane // HEAD_DIM) == 1
    return (jnp.asarray(tri, BF16), jnp.asarray(strict.astype(np.float32)), jnp.asarray(incl.astype(np.float32)),
            jnp.asarray(masks), jnp.asarray(np.eye(2 * t, dtype=np.float32)))


def _scan_dot(a, b, ca, cb):
    return _dot(a, b, ca, cb)


@functools.partial(jax.custom_vjp, nondiff_argnums=(2, 3))
def _dotf(a, b, ca, cb):
    return _scan_dot(a, b, ca, cb)


def _dotf_bwd(ca, cb, res, g):
    a, b = res
    if ca == 1:
        da = _scan_dot(g, b, 1, 1 - cb)
    else:
        da = _scan_dot(b, g, 1 - cb, 1)
    if cb == 0:
        db = _scan_dot(a, g, 1 - ca, 0)
    else:
        db = _scan_dot(g, a, 0, 1 - ca)
    return da, db


_dotf.defvjp(lambda a, b, ca, cb: (_scan_dot(a, b, ca, cb), (a, b)), _dotf_bwd)


def _tri_dot(tri, x, ct):
    acc = None
    for piece in _split3(x):
        t = lax.dot_general(tri, piece, (((ct,), (0,)), ((), ())), preferred_element_type=F32)
        acc = t if acc is None else acc + t
    return acc


@jax.custom_vjp
def _cumsum_rows(tri, x):
    return _tri_dot(tri, x, 1)


_cumsum_rows.defvjp(lambda tri, x: (_tri_dot(tri, x, 1), tri),
                    lambda tri, g: (jnp.zeros_like(tri), _tri_dot(tri, g, 0)))


@jax.custom_vjp
def _unstack2(x):
    t = x.shape[0] // 2
    return x[:t] + x[t:]


_unstack2.defvjp(lambda x: (_unstack2(x), None), lambda _, g: (jnp.concatenate([g, g], axis=0),))


@jax.custom_vjp
def _last_row(x):
    return x[x.shape[0] - 1:, :]


def _last_row_bwd(_, g):
    rows = lax.broadcasted_iota(jnp.int32, (SCAN_T, g.shape[1]), 0)
    return (jnp.where(rows == SCAN_T - 1, jnp.broadcast_to(g, (SCAN_T, g.shape[1])), 0.0),)


_last_row.defvjp(lambda x: (_last_row(x), None), _last_row_bwd)


@jax.custom_vjp
def _halves(x):
    n = x.shape[0] // 2
    return x[:n], x[n:]


_halves.defvjp(lambda x: (_halves(x), None), lambda _, g: (jnp.concatenate(list(g), axis=0),))


@jax.custom_vjp
def _quads(x):
    n, m = x.shape[0] // 2, x.shape[1] // 2
    return x[:n, :m], x[:n, m:], x[n:, :m], x[n:, m:]


_quads.defvjp(lambda x: (_quads(x), None),
              lambda _, g: (jnp.concatenate([jnp.concatenate([g[0], g[1]], axis=1),
                                             jnp.concatenate([g[2], g[3]], axis=1)], axis=0),))


@jax.custom_vjp
def _solve_saved(n, rhs, minv, u):
    return u


def _solve_saved_bwd(res, du):
    minv, u = res
    drhs = _dotf(minv, du, 0, 0)
    return _dotf(drhs, u, 1, 1), drhs, jnp.zeros_like(minv), jnp.zeros_like(u)


_solve_saved.defvjp(lambda n, rhs, minv, u: (u, (minv, u)), _solve_saved_bwd)


def scan_chunk(tri, strict, incl, m0, m1, eye, r, lw, k, v, a, b, s0, saved=None):
    lower = strict > 0
    lower_incl = incl > 0

    def stack(x):
        return jnp.concatenate([x * m0, x * m1], axis=0)

    def dots(xs, ys, ca, cb, mask=None):
        out = [_dotf(x, y, ca, cb) for x, y in zip(xs, ys)]
        return out if mask is None else [jnp.where(mask, o, 0.0) for o in out]

    cl = [_cumsum_rows(tri, x) for x in lw]
    gam = [jnp.exp(c) for c in cl]
    ginv = [jnp.exp(-c) for c in cl]
    ar_s = [jnp.concatenate([stack(x * jnp.exp(c - w)), stack(y * g)], axis=0)
            for x, c, w, y, g in zip(a, cl, lw, r, gam)]
    bk_s = [jnp.concatenate([stack(x * g), stack(y * g)], axis=0) for x, y, g in zip(b, k, ginv)]
    v_s = [stack(x) for x in v]
    quads = [_quads(x) for x in dots(ar_s, bk_s, 1, 1)]
    n_ab = [jnp.where(lower, q[0], 0.0) for q in quads]
    n_ak = [jnp.where(lower, q[1], 0.0) for q in quads]
    r_ab = [jnp.where(lower_incl, q[2], 0.0) for q in quads]
    r_ak = [jnp.where(lower_incl, q[3], 0.0) for q in quads]
    from_state = [_halves(x) for x in dots(ar_s, s0, 1, 1)]
    rhs = [x[0] + y for x, y in zip(from_state, dots(n_ak, v_s, 1, 0))]
    if saved is None:
        minv = [eye + n for n in n_ab]
        p = n_ab
        for _ in range(5):
            p = dots(p, p, 1, 0)
            minv = [m + mp for m, mp in zip(minv, dots(minv, p, 1, 0))]
        u_s = dots(minv, rhs, 1, 0)
    else:
        minv = saved[0]
        u_s = [_solve_saved(n, x, m, u) for n, x, m, u in zip(n_ab, rhs, *saved)]
    uv_s = [jnp.concatenate([x, y], axis=0) for x, y in zip(u_s, v_s)]
    r_uv = [jnp.concatenate([x, y], axis=1) for x, y in zip(r_ab, r_ak)]
    y = [_unstack2(x[1] + z) for x, z in zip(from_state, dots(r_uv, uv_s, 1, 0))]
    g_end = [_last_row(g) for g in gam]
    s1 = [s * g + x for s, g, x in zip(s0, g_end, dots(uv_s, [x * g for x, g in zip(bk_s, g_end)], 0, 0))]
    return y, s1, (minv, u_s)


SCAN_PAIRS = 8


def _scan_specs(consts, order):
    row = pl.BlockSpec((SCAN_T, PAIR * SCAN_PAIRS), lambda p, c: (order(c), p))
    state = pl.BlockSpec((None, SCAN_PAIRS, PAIR, PAIR), lambda p, c: (order(c), p, 0, 0))
    return row, state, [pl.BlockSpec(x.shape, _zero_map(x.ndim)) for x in consts]


def _pair_lanes(q):
    return slice(q * PAIR, (q + 1) * PAIR)


def scan_fwd(r, lw, k, v, a, b, shards=()):
    lp = r.shape[0]
    nch = lp // SCAN_T
    npair = D_MODEL // PAIR
    ng = len(shards)
    consts = _scan_consts()
    row, state, cspecs = _scan_specs(consts, lambda c: c)

    def body(tri, strict, incl, masks, eye, r_ref, lw_ref, k_ref, v_ref, a_ref, b_ref, *rest):
        src, (y_ref, s_ref, minv_ref, u_ref), dst = rest[:ng], rest[ng:ng + 4], rest[ng + 4:2 * ng + 4]
        carry = rest[2 * ng + 4]
        first = jnp.logical_and(pl.program_id(0) == 0, pl.program_id(1) == 0)
        last = jnp.logical_and(pl.program_id(0) == npair // SCAN_PAIRS - 1, pl.program_id(1) == nch - 1)
        if ng:
            sends, arrivals, forwards, forwarded = gather_copies(src, dst, *rest[2 * ng + 5:])

            @pl.when(first)
            def _():
                for cp in sends:
                    cp.start()

            @pl.when(jnp.logical_and(pl.program_id(0) == npair // SCAN_PAIRS - 1, pl.program_id(1) == nch * 3 // 4))
            def _():
                for landed, onward in zip(arrivals, forwards):
                    landed.wait_recv()
                    onward.start()

        @pl.when(pl.program_id(1) == 0)
        def _():
            carry[...] = jnp.zeros_like(carry)

        pairs = range(SCAN_PAIRS)
        s0 = [carry[q] for q in pairs]
        rows = [[ref[:, _pair_lanes(q)] for q in pairs] for ref in (r_ref, lw_ref, k_ref, v_ref, a_ref, b_ref)]
        y, s1, (minv, u) = scan_chunk(tri[...], strict[...], incl[...], masks[0:1, :], masks[1:2, :], eye[...],
                                      *rows, s0)
        for q in pairs:
            s_ref[q] = s0[q]
            minv_ref[q] = minv[q]
            u_ref[q] = u[q]
            y_ref[:, _pair_lanes(q)] = y[q]
            carry[q] = s1[q]

        if ng:
            @pl.when(last)
            def _():
                for cp in forwarded:
                    cp.wait_recv()
                for cp in sends + forwards:
                    cp.wait_send()

    mats = jax.ShapeDtypeStruct((nch, npair, PAIR, PAIR), F32)
    out = pl.pallas_call(
        body, name="rwkv_scan_fwd", grid=(npair // SCAN_PAIRS, nch), in_specs=cspecs + [row] * 6 + [ANY] * ng,
        out_specs=[row, state, state, state] + [ANY] * ng,
        out_shape=[jax.ShapeDtypeStruct((lp, D_MODEL), F32), mats, mats, mats] + gathered_shapes(shards),
        scratch_shapes=[pltpu.VMEM((SCAN_PAIRS, PAIR, PAIR), F32)] + (gather_scratch(ng) if ng else []),
        compiler_params=_params(),
    )(*consts, r, lw, k, v, a, b, *shards)
    return out[:4], fill_own(out[4:], shards)


def scan_bwd(r, lw, k, v, a, b, saved, dy, direct_grads, parts=()):
    lp = r.shape[0]
    nch = lp // SCAN_T
    npair = D_MODEL // PAIR
    consts = _scan_consts()
    row, state, cspecs = _scan_specs(consts, lambda c: nch - 1 - c)

    ng = len(parts)

    def body(tri, strict, incl, masks, eye, r_ref, lw_ref, k_ref, v_ref, a_ref, b_ref, s_ref, minv_ref, u_ref,
             dy_ref, dr_in, dk_in, dv_in, *rest):
        src, (dr_ref, dlw_ref, dk_ref, dv_ref, da_ref, db_ref), dst = rest[:ng], rest[ng:ng + 6], rest[ng + 6:2 * ng + 6]
        carry = rest[2 * ng + 6]
        first = jnp.logical_and(pl.program_id(0) == 0, pl.program_id(1) == 0)
        last = jnp.logical_and(pl.program_id(0) == npair // SCAN_PAIRS - 1, pl.program_id(1) == nch - 1)
        if ng:
            sends, arrivals = chip_exchange_copies(src, dst, *rest[2 * ng + 7:])

            @pl.when(first)
            def _():
                for cp in sends:
                    cp.start()

        @pl.when(pl.program_id(1) == 0)
        def _():
            carry[...] = jnp.zeros_like(carry)

        pairs = range(SCAN_PAIRS)
        kept = ([minv_ref[q] for q in pairs], [u_ref[q] for q in pairs])

        def fn(*args):
            y, s1, _ = scan_chunk(tri[...], strict[...], incl[...], masks[0:1, :], masks[1:2, :], eye[...], *args,
                                  saved=kept)
            return y, s1

        rows = [[ref[:, _pair_lanes(q)] for q in pairs] for ref in (r_ref, lw_ref, k_ref, v_ref, a_ref, b_ref)]
        _, vjp = jax.vjp(fn, *rows, [s_ref[q] for q in pairs])
        grads = vjp(([dy_ref[:, _pair_lanes(q)] for q in pairs], [carry[q] for q in pairs]))
        direct = (dr_in, None, dk_in, dv_in, None, None)
        for q in pairs:
            ln = _pair_lanes(q)
            for ref, g, extra in zip((dr_ref, dlw_ref, dk_ref, dv_ref, da_ref, db_ref), grads[:6], direct):
                ref[:, ln] = g[q] if extra is None else g[q] + extra[:, ln]
            carry[q] = grads[6][q]

        if ng:
            @pl.when(last)
            def _():
                for cp in arrivals:
                    cp.wait_recv()
                for cp in sends:
                    cp.wait_send()

    out = pl.pallas_call(
        body, name="rwkv_scan_bwd", grid=(npair // SCAN_PAIRS, nch),
        in_specs=cspecs + [row] * 6 + [state] * 3 + [row] * 4 + [ANY] * ng, out_specs=[row] * 6 + [ANY] * ng,
        out_shape=[jax.ShapeDtypeStruct((lp, D_MODEL), F32)] * 6 + [jax.ShapeDtypeStruct(p.shape, p.dtype) for p in parts],
        scratch_shapes=[pltpu.VMEM((SCAN_PAIRS, PAIR, PAIR), F32)] + (_sem_scratch(ng * len(XY_FLIPS)) if ng else []),
        compiler_params=_params(),
    )(*consts, r, lw, k, v, a, b, *saved, dy, *direct_grads, *parts)
    return out[:6], out[6:]


def _spread_matrices():
    rep = np.zeros((N_HEADS_KV, KV_DIM, KVW), np.float32)
    for h in range(N_HEADS_KV):
        for g in range(GROUP):
            rep[h, h * HEAD_DIM + np.arange(HEAD_DIM), g * HEAD_DIM + np.arange(HEAD_DIM)] = 1.0
    return jnp.asarray(rep, BF16)


KV_HEADS = range(N_HEADS_KV)


def _attn_operands(q_ref, kp, kc, vp, vc, rep_ref):
    lane = lax.broadcasted_iota(jnp.int32, (1, KVW), 1)
    gmask = [(lane // HEAD_DIM == g).astype(F32) for g in range(GROUP)]
    kk = jnp.concatenate([kp, kc], axis=0)
    vv = jnp.concatenate([vp, vc], axis=0)
    qs = [q_ref[:, h * KVW:(h + 1) * KVW] for h in KV_HEADS]
    q_s = [jnp.concatenate([q * gmask[g] for g in range(GROUP)], axis=0) for q in qs]
    keys = [_dot(kk, rep_ref[h], 1, 0) for h in KV_HEADS]
    vals = [_dot(vv, rep_ref[h], 1, 0) for h in KV_HEADS]
    return gmask, q_s, keys, vals


def _attn_probs(n, q_s, keys, sink_ref):
    qi = lax.broadcasted_iota(jnp.int32, (GROUP * BLOCK, 2 * BLOCK), 0) % BLOCK
    kj = lax.broadcasted_iota(jnp.int32, (GROUP * BLOCK, 2 * BLOCK), 1)
    rel = BLOCK + qi - kj
    valid = (rel >= 0) & (rel < BLOCK) & ((n - 1) * BLOCK + kj >= PAD_FRONT)
    s = [jnp.where(valid, _dot(x, y, 1, 1) * (HEAD_DIM ** -0.5), -1e30) for x, y in zip(q_s, keys)]
    sink_col = [jnp.concatenate([jnp.broadcast_to(sink_ref[h, g:g + 1, 0:1], (BLOCK, 1)) for g in range(GROUP)],
                                axis=0) for h in KV_HEADS]
    m = [jnp.maximum(jnp.max(x, axis=-1, keepdims=True), c) for x, c in zip(s, sink_col)]
    ex = [jnp.exp(x - y) for x, y in zip(s, m)]
    ex_sink = [jnp.exp(c - y) for c, y in zip(sink_col, m)]
    inv = [1.0 / (jnp.sum(x, axis=-1, keepdims=True) + c) for x, c in zip(ex, ex_sink)]
    return [x * y for x, y in zip(ex, inv)], [x * y for x, y in zip(ex_sink, inv)]


def _unstack_groups(x_s, gmask):
    out = None
    for g in range(GROUP):
        t = x_s[g * BLOCK:(g + 1) * BLOCK] * gmask[g]
        out = t if out is None else out + t
    return out


def _attn_specs():
    qspec = pl.BlockSpec((BLOCK, D_MODEL), lambda n: (n, 0))
    cur = pl.BlockSpec((BLOCK, KV_DIM), lambda n: (n, 0))
    prev = pl.BlockSpec((BLOCK, KV_DIM), lambda n: (jnp.maximum(n - 1, 0), 0))
    rep = pl.BlockSpec((N_HEADS_KV, KV_DIM, KVW), lambda n: (0, 0, 0))
    sink = pl.BlockSpec((N_HEADS_KV, 8, PAIR), lambda n: (0, 0, 0))
    return qspec, cur, prev, rep, sink


def _attn_params():
    return pltpu.CompilerParams(dimension_semantics=("arbitrary",), vmem_limit_bytes=VMEM_LIMIT)


def _prob_specs():
    rows = GROUP * BLOCK
    return (pl.BlockSpec((None, N_HEADS_KV, rows, 2 * BLOCK), lambda n: (n, 0, 0, 0)),
            pl.BlockSpec((None, rows, PAIR), lambda n: (n, 0, 0)))


SINK_LANES = PAIR // N_HEADS_KV


def attn_fwd(q, k, v, sinks_b):
    lp = q.shape[0]
    nb = lp // BLOCK
    qspec, cur, prev, rep, sink = _attn_specs()

    def body(q_ref, kp_ref, kc_ref, vp_ref, vc_ref, rep_ref, sink_ref, o_ref, p_ref, ps_ref):
        gmask, q_s, keys, vals = _attn_operands(q_ref, kp_ref[...], kc_ref[...], vp_ref[...], vc_ref[...], rep_ref)
        p, p_sink = _attn_probs(pl.program_id(0), q_s, keys, sink_ref)
        o = [_dot(x, y, 1, 0) for x, y in zip(p, vals)]
        for h in KV_HEADS:
            o_ref[:, h * KVW:(h + 1) * KVW] = _unstack_groups(o[h], gmask)
            p_ref[h] = p[h].astype(BF16)
        head = lax.broadcasted_iota(jnp.int32, (GROUP * BLOCK, PAIR), 1) // SINK_LANES
        packed = p_sink[N_HEADS_KV - 1]
        for h in reversed(range(N_HEADS_KV - 1)):
            packed = jnp.where(head == h, p_sink[h], packed)
        ps_ref[...] = packed

    return pl.pallas_call(
        body, name="swa_fwd", grid=(nb,), in_specs=[qspec, prev, cur, prev, cur, rep, sink],
        out_specs=[qspec, *_prob_specs()],
        out_shape=[jax.ShapeDtypeStruct((lp, D_MODEL), F32),
                   jax.ShapeDtypeStruct((nb, N_HEADS_KV, GROUP * BLOCK, 2 * BLOCK), BF16),
                   jax.ShapeDtypeStruct((nb, GROUP * BLOCK, PAIR), F32)],
        compiler_params=_attn_params(),
    )(q, k, k, v, v, _spread_matrices(), sinks_b)


def attn_bwd(q, k, v, probs, do):
    lp = q.shape[0]
    qspec, cur, prev, rep, sink = _attn_specs()

    def body(q_ref, kp_ref, kc_ref, vp_ref, vc_ref, rep_ref, p_ref, ps_ref, do_ref, dq_ref, dkc_ref, dkp_ref, dvc_ref,
             dvp_ref, dsink_ref):
        n = pl.program_id(0)
        gmask, q_s, keys, vals = _attn_operands(q_ref, kp_ref[...], kc_ref[...], vp_ref[...], vc_ref[...], rep_ref)
        p = [p_ref[h].astype(F32) for h in KV_HEADS]
        lane = lax.broadcasted_iota(jnp.int32, (GROUP * BLOCK, PAIR), 1)
        p_sink = [jnp.sum(jnp.where(lane == h * SINK_LANES, ps_ref[...], 0.0), axis=-1, keepdims=True)
                  for h in KV_HEADS]
        do_s = [jnp.concatenate([do_ref[:, h * KVW:(h + 1) * KVW] * gmask[g] for g in range(GROUP)], axis=0)
                for h in KV_HEADS]
        dp = [_dot(x, y, 1, 1) for x, y in zip(do_s, vals)]
        delta = [jnp.sum(x * y, axis=-1, keepdims=True) for x, y in zip(p, dp)]
        ds = [x * (y - z) * (HEAD_DIM ** -0.5) for x, y, z in zip(p, dp, delta)]
        dq = [_dot(x, y, 1, 0) for x, y in zip(ds, keys)]
        dkeys_s = [_dot(x, y, 0, 0) for x, y in zip(ds, q_s)]
        dvals_s = [_dot(x, y, 0, 0) for x, y in zip(p, do_s)]
        dkeys = [_exact_dot(x, rep_ref[h], cb=1) for h, x in enumerate(dkeys_s)]
        dvals = [_exact_dot(x, rep_ref[h], cb=1) for h, x in enumerate(dvals_s)]
        dk_all = (dkeys[0] + dkeys[1]) + (dkeys[2] + dkeys[3])
        dv_all = (dvals[0] + dvals[1]) + (dvals[2] + dvals[3])
        dkp_ref[...] = dk_all[:BLOCK]
        dkc_ref[...] = dk_all[BLOCK:]
        dvp_ref[...] = dv_all[:BLOCK]
        dvc_ref[...] = dv_all[BLOCK:]
        dsinks = []
        for h in KV_HEADS:
            dq_ref[:, h * KVW:(h + 1) * KVW] = _unstack_groups(dq[h], gmask)
            dsk = -(p_sink[h] * delta[h])
            rows = [jnp.broadcast_to(jnp.sum(dsk[g * BLOCK:(g + 1) * BLOCK], axis=0, keepdims=True), (1, PAIR))
                    for g in range(GROUP)]
            dsinks.append(jnp.concatenate(rows + [jnp.zeros((8 - GROUP, PAIR), F32)], axis=0))

        @pl.when(n == 0)
        def _():
            for h in KV_HEADS:
                dsink_ref[h] = dsinks[h]

        @pl.when(n > 0)
        def _():
            for h in KV_HEADS:
                dsink_ref[h] += dsinks[h]

    kv = jax.ShapeDtypeStruct((lp, KV_DIM), F32)
    return pl.pallas_call(
        body, name="swa_bwd", grid=(lp // BLOCK,), in_specs=[qspec, prev, cur, prev, cur, rep, *_prob_specs(), qspec],
        out_specs=[qspec, cur, cur, cur, cur, sink],
        out_shape=[jax.ShapeDtypeStruct((lp, D_MODEL), F32), kv, kv, kv, kv,
                   jax.ShapeDtypeStruct((N_HEADS_KV, 8, PAIR), F32)],
        compiler_params=_attn_params(),
    )(q, k, k, v, v, _spread_matrices(), *probs, do)


def _pick_tm(lp, want):
    for tm in (384, 192, 128, 64):
        if tm <= want and lp % tm == 0:
            return tm
    raise ValueError(lp)


def _acc(shape):
    return (tuple(shape), F32)


def _ff_one(w):
    return (w, (None, D_MODEL, D_MODEL), lambda c, i: (c, 0, 0))


def _mlp_layer_fwd(name, h, wup, wdown, lg, lb, tm):
    def fn(c, i, h, wup, wdown, lg, lb):
        out, pre = None, []
        for s in range(N_FF_CHUNK):
            u = mm(h, wup[s])
            pre.append(u.astype(BF16))
            t = mm(sq_relu(u), wdown[s])
            out = t if out is None else out + t
        z = ALPHA * h + out
        return (_layer_norm(z, lg, lb), z, jnp.stack(pre)), ()

    (h_out, z, pre), _ = rowwise(name, fn, [h], [wup, wdown, lg, lb],
                                 [(D_MODEL, F32), (D_MODEL, F32), (D_MODEL, BF16, False, N_FF_CHUNK)], [], tm)
    return h_out, z, pre


MLP_BWD_TILE = 528


def _mlp_layer_bwd(name, h_in, z, pre, dh_parts, wup, wdown, lg, lb, tm):
    n_parts = len(dh_parts)

    def fn_ln(c, i, z, *rest):
        dh = rest[0]
        for extra in rest[1:n_parts]:
            dh = dh + extra
        _, vjp = jax.vjp(_layer_norm, z, rest[n_parts], rest[n_parts + 1])
        dz, dlg, dlb = vjp(dh)
        return (dz,), (dlg, dlb)

    (dz,), (dlg, dlb) = rowwise(name + "_ln", fn_ln, [z] + list(dh_parts), [lg, lb], [(D_MODEL, F32)],
                                [_acc((1, D_MODEL)), _acc((1, D_MODEL))], tm)

    def fn_mlp(c, i, h, dz, wup, wdown, u):
        r = jnp.maximum(u.astype(F32), 0.0)
        du = _dot(dz, wdown, 1, 1) * (2.0 * r)
        return (_dot(du, wup, 1, 1),), (_dot(h, du, 0, 0), _dot(r * r, dz, 0, 0))

    aspec = ((N_FF_CHUNK, D_MODEL, D_MODEL), F32, (None, D_MODEL, D_MODEL), lambda c, i: (c, 0, 0))
    lp = h_in.shape[0]
    tile = MLP_BWD_TILE if lp % MLP_BWD_TILE == 0 else tm
    pre_chunk = (pre, (None, tile, D_MODEL), lambda c, i: (c, i, 0))
    (dx,), (dwup, dwdown) = rowwise(name + "_mm", fn_mlp, [h_in, dz], [_ff_one(wup), _ff_one(wdown), pre_chunk],
                                    [(D_MODEL, F32, True)], [aspec, aspec], tile, nc=N_FF_CHUNK)
    return dz, dx, dwup, dwdown, dlg, dlb


def _sum_parts(dz, dx):
    out = ALPHA * dz
    for s in range(N_FF_CHUNK):
        out = out + dx[s]
    return out


def local_step(x, loss_target, p, late=None, early_hook=None):
    seq = x.shape[0]
    lp = TOK0 + seq
    tm = _pick_tm(lp, 384)
    tms = _pick_tm(lp, 192)
    e, et = _head_matrices()
    h0 = jnp.concatenate([jnp.zeros((PAD_FRONT, D_MODEL), F32), p["meta_tokens"], x], axis=0)
    pos = jnp.maximum(jnp.arange(lp, dtype=F32) - PAD_FRONT, 0.0)
    inv_freq = 1.0 / (ROPE_THETA ** (jnp.arange(0, HEAD_DIM, 2, dtype=F32) / HEAD_DIM))
    ang = pos[:, None] * inv_freq[None, :]
    cos = jnp.tile(jnp.cos(ang), (1, PAIR // (HEAD_DIM // 2)))
    sin = jnp.tile(jnp.sin(ang), (1, PAIR // (HEAD_DIM // 2)))

    pre_vec = [p["a_mu"][j:j + 1] for j in range(6)] + [p["a_w0"], p["a_a0"], p["a_k_k"], p["a_k_a"]]
    pre_w = [p["a_w_r"], p["a_w_k"], p["a_w_v"], p["a_w1"], p["a_w2"], p["a_a1"], p["a_a2"], p["a_g1"], p["a_g2"]]
    n_vec = len(pre_vec)

    def fn_pre(c, i, h, before, e, et, *ws):
        return rwkv_pre(e, et, ws[n_vec:], None, h, _shift_down(h, before, i), *ws[:n_vec])[0], ()

    (r, lw, k2, v, an, bn, g), _, *pre_gathered = rowwise(
        "rwkv_pre", fn_pre, [h0, _halo_before(h0, tms)], [e, et] + pre_vec + pre_w, [(D_MODEL, F32)] * 7, [], tms,
        hosted=hosted_gather(late[0][0]) if late else None)
    (y, *scan_saved), scan_gathered = scan_fwd(r, lw, k2, v, an, bn, late[1][0] if late else ())
    if late:
        p = {**p, **late[0][1](pre_gathered[0]), **late[1][1](scan_gathered)}

    post_c = [p["a_w_o"], p["a_gn_w"], p["a_gn_b"], p["a_r_k"], p["ln_g00"], p["ln_b00"]]

    def fn_post(c, i, y, r, k2, v, g, h0, e, et, w_o, *vecs):
        return (rwkv_post(e, et, w_o, None, y, r, k2, v, g, h0, *vecs)[0],), ()

    (h1,), _ = rowwise("rwkv_post", fn_post, [y, r, k2, v, g, h0], [e, et] + post_c, [(D_MODEL, F32)], [], tm)
    h2, z2, pre2 = _mlp_layer_fwd("mlp0_fwd", h1, p["mlp_up0"], p["mlp_down0"], p["ln_g01"], p["ln_b01"], tm)

    qkv_w = [p["b_w_q"], p["kv_w_k"], p["kv_w_v"]]

    def fn_qkv(c, i, h, cos, sin, wq, wk, wv):
        return qkv_proj(cos, sin, wq, wk, wv, None, h)[0], ()

    (q, k, vv), _ = rowwise("qkv_proj", fn_qkv, [h2, cos, sin], qkv_w,
                            [(D_MODEL, F32), (KV_DIM, F32), (KV_DIM, F32)], [], tm)
    sinks_b = jnp.broadcast_to(p["b_sinks"].reshape(N_HEADS_KV, GROUP, 1), (N_HEADS_KV, GROUP, PAIR))
    sinks_b = jnp.concatenate([sinks_b, jnp.zeros((N_HEADS_KV, 8 - GROUP, PAIR), F32)], axis=1)
    o, *attn_probs = attn_fwd(q, k, vv, sinks_b)

    ao_c = [p["b_w_o"], p["ln_g10"], p["ln_b10"]]

    def fn_ao(c, i, o, h, w_o, lg, lb):
        return (attn_out(w_o, None, o, h, lg, lb)[0],), ()

    (h3,), _ = rowwise("attn_out", fn_ao, [o, h2], ao_c, [(D_MODEL, F32)], [], tm)
    h4, z4, pre4 = _mlp_layer_fwd("mlp1_fwd", h3, p["mlp_up1"], p["mlp_down1"], p["ln_g11"], p["ln_b11"], tm)

    per = tm // TOK0

    def fn_loss(c, i, h4, *tgt_blocks):
        real = (_row_ids(i, tm) >= TOK0).astype(F32)
        err = (h4 - jnp.concatenate(tgt_blocks, axis=0)) * real
        part = 0.5 * jnp.sum(jnp.sum(err * err, axis=-1, keepdims=True), axis=0, keepdims=True) / D_MODEL
        return (err * (1.0 / D_MODEL),), (jnp.broadcast_to(part, (8, PAIR)),)

    tgt_blocks = [(loss_target, TOK0, functools.partial(lambda j, i: jnp.maximum(i * per + j - 1, 0), j))
                  for j in range(per)]
    (dh4,), (loss_acc,) = rowwise("loss", fn_loss, [h4] + tgt_blocks, [], [(D_MODEL, F32)], [_acc((8, PAIR))], tm)
    loss = loss_acc[0, 0]

    grads = {}
    dz4, dx4, grads["mlp_up1"], grads["mlp_down1"], grads["ln_g11"], grads["ln_b11"] = _mlp_layer_bwd(
        "mlp1_bwd", h3, z4, pre4, [dh4], p["mlp_up1"], p["mlp_down1"], p["ln_g11"], p["ln_b11"], tm)

    def fn_ao_b(c, i, dz, dx, o, h, w_o, lg, lb):
        (do, dh, dlg, dlb), (dw_o,) = vjp_taps(functools.partial(attn_out, w_o), [(tm, D_MODEL)], [o, h, lg, lb],
                                               _sum_parts(dz, dx))
        return (do, dh), (dw_o, dlg, dlb)

    (do, dh2_a), (grads["b_w_o"], grads["ln_g10"], grads["ln_b10"]) = rowwise(
        "attn_out_bwd", fn_ao_b, [dz4, dx4, o, h2], ao_c, [(D_MODEL, F32)] * 2,
        [_acc((D_MODEL, D_MODEL)), _acc((1, D_MODEL)), _acc((1, D_MODEL))], tm)

    dq, dkc, dkp, dvc, dvp, dsinks = attn_bwd(q, k, vv, attn_probs, do)
    grads["b_sinks"] = dsinks[:, :GROUP, 0].reshape(1, N_HEADS)
    zblk = jnp.zeros((BLOCK, KV_DIM), F32)
    dkp_s = jnp.concatenate([dkp[BLOCK:], zblk], axis=0)
    dvp_s = jnp.concatenate([dvp[BLOCK:], zblk], axis=0)

    def fn_qkv_b(c, i, h, cos, sin, dq, dkc, dkp, dvc, dvp, wq, wk, wv):
        return vjp_taps(functools.partial(qkv_proj, cos, sin, wq, wk, wv),
                        [(tm, D_MODEL), (tm, KV_DIM), (tm, KV_DIM)], [h], (dq, dkc + dkp, dvc + dvp))

    (dh2_q,), (grads["b_w_q"], grads["kv_w_k"], grads["kv_w_v"]) = rowwise(
        "qkv_proj_bwd", fn_qkv_b, [h2, cos, sin, dq, dkc, dkp_s, dvc, dvp_s], qkv_w, [(D_MODEL, F32)],
        [_acc((D_MODEL, D_MODEL)), _acc((D_MODEL, KV_DIM)), _acc((D_MODEL, KV_DIM))], tm)

    dz2, dx2, grads["mlp_up0"], grads["mlp_down0"], grads["ln_g01"], grads["ln_b01"] = _mlp_layer_bwd(
        "mlp0_bwd", h1, z2, pre2, [dh2_a, dh2_q], p["mlp_up0"], p["mlp_down0"], p["ln_g01"], p["ln_b01"], tm)

    def fn_post_b(c, i, dz, dx, y, r, k2, v, g, h0, e, et, w_o, *vecs):
        out, dws = vjp_taps(functools.partial(rwkv_post, e, et, w_o), [(tms, D_MODEL)],
                            [y, r, k2, v, g, h0] + list(vecs), _sum_parts(dz, dx))
        return out[:6], tuple(dws) + tuple(out[6:])

    early_srcs = early_hook[0](grads) if early_hook else ()
    (dy, dr_c, dk_c, dv_c, dg, dh0_c), post_g, *early_got = rowwise(
        "rwkv_post_bwd", fn_post_b, [dz2, dx2, y, r, k2, v, g, h0], [e, et] + post_c, [(D_MODEL, F32)] * 6,
        [_acc((D_MODEL, D_MODEL))] + [_acc((1, D_MODEL))] * 5, tms,
        hosted=hosted_pair_exchange(early_srcs) if early_hook else None)
    for name, val in zip(["a_w_o", "a_gn_w", "a_gn_b", "a_r_k", "ln_g00", "ln_b00"], post_g):
        grads[name] = val

    (dr, dlw, dk2, dv, dan, dbn), early_from_chips = scan_bwd(
        r, lw, k2, v, an, bn, scan_saved, dy, (dr_c, dk_c, dv_c),
        early_hook[1](early_srcs, early_got[0]) if early_hook else ())

    def fn_pre_b(c, i, h, before, dr, dlw, dk2, dv, dan, dbn, dg, e, et, *ws):
        hp = _shift_down(h, before, i)
        real = (_row_ids(i, tms) >= PAD_FRONT).astype(F32)
        cot = tuple(t * real for t in (dr, dlw, dk2, dv, dan, dbn, dg))
        out, dws = vjp_taps(functools.partial(rwkv_pre, e, et, ws[n_vec:]), [(tms, n) for n in PRE_TAPS],
                            [h, hp] + list(ws[:n_vec]), cot)
        return out[:2], tuple(out[2:]) + tuple(dws)

    (dh0_p, dhp), pre_g = rowwise(
        "rwkv_pre_bwd", fn_pre_b, [h0, _halo_before(h0, tms), dr, dlw, dk2, dv, dan, dbn, dg],
        [e, et] + pre_vec + pre_w, [(D_MODEL, F32)] * 2,
        [_acc((1, D_MODEL))] * n_vec + [_acc(w.shape) for w in pre_w], tms)
    grads["a_mu"] = jnp.concatenate(pre_g[:6], axis=0)
    for name, val in zip(["a_w0", "a_a0", "a_k_k", "a_k_a", "a_w_r", "a_w_k", "a_w_v", "a_w1", "a_w2", "a_a1",
                          "a_a2", "a_g1", "a_g2"], pre_g[6:]):
        grads[name] = val

    def fn_add(c, i, a, b, d, after):
        return (a + b + _shift_up(d, after, i, lp // tm),), ()

    (dh0,), _ = rowwise("grad_h0", fn_add, [dh0_c, dh0_p, dhp, _halo_after(dhp, tm)], [], [(D_MODEL, F32)], [], tm)
    grads["meta_tokens"] = dh0[PAD_FRONT:TOK0]
    return loss, dh0[TOK0:], grads, early_from_chips


ANY = pl.BlockSpec(memory_space=pl.ANY)
XY_FLIPS = ((0, 1), (1, 0), (1, 1))


def _flip(v, bit):
    return 1 - v if bit else v


def _sem_scratch(n):
    return [pltpu.SemaphoreType.DMA((n,)), pltpu.SemaphoreType.DMA((n,))]


def gather_copies(src, dst, ici_send, ici_recv, d2d_send, d2d_recv):
    npeer = len(XY_FLIPS)
    x, y, c = lax.axis_index("x"), lax.axis_index("y"), lax.axis_index("c")

    def half(ref, k, which):
        h = src[k].shape[0] // 2
        start = which * h
        return ref.at[pl.ds(pl.multiple_of(start, 8) if h % 8 == 0 else start, h)]

    def ici(k, j, slot):
        fx, fy = XY_FLIPS[j]
        return pltpu.make_async_remote_copy(
            src_ref=half(src[k], k, c), dst_ref=half(dst[k].at[slot], k, c), send_sem=ici_send.at[k * npeer + j],
            recv_sem=ici_recv.at[k * npeer + j], device_id=(_flip(x, fx), _flip(y, fy), c), device_id_type=MESH)

    def d2d(k, j, which):
        fx, fy = XY_FLIPS[j]
        landed = half(dst[k].at[2 * _flip(x, fx) + _flip(y, fy)], k, which)
        return pltpu.make_async_remote_copy(
            src_ref=landed, dst_ref=landed, send_sem=d2d_send.at[k * npeer + j], recv_sem=d2d_recv.at[k * npeer + j],
            device_id=(x, y, 1 - c), device_id_type=MESH)

    pairs = [(k, j) for k in range(len(src)) for j in range(npeer)]
    return ([ici(k, j, 2 * x + y) for k, j in pairs],
            [ici(k, j, 2 * _flip(x, XY_FLIPS[j][0]) + _flip(y, XY_FLIPS[j][1])) for k, j in pairs],
            [d2d(k, j, c) for k, j in pairs], [d2d(k, j, 1 - c) for k, j in pairs])


def gather_scratch(n):
    return _sem_scratch(n * len(XY_FLIPS)) * 2


def gathered_shapes(shards):
    return [jax.ShapeDtypeStruct((N_SHARD,) + s.shape, s.dtype) for s in shards]


def fill_own(gathered, shards):
    if not shards:
        return []
    slot = 2 * lax.axis_index("x") + lax.axis_index("y")
    return [lax.dynamic_update_index_in_dim(g, s, slot, 0) for g, s in zip(gathered, shards)]


def all_gather_shards(shards):
    n = len(shards)

    def body(*refs):
        sends, arrivals, forwards, forwarded = gather_copies(refs[:n], refs[n:2 * n], *refs[2 * n:])
        for cp in sends:
            cp.start()
        for landed, onward in zip(arrivals, forwards):
            landed.wait_recv()
            onward.start()
        for cp in forwarded:
            cp.wait_recv()
        for cp in sends + forwards:
            cp.wait_send()

    out = pl.pallas_call(body, name="gather_weights", in_specs=[ANY] * n, out_specs=[ANY] * n,
                         out_shape=gathered_shapes(shards), scratch_shapes=gather_scratch(n))(*shards)
    return fill_own(out, shards)


def placement():
    x, y, c = lax.axis_index("x"), lax.axis_index("y"), lax.axis_index("c")
    me = 2 * x + y
    others = [j + (j >= me).astype(jnp.int32) for j in range(N_SHARD - 1)]
    return jnp.stack([c, me] + others).astype(jnp.int32)


def hosted_gather(shards):
    return (gather_copies, list(shards), gathered_shapes(shards), gather_scratch(len(shards)),
            lambda got: fill_own(got, shards))


def pair_exchange_copies(src, got, send_sems, recv_sems):
    x, y, c = lax.axis_index("x"), lax.axis_index("y"), lax.axis_index("c")

    def copy(k):
        half = src[k].shape[1] // 2
        theirs = src[k].at[:, pl.ds(pl.multiple_of((1 - c) * half, 8), half), :]
        return pltpu.make_async_remote_copy(
            src_ref=theirs, dst_ref=got[k], send_sem=send_sems.at[k], recv_sem=recv_sems.at[k],
            device_id=(x, y, 1 - c), device_id_type=MESH)

    sends = [copy(k) for k in range(len(src))]
    return sends, sends, [], []


def _half_shapes(sources):
    return [jax.ShapeDtypeStruct((s.shape[0], s.shape[1] // 2, s.shape[2]), s.dtype) for s in sources]


def hosted_pair_exchange(sources):
    return (pair_exchange_copies, list(sources), _half_shapes(sources), _sem_scratch(len(sources)), list)


def pair_exchange(name, sources):
    n = len(sources)

    def body(*refs):
        sends, arrivals, _, _ = pair_exchange_copies(refs[:n], refs[n:2 * n], *refs[2 * n:])
        for cp in sends:
            cp.start()
        for cp in arrivals:
            cp.wait_recv()
        for cp in sends:
            cp.wait_send()

    halves = _half_shapes(sources)
    return pl.pallas_call(body, name=name, in_specs=[ANY] * n, out_specs=[ANY] * n,
                          out_shape=halves, scratch_shapes=_sem_scratch(n))(*sources)


def chip_exchange(parts):
    n = len(parts)

    def body(*refs):
        sends, arrivals = chip_exchange_copies(refs[:n], refs[n:2 * n], *refs[2 * n:])
        for cp in sends:
            cp.start()
        for cp in arrivals:
            cp.wait_recv()
        for cp in sends:
            cp.wait_send()

    return pl.pallas_call(
        body, name="grads_chip_exchange", in_specs=[ANY] * n, out_specs=[ANY] * n,
        out_shape=[jax.ShapeDtypeStruct(p.shape, p.dtype) for p in parts],
        scratch_shapes=_sem_scratch(n * len(XY_FLIPS)),
    )(*parts)


def chip_exchange_copies(src, dst, send_sems, recv_sems):
    npeer = len(XY_FLIPS)
    x, y, c = lax.axis_index("x"), lax.axis_index("y"), lax.axis_index("c")
    me = 2 * x + y

    def copy(k, j, sending):
        fx, fy = XY_FLIPS[j]
        px, py = _flip(x, fx), _flip(y, fy)
        peer = 2 * px + py
        return pltpu.make_async_remote_copy(
            src_ref=src[k].at[peer], dst_ref=dst[k].at[me if sending else peer],
            send_sem=send_sems.at[k * npeer + j], recv_sem=recv_sems.at[k * npeer + j],
            device_id=(px, py, c), device_id_type=MESH)

    pairs = [(k, j) for k in range(len(src)) for j in range(npeer)]
    return [copy(k, j, True) for k, j in pairs], [copy(k, j, False) for k, j in pairs]


def sibling_share(halves):
    n = len(halves)

    def body(*refs):
        src, got = refs[:n], refs[n:2 * n]
        send_sems, recv_sems = refs[2 * n:]
        x, y, c = lax.axis_index("x"), lax.axis_index("y"), lax.axis_index("c")
        sends = [pltpu.make_async_remote_copy(
            src_ref=src[k], dst_ref=got[k], send_sem=send_sems.at[k], recv_sem=recv_sems.at[k],
            device_id=(x, y, 1 - c), device_id_type=MESH) for k in range(n)]
        for cp in sends:
            cp.start()
        for cp in sends:
            cp.wait_recv()
        for cp in sends:
            cp.wait_send()

    return pl.pallas_call(
        body, name="grads_sibling_share", in_specs=[ANY] * n, out_specs=[ANY] * n,
        out_shape=[jax.ShapeDtypeStruct(h.shape, h.dtype) for h in halves], scratch_shapes=_sem_scratch(n),
    )(*halves)


ADD_TILE_ELEMS = 512 * 1024


def _row_tile(rows, cols):
    return max(t for t in range(8, rows + 1, 8) if rows % t == 0 and t * cols <= ADD_TILE_ELEMS)


def _prefetch_call(body, name, place, grid, in_specs, out_specs, out_shape, args):
    return pl.pallas_call(
        body, name=name, out_shape=out_shape,
        grid_spec=pltpu.PrefetchScalarGridSpec(num_scalar_prefetch=1, grid=grid, in_specs=in_specs,
                                               out_specs=out_specs),
        compiler_params=pltpu.CompilerParams(dimension_semantics=("arbitrary",) * len(grid),
                                             vmem_limit_bytes=VMEM_LIMIT),
    )(place, *args)


def pair_add(name, place, src, got, dtype):
    n4, half, cols = got.shape
    tile = _row_tile(half, cols)
    nt = half // tile

    def body(pr, a_ref, b_ref, o_ref):
        o_ref[...] = (a_ref[...] + b_ref[...]).astype(o_ref.dtype)

    mine = pl.BlockSpec((None, tile, cols), lambda s, i, pr: (s, pr[0] * nt + i, 0))
    blk = pl.BlockSpec((None, tile, cols), lambda s, i, pr: (s, i, 0))
    return _prefetch_call(body, name, place, (n4, nt), [mine, blk], blk,
                          jax.ShapeDtypeStruct(got.shape, dtype), (src, got))


def chip_add(name, place, part, from_chips):
    _, half, cols = part.shape
    tile = _row_tile(half, cols)

    def body(pr, own_ref, r0_ref, r1_ref, r2_ref, o_ref):
        me = pr[1]
        own, r0, r1, r2 = (r[...].astype(F32) for r in (own_ref, r0_ref, r1_ref, r2_ref))
        t0 = jnp.where(me == 0, own, r0)
        t1 = jnp.where(me == 0, r0, jnp.where(me == 1, own, r1))
        t2 = jnp.where(me <= 1, r1, jnp.where(me == 2, own, r2))
        t3 = jnp.where(me == 3, own, r2)
        o_ref[...] = ((t0 + t1) + t2) + t3

    def slab(j):
        return pl.BlockSpec((None, tile, cols), lambda i, pr: (pr[j], i, 0))

    return _prefetch_call(body, name, place, (half // tile,), [slab(1), slab(2), slab(3), slab(4)],
                          pl.BlockSpec((tile, cols), lambda i, pr: (i, 0)),
                          jax.ShapeDtypeStruct((half, cols), F32), (part, from_chips, from_chips, from_chips))


def pair_adds(tag, place, sources, got, narrow):
    return [pair_add(f"grads_pair_add_{tag}{k}", place, s, g, BF16 if nar else F32)
            for k, (s, g, nar) in enumerate(zip(sources, got, narrow))]


def finish_sums(place, parts, from_chips):
    halves = [chip_add(f"grads_chip_add{k}", place, p, f) for k, (p, f) in enumerate(zip(parts, from_chips))]
    return list(zip(halves, sibling_share(halves)))


ADAM_ROWS = 256


def adamw_update(name, place, halves, w, m, v):
    nsub, rows, cols = w.shape
    half = rows // 2
    tr = ADAM_ROWS if half % ADAM_ROWS == 0 else half
    nth = half // tr

    def body(pr, *refs):
        g_refs, (w_ref, m_ref, v_ref, g_ref, d_ref, nm_ref, nv_ref) = refs[:2 * nsub], refs[2 * nsub:]
        l = pl.program_id(0)
        mine = (pl.program_id(1) // nth) == pr[0]
        g = None
        for s in range(nsub):
            gs = jnp.where(mine, g_refs[2 * s][...], g_refs[2 * s + 1][...])
            g = gs if g is None else jnp.where(l == s, gs, g)
        m2 = ADAM_B1 * m_ref[...] + (1.0 - ADAM_B1) * g
        v2 = ADAM_B2 * v_ref[...] + (1.0 - ADAM_B2) * (g * g)
        m_hat = m2 / (1.0 - ADAM_B1 ** ADAM_STEP)
        v_hat = v2 / (1.0 - ADAM_B2 ** ADAM_STEP)
        g_ref[...] = g
        d_ref[...] = -ADAM_LR * (m_hat / (jnp.sqrt(v_hat) + ADAM_EPS) + ADAM_WD * w_ref[...])
        nm_ref[...] = m2
        nv_ref[...] = v2

    own = pl.BlockSpec((tr, cols), lambda l, i, pr: (jnp.where(i // nth == pr[0], i % nth, 0), 0))
    got = pl.BlockSpec((tr, cols), lambda l, i, pr: (jnp.where(i // nth == pr[0], 0, i % nth), 0))
    blk = pl.BlockSpec((None, tr, cols), lambda l, i, pr: (l, i, 0))
    out = jax.ShapeDtypeStruct((nsub, rows, cols), F32)
    return _prefetch_call(body, name, place, (nsub, rows // tr), [own, got] * nsub + [blk] * 3, [blk] * 4,
                          [out] * 4, [h for pair in halves for h in pair] + [w, m, v])


WEIGHT_NAMES = ("meta_tokens", "a_mu", "a_w_r", "a_w_k", "a_w_v", "a_w_o", "a_w0", "a_w1", "a_w2", "a_a0", "a_a1",
                "a_a2", "a_g1", "a_g2", "a_k_k", "a_k_a", "a_r_k", "a_gn_w", "a_gn_b", "kv_w_k", "kv_w_v", "b_w_q",
                "b_sinks", "b_w_o", "mlp_w_up", "mlp_w_down", "ln_g", "ln_b")
BIG_NAMES = ("a_w_r", "a_w_k", "a_w_v", "a_w_o", "b_w_q", "b_w_o")
EARLY_NAMES, LATE_NAMES = BIG_NAMES[:3], BIG_NAMES[3:]
PACK_MATS = (("kv_w_k", 256), ("kv_w_v", 256), ("a_w1", 64), ("a_a1", 64), ("a_g1", 128), ("a_w2", 64),
             ("a_a2", 64), ("a_g2", 128))
COLUMN_CUT = ("a_w2", "a_a2", "a_g2")
PACK_VECS = (("a_mu", 6), ("a_w0", 1), ("a_a0", 1), ("a_k_k", 1), ("a_k_a", 1), ("a_gn_w", 1), ("a_gn_b", 1),
             ("ln_g", 4), ("ln_b", 4), ("meta_tokens", 16))
PACK_REPL = (("a_r_k", 4), ("b_sinks", 1))
SHARD_W = D_MODEL // N_SHARD


def _tiles(rows):
    return -(-rows // SUBLANES) * SUBLANES


N_MAT_ROWS = sum(_tiles(r) for _, r in PACK_MATS)
N_VEC_ROWS = sum(_tiles(r) for _, r in PACK_VECS)
N_PACK_ROWS = -(-(N_MAT_ROWS + N_VEC_ROWS + sum(_tiles(r) for _, r in PACK_REPL)) // 16) * 16
N_GATHER_VEC_ROWS = -(-N_VEC_ROWS // 16) * 16


def _pad_rows(arr, axis):
    rows = arr.shape[axis]
    pad = [(0, 0)] * arr.ndim
    pad[axis] = (0, _tiles(rows) - rows)
    return jnp.pad(arr, pad) if _tiles(rows) != rows else arr


def _pack_rows(arr):
    if arr.size == N_HEADS:
        arr = jnp.pad(arr.reshape(1, N_HEADS), ((0, 0), (0, SHARD_W - N_HEADS)))
    return _pad_rows(arr.reshape(-1, SHARD_W), 0)


def pack_small(get):
    parts = [_pack_rows(get(name)) for name, _ in PACK_MATS + PACK_VECS + PACK_REPL]
    used = sum(p.shape[0] for p in parts)
    return jnp.concatenate(parts + [jnp.zeros((N_PACK_ROWS - used, SHARD_W), F32)], axis=0)


def unpack_small(pack, shapes):
    out, off = {}, 0
    for name, rows in PACK_MATS + PACK_VECS + PACK_REPL:
        piece = pack[off:off + rows]
        off += _tiles(rows)
        out[name] = piece[:, :N_HEADS].reshape(shapes[name]) if name == "b_sinks" else piece.reshape(shapes[name])
    return out


def whole_weights(big_names, gathered_big, mats, vecs, a_r_k, b_sinks):
    p = {name: g.reshape(D_MODEL, D_MODEL) for name, g in zip(big_names, gathered_big)}
    off = 0
    for name, rows in PACK_MATS:
        piece = mats[:, off:off + rows]
        off += rows
        if name in COLUMN_CUT:
            p[name] = piece.transpose(1, 0, 2).reshape(rows, D_MODEL)
        else:
            p[name] = piece.reshape(D_MODEL, rows)
    v = vecs.transpose(1, 0, 2).reshape(-1, D_MODEL)
    off = 0
    for name, rows in PACK_VECS:
        p[name] = v[off:off + rows]
        off += _tiles(rows)
    for i in range(2):
        for j in range(2):
            p[f"ln_g{i}{j}"] = p["ln_g"][2 * i + j:2 * i + j + 1]
            p[f"ln_b{i}{j}"] = p["ln_b"][2 * i + j:2 * i + j + 1]
    p["a_r_k"] = a_r_k.reshape(1, D_MODEL)
    p["b_sinks"] = b_sinks
    return p


def small_grad_pack(g):
    parts = []
    for name, rows in PACK_MATS:
        if name in COLUMN_CUT:
            parts.append(g[name].reshape(rows, N_SHARD, SHARD_W).transpose(1, 0, 2))
        else:
            parts.append(g[name].reshape(N_SHARD, rows, SHARD_W))
    vecs = {n: g[n] for n in ("a_mu", "a_w0", "a_a0", "a_k_k", "a_k_a", "a_gn_w", "a_gn_b", "meta_tokens")}
    vecs["ln_g"] = jnp.concatenate([g[f"ln_g{i}{j}"] for i in range(2) for j in range(2)], axis=0)
    vecs["ln_b"] = jnp.concatenate([g[f"ln_b{i}{j}"] for i in range(2) for j in range(2)], axis=0)
    for name, rows in PACK_VECS:
        parts.append(_pad_rows(vecs[name].reshape(rows, N_SHARD, SHARD_W).transpose(1, 0, 2), 1))
    r_k = jnp.broadcast_to(g["a_r_k"].reshape(1, -1, SHARD_W), (N_SHARD, D_MODEL // SHARD_W, SHARD_W))
    sinks = jnp.pad(g["b_sinks"].reshape(1, 1, N_HEADS), ((0, 0), (0, 0), (0, SHARD_W - N_HEADS)))
    parts += [_pad_rows(r_k, 1), _pad_rows(jnp.broadcast_to(sinks, (N_SHARD, 1, SHARD_W)), 1)]
    used = sum(p.shape[1] for p in parts)
    parts.append(jnp.zeros((N_SHARD, N_PACK_ROWS - used, SHARD_W), F32))
    return jnp.concatenate(parts, axis=1)


def train_step(vals):
    w = {n: vals[n] for n in WEIGHT_NAMES}
    w_pack = pack_small(lambda n: w[n])
    early = [w[n][0].astype(BF16) for n in EARLY_NAMES]
    early += [w_pack[:N_MAT_ROWS].astype(BF16), w_pack[N_MAT_ROWS:N_MAT_ROWS + N_GATHER_VEC_ROWS]]
    gathered = all_gather_shards(early)
    ne = len(EARLY_NAMES)
    p = whole_weights(EARLY_NAMES, gathered[:ne], gathered[ne], gathered[ne + 1][:, :N_VEC_ROWS], w["a_r_k"],
                      w["b_sinks"])
    nb = len(BIG_NAMES)

    def late_set(big, layer):
        shards = [w[n][0].astype(BF16) for n in big]
        shards += [w["mlp_w_up"][layer].astype(BF16), w["mlp_w_down"][layer].astype(BF16)]

        def weights(got):
            out = {n: x.reshape(D_MODEL, D_MODEL) for n, x in zip(big, got)}
            out[f"mlp_up{layer}"], out[f"mlp_down{layer}"] = got[len(big):]
            return out

        return shards, weights

    late = (late_set((), 1), late_set(LATE_NAMES, 0))

    place = placement()
    ready = {}
    a_names, b_names = BIG_NAMES[:4], BIG_NAMES[4:]

    def early_sources(g):
        return ([g[n].reshape(N_SHARD, SHARD_W, D_MODEL) for n in b_names]
                + [g["mlp_up0"], g["mlp_up1"], g["mlp_down0"], g["mlp_down1"]])

    def early_parts(srcs, got):
        ready["parts"] = pair_adds("early", place, srcs, got, [True] * len(srcs))
        return ready["parts"]

    loss, gx, g, early_from_chips = local_step(vals["x"][0], vals["loss_target"][0], p, late,
                                               (early_sources, early_parts))
    loss = lax.psum(loss, ("x", "y", "c"))
    srcs = [g[n].reshape(N_SHARD, SHARD_W, D_MODEL) for n in a_names] + [small_grad_pack(g)]
    rest = pair_adds("late", place, srcs, pair_exchange("grads_pair_exchange", srcs), [True] * len(a_names) + [False])
    rest_from_chips = chip_exchange(rest)
    na = len(a_names)
    halves = finish_sums(place, rest[:na] + ready["parts"] + rest[na:],
                         list(rest_from_chips[:na]) + list(early_from_chips) + list(rest_from_chips[na:]))

    res = {}
    for k, n in enumerate(BIG_NAMES):
        res[n] = adamw_update("adamw_" + n, place, halves[k:k + 1], w[n], vals["m_" + n], vals["v_" + n])
    for k, n in ((nb, "mlp_w_up"), (nb + 2, "mlp_w_down")):
        res[n] = adamw_update("adamw_" + n, place, halves[k:k + 2], w[n], vals["m_" + n], vals["v_" + n])
    packs = adamw_update("adamw_small", place, halves[-1:], w_pack[None], pack_small(lambda n: vals["m_" + n])[None],
                         pack_small(lambda n: vals["v_" + n])[None])
    shapes = {n: w[n].shape for n in WEIGHT_NAMES}
    small = [unpack_small(pk[0], shapes) for pk in packs]
    outs = [loss, gx[None]]
    for t in range(4):
        outs += [res[n][t] if n in res else small[t][n] for n in WEIGHT_NAMES]
    return tuple(outs)


def kernel(x, meta_tokens, a_mu, a_w_r, a_w_k, a_w_v, a_w_o, a_w0, a_w1, a_w2, a_a0, a_a1, a_a2, a_g1, a_g2, a_k_k,
           a_k_a, a_r_k, a_gn_w, a_gn_b, kv_w_k, kv_w_v, b_w_q, b_sinks, b_w_o, mlp_w_up, mlp_w_down, ln_g, ln_b,
           loss_target, m_meta_tokens, m_a_mu, m_a_w_r, m_a_w_k, m_a_w_v, m_a_w_o, m_a_w0, m_a_w1, m_a_w2, m_a_a0,
           m_a_a1, m_a_a2, m_a_g1, m_a_g2, m_a_k_k, m_a_k_a, m_a_r_k, m_a_gn_w, m_a_gn_b, m_kv_w_k, m_kv_w_v,
           m_b_w_q, m_b_sinks, m_b_w_o, m_mlp_w_up, m_mlp_w_down, m_ln_g, m_ln_b, v_meta_tokens, v_a_mu, v_a_w_r,
           v_a_w_k, v_a_w_v, v_a_w_o, v_a_w0, v_a_w1, v_a_w2, v_a_a0, v_a_a1, v_a_a2, v_a_g1, v_a_g2, v_a_k_k,
           v_a_k_a, v_a_r_k, v_a_gn_w, v_a_gn_b, v_kv_w_k, v_kv_w_v, v_b_w_q, v_b_sinks, v_b_w_o, v_mlp_w_up,
           v_mlp_w_down, v_ln_g, v_ln_b):
    return train_step(dict(locals()))
```

```python
import functools

import numpy as np
import jax
import jax.numpy as jnp
from jax import lax
from jax.experimental import pallas as pl
from jax.experimental.pallas import tpu as pltpu

F32 = jnp.float32
BF16 = jnp.bfloat16

D_MODEL = 1024
N_HEADS = 16
HEAD_DIM = 64
N_HEADS_KV = 4
GROUP = 4
KV_DIM = N_HEADS_KV * HEAD_DIM
N_META = 16
BLOCK = 128
PAD_FRONT = BLOCK - N_META
TOK0 = PAD_FRONT + N_META
N_FF_CHUNK = 4
N_SHARD = 4
GN_EPS = 64e-5
LN_EPS = 1e-5
ROPE_THETA = 10000.0
ALPHA = 4.0 ** 0.25
ADAM_LR, ADAM_B1, ADAM_B2, ADAM_EPS, ADAM_WD, ADAM_STEP = 0.001, 0.9, 0.999, 1e-08, 0.01, 10
SCAN_T = 64
PAIR = 128
KVW = GROUP * HEAD_DIM
VMEM_LIMIT = 60 * 1024 * 1024
MESH = pl.DeviceIdType.MESH


def _dot(a, b, ca, cb):
    return lax.dot_general(a.astype(BF16), b.astype(BF16), (((ca,), (cb,)), ((), ())),
                           preferred_element_type=F32)


@jax.custom_vjp
def mm(a, b):
    return _dot(a, b, 1, 0)


def _mm_fwd(a, b):
    return mm(a, b), b


def _mm_bwd(b, g):
    return _dot(g, b, 1, 1), jnp.zeros_like(b)


mm.defvjp(_mm_fwd, _mm_bwd)


@jax.custom_vjp
def mm_tap(a, b, tap):
    return _dot(a, b, 1, 0)


mm_tap.defvjp(lambda a, b, tap: (_dot(a, b, 1, 0), b), lambda b, g: (_dot(g, b, 1, 1), jnp.zeros_like(b), g))


def tmm(x, w, taps, xs):
    y = mm(x, w) if taps is None else mm_tap(x, w, taps[len(xs)])
    xs.append(x)
    return y


def vjp_taps(core, tap_shapes, args, cot):
    taps = [jnp.zeros(s, F32) for s in tap_shapes]
    _, vjp, xs = jax.vjp(core, taps, *args, has_aux=True)
    out = vjp(cot)
    return out[1:], [_dot(x, g, 0, 0) for x, g in zip(xs, out[0])]


def _split3(x):
    x1 = x.astype(BF16)
    r1 = x - x1.astype(F32)
    x2 = r1.astype(BF16)
    x3 = (r1 - x2.astype(F32)).astype(BF16)
    return x1, x2, x3


def _exact_dot(x, m01, cb=0):
    acc = None
    for piece in _split3(x)[:2]:
        t = lax.dot_general(piece, m01, (((1,), (cb,)), ((), ())), preferred_element_type=F32)
        acc = t if acc is None else acc + t
    return acc


def _head_matrices():
    e = np.zeros((D_MODEL, N_HEADS), np.float32)
    e[np.arange(D_MODEL), np.arange(D_MODEL) // HEAD_DIM] = 1.0
    return jnp.asarray(e, BF16), jnp.asarray(e.T, BF16)


@jax.custom_vjp
def hsum(x, e, et):
    return _exact_dot(x, e)


@jax.custom_vjp
def hbc(s, e, et):
    return _exact_dot(s, et)


hsum.defvjp(lambda x, e, et: (_exact_dot(x, e), (e, et)),
            lambda res, g: (hbc(g, *res), jnp.zeros_like(res[0]), jnp.zeros_like(res[1])))
hbc.defvjp(lambda s, e, et: (_exact_dot(s, et), (e, et)),
           lambda res, g: (hsum(g, *res), jnp.zeros_like(res[0]), jnp.zeros_like(res[1])))


def _sigmoid(u):
    return 0.5 * (jnp.tanh(0.5 * u) + 1.0)


def _softplus(u):
    return jnp.maximum(u, 0.0) + jnp.log(1.0 + jnp.exp(-jnp.abs(u)))


def _layer_norm(z, g, b):
    mu = jnp.mean(z, axis=-1, keepdims=True)
    zc = z - mu
    var = jnp.mean(zc * zc, axis=-1, keepdims=True)
    return zc * lax.rsqrt(var + LN_EPS) * g + b


def _zero_map(nd):
    return lambda c, i: (0,) * nd


def _params():
    return pltpu.CompilerParams(dimension_semantics=("arbitrary", "arbitrary"), vmem_limit_bytes=VMEM_LIMIT)


def rowwise(name, fn, rows, consts, out_rows, out_accs, tm, nc=1, hosted=None):
    lp = rows[0].shape[-2]
    nt = lp // tm
    assert nt * tm == lp, (name, lp, tm)
    copies_fn, hosted_src, hosted_shapes, hosted_scratch, hosted_post = hosted or (None, (), [], [], None)
    ng = len(hosted_src)
    in_specs, args = [], []
    for a in rows:
        if isinstance(a, tuple):
            a, block_rows, block_index = a
            in_specs.append(pl.BlockSpec((block_rows, a.shape[1]),
                                         functools.partial(lambda f, c, i: (f(i), 0), block_index)))
        elif a.ndim == 2:
            in_specs.append(pl.BlockSpec((tm, a.shape[1]), lambda c, i: (i, 0)))
        else:
            in_specs.append(pl.BlockSpec((a.shape[0], tm, a.shape[2]), lambda c, i: (0, i, 0)))
        args.append(a)
    for cst in consts:
        if isinstance(cst, tuple):
            arr, bs, im = cst
            in_specs.append(pl.BlockSpec(bs, im))
        else:
            arr = cst
            in_specs.append(pl.BlockSpec(arr.shape, _zero_map(arr.ndim), pipeline_mode=pl.Buffered(1)))
        args.append(arr)
    out_shape, out_specs, acc_per_chunk = [], [], []
    for spec in out_rows:
        if len(spec) == 4:
            out_shape.append(jax.ShapeDtypeStruct((spec[3], lp, spec[0]), spec[1]))
            out_specs.append(pl.BlockSpec((spec[3], tm, spec[0]), lambda c, i: (0, i, 0)))
        elif len(spec) == 3 and spec[2]:
            out_shape.append(jax.ShapeDtypeStruct((nc, lp, spec[0]), spec[1]))
            out_specs.append(pl.BlockSpec((None, tm, spec[0]), lambda c, i: (c, i, 0)))
        else:
            out_shape.append(jax.ShapeDtypeStruct((lp, spec[0]), spec[1]))
            out_specs.append(pl.BlockSpec((tm, spec[0]), lambda c, i: (i, 0)))
    for spec in out_accs:
        out_shape.append(jax.ShapeDtypeStruct(spec[0], spec[1]))
        if len(spec) == 4:
            out_specs.append(pl.BlockSpec(spec[2], spec[3]))
            acc_per_chunk.append(True)
        else:
            out_specs.append(pl.BlockSpec(spec[0], _zero_map(len(spec[0])), pipeline_mode=pl.Buffered(1)))
            acc_per_chunk.append(False)
    n_in, n_or, n_out = len(args), len(out_rows), len(out_shape)

    def body(*refs):
        c = pl.program_id(0)
        i = pl.program_id(1)
        if ng:
            src, dst = refs[n_in:n_in + ng], refs[n_in + ng + n_out:n_in + 2 * ng + n_out]
            sends, arrivals, forwards, forwarded = copies_fn(src, dst, *refs[n_in + 2 * ng + n_out:])

            @pl.when(jnp.logical_and(c == 0, i == 0))
            def _():
                for cp in sends:
                    cp.start()

        vals = [r[...] for r in refs[:n_in]]
        outs_r, outs_a = fn(c, i, *vals)
        out_refs = refs[n_in + ng:n_in + ng + n_out]
        for ref, val in zip(out_refs[:n_or], outs_r):
            ref[...] = val.astype(ref.dtype)
        for ref, val, per_chunk in zip(out_refs[n_or:], outs_a, acc_per_chunk):
            first = (i == 0) if per_chunk else jnp.logical_and(i == 0, c == 0)

            @pl.when(first)
            def _():
                ref[...] = val.astype(ref.dtype)

            @pl.when(jnp.logical_not(first))
            def _():
                ref[...] += val.astype(ref.dtype)

        if ng:
            @pl.when(jnp.logical_and(c == nc - 1, i == max(nt - 3, 0)))
            def _():
                for k, landed in enumerate(arrivals):
                    landed.wait_recv()
                    if forwards:
                        forwards[k].start()

            @pl.when(jnp.logical_and(c == nc - 1, i == nt - 1))
            def _():
                for cp in forwarded:
                    cp.wait_recv()
                for cp in sends + forwards:
                    cp.wait_send()

    outs = pl.pallas_call(body, name=name, grid=(nc, nt), in_specs=in_specs + [ANY] * ng,
                          out_specs=out_specs + [ANY] * ng, out_shape=out_shape + list(hosted_shapes),
                          scratch_shapes=list(hosted_scratch), compiler_params=_params())(*args, *hosted_src)
    if ng:
        return outs[:n_or], outs[n_or:n_out], hosted_post(outs[n_out:])
    return outs[:n_or], outs[n_or:]


def _row_ids(i, tm):
    return i * tm + lax.broadcasted_iota(jnp.int32, (tm, 1), 0)


SUBLANES = 8


def _halo_before(arr, tm):
    return (arr, SUBLANES, lambda i: jnp.maximum(i * (tm // SUBLANES) - 1, 0))


def _halo_after(arr, tm):
    last = arr.shape[0] // SUBLANES - 1
    return (arr, SUBLANES, lambda i: jnp.minimum((i + 1) * (tm // SUBLANES), last))


def _pick_row(block8, row):
    rows = lax.broadcasted_iota(jnp.int32, block8.shape, 0)
    return jnp.sum(jnp.where(rows == row, block8, 0.0), axis=0, keepdims=True)


def _shift_down(x, before8, i):
    rows = lax.broadcasted_iota(jnp.int32, x.shape, 0)
    top = _pick_row(before8, SUBLANES - 1) * (i > 0).astype(F32)
    return jnp.where(rows == 0, top, pltpu.roll(x, 1, 0))


def _shift_up(x, after8, i, nt):
    rows = lax.broadcasted_iota(jnp.int32, x.shape, 0)
    bottom = _pick_row(after8, 0) * (i < nt - 1).astype(F32)
    return jnp.where(rows == x.shape[0] - 1, bottom, pltpu.roll(x, x.shape[0] - 1, 0))


LORA_DECAY, LORA_AAA, LORA_GATE = 64, 64, 128
PRE_TAPS = (D_MODEL, D_MODEL, D_MODEL, LORA_DECAY, D_MODEL, LORA_AAA, D_MODEL, LORA_GATE, D_MODEL)


def rwkv_pre(e, et, ws, taps, h, hp, mu_r, mu_w, mu_k, mu_v, mu_a, mu_g, w0, a0, k_k, k_a):
    w_r, w_k, w_v, w1, w2, a1, a2, g1, g2 = ws
    xs = []
    xx = hp - h
    r = tmm(h + xx * mu_r, w_r, taps, xs)
    k = tmm(h + xx * mu_k, w_k, taps, xs)
    v = tmm(h + xx * mu_v, w_v, taps, xs)
    wraw = -_softplus(-(w0 + tmm(jnp.tanh(tmm(h + xx * mu_w, w1, taps, xs)), w2, taps, xs))) - 0.5
    lw = -jnp.exp(wraw)
    a = _sigmoid(a0 + tmm(tmm(h + xx * mu_a, a1, taps, xs), a2, taps, xs))
    g = tmm(_sigmoid(tmm(h + xx * mu_g, g1, taps, xs)), g2, taps, xs)
    kk = k * k_k
    ss = hsum(kk * kk, e, et)
    pos = ss > 0.0
    nrm = jnp.where(pos, jnp.sqrt(jnp.where(pos, ss, 1.0)), 0.0)
    kk = kk * hbc(1.0 / jnp.maximum(nrm, 1e-12), e, et)
    k2 = k * (1.0 + (a - 1.0) * k_a)
    return (r, lw, k2, v, -kk, kk * a, g), xs


def rwkv_post(e, et, w_o, taps, y, r, k2, v, g, h0, gn_w, gn_b, rk, lg, lb):
    xs = []
    inv_n = 1.0 / HEAD_DIM
    yc = y - hbc(hsum(y, e, et) * inv_n, e, et)
    yv = hsum(yc * yc, e, et) * inv_n
    yn = yc * hbc(lax.rsqrt(yv + GN_EPS), e, et) * gn_w + gn_b
    bonus = hbc(hsum(r * k2 * rk, e, et), e, et) * v
    mix = tmm((yn + bonus) * g, w_o, taps, xs)
    return _layer_norm(ALPHA * h0 + mix, lg, lb), xs


@jax.custom_vjp
def sq_relu(x):
    r = jnp.maximum(x, 0.0)
    return r * r


sq_relu.defvjp(lambda x: (sq_relu(x), x), lambda x, g: (g * (2.0 * jnp.maximum(x, 0.0)),))


def _rot_half(t):
    n = t.shape[-1]
    lane = lax.broadcasted_iota(jnp.int32, t.shape, t.ndim - 1)
    lo = (lane % HEAD_DIM) < (HEAD_DIM // 2)
    return jnp.where(lo, -pltpu.roll(t, n - HEAD_DIM // 2, t.ndim - 1), pltpu.roll(t, HEAD_DIM // 2, t.ndim - 1))


@jax.custom_vjp
def rot_half(t):
    return _rot_half(t)


rot_half.defvjp(lambda t: (_rot_half(t), None), lambda _, g: (-_rot_half(g),))


def _tile_lanes(t, width):
    return jnp.concatenate([t] * (width // t.shape[-1]), axis=-1)


def qkv_proj(cos, sin, wq, wk, wv, taps, h):
    xs = []
    q = tmm(h, wq, taps, xs)
    k = tmm(h, wk, taps, xs)
    v = tmm(h, wv, taps, xs)
    cq, sq = _tile_lanes(cos, D_MODEL), _tile_lanes(sin, D_MODEL)
    ck, sk = _tile_lanes(cos, KV_DIM), _tile_lanes(sin, KV_DIM)
    return (q * cq + rot_half(q) * sq, k * ck + rot_half(k) * sk, v), xs


def attn_out(w_o, taps, o, h, lg, lb):
    xs = []
    return _layer_norm(ALPHA * h + tmm(o, w_o, taps, xs), lg, lb), xs


def _scan_consts():
    t = SCAN_T
    tri = np.tril(np.ones((t, t), np.float32))
    rows = np.arange(2 * t)
    same = (rows[:, None] // t) == (rows[None, :] // t)
    strict = same & ((rows[None, :] % t) < (rows[:, None] % t))
    incl = same & ((rows[None, :] % t) <= (rows[:, None] % t))
    lane = np.arange(PAIR)
    masks = np.zeros((8, PAIR), np.float32)
    masks[0] = (lane // HEAD_DIM) == 0
    masks[1] = (lane // HEAD_DIM) == 1
    return (jnp.asarray(tri, BF16), jnp.asarray(strict.astype(np.float32)), jnp.asarray(incl.astype(np.float32)),
            jnp.asarray(masks), jnp.asarray(np.eye(2 * t, dtype=np.float32)))


def _scan_dot(a, b, ca, cb):
    return _dot(a, b, ca, cb)


@functools.partial(jax.custom_vjp, nondiff_argnums=(2, 3))
def _dotf(a, b, ca, cb):
    return _scan_dot(a, b, ca, cb)


def _dotf_bwd(ca, cb, res, g):
    a, b = res
    if ca == 1:
        da = _scan_dot(g, b, 1, 1 - cb)
    else:
        da = _scan_dot(b, g, 1 - cb, 1)
    if cb == 0:
        db = _scan_dot(a, g, 1 - ca, 0)
    else:
        db = _scan_dot(g, a, 0, 1 - ca)
    return da, db


_dotf.defvjp(lambda a, b, ca, cb: (_scan_dot(a, b, ca, cb), (a, b)), _dotf_bwd)


def _tri_dot(tri, x, ct):
    acc = None
    for piece in _split3(x):
        t = lax.dot_general(tri, piece, (((ct,), (0,)), ((), ())), preferred_element_type=F32)
        acc = t if acc is None else acc + t
    return acc


@jax.custom_vjp
def _cumsum_rows(tri, x):
    return _tri_dot(tri, x, 1)


_cumsum_rows.defvjp(lambda tri, x: (_tri_dot(tri, x, 1), tri),
                    lambda tri, g: (jnp.zeros_like(tri), _tri_dot(tri, g, 0)))


@jax.custom_vjp
def _unstack2(x):
    t = x.shape[0] // 2
    return x[:t] + x[t:]


_unstack2.defvjp(lambda x: (_unstack2(x), None), lambda _, g: (jnp.concatenate([g, g], axis=0),))


@jax.custom_vjp
def _last_row(x):
    return x[x.shape[0] - 1:, :]


def _last_row_bwd(_, g):
    rows = lax.broadcasted_iota(jnp.int32, (SCAN_T, g.shape[1]), 0)
    return (jnp.where(rows == SCAN_T - 1, jnp.broadcast_to(g, (SCAN_T, g.shape[1])), 0.0),)


_last_row.defvjp(lambda x: (_last_row(x), None), _last_row_bwd)


@jax.custom_vjp
def _halves(x):
    n = x.shape[0] // 2
    return x[:n], x[n:]


_halves.defvjp(lambda x: (_halves(x), None), lambda _, g: (jnp.concatenate(list(g), axis=0),))


@jax.custom_vjp
def _quads(x):
    n, m = x.shape[0] // 2, x.shape[1] // 2
    return x[:n, :m], x[:n, m:], x[n:, :m], x[n:, m:]


_quads.defvjp(lambda x: (_quads(x), None),
              lambda _, g: (jnp.concatenate([jnp.concatenate([g[0], g[1]], axis=1),
                                             jnp.concatenate([g[2], g[3]], axis=1)], axis=0),))


@jax.custom_vjp
def _solve_saved(n, rhs, minv, u):
    return u


def _solve_saved_bwd(res, du):
    minv, u = res
    drhs = _dotf(minv, du, 0, 0)
    return _dotf(drhs, u, 1, 1), drhs, jnp.zeros_like(minv), jnp.zeros_like(u)


_solve_saved.defvjp(lambda n, rhs, minv, u: (u, (minv, u)), _solve_saved_bwd)


def scan_chunk(tri, strict, incl, m0, m1, eye, r, lw, k, v, a, b, s0, saved=None):
    lower = strict > 0
    lower_incl = incl > 0

    def stack(x):
        return jnp.concatenate([x * m0, x * m1], axis=0)

    def dots(xs, ys, ca, cb, mask=None):
        out = [_dotf(x, y, ca, cb) for x, y in zip(xs, ys)]
        return out if mask is None else [jnp.where(mask, o, 0.0) for o in out]

    cl = [_cumsum_rows(tri, x) for x in lw]
    gam = [jnp.exp(c) for c in cl]
    ginv = [jnp.exp(-c) for c in cl]
    ar_s = [jnp.concatenate([stack(x * jnp.exp(c - w)), stack(y * g)], axis=0)
            for x, c, w, y, g in zip(a, cl, lw, r, gam)]
    bk_s = [jnp.concatenate([stack(x * g), stack(y * g)], axis=0) for x, y, g in zip(b, k, ginv)]
    v_s = [stack(x) for x in v]
    quads = [_quads(x) for x in dots(ar_s, bk_s, 1, 1)]
    n_ab = [jnp.where(lower, q[0], 0.0) for q in quads]
    n_ak = [jnp.where(lower, q[1], 0.0) for q in quads]
    r_ab = [jnp.where(lower_incl, q[2], 0.0) for q in quads]
    r_ak = [jnp.where(lower_incl, q[3], 0.0) for q in quads]
    from_state = [_halves(x) for x in dots(ar_s, s0, 1, 1)]
    rhs = [x[0] + y for x, y in zip(from_state, dots(n_ak, v_s, 1, 0))]
    if saved is None:
        minv = [eye + n for n in n_ab]
        p = n_ab
        for _ in range(5):
            p = dots(p, p, 1, 0)
            minv = [m + mp for m, mp in zip(minv, dots(minv, p, 1, 0))]
        u_s = dots(minv, rhs, 1, 0)
    else:
        minv = saved[0]
        u_s = [_solve_saved(n, x, m, u) for n, x, m, u in zip(n_ab, rhs, *saved)]
    uv_s = [jnp.concatenate([x, y], axis=0) for x, y in zip(u_s, v_s)]
    r_uv = [jnp.concatenate([x, y], axis=1) for x, y in zip(r_ab, r_ak)]
    y = [_unstack2(x[1] + z) for x, z in zip(from_state, dots(r_uv, uv_s, 1, 0))]
    g_end = [_last_row(g) for g in gam]
    s1 = [s * g + x for s, g, x in zip(s0, g_end, dots(uv_s, [x * g for x, g in zip(bk_s, g_end)], 0, 0))]
    return y, s1, (minv, u_s)


SCAN_PAIRS = 8


def _scan_specs(consts, order):
    row = pl.BlockSpec((SCAN_T, PAIR * SCAN_PAIRS), lambda p, c: (order(c), p))
    state = pl.BlockSpec((None, SCAN_PAIRS, PAIR, PAIR), lambda p, c: (order(c), p, 0, 0))
    return row, state, [pl.BlockSpec(x.shape, _zero_map(x.ndim)) for x in consts]


def _pair_lanes(q):
    return slice(q * PAIR, (q + 1) * PAIR)


def scan_fwd(r, lw, k, v, a, b, shards=()):
    lp = r.shape[0]
    nch = lp // SCAN_T
    npair = D_MODEL // PAIR
    ng = len(shards)
    consts = _scan_consts()
    row, state, cspecs = _scan_specs(consts, lambda c: c)

    def body(tri, strict, incl, masks, eye, r_ref, lw_ref, k_ref, v_ref, a_ref, b_ref, *rest):
        src, (y_ref, s_ref, minv_ref, u_ref), dst = rest[:ng], rest[ng:ng + 4], rest[ng + 4:2 * ng + 4]
        carry = rest[2 * ng + 4]
        first = jnp.logical_and(pl.program_id(0) == 0, pl.program_id(1) == 0)
        last = jnp.logical_and(pl.program_id(0) == npair // SCAN_PAIRS - 1, pl.program_id(1) == nch - 1)
        if ng:
            sends, arrivals, forwards, forwarded = gather_copies(src, dst, *rest[2 * ng + 5:])

            @pl.when(first)
            def _():
                for cp in sends:
                    cp.start()

            @pl.when(jnp.logical_and(pl.program_id(0) == npair // SCAN_PAIRS - 1, pl.program_id(1) == nch * 3 // 4))
            def _():
                for landed, onward in zip(arrivals, forwards):
                    landed.wait_recv()
                    onward.start()

        @pl.when(pl.program_id(1) == 0)
        def _():
            carry[...] = jnp.zeros_like(carry)

        pairs = range(SCAN_PAIRS)
        s0 = [carry[q] for q in pairs]
        rows = [[ref[:, _pair_lanes(q)] for q in pairs] for ref in (r_ref, lw_ref, k_ref, v_ref, a_ref, b_ref)]
        y, s1, (minv, u) = scan_chunk(tri[...], strict[...], incl[...], masks[0:1, :], masks[1:2, :], eye[...],
                                      *rows, s0)
        for q in pairs:
            s_ref[q] = s0[q]
            minv_ref[q] = minv[q]
            u_ref[q] = u[q]
            y_ref[:, _pair_lanes(q)] = y[q]
            carry[q] = s1[q]

        if ng:
            @pl.when(last)
            def _():
                for cp in forwarded:
                    cp.wait_recv()
                for cp in sends + forwards:
                    cp.wait_send()

    mats = jax.ShapeDtypeStruct((nch, npair, PAIR, PAIR), F32)
    out = pl.pallas_call(
        body, name="rwkv_scan_fwd", grid=(npair // SCAN_PAIRS, nch), in_specs=cspecs + [row] * 6 + [ANY] * ng,
        out_specs=[row, state, state, state] + [ANY] * ng,
        out_shape=[jax.ShapeDtypeStruct((lp, D_MODEL), F32), mats, mats, mats] + gathered_shapes(shards),
        scratch_shapes=[pltpu.VMEM((SCAN_PAIRS, PAIR, PAIR), F32)] + (gather_scratch(ng) if ng else []),
        compiler_params=_params(),
    )(*consts, r, lw, k, v, a, b, *shards)
    return out[:4], fill_own(out[4:], shards)


def scan_bwd(r, lw, k, v, a, b, saved, dy, direct_grads, parts=()):
    lp = r.shape[0]
    nch = lp // SCAN_T
    npair = D_MODEL // PAIR
    consts = _scan_consts()
    row, state, cspecs = _scan_specs(consts, lambda c: nch - 1 - c)

    ng = len(parts)

    def body(tri, strict, incl, masks, eye, r_ref, lw_ref, k_ref, v_ref, a_ref, b_ref, s_ref, minv_ref, u_ref,
             dy_ref, dr_in, dk_in, dv_in, *rest):
        src, (dr_ref, dlw_ref, dk_ref, dv_ref, da_ref, db_ref), dst = rest[:ng], rest[ng:ng + 6], rest[ng + 6:2 * ng + 6]
        carry = rest[2 * ng + 6]
        first = jnp.logical_and(pl.program_id(0) == 0, pl.program_id(1) == 0)
        last = jnp.logical_and(pl.program_id(0) == npair // SCAN_PAIRS - 1, pl.program_id(1) == nch - 1)
        if ng:
            sends, arrivals = chip_exchange_copies(src, dst, *rest[2 * ng + 7:])

            @pl.when(first)
            def _():
                for cp in sends:
                    cp.start()

        @pl.when(pl.program_id(1) == 0)
        def _():
            carry[...] = jnp.zeros_like(carry)

        pairs = range(SCAN_PAIRS)
        kept = ([minv_ref[q] for q in pairs], [u_ref[q] for q in pairs])

        def fn(*args):
            y, s1, _ = scan_chunk(tri[...], strict[...], incl[...], masks[0:1, :], masks[1:2, :], eye[...], *args,
                                  saved=kept)
            return y, s1

        rows = [[ref[:, _pair_lanes(q)] for q in pairs] for ref in (r_ref, lw_ref, k_ref, v_ref, a_ref, b_ref)]
        _, vjp = jax.vjp(fn, *rows, [s_ref[q] for q in pairs])
        grads = vjp(([dy_ref[:, _pair_lanes(q)] for q in pairs], [carry[q] for q in pairs]))
        direct = (dr_in, None, dk_in, dv_in, None, None)
        for q in pairs:
            ln = _pair_lanes(q)
            for ref, g, extra in zip((dr_ref, dlw_ref, dk_ref, dv_ref, da_ref, db_ref), grads[:6], direct):
                ref[:, ln] = g[q] if extra is None else g[q] + extra[:, ln]
            carry[q] = grads[6][q]

        if ng:
            @pl.when(last)
            def _():
                for cp in arrivals:
                    cp.wait_recv()
                for cp in sends:
                    cp.wait_send()

    out = pl.pallas_call(
        body, name="rwkv_scan_bwd", grid=(npair // SCAN_PAIRS, nch),
        in_specs=cspecs + [row] * 6 + [state] * 3 + [row] * 4 + [ANY] * ng, out_specs=[row] * 6 + [ANY] * ng,
        out_shape=[jax.ShapeDtypeStruct((lp, D_MODEL), F32)] * 6 + [jax.ShapeDtypeStruct(p.shape, p.dtype) for p in parts],
        scratch_shapes=[pltpu.VMEM((SCAN_PAIRS, PAIR, PAIR), F32)] + (_sem_scratch(ng * len(XY_FLIPS)) if ng else []),
        compiler_params=_params(),
    )(*consts, r, lw, k, v, a, b, *saved, dy, *direct_grads, *parts)
    return out[:6], out[6:]


def _spread_matrices():
    rep = np.zeros((N_HEADS_KV, KV_DIM, KVW), np.float32)
    for h in range(N_HEADS_KV):
        for g in range(GROUP):
            rep[h, h * HEAD_DIM + np.arange(HEAD_DIM), g * HEAD_DIM + np.arange(HEAD_DIM)] = 1.0
    return jnp.asarray(rep, BF16)


KV_HEADS = range(N_HEADS_KV)


def _attn_operands(q_ref, kp, kc, vp, vc, rep_ref):
    lane = lax.broadcasted_iota(jnp.int32, (1, KVW), 1)
    gmask = [(lane // HEAD_DIM == g).astype(F32) for g in range(GROUP)]
    kk = jnp.concatenate([kp, kc], axis=0)
    vv = jnp.concatenate([vp, vc], axis=0)
    qs = [q_ref[:, h * KVW:(h + 1) * KVW] for h in KV_HEADS]
    q_s = [jnp.concatenate([q * gmask[g] for g in range(GROUP)], axis=0) for q in qs]
    keys = [_dot(kk, rep_ref[h], 1, 0) for h in KV_HEADS]
    vals = [_dot(vv, rep_ref[h], 1, 0) for h in KV_HEADS]
    return gmask, q_s, keys, vals


def _attn_probs(n, q_s, keys, sink_ref):
    qi = lax.broadcasted_iota(jnp.int32, (GROUP * BLOCK, 2 * BLOCK), 0) % BLOCK
    kj = lax.broadcasted_iota(jnp.int32, (GROUP * BLOCK, 2 * BLOCK), 1)
    rel = BLOCK + qi - kj
    valid = (rel >= 0) & (rel < BLOCK) & ((n - 1) * BLOCK + kj >= PAD_FRONT)
    s = [jnp.where(valid, _dot(x, y, 1, 1) * (HEAD_DIM ** -0.5), -1e30) for x, y in zip(q_s, keys)]
    sink_col = [jnp.concatenate([jnp.broadcast_to(sink_ref[h, g:g + 1, 0:1], (BLOCK, 1)) for g in range(GROUP)],
                                axis=0) for h in KV_HEADS]
    m = [jnp.maximum(jnp.max(x, axis=-1, keepdims=True), c) for x, c in zip(s, sink_col)]
    ex = [jnp.exp(x - y) for x, y in zip(s, m)]
    ex_sink = [jnp.exp(c - y) for c, y in zip(sink_col, m)]
    inv = [1.0 / (jnp.sum(x, axis=-1, keepdims=True) + c) for x, c in zip(ex, ex_sink)]
    return [x * y for x, y in zip(ex, inv)], [x * y for x, y in zip(ex_sink, inv)]


def _unstack_groups(x_s, gmask):
    out = None
    for g in range(GROUP):
        t = x_s[g * BLOCK:(g + 1) * BLOCK] * gmask[g]
        out = t if out is None else out + t
    return out


def _attn_specs():
    qspec = pl.BlockSpec((BLOCK, D_MODEL), lambda n: (n, 0))
    cur = pl.BlockSpec((BLOCK, KV_DIM), lambda n: (n, 0))
    prev = pl.BlockSpec((BLOCK, KV_DIM), lambda n: (jnp.maximum(n - 1, 0), 0))
    rep = pl.BlockSpec((N_HEADS_KV, KV_DIM, KVW), lambda n: (0, 0, 0))
    sink = pl.BlockSpec((N_HEADS_KV, 8, PAIR), lambda n: (0, 0, 0))
    return qspec, cur, prev, rep, sink


def _attn_params():
    return pltpu.CompilerParams(dimension_semantics=("arbitrary",), vmem_limit_bytes=VMEM_LIMIT)


def _prob_specs():
    rows = GROUP * BLOCK
    return (pl.BlockSpec((None, N_HEADS_KV, rows, 2 * BLOCK), lambda n: (n, 0, 0, 0)),
            pl.BlockSpec((None, rows, PAIR), lambda n: (n, 0, 0)))


SINK_LANES = PAIR // N_HEADS_KV


def attn_fwd(q, k, v, sinks_b):
    lp = q.shape[0]
    nb = lp // BLOCK
    qspec, cur, prev, rep, sink = _attn_specs()

    def body(q_ref, kp_ref, kc_ref, vp_ref, vc_ref, rep_ref, sink_ref, o_ref, p_ref, ps_ref):
        gmask, q_s, keys, vals = _attn_operands(q_ref, kp_ref[...], kc_ref[...], vp_ref[...], vc_ref[...], rep_ref)
        p, p_sink = _attn_probs(pl.program_id(0), q_s, keys, sink_ref)
        o = [_dot(x, y, 1, 0) for x, y in zip(p, vals)]
        for h in KV_HEADS:
            o_ref[:, h * KVW:(h + 1) * KVW] = _unstack_groups(o[h], gmask)
            p_ref[h] = p[h].astype(BF16)
        head = lax.broadcasted_iota(jnp.int32, (GROUP * BLOCK, PAIR), 1) // SINK_LANES
        packed = p_sink[N_HEADS_KV - 1]
        for h in reversed(range(N_HEADS_KV - 1)):
            packed = jnp.where(head == h, p_sink[h], packed)
        ps_ref[...] = packed

    return pl.pallas_call(
        body, name="swa_fwd", grid=(nb,), in_specs=[qspec, prev, cur, prev, cur, rep, sink],
        out_specs=[qspec, *_prob_specs()],
        out_shape=[jax.ShapeDtypeStruct((lp, D_MODEL), F32),
                   jax.ShapeDtypeStruct((nb, N_HEADS_KV, GROUP * BLOCK, 2 * BLOCK), BF16),
                   jax.ShapeDtypeStruct((nb, GROUP * BLOCK, PAIR), F32)],
        compiler_params=_attn_params(),
    )(q, k, k, v, v, _spread_matrices(), sinks_b)


def attn_bwd(q, k, v, probs, do):
    lp = q.shape[0]
    qspec, cur, prev, rep, sink = _attn_specs()

    def body(q_ref, kp_ref, kc_ref, vp_ref, vc_ref, rep_ref, p_ref, ps_ref, do_ref, dq_ref, dkc_ref, dkp_ref, dvc_ref,
             dvp_ref, dsink_ref):
        n = pl.program_id(0)
        gmask, q_s, keys, vals = _attn_operands(q_ref, kp_ref[...], kc_ref[...], vp_ref[...], vc_ref[...], rep_ref)
        p = [p_ref[h].astype(F32) for h in KV_HEADS]
        lane = lax.broadcasted_iota(jnp.int32, (GROUP * BLOCK, PAIR), 1)
        p_sink = [jnp.sum(jnp.where(lane == h * SINK_LANES, ps_ref[...], 0.0), axis=-1, keepdims=True)
                  for h in KV_HEADS]
        do_s = [jnp.concatenate([do_ref[:, h * KVW:(h + 1) * KVW] * gmask[g] for g in range(GROUP)], axis=0)
                for h in KV_HEADS]
        dp = [_dot(x, y, 1, 1) for x, y in zip(do_s, vals)]
        delta = [jnp.sum(x * y, axis=-1, keepdims=True) for x, y in zip(p, dp)]
        ds = [x * (y - z) * (HEAD_DIM ** -0.5) for x, y, z in zip(p, dp, delta)]
        dq = [_dot(x, y, 1, 0) for x, y in zip(ds, keys)]
        dkeys_s = [_dot(x, y, 0, 0) for x, y in zip(ds, q_s)]
        dvals_s = [_dot(x, y, 0, 0) for x, y in zip(p, do_s)]
        dkeys = [_exact_dot(x, rep_ref[h], cb=1) for h, x in enumerate(dkeys_s)]
        dvals = [_exact_dot(x, rep_ref[h], cb=1) for h, x in enumerate(dvals_s)]
        dk_all = (dkeys[0] + dkeys[1]) + (dkeys[2] + dkeys[3])
        dv_all = (dvals[0] + dvals[1]) + (dvals[2] + dvals[3])
        dkp_ref[...] = dk_all[:BLOCK]
        dkc_ref[...] = dk_all[BLOCK:]
        dvp_ref[...] = dv_all[:BLOCK]
        dvc_ref[...] = dv_all[BLOCK:]
        dsinks = []
        for h in KV_HEADS:
            dq_ref[:, h * KVW:(h + 1) * KVW] = _unstack_groups(dq[h], gmask)
            dsk = -(p_sink[h] * delta[h])
            rows = [jnp.broadcast_to(jnp.sum(dsk[g * BLOCK:(g + 1) * BLOCK], axis=0, keepdims=True), (1, PAIR))
                    for g in range(GROUP)]
            dsinks.append(jnp.concatenate(rows + [jnp.zeros((8 - GROUP, PAIR), F32)], axis=0))

        @pl.when(n == 0)
        def _():
            for h in KV_HEADS:
                dsink_ref[h] = dsinks[h]

        @pl.when(n > 0)
        def _():
            for h in KV_HEADS:
                dsink_ref[h] += dsinks[h]

    kv = jax.ShapeDtypeStruct((lp, KV_DIM), F32)
    return pl.pallas_call(
        body, name="swa_bwd", grid=(lp // BLOCK,), in_specs=[qspec, prev, cur, prev, cur, rep, *_prob_specs(), qspec],
        out_specs=[qspec, cur, cur, cur, cur, sink],
        out_shape=[jax.ShapeDtypeStruct((lp, D_MODEL), F32), kv, kv, kv, kv,
                   jax.ShapeDtypeStruct((N_HEADS_KV, 8, PAIR), F32)],
        compiler_params=_attn_params(),
    )(q, k, k, v, v, _spread_matrices(), *probs, do)


def _pick_tm(lp, want):
    for tm in (384, 192, 128, 64):
        if tm <= want and lp % tm == 0:
            return tm
    raise ValueError(lp)


def _acc(shape):
    return (tuple(shape), F32)


def _ff_one(w):
    return (w, (None, D_MODEL, D_MODEL), lambda c, i: (c, 0, 0))


def _mlp_layer_fwd(name, h, wup, wdown, lg, lb, tm):
    def fn(c, i, h, wup, wdown, lg, lb):
        out, pre = None, []
        for s in range(N_FF_CHUNK):
            u = mm(h, wup[s])
            pre.append(u.astype(BF16))
            t = mm(sq_relu(u), wdown[s])
            out = t if out is None else out + t
        z = ALPHA * h + out
        return (_layer_norm(z, lg, lb), z, jnp.stack(pre)), ()

    (h_out, z, pre), _ = rowwise(name, fn, [h], [wup, wdown, lg, lb],
                                 [(D_MODEL, F32), (D_MODEL, F32), (D_MODEL, BF16, False, N_FF_CHUNK)], [], tm)
    return h_out, z, pre


MLP_BWD_TILE = 528


def _mlp_layer_bwd(name, h_in, z, pre, dh_parts, wup, wdown, lg, lb, tm):
    n_parts = len(dh_parts)

    def fn_ln(c, i, z, *rest):
        dh = rest[0]
        for extra in rest[1:n_parts]:
            dh = dh + extra
        _, vjp = jax.vjp(_layer_norm, z, rest[n_parts], rest[n_parts + 1])
        dz, dlg, dlb = vjp(dh)
        return (dz,), (dlg, dlb)

    (dz,), (dlg, dlb) = rowwise(name + "_ln", fn_ln, [z] + list(dh_parts), [lg, lb], [(D_MODEL, F32)],
                                [_acc((1, D_MODEL)), _acc((1, D_MODEL))], tm)

    def fn_mlp(c, i, h, dz, wup, wdown, u):
        r = jnp.maximum(u.astype(F32), 0.0)
        du = _dot(dz, wdown, 1, 1) * (2.0 * r)
        return (_dot(du, wup, 1, 1),), (_dot(h, du, 0, 0), _dot(r * r, dz, 0, 0))

    aspec = ((N_FF_CHUNK, D_MODEL, D_MODEL), F32, (None, D_MODEL, D_MODEL), lambda c, i: (c, 0, 0))
    lp = h_in.shape[0]
    tile = MLP_BWD_TILE if lp % MLP_BWD_TILE == 0 else tm
    pre_chunk = (pre, (None, tile, D_MODEL), lambda c, i: (c, i, 0))
    (dx,), (dwup, dwdown) = rowwise(name + "_mm", fn_mlp, [h_in, dz], [_ff_one(wup), _ff_one(wdown), pre_chunk],
                                    [(D_MODEL, F32, True)], [aspec, aspec], tile, nc=N_FF_CHUNK)
    return dz, dx, dwup, dwdown, dlg, dlb


def _sum_parts(dz, dx):
    out = ALPHA * dz
    for s in range(N_FF_CHUNK):
        out = out + dx[s]
    return out


def local_step(x, loss_target, p, late=None, early_hook=None):
    seq = x.shape[0]
    lp = TOK0 + seq
    tm = _pick_tm(lp, 384)
    tms = _pick_tm(lp, 192)
    e, et = _head_matrices()
    h0 = jnp.concatenate([jnp.zeros((PAD_FRONT, D_MODEL), F32), p["meta_tokens"], x], axis=0)
    pos = jnp.maximum(jnp.arange(lp, dtype=F32) - PAD_FRONT, 0.0)
    inv_freq = 1.0 / (ROPE_THETA ** (jnp.arange(0, HEAD_DIM, 2, dtype=F32) / HEAD_DIM))
    ang = pos[:, None] * inv_freq[None, :]
    cos = jnp.tile(jnp.cos(ang), (1, PAIR // (HEAD_DIM // 2)))
    sin = jnp.tile(jnp.sin(ang), (1, PAIR // (HEAD_DIM // 2)))

    pre_vec = [p["a_mu"][j:j + 1] for j in range(6)] + [p["a_w0"], p["a_a0"], p["a_k_k"], p["a_k_a"]]
    pre_w = [p["a_w_r"], p["a_w_k"], p["a_w_v"], p["a_w1"], p["a_w2"], p["a_a1"], p["a_a2"], p["a_g1"], p["a_g2"]]
    n_vec = len(pre_vec)

    def fn_pre(c, i, h, before, e, et, *ws):
        return rwkv_pre(e, et, ws[n_vec:], None, h, _shift_down(h, before, i), *ws[:n_vec])[0], ()

    (r, lw, k2, v, an, bn, g), _, *pre_gathered = rowwise(
        "rwkv_pre", fn_pre, [h0, _halo_before(h0, tm)], [e, et] + pre_vec + pre_w, [(D_MODEL, F32)] * 7, [], tm,
        hosted=hosted_gather(late[0][0]) if late else None)
    (y, *scan_saved), scan_gathered = scan_fwd(r, lw, k2, v, an, bn, late[1][0] if late else ())
    if late:
        p = {**p, **late[0][1](pre_gathered[0]), **late[1][1](scan_gathered)}

    post_c = [p["a_w_o"], p["a_gn_w"], p["a_gn_b"], p["a_r_k"], p["ln_g00"], p["ln_b00"]]

    def fn_post(c, i, y, r, k2, v, g, h0, e, et, w_o, *vecs):
        return (rwkv_post(e, et, w_o, None, y, r, k2, v, g, h0, *vecs)[0],), ()

    (h1,), _ = rowwise("rwkv_post", fn_post, [y, r, k2, v, g, h0], [e, et] + post_c, [(D_MODEL, F32)], [], tm)
    h2, z2, pre2 = _mlp_layer_fwd("mlp0_fwd", h1, p["mlp_up0"], p["mlp_down0"], p["ln_g01"], p["ln_b01"], tm)

    qkv_w = [p["b_w_q"], p["kv_w_k"], p["kv_w_v"]]

    def fn_qkv(c, i, h, cos, sin, wq, wk, wv):
        return qkv_proj(cos, sin, wq, wk, wv, None, h)[0], ()

    (q, k, vv), _ = rowwise("qkv_proj", fn_qkv, [h2, cos, sin], qkv_w,
                            [(D_MODEL, F32), (KV_DIM, F32), (KV_DIM, F32)], [], tm)
    sinks_b = jnp.broadcast_to(p["b_sinks"].reshape(N_HEADS_KV, GROUP, 1), (N_HEADS_KV, GROUP, PAIR))
    sinks_b = jnp.concatenate([sinks_b, jnp.zeros((N_HEADS_KV, 8 - GROUP, PAIR), F32)], axis=1)
    o, *attn_probs = attn_fwd(q, k, vv, sinks_b)

    ao_c = [p["b_w_o"], p["ln_g10"], p["ln_b10"]]

    def fn_ao(c, i, o, h, w_o, lg, lb):
        return (attn_out(w_o, None, o, h, lg, lb)[0],), ()

    (h3,), _ = rowwise("attn_out", fn_ao, [o, h2], ao_c, [(D_MODEL, F32)], [], tm)
    h4, z4, pre4 = _mlp_layer_fwd("mlp1_fwd", h3, p["mlp_up1"], p["mlp_down1"], p["ln_g11"], p["ln_b11"], tm)

    per = tm // TOK0

    def fn_loss(c, i, h4, *tgt_blocks):
        real = (_row_ids(i, tm) >= TOK0).astype(F32)
        err = (h4 - jnp.concatenate(tgt_blocks, axis=0)) * real
        part = 0.5 * jnp.sum(jnp.sum(err * err, axis=-1, keepdims=True), axis=0, keepdims=True) / D_MODEL
        return (err * (1.0 / D_MODEL),), (jnp.broadcast_to(part, (8, PAIR)),)

    tgt_blocks = [(loss_target, TOK0, functools.partial(lambda j, i: jnp.maximum(i * per + j - 1, 0), j))
                  for j in range(per)]
    (dh4,), (loss_acc,) = rowwise("loss", fn_loss, [h4] + tgt_blocks, [], [(D_MODEL, F32)], [_acc((8, PAIR))], tm)
    loss = loss_acc[0, 0]

    grads = {}
    dz4, dx4, grads["mlp_up1"], grads["mlp_down1"], grads["ln_g11"], grads["ln_b11"] = _mlp_layer_bwd(
        "mlp1_bwd", h3, z4, pre4, [dh4], p["mlp_up1"], p["mlp_down1"], p["ln_g11"], p["ln_b11"], tm)

    def fn_ao_b(c, i, dz, dx, o, h, w_o, lg, lb):
        (do, dh, dlg, dlb), (dw_o,) = vjp_taps(functools.partial(attn_out, w_o), [(tm, D_MODEL)], [o, h, lg, lb],
                                               _sum_parts(dz, dx))
        return (do, dh), (dw_o, dlg, dlb)

    (do, dh2_a), (grads["b_w_o"], grads["ln_g10"], grads["ln_b10"]) = rowwise(
        "attn_out_bwd", fn_ao_b, [dz4, dx4, o, h2], ao_c, [(D_MODEL, F32)] * 2,
        [_acc((D_MODEL, D_MODEL)), _acc((1, D_MODEL)), _acc((1, D_MODEL))], tm)

    dq, dkc, dkp, dvc, dvp, dsinks = attn_bwd(q, k, vv, attn_probs, do)
    grads["b_sinks"] = dsinks[:, :GROUP, 0].reshape(1, N_HEADS)
    zblk = jnp.zeros((BLOCK, KV_DIM), F32)
    dkp_s = jnp.concatenate([dkp[BLOCK:], zblk], axis=0)
    dvp_s = jnp.concatenate([dvp[BLOCK:], zblk], axis=0)

    def fn_qkv_b(c, i, h, cos, sin, dq, dkc, dkp, dvc, dvp, wq, wk, wv):
        return vjp_taps(functools.partial(qkv_proj, cos, sin, wq, wk, wv),
                        [(tm, D_MODEL), (tm, KV_DIM), (tm, KV_DIM)], [h], (dq, dkc + dkp, dvc + dvp))

    (dh2_q,), (grads["b_w_q"], grads["kv_w_k"], grads["kv_w_v"]) = rowwise(
        "qkv_proj_bwd", fn_qkv_b, [h2, cos, sin, dq, dkc, dkp_s, dvc, dvp_s], qkv_w, [(D_MODEL, F32)],
        [_acc((D_MODEL, D_MODEL)), _acc((D_MODEL, KV_DIM)), _acc((D_MODEL, KV_DIM))], tm)

    dz2, dx2, grads["mlp_up0"], grads["mlp_down0"], grads["ln_g01"], grads["ln_b01"] = _mlp_layer_bwd(
        "mlp0_bwd", h1, z2, pre2, [dh2_a, dh2_q], p["mlp_up0"], p["mlp_down0"], p["ln_g01"], p["ln_b01"], tm)

    def fn_post_b(c, i, dz, dx, y, r, k2, v, g, h0, e, et, w_o, *vecs):
        out, dws = vjp_taps(functools.partial(rwkv_post, e, et, w_o), [(tms, D_MODEL)],
                            [y, r, k2, v, g, h0] + list(vecs), _sum_parts(dz, dx))
        return out[:6], tuple(dws) + tuple(out[6:])

    early_srcs = early_hook[0](grads) if early_hook else ()
    (dy, dr_c, dk_c, dv_c, dg, dh0_c), post_g, *early_got = rowwise(
        "rwkv_post_bwd", fn_post_b, [dz2, dx2, y, r, k2, v, g, h0], [e, et] + post_c, [(D_MODEL, F32)] * 6,
        [_acc((D_MODEL, D_MODEL))] + [_acc((1, D_MODEL))] * 5, tms,
        hosted=hosted_pair_exchange(early_srcs) if early_hook else None)
    for name, val in zip(["a_w_o", "a_gn_w", "a_gn_b", "a_r_k", "ln_g00", "ln_b00"], post_g):
        grads[name] = val

    (dr, dlw, dk2, dv, dan, dbn), early_from_chips = scan_bwd(
        r, lw, k2, v, an, bn, scan_saved, dy, (dr_c, dk_c, dv_c),
        early_hook[1](early_srcs, early_got[0]) if early_hook else ())

    def fn_pre_b(c, i, h, before, dr, dlw, dk2, dv, dan, dbn, dg, e, et, *ws):
        hp = _shift_down(h, before, i)
        real = (_row_ids(i, tms) >= PAD_FRONT).astype(F32)
        cot = tuple(t * real for t in (dr, dlw, dk2, dv, dan, dbn, dg))
        out, dws = vjp_taps(functools.partial(rwkv_pre, e, et, ws[n_vec:]), [(tms, n) for n in PRE_TAPS],
                            [h, hp] + list(ws[:n_vec]), cot)
        return out[:2], tuple(out[2:]) + tuple(dws)

    (dh0_p, dhp), pre_g = rowwise(
        "rwkv_pre_bwd", fn_pre_b, [h0, _halo_before(h0, tms), dr, dlw, dk2, dv, dan, dbn, dg],
        [e, et] + pre_vec + pre_w, [(D_MODEL, F32)] * 2,
        [_acc((1, D_MODEL))] * n_vec + [_acc(w.shape) for w in pre_w], tms)
    grads["a_mu"] = jnp.concatenate(pre_g[:6], axis=0)
    for name, val in zip(["a_w0", "a_a0", "a_k_k", "a_k_a", "a_w_r", "a_w_k", "a_w_v", "a_w1", "a_w2", "a_a1",
                          "a_a2", "a_g1", "a_g2"], pre_g[6:]):
        grads[name] = val

    def fn_add(c, i, a, b, d, after):
        return (a + b + _shift_up(d, after, i, lp // tm),), ()

    (dh0,), _ = rowwise("grad_h0", fn_add, [dh0_c, dh0_p, dhp, _halo_after(dhp, tm)], [], [(D_MODEL, F32)], [], tm)
    grads["meta_tokens"] = dh0[PAD_FRONT:TOK0]
    return loss, dh0[TOK0:], grads, early_from_chips


ANY = pl.BlockSpec(memory_space=pl.ANY)
XY_FLIPS = ((0, 1), (1, 0), (1, 1))


def _flip(v, bit):
    return 1 - v if bit else v


def _sem_scratch(n):
    return [pltpu.SemaphoreType.DMA((n,)), pltpu.SemaphoreType.DMA((n,))]


def gather_copies(src, dst, ici_send, ici_recv, d2d_send, d2d_recv):
    npeer = len(XY_FLIPS)
    x, y, c = lax.axis_index("x"), lax.axis_index("y"), lax.axis_index("c")

    def half(ref, k, which):
        h = src[k].shape[0] // 2
        start = which * h
        return ref.at[pl.ds(pl.multiple_of(start, 8) if h % 8 == 0 else start, h)]

    def ici(k, j, slot):
        fx, fy = XY_FLIPS[j]
        return pltpu.make_async_remote_copy(
            src_ref=half(src[k], k, c), dst_ref=half(dst[k].at[slot], k, c), send_sem=ici_send.at[k * npeer + j],
            recv_sem=ici_recv.at[k * npeer + j], device_id=(_flip(x, fx), _flip(y, fy), c), device_id_type=MESH)

    def d2d(k, j, which):
        fx, fy = XY_FLIPS[j]
        landed = half(dst[k].at[2 * _flip(x, fx) + _flip(y, fy)], k, which)
        return pltpu.make_async_remote_copy(
            src_ref=landed, dst_ref=landed, send_sem=d2d_send.at[k * npeer + j], recv_sem=d2d_recv.at[k * npeer + j],
            device_id=(x, y, 1 - c), device_id_type=MESH)

    pairs = [(k, j) for k in range(len(src)) for j in range(npeer)]
    return ([ici(k, j, 2 * x + y) for k, j in pairs],
            [ici(k, j, 2 * _flip(x, XY_FLIPS[j][0]) + _flip(y, XY_FLIPS[j][1])) for k, j in pairs],
            [d2d(k, j, c) for k, j in pairs], [d2d(k, j, 1 - c) for k, j in pairs])


def gather_scratch(n):
    return _sem_scratch(n * len(XY_FLIPS)) * 2


def gathered_shapes(shards):
    return [jax.ShapeDtypeStruct((N_SHARD,) + s.shape, s.dtype) for s in shards]


def fill_own(gathered, shards):
    if not shards:
        return []
    slot = 2 * lax.axis_index("x") + lax.axis_index("y")
    return [lax.dynamic_update_index_in_dim(g, s, slot, 0) for g, s in zip(gathered, shards)]


def all_gather_shards(shards):
    n = len(shards)

    def body(*refs):
        sends, arrivals, forwards, forwarded = gather_copies(refs[:n], refs[n:2 * n], *refs[2 * n:])
        for cp in sends:
            cp.start()
        for landed, onward in zip(arrivals, forwards):
            landed.wait_recv()
            onward.start()
        for cp in forwarded:
            cp.wait_recv()
        for cp in sends + forwards:
            cp.wait_send()

    out = pl.pallas_call(body, name="gather_weights", in_specs=[ANY] * n, out_specs=[ANY] * n,
                         out_shape=gathered_shapes(shards), scratch_shapes=gather_scratch(n))(*shards)
    return fill_own(out, shards)


def placement():
    x, y, c = lax.axis_index("x"), lax.axis_index("y"), lax.axis_index("c")
    me = 2 * x + y
    others = [j + (j >= me).astype(jnp.int32) for j in range(N_SHARD - 1)]
    return jnp.stack([c, me] + others).astype(jnp.int32)


def hosted_gather(shards):
    return (gather_copies, list(shards), gathered_shapes(shards), gather_scratch(len(shards)),
            lambda got: fill_own(got, shards))


def pair_exchange_copies(src, got, send_sems, recv_sems):
    x, y, c = lax.axis_index("x"), lax.axis_index("y"), lax.axis_index("c")

    def copy(k):
        half = src[k].shape[1] // 2
        theirs = src[k].at[:, pl.ds(pl.multiple_of((1 - c) * half, 8), half), :]
        return pltpu.make_async_remote_copy(
            src_ref=theirs, dst_ref=got[k], send_sem=send_sems.at[k], recv_sem=recv_sems.at[k],
            device_id=(x, y, 1 - c), device_id_type=MESH)

    sends = [copy(k) for k in range(len(src))]
    return sends, sends, [], []


def _half_shapes(sources):
    return [jax.ShapeDtypeStruct((s.shape[0], s.shape[1] // 2, s.shape[2]), s.dtype) for s in sources]


def hosted_pair_exchange(sources):
    return (pair_exchange_copies, list(sources), _half_shapes(sources), _sem_scratch(len(sources)), list)


def pair_exchange(name, sources):
    n = len(sources)

    def body(*refs):
        sends, arrivals, _, _ = pair_exchange_copies(refs[:n], refs[n:2 * n], *refs[2 * n:])
        for cp in sends:
            cp.start()
        for cp in arrivals:
            cp.wait_recv()
        for cp in sends:
            cp.wait_send()

    halves = _half_shapes(sources)
    return pl.pallas_call(body, name=name, in_specs=[ANY] * n, out_specs=[ANY] * n,
                          out_shape=halves, scratch_shapes=_sem_scratch(n))(*sources)


def chip_exchange(parts):
    n = len(parts)

    def body(*refs):
        sends, arrivals = chip_exchange_copies(refs[:n], refs[n:2 * n], *refs[2 * n:])
        for cp in sends:
            cp.start()
        for cp in arrivals:
            cp.wait_recv()
        for cp in sends:
            cp.wait_send()

    return pl.pallas_call(
        body, name="grads_chip_exchange", in_specs=[ANY] * n, out_specs=[ANY] * n,
        out_shape=[jax.ShapeDtypeStruct(p.shape, p.dtype) for p in parts],
        scratch_shapes=_sem_scratch(n * len(XY_FLIPS)),
    )(*parts)


def chip_exchange_copies(src, dst, send_sems, recv_sems):
    npeer = len(XY_FLIPS)
    x, y, c = lax.axis_index("x"), lax.axis_index("y"), lax.axis_index("c")
    me = 2 * x + y

    def copy(k, j, sending):
        fx, fy = XY_FLIPS[j]
        px, py = _flip(x, fx), _flip(y, fy)
        peer = 2 * px + py
        return pltpu.make_async_remote_copy(
            src_ref=src[k].at[peer], dst_ref=dst[k].at[me if sending else peer],
            send_sem=send_sems.at[k * npeer + j], recv_sem=recv_sems.at[k * npeer + j],
            device_id=(px, py, c), device_id_type=MESH)

    pairs = [(k, j) for k in range(len(src)) for j in range(npeer)]
    return [copy(k, j, True) for k, j in pairs], [copy(k, j, False) for k, j in pairs]


def sibling_share(halves):
    n = len(halves)

    def body(*refs):
        src, got = refs[:n], refs[n:2 * n]
        send_sems, recv_sems = refs[2 * n:]
        x, y, c = lax.axis_index("x"), lax.axis_index("y"), lax.axis_index("c")
        sends = [pltpu.make_async_remote_copy(
            src_ref=src[k], dst_ref=got[k], send_sem=send_sems.at[k], recv_sem=recv_sems.at[k],
            device_id=(x, y, 1 - c), device_id_type=MESH) for k in range(n)]
        for cp in sends:
            cp.start()
        for cp in sends:
            cp.wait_recv()
        for cp in sends:
            cp.wait_send()

    return pl.pallas_call(
        body, name="grads_sibling_share", in_specs=[ANY] * n, out_specs=[ANY] * n,
        out_shape=[jax.ShapeDtypeStruct(h.shape, h.dtype) for h in halves], scratch_shapes=_sem_scratch(n),
    )(*halves)


ADD_TILE_ELEMS = 512 * 1024


def _row_tile(rows, cols):
    return max(t for t in range(8, rows + 1, 8) if rows % t == 0 and t * cols <= ADD_TILE_ELEMS)


def _prefetch_call(body, name, place, grid, in_specs, out_specs, out_shape, args):
    return pl.pallas_call(
        body, name=name, out_shape=out_shape,
        grid_spec=pltpu.PrefetchScalarGridSpec(num_scalar_prefetch=1, grid=grid, in_specs=in_specs,
                                               out_specs=out_specs),
        compiler_params=pltpu.CompilerParams(dimension_semantics=("arbitrary",) * len(grid),
                                             vmem_limit_bytes=VMEM_LIMIT),
    )(place, *args)


def pair_add(name, place, src, got, dtype):
    n4, half, cols = got.shape
    tile = _row_tile(half, cols)
    nt = half // tile

    def body(pr, a_ref, b_ref, o_ref):
        o_ref[...] = (a_ref[...] + b_ref[...]).astype(o_ref.dtype)

    mine = pl.BlockSpec((None, tile, cols), lambda s, i, pr: (s, pr[0] * nt + i, 0))
    blk = pl.BlockSpec((None, tile, cols), lambda s, i, pr: (s, i, 0))
    return _prefetch_call(body, name, place, (n4, nt), [mine, blk], blk,
                          jax.ShapeDtypeStruct(got.shape, dtype), (src, got))


def chip_add(name, place, part, from_chips):
    _, half, cols = part.shape
    tile = _row_tile(half, cols)

    def body(pr, own_ref, r0_ref, r1_ref, r2_ref, o_ref):
        me = pr[1]
        own, r0, r1, r2 = (r[...].astype(F32) for r in (own_ref, r0_ref, r1_ref, r2_ref))
        t0 = jnp.where(me == 0, own, r0)
        t1 = jnp.where(me == 0, r0, jnp.where(me == 1, own, r1))
        t2 = jnp.where(me <= 1, r1, jnp.where(me == 2, own, r2))
        t3 = jnp.where(me == 3, own, r2)
        o_ref[...] = ((t0 + t1) + t2) + t3

    def slab(j):
        return pl.BlockSpec((None, tile, cols), lambda i, pr: (pr[j], i, 0))

    return _prefetch_call(body, name, place, (half // tile,), [slab(1), slab(2), slab(3), slab(4)],
                          pl.BlockSpec((tile, cols), lambda i, pr: (i, 0)),
                          jax.ShapeDtypeStruct((half, cols), F32), (part, from_chips, from_chips, from_chips))


def pair_adds(tag, place, sources, got, narrow):
    return [pair_add(f"grads_pair_add_{tag}{k}", place, s, g, BF16 if nar else F32)
            for k, (s, g, nar) in enumerate(zip(sources, got, narrow))]


def finish_sums(place, parts, from_chips):
    halves = [chip_add(f"grads_chip_add{k}", place, p, f) for k, (p, f) in enumerate(zip(parts, from_chips))]
    return list(zip(halves, sibling_share(halves)))


ADAM_ROWS = 512


def adamw_update(name, place, halves, w, m, v):
    nsub, rows, cols = w.shape
    half = rows // 2
    tr = ADAM_ROWS if half % ADAM_ROWS == 0 else half
    nth = half // tr

    def body(pr, *refs):
        g_refs, (w_ref, m_ref, v_ref, g_ref, d_ref, nm_ref, nv_ref) = refs[:2 * nsub], refs[2 * nsub:]
        l = pl.program_id(0)
        mine = (pl.program_id(1) // nth) == pr[0]
        g = None
        for s in range(nsub):
            gs = jnp.where(mine, g_refs[2 * s][...], g_refs[2 * s + 1][...])
            g = gs if g is None else jnp.where(l == s, gs, g)
        m2 = ADAM_B1 * m_ref[...] + (1.0 - ADAM_B1) * g
        v2 = ADAM_B2 * v_ref[...] + (1.0 - ADAM_B2) * (g * g)
        m_hat = m2 / (1.0 - ADAM_B1 ** ADAM_STEP)
        v_hat = v2 / (1.0 - ADAM_B2 ** ADAM_STEP)
        g_ref[...] = g
        d_ref[...] = -ADAM_LR * (m_hat / (jnp.sqrt(v_hat) + ADAM_EPS) + ADAM_WD * w_ref[...])
        nm_ref[...] = m2
        nv_ref[...] = v2

    own = pl.BlockSpec((tr, cols), lambda l, i, pr: (jnp.where(i // nth == pr[0], i % nth, 0), 0))
    got = pl.BlockSpec((tr, cols), lambda l, i, pr: (jnp.where(i // nth == pr[0], 0, i % nth), 0))
    blk = pl.BlockSpec((None, tr, cols), lambda l, i, pr: (l, i, 0))
    out = jax.ShapeDtypeStruct((nsub, rows, cols), F32)
    return _prefetch_call(body, name, place, (nsub, rows // tr), [own, got] * nsub + [blk] * 3, [blk] * 4,
                          [out] * 4, [h for pair in halves for h in pair] + [w, m, v])


WEIGHT_NAMES = ("meta_tokens", "a_mu", "a_w_r", "a_w_k", "a_w_v", "a_w_o", "a_w0", "a_w1", "a_w2", "a_a0", "a_a1",
                "a_a2", "a_g1", "a_g2", "a_k_k", "a_k_a", "a_r_k", "a_gn_w", "a_gn_b", "kv_w_k", "kv_w_v", "b_w_q",
                "b_sinks", "b_w_o", "mlp_w_up", "mlp_w_down", "ln_g", "ln_b")
BIG_NAMES = ("a_w_r", "a_w_k", "a_w_v", "a_w_o", "b_w_q", "b_w_o")
EARLY_NAMES, LATE_NAMES = BIG_NAMES[:3], BIG_NAMES[3:]
PACK_MATS = (("kv_w_k", 256), ("kv_w_v", 256), ("a_w1", 64), ("a_a1", 64), ("a_g1", 128), ("a_w2", 64),
             ("a_a2", 64), ("a_g2", 128))
COLUMN_CUT = ("a_w2", "a_a2", "a_g2")
PACK_VECS = (("a_mu", 6), ("a_w0", 1), ("a_a0", 1), ("a_k_k", 1), ("a_k_a", 1), ("a_gn_w", 1), ("a_gn_b", 1),
             ("ln_g", 4), ("ln_b", 4), ("meta_tokens", 16))
PACK_REPL = (("a_r_k", 4), ("b_sinks", 1))
SHARD_W = D_MODEL // N_SHARD


def _tiles(rows):
    return -(-rows // SUBLANES) * SUBLANES


N_MAT_ROWS = sum(_tiles(r) for _, r in PACK_MATS)
N_VEC_ROWS = sum(_tiles(r) for _, r in PACK_VECS)
N_PACK_ROWS = -(-(N_MAT_ROWS + N_VEC_ROWS + sum(_tiles(r) for _, r in PACK_REPL)) // 16) * 16
N_GATHER_VEC_ROWS = -(-N_VEC_ROWS // 16) * 16


def _pad_rows(arr, axis):
    rows = arr.shape[axis]
    pad = [(0, 0)] * arr.ndim
    pad[axis] = (0, _tiles(rows) - rows)
    return jnp.pad(arr, pad) if _tiles(rows) != rows else arr


def _pack_rows(arr):
    if arr.size == N_HEADS:
        arr = jnp.pad(arr.reshape(1, N_HEADS), ((0, 0), (0, SHARD_W - N_HEADS)))
    return _pad_rows(arr.reshape(-1, SHARD_W), 0)


def pack_small(get):
    parts = [_pack_rows(get(name)) for name, _ in PACK_MATS + PACK_VECS + PACK_REPL]
    used = sum(p.shape[0] for p in parts)
    return jnp.concatenate(parts + [jnp.zeros((N_PACK_ROWS - used, SHARD_W), F32)], axis=0)


def unpack_small(pack, shapes):
    out, off = {}, 0
    for name, rows in PACK_MATS + PACK_VECS + PACK_REPL:
        piece = pack[off:off + rows]
        off += _tiles(rows)
        out[name] = piece[:, :N_HEADS].reshape(shapes[name]) if name == "b_sinks" else piece.reshape(shapes[name])
    return out


def whole_weights(big_names, gathered_big, mats, vecs, a_r_k, b_sinks):
    p = {name: g.reshape(D_MODEL, D_MODEL) for name, g in zip(big_names, gathered_big)}
    off = 0
    for name, rows in PACK_MATS:
        piece = mats[:, off:off + rows]
        off += rows
        if name in COLUMN_CUT:
            p[name] = piece.transpose(1, 0, 2).reshape(rows, D_MODEL)
        else:
            p[name] = piece.reshape(D_MODEL, rows)
    v = vecs.transpose(1, 0, 2).reshape(-1, D_MODEL)
    off = 0
    for name, rows in PACK_VECS:
        p[name] = v[off:off + rows]
        off += _tiles(rows)
    for i in range(2):
        for j in range(2):
            p[f"ln_g{i}{j}"] = p["ln_g"][2 * i + j:2 * i + j + 1]
            p[f"ln_b{i}{j}"] = p["ln_b"][2 * i + j:2 * i + j + 1]
    p["a_r_k"] = a_r_k.reshape(1, D_MODEL)
    p["b_sinks"] = b_sinks
    return p


def small_grad_pack(g):
    parts = []
    for name, rows in PACK_MATS:
        if name in COLUMN_CUT:
            parts.append(g[name].reshape(rows, N_SHARD, SHARD_W).transpose(1, 0, 2))
        else:
            parts.append(g[name].reshape(N_SHARD, rows, SHARD_W))
    vecs = {n: g[n] for n in ("a_mu", "a_w0", "a_a0", "a_k_k", "a_k_a", "a_gn_w", "a_gn_b", "meta_tokens")}
    vecs["ln_g"] = jnp.concatenate([g[f"ln_g{i}{j}"] for i in range(2) for j in range(2)], axis=0)
    vecs["ln_b"] = jnp.concatenate([g[f"ln_b{i}{j}"] for i in range(2) for j in range(2)], axis=0)
    for name, rows in PACK_VECS:
        parts.append(_pad_rows(vecs[name].reshape(rows, N_SHARD, SHARD_W).transpose(1, 0, 2), 1))
    r_k = jnp.broadcast_to(g["a_r_k"].reshape(1, -1, SHARD_W), (N_SHARD, D_MODEL // SHARD_W, SHARD_W))
    sinks = jnp.pad(g["b_sinks"].reshape(1, 1, N_HEADS), ((0, 0), (0, 0), (0, SHARD_W - N_HEADS)))
    parts += [_pad_rows(r_k, 1), _pad_rows(jnp.broadcast_to(sinks, (N_SHARD, 1, SHARD_W)), 1)]
    used = sum(p.shape[1] for p in parts)
    parts.append(jnp.zeros((N_SHARD, N_PACK_ROWS - used, SHARD_W), F32))
    return jnp.concatenate(parts, axis=1)


def train_step(vals):
    w = {n: vals[n] for n in WEIGHT_NAMES}
    w_pack = pack_small(lambda n: w[n])
    early = [w[n][0].astype(BF16) for n in EARLY_NAMES]
    early += [w_pack[:N_MAT_ROWS].astype(BF16), w_pack[N_MAT_ROWS:N_MAT_ROWS + N_GATHER_VEC_ROWS]]
    gathered = all_gather_shards(early)
    ne = len(EARLY_NAMES)
    p = whole_weights(EARLY_NAMES, gathered[:ne], gathered[ne], gathered[ne + 1][:, :N_VEC_ROWS], w["a_r_k"],
                      w["b_sinks"])
    nb = len(BIG_NAMES)

    def late_set(big, layer):
        shards = [w[n][0].astype(BF16) for n in big]
        shards += [w["mlp_w_up"][layer].astype(BF16), w["mlp_w_down"][layer].astype(BF16)]

        def weights(got):
            out = {n: x.reshape(D_MODEL, D_MODEL) for n, x in zip(big, got)}
            out[f"mlp_up{layer}"], out[f"mlp_down{layer}"] = got[len(big):]
            return out

        return shards, weights

    late = (late_set((), 1), late_set(LATE_NAMES, 0))

    place = placement()
    ready = {}
    a_names, b_names = BIG_NAMES[:4], BIG_NAMES[4:]

    def early_sources(g):
        return ([g[n].reshape(N_SHARD, SHARD_W, D_MODEL) for n in b_names]
                + [g["mlp_up0"], g["mlp_up1"], g["mlp_down0"], g["mlp_down1"]])

    def early_parts(srcs, got):
        ready["parts"] = pair_adds("early", place, srcs, got, [True] * len(srcs))
        return ready["parts"]

    loss, gx, g, early_from_chips = local_step(vals["x"][0], vals["loss_target"][0], p, late,
                                               (early_sources, early_parts))
    loss = lax.psum(loss, ("x", "y", "c"))
    srcs = [g[n].reshape(N_SHARD, SHARD_W, D_MODEL) for n in a_names] + [small_grad_pack(g)]
    rest = pair_adds("late", place, srcs, pair_exchange("grads_pair_exchange", srcs), [True] * len(a_names) + [False])
    rest_from_chips = chip_exchange(rest)
    na = len(a_names)
    halves = finish_sums(place, rest[:na] + ready["parts"] + rest[na:],
                         list(rest_from_chips[:na]) + list(early_from_chips) + list(rest_from_chips[na:]))

    res = {}
    for k, n in enumerate(BIG_NAMES):
        res[n] = adamw_update("adamw_" + n, place, halves[k:k + 1], w[n], vals["m_" + n], vals["v_" + n])
    for k, n in ((nb, "mlp_w_up"), (nb + 2, "mlp_w_down")):
        res[n] = adamw_update("adamw_" + n, place, halves[k:k + 2], w[n], vals["m_" + n], vals["v_" + n])
    packs = adamw_update("adamw_small", place, halves[-1:], w_pack[None], pack_small(lambda n: vals["m_" + n])[None],
                         pack_small(lambda n: vals["v_" + n])[None])
    shapes = {n: w[n].shape for n in WEIGHT_NAMES}
    small = [unpack_small(pk[0], shapes) for pk in packs]
    outs = [loss, gx[None]]
    for t in range(4):
        outs += [res[n][t] if n in res else small[t][n] for n in WEIGHT_NAMES]
    return tuple(outs)


def kernel(x, meta_tokens, a_mu, a_w_r, a_w_k, a_w_v, a_w_o, a_w0, a_w1, a_w2, a_a0, a_a1, a_a2, a_g1, a_g2, a_k_k,
           a_k_a, a_r_k, a_gn_w, a_gn_b, kv_w_k, kv_w_v, b_w_q, b_sinks, b_w_o, mlp_w_up, mlp_w_down, ln_g, ln_b,
           loss_target, m_meta_tokens, m_a_mu, m_a_w_r, m_a_w_k, m_a_w_v, m_a_w_o, m_a_w0, m_a_w1, m_a_w2, m_a_a0,
           m_a_a1, m_a_a2, m_a_g1, m_a_g2, m_a_k_k, m_a_k_a, m_a_r_k, m_a_gn_w, m_a_gn_b, m_kv_w_k, m_kv_w_v,
           m_b_w_q, m_b_sinks, m_b_w_o, m_mlp_w_up, m_mlp_w_down, m_ln_g, m_ln_b, v_meta_tokens, v_a_mu, v_a_w_r,
           v_a_w_k, v_a_w_v, v_a_w_o, v_a_w0, v_a_w1, v_a_w2, v_a_a0, v_a_a1, v_a_a2, v_a_g1, v_a_g2, v_a_k_k,
           v_a_k_a, v_a_r_k, v_a_gn_w, v_a_gn_b, v_kv_w_k, v_kv_w_v, v_b_w_q, v_b_sinks, v_b_w_o, v_mlp_w_up,
           v_mlp_w_down, v_ln_g, v_ln_b):
    return train_step(dict(locals()))
```

```python
import functools

import numpy as np
import jax
import jax.numpy as jnp
from jax import lax
from jax.experimental import pallas as pl
from jax.experimental.pallas import tpu as pltpu

F32 = jnp.float32
BF16 = jnp.bfloat16

D_MODEL = 1024
N_HEADS = 16
HEAD_DIM = 64
N_HEADS_KV = 4
GROUP = 4
KV_DIM = N_HEADS_KV * HEAD_DIM
N_META = 16
BLOCK = 128
PAD_FRONT = BLOCK - N_META
TOK0 = PAD_FRONT + N_META
N_FF_CHUNK = 4
N_SHARD = 4
GN_EPS = 64e-5
LN_EPS = 1e-5
ROPE_THETA = 10000.0
ALPHA = 4.0 ** 0.25
ADAM_LR, ADAM_B1, ADAM_B2, ADAM_EPS, ADAM_WD, ADAM_STEP = 0.001, 0.9, 0.999, 1e-08, 0.01, 10
SCAN_T = 64
PAIR = 128
KVW = GROUP * HEAD_DIM
VMEM_LIMIT = 60 * 1024 * 1024
MESH = pl.DeviceIdType.MESH


def _dot(a, b, ca, cb):
    return lax.dot_general(a.astype(BF16), b.astype(BF16), (((ca,), (cb,)), ((), ())),
                           preferred_element_type=F32)


@jax.custom_vjp
def mm(a, b):
    return _dot(a, b, 1, 0)


def _mm_fwd(a, b):
    return mm(a, b), b


def _mm_bwd(b, g):
    return _dot(g, b, 1, 1), jnp.zeros_like(b)


mm.defvjp(_mm_fwd, _mm_bwd)


@jax.custom_vjp
def mm_tap(a, b, tap):
    return _dot(a, b, 1, 0)


mm_tap.defvjp(lambda a, b, tap: (_dot(a, b, 1, 0), b), lambda b, g: (_dot(g, b, 1, 1), jnp.zeros_like(b), g))


def tmm(x, w, taps, xs):
    y = mm(x, w) if taps is None else mm_tap(x, w, taps[len(xs)])
    xs.append(x)
    return y


def vjp_taps(core, tap_shapes, args, cot):
    taps = [jnp.zeros(s, F32) for s in tap_shapes]
    _, vjp, xs = jax.vjp(core, taps, *args, has_aux=True)
    out = vjp(cot)
    return out[1:], [_dot(x, g, 0, 0) for x, g in zip(xs, out[0])]


def _split3(x):
    x1 = x.astype(BF16)
    r1 = x - x1.astype(F32)
    x2 = r1.astype(BF16)
    x3 = (r1 - x2.astype(F32)).astype(BF16)
    return x1, x2, x3


def _exact_dot(x, m01, cb=0):
    acc = None
    for piece in _split3(x)[:2]:
        t = lax.dot_general(piece, m01, (((1,), (cb,)), ((), ())), preferred_element_type=F32)
        acc = t if acc is None else acc + t
    return acc


def _head_matrices():
    e = np.zeros((D_MODEL, N_HEADS), np.float32)
    e[np.arange(D_MODEL), np.arange(D_MODEL) // HEAD_DIM] = 1.0
    return jnp.asarray(e, BF16), jnp.asarray(e.T, BF16)


@jax.custom_vjp
def hsum(x, e, et):
    return _exact_dot(x, e)


@jax.custom_vjp
def hbc(s, e, et):
    return _exact_dot(s, et)


hsum.defvjp(lambda x, e, et: (_exact_dot(x, e), (e, et)),
            lambda res, g: (hbc(g, *res), jnp.zeros_like(res[0]), jnp.zeros_like(res[1])))
hbc.defvjp(lambda s, e, et: (_exact_dot(s, et), (e, et)),
           lambda res, g: (hsum(g, *res), jnp.zeros_like(res[0]), jnp.zeros_like(res[1])))


def _sigmoid(u):
    return 0.5 * (jnp.tanh(0.5 * u) + 1.0)


def _softplus(u):
    return jnp.maximum(u, 0.0) + jnp.log(1.0 + jnp.exp(-jnp.abs(u)))


def _layer_norm(z, g, b):
    mu = jnp.mean(z, axis=-1, keepdims=True)
    zc = z - mu
    var = jnp.mean(zc * zc, axis=-1, keepdims=True)
    return zc * lax.rsqrt(var + LN_EPS) * g + b


def _zero_map(nd):
    return lambda c, i: (0,) * nd


def _params():
    return pltpu.CompilerParams(dimension_semantics=("arbitrary", "arbitrary"), vmem_limit_bytes=VMEM_LIMIT)


def rowwise(name, fn, rows, consts, out_rows, out_accs, tm, nc=1, hosted=None):
    lp = rows[0].shape[-2]
    nt = lp // tm
    assert nt * tm == lp, (name, lp, tm)
    copies_fn, hosted_src, hosted_shapes, hosted_scratch, hosted_post = hosted or (None, (), [], [], None)
    ng = len(hosted_src)
    in_specs, args = [], []
    for a in rows:
        if isinstance(a, tuple):
            a, block_rows, block_index = a
            in_specs.append(pl.BlockSpec((block_rows, a.shape[1]),
                                         functools.partial(lambda f, c, i: (f(i), 0), block_index)))
        elif a.ndim == 2:
            in_specs.append(pl.BlockSpec((tm, a.shape[1]), lambda c, i: (i, 0)))
        else:
            in_specs.append(pl.BlockSpec((a.shape[0], tm, a.shape[2]), lambda c, i: (0, i, 0)))
        args.append(a)
    for cst in consts:
        if isinstance(cst, tuple):
            arr, bs, im = cst
            in_specs.append(pl.BlockSpec(bs, im))
        else:
            arr = cst
            in_specs.append(pl.BlockSpec(arr.shape, _zero_map(arr.ndim), pipeline_mode=pl.Buffered(1)))
        args.append(arr)
    out_shape, out_specs, acc_per_chunk = [], [], []
    for spec in out_rows:
        if len(spec) == 4:
            out_shape.append(jax.ShapeDtypeStruct((spec[3], lp, spec[0]), spec[1]))
            out_specs.append(pl.BlockSpec((spec[3], tm, spec[0]), lambda c, i: (0, i, 0)))
        elif len(spec) == 3 and spec[2]:
            out_shape.append(jax.ShapeDtypeStruct((nc, lp, spec[0]), spec[1]))
            out_specs.append(pl.BlockSpec((None, tm, spec[0]), lambda c, i: (c, i, 0)))
        else:
            out_shape.append(jax.ShapeDtypeStruct((lp, spec[0]), spec[1]))
            out_specs.append(pl.BlockSpec((tm, spec[0]), lambda c, i: (i, 0)))
    for spec in out_accs:
        out_shape.append(jax.ShapeDtypeStruct(spec[0], spec[1]))
        if len(spec) == 4:
            out_specs.append(pl.BlockSpec(spec[2], spec[3]))
            acc_per_chunk.append(True)
        else:
            out_specs.append(pl.BlockSpec(spec[0], _zero_map(len(spec[0])), pipeline_mode=pl.Buffered(1)))
            acc_per_chunk.append(False)
    n_in, n_or, n_out = len(args), len(out_rows), len(out_shape)

    def body(*refs):
        c = pl.program_id(0)
        i = pl.program_id(1)
        if ng:
            src, dst = refs[n_in:n_in + ng], refs[n_in + ng + n_out:n_in + 2 * ng + n_out]
            sends, arrivals, forwards, forwarded = copies_fn(src, dst, *refs[n_in + 2 * ng + n_out:])

            @pl.when(jnp.logical_and(c == 0, i == 0))
            def _():
                for cp in sends:
                    cp.start()

        vals = [r[...] for r in refs[:n_in]]
        outs_r, outs_a = fn(c, i, *vals)
        out_refs = refs[n_in + ng:n_in + ng + n_out]
        for ref, val in zip(out_refs[:n_or], outs_r):
            ref[...] = val.astype(ref.dtype)
        for ref, val, per_chunk in zip(out_refs[n_or:], outs_a, acc_per_chunk):
            first = (i == 0) if per_chunk else jnp.logical_and(i == 0, c == 0)

            @pl.when(first)
            def _():
                ref[...] = val.astype(ref.dtype)

            @pl.when(jnp.logical_not(first))
            def _():
                ref[...] += val.astype(ref.dtype)

        if ng:
            @pl.when(jnp.logical_and(c == nc - 1, i == max(nt - 3, 0)))
            def _():
                for k, landed in enumerate(arrivals):
                    landed.wait_recv()
                    if forwards:
                        forwards[k].start()

            @pl.when(jnp.logical_and(c == nc - 1, i == nt - 1))
            def _():
                for cp in forwarded:
                    cp.wait_recv()
                for cp in sends + forwards:
                    cp.wait_send()

    outs = pl.pallas_call(body, name=name, grid=(nc, nt), in_specs=in_specs + [ANY] * ng,
                          out_specs=out_specs + [ANY] * ng, out_shape=out_shape + list(hosted_shapes),
                          scratch_shapes=list(hosted_scratch), compiler_params=_params())(*args, *hosted_src)
    if ng:
        return outs[:n_or], outs[n_or:n_out], hosted_post(outs[n_out:])
    return outs[:n_or], outs[n_or:]


def _row_ids(i, tm):
    return i * tm + lax.broadcasted_iota(jnp.int32, (tm, 1), 0)


SUBLANES = 8


def _halo_before(arr, tm):
    return (arr, SUBLANES, lambda i: jnp.maximum(i * (tm // SUBLANES) - 1, 0))


def _halo_after(arr, tm):
    last = arr.shape[0] // SUBLANES - 1
    return (arr, SUBLANES, lambda i: jnp.minimum((i + 1) * (tm // SUBLANES), last))


def _pick_row(block8, row):
    rows = lax.broadcasted_iota(jnp.int32, block8.shape, 0)
    return jnp.sum(jnp.where(rows == row, block8, 0.0), axis=0, keepdims=True)


def _shift_down(x, before8, i):
    rows = lax.broadcasted_iota(jnp.int32, x.shape, 0)
    top = _pick_row(before8, SUBLANES - 1) * (i > 0).astype(F32)
    return jnp.where(rows == 0, top, pltpu.roll(x, 1, 0))


def _shift_up(x, after8, i, nt):
    rows = lax.broadcasted_iota(jnp.int32, x.shape, 0)
    bottom = _pick_row(after8, 0) * (i < nt - 1).astype(F32)
    return jnp.where(rows == x.shape[0] - 1, bottom, pltpu.roll(x, x.shape[0] - 1, 0))


LORA_DECAY, LORA_AAA, LORA_GATE = 64, 64, 128
PRE_TAPS = (D_MODEL, D_MODEL, D_MODEL, LORA_DECAY, D_MODEL, LORA_AAA, D_MODEL, LORA_GATE, D_MODEL)


def rwkv_pre(e, et, ws, taps, h, hp, mu_r, mu_w, mu_k, mu_v, mu_a, mu_g, w0, a0, k_k, k_a):
    w_r, w_k, w_v, w1, w2, a1, a2, g1, g2 = ws
    xs = []
    xx = hp - h
    r = tmm(h + xx * mu_r, w_r, taps, xs)
    k = tmm(h + xx * mu_k, w_k, taps, xs)
    v = tmm(h + xx * mu_v, w_v, taps, xs)
    wraw = -_softplus(-(w0 + tmm(jnp.tanh(tmm(h + xx * mu_w, w1, taps, xs)), w2, taps, xs))) - 0.5
    lw = -jnp.exp(wraw)
    a = _sigmoid(a0 + tmm(tmm(h + xx * mu_a, a1, taps, xs), a2, taps, xs))
    g = tmm(_sigmoid(tmm(h + xx * mu_g, g1, taps, xs)), g2, taps, xs)
    kk = k * k_k
    ss = hsum(kk * kk, e, et)
    pos = ss > 0.0
    nrm = jnp.where(pos, jnp.sqrt(jnp.where(pos, ss, 1.0)), 0.0)
    kk = kk * hbc(1.0 / jnp.maximum(nrm, 1e-12), e, et)
    k2 = k * (1.0 + (a - 1.0) * k_a)
    return (r, lw, k2, v, -kk, kk * a, g), xs


def rwkv_post(e, et, w_o, taps, y, r, k2, v, g, h0, gn_w, gn_b, rk, lg, lb):
    xs = []
    inv_n = 1.0 / HEAD_DIM
    yc = y - hbc(hsum(y, e, et) * inv_n, e, et)
    yv = hsum(yc * yc, e, et) * inv_n
    yn = yc * hbc(lax.rsqrt(yv + GN_EPS), e, et) * gn_w + gn_b
    bonus = hbc(hsum(r * k2 * rk, e, et), e, et) * v
    mix = tmm((yn + bonus) * g, w_o, taps, xs)
    return _layer_norm(ALPHA * h0 + mix, lg, lb), xs


@jax.custom_vjp
def sq_relu(x):
    r = jnp.maximum(x, 0.0)
    return r * r


sq_relu.defvjp(lambda x: (sq_relu(x), x), lambda x, g: (g * (2.0 * jnp.maximum(x, 0.0)),))


def _rot_half(t):
    n = t.shape[-1]
    lane = lax.broadcasted_iota(jnp.int32, t.shape, t.ndim - 1)
    lo = (lane % HEAD_DIM) < (HEAD_DIM // 2)
    return jnp.where(lo, -pltpu.roll(t, n - HEAD_DIM // 2, t.ndim - 1), pltpu.roll(t, HEAD_DIM // 2, t.ndim - 1))


@jax.custom_vjp
def rot_half(t):
    return _rot_half(t)


rot_half.defvjp(lambda t: (_rot_half(t), None), lambda _, g: (-_rot_half(g),))


def _tile_lanes(t, width):
    return jnp.concatenate([t] * (width // t.shape[-1]), axis=-1)


def qkv_proj(cos, sin, wq, wk, wv, taps, h):
    xs = []
    q = tmm(h, wq, taps, xs)
    k = tmm(h, wk, taps, xs)
    v = tmm(h, wv, taps, xs)
    cq, sq = _tile_lanes(cos, D_MODEL), _tile_lanes(sin, D_MODEL)
    ck, sk = _tile_lanes(cos, KV_DIM), _tile_lanes(sin, KV_DIM)
    return (q * cq + rot_half(q) * sq, k * ck + rot_half(k) * sk, v), xs


def attn_out(w_o, taps, o, h, lg, lb):
    xs = []
    return _layer_norm(ALPHA * h + tmm(o, w_o, taps, xs), lg, lb), xs


def _scan_consts():
    t = SCAN_T
    tri = np.tril(np.ones((t, t), np.float32))
    rows = np.arange(2 * t)
    same = (rows[:, None] // t) == (rows[None, :] // t)
    strict = same & ((rows[None, :] % t) < (rows[:, None] % t))
    incl = same & ((rows[None, :] % t) <= (rows[:, None] % t))
    lane = np.arange(PAIR)
    masks = np.zeros((8, PAIR), np.float32)
    masks[0] = (lane // HEAD_DIM) == 0
    masks[1] = (lane // HEAD_DIM) == 1
    return (jnp.asarray(tri, BF16), jnp.asarray(strict.astype(np.float32)), jnp.asarray(incl.astype(np.float32)),
            jnp.asarray(masks), jnp.asarray(np.eye(2 * t, dtype=np.float32)))


def _scan_dot(a, b, ca, cb):
    return _dot(a, b, ca, cb)


@functools.partial(jax.custom_vjp, nondiff_argnums=(2, 3))
def _dotf(a, b, ca, cb):
    return _scan_dot(a, b, ca, cb)


def _dotf_bwd(ca, cb, res, g):
    a, b = res
    if ca == 1:
        da = _scan_dot(g, b, 1, 1 - cb)
    else:
        da = _scan_dot(b, g, 1 - cb, 1)
    if cb == 0:
        db = _scan_dot(a, g, 1 - ca, 0)
    else:
        db = _scan_dot(g, a, 0, 1 - ca)
    return da, db


_dotf.defvjp(lambda a, b, ca, cb: (_scan_dot(a, b, ca, cb), (a, b)), _dotf_bwd)


def _tri_dot(tri, x, ct):
    acc = None
    for piece in _split3(x):
        t = lax.dot_general(tri, piece, (((ct,), (0,)), ((), ())), preferred_element_type=F32)
        acc = t if acc is None else acc + t
    return acc


@jax.custom_vjp
def _cumsum_rows(tri, x):
    return _tri_dot(tri, x, 1)


_cumsum_rows.defvjp(lambda tri, x: (_tri_dot(tri, x, 1), tri),
                    lambda tri, g: (jnp.zeros_like(tri), _tri_dot(tri, g, 0)))


@jax.custom_vjp
def _unstack2(x):
    t = x.shape[0] // 2
    return x[:t] + x[t:]


_unstack2.defvjp(lambda x: (_unstack2(x), None), lambda _, g: (jnp.concatenate([g, g], axis=0),))


@jax.custom_vjp
def _last_row(x):
    return x[x.shape[0] - 1:, :]


def _last_row_bwd(_, g):
    rows = lax.broadcasted_iota(jnp.int32, (SCAN_T, g.shape[1]), 0)
    return (jnp.where(rows == SCAN_T - 1, jnp.broadcast_to(g, (SCAN_T, g.shape[1])), 0.0),)


_last_row.defvjp(lambda x: (_last_row(x), None), _last_row_bwd)


@jax.custom_vjp
def _halves(x):
    n = x.shape[0] // 2
    return x[:n], x[n:]


_halves.defvjp(lambda x: (_halves(x), None), lambda _, g: (jnp.concatenate(list(g), axis=0),))


@jax.custom_vjp
def _quads(x):
    n, m = x.shape[0] // 2, x.shape[1] // 2
    return x[:n, :m], x[:n, m:], x[n:, :m], x[n:, m:]


_quads.defvjp(lambda x: (_quads(x), None),
              lambda _, g: (jnp.concatenate([jnp.concatenate([g[0], g[1]], axis=1),
                                             jnp.concatenate([g[2], g[3]], axis=1)], axis=0),))


@jax.custom_vjp
def _solve_saved(n, rhs, minv, u):
    return u


def _solve_saved_bwd(res, du):
    minv, u = res
    drhs = _dotf(minv, du, 0, 0)
    return _dotf(drhs, u, 1, 1), drhs, jnp.zeros_like(minv), jnp.zeros_like(u)


_solve_saved.defvjp(lambda n, rhs, minv, u: (u, (minv, u)), _solve_saved_bwd)


def scan_chunk(tri, strict, incl, m0, m1, eye, r, lw, k, v, a, b, s0, saved=None):
    lower = strict > 0
    lower_incl = incl > 0

    def stack(x):
        return jnp.concatenate([x * m0, x * m1], axis=0)

    def dots(xs, ys, ca, cb, mask=None):
        out = [_dotf(x, y, ca, cb) for x, y in zip(xs, ys)]
        return out if mask is None else [jnp.where(mask, o, 0.0) for o in out]

    cl = [_cumsum_rows(tri, x) for x in lw]
    gam = [jnp.exp(c) for c in cl]
    ginv = [jnp.exp(-c) for c in cl]
    ar_s = [jnp.concatenate([stack(x * jnp.exp(c - w)), stack(y * g)], axis=0)
            for x, c, w, y, g in zip(a, cl, lw, r, gam)]
    bk_s = [jnp.concatenate([stack(x * g), stack(y * g)], axis=0) for x, y, g in zip(b, k, ginv)]
    v_s = [stack(x) for x in v]
    quads = [_quads(x) for x in dots(ar_s, bk_s, 1, 1)]
    n_ab = [jnp.where(lower, q[0], 0.0) for q in quads]
    n_ak = [jnp.where(lower, q[1], 0.0) for q in quads]
    r_ab = [jnp.where(lower_incl, q[2], 0.0) for q in quads]
    r_ak = [jnp.where(lower_incl, q[3], 0.0) for q in quads]
    from_state = [_halves(x) for x in dots(ar_s, s0, 1, 1)]
    rhs = [x[0] + y for x, y in zip(from_state, dots(n_ak, v_s, 1, 0))]
    if saved is None:
        minv = [eye + n for n in n_ab]
        p = n_ab
        for _ in range(5):
            p = dots(p, p, 1, 0)
            minv = [m + mp for m, mp in zip(minv, dots(minv, p, 1, 0))]
        u_s = dots(minv, rhs, 1, 0)
    else:
        minv = saved[0]
        u_s = [_solve_saved(n, x, m, u) for n, x, m, u in zip(n_ab, rhs, *saved)]
    uv_s = [jnp.concatenate([x, y], axis=0) for x, y in zip(u_s, v_s)]
    r_uv = [jnp.concatenate([x, y], axis=1) for x, y in zip(r_ab, r_ak)]
    y = [_unstack2(x[1] + z) for x, z in zip(from_state, dots(r_uv, uv_s, 1, 0))]
    g_end = [_last_row(g) for g in gam]
    s1 = [s * g + x for s, g, x in zip(s0, g_end, dots(uv_s, [x * g for x, g in zip(bk_s, g_end)], 0, 0))]
    return y, s1, (minv, u_s)


SCAN_PAIRS = 8


def _scan_specs(consts, order):
    row = pl.BlockSpec((SCAN_T, PAIR * SCAN_PAIRS), lambda p, c: (order(c), p))
    state = pl.BlockSpec((None, SCAN_PAIRS, PAIR, PAIR), lambda p, c: (order(c), p, 0, 0))
    return row, state, [pl.BlockSpec(x.shape, _zero_map(x.ndim)) for x in consts]


def _pair_lanes(q):
    return slice(q * PAIR, (q + 1) * PAIR)


def scan_fwd(r, lw, k, v, a, b, shards=()):
    lp = r.shape[0]
    nch = lp // SCAN_T
    npair = D_MODEL // PAIR
    ng = len(shards)
    consts = _scan_consts()
    row, state, cspecs = _scan_specs(consts, lambda c: c)

    def body(tri, strict, incl, masks, eye, r_ref, lw_ref, k_ref, v_ref, a_ref, b_ref, *rest):
        src, (y_ref, s_ref, minv_ref, u_ref), dst = rest[:ng], rest[ng:ng + 4], rest[ng + 4:2 * ng + 4]
        carry = rest[2 * ng + 4]
        first = jnp.logical_and(pl.program_id(0) == 0, pl.program_id(1) == 0)
        last = jnp.logical_and(pl.program_id(0) == npair // SCAN_PAIRS - 1, pl.program_id(1) == nch - 1)
        if ng:
            sends, arrivals, forwards, forwarded = gather_copies(src, dst, *rest[2 * ng + 5:])

            @pl.when(first)
            def _():
                for cp in sends:
                    cp.start()

            @pl.when(jnp.logical_and(pl.program_id(0) == npair // SCAN_PAIRS - 1, pl.program_id(1) == nch * 3 // 4))
            def _():
                for landed, onward in zip(arrivals, forwards):
                    landed.wait_recv()
                    onward.start()

        @pl.when(pl.program_id(1) == 0)
        def _():
            carry[...] = jnp.zeros_like(carry)

        pairs = range(SCAN_PAIRS)
        s0 = [carry[q] for q in pairs]
        rows = [[ref[:, _pair_lanes(q)] for q in pairs] for ref in (r_ref, lw_ref, k_ref, v_ref, a_ref, b_ref)]
        y, s1, (minv, u) = scan_chunk(tri[...], strict[...], incl[...], masks[0:1, :], masks[1:2, :], eye[...],
                                      *rows, s0)
        for q in pairs:
            s_ref[q] = s0[q]
            minv_ref[q] = minv[q]
            u_ref[q] = u[q]
            y_ref[:, _pair_lanes(q)] = y[q]
            carry[q] = s1[q]

        if ng:
            @pl.when(last)
            def _():
                for cp in forwarded:
                    cp.wait_recv()
                for cp in sends + forwards:
                    cp.wait_send()

    mats = jax.ShapeDtypeStruct((nch, npair, PAIR, PAIR), F32)
    out = pl.pallas_call(
        body, name="rwkv_scan_fwd", grid=(npair // SCAN_PAIRS, nch), in_specs=cspecs + [row] * 6 + [ANY] * ng,
        out_specs=[row, state, state, state] + [ANY] * ng,
        out_shape=[jax.ShapeDtypeStruct((lp, D_MODEL), F32), mats, mats, mats] + gathered_shapes(shards),
        scratch_shapes=[pltpu.VMEM((SCAN_PAIRS, PAIR, PAIR), F32)] + (gather_scratch(ng) if ng else []),
        compiler_params=_params(),
    )(*consts, r, lw, k, v, a, b, *shards)
    return out[:4], fill_own(out[4:], shards)


def scan_bwd(r, lw, k, v, a, b, saved, dy, direct_grads, parts=()):
    lp = r.shape[0]
    nch = lp // SCAN_T
    npair = D_MODEL // PAIR
    consts = _scan_consts()
    row, state, cspecs = _scan_specs(consts, lambda c: nch - 1 - c)

    ng = len(parts)

    def body(tri, strict, incl, masks, eye, r_ref, lw_ref, k_ref, v_ref, a_ref, b_ref, s_ref, minv_ref, u_ref,
             dy_ref, dr_in, dk_in, dv_in, *rest):
        src, (dr_ref, dlw_ref, dk_ref, dv_ref, da_ref, db_ref), dst = rest[:ng], rest[ng:ng + 6], rest[ng + 6:2 * ng + 6]
        carry = rest[2 * ng + 6]
        first = jnp.logical_and(pl.program_id(0) == 0, pl.program_id(1) == 0)
        last = jnp.logical_and(pl.program_id(0) == npair // SCAN_PAIRS - 1, pl.program_id(1) == nch - 1)
        if ng:
            sends, arrivals = chip_exchange_copies(src, dst, *rest[2 * ng + 7:])

            @pl.when(first)
            def _():
                for cp in sends:
                    cp.start()

        @pl.when(pl.program_id(1) == 0)
        def _():
            carry[...] = jnp.zeros_like(carry)

        pairs = range(SCAN_PAIRS)
        kept = ([minv_ref[q] for q in pairs], [u_ref[q] for q in pairs])

        def fn(*args):
            y, s1, _ = scan_chunk(tri[...], strict[...], incl[...], masks[0:1, :], masks[1:2, :], eye[...], *args,
                                  saved=kept)
            return y, s1

        rows = [[ref[:, _pair_lanes(q)] for q in pairs] for ref in (r_ref, lw_ref, k_ref, v_ref, a_ref, b_ref)]
        _, vjp = jax.vjp(fn, *rows, [s_ref[q] for q in pairs])
        grads = vjp(([dy_ref[:, _pair_lanes(q)] for q in pairs], [carry[q] for q in pairs]))
        direct = (dr_in, None, dk_in, dv_in, None, None)
        for q in pairs:
            ln = _pair_lanes(q)
            for ref, g, extra in zip((dr_ref, dlw_ref, dk_ref, dv_ref, da_ref, db_ref), grads[:6], direct):
                ref[:, ln] = g[q] if extra is None else g[q] + extra[:, ln]
            carry[q] = grads[6][q]

        if ng:
            @pl.when(last)
            def _():
                for cp in arrivals:
                    cp.wait_recv()
                for cp in sends:
                    cp.wait_send()

    out = pl.pallas_call(
        body, name="rwkv_scan_bwd", grid=(npair // SCAN_PAIRS, nch),
        in_specs=cspecs + [row] * 6 + [state] * 3 + [row] * 4 + [ANY] * ng, out_specs=[row] * 6 + [ANY] * ng,
        out_shape=[jax.ShapeDtypeStruct((lp, D_MODEL), F32)] * 6 + [jax.ShapeDtypeStruct(p.shape, p.dtype) for p in parts],
        scratch_shapes=[pltpu.VMEM((SCAN_PAIRS, PAIR, PAIR), F32)] + (_sem_scratch(ng * len(XY_FLIPS)) if ng else []),
        compiler_params=_params(),
    )(*consts, r, lw, k, v, a, b, *saved, dy, *direct_grads, *parts)
    return out[:6], out[6:]


def _spread_matrices():
    rep = np.zeros((N_HEADS_KV, KV_DIM, KVW), np.float32)
    for h in range(N_HEADS_KV):
        for g in range(GROUP):
            rep[h, h * HEAD_DIM + np.arange(HEAD_DIM), g * HEAD_DIM + np.arange(HEAD_DIM)] = 1.0
    return jnp.asarray(rep, BF16)


KV_HEADS = range(N_HEADS_KV)


def _attn_operands(q_ref, kp, kc, vp, vc, rep_ref):
    lane = lax.broadcasted_iota(jnp.int32, (1, KVW), 1)
    gmask = [(lane // HEAD_DIM == g).astype(F32) for g in range(GROUP)]
    kk = jnp.concatenate([kp, kc], axis=0)
    vv = jnp.concatenate([vp, vc], axis=0)
    qs = [q_ref[:, h * KVW:(h + 1) * KVW] for h in KV_HEADS]
    q_s = [jnp.concatenate([q * gmask[g] for g in range(GROUP)], axis=0) for q in qs]
    keys = [_dot(kk, rep_ref[h], 1, 0) for h in KV_HEADS]
    vals = [_dot(vv, rep_ref[h], 1, 0) for h in KV_HEADS]
    return gmask, q_s, keys, vals


def _attn_probs(n, q_s, keys, sink_ref):
    qi = lax.broadcasted_iota(jnp.int32, (GROUP * BLOCK, 2 * BLOCK), 0) % BLOCK
    kj = lax.broadcasted_iota(jnp.int32, (GROUP * BLOCK, 2 * BLOCK), 1)
    rel = BLOCK + qi - kj
    valid = (rel >= 0) & (rel < BLOCK) & ((n - 1) * BLOCK + kj >= PAD_FRONT)
    s = [jnp.where(valid, _dot(x, y, 1, 1) * (HEAD_DIM ** -0.5), -1e30) for x, y in zip(q_s, keys)]
    sink_col = [jnp.concatenate([jnp.broadcast_to(sink_ref[h, g:g + 1, 0:1], (BLOCK, 1)) for g in range(GROUP)],
                                axis=0) for h in KV_HEADS]
    m = [jnp.maximum(jnp.max(x, axis=-1, keepdims=True), c) for x, c in zip(s, sink_col)]
    ex = [jnp.exp(x - y) for x, y in zip(s, m)]
    ex_sink = [jnp.exp(c - y) for c, y in zip(sink_col, m)]
    inv = [1.0 / (jnp.sum(x, axis=-1, keepdims=True) + c) for x, c in zip(ex, ex_sink)]
    return [x * y for x, y in zip(ex, inv)], [x * y for x, y in zip(ex_sink, inv)]


def _unstack_groups(x_s, gmask):
    out = None
    for g in range(GROUP):
        t = x_s[g * BLOCK:(g + 1) * BLOCK] * gmask[g]
        out = t if out is None else out + t
    return out


def _attn_specs():
    qspec = pl.BlockSpec((BLOCK, D_MODEL), lambda n: (n, 0))
    cur = pl.BlockSpec((BLOCK, KV_DIM), lambda n: (n, 0))
    prev = pl.BlockSpec((BLOCK, KV_DIM), lambda n: (jnp.maximum(n - 1, 0), 0))
    rep = pl.BlockSpec((N_HEADS_KV, KV_DIM, KVW), lambda n: (0, 0, 0))
    sink = pl.BlockSpec((N_HEADS_KV, 8, PAIR), lambda n: (0, 0, 0))
    return qspec, cur, prev, rep, sink


def _attn_params():
    return pltpu.CompilerParams(dimension_semantics=("arbitrary",), vmem_limit_bytes=VMEM_LIMIT)


def _prob_specs():
    rows = GROUP * BLOCK
    return (pl.BlockSpec((None, N_HEADS_KV, rows, 2 * BLOCK), lambda n: (n, 0, 0, 0)),
            pl.BlockSpec((None, rows, PAIR), lambda n: (n, 0, 0)))


SINK_LANES = PAIR // N_HEADS_KV


def attn_fwd(q, k, v, sinks_b):
    lp = q.shape[0]
    nb = lp // BLOCK
    qspec, cur, prev, rep, sink = _attn_specs()

    def body(q_ref, kp_ref, kc_ref, vp_ref, vc_ref, rep_ref, sink_ref, o_ref, p_ref, ps_ref):
        gmask, q_s, keys, vals = _attn_operands(q_ref, kp_ref[...], kc_ref[...], vp_ref[...], vc_ref[...], rep_ref)
        p, p_sink = _attn_probs(pl.program_id(0), q_s, keys, sink_ref)
        o = [_dot(x, y, 1, 0) for x, y in zip(p, vals)]
        for h in KV_HEADS:
            o_ref[:, h * KVW:(h + 1) * KVW] = _unstack_groups(o[h], gmask)
            p_ref[h] = p[h].astype(BF16)
        head = lax.broadcasted_iota(jnp.int32, (GROUP * BLOCK, PAIR), 1) // SINK_LANES
        packed = p_sink[N_HEADS_KV - 1]
        for h in reversed(range(N_HEADS_KV - 1)):
            packed = jnp.where(head == h, p_sink[h], packed)
        ps_ref[...] = packed

    return pl.pallas_call(
        body, name="swa_fwd", grid=(nb,), in_specs=[qspec, prev, cur, prev, cur, rep, sink],
        out_specs=[qspec, *_prob_specs()],
        out_shape=[jax.ShapeDtypeStruct((lp, D_MODEL), F32),
                   jax.ShapeDtypeStruct((nb, N_HEADS_KV, GROUP * BLOCK, 2 * BLOCK), BF16),
                   jax.ShapeDtypeStruct((nb, GROUP * BLOCK, PAIR), F32)],
        compiler_params=_attn_params(),
    )(q, k, k, v, v, _spread_matrices(), sinks_b)


def attn_bwd(q, k, v, probs, do):
    lp = q.shape[0]
    qspec, cur, prev, rep, sink = _attn_specs()

    def body(q_ref, kp_ref, kc_ref, vp_ref, vc_ref, rep_ref, p_ref, ps_ref, do_ref, dq_ref, dkc_ref, dkp_ref, dvc_ref,
             dvp_ref, dsink_ref):
        n = pl.program_id(0)
        gmask, q_s, keys, vals = _attn_operands(q_ref, kp_ref[...], kc_ref[...], vp_ref[...], vc_ref[...], rep_ref)
        p = [p_ref[h].astype(F32) for h in KV_HEADS]
        lane = lax.broadcasted_iota(jnp.int32, (GROUP * BLOCK, PAIR), 1)
        p_sink = [jnp.sum(jnp.where(lane == h * SINK_LANES, ps_ref[...], 0.0), axis=-1, keepdims=True)
                  for h in KV_HEADS]
        do_s = [jnp.concatenate([do_ref[:, h * KVW:(h + 1) * KVW] * gmask[g] for g in range(GROUP)], axis=0)
                for h in KV_HEADS]
        dp = [_dot(x, y, 1, 1) for x, y in zip(do_s, vals)]
        delta = [jnp.sum(x * y, axis=-1, keepdims=True) for x, y in zip(p, dp)]
        ds = [x * (y - z) * (HEAD_DIM ** -0.5) for x, y, z in zip(p, dp, delta)]
        dq = [_dot(x, y, 1, 0) for x, y in zip(ds, keys)]
        dkeys_s = [_dot(x, y, 0, 0) for x, y in zip(ds, q_s)]
        dvals_s = [_dot(x, y, 0, 0) for x, y in zip(p, do_s)]
        dkeys = [_exact_dot(x, rep_ref[h], cb=1) for h, x in enumerate(dkeys_s)]
        dvals = [_exact_dot(x, rep_ref[h], cb=1) for h, x in enumerate(dvals_s)]
        dk_all = (dkeys[0] + dkeys[1]) + (dkeys[2] + dkeys[3])
        dv_all = (dvals[0] + dvals[1]) + (dvals[2] + dvals[3])
        dkp_ref[...] = dk_all[:BLOCK]
        dkc_ref[...] = dk_all[BLOCK:]
        dvp_ref[...] = dv_all[:BLOCK]
        dvc_ref[...] = dv_all[BLOCK:]
        dsinks = []
        for h in KV_HEADS:
            dq_ref[:, h * KVW:(h + 1) * KVW] = _unstack_groups(dq[h], gmask)
            dsk = -(p_sink[h] * delta[h])
            rows = [jnp.broadcast_to(jnp.sum(dsk[g * BLOCK:(g + 1) * BLOCK], axis=0, keepdims=True), (1, PAIR))
                    for g in range(GROUP)]
            dsinks.append(jnp.concatenate(rows + [jnp.zeros((8 - GROUP, PAIR), F32)], axis=0))

        @pl.when(n == 0)
        def _():
            for h in KV_HEADS:
                dsink_ref[h] = dsinks[h]

        @pl.when(n > 0)
        def _():
            for h in KV_HEADS:
                dsink_ref[h] += dsinks[h]

    kv = jax.ShapeDtypeStruct((lp, KV_DIM), F32)
    return pl.pallas_call(
        body, name="swa_bwd", grid=(lp // BLOCK,), in_specs=[qspec, prev, cur, prev, cur, rep, *_prob_specs(), qspec],
        out_specs=[qspec, cur, cur, cur, cur, sink],
        out_shape=[jax.ShapeDtypeStruct((lp, D_MODEL), F32), kv, kv, kv, kv,
                   jax.ShapeDtypeStruct((N_HEADS_KV, 8, PAIR), F32)],
        compiler_params=_attn_params(),
    )(q, k, k, v, v, _spread_matrices(), *probs, do)


def _pick_tm(lp, want):
    for tm in (384, 192, 128, 64):
        if tm <= want and lp % tm == 0:
            return tm
    raise ValueError(lp)


def _acc(shape):
    return (tuple(shape), F32)


def _ff_one(w):
    return (w, (None, D_MODEL, D_MODEL), lambda c, i: (c, 0, 0))


def _mlp_layer_fwd(name, h, wup, wdown, lg, lb, tm, hosted=None):
    def fn(c, i, h, wup, wdown, lg, lb):
        out, pre = None, []
        for s in range(N_FF_CHUNK):
            u = mm(h, wup[s])
            pre.append(u.astype(BF16))
            t = mm(sq_relu(u), wdown[s])
            out = t if out is None else out + t
        z = ALPHA * h + out
        return (_layer_norm(z, lg, lb), z, jnp.stack(pre)), ()

    (h_out, z, pre), _, *got = rowwise(name, fn, [h], [wup, wdown, lg, lb],
                                       [(D_MODEL, F32), (D_MODEL, F32), (D_MODEL, BF16, False, N_FF_CHUNK)], [], tm,
                                       hosted=hosted)
    return h_out, z, pre, (got[0] if got else None)


MLP_BWD_TILE = 528


def _mlp_layer_bwd(name, h_in, z, pre, dh_parts, wup, wdown, lg, lb, tm):
    n_parts = len(dh_parts)

    def fn_ln(c, i, z, *rest):
        dh = rest[0]
        for extra in rest[1:n_parts]:
            dh = dh + extra
        _, vjp = jax.vjp(_layer_norm, z, rest[n_parts], rest[n_parts + 1])
        dz, dlg, dlb = vjp(dh)
        return (dz,), (dlg, dlb)

    (dz,), (dlg, dlb) = rowwise(name + "_ln", fn_ln, [z] + list(dh_parts), [lg, lb], [(D_MODEL, F32)],
                                [_acc((1, D_MODEL)), _acc((1, D_MODEL))], tm)

    def fn_mlp(c, i, h, dz, wup, wdown, u):
        r = jnp.maximum(u.astype(F32), 0.0)
        du = _dot(dz, wdown, 1, 1) * (2.0 * r)
        return (_dot(du, wup, 1, 1),), (_dot(h, du, 0, 0), _dot(r * r, dz, 0, 0))

    aspec = ((N_FF_CHUNK, D_MODEL, D_MODEL), F32, (None, D_MODEL, D_MODEL), lambda c, i: (c, 0, 0))
    lp = h_in.shape[0]
    tile = MLP_BWD_TILE if lp % MLP_BWD_TILE == 0 else tm
    pre_chunk = (pre, (None, tile, D_MODEL), lambda c, i: (c, i, 0))
    (dx,), (dwup, dwdown) = rowwise(name + "_mm", fn_mlp, [h_in, dz], [_ff_one(wup), _ff_one(wdown), pre_chunk],
                                    [(D_MODEL, F32, True)], [aspec, aspec], tile, nc=N_FF_CHUNK)
    return dz, dx, dwup, dwdown, dlg, dlb


def _sum_parts(dz, dx):
    out = ALPHA * dz
    for s in range(N_FF_CHUNK):
        out = out + dx[s]
    return out


def local_step(x, loss_target, p, late=None, early_hook=None):
    seq = x.shape[0]
    lp = TOK0 + seq
    tm = _pick_tm(lp, 384)
    tms = _pick_tm(lp, 192)
    e, et = _head_matrices()
    h0 = jnp.concatenate([jnp.zeros((PAD_FRONT, D_MODEL), F32), p["meta_tokens"], x], axis=0)
    pos = jnp.maximum(jnp.arange(lp, dtype=F32) - PAD_FRONT, 0.0)
    inv_freq = 1.0 / (ROPE_THETA ** (jnp.arange(0, HEAD_DIM, 2, dtype=F32) / HEAD_DIM))
    ang = pos[:, None] * inv_freq[None, :]
    cos = jnp.tile(jnp.cos(ang), (1, PAIR // (HEAD_DIM // 2)))
    sin = jnp.tile(jnp.sin(ang), (1, PAIR // (HEAD_DIM // 2)))

    pre_vec = [p["a_mu"][j:j + 1] for j in range(6)] + [p["a_w0"], p["a_a0"], p["a_k_k"], p["a_k_a"]]
    pre_w = [p["a_w_r"], p["a_w_k"], p["a_w_v"], p["a_w1"], p["a_w2"], p["a_a1"], p["a_a2"], p["a_g1"], p["a_g2"]]
    n_vec = len(pre_vec)

    def fn_pre(c, i, h, before, e, et, *ws):
        return rwkv_pre(e, et, ws[n_vec:], None, h, _shift_down(h, before, i), *ws[:n_vec])[0], ()

    (r, lw, k2, v, an, bn, g), _ = rowwise(
        "rwkv_pre", fn_pre, [h0, _halo_before(h0, tms)], [e, et] + pre_vec + pre_w, [(D_MODEL, F32)] * 7, [], tms)
    (y, *scan_saved), scan_gathered = scan_fwd(r, lw, k2, v, an, bn, late[1][0] if late else ())
    if late:
        p = {**p, **late[1][1](scan_gathered)}

    post_c = [p["a_w_o"], p["a_gn_w"], p["a_gn_b"], p["a_r_k"], p["ln_g00"], p["ln_b00"]]

    def fn_post(c, i, y, r, k2, v, g, h0, e, et, w_o, *vecs):
        return (rwkv_post(e, et, w_o, None, y, r, k2, v, g, h0, *vecs)[0],), ()

    (h1,), _ = rowwise("rwkv_post", fn_post, [y, r, k2, v, g, h0], [e, et] + post_c, [(D_MODEL, F32)], [], tm)
    h2, z2, pre2, mlp1_gathered = _mlp_layer_fwd("mlp0_fwd", h1, p["mlp_up0"], p["mlp_down0"], p["ln_g01"],
                                                 p["ln_b01"], tm, hosted_gather(late[0][0]) if late else None)
    if late:
        p = {**p, **late[0][1](mlp1_gathered)}

    qkv_w = [p["b_w_q"], p["kv_w_k"], p["kv_w_v"]]

    def fn_qkv(c, i, h, cos, sin, wq, wk, wv):
        return qkv_proj(cos, sin, wq, wk, wv, None, h)[0], ()

    (q, k, vv), _ = rowwise("qkv_proj", fn_qkv, [h2, cos, sin], qkv_w,
                            [(D_MODEL, F32), (KV_DIM, F32), (KV_DIM, F32)], [], tm)
    sinks_b = jnp.broadcast_to(p["b_sinks"].reshape(N_HEADS_KV, GROUP, 1), (N_HEADS_KV, GROUP, PAIR))
    sinks_b = jnp.concatenate([sinks_b, jnp.zeros((N_HEADS_KV, 8 - GROUP, PAIR), F32)], axis=1)
    o, *attn_probs = attn_fwd(q, k, vv, sinks_b)

    ao_c = [p["b_w_o"], p["ln_g10"], p["ln_b10"]]

    def fn_ao(c, i, o, h, w_o, lg, lb):
        return (attn_out(w_o, None, o, h, lg, lb)[0],), ()

    (h3,), _ = rowwise("attn_out", fn_ao, [o, h2], ao_c, [(D_MODEL, F32)], [], tm)
    h4, z4, pre4, _ = _mlp_layer_fwd("mlp1_fwd", h3, p["mlp_up1"], p["mlp_down1"], p["ln_g11"], p["ln_b11"], tm)

    per = tm // TOK0

    def fn_loss(c, i, h4, *tgt_blocks):
        real = (_row_ids(i, tm) >= TOK0).astype(F32)
        err = (h4 - jnp.concatenate(tgt_blocks, axis=0)) * real
        part = 0.5 * jnp.sum(jnp.sum(err * err, axis=-1, keepdims=True), axis=0, keepdims=True) / D_MODEL
        return (err * (1.0 / D_MODEL),), (jnp.broadcast_to(part, (8, PAIR)),)

    tgt_blocks = [(loss_target, TOK0, functools.partial(lambda j, i: jnp.maximum(i * per + j - 1, 0), j))
                  for j in range(per)]
    (dh4,), (loss_acc,) = rowwise("loss", fn_loss, [h4] + tgt_blocks, [], [(D_MODEL, F32)], [_acc((8, PAIR))], tm)
    loss = loss_acc[0, 0]

    grads = {}
    dz4, dx4, grads["mlp_up1"], grads["mlp_down1"], grads["ln_g11"], grads["ln_b11"] = _mlp_layer_bwd(
        "mlp1_bwd", h3, z4, pre4, [dh4], p["mlp_up1"], p["mlp_down1"], p["ln_g11"], p["ln_b11"], tm)

    def fn_ao_b(c, i, dz, dx, o, h, w_o, lg, lb):
        (do, dh, dlg, dlb), (dw_o,) = vjp_taps(functools.partial(attn_out, w_o), [(tm, D_MODEL)], [o, h, lg, lb],
                                               _sum_parts(dz, dx))
        return (do, dh), (dw_o, dlg, dlb)

    (do, dh2_a), (grads["b_w_o"], grads["ln_g10"], grads["ln_b10"]) = rowwise(
        "attn_out_bwd", fn_ao_b, [dz4, dx4, o, h2], ao_c, [(D_MODEL, F32)] * 2,
        [_acc((D_MODEL, D_MODEL)), _acc((1, D_MODEL)), _acc((1, D_MODEL))], tm)

    dq, dkc, dkp, dvc, dvp, dsinks = attn_bwd(q, k, vv, attn_probs, do)
    grads["b_sinks"] = dsinks[:, :GROUP, 0].reshape(1, N_HEADS)
    zblk = jnp.zeros((BLOCK, KV_DIM), F32)
    dkp_s = jnp.concatenate([dkp[BLOCK:], zblk], axis=0)
    dvp_s = jnp.concatenate([dvp[BLOCK:], zblk], axis=0)

    def fn_qkv_b(c, i, h, cos, sin, dq, dkc, dkp, dvc, dvp, wq, wk, wv):
        return vjp_taps(functools.partial(qkv_proj, cos, sin, wq, wk, wv),
                        [(tm, D_MODEL), (tm, KV_DIM), (tm, KV_DIM)], [h], (dq, dkc + dkp, dvc + dvp))

    (dh2_q,), (grads["b_w_q"], grads["kv_w_k"], grads["kv_w_v"]) = rowwise(
        "qkv_proj_bwd", fn_qkv_b, [h2, cos, sin, dq, dkc, dkp_s, dvc, dvp_s], qkv_w, [(D_MODEL, F32)],
        [_acc((D_MODEL, D_MODEL)), _acc((D_MODEL, KV_DIM)), _acc((D_MODEL, KV_DIM))], tm)

    dz2, dx2, grads["mlp_up0"], grads["mlp_down0"], grads["ln_g01"], grads["ln_b01"] = _mlp_layer_bwd(
        "mlp0_bwd", h1, z2, pre2, [dh2_a, dh2_q], p["mlp_up0"], p["mlp_down0"], p["ln_g01"], p["ln_b01"], tm)

    def fn_post_b(c, i, dz, dx, y, r, k2, v, g, h0, e, et, w_o, *vecs):
        out, dws = vjp_taps(functools.partial(rwkv_post, e, et, w_o), [(tms, D_MODEL)],
                            [y, r, k2, v, g, h0] + list(vecs), _sum_parts(dz, dx))
        return out[:6], tuple(dws) + tuple(out[6:])

    early_srcs = early_hook[0](grads) if early_hook else ()
    (dy, dr_c, dk_c, dv_c, dg, dh0_c), post_g, *early_got = rowwise(
        "rwkv_post_bwd", fn_post_b, [dz2, dx2, y, r, k2, v, g, h0], [e, et] + post_c, [(D_MODEL, F32)] * 6,
        [_acc((D_MODEL, D_MODEL))] + [_acc((1, D_MODEL))] * 5, tms,
        hosted=hosted_pair_exchange(early_srcs) if early_hook else None)
    for name, val in zip(["a_w_o", "a_gn_w", "a_gn_b", "a_r_k", "ln_g00", "ln_b00"], post_g):
        grads[name] = val

    (dr, dlw, dk2, dv, dan, dbn), early_from_chips = scan_bwd(
        r, lw, k2, v, an, bn, scan_saved, dy, (dr_c, dk_c, dv_c),
        early_hook[1](early_srcs, early_got[0]) if early_hook else ())

    def fn_pre_b(c, i, h, before, dr, dlw, dk2, dv, dan, dbn, dg, e, et, *ws):
        hp = _shift_down(h, before, i)
        real = (_row_ids(i, tms) >= PAD_FRONT).astype(F32)
        cot = tuple(t * real for t in (dr, dlw, dk2, dv, dan, dbn, dg))
        out, dws = vjp_taps(functools.partial(rwkv_pre, e, et, ws[n_vec:]), [(tms, n) for n in PRE_TAPS],
                            [h, hp] + list(ws[:n_vec]), cot)
        return out[:2], tuple(out[2:]) + tuple(dws)

    (dh0_p, dhp), pre_g = rowwise(
        "rwkv_pre_bwd", fn_pre_b, [h0, _halo_before(h0, tms), dr, dlw, dk2, dv, dan, dbn, dg],
        [e, et] + pre_vec + pre_w, [(D_MODEL, F32)] * 2,
        [_acc((1, D_MODEL))] * n_vec + [_acc(w.shape) for w in pre_w], tms)
    grads["a_mu"] = jnp.concatenate(pre_g[:6], axis=0)
    for name, val in zip(["a_w0", "a_a0", "a_k_k", "a_k_a", "a_w_r", "a_w_k", "a_w_v", "a_w1", "a_w2", "a_a1",
                          "a_a2", "a_g1", "a_g2"], pre_g[6:]):
        grads[name] = val

    def fn_add(c, i, a, b, d, after):
        return (a + b + _shift_up(d, after, i, lp // tm),), ()

    (dh0,), _ = rowwise("grad_h0", fn_add, [dh0_c, dh0_p, dhp, _halo_after(dhp, tm)], [], [(D_MODEL, F32)], [], tm)
    grads["meta_tokens"] = dh0[PAD_FRONT:TOK0]
    return loss, dh0[TOK0:], grads, early_from_chips


ANY = pl.BlockSpec(memory_space=pl.ANY)
XY_FLIPS = ((0, 1), (1, 0), (1, 1))


def _flip(v, bit):
    return 1 - v if bit else v


def _sem_scratch(n):
    return [pltpu.SemaphoreType.DMA((n,)), pltpu.SemaphoreType.DMA((n,))]


def gather_copies(src, dst, ici_send, ici_recv, d2d_send, d2d_recv):
    npeer = len(XY_FLIPS)
    x, y, c = lax.axis_index("x"), lax.axis_index("y"), lax.axis_index("c")

    def half(ref, k, which):
        h = src[k].shape[0] // 2
        start = which * h
        return ref.at[pl.ds(pl.multiple_of(start, 8) if h % 8 == 0 else start, h)]

    def ici(k, j, slot):
        fx, fy = XY_FLIPS[j]
        return pltpu.make_async_remote_copy(
            src_ref=half(src[k], k, c), dst_ref=half(dst[k].at[slot], k, c), send_sem=ici_send.at[k * npeer + j],
            recv_sem=ici_recv.at[k * npeer + j], device_id=(_flip(x, fx), _flip(y, fy), c), device_id_type=MESH)

    def d2d(k, j, which):
        fx, fy = XY_FLIPS[j]
        landed = half(dst[k].at[2 * _flip(x, fx) + _flip(y, fy)], k, which)
        return pltpu.make_async_remote_copy(
            src_ref=landed, dst_ref=landed, send_sem=d2d_send.at[k * npeer + j], recv_sem=d2d_recv.at[k * npeer + j],
            device_id=(x, y, 1 - c), device_id_type=MESH)

    pairs = [(k, j) for k in range(len(src)) for j in range(npeer)]
    return ([ici(k, j, 2 * x + y) for k, j in pairs],
            [ici(k, j, 2 * _flip(x, XY_FLIPS[j][0]) + _flip(y, XY_FLIPS[j][1])) for k, j in pairs],
            [d2d(k, j, c) for k, j in pairs], [d2d(k, j, 1 - c) for k, j in pairs])


def gather_scratch(n):
    return _sem_scratch(n * len(XY_FLIPS)) * 2


def gathered_shapes(shards):
    return [jax.ShapeDtypeStruct((N_SHARD,) + s.shape, s.dtype) for s in shards]


def fill_own(gathered, shards):
    if not shards:
        return []
    slot = 2 * lax.axis_index("x") + lax.axis_index("y")
    return [lax.dynamic_update_index_in_dim(g, s, slot, 0) for g, s in zip(gathered, shards)]


def all_gather_shards(shards):
    n = len(shards)

    def body(*refs):
        sends, arrivals, forwards, forwarded = gather_copies(refs[:n], refs[n:2 * n], *refs[2 * n:])
        for cp in sends:
            cp.start()
        for landed, onward in zip(arrivals, forwards):
            landed.wait_recv()
            onward.start()
        for cp in forwarded:
            cp.wait_recv()
        for cp in sends + forwards:
            cp.wait_send()

    out = pl.pallas_call(body, name="gather_weights", in_specs=[ANY] * n, out_specs=[ANY] * n,
                         out_shape=gathered_shapes(shards), scratch_shapes=gather_scratch(n))(*shards)
    return fill_own(out, shards)


def placement():
    x, y, c = lax.axis_index("x"), lax.axis_index("y"), lax.axis_index("c")
    me = 2 * x + y
    others = [j + (j >= me).astype(jnp.int32) for j in range(N_SHARD - 1)]
    return jnp.stack([c, me] + others).astype(jnp.int32)


def hosted_gather(shards):
    return (gather_copies, list(shards), gathered_shapes(shards), gather_scratch(len(shards)),
            lambda got: fill_own(got, shards))


def pair_exchange_copies(src, got, send_sems, recv_sems):
    x, y, c = lax.axis_index("x"), lax.axis_index("y"), lax.axis_index("c")

    def copy(k):
        half = src[k].shape[1] // 2
        theirs = src[k].at[:, pl.ds(pl.multiple_of((1 - c) * half, 8), half), :]
        return pltpu.make_async_remote_copy(
            src_ref=theirs, dst_ref=got[k], send_sem=send_sems.at[k], recv_sem=recv_sems.at[k],
            device_id=(x, y, 1 - c), device_id_type=MESH)

    sends = [copy(k) for k in range(len(src))]
    return sends, sends, [], []


def _half_shapes(sources):
    return [jax.ShapeDtypeStruct((s.shape[0], s.shape[1] // 2, s.shape[2]), s.dtype) for s in sources]


def hosted_pair_exchange(sources):
    return (pair_exchange_copies, list(sources), _half_shapes(sources), _sem_scratch(len(sources)), list)


def pair_exchange(name, sources):
    n = len(sources)

    def body(*refs):
        sends, arrivals, _, _ = pair_exchange_copies(refs[:n], refs[n:2 * n], *refs[2 * n:])
        for cp in sends:
            cp.start()
        for cp in arrivals:
            cp.wait_recv()
        for cp in sends:
            cp.wait_send()

    halves = _half_shapes(sources)
    return pl.pallas_call(body, name=name, in_specs=[ANY] * n, out_specs=[ANY] * n,
                          out_shape=halves, scratch_shapes=_sem_scratch(n))(*sources)


def chip_exchange(parts):
    n = len(parts)

    def body(*refs):
        sends, arrivals = chip_exchange_copies(refs[:n], refs[n:2 * n], *refs[2 * n:])
        for cp in sends:
            cp.start()
        for cp in arrivals:
            cp.wait_recv()
        for cp in sends:
            cp.wait_send()

    return pl.pallas_call(
        body, name="grads_chip_exchange", in_specs=[ANY] * n, out_specs=[ANY] * n,
        out_shape=[jax.ShapeDtypeStruct(p.shape, p.dtype) for p in parts],
        scratch_shapes=_sem_scratch(n * len(XY_FLIPS)),
    )(*parts)


def chip_exchange_copies(src, dst, send_sems, recv_sems):
    npeer = len(XY_FLIPS)
    x, y, c = lax.axis_index("x"), lax.axis_index("y"), lax.axis_index("c")
    me = 2 * x + y

    def copy(k, j, sending):
        fx, fy = XY_FLIPS[j]
        px, py = _flip(x, fx), _flip(y, fy)
        peer = 2 * px + py
        return pltpu.make_async_remote_copy(
            src_ref=src[k].at[peer], dst_ref=dst[k].at[me if sending else peer],
            send_sem=send_sems.at[k * npeer + j], recv_sem=recv_sems.at[k * npeer + j],
            device_id=(px, py, c), device_id_type=MESH)

    pairs = [(k, j) for k in range(len(src)) for j in range(npeer)]
    return [copy(k, j, True) for k, j in pairs], [copy(k, j, False) for k, j in pairs]


def sibling_share(halves):
    n = len(halves)

    def body(*refs):
        src, got = refs[:n], refs[n:2 * n]
        send_sems, recv_sems = refs[2 * n:]
        x, y, c = lax.axis_index("x"), lax.axis_index("y"), lax.axis_index("c")
        sends = [pltpu.make_async_remote_copy(
            src_ref=src[k], dst_ref=got[k], send_sem=send_sems.at[k], recv_sem=recv_sems.at[k],
            device_id=(x, y, 1 - c), device_id_type=MESH) for k in range(n)]
        for cp in sends:
            cp.start()
        for cp in sends:
            cp.wait_recv()
        for cp in sends:
            cp.wait_send()

    return pl.pallas_call(
        body, name="grads_sibling_share", in_specs=[ANY] * n, out_specs=[ANY] * n,
        out_shape=[jax.ShapeDtypeStruct(h.shape, h.dtype) for h in halves], scratch_shapes=_sem_scratch(n),
    )(*halves)


ADD_TILE_ELEMS = 512 * 1024


def _row_tile(rows, cols):
    return max(t for t in range(8, rows + 1, 8) if rows % t == 0 and t * cols <= ADD_TILE_ELEMS)


def _prefetch_call(body, name, place, grid, in_specs, out_specs, out_shape, args):
    return pl.pallas_call(
        body, name=name, out_shape=out_shape,
        grid_spec=pltpu.PrefetchScalarGridSpec(num_scalar_prefetch=1, grid=grid, in_specs=in_specs,
                                               out_specs=out_specs),
        compiler_params=pltpu.CompilerParams(dimension_semantics=("arbitrary",) * len(grid),
                                             vmem_limit_bytes=VMEM_LIMIT),
    )(place, *args)


def pair_add(name, place, src, got, dtype):
    n4, half, cols = got.shape
    tile = _row_tile(half, cols)
    nt = half // tile

    def body(pr, a_ref, b_ref, o_ref):
        o_ref[...] = (a_ref[...] + b_ref[...]).astype(o_ref.dtype)

    mine = pl.BlockSpec((None, tile, cols), lambda s, i, pr: (s, pr[0] * nt + i, 0))
    blk = pl.BlockSpec((None, tile, cols), lambda s, i, pr: (s, i, 0))
    return _prefetch_call(body, name, place, (n4, nt), [mine, blk], blk,
                          jax.ShapeDtypeStruct(got.shape, dtype), (src, got))


def chip_add(name, place, part, from_chips):
    _, half, cols = part.shape
    tile = _row_tile(half, cols)

    def body(pr, own_ref, r0_ref, r1_ref, r2_ref, o_ref):
        me = pr[1]
        own, r0, r1, r2 = (r[...].astype(F32) for r in (own_ref, r0_ref, r1_ref, r2_ref))
        t0 = jnp.where(me == 0, own, r0)
        t1 = jnp.where(me == 0, r0, jnp.where(me == 1, own, r1))
        t2 = jnp.where(me <= 1, r1, jnp.where(me == 2, own, r2))
        t3 = jnp.where(me == 3, own, r2)
        o_ref[...] = ((t0 + t1) + t2) + t3

    def slab(j):
        return pl.BlockSpec((None, tile, cols), lambda i, pr: (pr[j], i, 0))

    return _prefetch_call(body, name, place, (half // tile,), [slab(1), slab(2), slab(3), slab(4)],
                          pl.BlockSpec((tile, cols), lambda i, pr: (i, 0)),
                          jax.ShapeDtypeStruct((half, cols), F32), (part, from_chips, from_chips, from_chips))


def pair_adds(tag, place, sources, got, narrow):
    return [pair_add(f"grads_pair_add_{tag}{k}", place, s, g, BF16 if nar else F32)
            for k, (s, g, nar) in enumerate(zip(sources, got, narrow))]


def finish_sums(place, parts, from_chips):
    halves = [chip_add(f"grads_chip_add{k}", place, p, f) for k, (p, f) in enumerate(zip(parts, from_chips))]
    return list(zip(halves, sibling_share(halves)))


ADAM_ROWS = 256


def adamw_update(name, place, halves, w, m, v):
    nsub, rows, cols = w.shape
    half = rows // 2
    tr = ADAM_ROWS if half % ADAM_ROWS == 0 else half
    nth = half // tr

    def body(pr, *refs):
        g_refs, (w_ref, m_ref, v_ref, g_ref, d_ref, nm_ref, nv_ref) = refs[:2 * nsub], refs[2 * nsub:]
        l = pl.program_id(0)
        mine = (pl.program_id(1) // nth) == pr[0]
        g = None
        for s in range(nsub):
            gs = jnp.where(mine, g_refs[2 * s][...], g_refs[2 * s + 1][...])
            g = gs if g is None else jnp.where(l == s, gs, g)
        m2 = ADAM_B1 * m_ref[...] + (1.0 - ADAM_B1) * g
        v2 = ADAM_B2 * v_ref[...] + (1.0 - ADAM_B2) * (g * g)
        m_hat = m2 / (1.0 - ADAM_B1 ** ADAM_STEP)
        v_hat = v2 / (1.0 - ADAM_B2 ** ADAM_STEP)
        g_ref[...] = g
        d_ref[...] = -ADAM_LR * (m_hat / (jnp.sqrt(v_hat) + ADAM_EPS) + ADAM_WD * w_ref[...])
        nm_ref[...] = m2
        nv_ref[...] = v2

    own = pl.BlockSpec((tr, cols), lambda l, i, pr: (jnp.where(i // nth == pr[0], i % nth, 0), 0))
    got = pl.BlockSpec((tr, cols), lambda l, i, pr: (jnp.where(i // nth == pr[0], 0, i % nth), 0))
    blk = pl.BlockSpec((None, tr, cols), lambda l, i, pr: (l, i, 0))
    out = jax.ShapeDtypeStruct((nsub, rows, cols), F32)
    return _prefetch_call(body, name, place, (nsub, rows // tr), [own, got] * nsub + [blk] * 3, [blk] * 4,
                          [out] * 4, [h for pair in halves for h in pair] + [w, m, v])


WEIGHT_NAMES = ("meta_tokens", "a_mu", "a_w_r", "a_w_k", "a_w_v", "a_w_o", "a_w0", "a_w1", "a_w2", "a_a0", "a_a1",
                "a_a2", "a_g1", "a_g2", "a_k_k", "a_k_a", "a_r_k", "a_gn_w", "a_gn_b", "kv_w_k", "kv_w_v", "b_w_q",
                "b_sinks", "b_w_o", "mlp_w_up", "mlp_w_down", "ln_g", "ln_b")
BIG_NAMES = ("a_w_r", "a_w_k", "a_w_v", "a_w_o", "b_w_q", "b_w_o")
EARLY_NAMES, LATE_NAMES = BIG_NAMES[:3], BIG_NAMES[3:]
PACK_MATS = (("kv_w_k", 256), ("kv_w_v", 256), ("a_w1", 64), ("a_a1", 64), ("a_g1", 128), ("a_w2", 64),
             ("a_a2", 64), ("a_g2", 128))
COLUMN_CUT = ("a_w2", "a_a2", "a_g2")
PACK_VECS = (("a_mu", 6), ("a_w0", 1), ("a_a0", 1), ("a_k_k", 1), ("a_k_a", 1), ("a_gn_w", 1), ("a_gn_b", 1),
             ("ln_g", 4), ("ln_b", 4), ("meta_tokens", 16))
PACK_REPL = (("a_r_k", 4), ("b_sinks", 1))
SHARD_W = D_MODEL // N_SHARD


def _tiles(rows):
    return -(-rows // SUBLANES) * SUBLANES


N_MAT_ROWS = sum(_tiles(r) for _, r in PACK_MATS)
N_VEC_ROWS = sum(_tiles(r) for _, r in PACK_VECS)
N_PACK_ROWS = -(-(N_MAT_ROWS + N_VEC_ROWS + sum(_tiles(r) for _, r in PACK_REPL)) // 16) * 16
N_GATHER_VEC_ROWS = -(-N_VEC_ROWS // 16) * 16


def _pad_rows(arr, axis):
    rows = arr.shape[axis]
    pad = [(0, 0)] * arr.ndim
    pad[axis] = (0, _tiles(rows) - rows)
    return jnp.pad(arr, pad) if _tiles(rows) != rows else arr


def _pack_rows(arr):
    if arr.size == N_HEADS:
        arr = jnp.pad(arr.reshape(1, N_HEADS), ((0, 0), (0, SHARD_W - N_HEADS)))
    return _pad_rows(arr.reshape(-1, SHARD_W), 0)


def pack_small(get):
    parts = [_pack_rows(get(name)) for name, _ in PACK_MATS + PACK_VECS + PACK_REPL]
    used = sum(p.shape[0] for p in parts)
    return jnp.concatenate(parts + [jnp.zeros((N_PACK_ROWS - used, SHARD_W), F32)], axis=0)


def unpack_small(pack, shapes):
    out, off = {}, 0
    for name, rows in PACK_MATS + PACK_VECS + PACK_REPL:
        piece = pack[off:off + rows]
        off += _tiles(rows)
        out[name] = piece[:, :N_HEADS].reshape(shapes[name]) if name == "b_sinks" else piece.reshape(shapes[name])
    return out


def whole_weights(big_names, gathered_big, mats, vecs, a_r_k, b_sinks):
    p = {name: g.reshape(D_MODEL, D_MODEL) for name, g in zip(big_names, gathered_big)}
    off = 0
    for name, rows in PACK_MATS:
        piece = mats[:, off:off + rows]
        off += rows
        if name in COLUMN_CUT:
            p[name] = piece.transpose(1, 0, 2).reshape(rows, D_MODEL)
        else:
            p[name] = piece.reshape(D_MODEL, rows)
    v = vecs.transpose(1, 0, 2).reshape(-1, D_MODEL)
    off = 0
    for name, rows in PACK_VECS:
        p[name] = v[off:off + rows]
        off += _tiles(rows)
    for i in range(2):
        for j in range(2):
            p[f"ln_g{i}{j}"] = p["ln_g"][2 * i + j:2 * i + j + 1]
            p[f"ln_b{i}{j}"] = p["ln_b"][2 * i + j:2 * i + j + 1]
    p["a_r_k"] = a_r_k.reshape(1, D_MODEL)
    p["b_sinks"] = b_sinks
    return p


def small_grad_pack(g):
    parts = []
    for name, rows in PACK_MATS:
        if name in COLUMN_CUT:
            parts.append(g[name].reshape(rows, N_SHARD, SHARD_W).transpose(1, 0, 2))
        else:
            parts.append(g[name].reshape(N_SHARD, rows, SHARD_W))
    vecs = {n: g[n] for n in ("a_mu", "a_w0", "a_a0", "a_k_k", "a_k_a", "a_gn_w", "a_gn_b", "meta_tokens")}
    vecs["ln_g"] = jnp.concatenate([g[f"ln_g{i}{j}"] for i in range(2) for j in range(2)], axis=0)
    vecs["ln_b"] = jnp.concatenate([g[f"ln_b{i}{j}"] for i in range(2) for j in range(2)], axis=0)
    for name, rows in PACK_VECS:
        parts.append(_pad_rows(vecs[name].reshape(rows, N_SHARD, SHARD_W).transpose(1, 0, 2), 1))
    r_k = jnp.broadcast_to(g["a_r_k"].reshape(1, -1, SHARD_W), (N_SHARD, D_MODEL // SHARD_W, SHARD_W))
    sinks = jnp.pad(g["b_sinks"].reshape(1, 1, N_HEADS), ((0, 0), (0, 0), (0, SHARD_W - N_HEADS)))
    parts += [_pad_rows(r_k, 1), _pad_rows(jnp.broadcast_to(sinks, (N_SHARD, 1, SHARD_W)), 1)]
    used = sum(p.shape[1] for p in parts)
    parts.append(jnp.zeros((N_SHARD, N_PACK_ROWS - used, SHARD_W), F32))
    return jnp.concatenate(parts, axis=1)


def train_step(vals):
    w = {n: vals[n] for n in WEIGHT_NAMES}
    w_pack = pack_small(lambda n: w[n])
    early = [w[n][0].astype(BF16) for n in EARLY_NAMES]
    early += [w_pack[:N_MAT_ROWS].astype(BF16), w_pack[N_MAT_ROWS:N_MAT_ROWS + N_GATHER_VEC_ROWS]]
    gathered = all_gather_shards(early)
    ne = len(EARLY_NAMES)
    p = whole_weights(EARLY_NAMES, gathered[:ne], gathered[ne], gathered[ne + 1][:, :N_VEC_ROWS], w["a_r_k"],
                      w["b_sinks"])
    nb = len(BIG_NAMES)

    def late_set(big, layer):
        shards = [w[n][0].astype(BF16) for n in big]
        shards += [w["mlp_w_up"][layer].astype(BF16), w["mlp_w_down"][layer].astype(BF16)]

        def weights(got):
            out = {n: x.reshape(D_MODEL, D_MODEL) for n, x in zip(big, got)}
            out[f"mlp_up{layer}"], out[f"mlp_down{layer}"] = got[len(big):]
            return out

        return shards, weights

    late = (late_set((), 1), late_set(LATE_NAMES, 0))

    place = placement()
    ready = {}
    a_names, b_names = BIG_NAMES[:4], BIG_NAMES[4:]

    def early_sources(g):
        return ([g[n].reshape(N_SHARD, SHARD_W, D_MODEL) for n in b_names]
                + [g["mlp_up0"], g["mlp_up1"], g["mlp_down0"], g["mlp_down1"]])

    def early_parts(srcs, got):
        ready["parts"] = pair_adds("early", place, srcs, got, [True] * len(srcs))
        return ready["parts"]

    loss, gx, g, early_from_chips = local_step(vals["x"][0], vals["loss_target"][0], p, late,
                                               (early_sources, early_parts))
    loss = lax.psum(loss, ("x", "y", "c"))
    srcs = [g[n].reshape(N_SHARD, SHARD_W, D_MODEL) for n in a_names] + [small_grad_pack(g)]
    rest = pair_adds("late", place, srcs, pair_exchange("grads_pair_exchange", srcs), [True] * len(a_names) + [False])
    rest_from_chips = chip_exchange(rest)
    na = len(a_names)
    halves = finish_sums(place, rest[:na] + ready["parts"] + rest[na:],
                         list(rest_from_chips[:na]) + list(early_from_chips) + list(rest_from_chips[na:]))

    res = {}
    for k, n in enumerate(BIG_NAMES):
        res[n] = adamw_update("adamw_" + n, place, halves[k:k + 1], w[n], vals["m_" + n], vals["v_" + n])
    for k, n in ((nb, "mlp_w_up"), (nb + 2, "mlp_w_down")):
        res[n] = adamw_update("adamw_" + n, place, halves[k:k + 2], w[n], vals["m_" + n], vals["v_" + n])
    packs = adamw_update("adamw_small", place, halves[-1:], w_pack[None], pack_small(lambda n: vals["m_" + n])[None],
                         pack_small(lambda n: vals["v_" + n])[None])
    shapes = {n: w[n].shape for n in WEIGHT_NAMES}
    small = [unpack_small(pk[0], shapes) for pk in packs]
    outs = [loss, gx[None]]
    for t in range(4):
        outs += [res[n][t] if n in res else small[t][n] for n in WEIGHT_NAMES]
    return tuple(outs)


def kernel(x, meta_tokens, a_mu, a_w_r, a_w_k, a_w_v, a_w_o, a_w0, a_w1, a_w2, a_a0, a_a1, a_a2, a_g1, a_g2, a_k_k,
           a_k_a, a_r_k, a_gn_w, a_gn_b, kv_w_k, kv_w_v, b_w_q, b_sinks, b_w_o, mlp_w_up, mlp_w_down, ln_g, ln_b,
           loss_target, m_meta_tokens, m_a_mu, m_a_w_r, m_a_w_k, m_a_w_v, m_a_w_o, m_a_w0, m_a_w1, m_a_w2, m_a_a0,
           m_a_a1, m_a_a2, m_a_g1, m_a_g2, m_a_k_k, m_a_k_a, m_a_r_k, m_a_gn_w, m_a_gn_b, m_kv_w_k, m_kv_w_v,
           m_b_w_q, m_b_sinks, m_b_w_o, m_mlp_w_up, m_mlp_w_down, m_ln_g, m_ln_b, v_meta_tokens, v_a_mu, v_a_w_r,
           v_a_w_k, v_a_w_v, v_a_w_o, v_a_w0, v_a_w1, v_a_w2, v_a_a0, v_a_a1, v_a_a2, v_a_g1, v_a_g2, v_a_k_k,
           v_a_k_a, v_a_r_k, v_a_gn_w, v_a_gn_b, v_kv_w_k, v_kv_w_v, v_b_w_q, v_b_sinks, v_b_w_o, v_mlp_w_up,
           v_mlp_w_down, v_ln_g, v_ln_b):
    return train_step(dict(locals()))
```

```python
import functools

import numpy as np
import jax
import jax.numpy as jnp
from jax import lax
from jax.experimental import pallas as pl
from jax.experimental.pallas import tpu as pltpu

F32 = jnp.float32
BF16 = jnp.bfloat16

D_MODEL = 1024
N_HEADS = 16
HEAD_DIM = 64
N_HEADS_KV = 4
GROUP = 4
KV_DIM = N_HEADS_KV * HEAD_DIM
N_META = 16
BLOCK = 128
PAD_FRONT = BLOCK - N_META
TOK0 = PAD_FRONT + N_META
N_FF_CHUNK = 4
N_SHARD = 4
GN_EPS = 64e-5
LN_EPS = 1e-5
ROPE_THETA = 10000.0
ALPHA = 4.0 ** 0.25
ADAM_LR, ADAM_B1, ADAM_B2, ADAM_EPS, ADAM_WD, ADAM_STEP = 0.001, 0.9, 0.999, 1e-08, 0.01, 10
SCAN_T = 64
PAIR = 128
KVW = GROUP * HEAD_DIM
VMEM_LIMIT = 60 * 1024 * 1024
MESH = pl.DeviceIdType.MESH


def _dot(a, b, ca, cb):
    return lax.dot_general(a.astype(BF16), b.astype(BF16), (((ca,), (cb,)), ((), ())),
                           preferred_element_type=F32)


@jax.custom_vjp
def mm(a, b):
    return _dot(a, b, 1, 0)


def _mm_fwd(a, b):
    return mm(a, b), b


def _mm_bwd(b, g):
    return _dot(g, b, 1, 1), jnp.zeros_like(b)


mm.defvjp(_mm_fwd, _mm_bwd)


@jax.custom_vjp
def mm_tap(a, b, tap):
    return _dot(a, b, 1, 0)


mm_tap.defvjp(lambda a, b, tap: (_dot(a, b, 1, 0), b), lambda b, g: (_dot(g, b, 1, 1), jnp.zeros_like(b), g))


def tmm(x, w, taps, xs):
    y = mm(x, w) if taps is None else mm_tap(x, w, taps[len(xs)])
    xs.append(x)
    return y


def vjp_taps(core, tap_shapes, args, cot):
    taps = [jnp.zeros(s, F32) for s in tap_shapes]
    _, vjp, xs = jax.vjp(core, taps, *args, has_aux=True)
    out = vjp(cot)
    return out[1:], [_dot(x, g, 0, 0) for x, g in zip(xs, out[0])]


def _split3(x):
    x1 = x.astype(BF16)
    r1 = x - x1.astype(F32)
    x2 = r1.astype(BF16)
    x3 = (r1 - x2.astype(F32)).astype(BF16)
    return x1, x2, x3


def _exact_dot(x, m01, cb=0):
    acc = None
    for piece in _split3(x)[:2]:
        t = lax.dot_general(piece, m01, (((1,), (cb,)), ((), ())), preferred_element_type=F32)
        acc = t if acc is None else acc + t
    return acc


def _head_matrices():
    e = np.zeros((D_MODEL, N_HEADS), np.float32)
    e[np.arange(D_MODEL), np.arange(D_MODEL) // HEAD_DIM] = 1.0
    return jnp.asarray(e, BF16), jnp.asarray(e.T, BF16)


@jax.custom_vjp
def hsum(x, e, et):
    return _exact_dot(x, e)


@jax.custom_vjp
def hbc(s, e, et):
    return _exact_dot(s, et)


hsum.defvjp(lambda x, e, et: (_exact_dot(x, e), (e, et)),
            lambda res, g: (hbc(g, *res), jnp.zeros_like(res[0]), jnp.zeros_like(res[1])))
hbc.defvjp(lambda s, e, et: (_exact_dot(s, et), (e, et)),
           lambda res, g: (hsum(g, *res), jnp.zeros_like(res[0]), jnp.zeros_like(res[1])))


def _sigmoid(u):
    return 0.5 * (jnp.tanh(0.5 * u) + 1.0)


def _softplus(u):
    return jnp.maximum(u, 0.0) + jnp.log(1.0 + jnp.exp(-jnp.abs(u)))


def _layer_norm(z, g, b):
    mu = jnp.mean(z, axis=-1, keepdims=True)
    zc = z - mu
    var = jnp.mean(zc * zc, axis=-1, keepdims=True)
    return zc * lax.rsqrt(var + LN_EPS) * g + b


def _zero_map(nd):
    return lambda c, i: (0,) * nd


def _params():
    return pltpu.CompilerParams(dimension_semantics=("arbitrary", "arbitrary"), vmem_limit_bytes=VMEM_LIMIT)


def rowwise(name, fn, rows, consts, out_rows, out_accs, tm, nc=1, hosted=None):
    lp = rows[0].shape[-2]
    nt = lp // tm
    assert nt * tm == lp, (name, lp, tm)
    copies_fn, hosted_src, hosted_shapes, hosted_scratch, hosted_post = hosted or (None, (), [], [], None)
    ng = len(hosted_src)
    in_specs, args = [], []
    for a in rows:
        if isinstance(a, tuple):
            a, block_rows, block_index = a
            in_specs.append(pl.BlockSpec((block_rows, a.shape[1]),
                                         functools.partial(lambda f, c, i: (f(i), 0), block_index)))
        elif a.ndim == 2:
            in_specs.append(pl.BlockSpec((tm, a.shape[1]), lambda c, i: (i, 0)))
        else:
            in_specs.append(pl.BlockSpec((a.shape[0], tm, a.shape[2]), lambda c, i: (0, i, 0)))
        args.append(a)
    for cst in consts:
        if isinstance(cst, tuple):
            arr, bs, im = cst
            in_specs.append(pl.BlockSpec(bs, im))
        else:
            arr = cst
            in_specs.append(pl.BlockSpec(arr.shape, _zero_map(arr.ndim), pipeline_mode=pl.Buffered(1)))
        args.append(arr)
    out_shape, out_specs, acc_per_chunk = [], [], []
    for spec in out_rows:
        if len(spec) == 4:
            out_shape.append(jax.ShapeDtypeStruct((spec[3], lp, spec[0]), spec[1]))
            out_specs.append(pl.BlockSpec((spec[3], tm, spec[0]), lambda c, i: (0, i, 0)))
        elif len(spec) == 3 and spec[2]:
            out_shape.append(jax.ShapeDtypeStruct((nc, lp, spec[0]), spec[1]))
            out_specs.append(pl.BlockSpec((None, tm, spec[0]), lambda c, i: (c, i, 0)))
        else:
            out_shape.append(jax.ShapeDtypeStruct((lp, spec[0]), spec[1]))
            out_specs.append(pl.BlockSpec((tm, spec[0]), lambda c, i: (i, 0)))
    for spec in out_accs:
        out_shape.append(jax.ShapeDtypeStruct(spec[0], spec[1]))
        if len(spec) == 4:
            out_specs.append(pl.BlockSpec(spec[2], spec[3]))
            acc_per_chunk.append(True)
        else:
            out_specs.append(pl.BlockSpec(spec[0], _zero_map(len(spec[0])), pipeline_mode=pl.Buffered(1)))
            acc_per_chunk.append(False)
    n_in, n_or, n_out = len(args), len(out_rows), len(out_shape)

    def body(*refs):
        c = pl.program_id(0)
        i = pl.program_id(1)
        if ng:
            src, dst = refs[n_in:n_in + ng], refs[n_in + ng + n_out:n_in + 2 * ng + n_out]
            sends, arrivals, forwards, forwarded = copies_fn(src, dst, *refs[n_in + 2 * ng + n_out:])

            @pl.when(jnp.logical_and(c == 0, i == 0))
            def _():
                for cp in sends:
                    cp.start()

        vals = [r[...] for r in refs[:n_in]]
        outs_r, outs_a = fn(c, i, *vals)
        out_refs = refs[n_in + ng:n_in + ng + n_out]
        for ref, val in zip(out_refs[:n_or], outs_r):
            ref[...] = val.astype(ref.dtype)
        for ref, val, per_chunk in zip(out_refs[n_or:], outs_a, acc_per_chunk):
            first = (i == 0) if per_chunk else jnp.logical_and(i == 0, c == 0)

            @pl.when(first)
            def _():
                ref[...] = val.astype(ref.dtype)

            @pl.when(jnp.logical_not(first))
            def _():
                ref[...] += val.astype(ref.dtype)

        if ng:
            @pl.when(jnp.logical_and(c == nc - 1, i == max(nt - 3, 0)))
            def _():
                for k, landed in enumerate(arrivals):
                    landed.wait_recv()
                    if forwards:
                        forwards[k].start()

            @pl.when(jnp.logical_and(c == nc - 1, i == nt - 1))
            def _():
                for cp in forwarded:
                    cp.wait_recv()
                for cp in sends + forwards:
                    cp.wait_send()

    outs = pl.pallas_call(body, name=name, grid=(nc, nt), in_specs=in_specs + [ANY] * ng,
                          out_specs=out_specs + [ANY] * ng, out_shape=out_shape + list(hosted_shapes),
                          scratch_shapes=list(hosted_scratch), compiler_params=_params())(*args, *hosted_src)
    if ng:
        return outs[:n_or], outs[n_or:n_out], hosted_post(outs[n_out:])
    return outs[:n_or], outs[n_or:]


def _row_ids(i, tm):
    return i * tm + lax.broadcasted_iota(jnp.int32, (tm, 1), 0)


SUBLANES = 8


def _halo_before(arr, tm):
    return (arr, SUBLANES, lambda i: jnp.maximum(i * (tm // SUBLANES) - 1, 0))


def _halo_after(arr, tm):
    last = arr.shape[0] // SUBLANES - 1
    return (arr, SUBLANES, lambda i: jnp.minimum((i + 1) * (tm // SUBLANES), last))


def _pick_row(block8, row):
    rows = lax.broadcasted_iota(jnp.int32, block8.shape, 0)
    return jnp.sum(jnp.where(rows == row, block8, 0.0), axis=0, keepdims=True)


def _shift_down(x, before8, i):
    rows = lax.broadcasted_iota(jnp.int32, x.shape, 0)
    top = _pick_row(before8, SUBLANES - 1) * (i > 0).astype(F32)
    return jnp.where(rows == 0, top, pltpu.roll(x, 1, 0))


def _shift_up(x, after8, i, nt):
    rows = lax.broadcasted_iota(jnp.int32, x.shape, 0)
    bottom = _pick_row(after8, 0) * (i < nt - 1).astype(F32)
    return jnp.where(rows == x.shape[0] - 1, bottom, pltpu.roll(x, x.shape[0] - 1, 0))


LORA_DECAY, LORA_AAA, LORA_GATE = 64, 64, 128
PRE_TAPS = (D_MODEL, D_MODEL, D_MODEL, LORA_DECAY, D_MODEL, LORA_AAA, D_MODEL, LORA_GATE, D_MODEL)


def rwkv_pre(e, et, ws, taps, h, hp, mu_r, mu_w, mu_k, mu_v, mu_a, mu_g, w0, a0, k_k, k_a):
    w_r, w_k, w_v, w1, w2, a1, a2, g1, g2 = ws
    xs = []
    xx = hp - h
    r = tmm(h + xx * mu_r, w_r, taps, xs)
    k = tmm(h + xx * mu_k, w_k, taps, xs)
    v = tmm(h + xx * mu_v, w_v, taps, xs)
    wraw = -_softplus(-(w0 + tmm(jnp.tanh(tmm(h + xx * mu_w, w1, taps, xs)), w2, taps, xs))) - 0.5
    lw = -jnp.exp(wraw)
    a = _sigmoid(a0 + tmm(tmm(h + xx * mu_a, a1, taps, xs), a2, taps, xs))
    g = tmm(_sigmoid(tmm(h + xx * mu_g, g1, taps, xs)), g2, taps, xs)
    kk = k * k_k
    ss = hsum(kk * kk, e, et)
    pos = ss > 0.0
    nrm = jnp.where(pos, jnp.sqrt(jnp.where(pos, ss, 1.0)), 0.0)
    kk = kk * hbc(1.0 / jnp.maximum(nrm, 1e-12), e, et)
    k2 = k * (1.0 + (a - 1.0) * k_a)
    return (r, lw, k2, v, -kk, kk * a, g), xs


def rwkv_post(e, et, w_o, taps, y, r, k2, v, g, h0, gn_w, gn_b, rk, lg, lb):
    xs = []
    inv_n = 1.0 / HEAD_DIM
    yc = y - hbc(hsum(y, e, et) * inv_n, e, et)
    yv = hsum(yc * yc, e, et) * inv_n
    yn = yc * hbc(lax.rsqrt(yv + GN_EPS), e, et) * gn_w + gn_b
    bonus = hbc(hsum(r * k2 * rk, e, et), e, et) * v
    mix = tmm((yn + bonus) * g, w_o, taps, xs)
    return _layer_norm(ALPHA * h0 + mix, lg, lb), xs


@jax.custom_vjp
def sq_relu(x):
    r = jnp.maximum(x, 0.0)
    return r * r


sq_relu.defvjp(lambda x: (sq_relu(x), x), lambda x, g: (g * (2.0 * jnp.maximum(x, 0.0)),))


def _rot_half(t):
    n = t.shape[-1]
    lane = lax.broadcasted_iota(jnp.int32, t.shape, t.ndim - 1)
    lo = (lane % HEAD_DIM) < (HEAD_DIM // 2)
    return jnp.where(lo, -pltpu.roll(t, n - HEAD_DIM // 2, t.ndim - 1), pltpu.roll(t, HEAD_DIM // 2, t.ndim - 1))


@jax.custom_vjp
def rot_half(t):
    return _rot_half(t)


rot_half.defvjp(lambda t: (_rot_half(t), None), lambda _, g: (-_rot_half(g),))


def _tile_lanes(t, width):
    return jnp.concatenate([t] * (width // t.shape[-1]), axis=-1)


def qkv_proj(cos, sin, wq, wk, wv, taps, h):
    xs = []
    q = tmm(h, wq, taps, xs)
    k = tmm(h, wk, taps, xs)
    v = tmm(h, wv, taps, xs)
    cq, sq = _tile_lanes(cos, D_MODEL), _tile_lanes(sin, D_MODEL)
    ck, sk = _tile_lanes(cos, KV_DIM), _tile_lanes(sin, KV_DIM)
    return (q * cq + rot_half(q) * sq, k * ck + rot_half(k) * sk, v), xs


def attn_out(w_o, taps, o, h, lg, lb):
    xs = []
    return _layer_norm(ALPHA * h + tmm(o, w_o, taps, xs), lg, lb), xs


def _scan_consts():
    t = SCAN_T
    tri = np.tril(np.ones((t, t), np.float32))
    rows = np.arange(2 * t)
    same = (rows[:, None] // t) == (rows[None, :] // t)
    strict = same & ((rows[None, :] % t) < (rows[:, None] % t))
    incl = same & ((rows[None, :] % t) <= (rows[:, None] % t))
    lane = np.arange(PAIR)
    masks = np.zeros((8, PAIR), np.float32)
    masks[0] = (lane // HEAD_DIM) == 0
    masks[1] = (lane // HEAD_DIM) == 1
    return (jnp.asarray(tri, BF16), jnp.asarray(strict.astype(np.float32)), jnp.asarray(incl.astype(np.float32)),
            jnp.asarray(masks), jnp.asarray(np.eye(2 * t, dtype=np.float32)))


def _scan_dot(a, b, ca, cb):
    return _dot(a, b, ca, cb)


@functools.partial(jax.custom_vjp, nondiff_argnums=(2, 3))
def _dotf(a, b, ca, cb):
    return _scan_dot(a, b, ca, cb)


def _dotf_bwd(ca, cb, res, g):
    a, b = res
    if ca == 1:
        da = _scan_dot(g, b, 1, 1 - cb)
    else:
        da = _scan_dot(b, g, 1 - cb, 1)
    if cb == 0:
        db = _scan_dot(a, g, 1 - ca, 0)
    else:
        db = _scan_dot(g, a, 0, 1 - ca)
    return da, db


_dotf.defvjp(lambda a, b, ca, cb: (_scan_dot(a, b, ca, cb), (a, b)), _dotf_bwd)


def _tri_dot(tri, x, ct):
    acc = None
    for piece in _split3(x):
        t = lax.dot_general(tri, piece, (((ct,), (0,)), ((), ())), preferred_element_type=F32)
        acc = t if acc is None else acc + t
    return acc


@jax.custom_vjp
def _cumsum_rows(tri, x):
    return _tri_dot(tri, x, 1)


_cumsum_rows.defvjp(lambda tri, x: (_tri_dot(tri, x, 1), tri),
                    lambda tri, g: (jnp.zeros_like(tri), _tri_dot(tri, g, 0)))


@jax.custom_vjp
def _unstack2(x):
    t = x.shape[0] // 2
    return x[:t] + x[t:]


_unstack2.defvjp(lambda x: (_unstack2(x), None), lambda _, g: (jnp.concatenate([g, g], axis=0),))


@jax.custom_vjp
def _last_row(x):
    return x[x.shape[0] - 1:, :]


def _last_row_bwd(_, g):
    rows = lax.broadcasted_iota(jnp.int32, (SCAN_T, g.shape[1]), 0)
    return (jnp.where(rows == SCAN_T - 1, jnp.broadcast_to(g, (SCAN_T, g.shape[1])), 0.0),)


_last_row.defvjp(lambda x: (_last_row(x), None), _last_row_bwd)


@jax.custom_vjp
def _halves(x):
    n = x.shape[0] // 2
    return x[:n], x[n:]


_halves.defvjp(lambda x: (_halves(x), None), lambda _, g: (jnp.concatenate(list(g), axis=0),))


@jax.custom_vjp
def _quads(x):
    n, m = x.shape[0] // 2, x.shape[1] // 2
    return x[:n, :m], x[:n, m:], x[n:, :m], x[n:, m:]


_quads.defvjp(lambda x: (_quads(x), None),
              lambda _, g: (jnp.concatenate([jnp.concatenate([g[0], g[1]], axis=1),
                                             jnp.concatenate([g[2], g[3]], axis=1)], axis=0),))


@jax.custom_vjp
def _solve_saved(n, rhs, minv, u):
    return u


def _solve_saved_bwd(res, du):
    minv, u = res
    drhs = _dotf(minv, du, 0, 0)
    return _dotf(drhs, u, 1, 1), drhs, jnp.zeros_like(minv), jnp.zeros_like(u)


_solve_saved.defvjp(lambda n, rhs, minv, u: (u, (minv, u)), _solve_saved_bwd)


def scan_chunk(tri, strict, incl, m0, m1, eye, r, lw, k, v, a, b, s0, saved=None):
    lower = strict > 0
    lower_incl = incl > 0

    def stack(x):
        return jnp.concatenate([x * m0, x * m1], axis=0)

    def dots(xs, ys, ca, cb, mask=None):
        out = [_dotf(x, y, ca, cb) for x, y in zip(xs, ys)]
        return out if mask is None else [jnp.where(mask, o, 0.0) for o in out]

    cl = [_cumsum_rows(tri, x) for x in lw]
    gam = [jnp.exp(c) for c in cl]
    ginv = [jnp.exp(-c) for c in cl]
    ar_s = [jnp.concatenate([stack(x * jnp.exp(c - w)), stack(y * g)], axis=0)
            for x, c, w, y, g in zip(a, cl, lw, r, gam)]
    bk_s = [jnp.concatenate([stack(x * g), stack(y * g)], axis=0) for x, y, g in zip(b, k, ginv)]
    v_s = [stack(x) for x in v]
    quads = [_quads(x) for x in dots(ar_s, bk_s, 1, 1)]
    n_ab = [jnp.where(lower, q[0], 0.0) for q in quads]
    n_ak = [jnp.where(lower, q[1], 0.0) for q in quads]
    r_ab = [jnp.where(lower_incl, q[2], 0.0) for q in quads]
    r_ak = [jnp.where(lower_incl, q[3], 0.0) for q in quads]
    from_state = [_halves(x) for x in dots(ar_s, s0, 1, 1)]
    rhs = [x[0] + y for x, y in zip(from_state, dots(n_ak, v_s, 1, 0))]
    if saved is None:
        minv = [eye + n for n in n_ab]
        p = n_ab
        for _ in range(5):
            p = dots(p, p, 1, 0)
            minv = [m + mp for m, mp in zip(minv, dots(minv, p, 1, 0))]
        u_s = dots(minv, rhs, 1, 0)
    else:
        minv = saved[0]
        u_s = [_solve_saved(n, x, m, u) for n, x, m, u in zip(n_ab, rhs, *saved)]
    uv_s = [jnp.concatenate([x, y], axis=0) for x, y in zip(u_s, v_s)]
    r_uv = [jnp.concatenate([x, y], axis=1) for x, y in zip(r_ab, r_ak)]
    y = [_unstack2(x[1] + z) for x, z in zip(from_state, dots(r_uv, uv_s, 1, 0))]
    g_end = [_last_row(g) for g in gam]
    s1 = [s * g + x for s, g, x in zip(s0, g_end, dots(uv_s, [x * g for x, g in zip(bk_s, g_end)], 0, 0))]
    return y, s1, (minv, u_s)


SCAN_PAIRS = 8


def _scan_specs(consts, order):
    row = pl.BlockSpec((SCAN_T, PAIR * SCAN_PAIRS), lambda p, c: (order(c), p))
    state = pl.BlockSpec((None, SCAN_PAIRS, PAIR, PAIR), lambda p, c: (order(c), p, 0, 0))
    return row, state, [pl.BlockSpec(x.shape, _zero_map(x.ndim)) for x in consts]


def _pair_lanes(q):
    return slice(q * PAIR, (q + 1) * PAIR)


def scan_fwd(r, lw, k, v, a, b, shards=()):
    lp = r.shape[0]
    nch = lp // SCAN_T
    npair = D_MODEL // PAIR
    ng = len(shards)
    consts = _scan_consts()
    row, state, cspecs = _scan_specs(consts, lambda c: c)

    def body(tri, strict, incl, masks, eye, r_ref, lw_ref, k_ref, v_ref, a_ref, b_ref, *rest):
        src, (y_ref, s_ref, minv_ref, u_ref), dst = rest[:ng], rest[ng:ng + 4], rest[ng + 4:2 * ng + 4]
        carry = rest[2 * ng + 4]
        first = jnp.logical_and(pl.program_id(0) == 0, pl.program_id(1) == 0)
        last = jnp.logical_and(pl.program_id(0) == npair // SCAN_PAIRS - 1, pl.program_id(1) == nch - 1)
        if ng:
            sends, arrivals, forwards, forwarded = gather_copies(src, dst, *rest[2 * ng + 5:])

            @pl.when(first)
            def _():
                for cp in sends:
                    cp.start()

            @pl.when(jnp.logical_and(pl.program_id(0) == npair // SCAN_PAIRS - 1, pl.program_id(1) == nch * 3 // 4))
            def _():
                for landed, onward in zip(arrivals, forwards):
                    landed.wait_recv()
                    onward.start()

        @pl.when(pl.program_id(1) == 0)
        def _():
            carry[...] = jnp.zeros_like(carry)

        pairs = range(SCAN_PAIRS)
        s0 = [carry[q] for q in pairs]
        rows = [[ref[:, _pair_lanes(q)] for q in pairs] for ref in (r_ref, lw_ref, k_ref, v_ref, a_ref, b_ref)]
        y, s1, (minv, u) = scan_chunk(tri[...], strict[...], incl[...], masks[0:1, :], masks[1:2, :], eye[...],
                                      *rows, s0)
        for q in pairs:
            s_ref[q] = s0[q]
            minv_ref[q] = minv[q]
            u_ref[q] = u[q]
            y_ref[:, _pair_lanes(q)] = y[q]
            carry[q] = s1[q]

        if ng:
            @pl.when(last)
            def _():
                for cp in forwarded:
                    cp.wait_recv()
                for cp in sends + forwards:
                    cp.wait_send()

    mats = jax.ShapeDtypeStruct((nch, npair, PAIR, PAIR), F32)
    out = pl.pallas_call(
        body, name="rwkv_scan_fwd", grid=(npair // SCAN_PAIRS, nch), in_specs=cspecs + [row] * 6 + [ANY] * ng,
        out_specs=[row, state, state, state] + [ANY] * ng,
        out_shape=[jax.ShapeDtypeStruct((lp, D_MODEL), F32), mats, mats, mats] + gathered_shapes(shards),
        scratch_shapes=[pltpu.VMEM((SCAN_PAIRS, PAIR, PAIR), F32)] + (gather_scratch(ng) if ng else []),
        compiler_params=_params(),
    )(*consts, r, lw, k, v, a, b, *shards)
    return out[:4], fill_own(out[4:], shards)


def scan_bwd(r, lw, k, v, a, b, saved, dy, direct_grads, parts=()):
    lp = r.shape[0]
    nch = lp // SCAN_T
    npair = D_MODEL // PAIR
    consts = _scan_consts()
    row, state, cspecs = _scan_specs(consts, lambda c: nch - 1 - c)

    ng = len(parts)

    def body(tri, strict, incl, masks, eye, r_ref, lw_ref, k_ref, v_ref, a_ref, b_ref, s_ref, minv_ref, u_ref,
             dy_ref, dr_in, dk_in, dv_in, *rest):
        src, (dr_ref, dlw_ref, dk_ref, dv_ref, da_ref, db_ref), dst = rest[:ng], rest[ng:ng + 6], rest[ng + 6:2 * ng + 6]
        carry = rest[2 * ng + 6]
        first = jnp.logical_and(pl.program_id(0) == 0, pl.program_id(1) == 0)
        last = jnp.logical_and(pl.program_id(0) == npair // SCAN_PAIRS - 1, pl.program_id(1) == nch - 1)
        if ng:
            sends, arrivals = chip_exchange_copies(src, dst, *rest[2 * ng + 7:])

            @pl.when(first)
            def _():
                for cp in sends:
                    cp.start()

        @pl.when(pl.program_id(1) == 0)
        def _():
            carry[...] = jnp.zeros_like(carry)

        pairs = range(SCAN_PAIRS)
        kept = ([minv_ref[q] for q in pairs], [u_ref[q] for q in pairs])

        def fn(*args):
            y, s1, _ = scan_chunk(tri[...], strict[...], incl[...], masks[0:1, :], masks[1:2, :], eye[...], *args,
                                  saved=kept)
            return y, s1

        rows = [[ref[:, _pair_lanes(q)] for q in pairs] for ref in (r_ref, lw_ref, k_ref, v_ref, a_ref, b_ref)]
        _, vjp = jax.vjp(fn, *rows, [s_ref[q] for q in pairs])
        grads = vjp(([dy_ref[:, _pair_lanes(q)] for q in pairs], [carry[q] for q in pairs]))
        direct = (dr_in, None, dk_in, dv_in, None, None)
        for q in pairs:
            ln = _pair_lanes(q)
            for ref, g, extra in zip((dr_ref, dlw_ref, dk_ref, dv_ref, da_ref, db_ref), grads[:6], direct):
                ref[:, ln] = g[q] if extra is None else g[q] + extra[:, ln]
            carry[q] = grads[6][q]

        if ng:
            @pl.when(last)
            def _():
                for cp in arrivals:
                    cp.wait_recv()
                for cp in sends:
                    cp.wait_send()

    out = pl.pallas_call(
        body, name="rwkv_scan_bwd", grid=(npair // SCAN_PAIRS, nch),
        in_specs=cspecs + [row] * 6 + [state] * 3 + [row] * 4 + [ANY] * ng, out_specs=[row] * 6 + [ANY] * ng,
        out_shape=[jax.ShapeDtypeStruct((lp, D_MODEL), F32)] * 6 + [jax.ShapeDtypeStruct(p.shape, p.dtype) for p in parts],
        scratch_shapes=[pltpu.VMEM((SCAN_PAIRS, PAIR, PAIR), F32)] + (_sem_scratch(ng * len(XY_FLIPS)) if ng else []),
        compiler_params=_params(),
    )(*consts, r, lw, k, v, a, b, *saved, dy, *direct_grads, *parts)
    return out[:6], out[6:]


def _spread_matrices():
    rep = np.zeros((N_HEADS_KV, KV_DIM, KVW), np.float32)
    for h in range(N_HEADS_KV):
        for g in range(GROUP):
            rep[h, h * HEAD_DIM + np.arange(HEAD_DIM), g * HEAD_DIM + np.arange(HEAD_DIM)] = 1.0
    return jnp.asarray(rep, BF16)


KV_HEADS = range(N_HEADS_KV)


def _attn_operands(q_ref, kp, kc, vp, vc, rep_ref):
    lane = lax.broadcasted_iota(jnp.int32, (1, KVW), 1)
    gmask = [(lane // HEAD_DIM == g).astype(F32) for g in range(GROUP)]
    kk = jnp.concatenate([kp, kc], axis=0)
    vv = jnp.concatenate([vp, vc], axis=0)
    qs = [q_ref[:, h * KVW:(h + 1) * KVW] for h in KV_HEADS]
    q_s = [jnp.concatenate([q * gmask[g] for g in range(GROUP)], axis=0) for q in qs]
    keys = [_dot(kk, rep_ref[h], 1, 0) for h in KV_HEADS]
    vals = [_dot(vv, rep_ref[h], 1, 0) for h in KV_HEADS]
    return gmask, q_s, keys, vals


FWD_BLOCKS = 3


def _attn_probs(blocks, q_s, keys, sink_ref):
    qi = lax.broadcasted_iota(jnp.int32, (GROUP * BLOCK, 2 * BLOCK), 0) % BLOCK
    kj = lax.broadcasted_iota(jnp.int32, (GROUP * BLOCK, 2 * BLOCK), 1)
    rel = BLOCK + qi - kj
    window = (rel >= 0) & (rel < BLOCK)
    valid = [window & ((n - 1) * BLOCK + kj >= PAD_FRONT) for n in blocks for _ in KV_HEADS]
    s = [jnp.where(ok, _dot(x, y, 1, 1) * (HEAD_DIM ** -0.5), -1e30) for ok, x, y in zip(valid, q_s, keys)]
    sink_col = [jnp.concatenate([jnp.broadcast_to(sink_ref[h, g:g + 1, 0:1], (BLOCK, 1)) for g in range(GROUP)],
                                axis=0) for _ in blocks for h in KV_HEADS]
    m = [jnp.maximum(jnp.max(x, axis=-1, keepdims=True), c) for x, c in zip(s, sink_col)]
    ex = [jnp.exp(x - y) for x, y in zip(s, m)]
    ex_sink = [jnp.exp(c - y) for c, y in zip(sink_col, m)]
    inv = [1.0 / (jnp.sum(x, axis=-1, keepdims=True) + c) for x, c in zip(ex, ex_sink)]
    return [x * y for x, y in zip(ex, inv)], [x * y for x, y in zip(ex_sink, inv)]


def _unstack_groups(x_s, gmask):
    out = None
    for g in range(GROUP):
        t = x_s[g * BLOCK:(g + 1) * BLOCK] * gmask[g]
        out = t if out is None else out + t
    return out


def _attn_specs():
    qspec = pl.BlockSpec((BLOCK, D_MODEL), lambda n: (n, 0))
    cur = pl.BlockSpec((BLOCK, KV_DIM), lambda n: (n, 0))
    prev = pl.BlockSpec((BLOCK, KV_DIM), lambda n: (jnp.maximum(n - 1, 0), 0))
    rep = pl.BlockSpec((N_HEADS_KV, KV_DIM, KVW), lambda n: (0, 0, 0))
    sink = pl.BlockSpec((N_HEADS_KV, 8, PAIR), lambda n: (0, 0, 0))
    return qspec, cur, prev, rep, sink


def _attn_params():
    return pltpu.CompilerParams(dimension_semantics=("arbitrary",), vmem_limit_bytes=VMEM_LIMIT)


def _prob_specs():
    rows = GROUP * BLOCK
    return (pl.BlockSpec((None, N_HEADS_KV, rows, 2 * BLOCK), lambda n: (n, 0, 0, 0)),
            pl.BlockSpec((None, rows, PAIR), lambda n: (n, 0, 0)))


SINK_LANES = PAIR // N_HEADS_KV


def attn_fwd(q, k, v, sinks_b):
    lp = q.shape[0]
    nb = lp // BLOCK
    nbk = FWD_BLOCKS if nb % FWD_BLOCKS == 0 else 1
    _, _, _, rep, sink = _attn_specs()
    rows = GROUP * BLOCK
    qspec = pl.BlockSpec((nbk * BLOCK, D_MODEL), lambda n: (n, 0))
    cur = pl.BlockSpec((nbk * BLOCK, KV_DIM), lambda n: (n, 0))
    prev = pl.BlockSpec((BLOCK, KV_DIM), lambda n: (jnp.maximum(nbk * n - 1, 0), 0))
    pspecs = (pl.BlockSpec((nbk, N_HEADS_KV, rows, 2 * BLOCK), lambda n: (n, 0, 0, 0)),
              pl.BlockSpec((nbk, rows, PAIR), lambda n: (n, 0, 0)))

    def body(q_ref, kp_ref, kc_ref, vp_ref, vc_ref, rep_ref, sink_ref, o_ref, p_ref, ps_ref):
        first = pl.program_id(0) * nbk
        kc, vc = kc_ref[...], vc_ref[...]
        q_s, keys, vals = [], [], []
        for j in range(nbk):
            here = slice(j * BLOCK, (j + 1) * BLOCK)
            before = slice((j - 1) * BLOCK, j * BLOCK)
            gmask, qj, kj, vj = _attn_operands(q_ref.at[pl.ds(j * BLOCK, BLOCK), :],
                                               kp_ref[...] if j == 0 else kc[before], kc[here],
                                               vp_ref[...] if j == 0 else vc[before], vc[here], rep_ref)
            q_s, keys, vals = q_s + qj, keys + kj, vals + vj
        p, p_sink = _attn_probs([first + j for j in range(nbk)], q_s, keys, sink_ref)
        o = [_dot(x, y, 1, 0) for x, y in zip(p, vals)]
        head = lax.broadcasted_iota(jnp.int32, (rows, PAIR), 1) // SINK_LANES
        for j in range(nbk):
            for h in KV_HEADS:
                e = j * N_HEADS_KV + h
                o_ref[j * BLOCK:(j + 1) * BLOCK, h * KVW:(h + 1) * KVW] = _unstack_groups(o[e], gmask)
                p_ref[j, h] = p[e].astype(BF16)
            packed = p_sink[j * N_HEADS_KV + N_HEADS_KV - 1]
            for h in reversed(range(N_HEADS_KV - 1)):
                packed = jnp.where(head == h, p_sink[j * N_HEADS_KV + h], packed)
            ps_ref[j] = packed

    return pl.pallas_call(
        body, name="swa_fwd", grid=(nb // nbk,), in_specs=[qspec, prev, cur, prev, cur, rep, sink],
        out_specs=[qspec, *pspecs],
        out_shape=[jax.ShapeDtypeStruct((lp, D_MODEL), F32),
                   jax.ShapeDtypeStruct((nb, N_HEADS_KV, GROUP * BLOCK, 2 * BLOCK), BF16),
                   jax.ShapeDtypeStruct((nb, GROUP * BLOCK, PAIR), F32)],
        compiler_params=_attn_params(),
    )(q, k, k, v, v, _spread_matrices(), sinks_b)


def attn_bwd(q, k, v, probs, do):
    lp = q.shape[0]
    qspec, cur, prev, rep, sink = _attn_specs()

    def body(q_ref, kp_ref, kc_ref, vp_ref, vc_ref, rep_ref, p_ref, ps_ref, do_ref, dq_ref, dkc_ref, dkp_ref, dvc_ref,
             dvp_ref, dsink_ref):
        n = pl.program_id(0)
        gmask, q_s, keys, vals = _attn_operands(q_ref, kp_ref[...], kc_ref[...], vp_ref[...], vc_ref[...], rep_ref)
        p = [p_ref[h].astype(F32) for h in KV_HEADS]
        lane = lax.broadcasted_iota(jnp.int32, (GROUP * BLOCK, PAIR), 1)
        p_sink = [jnp.sum(jnp.where(lane == h * SINK_LANES, ps_ref[...], 0.0), axis=-1, keepdims=True)
                  for h in KV_HEADS]
        do_s = [jnp.concatenate([do_ref[:, h * KVW:(h + 1) * KVW] * gmask[g] for g in range(GROUP)], axis=0)
                for h in KV_HEADS]
        dp = [_dot(x, y, 1, 1) for x, y in zip(do_s, vals)]
        delta = [jnp.sum(x * y, axis=-1, keepdims=True) for x, y in zip(p, dp)]
        ds = [x * (y - z) * (HEAD_DIM ** -0.5) for x, y, z in zip(p, dp, delta)]
        dq = [_dot(x, y, 1, 0) for x, y in zip(ds, keys)]
        dkeys_s = [_dot(x, y, 0, 0) for x, y in zip(ds, q_s)]
        dvals_s = [_dot(x, y, 0, 0) for x, y in zip(p, do_s)]
        dkeys = [_exact_dot(x, rep_ref[h], cb=1) for h, x in enumerate(dkeys_s)]
        dvals = [_exact_dot(x, rep_ref[h], cb=1) for h, x in enumerate(dvals_s)]
        dk_all = (dkeys[0] + dkeys[1]) + (dkeys[2] + dkeys[3])
        dv_all = (dvals[0] + dvals[1]) + (dvals[2] + dvals[3])
        dkp_ref[...] = dk_all[:BLOCK]
        dkc_ref[...] = dk_all[BLOCK:]
        dvp_ref[...] = dv_all[:BLOCK]
        dvc_ref[...] = dv_all[BLOCK:]
        dsinks = []
        for h in KV_HEADS:
            dq_ref[:, h * KVW:(h + 1) * KVW] = _unstack_groups(dq[h], gmask)
            dsk = -(p_sink[h] * delta[h])
            rows = [jnp.broadcast_to(jnp.sum(dsk[g * BLOCK:(g + 1) * BLOCK], axis=0, keepdims=True), (1, PAIR))
                    for g in range(GROUP)]
            dsinks.append(jnp.concatenate(rows + [jnp.zeros((8 - GROUP, PAIR), F32)], axis=0))

        @pl.when(n == 0)
        def _():
            for h in KV_HEADS:
                dsink_ref[h] = dsinks[h]

        @pl.when(n > 0)
        def _():
            for h in KV_HEADS:
                dsink_ref[h] += dsinks[h]

    kv = jax.ShapeDtypeStruct((lp, KV_DIM), F32)
    return pl.pallas_call(
        body, name="swa_bwd", grid=(lp // BLOCK,), in_specs=[qspec, prev, cur, prev, cur, rep, *_prob_specs(), qspec],
        out_specs=[qspec, cur, cur, cur, cur, sink],
        out_shape=[jax.ShapeDtypeStruct((lp, D_MODEL), F32), kv, kv, kv, kv,
                   jax.ShapeDtypeStruct((N_HEADS_KV, 8, PAIR), F32)],
        compiler_params=_attn_params(),
    )(q, k, k, v, v, _spread_matrices(), *probs, do)


def _pick_tm(lp, want):
    for tm in (384, 192, 128, 64):
        if tm <= want and lp % tm == 0:
            return tm
    raise ValueError(lp)


def _acc(shape):
    return (tuple(shape), F32)


def _ff_one(w):
    return (w, (None, D_MODEL, D_MODEL), lambda c, i: (c, 0, 0))


def _mlp_layer_fwd(name, h, wup, wdown, lg, lb, tm):
    def fn(c, i, h, wup, wdown, lg, lb):
        out, pre = None, []
        for s in range(N_FF_CHUNK):
            u = mm(h, wup[s])
            pre.append(u.astype(BF16))
            t = mm(sq_relu(u), wdown[s])
            out = t if out is None else out + t
        z = ALPHA * h + out
        return (_layer_norm(z, lg, lb), z, jnp.stack(pre)), ()

    (h_out, z, pre), _ = rowwise(name, fn, [h], [wup, wdown, lg, lb],
                                 [(D_MODEL, F32), (D_MODEL, F32), (D_MODEL, BF16, False, N_FF_CHUNK)], [], tm)
    return h_out, z, pre


MLP_BWD_TILE = 528


def _mlp_layer_bwd(name, h_in, z, pre, dh_parts, wup, wdown, lg, lb, tm):
    n_parts = len(dh_parts)

    def fn_ln(c, i, z, *rest):
        dh = rest[0]
        for extra in rest[1:n_parts]:
            dh = dh + extra
        _, vjp = jax.vjp(_layer_norm, z, rest[n_parts], rest[n_parts + 1])
        dz, dlg, dlb = vjp(dh)
        return (dz,), (dlg, dlb)

    (dz,), (dlg, dlb) = rowwise(name + "_ln", fn_ln, [z] + list(dh_parts), [lg, lb], [(D_MODEL, F32)],
                                [_acc((1, D_MODEL)), _acc((1, D_MODEL))], tm)

    def fn_mlp(c, i, h, dz, wup, wdown, u):
        r = jnp.maximum(u.astype(F32), 0.0)
        du = _dot(dz, wdown, 1, 1) * (2.0 * r)
        return (_dot(du, wup, 1, 1),), (_dot(h, du, 0, 0), _dot(r * r, dz, 0, 0))

    aspec = ((N_FF_CHUNK, D_MODEL, D_MODEL), F32, (None, D_MODEL, D_MODEL), lambda c, i: (c, 0, 0))
    lp = h_in.shape[0]
    tile = MLP_BWD_TILE if lp % MLP_BWD_TILE == 0 else tm
    pre_chunk = (pre, (None, tile, D_MODEL), lambda c, i: (c, i, 0))
    (dx,), (dwup, dwdown) = rowwise(name + "_mm", fn_mlp, [h_in, dz], [_ff_one(wup), _ff_one(wdown), pre_chunk],
                                    [(D_MODEL, F32, True)], [aspec, aspec], tile, nc=N_FF_CHUNK)
    return dz, dx, dwup, dwdown, dlg, dlb


def _sum_parts(dz, dx):
    out = ALPHA * dz
    for s in range(N_FF_CHUNK):
        out = out + dx[s]
    return out


def local_step(x, loss_target, p, late=None, early_hook=None):
    seq = x.shape[0]
    lp = TOK0 + seq
    tm = _pick_tm(lp, 384)
    tms = _pick_tm(lp, 192)
    e, et = _head_matrices()
    h0 = jnp.concatenate([jnp.zeros((PAD_FRONT, D_MODEL), F32), p["meta_tokens"], x], axis=0)
    pos = jnp.maximum(jnp.arange(lp, dtype=F32) - PAD_FRONT, 0.0)
    inv_freq = 1.0 / (ROPE_THETA ** (jnp.arange(0, HEAD_DIM, 2, dtype=F32) / HEAD_DIM))
    ang = pos[:, None] * inv_freq[None, :]
    cos = jnp.tile(jnp.cos(ang), (1, PAIR // (HEAD_DIM // 2)))
    sin = jnp.tile(jnp.sin(ang), (1, PAIR // (HEAD_DIM // 2)))

    pre_vec = [p["a_mu"][j:j + 1] for j in range(6)] + [p["a_w0"], p["a_a0"], p["a_k_k"], p["a_k_a"]]
    pre_w = [p["a_w_r"], p["a_w_k"], p["a_w_v"], p["a_w1"], p["a_w2"], p["a_a1"], p["a_a2"], p["a_g1"], p["a_g2"]]
    n_vec = len(pre_vec)

    def fn_pre(c, i, h, before, e, et, *ws):
        return rwkv_pre(e, et, ws[n_vec:], None, h, _shift_down(h, before, i), *ws[:n_vec])[0], ()

    (r, lw, k2, v, an, bn, g), _, *pre_gathered = rowwise(
        "rwkv_pre", fn_pre, [h0, _halo_before(h0, tms)], [e, et] + pre_vec + pre_w, [(D_MODEL, F32)] * 7, [], tms,
        hosted=hosted_gather(late[0][0]) if late else None)
    (y, *scan_saved), scan_gathered = scan_fwd(r, lw, k2, v, an, bn, late[1][0] if late else ())
    if late:
        p = {**p, **late[0][1](pre_gathered[0]), **late[1][1](scan_gathered)}

    post_c = [p["a_w_o"], p["a_gn_w"], p["a_gn_b"], p["a_r_k"], p["ln_g00"], p["ln_b00"]]

    def fn_post(c, i, y, r, k2, v, g, h0, e, et, w_o, *vecs):
        return (rwkv_post(e, et, w_o, None, y, r, k2, v, g, h0, *vecs)[0],), ()

    (h1,), _ = rowwise("rwkv_post", fn_post, [y, r, k2, v, g, h0], [e, et] + post_c, [(D_MODEL, F32)], [], tm)
    h2, z2, pre2 = _mlp_layer_fwd("mlp0_fwd", h1, p["mlp_up0"], p["mlp_down0"], p["ln_g01"], p["ln_b01"], tm)

    qkv_w = [p["b_w_q"], p["kv_w_k"], p["kv_w_v"]]

    def fn_qkv(c, i, h, cos, sin, wq, wk, wv):
        return qkv_proj(cos, sin, wq, wk, wv, None, h)[0], ()

    (q, k, vv), _ = rowwise("qkv_proj", fn_qkv, [h2, cos, sin], qkv_w,
                            [(D_MODEL, F32), (KV_DIM, F32), (KV_DIM, F32)], [], tm)
    sinks_b = jnp.broadcast_to(p["b_sinks"].reshape(N_HEADS_KV, GROUP, 1), (N_HEADS_KV, GROUP, PAIR))
    sinks_b = jnp.concatenate([sinks_b, jnp.zeros((N_HEADS_KV, 8 - GROUP, PAIR), F32)], axis=1)
    o, *attn_probs = attn_fwd(q, k, vv, sinks_b)

    ao_c = [p["b_w_o"], p["ln_g10"], p["ln_b10"]]

    def fn_ao(c, i, o, h, w_o, lg, lb):
        return (attn_out(w_o, None, o, h, lg, lb)[0],), ()

    (h3,), _ = rowwise("attn_out", fn_ao, [o, h2], ao_c, [(D_MODEL, F32)], [], tm)
    h4, z4, pre4 = _mlp_layer_fwd("mlp1_fwd", h3, p["mlp_up1"], p["mlp_down1"], p["ln_g11"], p["ln_b11"], tm)

    per = tm // TOK0

    def fn_loss(c, i, h4, *tgt_blocks):
        real = (_row_ids(i, tm) >= TOK0).astype(F32)
        err = (h4 - jnp.concatenate(tgt_blocks, axis=0)) * real
        part = 0.5 * jnp.sum(jnp.sum(err * err, axis=-1, keepdims=True), axis=0, keepdims=True) / D_MODEL
        return (err * (1.0 / D_MODEL),), (jnp.broadcast_to(part, (8, PAIR)),)

    tgt_blocks = [(loss_target, TOK0, functools.partial(lambda j, i: jnp.maximum(i * per + j - 1, 0), j))
                  for j in range(per)]
    (dh4,), (loss_acc,) = rowwise("loss", fn_loss, [h4] + tgt_blocks, [], [(D_MODEL, F32)], [_acc((8, PAIR))], tm)
    loss = loss_acc[0, 0]

    grads = {}
    dz4, dx4, grads["mlp_up1"], grads["mlp_down1"], grads["ln_g11"], grads["ln_b11"] = _mlp_layer_bwd(
        "mlp1_bwd", h3, z4, pre4, [dh4], p["mlp_up1"], p["mlp_down1"], p["ln_g11"], p["ln_b11"], tm)

    def fn_ao_b(c, i, dz, dx, o, h, w_o, lg, lb):
        (do, dh, dlg, dlb), (dw_o,) = vjp_taps(functools.partial(attn_out, w_o), [(tm, D_MODEL)], [o, h, lg, lb],
                                               _sum_parts(dz, dx))
        return (do, dh), (dw_o, dlg, dlb)

    (do, dh2_a), (grads["b_w_o"], grads["ln_g10"], grads["ln_b10"]) = rowwise(
        "attn_out_bwd", fn_ao_b, [dz4, dx4, o, h2], ao_c, [(D_MODEL, F32)] * 2,
        [_acc((D_MODEL, D_MODEL)), _acc((1, D_MODEL)), _acc((1, D_MODEL))], tm)

    dq, dkc, dkp, dvc, dvp, dsinks = attn_bwd(q, k, vv, attn_probs, do)
    grads["b_sinks"] = dsinks[:, :GROUP, 0].reshape(1, N_HEADS)
    zblk = jnp.zeros((BLOCK, KV_DIM), F32)
    dkp_s = jnp.concatenate([dkp[BLOCK:], zblk], axis=0)
    dvp_s = jnp.concatenate([dvp[BLOCK:], zblk], axis=0)

    def fn_qkv_b(c, i, h, cos, sin, dq, dkc, dkp, dvc, dvp, wq, wk, wv):
        return vjp_taps(functools.partial(qkv_proj, cos, sin, wq, wk, wv),
                        [(tm, D_MODEL), (tm, KV_DIM), (tm, KV_DIM)], [h], (dq, dkc + dkp, dvc + dvp))

    (dh2_q,), (grads["b_w_q"], grads["kv_w_k"], grads["kv_w_v"]) = rowwise(
        "qkv_proj_bwd", fn_qkv_b, [h2, cos, sin, dq, dkc, dkp_s, dvc, dvp_s], qkv_w, [(D_MODEL, F32)],
        [_acc((D_MODEL, D_MODEL)), _acc((D_MODEL, KV_DIM)), _acc((D_MODEL, KV_DIM))], tm)

    dz2, dx2, grads["mlp_up0"], grads["mlp_down0"], grads["ln_g01"], grads["ln_b01"] = _mlp_layer_bwd(
        "mlp0_bwd", h1, z2, pre2, [dh2_a, dh2_q], p["mlp_up0"], p["mlp_down0"], p["ln_g01"], p["ln_b01"], tm)

    def fn_post_b(c, i, dz, dx, y, r, k2, v, g, h0, e, et, w_o, *vecs):
        out, dws = vjp_taps(functools.partial(rwkv_post, e, et, w_o), [(tms, D_MODEL)],
                            [y, r, k2, v, g, h0] + list(vecs), _sum_parts(dz, dx))
        return out[:6], tuple(dws) + tuple(out[6:])

    early_srcs = early_hook[0](grads) if early_hook else ()
    (dy, dr_c, dk_c, dv_c, dg, dh0_c), post_g, *early_got = rowwise(
        "rwkv_post_bwd", fn_post_b, [dz2, dx2, y, r, k2, v, g, h0], [e, et] + post_c, [(D_MODEL, F32)] * 6,
        [_acc((D_MODEL, D_MODEL))] + [_acc((1, D_MODEL))] * 5, tms,
        hosted=hosted_pair_exchange(early_srcs) if early_hook else None)
    for name, val in zip(["a_w_o", "a_gn_w", "a_gn_b", "a_r_k", "ln_g00", "ln_b00"], post_g):
        grads[name] = val

    (dr, dlw, dk2, dv, dan, dbn), early_from_chips = scan_bwd(
        r, lw, k2, v, an, bn, scan_saved, dy, (dr_c, dk_c, dv_c),
        early_hook[1](early_srcs, early_got[0]) if early_hook else ())

    def fn_pre_b(c, i, h, before, dr, dlw, dk2, dv, dan, dbn, dg, e, et, *ws):
        hp = _shift_down(h, before, i)
        real = (_row_ids(i, tms) >= PAD_FRONT).astype(F32)
        cot = tuple(t * real for t in (dr, dlw, dk2, dv, dan, dbn, dg))
        out, dws = vjp_taps(functools.partial(rwkv_pre, e, et, ws[n_vec:]), [(tms, n) for n in PRE_TAPS],
                            [h, hp] + list(ws[:n_vec]), cot)
        return out[:2], tuple(out[2:]) + tuple(dws)

    (dh0_p, dhp), pre_g = rowwise(
        "rwkv_pre_bwd", fn_pre_b, [h0, _halo_before(h0, tms), dr, dlw, dk2, dv, dan, dbn, dg],
        [e, et] + pre_vec + pre_w, [(D_MODEL, F32)] * 2,
        [_acc((1, D_MODEL))] * n_vec + [_acc(w.shape) for w in pre_w], tms)
    grads["a_mu"] = jnp.concatenate(pre_g[:6], axis=0)
    for name, val in zip(["a_w0", "a_a0", "a_k_k", "a_k_a", "a_w_r", "a_w_k", "a_w_v", "a_w1", "a_w2", "a_a1",
                          "a_a2", "a_g1", "a_g2"], pre_g[6:]):
        grads[name] = val

    def fn_add(c, i, a, b, d, after):
        return (a + b + _shift_up(d, after, i, lp // tm),), ()

    (dh0,), _ = rowwise("grad_h0", fn_add, [dh0_c, dh0_p, dhp, _halo_after(dhp, tm)], [], [(D_MODEL, F32)], [], tm)
    grads["meta_tokens"] = dh0[PAD_FRONT:TOK0]
    return loss, dh0[TOK0:], grads, early_from_chips


ANY = pl.BlockSpec(memory_space=pl.ANY)
XY_FLIPS = ((0, 1), (1, 0), (1, 1))


def _flip(v, bit):
    return 1 - v if bit else v


def _sem_scratch(n):
    return [pltpu.SemaphoreType.DMA((n,)), pltpu.SemaphoreType.DMA((n,))]


def gather_copies(src, dst, ici_send, ici_recv, d2d_send, d2d_recv):
    npeer = len(XY_FLIPS)
    x, y, c = lax.axis_index("x"), lax.axis_index("y"), lax.axis_index("c")

    def half(ref, k, which):
        h = src[k].shape[0] // 2
        start = which * h
        return ref.at[pl.ds(pl.multiple_of(start, 8) if h % 8 == 0 else start, h)]

    def ici(k, j, slot):
        fx, fy = XY_FLIPS[j]
        return pltpu.make_async_remote_copy(
            src_ref=half(src[k], k, c), dst_ref=half(dst[k].at[slot], k, c), send_sem=ici_send.at[k * npeer + j],
            recv_sem=ici_recv.at[k * npeer + j], device_id=(_flip(x, fx), _flip(y, fy), c), device_id_type=MESH)

    def d2d(k, j, which):
        fx, fy = XY_FLIPS[j]
        landed = half(dst[k].at[2 * _flip(x, fx) + _flip(y, fy)], k, which)
        return pltpu.make_async_remote_copy(
            src_ref=landed, dst_ref=landed, send_sem=d2d_send.at[k * npeer + j], recv_sem=d2d_recv.at[k * npeer + j],
            device_id=(x, y, 1 - c), device_id_type=MESH)

    pairs = [(k, j) for k in range(len(src)) for j in range(npeer)]
    return ([ici(k, j, 2 * x + y) for k, j in pairs],
            [ici(k, j, 2 * _flip(x, XY_FLIPS[j][0]) + _flip(y, XY_FLIPS[j][1])) for k, j in pairs],
            [d2d(k, j, c) for k, j in pairs], [d2d(k, j, 1 - c) for k, j in pairs])


def gather_scratch(n):
    return _sem_scratch(n * len(XY_FLIPS)) * 2


def gathered_shapes(shards):
    return [jax.ShapeDtypeStruct((N_SHARD,) + s.shape, s.dtype) for s in shards]


def fill_own(gathered, shards):
    if not shards:
        return []
    slot = 2 * lax.axis_index("x") + lax.axis_index("y")
    return [lax.dynamic_update_index_in_dim(g, s, slot, 0) for g, s in zip(gathered, shards)]


def all_gather_shards(shards):
    n = len(shards)

    def body(*refs):
        sends, arrivals, forwards, forwarded = gather_copies(refs[:n], refs[n:2 * n], *refs[2 * n:])
        for cp in sends:
            cp.start()
        for landed, onward in zip(arrivals, forwards):
            landed.wait_recv()
            onward.start()
        for cp in forwarded:
            cp.wait_recv()
        for cp in sends + forwards:
            cp.wait_send()

    out = pl.pallas_call(body, name="gather_weights", in_specs=[ANY] * n, out_specs=[ANY] * n,
                         out_shape=gathered_shapes(shards), scratch_shapes=gather_scratch(n))(*shards)
    return fill_own(out, shards)


def placement():
    x, y, c = lax.axis_index("x"), lax.axis_index("y"), lax.axis_index("c")
    me = 2 * x + y
    others = [j + (j >= me).astype(jnp.int32) for j in range(N_SHARD - 1)]
    return jnp.stack([c, me] + others).astype(jnp.int32)


def hosted_gather(shards):
    return (gather_copies, list(shards), gathered_shapes(shards), gather_scratch(len(shards)),
            lambda got: fill_own(got, shards))


def pair_exchange_copies(src, got, send_sems, recv_sems):
    x, y, c = lax.axis_index("x"), lax.axis_index("y"), lax.axis_index("c")

    def copy(k):
        half = src[k].shape[1] // 2
        theirs = src[k].at[:, pl.ds(pl.multiple_of((1 - c) * half, 8), half), :]
        return pltpu.make_async_remote_copy(
            src_ref=theirs, dst_ref=got[k], send_sem=send_sems.at[k], recv_sem=recv_sems.at[k],
            device_id=(x, y, 1 - c), device_id_type=MESH)

    sends = [copy(k) for k in range(len(src))]
    return sends, sends, [], []


def _half_shapes(sources):
    return [jax.ShapeDtypeStruct((s.shape[0], s.shape[1] // 2, s.shape[2]), s.dtype) for s in sources]


def hosted_pair_exchange(sources):
    return (pair_exchange_copies, list(sources), _half_shapes(sources), _sem_scratch(len(sources)), list)


def pair_exchange(name, sources):
    n = len(sources)

    def body(*refs):
        sends, arrivals, _, _ = pair_exchange_copies(refs[:n], refs[n:2 * n], *refs[2 * n:])
        for cp in sends:
            cp.start()
        for cp in arrivals:
            cp.wait_recv()
        for cp in sends:
            cp.wait_send()

    halves = _half_shapes(sources)
    return pl.pallas_call(body, name=name, in_specs=[ANY] * n, out_specs=[ANY] * n,
                          out_shape=halves, scratch_shapes=_sem_scratch(n))(*sources)


def chip_exchange(parts):
    n = len(parts)

    def body(*refs):
        sends, arrivals = chip_exchange_copies(refs[:n], refs[n:2 * n], *refs[2 * n:])
        for cp in sends:
            cp.start()
        for cp in arrivals:
            cp.wait_recv()
        for cp in sends:
            cp.wait_send()

    return pl.pallas_call(
        body, name="grads_chip_exchange", in_specs=[ANY] * n, out_specs=[ANY] * n,
        out_shape=[jax.ShapeDtypeStruct(p.shape, p.dtype) for p in parts],
        scratch_shapes=_sem_scratch(n * len(XY_FLIPS)),
    )(*parts)


def chip_exchange_copies(src, dst, send_sems, recv_sems):
    npeer = len(XY_FLIPS)
    x, y, c = lax.axis_index("x"), lax.axis_index("y"), lax.axis_index("c")
    me = 2 * x + y

    def copy(k, j, sending):
        fx, fy = XY_FLIPS[j]
        px, py = _flip(x, fx), _flip(y, fy)
        peer = 2 * px + py
        return pltpu.make_async_remote_copy(
            src_ref=src[k].at[peer], dst_ref=dst[k].at[me if sending else peer],
            send_sem=send_sems.at[k * npeer + j], recv_sem=recv_sems.at[k * npeer + j],
            device_id=(px, py, c), device_id_type=MESH)

    pairs = [(k, j) for k in range(len(src)) for j in range(npeer)]
    return [copy(k, j, True) for k, j in pairs], [copy(k, j, False) for k, j in pairs]


def sibling_share(halves):
    n = len(halves)

    def body(*refs):
        src, got = refs[:n], refs[n:2 * n]
        send_sems, recv_sems = refs[2 * n:]
        x, y, c = lax.axis_index("x"), lax.axis_index("y"), lax.axis_index("c")
        sends = [pltpu.make_async_remote_copy(
            src_ref=src[k], dst_ref=got[k], send_sem=send_sems.at[k], recv_sem=recv_sems.at[k],
            device_id=(x, y, 1 - c), device_id_type=MESH) for k in range(n)]
        for cp in sends:
            cp.start()
        for cp in sends:
            cp.wait_recv()
        for cp in sends:
            cp.wait_send()

    return pl.pallas_call(
        body, name="grads_sibling_share", in_specs=[ANY] * n, out_specs=[ANY] * n,
        out_shape=[jax.ShapeDtypeStruct(h.shape, h.dtype) for h in halves], scratch_shapes=_sem_scratch(n),
    )(*halves)


ADD_TILE_ELEMS = 512 * 1024


def _row_tile(rows, cols):
    return max(t for t in range(8, rows + 1, 8) if rows % t == 0 and t * cols <= ADD_TILE_ELEMS)


def _prefetch_call(body, name, place, grid, in_specs, out_specs, out_shape, args):
    return pl.pallas_call(
        body, name=name, out_shape=out_shape,
        grid_spec=pltpu.PrefetchScalarGridSpec(num_scalar_prefetch=1, grid=grid, in_specs=in_specs,
                                               out_specs=out_specs),
        compiler_params=pltpu.CompilerParams(dimension_semantics=("arbitrary",) * len(grid),
                                             vmem_limit_bytes=VMEM_LIMIT),
    )(place, *args)


def pair_add(name, place, src, got, dtype):
    n4, half, cols = got.shape
    tile = _row_tile(half, cols)
    nt = half // tile

    def body(pr, a_ref, b_ref, o_ref):
        o_ref[...] = (a_ref[...] + b_ref[...]).astype(o_ref.dtype)

    mine = pl.BlockSpec((None, tile, cols), lambda s, i, pr: (s, pr[0] * nt + i, 0))
    blk = pl.BlockSpec((None, tile, cols), lambda s, i, pr: (s, i, 0))
    return _prefetch_call(body, name, place, (n4, nt), [mine, blk], blk,
                          jax.ShapeDtypeStruct(got.shape, dtype), (src, got))


def chip_add(name, place, part, from_chips):
    _, half, cols = part.shape
    tile = _row_tile(half, cols)

    def body(pr, own_ref, r0_ref, r1_ref, r2_ref, o_ref):
        me = pr[1]
        own, r0, r1, r2 = (r[...].astype(F32) for r in (own_ref, r0_ref, r1_ref, r2_ref))
        t0 = jnp.where(me == 0, own, r0)
        t1 = jnp.where(me == 0, r0, jnp.where(me == 1, own, r1))
        t2 = jnp.where(me <= 1, r1, jnp.where(me == 2, own, r2))
        t3 = jnp.where(me == 3, own, r2)
        o_ref[...] = ((t0 + t1) + t2) + t3

    def slab(j):
        return pl.BlockSpec((None, tile, cols), lambda i, pr: (pr[j], i, 0))

    return _prefetch_call(body, name, place, (half // tile,), [slab(1), slab(2), slab(3), slab(4)],
                          pl.BlockSpec((tile, cols), lambda i, pr: (i, 0)),
                          jax.ShapeDtypeStruct((half, cols), F32), (part, from_chips, from_chips, from_chips))


def pair_adds(tag, place, sources, got, narrow):
    return [pair_add(f"grads_pair_add_{tag}{k}", place, s, g, BF16 if nar else F32)
            for k, (s, g, nar) in enumerate(zip(sources, got, narrow))]


def finish_sums(place, parts, from_chips):
    halves = [chip_add(f"grads_chip_add{k}", place, p, f) for k, (p, f) in enumerate(zip(parts, from_chips))]
    return list(zip(halves, sibling_share(halves)))


ADAM_ROWS = 256


def adamw_update(name, place, halves, w, m, v):
    nsub, rows, cols = w.shape
    half = rows // 2
    tr = ADAM_ROWS if half % ADAM_ROWS == 0 else half
    nth = half // tr

    def body(pr, *refs):
        g_refs, (w_ref, m_ref, v_ref, g_ref, d_ref, nm_ref, nv_ref) = refs[:2 * nsub], refs[2 * nsub:]
        l = pl.program_id(0)
        mine = (pl.program_id(1) // nth) == pr[0]
        g = None
        for s in range(nsub):
            gs = jnp.where(mine, g_refs[2 * s][...], g_refs[2 * s + 1][...])
            g = gs if g is None else jnp.where(l == s, gs, g)
        m2 = ADAM_B1 * m_ref[...] + (1.0 - ADAM_B1) * g
        v2 = ADAM_B2 * v_ref[...] + (1.0 - ADAM_B2) * (g * g)
        m_hat = m2 / (1.0 - ADAM_B1 ** ADAM_STEP)
        v_hat = v2 / (1.0 - ADAM_B2 ** ADAM_STEP)
        g_ref[...] = g
        d_ref[...] = -ADAM_LR * (m_hat / (jnp.sqrt(v_hat) + ADAM_EPS) + ADAM_WD * w_ref[...])
        nm_ref[...] = m2
        nv_ref[...] = v2

    own = pl.BlockSpec((tr, cols), lambda l, i, pr: (jnp.where(i // nth == pr[0], i % nth, 0), 0))
    got = pl.BlockSpec((tr, cols), lambda l, i, pr: (jnp.where(i // nth == pr[0], 0, i % nth), 0))
    blk = pl.BlockSpec((None, tr, cols), lambda l, i, pr: (l, i, 0))
    out = jax.ShapeDtypeStruct((nsub, rows, cols), F32)
    return _prefetch_call(body, name, place, (nsub, rows // tr), [own, got] * nsub + [blk] * 3, [blk] * 4,
                          [out] * 4, [h for pair in halves for h in pair] + [w, m, v])


WEIGHT_NAMES = ("meta_tokens", "a_mu", "a_w_r", "a_w_k", "a_w_v", "a_w_o", "a_w0", "a_w1", "a_w2", "a_a0", "a_a1",
                "a_a2", "a_g1", "a_g2", "a_k_k", "a_k_a", "a_r_k", "a_gn_w", "a_gn_b", "kv_w_k", "kv_w_v", "b_w_q",
                "b_sinks", "b_w_o", "mlp_w_up", "mlp_w_down", "ln_g", "ln_b")
BIG_NAMES = ("a_w_r", "a_w_k", "a_w_v", "a_w_o", "b_w_q", "b_w_o")
EARLY_NAMES, LATE_NAMES = BIG_NAMES[:3], BIG_NAMES[3:]
PACK_MATS = (("kv_w_k", 256), ("kv_w_v", 256), ("a_w1", 64), ("a_a1", 64), ("a_g1", 128), ("a_w2", 64),
             ("a_a2", 64), ("a_g2", 128))
COLUMN_CUT = ("a_w2", "a_a2", "a_g2")
PACK_VECS = (("a_mu", 6), ("a_w0", 1), ("a_a0", 1), ("a_k_k", 1), ("a_k_a", 1), ("a_gn_w", 1), ("a_gn_b", 1),
             ("ln_g", 4), ("ln_b", 4), ("meta_tokens", 16))
PACK_REPL = (("a_r_k", 4), ("b_sinks", 1))
SHARD_W = D_MODEL // N_SHARD


def _tiles(rows):
    return -(-rows // SUBLANES) * SUBLANES


N_MAT_ROWS = sum(_tiles(r) for _, r in PACK_MATS)
N_VEC_ROWS = sum(_tiles(r) for _, r in PACK_VECS)
N_PACK_ROWS = -(-(N_MAT_ROWS + N_VEC_ROWS + sum(_tiles(r) for _, r in PACK_REPL)) // 16) * 16
N_GATHER_VEC_ROWS = -(-N_VEC_ROWS // 16) * 16


def _pad_rows(arr, axis):
    rows = arr.shape[axis]
    pad = [(0, 0)] * arr.ndim
    pad[axis] = (0, _tiles(rows) - rows)
    return jnp.pad(arr, pad) if _tiles(rows) != rows else arr


def _pack_rows(arr):
    if arr.size == N_HEADS:
        arr = jnp.pad(arr.reshape(1, N_HEADS), ((0, 0), (0, SHARD_W - N_HEADS)))
    return _pad_rows(arr.reshape(-1, SHARD_W), 0)


def pack_small(get):
    parts = [_pack_rows(get(name)) for name, _ in PACK_MATS + PACK_VECS + PACK_REPL]
    used = sum(p.shape[0] for p in parts)
    return jnp.concatenate(parts + [jnp.zeros((N_PACK_ROWS - used, SHARD_W), F32)], axis=0)


def unpack_small(pack, shapes):
    out, off = {}, 0
    for name, rows in PACK_MATS + PACK_VECS + PACK_REPL:
        piece = pack[off:off + rows]
        off += _tiles(rows)
        out[name] = piece[:, :N_HEADS].reshape(shapes[name]) if name == "b_sinks" else piece.reshape(shapes[name])
    return out


def whole_weights(big_names, gathered_big, mats, vecs, a_r_k, b_sinks):
    p = {name: g.reshape(D_MODEL, D_MODEL) for name, g in zip(big_names, gathered_big)}
    off = 0
    for name, rows in PACK_MATS:
        piece = mats[:, off:off + rows]
        off += rows
        if name in COLUMN_CUT:
            p[name] = piece.transpose(1, 0, 2).reshape(rows, D_MODEL)
        else:
            p[name] = piece.reshape(D_MODEL, rows)
    v = vecs.transpose(1, 0, 2).reshape(-1, D_MODEL)
    off = 0
    for name, rows in PACK_VECS:
        p[name] = v[off:off + rows]
        off += _tiles(rows)
    for i in range(2):
        for j in range(2):
            p[f"ln_g{i}{j}"] = p["ln_g"][2 * i + j:2 * i + j + 1]
            p[f"ln_b{i}{j}"] = p["ln_b"][2 * i + j:2 * i + j + 1]
    p["a_r_k"] = a_r_k.reshape(1, D_MODEL)
    p["b_sinks"] = b_sinks
    return p


def small_grad_pack(g):
    parts = []
    for name, rows in PACK_MATS:
        if name in COLUMN_CUT:
            parts.append(g[name].reshape(rows, N_SHARD, SHARD_W).transpose(1, 0, 2))
        else:
            parts.append(g[name].reshape(N_SHARD, rows, SHARD_W))
    vecs = {n: g[n] for n in ("a_mu", "a_w0", "a_a0", "a_k_k", "a_k_a", "a_gn_w", "a_gn_b", "meta_tokens")}
    vecs["ln_g"] = jnp.concatenate([g[f"ln_g{i}{j}"] for i in range(2) for j in range(2)], axis=0)
    vecs["ln_b"] = jnp.concatenate([g[f"ln_b{i}{j}"] for i in range(2) for j in range(2)], axis=0)
    for name, rows in PACK_VECS:
        parts.append(_pad_rows(vecs[name].reshape(rows, N_SHARD, SHARD_W).transpose(1, 0, 2), 1))
    r_k = jnp.broadcast_to(g["a_r_k"].reshape(1, -1, SHARD_W), (N_SHARD, D_MODEL // SHARD_W, SHARD_W))
    sinks = jnp.pad(g["b_sinks"].reshape(1, 1, N_HEADS), ((0, 0), (0, 0), (0, SHARD_W - N_HEADS)))
    parts += [_pad_rows(r_k, 1), _pad_rows(jnp.broadcast_to(sinks, (N_SHARD, 1, SHARD_W)), 1)]
    used = sum(p.shape[1] for p in parts)
    parts.append(jnp.zeros((N_SHARD, N_PACK_ROWS - used, SHARD_W), F32))
    return jnp.concatenate(parts, axis=1)


def train_step(vals):
    w = {n: vals[n] for n in WEIGHT_NAMES}
    w_pack = pack_small(lambda n: w[n])
    early = [w[n][0].astype(BF16) for n in EARLY_NAMES]
    early += [w_pack[:N_MAT_ROWS].astype(BF16), w_pack[N_MAT_ROWS:N_MAT_ROWS + N_GATHER_VEC_ROWS]]
    gathered = all_gather_shards(early)
    ne = len(EARLY_NAMES)
    p = whole_weights(EARLY_NAMES, gathered[:ne], gathered[ne], gathered[ne + 1][:, :N_VEC_ROWS], w["a_r_k"],
                      w["b_sinks"])
    nb = len(BIG_NAMES)

    def late_set(big, layer):
        shards = [w[n][0].astype(BF16) for n in big]
        shards += [w["mlp_w_up"][layer].astype(BF16), w["mlp_w_down"][layer].astype(BF16)]

        def weights(got):
            out = {n: x.reshape(D_MODEL, D_MODEL) for n, x in zip(big, got)}
            out[f"mlp_up{layer}"], out[f"mlp_down{layer}"] = got[len(big):]
            return out

        return shards, weights

    late = (late_set((), 1), late_set(LATE_NAMES, 0))

    place = placement()
    ready = {}
    a_names, b_names = BIG_NAMES[:4], BIG_NAMES[4:]

    def early_sources(g):
        return ([g[n].reshape(N_SHARD, SHARD_W, D_MODEL) for n in b_names]
                + [g["mlp_up0"], g["mlp_up1"], g["mlp_down0"], g["mlp_down1"]])

    def early_parts(srcs, got):
        ready["parts"] = pair_adds("early", place, srcs, got, [True] * len(srcs))
        return ready["parts"]

    loss, gx, g, early_from_chips = local_step(vals["x"][0], vals["loss_target"][0], p, late,
                                               (early_sources, early_parts))
    loss = lax.psum(loss, ("x", "y", "c"))
    srcs = [g[n].reshape(N_SHARD, SHARD_W, D_MODEL) for n in a_names] + [small_grad_pack(g)]
    rest = pair_adds("late", place, srcs, pair_exchange("grads_pair_exchange", srcs), [True] * len(a_names) + [False])
    rest_from_chips = chip_exchange(rest)
    na = len(a_names)
    halves = finish_sums(place, rest[:na] + ready["parts"] + rest[na:],
                         list(rest_from_chips[:na]) + list(early_from_chips) + list(rest_from_chips[na:]))

    res = {}
    for k, n in enumerate(BIG_NAMES):
        res[n] = adamw_update("adamw_" + n, place, halves[k:k + 1], w[n], vals["m_" + n], vals["v_" + n])
    for k, n in ((nb, "mlp_w_up"), (nb + 2, "mlp_w_down")):
        res[n] = adamw_update("adamw_" + n, place, halves[k:k + 2], w[n], vals["m_" + n], vals["v_" + n])
    packs = adamw_update("adamw_small", place, halves[-1:], w_pack[None], pack_small(lambda n: vals["m_" + n])[None],
                         pack_small(lambda n: vals["v_" + n])[None])
    shapes = {n: w[n].shape for n in WEIGHT_NAMES}
    small = [unpack_small(pk[0], shapes) for pk in packs]
    outs = [loss, gx[None]]
    for t in range(4):
        outs += [res[n][t] if n in res else small[t][n] for n in WEIGHT_NAMES]
    return tuple(outs)


def kernel(x, meta_tokens, a_mu, a_w_r, a_w_k, a_w_v, a_w_o, a_w0, a_w1, a_w2, a_a0, a_a1, a_a2, a_g1, a_g2, a_k_k,
           a_k_a, a_r_k, a_gn_w, a_gn_b, kv_w_k, kv_w_v, b_w_q, b_sinks, b_w_o, mlp_w_up, mlp_w_down, ln_g, ln_b,
           loss_target, m_meta_tokens, m_a_mu, m_a_w_r, m_a_w_k, m_a_w_v, m_a_w_o, m_a_w0, m_a_w1, m_a_w2, m_a_a0,
           m_a_a1, m_a_a2, m_a_g1, m_a_g2, m_a_k_k, m_a_k_a, m_a_r_k, m_a_gn_w, m_a_gn_b, m_kv_w_k, m_kv_w_v,
           m_b_w_q, m_b_sinks, m_b_w_o, m_mlp_w_up, m_mlp_w_down, m_ln_g, m_ln_b, v_meta_tokens, v_a_mu, v_a_w_r,
           v_a_w_k, v_a_w_v, v_a_w_o, v_a_w0, v_a_w1, v_a_w2, v_a_a0, v_a_a1, v_a_a2, v_a_g1, v_a_g2, v_a_k_k,
           v_a_k_a, v_a_r_k, v_a_gn_w, v_a_gn_b, v_kv_w_k, v_kv_w_v, v_b_w_q, v_b_sinks, v_b_w_o, v_mlp_w_up,
           v_mlp_w_down, v_ln_g, v_ln_b):
    return train_step(dict(locals()))
```
